```python
import jax, jax.numpy as jnp
from jax import lax
import numpy as np

D_MODEL = 1024
BATCH = 8
SEQ = 8192
DEPTH = 2

CHUNK = 64
BAND_CHUNKS = 9
ATTN_HEADS = 8
HEAD_DIM = 64
ATTN_WIDTH = ATTN_HEADS * HEAD_DIM
POOL_WINDOWS = (2, 4, 8, 16)
POOL_GROUPS = len(POOL_WINDOWS)
POOL_WIDTH = D_MODEL // 2
POOL_GROUP_DIM = POOL_WIDTH // POOL_GROUPS
MAX_REL_DIST = 256
N_REL = 2 * MAX_REL_DIST + 1
D_FF = 2816
CONV_WIDTH = 3
N_BRANCH = 2
IN_WIDTH = 3 * ATTN_WIDTH + POOL_WIDTH + N_BRANCH * D_MODEL
EPS = 1e-6

kernel_name = "hybrid_chunk_attn_pool_sandwich"


def rms_norm(x, g):
    xf = x.astype(jnp.float32)
    y = xf * lax.rsqrt(jnp.mean(xf * xf, axis=-1, keepdims=True) + EPS)
    return (y * g.astype(jnp.float32)).astype(x.dtype)


def chunk_band_attention(q, k, v, rel_bias):
    b, s, h, dh = q.shape
    n_chunks = s // CHUNK
    band = BAND_CHUNKS * CHUNK
    lead = (BAND_CHUNKS - 1) * CHUNK
    pad = ((0, 0), (lead, 0), (0, 0), (0, 0))
    k_pad = jnp.pad(k, pad)
    v_pad = jnp.pad(v, pad)
    dist = jnp.arange(CHUNK)[:, None] + lead - jnp.arange(band)[None, :]
    idx = jnp.clip(dist, -MAX_REL_DIST, MAX_REL_DIST) + MAX_REL_DIST
    bias = rel_bias.astype(jnp.float32)[:, idx]
    scale = HEAD_DIM ** -0.5
    key_offsets = jnp.arange(band)

    def one_chunk(c):
        start = c * CHUNK
        q_c = lax.dynamic_slice_in_dim(q, start, CHUNK, axis=1)
        k_c = lax.dynamic_slice_in_dim(k_pad, start, band, axis=1)
        v_c = lax.dynamic_slice_in_dim(v_pad, start, band, axis=1)
        sc = jnp.einsum('bqhd,bkhd->bhqk', q_c, k_c,
                        preferred_element_type=jnp.float32) * scale + bias
        valid = (start - lead + key_offsets) >= 0
        sc = jnp.where(valid[None, None, None, :], sc, -1e30)
        p = jax.nn.softmax(sc, axis=-1).astype(v.dtype)
        return jnp.einsum('bhqk,bkhd->bqhd', p, v_c)

    out = lax.map(one_chunk, jnp.arange(n_chunks))
    return jnp.moveaxis(out, 0, 1).reshape(b, s, h * dh)


def multiscale_pool(u, w_group, scale):
    b, s, c = u.shape
    uf = u.astype(jnp.float32)
    max_w = max(POOL_WINDOWS)
    cs = jnp.pad(jnp.cumsum(uf, axis=1), ((0, 0), (max_w, 0), (0, 0)))
    t = jnp.arange(s)
    outs = []
    for g, w in enumerate(POOL_WINDOWS):
        sl = slice(g * POOL_GROUP_DIM, (g + 1) * POOL_GROUP_DIM)
        win = cs[:, max_w:, sl] - cs[:, max_w - w:max_w - w + s, sl]
        cnt = jnp.minimum(t + 1, w).astype(jnp.float32)[None, :, None]
        outs.append(win / cnt - uf[:, :, sl])
    pooled = jnp.stack(outs, axis=2).astype(u.dtype)
    mixed = jnp.einsum('bsgc,gcd->bsgd', pooled, w_group).reshape(b, s, c)
    return mixed * scale


def conv_gated_ffn(x, w_up, conv_w, conv_b, w_down):
    hu = x @ w_up
    s = hu.shape[1]
    hp = jnp.pad(hu, ((0, 0), (CONV_WIDTH - 1, 0), (0, 0)))
    hc = conv_b + conv_w[CONV_WIDTH - 1] * hu
    for i in range(CONV_WIDTH - 1):
        hc = hc + conv_w[i] * hp[:, i:i + s]
    val, gate = jnp.split(hc, 2, axis=-1)
    return (jax.nn.gelu(gate, approximate=True) * val) @ w_down


def _fwd_setup_inputs(seed: int = 0) -> dict:
    key = jax.random.key(seed)
    ks = jax.random.split(key, 20)
    f32 = jnp.float32

    def nrm(k, shape, s):
        return jax.random.normal(k, shape, f32) * s

    return {
        "x": jax.random.normal(ks[0], (BATCH, SEQ, D_MODEL), f32),
        "norm_mix_pre": 1.0 + nrm(ks[1], (DEPTH, D_MODEL), 0.05),
        "w_in": nrm(ks[2], (DEPTH, D_MODEL, IN_WIDTH), D_MODEL ** -0.5),
        "b_gate": nrm(ks[3], (DEPTH, N_BRANCH * D_MODEL), 0.01),
        "rel_bias": nrm(ks[4], (DEPTH, ATTN_HEADS, N_REL), 0.1),
        "w_attn_out": nrm(ks[5], (DEPTH, ATTN_WIDTH, D_MODEL), ATTN_WIDTH ** -0.5),
        "w_pool_group": nrm(ks[6], (DEPTH, POOL_GROUPS, POOL_GROUP_DIM, POOL_GROUP_DIM), POOL_GROUP_DIM ** -0.5),
        "pool_scale": 1.0 + nrm(ks[7], (DEPTH, POOL_WIDTH), 0.1),
        "w_pool_out": nrm(ks[8], (DEPTH, POOL_WIDTH, D_MODEL), POOL_WIDTH ** -0.5),
        "w_o": nrm(ks[9], (DEPTH, D_MODEL, D_MODEL), D_MODEL ** -0.5),
        "norm_mix_post": 1.0 + nrm(ks[10], (DEPTH, D_MODEL), 0.05),
        "norm_ffn_pre": 1.0 + nrm(ks[11], (DEPTH, D_MODEL), 0.05),
        "w_up": nrm(ks[12], (DEPTH, D_MODEL, 2 * D_FF), D_MODEL ** -0.5),
        "conv_w": nrm(ks[13], (DEPTH, CONV_WIDTH, 2 * D_FF), CONV_WIDTH ** -0.5),
        "conv_b": nrm(ks[14], (DEPTH, 2 * D_FF), 0.01),
        "w_down": nrm(ks[15], (DEPTH, D_FF, D_MODEL), D_FF ** -0.5),
        "norm_ffn_post": 1.0 + nrm(ks[16], (DEPTH, D_MODEL), 0.05),
    }


def _fwd_reference(x, norm_mix_pre, w_in, b_gate, rel_bias, w_attn_out, w_pool_group, pool_scale,
              w_pool_out, w_o, norm_mix_post, norm_ffn_pre, w_up, conv_w, conv_b, w_down,
              norm_ffn_post):
    b, s, _ = x.shape
    splits = [ATTN_WIDTH, 2 * ATTN_WIDTH, 3 * ATTN_WIDTH, 3 * ATTN_WIDTH + POOL_WIDTH]
    for l in range(DEPTH):
        h = rms_norm(x, norm_mix_pre[l])
        proj = h @ w_in[l]
        q, k, v, u, gates = jnp.split(proj, splits, axis=-1)
        q = q.reshape(b, s, ATTN_HEADS, HEAD_DIM)
        k = k.reshape(b, s, ATTN_HEADS, HEAD_DIM)
        v = v.reshape(b, s, ATTN_HEADS, HEAD_DIM)
        y_a = chunk_band_attention(q, k, v, rel_bias[l]) @ w_attn_out[l]
        y_b = multiscale_pool(u, w_pool_group[l], pool_scale[l]) @ w_pool_out[l]
        g_a, g_b = jnp.split(jax.nn.sigmoid(gates + b_gate[l]), N_BRANCH, axis=-1)
        mix = (g_a * y_a + g_b * y_b) @ w_o[l]
        x = x + rms_norm(mix, norm_mix_post[l])
        f = conv_gated_ffn(rms_norm(x, norm_ffn_pre[l]), w_up[l], conv_w[l], conv_b[l], w_down[l])
        x = x + rms_norm(f, norm_ffn_post[l])
    return x


import jax as _jax
import jax.numpy as _jnp

TWIN_FORMAT = 'train_step'
FWD_PARAMS = ['x', 'norm_mix_pre', 'w_in', 'b_gate', 'rel_bias', 'w_attn_out', 'w_pool_group', 'pool_scale', 'w_pool_out', 'w_o', 'norm_mix_post', 'norm_ffn_pre', 'w_up', 'conv_w', 'conv_b', 'w_down', 'norm_ffn_post']
TWIN_WEIGHTS = ['norm_mix_pre', 'w_in', 'b_gate', 'rel_bias', 'w_attn_out', 'w_pool_group', 'pool_scale', 'w_pool_out', 'w_o', 'norm_mix_post', 'norm_ffn_pre', 'w_up', 'conv_w', 'conv_b', 'w_down', 'norm_ffn_post']
TWIN_DIFF_INPUT = 'x'
TWIN_INPUTS = ['x', 'norm_mix_pre', 'w_in', 'b_gate', 'rel_bias', 'w_attn_out', 'w_pool_group', 'pool_scale', 'w_pool_out', 'w_o', 'norm_mix_post', 'norm_ffn_pre', 'w_up', 'conv_w', 'conv_b', 'w_down', 'norm_ffn_post', 'loss_target', 'm_norm_mix_pre', 'm_w_in', 'm_b_gate', 'm_rel_bias', 'm_w_attn_out', 'm_w_pool_group', 'm_pool_scale', 'm_w_pool_out', 'm_w_o', 'm_norm_mix_post', 'm_norm_ffn_pre', 'm_w_up', 'm_conv_w', 'm_conv_b', 'm_w_down', 'm_norm_ffn_post', 'v_norm_mix_pre', 'v_w_in', 'v_b_gate', 'v_rel_bias', 'v_w_attn_out', 'v_w_pool_group', 'v_pool_scale', 'v_w_pool_out', 'v_w_o', 'v_norm_mix_post', 'v_norm_ffn_pre', 'v_w_up', 'v_conv_w', 'v_conv_b', 'v_w_down', 'v_norm_ffn_post']
TWIN_OUTPUTS = ['loss', 'grad_x', 'grad_norm_mix_pre', 'grad_w_in', 'grad_b_gate', 'grad_rel_bias', 'grad_w_attn_out', 'grad_w_pool_group', 'grad_pool_scale', 'grad_w_pool_out', 'grad_w_o', 'grad_norm_mix_post', 'grad_norm_ffn_pre', 'grad_w_up', 'grad_conv_w', 'grad_conv_b', 'grad_w_down', 'grad_norm_ffn_post', 'delta_norm_mix_pre', 'delta_w_in', 'delta_b_gate', 'delta_rel_bias', 'delta_w_attn_out', 'delta_w_pool_group', 'delta_pool_scale', 'delta_w_pool_out', 'delta_w_o', 'delta_norm_mix_post', 'delta_norm_ffn_pre', 'delta_w_up', 'delta_conv_w', 'delta_conv_b', 'delta_w_down', 'delta_norm_ffn_post', 'new_m_norm_mix_pre', 'new_m_w_in', 'new_m_b_gate', 'new_m_rel_bias', 'new_m_w_attn_out', 'new_m_w_pool_group', 'new_m_pool_scale', 'new_m_w_pool_out', 'new_m_w_o', 'new_m_norm_mix_post', 'new_m_norm_ffn_pre', 'new_m_w_up', 'new_m_conv_w', 'new_m_conv_b', 'new_m_w_down', 'new_m_norm_ffn_post', 'new_v_norm_mix_pre', 'new_v_w_in', 'new_v_b_gate', 'new_v_rel_bias', 'new_v_w_attn_out', 'new_v_w_pool_group', 'new_v_pool_scale', 'new_v_w_pool_out', 'new_v_w_o', 'new_v_norm_mix_post', 'new_v_norm_ffn_pre', 'new_v_w_up', 'new_v_conv_w', 'new_v_conv_b', 'new_v_w_down', 'new_v_norm_ffn_post']
TWIN_LEAF_KINDS = {'loss': 'loss', 'grad_x': 'grad_x', 'grad_norm_mix_pre': 'grad_w', 'grad_w_in': 'grad_w', 'grad_b_gate': 'grad_w', 'grad_rel_bias': 'grad_w', 'grad_w_attn_out': 'grad_w', 'grad_w_pool_group': 'grad_w', 'grad_pool_scale': 'grad_w', 'grad_w_pool_out': 'grad_w', 'grad_w_o': 'grad_w', 'grad_norm_mix_post': 'grad_w', 'grad_norm_ffn_pre': 'grad_w', 'grad_w_up': 'grad_w', 'grad_conv_w': 'grad_w', 'grad_conv_b': 'grad_w', 'grad_w_down': 'grad_w', 'grad_norm_ffn_post': 'grad_w', 'delta_norm_mix_pre': 'delta_w', 'delta_w_in': 'delta_w', 'delta_b_gate': 'delta_w', 'delta_rel_bias': 'delta_w', 'delta_w_attn_out': 'delta_w', 'delta_w_pool_group': 'delta_w', 'delta_pool_scale': 'delta_w', 'delta_w_pool_out': 'delta_w', 'delta_w_o': 'delta_w', 'delta_norm_mix_post': 'delta_w', 'delta_norm_ffn_pre': 'delta_w', 'delta_w_up': 'delta_w', 'delta_conv_w': 'delta_w', 'delta_conv_b': 'delta_w', 'delta_w_down': 'delta_w', 'delta_norm_ffn_post': 'delta_w', 'new_m_norm_mix_pre': 'new_m', 'new_m_w_in': 'new_m', 'new_m_b_gate': 'new_m', 'new_m_rel_bias': 'new_m', 'new_m_w_attn_out': 'new_m', 'new_m_w_pool_group': 'new_m', 'new_m_pool_scale': 'new_m', 'new_m_w_pool_out': 'new_m', 'new_m_w_o': 'new_m', 'new_m_norm_mix_post': 'new_m', 'new_m_norm_ffn_pre': 'new_m', 'new_m_w_up': 'new_m', 'new_m_conv_w': 'new_m', 'new_m_conv_b': 'new_m', 'new_m_w_down': 'new_m', 'new_m_norm_ffn_post': 'new_m', 'new_v_norm_mix_pre': 'new_v', 'new_v_w_in': 'new_v', 'new_v_b_gate': 'new_v', 'new_v_rel_bias': 'new_v', 'new_v_w_attn_out': 'new_v', 'new_v_w_pool_group': 'new_v', 'new_v_pool_scale': 'new_v', 'new_v_w_pool_out': 'new_v', 'new_v_w_o': 'new_v', 'new_v_norm_mix_post': 'new_v', 'new_v_norm_ffn_pre': 'new_v', 'new_v_w_up': 'new_v', 'new_v_conv_w': 'new_v', 'new_v_conv_b': 'new_v', 'new_v_w_down': 'new_v', 'new_v_norm_ffn_post': 'new_v'}


def _forward(args):
    return _fwd_reference(*[args[k] for k in FWD_PARAMS])


def _output_shape():
    out = _jax.eval_shape(lambda: _forward(_fwd_setup_inputs(0)))
    return out.shape, out.dtype

N_MICROBATCH = 1
ADAM_LR = 0.001
ADAM_B1 = 0.9
ADAM_B2 = 0.999
ADAM_EPS = 1e-08
ADAM_WD = 0.01
ADAM_STEP = 10
PER_EXAMPLE_BATCH_AXIS = {'x': 0, 'loss_target': 0}
SHARED_INPUTS = []
_WEIGHT_DTYPES = {'norm_mix_pre': _jnp.float32, 'w_in': _jnp.float32, 'b_gate': _jnp.float32, 'rel_bias': _jnp.float32, 'w_attn_out': _jnp.float32, 'w_pool_group': _jnp.float32, 'pool_scale': _jnp.float32, 'w_pool_out': _jnp.float32, 'w_o': _jnp.float32, 'norm_mix_post': _jnp.float32, 'norm_ffn_pre': _jnp.float32, 'w_up': _jnp.float32, 'conv_w': _jnp.float32, 'conv_b': _jnp.float32, 'w_down': _jnp.float32, 'norm_ffn_post': _jnp.float32}
MOMENT_SCALE = {'norm_mix_pre': 1.855259e+00, 'w_in': 9.302312e-01, 'b_gate': 6.444342e-01, 'rel_bias': 8.174016e-02, 'w_attn_out': 1.893020e-01, 'w_pool_group': 2.990107e+00, 'pool_scale': 3.251725e+00, 'w_pool_out': 2.279445e+00, 'w_o': 2.379087e+00, 'norm_mix_post': 6.431781e+01, 'norm_ffn_pre': 1.089556e+00, 'w_up': 5.156188e-01, 'conv_w': 5.576941e-01, 'conv_b': 9.962363e-01, 'w_down': 1.122427e+00, 'norm_ffn_post': 6.364437e+01}


def _to_microbatches(a, axis):
    t = _jnp.moveaxis(a, axis, 0)
    t = t.reshape((N_MICROBATCH, t.shape[0] // N_MICROBATCH) + t.shape[1:])
    return _jnp.moveaxis(t, 1, axis + 1)


def setup_inputs(seed: int = 0) -> dict:
    inp = _fwd_setup_inputs(seed)
    key = _jax.random.fold_in(_jax.random.key(seed), 7919)
    shape, _ = _output_shape()
    out = dict(inp)
    out["loss_target"] = _jax.random.normal(_jax.random.fold_in(key, 0), shape, _jnp.float32)
    for i, name in enumerate(TWIN_WEIGHTS):
        w = inp[name].astype(_jnp.float32)
        if MOMENT_SCALE is None:
            s = _jnp.sqrt(_jnp.mean(_jnp.square(w)) + 1e-30)
        else:
            s = MOMENT_SCALE[name]
        km, kv = _jax.random.split(_jax.random.fold_in(key, i + 1))
        out[name] = w
        out["m_" + name] = s * _jax.random.normal(km, w.shape, _jnp.float32)
        out["v_" + name] = (s * s) * _jax.random.uniform(kv, w.shape, _jnp.float32, 0.5, 1.5)
    if N_MICROBATCH > 1:
        for name, axis in PER_EXAMPLE_BATCH_AXIS.items():
            out[name] = _to_microbatches(out[name], axis)
    return {'x': out['x'], 'norm_mix_pre': out['norm_mix_pre'], 'w_in': out['w_in'], 'b_gate': out['b_gate'], 'rel_bias': out['rel_bias'], 'w_attn_out': out['w_attn_out'], 'w_pool_group': out['w_pool_group'], 'pool_scale': out['pool_scale'], 'w_pool_out': out['w_pool_out'], 'w_o': out['w_o'], 'norm_mix_post': out['norm_mix_post'], 'norm_ffn_pre': out['norm_ffn_pre'], 'w_up': out['w_up'], 'conv_w': out['conv_w'], 'conv_b': out['conv_b'], 'w_down': out['w_down'], 'norm_ffn_post': out['norm_ffn_post'], 'loss_target': out['loss_target'], 'm_norm_mix_pre': out['m_norm_mix_pre'], 'm_w_in': out['m_w_in'], 'm_b_gate': out['m_b_gate'], 'm_rel_bias': out['m_rel_bias'], 'm_w_attn_out': out['m_w_attn_out'], 'm_w_pool_group': out['m_w_pool_group'], 'm_pool_scale': out['m_pool_scale'], 'm_w_pool_out': out['m_w_pool_out'], 'm_w_o': out['m_w_o'], 'm_norm_mix_post': out['m_norm_mix_post'], 'm_norm_ffn_pre': out['m_norm_ffn_pre'], 'm_w_up': out['m_w_up'], 'm_conv_w': out['m_conv_w'], 'm_conv_b': out['m_conv_b'], 'm_w_down': out['m_w_down'], 'm_norm_ffn_post': out['m_norm_ffn_post'], 'v_norm_mix_pre': out['v_norm_mix_pre'], 'v_w_in': out['v_w_in'], 'v_b_gate': out['v_b_gate'], 'v_rel_bias': out['v_rel_bias'], 'v_w_attn_out': out['v_w_attn_out'], 'v_w_pool_group': out['v_w_pool_group'], 'v_pool_scale': out['v_pool_scale'], 'v_w_pool_out': out['v_w_pool_out'], 'v_w_o': out['v_w_o'], 'v_norm_mix_post': out['v_norm_mix_post'], 'v_norm_ffn_pre': out['v_norm_ffn_pre'], 'v_w_up': out['v_w_up'], 'v_conv_w': out['v_conv_w'], 'v_conv_b': out['v_conv_b'], 'v_w_down': out['v_w_down'], 'v_norm_ffn_post': out['v_norm_ffn_post']}


def _loss(weights, diff, rest, loss_target):
    with _jax.named_scope("forward"):
        args = {**rest, TWIN_DIFF_INPUT: diff, **{k: w.astype(_WEIGHT_DTYPES[k]) for k, w in weights.items()}}
        y = _forward(args)
    with _jax.named_scope("loss_head"):
        err = _jnp.square(y.astype(_jnp.float32) - loss_target)
        return 0.5 * _jnp.sum(_jnp.mean(err, axis=-1)) if err.ndim else 0.5 * err


def _adamw(w, g, m, v):
    m = ADAM_B1 * m + (1.0 - ADAM_B1) * g
    v = ADAM_B2 * v + (1.0 - ADAM_B2) * _jnp.square(g)
    m_hat = m / (1.0 - ADAM_B1 ** ADAM_STEP)
    v_hat = v / (1.0 - ADAM_B2 ** ADAM_STEP)
    delta = -ADAM_LR * (m_hat / (_jnp.sqrt(v_hat) + ADAM_EPS) + ADAM_WD * w)
    return delta, m, v


def reference(x, norm_mix_pre, w_in, b_gate, rel_bias, w_attn_out, w_pool_group, pool_scale, w_pool_out, w_o, norm_mix_post, norm_ffn_pre, w_up, conv_w, conv_b, w_down, norm_ffn_post, loss_target, m_norm_mix_pre, m_w_in, m_b_gate, m_rel_bias, m_w_attn_out, m_w_pool_group, m_pool_scale, m_w_pool_out, m_w_o, m_norm_mix_post, m_norm_ffn_pre, m_w_up, m_conv_w, m_conv_b, m_w_down, m_norm_ffn_post, v_norm_mix_pre, v_w_in, v_b_gate, v_rel_bias, v_w_attn_out, v_w_pool_group, v_pool_scale, v_w_pool_out, v_w_o, v_norm_mix_post, v_norm_ffn_pre, v_w_up, v_conv_w, v_conv_b, v_w_down, v_norm_ffn_post):
    given = dict(x=x, norm_mix_pre=norm_mix_pre, w_in=w_in, b_gate=b_gate, rel_bias=rel_bias, w_attn_out=w_attn_out, w_pool_group=w_pool_group, pool_scale=pool_scale, w_pool_out=w_pool_out, w_o=w_o, norm_mix_post=norm_mix_post, norm_ffn_pre=norm_ffn_pre, w_up=w_up, conv_w=conv_w, conv_b=conv_b, w_down=w_down, norm_ffn_post=norm_ffn_post, loss_target=loss_target, m_norm_mix_pre=m_norm_mix_pre, m_w_in=m_w_in, m_b_gate=m_b_gate, m_rel_bias=m_rel_bias, m_w_attn_out=m_w_attn_out, m_w_pool_group=m_w_pool_group, m_pool_scale=m_pool_scale, m_w_pool_out=m_w_pool_out, m_w_o=m_w_o, m_norm_mix_post=m_norm_mix_post, m_norm_ffn_pre=m_norm_ffn_pre, m_w_up=m_w_up, m_conv_w=m_conv_w, m_conv_b=m_conv_b, m_w_down=m_w_down, m_norm_ffn_post=m_norm_ffn_post, v_norm_mix_pre=v_norm_mix_pre, v_w_in=v_w_in, v_b_gate=v_b_gate, v_rel_bias=v_rel_bias, v_w_attn_out=v_w_attn_out, v_w_pool_group=v_w_pool_group, v_pool_scale=v_pool_scale, v_w_pool_out=v_w_pool_out, v_w_o=v_w_o, v_norm_mix_post=v_norm_mix_post, v_norm_ffn_pre=v_norm_ffn_pre, v_w_up=v_w_up, v_conv_w=v_conv_w, v_conv_b=v_conv_b, v_w_down=v_w_down, v_norm_ffn_post=v_norm_ffn_post)
    weights = {n: given[n] for n in TWIN_WEIGHTS}
    shared = {n: given[n] for n in SHARED_INPUTS}
    per_example = {n: given[n] for n in ['x']}
    grad_fn = _jax.value_and_grad(_loss, argnums=(0, 1))

    def one_microbatch(ex, loss_target):
        ex = dict(ex)
        diff = ex.pop(TWIN_DIFF_INPUT)
        return grad_fn(weights, diff, {**shared, **ex}, loss_target)

    if N_MICROBATCH == 1:
        loss, (grad_w, grad_x) = one_microbatch(per_example, given["loss_target"])
    else:
        def body(carry, xs):
            loss_sum, grad_sum = carry
            l_k, (gw_k, gx_k) = one_microbatch(xs[0], xs[1])
            with _jax.named_scope("update"):
                return (loss_sum + l_k, _jax.tree.map(_jnp.add, grad_sum, gw_k)), gx_k

        init = (_jnp.zeros((), _jnp.float32), _jax.tree.map(_jnp.zeros_like, weights))
        (loss, grad_w), grad_x = _jax.lax.scan(body, init, (per_example, given["loss_target"]))
    with _jax.named_scope("update"):
        delta_w, new_m, new_v = {}, {}, {}
        for n in TWIN_WEIGHTS:
            delta_w[n], new_m[n], new_v[n] = _adamw(weights[n], grad_w[n], given["m_" + n], given["v_" + n])
    return (loss, grad_x, *[grad_w[n] for n in TWIN_WEIGHTS], *[delta_w[n] for n in TWIN_WEIGHTS],
            *[new_m[n] for n in TWIN_WEIGHTS], *[new_v[n] for n in TWIN_WEIGHTS])
```

```python
import functools
import math

import jax
import jax.numpy as jnp
from jax import lax
from jax.experimental import pallas as pl
from jax.experimental.pallas import tpu as pltpu

F32 = jnp.float32
BF16 = jnp.bfloat16
MESH = pl.DeviceIdType.MESH

D_MODEL = 1024
DEPTH = 2
CHUNK = 64
BAND_CHUNKS = 9
BAND = BAND_CHUNKS * CHUNK
HEADS = 8
HEAD_DIM = 64
ATTN_W = HEADS * HEAD_DIM
POOL_WINDOWS = (2, 4, 8, 16)
POOL_W = 512
POOL_GD = 128
MAX_REL = 256
N_REL = 2 * MAX_REL + 1
D_FF = 2816
IN_W = 3 * ATTN_W + POOL_W + 2 * D_MODEL
EPS = 1e-6
ATTN_SCALE = HEAD_DIM ** -0.5
BAND_PAD = 640
BIAS_LANES = BAND_PAD
N_CHIPS = 4

ADAM_LR = 0.001
ADAM_B1 = 0.9
ADAM_B2 = 0.999
ADAM_EPS = 1e-08
ADAM_WD = 0.01
ADAM_STEP = 10

VMEM_LIMIT_V7X = 56 * 1024 * 1024
TOK = 512
ATT_BLK = 8 * CHUNK
FF_COL = 256
HALO = 32


def _cparams(*sem):
    return pltpu.CompilerParams(dimension_semantics=sem, vmem_limit_bytes=VMEM_LIMIT_V7X)


def _sds(shape, dtype):
    return jax.ShapeDtypeStruct(shape, dtype)


def _matmul(name, a, b, a_spec, b_spec, o_spec, out_shape, grid, contract, nk, acc_shape):
    def body(a_ref, b_ref, o_ref, *scratch):
        part = lax.dot_general(a_ref[...], b_ref[...], (contract, ((), ())),
                               preferred_element_type=F32)
        if nk == 1:
            o_ref[...] = part.astype(o_ref.dtype)
        else:
            acc_ref = scratch[0]
            k = pl.program_id(2)

            @pl.when(k == 0)
            def _():
                acc_ref[...] = part

            @pl.when(k > 0)
            def _():
                acc_ref[...] += part

            @pl.when(k == nk - 1)
            def _():
                o_ref[...] = acc_ref[...].astype(o_ref.dtype)

    scratch = [] if nk == 1 else [pltpu.VMEM(acc_shape, F32)]
    return pl.pallas_call(
        body, name=name, grid=grid, in_specs=[a_spec, b_spec], out_specs=o_spec,
        out_shape=out_shape, scratch_shapes=scratch,
        compiler_params=_cparams("parallel", "parallel", "arbitrary"),
    )(a, b)


NN = ((1,), (0,))
NT = ((1,), (1,))
TN = ((0,), (0,))


def _tm(t):
    return min(t, 1024)


def _mm_nn_blocked(name, a, w, l, out_dtype):
    t, k = a.shape
    nb = w.shape[3]
    tm = _tm(t)
    return _matmul(
        name, a, w,
        pl.BlockSpec((tm, k), lambda i, n, kk: (i, 0)),
        pl.BlockSpec((None, None, k, nb), lambda i, n, kk: (l, n, 0, 0)),
        pl.BlockSpec((tm, nb), lambda i, n, kk: (i, n)),
        _sds((t, N_CHIPS * nb), out_dtype), (t // tm, N_CHIPS, 1), NN, 1, None)


def _mm_nt_blocked(name, a, w, l, out_dtype):
    t = a.shape[0]
    k, nb = w.shape[2], w.shape[3]
    tm = _tm(t)
    return _matmul(
        name, a, w,
        pl.BlockSpec((tm, nb), lambda i, n, kk: (i, kk)),
        pl.BlockSpec((None, None, k, nb), lambda i, n, kk: (l, kk, 0, 0)),
        pl.BlockSpec((tm, k), lambda i, n, kk: (i, 0)),
        _sds((t, k), out_dtype), (t // tm, 1, N_CHIPS), NT, N_CHIPS, (tm, k))


def _mm_tn_blocked(name, a, g):
    t, k = a.shape
    nb = g.shape[1] // N_CHIPS
    nt = t // TOK
    return _matmul(
        name, a, g,
        pl.BlockSpec((TOK, k), lambda n, j, kk: (kk, 0)),
        pl.BlockSpec((TOK, nb), lambda n, j, kk: (kk, n)),
        pl.BlockSpec((None, k, nb), lambda n, j, kk: (n, 0, 0)),
        _sds((N_CHIPS, k, nb), BF16), (N_CHIPS, 1, nt), TN, nt, (k, nb))


def _mm_nn(name, a, w, l, tk, out_dtype):
    t, k = a.shape
    n = w.shape[2]
    tm = _tm(t)
    nk = k // tk
    return _matmul(
        name, a, w,
        pl.BlockSpec((tm, tk), lambda i, j, kk: (i, kk)),
        pl.BlockSpec((None, tk, n), lambda i, j, kk: (l, kk, 0)),
        pl.BlockSpec((tm, n), lambda i, j, kk: (i, 0)),
        _sds((t, n), out_dtype), (t // tm, 1, nk), NN, nk, (tm, n))


def _mm_nt(name, a, w, l, tn, out_dtype):
    t, n = a.shape
    k = w.shape[1]
    tm = _tm(t)
    return _matmul(
        name, a, w,
        pl.BlockSpec((tm, n), lambda i, j, kk: (i, 0)),
        pl.BlockSpec((None, tn, n), lambda i, j, kk: (l, j, 0)),
        pl.BlockSpec((tm, tn), lambda i, j, kk: (i, j)),
        _sds((t, k), out_dtype), (t // tm, k // tn, 1), NT, 1, None)


def _mm_tn(name, a, g, tko):
    t, k = a.shape
    n = g.shape[1]
    nt = t // TOK
    return _matmul(
        name, a, g,
        pl.BlockSpec((TOK, tko), lambda i, j, kk: (kk, i)),
        pl.BlockSpec((TOK, n), lambda i, j, kk: (kk, 0)),
        pl.BlockSpec((tko, n), lambda i, j, kk: (i, 0)),
        _sds((k, n), BF16), (k // tko, 1, nt), TN, nt, (tko, n))


def _row_spec(width, col=0):
    return pl.BlockSpec((TOK, width), lambda i: (i, col))


def _vec_spec(width):
    return pl.BlockSpec((1, width), lambda i: (0, 0))


def _rms(x):
    return lax.rsqrt(jnp.mean(x * x, axis=-1, keepdims=True) + EPS)


def _norm_fwd(name, x, g):
    t = x.shape[0]

    def body(x_ref, g_ref, h_ref):
        xv = x_ref[...]
        h_ref[...] = (xv * _rms(xv) * g_ref[...]).astype(BF16)

    return pl.pallas_call(
        body, name=name, grid=(t // TOK,), in_specs=[_row_spec(D_MODEL), _vec_spec(D_MODEL)],
        out_specs=_row_spec(D_MODEL), out_shape=_sds((t, D_MODEL), BF16),
        compiler_params=_cparams("parallel"))(x, g)


def _norm_residual_fwd(name, xres, m, g):
    t = xres.shape[0]

    def body(x_ref, m_ref, g_ref, o_ref):
        mv = m_ref[...]
        o_ref[...] = x_ref[...] + mv * _rms(mv) * g_ref[...]

    return pl.pallas_call(
        body, name=name, grid=(t // TOK,),
        in_specs=[_row_spec(D_MODEL), _row_spec(D_MODEL), _vec_spec(D_MODEL)],
        out_specs=_row_spec(D_MODEL), out_shape=_sds((t, D_MODEL), F32),
        compiler_params=_cparams("parallel"))(xres, m, g)


def _norm_post_bwd(name, dxo, m, g):
    t = dxo.shape[0]

    def body(d_ref, m_ref, g_ref, dm_ref, dg_ref):
        mv = m_ref[...]
        dv = d_ref[...]
        r = _rms(mv)
        n = mv * r
        dn = dv * g_ref[...]
        dm_ref[...] = (r * (dn - n * jnp.mean(dn * n, axis=-1, keepdims=True))).astype(BF16)
        part = jnp.sum(dv * n, axis=0, keepdims=True)

        @pl.when(pl.program_id(0) == 0)
        def _():
            dg_ref[...] = part

        @pl.when(pl.program_id(0) > 0)
        def _():
            dg_ref[...] += part

    return pl.pallas_call(
        body, name=name, grid=(t // TOK,),
        in_specs=[_row_spec(D_MODEL), _row_spec(D_MODEL), _vec_spec(D_MODEL)],
        out_specs=[_row_spec(D_MODEL), _vec_spec(D_MODEL)],
        out_shape=[_sds((t, D_MODEL), BF16), _sds((1, D_MODEL), F32)],
        compiler_params=_cparams("arbitrary"))(dxo, m, g)


def _norm_pre_bwd(name, dh, xin, dxo, g):
    t = dh.shape[0]

    def body(dh_ref, x_ref, d_ref, g_ref, dx_ref, dg_ref):
        xv = x_ref[...]
        dhv = dh_ref[...]
        r = _rms(xv)
        n = xv * r
        dn = dhv * g_ref[...]
        dx_ref[...] = d_ref[...] + r * (dn - n * jnp.mean(dn * n, axis=-1, keepdims=True))
        part = jnp.sum(dhv * n, axis=0, keepdims=True)

        @pl.when(pl.program_id(0) == 0)
        def _():
            dg_ref[...] = part

        @pl.when(pl.program_id(0) > 0)
        def _():
            dg_ref[...] += part

    return pl.pallas_call(
        body, name=name, grid=(t // TOK,),
        in_specs=[_row_spec(D_MODEL), _row_spec(D_MODEL), _row_spec(D_MODEL), _vec_spec(D_MODEL)],
        out_specs=[_row_spec(D_MODEL), _vec_spec(D_MODEL)],
        out_shape=[_sds((t, D_MODEL), F32), _sds((1, D_MODEL), F32)],
        compiler_params=_cparams("arbitrary"))(dh, xin, dxo, g)


def _loss_head(y, target):
    t = y.shape[0]

    def body(y_ref, t_ref, dy_ref, l_ref):
        e = y_ref[...] - t_ref[...]
        dy_ref[...] = e * (1.0 / D_MODEL)
        part = jnp.sum(jnp.sum(e * e, axis=0, keepdims=True), axis=1, keepdims=True)

        @pl.when(pl.program_id(0) == 0)
        def _():
            l_ref[...] = part

        @pl.when(pl.program_id(0) > 0)
        def _():
            l_ref[...] += part

    dy, sq = pl.pallas_call(
        body, name="loss_head", grid=(t // TOK,),
        in_specs=[_row_spec(D_MODEL), _row_spec(D_MODEL)],
        out_specs=[_row_spec(D_MODEL), pl.BlockSpec((1, 1), lambda i: (0, 0))],
        out_shape=[_sds((t, D_MODEL), F32), _sds((1, 1), F32)],
        compiler_params=_cparams("arbitrary"))(y, target)
    return dy, sq[0, 0] * (0.5 / D_MODEL)


def _gate_fwd(name, proj, b_gate, ya, yb):
    t = proj.shape[0]

    def body(ga_ref, gb_ref, b_ref, ya_ref, yb_ref, z_ref):
        sa = jax.nn.sigmoid(ga_ref[...].astype(F32) + b_ref[:, :D_MODEL])
        sb = jax.nn.sigmoid(gb_ref[...].astype(F32) + b_ref[:, D_MODEL:])
        z_ref[...] = (sa * ya_ref[...].astype(F32) + sb * yb_ref[...].astype(F32)).astype(BF16)

    return pl.pallas_call(
        body, name=name, grid=(t // TOK,),
        in_specs=[_row_spec(D_MODEL, 2), _row_spec(D_MODEL, 3), _vec_spec(2 * D_MODEL),
                  _row_spec(D_MODEL), _row_spec(D_MODEL)],
        out_specs=_row_spec(D_MODEL), out_shape=_sds((t, D_MODEL), BF16),
        compiler_params=_cparams("parallel"))(proj, proj, b_gate, ya, yb)


def _gate_bwd(name, dz, proj, b_gate, ya, yb):
    t = proj.shape[0]

    def body(dz_ref, ga_ref, gb_ref, b_ref, ya_ref, yb_ref, dya_ref, dyb_ref, dg_ref, db_ref):
        dzv = dz_ref[...].astype(F32)
        sa = jax.nn.sigmoid(ga_ref[...].astype(F32) + b_ref[:, :D_MODEL])
        sb = jax.nn.sigmoid(gb_ref[...].astype(F32) + b_ref[:, D_MODEL:])
        dya_ref[...] = (dzv * sa).astype(BF16)
        dyb_ref[...] = (dzv * sb).astype(BF16)
        dga = dzv * ya_ref[...].astype(F32) * sa * (1.0 - sa)
        dgb = dzv * yb_ref[...].astype(F32) * sb * (1.0 - sb)
        dg_ref[:, :D_MODEL] = dga.astype(BF16)
        dg_ref[:, D_MODEL:] = dgb.astype(BF16)
        pa = jnp.sum(dga, axis=0, keepdims=True)
        pb = jnp.sum(dgb, axis=0, keepdims=True)

        @pl.when(pl.program_id(0) == 0)
        def _():
            db_ref[:, :D_MODEL] = pa
            db_ref[:, D_MODEL:] = pb

        @pl.when(pl.program_id(0) > 0)
        def _():
            db_ref[:, :D_MODEL] += pa
            db_ref[:, D_MODEL:] += pb

    return pl.pallas_call(
        body, name=name, grid=(t // TOK,),
        in_specs=[_row_spec(D_MODEL), _row_spec(D_MODEL, 2), _row_spec(D_MODEL, 3),
                  _vec_spec(2 * D_MODEL), _row_spec(D_MODEL), _row_spec(D_MODEL)],
        out_specs=[_row_spec(D_MODEL), _row_spec(D_MODEL), _row_spec(2 * D_MODEL),
                   _vec_spec(2 * D_MODEL)],
        out_shape=[_sds((t, D_MODEL), BF16), _sds((t, D_MODEL), BF16),
                   _sds((t, 2 * D_MODEL), BF16), _sds((1, 2 * D_MODEL), F32)],
        compiler_params=_cparams("arbitrary"))(dz, proj, proj, b_gate, ya, yb)


def _head_masks():
    lane = lax.broadcasted_iota(jnp.int32, (1, 2 * HEAD_DIM), 1)
    return lane < HEAD_DIM


BAND_ROWS = 2 * ATT_BLK + CHUNK


def _fill_band(band, prev_ref, cur_ref):
    band[0:ATT_BLK, :] = prev_ref[...]
    band[ATT_BLK:2 * ATT_BLK, :] = cur_ref[...]
    band[2 * ATT_BLK:, :] = jnp.zeros((CHUNK, ATTN_W), BF16)


def _softmax_rows(sc):
    m = jnp.max(sc, axis=-1, keepdims=True)
    e = jnp.exp(sc - m)
    return e / jnp.sum(e, axis=-1, keepdims=True)


def _attn_specs(nblk):
    cur = lambda col: pl.BlockSpec((ATT_BLK, ATTN_W), lambda s: (jnp.minimum(s, nblk - 1), col))
    prev = lambda col: pl.BlockSpec(
        (ATT_BLK, ATTN_W), lambda s: (jnp.maximum(jnp.minimum(s, nblk - 1) - 1, 0), col))
    return cur, prev


def _attn_fwd(name, proj, bias):
    t = proj.shape[0]
    nblk = t // ATT_BLK
    cur, prev = _attn_specs(nblk)

    def body(q_ref, kp_ref, kc_ref, vp_ref, vc_ref, b_ref, o_ref, kband, vband):
        s = pl.program_id(0)
        _fill_band(kband, kp_ref, kc_ref)
        _fill_band(vband, vp_ref, vc_ref)
        kpos = lax.broadcasted_iota(jnp.int32, (CHUNK, BAND_PAD), 1)
        low = _head_masks()

        def chunk(ci, carry):
            r0 = pl.multiple_of(ci * CHUNK, CHUNK)
            valid = ((kpos + (s * 8 - 8 + ci) * CHUNK) >= 0) & (kpos < BAND)
            for hp in range(HEADS // 2):
                cols = slice(hp * 128, (hp + 1) * 128)
                q2 = q_ref[pl.ds(r0, CHUNK), cols]
                k2 = kband[pl.ds(r0, BAND_PAD), cols]
                v2 = vband[pl.ds(r0, BAND_PAD), cols]
                outs = []
                for par in range(2):
                    qm = jnp.where(low if par == 0 else ~low, q2, jnp.zeros_like(q2))
                    sc = lax.dot_general(qm, k2, (NT, ((), ())), preferred_element_type=F32)
                    sc = sc * ATTN_SCALE + b_ref[2 * hp + par]
                    p = _softmax_rows(jnp.where(valid, sc, -1e30))
                    outs.append(jnp.dot(p.astype(BF16), v2, preferred_element_type=F32))
                o_ref[pl.ds(r0, CHUNK), cols] = jnp.where(low, outs[0], outs[1]).astype(BF16)
            return carry

        lax.fori_loop(0, 8, chunk, 0)

    return pl.pallas_call(
        body, name=name, grid=(nblk,),
        in_specs=[cur(0), prev(1), cur(1), prev(2), cur(2),
                  pl.BlockSpec((HEADS, CHUNK, BIAS_LANES), lambda s: (0, 0, 0))],
        out_specs=pl.BlockSpec((ATT_BLK, ATTN_W), lambda s: (s, 0)),
        out_shape=_sds((t, ATTN_W), BF16),
        scratch_shapes=[pltpu.VMEM((BAND_ROWS, ATTN_W), BF16),
                        pltpu.VMEM((BAND_ROWS, ATTN_W), BF16)],
        compiler_params=_cparams("arbitrary"))(proj, proj, proj, proj, proj, bias)


def _attn_bwd(name, proj, datt, bias):
    t = proj.shape[0]
    nblk = t // ATT_BLK
    cur, prev = _attn_specs(nblk)
    late = pl.BlockSpec((ATT_BLK, ATTN_W), lambda s: (jnp.maximum(s - 1, 0), 0))

    def body(q_ref, kp_ref, kc_ref, vp_ref, vc_ref, do_ref, b_ref,
             dq_ref, dk_ref, dv_ref, db_ref, kband, vband, dkacc, dvacc):
        s = pl.program_id(0)

        @pl.when(s == 0)
        def _():
            dkacc[...] = jnp.zeros_like(dkacc)
            dvacc[...] = jnp.zeros_like(dvacc)
            db_ref[...] = jnp.zeros_like(db_ref)

        @pl.when(s < nblk)
        def _():
            _fill_band(kband, kp_ref, kc_ref)
            _fill_band(vband, vp_ref, vc_ref)
            kpos = lax.broadcasted_iota(jnp.int32, (CHUNK, BAND_PAD), 1)
            low = _head_masks()

            def chunk(ci, carry):
                r0 = pl.multiple_of(ci * CHUNK, CHUNK)
                valid = ((kpos + (s * 8 - 8 + ci) * CHUNK) >= 0) & (kpos < BAND)
                for hp in range(HEADS // 2):
                    cols = slice(hp * 128, (hp + 1) * 128)
                    q2 = q_ref[pl.ds(r0, CHUNK), cols]
                    do2 = do_ref[pl.ds(r0, CHUNK), cols]
                    k2 = kband[pl.ds(r0, BAND_PAD), cols]
                    v2 = vband[pl.ds(r0, BAND_PAD), cols]
                    dqs, dks, dvs = [], [], []
                    for par in range(2):
                        msk = low if par == 0 else ~low
                        qm = jnp.where(msk, q2, jnp.zeros_like(q2))
                        dom = jnp.where(msk, do2, jnp.zeros_like(do2))
                        sc = lax.dot_general(qm, k2, (NT, ((), ())), preferred_element_type=F32)
                        sc = sc * ATTN_SCALE + b_ref[2 * hp + par]
                        p = _softmax_rows(jnp.where(valid, sc, -1e30))
                        dp = lax.dot_general(dom, v2, (NT, ((), ())), preferred_element_type=F32)
                        ds = p * (dp - jnp.sum(p * dp, axis=-1, keepdims=True))
                        db_ref[2 * hp + par] += ds
                        dsb = (ds * ATTN_SCALE).astype(BF16)
                        pb = p.astype(BF16)
                        dqs.append(jnp.dot(dsb, k2, preferred_element_type=F32))
                        dks.append(lax.dot_general(dsb, q2, (TN, ((), ())),
                                                   preferred_element_type=F32))
                        dvs.append(lax.dot_general(pb, do2, (TN, ((), ())),
                                                   preferred_element_type=F32))
                    dq_ref[pl.ds(r0, CHUNK), cols] = jnp.where(low, dqs[0], dqs[1]).astype(BF16)
                    dkacc[pl.ds(r0, BAND_PAD), cols] += jnp.where(low, dks[0], dks[1])
                    dvacc[pl.ds(r0, BAND_PAD), cols] += jnp.where(low, dvs[0], dvs[1])
                return carry

            lax.fori_loop(0, 8, chunk, 0)

        dk_ref[...] = dkacc[0:ATT_BLK, :].astype(BF16)
        dv_ref[...] = dvacc[0:ATT_BLK, :].astype(BF16)
        dkacc[0:ATT_BLK, :] = dkacc[ATT_BLK:2 * ATT_BLK, :]
        dvacc[0:ATT_BLK, :] = dvacc[ATT_BLK:2 * ATT_BLK, :]
        dkacc[ATT_BLK:, :] = jnp.zeros((ATT_BLK + CHUNK, ATTN_W), F32)
        dvacc[ATT_BLK:, :] = jnp.zeros((ATT_BLK + CHUNK, ATTN_W), F32)

    blk = _sds((t, ATTN_W), BF16)
    return pl.pallas_call(
        body, name=name, grid=(nblk + 1,),
        in_specs=[cur(0), prev(1), cur(1), prev(2), cur(2),
                  pl.BlockSpec((ATT_BLK, ATTN_W), lambda s: (jnp.minimum(s, nblk - 1), 0)),
                  pl.BlockSpec((HEADS, CHUNK, BIAS_LANES), lambda s: (0, 0, 0))],
        out_specs=[pl.BlockSpec((ATT_BLK, ATTN_W), lambda s: (jnp.minimum(s, nblk - 1), 0)),
                   late, late,
                   pl.BlockSpec((HEADS, CHUNK, BIAS_LANES), lambda s: (0, 0, 0))],
        out_shape=[blk, blk, blk, _sds((HEADS, CHUNK, BIAS_LANES), F32)],
        scratch_shapes=[pltpu.VMEM((BAND_ROWS, ATTN_W), BF16),
                        pltpu.VMEM((BAND_ROWS, ATTN_W), BF16),
                        pltpu.VMEM((BAND_ROWS, ATTN_W), F32),
                        pltpu.VMEM((BAND_ROWS, ATTN_W), F32)],
        compiler_params=_cparams("arbitrary"))(proj, proj, proj, proj, proj, datt, bias)


def _bias_table(rel_bias_l):
    lead = (BAND_CHUNKS - 1) * CHUNK
    dist = jnp.arange(CHUNK)[:, None] + lead - jnp.arange(BAND)[None, :]
    idx = jnp.clip(dist, -MAX_REL, MAX_REL) + MAX_REL
    return jnp.pad(rel_bias_l[:, idx], ((0, 0), (0, 0), (0, BIAS_LANES - BAND)))


def _bias_fold(name, dbias):
    rows = HEADS * CHUNK

    def body(d_ref, o_ref):
        rowid = lax.broadcasted_iota(jnp.int32, (8, BIAS_LANES), 0)
        diags = []
        for h in range(HEADS):
            acc = d_ref[h * CHUNK + 56:h * CHUNK + 64, :]
            for a in range(7):
                slab = d_ref[h * CHUNK + 8 * a:h * CHUNK + 8 * a + 8, :]
                acc = acc + pltpu.roll(slab, 56 - 8 * a, axis=1)
            tot = jnp.where(rowid == 7, acc, 0.0)
            for b in range(7):
                tot = tot + jnp.where(rowid == b, pltpu.roll(acc, 7 - b, axis=1), 0.0)
            diags.append(jnp.sum(tot, axis=0, keepdims=True))
        diag = jnp.concatenate(diags, axis=0)
        nn = lax.broadcasted_iota(jnp.int32, (BIAS_LANES, BIAS_LANES), 0)
        rr = lax.broadcasted_iota(jnp.int32, (BIAS_LANES, BIAS_LANES), 1)
        tgt = jnp.minimum(BAND - 1 + MAX_REL - nn, 2 * MAX_REL)
        onehot = jnp.where((rr == tgt) & (nn < BAND + CHUNK - 1), 1.0, 0.0).astype(F32)
        o_ref[...] = jnp.dot(diag, onehot, preferred_element_type=F32,
                             precision=lax.Precision.HIGHEST)

    return pl.pallas_call(
        body, name=name,
        in_specs=[pl.BlockSpec(memory_space=pltpu.VMEM)],
        out_specs=pl.BlockSpec(memory_space=pltpu.VMEM),
        out_shape=_sds((HEADS, BIAS_LANES), F32),
    )(dbias.reshape(rows, BIAS_LANES))


def _inv_counts(i):
    trow = lax.broadcasted_iota(jnp.int32, (TOK + HALO, 1), 0) + i * TOK
    return [1.0 / jnp.minimum(trow + 1, w).astype(F32) for w in POOL_WINDOWS]


def _pool_fwd(name, proj, wg, scale):
    t = proj.shape[0]
    hb = TOK // HALO

    def body(u_ref, up_ref, wg_ref, sc_ref, pooled_ref, mixed_ref, b0, b1, b2, b3):
        i = pl.program_id(0)
        halo = up_ref[...].astype(F32)
        b0[0:HALO, :] = jnp.where(i == 0, jnp.zeros_like(halo), halo)
        b0[HALO:, :] = u_ref[...].astype(F32)
        n = TOK + HALO
        b1[8:n, :] = b0[8:n, :] + b0[7:n - 1, :]
        b2[16:n, 128:] = b1[16:n, 128:] + b1[14:n - 2, 128:]
        b3[24:n, 256:] = b2[24:n, 256:] + b2[20:n - 4, 256:]
        wins = [b1[HALO:n, 0:128], b2[HALO:n, 128:256], b3[HALO:n, 256:384],
                b3[HALO:n, 384:512] + b3[HALO - 8:n - 8, 384:512]]
        inv = _inv_counts(i)
        for g in range(4):
            cols = slice(g * POOL_GD, (g + 1) * POOL_GD)
            pooled = (wins[g] * inv[g][0:TOK] - b0[HALO:n, cols]).astype(BF16)
            pooled_ref[:, cols] = pooled
            pre = jnp.dot(pooled, wg_ref[g], preferred_element_type=F32)
            mixed_ref[:, cols] = (pre * sc_ref[:, cols]).astype(BF16)

    buf = pltpu.VMEM((TOK + HALO, POOL_W), F32)
    return pl.pallas_call(
        body, name=name, grid=(t // TOK,),
        in_specs=[_row_spec(POOL_W, 3),
                  pl.BlockSpec((HALO, POOL_W), lambda i: (jnp.maximum(i * hb - 1, 0), 3)),
                  pl.BlockSpec((4, POOL_GD, POOL_GD), lambda i: (0, 0, 0)),
                  _vec_spec(POOL_W)],
        out_specs=[_row_spec(POOL_W), _row_spec(POOL_W)],
        out_shape=[_sds((t, POOL_W), BF16), _sds((t, POOL_W), BF16)],
        scratch_shapes=[buf, buf, buf, buf],
        compiler_params=_cparams("parallel"))(proj, proj, wg, scale)


def _pool_bwd(name, dmixed, pooled, wg, scale):
    t = dmixed.shape[0]
    nt = t // TOK
    hb = TOK // HALO

    def body(dm_ref, dmn_ref, p_ref, wg_ref, sc_ref, du_ref, dwg_ref, dsc_ref, c0, c1, c2, c3):
        i = pl.program_id(0)

        @pl.when(i == 0)
        def _():
            dwg_ref[...] = jnp.zeros_like(dwg_ref)
            dsc_ref[...] = jnp.zeros_like(dsc_ref)

        n = TOK + HALO
        inv = _inv_counts(i)
        dmv = dm_ref[...].astype(F32)
        dmn = dmn_ref[...].astype(F32)
        dmn = jnp.where(i == nt - 1, jnp.zeros_like(dmn), dmn)
        for g in range(4):
            cols = slice(g * POOL_GD, (g + 1) * POOL_GD)
            scg = sc_ref[:, cols]
            pg = p_ref[:, cols]
            dpre = (dmv[:, cols] * scg).astype(BF16)
            dpre_n = (dmn[:, cols] * scg).astype(BF16)
            pre = jnp.dot(pg, wg_ref[g], preferred_element_type=F32)
            dsc_ref[:, cols] += jnp.sum(dmv[:, cols] * pre, axis=0, keepdims=True)
            dwg_ref[g] += lax.dot_general(pg, dpre, (TN, ((), ())), preferred_element_type=F32)
            dpool = lax.dot_general(dpre, wg_ref[g], (NT, ((), ())), preferred_element_type=F32)
            dpool_n = lax.dot_general(dpre_n, wg_ref[g], (NT, ((), ())),
                                      preferred_element_type=F32)
            c0[0:TOK, cols] = dpool
            c0[TOK:n, cols] = dpool_n
            c1[0:TOK, cols] = dpool * inv[g][0:TOK]
            c1[TOK:n, cols] = dpool_n * inv[g][TOK:n]
        c2[0:n - 8, :] = c1[0:n - 8, :] + c1[1:n - 7, :]
        c3[0:n - 16, 128:] = c2[0:n - 16, 128:] + c2[2:n - 14, 128:]
        c1[0:n - 24, 256:] = c3[0:n - 24, 256:] + c3[4:n - 20, 256:]
        wins = [c2[0:TOK, 0:128], c3[0:TOK, 128:256], c1[0:TOK, 256:384],
                c1[0:TOK, 384:512] + c1[8:TOK + 8, 384:512]]
        for g in range(4):
            cols = slice(g * POOL_GD, (g + 1) * POOL_GD)
            du_ref[:, cols] = (wins[g] - c0[0:TOK, cols]).astype(BF16)

    buf = pltpu.VMEM((TOK + HALO, POOL_W), F32)
    return pl.pallas_call(
        body, name=name, grid=(nt,),
        in_specs=[_row_spec(POOL_W),
                  pl.BlockSpec((HALO, POOL_W), lambda i: (jnp.minimum((i + 1) * hb, nt * hb - 1), 0)),
                  _row_spec(POOL_W),
                  pl.BlockSpec((4, POOL_GD, POOL_GD), lambda i: (0, 0, 0)),
                  _vec_spec(POOL_W)],
        out_specs=[_row_spec(POOL_W), pl.BlockSpec((4, POOL_GD, POOL_GD), lambda i: (0, 0, 0)),
                   _vec_spec(POOL_W)],
        out_shape=[_sds((t, POOL_W), BF16), _sds((4, POOL_GD, POOL_GD), F32),
                   _sds((1, POOL_W), F32)],
        scratch_shapes=[buf, buf, buf, buf],
        compiler_params=_cparams("arbitrary"))(dmixed, dmixed, pooled, wg, scale)


GELU_C = math.sqrt(2.0 / math.pi)


def _gelu(x):
    return 0.5 * x * (1.0 + jnp.tanh(GELU_C * (x + 0.044715 * x * x * x)))


def _gelu_grad(x):
    th = jnp.tanh(GELU_C * (x + 0.044715 * x * x * x))
    return 0.5 * (1.0 + th) + 0.5 * x * (1.0 - th * th) * GELU_C * (1.0 + 3 * 0.044715 * x * x)


def _conv_rows(buf, w_ref, b_ref, start, rows):
    return (b_ref[...] + w_ref[2:3, :] * buf[start:start + rows, :]
            + w_ref[1:2, :] * buf[start - 1:start - 1 + rows, :]
            + w_ref[0:1, :] * buf[start - 2:start - 2 + rows, :])


def _ffn_gate_fwd(name, hu, conv_w, conv_b):
    t = hu.shape[0]
    ncol = D_FF // FF_COL
    hb = TOK // 8

    def tile(off):
        return pl.BlockSpec((TOK, FF_COL), lambda i, j: (i, j + off))

    def halo(off):
        return pl.BlockSpec((8, FF_COL), lambda i, j: (jnp.maximum(i * hb - 1, 0), j + off))

    def wspec(off):
        return pl.BlockSpec((3, FF_COL), lambda i, j: (0, j + off))

    def bspec(off):
        return pl.BlockSpec((1, FF_COL), lambda i, j: (0, j + off))

    def body(v_ref, vp_ref, g_ref, gp_ref, wv_ref, wg_ref, bv_ref, bg_ref, a_ref, vb, gb):
        i = pl.program_id(0)
        for src, prv, dst in ((v_ref, vp_ref, vb), (g_ref, gp_ref, gb)):
            h = prv[...].astype(F32)
            dst[0:8, :] = jnp.where(i == 0, jnp.zeros_like(h), h)
            dst[8:, :] = src[...].astype(F32)
        val = _conv_rows(vb, wv_ref, bv_ref, 8, TOK)
        gate = _conv_rows(gb, wg_ref, bg_ref, 8, TOK)
        a_ref[...] = (_gelu(gate) * val).astype(BF16)

    buf = pltpu.VMEM((TOK + 8, FF_COL), F32)
    return pl.pallas_call(
        body, name=name, grid=(t // TOK, ncol),
        in_specs=[tile(0), halo(0), tile(ncol), halo(ncol), wspec(0), wspec(ncol),
                  bspec(0), bspec(ncol)],
        out_specs=pl.BlockSpec((TOK, FF_COL), lambda i, j: (i, j)),
        out_shape=_sds((t, D_FF), BF16), scratch_shapes=[buf, buf],
        compiler_params=_cparams("parallel", "parallel"))(
            hu, hu, hu, hu, conv_w, conv_w, conv_b, conv_b)


def _ffn_gate_bwd(name, da, hu, conv_w, conv_b):
    t = hu.shape[0]
    nt = t // TOK
    ncol = D_FF // FF_COL
    hb = TOK // 8
    ext = TOK + 8

    def tile(off):
        return pl.BlockSpec((TOK, FF_COL), lambda j, i: (i, j + off))

    def prev(off):
        return pl.BlockSpec((8, FF_COL), lambda j, i: (jnp.maximum(i * hb - 1, 0), j + off))

    def nxt(off):
        return pl.BlockSpec((8, FF_COL), lambda j, i: (jnp.minimum((i + 1) * hb, nt * hb - 1), j + off))

    def wspec(off):
        return pl.BlockSpec((3, FF_COL), lambda j, i: (0, j + off))

    def bspec(off):
        return pl.BlockSpec((1, FF_COL), lambda j, i: (0, j + off))

    def body(da_ref, dan_ref, v_ref, vp_ref, vn_ref, g_ref, gp_ref, gn_ref,
             wv_ref, wg_ref, bv_ref, bg_ref, dhv_ref, dhg_ref, dwv_ref, dwg_ref,
             vb, gb, dvb, dgb):
        i = pl.program_id(1)

        @pl.when(i == 0)
        def _():
            dwv_ref[...] = jnp.zeros_like(dwv_ref)
            dwg_ref[...] = jnp.zeros_like(dwg_ref)

        for src, prv, nx, dst in ((v_ref, vp_ref, vn_ref, vb), (g_ref, gp_ref, gn_ref, gb)):
            h = prv[...].astype(F32)
            dst[0:8, :] = jnp.where(i == 0, jnp.zeros_like(h), h)
            dst[8:8 + TOK, :] = src[...].astype(F32)
            dst[8 + TOK:, :] = nx[...].astype(F32)
        val = _conv_rows(vb, wv_ref, bv_ref, 8, ext)
        gate = _conv_rows(gb, wg_ref, bg_ref, 8, ext)
        dan = dan_ref[...].astype(F32)
        dan = jnp.where(i == nt - 1, jnp.zeros_like(dan), dan)
        da_ext = jnp.concatenate([da_ref[...].astype(F32), dan], axis=0)
        dvb[...] = da_ext * _gelu(gate)
        dgb[...] = da_ext * val * _gelu_grad(gate)
        for dbuf, hbuf, w_ref, dh_ref, dw_ref in ((dvb, vb, wv_ref, dhv_ref, dwv_ref),
                                                  (dgb, gb, wg_ref, dhg_ref, dwg_ref)):
            d0 = dbuf[0:TOK, :]
            dh_ref[...] = (w_ref[2:3, :] * d0 + w_ref[1:2, :] * dbuf[1:TOK + 1, :]
                           + w_ref[0:1, :] * dbuf[2:TOK + 2, :]).astype(BF16)
            dw_ref[0:1, :] += jnp.sum(d0 * hbuf[6:6 + TOK, :], axis=0, keepdims=True)
            dw_ref[1:2, :] += jnp.sum(d0 * hbuf[7:7 + TOK, :], axis=0, keepdims=True)
            dw_ref[2:3, :] += jnp.sum(d0 * hbuf[8:8 + TOK, :], axis=0, keepdims=True)
            dw_ref[3:4, :] += jnp.sum(d0, axis=0, keepdims=True)

    hbuf = pltpu.VMEM((TOK + 16, FF_COL), F32)
    dbuf = pltpu.VMEM((ext, FF_COL), F32)
    dhu_v, dhu_g, dwv, dwg = pl.pallas_call(
        body, name=name, grid=(ncol, nt),
        in_specs=[tile(0), nxt(0), tile(0), prev(0), nxt(0), tile(ncol), prev(ncol), nxt(ncol),
                  wspec(0), wspec(ncol), bspec(0), bspec(ncol)],
        out_specs=[pl.BlockSpec((TOK, FF_COL), lambda j, i: (i, j)),
                   pl.BlockSpec((TOK, FF_COL), lambda j, i: (i, j)),
                   pl.BlockSpec((8, FF_COL), lambda j, i: (0, j)),
                   pl.BlockSpec((8, FF_COL), lambda j, i: (0, j))],
        out_shape=[_sds((t, D_FF), BF16), _sds((t, D_FF), BF16),
                   _sds((8, D_FF), F32), _sds((8, D_FF), F32)],
        scratch_shapes=[hbuf, hbuf, dbuf, dbuf],
        compiler_params=_cparams("parallel", "arbitrary"))(
            da, da, hu, hu, hu, hu, hu, hu, conv_w, conv_w, conv_b, conv_b)
    return (jnp.concatenate([dhu_v, dhu_g], axis=1), jnp.concatenate([dwv, dwg], axis=1))


def _mesh_pos():
    x, y, c = lax.axis_index("x"), lax.axis_index("y"), lax.axis_index("c")
    return x, y, c, [(1 - x, y), (x, 1 - y), (1 - x, 1 - y)]


def _any_specs(n):
    return [pl.BlockSpec(memory_space=pl.ANY)] * n


def _allgather_weights(shards):
    n = len(shards)

    def body(*refs):
        ins, outs = refs[:n], refs[n:2 * n]
        send_sems, recv_sems, local_sems = refs[2 * n:]
        x, y, c, chips = _mesh_pos()
        me = 2 * x + y
        started = []
        local = []
        for k in range(n):
            for l in range(2):
                cp = pltpu.make_async_copy(ins[k].at[l], outs[k].at[l, me], local_sems.at[k, l])
                cp.start()
                local.append(cp)
            for j, (cx, cy) in enumerate(chips):
                cp = pltpu.make_async_remote_copy(
                    src_ref=ins[k].at[c], dst_ref=outs[k].at[c, me],
                    send_sem=send_sems.at[k, j], recv_sem=recv_sems.at[k, j],
                    device_id=(cx, cy, c), device_id_type=MESH)
                cp.start()
                started.append(cp)
        for k in range(n):
            for j, (cx, cy) in enumerate(chips):
                landed = outs[k].at[c, 2 * cx + cy]
                pltpu.make_async_remote_copy(
                    src_ref=ins[k].at[c], dst_ref=landed,
                    send_sem=send_sems.at[k, j], recv_sem=recv_sems.at[k, j],
                    device_id=(cx, cy, c), device_id_type=MESH).wait_recv()
                fw = pltpu.make_async_remote_copy(
                    src_ref=landed, dst_ref=landed,
                    send_sem=send_sems.at[k, 3 + j], recv_sem=recv_sems.at[k, 3 + j],
                    device_id=(x, y, 1 - c), device_id_type=MESH)
                fw.start()
                started.append(fw)
        for k in range(n):
            for j, (cx, cy) in enumerate(chips):
                theirs = outs[k].at[1 - c, 2 * cx + cy]
                pltpu.make_async_remote_copy(
                    src_ref=theirs, dst_ref=theirs,
                    send_sem=send_sems.at[k, 3 + j], recv_sem=recv_sems.at[k, 3 + j],
                    device_id=(x, y, 1 - c), device_id_type=MESH).wait_recv()
        for cp in started:
            cp.wait_send()
        for cp in local:
            cp.wait()

    return pl.pallas_call(
        body, name="allgather_weights",
        in_specs=_any_specs(n), out_specs=_any_specs(n),
        out_shape=[_sds((2, N_CHIPS) + s.shape[1:], s.dtype) for s in shards],
        scratch_shapes=[pltpu.SemaphoreType.DMA((n, 6)), pltpu.SemaphoreType.DMA((n, 6)),
                        pltpu.SemaphoreType.DMA((n, 2))],
    )(*shards)


def _swap_layers(grads):
    n = len(grads)

    def body(*refs):
        ins, outs = refs[:n], refs[n:2 * n]
        send_sems, recv_sems = refs[2 * n:]
        x, y, c, _ = _mesh_pos()
        cps = []
        for k in range(n):
            cp = pltpu.make_async_remote_copy(
                src_ref=ins[k].at[1 - c], dst_ref=outs[k],
                send_sem=send_sems.at[k], recv_sem=recv_sems.at[k],
                device_id=(x, y, 1 - c), device_id_type=MESH)
            cp.start()
            cps.append(cp)
        for cp in cps:
            cp.wait()

    return pl.pallas_call(
        body, name="swap_layers",
        in_specs=_any_specs(n), out_specs=_any_specs(n),
        out_shape=[_sds(g.shape[1:], g.dtype) for g in grads],
        scratch_shapes=[pltpu.SemaphoreType.DMA((n,)), pltpu.SemaphoreType.DMA((n,))],
    )(*grads)


def _scatter_blocks(sums):
    n = len(sums)

    def body(*refs):
        ins, outs = refs[:n], refs[n:2 * n]
        send_sems, recv_sems = refs[2 * n:]
        x, y, c, chips = _mesh_pos()
        cps = []
        for k in range(n):
            for j, (cx, cy) in enumerate(chips):
                cp = pltpu.make_async_remote_copy(
                    src_ref=ins[k].at[2 * cx + cy], dst_ref=outs[k].at[j],
                    send_sem=send_sems.at[k, j], recv_sem=recv_sems.at[k, j],
                    device_id=(cx, cy, c), device_id_type=MESH)
                cp.start()
                cps.append(cp)
        for cp in cps:
            cp.wait()

    return pl.pallas_call(
        body, name="scatter_blocks",
        in_specs=_any_specs(n), out_specs=_any_specs(n),
        out_shape=[_sds((3,) + s.shape[1:], s.dtype) for s in sums],
        scratch_shapes=[pltpu.SemaphoreType.DMA((n, 3)), pltpu.SemaphoreType.DMA((n, 3))],
    )(*sums)


def _exchange_reduced(reds):
    n = len(reds)

    def body(*refs):
        ins, outs = refs[:n], refs[n:2 * n]
        send_sems, recv_sems, local_sems = refs[2 * n:]
        x, y, c, _ = _mesh_pos()
        cps = []
        for k in range(n):
            lc = pltpu.make_async_copy(ins[k], outs[k].at[c], local_sems.at[k])
            lc.start()
            cps.append(lc)
            cp = pltpu.make_async_remote_copy(
                src_ref=ins[k], dst_ref=outs[k].at[c],
                send_sem=send_sems.at[k], recv_sem=recv_sems.at[k],
                device_id=(x, y, 1 - c), device_id_type=MESH)
            cp.start()
            cps.append(cp)
        for cp in cps:
            cp.wait()

    return pl.pallas_call(
        body, name="exchange_reduced",
        in_specs=_any_specs(n), out_specs=_any_specs(n),
        out_shape=[_sds((2,) + r.shape, r.dtype) for r in reds],
        scratch_shapes=[pltpu.SemaphoreType.DMA((n,)), pltpu.SemaphoreType.DMA((n,)),
                        pltpu.SemaphoreType.DMA((n,))],
    )(*reds)


def _allreduce_small(pack):
    n = pack.shape[0]

    def body(x_ref, o_ref, gbuf, send_sems, recv_sems):
        x, y, c, chips = _mesh_pos()
        sibling = (x, y, 1 - c)

        def slot(px, py, pc):
            return gbuf.at[4 * px + 2 * py + pc]

        def copy(k, block, to, src=None):
            return pltpu.make_async_remote_copy(
                src_ref=slot(*block) if src is None else src, dst_ref=slot(*block),
                send_sem=send_sems.at[k], recv_sem=recv_sems.at[k],
                device_id=to, device_id_type=MESH)

        me = (x, y, c)
        first = [copy(0, me, sibling, src=x_ref)]
        first += [copy(1 + j, me, (*chip, c), src=x_ref) for j, chip in enumerate(chips)]
        for cp in first:
            cp.start()
        gbuf[4 * x + 2 * y + c] = x_ref[...]
        passed = [copy(4 + j, (*chip, c), sibling) for j, chip in enumerate(chips)]
        for j, chip in enumerate(chips):
            copy(1 + j, (*chip, c), me).wait_recv()
            passed[j].start()
        copy(0, sibling, me).wait_recv()
        for j, chip in enumerate(chips):
            copy(4 + j, (*chip, 1 - c), me).wait_recv()
        for cp in first + passed:
            cp.wait_send()
        acc = gbuf[0]
        for d in range(1, 8):
            acc = acc + gbuf[d]
        o_ref[...] = acc

    return pl.pallas_call(
        body, name="allreduce_small",
        in_specs=[pl.BlockSpec(memory_space=pltpu.VMEM)],
        out_specs=pl.BlockSpec(memory_space=pltpu.VMEM),
        out_shape=_sds((n, 128), F32),
        scratch_shapes=[pltpu.VMEM((8, n, 128), F32), pltpu.SemaphoreType.DMA((7,)),
                        pltpu.SemaphoreType.DMA((7,))],
        compiler_params=pltpu.CompilerParams(vmem_limit_bytes=VMEM_LIMIT_V7X),
    )(pack)


def _core_index():
    return jnp.reshape(lax.axis_index("c"), (1,)).astype(jnp.int32)


def _chip_index():
    return jnp.reshape(2 * lax.axis_index("x") + lax.axis_index("y"), (1,)).astype(jnp.int32)


def _chip_sum(name, stacked, sib):
    _, nb, r, cdim = stacked.shape

    def body(c_ref, a_ref, b_ref, o_ref):
        o_ref[...] = (a_ref[...].astype(F32) + b_ref[...].astype(F32)).astype(BF16)

    return pl.pallas_call(
        body, name=name,
        grid_spec=pltpu.PrefetchScalarGridSpec(
            num_scalar_prefetch=1, grid=(nb,),
            in_specs=[pl.BlockSpec((None, None, r, cdim), lambda j, cr: (cr[0], j, 0, 0)),
                      pl.BlockSpec((None, r, cdim), lambda j, cr: (j, 0, 0))],
            out_specs=pl.BlockSpec((None, r, cdim), lambda j, cr: (j, 0, 0))),
        out_shape=_sds((nb, r, cdim), BF16),
        compiler_params=_cparams("parallel"))(_core_index(), stacked, sib)


def _final_sum(name, sums, recv):
    _, r, cdim = sums.shape
    tr = r // 2

    def body(m_ref, a_ref, b_ref, o_ref):
        acc = a_ref[...].astype(F32)
        for j in range(3):
            acc = acc + b_ref[j].astype(F32)
        o_ref[...] = acc

    return pl.pallas_call(
        body, name=name,
        grid_spec=pltpu.PrefetchScalarGridSpec(
            num_scalar_prefetch=1, grid=(2,),
            in_specs=[pl.BlockSpec((None, tr, cdim), lambda i, mr: (mr[0], i, 0)),
                      pl.BlockSpec((3, tr, cdim), lambda i, mr: (0, i, 0))],
            out_specs=pl.BlockSpec((tr, cdim), lambda i, mr: (i, 0))),
        out_shape=_sds((r, cdim), F32),
        compiler_params=_cparams("parallel"))(_chip_index(), sums, recv)


def _adamw(name, w, g, m, v):
    nl, r, cdim = w.shape
    tr = r // 4 if r % 32 == 0 else r
    c1 = 1.0 - ADAM_B1 ** ADAM_STEP
    c2 = 1.0 - ADAM_B2 ** ADAM_STEP

    def body(w_ref, g_ref, m_ref, v_ref, d_ref, nm_ref, nv_ref):
        gv = g_ref[...]
        nm = ADAM_B1 * m_ref[...] + (1.0 - ADAM_B1) * gv
        nv = ADAM_B2 * v_ref[...] + (1.0 - ADAM_B2) * (gv * gv)
        nm_ref[...] = nm
        nv_ref[...] = nv
        d_ref[...] = -ADAM_LR * ((nm / c1) / (jnp.sqrt(nv / c2) + ADAM_EPS) + ADAM_WD * w_ref[...])

    spec = pl.BlockSpec((None, tr, cdim), lambda l, i: (l, i, 0))
    out = _sds(w.shape, F32)
    return pl.pallas_call(
        body, name=name, grid=(nl, r // tr),
        in_specs=[spec] * 4, out_specs=[spec] * 3, out_shape=[out] * 3,
        compiler_params=_cparams("parallel", "parallel"))(w, g, m, v)


def _rows128(a):
    return a.reshape(-1, 128)


def kernel(x, norm_mix_pre, w_in, b_gate, rel_bias, w_attn_out, w_pool_group, pool_scale, w_pool_out, w_o, norm_mix_post, norm_ffn_pre, w_up, conv_w, conv_b, w_down, norm_ffn_post, loss_target, m_norm_mix_pre, m_w_in, m_b_gate, m_rel_bias, m_w_attn_out, m_w_pool_group, m_pool_scale, m_w_pool_out, m_w_o, m_norm_mix_post, m_norm_ffn_pre, m_w_up, m_conv_w, m_conv_b, m_w_down, m_norm_ffn_post, v_norm_mix_pre, v_w_in, v_b_gate, v_rel_bias, v_w_attn_out, v_w_pool_group, v_pool_scale, v_w_pool_out, v_w_o, v_norm_mix_post, v_norm_ffn_pre, v_w_up, v_conv_w, v_conv_b, v_w_down, v_norm_ffn_post):
    t = x.shape[1]
    xs = x.reshape(t, D_MODEL)
    target = loss_target.reshape(t, D_MODEL)

    big = [w_in, w_attn_out, w_pool_out, w_o, w_up, w_down]
    gathered = _allgather_weights([w.astype(BF16) for w in big] + [conv_w])
    win_g, wao_g, wpo_g, wo_g, wup_g, wdn_g, cw_g = gathered
    wo_full = wo_g.reshape(DEPTH, D_MODEL, D_MODEL)
    wdn_full = wdn_g.reshape(DEPTH, D_FF, D_MODEL)
    cw_full = jnp.transpose(cw_g, (0, 2, 1, 3)).reshape(DEPTH, 3, 2 * D_FF)
    wg_bf = w_pool_group.astype(BF16)

    saved = []
    xcur = xs
    for l in range(DEPTH):
        tag = f"l{l}_"
        bias = _bias_table(rel_bias[l])
        h = _norm_fwd(tag + "norm_mix_pre", xcur, norm_mix_pre[l:l + 1])
        proj = _mm_nn_blocked(tag + "proj", h, win_g, l, BF16)
        att = _attn_fwd(tag + "attn_fwd", proj, bias)
        pooled, mixed = _pool_fwd(tag + "pool_fwd", proj, wg_bf[l], pool_scale[l:l + 1])
        ya = _mm_nn_blocked(tag + "attn_out", att, wao_g, l, BF16)
        yb = _mm_nn_blocked(tag + "pool_out", mixed, wpo_g, l, BF16)
        z = _gate_fwd(tag + "gate_fwd", proj, b_gate[l:l + 1], ya, yb)
        mix = _mm_nn(tag + "mix", z, wo_full, l, D_MODEL, F32)
        x1 = _norm_residual_fwd(tag + "norm_mix_post", xcur, mix, norm_mix_post[l:l + 1])
        h2 = _norm_fwd(tag + "norm_ffn_pre", x1, norm_ffn_pre[l:l + 1])
        hu = _mm_nn_blocked(tag + "ffn_up", h2, wup_g, l, BF16)
        a = _ffn_gate_fwd(tag + "ffn_gate_fwd", hu, cw_full[l], conv_b[l:l + 1])
        f = _mm_nn(tag + "ffn_down", a, wdn_full, l, D_FF // 2, F32)
        x2 = _norm_residual_fwd(tag + "norm_ffn_post", x1, f, norm_ffn_post[l:l + 1])
        saved.append(dict(x=xcur, h=h, proj=proj, att=att, pooled=pooled, mixed=mixed, ya=ya,
                          yb=yb, z=z, mix=mix, x1=x1, h2=h2, hu=hu, a=a, f=f, bias=bias))
        xcur = x2

    dy, loss_local = _loss_head(xcur, target)
    loss = lax.psum(loss_local, ("x", "y", "c"))

    dx = dy
    big_grads = [None] * DEPTH
    small_grads = [None] * DEPTH
    for l in reversed(range(DEPTH)):
        tag = f"l{l}_"
        sv = saved[l]
        df, d_nfpost = _norm_post_bwd(tag + "norm_ffn_post_bwd", dx, sv["f"], norm_ffn_post[l:l + 1])
        da = _mm_nt(tag + "ffn_down_dx", df, wdn_full, l, D_FF // 2, BF16)
        dw_down = _mm_tn(tag + "ffn_down_dw", sv["a"], df, D_FF // 2)
        dhu, dconv = _ffn_gate_bwd(tag + "ffn_gate_bwd", da, sv["hu"], cw_full[l], conv_b[l:l + 1])
        dh2 = _mm_nt_blocked(tag + "ffn_up_dx", dhu, wup_g, l, F32)
        dw_up = _mm_tn_blocked(tag + "ffn_up_dw", sv["h2"], dhu)
        dx1, d_nfpre = _norm_pre_bwd(tag + "norm_ffn_pre_bwd", dh2, sv["x1"], dx, norm_ffn_pre[l:l + 1])
        dmix, d_nmpost = _norm_post_bwd(tag + "norm_mix_post_bwd", dx1, sv["mix"], norm_mix_post[l:l + 1])
        dz = _mm_nt(tag + "mix_dx", dmix, wo_full, l, D_MODEL, BF16)
        dw_o = _mm_tn(tag + "mix_dw", sv["z"], dmix, D_MODEL)
        dya, dyb, dgates, d_bgate = _gate_bwd(tag + "gate_bwd", dz, sv["proj"], b_gate[l:l + 1],
                                              sv["ya"], sv["yb"])
        datt = _mm_nt_blocked(tag + "attn_out_dx", dya, wao_g, l, BF16)
        dw_ao = _mm_tn_blocked(tag + "attn_out_dw", sv["att"], dya)
        dmixed = _mm_nt_blocked(tag + "pool_out_dx", dyb, wpo_g, l, BF16)
        dw_po = _mm_tn_blocked(tag + "pool_out_dw", sv["mixed"], dyb)
        du, d_wg, d_pscale = _pool_bwd(tag + "pool_bwd", dmixed, sv["pooled"], wg_bf[l],
                                       pool_scale[l:l + 1])
        dq, dk, dv, dbias = _attn_bwd(tag + "attn_bwd", sv["proj"], datt, sv["bias"])
        d_rel = _bias_fold(tag + "bias_fold", dbias)
        dproj = jnp.concatenate([dq, dk, dv, du, dgates], axis=1)
        dh = _mm_nt_blocked(tag + "proj_dx", dproj, win_g, l, F32)
        dw_in = _mm_tn_blocked(tag + "proj_dw", sv["h"], dproj)
        dx, d_nmpre = _norm_pre_bwd(tag + "norm_mix_pre_bwd", dh, sv["x"], dx1, norm_mix_pre[l:l + 1])
        big_grads[l] = [dw_in, dw_ao, dw_po, dw_o.reshape(N_CHIPS, D_MODEL // N_CHIPS, D_MODEL),
                        dw_up, dw_down.reshape(N_CHIPS, D_FF // N_CHIPS, D_MODEL)]
        small_grads[l] = [d_nmpre, d_nmpost, d_nfpre, d_nfpost, d_bgate, d_rel, d_wg, d_pscale,
                          dconv[3:4], dconv[0:3]]

    grad_x = dx.reshape(x.shape)

    stacked = [jnp.stack([big_grads[0][k], big_grads[1][k]]) for k in range(6)]
    sib = _swap_layers(stacked)
    names = ["w_in", "w_attn_out", "w_pool_out", "w_o", "w_up", "w_down"]
    sums = [_chip_sum("chip_sum_" + names[k], stacked[k], sib[k]) for k in range(6)]
    recv = _scatter_blocks(sums)
    reds = [_final_sum("final_sum_" + names[k], sums[k], recv[k]) for k in range(6)]
    g_big = _exchange_reduced(reds)

    pieces = []
    for idx in range(10):
        pieces.append(jnp.stack([small_grads[0][idx], small_grads[1][idx]]))
    pack = jnp.concatenate([_rows128(p) for p in pieces], axis=0)
    red = _allreduce_small(pack)
    shapes = [p.shape for p in pieces]
    outs = []
    row = 0
    for shp in shapes:
        nrow = math.prod(shp) // 128
        outs.append(red[row:row + nrow].reshape(shp))
        row += nrow
    (g_nmpre, g_nmpost, g_nfpre, g_nfpost, g_bgate, g_rel, g_wg, g_pscale, g_cb, g_cw) = outs
    g_nmpre, g_nmpost, g_nfpre, g_nfpost = [a.reshape(DEPTH, D_MODEL)
                                            for a in (g_nmpre, g_nmpost, g_nfpre, g_nfpost)]
    g_bgate = g_bgate.reshape(DEPTH, 2 * D_MODEL)
    g_rel = g_rel[:, :, :N_REL]
    g_pscale = g_pscale.reshape(DEPTH, POOL_W)
    g_cb = g_cb.reshape(DEPTH, 2 * D_FF)
    ncw = conv_w.shape[2]
    chip = 2 * lax.axis_index("x") + lax.axis_index("y")
    g_cw = lax.dynamic_slice_in_dim(g_cw, chip * ncw, ncw, axis=2)

    grads = dict(norm_mix_pre=g_nmpre, w_in=g_big[0], b_gate=g_bgate, rel_bias=g_rel,
                 w_attn_out=g_big[1], w_pool_group=g_wg, pool_scale=g_pscale, w_pool_out=g_big[2],
                 w_o=g_big[3], norm_mix_post=g_nmpost, norm_ffn_pre=g_nfpre, w_up=g_big[4],
                 conv_w=g_cw, conv_b=g_cb, w_down=g_big[5], norm_ffn_post=g_nfpost)
    weights = dict(norm_mix_pre=norm_mix_pre, w_in=w_in, b_gate=b_gate, rel_bias=rel_bias,
                   w_attn_out=w_attn_out, w_pool_group=w_pool_group, pool_scale=pool_scale,
                   w_pool_out=w_pool_out, w_o=w_o, norm_mix_post=norm_mix_post,
                   norm_ffn_pre=norm_ffn_pre, w_up=w_up, conv_w=conv_w, conv_b=conv_b,
                   w_down=w_down, norm_ffn_post=norm_ffn_post)
    moms = dict(norm_mix_pre=(m_norm_mix_pre, v_norm_mix_pre), w_in=(m_w_in, v_w_in),
                b_gate=(m_b_gate, v_b_gate), rel_bias=(m_rel_bias, v_rel_bias),
                w_attn_out=(m_w_attn_out, v_w_attn_out),
                w_pool_group=(m_w_pool_group, v_w_pool_group),
                pool_scale=(m_pool_scale, v_pool_scale), w_pool_out=(m_w_pool_out, v_w_pool_out),
                w_o=(m_w_o, v_w_o), norm_mix_post=(m_norm_mix_post, v_norm_mix_post),
                norm_ffn_pre=(m_norm_ffn_pre, v_norm_ffn_pre), w_up=(m_w_up, v_w_up),
                conv_w=(m_conv_w, v_conv_w), conv_b=(m_conv_b, v_conv_b),
                w_down=(m_w_down, v_w_down), norm_ffn_post=(m_norm_ffn_post, v_norm_ffn_post))
    order = list(weights.keys())

    delta, new_m, new_v = {}, {}, {}
    small_names = [nm for nm in order if nm not in names]
    for nm in names:
        delta[nm], new_m[nm], new_v[nm] = _adamw("adamw_" + nm, weights[nm], grads[nm], *moms[nm])

    def pack_small(get):
        flat = [get(nm).reshape(-1) for nm in small_names]
        total = sum(f.shape[0] for f in flat)
        padded = -(-total // 1024) * 1024
        flat.append(jnp.zeros((padded - total,), F32))
        return jnp.concatenate(flat).reshape(1, padded // 128, 128)

    d_s, m_s, v_s = _adamw(
        "adamw_small", pack_small(lambda nm: weights[nm]), pack_small(lambda nm: grads[nm]),
        pack_small(lambda nm: moms[nm][0]) , pack_small(lambda nm: moms[nm][1]))
    off = 0
    for nm in small_names:
        size = math.prod(weights[nm].shape)
        for dst, src in ((delta, d_s), (new_m, m_s), (new_v, v_s)):
            dst[nm] = src.reshape(-1)[off:off + size].reshape(weights[nm].shape)
        off += size

    return (loss, grad_x, *[grads[nm] for nm in order], *[delta[nm] for nm in order],
            *[new_m[nm] for nm in order], *[new_v[nm] for nm in order])
```

```python
import functools
import math

import jax
import jax.numpy as jnp
from jax import lax
from jax.experimental import pallas as pl
from jax.experimental.pallas import tpu as pltpu

F32 = jnp.float32
BF16 = jnp.bfloat16
MESH = pl.DeviceIdType.MESH

D_MODEL = 1024
DEPTH = 2
CHUNK = 64
BAND_CHUNKS = 9
BAND = BAND_CHUNKS * CHUNK
HEADS = 8
HEAD_DIM = 64
ATTN_W = HEADS * HEAD_DIM
POOL_WINDOWS = (2, 4, 8, 16)
POOL_W = 512
POOL_GD = 128
MAX_REL = 256
N_REL = 2 * MAX_REL + 1
D_FF = 2816
IN_W = 3 * ATTN_W + POOL_W + 2 * D_MODEL
EPS = 1e-6
ATTN_SCALE = HEAD_DIM ** -0.5
BAND_PAD = 640
BIAS_LANES = BAND_PAD
N_CHIPS = 4

ADAM_LR = 0.001
ADAM_B1 = 0.9
ADAM_B2 = 0.999
ADAM_EPS = 1e-08
ADAM_WD = 0.01
ADAM_STEP = 10

VMEM_LIMIT_V7X = 56 * 1024 * 1024
TOK = 512
ATT_BLK = 8 * CHUNK
FF_COL = 256
HALO = 32


def _cparams(*sem):
    return pltpu.CompilerParams(dimension_semantics=sem, vmem_limit_bytes=VMEM_LIMIT_V7X)


def _sds(shape, dtype):
    return jax.ShapeDtypeStruct(shape, dtype)


def _matmul(name, a, b, a_spec, b_spec, o_spec, out_shape, grid, contract, nk, acc_shape,
            fill=None):
    def body(*refs):
        a_ref, b_ref = refs[0], refs[1]
        o_ref = refs[2 if fill is None else 3]
        scratch = refs[(3 if fill is None else 4):]
        part = lax.dot_general(a_ref[...], b_ref[...], (contract, ((), ())),
                               preferred_element_type=F32)
        if nk == 1:
            o_ref[...] = part.astype(o_ref.dtype)
        else:
            acc_ref = scratch[0]
            k = pl.program_id(2)

            @pl.when(k == 0)
            def _():
                acc_ref[...] = part

            @pl.when(k > 0)
            def _():
                acc_ref[...] += part

            @pl.when(k == nk - 1)
            def _():
                o_ref[...] = acc_ref[...].astype(o_ref.dtype)

    scratch = [] if nk == 1 else [pltpu.VMEM(acc_shape, F32)]
    in_specs, args, aliases = [a_spec, b_spec], [a, b], {}
    if fill is not None:
        in_specs.append(pl.BlockSpec(memory_space=pl.ANY))
        args.append(fill)
        aliases = {2: 0}
    return pl.pallas_call(
        body, name=name, grid=grid, in_specs=in_specs, out_specs=o_spec,
        out_shape=out_shape, scratch_shapes=scratch, input_output_aliases=aliases,
        compiler_params=_cparams("parallel", "parallel", "arbitrary"),
    )(*args)


NN = ((1,), (0,))
NT = ((1,), (1,))
TN = ((0,), (0,))


def _tm(t):
    return min(t, 1024)


def _mm_nn_blocked(name, a, w, l, out_dtype):
    t, k = a.shape
    nb = w.shape[3]
    tm = _tm(t)
    return _matmul(
        name, a, w,
        pl.BlockSpec((tm, k), lambda i, n, kk: (i, 0)),
        pl.BlockSpec((None, None, k, nb), lambda i, n, kk: (l, n, 0, 0)),
        pl.BlockSpec((tm, nb), lambda i, n, kk: (i, n)),
        _sds((t, N_CHIPS * nb), out_dtype), (t // tm, N_CHIPS, 1), NN, 1, None)


def _mm_nt_blocked(name, a, w, l, out_dtype):
    t = a.shape[0]
    k, nb = w.shape[2], w.shape[3]
    tm = _tm(t)
    return _matmul(
        name, a, w,
        pl.BlockSpec((tm, nb), lambda i, n, kk: (i, kk)),
        pl.BlockSpec((None, None, k, nb), lambda i, n, kk: (l, kk, 0, 0)),
        pl.BlockSpec((tm, k), lambda i, n, kk: (i, 0)),
        _sds((t, k), out_dtype), (t // tm, 1, N_CHIPS), NT, N_CHIPS, (tm, k))


def _mm_tn_blocked(name, a, g, l, fill):
    t, k = a.shape
    nb = g.shape[1] // N_CHIPS
    nt = t // TOK
    return _matmul(
        name, a, g,
        pl.BlockSpec((TOK, k), lambda n, j, kk: (kk, 0)),
        pl.BlockSpec((TOK, nb), lambda n, j, kk: (kk, n)),
        pl.BlockSpec((None, None, k, nb), lambda n, j, kk: (l, n, 0, 0)),
        _sds((DEPTH, N_CHIPS, k, nb), BF16), (N_CHIPS, 1, nt), TN, nt, (k, nb), fill)


def _mm_nn(name, a, w, l, tk, out_dtype):
    t, k = a.shape
    n = w.shape[2]
    tm = _tm(t)
    nk = k // tk
    return _matmul(
        name, a, w,
        pl.BlockSpec((tm, tk), lambda i, j, kk: (i, kk)),
        pl.BlockSpec((None, tk, n), lambda i, j, kk: (l, kk, 0)),
        pl.BlockSpec((tm, n), lambda i, j, kk: (i, 0)),
        _sds((t, n), out_dtype), (t // tm, 1, nk), NN, nk, (tm, n))


def _mm_nt(name, a, w, l, tn, out_dtype):
    t, n = a.shape
    k = w.shape[1]
    tm = _tm(t)
    return _matmul(
        name, a, w,
        pl.BlockSpec((tm, n), lambda i, j, kk: (i, 0)),
        pl.BlockSpec((None, tn, n), lambda i, j, kk: (l, j, 0)),
        pl.BlockSpec((tm, tn), lambda i, j, kk: (i, j)),
        _sds((t, k), out_dtype), (t // tm, k // tn, 1), NT, 1, None)


def _mm_tn(name, a, g, tko, l, fill):
    t, k = a.shape
    n = g.shape[1]
    nt = t // TOK
    return _matmul(
        name, a, g,
        pl.BlockSpec((TOK, tko), lambda i, j, kk: (kk, i)),
        pl.BlockSpec((TOK, n), lambda i, j, kk: (kk, 0)),
        pl.BlockSpec((None, tko, n), lambda i, j, kk: (l, i, 0)),
        _sds((DEPTH, k, n), BF16), (k // tko, 1, nt), TN, nt, (tko, n), fill)


def _row_spec(width, col=0):
    return pl.BlockSpec((TOK, width), lambda i: (i, col))


def _vec_spec(width):
    return pl.BlockSpec((1, width), lambda i: (0, 0))


def _rms(x):
    return lax.rsqrt(jnp.mean(x * x, axis=-1, keepdims=True) + EPS)


def _norm_fwd(name, x, g):
    t = x.shape[0]

    def body(x_ref, g_ref, h_ref):
        xv = x_ref[...]
        h_ref[...] = (xv * _rms(xv) * g_ref[...]).astype(BF16)

    return pl.pallas_call(
        body, name=name, grid=(t // TOK,), in_specs=[_row_spec(D_MODEL), _vec_spec(D_MODEL)],
        out_specs=_row_spec(D_MODEL), out_shape=_sds((t, D_MODEL), BF16),
        compiler_params=_cparams("parallel"))(x, g)


def _norm_residual_fwd(name, xres, m, g):
    t = xres.shape[0]

    def body(x_ref, m_ref, g_ref, o_ref):
        mv = m_ref[...]
        o_ref[...] = x_ref[...] + mv * _rms(mv) * g_ref[...]

    return pl.pallas_call(
        body, name=name, grid=(t // TOK,),
        in_specs=[_row_spec(D_MODEL), _row_spec(D_MODEL), _vec_spec(D_MODEL)],
        out_specs=_row_spec(D_MODEL), out_shape=_sds((t, D_MODEL), F32),
        compiler_params=_cparams("parallel"))(xres, m, g)


def _norm_post_bwd(name, dxo, m, g):
    t = dxo.shape[0]

    def body(d_ref, m_ref, g_ref, dm_ref, dg_ref):
        mv = m_ref[...]
        dv = d_ref[...]
        r = _rms(mv)
        n = mv * r
        dn = dv * g_ref[...]
        dm_ref[...] = (r * (dn - n * jnp.mean(dn * n, axis=-1, keepdims=True))).astype(BF16)
        part = jnp.sum(dv * n, axis=0, keepdims=True)

        @pl.when(pl.program_id(0) == 0)
        def _():
            dg_ref[...] = part

        @pl.when(pl.program_id(0) > 0)
        def _():
            dg_ref[...] += part

    return pl.pallas_call(
        body, name=name, grid=(t // TOK,),
        in_specs=[_row_spec(D_MODEL), _row_spec(D_MODEL), _vec_spec(D_MODEL)],
        out_specs=[_row_spec(D_MODEL), _vec_spec(D_MODEL)],
        out_shape=[_sds((t, D_MODEL), BF16), _sds((1, D_MODEL), F32)],
        compiler_params=_cparams("arbitrary"))(dxo, m, g)


def _norm_pre_bwd(name, dh, xin, dxo, g):
    t = dh.shape[0]

    def body(dh_ref, x_ref, d_ref, g_ref, dx_ref, dg_ref):
        xv = x_ref[...]
        dhv = dh_ref[...]
        r = _rms(xv)
        n = xv * r
        dn = dhv * g_ref[...]
        dx_ref[...] = d_ref[...] + r * (dn - n * jnp.mean(dn * n, axis=-1, keepdims=True))
        part = jnp.sum(dhv * n, axis=0, keepdims=True)

        @pl.when(pl.program_id(0) == 0)
        def _():
            dg_ref[...] = part

        @pl.when(pl.program_id(0) > 0)
        def _():
            dg_ref[...] += part

    return pl.pallas_call(
        body, name=name, grid=(t // TOK,),
        in_specs=[_row_spec(D_MODEL), _row_spec(D_MODEL), _row_spec(D_MODEL), _vec_spec(D_MODEL)],
        out_specs=[_row_spec(D_MODEL), _vec_spec(D_MODEL)],
        out_shape=[_sds((t, D_MODEL), F32), _sds((1, D_MODEL), F32)],
        compiler_params=_cparams("arbitrary"))(dh, xin, dxo, g)


def _loss_head(y, target):
    t = y.shape[0]

    def body(y_ref, t_ref, dy_ref, l_ref):
        e = y_ref[...] - t_ref[...]
        dy_ref[...] = e * (1.0 / D_MODEL)
        part = jnp.sum(jnp.sum(e * e, axis=0, keepdims=True), axis=1, keepdims=True)

        @pl.when(pl.program_id(0) == 0)
        def _():
            l_ref[...] = part

        @pl.when(pl.program_id(0) > 0)
        def _():
            l_ref[...] += part

    dy, sq = pl.pallas_call(
        body, name="loss_head", grid=(t // TOK,),
        in_specs=[_row_spec(D_MODEL), _row_spec(D_MODEL)],
        out_specs=[_row_spec(D_MODEL), pl.BlockSpec((1, 1), lambda i: (0, 0))],
        out_shape=[_sds((t, D_MODEL), F32), _sds((1, 1), F32)],
        compiler_params=_cparams("arbitrary"))(y, target)
    return dy, sq[0, 0] * (0.5 / D_MODEL)


def _gate_fwd(name, proj, b_gate, ya, yb):
    t = proj.shape[0]

    def body(ga_ref, gb_ref, b_ref, ya_ref, yb_ref, z_ref):
        sa = jax.nn.sigmoid(ga_ref[...].astype(F32) + b_ref[:, :D_MODEL])
        sb = jax.nn.sigmoid(gb_ref[...].astype(F32) + b_ref[:, D_MODEL:])
        z_ref[...] = (sa * ya_ref[...].astype(F32) + sb * yb_ref[...].astype(F32)).astype(BF16)

    return pl.pallas_call(
        body, name=name, grid=(t // TOK,),
        in_specs=[_row_spec(D_MODEL, 2), _row_spec(D_MODEL, 3), _vec_spec(2 * D_MODEL),
                  _row_spec(D_MODEL), _row_spec(D_MODEL)],
        out_specs=_row_spec(D_MODEL), out_shape=_sds((t, D_MODEL), BF16),
        compiler_params=_cparams("parallel"))(proj, proj, b_gate, ya, yb)


def _gate_bwd(name, dz, proj, b_gate, ya, yb):
    t = proj.shape[0]

    def body(dz_ref, ga_ref, gb_ref, b_ref, ya_ref, yb_ref, dya_ref, dyb_ref, dg_ref, db_ref):
        dzv = dz_ref[...].astype(F32)
        sa = jax.nn.sigmoid(ga_ref[...].astype(F32) + b_ref[:, :D_MODEL])
        sb = jax.nn.sigmoid(gb_ref[...].astype(F32) + b_ref[:, D_MODEL:])
        dya_ref[...] = (dzv * sa).astype(BF16)
        dyb_ref[...] = (dzv * sb).astype(BF16)
        dga = dzv * ya_ref[...].astype(F32) * sa * (1.0 - sa)
        dgb = dzv * yb_ref[...].astype(F32) * sb * (1.0 - sb)
        dg_ref[:, :D_MODEL] = dga.astype(BF16)
        dg_ref[:, D_MODEL:] = dgb.astype(BF16)
        pa = jnp.sum(dga, axis=0, keepdims=True)
        pb = jnp.sum(dgb, axis=0, keepdims=True)

        @pl.when(pl.program_id(0) == 0)
        def _():
            db_ref[:, :D_MODEL] = pa
            db_ref[:, D_MODEL:] = pb

        @pl.when(pl.program_id(0) > 0)
        def _():
            db_ref[:, :D_MODEL] += pa
            db_ref[:, D_MODEL:] += pb

    return pl.pallas_call(
        body, name=name, grid=(t // TOK,),
        in_specs=[_row_spec(D_MODEL), _row_spec(D_MODEL, 2), _row_spec(D_MODEL, 3),
                  _vec_spec(2 * D_MODEL), _row_spec(D_MODEL), _row_spec(D_MODEL)],
        out_specs=[_row_spec(D_MODEL), _row_spec(D_MODEL), _row_spec(2 * D_MODEL),
                   _vec_spec(2 * D_MODEL)],
        out_shape=[_sds((t, D_MODEL), BF16), _sds((t, D_MODEL), BF16),
                   _sds((t, 2 * D_MODEL), BF16), _sds((1, 2 * D_MODEL), F32)],
        compiler_params=_cparams("arbitrary"))(dz, proj, proj, b_gate, ya, yb)


def _head_masks():
    lane = lax.broadcasted_iota(jnp.int32, (1, 2 * HEAD_DIM), 1)
    return lane < HEAD_DIM


BAND_ROWS = 2 * ATT_BLK + CHUNK


def _fill_band(band, prev_ref, cur_ref):
    band[0:ATT_BLK, :] = prev_ref[...]
    band[ATT_BLK:2 * ATT_BLK, :] = cur_ref[...]
    band[2 * ATT_BLK:, :] = jnp.zeros((CHUNK, ATTN_W), BF16)


def _pair_rows(x2, low):
    zero = jnp.zeros_like(x2)
    return jnp.concatenate([jnp.where(low, x2, zero), jnp.where(low, zero, x2)], axis=0)


def _pair_diag(o2, low):
    return jnp.where(low, o2[0:CHUNK, :], o2[CHUNK:, :])


def _band_probs(k2, qcat, bias_t, valid):
    st = lax.dot_general(k2, qcat, (NT, ((), ())), preferred_element_type=F32)
    st = jnp.where(valid, st * ATTN_SCALE + bias_t, -1e30)
    e = jnp.exp(st - jnp.max(st, axis=0, keepdims=True))
    return e * (1.0 / jnp.sum(e, axis=0, keepdims=True))


def _band_valid(s, ci):
    kpos = lax.broadcasted_iota(jnp.int32, (BAND_PAD, 1), 0)
    return ((kpos + (s * 8 - 8 + ci) * CHUNK) >= 0) & (kpos < BAND)


def _attn_specs(nblk):
    cur = lambda col: pl.BlockSpec((ATT_BLK, ATTN_W), lambda s: (jnp.minimum(s, nblk - 1), col))
    prev = lambda col: pl.BlockSpec(
        (ATT_BLK, ATTN_W), lambda s: (jnp.maximum(jnp.minimum(s, nblk - 1) - 1, 0), col))
    return cur, prev


def _attn_fwd(name, proj, bias):
    t = proj.shape[0]
    nblk = t // ATT_BLK
    cur, prev = _attn_specs(nblk)

    def body(q_ref, kp_ref, kc_ref, vp_ref, vc_ref, b_ref, o_ref, kband, vband):
        s = pl.program_id(0)
        _fill_band(kband, kp_ref, kc_ref)
        _fill_band(vband, vp_ref, vc_ref)
        low = _head_masks()

        def chunk(ci, carry):
            r0 = pl.multiple_of(ci * CHUNK, CHUNK)
            valid = _band_valid(s, ci)
            for hp in range(HEADS // 2):
                cols = slice(hp * 128, (hp + 1) * 128)
                qcat = _pair_rows(q_ref[pl.ds(r0, CHUNK), cols], low)
                k2 = kband[pl.ds(r0, BAND_PAD), cols]
                v2 = vband[pl.ds(r0, BAND_PAD), cols]
                p = _band_probs(k2, qcat, b_ref[hp], valid)
                o2 = lax.dot_general(p.astype(BF16), v2, (TN, ((), ())),
                                     preferred_element_type=F32)
                o_ref[pl.ds(r0, CHUNK), cols] = _pair_diag(o2, low).astype(BF16)
            return carry

        lax.fori_loop(0, 8, chunk, 0)

    return pl.pallas_call(
        body, name=name, grid=(nblk,),
        in_specs=[cur(0), prev(1), cur(1), prev(2), cur(2),
                  pl.BlockSpec((HEADS // 2, BAND_PAD, 128), lambda s: (0, 0, 0))],
        out_specs=pl.BlockSpec((ATT_BLK, ATTN_W), lambda s: (s, 0)),
        out_shape=_sds((t, ATTN_W), BF16),
        scratch_shapes=[pltpu.VMEM((BAND_ROWS, ATTN_W), BF16),
                        pltpu.VMEM((BAND_ROWS, ATTN_W), BF16)],
        compiler_params=_cparams("arbitrary"))(proj, proj, proj, proj, proj, bias)


def _attn_bwd(name, proj, datt, bias):
    t = proj.shape[0]
    nblk = t // ATT_BLK
    cur, prev = _attn_specs(nblk)
    late = pl.BlockSpec((ATT_BLK, ATTN_W), lambda s: (jnp.maximum(s - 1, 0), 0))

    def body(q_ref, kp_ref, kc_ref, vp_ref, vc_ref, do_ref, b_ref,
             dq_ref, dk_ref, dv_ref, db_ref, kband, vband, dkacc, dvacc):
        s = pl.program_id(0)

        @pl.when(s == 0)
        def _():
            dkacc[...] = jnp.zeros_like(dkacc)
            dvacc[...] = jnp.zeros_like(dvacc)
            db_ref[...] = jnp.zeros_like(db_ref)

        @pl.when(s < nblk)
        def _():
            _fill_band(kband, kp_ref, kc_ref)
            _fill_band(vband, vp_ref, vc_ref)
            low = _head_masks()

            def chunk(ci, carry):
                r0 = pl.multiple_of(ci * CHUNK, CHUNK)
                valid = _band_valid(s, ci)
                for hp in range(HEADS // 2):
                    cols = slice(hp * 128, (hp + 1) * 128)
                    qcat = _pair_rows(q_ref[pl.ds(r0, CHUNK), cols], low)
                    docat = _pair_rows(do_ref[pl.ds(r0, CHUNK), cols], low)
                    k2 = kband[pl.ds(r0, BAND_PAD), cols]
                    v2 = vband[pl.ds(r0, BAND_PAD), cols]
                    p = _band_probs(k2, qcat, b_ref[hp], valid)
                    dp = lax.dot_general(v2, docat, (NT, ((), ())), preferred_element_type=F32)
                    ds = p * (dp - jnp.sum(p * dp, axis=0, keepdims=True))
                    db_ref[hp] += ds
                    dsb = (ds * ATTN_SCALE).astype(BF16)
                    dq2 = lax.dot_general(dsb, k2, (TN, ((), ())), preferred_element_type=F32)
                    dq_ref[pl.ds(r0, CHUNK), cols] = _pair_diag(dq2, low).astype(BF16)
                    dkacc[pl.ds(r0, BAND_PAD), cols] += jnp.dot(dsb, qcat,
                                                               preferred_element_type=F32)
                    dvacc[pl.ds(r0, BAND_PAD), cols] += jnp.dot(p.astype(BF16), docat,
                                                               preferred_element_type=F32)
                return carry

            lax.fori_loop(0, 8, chunk, 0)

        dk_ref[...] = dkacc[0:ATT_BLK, :].astype(BF16)
        dv_ref[...] = dvacc[0:ATT_BLK, :].astype(BF16)
        dkacc[0:ATT_BLK, :] = dkacc[ATT_BLK:2 * ATT_BLK, :]
        dvacc[0:ATT_BLK, :] = dvacc[ATT_BLK:2 * ATT_BLK, :]
        dkacc[ATT_BLK:, :] = jnp.zeros((ATT_BLK + CHUNK, ATTN_W), F32)
        dvacc[ATT_BLK:, :] = jnp.zeros((ATT_BLK + CHUNK, ATTN_W), F32)

    blk = _sds((t, ATTN_W), BF16)
    return pl.pallas_call(
        body, name=name, grid=(nblk + 1,),
        in_specs=[cur(0), prev(1), cur(1), prev(2), cur(2),
                  pl.BlockSpec((ATT_BLK, ATTN_W), lambda s: (jnp.minimum(s, nblk - 1), 0)),
                  pl.BlockSpec((HEADS // 2, BAND_PAD, 128), lambda s: (0, 0, 0))],
        out_specs=[pl.BlockSpec((ATT_BLK, ATTN_W), lambda s: (jnp.minimum(s, nblk - 1), 0)),
                   late, late,
                   pl.BlockSpec((HEADS // 2, BAND_PAD, 128), lambda s: (0, 0, 0))],
        out_shape=[blk, blk, blk, _sds((HEADS // 2, BAND_PAD, 128), F32)],
        scratch_shapes=[pltpu.VMEM((BAND_ROWS, ATTN_W), BF16),
                        pltpu.VMEM((BAND_ROWS, ATTN_W), BF16),
                        pltpu.VMEM((BAND_ROWS, ATTN_W), F32),
                        pltpu.VMEM((BAND_ROWS, ATTN_W), F32)],
        compiler_params=_cparams("arbitrary"))(proj, proj, proj, proj, proj, datt, bias)


def _diag_onehot(rel_rows):
    d0 = lax.broadcasted_iota(jnp.int32, (BIAS_LANES, BIAS_LANES), 0)
    d1 = lax.broadcasted_iota(jnp.int32, (BIAS_LANES, BIAS_LANES), 1)
    m, n = (d0, d1) if rel_rows else (d1, d0)
    hit = (m == jnp.minimum(BAND - 1 + MAX_REL - n, 2 * MAX_REL)) & (n < BAND + CHUNK - 1)
    return jnp.where(hit, 1.0, 0.0).astype(F32)


def _bias_table(name, rel_bias_l):
    rel_pad = jnp.pad(rel_bias_l, ((0, 0), (0, BIAS_LANES - N_REL)))

    def body(r_ref, o_ref):
        diag = jnp.dot(r_ref[...], _diag_onehot(True), preferred_element_type=F32,
                       precision=lax.Precision.HIGHEST)
        rowid = lax.broadcasted_iota(jnp.int32, (8, BIAS_LANES), 0)
        lane = lax.broadcasted_iota(jnp.int32, (8, BIAS_LANES), 1)
        for h in range(HEADS):
            d8 = jnp.broadcast_to(diag[h:h + 1, :], (8, BIAS_LANES))
            slab0 = pltpu.roll(d8, BIAS_LANES - CHUNK + 1, axis=1)
            for b in range(1, 8):
                slab0 = jnp.where(rowid == b, pltpu.roll(d8, BIAS_LANES - CHUNK + 1 + b, axis=1),
                                  slab0)
            for a in range(8):
                slab = slab0 if a == 0 else pltpu.roll(slab0, 8 * a, axis=1)
                o_ref[h * CHUNK + 8 * a:h * CHUNK + 8 * a + 8, :] = jnp.where(lane < BAND, slab, 0.0)

    tab = pl.pallas_call(
        body, name=name,
        in_specs=[pl.BlockSpec(memory_space=pltpu.VMEM)],
        out_specs=pl.BlockSpec(memory_space=pltpu.VMEM),
        out_shape=_sds((HEADS * CHUNK, BIAS_LANES), F32),
    )(rel_pad)
    tab = tab.reshape(HEADS // 2, 2, CHUNK, BIAS_LANES)
    return jnp.transpose(tab, (0, 3, 1, 2)).reshape(HEADS // 2, BIAS_LANES, 2 * CHUNK)


def _bias_fold(name, dbias_t):
    rows = HEADS * CHUNK
    dbias = jnp.transpose(dbias_t.reshape(HEADS // 2, BIAS_LANES, 2, CHUNK), (0, 2, 3, 1))

    def body(d_ref, o_ref):
        rowid = lax.broadcasted_iota(jnp.int32, (8, BIAS_LANES), 0)
        diags = []
        for h in range(HEADS):
            acc = d_ref[h * CHUNK + 56:h * CHUNK + 64, :]
            for a in range(7):
                slab = d_ref[h * CHUNK + 8 * a:h * CHUNK + 8 * a + 8, :]
                acc = acc + pltpu.roll(slab, 56 - 8 * a, axis=1)
            tot = jnp.where(rowid == 7, acc, 0.0)
            for b in range(7):
                tot = tot + jnp.where(rowid == b, pltpu.roll(acc, 7 - b, axis=1), 0.0)
            diags.append(jnp.sum(tot, axis=0, keepdims=True))
        diag = jnp.concatenate(diags, axis=0)
        o_ref[...] = jnp.dot(diag, _diag_onehot(False), preferred_element_type=F32,
                             precision=lax.Precision.HIGHEST)

    return pl.pallas_call(
        body, name=name,
        in_specs=[pl.BlockSpec(memory_space=pltpu.VMEM)],
        out_specs=pl.BlockSpec(memory_space=pltpu.VMEM),
        out_shape=_sds((HEADS, BIAS_LANES), F32),
    )(dbias.reshape(rows, BIAS_LANES))


def _inv_counts(i):
    trow = lax.broadcasted_iota(jnp.int32, (TOK + HALO, 1), 0) + i * TOK
    return [1.0 / jnp.minimum(trow + 1, w).astype(F32) for w in POOL_WINDOWS]


def _pool_fwd(name, proj, wg, scale):
    t = proj.shape[0]
    hb = TOK // HALO

    def body(u_ref, up_ref, wg_ref, sc_ref, pooled_ref, mixed_ref, b0, b1, b2, b3):
        i = pl.program_id(0)
        halo = up_ref[...].astype(F32)
        b0[0:HALO, :] = jnp.where(i == 0, jnp.zeros_like(halo), halo)
        b0[HALO:, :] = u_ref[...].astype(F32)
        n = TOK + HALO
        b1[8:n, :] = b0[8:n, :] + b0[7:n - 1, :]
        b2[16:n, 128:] = b1[16:n, 128:] + b1[14:n - 2, 128:]
        b3[24:n, 256:] = b2[24:n, 256:] + b2[20:n - 4, 256:]
        wins = [b1[HALO:n, 0:128], b2[HALO:n, 128:256], b3[HALO:n, 256:384],
                b3[HALO:n, 384:512] + b3[HALO - 8:n - 8, 384:512]]
        inv = _inv_counts(i)
        for g in range(4):
            cols = slice(g * POOL_GD, (g + 1) * POOL_GD)
            pooled = (wins[g] * inv[g][0:TOK] - b0[HALO:n, cols]).astype(BF16)
            pooled_ref[:, cols] = pooled
            pre = jnp.dot(pooled, wg_ref[g], preferred_element_type=F32)
            mixed_ref[:, cols] = (pre * sc_ref[:, cols]).astype(BF16)

    buf = pltpu.VMEM((TOK + HALO, POOL_W), F32)
    return pl.pallas_call(
        body, name=name, grid=(t // TOK,),
        in_specs=[_row_spec(POOL_W, 3),
                  pl.BlockSpec((HALO, POOL_W), lambda i: (jnp.maximum(i * hb - 1, 0), 3)),
                  pl.BlockSpec((4, POOL_GD, POOL_GD), lambda i: (0, 0, 0)),
                  _vec_spec(POOL_W)],
        out_specs=[_row_spec(POOL_W), _row_spec(POOL_W)],
        out_shape=[_sds((t, POOL_W), BF16), _sds((t, POOL_W), BF16)],
        scratch_shapes=[buf, buf, buf, buf],
        compiler_params=_cparams("parallel"))(proj, proj, wg, scale)


def _pool_bwd(name, dmixed, pooled, wg, scale):
    t = dmixed.shape[0]
    nt = t // TOK
    hb = TOK // HALO

    def body(dm_ref, dmn_ref, p_ref, wg_ref, sc_ref, du_ref, dwg_ref, dsc_ref, c0, c1, c2, c3):
        i = pl.program_id(0)

        @pl.when(i == 0)
        def _():
            dwg_ref[...] = jnp.zeros_like(dwg_ref)
            dsc_ref[...] = jnp.zeros_like(dsc_ref)

        n = TOK + HALO
        inv = _inv_counts(i)
        dmv = dm_ref[...].astype(F32)
        dmn = dmn_ref[...].astype(F32)
        dmn = jnp.where(i == nt - 1, jnp.zeros_like(dmn), dmn)
        for g in range(4):
            cols = slice(g * POOL_GD, (g + 1) * POOL_GD)
            scg = sc_ref[:, cols]
            pg = p_ref[:, cols]
            dpre = (dmv[:, cols] * scg).astype(BF16)
            dpre_n = (dmn[:, cols] * scg).astype(BF16)
            pre = jnp.dot(pg, wg_ref[g], preferred_element_type=F32)
            dsc_ref[:, cols] += jnp.sum(dmv[:, cols] * pre, axis=0, keepdims=True)
            dwg_ref[g] += lax.dot_general(pg, dpre, (TN, ((), ())), preferred_element_type=F32)
            dpool = lax.dot_general(dpre, wg_ref[g], (NT, ((), ())), preferred_element_type=F32)
            dpool_n = lax.dot_general(dpre_n, wg_ref[g], (NT, ((), ())),
                                      preferred_element_type=F32)
            c0[0:TOK, cols] = dpool
            c0[TOK:n, cols] = dpool_n
            c1[0:TOK, cols] = dpool * inv[g][0:TOK]
            c1[TOK:n, cols] = dpool_n * inv[g][TOK:n]
        c2[0:n - 8, :] = c1[0:n - 8, :] + c1[1:n - 7, :]
        c3[0:n - 16, 128:] = c2[0:n - 16, 128:] + c2[2:n - 14, 128:]
        c1[0:n - 24, 256:] = c3[0:n - 24, 256:] + c3[4:n - 20, 256:]
        wins = [c2[0:TOK, 0:128], c3[0:TOK, 128:256], c1[0:TOK, 256:384],
                c1[0:TOK, 384:512] + c1[8:TOK + 8, 384:512]]
        for g in range(4):
            cols = slice(g * POOL_GD, (g + 1) * POOL_GD)
            du_ref[:, cols] = (wins[g] - c0[0:TOK, cols]).astype(BF16)

    buf = pltpu.VMEM((TOK + HALO, POOL_W), F32)
    return pl.pallas_call(
        body, name=name, grid=(nt,),
        in_specs=[_row_spec(POOL_W),
                  pl.BlockSpec((HALO, POOL_W), lambda i: (jnp.minimum((i + 1) * hb, nt * hb - 1), 0)),
                  _row_spec(POOL_W),
                  pl.BlockSpec((4, POOL_GD, POOL_GD), lambda i: (0, 0, 0)),
                  _vec_spec(POOL_W)],
        out_specs=[_row_spec(POOL_W), pl.BlockSpec((4, POOL_GD, POOL_GD), lambda i: (0, 0, 0)),
                   _vec_spec(POOL_W)],
        out_shape=[_sds((t, POOL_W), BF16), _sds((4, POOL_GD, POOL_GD), F32),
                   _sds((1, POOL_W), F32)],
        scratch_shapes=[buf, buf, buf, buf],
        compiler_params=_cparams("arbitrary"))(dmixed, dmixed, pooled, wg, scale)


GELU_C = math.sqrt(2.0 / math.pi)


def _gelu(x):
    return 0.5 * x * (1.0 + jnp.tanh(GELU_C * (x + 0.044715 * x * x * x)))


def _gelu_grad(x):
    th = jnp.tanh(GELU_C * (x + 0.044715 * x * x * x))
    return 0.5 * (1.0 + th) + 0.5 * x * (1.0 - th * th) * GELU_C * (1.0 + 3 * 0.044715 * x * x)


def _conv_rows(buf, w_ref, b_ref, start, rows):
    return (b_ref[...] + w_ref[2:3, :] * buf[start:start + rows, :]
            + w_ref[1:2, :] * buf[start - 1:start - 1 + rows, :]
            + w_ref[0:1, :] * buf[start - 2:start - 2 + rows, :])


def _ffn_gate_fwd(name, hu, conv_w, conv_b):
    t = hu.shape[0]
    ncol = D_FF // FF_COL
    hb = TOK // 8

    def tile(off):
        return pl.BlockSpec((TOK, FF_COL), lambda i, j: (i, j + off))

    def halo(off):
        return pl.BlockSpec((8, FF_COL), lambda i, j: (jnp.maximum(i * hb - 1, 0), j + off))

    def wspec(off):
        return pl.BlockSpec((3, FF_COL), lambda i, j: (0, j + off))

    def bspec(off):
        return pl.BlockSpec((1, FF_COL), lambda i, j: (0, j + off))

    def body(v_ref, vp_ref, g_ref, gp_ref, wv_ref, wg_ref, bv_ref, bg_ref, a_ref, vb, gb):
        i = pl.program_id(0)
        for src, prv, dst in ((v_ref, vp_ref, vb), (g_ref, gp_ref, gb)):
            h = prv[...].astype(F32)
            dst[0:8, :] = jnp.where(i == 0, jnp.zeros_like(h), h)
            dst[8:, :] = src[...].astype(F32)
        val = _conv_rows(vb, wv_ref, bv_ref, 8, TOK)
        gate = _conv_rows(gb, wg_ref, bg_ref, 8, TOK)
        a_ref[...] = (_gelu(gate) * val).astype(BF16)

    buf = pltpu.VMEM((TOK + 8, FF_COL), F32)
    return pl.pallas_call(
        body, name=name, grid=(t // TOK, ncol),
        in_specs=[tile(0), halo(0), tile(ncol), halo(ncol), wspec(0), wspec(ncol),
                  bspec(0), bspec(ncol)],
        out_specs=pl.BlockSpec((TOK, FF_COL), lambda i, j: (i, j)),
        out_shape=_sds((t, D_FF), BF16), scratch_shapes=[buf, buf],
        compiler_params=_cparams("parallel", "parallel"))(
            hu, hu, hu, hu, conv_w, conv_w, conv_b, conv_b)


def _ffn_gate_bwd(name, da, hu, conv_w, conv_b):
    t = hu.shape[0]
    nt = t // TOK
    ncol = D_FF // FF_COL
    hb = TOK // 8
    ext = TOK + 8

    def tile(off):
        return pl.BlockSpec((TOK, FF_COL), lambda j, i: (i, j + off))

    def prev(off):
        return pl.BlockSpec((8, FF_COL), lambda j, i: (jnp.maximum(i * hb - 1, 0), j + off))

    def nxt(off):
        return pl.BlockSpec((8, FF_COL), lambda j, i: (jnp.minimum((i + 1) * hb, nt * hb - 1), j + off))

    def wspec(off):
        return pl.BlockSpec((3, FF_COL), lambda j, i: (0, j + off))

    def bspec(off):
        return pl.BlockSpec((1, FF_COL), lambda j, i: (0, j + off))

    def body(da_ref, dan_ref, v_ref, vp_ref, vn_ref, g_ref, gp_ref, gn_ref,
             wv_ref, wg_ref, bv_ref, bg_ref, dhv_ref, dhg_ref, dwv_ref, dwg_ref,
             vb, gb, dvb, dgb):
        i = pl.program_id(1)

        @pl.when(i == 0)
        def _():
            dwv_ref[...] = jnp.zeros_like(dwv_ref)
            dwg_ref[...] = jnp.zeros_like(dwg_ref)

        for src, prv, nx, dst in ((v_ref, vp_ref, vn_ref, vb), (g_ref, gp_ref, gn_ref, gb)):
            h = prv[...].astype(F32)
            dst[0:8, :] = jnp.where(i == 0, jnp.zeros_like(h), h)
            dst[8:8 + TOK, :] = src[...].astype(F32)
            dst[8 + TOK:, :] = nx[...].astype(F32)
        val = _conv_rows(vb, wv_ref, bv_ref, 8, ext)
        gate = _conv_rows(gb, wg_ref, bg_ref, 8, ext)
        dan = dan_ref[...].astype(F32)
        dan = jnp.where(i == nt - 1, jnp.zeros_like(dan), dan)
        da_ext = jnp.concatenate([da_ref[...].astype(F32), dan], axis=0)
        dvb[...] = da_ext * _gelu(gate)
        dgb[...] = da_ext * val * _gelu_grad(gate)
        for dbuf, hbuf, w_ref, dh_ref, dw_ref in ((dvb, vb, wv_ref, dhv_ref, dwv_ref),
                                                  (dgb, gb, wg_ref, dhg_ref, dwg_ref)):
            d0 = dbuf[0:TOK, :]
            dh_ref[...] = (w_ref[2:3, :] * d0 + w_ref[1:2, :] * dbuf[1:TOK + 1, :]
                           + w_ref[0:1, :] * dbuf[2:TOK + 2, :]).astype(BF16)
            dw_ref[0:1, :] += jnp.sum(d0 * hbuf[6:6 + TOK, :], axis=0, keepdims=True)
            dw_ref[1:2, :] += jnp.sum(d0 * hbuf[7:7 + TOK, :], axis=0, keepdims=True)
            dw_ref[2:3, :] += jnp.sum(d0 * hbuf[8:8 + TOK, :], axis=0, keepdims=True)
            dw_ref[3:4, :] += jnp.sum(d0, axis=0, keepdims=True)

    hbuf = pltpu.VMEM((TOK + 16, FF_COL), F32)
    dbuf = pltpu.VMEM((ext, FF_COL), F32)
    dhu_v, dhu_g, dwv, dwg = pl.pallas_call(
        body, name=name, grid=(ncol, nt),
        in_specs=[tile(0), nxt(0), tile(0), prev(0), nxt(0), tile(ncol), prev(ncol), nxt(ncol),
                  wspec(0), wspec(ncol), bspec(0), bspec(ncol)],
        out_specs=[pl.BlockSpec((TOK, FF_COL), lambda j, i: (i, j)),
                   pl.BlockSpec((TOK, FF_COL), lambda j, i: (i, j)),
                   pl.BlockSpec((8, FF_COL), lambda j, i: (0, j)),
                   pl.BlockSpec((8, FF_COL), lambda j, i: (0, j))],
        out_shape=[_sds((t, D_FF), BF16), _sds((t, D_FF), BF16),
                   _sds((8, D_FF), F32), _sds((8, D_FF), F32)],
        scratch_shapes=[hbuf, hbuf, dbuf, dbuf],
        compiler_params=_cparams("parallel", "arbitrary"))(
            da, da, hu, hu, hu, hu, hu, hu, conv_w, conv_w, conv_b, conv_b)
    return (jnp.concatenate([dhu_v, dhu_g], axis=1), jnp.concatenate([dwv, dwg], axis=1))


def _mesh_pos():
    x, y, c = lax.axis_index("x"), lax.axis_index("y"), lax.axis_index("c")
    return x, y, c, [(1 - x, y), (x, 1 - y), (1 - x, 1 - y)]


def _any_specs(n):
    return [pl.BlockSpec(memory_space=pl.ANY)] * n


def _allgather_weights(shards):
    n = len(shards)

    def body(*refs):
        ins, outs = refs[:n], refs[n:2 * n]
        send_sems, recv_sems = refs[2 * n:]
        x, y, c, chips = _mesh_pos()
        me = 2 * x + y
        started = []
        own = []
        for k in range(n):
            for l in range(2):
                cp = pltpu.make_async_remote_copy(
                    src_ref=ins[k].at[l], dst_ref=outs[k].at[l, me],
                    send_sem=send_sems.at[k, 6 + l], recv_sem=recv_sems.at[k, 6 + l],
                    device_id=(x, y, 1 - c), device_id_type=MESH)
                cp.start()
                own.append(cp)
            for j, (cx, cy) in enumerate(chips):
                cp = pltpu.make_async_remote_copy(
                    src_ref=ins[k].at[c], dst_ref=outs[k].at[c, me],
                    send_sem=send_sems.at[k, j], recv_sem=recv_sems.at[k, j],
                    device_id=(cx, cy, c), device_id_type=MESH)
                cp.start()
                started.append(cp)
        for k in range(n):
            for j, (cx, cy) in enumerate(chips):
                landed = outs[k].at[c, 2 * cx + cy]
                pltpu.make_async_remote_copy(
                    src_ref=ins[k].at[c], dst_ref=landed,
                    send_sem=send_sems.at[k, j], recv_sem=recv_sems.at[k, j],
                    device_id=(cx, cy, c), device_id_type=MESH).wait_recv()
                fw = pltpu.make_async_remote_copy(
                    src_ref=landed, dst_ref=landed,
                    send_sem=send_sems.at[k, 3 + j], recv_sem=recv_sems.at[k, 3 + j],
                    device_id=(x, y, 1 - c), device_id_type=MESH)
                fw.start()
                started.append(fw)
        for k in range(n):
            for j, (cx, cy) in enumerate(chips):
                theirs = outs[k].at[1 - c, 2 * cx + cy]
                pltpu.make_async_remote_copy(
                    src_ref=theirs, dst_ref=theirs,
                    send_sem=send_sems.at[k, 3 + j], recv_sem=recv_sems.at[k, 3 + j],
                    device_id=(x, y, 1 - c), device_id_type=MESH).wait_recv()
        for cp in started:
            cp.wait_send()
        for cp in own:
            cp.wait()

    return pl.pallas_call(
        body, name="allgather_weights",
        in_specs=_any_specs(n), out_specs=_any_specs(n),
        out_shape=[_sds((2, N_CHIPS) + s.shape[1:], s.dtype) for s in shards],
        scratch_shapes=[pltpu.SemaphoreType.DMA((n, 8)), pltpu.SemaphoreType.DMA((n, 8))],
    )(*shards)


def _swap_layers(grads):
    n = len(grads)

    def body(*refs):
        ins, outs = refs[:n], refs[n:2 * n]
        send_sems, recv_sems = refs[2 * n:]
        x, y, c, _ = _mesh_pos()
        cps = []
        for k in range(n):
            cp = pltpu.make_async_remote_copy(
                src_ref=ins[k].at[1 - c], dst_ref=outs[k],
                send_sem=send_sems.at[k], recv_sem=recv_sems.at[k],
                device_id=(x, y, 1 - c), device_id_type=MESH)
            cp.start()
            cps.append(cp)
        for cp in cps:
            cp.wait()

    return pl.pallas_call(
        body, name="swap_layers",
        in_specs=_any_specs(n), out_specs=_any_specs(n),
        out_shape=[_sds(g.shape[1:], g.dtype) for g in grads],
        scratch_shapes=[pltpu.SemaphoreType.DMA((n,)), pltpu.SemaphoreType.DMA((n,))],
    )(*grads)


def _scatter_blocks(sums):
    n = len(sums)

    def body(*refs):
        ins, outs = refs[:n], refs[n:2 * n]
        send_sems, recv_sems = refs[2 * n:]
        x, y, c, chips = _mesh_pos()
        cps = []
        for k in range(n):
            for j, (cx, cy) in enumerate(chips):
                cp = pltpu.make_async_remote_copy(
                    src_ref=ins[k].at[2 * cx + cy], dst_ref=outs[k].at[j],
                    send_sem=send_sems.at[k, j], recv_sem=recv_sems.at[k, j],
                    device_id=(cx, cy, c), device_id_type=MESH)
                cp.start()
                cps.append(cp)
        for cp in cps:
            cp.wait()

    return pl.pallas_call(
        body, name="scatter_blocks",
        in_specs=_any_specs(n), out_specs=_any_specs(n),
        out_shape=[_sds((3,) + s.shape[1:], s.dtype) for s in sums],
        scratch_shapes=[pltpu.SemaphoreType.DMA((n, 3)), pltpu.SemaphoreType.DMA((n, 3))],
    )(*sums)


def _exchange_reduced(reds):
    n = len(reds)

    def body(*refs):
        outs = refs[n:2 * n]
        send_sems, recv_sems = refs[2 * n:]
        x, y, c, _ = _mesh_pos()
        cps = []
        for k in range(n):
            cp = pltpu.make_async_remote_copy(
                src_ref=outs[k].at[c], dst_ref=outs[k].at[c],
                send_sem=send_sems.at[k], recv_sem=recv_sems.at[k],
                device_id=(x, y, 1 - c), device_id_type=MESH)
            cp.start()
            cps.append(cp)
        for k in range(n):
            pltpu.make_async_remote_copy(
                src_ref=outs[k].at[c], dst_ref=outs[k].at[1 - c],
                send_sem=send_sems.at[k], recv_sem=recv_sems.at[k],
                device_id=(x, y, 1 - c), device_id_type=MESH).wait_recv()
        for cp in cps:
            cp.wait_send()

    return pl.pallas_call(
        body, name="exchange_reduced",
        in_specs=_any_specs(n), out_specs=_any_specs(n),
        out_shape=[_sds(r.shape, r.dtype) for r in reds],
        input_output_aliases={k: k for k in range(n)},
        scratch_shapes=[pltpu.SemaphoreType.DMA((n,)), pltpu.SemaphoreType.DMA((n,))],
    )(*reds)


def _allreduce_small(pack):
    n = pack.shape[0]

    def body(x_ref, o_ref, gbuf, send_sems, recv_sems):
        x, y, c, chips = _mesh_pos()
        sibling = (x, y, 1 - c)

        def slot(px, py, pc):
            return gbuf.at[4 * px + 2 * py + pc]

        def copy(k, block, to, src=None):
            return pltpu.make_async_remote_copy(
                src_ref=slot(*block) if src is None else src, dst_ref=slot(*block),
                send_sem=send_sems.at[k], recv_sem=recv_sems.at[k],
                device_id=to, device_id_type=MESH)

        me = (x, y, c)
        first = [copy(0, me, sibling, src=x_ref)]
        first += [copy(1 + j, me, (*chip, c), src=x_ref) for j, chip in enumerate(chips)]
        for cp in first:
            cp.start()
        gbuf[4 * x + 2 * y + c] = x_ref[...]
        passed = [copy(4 + j, (*chip, c), sibling) for j, chip in enumerate(chips)]
        for j, chip in enumerate(chips):
            copy(1 + j, (*chip, c), me).wait_recv()
            passed[j].start()
        copy(0, sibling, me).wait_recv()
        for j, chip in enumerate(chips):
            copy(4 + j, (*chip, 1 - c), me).wait_recv()
        for cp in first + passed:
            cp.wait_send()
        acc = gbuf[0]
        for d in range(1, 8):
            acc = acc + gbuf[d]
        o_ref[...] = acc

    return pl.pallas_call(
        body, name="allreduce_small",
        in_specs=[pl.BlockSpec(memory_space=pltpu.VMEM)],
        out_specs=pl.BlockSpec(memory_space=pltpu.VMEM),
        out_shape=_sds((n, 128), F32),
        scratch_shapes=[pltpu.VMEM((8, n, 128), F32), pltpu.SemaphoreType.DMA((7,)),
                        pltpu.SemaphoreType.DMA((7,))],
        compiler_params=pltpu.CompilerParams(vmem_limit_bytes=VMEM_LIMIT_V7X),
    )(pack)


def _core_index():
    return jnp.reshape(lax.axis_index("c"), (1,)).astype(jnp.int32)


def _chip_index():
    return jnp.reshape(2 * lax.axis_index("x") + lax.axis_index("y"), (1,)).astype(jnp.int32)


def _chip_sum(name, stacked, sib):
    _, nb, r, cdim = stacked.shape

    def body(c_ref, a_ref, b_ref, o_ref):
        o_ref[...] = (a_ref[...].astype(F32) + b_ref[...].astype(F32)).astype(BF16)

    return pl.pallas_call(
        body, name=name,
        grid_spec=pltpu.PrefetchScalarGridSpec(
            num_scalar_prefetch=1, grid=(nb,),
            in_specs=[pl.BlockSpec((None, None, r, cdim), lambda j, cr: (cr[0], j, 0, 0)),
                      pl.BlockSpec((None, r, cdim), lambda j, cr: (j, 0, 0))],
            out_specs=pl.BlockSpec((None, r, cdim), lambda j, cr: (j, 0, 0))),
        out_shape=_sds((nb, r, cdim), BF16),
        compiler_params=_cparams("parallel"))(_core_index(), stacked, sib)


def _final_sum(name, sums, recv):
    _, r, cdim = sums.shape
    tr = r // 2

    def body(m_ref, a_ref, b_ref, o_ref):
        acc = a_ref[...].astype(F32)
        for j in range(3):
            acc = acc + b_ref[j].astype(F32)
        o_ref[...] = acc

    return pl.pallas_call(
        body, name=name,
        grid_spec=pltpu.PrefetchScalarGridSpec(
            num_scalar_prefetch=1, grid=(2,),
            in_specs=[pl.BlockSpec((None, tr, cdim), lambda i, mr: (mr[0], i, 0)),
                      pl.BlockSpec((3, tr, cdim), lambda i, mr: (0, i, 0))],
            out_specs=pl.BlockSpec((None, tr, cdim), lambda i, mr: (mr[1], i, 0))),
        out_shape=_sds((DEPTH, r, cdim), F32),
        compiler_params=_cparams("parallel"))(
            jnp.concatenate([_chip_index(), _core_index()]), sums, recv)


def _adamw(name, w, g, m, v):
    nl, r, cdim = w.shape
    tr = r // 4 if r % 32 == 0 else r
    c1 = 1.0 - ADAM_B1 ** ADAM_STEP
    c2 = 1.0 - ADAM_B2 ** ADAM_STEP

    def body(w_ref, g_ref, m_ref, v_ref, d_ref, nm_ref, nv_ref):
        gv = g_ref[...]
        nm = ADAM_B1 * m_ref[...] + (1.0 - ADAM_B1) * gv
        nv = ADAM_B2 * v_ref[...] + (1.0 - ADAM_B2) * (gv * gv)
        nm_ref[...] = nm
        nv_ref[...] = nv
        d_ref[...] = -ADAM_LR * ((nm / c1) / (jnp.sqrt(nv / c2) + ADAM_EPS) + ADAM_WD * w_ref[...])

    spec = pl.BlockSpec((None, tr, cdim), lambda l, i: (l, i, 0))
    out = _sds(w.shape, F32)
    return pl.pallas_call(
        body, name=name, grid=(nl, r // tr),
        in_specs=[spec] * 4, out_specs=[spec] * 3, out_shape=[out] * 3,
        compiler_params=_cparams("parallel", "parallel"))(w, g, m, v)


def _rows128(a):
    return a.reshape(-1, 128)


def kernel(x, norm_mix_pre, w_in, b_gate, rel_bias, w_attn_out, w_pool_group, pool_scale, w_pool_out, w_o, norm_mix_post, norm_ffn_pre, w_up, conv_w, conv_b, w_down, norm_ffn_post, loss_target, m_norm_mix_pre, m_w_in, m_b_gate, m_rel_bias, m_w_attn_out, m_w_pool_group, m_pool_scale, m_w_pool_out, m_w_o, m_norm_mix_post, m_norm_ffn_pre, m_w_up, m_conv_w, m_conv_b, m_w_down, m_norm_ffn_post, v_norm_mix_pre, v_w_in, v_b_gate, v_rel_bias, v_w_attn_out, v_w_pool_group, v_pool_scale, v_w_pool_out, v_w_o, v_norm_mix_post, v_norm_ffn_pre, v_w_up, v_conv_w, v_conv_b, v_w_down, v_norm_ffn_post):
    t = x.shape[1]
    xs = x.reshape(t, D_MODEL)
    target = loss_target.reshape(t, D_MODEL)

    big = [w_in, w_attn_out, w_pool_out, w_o, w_up, w_down]
    gathered = _allgather_weights([w.astype(BF16) for w in big] + [conv_w])
    win_g, wao_g, wpo_g, wo_g, wup_g, wdn_g, cw_g = gathered
    wo_full = wo_g.reshape(DEPTH, D_MODEL, D_MODEL)
    wdn_full = wdn_g.reshape(DEPTH, D_FF, D_MODEL)
    cw_full = jnp.transpose(cw_g, (0, 2, 1, 3)).reshape(DEPTH, 3, 2 * D_FF)
    wg_bf = w_pool_group.astype(BF16)

    saved = []
    xcur = xs
    for l in range(DEPTH):
        tag = f"l{l}_"
        bias = _bias_table(tag + "bias_table", rel_bias[l])
        h = _norm_fwd(tag + "norm_mix_pre", xcur, norm_mix_pre[l:l + 1])
        proj = _mm_nn_blocked(tag + "proj", h, win_g, l, BF16)
        att = _attn_fwd(tag + "attn_fwd", proj, bias)
        pooled, mixed = _pool_fwd(tag + "pool_fwd", proj, wg_bf[l], pool_scale[l:l + 1])
        ya = _mm_nn_blocked(tag + "attn_out", att, wao_g, l, BF16)
        yb = _mm_nn_blocked(tag + "pool_out", mixed, wpo_g, l, BF16)
        z = _gate_fwd(tag + "gate_fwd", proj, b_gate[l:l + 1], ya, yb)
        mix = _mm_nn(tag + "mix", z, wo_full, l, D_MODEL, F32)
        x1 = _norm_residual_fwd(tag + "norm_mix_post", xcur, mix, norm_mix_post[l:l + 1])
        h2 = _norm_fwd(tag + "norm_ffn_pre", x1, norm_ffn_pre[l:l + 1])
        hu = _mm_nn_blocked(tag + "ffn_up", h2, wup_g, l, BF16)
        a = _ffn_gate_fwd(tag + "ffn_gate_fwd", hu, cw_full[l], conv_b[l:l + 1])
        f = _mm_nn(tag + "ffn_down", a, wdn_full, l, D_FF // 2, F32)
        x2 = _norm_residual_fwd(tag + "norm_ffn_post", x1, f, norm_ffn_post[l:l + 1])
        saved.append(dict(x=xcur, h=h, proj=proj, att=att, pooled=pooled, mixed=mixed, ya=ya,
                          yb=yb, z=z, mix=mix, x1=x1, h2=h2, hu=hu, a=a, f=f, bias=bias))
        xcur = x2

    dy, loss_local = _loss_head(xcur, target)
    loss = lax.psum(loss_local, ("x", "y", "c"))

    dx = dy
    names = ["w_in", "w_attn_out", "w_pool_out", "w_o", "w_up", "w_down"]
    dws = dict.fromkeys(names)
    small_grads = [None] * DEPTH
    for l in reversed(range(DEPTH)):
        tag = f"l{l}_"
        sv = saved[l]
        df, d_nfpost = _norm_post_bwd(tag + "norm_ffn_post_bwd", dx, sv["f"], norm_ffn_post[l:l + 1])
        da = _mm_nt(tag + "ffn_down_dx", df, wdn_full, l, D_FF // 2, BF16)
        dws["w_down"] = _mm_tn(tag + "ffn_down_dw", sv["a"], df, D_FF // 2, l, dws["w_down"])
        dhu, dconv = _ffn_gate_bwd(tag + "ffn_gate_bwd", da, sv["hu"], cw_full[l], conv_b[l:l + 1])
        dh2 = _mm_nt_blocked(tag + "ffn_up_dx", dhu, wup_g, l, F32)
        dws["w_up"] = _mm_tn_blocked(tag + "ffn_up_dw", sv["h2"], dhu, l, dws["w_up"])
        dx1, d_nfpre = _norm_pre_bwd(tag + "norm_ffn_pre_bwd", dh2, sv["x1"], dx, norm_ffn_pre[l:l + 1])
        dmix, d_nmpost = _norm_post_bwd(tag + "norm_mix_post_bwd", dx1, sv["mix"], norm_mix_post[l:l + 1])
        dz = _mm_nt(tag + "mix_dx", dmix, wo_full, l, D_MODEL, BF16)
        dws["w_o"] = _mm_tn(tag + "mix_dw", sv["z"], dmix, D_MODEL, l, dws["w_o"])
        dya, dyb, dgates, d_bgate = _gate_bwd(tag + "gate_bwd", dz, sv["proj"], b_gate[l:l + 1],
                                              sv["ya"], sv["yb"])
        datt = _mm_nt_blocked(tag + "attn_out_dx", dya, wao_g, l, BF16)
        dws["w_attn_out"] = _mm_tn_blocked(tag + "attn_out_dw", sv["att"], dya, l, dws["w_attn_out"])
        dmixed = _mm_nt_blocked(tag + "pool_out_dx", dyb, wpo_g, l, BF16)
        dws["w_pool_out"] = _mm_tn_blocked(tag + "pool_out_dw", sv["mixed"], dyb, l, dws["w_pool_out"])
        du, d_wg, d_pscale = _pool_bwd(tag + "pool_bwd", dmixed, sv["pooled"], wg_bf[l],
                                       pool_scale[l:l + 1])
        dq, dk, dv, dbias = _attn_bwd(tag + "attn_bwd", sv["proj"], datt, sv["bias"])
        d_rel = _bias_fold(tag + "bias_fold", dbias)
        dproj = jnp.concatenate([dq, dk, dv, du, dgates], axis=1)
        dh = _mm_nt_blocked(tag + "proj_dx", dproj, win_g, l, F32)
        dws["w_in"] = _mm_tn_blocked(tag + "proj_dw", sv["h"], dproj, l, dws["w_in"])
        dx, d_nmpre = _norm_pre_bwd(tag + "norm_mix_pre_bwd", dh, sv["x"], dx1, norm_mix_pre[l:l + 1])
        small_grads[l] = [d_nmpre, d_nmpost, d_nfpre, d_nfpost, d_bgate, d_rel, d_wg, d_pscale,
                          dconv[3:4], dconv[0:3]]

    grad_x = dx.reshape(x.shape)

    stacked = [dws[nm] for nm in names]
    stacked[3] = stacked[3].reshape(DEPTH, N_CHIPS, D_MODEL // N_CHIPS, D_MODEL)
    stacked[5] = stacked[5].reshape(DEPTH, N_CHIPS, D_FF // N_CHIPS, D_MODEL)
    sib = _swap_layers(stacked)
    sums = [_chip_sum("chip_sum_" + names[k], stacked[k], sib[k]) for k in range(6)]
    recv = _scatter_blocks(sums)
    reds = [_final_sum("final_sum_" + names[k], sums[k], recv[k]) for k in range(6)]
    g_big = _exchange_reduced(reds)

    pieces = []
    for idx in range(10):
        pieces.append(jnp.stack([small_grads[0][idx], small_grads[1][idx]]))
    pack = jnp.concatenate([_rows128(p) for p in pieces], axis=0)
    red = _allreduce_small(pack)
    shapes = [p.shape for p in pieces]
    outs = []
    row = 0
    for shp in shapes:
        nrow = math.prod(shp) // 128
        outs.append(red[row:row + nrow].reshape(shp))
        row += nrow
    (g_nmpre, g_nmpost, g_nfpre, g_nfpost, g_bgate, g_rel, g_wg, g_pscale, g_cb, g_cw) = outs
    g_nmpre, g_nmpost, g_nfpre, g_nfpost = [a.reshape(DEPTH, D_MODEL)
                                            for a in (g_nmpre, g_nmpost, g_nfpre, g_nfpost)]
    g_bgate = g_bgate.reshape(DEPTH, 2 * D_MODEL)
    g_rel = g_rel[:, :, :N_REL]
    g_pscale = g_pscale.reshape(DEPTH, POOL_W)
    g_cb = g_cb.reshape(DEPTH, 2 * D_FF)
    ncw = conv_w.shape[2]
    chip = 2 * lax.axis_index("x") + lax.axis_index("y")
    g_cw = lax.dynamic_slice_in_dim(g_cw, chip * ncw, ncw, axis=2)

    grads = dict(norm_mix_pre=g_nmpre, w_in=g_big[0], b_gate=g_bgate, rel_bias=g_rel,
                 w_attn_out=g_big[1], w_pool_group=g_wg, pool_scale=g_pscale, w_pool_out=g_big[2],
                 w_o=g_big[3], norm_mix_post=g_nmpost, norm_ffn_pre=g_nfpre, w_up=g_big[4],
                 conv_w=g_cw, conv_b=g_cb, w_down=g_big[5], norm_ffn_post=g_nfpost)
    weights = dict(norm_mix_pre=norm_mix_pre, w_in=w_in, b_gate=b_gate, rel_bias=rel_bias,
                   w_attn_out=w_attn_out, w_pool_group=w_pool_group, pool_scale=pool_scale,
                   w_pool_out=w_pool_out, w_o=w_o, norm_mix_post=norm_mix_post,
                   norm_ffn_pre=norm_ffn_pre, w_up=w_up, conv_w=conv_w, conv_b=conv_b,
                   w_down=w_down, norm_ffn_post=norm_ffn_post)
    moms = dict(norm_mix_pre=(m_norm_mix_pre, v_norm_mix_pre), w_in=(m_w_in, v_w_in),
                b_gate=(m_b_gate, v_b_gate), rel_bias=(m_rel_bias, v_rel_bias),
                w_attn_out=(m_w_attn_out, v_w_attn_out),
                w_pool_group=(m_w_pool_group, v_w_pool_group),
                pool_scale=(m_pool_scale, v_pool_scale), w_pool_out=(m_w_pool_out, v_w_pool_out),
                w_o=(m_w_o, v_w_o), norm_mix_post=(m_norm_mix_post, v_norm_mix_post),
                norm_ffn_pre=(m_norm_ffn_pre, v_norm_ffn_pre), w_up=(m_w_up, v_w_up),
                conv_w=(m_conv_w, v_conv_w), conv_b=(m_conv_b, v_conv_b),
                w_down=(m_w_down, v_w_down), norm_ffn_post=(m_norm_ffn_post, v_norm_ffn_post))
    order = list(weights.keys())

    delta, new_m, new_v = {}, {}, {}
    small_names = [nm for nm in order if nm not in names]
    for nm in names:
        delta[nm], new_m[nm], new_v[nm] = _adamw("adamw_" + nm, weights[nm], grads[nm], *moms[nm])

    def pack_small(get):
        flat = [get(nm).reshape(-1) for nm in small_names]
        total = sum(f.shape[0] for f in flat)
        padded = -(-total // 1024) * 1024
        flat.append(jnp.zeros((padded - total,), F32))
        return jnp.concatenate(flat).reshape(1, padded // 128, 128)

    d_s, m_s, v_s = _adamw(
        "adamw_small", pack_small(lambda nm: weights[nm]), pack_small(lambda nm: grads[nm]),
        pack_small(lambda nm: moms[nm][0]) , pack_small(lambda nm: moms[nm][1]))
    off = 0
    for nm in small_names:
        size = math.prod(weights[nm].shape)
        for dst, src in ((delta, d_s), (new_m, m_s), (new_v, v_s)):
            dst[nm] = src.reshape(-1)[off:off + size].reshape(weights[nm].shape)
        off += size

    return (loss, grad_x, *[grads[nm] for nm in order], *[delta[nm] for nm in order],
            *[new_m[nm] for nm in order], *[new_v[nm] for nm in order])
```

```python
import functools
import math

import jax
import jax.numpy as jnp
from jax import lax
from jax.experimental import pallas as pl
from jax.experimental.pallas import tpu as pltpu

F32 = jnp.float32
BF16 = jnp.bfloat16
MESH = pl.DeviceIdType.MESH

D_MODEL = 1024
DEPTH = 2
CHUNK = 64
BAND_CHUNKS = 9
BAND = BAND_CHUNKS * CHUNK
HEADS = 8
HEAD_DIM = 64
ATTN_W = HEADS * HEAD_DIM
POOL_WINDOWS = (2, 4, 8, 16)
POOL_W = 512
POOL_GD = 128
MAX_REL = 256
N_REL = 2 * MAX_REL + 1
D_FF = 2816
IN_W = 3 * ATTN_W + POOL_W + 2 * D_MODEL
EPS = 1e-6
ATTN_SCALE = HEAD_DIM ** -0.5
BAND_PAD = 640
BIAS_LANES = BAND_PAD
N_CHIPS = 4

ADAM_LR = 0.001
ADAM_B1 = 0.9
ADAM_B2 = 0.999
ADAM_EPS = 1e-08
ADAM_WD = 0.01
ADAM_STEP = 10

VMEM_LIMIT_V7X = 56 * 1024 * 1024
TOK = 512
ATT_BLK = 8 * CHUNK
FF_COL = 256
HALO = 32


def _cparams(*sem):
    return pltpu.CompilerParams(dimension_semantics=sem, vmem_limit_bytes=VMEM_LIMIT_V7X)


def _sds(shape, dtype):
    return jax.ShapeDtypeStruct(shape, dtype)


def _matmul(name, a, b, a_spec, b_spec, o_spec, out_shape, grid, contract, nk, acc_shape,
            fill=None):
    def body(*refs):
        a_ref, b_ref = refs[0], refs[1]
        o_ref = refs[2 if fill is None else 3]
        scratch = refs[(3 if fill is None else 4):]
        part = lax.dot_general(a_ref[...], b_ref[...], (contract, ((), ())),
                               preferred_element_type=F32)
        if nk == 1:
            o_ref[...] = part.astype(o_ref.dtype)
        else:
            acc_ref = scratch[0]
            k = pl.program_id(2)

            @pl.when(k == 0)
            def _():
                acc_ref[...] = part

            @pl.when(k > 0)
            def _():
                acc_ref[...] += part

            @pl.when(k == nk - 1)
            def _():
                o_ref[...] = acc_ref[...].astype(o_ref.dtype)

    scratch = [] if nk == 1 else [pltpu.VMEM(acc_shape, F32)]
    in_specs, args, aliases = [a_spec, b_spec], [a, b], {}
    if fill is not None:
        in_specs.append(pl.BlockSpec(memory_space=pl.ANY))
        args.append(fill)
        aliases = {2: 0}
    return pl.pallas_call(
        body, name=name, grid=grid, in_specs=in_specs, out_specs=o_spec,
        out_shape=out_shape, scratch_shapes=scratch, input_output_aliases=aliases,
        compiler_params=_cparams("parallel", "parallel", "arbitrary"),
    )(*args)


NN = ((1,), (0,))
NT = ((1,), (1,))
TN = ((0,), (0,))


def _tm(t):
    return min(t, 1024)


def _col_block_spec(a, rows, nb, row_col):
    if a.ndim == 2:
        return pl.BlockSpec((rows, nb), row_col)

    def halves(*ids):
        r, c = row_col(*ids)
        return c // 2, r, c % 2

    return pl.BlockSpec((None, rows, nb), halves)


def _mm_nn_blocked(name, a, w, l, out_dtype):
    t, k = a.shape
    nb = w.shape[3]
    tm = _tm(t)
    return _matmul(
        name, a, w,
        pl.BlockSpec((tm, k), lambda i, n, kk: (i, 0)),
        pl.BlockSpec((None, None, k, nb), lambda i, n, kk: (l, n, 0, 0)),
        pl.BlockSpec((tm, nb), lambda i, n, kk: (i, n)),
        _sds((t, N_CHIPS * nb), out_dtype), (t // tm, N_CHIPS, 1), NN, 1, None)


def _mm_nt_blocked(name, a, w, l, out_dtype):
    t = a.shape[-2]
    k, nb = w.shape[2], w.shape[3]
    tm = _tm(t)
    return _matmul(
        name, a, w,
        _col_block_spec(a, tm, nb, lambda i, n, kk: (i, kk)),
        pl.BlockSpec((None, None, k, nb), lambda i, n, kk: (l, kk, 0, 0)),
        pl.BlockSpec((tm, k), lambda i, n, kk: (i, 0)),
        _sds((t, k), out_dtype), (t // tm, 1, N_CHIPS), NT, N_CHIPS, (tm, k))


def _mm_tn_blocked(name, a, g, l, fill):
    t, k = a.shape
    nb = g.shape[-1] * (g.ndim - 1) // N_CHIPS
    nt = t // TOK
    return _matmul(
        name, a, g,
        pl.BlockSpec((TOK, k), lambda n, j, kk: (kk, 0)),
        _col_block_spec(g, TOK, nb, lambda n, j, kk: (kk, n)),
        pl.BlockSpec((None, None, k, nb), lambda n, j, kk: (l, n, 0, 0)),
        _sds((DEPTH, N_CHIPS, k, nb), BF16), (N_CHIPS, 1, nt), TN, nt, (k, nb), fill)


def _mm_nn(name, a, w, l, tk, out_dtype):
    t, k = a.shape
    n = w.shape[2]
    tm = _tm(t)
    nk = k // tk
    return _matmul(
        name, a, w,
        pl.BlockSpec((tm, tk), lambda i, j, kk: (i, kk)),
        pl.BlockSpec((None, tk, n), lambda i, j, kk: (l, kk, 0)),
        pl.BlockSpec((tm, n), lambda i, j, kk: (i, 0)),
        _sds((t, n), out_dtype), (t // tm, 1, nk), NN, nk, (tm, n))


def _mm_nt(name, a, w, l, tn, out_dtype):
    t, n = a.shape
    k = w.shape[1]
    tm = _tm(t)
    return _matmul(
        name, a, w,
        pl.BlockSpec((tm, n), lambda i, j, kk: (i, 0)),
        pl.BlockSpec((None, tn, n), lambda i, j, kk: (l, j, 0)),
        pl.BlockSpec((tm, tn), lambda i, j, kk: (i, j)),
        _sds((t, k), out_dtype), (t // tm, k // tn, 1), NT, 1, None)


def _mm_tn(name, a, g, tko, l, fill):
    t, k = a.shape
    n = g.shape[1]
    nt = t // TOK
    return _matmul(
        name, a, g,
        pl.BlockSpec((TOK, tko), lambda i, j, kk: (kk, i)),
        pl.BlockSpec((TOK, n), lambda i, j, kk: (kk, 0)),
        pl.BlockSpec((None, tko, n), lambda i, j, kk: (l, i, 0)),
        _sds((DEPTH, k, n), BF16), (k // tko, 1, nt), TN, nt, (tko, n), fill)


def _row_spec(width, col=0):
    return pl.BlockSpec((TOK, width), lambda i: (i, col))


def _vec_spec(width):
    return pl.BlockSpec((1, width), lambda i: (0, 0))


def _rms(x):
    return lax.rsqrt(jnp.mean(x * x, axis=-1, keepdims=True) + EPS)


def _norm_fwd(name, x, g):
    t = x.shape[0]

    def body(x_ref, g_ref, h_ref):
        xv = x_ref[...]
        h_ref[...] = (xv * _rms(xv) * g_ref[...]).astype(BF16)

    return pl.pallas_call(
        body, name=name, grid=(t // TOK,), in_specs=[_row_spec(D_MODEL), _vec_spec(D_MODEL)],
        out_specs=_row_spec(D_MODEL), out_shape=_sds((t, D_MODEL), BF16),
        compiler_params=_cparams("parallel"))(x, g)


def _norm_residual_fwd(name, xres, m, g):
    t = xres.shape[0]

    def body(x_ref, m_ref, g_ref, o_ref):
        mv = m_ref[...]
        o_ref[...] = x_ref[...] + mv * _rms(mv) * g_ref[...]

    return pl.pallas_call(
        body, name=name, grid=(t // TOK,),
        in_specs=[_row_spec(D_MODEL), _row_spec(D_MODEL), _vec_spec(D_MODEL)],
        out_specs=_row_spec(D_MODEL), out_shape=_sds((t, D_MODEL), F32),
        compiler_params=_cparams("parallel"))(xres, m, g)


def _norm_post_bwd(name, dxo, m, g):
    t = dxo.shape[0]

    def body(d_ref, m_ref, g_ref, dm_ref, dg_ref):
        mv = m_ref[...]
        dv = d_ref[...]
        r = _rms(mv)
        n = mv * r
        dn = dv * g_ref[...]
        dm_ref[...] = (r * (dn - n * jnp.mean(dn * n, axis=-1, keepdims=True))).astype(BF16)
        part = jnp.sum(dv * n, axis=0, keepdims=True)

        @pl.when(pl.program_id(0) == 0)
        def _():
            dg_ref[...] = part

        @pl.when(pl.program_id(0) > 0)
        def _():
            dg_ref[...] += part

    return pl.pallas_call(
        body, name=name, grid=(t // TOK,),
        in_specs=[_row_spec(D_MODEL), _row_spec(D_MODEL), _vec_spec(D_MODEL)],
        out_specs=[_row_spec(D_MODEL), _vec_spec(D_MODEL)],
        out_shape=[_sds((t, D_MODEL), BF16), _sds((1, D_MODEL), F32)],
        compiler_params=_cparams("arbitrary"))(dxo, m, g)


def _norm_pre_bwd(name, dh, xin, dxo, g):
    t = dh.shape[0]

    def body(dh_ref, x_ref, d_ref, g_ref, dx_ref, dg_ref):
        xv = x_ref[...]
        dhv = dh_ref[...]
        r = _rms(xv)
        n = xv * r
        dn = dhv * g_ref[...]
        dx_ref[...] = d_ref[...] + r * (dn - n * jnp.mean(dn * n, axis=-1, keepdims=True))
        part = jnp.sum(dhv * n, axis=0, keepdims=True)

        @pl.when(pl.program_id(0) == 0)
        def _():
            dg_ref[...] = part

        @pl.when(pl.program_id(0) > 0)
        def _():
            dg_ref[...] += part

    return pl.pallas_call(
        body, name=name, grid=(t // TOK,),
        in_specs=[_row_spec(D_MODEL), _row_spec(D_MODEL), _row_spec(D_MODEL), _vec_spec(D_MODEL)],
        out_specs=[_row_spec(D_MODEL), _vec_spec(D_MODEL)],
        out_shape=[_sds((t, D_MODEL), F32), _sds((1, D_MODEL), F32)],
        compiler_params=_cparams("arbitrary"))(dh, xin, dxo, g)


def _loss_head(y, target):
    t = y.shape[0]

    def body(y_ref, t_ref, dy_ref, l_ref):
        e = y_ref[...] - t_ref[...]
        dy_ref[...] = e * (1.0 / D_MODEL)
        part = jnp.sum(jnp.sum(e * e, axis=0, keepdims=True), axis=1, keepdims=True)

        @pl.when(pl.program_id(0) == 0)
        def _():
            l_ref[...] = part

        @pl.when(pl.program_id(0) > 0)
        def _():
            l_ref[...] += part

    dy, sq = pl.pallas_call(
        body, name="loss_head", grid=(t // TOK,),
        in_specs=[_row_spec(D_MODEL), _row_spec(D_MODEL)],
        out_specs=[_row_spec(D_MODEL), pl.BlockSpec((1, 1), lambda i: (0, 0))],
        out_shape=[_sds((t, D_MODEL), F32), _sds((1, 1), F32)],
        compiler_params=_cparams("arbitrary"))(y, target)
    return dy, sq[0, 0] * (0.5 / D_MODEL)


def _gate_fwd(name, proj, b_gate, ya, yb):
    t = proj.shape[0]

    def body(ga_ref, gb_ref, b_ref, ya_ref, yb_ref, z_ref):
        sa = jax.nn.sigmoid(ga_ref[...].astype(F32) + b_ref[:, :D_MODEL])
        sb = jax.nn.sigmoid(gb_ref[...].astype(F32) + b_ref[:, D_MODEL:])
        z_ref[...] = (sa * ya_ref[...].astype(F32) + sb * yb_ref[...].astype(F32)).astype(BF16)

    return pl.pallas_call(
        body, name=name, grid=(t // TOK,),
        in_specs=[_row_spec(D_MODEL, 2), _row_spec(D_MODEL, 3), _vec_spec(2 * D_MODEL),
                  _row_spec(D_MODEL), _row_spec(D_MODEL)],
        out_specs=_row_spec(D_MODEL), out_shape=_sds((t, D_MODEL), BF16),
        compiler_params=_cparams("parallel"))(proj, proj, b_gate, ya, yb)


def _gate_bwd(name, dz, proj, b_gate, ya, yb):
    t = proj.shape[0]

    def body(dz_ref, ga_ref, gb_ref, b_ref, ya_ref, yb_ref, dya_ref, dyb_ref, dg_ref, db_ref):
        dzv = dz_ref[...].astype(F32)
        sa = jax.nn.sigmoid(ga_ref[...].astype(F32) + b_ref[:, :D_MODEL])
        sb = jax.nn.sigmoid(gb_ref[...].astype(F32) + b_ref[:, D_MODEL:])
        dya_ref[...] = (dzv * sa).astype(BF16)
        dyb_ref[...] = (dzv * sb).astype(BF16)
        dga = dzv * ya_ref[...].astype(F32) * sa * (1.0 - sa)
        dgb = dzv * yb_ref[...].astype(F32) * sb * (1.0 - sb)
        dg_ref[:, :D_MODEL] = dga.astype(BF16)
        dg_ref[:, D_MODEL:] = dgb.astype(BF16)
        pa = jnp.sum(dga, axis=0, keepdims=True)
        pb = jnp.sum(dgb, axis=0, keepdims=True)

        @pl.when(pl.program_id(0) == 0)
        def _():
            db_ref[:, :D_MODEL] = pa
            db_ref[:, D_MODEL:] = pb

        @pl.when(pl.program_id(0) > 0)
        def _():
            db_ref[:, :D_MODEL] += pa
            db_ref[:, D_MODEL:] += pb

    return pl.pallas_call(
        body, name=name, grid=(t // TOK,),
        in_specs=[_row_spec(D_MODEL), _row_spec(D_MODEL, 2), _row_spec(D_MODEL, 3),
                  _vec_spec(2 * D_MODEL), _row_spec(D_MODEL), _row_spec(D_MODEL)],
        out_specs=[_row_spec(D_MODEL), _row_spec(D_MODEL), _row_spec(2 * D_MODEL),
                   _vec_spec(2 * D_MODEL)],
        out_shape=[_sds((t, D_MODEL), BF16), _sds((t, D_MODEL), BF16),
                   _sds((t, 2 * D_MODEL), BF16), _sds((1, 2 * D_MODEL), F32)],
        compiler_params=_cparams("arbitrary"))(dz, proj, proj, b_gate, ya, yb)


def _head_masks():
    lane = lax.broadcasted_iota(jnp.int32, (1, 2 * HEAD_DIM), 1)
    return lane < HEAD_DIM


BAND_ROWS = 2 * ATT_BLK + CHUNK


def _fill_band(band, prev_ref, cur_ref):
    band[0:ATT_BLK, :] = prev_ref[...]
    band[ATT_BLK:2 * ATT_BLK, :] = cur_ref[...]
    band[2 * ATT_BLK:, :] = jnp.zeros((CHUNK, ATTN_W), BF16)


def _pair_rows(x2, low):
    zero = jnp.zeros_like(x2)
    return jnp.concatenate([jnp.where(low, x2, zero), jnp.where(low, zero, x2)], axis=0)


def _pair_diag(o2, low):
    return jnp.where(low, o2[0:CHUNK, :], o2[CHUNK:, :])


def _band_probs(k2, qcat, bias_t, valid):
    st = lax.dot_general(k2, qcat, (NT, ((), ())), preferred_element_type=F32)
    st = jnp.where(valid, st * ATTN_SCALE + bias_t, -1e30)
    e = jnp.exp(st - jnp.max(st, axis=0, keepdims=True))
    return e * (1.0 / jnp.sum(e, axis=0, keepdims=True))


def _band_valid(s, ci):
    kpos = lax.broadcasted_iota(jnp.int32, (BAND_PAD, 1), 0)
    return ((kpos + (s * 8 - 8 + ci) * CHUNK) >= 0) & (kpos < BAND)


def _attn_specs(nblk):
    cur = lambda col: pl.BlockSpec((ATT_BLK, ATTN_W), lambda s: (jnp.minimum(s, nblk - 1), col))
    prev = lambda col: pl.BlockSpec(
        (ATT_BLK, ATTN_W), lambda s: (jnp.maximum(jnp.minimum(s, nblk - 1) - 1, 0), col))
    return cur, prev


def _attn_fwd(name, proj, bias):
    t = proj.shape[0]
    nblk = t // ATT_BLK
    cur, prev = _attn_specs(nblk)

    def body(q_ref, kp_ref, kc_ref, vp_ref, vc_ref, b_ref, o_ref, kband, vband):
        s = pl.program_id(0)
        _fill_band(kband, kp_ref, kc_ref)
        _fill_band(vband, vp_ref, vc_ref)
        low = _head_masks()

        def chunk(ci, carry):
            r0 = pl.multiple_of(ci * CHUNK, CHUNK)
            valid = _band_valid(s, ci)
            for hp in range(HEADS // 2):
                cols = slice(hp * 128, (hp + 1) * 128)
                qcat = _pair_rows(q_ref[pl.ds(r0, CHUNK), cols], low)
                k2 = kband[pl.ds(r0, BAND_PAD), cols]
                v2 = vband[pl.ds(r0, BAND_PAD), cols]
                p = _band_probs(k2, qcat, b_ref[hp], valid)
                o2 = lax.dot_general(p.astype(BF16), v2, (TN, ((), ())),
                                     preferred_element_type=F32)
                o_ref[pl.ds(r0, CHUNK), cols] = _pair_diag(o2, low).astype(BF16)
            return carry

        lax.fori_loop(0, 8, chunk, 0)

    return pl.pallas_call(
        body, name=name, grid=(nblk,),
        in_specs=[cur(0), prev(1), cur(1), prev(2), cur(2),
                  pl.BlockSpec((HEADS // 2, BAND_PAD, 128), lambda s: (0, 0, 0))],
        out_specs=pl.BlockSpec((ATT_BLK, ATTN_W), lambda s: (s, 0)),
        out_shape=_sds((t, ATTN_W), BF16),
        scratch_shapes=[pltpu.VMEM((BAND_ROWS, ATTN_W), BF16),
                        pltpu.VMEM((BAND_ROWS, ATTN_W), BF16)],
        compiler_params=_cparams("arbitrary"))(proj, proj, proj, proj, proj, bias)


def _attn_bwd(name, proj, datt, bias):
    t = proj.shape[0]
    nblk = t // ATT_BLK
    cur, prev = _attn_specs(nblk)
    late = pl.BlockSpec((ATT_BLK, ATTN_W), lambda s: (jnp.maximum(s - 1, 0), 0))

    def body(q_ref, kp_ref, kc_ref, vp_ref, vc_ref, do_ref, b_ref,
             dq_ref, dk_ref, dv_ref, db_ref, kband, vband, dkacc, dvacc):
        s = pl.program_id(0)

        @pl.when(s == 0)
        def _():
            dkacc[...] = jnp.zeros_like(dkacc)
            dvacc[...] = jnp.zeros_like(dvacc)
            db_ref[...] = jnp.zeros_like(db_ref)

        @pl.when(s < nblk)
        def _():
            _fill_band(kband, kp_ref, kc_ref)
            _fill_band(vband, vp_ref, vc_ref)
            low = _head_masks()

            def chunk(ci, carry):
                r0 = pl.multiple_of(ci * CHUNK, CHUNK)
                valid = _band_valid(s, ci)
                for hp in range(HEADS // 2):
                    cols = slice(hp * 128, (hp + 1) * 128)
                    qcat = _pair_rows(q_ref[pl.ds(r0, CHUNK), cols], low)
                    docat = _pair_rows(do_ref[pl.ds(r0, CHUNK), cols], low)
                    k2 = kband[pl.ds(r0, BAND_PAD), cols]
                    v2 = vband[pl.ds(r0, BAND_PAD), cols]
                    p = _band_probs(k2, qcat, b_ref[hp], valid)
                    dp = lax.dot_general(v2, docat, (NT, ((), ())), preferred_element_type=F32)
                    ds = p * (dp - jnp.sum(p * dp, axis=0, keepdims=True))
                    db_ref[hp] += ds
                    dsb = (ds * ATTN_SCALE).astype(BF16)
                    dq2 = lax.dot_general(dsb, k2, (TN, ((), ())), preferred_element_type=F32)
                    dq_ref[pl.ds(r0, CHUNK), cols] = _pair_diag(dq2, low).astype(BF16)
                    dkacc[pl.ds(r0, BAND_PAD), cols] += jnp.dot(dsb, qcat,
                                                               preferred_element_type=F32)
                    dvacc[pl.ds(r0, BAND_PAD), cols] += jnp.dot(p.astype(BF16), docat,
                                                               preferred_element_type=F32)
                return carry

            lax.fori_loop(0, 8, chunk, 0)

        dk_ref[...] = dkacc[0:ATT_BLK, :].astype(BF16)
        dv_ref[...] = dvacc[0:ATT_BLK, :].astype(BF16)
        dkacc[0:ATT_BLK, :] = dkacc[ATT_BLK:2 * ATT_BLK, :]
        dvacc[0:ATT_BLK, :] = dvacc[ATT_BLK:2 * ATT_BLK, :]
        dkacc[ATT_BLK:, :] = jnp.zeros((ATT_BLK + CHUNK, ATTN_W), F32)
        dvacc[ATT_BLK:, :] = jnp.zeros((ATT_BLK + CHUNK, ATTN_W), F32)

    blk = _sds((t, ATTN_W), BF16)
    return pl.pallas_call(
        body, name=name, grid=(nblk + 1,),
        in_specs=[cur(0), prev(1), cur(1), prev(2), cur(2),
                  pl.BlockSpec((ATT_BLK, ATTN_W), lambda s: (jnp.minimum(s, nblk - 1), 0)),
                  pl.BlockSpec((HEADS // 2, BAND_PAD, 128), lambda s: (0, 0, 0))],
        out_specs=[pl.BlockSpec((ATT_BLK, ATTN_W), lambda s: (jnp.minimum(s, nblk - 1), 0)),
                   late, late,
                   pl.BlockSpec((HEADS // 2, BAND_PAD, 128), lambda s: (0, 0, 0))],
        out_shape=[blk, blk, blk, _sds((HEADS // 2, BAND_PAD, 128), F32)],
        scratch_shapes=[pltpu.VMEM((BAND_ROWS, ATTN_W), BF16),
                        pltpu.VMEM((BAND_ROWS, ATTN_W), BF16),
                        pltpu.VMEM((BAND_ROWS, ATTN_W), F32),
                        pltpu.VMEM((BAND_ROWS, ATTN_W), F32)],
        compiler_params=_cparams("arbitrary"))(proj, proj, proj, proj, proj, datt, bias)


def _diag_onehot(rel_rows):
    d0 = lax.broadcasted_iota(jnp.int32, (BIAS_LANES, BIAS_LANES), 0)
    d1 = lax.broadcasted_iota(jnp.int32, (BIAS_LANES, BIAS_LANES), 1)
    m, n = (d0, d1) if rel_rows else (d1, d0)
    hit = (m == jnp.minimum(BAND - 1 + MAX_REL - n, 2 * MAX_REL)) & (n < BAND + CHUNK - 1)
    return jnp.where(hit, 1.0, 0.0).astype(F32)


def _bias_table(name, rel_bias_l):
    rel_pad = jnp.pad(rel_bias_l, ((0, 0), (0, BIAS_LANES - N_REL)))

    def body(r_ref, o_ref):
        diag = jnp.dot(r_ref[...], _diag_onehot(True), preferred_element_type=F32,
                       precision=lax.Precision.HIGHEST)
        rowid = lax.broadcasted_iota(jnp.int32, (8, BIAS_LANES), 0)
        lane = lax.broadcasted_iota(jnp.int32, (8, BIAS_LANES), 1)
        for h in range(HEADS):
            d8 = jnp.broadcast_to(diag[h:h + 1, :], (8, BIAS_LANES))
            slab0 = pltpu.roll(d8, BIAS_LANES - CHUNK + 1, axis=1)
            for b in range(1, 8):
                slab0 = jnp.where(rowid == b, pltpu.roll(d8, BIAS_LANES - CHUNK + 1 + b, axis=1),
                                  slab0)
            for a in range(8):
                slab = slab0 if a == 0 else pltpu.roll(slab0, 8 * a, axis=1)
                o_ref[h * CHUNK + 8 * a:h * CHUNK + 8 * a + 8, :] = jnp.where(lane < BAND, slab, 0.0)

    tab = pl.pallas_call(
        body, name=name,
        in_specs=[pl.BlockSpec(memory_space=pltpu.VMEM)],
        out_specs=pl.BlockSpec(memory_space=pltpu.VMEM),
        out_shape=_sds((HEADS * CHUNK, BIAS_LANES), F32),
    )(rel_pad)
    tab = tab.reshape(HEADS // 2, 2, CHUNK, BIAS_LANES)
    return jnp.transpose(tab, (0, 3, 1, 2)).reshape(HEADS // 2, BIAS_LANES, 2 * CHUNK)


def _bias_fold(name, dbias_t):
    rows = HEADS * CHUNK
    dbias = jnp.transpose(dbias_t.reshape(HEADS // 2, BIAS_LANES, 2, CHUNK), (0, 2, 3, 1))

    def body(d_ref, o_ref):
        rowid = lax.broadcasted_iota(jnp.int32, (8, BIAS_LANES), 0)
        diags = []
        for h in range(HEADS):
            acc = d_ref[h * CHUNK + 56:h * CHUNK + 64, :]
            for a in range(7):
                slab = d_ref[h * CHUNK + 8 * a:h * CHUNK + 8 * a + 8, :]
                acc = acc + pltpu.roll(slab, 56 - 8 * a, axis=1)
            tot = jnp.where(rowid == 7, acc, 0.0)
            for b in range(7):
                tot = tot + jnp.where(rowid == b, pltpu.roll(acc, 7 - b, axis=1), 0.0)
            diags.append(jnp.sum(tot, axis=0, keepdims=True))
        diag = jnp.concatenate(diags, axis=0)
        o_ref[...] = jnp.dot(diag, _diag_onehot(False), preferred_element_type=F32,
                             precision=lax.Precision.HIGHEST)

    return pl.pallas_call(
        body, name=name,
        in_specs=[pl.BlockSpec(memory_space=pltpu.VMEM)],
        out_specs=pl.BlockSpec(memory_space=pltpu.VMEM),
        out_shape=_sds((HEADS, BIAS_LANES), F32),
    )(dbias.reshape(rows, BIAS_LANES))


def _inv_counts(i):
    trow = lax.broadcasted_iota(jnp.int32, (TOK + HALO, 1), 0) + i * TOK
    return [1.0 / jnp.minimum(trow + 1, w).astype(F32) for w in POOL_WINDOWS]


def _pool_fwd(name, proj, wg, scale):
    t = proj.shape[0]
    hb = TOK // HALO

    def body(u_ref, up_ref, wg_ref, sc_ref, pooled_ref, mixed_ref, b0, b1, b2, b3):
        i = pl.program_id(0)
        halo = up_ref[...].astype(F32)
        b0[0:HALO, :] = jnp.where(i == 0, jnp.zeros_like(halo), halo)
        b0[HALO:, :] = u_ref[...].astype(F32)
        n = TOK + HALO
        b1[8:n, :] = b0[8:n, :] + b0[7:n - 1, :]
        b2[16:n, 128:] = b1[16:n, 128:] + b1[14:n - 2, 128:]
        b3[24:n, 256:] = b2[24:n, 256:] + b2[20:n - 4, 256:]
        wins = [b1[HALO:n, 0:128], b2[HALO:n, 128:256], b3[HALO:n, 256:384],
                b3[HALO:n, 384:512] + b3[HALO - 8:n - 8, 384:512]]
        inv = _inv_counts(i)
        for g in range(4):
            cols = slice(g * POOL_GD, (g + 1) * POOL_GD)
            pooled = (wins[g] * inv[g][0:TOK] - b0[HALO:n, cols]).astype(BF16)
            pooled_ref[:, cols] = pooled
            pre = jnp.dot(pooled, wg_ref[g], preferred_element_type=F32)
            mixed_ref[:, cols] = (pre * sc_ref[:, cols]).astype(BF16)

    buf = pltpu.VMEM((TOK + HALO, POOL_W), F32)
    return pl.pallas_call(
        body, name=name, grid=(t // TOK,),
        in_specs=[_row_spec(POOL_W, 3),
                  pl.BlockSpec((HALO, POOL_W), lambda i: (jnp.maximum(i * hb - 1, 0), 3)),
                  pl.BlockSpec((4, POOL_GD, POOL_GD), lambda i: (0, 0, 0)),
                  _vec_spec(POOL_W)],
        out_specs=[_row_spec(POOL_W), _row_spec(POOL_W)],
        out_shape=[_sds((t, POOL_W), BF16), _sds((t, POOL_W), BF16)],
        scratch_shapes=[buf, buf, buf, buf],
        compiler_params=_cparams("parallel"))(proj, proj, wg, scale)


def _pool_bwd(name, dmixed, pooled, wg, scale):
    t = dmixed.shape[0]
    nt = t // TOK
    hb = TOK // HALO

    def body(dm_ref, dmn_ref, p_ref, wg_ref, sc_ref, du_ref, dwg_ref, dsc_ref, c0, c1, c2, c3):
        i = pl.program_id(0)

        @pl.when(i == 0)
        def _():
            dwg_ref[...] = jnp.zeros_like(dwg_ref)
            dsc_ref[...] = jnp.zeros_like(dsc_ref)

        n = TOK + HALO
        inv = _inv_counts(i)
        dmv = dm_ref[...].astype(F32)
        dmn = dmn_ref[...].astype(F32)
        dmn = jnp.where(i == nt - 1, jnp.zeros_like(dmn), dmn)
        for g in range(4):
            cols = slice(g * POOL_GD, (g + 1) * POOL_GD)
            scg = sc_ref[:, cols]
            pg = p_ref[:, cols]
            dpre = (dmv[:, cols] * scg).astype(BF16)
            dpre_n = (dmn[:, cols] * scg).astype(BF16)
            pre = jnp.dot(pg, wg_ref[g], preferred_element_type=F32)
            dsc_ref[:, cols] += jnp.sum(dmv[:, cols] * pre, axis=0, keepdims=True)
            dwg_ref[g] += lax.dot_general(pg, dpre, (TN, ((), ())), preferred_element_type=F32)
            dpool = lax.dot_general(dpre, wg_ref[g], (NT, ((), ())), preferred_element_type=F32)
            dpool_n = lax.dot_general(dpre_n, wg_ref[g], (NT, ((), ())),
                                      preferred_element_type=F32)
            c0[0:TOK, cols] = dpool
            c0[TOK:n, cols] = dpool_n
            c1[0:TOK, cols] = dpool * inv[g][0:TOK]
            c1[TOK:n, cols] = dpool_n * inv[g][TOK:n]
        c2[0:n - 8, :] = c1[0:n - 8, :] + c1[1:n - 7, :]
        c3[0:n - 16, 128:] = c2[0:n - 16, 128:] + c2[2:n - 14, 128:]
        c1[0:n - 24, 256:] = c3[0:n - 24, 256:] + c3[4:n - 20, 256:]
        wins = [c2[0:TOK, 0:128], c3[0:TOK, 128:256], c1[0:TOK, 256:384],
                c1[0:TOK, 384:512] + c1[8:TOK + 8, 384:512]]
        for g in range(4):
            cols = slice(g * POOL_GD, (g + 1) * POOL_GD)
            du_ref[:, cols] = (wins[g] - c0[0:TOK, cols]).astype(BF16)

    buf = pltpu.VMEM((TOK + HALO, POOL_W), F32)
    return pl.pallas_call(
        body, name=name, grid=(nt,),
        in_specs=[_row_spec(POOL_W),
                  pl.BlockSpec((HALO, POOL_W), lambda i: (jnp.minimum((i + 1) * hb, nt * hb - 1), 0)),
                  _row_spec(POOL_W),
                  pl.BlockSpec((4, POOL_GD, POOL_GD), lambda i: (0, 0, 0)),
                  _vec_spec(POOL_W)],
        out_specs=[_row_spec(POOL_W), pl.BlockSpec((4, POOL_GD, POOL_GD), lambda i: (0, 0, 0)),
                   _vec_spec(POOL_W)],
        out_shape=[_sds((t, POOL_W), BF16), _sds((4, POOL_GD, POOL_GD), F32),
                   _sds((1, POOL_W), F32)],
        scratch_shapes=[buf, buf, buf, buf],
        compiler_params=_cparams("arbitrary"))(dmixed, dmixed, pooled, wg, scale)


GELU_C = math.sqrt(2.0 / math.pi)


def _gelu(x):
    return 0.5 * x * (1.0 + jnp.tanh(GELU_C * (x + 0.044715 * x * x * x)))


def _gelu_grad(x):
    th = jnp.tanh(GELU_C * (x + 0.044715 * x * x * x))
    return 0.5 * (1.0 + th) + 0.5 * x * (1.0 - th * th) * GELU_C * (1.0 + 3 * 0.044715 * x * x)


def _taps(buf, r, rows):
    a = buf[pl.ds(r, rows + 8), :]
    return a[8:], pltpu.roll(a, 1, axis=0)[8:], pltpu.roll(a, 2, axis=0)[8:]


def _conv(taps, w_ref, b_ref):
    return b_ref[...] + w_ref[2:3, :] * taps[0] + w_ref[1:2, :] * taps[1] + w_ref[0:1, :] * taps[2]


def _stage(dst, prev_ref, cur_ref, next_ref, first, last):
    rows = cur_ref.shape[0]
    h = prev_ref[...].astype(F32)
    dst[0:8, :] = jnp.where(first, jnp.zeros_like(h), h)
    dst[8:8 + rows, :] = cur_ref[...].astype(F32)
    if next_ref is not None:
        h = next_ref[...].astype(F32)
        dst[8 + rows:, :] = jnp.where(last, jnp.zeros_like(h), h)


FWD_STRIP = 32
BWD_STRIP = 16


def _ffn_gate_fwd(name, hu, conv_w, conv_b):
    t = hu.shape[0]
    ncol = D_FF // FF_COL
    hb = TOK // 8

    def tile(off):
        return pl.BlockSpec((TOK, FF_COL), lambda i, j: (i, j + off))

    def halo(off):
        return pl.BlockSpec((8, FF_COL), lambda i, j: (jnp.maximum(i * hb - 1, 0), j + off))

    def wspec(off):
        return pl.BlockSpec((3, FF_COL), lambda i, j: (0, j + off))

    def bspec(off):
        return pl.BlockSpec((1, FF_COL), lambda i, j: (0, j + off))

    def body(v_ref, vp_ref, g_ref, gp_ref, wv_ref, wg_ref, bv_ref, bg_ref, a_ref, vb, gb):
        first = pl.program_id(0) == 0
        _stage(vb, vp_ref, v_ref, None, first, None)
        _stage(gb, gp_ref, g_ref, None, first, None)

        def strip(k, carry):
            r = pl.multiple_of(k * FWD_STRIP, FWD_STRIP)
            val = _conv(_taps(vb, r, FWD_STRIP), wv_ref, bv_ref)
            gate = _conv(_taps(gb, r, FWD_STRIP), wg_ref, bg_ref)
            a_ref[pl.ds(r, FWD_STRIP), :] = (_gelu(gate) * val).astype(BF16)
            return carry

        lax.fori_loop(0, TOK // FWD_STRIP, strip, 0)

    buf = pltpu.VMEM((TOK + 8, FF_COL), F32)
    return pl.pallas_call(
        body, name=name, grid=(t // TOK, ncol),
        in_specs=[tile(0), halo(0), tile(ncol), halo(ncol), wspec(0), wspec(ncol),
                  bspec(0), bspec(ncol)],
        out_specs=pl.BlockSpec((TOK, FF_COL), lambda i, j: (i, j)),
        out_shape=_sds((t, D_FF), BF16), scratch_shapes=[buf, buf],
        compiler_params=_cparams("parallel", "parallel"))(
            hu, hu, hu, hu, conv_w, conv_w, conv_b, conv_b)


def _ffn_gate_bwd(name, da, hu, conv_w, conv_b):
    t = hu.shape[0]
    nt = t // TOK
    ncol = D_FF // FF_COL
    hb = TOK // 8
    ext = TOK + 8

    def tile(off):
        return pl.BlockSpec((TOK, FF_COL), lambda j, i: (i, j + off))

    def prev(off):
        return pl.BlockSpec((8, FF_COL), lambda j, i: (jnp.maximum(i * hb - 1, 0), j + off))

    def nxt(off):
        return pl.BlockSpec((8, FF_COL), lambda j, i: (jnp.minimum((i + 1) * hb, nt * hb - 1), j + off))

    def wspec(off):
        return pl.BlockSpec((3, FF_COL), lambda j, i: (0, j + off))

    def bspec(off):
        return pl.BlockSpec((1, FF_COL), lambda j, i: (0, j + off))

    def body(da_ref, dan_ref, v_ref, vp_ref, vn_ref, g_ref, gp_ref, gn_ref,
             wv_ref, wg_ref, bv_ref, bg_ref, dh_ref, dwv_ref, dwg_ref, vb, gb, dab):
        i = pl.program_id(1)
        first, last = i == 0, i == nt - 1

        @pl.when(first)
        def _():
            dwv_ref[...] = jnp.zeros_like(dwv_ref)
            dwg_ref[...] = jnp.zeros_like(dwg_ref)

        _stage(vb, vp_ref, v_ref, vn_ref, first, last)
        _stage(gb, gp_ref, g_ref, gn_ref, first, last)
        dab[0:TOK, :] = da_ref[...].astype(F32)
        h = dan_ref[...].astype(F32)
        dab[TOK:, :] = jnp.where(last, jnp.zeros_like(h), h)

        def grads(r, rows):
            tv, tg = _taps(vb, r, rows), _taps(gb, r, rows)
            gate = _conv(tg, wg_ref, bg_ref)
            dav = dab[pl.ds(r, rows), :]
            dval = dav * _gelu(gate)
            dgate = dav * _conv(tv, wv_ref, bv_ref) * _gelu_grad(gate)
            return dval, dgate, tv, tg

        def fold(x):
            return x[0:8] + x[8:16]

        def strip(k, carry):
            r = pl.multiple_of(TOK - BWD_STRIP - k * BWD_STRIP, BWD_STRIP)
            dval, dgate, tv, tg = grads(r, BWD_STRIP)
            new = (dval[0:8], dgate[0:8])
            for half, (d, nxt_rows, taps, w_ref, dw_ref) in enumerate((
                    (dval, carry[0], tv, wv_ref, dwv_ref), (dgate, carry[1], tg, wg_ref, dwg_ref))):
                e = jnp.concatenate([d, nxt_rows], axis=0)
                dh = (w_ref[2:3, :] * d
                      + w_ref[1:2, :] * pltpu.roll(e, BWD_STRIP + 7, axis=0)[0:BWD_STRIP]
                      + w_ref[0:1, :] * pltpu.roll(e, BWD_STRIP + 6, axis=0)[0:BWD_STRIP])
                dh_ref[half, pl.ds(r, BWD_STRIP), :] = dh.astype(BF16)
                dw_ref[0:8, :] += fold(d * taps[2])
                dw_ref[8:16, :] += fold(d * taps[1])
                dw_ref[16:24, :] += fold(d * taps[0])
                dw_ref[24:32, :] += fold(d)
            return new

        dval, dgate, _, _ = grads(TOK, 8)
        lax.fori_loop(0, TOK // BWD_STRIP, strip, (dval, dgate))

        @pl.when(last)
        def _():
            for dw_ref in (dwv_ref, dwg_ref):
                for q in range(4):
                    dw_ref[8 * q:8 * q + 1, :] = jnp.sum(dw_ref[8 * q:8 * q + 8, :], axis=0,
                                                         keepdims=True)

    hbuf = pltpu.VMEM((TOK + 16, FF_COL), F32)
    acc = pl.BlockSpec((32, FF_COL), lambda j, i: (0, j))
    dhu, dwv, dwg = pl.pallas_call(
        body, name=name, grid=(ncol, nt),
        in_specs=[tile(0), nxt(0), tile(0), prev(0), nxt(0), tile(ncol), prev(ncol), nxt(ncol),
                  wspec(0), wspec(ncol), bspec(0), bspec(ncol)],
        out_specs=[pl.BlockSpec((2, TOK, FF_COL), lambda j, i: (0, i, j)), acc, acc],
        out_shape=[_sds((2, t, D_FF), BF16), _sds((32, D_FF), F32), _sds((32, D_FF), F32)],
        scratch_shapes=[hbuf, hbuf, pltpu.VMEM((ext, FF_COL), F32)],
        compiler_params=_cparams("parallel", "arbitrary"))(
            da, da, hu, hu, hu, hu, hu, hu, conv_w, conv_w, conv_b, conv_b)
    return dhu, jnp.concatenate([dwv, dwg], axis=1).reshape(4, 8, 2 * D_FF)[:, 0]


def _mesh_pos():
    x, y, c = lax.axis_index("x"), lax.axis_index("y"), lax.axis_index("c")
    return x, y, c, [(1 - x, y), (x, 1 - y), (1 - x, 1 - y)]


def _any_specs(n):
    return [pl.BlockSpec(memory_space=pl.ANY)] * n


def _allgather_weights(shards):
    n = len(shards)

    def body(*refs):
        ins, outs = refs[:n], refs[n:2 * n]
        send_sems, recv_sems = refs[2 * n:]
        x, y, c, chips = _mesh_pos()
        me = 2 * x + y
        started = []
        own = []
        for k in range(n):
            for l in range(2):
                cp = pltpu.make_async_remote_copy(
                    src_ref=ins[k].at[l], dst_ref=outs[k].at[l, me],
                    send_sem=send_sems.at[k, 6 + l], recv_sem=recv_sems.at[k, 6 + l],
                    device_id=(x, y, 1 - c), device_id_type=MESH)
                cp.start()
                own.append(cp)
            for j, (cx, cy) in enumerate(chips):
                cp = pltpu.make_async_remote_copy(
                    src_ref=ins[k].at[c], dst_ref=outs[k].at[c, me],
                    send_sem=send_sems.at[k, j], recv_sem=recv_sems.at[k, j],
                    device_id=(cx, cy, c), device_id_type=MESH)
                cp.start()
                started.append(cp)
        for k in range(n):
            for j, (cx, cy) in enumerate(chips):
                landed = outs[k].at[c, 2 * cx + cy]
                pltpu.make_async_remote_copy(
                    src_ref=ins[k].at[c], dst_ref=landed,
                    send_sem=send_sems.at[k, j], recv_sem=recv_sems.at[k, j],
                    device_id=(cx, cy, c), device_id_type=MESH).wait_recv()
                fw = pltpu.make_async_remote_copy(
                    src_ref=landed, dst_ref=landed,
                    send_sem=send_sems.at[k, 3 + j], recv_sem=recv_sems.at[k, 3 + j],
                    device_id=(x, y, 1 - c), device_id_type=MESH)
                fw.start()
                started.append(fw)
        for k in range(n):
            for j, (cx, cy) in enumerate(chips):
                theirs = outs[k].at[1 - c, 2 * cx + cy]
                pltpu.make_async_remote_copy(
                    src_ref=theirs, dst_ref=theirs,
                    send_sem=send_sems.at[k, 3 + j], recv_sem=recv_sems.at[k, 3 + j],
                    device_id=(x, y, 1 - c), device_id_type=MESH).wait_recv()
        for cp in started:
            cp.wait_send()
        for cp in own:
            cp.wait()

    return pl.pallas_call(
        body, name="allgather_weights",
        in_specs=_any_specs(n), out_specs=_any_specs(n),
        out_shape=[_sds((2, N_CHIPS) + s.shape[1:], s.dtype) for s in shards],
        scratch_shapes=[pltpu.SemaphoreType.DMA((n, 8)), pltpu.SemaphoreType.DMA((n, 8))],
    )(*shards)


def _swap_layers(grads):
    n = len(grads)

    def body(*refs):
        ins, outs = refs[:n], refs[n:2 * n]
        send_sems, recv_sems = refs[2 * n:]
        x, y, c, _ = _mesh_pos()
        cps = []
        for k in range(n):
            cp = pltpu.make_async_remote_copy(
                src_ref=ins[k].at[1 - c], dst_ref=outs[k],
                send_sem=send_sems.at[k], recv_sem=recv_sems.at[k],
                device_id=(x, y, 1 - c), device_id_type=MESH)
            cp.start()
            cps.append(cp)
        for cp in cps:
            cp.wait()

    return pl.pallas_call(
        body, name="swap_layers",
        in_specs=_any_specs(n), out_specs=_any_specs(n),
        out_shape=[_sds(g.shape[1:], g.dtype) for g in grads],
        scratch_shapes=[pltpu.SemaphoreType.DMA((n,)), pltpu.SemaphoreType.DMA((n,))],
    )(*grads)


def _scatter_blocks(sums):
    n = len(sums)

    def body(*refs):
        ins, outs = refs[:n], refs[n:2 * n]
        send_sems, recv_sems = refs[2 * n:]
        x, y, c, chips = _mesh_pos()
        cps = []
        for k in range(n):
            for j, (cx, cy) in enumerate(chips):
                cp = pltpu.make_async_remote_copy(
                    src_ref=ins[k].at[2 * cx + cy], dst_ref=outs[k].at[j],
                    send_sem=send_sems.at[k, j], recv_sem=recv_sems.at[k, j],
                    device_id=(cx, cy, c), device_id_type=MESH)
                cp.start()
                cps.append(cp)
        for cp in cps:
            cp.wait()

    return pl.pallas_call(
        body, name="scatter_blocks",
        in_specs=_any_specs(n), out_specs=_any_specs(n),
        out_shape=[_sds((3,) + s.shape[1:], s.dtype) for s in sums],
        scratch_shapes=[pltpu.SemaphoreType.DMA((n, 3)), pltpu.SemaphoreType.DMA((n, 3))],
    )(*sums)


def _exchange_reduced(reds):
    n = len(reds)

    def body(*refs):
        outs = refs[n:2 * n]
        send_sems, recv_sems = refs[2 * n:]
        x, y, c, _ = _mesh_pos()
        cps = []
        for k in range(n):
            cp = pltpu.make_async_remote_copy(
                src_ref=outs[k].at[c], dst_ref=outs[k].at[c],
                send_sem=send_sems.at[k], recv_sem=recv_sems.at[k],
                device_id=(x, y, 1 - c), device_id_type=MESH)
            cp.start()
            cps.append(cp)
        for k in range(n):
            pltpu.make_async_remote_copy(
                src_ref=outs[k].at[c], dst_ref=outs[k].at[1 - c],
                send_sem=send_sems.at[k], recv_sem=recv_sems.at[k],
                device_id=(x, y, 1 - c), device_id_type=MESH).wait_recv()
        for cp in cps:
            cp.wait_send()

    return pl.pallas_call(
        body, name="exchange_reduced",
        in_specs=_any_specs(n), out_specs=_any_specs(n),
        out_shape=[_sds(r.shape, r.dtype) for r in reds],
        input_output_aliases={k: k for k in range(n)},
        scratch_shapes=[pltpu.SemaphoreType.DMA((n,)), pltpu.SemaphoreType.DMA((n,))],
    )(*reds)


def _allreduce_small(pack):
    n = pack.shape[0]

    def body(x_ref, o_ref, gbuf, send_sems, recv_sems):
        x, y, c, chips = _mesh_pos()
        sibling = (x, y, 1 - c)

        def slot(px, py, pc):
            return gbuf.at[4 * px + 2 * py + pc]

        def copy(k, block, to, src=None):
            return pltpu.make_async_remote_copy(
                src_ref=slot(*block) if src is None else src, dst_ref=slot(*block),
                send_sem=send_sems.at[k], recv_sem=recv_sems.at[k],
                device_id=to, device_id_type=MESH)

        me = (x, y, c)
        first = [copy(0, me, sibling, src=x_ref)]
        first += [copy(1 + j, me, (*chip, c), src=x_ref) for j, chip in enumerate(chips)]
        for cp in first:
            cp.start()
        gbuf[4 * x + 2 * y + c] = x_ref[...]
        passed = [copy(4 + j, (*chip, c), sibling) for j, chip in enumerate(chips)]
        for j, chip in enumerate(chips):
            copy(1 + j, (*chip, c), me).wait_recv()
            passed[j].start()
        copy(0, sibling, me).wait_recv()
        for j, chip in enumerate(chips):
            copy(4 + j, (*chip, 1 - c), me).wait_recv()
        for cp in first + passed:
            cp.wait_send()
        acc = gbuf[0]
        for d in range(1, 8):
            acc = acc + gbuf[d]
        o_ref[...] = acc

    return pl.pallas_call(
        body, name="allreduce_small",
        in_specs=[pl.BlockSpec(memory_space=pltpu.VMEM)],
        out_specs=pl.BlockSpec(memory_space=pltpu.VMEM),
        out_shape=_sds((n, 128), F32),
        scratch_shapes=[pltpu.VMEM((8, n, 128), F32), pltpu.SemaphoreType.DMA((7,)),
                        pltpu.SemaphoreType.DMA((7,))],
        compiler_params=pltpu.CompilerParams(vmem_limit_bytes=VMEM_LIMIT_V7X),
    )(pack)


def _core_index():
    return jnp.reshape(lax.axis_index("c"), (1,)).astype(jnp.int32)


def _chip_index():
    return jnp.reshape(2 * lax.axis_index("x") + lax.axis_index("y"), (1,)).astype(jnp.int32)


def _chip_sum(name, stacked, sib):
    _, nb, r, cdim = stacked.shape

    def body(c_ref, a_ref, b_ref, o_ref):
        o_ref[...] = (a_ref[...].astype(F32) + b_ref[...].astype(F32)).astype(BF16)

    return pl.pallas_call(
        body, name=name,
        grid_spec=pltpu.PrefetchScalarGridSpec(
            num_scalar_prefetch=1, grid=(nb,),
            in_specs=[pl.BlockSpec((None, None, r, cdim), lambda j, cr: (cr[0], j, 0, 0)),
                      pl.BlockSpec((None, r, cdim), lambda j, cr: (j, 0, 0))],
            out_specs=pl.BlockSpec((None, r, cdim), lambda j, cr: (j, 0, 0))),
        out_shape=_sds((nb, r, cdim), BF16),
        compiler_params=_cparams("parallel"))(_core_index(), stacked, sib)


def _final_sum(name, sums, recv):
    _, r, cdim = sums.shape
    tr = r // 2

    def body(m_ref, a_ref, b_ref, o_ref):
        acc = a_ref[...].astype(F32)
        for j in range(3):
            acc = acc + b_ref[j].astype(F32)
        o_ref[...] = acc

    return pl.pallas_call(
        body, name=name,
        grid_spec=pltpu.PrefetchScalarGridSpec(
            num_scalar_prefetch=1, grid=(2,),
            in_specs=[pl.BlockSpec((None, tr, cdim), lambda i, mr: (mr[0], i, 0)),
                      pl.BlockSpec((3, tr, cdim), lambda i, mr: (0, i, 0))],
            out_specs=pl.BlockSpec((None, tr, cdim), lambda i, mr: (mr[1], i, 0))),
        out_shape=_sds((DEPTH, r, cdim), F32),
        compiler_params=_cparams("parallel"))(
            jnp.concatenate([_chip_index(), _core_index()]), sums, recv)


def _adamw(name, w, g, m, v):
    nl, r, cdim = w.shape
    tr = r // 4 if r % 32 == 0 else r
    c1 = 1.0 - ADAM_B1 ** ADAM_STEP
    c2 = 1.0 - ADAM_B2 ** ADAM_STEP

    def body(w_ref, g_ref, m_ref, v_ref, d_ref, nm_ref, nv_ref):
        gv = g_ref[...]
        nm = ADAM_B1 * m_ref[...] + (1.0 - ADAM_B1) * gv
        nv = ADAM_B2 * v_ref[...] + (1.0 - ADAM_B2) * (gv * gv)
        nm_ref[...] = nm
        nv_ref[...] = nv
        d_ref[...] = -ADAM_LR * ((nm / c1) / (jnp.sqrt(nv / c2) + ADAM_EPS) + ADAM_WD * w_ref[...])

    spec = pl.BlockSpec((None, tr, cdim), lambda l, i: (l, i, 0))
    out = _sds(w.shape, F32)
    return pl.pallas_call(
        body, name=name, grid=(nl, r // tr),
        in_specs=[spec] * 4, out_specs=[spec] * 3, out_shape=[out] * 3,
        compiler_params=_cparams("parallel", "parallel"))(w, g, m, v)


def _rows128(a):
    return a.reshape(-1, 128)


def kernel(x, norm_mix_pre, w_in, b_gate, rel_bias, w_attn_out, w_pool_group, pool_scale, w_pool_out, w_o, norm_mix_post, norm_ffn_pre, w_up, conv_w, conv_b, w_down, norm_ffn_post, loss_target, m_norm_mix_pre, m_w_in, m_b_gate, m_rel_bias, m_w_attn_out, m_w_pool_group, m_pool_scale, m_w_pool_out, m_w_o, m_norm_mix_post, m_norm_ffn_pre, m_w_up, m_conv_w, m_conv_b, m_w_down, m_norm_ffn_post, v_norm_mix_pre, v_w_in, v_b_gate, v_rel_bias, v_w_attn_out, v_w_pool_group, v_pool_scale, v_w_pool_out, v_w_o, v_norm_mix_post, v_norm_ffn_pre, v_w_up, v_conv_w, v_conv_b, v_w_down, v_norm_ffn_post):
    t = x.shape[1]
    xs = x.reshape(t, D_MODEL)
    target = loss_target.reshape(t, D_MODEL)

    big = [w_in, w_attn_out, w_pool_out, w_o, w_up, w_down]
    gathered = _allgather_weights([w.astype(BF16) for w in big] + [conv_w])
    win_g, wao_g, wpo_g, wo_g, wup_g, wdn_g, cw_g = gathered
    wo_full = wo_g.reshape(DEPTH, D_MODEL, D_MODEL)
    wdn_full = wdn_g.reshape(DEPTH, D_FF, D_MODEL)
    cw_full = jnp.transpose(cw_g, (0, 2, 1, 3)).reshape(DEPTH, 3, 2 * D_FF)
    wg_bf = w_pool_group.astype(BF16)

    saved = []
    xcur = xs
    for l in range(DEPTH):
        tag = f"l{l}_"
        bias = _bias_table(tag + "bias_table", rel_bias[l])
        h = _norm_fwd(tag + "norm_mix_pre", xcur, norm_mix_pre[l:l + 1])
        proj = _mm_nn_blocked(tag + "proj", h, win_g, l, BF16)
        att = _attn_fwd(tag + "attn_fwd", proj, bias)
        pooled, mixed = _pool_fwd(tag + "pool_fwd", proj, wg_bf[l], pool_scale[l:l + 1])
        ya = _mm_nn_blocked(tag + "attn_out", att, wao_g, l, BF16)
        yb = _mm_nn_blocked(tag + "pool_out", mixed, wpo_g, l, BF16)
        z = _gate_fwd(tag + "gate_fwd", proj, b_gate[l:l + 1], ya, yb)
        mix = _mm_nn(tag + "mix", z, wo_full, l, D_MODEL, F32)
        x1 = _norm_residual_fwd(tag + "norm_mix_post", xcur, mix, norm_mix_post[l:l + 1])
        h2 = _norm_fwd(tag + "norm_ffn_pre", x1, norm_ffn_pre[l:l + 1])
        hu = _mm_nn_blocked(tag + "ffn_up", h2, wup_g, l, BF16)
        a = _ffn_gate_fwd(tag + "ffn_gate_fwd", hu, cw_full[l], conv_b[l:l + 1])
        f = _mm_nn(tag + "ffn_down", a, wdn_full, l, D_FF // 2, F32)
        x2 = _norm_residual_fwd(tag + "norm_ffn_post", x1, f, norm_ffn_post[l:l + 1])
        saved.append(dict(x=xcur, h=h, proj=proj, att=att, pooled=pooled, mixed=mixed, ya=ya,
                          yb=yb, z=z, mix=mix, x1=x1, h2=h2, hu=hu, a=a, f=f, bias=bias))
        xcur = x2

    dy, loss_local = _loss_head(xcur, target)
    loss = lax.psum(loss_local, ("x", "y", "c"))

    dx = dy
    names = ["w_in", "w_attn_out", "w_pool_out", "w_o", "w_up", "w_down"]
    dws = dict.fromkeys(names)
    small_grads = [None] * DEPTH
    for l in reversed(range(DEPTH)):
        tag = f"l{l}_"
        sv = saved[l]
        df, d_nfpost = _norm_post_bwd(tag + "norm_ffn_post_bwd", dx, sv["f"], norm_ffn_post[l:l + 1])
        da = _mm_nt(tag + "ffn_down_dx", df, wdn_full, l, D_FF // 2, BF16)
        dws["w_down"] = _mm_tn(tag + "ffn_down_dw", sv["a"], df, D_FF // 2, l, dws["w_down"])
        dhu, dconv = _ffn_gate_bwd(tag + "ffn_gate_bwd", da, sv["hu"], cw_full[l], conv_b[l:l + 1])
        dh2 = _mm_nt_blocked(tag + "ffn_up_dx", dhu, wup_g, l, F32)
        dws["w_up"] = _mm_tn_blocked(tag + "ffn_up_dw", sv["h2"], dhu, l, dws["w_up"])
        dx1, d_nfpre = _norm_pre_bwd(tag + "norm_ffn_pre_bwd", dh2, sv["x1"], dx, norm_ffn_pre[l:l + 1])
        dmix, d_nmpost = _norm_post_bwd(tag + "norm_mix_post_bwd", dx1, sv["mix"], norm_mix_post[l:l + 1])
        dz = _mm_nt(tag + "mix_dx", dmix, wo_full, l, D_MODEL, BF16)
        dws["w_o"] = _mm_tn(tag + "mix_dw", sv["z"], dmix, D_MODEL, l, dws["w_o"])
        dya, dyb, dgates, d_bgate = _gate_bwd(tag + "gate_bwd", dz, sv["proj"], b_gate[l:l + 1],
                                              sv["ya"], sv["yb"])
        datt = _mm_nt_blocked(tag + "attn_out_dx", dya, wao_g, l, BF16)
        dws["w_attn_out"] = _mm_tn_blocked(tag + "attn_out_dw", sv["att"], dya, l, dws["w_attn_out"])
        dmixed = _mm_nt_blocked(tag + "pool_out_dx", dyb, wpo_g, l, BF16)
        dws["w_pool_out"] = _mm_tn_blocked(tag + "pool_out_dw", sv["mixed"], dyb, l, dws["w_pool_out"])
        du, d_wg, d_pscale = _pool_bwd(tag + "pool_bwd", dmixed, sv["pooled"], wg_bf[l],
                                       pool_scale[l:l + 1])
        dq, dk, dv, dbias = _attn_bwd(tag + "attn_bwd", sv["proj"], datt, sv["bias"])
        d_rel = _bias_fold(tag + "bias_fold", dbias)
        dproj = jnp.concatenate([dq, dk, dv, du, dgates], axis=1)
        dh = _mm_nt_blocked(tag + "proj_dx", dproj, win_g, l, F32)
        dws["w_in"] = _mm_tn_blocked(tag + "proj_dw", sv["h"], dproj, l, dws["w_in"])
        dx, d_nmpre = _norm_pre_bwd(tag + "norm_mix_pre_bwd", dh, sv["x"], dx1, norm_mix_pre[l:l + 1])
        small_grads[l] = [d_nmpre, d_nmpost, d_nfpre, d_nfpost, d_bgate, d_rel, d_wg, d_pscale,
                          dconv[3:4], dconv[0:3]]

    grad_x = dx.reshape(x.shape)

    stacked = [dws[nm] for nm in names]
    stacked[3] = stacked[3].reshape(DEPTH, N_CHIPS, D_MODEL // N_CHIPS, D_MODEL)
    stacked[5] = stacked[5].reshape(DEPTH, N_CHIPS, D_FF // N_CHIPS, D_MODEL)
    sib = _swap_layers(stacked)
    sums = [_chip_sum("chip_sum_" + names[k], stacked[k], sib[k]) for k in range(6)]
    recv = _scatter_blocks(sums)
    reds = [_final_sum("final_sum_" + names[k], sums[k], recv[k]) for k in range(6)]
    g_big = _exchange_reduced(reds)

    pieces = []
    for idx in range(10):
        pieces.append(jnp.stack([small_grads[0][idx], small_grads[1][idx]]))
    pack = jnp.concatenate([_rows128(p) for p in pieces], axis=0)
    red = _allreduce_small(pack)
    shapes = [p.shape for p in pieces]
    outs = []
    row = 0
    for shp in shapes:
        nrow = math.prod(shp) // 128
        outs.append(red[row:row + nrow].reshape(shp))
        row += nrow
    (g_nmpre, g_nmpost, g_nfpre, g_nfpost, g_bgate, g_rel, g_wg, g_pscale, g_cb, g_cw) = outs
    g_nmpre, g_nmpost, g_nfpre, g_nfpost = [a.reshape(DEPTH, D_MODEL)
                                            for a in (g_nmpre, g_nmpost, g_nfpre, g_nfpost)]
    g_bgate = g_bgate.reshape(DEPTH, 2 * D_MODEL)
    g_rel = g_rel[:, :, :N_REL]
    g_pscale = g_pscale.reshape(DEPTH, POOL_W)
    g_cb = g_cb.reshape(DEPTH, 2 * D_FF)
    ncw = conv_w.shape[2]
    chip = 2 * lax.axis_index("x") + lax.axis_index("y")
    g_cw = lax.dynamic_slice_in_dim(g_cw, chip * ncw, ncw, axis=2)

    grads = dict(norm_mix_pre=g_nmpre, w_in=g_big[0], b_gate=g_bgate, rel_bias=g_rel,
                 w_attn_out=g_big[1], w_pool_group=g_wg, pool_scale=g_pscale, w_pool_out=g_big[2],
                 w_o=g_big[3], norm_mix_post=g_nmpost, norm_ffn_pre=g_nfpre, w_up=g_big[4],
                 conv_w=g_cw, conv_b=g_cb, w_down=g_big[5], norm_ffn_post=g_nfpost)
    weights = dict(norm_mix_pre=norm_mix_pre, w_in=w_in, b_gate=b_gate, rel_bias=rel_bias,
                   w_attn_out=w_attn_out, w_pool_group=w_pool_group, pool_scale=pool_scale,
                   w_pool_out=w_pool_out, w_o=w_o, norm_mix_post=norm_mix_post,
                   norm_ffn_pre=norm_ffn_pre, w_up=w_up, conv_w=conv_w, conv_b=conv_b,
                   w_down=w_down, norm_ffn_post=norm_ffn_post)
    moms = dict(norm_mix_pre=(m_norm_mix_pre, v_norm_mix_pre), w_in=(m_w_in, v_w_in),
                b_gate=(m_b_gate, v_b_gate), rel_bias=(m_rel_bias, v_rel_bias),
                w_attn_out=(m_w_attn_out, v_w_attn_out),
                w_pool_group=(m_w_pool_group, v_w_pool_group),
                pool_scale=(m_pool_scale, v_pool_scale), w_pool_out=(m_w_pool_out, v_w_pool_out),
                w_o=(m_w_o, v_w_o), norm_mix_post=(m_norm_mix_post, v_norm_mix_post),
                norm_ffn_pre=(m_norm_ffn_pre, v_norm_ffn_pre), w_up=(m_w_up, v_w_up),
                conv_w=(m_conv_w, v_conv_w), conv_b=(m_conv_b, v_conv_b),
                w_down=(m_w_down, v_w_down), norm_ffn_post=(m_norm_ffn_post, v_norm_ffn_post))
    order = list(weights.keys())

    delta, new_m, new_v = {}, {}, {}
    small_names = [nm for nm in order if nm not in names]
    for nm in names:
        delta[nm], new_m[nm], new_v[nm] = _adamw("adamw_" + nm, weights[nm], grads[nm], *moms[nm])

    def pack_small(get):
        flat = [get(nm).reshape(-1) for nm in small_names]
        total = sum(f.shape[0] for f in flat)
        padded = -(-total // 1024) * 1024
        flat.append(jnp.zeros((padded - total,), F32))
        return jnp.concatenate(flat).reshape(1, padded // 128, 128)

    d_s, m_s, v_s = _adamw(
        "adamw_small", pack_small(lambda nm: weights[nm]), pack_small(lambda nm: grads[nm]),
        pack_small(lambda nm: moms[nm][0]) , pack_small(lambda nm: moms[nm][1]))
    off = 0
    for nm in small_names:
        size = math.prod(weights[nm].shape)
        for dst, src in ((delta, d_s), (new_m, m_s), (new_v, v_s)):
            dst[nm] = src.reshape(-1)[off:off + size].reshape(weights[nm].shape)
        off += size

    return (loss, grad_x, *[grads[nm] for nm in order], *[delta[nm] for nm in order],
            *[new_m[nm] for nm in order], *[new_v[nm] for nm in order])
```

```python
import functools
import math

import jax
import jax.numpy as jnp
from jax import lax
from jax.experimental import pallas as pl
from jax.experimental.pallas import tpu as pltpu

F32 = jnp.float32
BF16 = jnp.bfloat16
MESH = pl.DeviceIdType.MESH

D_MODEL = 1024
DEPTH = 2
CHUNK = 64
BAND_CHUNKS = 9
BAND = BAND_CHUNKS * CHUNK
HEADS = 8
HEAD_DIM = 64
ATTN_W = HEADS * HEAD_DIM
POOL_WINDOWS = (2, 4, 8, 16)
POOL_W = 512
POOL_GD = 128
MAX_REL = 256
N_REL = 2 * MAX_REL + 1
D_FF = 2816
IN_W = 3 * ATTN_W + POOL_W + 2 * D_MODEL
EPS = 1e-6
ATTN_SCALE = HEAD_DIM ** -0.5
BAND_PAD = 640
BIAS_LANES = BAND_PAD
N_CHIPS = 4

ADAM_LR = 0.001
ADAM_B1 = 0.9
ADAM_B2 = 0.999
ADAM_EPS = 1e-08
ADAM_WD = 0.01
ADAM_STEP = 10

VMEM_LIMIT_V7X = 56 * 1024 * 1024
TOK = 512
ATT_BLK = 8 * CHUNK
FF_COL = 256
HALO = 32


def _cparams(*sem):
    return pltpu.CompilerParams(dimension_semantics=sem, vmem_limit_bytes=VMEM_LIMIT_V7X)


def _sds(shape, dtype):
    return jax.ShapeDtypeStruct(shape, dtype)


def _matmul(name, a, b, a_spec, b_spec, o_spec, out_shape, grid, contract, nk, acc_shape,
            fill=None):
    def body(*refs):
        a_ref, b_ref = refs[0], refs[1]
        o_ref = refs[2 if fill is None else 3]
        scratch = refs[(3 if fill is None else 4):]
        part = lax.dot_general(a_ref[...], b_ref[...], (contract, ((), ())),
                               preferred_element_type=F32)
        if nk == 1:
            o_ref[...] = part.astype(o_ref.dtype)
        else:
            acc_ref = scratch[0]
            k = pl.program_id(2)

            @pl.when(k == 0)
            def _():
                acc_ref[...] = part

            @pl.when(k > 0)
            def _():
                acc_ref[...] += part

            @pl.when(k == nk - 1)
            def _():
                o_ref[...] = acc_ref[...].astype(o_ref.dtype)

    scratch = [] if nk == 1 else [pltpu.VMEM(acc_shape, F32)]
    in_specs, args, aliases = [a_spec, b_spec], [a, b], {}
    if fill is not None:
        in_specs.append(pl.BlockSpec(memory_space=pl.ANY))
        args.append(fill)
        aliases = {2: 0}
    return pl.pallas_call(
        body, name=name, grid=grid, in_specs=in_specs, out_specs=o_spec,
        out_shape=out_shape, scratch_shapes=scratch, input_output_aliases=aliases,
        compiler_params=_cparams("parallel", "parallel", "arbitrary"),
    )(*args)


NN = ((1,), (0,))
NT = ((1,), (1,))
TN = ((0,), (0,))


def _tm(t):
    return min(t, 1024)


def _col_block_spec(a, rows, nb, row_col):
    if a.ndim == 2:
        return pl.BlockSpec((rows, nb), row_col)

    def halves(*ids):
        r, c = row_col(*ids)
        return c // 2, r, c % 2

    return pl.BlockSpec((None, rows, nb), halves)


def _mm_nn_blocked(name, a, w, l, out_dtype):
    t, k = a.shape
    nb = w.shape[3]
    tm = _tm(t)
    return _matmul(
        name, a, w,
        pl.BlockSpec((tm, k), lambda i, n, kk: (i, 0)),
        pl.BlockSpec((None, None, k, nb), lambda i, n, kk: (l, n, 0, 0)),
        pl.BlockSpec((tm, nb), lambda i, n, kk: (i, n)),
        _sds((t, N_CHIPS * nb), out_dtype), (t // tm, N_CHIPS, 1), NN, 1, None)


def _mm_nt_blocked(name, a, w, l, out_dtype):
    t = a.shape[-2]
    k, nb = w.shape[2], w.shape[3]
    tm = _tm(t)
    return _matmul(
        name, a, w,
        _col_block_spec(a, tm, nb, lambda i, n, kk: (i, kk)),
        pl.BlockSpec((None, None, k, nb), lambda i, n, kk: (l, kk, 0, 0)),
        pl.BlockSpec((tm, k), lambda i, n, kk: (i, 0)),
        _sds((t, k), out_dtype), (t // tm, 1, N_CHIPS), NT, N_CHIPS, (tm, k))


def _mm_tn_blocked(name, a, g, l, fill):
    t, k = a.shape
    nb = g.shape[-1] * (g.ndim - 1) // N_CHIPS
    tt = _tm(t)
    nt = t // tt
    return _matmul(
        name, a, g,
        pl.BlockSpec((tt, k), lambda n, j, kk: (kk, 0)),
        _col_block_spec(g, tt, nb, lambda n, j, kk: (kk, n)),
        pl.BlockSpec((None, None, k, nb), lambda n, j, kk: (l, n, 0, 0)),
        _sds((DEPTH, N_CHIPS, k, nb), BF16), (N_CHIPS, 1, nt), TN, nt, (k, nb), fill)


def _mm_nn(name, a, w, l, tk, out_dtype):
    t, k = a.shape
    n = w.shape[2]
    tm = _tm(t)
    nk = k // tk
    return _matmul(
        name, a, w,
        pl.BlockSpec((tm, tk), lambda i, j, kk: (i, kk)),
        pl.BlockSpec((None, tk, n), lambda i, j, kk: (l, kk, 0)),
        pl.BlockSpec((tm, n), lambda i, j, kk: (i, 0)),
        _sds((t, n), out_dtype), (t // tm, 1, nk), NN, nk, (tm, n))


def _mm_nt(name, a, w, l, tn, out_dtype):
    t, n = a.shape
    k = w.shape[1]
    tm = _tm(t)
    return _matmul(
        name, a, w,
        pl.BlockSpec((tm, n), lambda i, j, kk: (i, 0)),
        pl.BlockSpec((None, tn, n), lambda i, j, kk: (l, j, 0)),
        pl.BlockSpec((tm, tn), lambda i, j, kk: (i, j)),
        _sds((t, k), out_dtype), (t // tm, k // tn, 1), NT, 1, None)


def _mm_tn(name, a, g, tko, l, fill):
    t, k = a.shape
    n = g.shape[1]
    tt = _tm(t)
    nt = t // tt
    return _matmul(
        name, a, g,
        pl.BlockSpec((tt, tko), lambda i, j, kk: (kk, i)),
        pl.BlockSpec((tt, n), lambda i, j, kk: (kk, 0)),
        pl.BlockSpec((None, tko, n), lambda i, j, kk: (l, i, 0)),
        _sds((DEPTH, k, n), BF16), (k // tko, 1, nt), TN, nt, (tko, n), fill)


def _row_spec(width, col=0):
    return pl.BlockSpec((TOK, width), lambda i: (i, col))


def _vec_spec(width):
    return pl.BlockSpec((1, width), lambda i: (0, 0))


def _rms(x):
    return lax.rsqrt(jnp.mean(x * x, axis=-1, keepdims=True) + EPS)


def _norm_fwd(name, x, g):
    t = x.shape[0]

    def body(x_ref, g_ref, h_ref):
        xv = x_ref[...]
        h_ref[...] = (xv * _rms(xv) * g_ref[...]).astype(BF16)

    return pl.pallas_call(
        body, name=name, grid=(t // TOK,), in_specs=[_row_spec(D_MODEL), _vec_spec(D_MODEL)],
        out_specs=_row_spec(D_MODEL), out_shape=_sds((t, D_MODEL), BF16),
        compiler_params=_cparams("parallel"))(x, g)


def _norm_residual_fwd(name, xres, m, g):
    t = xres.shape[0]

    def body(x_ref, m_ref, g_ref, o_ref):
        mv = m_ref[...]
        o_ref[...] = x_ref[...] + mv * _rms(mv) * g_ref[...]

    return pl.pallas_call(
        body, name=name, grid=(t // TOK,),
        in_specs=[_row_spec(D_MODEL), _row_spec(D_MODEL), _vec_spec(D_MODEL)],
        out_specs=_row_spec(D_MODEL), out_shape=_sds((t, D_MODEL), F32),
        compiler_params=_cparams("parallel"))(xres, m, g)


def _norm_post_bwd(name, dxo, m, g):
    t = dxo.shape[0]

    def body(d_ref, m_ref, g_ref, dm_ref, dg_ref):
        mv = m_ref[...]
        dv = d_ref[...]
        r = _rms(mv)
        n = mv * r
        dn = dv * g_ref[...]
        dm_ref[...] = (r * (dn - n * jnp.mean(dn * n, axis=-1, keepdims=True))).astype(BF16)
        part = jnp.sum(dv * n, axis=0, keepdims=True)

        @pl.when(pl.program_id(0) == 0)
        def _():
            dg_ref[...] = part

        @pl.when(pl.program_id(0) > 0)
        def _():
            dg_ref[...] += part

    return pl.pallas_call(
        body, name=name, grid=(t // TOK,),
        in_specs=[_row_spec(D_MODEL), _row_spec(D_MODEL), _vec_spec(D_MODEL)],
        out_specs=[_row_spec(D_MODEL), _vec_spec(D_MODEL)],
        out_shape=[_sds((t, D_MODEL), BF16), _sds((1, D_MODEL), F32)],
        compiler_params=_cparams("arbitrary"))(dxo, m, g)


def _norm_pre_bwd(name, dh, xin, dxo, g):
    t = dh.shape[0]

    def body(dh_ref, x_ref, d_ref, g_ref, dx_ref, dg_ref):
        xv = x_ref[...]
        dhv = dh_ref[...]
        r = _rms(xv)
        n = xv * r
        dn = dhv * g_ref[...]
        dx_ref[...] = d_ref[...] + r * (dn - n * jnp.mean(dn * n, axis=-1, keepdims=True))
        part = jnp.sum(dhv * n, axis=0, keepdims=True)

        @pl.when(pl.program_id(0) == 0)
        def _():
            dg_ref[...] = part

        @pl.when(pl.program_id(0) > 0)
        def _():
            dg_ref[...] += part

    return pl.pallas_call(
        body, name=name, grid=(t // TOK,),
        in_specs=[_row_spec(D_MODEL), _row_spec(D_MODEL), _row_spec(D_MODEL), _vec_spec(D_MODEL)],
        out_specs=[_row_spec(D_MODEL), _vec_spec(D_MODEL)],
        out_shape=[_sds((t, D_MODEL), F32), _sds((1, D_MODEL), F32)],
        compiler_params=_cparams("arbitrary"))(dh, xin, dxo, g)


def _loss_head(y, target):
    t = y.shape[0]

    def body(y_ref, t_ref, dy_ref, l_ref):
        e = y_ref[...] - t_ref[...]
        dy_ref[...] = e * (1.0 / D_MODEL)
        part = jnp.sum(jnp.sum(e * e, axis=0, keepdims=True), axis=1, keepdims=True)

        @pl.when(pl.program_id(0) == 0)
        def _():
            l_ref[...] = part

        @pl.when(pl.program_id(0) > 0)
        def _():
            l_ref[...] += part

    dy, sq = pl.pallas_call(
        body, name="loss_head", grid=(t // TOK,),
        in_specs=[_row_spec(D_MODEL), _row_spec(D_MODEL)],
        out_specs=[_row_spec(D_MODEL), pl.BlockSpec((1, 1), lambda i: (0, 0))],
        out_shape=[_sds((t, D_MODEL), F32), _sds((1, 1), F32)],
        compiler_params=_cparams("arbitrary"))(y, target)
    return dy, sq[0, 0] * (0.5 / D_MODEL)


def _gate_fwd(name, proj, b_gate, ya, yb):
    t = proj.shape[0]

    def body(ga_ref, gb_ref, b_ref, ya_ref, yb_ref, z_ref):
        sa = jax.nn.sigmoid(ga_ref[...].astype(F32) + b_ref[:, :D_MODEL])
        sb = jax.nn.sigmoid(gb_ref[...].astype(F32) + b_ref[:, D_MODEL:])
        z_ref[...] = (sa * ya_ref[...].astype(F32) + sb * yb_ref[...].astype(F32)).astype(BF16)

    return pl.pallas_call(
        body, name=name, grid=(t // TOK,),
        in_specs=[_row_spec(D_MODEL, 2), _row_spec(D_MODEL, 3), _vec_spec(2 * D_MODEL),
                  _row_spec(D_MODEL), _row_spec(D_MODEL)],
        out_specs=_row_spec(D_MODEL), out_shape=_sds((t, D_MODEL), BF16),
        compiler_params=_cparams("parallel"))(proj, proj, b_gate, ya, yb)


def _gate_bwd(name, dz, proj, b_gate, ya, yb):
    t = proj.shape[0]

    def body(dz_ref, ga_ref, gb_ref, b_ref, ya_ref, yb_ref, dya_ref, dyb_ref, dg_ref, db_ref):
        dzv = dz_ref[...].astype(F32)
        sa = jax.nn.sigmoid(ga_ref[...].astype(F32) + b_ref[:, :D_MODEL])
        sb = jax.nn.sigmoid(gb_ref[...].astype(F32) + b_ref[:, D_MODEL:])
        dya_ref[...] = (dzv * sa).astype(BF16)
        dyb_ref[...] = (dzv * sb).astype(BF16)
        dga = dzv * ya_ref[...].astype(F32) * sa * (1.0 - sa)
        dgb = dzv * yb_ref[...].astype(F32) * sb * (1.0 - sb)
        dg_ref[:, :D_MODEL] = dga.astype(BF16)
        dg_ref[:, D_MODEL:] = dgb.astype(BF16)
        pa = jnp.sum(dga, axis=0, keepdims=True)
        pb = jnp.sum(dgb, axis=0, keepdims=True)

        @pl.when(pl.program_id(0) == 0)
        def _():
            db_ref[:, :D_MODEL] = pa
            db_ref[:, D_MODEL:] = pb

        @pl.when(pl.program_id(0) > 0)
        def _():
            db_ref[:, :D_MODEL] += pa
            db_ref[:, D_MODEL:] += pb

    return pl.pallas_call(
        body, name=name, grid=(t // TOK,),
        in_specs=[_row_spec(D_MODEL), _row_spec(D_MODEL, 2), _row_spec(D_MODEL, 3),
                  _vec_spec(2 * D_MODEL), _row_spec(D_MODEL), _row_spec(D_MODEL)],
        out_specs=[_row_spec(D_MODEL), _row_spec(D_MODEL), _row_spec(2 * D_MODEL),
                   _vec_spec(2 * D_MODEL)],
        out_shape=[_sds((t, D_MODEL), BF16), _sds((t, D_MODEL), BF16),
                   _sds((t, 2 * D_MODEL), BF16), _sds((1, 2 * D_MODEL), F32)],
        compiler_params=_cparams("arbitrary"))(dz, proj, proj, b_gate, ya, yb)


def _head_masks():
    lane = lax.broadcasted_iota(jnp.int32, (1, 2 * HEAD_DIM), 1)
    return lane < HEAD_DIM


BAND_ROWS = 2 * ATT_BLK + CHUNK


def _fill_band(band, prev_ref, cur_ref):
    band[0:ATT_BLK, :] = prev_ref[...]
    band[ATT_BLK:2 * ATT_BLK, :] = cur_ref[...]
    band[2 * ATT_BLK:, :] = jnp.zeros((CHUNK, ATTN_W), BF16)


def _pair_rows(x2, low):
    zero = jnp.zeros_like(x2)
    return jnp.concatenate([jnp.where(low, x2, zero), jnp.where(low, zero, x2)], axis=0)


def _pair_diag(o2, low):
    return jnp.where(low, o2[0:CHUNK, :], o2[CHUNK:, :])


N_PAIRS = HEADS // 2
SM_STRIP = 32
N_STRIPS = BAND_PAD // SM_STRIP
NEG = -1e30


def _fold8(x, op):
    return op(op(x[0:8], x[8:16]), op(x[16:24], x[24:32]))


def _strip(k):
    return pl.ds(pl.multiple_of(k * SM_STRIP, SM_STRIP), SM_STRIP)


def _band_probs(k2, qcat, bias_t, first_key):
    kpos = lax.broadcasted_iota(jnp.int32, (BAND_PAD, 1), 0)
    st = lax.dot_general(k2, qcat, (NT, ((), ())), preferred_element_type=F32)
    st = jnp.where(kpos + first_key >= 0, st + bias_t, NEG)
    e = jnp.exp(st - jnp.max(st, axis=0, keepdims=True))
    return e * (1.0 / jnp.sum(e, axis=0, keepdims=True))


def _band_softmax_stats(st_ref, b_ref, first_key, dp_ref):
    rowi = lax.broadcasted_iota(jnp.int32, (SM_STRIP, 128), 0)

    def scores(k, mx):
        rows = _strip(k)
        live = (rowi + (k * SM_STRIP + first_key)) >= 0
        out = []
        for hp in range(N_PAIRS):
            x = jnp.where(live, st_ref[hp, rows, :] + b_ref[hp, rows, :], NEG)
            st_ref[hp, rows, :] = x
            out.append(jnp.maximum(mx[hp], _fold8(x, jnp.maximum)))
        return tuple(out)

    mx = lax.fori_loop(0, N_STRIPS, scores, (jnp.full((8, 128), NEG, F32),) * N_PAIRS, unroll=2)
    top = [jnp.max(m, axis=0, keepdims=True) for m in mx]

    def sums(k, acc):
        rows = _strip(k)
        ls, eds = [], []
        for hp in range(N_PAIRS):
            e = jnp.exp(st_ref[hp, rows, :] - top[hp])
            ls.append(acc[hp] + _fold8(e, jnp.add))
            eds.append(acc[N_PAIRS + hp] + _fold8(e * dp_ref[hp, rows, :], jnp.add))
        return tuple(ls + eds)

    acc = lax.fori_loop(0, N_STRIPS, sums, (jnp.zeros((8, 128), F32),) * (2 * N_PAIRS), unroll=2)
    inv = [1.0 / jnp.sum(a, axis=0, keepdims=True) for a in acc[:N_PAIRS]]
    delta = [jnp.sum(a, axis=0, keepdims=True) * i for a, i in zip(acc[N_PAIRS:], inv)]
    return top, inv, delta


def _attn_specs(nblk):
    cur = lambda col: pl.BlockSpec((ATT_BLK, ATTN_W), lambda s: (jnp.minimum(s, nblk - 1), col))
    prev = lambda col: pl.BlockSpec(
        (ATT_BLK, ATTN_W), lambda s: (jnp.maximum(jnp.minimum(s, nblk - 1) - 1, 0), col))
    return cur, prev


def _attn_fwd(name, proj, bias):
    t = proj.shape[0]
    nblk = t // ATT_BLK
    cur, prev = _attn_specs(nblk)

    def body(q_ref, kp_ref, kc_ref, vp_ref, vc_ref, b_ref, o_ref, kband, vband):
        s = pl.program_id(0)
        _fill_band(kband, kp_ref, kc_ref)
        _fill_band(vband, vp_ref, vc_ref)
        low = _head_masks()

        def chunk(ci, carry):
            r0 = pl.multiple_of(ci * CHUNK, CHUNK)
            for hp in range(N_PAIRS):
                cols = slice(hp * 128, (hp + 1) * 128)
                qcat = _pair_rows(q_ref[pl.ds(r0, CHUNK), cols] * ATTN_SCALE, low)
                p = _band_probs(kband[pl.ds(r0, BAND_PAD), cols], qcat, b_ref[hp],
                                (s * 8 - 8 + ci) * CHUNK)
                o2 = lax.dot_general(p.astype(BF16), vband[pl.ds(r0, BAND_PAD), cols],
                                     (TN, ((), ())), preferred_element_type=F32)
                o_ref[pl.ds(r0, CHUNK), cols] = _pair_diag(o2, low).astype(BF16)
            return carry

        lax.fori_loop(0, 8, chunk, 0)

    return pl.pallas_call(
        body, name=name, grid=(nblk,),
        in_specs=[cur(0), prev(1), cur(1), prev(2), cur(2),
                  pl.BlockSpec((N_PAIRS, BAND_PAD, 128), lambda s: (0, 0, 0))],
        out_specs=pl.BlockSpec((ATT_BLK, ATTN_W), lambda s: (s, 0)),
        out_shape=_sds((t, ATTN_W), BF16),
        scratch_shapes=[pltpu.VMEM((BAND_ROWS, ATTN_W), BF16),
                        pltpu.VMEM((BAND_ROWS, ATTN_W), BF16)],
        compiler_params=_cparams("arbitrary"))(proj, proj, proj, proj, proj, bias)


def _attn_bwd(name, proj, datt, bias):
    t = proj.shape[0]
    nblk = t // ATT_BLK
    cur, prev = _attn_specs(nblk)
    late = pl.BlockSpec((ATT_BLK, ATTN_W), lambda s: (jnp.maximum(s - 1, 0), 0))

    def body(q_ref, kp_ref, kc_ref, vp_ref, vc_ref, do_ref, b_ref,
             dq_ref, dk_ref, dv_ref, db_ref, kband, vband, dkacc, dvacc,
             st_ref, dp_ref, pb_ref, dsb_ref, qc_ref, dc_ref):
        s = pl.program_id(0)

        @pl.when(s == 0)
        def _():
            dkacc[...] = jnp.zeros_like(dkacc)
            dvacc[...] = jnp.zeros_like(dvacc)
            db_ref[...] = jnp.zeros_like(db_ref)

        @pl.when(s < nblk)
        def _():
            _fill_band(kband, kp_ref, kc_ref)
            _fill_band(vband, vp_ref, vc_ref)
            low = _head_masks()

            def chunk(ci, carry):
                r0 = pl.multiple_of(ci * CHUNK, CHUNK)
                for hp in range(N_PAIRS):
                    cols = slice(hp * 128, (hp + 1) * 128)
                    qc_ref[hp] = _pair_rows(q_ref[pl.ds(r0, CHUNK), cols] * ATTN_SCALE, low)
                    dc_ref[hp] = _pair_rows(do_ref[pl.ds(r0, CHUNK), cols], low)
                    st_ref[hp] = lax.dot_general(kband[pl.ds(r0, BAND_PAD), cols], qc_ref[hp],
                                                 (NT, ((), ())), preferred_element_type=F32)
                    dp_ref[hp] = lax.dot_general(vband[pl.ds(r0, BAND_PAD), cols], dc_ref[hp],
                                                 (NT, ((), ())), preferred_element_type=F32)
                top, inv, delta = _band_softmax_stats(st_ref, b_ref, (s * 8 - 8 + ci) * CHUNK,
                                                      dp_ref)

                def grads(k, c):
                    rows = _strip(k)
                    for hp in range(N_PAIRS):
                        p = jnp.exp(st_ref[hp, rows, :] - top[hp]) * inv[hp]
                        ds = p * (dp_ref[hp, rows, :] - delta[hp])
                        db_ref[hp, rows, :] += ds
                        dsb_ref[hp, rows, :] = ds.astype(BF16)
                        pb_ref[hp, rows, :] = p.astype(BF16)
                    return c

                lax.fori_loop(0, N_STRIPS, grads, 0, unroll=2)
                for hp in range(N_PAIRS):
                    cols = slice(hp * 128, (hp + 1) * 128)
                    dq2 = lax.dot_general(dsb_ref[hp], kband[pl.ds(r0, BAND_PAD), cols],
                                          (TN, ((), ())), preferred_element_type=F32)
                    dq_ref[pl.ds(r0, CHUNK), cols] = (_pair_diag(dq2, low) * ATTN_SCALE).astype(BF16)
                    dkacc[pl.ds(r0, BAND_PAD), cols] += jnp.dot(dsb_ref[hp], qc_ref[hp],
                                                               preferred_element_type=F32)
                    dvacc[pl.ds(r0, BAND_PAD), cols] += jnp.dot(pb_ref[hp], dc_ref[hp],
                                                               preferred_element_type=F32)
                return carry

            lax.fori_loop(0, 8, chunk, 0)

        dk_ref[...] = dkacc[0:ATT_BLK, :].astype(BF16)
        dv_ref[...] = dvacc[0:ATT_BLK, :].astype(BF16)
        dkacc[0:ATT_BLK, :] = dkacc[ATT_BLK:2 * ATT_BLK, :]
        dvacc[0:ATT_BLK, :] = dvacc[ATT_BLK:2 * ATT_BLK, :]
        dkacc[ATT_BLK:, :] = jnp.zeros((ATT_BLK + CHUNK, ATTN_W), F32)
        dvacc[ATT_BLK:, :] = jnp.zeros((ATT_BLK + CHUNK, ATTN_W), F32)

    blk = _sds((t, ATTN_W), BF16)
    return pl.pallas_call(
        body, name=name, grid=(nblk + 1,),
        in_specs=[cur(0), prev(1), cur(1), prev(2), cur(2),
                  pl.BlockSpec((ATT_BLK, ATTN_W), lambda s: (jnp.minimum(s, nblk - 1), 0)),
                  pl.BlockSpec((HEADS // 2, BAND_PAD, 128), lambda s: (0, 0, 0))],
        out_specs=[pl.BlockSpec((ATT_BLK, ATTN_W), lambda s: (jnp.minimum(s, nblk - 1), 0)),
                   late, late,
                   pl.BlockSpec((HEADS // 2, BAND_PAD, 128), lambda s: (0, 0, 0))],
        out_shape=[blk, blk, blk, _sds((HEADS // 2, BAND_PAD, 128), F32)],
        scratch_shapes=[pltpu.VMEM((BAND_ROWS, ATTN_W), BF16),
                        pltpu.VMEM((BAND_ROWS, ATTN_W), BF16),
                        pltpu.VMEM((BAND_ROWS, ATTN_W), F32),
                        pltpu.VMEM((BAND_ROWS, ATTN_W), F32),
                        pltpu.VMEM((N_PAIRS, BAND_PAD, 128), F32),
                        pltpu.VMEM((N_PAIRS, BAND_PAD, 128), F32),
                        pltpu.VMEM((N_PAIRS, BAND_PAD, 128), BF16),
                        pltpu.VMEM((N_PAIRS, BAND_PAD, 128), BF16),
                        pltpu.VMEM((N_PAIRS, 2 * CHUNK, 128), BF16),
                        pltpu.VMEM((N_PAIRS, 2 * CHUNK, 128), BF16)],
        compiler_params=_cparams("arbitrary"))(proj, proj, proj, proj, proj, datt, bias)


def _diag_onehot(rel_rows):
    d0 = lax.broadcasted_iota(jnp.int32, (BIAS_LANES, BIAS_LANES), 0)
    d1 = lax.broadcasted_iota(jnp.int32, (BIAS_LANES, BIAS_LANES), 1)
    m, n = (d0, d1) if rel_rows else (d1, d0)
    hit = (m == jnp.minimum(BAND - 1 + MAX_REL - n, 2 * MAX_REL)) & (n < BAND + CHUNK - 1)
    return jnp.where(hit, 1.0, 0.0).astype(F32)


def _bias_table(name, rel_bias_l):
    rel_pad = jnp.pad(rel_bias_l, ((0, 0), (0, BIAS_LANES - N_REL)))

    def body(r_ref, o_ref):
        diag = jnp.dot(r_ref[...], _diag_onehot(True), preferred_element_type=F32,
                       precision=lax.Precision.HIGHEST)
        rowid = lax.broadcasted_iota(jnp.int32, (8, BIAS_LANES), 0)
        lane = lax.broadcasted_iota(jnp.int32, (8, BIAS_LANES), 1)
        for h in range(HEADS):
            d8 = jnp.broadcast_to(diag[h:h + 1, :], (8, BIAS_LANES))
            slab0 = pltpu.roll(d8, BIAS_LANES - CHUNK + 1, axis=1)
            for b in range(1, 8):
                slab0 = jnp.where(rowid == b, pltpu.roll(d8, BIAS_LANES - CHUNK + 1 + b, axis=1),
                                  slab0)
            for a in range(8):
                slab = slab0 if a == 0 else pltpu.roll(slab0, 8 * a, axis=1)
                o_ref[h * CHUNK + 8 * a:h * CHUNK + 8 * a + 8, :] = jnp.where(lane < BAND, slab, NEG)

    tab = pl.pallas_call(
        body, name=name,
        in_specs=[pl.BlockSpec(memory_space=pltpu.VMEM)],
        out_specs=pl.BlockSpec(memory_space=pltpu.VMEM),
        out_shape=_sds((HEADS * CHUNK, BIAS_LANES), F32),
    )(rel_pad)
    tab = tab.reshape(HEADS // 2, 2, CHUNK, BIAS_LANES)
    return jnp.transpose(tab, (0, 3, 1, 2)).reshape(HEADS // 2, BIAS_LANES, 2 * CHUNK)


def _bias_fold(name, dbias_t):
    rows = HEADS * CHUNK
    dbias = jnp.transpose(dbias_t.reshape(HEADS // 2, BIAS_LANES, 2, CHUNK), (0, 2, 3, 1))

    def body(d_ref, o_ref):
        rowid = lax.broadcasted_iota(jnp.int32, (8, BIAS_LANES), 0)
        diags = []
        for h in range(HEADS):
            acc = d_ref[h * CHUNK + 56:h * CHUNK + 64, :]
            for a in range(7):
                slab = d_ref[h * CHUNK + 8 * a:h * CHUNK + 8 * a + 8, :]
                acc = acc + pltpu.roll(slab, 56 - 8 * a, axis=1)
            tot = jnp.where(rowid == 7, acc, 0.0)
            for b in range(7):
                tot = tot + jnp.where(rowid == b, pltpu.roll(acc, 7 - b, axis=1), 0.0)
            diags.append(jnp.sum(tot, axis=0, keepdims=True))
        diag = jnp.concatenate(diags, axis=0)
        o_ref[...] = jnp.dot(diag, _diag_onehot(False), preferred_element_type=F32,
                             precision=lax.Precision.HIGHEST)

    return pl.pallas_call(
        body, name=name,
        in_specs=[pl.BlockSpec(memory_space=pltpu.VMEM)],
        out_specs=pl.BlockSpec(memory_space=pltpu.VMEM),
        out_shape=_sds((HEADS, BIAS_LANES), F32),
    )(dbias.reshape(rows, BIAS_LANES))


def _inv_counts(i):
    trow = lax.broadcasted_iota(jnp.int32, (TOK + HALO, 1), 0) + i * TOK
    return [1.0 / jnp.minimum(trow + 1, w).astype(F32) for w in POOL_WINDOWS]


def _pool_fwd(name, proj, wg, scale):
    t = proj.shape[0]
    hb = TOK // HALO

    def body(u_ref, up_ref, wg_ref, sc_ref, pooled_ref, mixed_ref, b0, b1, b2, b3):
        i = pl.program_id(0)
        halo = up_ref[...].astype(F32)
        b0[0:HALO, :] = jnp.where(i == 0, jnp.zeros_like(halo), halo)
        b0[HALO:, :] = u_ref[...].astype(F32)
        n = TOK + HALO
        b1[8:n, :] = b0[8:n, :] + b0[7:n - 1, :]
        b2[16:n, 128:] = b1[16:n, 128:] + b1[14:n - 2, 128:]
        b3[24:n, 256:] = b2[24:n, 256:] + b2[20:n - 4, 256:]
        wins = [b1[HALO:n, 0:128], b2[HALO:n, 128:256], b3[HALO:n, 256:384],
                b3[HALO:n, 384:512] + b3[HALO - 8:n - 8, 384:512]]
        inv = _inv_counts(i)
        for g in range(4):
            cols = slice(g * POOL_GD, (g + 1) * POOL_GD)
            pooled = (wins[g] * inv[g][0:TOK] - b0[HALO:n, cols]).astype(BF16)
            pooled_ref[:, cols] = pooled
            pre = jnp.dot(pooled, wg_ref[g], preferred_element_type=F32)
            mixed_ref[:, cols] = (pre * sc_ref[:, cols]).astype(BF16)

    buf = pltpu.VMEM((TOK + HALO, POOL_W), F32)
    return pl.pallas_call(
        body, name=name, grid=(t // TOK,),
        in_specs=[_row_spec(POOL_W, 3),
                  pl.BlockSpec((HALO, POOL_W), lambda i: (jnp.maximum(i * hb - 1, 0), 3)),
                  pl.BlockSpec((4, POOL_GD, POOL_GD), lambda i: (0, 0, 0)),
                  _vec_spec(POOL_W)],
        out_specs=[_row_spec(POOL_W), _row_spec(POOL_W)],
        out_shape=[_sds((t, POOL_W), BF16), _sds((t, POOL_W), BF16)],
        scratch_shapes=[buf, buf, buf, buf],
        compiler_params=_cparams("parallel"))(proj, proj, wg, scale)


def _pool_bwd(name, dmixed, pooled, wg, scale):
    t = dmixed.shape[0]
    nt = t // TOK
    hb = TOK // HALO

    def body(dm_ref, dmn_ref, p_ref, wg_ref, sc_ref, du_ref, dwg_ref, dsc_ref, c0, c1, c2, c3):
        i = pl.program_id(0)

        @pl.when(i == 0)
        def _():
            dwg_ref[...] = jnp.zeros_like(dwg_ref)
            dsc_ref[...] = jnp.zeros_like(dsc_ref)

        n = TOK + HALO
        inv = _inv_counts(i)
        dmv = dm_ref[...].astype(F32)
        dmn = dmn_ref[...].astype(F32)
        dmn = jnp.where(i == nt - 1, jnp.zeros_like(dmn), dmn)
        for g in range(4):
            cols = slice(g * POOL_GD, (g + 1) * POOL_GD)
            scg = sc_ref[:, cols]
            pg = p_ref[:, cols]
            dpre = (dmv[:, cols] * scg).astype(BF16)
            dpre_n = (dmn[:, cols] * scg).astype(BF16)
            pre = jnp.dot(pg, wg_ref[g], preferred_element_type=F32)
            dsc_ref[:, cols] += jnp.sum(dmv[:, cols] * pre, axis=0, keepdims=True)
            dwg_ref[g] += lax.dot_general(pg, dpre, (TN, ((), ())), preferred_element_type=F32)
            dpool = lax.dot_general(dpre, wg_ref[g], (NT, ((), ())), preferred_element_type=F32)
            dpool_n = lax.dot_general(dpre_n, wg_ref[g], (NT, ((), ())),
                                      preferred_element_type=F32)
            c0[0:TOK, cols] = dpool
            c0[TOK:n, cols] = dpool_n
            c1[0:TOK, cols] = dpool * inv[g][0:TOK]
            c1[TOK:n, cols] = dpool_n * inv[g][TOK:n]
        c2[0:n - 8, :] = c1[0:n - 8, :] + c1[1:n - 7, :]
        c3[0:n - 16, 128:] = c2[0:n - 16, 128:] + c2[2:n - 14, 128:]
        c1[0:n - 24, 256:] = c3[0:n - 24, 256:] + c3[4:n - 20, 256:]
        wins = [c2[0:TOK, 0:128], c3[0:TOK, 128:256], c1[0:TOK, 256:384],
                c1[0:TOK, 384:512] + c1[8:TOK + 8, 384:512]]
        for g in range(4):
            cols = slice(g * POOL_GD, (g + 1) * POOL_GD)
            du_ref[:, cols] = (wins[g] - c0[0:TOK, cols]).astype(BF16)

    buf = pltpu.VMEM((TOK + HALO, POOL_W), F32)
    return pl.pallas_call(
        body, name=name, grid=(nt,),
        in_specs=[_row_spec(POOL_W),
                  pl.BlockSpec((HALO, POOL_W), lambda i: (jnp.minimum((i + 1) * hb, nt * hb - 1), 0)),
                  _row_spec(POOL_W),
                  pl.BlockSpec((4, POOL_GD, POOL_GD), lambda i: (0, 0, 0)),
                  _vec_spec(POOL_W)],
        out_specs=[_row_spec(POOL_W), pl.BlockSpec((4, POOL_GD, POOL_GD), lambda i: (0, 0, 0)),
                   _vec_spec(POOL_W)],
        out_shape=[_sds((t, POOL_W), BF16), _sds((4, POOL_GD, POOL_GD), F32),
                   _sds((1, POOL_W), F32)],
        scratch_shapes=[buf, buf, buf, buf],
        compiler_params=_cparams("arbitrary"))(dmixed, dmixed, pooled, wg, scale)


GELU_C = math.sqrt(2.0 / math.pi)


GELU_K = 0.044715


def _gelu_parts(x):
    x2 = x * x
    s = 0.5 + 0.5 * jnp.tanh(x * (GELU_C + (GELU_C * GELU_K) * x2))
    return x * s, s, x2


def _gelu(x):
    return _gelu_parts(x)[0]


def _gelu_and_grad(x):
    g, s, x2 = _gelu_parts(x)
    return g, s + g * (1.0 - s) * ((2 * GELU_C) + (6 * GELU_C * GELU_K) * x2)


def _taps(buf, r, rows):
    a = buf[pl.ds(r, rows + 8), :]
    return a[8:], pltpu.roll(a, 1, axis=0)[8:], pltpu.roll(a, 2, axis=0)[8:]


def _conv(taps, w_ref, b_ref):
    return b_ref[...] + w_ref[2:3, :] * taps[0] + w_ref[1:2, :] * taps[1] + w_ref[0:1, :] * taps[2]


def _stage(dst, prev_ref, cur_ref, next_ref, first, last):
    rows = cur_ref.shape[0]
    h = prev_ref[...].astype(F32)
    dst[0:8, :] = jnp.where(first, jnp.zeros_like(h), h)
    dst[8:8 + rows, :] = cur_ref[...].astype(F32)
    if next_ref is not None:
        h = next_ref[...].astype(F32)
        dst[8 + rows:, :] = jnp.where(last, jnp.zeros_like(h), h)


FWD_STRIP = 32
BWD_STRIP = 16


def _ffn_gate_fwd(name, hu, conv_w, conv_b):
    t = hu.shape[0]
    ncol = D_FF // FF_COL
    hb = TOK // 8

    def tile(off):
        return pl.BlockSpec((TOK, FF_COL), lambda i, j: (i, j + off))

    def halo(off):
        return pl.BlockSpec((8, FF_COL), lambda i, j: (jnp.maximum(i * hb - 1, 0), j + off))

    def wspec(off):
        return pl.BlockSpec((3, FF_COL), lambda i, j: (0, j + off))

    def bspec(off):
        return pl.BlockSpec((1, FF_COL), lambda i, j: (0, j + off))

    def body(v_ref, vp_ref, g_ref, gp_ref, wv_ref, wg_ref, bv_ref, bg_ref, a_ref, vb, gb):
        first = pl.program_id(0) == 0
        _stage(vb, vp_ref, v_ref, None, first, None)
        _stage(gb, gp_ref, g_ref, None, first, None)

        def strip(k, carry):
            r = pl.multiple_of(k * FWD_STRIP, FWD_STRIP)
            val = _conv(_taps(vb, r, FWD_STRIP), wv_ref, bv_ref)
            gate = _conv(_taps(gb, r, FWD_STRIP), wg_ref, bg_ref)
            a_ref[pl.ds(r, FWD_STRIP), :] = (_gelu(gate) * val).astype(BF16)
            return carry

        lax.fori_loop(0, TOK // FWD_STRIP, strip, 0)

    buf = pltpu.VMEM((TOK + 8, FF_COL), F32)
    return pl.pallas_call(
        body, name=name, grid=(t // TOK, ncol),
        in_specs=[tile(0), halo(0), tile(ncol), halo(ncol), wspec(0), wspec(ncol),
                  bspec(0), bspec(ncol)],
        out_specs=pl.BlockSpec((TOK, FF_COL), lambda i, j: (i, j)),
        out_shape=_sds((t, D_FF), BF16), scratch_shapes=[buf, buf],
        compiler_params=_cparams("parallel", "parallel"))(
            hu, hu, hu, hu, conv_w, conv_w, conv_b, conv_b)


def _ffn_gate_bwd(name, da, hu, conv_w, conv_b):
    t = hu.shape[0]
    nt = t // TOK
    ncol = D_FF // FF_COL
    hb = TOK // 8
    ext = TOK + 8

    def tile(off):
        return pl.BlockSpec((TOK, FF_COL), lambda j, i: (i, j + off))

    def prev(off):
        return pl.BlockSpec((8, FF_COL), lambda j, i: (jnp.maximum(i * hb - 1, 0), j + off))

    def nxt(off):
        return pl.BlockSpec((8, FF_COL), lambda j, i: (jnp.minimum((i + 1) * hb, nt * hb - 1), j + off))

    def wspec(off):
        return pl.BlockSpec((3, FF_COL), lambda j, i: (0, j + off))

    def bspec(off):
        return pl.BlockSpec((1, FF_COL), lambda j, i: (0, j + off))

    def body(da_ref, dan_ref, v_ref, vp_ref, vn_ref, g_ref, gp_ref, gn_ref,
             wv_ref, wg_ref, bv_ref, bg_ref, dh_ref, dwv_ref, dwg_ref, vb, gb, dab):
        i = pl.program_id(1)
        first, last = i == 0, i == nt - 1

        @pl.when(first)
        def _():
            dwv_ref[...] = jnp.zeros_like(dwv_ref)
            dwg_ref[...] = jnp.zeros_like(dwg_ref)

        _stage(vb, vp_ref, v_ref, vn_ref, first, last)
        _stage(gb, gp_ref, g_ref, gn_ref, first, last)
        dab[0:TOK, :] = da_ref[...].astype(F32)
        h = dan_ref[...].astype(F32)
        dab[TOK:, :] = jnp.where(last, jnp.zeros_like(h), h)

        def grads(r, rows):
            tv, tg = _taps(vb, r, rows), _taps(gb, r, rows)
            gate = _conv(tg, wg_ref, bg_ref)
            dav = dab[pl.ds(r, rows), :]
            g, dg = _gelu_and_grad(gate)
            dval = dav * g
            dgate = dav * _conv(tv, wv_ref, bv_ref) * dg
            return dval, dgate, tv, tg

        def fold(x):
            return x[0:8] + x[8:16]

        def strip(k, carry):
            r = pl.multiple_of(TOK - BWD_STRIP - k * BWD_STRIP, BWD_STRIP)
            dval, dgate, tv, tg = grads(r, BWD_STRIP)
            new = (dval[0:8], dgate[0:8])
            for half, (d, nxt_rows, taps, w_ref, dw_ref) in enumerate((
                    (dval, carry[0], tv, wv_ref, dwv_ref), (dgate, carry[1], tg, wg_ref, dwg_ref))):
                e = jnp.concatenate([d, nxt_rows], axis=0)
                dh = (w_ref[2:3, :] * d
                      + w_ref[1:2, :] * pltpu.roll(e, BWD_STRIP + 7, axis=0)[0:BWD_STRIP]
                      + w_ref[0:1, :] * pltpu.roll(e, BWD_STRIP + 6, axis=0)[0:BWD_STRIP])
                dh_ref[half, pl.ds(r, BWD_STRIP), :] = dh.astype(BF16)
                dw_ref[0:8, :] += fold(d * taps[2])
                dw_ref[8:16, :] += fold(d * taps[1])
                dw_ref[16:24, :] += fold(d * taps[0])
                dw_ref[24:32, :] += fold(d)
            return new

        dval, dgate, _, _ = grads(TOK, 8)
        lax.fori_loop(0, TOK // BWD_STRIP, strip, (dval, dgate))

        @pl.when(last)
        def _():
            for dw_ref in (dwv_ref, dwg_ref):
                for q in range(4):
                    dw_ref[8 * q:8 * q + 1, :] = jnp.sum(dw_ref[8 * q:8 * q + 8, :], axis=0,
                                                         keepdims=True)

    hbuf = pltpu.VMEM((TOK + 16, FF_COL), F32)
    acc = pl.BlockSpec((32, FF_COL), lambda j, i: (0, j))
    dhu, dwv, dwg = pl.pallas_call(
        body, name=name, grid=(ncol, nt),
        in_specs=[tile(0), nxt(0), tile(0), prev(0), nxt(0), tile(ncol), prev(ncol), nxt(ncol),
                  wspec(0), wspec(ncol), bspec(0), bspec(ncol)],
        out_specs=[pl.BlockSpec((2, TOK, FF_COL), lambda j, i: (0, i, j)), acc, acc],
        out_shape=[_sds((2, t, D_FF), BF16), _sds((32, D_FF), F32), _sds((32, D_FF), F32)],
        scratch_shapes=[hbuf, hbuf, pltpu.VMEM((ext, FF_COL), F32)],
        compiler_params=_cparams("parallel", "arbitrary"))(
            da, da, hu, hu, hu, hu, hu, hu, conv_w, conv_w, conv_b, conv_b)
    return dhu, jnp.concatenate([dwv, dwg], axis=1).reshape(4, 8, 2 * D_FF)[:, 0]


def _mesh_pos():
    x, y, c = lax.axis_index("x"), lax.axis_index("y"), lax.axis_index("c")
    return x, y, c, [(1 - x, y), (x, 1 - y), (1 - x, 1 - y)]


def _any_specs(n):
    return [pl.BlockSpec(memory_space=pl.ANY)] * n


def _allgather_weights(shards):
    n = len(shards)

    def body(*refs):
        ins, outs = refs[:n], refs[n:2 * n]
        send_sems, recv_sems = refs[2 * n:]
        x, y, c, chips = _mesh_pos()
        me = 2 * x + y
        started = []
        own = []
        for k in range(n):
            for l in range(2):
                cp = pltpu.make_async_remote_copy(
                    src_ref=ins[k].at[l], dst_ref=outs[k].at[l, me],
                    send_sem=send_sems.at[k, 6 + l], recv_sem=recv_sems.at[k, 6 + l],
                    device_id=(x, y, 1 - c), device_id_type=MESH)
                cp.start()
                own.append(cp)
            for j, (cx, cy) in enumerate(chips):
                cp = pltpu.make_async_remote_copy(
                    src_ref=ins[k].at[c], dst_ref=outs[k].at[c, me],
                    send_sem=send_sems.at[k, j], recv_sem=recv_sems.at[k, j],
                    device_id=(cx, cy, c), device_id_type=MESH)
                cp.start()
                started.append(cp)
        for k in range(n):
            for j, (cx, cy) in enumerate(chips):
                landed = outs[k].at[c, 2 * cx + cy]
                pltpu.make_async_remote_copy(
                    src_ref=ins[k].at[c], dst_ref=landed,
                    send_sem=send_sems.at[k, j], recv_sem=recv_sems.at[k, j],
                    device_id=(cx, cy, c), device_id_type=MESH).wait_recv()
                fw = pltpu.make_async_remote_copy(
                    src_ref=landed, dst_ref=landed,
                    send_sem=send_sems.at[k, 3 + j], recv_sem=recv_sems.at[k, 3 + j],
                    device_id=(x, y, 1 - c), device_id_type=MESH)
                fw.start()
                started.append(fw)
        for k in range(n):
            for j, (cx, cy) in enumerate(chips):
                theirs = outs[k].at[1 - c, 2 * cx + cy]
                pltpu.make_async_remote_copy(
                    src_ref=theirs, dst_ref=theirs,
                    send_sem=send_sems.at[k, 3 + j], recv_sem=recv_sems.at[k, 3 + j],
                    device_id=(x, y, 1 - c), device_id_type=MESH).wait_recv()
        for cp in started:
            cp.wait_send()
        for cp in own:
            cp.wait()

    return pl.pallas_call(
        body, name="allgather_weights",
        in_specs=_any_specs(n), out_specs=_any_specs(n),
        out_shape=[_sds((2, N_CHIPS) + s.shape[1:], s.dtype) for s in shards],
        scratch_shapes=[pltpu.SemaphoreType.DMA((n, 8)), pltpu.SemaphoreType.DMA((n, 8))],
    )(*shards)


def _swap_layers(grads):
    n = len(grads)

    def body(*refs):
        ins, outs = refs[:n], refs[n:2 * n]
        send_sems, recv_sems = refs[2 * n:]
        x, y, c, _ = _mesh_pos()
        cps = []
        for k in range(n):
            cp = pltpu.make_async_remote_copy(
                src_ref=ins[k].at[1 - c], dst_ref=outs[k],
                send_sem=send_sems.at[k], recv_sem=recv_sems.at[k],
                device_id=(x, y, 1 - c), device_id_type=MESH)
            cp.start()
            cps.append(cp)
        for cp in cps:
            cp.wait()

    return pl.pallas_call(
        body, name="swap_layers",
        in_specs=_any_specs(n), out_specs=_any_specs(n),
        out_shape=[_sds(g.shape[1:], g.dtype) for g in grads],
        scratch_shapes=[pltpu.SemaphoreType.DMA((n,)), pltpu.SemaphoreType.DMA((n,))],
    )(*grads)


def _scatter_blocks(sums):
    n = len(sums)

    def body(*refs):
        ins, outs = refs[:n], refs[n:2 * n]
        send_sems, recv_sems = refs[2 * n:]
        x, y, c, chips = _mesh_pos()
        cps = []
        for k in range(n):
            for j, (cx, cy) in enumerate(chips):
                cp = pltpu.make_async_remote_copy(
                    src_ref=ins[k].at[2 * cx + cy], dst_ref=outs[k].at[j],
                    send_sem=send_sems.at[k, j], recv_sem=recv_sems.at[k, j],
                    device_id=(cx, cy, c), device_id_type=MESH)
                cp.start()
                cps.append(cp)
        for cp in cps:
            cp.wait()

    return pl.pallas_call(
        body, name="scatter_blocks",
        in_specs=_any_specs(n), out_specs=_any_specs(n),
        out_shape=[_sds((3,) + s.shape[1:], s.dtype) for s in sums],
        scratch_shapes=[pltpu.SemaphoreType.DMA((n, 3)), pltpu.SemaphoreType.DMA((n, 3))],
    )(*sums)


def _exchange_reduced(reds):
    n = len(reds)

    def body(*refs):
        outs = refs[n:2 * n]
        send_sems, recv_sems = refs[2 * n:]
        x, y, c, _ = _mesh_pos()
        cps = []
        for k in range(n):
            cp = pltpu.make_async_remote_copy(
                src_ref=outs[k].at[c], dst_ref=outs[k].at[c],
                send_sem=send_sems.at[k], recv_sem=recv_sems.at[k],
                device_id=(x, y, 1 - c), device_id_type=MESH)
            cp.start()
            cps.append(cp)
        for k in range(n):
            pltpu.make_async_remote_copy(
                src_ref=outs[k].at[c], dst_ref=outs[k].at[1 - c],
                send_sem=send_sems.at[k], recv_sem=recv_sems.at[k],
                device_id=(x, y, 1 - c), device_id_type=MESH).wait_recv()
        for cp in cps:
            cp.wait_send()

    return pl.pallas_call(
        body, name="exchange_reduced",
        in_specs=_any_specs(n), out_specs=_any_specs(n),
        out_shape=[_sds(r.shape, r.dtype) for r in reds],
        input_output_aliases={k: k for k in range(n)},
        scratch_shapes=[pltpu.SemaphoreType.DMA((n,)), pltpu.SemaphoreType.DMA((n,))],
    )(*reds)


def _allreduce_small(pack):
    n = pack.shape[0]

    def body(x_ref, o_ref, gbuf, send_sems, recv_sems):
        x, y, c, chips = _mesh_pos()
        sibling = (x, y, 1 - c)

        def slot(px, py, pc):
            return gbuf.at[4 * px + 2 * py + pc]

        def copy(k, block, to, src=None):
            return pltpu.make_async_remote_copy(
                src_ref=slot(*block) if src is None else src, dst_ref=slot(*block),
                send_sem=send_sems.at[k], recv_sem=recv_sems.at[k],
                device_id=to, device_id_type=MESH)

        me = (x, y, c)
        first = [copy(0, me, sibling, src=x_ref)]
        first += [copy(1 + j, me, (*chip, c), src=x_ref) for j, chip in enumerate(chips)]
        for cp in first:
            cp.start()
        gbuf[4 * x + 2 * y + c] = x_ref[...]
        passed = [copy(4 + j, (*chip, c), sibling) for j, chip in enumerate(chips)]
        for j, chip in enumerate(chips):
            copy(1 + j, (*chip, c), me).wait_recv()
            passed[j].start()
        copy(0, sibling, me).wait_recv()
        for j, chip in enumerate(chips):
            copy(4 + j, (*chip, 1 - c), me).wait_recv()
        for cp in first + passed:
            cp.wait_send()
        acc = gbuf[0]
        for d in range(1, 8):
            acc = acc + gbuf[d]
        o_ref[...] = acc

    return pl.pallas_call(
        body, name="allreduce_small",
        in_specs=[pl.BlockSpec(memory_space=pltpu.VMEM)],
        out_specs=pl.BlockSpec(memory_space=pltpu.VMEM),
        out_shape=_sds((n, 128), F32),
        scratch_shapes=[pltpu.VMEM((8, n, 128), F32), pltpu.SemaphoreType.DMA((7,)),
                        pltpu.SemaphoreType.DMA((7,))],
        compiler_params=pltpu.CompilerParams(vmem_limit_bytes=VMEM_LIMIT_V7X),
    )(pack)


def _core_index():
    return jnp.reshape(lax.axis_index("c"), (1,)).astype(jnp.int32)


def _chip_index():
    return jnp.reshape(2 * lax.axis_index("x") + lax.axis_index("y"), (1,)).astype(jnp.int32)


def _chip_sum(name, stacked, sib):
    _, nb, r, cdim = stacked.shape

    def body(c_ref, a_ref, b_ref, o_ref):
        o_ref[...] = (a_ref[...].astype(F32) + b_ref[...].astype(F32)).astype(BF16)

    return pl.pallas_call(
        body, name=name,
        grid_spec=pltpu.PrefetchScalarGridSpec(
            num_scalar_prefetch=1, grid=(nb,),
            in_specs=[pl.BlockSpec((None, None, r, cdim), lambda j, cr: (cr[0], j, 0, 0)),
                      pl.BlockSpec((None, r, cdim), lambda j, cr: (j, 0, 0))],
            out_specs=pl.BlockSpec((None, r, cdim), lambda j, cr: (j, 0, 0))),
        out_shape=_sds((nb, r, cdim), BF16),
        compiler_params=_cparams("parallel"))(_core_index(), stacked, sib)


def _final_sum(name, sums, recv):
    _, r, cdim = sums.shape
    tr = r // 2

    def body(m_ref, a_ref, b_ref, o_ref):
        acc = a_ref[...].astype(F32)
        for j in range(3):
            acc = acc + b_ref[j].astype(F32)
        o_ref[...] = acc

    return pl.pallas_call(
        body, name=name,
        grid_spec=pltpu.PrefetchScalarGridSpec(
            num_scalar_prefetch=1, grid=(2,),
            in_specs=[pl.BlockSpec((None, tr, cdim), lambda i, mr: (mr[0], i, 0)),
                      pl.BlockSpec((3, tr, cdim), lambda i, mr: (0, i, 0))],
            out_specs=pl.BlockSpec((None, tr, cdim), lambda i, mr: (mr[1], i, 0))),
        out_shape=_sds((DEPTH, r, cdim), F32),
        compiler_params=_cparams("parallel"))(
            jnp.concatenate([_chip_index(), _core_index()]), sums, recv)


def _adamw(name, w, g, m, v):
    nl, r, cdim = w.shape
    tr = r // 4 if r % 32 == 0 else r
    c1 = 1.0 - ADAM_B1 ** ADAM_STEP
    c2 = 1.0 - ADAM_B2 ** ADAM_STEP

    def body(w_ref, g_ref, m_ref, v_ref, d_ref, nm_ref, nv_ref):
        gv = g_ref[...]
        nm = ADAM_B1 * m_ref[...] + (1.0 - ADAM_B1) * gv
        nv = ADAM_B2 * v_ref[...] + (1.0 - ADAM_B2) * (gv * gv)
        nm_ref[...] = nm
        nv_ref[...] = nv
        d_ref[...] = -ADAM_LR * ((nm / c1) / (jnp.sqrt(nv / c2) + ADAM_EPS) + ADAM_WD * w_ref[...])

    spec = pl.BlockSpec((None, tr, cdim), lambda l, i: (l, i, 0))
    out = _sds(w.shape, F32)
    return pl.pallas_call(
        body, name=name, grid=(nl, r // tr),
        in_specs=[spec] * 4, out_specs=[spec] * 3, out_shape=[out] * 3,
        compiler_params=_cparams("parallel", "parallel"))(w, g, m, v)


def _rows128(a):
    return a.reshape(-1, 128)


def kernel(x, norm_mix_pre, w_in, b_gate, rel_bias, w_attn_out, w_pool_group, pool_scale, w_pool_out, w_o, norm_mix_post, norm_ffn_pre, w_up, conv_w, conv_b, w_down, norm_ffn_post, loss_target, m_norm_mix_pre, m_w_in, m_b_gate, m_rel_bias, m_w_attn_out, m_w_pool_group, m_pool_scale, m_w_pool_out, m_w_o, m_norm_mix_post, m_norm_ffn_pre, m_w_up, m_conv_w, m_conv_b, m_w_down, m_norm_ffn_post, v_norm_mix_pre, v_w_in, v_b_gate, v_rel_bias, v_w_attn_out, v_w_pool_group, v_pool_scale, v_w_pool_out, v_w_o, v_norm_mix_post, v_norm_ffn_pre, v_w_up, v_conv_w, v_conv_b, v_w_down, v_norm_ffn_post):
    t = x.shape[1]
    xs = x.reshape(t, D_MODEL)
    target = loss_target.reshape(t, D_MODEL)

    big = [w_in, w_attn_out, w_pool_out, w_o, w_up, w_down]
    gathered = _allgather_weights([w.astype(BF16) for w in big] + [conv_w])
    win_g, wao_g, wpo_g, wo_g, wup_g, wdn_g, cw_g = gathered
    wo_full = wo_g.reshape(DEPTH, D_MODEL, D_MODEL)
    wdn_full = wdn_g.reshape(DEPTH, D_FF, D_MODEL)
    cw_full = jnp.transpose(cw_g, (0, 2, 1, 3)).reshape(DEPTH, 3, 2 * D_FF)
    wg_bf = w_pool_group.astype(BF16)

    saved = []
    xcur = xs
    for l in range(DEPTH):
        tag = f"l{l}_"
        bias = _bias_table(tag + "bias_table", rel_bias[l])
        h = _norm_fwd(tag + "norm_mix_pre", xcur, norm_mix_pre[l:l + 1])
        proj = _mm_nn_blocked(tag + "proj", h, win_g, l, BF16)
        att = _attn_fwd(tag + "attn_fwd", proj, bias)
        pooled, mixed = _pool_fwd(tag + "pool_fwd", proj, wg_bf[l], pool_scale[l:l + 1])
        ya = _mm_nn_blocked(tag + "attn_out", att, wao_g, l, BF16)
        yb = _mm_nn_blocked(tag + "pool_out", mixed, wpo_g, l, BF16)
        z = _gate_fwd(tag + "gate_fwd", proj, b_gate[l:l + 1], ya, yb)
        mix = _mm_nn(tag + "mix", z, wo_full, l, D_MODEL, F32)
        x1 = _norm_residual_fwd(tag + "norm_mix_post", xcur, mix, norm_mix_post[l:l + 1])
        h2 = _norm_fwd(tag + "norm_ffn_pre", x1, norm_ffn_pre[l:l + 1])
        hu = _mm_nn_blocked(tag + "ffn_up", h2, wup_g, l, BF16)
        a = _ffn_gate_fwd(tag + "ffn_gate_fwd", hu, cw_full[l], conv_b[l:l + 1])
        f = _mm_nn(tag + "ffn_down", a, wdn_full, l, D_FF // 2, F32)
        x2 = _norm_residual_fwd(tag + "norm_ffn_post", x1, f, norm_ffn_post[l:l + 1])
        saved.append(dict(x=xcur, h=h, proj=proj, att=att, pooled=pooled, mixed=mixed, ya=ya,
                          yb=yb, z=z, mix=mix, x1=x1, h2=h2, hu=hu, a=a, f=f, bias=bias))
        xcur = x2

    dy, loss_local = _loss_head(xcur, target)
    loss = lax.psum(loss_local, ("x", "y", "c"))

    dx = dy
    names = ["w_in", "w_attn_out", "w_pool_out", "w_o", "w_up", "w_down"]
    dws = dict.fromkeys(names)
    small_grads = [None] * DEPTH
    for l in reversed(range(DEPTH)):
        tag = f"l{l}_"
        sv = saved[l]
        df, d_nfpost = _norm_post_bwd(tag + "norm_ffn_post_bwd", dx, sv["f"], norm_ffn_post[l:l + 1])
        da = _mm_nt(tag + "ffn_down_dx", df, wdn_full, l, D_FF // 2, BF16)
        dws["w_down"] = _mm_tn(tag + "ffn_down_dw", sv["a"], df, D_FF // 2, l, dws["w_down"])
        dhu, dconv = _ffn_gate_bwd(tag + "ffn_gate_bwd", da, sv["hu"], cw_full[l], conv_b[l:l + 1])
        dh2 = _mm_nt_blocked(tag + "ffn_up_dx", dhu, wup_g, l, F32)
        dws["w_up"] = _mm_tn_blocked(tag + "ffn_up_dw", sv["h2"], dhu, l, dws["w_up"])
        dx1, d_nfpre = _norm_pre_bwd(tag + "norm_ffn_pre_bwd", dh2, sv["x1"], dx, norm_ffn_pre[l:l + 1])
        dmix, d_nmpost = _norm_post_bwd(tag + "norm_mix_post_bwd", dx1, sv["mix"], norm_mix_post[l:l + 1])
        dz = _mm_nt(tag + "mix_dx", dmix, wo_full, l, D_MODEL, BF16)
        dws["w_o"] = _mm_tn(tag + "mix_dw", sv["z"], dmix, D_MODEL, l, dws["w_o"])
        dya, dyb, dgates, d_bgate = _gate_bwd(tag + "gate_bwd", dz, sv["proj"], b_gate[l:l + 1],
                                              sv["ya"], sv["yb"])
        datt = _mm_nt_blocked(tag + "attn_out_dx", dya, wao_g, l, BF16)
        dws["w_attn_out"] = _mm_tn_blocked(tag + "attn_out_dw", sv["att"], dya, l, dws["w_attn_out"])
        dmixed = _mm_nt_blocked(tag + "pool_out_dx", dyb, wpo_g, l, BF16)
        dws["w_pool_out"] = _mm_tn_blocked(tag + "pool_out_dw", sv["mixed"], dyb, l, dws["w_pool_out"])
        du, d_wg, d_pscale = _pool_bwd(tag + "pool_bwd", dmixed, sv["pooled"], wg_bf[l],
                                       pool_scale[l:l + 1])
        dq, dk, dv, dbias = _attn_bwd(tag + "attn_bwd", sv["proj"], datt, sv["bias"])
        d_rel = _bias_fold(tag + "bias_fold", dbias)
        dproj = jnp.concatenate([dq, dk, dv, du, dgates], axis=1)
        dh = _mm_nt_blocked(tag + "proj_dx", dproj, win_g, l, F32)
        dws["w_in"] = _mm_tn_blocked(tag + "proj_dw", sv["h"], dproj, l, dws["w_in"])
        dx, d_nmpre = _norm_pre_bwd(tag + "norm_mix_pre_bwd", dh, sv["x"], dx1, norm_mix_pre[l:l + 1])
        small_grads[l] = [d_nmpre, d_nmpost, d_nfpre, d_nfpost, d_bgate, d_rel, d_wg, d_pscale,
                          dconv[3:4], dconv[0:3]]

    grad_x = dx.reshape(x.shape)

    stacked = [dws[nm] for nm in names]
    stacked[3] = stacked[3].reshape(DEPTH, N_CHIPS, D_MODEL // N_CHIPS, D_MODEL)
    stacked[5] = stacked[5].reshape(DEPTH, N_CHIPS, D_FF // N_CHIPS, D_MODEL)
    sib = _swap_layers(stacked)
    sums = [_chip_sum("chip_sum_" + names[k], stacked[k], sib[k]) for k in range(6)]
    recv = _scatter_blocks(sums)
    reds = [_final_sum("final_sum_" + names[k], sums[k], recv[k]) for k in range(6)]
    g_big = _exchange_reduced(reds)

    pieces = []
    for idx in range(10):
        pieces.append(jnp.stack([small_grads[0][idx], small_grads[1][idx]]))
    pack = jnp.concatenate([_rows128(p) for p in pieces], axis=0)
    red = _allreduce_small(pack)
    shapes = [p.shape for p in pieces]
    outs = []
    row = 0
    for shp in shapes:
        nrow = math.prod(shp) // 128
        outs.append(red[row:row + nrow].reshape(shp))
        row += nrow
    (g_nmpre, g_nmpost, g_nfpre, g_nfpost, g_bgate, g_rel, g_wg, g_pscale, g_cb, g_cw) = outs
    g_nmpre, g_nmpost, g_nfpre, g_nfpost = [a.reshape(DEPTH, D_MODEL)
                                            for a in (g_nmpre, g_nmpost, g_nfpre, g_nfpost)]
    g_bgate = g_bgate.reshape(DEPTH, 2 * D_MODEL)
    g_rel = g_rel[:, :, :N_REL]
    g_pscale = g_pscale.reshape(DEPTH, POOL_W)
    g_cb = g_cb.reshape(DEPTH, 2 * D_FF)
    ncw = conv_w.shape[2]
    chip = 2 * lax.axis_index("x") + lax.axis_index("y")
    g_cw = lax.dynamic_slice_in_dim(g_cw, chip * ncw, ncw, axis=2)

    grads = dict(norm_mix_pre=g_nmpre, w_in=g_big[0], b_gate=g_bgate, rel_bias=g_rel,
                 w_attn_out=g_big[1], w_pool_group=g_wg, pool_scale=g_pscale, w_pool_out=g_big[2],
                 w_o=g_big[3], norm_mix_post=g_nmpost, norm_ffn_pre=g_nfpre, w_up=g_big[4],
                 conv_w=g_cw, conv_b=g_cb, w_down=g_big[5], norm_ffn_post=g_nfpost)
    weights = dict(norm_mix_pre=norm_mix_pre, w_in=w_in, b_gate=b_gate, rel_bias=rel_bias,
                   w_attn_out=w_attn_out, w_pool_group=w_pool_group, pool_scale=pool_scale,
                   w_pool_out=w_pool_out, w_o=w_o, norm_mix_post=norm_mix_post,
                   norm_ffn_pre=norm_ffn_pre, w_up=w_up, conv_w=conv_w, conv_b=conv_b,
                   w_down=w_down, norm_ffn_post=norm_ffn_post)
    moms = dict(norm_mix_pre=(m_norm_mix_pre, v_norm_mix_pre), w_in=(m_w_in, v_w_in),
                b_gate=(m_b_gate, v_b_gate), rel_bias=(m_rel_bias, v_rel_bias),
                w_attn_out=(m_w_attn_out, v_w_attn_out),
                w_pool_group=(m_w_pool_group, v_w_pool_group),
                pool_scale=(m_pool_scale, v_pool_scale), w_pool_out=(m_w_pool_out, v_w_pool_out),
                w_o=(m_w_o, v_w_o), norm_mix_post=(m_norm_mix_post, v_norm_mix_post),
                norm_ffn_pre=(m_norm_ffn_pre, v_norm_ffn_pre), w_up=(m_w_up, v_w_up),
                conv_w=(m_conv_w, v_conv_w), conv_b=(m_conv_b, v_conv_b),
                w_down=(m_w_down, v_w_down), norm_ffn_post=(m_norm_ffn_post, v_norm_ffn_post))
    order = list(weights.keys())

    delta, new_m, new_v = {}, {}, {}
    small_names = [nm for nm in order if nm not in names]
    for nm in names:
        delta[nm], new_m[nm], new_v[nm] = _adamw("adamw_" + nm, weights[nm], grads[nm], *moms[nm])

    def pack_small(get):
        flat = [get(nm).reshape(-1) for nm in small_names]
        total = sum(f.shape[0] for f in flat)
        padded = -(-total // 1024) * 1024
        flat.append(jnp.zeros((padded - total,), F32))
        return jnp.concatenate(flat).reshape(1, padded // 128, 128)

    d_s, m_s, v_s = _adamw(
        "adamw_small", pack_small(lambda nm: weights[nm]), pack_small(lambda nm: grads[nm]),
        pack_small(lambda nm: moms[nm][0]) , pack_small(lambda nm: moms[nm][1]))
    off = 0
    for nm in small_names:
        size = math.prod(weights[nm].shape)
        for dst, src in ((delta, d_s), (new_m, m_s), (new_v, v_s)):
            dst[nm] = src.reshape(-1)[off:off + size].reshape(weights[nm].shape)
        off += size

    return (loss, grad_x, *[grads[nm] for nm in order], *[delta[nm] for nm in order],
            *[new_m[nm] for nm in order], *[new_v[nm] for nm in order])
```

```python
import functools
import math

import jax
import jax.numpy as jnp
from jax import lax
from jax.experimental import pallas as pl
from jax.experimental.pallas import tpu as pltpu

F32 = jnp.float32
BF16 = jnp.bfloat16
MESH = pl.DeviceIdType.MESH

D_MODEL = 1024
DEPTH = 2
CHUNK = 64
BAND_CHUNKS = 9
BAND = BAND_CHUNKS * CHUNK
HEADS = 8
HEAD_DIM = 64
ATTN_W = HEADS * HEAD_DIM
POOL_WINDOWS = (2, 4, 8, 16)
POOL_W = 512
POOL_GD = 128
MAX_REL = 256
N_REL = 2 * MAX_REL + 1
D_FF = 2816
IN_W = 3 * ATTN_W + POOL_W + 2 * D_MODEL
EPS = 1e-6
ATTN_SCALE = HEAD_DIM ** -0.5
BAND_PAD = 640
BIAS_LANES = BAND_PAD
N_CHIPS = 4

ADAM_LR = 0.001
ADAM_B1 = 0.9
ADAM_B2 = 0.999
ADAM_EPS = 1e-08
ADAM_WD = 0.01
ADAM_STEP = 10

VMEM_LIMIT_V7X = 56 * 1024 * 1024
TOK = 512
ATT_BLK = 8 * CHUNK
FF_COL = 256
HALO = 32


def _cparams(*sem):
    return pltpu.CompilerParams(dimension_semantics=sem, vmem_limit_bytes=VMEM_LIMIT_V7X)


def _sds(shape, dtype):
    return jax.ShapeDtypeStruct(shape, dtype)


class _Comm:
    def __init__(self, ins, outs, copies, n_sems, alias=None):
        self.ins, self.outs, self.copies, self.n_sems = list(ins), list(outs), copies, n_sems
        self.alias = dict(alias or {})


def _pcall(body, name, grid, in_specs, out_specs, out_shape, scratch_shapes, compiler_params, args,
           comm=None, aliases=None):
    single = not isinstance(out_shape, (list, tuple))
    out_specs = [out_specs] if single else list(out_specs)
    out_shape = [out_shape] if single else list(out_shape)
    n_in, n_out = len(in_specs), len(out_specs)
    aliases = dict(aliases or {})
    if comm is None:
        res = pl.pallas_call(
            body, name=name, grid=grid, in_specs=list(in_specs), out_specs=out_specs,
            out_shape=out_shape, scratch_shapes=list(scratch_shapes),
            input_output_aliases=aliases, compiler_params=compiler_params)(*args)
        return (res[0] if single else res), None
    ci, co = len(comm.ins), len(comm.outs)

    def hosted(*refs):
        main_in, cin = refs[:n_in], refs[n_in:n_in + ci]
        main_out = refs[n_in + ci:n_in + ci + n_out]
        cout = refs[n_in + ci + n_out:n_in + ci + n_out + co]
        rest = refs[n_in + ci + n_out + co:]
        copies = comm.copies(cin, cout, rest[-2], rest[-1])
        ids = [pl.program_id(a) for a in range(len(grid))]
        first = functools.reduce(jnp.logical_and, [i == 0 for i in ids])
        last = functools.reduce(jnp.logical_and, [i == g - 1 for i, g in zip(ids, grid)])

        @pl.when(first)
        def _():
            for cp in copies:
                cp.start()

        body(*main_in, *main_out, *rest[:-2])

        @pl.when(last)
        def _():
            for cp in copies:
                cp.wait()

    for i, o in comm.alias.items():
        aliases[n_in + i] = n_out + o
    hbm = pl.BlockSpec(memory_space=pl.ANY)
    sems = pltpu.SemaphoreType.DMA((comm.n_sems,))
    res = pl.pallas_call(
        hosted, name=name, grid=grid, in_specs=list(in_specs) + [hbm] * ci,
        out_specs=out_specs + [hbm] * co, out_shape=out_shape + comm.outs,
        scratch_shapes=list(scratch_shapes) + [sems, sems],
        input_output_aliases=aliases, compiler_params=compiler_params)(*args, *comm.ins)
    return (res[0] if single else list(res[:n_out])), list(res[n_out:])


def _comm_call(name, comm):
    ci = len(comm.ins)

    def body(*refs):
        copies = comm.copies(refs[:ci], refs[ci:-2], refs[-2], refs[-1])
        for cp in copies:
            cp.start()
        for cp in copies:
            cp.wait()

    hbm = pl.BlockSpec(memory_space=pl.ANY)
    sems = pltpu.SemaphoreType.DMA((comm.n_sems,))
    return list(pl.pallas_call(
        body, name=name, in_specs=[hbm] * ci, out_specs=[hbm] * len(comm.outs),
        out_shape=comm.outs, scratch_shapes=[sems, sems],
        input_output_aliases=comm.alias)(*comm.ins))


def _matmul(name, a, b, a_spec, b_spec, o_spec, out_shape, grid, contract, nk, acc_shape,
            fill=None, comm=None):
    def body(*refs):
        a_ref, b_ref = refs[0], refs[1]
        o_ref = refs[2 if fill is None else 3]
        scratch = refs[(3 if fill is None else 4):]
        part = lax.dot_general(a_ref[...], b_ref[...], (contract, ((), ())),
                               preferred_element_type=F32)
        if nk == 1:
            o_ref[...] = part.astype(o_ref.dtype)
        else:
            acc_ref = scratch[0]
            k = pl.program_id(2)

            @pl.when(k == 0)
            def _():
                acc_ref[...] = part

            @pl.when(k > 0)
            def _():
                acc_ref[...] += part

            @pl.when(k == nk - 1)
            def _():
                o_ref[...] = acc_ref[...].astype(o_ref.dtype)

    scratch = [] if nk == 1 else [pltpu.VMEM(acc_shape, F32)]
    in_specs, args, aliases = [a_spec, b_spec], [a, b], {}
    if fill is not None:
        in_specs.append(pl.BlockSpec(memory_space=pl.ANY))
        args.append(fill)
        aliases = {2: 0}
    out, moved = _pcall(body, name, grid, in_specs, o_spec, out_shape, scratch,
                        _cparams("parallel", "parallel", "arbitrary"), args, comm, aliases)
    return out if comm is None else (out, moved)


NN = ((1,), (0,))
NT = ((1,), (1,))
TN = ((0,), (0,))


def _tm(t):
    return min(t, 1024)


def _col_block_spec(a, rows, nb, row_col):
    if a.ndim == 2:
        return pl.BlockSpec((rows, nb), row_col)

    def halves(*ids):
        r, c = row_col(*ids)
        return c // 2, r, c % 2

    return pl.BlockSpec((None, rows, nb), halves)


def _mm_nn_blocked(name, a, w, l, out_dtype):
    t, k = a.shape
    nb = w.shape[3]
    tm = _tm(t)
    return _matmul(
        name, a, w,
        pl.BlockSpec((tm, k), lambda i, n, kk: (i, 0)),
        pl.BlockSpec((None, None, k, nb), lambda i, n, kk: (l, n, 0, 0)),
        pl.BlockSpec((tm, nb), lambda i, n, kk: (i, n)),
        _sds((t, N_CHIPS * nb), out_dtype), (t // tm, N_CHIPS, 1), NN, 1, None)


def _mm_nt_blocked(name, a, w, l, out_dtype, comm=None):
    t = a.shape[-2]
    k, nb = w.shape[2], w.shape[3]
    tm = _tm(t)
    return _matmul(
        name, a, w,
        _col_block_spec(a, tm, nb, lambda i, n, kk: (i, kk)),
        pl.BlockSpec((None, None, k, nb), lambda i, n, kk: (l, kk, 0, 0)),
        pl.BlockSpec((tm, k), lambda i, n, kk: (i, 0)),
        _sds((t, k), out_dtype), (t // tm, 1, N_CHIPS), NT, N_CHIPS, (tm, k), comm=comm)


def _mm_tn_blocked(name, a, g, l, fill):
    t, k = a.shape
    nb = g.shape[-1] * (g.ndim - 1) // N_CHIPS
    tt = _tm(t)
    nt = t // tt
    return _matmul(
        name, a, g,
        pl.BlockSpec((tt, k), lambda n, j, kk: (kk, 0)),
        _col_block_spec(g, tt, nb, lambda n, j, kk: (kk, n)),
        pl.BlockSpec((None, None, k, nb), lambda n, j, kk: (l, n, 0, 0)),
        _sds((DEPTH, N_CHIPS, k, nb), BF16), (N_CHIPS, 1, nt), TN, nt, (k, nb), fill)


def _mm_nn(name, a, w, l, tk, out_dtype):
    t, k = a.shape
    n = w.shape[2]
    tm = _tm(t)
    nk = k // tk
    return _matmul(
        name, a, w,
        pl.BlockSpec((tm, tk), lambda i, j, kk: (i, kk)),
        pl.BlockSpec((None, tk, n), lambda i, j, kk: (l, kk, 0)),
        pl.BlockSpec((tm, n), lambda i, j, kk: (i, 0)),
        _sds((t, n), out_dtype), (t // tm, 1, nk), NN, nk, (tm, n))


def _mm_nt(name, a, w, l, tn, out_dtype, comm=None):
    t, n = a.shape
    k = w.shape[1]
    tm = _tm(t)
    return _matmul(
        name, a, w,
        pl.BlockSpec((tm, n), lambda i, j, kk: (i, 0)),
        pl.BlockSpec((None, tn, n), lambda i, j, kk: (l, j, 0)),
        pl.BlockSpec((tm, tn), lambda i, j, kk: (i, j)),
        _sds((t, k), out_dtype), (t // tm, k // tn, 1), NT, 1, None, comm=comm)


def _mm_tn(name, a, g, tko, l, fill):
    t, k = a.shape
    n = g.shape[1]
    tt = _tm(t)
    nt = t // tt
    return _matmul(
        name, a, g,
        pl.BlockSpec((tt, tko), lambda i, j, kk: (kk, i)),
        pl.BlockSpec((tt, n), lambda i, j, kk: (kk, 0)),
        pl.BlockSpec((None, tko, n), lambda i, j, kk: (l, i, 0)),
        _sds((DEPTH, k, n), BF16), (k // tko, 1, nt), TN, nt, (tko, n), fill)


def _row_spec(width, col=0):
    return pl.BlockSpec((TOK, width), lambda i: (i, col))


def _vec_spec(width):
    return pl.BlockSpec((1, width), lambda i: (0, 0))


def _rms(x):
    return lax.rsqrt(jnp.mean(x * x, axis=-1, keepdims=True) + EPS)


def _norm_fwd(name, x, g):
    t = x.shape[0]

    def body(x_ref, g_ref, h_ref):
        xv = x_ref[...]
        h_ref[...] = (xv * _rms(xv) * g_ref[...]).astype(BF16)

    return pl.pallas_call(
        body, name=name, grid=(t // TOK,), in_specs=[_row_spec(D_MODEL), _vec_spec(D_MODEL)],
        out_specs=_row_spec(D_MODEL), out_shape=_sds((t, D_MODEL), BF16),
        compiler_params=_cparams("parallel"))(x, g)


def _norm_residual_fwd(name, xres, m, g, comm=None):
    t = xres.shape[0]

    def body(x_ref, m_ref, g_ref, o_ref):
        mv = m_ref[...]
        o_ref[...] = x_ref[...] + mv * _rms(mv) * g_ref[...]

    out, moved = _pcall(
        body, name, (t // TOK,),
        [_row_spec(D_MODEL), _row_spec(D_MODEL), _vec_spec(D_MODEL)],
        _row_spec(D_MODEL), _sds((t, D_MODEL), F32), [], _cparams("arbitrary"),
        (xres, m, g), comm)
    return out if comm is None else (out, moved)


def _norm_post_bwd(name, dxo, m, g):
    t = dxo.shape[0]

    def body(d_ref, m_ref, g_ref, dm_ref, dg_ref):
        mv = m_ref[...]
        dv = d_ref[...]
        r = _rms(mv)
        n = mv * r
        dn = dv * g_ref[...]
        dm_ref[...] = (r * (dn - n * jnp.mean(dn * n, axis=-1, keepdims=True))).astype(BF16)
        part = jnp.sum(dv * n, axis=0, keepdims=True)

        @pl.when(pl.program_id(0) == 0)
        def _():
            dg_ref[...] = part

        @pl.when(pl.program_id(0) > 0)
        def _():
            dg_ref[...] += part

    return pl.pallas_call(
        body, name=name, grid=(t // TOK,),
        in_specs=[_row_spec(D_MODEL), _row_spec(D_MODEL), _vec_spec(D_MODEL)],
        out_specs=[_row_spec(D_MODEL), _vec_spec(D_MODEL)],
        out_shape=[_sds((t, D_MODEL), BF16), _sds((1, D_MODEL), F32)],
        compiler_params=_cparams("arbitrary"))(dxo, m, g)


def _norm_pre_bwd(name, dh, xin, dxo, g, comm=None):
    t = dh.shape[0]

    def body(dh_ref, x_ref, d_ref, g_ref, dx_ref, dg_ref):
        xv = x_ref[...]
        dhv = dh_ref[...]
        r = _rms(xv)
        n = xv * r
        dn = dhv * g_ref[...]
        dx_ref[...] = d_ref[...] + r * (dn - n * jnp.mean(dn * n, axis=-1, keepdims=True))
        part = jnp.sum(dhv * n, axis=0, keepdims=True)

        @pl.when(pl.program_id(0) == 0)
        def _():
            dg_ref[...] = part

        @pl.when(pl.program_id(0) > 0)
        def _():
            dg_ref[...] += part

    out, moved = _pcall(
        body, name, (t // TOK,),
        [_row_spec(D_MODEL), _row_spec(D_MODEL), _row_spec(D_MODEL), _vec_spec(D_MODEL)],
        [_row_spec(D_MODEL), _vec_spec(D_MODEL)],
        [_sds((t, D_MODEL), F32), _sds((1, D_MODEL), F32)], [], _cparams("arbitrary"),
        (dh, xin, dxo, g), comm)
    return out if comm is None else (*out, moved)


def _loss_head(y, target):
    t = y.shape[0]

    def body(y_ref, t_ref, dy_ref, l_ref):
        e = y_ref[...] - t_ref[...]
        dy_ref[...] = e * (1.0 / D_MODEL)
        part = jnp.sum(jnp.sum(e * e, axis=0, keepdims=True), axis=1, keepdims=True)

        @pl.when(pl.program_id(0) == 0)
        def _():
            l_ref[...] = part

        @pl.when(pl.program_id(0) > 0)
        def _():
            l_ref[...] += part

    dy, sq = pl.pallas_call(
        body, name="loss_head", grid=(t // TOK,),
        in_specs=[_row_spec(D_MODEL), _row_spec(D_MODEL)],
        out_specs=[_row_spec(D_MODEL), pl.BlockSpec((1, 1), lambda i: (0, 0))],
        out_shape=[_sds((t, D_MODEL), F32), _sds((1, 1), F32)],
        compiler_params=_cparams("arbitrary"))(y, target)
    return dy, sq[0, 0] * (0.5 / D_MODEL)


def _gate_fwd(name, proj, b_gate, ya, yb):
    t = proj.shape[0]

    def body(ga_ref, gb_ref, b_ref, ya_ref, yb_ref, z_ref):
        sa = jax.nn.sigmoid(ga_ref[...].astype(F32) + b_ref[:, :D_MODEL])
        sb = jax.nn.sigmoid(gb_ref[...].astype(F32) + b_ref[:, D_MODEL:])
        z_ref[...] = (sa * ya_ref[...].astype(F32) + sb * yb_ref[...].astype(F32)).astype(BF16)

    return pl.pallas_call(
        body, name=name, grid=(t // TOK,),
        in_specs=[_row_spec(D_MODEL, 2), _row_spec(D_MODEL, 3), _vec_spec(2 * D_MODEL),
                  _row_spec(D_MODEL), _row_spec(D_MODEL)],
        out_specs=_row_spec(D_MODEL), out_shape=_sds((t, D_MODEL), BF16),
        compiler_params=_cparams("parallel"))(proj, proj, b_gate, ya, yb)


def _gate_bwd(name, dz, proj, b_gate, ya, yb):
    t = proj.shape[0]

    def body(dz_ref, ga_ref, gb_ref, b_ref, ya_ref, yb_ref, dya_ref, dyb_ref, dg_ref, db_ref):
        dzv = dz_ref[...].astype(F32)
        sa = jax.nn.sigmoid(ga_ref[...].astype(F32) + b_ref[:, :D_MODEL])
        sb = jax.nn.sigmoid(gb_ref[...].astype(F32) + b_ref[:, D_MODEL:])
        dya_ref[...] = (dzv * sa).astype(BF16)
        dyb_ref[...] = (dzv * sb).astype(BF16)
        dga = dzv * ya_ref[...].astype(F32) * sa * (1.0 - sa)
        dgb = dzv * yb_ref[...].astype(F32) * sb * (1.0 - sb)
        dg_ref[:, :D_MODEL] = dga.astype(BF16)
        dg_ref[:, D_MODEL:] = dgb.astype(BF16)
        pa = jnp.sum(dga, axis=0, keepdims=True)
        pb = jnp.sum(dgb, axis=0, keepdims=True)

        @pl.when(pl.program_id(0) == 0)
        def _():
            db_ref[:, :D_MODEL] = pa
            db_ref[:, D_MODEL:] = pb

        @pl.when(pl.program_id(0) > 0)
        def _():
            db_ref[:, :D_MODEL] += pa
            db_ref[:, D_MODEL:] += pb

    return pl.pallas_call(
        body, name=name, grid=(t // TOK,),
        in_specs=[_row_spec(D_MODEL), _row_spec(D_MODEL, 2), _row_spec(D_MODEL, 3),
                  _vec_spec(2 * D_MODEL), _row_spec(D_MODEL), _row_spec(D_MODEL)],
        out_specs=[_row_spec(D_MODEL), _row_spec(D_MODEL), _row_spec(2 * D_MODEL),
                   _vec_spec(2 * D_MODEL)],
        out_shape=[_sds((t, D_MODEL), BF16), _sds((t, D_MODEL), BF16),
                   _sds((t, 2 * D_MODEL), BF16), _sds((1, 2 * D_MODEL), F32)],
        compiler_params=_cparams("arbitrary"))(dz, proj, proj, b_gate, ya, yb)


def _head_masks():
    lane = lax.broadcasted_iota(jnp.int32, (1, 2 * HEAD_DIM), 1)
    return lane < HEAD_DIM


BAND_ROWS = 2 * ATT_BLK + CHUNK


def _fill_band(band, prev_ref, cur_ref):
    band[0:ATT_BLK, :] = prev_ref[...]
    band[ATT_BLK:2 * ATT_BLK, :] = cur_ref[...]
    band[2 * ATT_BLK:, :] = jnp.zeros((CHUNK, ATTN_W), BF16)


def _pair_rows(x2, low):
    zero = jnp.zeros_like(x2)
    return jnp.concatenate([jnp.where(low, x2, zero), jnp.where(low, zero, x2)], axis=0)


def _pair_diag(o2, low):
    return jnp.where(low, o2[0:CHUNK, :], o2[CHUNK:, :])


N_PAIRS = HEADS // 2
SM_STRIP = 32
N_STRIPS = BAND_PAD // SM_STRIP
NEG = -1e30


def _fold8(x, op):
    return op(op(x[0:8], x[8:16]), op(x[16:24], x[24:32]))


def _strip(k):
    return pl.ds(pl.multiple_of(k * SM_STRIP, SM_STRIP), SM_STRIP)


def _band_probs(k2, qcat, bias_t, first_key):
    kpos = lax.broadcasted_iota(jnp.int32, (BAND_PAD, 1), 0)
    st = lax.dot_general(k2, qcat, (NT, ((), ())), preferred_element_type=F32)
    st = jnp.where(kpos + first_key >= 0, st + bias_t, NEG)
    e = jnp.exp(st - jnp.max(st, axis=0, keepdims=True))
    return e * (1.0 / jnp.sum(e, axis=0, keepdims=True))


def _band_softmax_stats(st_ref, b_ref, first_key, dp_ref):
    rowi = lax.broadcasted_iota(jnp.int32, (SM_STRIP, 128), 0)

    def scores(k, mx):
        rows = _strip(k)
        live = (rowi + (k * SM_STRIP + first_key)) >= 0
        out = []
        for hp in range(N_PAIRS):
            x = jnp.where(live, st_ref[hp, rows, :] + b_ref[hp, rows, :], NEG)
            st_ref[hp, rows, :] = x
            out.append(jnp.maximum(mx[hp], _fold8(x, jnp.maximum)))
        return tuple(out)

    mx = lax.fori_loop(0, N_STRIPS, scores, (jnp.full((8, 128), NEG, F32),) * N_PAIRS, unroll=2)
    top = [jnp.max(m, axis=0, keepdims=True) for m in mx]

    def sums(k, acc):
        rows = _strip(k)
        ls, eds = [], []
        for hp in range(N_PAIRS):
            e = jnp.exp(st_ref[hp, rows, :] - top[hp])
            ls.append(acc[hp] + _fold8(e, jnp.add))
            eds.append(acc[N_PAIRS + hp] + _fold8(e * dp_ref[hp, rows, :], jnp.add))
        return tuple(ls + eds)

    acc = lax.fori_loop(0, N_STRIPS, sums, (jnp.zeros((8, 128), F32),) * (2 * N_PAIRS), unroll=2)
    inv = [1.0 / jnp.sum(a, axis=0, keepdims=True) for a in acc[:N_PAIRS]]
    delta = [jnp.sum(a, axis=0, keepdims=True) * i for a, i in zip(acc[N_PAIRS:], inv)]
    return top, inv, delta


def _attn_specs(nblk):
    cur = lambda col: pl.BlockSpec((ATT_BLK, ATTN_W), lambda s: (jnp.minimum(s, nblk - 1), col))
    prev = lambda col: pl.BlockSpec(
        (ATT_BLK, ATTN_W), lambda s: (jnp.maximum(jnp.minimum(s, nblk - 1) - 1, 0), col))
    return cur, prev


def _attn_fwd(name, proj, bias):
    t = proj.shape[0]
    nblk = t // ATT_BLK
    cur, prev = _attn_specs(nblk)

    def body(q_ref, kp_ref, kc_ref, vp_ref, vc_ref, b_ref, o_ref, kband, vband):
        s = pl.program_id(0)
        _fill_band(kband, kp_ref, kc_ref)
        _fill_band(vband, vp_ref, vc_ref)
        low = _head_masks()

        def chunk(ci, carry):
            r0 = pl.multiple_of(ci * CHUNK, CHUNK)
            for hp in range(N_PAIRS):
                cols = slice(hp * 128, (hp + 1) * 128)
                qcat = _pair_rows(q_ref[pl.ds(r0, CHUNK), cols] * ATTN_SCALE, low)
                p = _band_probs(kband[pl.ds(r0, BAND_PAD), cols], qcat, b_ref[hp],
                                (s * 8 - 8 + ci) * CHUNK)
                o2 = lax.dot_general(p.astype(BF16), vband[pl.ds(r0, BAND_PAD), cols],
                                     (TN, ((), ())), preferred_element_type=F32)
                o_ref[pl.ds(r0, CHUNK), cols] = _pair_diag(o2, low).astype(BF16)
            return carry

        lax.fori_loop(0, 8, chunk, 0)

    return pl.pallas_call(
        body, name=name, grid=(nblk,),
        in_specs=[cur(0), prev(1), cur(1), prev(2), cur(2),
                  pl.BlockSpec((N_PAIRS, BAND_PAD, 128), lambda s: (0, 0, 0))],
        out_specs=pl.BlockSpec((ATT_BLK, ATTN_W), lambda s: (s, 0)),
        out_shape=_sds((t, ATTN_W), BF16),
        scratch_shapes=[pltpu.VMEM((BAND_ROWS, ATTN_W), BF16),
                        pltpu.VMEM((BAND_ROWS, ATTN_W), BF16)],
        compiler_params=_cparams("arbitrary"))(proj, proj, proj, proj, proj, bias)


def _attn_bwd(name, proj, datt, bias, comm=None):
    t = proj.shape[0]
    nblk = t // ATT_BLK
    cur, prev = _attn_specs(nblk)
    late = pl.BlockSpec((ATT_BLK, ATTN_W), lambda s: (jnp.maximum(s - 1, 0), 0))

    def body(q_ref, kp_ref, kc_ref, vp_ref, vc_ref, do_ref, b_ref,
             dq_ref, dk_ref, dv_ref, db_ref, kband, vband, dkacc, dvacc,
             st_ref, dp_ref, pb_ref, dsb_ref, qc_ref, dc_ref):
        s = pl.program_id(0)

        @pl.when(s == 0)
        def _():
            dkacc[...] = jnp.zeros_like(dkacc)
            dvacc[...] = jnp.zeros_like(dvacc)
            db_ref[...] = jnp.zeros_like(db_ref)

        @pl.when(s < nblk)
        def _():
            _fill_band(kband, kp_ref, kc_ref)
            _fill_band(vband, vp_ref, vc_ref)
            low = _head_masks()

            def chunk(ci, carry):
                r0 = pl.multiple_of(ci * CHUNK, CHUNK)
                for hp in range(N_PAIRS):
                    cols = slice(hp * 128, (hp + 1) * 128)
                    qc_ref[hp] = _pair_rows(q_ref[pl.ds(r0, CHUNK), cols] * ATTN_SCALE, low)
                    dc_ref[hp] = _pair_rows(do_ref[pl.ds(r0, CHUNK), cols], low)
                    st_ref[hp] = lax.dot_general(kband[pl.ds(r0, BAND_PAD), cols], qc_ref[hp],
                                                 (NT, ((), ())), preferred_element_type=F32)
                    dp_ref[hp] = lax.dot_general(vband[pl.ds(r0, BAND_PAD), cols], dc_ref[hp],
                                                 (NT, ((), ())), preferred_element_type=F32)
                top, inv, delta = _band_softmax_stats(st_ref, b_ref, (s * 8 - 8 + ci) * CHUNK,
                                                      dp_ref)

                def grads(k, c):
                    rows = _strip(k)
                    for hp in range(N_PAIRS):
                        p = jnp.exp(st_ref[hp, rows, :] - top[hp]) * inv[hp]
                        ds = p * (dp_ref[hp, rows, :] - delta[hp])
                        db_ref[hp, rows, :] += ds
                        dsb_ref[hp, rows, :] = ds.astype(BF16)
                        pb_ref[hp, rows, :] = p.astype(BF16)
                    return c

                lax.fori_loop(0, N_STRIPS, grads, 0, unroll=2)
                for hp in range(N_PAIRS):
                    cols = slice(hp * 128, (hp + 1) * 128)
                    dq2 = lax.dot_general(dsb_ref[hp], kband[pl.ds(r0, BAND_PAD), cols],
                                          (TN, ((), ())), preferred_element_type=F32)
                    dq_ref[pl.ds(r0, CHUNK), cols] = (_pair_diag(dq2, low) * ATTN_SCALE).astype(BF16)
                    dkacc[pl.ds(r0, BAND_PAD), cols] += jnp.dot(dsb_ref[hp], qc_ref[hp],
                                                               preferred_element_type=F32)
                    dvacc[pl.ds(r0, BAND_PAD), cols] += jnp.dot(pb_ref[hp], dc_ref[hp],
                                                               preferred_element_type=F32)
                return carry

            lax.fori_loop(0, 8, chunk, 0)

        dk_ref[...] = dkacc[0:ATT_BLK, :].astype(BF16)
        dv_ref[...] = dvacc[0:ATT_BLK, :].astype(BF16)
        dkacc[0:ATT_BLK, :] = dkacc[ATT_BLK:2 * ATT_BLK, :]
        dvacc[0:ATT_BLK, :] = dvacc[ATT_BLK:2 * ATT_BLK, :]
        dkacc[ATT_BLK:, :] = jnp.zeros((ATT_BLK + CHUNK, ATTN_W), F32)
        dvacc[ATT_BLK:, :] = jnp.zeros((ATT_BLK + CHUNK, ATTN_W), F32)

    blk = _sds((t, ATTN_W), BF16)
    outs, moved = _pcall(
        body, name, (nblk + 1,),
        [cur(0), prev(1), cur(1), prev(2), cur(2),
         pl.BlockSpec((ATT_BLK, ATTN_W), lambda s: (jnp.minimum(s, nblk - 1), 0)),
         pl.BlockSpec((HEADS // 2, BAND_PAD, 128), lambda s: (0, 0, 0))],
        [pl.BlockSpec((ATT_BLK, ATTN_W), lambda s: (jnp.minimum(s, nblk - 1), 0)), late, late,
         pl.BlockSpec((HEADS // 2, BAND_PAD, 128), lambda s: (0, 0, 0))],
        [blk, blk, blk, _sds((HEADS // 2, BAND_PAD, 128), F32)],
        [pltpu.VMEM((BAND_ROWS, ATTN_W), BF16), pltpu.VMEM((BAND_ROWS, ATTN_W), BF16),
         pltpu.VMEM((BAND_ROWS, ATTN_W), F32), pltpu.VMEM((BAND_ROWS, ATTN_W), F32),
         pltpu.VMEM((N_PAIRS, BAND_PAD, 128), F32), pltpu.VMEM((N_PAIRS, BAND_PAD, 128), F32),
         pltpu.VMEM((N_PAIRS, BAND_PAD, 128), BF16), pltpu.VMEM((N_PAIRS, BAND_PAD, 128), BF16),
         pltpu.VMEM((N_PAIRS, 2 * CHUNK, 128), BF16), pltpu.VMEM((N_PAIRS, 2 * CHUNK, 128), BF16)],
        _cparams("arbitrary"), (proj, proj, proj, proj, proj, datt, bias), comm)
    return outs if comm is None else (*outs, moved)


def _diag_onehot(rel_rows):
    d0 = lax.broadcasted_iota(jnp.int32, (BIAS_LANES, BIAS_LANES), 0)
    d1 = lax.broadcasted_iota(jnp.int32, (BIAS_LANES, BIAS_LANES), 1)
    m, n = (d0, d1) if rel_rows else (d1, d0)
    hit = (m == jnp.minimum(BAND - 1 + MAX_REL - n, 2 * MAX_REL)) & (n < BAND + CHUNK - 1)
    return jnp.where(hit, 1.0, 0.0).astype(F32)


def _bias_table(name, rel_bias_l):
    rel_pad = jnp.pad(rel_bias_l, ((0, 0), (0, BIAS_LANES - N_REL)))

    def body(r_ref, o_ref):
        diag = jnp.dot(r_ref[...], _diag_onehot(True), preferred_element_type=F32,
                       precision=lax.Precision.HIGHEST)
        rowid = lax.broadcasted_iota(jnp.int32, (8, BIAS_LANES), 0)
        lane = lax.broadcasted_iota(jnp.int32, (8, BIAS_LANES), 1)
        for h in range(HEADS):
            d8 = jnp.broadcast_to(diag[h:h + 1, :], (8, BIAS_LANES))
            slab0 = pltpu.roll(d8, BIAS_LANES - CHUNK + 1, axis=1)
            for b in range(1, 8):
                slab0 = jnp.where(rowid == b, pltpu.roll(d8, BIAS_LANES - CHUNK + 1 + b, axis=1),
                                  slab0)
            for a in range(8):
                slab = slab0 if a == 0 else pltpu.roll(slab0, 8 * a, axis=1)
                o_ref[h * CHUNK + 8 * a:h * CHUNK + 8 * a + 8, :] = jnp.where(lane < BAND, slab, NEG)

    tab = pl.pallas_call(
        body, name=name,
        in_specs=[pl.BlockSpec(memory_space=pltpu.VMEM)],
        out_specs=pl.BlockSpec(memory_space=pltpu.VMEM),
        out_shape=_sds((HEADS * CHUNK, BIAS_LANES), F32),
    )(rel_pad)
    tab = tab.reshape(HEADS // 2, 2, CHUNK, BIAS_LANES)
    return jnp.transpose(tab, (0, 3, 1, 2)).reshape(HEADS // 2, BIAS_LANES, 2 * CHUNK)


def _bias_fold(name, dbias_t):
    rows = HEADS * CHUNK
    dbias = jnp.transpose(dbias_t.reshape(HEADS // 2, BIAS_LANES, 2, CHUNK), (0, 2, 3, 1))

    def body(d_ref, o_ref):
        rowid = lax.broadcasted_iota(jnp.int32, (8, BIAS_LANES), 0)
        diags = []
        for h in range(HEADS):
            acc = d_ref[h * CHUNK + 56:h * CHUNK + 64, :]
            for a in range(7):
                slab = d_ref[h * CHUNK + 8 * a:h * CHUNK + 8 * a + 8, :]
                acc = acc + pltpu.roll(slab, 56 - 8 * a, axis=1)
            tot = jnp.where(rowid == 7, acc, 0.0)
            for b in range(7):
                tot = tot + jnp.where(rowid == b, pltpu.roll(acc, 7 - b, axis=1), 0.0)
            diags.append(jnp.sum(tot, axis=0, keepdims=True))
        diag = jnp.concatenate(diags, axis=0)
        o_ref[...] = jnp.dot(diag, _diag_onehot(False), preferred_element_type=F32,
                             precision=lax.Precision.HIGHEST)

    return pl.pallas_call(
        body, name=name,
        in_specs=[pl.BlockSpec(memory_space=pltpu.VMEM)],
        out_specs=pl.BlockSpec(memory_space=pltpu.VMEM),
        out_shape=_sds((HEADS, BIAS_LANES), F32),
    )(dbias.reshape(rows, BIAS_LANES))


def _inv_counts(i):
    trow = lax.broadcasted_iota(jnp.int32, (TOK + HALO, 1), 0) + i * TOK
    return [1.0 / jnp.minimum(trow + 1, w).astype(F32) for w in POOL_WINDOWS]


def _pool_fwd(name, proj, wg, scale):
    t = proj.shape[0]
    hb = TOK // HALO

    def body(u_ref, up_ref, wg_ref, sc_ref, pooled_ref, mixed_ref, b0, b1, b2, b3):
        i = pl.program_id(0)
        halo = up_ref[...].astype(F32)
        b0[0:HALO, :] = jnp.where(i == 0, jnp.zeros_like(halo), halo)
        b0[HALO:, :] = u_ref[...].astype(F32)
        n = TOK + HALO
        b1[8:n, :] = b0[8:n, :] + b0[7:n - 1, :]
        b2[16:n, 128:] = b1[16:n, 128:] + b1[14:n - 2, 128:]
        b3[24:n, 256:] = b2[24:n, 256:] + b2[20:n - 4, 256:]
        wins = [b1[HALO:n, 0:128], b2[HALO:n, 128:256], b3[HALO:n, 256:384],
                b3[HALO:n, 384:512] + b3[HALO - 8:n - 8, 384:512]]
        inv = _inv_counts(i)
        for g in range(4):
            cols = slice(g * POOL_GD, (g + 1) * POOL_GD)
            pooled = (wins[g] * inv[g][0:TOK] - b0[HALO:n, cols]).astype(BF16)
            pooled_ref[:, cols] = pooled
            pre = jnp.dot(pooled, wg_ref[g], preferred_element_type=F32)
            mixed_ref[:, cols] = (pre * sc_ref[:, cols]).astype(BF16)

    buf = pltpu.VMEM((TOK + HALO, POOL_W), F32)
    return pl.pallas_call(
        body, name=name, grid=(t // TOK,),
        in_specs=[_row_spec(POOL_W, 3),
                  pl.BlockSpec((HALO, POOL_W), lambda i: (jnp.maximum(i * hb - 1, 0), 3)),
                  pl.BlockSpec((4, POOL_GD, POOL_GD), lambda i: (0, 0, 0)),
                  _vec_spec(POOL_W)],
        out_specs=[_row_spec(POOL_W), _row_spec(POOL_W)],
        out_shape=[_sds((t, POOL_W), BF16), _sds((t, POOL_W), BF16)],
        scratch_shapes=[buf, buf, buf, buf],
        compiler_params=_cparams("parallel"))(proj, proj, wg, scale)


def _pool_bwd(name, dmixed, pooled, wg, scale):
    t = dmixed.shape[0]
    nt = t // TOK
    hb = TOK // HALO

    def body(dm_ref, dmn_ref, p_ref, wg_ref, sc_ref, du_ref, dwg_ref, dsc_ref, c0, c1, c2, c3):
        i = pl.program_id(0)

        @pl.when(i == 0)
        def _():
            dwg_ref[...] = jnp.zeros_like(dwg_ref)
            dsc_ref[...] = jnp.zeros_like(dsc_ref)

        n = TOK + HALO
        inv = _inv_counts(i)
        dmv = dm_ref[...].astype(F32)
        dmn = dmn_ref[...].astype(F32)
        dmn = jnp.where(i == nt - 1, jnp.zeros_like(dmn), dmn)
        for g in range(4):
            cols = slice(g * POOL_GD, (g + 1) * POOL_GD)
            scg = sc_ref[:, cols]
            pg = p_ref[:, cols]
            dpre = (dmv[:, cols] * scg).astype(BF16)
            dpre_n = (dmn[:, cols] * scg).astype(BF16)
            pre = jnp.dot(pg, wg_ref[g], preferred_element_type=F32)
            dsc_ref[:, cols] += jnp.sum(dmv[:, cols] * pre, axis=0, keepdims=True)
            dwg_ref[g] += lax.dot_general(pg, dpre, (TN, ((), ())), preferred_element_type=F32)
            dpool = lax.dot_general(dpre, wg_ref[g], (NT, ((), ())), preferred_element_type=F32)
            dpool_n = lax.dot_general(dpre_n, wg_ref[g], (NT, ((), ())),
                                      preferred_element_type=F32)
            c0[0:TOK, cols] = dpool
            c0[TOK:n, cols] = dpool_n
            c1[0:TOK, cols] = dpool * inv[g][0:TOK]
            c1[TOK:n, cols] = dpool_n * inv[g][TOK:n]
        c2[0:n - 8, :] = c1[0:n - 8, :] + c1[1:n - 7, :]
        c3[0:n - 16, 128:] = c2[0:n - 16, 128:] + c2[2:n - 14, 128:]
        c1[0:n - 24, 256:] = c3[0:n - 24, 256:] + c3[4:n - 20, 256:]
        wins = [c2[0:TOK, 0:128], c3[0:TOK, 128:256], c1[0:TOK, 256:384],
                c1[0:TOK, 384:512] + c1[8:TOK + 8, 384:512]]
        for g in range(4):
            cols = slice(g * POOL_GD, (g + 1) * POOL_GD)
            du_ref[:, cols] = (wins[g] - c0[0:TOK, cols]).astype(BF16)

    buf = pltpu.VMEM((TOK + HALO, POOL_W), F32)
    return pl.pallas_call(
        body, name=name, grid=(nt,),
        in_specs=[_row_spec(POOL_W),
                  pl.BlockSpec((HALO, POOL_W), lambda i: (jnp.minimum((i + 1) * hb, nt * hb - 1), 0)),
                  _row_spec(POOL_W),
                  pl.BlockSpec((4, POOL_GD, POOL_GD), lambda i: (0, 0, 0)),
                  _vec_spec(POOL_W)],
        out_specs=[_row_spec(POOL_W), pl.BlockSpec((4, POOL_GD, POOL_GD), lambda i: (0, 0, 0)),
                   _vec_spec(POOL_W)],
        out_shape=[_sds((t, POOL_W), BF16), _sds((4, POOL_GD, POOL_GD), F32),
                   _sds((1, POOL_W), F32)],
        scratch_shapes=[buf, buf, buf, buf],
        compiler_params=_cparams("arbitrary"))(dmixed, dmixed, pooled, wg, scale)


GELU_C = math.sqrt(2.0 / math.pi)


GELU_K = 0.044715


def _gelu_parts(x):
    x2 = x * x
    s = 0.5 + 0.5 * jnp.tanh(x * (GELU_C + (GELU_C * GELU_K) * x2))
    return x * s, s, x2


def _gelu(x):
    return _gelu_parts(x)[0]


def _gelu_and_grad(x):
    g, s, x2 = _gelu_parts(x)
    return g, s + g * (1.0 - s) * ((2 * GELU_C) + (6 * GELU_C * GELU_K) * x2)


def _taps(buf, r, rows):
    a = buf[pl.ds(r, rows + 8), :]
    return a[8:], pltpu.roll(a, 1, axis=0)[8:], pltpu.roll(a, 2, axis=0)[8:]


def _conv(taps, w_ref, b_ref):
    return b_ref[...] + w_ref[2:3, :] * taps[0] + w_ref[1:2, :] * taps[1] + w_ref[0:1, :] * taps[2]


def _stage(dst, prev_ref, cur_ref, next_ref, first, last):
    rows = cur_ref.shape[0]
    h = prev_ref[...].astype(F32)
    dst[0:8, :] = jnp.where(first, jnp.zeros_like(h), h)
    dst[8:8 + rows, :] = cur_ref[...].astype(F32)
    if next_ref is not None:
        h = next_ref[...].astype(F32)
        dst[8 + rows:, :] = jnp.where(last, jnp.zeros_like(h), h)


FWD_STRIP = 32
BWD_STRIP = 16


def _ffn_gate_fwd(name, hu, conv_w, conv_b, comm=None):
    t = hu.shape[0]
    ncol = D_FF // FF_COL
    hb = TOK // 8

    def tile(off):
        return pl.BlockSpec((TOK, FF_COL), lambda i, j: (i, j + off))

    def halo(off):
        return pl.BlockSpec((8, FF_COL), lambda i, j: (jnp.maximum(i * hb - 1, 0), j + off))

    def wspec(off):
        return pl.BlockSpec((3, FF_COL), lambda i, j: (0, j + off))

    def bspec(off):
        return pl.BlockSpec((1, FF_COL), lambda i, j: (0, j + off))

    def body(v_ref, vp_ref, g_ref, gp_ref, wv_ref, wg_ref, bv_ref, bg_ref, a_ref, vb, gb):
        first = pl.program_id(0) == 0
        _stage(vb, vp_ref, v_ref, None, first, None)
        _stage(gb, gp_ref, g_ref, None, first, None)

        def strip(k, carry):
            r = pl.multiple_of(k * FWD_STRIP, FWD_STRIP)
            val = _conv(_taps(vb, r, FWD_STRIP), wv_ref, bv_ref)
            gate = _conv(_taps(gb, r, FWD_STRIP), wg_ref, bg_ref)
            a_ref[pl.ds(r, FWD_STRIP), :] = (_gelu(gate) * val).astype(BF16)
            return carry

        lax.fori_loop(0, TOK // FWD_STRIP, strip, 0)

    buf = pltpu.VMEM((TOK + 8, FF_COL), F32)
    out, moved = _pcall(
        body, name, (t // TOK, ncol),
        [tile(0), halo(0), tile(ncol), halo(ncol), wspec(0), wspec(ncol), bspec(0), bspec(ncol)],
        pl.BlockSpec((TOK, FF_COL), lambda i, j: (i, j)), _sds((t, D_FF), BF16), [buf, buf],
        _cparams("arbitrary", "arbitrary"),
        (hu, hu, hu, hu, conv_w, conv_w, conv_b, conv_b), comm)
    return out if comm is None else (out, moved)


def _ffn_gate_bwd(name, da, hu, conv_w, conv_b, comm=None):
    t = hu.shape[0]
    nt = t // TOK
    ncol = D_FF // FF_COL
    hb = TOK // 8
    ext = TOK + 8

    def tile(off):
        return pl.BlockSpec((TOK, FF_COL), lambda j, i: (i, j + off))

    def prev(off):
        return pl.BlockSpec((8, FF_COL), lambda j, i: (jnp.maximum(i * hb - 1, 0), j + off))

    def nxt(off):
        return pl.BlockSpec((8, FF_COL), lambda j, i: (jnp.minimum((i + 1) * hb, nt * hb - 1), j + off))

    def wspec(off):
        return pl.BlockSpec((3, FF_COL), lambda j, i: (0, j + off))

    def bspec(off):
        return pl.BlockSpec((1, FF_COL), lambda j, i: (0, j + off))

    def body(da_ref, dan_ref, v_ref, vp_ref, vn_ref, g_ref, gp_ref, gn_ref,
             wv_ref, wg_ref, bv_ref, bg_ref, dh_ref, dwv_ref, dwg_ref, vb, gb, dab):
        i = pl.program_id(1)
        first, last = i == 0, i == nt - 1

        @pl.when(first)
        def _():
            dwv_ref[...] = jnp.zeros_like(dwv_ref)
            dwg_ref[...] = jnp.zeros_like(dwg_ref)

        _stage(vb, vp_ref, v_ref, vn_ref, first, last)
        _stage(gb, gp_ref, g_ref, gn_ref, first, last)
        dab[0:TOK, :] = da_ref[...].astype(F32)
        h = dan_ref[...].astype(F32)
        dab[TOK:, :] = jnp.where(last, jnp.zeros_like(h), h)

        def grads(r, rows):
            tv, tg = _taps(vb, r, rows), _taps(gb, r, rows)
            gate = _conv(tg, wg_ref, bg_ref)
            dav = dab[pl.ds(r, rows), :]
            g, dg = _gelu_and_grad(gate)
            dval = dav * g
            dgate = dav * _conv(tv, wv_ref, bv_ref) * dg
            return dval, dgate, tv, tg

        def fold(x):
            return x[0:8] + x[8:16]

        def strip(k, carry):
            r = pl.multiple_of(TOK - BWD_STRIP - k * BWD_STRIP, BWD_STRIP)
            dval, dgate, tv, tg = grads(r, BWD_STRIP)
            new = (dval[0:8], dgate[0:8])
            for half, (d, nxt_rows, taps, w_ref, dw_ref) in enumerate((
                    (dval, carry[0], tv, wv_ref, dwv_ref), (dgate, carry[1], tg, wg_ref, dwg_ref))):
                e = jnp.concatenate([d, nxt_rows], axis=0)
                dh = (w_ref[2:3, :] * d
                      + w_ref[1:2, :] * pltpu.roll(e, BWD_STRIP + 7, axis=0)[0:BWD_STRIP]
                      + w_ref[0:1, :] * pltpu.roll(e, BWD_STRIP + 6, axis=0)[0:BWD_STRIP])
                dh_ref[half, pl.ds(r, BWD_STRIP), :] = dh.astype(BF16)
                dw_ref[0:8, :] += fold(d * taps[2])
                dw_ref[8:16, :] += fold(d * taps[1])
                dw_ref[16:24, :] += fold(d * taps[0])
                dw_ref[24:32, :] += fold(d)
            return new

        dval, dgate, _, _ = grads(TOK, 8)
        lax.fori_loop(0, TOK // BWD_STRIP, strip, (dval, dgate))

        @pl.when(last)
        def _():
            for dw_ref in (dwv_ref, dwg_ref):
                for q in range(4):
                    dw_ref[8 * q:8 * q + 1, :] = jnp.sum(dw_ref[8 * q:8 * q + 8, :], axis=0,
                                                         keepdims=True)

    hbuf = pltpu.VMEM((TOK + 16, FF_COL), F32)
    acc = pl.BlockSpec((32, FF_COL), lambda j, i: (0, j))
    (dhu, dwv, dwg), moved = _pcall(
        body, name, (ncol, nt),
        [tile(0), nxt(0), tile(0), prev(0), nxt(0), tile(ncol), prev(ncol), nxt(ncol),
         wspec(0), wspec(ncol), bspec(0), bspec(ncol)],
        [pl.BlockSpec((2, TOK, FF_COL), lambda j, i: (0, i, j)), acc, acc],
        [_sds((2, t, D_FF), BF16), _sds((32, D_FF), F32), _sds((32, D_FF), F32)],
        [hbuf, hbuf, pltpu.VMEM((ext, FF_COL), F32)], _cparams("arbitrary", "arbitrary"),
        (da, da, hu, hu, hu, hu, hu, hu, conv_w, conv_w, conv_b, conv_b), comm)
    dconv = jnp.concatenate([dwv, dwg], axis=1).reshape(4, 8, 2 * D_FF)[:, 0]
    return (dhu, dconv) if comm is None else (dhu, dconv, moved)


def _mesh_pos():
    x, y, c = lax.axis_index("x"), lax.axis_index("y"), lax.axis_index("c")
    return x, y, c, [(1 - x, y), (x, 1 - y), (1 - x, 1 - y)]


def _any_specs(n):
    return [pl.BlockSpec(memory_space=pl.ANY)] * n


def _remote(src, dst, send_sems, recv_sems, i, dev):
    return pltpu.make_async_remote_copy(src_ref=src, dst_ref=dst, send_sem=send_sems.at[i],
                                        recv_sem=recv_sems.at[i], device_id=dev,
                                        device_id_type=MESH)


def _mine(c, rows):
    return pl.ds(pl.multiple_of(c * (rows // 2), 16), rows // 2)


def _gather_send(shards, conv_shard, gathered, l):
    nbig = len(shards)
    with_conv = gathered is None
    if with_conv:
        ins = list(shards) + [conv_shard]
        outs = [_sds((DEPTH, N_CHIPS) + s.shape[1:], s.dtype) for s in ins]
        alias = {}
    else:
        ins = list(shards) + list(gathered)
        outs = [_sds(g.shape, g.dtype) for g in gathered]
        alias = {nbig + k: k for k in range(nbig)}

    def copies(cin, cout, ssem, rsem):
        x, y, c, chips = _mesh_pos()
        me = 2 * x + y
        out = []
        for k in range(nbig):
            rows = shards[k].shape[1]
            for j, (cx, cy) in enumerate(chips):
                out.append(_remote(cin[k].at[l, _mine(c, rows)], cout[k].at[l, me, _mine(c, rows)],
                                   ssem, rsem, 4 * k + j, (cx, cy, c)))
            out.append(_remote(cin[k].at[l], cout[k].at[l, me], ssem, rsem, 4 * k + 3,
                               (x, y, 1 - c)))
        if with_conv:
            base = 4 * nbig
            for j, (cx, cy) in enumerate(chips):
                out.append(_remote(cin[nbig].at[c], cout[nbig].at[c, me], ssem, rsem, base + j,
                                   (cx, cy, c)))
            for ll in range(DEPTH):
                out.append(_remote(cin[nbig].at[ll], cout[nbig].at[ll, me], ssem, rsem,
                                   base + 3 + ll, (x, y, 1 - c)))
        return out

    return _Comm(ins, outs, copies, 4 * nbig + 5, alias)


def _gather_forward(gathered, nbig, rows, l):
    with_conv = len(gathered) > nbig
    alias = {k: k for k in range(len(gathered))}

    def copies(cin, cout, ssem, rsem):
        x, y, c, chips = _mesh_pos()
        out = []
        for k in range(nbig):
            for j, (cx, cy) in enumerate(chips):
                blk = cout[k].at[l, 2 * cx + cy, _mine(c, rows[k])]
                out.append(_remote(blk, blk, ssem, rsem, 3 * k + j, (x, y, 1 - c)))
        if with_conv:
            for j, (cx, cy) in enumerate(chips):
                blk = cout[nbig].at[c, 2 * cx + cy]
                out.append(_remote(blk, blk, ssem, rsem, 3 * nbig + j, (x, y, 1 - c)))
        return out

    return _Comm(gathered, [_sds(g.shape, g.dtype) for g in gathered], copies, 3 * nbig + 3, alias)


def _reduce_swap(grads, l):
    def copies(cin, cout, ssem, rsem):
        x, y, c, _ = _mesh_pos()
        return [_remote(cin[k].at[l, :, _mine(1 - c, g.shape[2])], cout[k], ssem, rsem, k,
                        (x, y, 1 - c)) for k, g in enumerate(grads)]

    outs = [_sds((N_CHIPS, g.shape[2] // 2, g.shape[3]), g.dtype) for g in grads]
    return _Comm(grads, outs, copies, len(grads))


def _reduce_scatter(sums):
    def copies(cin, cout, ssem, rsem):
        x, y, c, chips = _mesh_pos()
        return [_remote(cin[k].at[2 * cx + cy], cout[k].at[j], ssem, rsem, 3 * k + j, (cx, cy, c))
                for k in range(len(sums)) for j, (cx, cy) in enumerate(chips)]

    outs = [_sds((3,) + s.shape[1:], s.dtype) for s in sums]
    return _Comm(sums, outs, copies, 3 * len(sums))


def _reduce_share(reds, l):
    def copies(cin, cout, ssem, rsem):
        x, y, c, _ = _mesh_pos()
        out = []
        for k, r in enumerate(reds):
            half = cout[k].at[l, _mine(c, r.shape[1])]
            out.append(_remote(half, half, ssem, rsem, k, (x, y, 1 - c)))
        return out

    return _Comm(reds, [_sds(r.shape, r.dtype) for r in reds], copies, len(reds),
                 {k: k for k in range(len(reds))})


def _allgather_weights(shards):
    n = len(shards)

    def body(*refs):
        ins, outs = refs[:n], refs[n:2 * n]
        send_sems, recv_sems = refs[2 * n:]
        x, y, c, chips = _mesh_pos()
        me = 2 * x + y
        started = []
        own = []
        for k in range(n):
            for l in range(2):
                cp = pltpu.make_async_remote_copy(
                    src_ref=ins[k].at[l], dst_ref=outs[k].at[l, me],
                    send_sem=send_sems.at[k, 6 + l], recv_sem=recv_sems.at[k, 6 + l],
                    device_id=(x, y, 1 - c), device_id_type=MESH)
                cp.start()
                own.append(cp)
            for j, (cx, cy) in enumerate(chips):
                cp = pltpu.make_async_remote_copy(
                    src_ref=ins[k].at[c], dst_ref=outs[k].at[c, me],
                    send_sem=send_sems.at[k, j], recv_sem=recv_sems.at[k, j],
                    device_id=(cx, cy, c), device_id_type=MESH)
                cp.start()
                started.append(cp)
        for k in range(n):
            for j, (cx, cy) in enumerate(chips):
                landed = outs[k].at[c, 2 * cx + cy]
                pltpu.make_async_remote_copy(
                    src_ref=ins[k].at[c], dst_ref=landed,
                    send_sem=send_sems.at[k, j], recv_sem=recv_sems.at[k, j],
                    device_id=(cx, cy, c), device_id_type=MESH).wait_recv()
                fw = pltpu.make_async_remote_copy(
                    src_ref=landed, dst_ref=landed,
                    send_sem=send_sems.at[k, 3 + j], recv_sem=recv_sems.at[k, 3 + j],
                    device_id=(x, y, 1 - c), device_id_type=MESH)
                fw.start()
                started.append(fw)
        for k in range(n):
            for j, (cx, cy) in enumerate(chips):
                theirs = outs[k].at[1 - c, 2 * cx + cy]
                pltpu.make_async_remote_copy(
                    src_ref=theirs, dst_ref=theirs,
                    send_sem=send_sems.at[k, 3 + j], recv_sem=recv_sems.at[k, 3 + j],
                    device_id=(x, y, 1 - c), device_id_type=MESH).wait_recv()
        for cp in started:
            cp.wait_send()
        for cp in own:
            cp.wait()

    return pl.pallas_call(
        body, name="allgather_weights",
        in_specs=_any_specs(n), out_specs=_any_specs(n),
        out_shape=[_sds((2, N_CHIPS) + s.shape[1:], s.dtype) for s in shards],
        scratch_shapes=[pltpu.SemaphoreType.DMA((n, 8)), pltpu.SemaphoreType.DMA((n, 8))],
    )(*shards)


def _swap_layers(grads):
    n = len(grads)

    def body(*refs):
        ins, outs = refs[:n], refs[n:2 * n]
        send_sems, recv_sems = refs[2 * n:]
        x, y, c, _ = _mesh_pos()
        cps = []
        for k in range(n):
            cp = pltpu.make_async_remote_copy(
                src_ref=ins[k].at[1 - c], dst_ref=outs[k],
                send_sem=send_sems.at[k], recv_sem=recv_sems.at[k],
                device_id=(x, y, 1 - c), device_id_type=MESH)
            cp.start()
            cps.append(cp)
        for cp in cps:
            cp.wait()

    return pl.pallas_call(
        body, name="swap_layers",
        in_specs=_any_specs(n), out_specs=_any_specs(n),
        out_shape=[_sds(g.shape[1:], g.dtype) for g in grads],
        scratch_shapes=[pltpu.SemaphoreType.DMA((n,)), pltpu.SemaphoreType.DMA((n,))],
    )(*grads)


def _scatter_blocks(sums):
    n = len(sums)

    def body(*refs):
        ins, outs = refs[:n], refs[n:2 * n]
        send_sems, recv_sems = refs[2 * n:]
        x, y, c, chips = _mesh_pos()
        cps = []
        for k in range(n):
            for j, (cx, cy) in enumerate(chips):
                cp = pltpu.make_async_remote_copy(
                    src_ref=ins[k].at[2 * cx + cy], dst_ref=outs[k].at[j],
                    send_sem=send_sems.at[k, j], recv_sem=recv_sems.at[k, j],
                    device_id=(cx, cy, c), device_id_type=MESH)
                cp.start()
                cps.append(cp)
        for cp in cps:
            cp.wait()

    return pl.pallas_call(
        body, name="scatter_blocks",
        in_specs=_any_specs(n), out_specs=_any_specs(n),
        out_shape=[_sds((3,) + s.shape[1:], s.dtype) for s in sums],
        scratch_shapes=[pltpu.SemaphoreType.DMA((n, 3)), pltpu.SemaphoreType.DMA((n, 3))],
    )(*sums)


def _exchange_reduced(reds):
    n = len(reds)

    def body(*refs):
        outs = refs[n:2 * n]
        send_sems, recv_sems = refs[2 * n:]
        x, y, c, _ = _mesh_pos()
        cps = []
        for k in range(n):
            cp = pltpu.make_async_remote_copy(
                src_ref=outs[k].at[c], dst_ref=outs[k].at[c],
                send_sem=send_sems.at[k], recv_sem=recv_sems.at[k],
                device_id=(x, y, 1 - c), device_id_type=MESH)
            cp.start()
            cps.append(cp)
        for k in range(n):
            pltpu.make_async_remote_copy(
                src_ref=outs[k].at[c], dst_ref=outs[k].at[1 - c],
                send_sem=send_sems.at[k], recv_sem=recv_sems.at[k],
                device_id=(x, y, 1 - c), device_id_type=MESH).wait_recv()
        for cp in cps:
            cp.wait_send()

    return pl.pallas_call(
        body, name="exchange_reduced",
        in_specs=_any_specs(n), out_specs=_any_specs(n),
        out_shape=[_sds(r.shape, r.dtype) for r in reds],
        input_output_aliases={k: k for k in range(n)},
        scratch_shapes=[pltpu.SemaphoreType.DMA((n,)), pltpu.SemaphoreType.DMA((n,))],
    )(*reds)


def _allreduce_small(pack):
    n = pack.shape[0]

    def body(x_ref, o_ref, gbuf, send_sems, recv_sems):
        x, y, c, chips = _mesh_pos()
        sibling = (x, y, 1 - c)

        def slot(px, py, pc):
            return gbuf.at[4 * px + 2 * py + pc]

        def copy(k, block, to, src=None):
            return pltpu.make_async_remote_copy(
                src_ref=slot(*block) if src is None else src, dst_ref=slot(*block),
                send_sem=send_sems.at[k], recv_sem=recv_sems.at[k],
                device_id=to, device_id_type=MESH)

        me = (x, y, c)
        first = [copy(0, me, sibling, src=x_ref)]
        first += [copy(1 + j, me, (*chip, c), src=x_ref) for j, chip in enumerate(chips)]
        for cp in first:
            cp.start()
        gbuf[4 * x + 2 * y + c] = x_ref[...]
        passed = [copy(4 + j, (*chip, c), sibling) for j, chip in enumerate(chips)]
        for j, chip in enumerate(chips):
            copy(1 + j, (*chip, c), me).wait_recv()
            passed[j].start()
        copy(0, sibling, me).wait_recv()
        for j, chip in enumerate(chips):
            copy(4 + j, (*chip, 1 - c), me).wait_recv()
        for cp in first + passed:
            cp.wait_send()
        acc = gbuf[0]
        for d in range(1, 8):
            acc = acc + gbuf[d]
        o_ref[...] = acc

    return pl.pallas_call(
        body, name="allreduce_small",
        in_specs=[pl.BlockSpec(memory_space=pltpu.VMEM)],
        out_specs=pl.BlockSpec(memory_space=pltpu.VMEM),
        out_shape=_sds((n, 128), F32),
        scratch_shapes=[pltpu.VMEM((8, n, 128), F32), pltpu.SemaphoreType.DMA((7,)),
                        pltpu.SemaphoreType.DMA((7,))],
        compiler_params=pltpu.CompilerParams(vmem_limit_bytes=VMEM_LIMIT_V7X),
    )(pack)


def _core_index():
    return jnp.reshape(lax.axis_index("c"), (1,)).astype(jnp.int32)


def _chip_index():
    return jnp.reshape(2 * lax.axis_index("x") + lax.axis_index("y"), (1,)).astype(jnp.int32)


def _chip_sum(name, stacked, sib, l):
    _, nb, r, cdim = stacked.shape
    hr = r // 2

    def body(c_ref, a_ref, b_ref, o_ref):
        o_ref[...] = (a_ref[...].astype(F32) + b_ref[...].astype(F32)).astype(BF16)

    return pl.pallas_call(
        body, name=name,
        grid_spec=pltpu.PrefetchScalarGridSpec(
            num_scalar_prefetch=1, grid=(nb,),
            in_specs=[pl.BlockSpec((None, None, hr, cdim), lambda j, cr: (l, j, cr[0], 0)),
                      pl.BlockSpec((None, hr, cdim), lambda j, cr: (j, 0, 0))],
            out_specs=pl.BlockSpec((None, hr, cdim), lambda j, cr: (j, 0, 0))),
        out_shape=_sds((nb, hr, cdim), BF16),
        compiler_params=_cparams("parallel"))(_core_index(), stacked, sib)


def _final_sum(name, sums, recv, l, fill):
    _, hr, cdim = sums.shape
    tr = hr // 2

    def body(m_ref, a_ref, b_ref, *rest):
        acc = a_ref[...].astype(F32)
        for j in range(3):
            acc = acc + b_ref[j].astype(F32)
        rest[-1][...] = acc

    in_specs = [pl.BlockSpec((None, tr, cdim), lambda i, mr: (mr[0], i, 0)),
                pl.BlockSpec((3, tr, cdim), lambda i, mr: (0, i, 0))]
    args = [jnp.concatenate([_chip_index(), _core_index()]), sums, recv]
    aliases = {}
    if fill is not None:
        in_specs.append(pl.BlockSpec(memory_space=pl.ANY))
        args.append(fill)
        aliases = {3: 0}
    return pl.pallas_call(
        body, name=name,
        grid_spec=pltpu.PrefetchScalarGridSpec(
            num_scalar_prefetch=1, grid=(2,), in_specs=in_specs,
            out_specs=pl.BlockSpec((None, tr, cdim), lambda i, mr: (l, 2 * mr[1] + i, 0))),
        out_shape=_sds((DEPTH, 2 * hr, cdim), F32), input_output_aliases=aliases,
        compiler_params=_cparams("parallel"))(*args)


def _adamw(name, w, g, m, v):
    nl, r, cdim = w.shape
    tr = r // 4 if r % 32 == 0 else r
    c1 = 1.0 - ADAM_B1 ** ADAM_STEP
    c2 = 1.0 - ADAM_B2 ** ADAM_STEP

    def body(w_ref, g_ref, m_ref, v_ref, d_ref, nm_ref, nv_ref):
        gv = g_ref[...]
        nm = ADAM_B1 * m_ref[...] + (1.0 - ADAM_B1) * gv
        nv = ADAM_B2 * v_ref[...] + (1.0 - ADAM_B2) * (gv * gv)
        nm_ref[...] = nm
        nv_ref[...] = nv
        d_ref[...] = -ADAM_LR * ((nm / c1) / (jnp.sqrt(nv / c2) + ADAM_EPS) + ADAM_WD * w_ref[...])

    spec = pl.BlockSpec((None, tr, cdim), lambda l, i: (l, i, 0))
    out = _sds(w.shape, F32)
    return pl.pallas_call(
        body, name=name, grid=(nl, r // tr),
        in_specs=[spec] * 4, out_specs=[spec] * 3, out_shape=[out] * 3,
        compiler_params=_cparams("parallel", "parallel"))(w, g, m, v)


def _rows128(a):
    return a.reshape(-1, 128)


def kernel(x, norm_mix_pre, w_in, b_gate, rel_bias, w_attn_out, w_pool_group, pool_scale, w_pool_out, w_o, norm_mix_post, norm_ffn_pre, w_up, conv_w, conv_b, w_down, norm_ffn_post, loss_target, m_norm_mix_pre, m_w_in, m_b_gate, m_rel_bias, m_w_attn_out, m_w_pool_group, m_pool_scale, m_w_pool_out, m_w_o, m_norm_mix_post, m_norm_ffn_pre, m_w_up, m_conv_w, m_conv_b, m_w_down, m_norm_ffn_post, v_norm_mix_pre, v_w_in, v_b_gate, v_rel_bias, v_w_attn_out, v_w_pool_group, v_pool_scale, v_w_pool_out, v_w_o, v_norm_mix_post, v_norm_ffn_pre, v_w_up, v_conv_w, v_conv_b, v_w_down, v_norm_ffn_post):
    t = x.shape[1]
    xs = x.reshape(t, D_MODEL)
    target = loss_target.reshape(t, D_MODEL)

    names = ["w_in", "w_attn_out", "w_pool_out", "w_o", "w_up", "w_down"]
    shards = [w.astype(BF16) for w in (w_in, w_attn_out, w_pool_out, w_o, w_up, w_down)]
    rows = [s.shape[1] for s in shards]
    nbig = len(shards)
    g = _comm_call("gather0_send", _gather_send(shards, conv_w, None, 0))
    g = _comm_call("gather0_forward", _gather_forward(g, nbig, rows, 0))
    cw_full = jnp.transpose(g[nbig], (0, 2, 1, 3)).reshape(DEPTH, 3, 2 * D_FF)
    g = g[:nbig]
    wg_bf = w_pool_group.astype(BF16)

    def weights(gathered):
        win_g, wao_g, wpo_g, wo_g, wup_g, wdn_g = gathered
        return (win_g, wao_g, wpo_g, wo_g.reshape(DEPTH, D_MODEL, D_MODEL), wup_g,
                wdn_g.reshape(DEPTH, D_FF, D_MODEL))

    saved = []
    xcur = xs
    for l in range(DEPTH):
        tag = f"l{l}_"
        win_g, wao_g, wpo_g, wo_full, wup_g, wdn_full = weights(g)
        bias = _bias_table(tag + "bias_table", rel_bias[l])
        h = _norm_fwd(tag + "norm_mix_pre", xcur, norm_mix_pre[l:l + 1])
        proj = _mm_nn_blocked(tag + "proj", h, win_g, l, BF16)
        att = _attn_fwd(tag + "attn_fwd", proj, bias)
        pooled, mixed = _pool_fwd(tag + "pool_fwd", proj, wg_bf[l], pool_scale[l:l + 1])
        ya = _mm_nn_blocked(tag + "attn_out", att, wao_g, l, BF16)
        yb = _mm_nn_blocked(tag + "pool_out", mixed, wpo_g, l, BF16)
        z = _gate_fwd(tag + "gate_fwd", proj, b_gate[l:l + 1], ya, yb)
        mix = _mm_nn(tag + "mix", z, wo_full, l, D_MODEL, F32)
        x1 = _norm_residual_fwd(tag + "norm_mix_post", xcur, mix, norm_mix_post[l:l + 1])
        h2 = _norm_fwd(tag + "norm_ffn_pre", x1, norm_ffn_pre[l:l + 1])
        hu = _mm_nn_blocked(tag + "ffn_up", h2, wup_g, l, BF16)
        if l == 0:
            a, g = _ffn_gate_fwd(tag + "ffn_gate_fwd", hu, cw_full[l], conv_b[l:l + 1],
                                 _gather_send(shards, None, g, 1))
            wdn_full = weights(g)[5]
        else:
            a = _ffn_gate_fwd(tag + "ffn_gate_fwd", hu, cw_full[l], conv_b[l:l + 1])
        f = _mm_nn(tag + "ffn_down", a, wdn_full, l, D_FF // 2, F32)
        if l == 0:
            x2, g = _norm_residual_fwd(tag + "norm_ffn_post", x1, f, norm_ffn_post[l:l + 1],
                                       _gather_forward(g, nbig, rows, 1))
        else:
            x2 = _norm_residual_fwd(tag + "norm_ffn_post", x1, f, norm_ffn_post[l:l + 1])
        saved.append(dict(x=xcur, h=h, proj=proj, att=att, pooled=pooled, mixed=mixed, ya=ya,
                          yb=yb, z=z, mix=mix, x1=x1, h2=h2, hu=hu, a=a, f=f, bias=bias))
        xcur = x2
    win_g, wao_g, wpo_g, wo_full, wup_g, wdn_full = weights(g)

    dy, loss_local = _loss_head(xcur, target)
    loss = lax.psum(loss_local, ("x", "y", "c"))

    dx = dy
    dws = dict.fromkeys(names)
    reds = [None] * nbig
    small_grads = [None] * DEPTH
    ffn = [4, 5]

    def blocks(ks):
        return [dws[names[k]].reshape(DEPTH, N_CHIPS, rows[k], -1) for k in ks]

    def chip_sums(ks, sib, l):
        return [_chip_sum(f"chip_sum{l}_" + names[k], b, s, l)
                for k, b, s in zip(ks, blocks(ks), sib)]

    def final_sums(ks, sums, recv, l):
        for k, s, r in zip(ks, sums, recv):
            reds[k] = _final_sum(f"final_sum{l}_" + names[k], s, r, l, reds[k])

    for l in reversed(range(DEPTH)):
        tag = f"l{l}_"
        sv = saved[l]
        every = list(range(nbig))
        df, d_nfpost = _norm_post_bwd(tag + "norm_ffn_post_bwd", dx, sv["f"], norm_ffn_post[l:l + 1])
        if l == 0:
            da, sib = _mm_nt(tag + "ffn_down_dx", df, wdn_full, l, D_FF // 2, BF16,
                             _reduce_swap(blocks(every), 1))
            sums = chip_sums(every, sib, 1)
        else:
            da = _mm_nt(tag + "ffn_down_dx", df, wdn_full, l, D_FF // 2, BF16)
        dws["w_down"] = _mm_tn(tag + "ffn_down_dw", sv["a"], df, D_FF // 2, l, dws["w_down"])
        if l == 0:
            dhu, dconv, recv = _ffn_gate_bwd(tag + "ffn_gate_bwd", da, sv["hu"], cw_full[l],
                                             conv_b[l:l + 1], _reduce_scatter(sums))
            final_sums(every, sums, recv, 1)
            dh2, reds = _mm_nt_blocked(tag + "ffn_up_dx", dhu, wup_g, l, F32,
                                       _reduce_share(reds, 1))
        else:
            dhu, dconv = _ffn_gate_bwd(tag + "ffn_gate_bwd", da, sv["hu"], cw_full[l],
                                       conv_b[l:l + 1])
            dh2 = _mm_nt_blocked(tag + "ffn_up_dx", dhu, wup_g, l, F32)
        dws["w_up"] = _mm_tn_blocked(tag + "ffn_up_dw", sv["h2"], dhu, l, dws["w_up"])
        if l == 0:
            dx1, d_nfpre, sib = _norm_pre_bwd(tag + "norm_ffn_pre_bwd", dh2, sv["x1"], dx,
                                              norm_ffn_pre[l:l + 1], _reduce_swap(blocks(ffn), 0))
            sums = chip_sums(ffn, sib, 0)
        else:
            dx1, d_nfpre = _norm_pre_bwd(tag + "norm_ffn_pre_bwd", dh2, sv["x1"], dx,
                                         norm_ffn_pre[l:l + 1])
        dmix, d_nmpost = _norm_post_bwd(tag + "norm_mix_post_bwd", dx1, sv["mix"], norm_mix_post[l:l + 1])
        dz = _mm_nt(tag + "mix_dx", dmix, wo_full, l, D_MODEL, BF16)
        dws["w_o"] = _mm_tn(tag + "mix_dw", sv["z"], dmix, D_MODEL, l, dws["w_o"])
        dya, dyb, dgates, d_bgate = _gate_bwd(tag + "gate_bwd", dz, sv["proj"], b_gate[l:l + 1],
                                              sv["ya"], sv["yb"])
        datt = _mm_nt_blocked(tag + "attn_out_dx", dya, wao_g, l, BF16)
        dws["w_attn_out"] = _mm_tn_blocked(tag + "attn_out_dw", sv["att"], dya, l, dws["w_attn_out"])
        dmixed = _mm_nt_blocked(tag + "pool_out_dx", dyb, wpo_g, l, BF16)
        dws["w_pool_out"] = _mm_tn_blocked(tag + "pool_out_dw", sv["mixed"], dyb, l, dws["w_pool_out"])
        du, d_wg, d_pscale = _pool_bwd(tag + "pool_bwd", dmixed, sv["pooled"], wg_bf[l],
                                       pool_scale[l:l + 1])
        if l == 0:
            dq, dk, dv, dbias, recv = _attn_bwd(tag + "attn_bwd", sv["proj"], datt, sv["bias"],
                                                _reduce_scatter(sums))
            final_sums(ffn, sums, recv, 0)
        else:
            dq, dk, dv, dbias = _attn_bwd(tag + "attn_bwd", sv["proj"], datt, sv["bias"])
        d_rel = _bias_fold(tag + "bias_fold", dbias)
        dproj = jnp.concatenate([dq, dk, dv, du, dgates], axis=1)
        if l == 0:
            dh, shared = _mm_nt_blocked(tag + "proj_dx", dproj, win_g, l, F32,
                                        _reduce_share([reds[k] for k in ffn], 0))
            for k, r in zip(ffn, shared):
                reds[k] = r
        else:
            dh = _mm_nt_blocked(tag + "proj_dx", dproj, win_g, l, F32)
        dws["w_in"] = _mm_tn_blocked(tag + "proj_dw", sv["h"], dproj, l, dws["w_in"])
        dx, d_nmpre = _norm_pre_bwd(tag + "norm_mix_pre_bwd", dh, sv["x"], dx1, norm_mix_pre[l:l + 1])
        small_grads[l] = [d_nmpre, d_nmpost, d_nfpre, d_nfpost, d_bgate, d_rel, d_wg, d_pscale,
                          dconv[3:4], dconv[0:3]]

    grad_x = dx.reshape(x.shape)

    mixing = [0, 1, 2, 3]
    sib = _comm_call("reduce_swap", _reduce_swap(blocks(mixing), 0))
    sums = chip_sums(mixing, sib, 0)
    recv = _comm_call("reduce_scatter", _reduce_scatter(sums))
    final_sums(mixing, sums, recv, 0)
    shared = _comm_call("reduce_share", _reduce_share([reds[k] for k in mixing], 0))
    g_big = shared + [reds[k] for k in ffn]

    pieces = []
    for idx in range(10):
        pieces.append(jnp.stack([small_grads[0][idx], small_grads[1][idx]]))
    pack = jnp.concatenate([_rows128(p) for p in pieces], axis=0)
    red = _allreduce_small(pack)
    shapes = [p.shape for p in pieces]
    outs = []
    row = 0
    for shp in shapes:
        nrow = math.prod(shp) // 128
        outs.append(red[row:row + nrow].reshape(shp))
        row += nrow
    (g_nmpre, g_nmpost, g_nfpre, g_nfpost, g_bgate, g_rel, g_wg, g_pscale, g_cb, g_cw) = outs
    g_nmpre, g_nmpost, g_nfpre, g_nfpost = [a.reshape(DEPTH, D_MODEL)
                                            for a in (g_nmpre, g_nmpost, g_nfpre, g_nfpost)]
    g_bgate = g_bgate.reshape(DEPTH, 2 * D_MODEL)
    g_rel = g_rel[:, :, :N_REL]
    g_pscale = g_pscale.reshape(DEPTH, POOL_W)
    g_cb = g_cb.reshape(DEPTH, 2 * D_FF)
    ncw = conv_w.shape[2]
    chip = 2 * lax.axis_index("x") + lax.axis_index("y")
    g_cw = lax.dynamic_slice_in_dim(g_cw, chip * ncw, ncw, axis=2)

    grads = dict(norm_mix_pre=g_nmpre, w_in=g_big[0], b_gate=g_bgate, rel_bias=g_rel,
                 w_attn_out=g_big[1], w_pool_group=g_wg, pool_scale=g_pscale, w_pool_out=g_big[2],
                 w_o=g_big[3], norm_mix_post=g_nmpost, norm_ffn_pre=g_nfpre, w_up=g_big[4],
                 conv_w=g_cw, conv_b=g_cb, w_down=g_big[5], norm_ffn_post=g_nfpost)
    weights = dict(norm_mix_pre=norm_mix_pre, w_in=w_in, b_gate=b_gate, rel_bias=rel_bias,
                   w_attn_out=w_attn_out, w_pool_group=w_pool_group, pool_scale=pool_scale,
                   w_pool_out=w_pool_out, w_o=w_o, norm_mix_post=norm_mix_post,
                   norm_ffn_pre=norm_ffn_pre, w_up=w_up, conv_w=conv_w, conv_b=conv_b,
                   w_down=w_down, norm_ffn_post=norm_ffn_post)
    moms = dict(norm_mix_pre=(m_norm_mix_pre, v_norm_mix_pre), w_in=(m_w_in, v_w_in),
                b_gate=(m_b_gate, v_b_gate), rel_bias=(m_rel_bias, v_rel_bias),
                w_attn_out=(m_w_attn_out, v_w_attn_out),
                w_pool_group=(m_w_pool_group, v_w_pool_group),
                pool_scale=(m_pool_scale, v_pool_scale), w_pool_out=(m_w_pool_out, v_w_pool_out),
                w_o=(m_w_o, v_w_o), norm_mix_post=(m_norm_mix_post, v_norm_mix_post),
                norm_ffn_pre=(m_norm_ffn_pre, v_norm_ffn_pre), w_up=(m_w_up, v_w_up),
                conv_w=(m_conv_w, v_conv_w), conv_b=(m_conv_b, v_conv_b),
                w_down=(m_w_down, v_w_down), norm_ffn_post=(m_norm_ffn_post, v_norm_ffn_post))
    order = list(weights.keys())

    delta, new_m, new_v = {}, {}, {}
    small_names = [nm for nm in order if nm not in names]
    for nm in names:
        delta[nm], new_m[nm], new_v[nm] = _adamw("adamw_" + nm, weights[nm], grads[nm], *moms[nm])

    def pack_small(get):
        flat = [get(nm).reshape(-1) for nm in small_names]
        total = sum(f.shape[0] for f in flat)
        padded = -(-total // 1024) * 1024
        flat.append(jnp.zeros((padded - total,), F32))
        return jnp.concatenate(flat).reshape(1, padded // 128, 128)

    d_s, m_s, v_s = _adamw(
        "adamw_small", pack_small(lambda nm: weights[nm]), pack_small(lambda nm: grads[nm]),
        pack_small(lambda nm: moms[nm][0]) , pack_small(lambda nm: moms[nm][1]))
    off = 0
    for nm in small_names:
        size = math.prod(weights[nm].shape)
        for dst, src in ((delta, d_s), (new_m, m_s), (new_v, v_s)):
            dst[nm] = src.reshape(-1)[off:off + size].reshape(weights[nm].shape)
        off += size

    return (loss, grad_x, *[grads[nm] for nm in order], *[delta[nm] for nm in order],
            *[new_m[nm] for nm in order], *[new_v[nm] for nm in order])
```

```python
import functools
import math

import jax
import jax.numpy as jnp
from jax import lax
from jax.experimental import pallas as pl
from jax.experimental.pallas import tpu as pltpu

F32 = jnp.float32
BF16 = jnp.bfloat16
MESH = pl.DeviceIdType.MESH

D_MODEL = 1024
DEPTH = 2
CHUNK = 64
BAND_CHUNKS = 9
BAND = BAND_CHUNKS * CHUNK
HEADS = 8
HEAD_DIM = 64
ATTN_W = HEADS * HEAD_DIM
POOL_WINDOWS = (2, 4, 8, 16)
POOL_W = 512
POOL_GD = 128
MAX_REL = 256
N_REL = 2 * MAX_REL + 1
D_FF = 2816
IN_W = 3 * ATTN_W + POOL_W + 2 * D_MODEL
EPS = 1e-6
ATTN_SCALE = HEAD_DIM ** -0.5
BAND_PAD = 640
BIAS_LANES = BAND_PAD
N_CHIPS = 4

ADAM_LR = 0.001
ADAM_B1 = 0.9
ADAM_B2 = 0.999
ADAM_EPS = 1e-08
ADAM_WD = 0.01
ADAM_STEP = 10

VMEM_LIMIT_V7X = 56 * 1024 * 1024
TOK = 512
ATT_BLK = 8 * CHUNK
FF_COL = 256
HALO = 32


def _cparams(*sem):
    return pltpu.CompilerParams(dimension_semantics=sem, vmem_limit_bytes=VMEM_LIMIT_V7X)


def _sds(shape, dtype):
    return jax.ShapeDtypeStruct(shape, dtype)


class _Comm:
    def __init__(self, ins, outs, copies, n_sems, alias=None):
        self.ins, self.outs, self.copies, self.n_sems = list(ins), list(outs), copies, n_sems
        self.alias = dict(alias or {})


class _SemsFrom:
    def __init__(self, sems, start):
        self.sems, self.start = sems, start

    @property
    def at(self):
        return self

    def __getitem__(self, i):
        return self.sems.at[self.start + i]


def _both(a, b):
    na, nao = len(a.ins), len(a.outs)

    def copies(cin, cout, ssem, rsem):
        return (a.copies(cin[:na], cout[:nao], ssem, rsem)
                + b.copies(cin[na:], cout[nao:], _SemsFrom(ssem, a.n_sems), _SemsFrom(rsem, a.n_sems)))

    alias = dict(a.alias)
    alias.update({na + i: nao + o for i, o in b.alias.items()})
    return _Comm(a.ins + b.ins, a.outs + b.outs, copies, a.n_sems + b.n_sems, alias)


def _pcall(body, name, grid, in_specs, out_specs, out_shape, scratch_shapes, compiler_params, args,
           comm=None, aliases=None):
    single = not isinstance(out_shape, (list, tuple))
    out_specs = [out_specs] if single else list(out_specs)
    out_shape = [out_shape] if single else list(out_shape)
    n_in, n_out = len(in_specs), len(out_specs)
    aliases = dict(aliases or {})
    if comm is None:
        res = pl.pallas_call(
            body, name=name, grid=grid, in_specs=list(in_specs), out_specs=out_specs,
            out_shape=out_shape, scratch_shapes=list(scratch_shapes),
            input_output_aliases=aliases, compiler_params=compiler_params)(*args)
        return (res[0] if single else res), None
    ci, co = len(comm.ins), len(comm.outs)

    def hosted(*refs):
        main_in, cin = refs[:n_in], refs[n_in:n_in + ci]
        main_out = refs[n_in + ci:n_in + ci + n_out]
        cout = refs[n_in + ci + n_out:n_in + ci + n_out + co]
        rest = refs[n_in + ci + n_out + co:]
        copies = comm.copies(cin, cout, rest[-2], rest[-1])
        ids = [pl.program_id(a) for a in range(len(grid))]
        first = functools.reduce(jnp.logical_and, [i == 0 for i in ids])
        last = functools.reduce(jnp.logical_and, [i == g - 1 for i, g in zip(ids, grid)])

        @pl.when(first)
        def _():
            for cp in copies:
                cp.start()

        body(*main_in, *main_out, *rest[:-2])

        @pl.when(last)
        def _():
            for cp in copies:
                cp.wait()

    for i, o in comm.alias.items():
        aliases[n_in + i] = n_out + o
    hbm = pl.BlockSpec(memory_space=pl.ANY)
    sems = pltpu.SemaphoreType.DMA((comm.n_sems,))
    res = pl.pallas_call(
        hosted, name=name, grid=grid, in_specs=list(in_specs) + [hbm] * ci,
        out_specs=out_specs + [hbm] * co, out_shape=out_shape + comm.outs,
        scratch_shapes=list(scratch_shapes) + [sems, sems],
        input_output_aliases=aliases, compiler_params=compiler_params)(*args, *comm.ins)
    return (res[0] if single else list(res[:n_out])), list(res[n_out:])


def _comm_call(name, comm):
    ci = len(comm.ins)

    def body(*refs):
        copies = comm.copies(refs[:ci], refs[ci:-2], refs[-2], refs[-1])
        for cp in copies:
            cp.start()
        for cp in copies:
            cp.wait()

    hbm = pl.BlockSpec(memory_space=pl.ANY)
    sems = pltpu.SemaphoreType.DMA((comm.n_sems,))
    return list(pl.pallas_call(
        body, name=name, in_specs=[hbm] * ci, out_specs=[hbm] * len(comm.outs),
        out_shape=comm.outs, scratch_shapes=[sems, sems],
        input_output_aliases=comm.alias)(*comm.ins))


def _matmul(name, a, b, a_spec, b_spec, o_spec, out_shape, grid, contract, nk, acc_shape,
            fill=None, comm=None):
    def body(*refs):
        a_ref, b_ref = refs[0], refs[1]
        o_ref = refs[2 if fill is None else 3]
        scratch = refs[(3 if fill is None else 4):]
        part = lax.dot_general(a_ref[...], b_ref[...], (contract, ((), ())),
                               preferred_element_type=F32)
        if nk == 1:
            o_ref[...] = part.astype(o_ref.dtype)
        else:
            acc_ref = scratch[0]
            k = pl.program_id(2)

            @pl.when(k == 0)
            def _():
                acc_ref[...] = part

            @pl.when(k > 0)
            def _():
                acc_ref[...] += part

            @pl.when(k == nk - 1)
            def _():
                o_ref[...] = acc_ref[...].astype(o_ref.dtype)

    scratch = [] if nk == 1 else [pltpu.VMEM(acc_shape, F32)]
    in_specs, args, aliases = [a_spec, b_spec], [a, b], {}
    if fill is not None:
        in_specs.append(pl.BlockSpec(memory_space=pl.ANY))
        args.append(fill)
        aliases = {2: 0}
    out, moved = _pcall(body, name, grid, in_specs, o_spec, out_shape, scratch,
                        _cparams("parallel", "parallel", "arbitrary"), args, comm, aliases)
    return out if comm is None else (out, moved)


NN = ((1,), (0,))
NT = ((1,), (1,))
TN = ((0,), (0,))


def _tm(t):
    return min(t, 1024)


def _col_block_spec(a, rows, nb, row_col):
    if a.ndim == 2:
        return pl.BlockSpec((rows, nb), row_col)

    def halves(*ids):
        r, c = row_col(*ids)
        return c // 2, r, c % 2

    return pl.BlockSpec((None, rows, nb), halves)


def _mm_nn_blocked(name, a, w, l, out_dtype):
    t, k = a.shape
    nb = w.shape[3]
    tm = _tm(t)
    return _matmul(
        name, a, w,
        pl.BlockSpec((tm, k), lambda i, n, kk: (i, 0)),
        pl.BlockSpec((None, None, k, nb), lambda i, n, kk: (l, n, 0, 0)),
        pl.BlockSpec((tm, nb), lambda i, n, kk: (i, n)),
        _sds((t, N_CHIPS * nb), out_dtype), (t // tm, N_CHIPS, 1), NN, 1, None)


def _mm_nt_blocked(name, a, w, l, out_dtype, comm=None):
    t = a.shape[-2]
    k, nb = w.shape[2], w.shape[3]
    tm = _tm(t)
    return _matmul(
        name, a, w,
        _col_block_spec(a, tm, nb, lambda i, n, kk: (i, kk)),
        pl.BlockSpec((None, None, k, nb), lambda i, n, kk: (l, kk, 0, 0)),
        pl.BlockSpec((tm, k), lambda i, n, kk: (i, 0)),
        _sds((t, k), out_dtype), (t // tm, 1, N_CHIPS), NT, N_CHIPS, (tm, k), comm=comm)


def _mm_tn_blocked(name, a, g, l, fill):
    t, k = a.shape
    nb = g.shape[-1] * (g.ndim - 1) // N_CHIPS
    tt = _tm(t)
    nt = t // tt
    return _matmul(
        name, a, g,
        pl.BlockSpec((tt, k), lambda n, j, kk: (kk, 0)),
        _col_block_spec(g, tt, nb, lambda n, j, kk: (kk, n)),
        pl.BlockSpec((None, None, k, nb), lambda n, j, kk: (l, n, 0, 0)),
        _sds((DEPTH, N_CHIPS, k, nb), BF16), (N_CHIPS, 1, nt), TN, nt, (k, nb), fill)


def _mm_nn(name, a, w, l, tk, out_dtype):
    t, k = a.shape
    n = w.shape[2]
    tm = _tm(t)
    nk = k // tk
    return _matmul(
        name, a, w,
        pl.BlockSpec((tm, tk), lambda i, j, kk: (i, kk)),
        pl.BlockSpec((None, tk, n), lambda i, j, kk: (l, kk, 0)),
        pl.BlockSpec((tm, n), lambda i, j, kk: (i, 0)),
        _sds((t, n), out_dtype), (t // tm, 1, nk), NN, nk, (tm, n))


def _mm_nt(name, a, w, l, tn, out_dtype, comm=None):
    t, n = a.shape
    k = w.shape[1]
    tm = _tm(t)
    return _matmul(
        name, a, w,
        pl.BlockSpec((tm, n), lambda i, j, kk: (i, 0)),
        pl.BlockSpec((None, tn, n), lambda i, j, kk: (l, j, 0)),
        pl.BlockSpec((tm, tn), lambda i, j, kk: (i, j)),
        _sds((t, k), out_dtype), (t // tm, k // tn, 1), NT, 1, None, comm=comm)


def _mm_tn(name, a, g, tko, l, fill):
    t, k = a.shape
    n = g.shape[1]
    tt = _tm(t)
    nt = t // tt
    return _matmul(
        name, a, g,
        pl.BlockSpec((tt, tko), lambda i, j, kk: (kk, i)),
        pl.BlockSpec((tt, n), lambda i, j, kk: (kk, 0)),
        pl.BlockSpec((None, tko, n), lambda i, j, kk: (l, i, 0)),
        _sds((DEPTH, k, n), BF16), (k // tko, 1, nt), TN, nt, (tko, n), fill)


def _row_spec(width, col=0):
    return pl.BlockSpec((TOK, width), lambda i: (i, col))


def _vec_spec(width):
    return pl.BlockSpec((1, width), lambda i: (0, 0))


def _rms(x):
    return lax.rsqrt(jnp.mean(x * x, axis=-1, keepdims=True) + EPS)


def _norm_fwd(name, x, g):
    t = x.shape[0]

    def body(x_ref, g_ref, h_ref):
        xv = x_ref[...]
        h_ref[...] = (xv * _rms(xv) * g_ref[...]).astype(BF16)

    return pl.pallas_call(
        body, name=name, grid=(t // TOK,), in_specs=[_row_spec(D_MODEL), _vec_spec(D_MODEL)],
        out_specs=_row_spec(D_MODEL), out_shape=_sds((t, D_MODEL), BF16),
        compiler_params=_cparams("parallel"))(x, g)


def _norm_residual_fwd(name, xres, m, g, comm=None):
    t = xres.shape[0]

    def body(x_ref, m_ref, g_ref, o_ref):
        mv = m_ref[...]
        o_ref[...] = x_ref[...] + mv * _rms(mv) * g_ref[...]

    out, moved = _pcall(
        body, name, (t // TOK,),
        [_row_spec(D_MODEL), _row_spec(D_MODEL), _vec_spec(D_MODEL)],
        _row_spec(D_MODEL), _sds((t, D_MODEL), F32), [], _cparams("arbitrary"),
        (xres, m, g), comm)
    return out if comm is None else (out, moved)


def _norm_post_bwd(name, dxo, m, g):
    t = dxo.shape[0]

    def body(d_ref, m_ref, g_ref, dm_ref, dg_ref):
        mv = m_ref[...]
        dv = d_ref[...]
        r = _rms(mv)
        n = mv * r
        dn = dv * g_ref[...]
        dm_ref[...] = (r * (dn - n * jnp.mean(dn * n, axis=-1, keepdims=True))).astype(BF16)
        part = jnp.sum(dv * n, axis=0, keepdims=True)

        @pl.when(pl.program_id(0) == 0)
        def _():
            dg_ref[...] = part

        @pl.when(pl.program_id(0) > 0)
        def _():
            dg_ref[...] += part

    return pl.pallas_call(
        body, name=name, grid=(t // TOK,),
        in_specs=[_row_spec(D_MODEL), _row_spec(D_MODEL), _vec_spec(D_MODEL)],
        out_specs=[_row_spec(D_MODEL), _vec_spec(D_MODEL)],
        out_shape=[_sds((t, D_MODEL), BF16), _sds((1, D_MODEL), F32)],
        compiler_params=_cparams("arbitrary"))(dxo, m, g)


def _norm_pre_bwd(name, dh, xin, dxo, g, comm=None):
    t = dh.shape[0]

    def body(dh_ref, x_ref, d_ref, g_ref, dx_ref, dg_ref):
        xv = x_ref[...]
        dhv = dh_ref[...]
        r = _rms(xv)
        n = xv * r
        dn = dhv * g_ref[...]
        dx_ref[...] = d_ref[...] + r * (dn - n * jnp.mean(dn * n, axis=-1, keepdims=True))
        part = jnp.sum(dhv * n, axis=0, keepdims=True)

        @pl.when(pl.program_id(0) == 0)
        def _():
            dg_ref[...] = part

        @pl.when(pl.program_id(0) > 0)
        def _():
            dg_ref[...] += part

    out, moved = _pcall(
        body, name, (t // TOK,),
        [_row_spec(D_MODEL), _row_spec(D_MODEL), _row_spec(D_MODEL), _vec_spec(D_MODEL)],
        [_row_spec(D_MODEL), _vec_spec(D_MODEL)],
        [_sds((t, D_MODEL), F32), _sds((1, D_MODEL), F32)], [], _cparams("arbitrary"),
        (dh, xin, dxo, g), comm)
    return out if comm is None else (*out, moved)


def _loss_head(y, target):
    t = y.shape[0]

    def body(y_ref, t_ref, dy_ref, l_ref):
        e = y_ref[...] - t_ref[...]
        dy_ref[...] = e * (1.0 / D_MODEL)
        part = jnp.sum(jnp.sum(e * e, axis=0, keepdims=True), axis=1, keepdims=True)

        @pl.when(pl.program_id(0) == 0)
        def _():
            l_ref[...] = part

        @pl.when(pl.program_id(0) > 0)
        def _():
            l_ref[...] += part

    dy, sq = pl.pallas_call(
        body, name="loss_head", grid=(t // TOK,),
        in_specs=[_row_spec(D_MODEL), _row_spec(D_MODEL)],
        out_specs=[_row_spec(D_MODEL), pl.BlockSpec((1, 1), lambda i: (0, 0))],
        out_shape=[_sds((t, D_MODEL), F32), _sds((1, 1), F32)],
        compiler_params=_cparams("arbitrary"))(y, target)
    return dy, sq[0, 0] * (0.5 / D_MODEL)


def _gate_fwd(name, proj, b_gate, ya, yb):
    t = proj.shape[0]

    def body(ga_ref, gb_ref, b_ref, ya_ref, yb_ref, z_ref):
        sa = jax.nn.sigmoid(ga_ref[...].astype(F32) + b_ref[:, :D_MODEL])
        sb = jax.nn.sigmoid(gb_ref[...].astype(F32) + b_ref[:, D_MODEL:])
        z_ref[...] = (sa * ya_ref[...].astype(F32) + sb * yb_ref[...].astype(F32)).astype(BF16)

    return pl.pallas_call(
        body, name=name, grid=(t // TOK,),
        in_specs=[_row_spec(D_MODEL, 2), _row_spec(D_MODEL, 3), _vec_spec(2 * D_MODEL),
                  _row_spec(D_MODEL), _row_spec(D_MODEL)],
        out_specs=_row_spec(D_MODEL), out_shape=_sds((t, D_MODEL), BF16),
        compiler_params=_cparams("parallel"))(proj, proj, b_gate, ya, yb)


def _gate_bwd(name, dz, proj, b_gate, ya, yb):
    t = proj.shape[0]

    def body(dz_ref, ga_ref, gb_ref, b_ref, ya_ref, yb_ref, dya_ref, dyb_ref, dg_ref, db_ref):
        dzv = dz_ref[...].astype(F32)
        sa = jax.nn.sigmoid(ga_ref[...].astype(F32) + b_ref[:, :D_MODEL])
        sb = jax.nn.sigmoid(gb_ref[...].astype(F32) + b_ref[:, D_MODEL:])
        dya_ref[...] = (dzv * sa).astype(BF16)
        dyb_ref[...] = (dzv * sb).astype(BF16)
        dga = dzv * ya_ref[...].astype(F32) * sa * (1.0 - sa)
        dgb = dzv * yb_ref[...].astype(F32) * sb * (1.0 - sb)
        dg_ref[:, :D_MODEL] = dga.astype(BF16)
        dg_ref[:, D_MODEL:] = dgb.astype(BF16)
        pa = jnp.sum(dga, axis=0, keepdims=True)
        pb = jnp.sum(dgb, axis=0, keepdims=True)

        @pl.when(pl.program_id(0) == 0)
        def _():
            db_ref[:, :D_MODEL] = pa
            db_ref[:, D_MODEL:] = pb

        @pl.when(pl.program_id(0) > 0)
        def _():
            db_ref[:, :D_MODEL] += pa
            db_ref[:, D_MODEL:] += pb

    return pl.pallas_call(
        body, name=name, grid=(t // TOK,),
        in_specs=[_row_spec(D_MODEL), _row_spec(D_MODEL, 2), _row_spec(D_MODEL, 3),
                  _vec_spec(2 * D_MODEL), _row_spec(D_MODEL), _row_spec(D_MODEL)],
        out_specs=[_row_spec(D_MODEL), _row_spec(D_MODEL), _row_spec(2 * D_MODEL),
                   _vec_spec(2 * D_MODEL)],
        out_shape=[_sds((t, D_MODEL), BF16), _sds((t, D_MODEL), BF16),
                   _sds((t, 2 * D_MODEL), BF16), _sds((1, 2 * D_MODEL), F32)],
        compiler_params=_cparams("arbitrary"))(dz, proj, proj, b_gate, ya, yb)


def _head_masks():
    lane = lax.broadcasted_iota(jnp.int32, (1, 2 * HEAD_DIM), 1)
    return lane < HEAD_DIM


BAND_ROWS = 2 * ATT_BLK + CHUNK


def _fill_band(band, prev_ref, cur_ref):
    band[0:ATT_BLK, :] = prev_ref[...]
    band[ATT_BLK:2 * ATT_BLK, :] = cur_ref[...]
    band[2 * ATT_BLK:, :] = jnp.zeros((CHUNK, ATTN_W), BF16)


def _pair_rows(x2, low):
    zero = jnp.zeros_like(x2)
    return jnp.concatenate([jnp.where(low, x2, zero), jnp.where(low, zero, x2)], axis=0)


def _pair_diag(o2, low):
    return jnp.where(low, o2[0:CHUNK, :], o2[CHUNK:, :])


N_PAIRS = HEADS // 2
SM_STRIP = 32
N_STRIPS = BAND_PAD // SM_STRIP
NEG = -1e30


def _fold8(x, op):
    return op(op(x[0:8], x[8:16]), op(x[16:24], x[24:32]))


def _strip(k):
    return pl.ds(pl.multiple_of(k * SM_STRIP, SM_STRIP), SM_STRIP)


def _band_probs(k2, qcat, bias_t, first_key):
    kpos = lax.broadcasted_iota(jnp.int32, (BAND_PAD, 1), 0)
    st = lax.dot_general(k2, qcat, (NT, ((), ())), preferred_element_type=F32)
    st = jnp.where(kpos + first_key >= 0, st + bias_t, NEG)
    e = jnp.exp(st - jnp.max(st, axis=0, keepdims=True))
    return e * (1.0 / jnp.sum(e, axis=0, keepdims=True))


def _band_softmax_stats(st_ref, b_ref, first_key, dp_ref):
    rowi = lax.broadcasted_iota(jnp.int32, (SM_STRIP, 128), 0)

    def scores(k, mx):
        rows = _strip(k)
        live = (rowi + (k * SM_STRIP + first_key)) >= 0
        out = []
        for hp in range(N_PAIRS):
            x = jnp.where(live, st_ref[hp, rows, :] + b_ref[hp, rows, :], NEG)
            st_ref[hp, rows, :] = x
            out.append(jnp.maximum(mx[hp], _fold8(x, jnp.maximum)))
        return tuple(out)

    mx = lax.fori_loop(0, N_STRIPS, scores, (jnp.full((8, 128), NEG, F32),) * N_PAIRS, unroll=2)
    top = [jnp.max(m, axis=0, keepdims=True) for m in mx]

    def sums(k, acc):
        rows = _strip(k)
        ls, eds = [], []
        for hp in range(N_PAIRS):
            e = jnp.exp(st_ref[hp, rows, :] - top[hp])
            ls.append(acc[hp] + _fold8(e, jnp.add))
            eds.append(acc[N_PAIRS + hp] + _fold8(e * dp_ref[hp, rows, :], jnp.add))
        return tuple(ls + eds)

    acc = lax.fori_loop(0, N_STRIPS, sums, (jnp.zeros((8, 128), F32),) * (2 * N_PAIRS), unroll=2)
    inv = [1.0 / jnp.sum(a, axis=0, keepdims=True) for a in acc[:N_PAIRS]]
    delta = [jnp.sum(a, axis=0, keepdims=True) * i for a, i in zip(acc[N_PAIRS:], inv)]
    return top, inv, delta


def _attn_specs(nblk):
    cur = lambda col: pl.BlockSpec((ATT_BLK, ATTN_W), lambda s: (jnp.minimum(s, nblk - 1), col))
    prev = lambda col: pl.BlockSpec(
        (ATT_BLK, ATTN_W), lambda s: (jnp.maximum(jnp.minimum(s, nblk - 1) - 1, 0), col))
    return cur, prev


def _attn_fwd(name, proj, bias, comm=None):
    t = proj.shape[0]
    nblk = t // ATT_BLK
    cur, prev = _attn_specs(nblk)

    def body(q_ref, kp_ref, kc_ref, vp_ref, vc_ref, b_ref, o_ref, kband, vband):
        s = pl.program_id(0)
        _fill_band(kband, kp_ref, kc_ref)
        _fill_band(vband, vp_ref, vc_ref)
        low = _head_masks()

        def chunk(ci, carry):
            r0 = pl.multiple_of(ci * CHUNK, CHUNK)
            for hp in range(N_PAIRS):
                cols = slice(hp * 128, (hp + 1) * 128)
                qcat = _pair_rows(q_ref[pl.ds(r0, CHUNK), cols] * ATTN_SCALE, low)
                p = _band_probs(kband[pl.ds(r0, BAND_PAD), cols], qcat, b_ref[hp],
                                (s * 8 - 8 + ci) * CHUNK)
                o2 = lax.dot_general(p.astype(BF16), vband[pl.ds(r0, BAND_PAD), cols],
                                     (TN, ((), ())), preferred_element_type=F32)
                o_ref[pl.ds(r0, CHUNK), cols] = _pair_diag(o2, low).astype(BF16)
            return carry

        lax.fori_loop(0, 8, chunk, 0)

    out, moved = _pcall(
        body, name, (nblk,),
        [cur(0), prev(1), cur(1), prev(2), cur(2),
         pl.BlockSpec((N_PAIRS, BAND_PAD, 128), lambda s: (0, 0, 0))],
        pl.BlockSpec((ATT_BLK, ATTN_W), lambda s: (s, 0)), _sds((t, ATTN_W), BF16),
        [pltpu.VMEM((BAND_ROWS, ATTN_W), BF16), pltpu.VMEM((BAND_ROWS, ATTN_W), BF16)],
        _cparams("arbitrary"), (proj, proj, proj, proj, proj, bias), comm)
    return out if comm is None else (out, moved)


def _attn_bwd(name, proj, datt, bias, comm=None):
    t = proj.shape[0]
    nblk = t // ATT_BLK
    cur, prev = _attn_specs(nblk)
    late = pl.BlockSpec((ATT_BLK, ATTN_W), lambda s: (jnp.maximum(s - 1, 0), 0))

    def body(q_ref, kp_ref, kc_ref, vp_ref, vc_ref, do_ref, b_ref,
             dq_ref, dk_ref, dv_ref, db_ref, kband, vband, dkacc, dvacc,
             st_ref, dp_ref, pb_ref, dsb_ref, qc_ref, dc_ref):
        s = pl.program_id(0)

        @pl.when(s == 0)
        def _():
            dkacc[...] = jnp.zeros_like(dkacc)
            dvacc[...] = jnp.zeros_like(dvacc)
            db_ref[...] = jnp.zeros_like(db_ref)

        @pl.when(s < nblk)
        def _():
            _fill_band(kband, kp_ref, kc_ref)
            _fill_band(vband, vp_ref, vc_ref)
            low = _head_masks()

            def chunk(ci, carry):
                r0 = pl.multiple_of(ci * CHUNK, CHUNK)
                for hp in range(N_PAIRS):
                    cols = slice(hp * 128, (hp + 1) * 128)
                    qc_ref[hp] = _pair_rows(q_ref[pl.ds(r0, CHUNK), cols] * ATTN_SCALE, low)
                    dc_ref[hp] = _pair_rows(do_ref[pl.ds(r0, CHUNK), cols], low)
                    st_ref[hp] = lax.dot_general(kband[pl.ds(r0, BAND_PAD), cols], qc_ref[hp],
                                                 (NT, ((), ())), preferred_element_type=F32)
                    dp_ref[hp] = lax.dot_general(vband[pl.ds(r0, BAND_PAD), cols], dc_ref[hp],
                                                 (NT, ((), ())), preferred_element_type=F32)
                top, inv, delta = _band_softmax_stats(st_ref, b_ref, (s * 8 - 8 + ci) * CHUNK,
                                                      dp_ref)

                def grads(k, c):
                    rows = _strip(k)
                    for hp in range(N_PAIRS):
                        p = jnp.exp(st_ref[hp, rows, :] - top[hp]) * inv[hp]
                        ds = p * (dp_ref[hp, rows, :] - delta[hp])
                        db_ref[hp, rows, :] += ds
                        dsb_ref[hp, rows, :] = ds.astype(BF16)
                        pb_ref[hp, rows, :] = p.astype(BF16)
                    return c

                lax.fori_loop(0, N_STRIPS, grads, 0, unroll=2)
                for hp in range(N_PAIRS):
                    cols = slice(hp * 128, (hp + 1) * 128)
                    dq2 = lax.dot_general(dsb_ref[hp], kband[pl.ds(r0, BAND_PAD), cols],
                                          (TN, ((), ())), preferred_element_type=F32)
                    dq_ref[pl.ds(r0, CHUNK), cols] = (_pair_diag(dq2, low) * ATTN_SCALE).astype(BF16)
                    dkacc[pl.ds(r0, BAND_PAD), cols] += jnp.dot(dsb_ref[hp], qc_ref[hp],
                                                               preferred_element_type=F32)
                    dvacc[pl.ds(r0, BAND_PAD), cols] += jnp.dot(pb_ref[hp], dc_ref[hp],
                                                               preferred_element_type=F32)
                return carry

            lax.fori_loop(0, 8, chunk, 0)

        dk_ref[...] = dkacc[0:ATT_BLK, :].astype(BF16)
        dv_ref[...] = dvacc[0:ATT_BLK, :].astype(BF16)
        dkacc[0:ATT_BLK, :] = dkacc[ATT_BLK:2 * ATT_BLK, :]
        dvacc[0:ATT_BLK, :] = dvacc[ATT_BLK:2 * ATT_BLK, :]
        dkacc[ATT_BLK:, :] = jnp.zeros((ATT_BLK + CHUNK, ATTN_W), F32)
        dvacc[ATT_BLK:, :] = jnp.zeros((ATT_BLK + CHUNK, ATTN_W), F32)

    blk = _sds((t, ATTN_W), BF16)
    outs, moved = _pcall(
        body, name, (nblk + 1,),
        [cur(0), prev(1), cur(1), prev(2), cur(2),
         pl.BlockSpec((ATT_BLK, ATTN_W), lambda s: (jnp.minimum(s, nblk - 1), 0)),
         pl.BlockSpec((HEADS // 2, BAND_PAD, 128), lambda s: (0, 0, 0))],
        [pl.BlockSpec((ATT_BLK, ATTN_W), lambda s: (jnp.minimum(s, nblk - 1), 0)), late, late,
         pl.BlockSpec((HEADS // 2, BAND_PAD, 128), lambda s: (0, 0, 0))],
        [blk, blk, blk, _sds((HEADS // 2, BAND_PAD, 128), F32)],
        [pltpu.VMEM((BAND_ROWS, ATTN_W), BF16), pltpu.VMEM((BAND_ROWS, ATTN_W), BF16),
         pltpu.VMEM((BAND_ROWS, ATTN_W), F32), pltpu.VMEM((BAND_ROWS, ATTN_W), F32),
         pltpu.VMEM((N_PAIRS, BAND_PAD, 128), F32), pltpu.VMEM((N_PAIRS, BAND_PAD, 128), F32),
         pltpu.VMEM((N_PAIRS, BAND_PAD, 128), BF16), pltpu.VMEM((N_PAIRS, BAND_PAD, 128), BF16),
         pltpu.VMEM((N_PAIRS, 2 * CHUNK, 128), BF16), pltpu.VMEM((N_PAIRS, 2 * CHUNK, 128), BF16)],
        _cparams("arbitrary"), (proj, proj, proj, proj, proj, datt, bias), comm)
    return outs if comm is None else (*outs, moved)


def _diag_onehot(rel_rows):
    d0 = lax.broadcasted_iota(jnp.int32, (BIAS_LANES, BIAS_LANES), 0)
    d1 = lax.broadcasted_iota(jnp.int32, (BIAS_LANES, BIAS_LANES), 1)
    m, n = (d0, d1) if rel_rows else (d1, d0)
    hit = (m == jnp.minimum(BAND - 1 + MAX_REL - n, 2 * MAX_REL)) & (n < BAND + CHUNK - 1)
    return jnp.where(hit, 1.0, 0.0).astype(F32)


def _bias_table(name, rel_bias_l):
    rel_pad = jnp.pad(rel_bias_l, ((0, 0), (0, BIAS_LANES - N_REL)))

    def body(r_ref, o_ref):
        diag = jnp.dot(r_ref[...], _diag_onehot(True), preferred_element_type=F32,
                       precision=lax.Precision.HIGHEST)
        rowid = lax.broadcasted_iota(jnp.int32, (8, BIAS_LANES), 0)
        lane = lax.broadcasted_iota(jnp.int32, (8, BIAS_LANES), 1)
        for h in range(HEADS):
            d8 = jnp.broadcast_to(diag[h:h + 1, :], (8, BIAS_LANES))
            slab0 = pltpu.roll(d8, BIAS_LANES - CHUNK + 1, axis=1)
            for b in range(1, 8):
                slab0 = jnp.where(rowid == b, pltpu.roll(d8, BIAS_LANES - CHUNK + 1 + b, axis=1),
                                  slab0)
            for a in range(8):
                slab = slab0 if a == 0 else pltpu.roll(slab0, 8 * a, axis=1)
                o_ref[h * CHUNK + 8 * a:h * CHUNK + 8 * a + 8, :] = jnp.where(lane < BAND, slab, NEG)

    tab = pl.pallas_call(
        body, name=name,
        in_specs=[pl.BlockSpec(memory_space=pltpu.VMEM)],
        out_specs=pl.BlockSpec(memory_space=pltpu.VMEM),
        out_shape=_sds((HEADS * CHUNK, BIAS_LANES), F32),
    )(rel_pad)
    tab = tab.reshape(HEADS // 2, 2, CHUNK, BIAS_LANES)
    return jnp.transpose(tab, (0, 3, 1, 2)).reshape(HEADS // 2, BIAS_LANES, 2 * CHUNK)


def _bias_fold(name, dbias_t):
    rows = HEADS * CHUNK
    dbias = jnp.transpose(dbias_t.reshape(HEADS // 2, BIAS_LANES, 2, CHUNK), (0, 2, 3, 1))

    def body(d_ref, o_ref):
        rowid = lax.broadcasted_iota(jnp.int32, (8, BIAS_LANES), 0)
        diags = []
        for h in range(HEADS):
            acc = d_ref[h * CHUNK + 56:h * CHUNK + 64, :]
            for a in range(7):
                slab = d_ref[h * CHUNK + 8 * a:h * CHUNK + 8 * a + 8, :]
                acc = acc + pltpu.roll(slab, 56 - 8 * a, axis=1)
            tot = jnp.where(rowid == 7, acc, 0.0)
            for b in range(7):
                tot = tot + jnp.where(rowid == b, pltpu.roll(acc, 7 - b, axis=1), 0.0)
            diags.append(jnp.sum(tot, axis=0, keepdims=True))
        diag = jnp.concatenate(diags, axis=0)
        o_ref[...] = jnp.dot(diag, _diag_onehot(False), preferred_element_type=F32,
                             precision=lax.Precision.HIGHEST)

    return pl.pallas_call(
        body, name=name,
        in_specs=[pl.BlockSpec(memory_space=pltpu.VMEM)],
        out_specs=pl.BlockSpec(memory_space=pltpu.VMEM),
        out_shape=_sds((HEADS, BIAS_LANES), F32),
    )(dbias.reshape(rows, BIAS_LANES))


def _inv_counts(i):
    trow = lax.broadcasted_iota(jnp.int32, (TOK + HALO, 1), 0) + i * TOK
    return [1.0 / jnp.minimum(trow + 1, w).astype(F32) for w in POOL_WINDOWS]


def _pool_fwd(name, proj, wg, scale, comm=None):
    t = proj.shape[0]
    hb = TOK // HALO

    def body(u_ref, up_ref, wg_ref, sc_ref, pooled_ref, mixed_ref, b0, b1, b2, b3):
        i = pl.program_id(0)
        halo = up_ref[...].astype(F32)
        b0[0:HALO, :] = jnp.where(i == 0, jnp.zeros_like(halo), halo)
        b0[HALO:, :] = u_ref[...].astype(F32)
        n = TOK + HALO
        b1[8:n, :] = b0[8:n, :] + b0[7:n - 1, :]
        b2[16:n, 128:] = b1[16:n, 128:] + b1[14:n - 2, 128:]
        b3[24:n, 256:] = b2[24:n, 256:] + b2[20:n - 4, 256:]
        wins = [b1[HALO:n, 0:128], b2[HALO:n, 128:256], b3[HALO:n, 256:384],
                b3[HALO:n, 384:512] + b3[HALO - 8:n - 8, 384:512]]
        inv = _inv_counts(i)
        for g in range(4):
            cols = slice(g * POOL_GD, (g + 1) * POOL_GD)
            pooled = (wins[g] * inv[g][0:TOK] - b0[HALO:n, cols]).astype(BF16)
            pooled_ref[:, cols] = pooled
            pre = jnp.dot(pooled, wg_ref[g], preferred_element_type=F32)
            mixed_ref[:, cols] = (pre * sc_ref[:, cols]).astype(BF16)

    buf = pltpu.VMEM((TOK + HALO, POOL_W), F32)
    outs, moved = _pcall(
        body, name, (t // TOK,),
        [_row_spec(POOL_W, 3),
         pl.BlockSpec((HALO, POOL_W), lambda i: (jnp.maximum(i * hb - 1, 0), 3)),
         pl.BlockSpec((4, POOL_GD, POOL_GD), lambda i: (0, 0, 0)), _vec_spec(POOL_W)],
        [_row_spec(POOL_W), _row_spec(POOL_W)],
        [_sds((t, POOL_W), BF16), _sds((t, POOL_W), BF16)], [buf, buf, buf, buf],
        _cparams("arbitrary"), (proj, proj, wg, scale), comm)
    return outs if comm is None else (*outs, moved)


def _pool_bwd(name, dmixed, pooled, wg, scale, comm=None):
    t = dmixed.shape[0]
    nt = t // TOK
    hb = TOK // HALO

    def body(dm_ref, dmn_ref, p_ref, wg_ref, sc_ref, du_ref, dwg_ref, dsc_ref, c0, c1, c2, c3):
        i = pl.program_id(0)

        @pl.when(i == 0)
        def _():
            dwg_ref[...] = jnp.zeros_like(dwg_ref)
            dsc_ref[...] = jnp.zeros_like(dsc_ref)

        n = TOK + HALO
        inv = _inv_counts(i)
        dmv = dm_ref[...].astype(F32)
        dmn = dmn_ref[...].astype(F32)
        dmn = jnp.where(i == nt - 1, jnp.zeros_like(dmn), dmn)
        for g in range(4):
            cols = slice(g * POOL_GD, (g + 1) * POOL_GD)
            scg = sc_ref[:, cols]
            pg = p_ref[:, cols]
            dpre = (dmv[:, cols] * scg).astype(BF16)
            dpre_n = (dmn[:, cols] * scg).astype(BF16)
            pre = jnp.dot(pg, wg_ref[g], preferred_element_type=F32)
            dsc_ref[:, cols] += jnp.sum(dmv[:, cols] * pre, axis=0, keepdims=True)
            dwg_ref[g] += lax.dot_general(pg, dpre, (TN, ((), ())), preferred_element_type=F32)
            dpool = lax.dot_general(dpre, wg_ref[g], (NT, ((), ())), preferred_element_type=F32)
            dpool_n = lax.dot_general(dpre_n, wg_ref[g], (NT, ((), ())),
                                      preferred_element_type=F32)
            c0[0:TOK, cols] = dpool
            c0[TOK:n, cols] = dpool_n
            c1[0:TOK, cols] = dpool * inv[g][0:TOK]
            c1[TOK:n, cols] = dpool_n * inv[g][TOK:n]
        c2[0:n - 8, :] = c1[0:n - 8, :] + c1[1:n - 7, :]
        c3[0:n - 16, 128:] = c2[0:n - 16, 128:] + c2[2:n - 14, 128:]
        c1[0:n - 24, 256:] = c3[0:n - 24, 256:] + c3[4:n - 20, 256:]
        wins = [c2[0:TOK, 0:128], c3[0:TOK, 128:256], c1[0:TOK, 256:384],
                c1[0:TOK, 384:512] + c1[8:TOK + 8, 384:512]]
        for g in range(4):
            cols = slice(g * POOL_GD, (g + 1) * POOL_GD)
            du_ref[:, cols] = (wins[g] - c0[0:TOK, cols]).astype(BF16)

    buf = pltpu.VMEM((TOK + HALO, POOL_W), F32)
    outs, moved = _pcall(
        body, name, (nt,),
        [_row_spec(POOL_W),
         pl.BlockSpec((HALO, POOL_W), lambda i: (jnp.minimum((i + 1) * hb, nt * hb - 1), 0)),
         _row_spec(POOL_W), pl.BlockSpec((4, POOL_GD, POOL_GD), lambda i: (0, 0, 0)),
         _vec_spec(POOL_W)],
        [_row_spec(POOL_W), pl.BlockSpec((4, POOL_GD, POOL_GD), lambda i: (0, 0, 0)),
         _vec_spec(POOL_W)],
        [_sds((t, POOL_W), BF16), _sds((4, POOL_GD, POOL_GD), F32), _sds((1, POOL_W), F32)],
        [buf, buf, buf, buf], _cparams("arbitrary"), (dmixed, dmixed, pooled, wg, scale), comm)
    return outs if comm is None else (*outs, moved)


GELU_C = math.sqrt(2.0 / math.pi)


GELU_K = 0.044715


def _gelu_parts(x):
    x2 = x * x
    s = 0.5 + 0.5 * jnp.tanh(x * (GELU_C + (GELU_C * GELU_K) * x2))
    return x * s, s, x2


def _gelu(x):
    return _gelu_parts(x)[0]


def _gelu_and_grad(x):
    g, s, x2 = _gelu_parts(x)
    return g, s + g * (1.0 - s) * ((2 * GELU_C) + (6 * GELU_C * GELU_K) * x2)


def _taps(buf, r, rows):
    a = buf[pl.ds(r, rows + 8), :]
    return a[8:], pltpu.roll(a, 1, axis=0)[8:], pltpu.roll(a, 2, axis=0)[8:]


def _conv(taps, w_ref, b_ref):
    return b_ref[...] + w_ref[2:3, :] * taps[0] + w_ref[1:2, :] * taps[1] + w_ref[0:1, :] * taps[2]


def _stage(dst, prev_ref, cur_ref, next_ref, first, last):
    rows = cur_ref.shape[0]
    h = prev_ref[...].astype(F32)
    dst[0:8, :] = jnp.where(first, jnp.zeros_like(h), h)
    dst[8:8 + rows, :] = cur_ref[...].astype(F32)
    if next_ref is not None:
        h = next_ref[...].astype(F32)
        dst[8 + rows:, :] = jnp.where(last, jnp.zeros_like(h), h)


FWD_STRIP = 32
BWD_STRIP = 16


def _ffn_gate_fwd(name, hu, conv_w, conv_b, comm=None):
    t = hu.shape[0]
    ncol = D_FF // FF_COL
    hb = TOK // 8

    def tile(off):
        return pl.BlockSpec((TOK, FF_COL), lambda i, j: (i, j + off))

    def halo(off):
        return pl.BlockSpec((8, FF_COL), lambda i, j: (jnp.maximum(i * hb - 1, 0), j + off))

    def wspec(off):
        return pl.BlockSpec((3, FF_COL), lambda i, j: (0, j + off))

    def bspec(off):
        return pl.BlockSpec((1, FF_COL), lambda i, j: (0, j + off))

    def body(v_ref, vp_ref, g_ref, gp_ref, wv_ref, wg_ref, bv_ref, bg_ref, a_ref, vb, gb):
        first = pl.program_id(0) == 0
        _stage(vb, vp_ref, v_ref, None, first, None)
        _stage(gb, gp_ref, g_ref, None, first, None)

        def strip(k, carry):
            r = pl.multiple_of(k * FWD_STRIP, FWD_STRIP)
            val = _conv(_taps(vb, r, FWD_STRIP), wv_ref, bv_ref)
            gate = _conv(_taps(gb, r, FWD_STRIP), wg_ref, bg_ref)
            a_ref[pl.ds(r, FWD_STRIP), :] = (_gelu(gate) * val).astype(BF16)
            return carry

        lax.fori_loop(0, TOK // FWD_STRIP, strip, 0)

    buf = pltpu.VMEM((TOK + 8, FF_COL), F32)
    out, moved = _pcall(
        body, name, (t // TOK, ncol),
        [tile(0), halo(0), tile(ncol), halo(ncol), wspec(0), wspec(ncol), bspec(0), bspec(ncol)],
        pl.BlockSpec((TOK, FF_COL), lambda i, j: (i, j)), _sds((t, D_FF), BF16), [buf, buf],
        _cparams("arbitrary", "arbitrary"),
        (hu, hu, hu, hu, conv_w, conv_w, conv_b, conv_b), comm)
    return out if comm is None else (out, moved)


def _ffn_gate_bwd(name, da, hu, conv_w, conv_b, comm=None):
    t = hu.shape[0]
    nt = t // TOK
    ncol = D_FF // FF_COL
    hb = TOK // 8
    ext = TOK + 8

    def tile(off):
        return pl.BlockSpec((TOK, FF_COL), lambda j, i: (i, j + off))

    def prev(off):
        return pl.BlockSpec((8, FF_COL), lambda j, i: (jnp.maximum(i * hb - 1, 0), j + off))

    def nxt(off):
        return pl.BlockSpec((8, FF_COL), lambda j, i: (jnp.minimum((i + 1) * hb, nt * hb - 1), j + off))

    def wspec(off):
        return pl.BlockSpec((3, FF_COL), lambda j, i: (0, j + off))

    def bspec(off):
        return pl.BlockSpec((1, FF_COL), lambda j, i: (0, j + off))

    def body(da_ref, dan_ref, v_ref, vp_ref, vn_ref, g_ref, gp_ref, gn_ref,
             wv_ref, wg_ref, bv_ref, bg_ref, dh_ref, dwv_ref, dwg_ref, vb, gb, dab):
        i = pl.program_id(1)
        first, last = i == 0, i == nt - 1

        @pl.when(first)
        def _():
            dwv_ref[...] = jnp.zeros_like(dwv_ref)
            dwg_ref[...] = jnp.zeros_like(dwg_ref)

        _stage(vb, vp_ref, v_ref, vn_ref, first, last)
        _stage(gb, gp_ref, g_ref, gn_ref, first, last)
        dab[0:TOK, :] = da_ref[...].astype(F32)
        h = dan_ref[...].astype(F32)
        dab[TOK:, :] = jnp.where(last, jnp.zeros_like(h), h)

        def grads(r, rows):
            tv, tg = _taps(vb, r, rows), _taps(gb, r, rows)
            gate = _conv(tg, wg_ref, bg_ref)
            dav = dab[pl.ds(r, rows), :]
            g, dg = _gelu_and_grad(gate)
            dval = dav * g
            dgate = dav * _conv(tv, wv_ref, bv_ref) * dg
            return dval, dgate, tv, tg

        def fold(x):
            return x[0:8] + x[8:16]

        def strip(k, carry):
            r = pl.multiple_of(TOK - BWD_STRIP - k * BWD_STRIP, BWD_STRIP)
            dval, dgate, tv, tg = grads(r, BWD_STRIP)
            new = (dval[0:8], dgate[0:8])
            for half, (d, nxt_rows, taps, w_ref, dw_ref) in enumerate((
                    (dval, carry[0], tv, wv_ref, dwv_ref), (dgate, carry[1], tg, wg_ref, dwg_ref))):
                e = jnp.concatenate([d, nxt_rows], axis=0)
                dh = (w_ref[2:3, :] * d
                      + w_ref[1:2, :] * pltpu.roll(e, BWD_STRIP + 7, axis=0)[0:BWD_STRIP]
                      + w_ref[0:1, :] * pltpu.roll(e, BWD_STRIP + 6, axis=0)[0:BWD_STRIP])
                dh_ref[half, pl.ds(r, BWD_STRIP), :] = dh.astype(BF16)
                dw_ref[0:8, :] += fold(d * taps[2])
                dw_ref[8:16, :] += fold(d * taps[1])
                dw_ref[16:24, :] += fold(d * taps[0])
                dw_ref[24:32, :] += fold(d)
            return new

        dval, dgate, _, _ = grads(TOK, 8)
        lax.fori_loop(0, TOK // BWD_STRIP, strip, (dval, dgate))

        @pl.when(last)
        def _():
            for dw_ref in (dwv_ref, dwg_ref):
                for q in range(4):
                    dw_ref[8 * q:8 * q + 1, :] = jnp.sum(dw_ref[8 * q:8 * q + 8, :], axis=0,
                                                         keepdims=True)

    hbuf = pltpu.VMEM((TOK + 16, FF_COL), F32)
    acc = pl.BlockSpec((32, FF_COL), lambda j, i: (0, j))
    (dhu, dwv, dwg), moved = _pcall(
        body, name, (ncol, nt),
        [tile(0), nxt(0), tile(0), prev(0), nxt(0), tile(ncol), prev(ncol), nxt(ncol),
         wspec(0), wspec(ncol), bspec(0), bspec(ncol)],
        [pl.BlockSpec((2, TOK, FF_COL), lambda j, i: (0, i, j)), acc, acc],
        [_sds((2, t, D_FF), BF16), _sds((32, D_FF), F32), _sds((32, D_FF), F32)],
        [hbuf, hbuf, pltpu.VMEM((ext, FF_COL), F32)], _cparams("arbitrary", "arbitrary"),
        (da, da, hu, hu, hu, hu, hu, hu, conv_w, conv_w, conv_b, conv_b), comm)
    dconv = jnp.concatenate([dwv, dwg], axis=1).reshape(4, 8, 2 * D_FF)[:, 0]
    return (dhu, dconv) if comm is None else (dhu, dconv, moved)


def _mesh_pos():
    x, y, c = lax.axis_index("x"), lax.axis_index("y"), lax.axis_index("c")
    return x, y, c, [(1 - x, y), (x, 1 - y), (1 - x, 1 - y)]


def _any_specs(n):
    return [pl.BlockSpec(memory_space=pl.ANY)] * n


def _remote(src, dst, send_sems, recv_sems, i, dev):
    return pltpu.make_async_remote_copy(src_ref=src, dst_ref=dst, send_sem=send_sems.at[i],
                                        recv_sem=recv_sems.at[i], device_id=dev,
                                        device_id_type=MESH)


def _mine(c, rows):
    return pl.ds(pl.multiple_of(c * (rows // 2), 16), rows // 2)


def _gather_send(shards, conv_shard, gathered, l):
    nbig = len(shards)
    with_conv = conv_shard is not None
    if gathered is None:
        ins = list(shards) + ([conv_shard] if with_conv else [])
        outs = [_sds((DEPTH, N_CHIPS) + s.shape[1:], s.dtype) for s in ins]
        alias = {}
    else:
        ins = list(shards) + list(gathered)
        outs = [_sds(g.shape, g.dtype) for g in gathered]
        alias = {nbig + k: k for k in range(nbig)}

    def copies(cin, cout, ssem, rsem):
        x, y, c, chips = _mesh_pos()
        me = 2 * x + y
        out = []
        for k in range(nbig):
            rows = shards[k].shape[1]
            for j, (cx, cy) in enumerate(chips):
                out.append(_remote(cin[k].at[l, _mine(c, rows)], cout[k].at[l, me, _mine(c, rows)],
                                   ssem, rsem, 4 * k + j, (cx, cy, c)))
            out.append(_remote(cin[k].at[l], cout[k].at[l, me], ssem, rsem, 4 * k + 3,
                               (x, y, 1 - c)))
        if with_conv:
            base = 4 * nbig
            for j, (cx, cy) in enumerate(chips):
                out.append(_remote(cin[nbig].at[c], cout[nbig].at[c, me], ssem, rsem, base + j,
                                   (cx, cy, c)))
            for ll in range(DEPTH):
                out.append(_remote(cin[nbig].at[ll], cout[nbig].at[ll, me], ssem, rsem,
                                   base + 3 + ll, (x, y, 1 - c)))
        return out

    return _Comm(ins, outs, copies, 4 * nbig + 5, alias)


def _gather_forward(gathered, nbig, rows, l):
    with_conv = len(gathered) > nbig
    alias = {k: k for k in range(len(gathered))}

    def copies(cin, cout, ssem, rsem):
        x, y, c, chips = _mesh_pos()
        out = []
        for k in range(nbig):
            for j, (cx, cy) in enumerate(chips):
                blk = cout[k].at[l, 2 * cx + cy, _mine(c, rows[k])]
                out.append(_remote(blk, blk, ssem, rsem, 3 * k + j, (x, y, 1 - c)))
        if with_conv:
            for j, (cx, cy) in enumerate(chips):
                blk = cout[nbig].at[c, 2 * cx + cy]
                out.append(_remote(blk, blk, ssem, rsem, 3 * nbig + j, (x, y, 1 - c)))
        return out

    return _Comm(gathered, [_sds(g.shape, g.dtype) for g in gathered], copies, 3 * nbig + 3, alias)


def _reduce_swap(grads, l):
    def copies(cin, cout, ssem, rsem):
        x, y, c, _ = _mesh_pos()
        return [_remote(cin[k].at[l, :, _mine(1 - c, g.shape[2])], cout[k], ssem, rsem, k,
                        (x, y, 1 - c)) for k, g in enumerate(grads)]

    outs = [_sds((N_CHIPS, g.shape[2] // 2, g.shape[3]), g.dtype) for g in grads]
    return _Comm(grads, outs, copies, len(grads))


def _reduce_scatter(sums):
    def copies(cin, cout, ssem, rsem):
        x, y, c, chips = _mesh_pos()
        return [_remote(cin[k].at[2 * cx + cy], cout[k].at[j], ssem, rsem, 3 * k + j, (cx, cy, c))
                for k in range(len(sums)) for j, (cx, cy) in enumerate(chips)]

    outs = [_sds((3,) + s.shape[1:], s.dtype) for s in sums]
    return _Comm(sums, outs, copies, 3 * len(sums))


def _reduce_share(reds, l):
    def copies(cin, cout, ssem, rsem):
        x, y, c, _ = _mesh_pos()
        out = []
        for k, r in enumerate(reds):
            half = cout[k].at[l, _mine(c, r.shape[1])]
            out.append(_remote(half, half, ssem, rsem, k, (x, y, 1 - c)))
        return out

    return _Comm(reds, [_sds(r.shape, r.dtype) for r in reds], copies, len(reds),
                 {k: k for k in range(len(reds))})


def _allgather_weights(shards):
    n = len(shards)

    def body(*refs):
        ins, outs = refs[:n], refs[n:2 * n]
        send_sems, recv_sems = refs[2 * n:]
        x, y, c, chips = _mesh_pos()
        me = 2 * x + y
        started = []
        own = []
        for k in range(n):
            for l in range(2):
                cp = pltpu.make_async_remote_copy(
                    src_ref=ins[k].at[l], dst_ref=outs[k].at[l, me],
                    send_sem=send_sems.at[k, 6 + l], recv_sem=recv_sems.at[k, 6 + l],
                    device_id=(x, y, 1 - c), device_id_type=MESH)
                cp.start()
                own.append(cp)
            for j, (cx, cy) in enumerate(chips):
                cp = pltpu.make_async_remote_copy(
                    src_ref=ins[k].at[c], dst_ref=outs[k].at[c, me],
                    send_sem=send_sems.at[k, j], recv_sem=recv_sems.at[k, j],
                    device_id=(cx, cy, c), device_id_type=MESH)
                cp.start()
                started.append(cp)
        for k in range(n):
            for j, (cx, cy) in enumerate(chips):
                landed = outs[k].at[c, 2 * cx + cy]
                pltpu.make_async_remote_copy(
                    src_ref=ins[k].at[c], dst_ref=landed,
                    send_sem=send_sems.at[k, j], recv_sem=recv_sems.at[k, j],
                    device_id=(cx, cy, c), device_id_type=MESH).wait_recv()
                fw = pltpu.make_async_remote_copy(
                    src_ref=landed, dst_ref=landed,
                    send_sem=send_sems.at[k, 3 + j], recv_sem=recv_sems.at[k, 3 + j],
                    device_id=(x, y, 1 - c), device_id_type=MESH)
                fw.start()
                started.append(fw)
        for k in range(n):
            for j, (cx, cy) in enumerate(chips):
                theirs = outs[k].at[1 - c, 2 * cx + cy]
                pltpu.make_async_remote_copy(
                    src_ref=theirs, dst_ref=theirs,
                    send_sem=send_sems.at[k, 3 + j], recv_sem=recv_sems.at[k, 3 + j],
                    device_id=(x, y, 1 - c), device_id_type=MESH).wait_recv()
        for cp in started:
            cp.wait_send()
        for cp in own:
            cp.wait()

    return pl.pallas_call(
        body, name="allgather_weights",
        in_specs=_any_specs(n), out_specs=_any_specs(n),
        out_shape=[_sds((2, N_CHIPS) + s.shape[1:], s.dtype) for s in shards],
        scratch_shapes=[pltpu.SemaphoreType.DMA((n, 8)), pltpu.SemaphoreType.DMA((n, 8))],
    )(*shards)


def _swap_layers(grads):
    n = len(grads)

    def body(*refs):
        ins, outs = refs[:n], refs[n:2 * n]
        send_sems, recv_sems = refs[2 * n:]
        x, y, c, _ = _mesh_pos()
        cps = []
        for k in range(n):
            cp = pltpu.make_async_remote_copy(
                src_ref=ins[k].at[1 - c], dst_ref=outs[k],
                send_sem=send_sems.at[k], recv_sem=recv_sems.at[k],
                device_id=(x, y, 1 - c), device_id_type=MESH)
            cp.start()
            cps.append(cp)
        for cp in cps:
            cp.wait()

    return pl.pallas_call(
        body, name="swap_layers",
        in_specs=_any_specs(n), out_specs=_any_specs(n),
        out_shape=[_sds(g.shape[1:], g.dtype) for g in grads],
        scratch_shapes=[pltpu.SemaphoreType.DMA((n,)), pltpu.SemaphoreType.DMA((n,))],
    )(*grads)


def _scatter_blocks(sums):
    n = len(sums)

    def body(*refs):
        ins, outs = refs[:n], refs[n:2 * n]
        send_sems, recv_sems = refs[2 * n:]
        x, y, c, chips = _mesh_pos()
        cps = []
        for k in range(n):
            for j, (cx, cy) in enumerate(chips):
                cp = pltpu.make_async_remote_copy(
                    src_ref=ins[k].at[2 * cx + cy], dst_ref=outs[k].at[j],
                    send_sem=send_sems.at[k, j], recv_sem=recv_sems.at[k, j],
                    device_id=(cx, cy, c), device_id_type=MESH)
                cp.start()
                cps.append(cp)
        for cp in cps:
            cp.wait()

    return pl.pallas_call(
        body, name="scatter_blocks",
        in_specs=_any_specs(n), out_specs=_any_specs(n),
        out_shape=[_sds((3,) + s.shape[1:], s.dtype) for s in sums],
        scratch_shapes=[pltpu.SemaphoreType.DMA((n, 3)), pltpu.SemaphoreType.DMA((n, 3))],
    )(*sums)


def _exchange_reduced(reds):
    n = len(reds)

    def body(*refs):
        outs = refs[n:2 * n]
        send_sems, recv_sems = refs[2 * n:]
        x, y, c, _ = _mesh_pos()
        cps = []
        for k in range(n):
            cp = pltpu.make_async_remote_copy(
                src_ref=outs[k].at[c], dst_ref=outs[k].at[c],
                send_sem=send_sems.at[k], recv_sem=recv_sems.at[k],
                device_id=(x, y, 1 - c), device_id_type=MESH)
            cp.start()
            cps.append(cp)
        for k in range(n):
            pltpu.make_async_remote_copy(
                src_ref=outs[k].at[c], dst_ref=outs[k].at[1 - c],
                send_sem=send_sems.at[k], recv_sem=recv_sems.at[k],
                device_id=(x, y, 1 - c), device_id_type=MESH).wait_recv()
        for cp in cps:
            cp.wait_send()

    return pl.pallas_call(
        body, name="exchange_reduced",
        in_specs=_any_specs(n), out_specs=_any_specs(n),
        out_shape=[_sds(r.shape, r.dtype) for r in reds],
        input_output_aliases={k: k for k in range(n)},
        scratch_shapes=[pltpu.SemaphoreType.DMA((n,)), pltpu.SemaphoreType.DMA((n,))],
    )(*reds)


def _allreduce_small(pack):
    n = pack.shape[0]

    def body(x_ref, o_ref, gbuf, send_sems, recv_sems):
        x, y, c, chips = _mesh_pos()
        sibling = (x, y, 1 - c)

        def slot(px, py, pc):
            return gbuf.at[4 * px + 2 * py + pc]

        def copy(k, block, to, src=None):
            return pltpu.make_async_remote_copy(
                src_ref=slot(*block) if src is None else src, dst_ref=slot(*block),
                send_sem=send_sems.at[k], recv_sem=recv_sems.at[k],
                device_id=to, device_id_type=MESH)

        me = (x, y, c)
        first = [copy(0, me, sibling, src=x_ref)]
        first += [copy(1 + j, me, (*chip, c), src=x_ref) for j, chip in enumerate(chips)]
        for cp in first:
            cp.start()
        gbuf[4 * x + 2 * y + c] = x_ref[...]
        passed = [copy(4 + j, (*chip, c), sibling) for j, chip in enumerate(chips)]
        for j, chip in enumerate(chips):
            copy(1 + j, (*chip, c), me).wait_recv()
            passed[j].start()
        copy(0, sibling, me).wait_recv()
        for j, chip in enumerate(chips):
            copy(4 + j, (*chip, 1 - c), me).wait_recv()
        for cp in first + passed:
            cp.wait_send()
        acc = gbuf[0]
        for d in range(1, 8):
            acc = acc + gbuf[d]
        o_ref[...] = acc

    return pl.pallas_call(
        body, name="allreduce_small",
        in_specs=[pl.BlockSpec(memory_space=pltpu.VMEM)],
        out_specs=pl.BlockSpec(memory_space=pltpu.VMEM),
        out_shape=_sds((n, 128), F32),
        scratch_shapes=[pltpu.VMEM((8, n, 128), F32), pltpu.SemaphoreType.DMA((7,)),
                        pltpu.SemaphoreType.DMA((7,))],
        compiler_params=pltpu.CompilerParams(vmem_limit_bytes=VMEM_LIMIT_V7X),
    )(pack)


def _core_index():
    return jnp.reshape(lax.axis_index("c"), (1,)).astype(jnp.int32)


def _chip_index():
    return jnp.reshape(2 * lax.axis_index("x") + lax.axis_index("y"), (1,)).astype(jnp.int32)


def _chip_sum(name, stacked, sib, l):
    _, nb, r, cdim = stacked.shape
    hr = r // 2

    def body(c_ref, a_ref, b_ref, o_ref):
        o_ref[...] = (a_ref[...].astype(F32) + b_ref[...].astype(F32)).astype(BF16)

    return pl.pallas_call(
        body, name=name,
        grid_spec=pltpu.PrefetchScalarGridSpec(
            num_scalar_prefetch=1, grid=(nb,),
            in_specs=[pl.BlockSpec((None, None, hr, cdim), lambda j, cr: (l, j, cr[0], 0)),
                      pl.BlockSpec((None, hr, cdim), lambda j, cr: (j, 0, 0))],
            out_specs=pl.BlockSpec((None, hr, cdim), lambda j, cr: (j, 0, 0))),
        out_shape=_sds((nb, hr, cdim), BF16),
        compiler_params=_cparams("parallel"))(_core_index(), stacked, sib)


def _final_sum(name, sums, recv, l, fill):
    _, hr, cdim = sums.shape
    tr = hr // 2

    def body(m_ref, a_ref, b_ref, *rest):
        acc = a_ref[...].astype(F32)
        for j in range(3):
            acc = acc + b_ref[j].astype(F32)
        rest[-1][...] = acc

    in_specs = [pl.BlockSpec((None, tr, cdim), lambda i, mr: (mr[0], i, 0)),
                pl.BlockSpec((3, tr, cdim), lambda i, mr: (0, i, 0))]
    args = [jnp.concatenate([_chip_index(), _core_index()]), sums, recv]
    aliases = {}
    if fill is not None:
        in_specs.append(pl.BlockSpec(memory_space=pl.ANY))
        args.append(fill)
        aliases = {3: 0}
    return pl.pallas_call(
        body, name=name,
        grid_spec=pltpu.PrefetchScalarGridSpec(
            num_scalar_prefetch=1, grid=(2,), in_specs=in_specs,
            out_specs=pl.BlockSpec((None, tr, cdim), lambda i, mr: (l, 2 * mr[1] + i, 0))),
        out_shape=_sds((DEPTH, 2 * hr, cdim), F32), input_output_aliases=aliases,
        compiler_params=_cparams("parallel"))(*args)


def _adamw(name, w, g, m, v):
    nl, r, cdim = w.shape
    tr = r // 4 if r % 32 == 0 else r
    c1 = 1.0 - ADAM_B1 ** ADAM_STEP
    c2 = 1.0 - ADAM_B2 ** ADAM_STEP

    def body(w_ref, g_ref, m_ref, v_ref, d_ref, nm_ref, nv_ref):
        gv = g_ref[...]
        nm = ADAM_B1 * m_ref[...] + (1.0 - ADAM_B1) * gv
        nv = ADAM_B2 * v_ref[...] + (1.0 - ADAM_B2) * (gv * gv)
        nm_ref[...] = nm
        nv_ref[...] = nv
        d_ref[...] = -ADAM_LR * ((nm / c1) / (jnp.sqrt(nv / c2) + ADAM_EPS) + ADAM_WD * w_ref[...])

    spec = pl.BlockSpec((None, tr, cdim), lambda l, i: (l, i, 0))
    out = _sds(w.shape, F32)
    return pl.pallas_call(
        body, name=name, grid=(nl, r // tr),
        in_specs=[spec] * 4, out_specs=[spec] * 3, out_shape=[out] * 3,
        compiler_params=_cparams("parallel", "parallel"))(w, g, m, v)


def _rows128(a):
    return a.reshape(-1, 128)


def kernel(x, norm_mix_pre, w_in, b_gate, rel_bias, w_attn_out, w_pool_group, pool_scale, w_pool_out, w_o, norm_mix_post, norm_ffn_pre, w_up, conv_w, conv_b, w_down, norm_ffn_post, loss_target, m_norm_mix_pre, m_w_in, m_b_gate, m_rel_bias, m_w_attn_out, m_w_pool_group, m_pool_scale, m_w_pool_out, m_w_o, m_norm_mix_post, m_norm_ffn_pre, m_w_up, m_conv_w, m_conv_b, m_w_down, m_norm_ffn_post, v_norm_mix_pre, v_w_in, v_b_gate, v_rel_bias, v_w_attn_out, v_w_pool_group, v_pool_scale, v_w_pool_out, v_w_o, v_norm_mix_post, v_norm_ffn_pre, v_w_up, v_conv_w, v_conv_b, v_w_down, v_norm_ffn_post):
    t = x.shape[1]
    xs = x.reshape(t, D_MODEL)
    target = loss_target.reshape(t, D_MODEL)

    names = ["w_in", "w_attn_out", "w_pool_out", "w_o", "w_up", "w_down"]
    shards = [w.astype(BF16) for w in (w_in, w_attn_out, w_pool_out, w_o, w_up, w_down)]
    rows = [s.shape[1] for s in shards]
    nbig = len(shards)
    g = _comm_call("gather0_send", _gather_send(shards[:1], conv_w, None, 0))
    g = _comm_call("gather0_forward", _gather_forward(g, 1, rows[:1], 0))
    cw_full = jnp.transpose(g[1], (0, 2, 1, 3)).reshape(DEPTH, 3, 2 * D_FF)
    g = g[:1]
    wg_bf = w_pool_group.astype(BF16)

    def views(gathered):
        win_g, wao_g, wpo_g, wo_g, wup_g, wdn_g = gathered
        return (win_g, wao_g, wpo_g, wo_g.reshape(DEPTH, D_MODEL, D_MODEL), wup_g,
                wdn_g.reshape(DEPTH, D_FF, D_MODEL))

    saved = []
    xcur = xs
    for l in range(DEPTH):
        tag = f"l{l}_"
        bias = _bias_table(tag + "bias_table", rel_bias[l])
        h = _norm_fwd(tag + "norm_mix_pre", xcur, norm_mix_pre[l:l + 1])
        proj = _mm_nn_blocked(tag + "proj", h, g[0], l, BF16)
        if l == 0:
            att, rest = _attn_fwd(tag + "attn_fwd", proj, bias,
                                  _gather_send(shards[1:], None, None, 0))
            pooled, mixed, rest = _pool_fwd(tag + "pool_fwd", proj, wg_bf[l], pool_scale[l:l + 1],
                                            _gather_forward(rest, nbig - 1, rows[1:], 0))
            g = g + rest
        else:
            att = _attn_fwd(tag + "attn_fwd", proj, bias)
            pooled, mixed = _pool_fwd(tag + "pool_fwd", proj, wg_bf[l], pool_scale[l:l + 1])
        win_g, wao_g, wpo_g, wo_full, wup_g, wdn_full = views(g)
        ya = _mm_nn_blocked(tag + "attn_out", att, wao_g, l, BF16)
        yb = _mm_nn_blocked(tag + "pool_out", mixed, wpo_g, l, BF16)
        z = _gate_fwd(tag + "gate_fwd", proj, b_gate[l:l + 1], ya, yb)
        mix = _mm_nn(tag + "mix", z, wo_full, l, D_MODEL, F32)
        x1 = _norm_residual_fwd(tag + "norm_mix_post", xcur, mix, norm_mix_post[l:l + 1])
        h2 = _norm_fwd(tag + "norm_ffn_pre", x1, norm_ffn_pre[l:l + 1])
        hu = _mm_nn_blocked(tag + "ffn_up", h2, wup_g, l, BF16)
        if l == 0:
            a, g = _ffn_gate_fwd(tag + "ffn_gate_fwd", hu, cw_full[l], conv_b[l:l + 1],
                                 _gather_send(shards, None, g, 1))
            wdn_full = views(g)[5]
        else:
            a = _ffn_gate_fwd(tag + "ffn_gate_fwd", hu, cw_full[l], conv_b[l:l + 1])
        f = _mm_nn(tag + "ffn_down", a, wdn_full, l, D_FF // 2, F32)
        if l == 0:
            x2, g = _norm_residual_fwd(tag + "norm_ffn_post", x1, f, norm_ffn_post[l:l + 1],
                                       _gather_forward(g, nbig, rows, 1))
        else:
            x2 = _norm_residual_fwd(tag + "norm_ffn_post", x1, f, norm_ffn_post[l:l + 1])
        saved.append(dict(x=xcur, h=h, proj=proj, att=att, pooled=pooled, mixed=mixed, ya=ya,
                          yb=yb, z=z, mix=mix, x1=x1, h2=h2, hu=hu, a=a, f=f, bias=bias))
        xcur = x2
    win_g, wao_g, wpo_g, wo_full, wup_g, wdn_full = views(g)

    dy, loss_local = _loss_head(xcur, target)
    loss = lax.psum(loss_local, ("x", "y", "c"))

    dx = dy
    dws = dict.fromkeys(names)
    reds = [None] * nbig
    small_grads = [None] * DEPTH
    ffn = [4, 5]
    outs3 = [1, 2, 3]

    def blocks(ks):
        return [dws[names[k]].reshape(DEPTH, N_CHIPS, rows[k], -1) for k in ks]

    def chip_sums(ks, sib, l):
        return [_chip_sum(f"chip_sum{l}_" + names[k], b, s, l)
                for k, b, s in zip(ks, blocks(ks), sib)]

    def final_sums(ks, sums, recv, l):
        for k, s, r in zip(ks, sums, recv):
            reds[k] = _final_sum(f"final_sum{l}_" + names[k], s, r, l, reds[k])

    for l in reversed(range(DEPTH)):
        tag = f"l{l}_"
        sv = saved[l]
        every = list(range(nbig))
        df, d_nfpost = _norm_post_bwd(tag + "norm_ffn_post_bwd", dx, sv["f"], norm_ffn_post[l:l + 1])
        if l == 0:
            da, sib = _mm_nt(tag + "ffn_down_dx", df, wdn_full, l, D_FF // 2, BF16,
                             _reduce_swap(blocks(every), 1))
            sums = chip_sums(every, sib, 1)
        else:
            da = _mm_nt(tag + "ffn_down_dx", df, wdn_full, l, D_FF // 2, BF16)
        dws["w_down"] = _mm_tn(tag + "ffn_down_dw", sv["a"], df, D_FF // 2, l, dws["w_down"])
        if l == 0:
            dhu, dconv, recv = _ffn_gate_bwd(tag + "ffn_gate_bwd", da, sv["hu"], cw_full[l],
                                             conv_b[l:l + 1], _reduce_scatter(sums))
            final_sums(every, sums, recv, 1)
            dh2, reds = _mm_nt_blocked(tag + "ffn_up_dx", dhu, wup_g, l, F32,
                                       _reduce_share(reds, 1))
        else:
            dhu, dconv = _ffn_gate_bwd(tag + "ffn_gate_bwd", da, sv["hu"], cw_full[l],
                                       conv_b[l:l + 1])
            dh2 = _mm_nt_blocked(tag + "ffn_up_dx", dhu, wup_g, l, F32)
        dws["w_up"] = _mm_tn_blocked(tag + "ffn_up_dw", sv["h2"], dhu, l, dws["w_up"])
        if l == 0:
            dx1, d_nfpre, sib = _norm_pre_bwd(tag + "norm_ffn_pre_bwd", dh2, sv["x1"], dx,
                                              norm_ffn_pre[l:l + 1], _reduce_swap(blocks(ffn), 0))
            sums = chip_sums(ffn, sib, 0)
        else:
            dx1, d_nfpre = _norm_pre_bwd(tag + "norm_ffn_pre_bwd", dh2, sv["x1"], dx,
                                         norm_ffn_pre[l:l + 1])
        dmix, d_nmpost = _norm_post_bwd(tag + "norm_mix_post_bwd", dx1, sv["mix"], norm_mix_post[l:l + 1])
        dz = _mm_nt(tag + "mix_dx", dmix, wo_full, l, D_MODEL, BF16)
        dws["w_o"] = _mm_tn(tag + "mix_dw", sv["z"], dmix, D_MODEL, l, dws["w_o"])
        dya, dyb, dgates, d_bgate = _gate_bwd(tag + "gate_bwd", dz, sv["proj"], b_gate[l:l + 1],
                                              sv["ya"], sv["yb"])
        datt = _mm_nt_blocked(tag + "attn_out_dx", dya, wao_g, l, BF16)
        dws["w_attn_out"] = _mm_tn_blocked(tag + "attn_out_dw", sv["att"], dya, l, dws["w_attn_out"])
        dmixed = _mm_nt_blocked(tag + "pool_out_dx", dyb, wpo_g, l, BF16)
        dws["w_pool_out"] = _mm_tn_blocked(tag + "pool_out_dw", sv["mixed"], dyb, l, dws["w_pool_out"])
        if l == 0:
            du, d_wg, d_pscale, sib = _pool_bwd(tag + "pool_bwd", dmixed, sv["pooled"], wg_bf[l],
                                                pool_scale[l:l + 1], _reduce_swap(blocks(outs3), 0))
            sums3 = chip_sums(outs3, sib, 0)
            dq, dk, dv, dbias, recv = _attn_bwd(
                tag + "attn_bwd", sv["proj"], datt, sv["bias"],
                _both(_reduce_scatter(sums), _reduce_scatter(sums3)))
            final_sums(ffn, sums, recv[:len(ffn)], 0)
            final_sums(outs3, sums3, recv[len(ffn):], 0)
        else:
            du, d_wg, d_pscale = _pool_bwd(tag + "pool_bwd", dmixed, sv["pooled"], wg_bf[l],
                                           pool_scale[l:l + 1])
            dq, dk, dv, dbias = _attn_bwd(tag + "attn_bwd", sv["proj"], datt, sv["bias"])
        d_rel = _bias_fold(tag + "bias_fold", dbias)
        dproj = jnp.concatenate([dq, dk, dv, du, dgates], axis=1)
        if l == 0:
            dh, shared = _mm_nt_blocked(tag + "proj_dx", dproj, win_g, l, F32,
                                        _reduce_share([reds[k] for k in ffn + outs3], 0))
            for k, r in zip(ffn + outs3, shared):
                reds[k] = r
        else:
            dh = _mm_nt_blocked(tag + "proj_dx", dproj, win_g, l, F32)
        dws["w_in"] = _mm_tn_blocked(tag + "proj_dw", sv["h"], dproj, l, dws["w_in"])
        dx, d_nmpre = _norm_pre_bwd(tag + "norm_mix_pre_bwd", dh, sv["x"], dx1, norm_mix_pre[l:l + 1])
        small_grads[l] = [d_nmpre, d_nmpost, d_nfpre, d_nfpost, d_bgate, d_rel, d_wg, d_pscale,
                          dconv[3:4], dconv[0:3]]

    grad_x = dx.reshape(x.shape)

    sib = _comm_call("reduce_swap", _reduce_swap(blocks([0]), 0))
    sums = chip_sums([0], sib, 0)
    recv = _comm_call("reduce_scatter", _reduce_scatter(sums))
    final_sums([0], sums, recv, 0)
    g_big = _comm_call("reduce_share", _reduce_share([reds[0]], 0)) + reds[1:]

    pieces = []
    for idx in range(10):
        pieces.append(jnp.stack([small_grads[0][idx], small_grads[1][idx]]))
    pack = jnp.concatenate([_rows128(p) for p in pieces], axis=0)
    red = _allreduce_small(pack)
    shapes = [p.shape for p in pieces]
    outs = []
    row = 0
    for shp in shapes:
        nrow = math.prod(shp) // 128
        outs.append(red[row:row + nrow].reshape(shp))
        row += nrow
    (g_nmpre, g_nmpost, g_nfpre, g_nfpost, g_bgate, g_rel, g_wg, g_pscale, g_cb, g_cw) = outs
    g_nmpre, g_nmpost, g_nfpre, g_nfpost = [a.reshape(DEPTH, D_MODEL)
                                            for a in (g_nmpre, g_nmpost, g_nfpre, g_nfpost)]
    g_bgate = g_bgate.reshape(DEPTH, 2 * D_MODEL)
    g_rel = g_rel[:, :, :N_REL]
    g_pscale = g_pscale.reshape(DEPTH, POOL_W)
    g_cb = g_cb.reshape(DEPTH, 2 * D_FF)
    ncw = conv_w.shape[2]
    chip = 2 * lax.axis_index("x") + lax.axis_index("y")
    g_cw = lax.dynamic_slice_in_dim(g_cw, chip * ncw, ncw, axis=2)

    grads = dict(norm_mix_pre=g_nmpre, w_in=g_big[0], b_gate=g_bgate, rel_bias=g_rel,
                 w_attn_out=g_big[1], w_pool_group=g_wg, pool_scale=g_pscale, w_pool_out=g_big[2],
                 w_o=g_big[3], norm_mix_post=g_nmpost, norm_ffn_pre=g_nfpre, w_up=g_big[4],
                 conv_w=g_cw, conv_b=g_cb, w_down=g_big[5], norm_ffn_post=g_nfpost)
    weights = dict(norm_mix_pre=norm_mix_pre, w_in=w_in, b_gate=b_gate, rel_bias=rel_bias,
                   w_attn_out=w_attn_out, w_pool_group=w_pool_group, pool_scale=pool_scale,
                   w_pool_out=w_pool_out, w_o=w_o, norm_mix_post=norm_mix_post,
                   norm_ffn_pre=norm_ffn_pre, w_up=w_up, conv_w=conv_w, conv_b=conv_b,
                   w_down=w_down, norm_ffn_post=norm_ffn_post)
    moms = dict(norm_mix_pre=(m_norm_mix_pre, v_norm_mix_pre), w_in=(m_w_in, v_w_in),
                b_gate=(m_b_gate, v_b_gate), rel_bias=(m_rel_bias, v_rel_bias),
                w_attn_out=(m_w_attn_out, v_w_attn_out),
                w_pool_group=(m_w_pool_group, v_w_pool_group),
                pool_scale=(m_pool_scale, v_pool_scale), w_pool_out=(m_w_pool_out, v_w_pool_out),
                w_o=(m_w_o, v_w_o), norm_mix_post=(m_norm_mix_post, v_norm_mix_post),
                norm_ffn_pre=(m_norm_ffn_pre, v_norm_ffn_pre), w_up=(m_w_up, v_w_up),
                conv_w=(m_conv_w, v_conv_w), conv_b=(m_conv_b, v_conv_b),
                w_down=(m_w_down, v_w_down), norm_ffn_post=(m_norm_ffn_post, v_norm_ffn_post))
    order = list(weights.keys())

    delta, new_m, new_v = {}, {}, {}
    small_names = [nm for nm in order if nm not in names]
    for nm in names:
        delta[nm], new_m[nm], new_v[nm] = _adamw("adamw_" + nm, weights[nm], grads[nm], *moms[nm])

    def pack_small(get):
        flat = [get(nm).reshape(-1) for nm in small_names]
        total = sum(f.shape[0] for f in flat)
        padded = -(-total // 1024) * 1024
        flat.append(jnp.zeros((padded - total,), F32))
        return jnp.concatenate(flat).reshape(1, padded // 128, 128)

    d_s, m_s, v_s = _adamw(
        "adamw_small", pack_small(lambda nm: weights[nm]), pack_small(lambda nm: grads[nm]),
        pack_small(lambda nm: moms[nm][0]) , pack_small(lambda nm: moms[nm][1]))
    off = 0
    for nm in small_names:
        size = math.prod(weights[nm].shape)
        for dst, src in ((delta, d_s), (new_m, m_s), (new_v, v_s)):
            dst[nm] = src.reshape(-1)[off:off + size].reshape(weights[nm].shape)
        off += size

    return (loss, grad_x, *[grads[nm] for nm in order], *[delta[nm] for nm in order],
            *[new_m[nm] for nm in order], *[new_v[nm] for nm in order])
```

```python
import functools
import math

import jax
import jax.numpy as jnp
from jax import lax
from jax.experimental import pallas as pl
from jax.experimental.pallas import tpu as pltpu

F32 = jnp.float32
BF16 = jnp.bfloat16
MESH = pl.DeviceIdType.MESH

D_MODEL = 1024
DEPTH = 2
CHUNK = 64
BAND_CHUNKS = 9
BAND = BAND_CHUNKS * CHUNK
HEADS = 8
HEAD_DIM = 64
ATTN_W = HEADS * HEAD_DIM
POOL_WINDOWS = (2, 4, 8, 16)
POOL_W = 512
POOL_GD = 128
MAX_REL = 256
N_REL = 2 * MAX_REL + 1
D_FF = 2816
IN_W = 3 * ATTN_W + POOL_W + 2 * D_MODEL
EPS = 1e-6
ATTN_SCALE = HEAD_DIM ** -0.5
BAND_PAD = 640
BIAS_LANES = BAND_PAD
N_CHIPS = 4

ADAM_LR = 0.001
ADAM_B1 = 0.9
ADAM_B2 = 0.999
ADAM_EPS = 1e-08
ADAM_WD = 0.01
ADAM_STEP = 10

VMEM_LIMIT_V7X = 56 * 1024 * 1024
TOK = 512
ATT_BLK = 8 * CHUNK
FF_COL = 256
HALO = 32


def _cparams(*sem):
    return pltpu.CompilerParams(dimension_semantics=sem, vmem_limit_bytes=VMEM_LIMIT_V7X)


def _sds(shape, dtype):
    return jax.ShapeDtypeStruct(shape, dtype)


class _Comm:
    def __init__(self, ins, outs, copies, n_sems, alias=None):
        self.ins, self.outs, self.copies, self.n_sems = list(ins), list(outs), copies, n_sems
        self.alias = dict(alias or {})


class _SemsFrom:
    def __init__(self, sems, start):
        self.sems, self.start = sems, start

    @property
    def at(self):
        return self

    def __getitem__(self, i):
        return self.sems.at[self.start + i]


def _both(a, b):
    na, nao = len(a.ins), len(a.outs)

    def copies(cin, cout, ssem, rsem):
        return (a.copies(cin[:na], cout[:nao], ssem, rsem)
                + b.copies(cin[na:], cout[nao:], _SemsFrom(ssem, a.n_sems), _SemsFrom(rsem, a.n_sems)))

    alias = dict(a.alias)
    alias.update({na + i: nao + o for i, o in b.alias.items()})
    return _Comm(a.ins + b.ins, a.outs + b.outs, copies, a.n_sems + b.n_sems, alias)


def _pcall(body, name, grid, in_specs, out_specs, out_shape, scratch_shapes, compiler_params, args,
           comm=None, aliases=None):
    single = not isinstance(out_shape, (list, tuple))
    out_specs = [out_specs] if single else list(out_specs)
    out_shape = [out_shape] if single else list(out_shape)
    n_in, n_out = len(in_specs), len(out_specs)
    aliases = dict(aliases or {})
    if comm is None:
        res = pl.pallas_call(
            body, name=name, grid=grid, in_specs=list(in_specs), out_specs=out_specs,
            out_shape=out_shape, scratch_shapes=list(scratch_shapes),
            input_output_aliases=aliases, compiler_params=compiler_params)(*args)
        return (res[0] if single else res), None
    ci, co = len(comm.ins), len(comm.outs)

    def hosted(*refs):
        main_in, cin = refs[:n_in], refs[n_in:n_in + ci]
        main_out = refs[n_in + ci:n_in + ci + n_out]
        cout = refs[n_in + ci + n_out:n_in + ci + n_out + co]
        rest = refs[n_in + ci + n_out + co:]
        copies = comm.copies(cin, cout, rest[-2], rest[-1])
        ids = [pl.program_id(a) for a in range(len(grid))]
        first = functools.reduce(jnp.logical_and, [i == 0 for i in ids])
        last = functools.reduce(jnp.logical_and, [i == g - 1 for i, g in zip(ids, grid)])

        @pl.when(first)
        def _():
            for cp in copies:
                cp.start()

        body(*main_in, *main_out, *rest[:-2])

        @pl.when(last)
        def _():
            for cp in copies:
                cp.wait()

    for i, o in comm.alias.items():
        aliases[n_in + i] = n_out + o
    hbm = pl.BlockSpec(memory_space=pl.ANY)
    sems = pltpu.SemaphoreType.DMA((comm.n_sems,))
    res = pl.pallas_call(
        hosted, name=name, grid=grid, in_specs=list(in_specs) + [hbm] * ci,
        out_specs=out_specs + [hbm] * co, out_shape=out_shape + comm.outs,
        scratch_shapes=list(scratch_shapes) + [sems, sems],
        input_output_aliases=aliases, compiler_params=compiler_params)(*args, *comm.ins)
    return (res[0] if single else list(res[:n_out])), list(res[n_out:])


def _comm_call(name, comm):
    ci = len(comm.ins)

    def body(*refs):
        copies = comm.copies(refs[:ci], refs[ci:-2], refs[-2], refs[-1])
        for cp in copies:
            cp.start()
        for cp in copies:
            cp.wait()

    hbm = pl.BlockSpec(memory_space=pl.ANY)
    sems = pltpu.SemaphoreType.DMA((comm.n_sems,))
    return list(pl.pallas_call(
        body, name=name, in_specs=[hbm] * ci, out_specs=[hbm] * len(comm.outs),
        out_shape=comm.outs, scratch_shapes=[sems, sems],
        input_output_aliases=comm.alias)(*comm.ins))


def _matmul(name, a, b, a_spec, b_spec, o_spec, out_shape, grid, contract, nk, acc_shape,
            fill=None, comm=None):
    def body(*refs):
        a_ref, b_ref = refs[0], refs[1]
        o_ref = refs[2 if fill is None else 3]
        scratch = refs[(3 if fill is None else 4):]
        part = lax.dot_general(a_ref[...], b_ref[...], (contract, ((), ())),
                               preferred_element_type=F32)
        if nk == 1:
            o_ref[...] = part.astype(o_ref.dtype)
        else:
            acc_ref = scratch[0]
            k = pl.program_id(2)

            @pl.when(k == 0)
            def _():
                acc_ref[...] = part

            @pl.when(k > 0)
            def _():
                acc_ref[...] += part

            @pl.when(k == nk - 1)
            def _():
                o_ref[...] = acc_ref[...].astype(o_ref.dtype)

    scratch = [] if nk == 1 else [pltpu.VMEM(acc_shape, F32)]
    in_specs, args, aliases = [a_spec, b_spec], [a, b], {}
    if fill is not None:
        in_specs.append(pl.BlockSpec(memory_space=pl.ANY))
        args.append(fill)
        aliases = {2: 0}
    out, moved = _pcall(body, name, grid, in_specs, o_spec, out_shape, scratch,
                        _cparams("parallel", "parallel", "arbitrary"), args, comm, aliases)
    return out if comm is None else (out, moved)


NN = ((1,), (0,))
NT = ((1,), (1,))
TN = ((0,), (0,))


def _tm(t):
    return min(t, 1024)


def _col_block_spec(a, rows, nb, row_col):
    if a.ndim == 2:
        return pl.BlockSpec((rows, nb), row_col)

    def halves(*ids):
        r, c = row_col(*ids)
        return c // 2, r, c % 2

    return pl.BlockSpec((None, rows, nb), halves)


def _mm_nn_blocked(name, a, w, l, out_dtype):
    t, k = a.shape
    nb = w.shape[3]
    tm = _tm(t)
    return _matmul(
        name, a, w,
        pl.BlockSpec((tm, k), lambda i, n, kk: (i, 0)),
        pl.BlockSpec((None, None, k, nb), lambda i, n, kk: (l, n, 0, 0)),
        pl.BlockSpec((tm, nb), lambda i, n, kk: (i, n)),
        _sds((t, N_CHIPS * nb), out_dtype), (t // tm, N_CHIPS, 1), NN, 1, None)


def _mm_nt_blocked(name, a, w, l, out_dtype, comm=None):
    t = a.shape[-2]
    k, nb = w.shape[2], w.shape[3]
    tm = _tm(t)
    return _matmul(
        name, a, w,
        _col_block_spec(a, tm, nb, lambda i, n, kk: (i, kk)),
        pl.BlockSpec((None, None, k, nb), lambda i, n, kk: (l, kk, 0, 0)),
        pl.BlockSpec((tm, k), lambda i, n, kk: (i, 0)),
        _sds((t, k), out_dtype), (t // tm, 1, N_CHIPS), NT, N_CHIPS, (tm, k), comm=comm)


def _mm_tn_blocked(name, a, g, l, fill):
    t, k = a.shape
    nb = g.shape[-1] * (g.ndim - 1) // N_CHIPS
    tt = _tm(t)
    nt = t // tt
    return _matmul(
        name, a, g,
        pl.BlockSpec((tt, k), lambda n, j, kk: (kk, 0)),
        _col_block_spec(g, tt, nb, lambda n, j, kk: (kk, n)),
        pl.BlockSpec((None, None, k, nb), lambda n, j, kk: (l, n, 0, 0)),
        _sds((DEPTH, N_CHIPS, k, nb), BF16), (N_CHIPS, 1, nt), TN, nt, (k, nb), fill)


def _narrow_nn(name, a, w, l):
    t, k = a.shape
    nb = w.shape[3]
    tm = _tm(t)

    def body(a_ref, w_ref, o_ref):
        av = a_ref[...]
        for j in range(N_CHIPS):
            o_ref[:, j * nb:(j + 1) * nb] = jnp.dot(
                av, w_ref[j], preferred_element_type=F32).astype(BF16)

    return pl.pallas_call(
        body, name=name, grid=(t // tm,),
        in_specs=[pl.BlockSpec((tm, k), lambda i: (i, 0)),
                  pl.BlockSpec((None, N_CHIPS, k, nb), lambda i: (l, 0, 0, 0))],
        out_specs=pl.BlockSpec((tm, N_CHIPS * nb), lambda i: (i, 0)),
        out_shape=_sds((t, N_CHIPS * nb), BF16), compiler_params=_cparams("parallel"))(a, w)


def _narrow_nt(name, a, w, l):
    t = a.shape[0]
    k, nb = w.shape[2], w.shape[3]
    tm = _tm(t)

    def body(a_ref, w_ref, o_ref):
        acc = lax.dot_general(a_ref[:, 0:nb], w_ref[0], (NT, ((), ())), preferred_element_type=F32)
        for j in range(1, N_CHIPS):
            acc = acc + lax.dot_general(a_ref[:, j * nb:(j + 1) * nb], w_ref[j], (NT, ((), ())),
                                        preferred_element_type=F32)
        o_ref[...] = acc.astype(BF16)

    return pl.pallas_call(
        body, name=name, grid=(t // tm,),
        in_specs=[pl.BlockSpec((tm, N_CHIPS * nb), lambda i: (i, 0)),
                  pl.BlockSpec((None, N_CHIPS, k, nb), lambda i: (l, 0, 0, 0))],
        out_specs=pl.BlockSpec((tm, k), lambda i: (i, 0)),
        out_shape=_sds((t, k), BF16), compiler_params=_cparams("parallel"))(a, w)


def _narrow_tn(name, a, g, l, fill):
    t, k = a.shape
    nb = g.shape[1] // N_CHIPS
    tt = _tm(t)
    nt = t // tt

    def body(*refs):
        a_ref, g_ref, o_ref, acc_ref = refs[0], refs[1], refs[-2], refs[-1]
        i = pl.program_id(0)
        part = lax.dot_general(a_ref[...], g_ref[...], (TN, ((), ())), preferred_element_type=F32)

        @pl.when(i == 0)
        def _():
            acc_ref[...] = part

        @pl.when(i > 0)
        def _():
            acc_ref[...] += part

        @pl.when(i == nt - 1)
        def _():
            for j in range(N_CHIPS):
                o_ref[j] = acc_ref[:, j * nb:(j + 1) * nb].astype(BF16)

    in_specs = [pl.BlockSpec((tt, k), lambda i: (i, 0)),
                pl.BlockSpec((tt, N_CHIPS * nb), lambda i: (i, 0))]
    args, aliases = [a, g], {}
    if fill is not None:
        in_specs.append(pl.BlockSpec(memory_space=pl.ANY))
        args.append(fill)
        aliases = {2: 0}
    return pl.pallas_call(
        body, name=name, grid=(nt,), in_specs=in_specs,
        out_specs=pl.BlockSpec((None, N_CHIPS, k, nb), lambda i: (l, 0, 0, 0)),
        out_shape=_sds((DEPTH, N_CHIPS, k, nb), BF16),
        scratch_shapes=[pltpu.VMEM((k, N_CHIPS * nb), F32)], input_output_aliases=aliases,
        compiler_params=_cparams("arbitrary"))(*args)


def _mm_nn(name, a, w, l, tk, out_dtype):
    t, k = a.shape
    n = w.shape[2]
    tm = _tm(t)
    nk = k // tk
    return _matmul(
        name, a, w,
        pl.BlockSpec((tm, tk), lambda i, j, kk: (i, kk)),
        pl.BlockSpec((None, tk, n), lambda i, j, kk: (l, kk, 0)),
        pl.BlockSpec((tm, n), lambda i, j, kk: (i, 0)),
        _sds((t, n), out_dtype), (t // tm, 1, nk), NN, nk, (tm, n))


def _mm_nt(name, a, w, l, tn, out_dtype, comm=None):
    t, n = a.shape
    k = w.shape[1]
    tm = _tm(t)
    return _matmul(
        name, a, w,
        pl.BlockSpec((tm, n), lambda i, j, kk: (i, 0)),
        pl.BlockSpec((None, tn, n), lambda i, j, kk: (l, j, 0)),
        pl.BlockSpec((tm, tn), lambda i, j, kk: (i, j)),
        _sds((t, k), out_dtype), (t // tm, k // tn, 1), NT, 1, None, comm=comm)


def _mm_tn(name, a, g, tko, l, fill):
    t, k = a.shape
    n = g.shape[1]
    tt = _tm(t)
    nt = t // tt
    return _matmul(
        name, a, g,
        pl.BlockSpec((tt, tko), lambda i, j, kk: (kk, i)),
        pl.BlockSpec((tt, n), lambda i, j, kk: (kk, 0)),
        pl.BlockSpec((None, tko, n), lambda i, j, kk: (l, i, 0)),
        _sds((DEPTH, k, n), BF16), (k // tko, 1, nt), TN, nt, (tko, n), fill)


def _row_spec(width, col=0):
    return pl.BlockSpec((TOK, width), lambda i: (i, col))


def _vec_spec(width):
    return pl.BlockSpec((1, width), lambda i: (0, 0))


def _rms(x):
    return lax.rsqrt(jnp.mean(x * x, axis=-1, keepdims=True) + EPS)


def _norm_fwd(name, x, g):
    t = x.shape[0]

    def body(x_ref, g_ref, h_ref):
        xv = x_ref[...]
        h_ref[...] = (xv * _rms(xv) * g_ref[...]).astype(BF16)

    return pl.pallas_call(
        body, name=name, grid=(t // TOK,), in_specs=[_row_spec(D_MODEL), _vec_spec(D_MODEL)],
        out_specs=_row_spec(D_MODEL), out_shape=_sds((t, D_MODEL), BF16),
        compiler_params=_cparams("parallel"))(x, g)


def _norm_residual_fwd(name, xres, m, g, comm=None):
    t = xres.shape[0]

    def body(x_ref, m_ref, g_ref, o_ref):
        mv = m_ref[...]
        o_ref[...] = x_ref[...] + mv * _rms(mv) * g_ref[...]

    out, moved = _pcall(
        body, name, (t // TOK,),
        [_row_spec(D_MODEL), _row_spec(D_MODEL), _vec_spec(D_MODEL)],
        _row_spec(D_MODEL), _sds((t, D_MODEL), F32), [], _cparams("arbitrary"),
        (xres, m, g), comm)
    return out if comm is None else (out, moved)


def _norm_post_bwd(name, dxo, m, g):
    t = dxo.shape[0]

    def body(d_ref, m_ref, g_ref, dm_ref, dg_ref):
        mv = m_ref[...]
        dv = d_ref[...]
        r = _rms(mv)
        n = mv * r
        dn = dv * g_ref[...]
        dm_ref[...] = (r * (dn - n * jnp.mean(dn * n, axis=-1, keepdims=True))).astype(BF16)
        part = jnp.sum(dv * n, axis=0, keepdims=True)

        @pl.when(pl.program_id(0) == 0)
        def _():
            dg_ref[...] = part

        @pl.when(pl.program_id(0) > 0)
        def _():
            dg_ref[...] += part

    return pl.pallas_call(
        body, name=name, grid=(t // TOK,),
        in_specs=[_row_spec(D_MODEL), _row_spec(D_MODEL), _vec_spec(D_MODEL)],
        out_specs=[_row_spec(D_MODEL), _vec_spec(D_MODEL)],
        out_shape=[_sds((t, D_MODEL), BF16), _sds((1, D_MODEL), F32)],
        compiler_params=_cparams("arbitrary"))(dxo, m, g)


def _norm_pre_bwd(name, dh, xin, dxo, g, comm=None):
    t = dh.shape[0]

    def body(dh_ref, x_ref, d_ref, g_ref, dx_ref, dg_ref):
        xv = x_ref[...]
        dhv = dh_ref[...]
        r = _rms(xv)
        n = xv * r
        dn = dhv * g_ref[...]
        dx_ref[...] = d_ref[...] + r * (dn - n * jnp.mean(dn * n, axis=-1, keepdims=True))
        part = jnp.sum(dhv * n, axis=0, keepdims=True)

        @pl.when(pl.program_id(0) == 0)
        def _():
            dg_ref[...] = part

        @pl.when(pl.program_id(0) > 0)
        def _():
            dg_ref[...] += part

    out, moved = _pcall(
        body, name, (t // TOK,),
        [_row_spec(D_MODEL), _row_spec(D_MODEL), _row_spec(D_MODEL), _vec_spec(D_MODEL)],
        [_row_spec(D_MODEL), _vec_spec(D_MODEL)],
        [_sds((t, D_MODEL), F32), _sds((1, D_MODEL), F32)], [], _cparams("arbitrary"),
        (dh, xin, dxo, g), comm)
    return out if comm is None else (*out, moved)


def _loss_head(y, target):
    t = y.shape[0]

    def body(y_ref, t_ref, dy_ref, l_ref):
        e = y_ref[...] - t_ref[...]
        dy_ref[...] = e * (1.0 / D_MODEL)
        part = jnp.sum(jnp.sum(e * e, axis=0, keepdims=True), axis=1, keepdims=True)

        @pl.when(pl.program_id(0) == 0)
        def _():
            l_ref[...] = part

        @pl.when(pl.program_id(0) > 0)
        def _():
            l_ref[...] += part

    dy, sq = pl.pallas_call(
        body, name="loss_head", grid=(t // TOK,),
        in_specs=[_row_spec(D_MODEL), _row_spec(D_MODEL)],
        out_specs=[_row_spec(D_MODEL), pl.BlockSpec((1, 1), lambda i: (0, 0))],
        out_shape=[_sds((t, D_MODEL), F32), _sds((1, 1), F32)],
        compiler_params=_cparams("arbitrary"))(y, target)
    return dy, sq[0, 0] * (0.5 / D_MODEL)


def _gate_fwd(name, proj, b_gate, ya, yb):
    t = proj.shape[0]

    def body(ga_ref, gb_ref, b_ref, ya_ref, yb_ref, z_ref):
        sa = jax.nn.sigmoid(ga_ref[...].astype(F32) + b_ref[:, :D_MODEL])
        sb = jax.nn.sigmoid(gb_ref[...].astype(F32) + b_ref[:, D_MODEL:])
        z_ref[...] = (sa * ya_ref[...].astype(F32) + sb * yb_ref[...].astype(F32)).astype(BF16)

    return pl.pallas_call(
        body, name=name, grid=(t // TOK,),
        in_specs=[_row_spec(D_MODEL, 2), _row_spec(D_MODEL, 3), _vec_spec(2 * D_MODEL),
                  _row_spec(D_MODEL), _row_spec(D_MODEL)],
        out_specs=_row_spec(D_MODEL), out_shape=_sds((t, D_MODEL), BF16),
        compiler_params=_cparams("parallel"))(proj, proj, b_gate, ya, yb)


def _gate_bwd(name, dz, proj, b_gate, ya, yb):
    t = proj.shape[0]

    def body(dz_ref, ga_ref, gb_ref, b_ref, ya_ref, yb_ref, dya_ref, dyb_ref, dg_ref, db_ref):
        dzv = dz_ref[...].astype(F32)
        sa = jax.nn.sigmoid(ga_ref[...].astype(F32) + b_ref[:, :D_MODEL])
        sb = jax.nn.sigmoid(gb_ref[...].astype(F32) + b_ref[:, D_MODEL:])
        dya_ref[...] = (dzv * sa).astype(BF16)
        dyb_ref[...] = (dzv * sb).astype(BF16)
        dga = dzv * ya_ref[...].astype(F32) * sa * (1.0 - sa)
        dgb = dzv * yb_ref[...].astype(F32) * sb * (1.0 - sb)
        dg_ref[:, :D_MODEL] = dga.astype(BF16)
        dg_ref[:, D_MODEL:] = dgb.astype(BF16)
        pa = jnp.sum(dga, axis=0, keepdims=True)
        pb = jnp.sum(dgb, axis=0, keepdims=True)

        @pl.when(pl.program_id(0) == 0)
        def _():
            db_ref[:, :D_MODEL] = pa
            db_ref[:, D_MODEL:] = pb

        @pl.when(pl.program_id(0) > 0)
        def _():
            db_ref[:, :D_MODEL] += pa
            db_ref[:, D_MODEL:] += pb

    return pl.pallas_call(
        body, name=name, grid=(t // TOK,),
        in_specs=[_row_spec(D_MODEL), _row_spec(D_MODEL, 2), _row_spec(D_MODEL, 3),
                  _vec_spec(2 * D_MODEL), _row_spec(D_MODEL), _row_spec(D_MODEL)],
        out_specs=[_row_spec(D_MODEL), _row_spec(D_MODEL), _row_spec(2 * D_MODEL),
                   _vec_spec(2 * D_MODEL)],
        out_shape=[_sds((t, D_MODEL), BF16), _sds((t, D_MODEL), BF16),
                   _sds((t, 2 * D_MODEL), BF16), _sds((1, 2 * D_MODEL), F32)],
        compiler_params=_cparams("arbitrary"))(dz, proj, proj, b_gate, ya, yb)


def _head_masks():
    lane = lax.broadcasted_iota(jnp.int32, (1, 2 * HEAD_DIM), 1)
    return lane < HEAD_DIM


BAND_ROWS = 2 * ATT_BLK + CHUNK


def _fill_band(band, prev_ref, cur_ref):
    band[0:ATT_BLK, :] = prev_ref[...]
    band[ATT_BLK:2 * ATT_BLK, :] = cur_ref[...]
    band[2 * ATT_BLK:, :] = jnp.zeros((CHUNK, ATTN_W), BF16)


def _pair_rows(x2, low):
    zero = jnp.zeros_like(x2)
    return jnp.concatenate([jnp.where(low, x2, zero), jnp.where(low, zero, x2)], axis=0)


def _pair_diag(o2, low):
    return jnp.where(low, o2[0:CHUNK, :], o2[CHUNK:, :])


N_PAIRS = HEADS // 2
SM_STRIP = 32
N_STRIPS = BAND_PAD // SM_STRIP
NEG = -1e30


def _fold8(x, op):
    return op(op(x[0:8], x[8:16]), op(x[16:24], x[24:32]))


def _strip(k):
    return pl.ds(pl.multiple_of(k * SM_STRIP, SM_STRIP), SM_STRIP)


def _band_probs(k2, qcat, bias_t, first_key):
    kpos = lax.broadcasted_iota(jnp.int32, (BAND_PAD, 1), 0)
    st = lax.dot_general(k2, qcat, (NT, ((), ())), preferred_element_type=F32)
    st = jnp.where(kpos + first_key >= 0, st + bias_t, NEG)
    e = jnp.exp(st - jnp.max(st, axis=0, keepdims=True))
    return e * (1.0 / jnp.sum(e, axis=0, keepdims=True))


def _band_softmax_stats(st_ref, b_ref, first_key, dp_ref):
    rowi = lax.broadcasted_iota(jnp.int32, (SM_STRIP, 128), 0)

    def scores(k, mx):
        rows = _strip(k)
        live = (rowi + (k * SM_STRIP + first_key)) >= 0
        out = []
        for hp in range(N_PAIRS):
            x = jnp.where(live, st_ref[hp, rows, :] + b_ref[hp, rows, :], NEG)
            st_ref[hp, rows, :] = x
            out.append(jnp.maximum(mx[hp], _fold8(x, jnp.maximum)))
        return tuple(out)

    mx = lax.fori_loop(0, N_STRIPS, scores, (jnp.full((8, 128), NEG, F32),) * N_PAIRS, unroll=2)
    top = [jnp.max(m, axis=0, keepdims=True) for m in mx]

    def sums(k, acc):
        rows = _strip(k)
        ls, eds = [], []
        for hp in range(N_PAIRS):
            e = jnp.exp(st_ref[hp, rows, :] - top[hp])
            ls.append(acc[hp] + _fold8(e, jnp.add))
            eds.append(acc[N_PAIRS + hp] + _fold8(e * dp_ref[hp, rows, :], jnp.add))
        return tuple(ls + eds)

    acc = lax.fori_loop(0, N_STRIPS, sums, (jnp.zeros((8, 128), F32),) * (2 * N_PAIRS), unroll=2)
    inv = [1.0 / jnp.sum(a, axis=0, keepdims=True) for a in acc[:N_PAIRS]]
    delta = [jnp.sum(a, axis=0, keepdims=True) * i for a, i in zip(acc[N_PAIRS:], inv)]
    return top, inv, delta


def _attn_specs(nblk):
    cur = lambda col: pl.BlockSpec((ATT_BLK, ATTN_W), lambda s: (jnp.minimum(s, nblk - 1), col))
    prev = lambda col: pl.BlockSpec(
        (ATT_BLK, ATTN_W), lambda s: (jnp.maximum(jnp.minimum(s, nblk - 1) - 1, 0), col))
    return cur, prev


def _attn_fwd(name, proj, bias, comm=None):
    t = proj.shape[0]
    nblk = t // ATT_BLK
    cur, prev = _attn_specs(nblk)

    def body(q_ref, kp_ref, kc_ref, vp_ref, vc_ref, b_ref, o_ref, kband, vband):
        s = pl.program_id(0)
        _fill_band(kband, kp_ref, kc_ref)
        _fill_band(vband, vp_ref, vc_ref)
        low = _head_masks()

        def chunk(ci, carry):
            r0 = pl.multiple_of(ci * CHUNK, CHUNK)
            for hp in range(N_PAIRS):
                cols = slice(hp * 128, (hp + 1) * 128)
                qcat = _pair_rows(q_ref[pl.ds(r0, CHUNK), cols] * ATTN_SCALE, low)
                p = _band_probs(kband[pl.ds(r0, BAND_PAD), cols], qcat, b_ref[hp],
                                (s * 8 - 8 + ci) * CHUNK)
                o2 = lax.dot_general(p.astype(BF16), vband[pl.ds(r0, BAND_PAD), cols],
                                     (TN, ((), ())), preferred_element_type=F32)
                o_ref[pl.ds(r0, CHUNK), cols] = _pair_diag(o2, low).astype(BF16)
            return carry

        lax.fori_loop(0, 8, chunk, 0)

    out, moved = _pcall(
        body, name, (nblk,),
        [cur(0), prev(1), cur(1), prev(2), cur(2),
         pl.BlockSpec((N_PAIRS, BAND_PAD, 128), lambda s: (0, 0, 0))],
        pl.BlockSpec((ATT_BLK, ATTN_W), lambda s: (s, 0)), _sds((t, ATTN_W), BF16),
        [pltpu.VMEM((BAND_ROWS, ATTN_W), BF16), pltpu.VMEM((BAND_ROWS, ATTN_W), BF16)],
        _cparams("arbitrary"), (proj, proj, proj, proj, proj, bias), comm)
    return out if comm is None else (out, moved)


def _attn_bwd(name, proj, datt, bias, comm=None):
    t = proj.shape[0]
    nblk = t // ATT_BLK
    cur, prev = _attn_specs(nblk)
    late = pl.BlockSpec((ATT_BLK, ATTN_W), lambda s: (jnp.maximum(s - 1, 0), 0))

    def body(q_ref, kp_ref, kc_ref, vp_ref, vc_ref, do_ref, b_ref,
             dq_ref, dk_ref, dv_ref, db_ref, kband, vband, dkacc, dvacc,
             st_ref, dp_ref, pb_ref, dsb_ref, qc_ref, dc_ref):
        s = pl.program_id(0)

        @pl.when(s == 0)
        def _():
            dkacc[...] = jnp.zeros_like(dkacc)
            dvacc[...] = jnp.zeros_like(dvacc)
            db_ref[...] = jnp.zeros_like(db_ref)

        @pl.when(s < nblk)
        def _():
            _fill_band(kband, kp_ref, kc_ref)
            _fill_band(vband, vp_ref, vc_ref)
            low = _head_masks()

            def chunk(ci, carry):
                r0 = pl.multiple_of(ci * CHUNK, CHUNK)
                for hp in range(N_PAIRS):
                    cols = slice(hp * 128, (hp + 1) * 128)
                    qc_ref[hp] = _pair_rows(q_ref[pl.ds(r0, CHUNK), cols] * ATTN_SCALE, low)
                    dc_ref[hp] = _pair_rows(do_ref[pl.ds(r0, CHUNK), cols], low)
                    st_ref[hp] = lax.dot_general(kband[pl.ds(r0, BAND_PAD), cols], qc_ref[hp],
                                                 (NT, ((), ())), preferred_element_type=F32)
                    dp_ref[hp] = lax.dot_general(vband[pl.ds(r0, BAND_PAD), cols], dc_ref[hp],
                                                 (NT, ((), ())), preferred_element_type=F32)
                top, inv, delta = _band_softmax_stats(st_ref, b_ref, (s * 8 - 8 + ci) * CHUNK,
                                                      dp_ref)

                def grads(k, c):
                    rows = _strip(k)
                    for hp in range(N_PAIRS):
                        p = jnp.exp(st_ref[hp, rows, :] - top[hp]) * inv[hp]
                        ds = p * (dp_ref[hp, rows, :] - delta[hp])
                        db_ref[hp, rows, :] += ds
                        dsb_ref[hp, rows, :] = ds.astype(BF16)
                        pb_ref[hp, rows, :] = p.astype(BF16)
                    return c

                lax.fori_loop(0, N_STRIPS, grads, 0, unroll=2)
                for hp in range(N_PAIRS):
                    cols = slice(hp * 128, (hp + 1) * 128)
                    dq2 = lax.dot_general(dsb_ref[hp], kband[pl.ds(r0, BAND_PAD), cols],
                                          (TN, ((), ())), preferred_element_type=F32)
                    dq_ref[pl.ds(r0, CHUNK), cols] = (_pair_diag(dq2, low) * ATTN_SCALE).astype(BF16)
                    dkacc[pl.ds(r0, BAND_PAD), cols] += jnp.dot(dsb_ref[hp], qc_ref[hp],
                                                               preferred_element_type=F32)
                    dvacc[pl.ds(r0, BAND_PAD), cols] += jnp.dot(pb_ref[hp], dc_ref[hp],
                                                               preferred_element_type=F32)
                return carry

            lax.fori_loop(0, 8, chunk, 0)

        dk_ref[...] = dkacc[0:ATT_BLK, :].astype(BF16)
        dv_ref[...] = dvacc[0:ATT_BLK, :].astype(BF16)
        dkacc[0:ATT_BLK, :] = dkacc[ATT_BLK:2 * ATT_BLK, :]
        dvacc[0:ATT_BLK, :] = dvacc[ATT_BLK:2 * ATT_BLK, :]
        dkacc[ATT_BLK:, :] = jnp.zeros((ATT_BLK + CHUNK, ATTN_W), F32)
        dvacc[ATT_BLK:, :] = jnp.zeros((ATT_BLK + CHUNK, ATTN_W), F32)

    blk = _sds((t, ATTN_W), BF16)
    outs, moved = _pcall(
        body, name, (nblk + 1,),
        [cur(0), prev(1), cur(1), prev(2), cur(2),
         pl.BlockSpec((ATT_BLK, ATTN_W), lambda s: (jnp.minimum(s, nblk - 1), 0)),
         pl.BlockSpec((HEADS // 2, BAND_PAD, 128), lambda s: (0, 0, 0))],
        [pl.BlockSpec((ATT_BLK, ATTN_W), lambda s: (jnp.minimum(s, nblk - 1), 0)), late, late,
         pl.BlockSpec((HEADS // 2, BAND_PAD, 128), lambda s: (0, 0, 0))],
        [blk, blk, blk, _sds((HEADS // 2, BAND_PAD, 128), F32)],
        [pltpu.VMEM((BAND_ROWS, ATTN_W), BF16), pltpu.VMEM((BAND_ROWS, ATTN_W), BF16),
         pltpu.VMEM((BAND_ROWS, ATTN_W), F32), pltpu.VMEM((BAND_ROWS, ATTN_W), F32),
         pltpu.VMEM((N_PAIRS, BAND_PAD, 128), F32), pltpu.VMEM((N_PAIRS, BAND_PAD, 128), F32),
         pltpu.VMEM((N_PAIRS, BAND_PAD, 128), BF16), pltpu.VMEM((N_PAIRS, BAND_PAD, 128), BF16),
         pltpu.VMEM((N_PAIRS, 2 * CHUNK, 128), BF16), pltpu.VMEM((N_PAIRS, 2 * CHUNK, 128), BF16)],
        _cparams("arbitrary"), (proj, proj, proj, proj, proj, datt, bias), comm)
    return outs if comm is None else (*outs, moved)


def _diag_onehot(rel_rows):
    d0 = lax.broadcasted_iota(jnp.int32, (BIAS_LANES, BIAS_LANES), 0)
    d1 = lax.broadcasted_iota(jnp.int32, (BIAS_LANES, BIAS_LANES), 1)
    m, n = (d0, d1) if rel_rows else (d1, d0)
    hit = (m == jnp.minimum(BAND - 1 + MAX_REL - n, 2 * MAX_REL)) & (n < BAND + CHUNK - 1)
    return jnp.where(hit, 1.0, 0.0).astype(F32)


def _bias_table(name, rel_bias_l):
    rel_pad = jnp.pad(rel_bias_l, ((0, 0), (0, BIAS_LANES - N_REL)))

    def body(r_ref, o_ref):
        diag = jnp.dot(r_ref[...], _diag_onehot(True), preferred_element_type=F32,
                       precision=lax.Precision.HIGHEST)
        rowid = lax.broadcasted_iota(jnp.int32, (8, BIAS_LANES), 0)
        lane = lax.broadcasted_iota(jnp.int32, (8, BIAS_LANES), 1)
        for h in range(HEADS):
            d8 = jnp.broadcast_to(diag[h:h + 1, :], (8, BIAS_LANES))
            slab0 = pltpu.roll(d8, BIAS_LANES - CHUNK + 1, axis=1)
            for b in range(1, 8):
                slab0 = jnp.where(rowid == b, pltpu.roll(d8, BIAS_LANES - CHUNK + 1 + b, axis=1),
                                  slab0)
            for a in range(8):
                slab = slab0 if a == 0 else pltpu.roll(slab0, 8 * a, axis=1)
                o_ref[h * CHUNK + 8 * a:h * CHUNK + 8 * a + 8, :] = jnp.where(lane < BAND, slab, NEG)

    tab = pl.pallas_call(
        body, name=name,
        in_specs=[pl.BlockSpec(memory_space=pltpu.VMEM)],
        out_specs=pl.BlockSpec(memory_space=pltpu.VMEM),
        out_shape=_sds((HEADS * CHUNK, BIAS_LANES), F32),
    )(rel_pad)
    tab = tab.reshape(HEADS // 2, 2, CHUNK, BIAS_LANES)
    return jnp.transpose(tab, (0, 3, 1, 2)).reshape(HEADS // 2, BIAS_LANES, 2 * CHUNK)


def _bias_fold(name, dbias_t):
    rows = HEADS * CHUNK
    dbias = jnp.transpose(dbias_t.reshape(HEADS // 2, BIAS_LANES, 2, CHUNK), (0, 2, 3, 1))

    def body(d_ref, o_ref):
        rowid = lax.broadcasted_iota(jnp.int32, (8, BIAS_LANES), 0)
        diags = []
        for h in range(HEADS):
            acc = d_ref[h * CHUNK + 56:h * CHUNK + 64, :]
            for a in range(7):
                slab = d_ref[h * CHUNK + 8 * a:h * CHUNK + 8 * a + 8, :]
                acc = acc + pltpu.roll(slab, 56 - 8 * a, axis=1)
            tot = jnp.where(rowid == 7, acc, 0.0)
            for b in range(7):
                tot = tot + jnp.where(rowid == b, pltpu.roll(acc, 7 - b, axis=1), 0.0)
            diags.append(jnp.sum(tot, axis=0, keepdims=True))
        diag = jnp.concatenate(diags, axis=0)
        o_ref[...] = jnp.dot(diag, _diag_onehot(False), preferred_element_type=F32,
                             precision=lax.Precision.HIGHEST)

    return pl.pallas_call(
        body, name=name,
        in_specs=[pl.BlockSpec(memory_space=pltpu.VMEM)],
        out_specs=pl.BlockSpec(memory_space=pltpu.VMEM),
        out_shape=_sds((HEADS, BIAS_LANES), F32),
    )(dbias.reshape(rows, BIAS_LANES))


def _inv_counts(i):
    trow = lax.broadcasted_iota(jnp.int32, (TOK + HALO, 1), 0) + i * TOK
    return [1.0 / jnp.minimum(trow + 1, w).astype(F32) for w in POOL_WINDOWS]


def _pool_fwd(name, proj, wg, scale, comm=None):
    t = proj.shape[0]
    hb = TOK // HALO

    def body(u_ref, up_ref, wg_ref, sc_ref, pooled_ref, mixed_ref, b0, b1, b2, b3):
        i = pl.program_id(0)
        halo = up_ref[...].astype(F32)
        b0[0:HALO, :] = jnp.where(i == 0, jnp.zeros_like(halo), halo)
        b0[HALO:, :] = u_ref[...].astype(F32)
        n = TOK + HALO
        b1[8:n, :] = b0[8:n, :] + b0[7:n - 1, :]
        b2[16:n, 128:] = b1[16:n, 128:] + b1[14:n - 2, 128:]
        b3[24:n, 256:] = b2[24:n, 256:] + b2[20:n - 4, 256:]
        wins = [b1[HALO:n, 0:128], b2[HALO:n, 128:256], b3[HALO:n, 256:384],
                b3[HALO:n, 384:512] + b3[HALO - 8:n - 8, 384:512]]
        inv = _inv_counts(i)
        for g in range(4):
            cols = slice(g * POOL_GD, (g + 1) * POOL_GD)
            pooled = (wins[g] * inv[g][0:TOK] - b0[HALO:n, cols]).astype(BF16)
            pooled_ref[:, cols] = pooled
            pre = jnp.dot(pooled, wg_ref[g], preferred_element_type=F32)
            mixed_ref[:, cols] = (pre * sc_ref[:, cols]).astype(BF16)

    buf = pltpu.VMEM((TOK + HALO, POOL_W), F32)
    outs, moved = _pcall(
        body, name, (t // TOK,),
        [_row_spec(POOL_W, 3),
         pl.BlockSpec((HALO, POOL_W), lambda i: (jnp.maximum(i * hb - 1, 0), 3)),
         pl.BlockSpec((4, POOL_GD, POOL_GD), lambda i: (0, 0, 0)), _vec_spec(POOL_W)],
        [_row_spec(POOL_W), _row_spec(POOL_W)],
        [_sds((t, POOL_W), BF16), _sds((t, POOL_W), BF16)], [buf, buf, buf, buf],
        _cparams("arbitrary"), (proj, proj, wg, scale), comm)
    return outs if comm is None else (*outs, moved)


def _pool_bwd(name, dmixed, pooled, wg, scale, comm=None):
    t = dmixed.shape[0]
    nt = t // TOK
    hb = TOK // HALO

    def body(dm_ref, dmn_ref, p_ref, wg_ref, sc_ref, du_ref, dwg_ref, dsc_ref, c0, c1, c2, c3):
        i = pl.program_id(0)

        @pl.when(i == 0)
        def _():
            dwg_ref[...] = jnp.zeros_like(dwg_ref)
            dsc_ref[...] = jnp.zeros_like(dsc_ref)

        n = TOK + HALO
        inv = _inv_counts(i)
        dmv = dm_ref[...].astype(F32)
        dmn = dmn_ref[...].astype(F32)
        dmn = jnp.where(i == nt - 1, jnp.zeros_like(dmn), dmn)
        for g in range(4):
            cols = slice(g * POOL_GD, (g + 1) * POOL_GD)
            scg = sc_ref[:, cols]
            pg = p_ref[:, cols]
            dpre = (dmv[:, cols] * scg).astype(BF16)
            dpre_n = (dmn[:, cols] * scg).astype(BF16)
            pre = jnp.dot(pg, wg_ref[g], preferred_element_type=F32)
            dsc_ref[:, cols] += jnp.sum(dmv[:, cols] * pre, axis=0, keepdims=True)
            dwg_ref[g] += lax.dot_general(pg, dpre, (TN, ((), ())), preferred_element_type=F32)
            dpool = lax.dot_general(dpre, wg_ref[g], (NT, ((), ())), preferred_element_type=F32)
            dpool_n = lax.dot_general(dpre_n, wg_ref[g], (NT, ((), ())),
                                      preferred_element_type=F32)
            c0[0:TOK, cols] = dpool
            c0[TOK:n, cols] = dpool_n
            c1[0:TOK, cols] = dpool * inv[g][0:TOK]
            c1[TOK:n, cols] = dpool_n * inv[g][TOK:n]
        c2[0:n - 8, :] = c1[0:n - 8, :] + c1[1:n - 7, :]
        c3[0:n - 16, 128:] = c2[0:n - 16, 128:] + c2[2:n - 14, 128:]
        c1[0:n - 24, 256:] = c3[0:n - 24, 256:] + c3[4:n - 20, 256:]
        wins = [c2[0:TOK, 0:128], c3[0:TOK, 128:256], c1[0:TOK, 256:384],
                c1[0:TOK, 384:512] + c1[8:TOK + 8, 384:512]]
        for g in range(4):
            cols = slice(g * POOL_GD, (g + 1) * POOL_GD)
            du_ref[:, cols] = (wins[g] - c0[0:TOK, cols]).astype(BF16)

    buf = pltpu.VMEM((TOK + HALO, POOL_W), F32)
    outs, moved = _pcall(
        body, name, (nt,),
        [_row_spec(POOL_W),
         pl.BlockSpec((HALO, POOL_W), lambda i: (jnp.minimum((i + 1) * hb, nt * hb - 1), 0)),
         _row_spec(POOL_W), pl.BlockSpec((4, POOL_GD, POOL_GD), lambda i: (0, 0, 0)),
         _vec_spec(POOL_W)],
        [_row_spec(POOL_W), pl.BlockSpec((4, POOL_GD, POOL_GD), lambda i: (0, 0, 0)),
         _vec_spec(POOL_W)],
        [_sds((t, POOL_W), BF16), _sds((4, POOL_GD, POOL_GD), F32), _sds((1, POOL_W), F32)],
        [buf, buf, buf, buf], _cparams("arbitrary"), (dmixed, dmixed, pooled, wg, scale), comm)
    return outs if comm is None else (*outs, moved)


GELU_C = math.sqrt(2.0 / math.pi)


GELU_K = 0.044715


def _gelu_parts(x):
    x2 = x * x
    s = 0.5 + 0.5 * jnp.tanh(x * (GELU_C + (GELU_C * GELU_K) * x2))
    return x * s, s, x2


def _gelu(x):
    return _gelu_parts(x)[0]


def _gelu_and_grad(x):
    g, s, x2 = _gelu_parts(x)
    return g, s + g * (1.0 - s) * ((2 * GELU_C) + (6 * GELU_C * GELU_K) * x2)


def _taps(buf, r, rows):
    a = buf[pl.ds(r, rows + 8), :]
    return a[8:], pltpu.roll(a, 1, axis=0)[8:], pltpu.roll(a, 2, axis=0)[8:]


def _conv(taps, w_ref, b_ref):
    return b_ref[...] + w_ref[2:3, :] * taps[0] + w_ref[1:2, :] * taps[1] + w_ref[0:1, :] * taps[2]


def _stage(dst, prev_ref, cur_ref, next_ref, first, last):
    rows = cur_ref.shape[0]
    h = prev_ref[...].astype(F32)
    dst[0:8, :] = jnp.where(first, jnp.zeros_like(h), h)
    dst[8:8 + rows, :] = cur_ref[...].astype(F32)
    if next_ref is not None:
        h = next_ref[...].astype(F32)
        dst[8 + rows:, :] = jnp.where(last, jnp.zeros_like(h), h)


FWD_STRIP = 32
BWD_STRIP = 16


def _ffn_gate_fwd(name, hu, conv_w, conv_b, comm=None):
    t = hu.shape[0]
    ncol = D_FF // FF_COL
    hb = TOK // 8

    def tile(off):
        return pl.BlockSpec((TOK, FF_COL), lambda i, j: (i, j + off))

    def halo(off):
        return pl.BlockSpec((8, FF_COL), lambda i, j: (jnp.maximum(i * hb - 1, 0), j + off))

    def wspec(off):
        return pl.BlockSpec((3, FF_COL), lambda i, j: (0, j + off))

    def bspec(off):
        return pl.BlockSpec((1, FF_COL), lambda i, j: (0, j + off))

    def body(v_ref, vp_ref, g_ref, gp_ref, wv_ref, wg_ref, bv_ref, bg_ref, a_ref, vb, gb):
        first = pl.program_id(0) == 0
        _stage(vb, vp_ref, v_ref, None, first, None)
        _stage(gb, gp_ref, g_ref, None, first, None)

        def strip(k, carry):
            r = pl.multiple_of(k * FWD_STRIP, FWD_STRIP)
            val = _conv(_taps(vb, r, FWD_STRIP), wv_ref, bv_ref)
            gate = _conv(_taps(gb, r, FWD_STRIP), wg_ref, bg_ref)
            a_ref[pl.ds(r, FWD_STRIP), :] = (_gelu(gate) * val).astype(BF16)
            return carry

        lax.fori_loop(0, TOK // FWD_STRIP, strip, 0)

    buf = pltpu.VMEM((TOK + 8, FF_COL), F32)
    out, moved = _pcall(
        body, name, (t // TOK, ncol),
        [tile(0), halo(0), tile(ncol), halo(ncol), wspec(0), wspec(ncol), bspec(0), bspec(ncol)],
        pl.BlockSpec((TOK, FF_COL), lambda i, j: (i, j)), _sds((t, D_FF), BF16), [buf, buf],
        _cparams("arbitrary", "arbitrary"),
        (hu, hu, hu, hu, conv_w, conv_w, conv_b, conv_b), comm)
    return out if comm is None else (out, moved)


def _ffn_gate_bwd(name, da, hu, conv_w, conv_b, comm=None):
    t = hu.shape[0]
    nt = t // TOK
    ncol = D_FF // FF_COL
    hb = TOK // 8
    ext = TOK + 8

    def tile(off):
        return pl.BlockSpec((TOK, FF_COL), lambda j, i: (i, j + off))

    def prev(off):
        return pl.BlockSpec((8, FF_COL), lambda j, i: (jnp.maximum(i * hb - 1, 0), j + off))

    def nxt(off):
        return pl.BlockSpec((8, FF_COL), lambda j, i: (jnp.minimum((i + 1) * hb, nt * hb - 1), j + off))

    def wspec(off):
        return pl.BlockSpec((3, FF_COL), lambda j, i: (0, j + off))

    def bspec(off):
        return pl.BlockSpec((1, FF_COL), lambda j, i: (0, j + off))

    def body(da_ref, dan_ref, v_ref, vp_ref, vn_ref, g_ref, gp_ref, gn_ref,
             wv_ref, wg_ref, bv_ref, bg_ref, dh_ref, dwv_ref, dwg_ref, vb, gb, dab):
        i = pl.program_id(1)
        first, last = i == 0, i == nt - 1

        @pl.when(first)
        def _():
            dwv_ref[...] = jnp.zeros_like(dwv_ref)
            dwg_ref[...] = jnp.zeros_like(dwg_ref)

        _stage(vb, vp_ref, v_ref, vn_ref, first, last)
        _stage(gb, gp_ref, g_ref, gn_ref, first, last)
        dab[0:TOK, :] = da_ref[...].astype(F32)
        h = dan_ref[...].astype(F32)
        dab[TOK:, :] = jnp.where(last, jnp.zeros_like(h), h)

        def grads(r, rows):
            tv, tg = _taps(vb, r, rows), _taps(gb, r, rows)
            gate = _conv(tg, wg_ref, bg_ref)
            dav = dab[pl.ds(r, rows), :]
            g, dg = _gelu_and_grad(gate)
            dval = dav * g
            dgate = dav * _conv(tv, wv_ref, bv_ref) * dg
            return dval, dgate, tv, tg

        def fold(x):
            return x[0:8] + x[8:16]

        def strip(k, carry):
            r = pl.multiple_of(TOK - BWD_STRIP - k * BWD_STRIP, BWD_STRIP)
            dval, dgate, tv, tg = grads(r, BWD_STRIP)
            new = (dval[0:8], dgate[0:8])
            for half, (d, nxt_rows, taps, w_ref, dw_ref) in enumerate((
                    (dval, carry[0], tv, wv_ref, dwv_ref), (dgate, carry[1], tg, wg_ref, dwg_ref))):
                e = jnp.concatenate([d, nxt_rows], axis=0)
                dh = (w_ref[2:3, :] * d
                      + w_ref[1:2, :] * pltpu.roll(e, BWD_STRIP + 7, axis=0)[0:BWD_STRIP]
                      + w_ref[0:1, :] * pltpu.roll(e, BWD_STRIP + 6, axis=0)[0:BWD_STRIP])
                dh_ref[half, pl.ds(r, BWD_STRIP), :] = dh.astype(BF16)
                dw_ref[0:8, :] += fold(d * taps[2])
                dw_ref[8:16, :] += fold(d * taps[1])
                dw_ref[16:24, :] += fold(d * taps[0])
                dw_ref[24:32, :] += fold(d)
            return new

        dval, dgate, _, _ = grads(TOK, 8)
        lax.fori_loop(0, TOK // BWD_STRIP, strip, (dval, dgate))

        @pl.when(last)
        def _():
            for dw_ref in (dwv_ref, dwg_ref):
                for q in range(4):
                    dw_ref[8 * q:8 * q + 1, :] = jnp.sum(dw_ref[8 * q:8 * q + 8, :], axis=0,
                                                         keepdims=True)

    hbuf = pltpu.VMEM((TOK + 16, FF_COL), F32)
    acc = pl.BlockSpec((32, FF_COL), lambda j, i: (0, j))
    (dhu, dwv, dwg), moved = _pcall(
        body, name, (ncol, nt),
        [tile(0), nxt(0), tile(0), prev(0), nxt(0), tile(ncol), prev(ncol), nxt(ncol),
         wspec(0), wspec(ncol), bspec(0), bspec(ncol)],
        [pl.BlockSpec((2, TOK, FF_COL), lambda j, i: (0, i, j)), acc, acc],
        [_sds((2, t, D_FF), BF16), _sds((32, D_FF), F32), _sds((32, D_FF), F32)],
        [hbuf, hbuf, pltpu.VMEM((ext, FF_COL), F32)], _cparams("arbitrary", "arbitrary"),
        (da, da, hu, hu, hu, hu, hu, hu, conv_w, conv_w, conv_b, conv_b), comm)
    dconv = jnp.concatenate([dwv, dwg], axis=1).reshape(4, 8, 2 * D_FF)[:, 0]
    return (dhu, dconv) if comm is None else (dhu, dconv, moved)


def _mesh_pos():
    x, y, c = lax.axis_index("x"), lax.axis_index("y"), lax.axis_index("c")
    return x, y, c, [(1 - x, y), (x, 1 - y), (1 - x, 1 - y)]


def _any_specs(n):
    return [pl.BlockSpec(memory_space=pl.ANY)] * n


def _remote(src, dst, send_sems, recv_sems, i, dev):
    return pltpu.make_async_remote_copy(src_ref=src, dst_ref=dst, send_sem=send_sems.at[i],
                                        recv_sem=recv_sems.at[i], device_id=dev,
                                        device_id_type=MESH)


def _mine(c, rows):
    return pl.ds(pl.multiple_of(c * (rows // 2), 16), rows // 2)


def _gather_send(shards, conv_shard, gathered, l):
    nbig = len(shards)
    with_conv = conv_shard is not None
    if gathered is None:
        ins = list(shards) + ([conv_shard] if with_conv else [])
        outs = [_sds((DEPTH, N_CHIPS) + s.shape[1:], s.dtype) for s in ins]
        alias = {}
    else:
        ins = list(shards) + list(gathered)
        outs = [_sds(g.shape, g.dtype) for g in gathered]
        alias = {nbig + k: k for k in range(nbig)}

    def copies(cin, cout, ssem, rsem):
        x, y, c, chips = _mesh_pos()
        me = 2 * x + y
        out = []
        for k in range(nbig):
            rows = shards[k].shape[1]
            for j, (cx, cy) in enumerate(chips):
                out.append(_remote(cin[k].at[l, _mine(c, rows)], cout[k].at[l, me, _mine(c, rows)],
                                   ssem, rsem, 4 * k + j, (cx, cy, c)))
            out.append(_remote(cin[k].at[l], cout[k].at[l, me], ssem, rsem, 4 * k + 3,
                               (x, y, 1 - c)))
        if with_conv:
            base = 4 * nbig
            for j, (cx, cy) in enumerate(chips):
                out.append(_remote(cin[nbig].at[c], cout[nbig].at[c, me], ssem, rsem, base + j,
                                   (cx, cy, c)))
            for ll in range(DEPTH):
                out.append(_remote(cin[nbig].at[ll], cout[nbig].at[ll, me], ssem, rsem,
                                   base + 3 + ll, (x, y, 1 - c)))
        return out

    return _Comm(ins, outs, copies, 4 * nbig + 5, alias)


def _gather_forward(gathered, nbig, rows, l):
    with_conv = len(gathered) > nbig
    alias = {k: k for k in range(len(gathered))}

    def copies(cin, cout, ssem, rsem):
        x, y, c, chips = _mesh_pos()
        out = []
        for k in range(nbig):
            for j, (cx, cy) in enumerate(chips):
                blk = cout[k].at[l, 2 * cx + cy, _mine(c, rows[k])]
                out.append(_remote(blk, blk, ssem, rsem, 3 * k + j, (x, y, 1 - c)))
        if with_conv:
            for j, (cx, cy) in enumerate(chips):
                blk = cout[nbig].at[c, 2 * cx + cy]
                out.append(_remote(blk, blk, ssem, rsem, 3 * nbig + j, (x, y, 1 - c)))
        return out

    return _Comm(gathered, [_sds(g.shape, g.dtype) for g in gathered], copies, 3 * nbig + 3, alias)


def _reduce_swap(grads, l):
    def copies(cin, cout, ssem, rsem):
        x, y, c, _ = _mesh_pos()
        return [_remote(cin[k].at[l, :, _mine(1 - c, g.shape[2])], cout[k], ssem, rsem, k,
                        (x, y, 1 - c)) for k, g in enumerate(grads)]

    outs = [_sds((N_CHIPS, g.shape[2] // 2, g.shape[3]), g.dtype) for g in grads]
    return _Comm(grads, outs, copies, len(grads))


def _reduce_scatter(sums):
    def copies(cin, cout, ssem, rsem):
        x, y, c, chips = _mesh_pos()
        return [_remote(cin[k].at[2 * cx + cy], cout[k].at[j], ssem, rsem, 3 * k + j, (cx, cy, c))
                for k in range(len(sums)) for j, (cx, cy) in enumerate(chips)]

    outs = [_sds((3,) + s.shape[1:], s.dtype) for s in sums]
    return _Comm(sums, outs, copies, 3 * len(sums))


def _reduce_share(reds, l):
    def copies(cin, cout, ssem, rsem):
        x, y, c, _ = _mesh_pos()
        out = []
        for k, r in enumerate(reds):
            half = cout[k].at[l, _mine(c, r.shape[1])]
            out.append(_remote(half, half, ssem, rsem, k, (x, y, 1 - c)))
        return out

    return _Comm(reds, [_sds(r.shape, r.dtype) for r in reds], copies, len(reds),
                 {k: k for k in range(len(reds))})


def _allgather_weights(shards):
    n = len(shards)

    def body(*refs):
        ins, outs = refs[:n], refs[n:2 * n]
        send_sems, recv_sems = refs[2 * n:]
        x, y, c, chips = _mesh_pos()
        me = 2 * x + y
        started = []
        own = []
        for k in range(n):
            for l in range(2):
                cp = pltpu.make_async_remote_copy(
                    src_ref=ins[k].at[l], dst_ref=outs[k].at[l, me],
                    send_sem=send_sems.at[k, 6 + l], recv_sem=recv_sems.at[k, 6 + l],
                    device_id=(x, y, 1 - c), device_id_type=MESH)
                cp.start()
                own.append(cp)
            for j, (cx, cy) in enumerate(chips):
                cp = pltpu.make_async_remote_copy(
                    src_ref=ins[k].at[c], dst_ref=outs[k].at[c, me],
                    send_sem=send_sems.at[k, j], recv_sem=recv_sems.at[k, j],
                    device_id=(cx, cy, c), device_id_type=MESH)
                cp.start()
                started.append(cp)
        for k in range(n):
            for j, (cx, cy) in enumerate(chips):
                landed = outs[k].at[c, 2 * cx + cy]
                pltpu.make_async_remote_copy(
                    src_ref=ins[k].at[c], dst_ref=landed,
                    send_sem=send_sems.at[k, j], recv_sem=recv_sems.at[k, j],
                    device_id=(cx, cy, c), device_id_type=MESH).wait_recv()
                fw = pltpu.make_async_remote_copy(
                    src_ref=landed, dst_ref=landed,
                    send_sem=send_sems.at[k, 3 + j], recv_sem=recv_sems.at[k, 3 + j],
                    device_id=(x, y, 1 - c), device_id_type=MESH)
                fw.start()
                started.append(fw)
        for k in range(n):
            for j, (cx, cy) in enumerate(chips):
                theirs = outs[k].at[1 - c, 2 * cx + cy]
                pltpu.make_async_remote_copy(
                    src_ref=theirs, dst_ref=theirs,
                    send_sem=send_sems.at[k, 3 + j], recv_sem=recv_sems.at[k, 3 + j],
                    device_id=(x, y, 1 - c), device_id_type=MESH).wait_recv()
        for cp in started:
            cp.wait_send()
        for cp in own:
            cp.wait()

    return pl.pallas_call(
        body, name="allgather_weights",
        in_specs=_any_specs(n), out_specs=_any_specs(n),
        out_shape=[_sds((2, N_CHIPS) + s.shape[1:], s.dtype) for s in shards],
        scratch_shapes=[pltpu.SemaphoreType.DMA((n, 8)), pltpu.SemaphoreType.DMA((n, 8))],
    )(*shards)


def _swap_layers(grads):
    n = len(grads)

    def body(*refs):
        ins, outs = refs[:n], refs[n:2 * n]
        send_sems, recv_sems = refs[2 * n:]
        x, y, c, _ = _mesh_pos()
        cps = []
        for k in range(n):
            cp = pltpu.make_async_remote_copy(
                src_ref=ins[k].at[1 - c], dst_ref=outs[k],
                send_sem=send_sems.at[k], recv_sem=recv_sems.at[k],
                device_id=(x, y, 1 - c), device_id_type=MESH)
            cp.start()
            cps.append(cp)
        for cp in cps:
            cp.wait()

    return pl.pallas_call(
        body, name="swap_layers",
        in_specs=_any_specs(n), out_specs=_any_specs(n),
        out_shape=[_sds(g.shape[1:], g.dtype) for g in grads],
        scratch_shapes=[pltpu.SemaphoreType.DMA((n,)), pltpu.SemaphoreType.DMA((n,))],
    )(*grads)


def _scatter_blocks(sums):
    n = len(sums)

    def body(*refs):
        ins, outs = refs[:n], refs[n:2 * n]
        send_sems, recv_sems = refs[2 * n:]
        x, y, c, chips = _mesh_pos()
        cps = []
        for k in range(n):
            for j, (cx, cy) in enumerate(chips):
                cp = pltpu.make_async_remote_copy(
                    src_ref=ins[k].at[2 * cx + cy], dst_ref=outs[k].at[j],
                    send_sem=send_sems.at[k, j], recv_sem=recv_sems.at[k, j],
                    device_id=(cx, cy, c), device_id_type=MESH)
                cp.start()
                cps.append(cp)
        for cp in cps:
            cp.wait()

    return pl.pallas_call(
        body, name="scatter_blocks",
        in_specs=_any_specs(n), out_specs=_any_specs(n),
        out_shape=[_sds((3,) + s.shape[1:], s.dtype) for s in sums],
        scratch_shapes=[pltpu.SemaphoreType.DMA((n, 3)), pltpu.SemaphoreType.DMA((n, 3))],
    )(*sums)


def _exchange_reduced(reds):
    n = len(reds)

    def body(*refs):
        outs = refs[n:2 * n]
        send_sems, recv_sems = refs[2 * n:]
        x, y, c, _ = _mesh_pos()
        cps = []
        for k in range(n):
            cp = pltpu.make_async_remote_copy(
                src_ref=outs[k].at[c], dst_ref=outs[k].at[c],
                send_sem=send_sems.at[k], recv_sem=recv_sems.at[k],
                device_id=(x, y, 1 - c), device_id_type=MESH)
            cp.start()
            cps.append(cp)
        for k in range(n):
            pltpu.make_async_remote_copy(
                src_ref=outs[k].at[c], dst_ref=outs[k].at[1 - c],
                send_sem=send_sems.at[k], recv_sem=recv_sems.at[k],
                device_id=(x, y, 1 - c), device_id_type=MESH).wait_recv()
        for cp in cps:
            cp.wait_send()

    return pl.pallas_call(
        body, name="exchange_reduced",
        in_specs=_any_specs(n), out_specs=_any_specs(n),
        out_shape=[_sds(r.shape, r.dtype) for r in reds],
        input_output_aliases={k: k for k in range(n)},
        scratch_shapes=[pltpu.SemaphoreType.DMA((n,)), pltpu.SemaphoreType.DMA((n,))],
    )(*reds)


def _allreduce_small(pack):
    n = pack.shape[0]

    def body(x_ref, o_ref, gbuf, send_sems, recv_sems):
        x, y, c, chips = _mesh_pos()
        sibling = (x, y, 1 - c)

        def slot(px, py, pc):
            return gbuf.at[4 * px + 2 * py + pc]

        def copy(k, block, to, src=None):
            return pltpu.make_async_remote_copy(
                src_ref=slot(*block) if src is None else src, dst_ref=slot(*block),
                send_sem=send_sems.at[k], recv_sem=recv_sems.at[k],
                device_id=to, device_id_type=MESH)

        me = (x, y, c)
        first = [copy(0, me, sibling, src=x_ref)]
        first += [copy(1 + j, me, (*chip, c), src=x_ref) for j, chip in enumerate(chips)]
        for cp in first:
            cp.start()
        gbuf[4 * x + 2 * y + c] = x_ref[...]
        passed = [copy(4 + j, (*chip, c), sibling) for j, chip in enumerate(chips)]
        for j, chip in enumerate(chips):
            copy(1 + j, (*chip, c), me).wait_recv()
            passed[j].start()
        copy(0, sibling, me).wait_recv()
        for j, chip in enumerate(chips):
            copy(4 + j, (*chip, 1 - c), me).wait_recv()
        for cp in first + passed:
            cp.wait_send()
        acc = gbuf[0]
        for d in range(1, 8):
            acc = acc + gbuf[d]
        o_ref[...] = acc

    return pl.pallas_call(
        body, name="allreduce_small",
        in_specs=[pl.BlockSpec(memory_space=pltpu.VMEM)],
        out_specs=pl.BlockSpec(memory_space=pltpu.VMEM),
        out_shape=_sds((n, 128), F32),
        scratch_shapes=[pltpu.VMEM((8, n, 128), F32), pltpu.SemaphoreType.DMA((7,)),
                        pltpu.SemaphoreType.DMA((7,))],
        compiler_params=pltpu.CompilerParams(vmem_limit_bytes=VMEM_LIMIT_V7X),
    )(pack)


def _core_index():
    return jnp.reshape(lax.axis_index("c"), (1,)).astype(jnp.int32)


def _chip_index():
    return jnp.reshape(2 * lax.axis_index("x") + lax.axis_index("y"), (1,)).astype(jnp.int32)


def _chip_sum(name, stacked, sib, l):
    _, nb, r, cdim = stacked.shape
    hr = r // 2

    def body(c_ref, a_ref, b_ref, o_ref):
        o_ref[...] = (a_ref[...].astype(F32) + b_ref[...].astype(F32)).astype(BF16)

    return pl.pallas_call(
        body, name=name,
        grid_spec=pltpu.PrefetchScalarGridSpec(
            num_scalar_prefetch=1, grid=(nb,),
            in_specs=[pl.BlockSpec((None, None, hr, cdim), lambda j, cr: (l, j, cr[0], 0)),
                      pl.BlockSpec((None, hr, cdim), lambda j, cr: (j, 0, 0))],
            out_specs=pl.BlockSpec((None, hr, cdim), lambda j, cr: (j, 0, 0))),
        out_shape=_sds((nb, hr, cdim), BF16),
        compiler_params=_cparams("parallel"))(_core_index(), stacked, sib)


def _final_sum(name, sums, recv, l, fill):
    _, hr, cdim = sums.shape
    tr = hr // 2

    def body(m_ref, a_ref, b_ref, *rest):
        acc = a_ref[...].astype(F32)
        for j in range(3):
            acc = acc + b_ref[j].astype(F32)
        rest[-1][...] = acc

    in_specs = [pl.BlockSpec((None, tr, cdim), lambda i, mr: (mr[0], i, 0)),
                pl.BlockSpec((3, tr, cdim), lambda i, mr: (0, i, 0))]
    args = [jnp.concatenate([_chip_index(), _core_index()]), sums, recv]
    aliases = {}
    if fill is not None:
        in_specs.append(pl.BlockSpec(memory_space=pl.ANY))
        args.append(fill)
        aliases = {3: 0}
    return pl.pallas_call(
        body, name=name,
        grid_spec=pltpu.PrefetchScalarGridSpec(
            num_scalar_prefetch=1, grid=(2,), in_specs=in_specs,
            out_specs=pl.BlockSpec((None, tr, cdim), lambda i, mr: (l, 2 * mr[1] + i, 0))),
        out_shape=_sds((DEPTH, 2 * hr, cdim), F32), input_output_aliases=aliases,
        compiler_params=_cparams("parallel"))(*args)


def _adamw(name, w, g, m, v):
    nl, r, cdim = w.shape
    tr = r // 4 if r % 32 == 0 else r
    c1 = 1.0 - ADAM_B1 ** ADAM_STEP
    c2 = 1.0 - ADAM_B2 ** ADAM_STEP

    def body(w_ref, g_ref, m_ref, v_ref, d_ref, nm_ref, nv_ref):
        gv = g_ref[...]
        nm = ADAM_B1 * m_ref[...] + (1.0 - ADAM_B1) * gv
        nv = ADAM_B2 * v_ref[...] + (1.0 - ADAM_B2) * (gv * gv)
        nm_ref[...] = nm
        nv_ref[...] = nv
        d_ref[...] = -ADAM_LR * ((nm / c1) / (jnp.sqrt(nv / c2) + ADAM_EPS) + ADAM_WD * w_ref[...])

    spec = pl.BlockSpec((None, tr, cdim), lambda l, i: (l, i, 0))
    out = _sds(w.shape, F32)
    return pl.pallas_call(
        body, name=name, grid=(nl, r // tr),
        in_specs=[spec] * 4, out_specs=[spec] * 3, out_shape=[out] * 3,
        compiler_params=_cparams("parallel", "parallel"))(w, g, m, v)


def _rows128(a):
    return a.reshape(-1, 128)


def kernel(x, norm_mix_pre, w_in, b_gate, rel_bias, w_attn_out, w_pool_group, pool_scale, w_pool_out, w_o, norm_mix_post, norm_ffn_pre, w_up, conv_w, conv_b, w_down, norm_ffn_post, loss_target, m_norm_mix_pre, m_w_in, m_b_gate, m_rel_bias, m_w_attn_out, m_w_pool_group, m_pool_scale, m_w_pool_out, m_w_o, m_norm_mix_post, m_norm_ffn_pre, m_w_up, m_conv_w, m_conv_b, m_w_down, m_norm_ffn_post, v_norm_mix_pre, v_w_in, v_b_gate, v_rel_bias, v_w_attn_out, v_w_pool_group, v_pool_scale, v_w_pool_out, v_w_o, v_norm_mix_post, v_norm_ffn_pre, v_w_up, v_conv_w, v_conv_b, v_w_down, v_norm_ffn_post):
    t = x.shape[1]
    xs = x.reshape(t, D_MODEL)
    target = loss_target.reshape(t, D_MODEL)

    names = ["w_in", "w_attn_out", "w_pool_out", "w_o", "w_up", "w_down"]
    shards = [w.astype(BF16) for w in (w_in, w_attn_out, w_pool_out, w_o, w_up, w_down)]
    rows = [s.shape[1] for s in shards]
    nbig = len(shards)
    g = _comm_call("gather0_send", _gather_send(shards[:1], conv_w, None, 0))
    g = _comm_call("gather0_forward", _gather_forward(g, 1, rows[:1], 0))
    cw_full = jnp.transpose(g[1], (0, 2, 1, 3)).reshape(DEPTH, 3, 2 * D_FF)
    g = g[:1]
    wg_bf = w_pool_group.astype(BF16)

    def views(gathered):
        win_g, wao_g, wpo_g, wo_g, wup_g, wdn_g = gathered
        return (win_g, wao_g, wpo_g, wo_g.reshape(DEPTH, D_MODEL, D_MODEL), wup_g,
                wdn_g.reshape(DEPTH, D_FF, D_MODEL))

    saved = []
    xcur = xs
    for l in range(DEPTH):
        tag = f"l{l}_"
        bias = _bias_table(tag + "bias_table", rel_bias[l])
        h = _norm_fwd(tag + "norm_mix_pre", xcur, norm_mix_pre[l:l + 1])
        proj = _mm_nn_blocked(tag + "proj", h, g[0], l, BF16)
        if l == 0:
            att, rest = _attn_fwd(tag + "attn_fwd", proj, bias,
                                  _gather_send(shards[1:], None, None, 0))
            pooled, mixed, rest = _pool_fwd(tag + "pool_fwd", proj, wg_bf[l], pool_scale[l:l + 1],
                                            _gather_forward(rest, nbig - 1, rows[1:], 0))
            g = g + rest
        else:
            att = _attn_fwd(tag + "attn_fwd", proj, bias)
            pooled, mixed = _pool_fwd(tag + "pool_fwd", proj, wg_bf[l], pool_scale[l:l + 1])
        win_g, wao_g, wpo_g, wo_full, wup_g, wdn_full = views(g)
        ya = _narrow_nn(tag + "attn_out", att, wao_g, l)
        yb = _narrow_nn(tag + "pool_out", mixed, wpo_g, l)
        z = _gate_fwd(tag + "gate_fwd", proj, b_gate[l:l + 1], ya, yb)
        mix = _mm_nn(tag + "mix", z, wo_full, l, D_MODEL, F32)
        x1 = _norm_residual_fwd(tag + "norm_mix_post", xcur, mix, norm_mix_post[l:l + 1])
        h2 = _norm_fwd(tag + "norm_ffn_pre", x1, norm_ffn_pre[l:l + 1])
        hu = _mm_nn_blocked(tag + "ffn_up", h2, wup_g, l, BF16)
        if l == 0:
            a, g = _ffn_gate_fwd(tag + "ffn_gate_fwd", hu, cw_full[l], conv_b[l:l + 1],
                                 _gather_send(shards, None, g, 1))
            wdn_full = views(g)[5]
        else:
            a = _ffn_gate_fwd(tag + "ffn_gate_fwd", hu, cw_full[l], conv_b[l:l + 1])
        f = _mm_nn(tag + "ffn_down", a, wdn_full, l, D_FF // 2, F32)
        if l == 0:
            x2, g = _norm_residual_fwd(tag + "norm_ffn_post", x1, f, norm_ffn_post[l:l + 1],
                                       _gather_forward(g, nbig, rows, 1))
        else:
            x2 = _norm_residual_fwd(tag + "norm_ffn_post", x1, f, norm_ffn_post[l:l + 1])
        saved.append(dict(x=xcur, h=h, proj=proj, att=att, pooled=pooled, mixed=mixed, ya=ya,
                          yb=yb, z=z, mix=mix, x1=x1, h2=h2, hu=hu, a=a, f=f, bias=bias))
        xcur = x2
    win_g, wao_g, wpo_g, wo_full, wup_g, wdn_full = views(g)

    dy, loss_local = _loss_head(xcur, target)
    loss = lax.psum(loss_local, ("x", "y", "c"))

    dx = dy
    dws = dict.fromkeys(names)
    reds = [None] * nbig
    small_grads = [None] * DEPTH
    ffn = [4, 5]
    outs3 = [1, 2, 3]

    def blocks(ks):
        return [dws[names[k]].reshape(DEPTH, N_CHIPS, rows[k], -1) for k in ks]

    def chip_sums(ks, sib, l):
        return [_chip_sum(f"chip_sum{l}_" + names[k], b, s, l)
                for k, b, s in zip(ks, blocks(ks), sib)]

    def final_sums(ks, sums, recv, l):
        for k, s, r in zip(ks, sums, recv):
            reds[k] = _final_sum(f"final_sum{l}_" + names[k], s, r, l, reds[k])

    for l in reversed(range(DEPTH)):
        tag = f"l{l}_"
        sv = saved[l]
        every = list(range(nbig))
        df, d_nfpost = _norm_post_bwd(tag + "norm_ffn_post_bwd", dx, sv["f"], norm_ffn_post[l:l + 1])
        if l == 0:
            da, sib = _mm_nt(tag + "ffn_down_dx", df, wdn_full, l, D_FF // 2, BF16,
                             _reduce_swap(blocks(every), 1))
            sums = chip_sums(every, sib, 1)
        else:
            da = _mm_nt(tag + "ffn_down_dx", df, wdn_full, l, D_FF // 2, BF16)
        dws["w_down"] = _mm_tn(tag + "ffn_down_dw", sv["a"], df, D_FF // 2, l, dws["w_down"])
        if l == 0:
            dhu, dconv, recv = _ffn_gate_bwd(tag + "ffn_gate_bwd", da, sv["hu"], cw_full[l],
                                             conv_b[l:l + 1], _reduce_scatter(sums))
            final_sums(every, sums, recv, 1)
            dh2, reds = _mm_nt_blocked(tag + "ffn_up_dx", dhu, wup_g, l, F32,
                                       _reduce_share(reds, 1))
        else:
            dhu, dconv = _ffn_gate_bwd(tag + "ffn_gate_bwd", da, sv["hu"], cw_full[l],
                                       conv_b[l:l + 1])
            dh2 = _mm_nt_blocked(tag + "ffn_up_dx", dhu, wup_g, l, F32)
        dws["w_up"] = _mm_tn_blocked(tag + "ffn_up_dw", sv["h2"], dhu, l, dws["w_up"])
        if l == 0:
            dx1, d_nfpre, sib = _norm_pre_bwd(tag + "norm_ffn_pre_bwd", dh2, sv["x1"], dx,
                                              norm_ffn_pre[l:l + 1], _reduce_swap(blocks(ffn), 0))
            sums = chip_sums(ffn, sib, 0)
        else:
            dx1, d_nfpre = _norm_pre_bwd(tag + "norm_ffn_pre_bwd", dh2, sv["x1"], dx,
                                         norm_ffn_pre[l:l + 1])
        dmix, d_nmpost = _norm_post_bwd(tag + "norm_mix_post_bwd", dx1, sv["mix"], norm_mix_post[l:l + 1])
        dz = _mm_nt(tag + "mix_dx", dmix, wo_full, l, D_MODEL, BF16)
        dws["w_o"] = _mm_tn(tag + "mix_dw", sv["z"], dmix, D_MODEL, l, dws["w_o"])
        dya, dyb, dgates, d_bgate = _gate_bwd(tag + "gate_bwd", dz, sv["proj"], b_gate[l:l + 1],
                                              sv["ya"], sv["yb"])
        datt = _narrow_nt(tag + "attn_out_dx", dya, wao_g, l)
        dws["w_attn_out"] = _narrow_tn(tag + "attn_out_dw", sv["att"], dya, l, dws["w_attn_out"])
        dmixed = _narrow_nt(tag + "pool_out_dx", dyb, wpo_g, l)
        dws["w_pool_out"] = _narrow_tn(tag + "pool_out_dw", sv["mixed"], dyb, l, dws["w_pool_out"])
        if l == 0:
            du, d_wg, d_pscale, sib = _pool_bwd(tag + "pool_bwd", dmixed, sv["pooled"], wg_bf[l],
                                                pool_scale[l:l + 1], _reduce_swap(blocks(outs3), 0))
            sums3 = chip_sums(outs3, sib, 0)
            dq, dk, dv, dbias, recv = _attn_bwd(
                tag + "attn_bwd", sv["proj"], datt, sv["bias"],
                _both(_reduce_scatter(sums), _reduce_scatter(sums3)))
            final_sums(ffn, sums, recv[:len(ffn)], 0)
            final_sums(outs3, sums3, recv[len(ffn):], 0)
        else:
            du, d_wg, d_pscale = _pool_bwd(tag + "pool_bwd", dmixed, sv["pooled"], wg_bf[l],
                                           pool_scale[l:l + 1])
            dq, dk, dv, dbias = _attn_bwd(tag + "attn_bwd", sv["proj"], datt, sv["bias"])
        d_rel = _bias_fold(tag + "bias_fold", dbias)
        dproj = jnp.concatenate([dq, dk, dv, du, dgates], axis=1)
        if l == 0:
            dh, shared = _mm_nt_blocked(tag + "proj_dx", dproj, win_g, l, F32,
                                        _reduce_share([reds[k] for k in ffn + outs3], 0))
            for k, r in zip(ffn + outs3, shared):
                reds[k] = r
        else:
            dh = _mm_nt_blocked(tag + "proj_dx", dproj, win_g, l, F32)
        dws["w_in"] = _mm_tn_blocked(tag + "proj_dw", sv["h"], dproj, l, dws["w_in"])
        dx, d_nmpre = _norm_pre_bwd(tag + "norm_mix_pre_bwd", dh, sv["x"], dx1, norm_mix_pre[l:l + 1])
        small_grads[l] = [d_nmpre, d_nmpost, d_nfpre, d_nfpost, d_bgate, d_rel, d_wg, d_pscale,
                          dconv[3:4], dconv[0:3]]

    grad_x = dx.reshape(x.shape)

    sib = _comm_call("reduce_swap", _reduce_swap(blocks([0]), 0))
    sums = chip_sums([0], sib, 0)
    recv = _comm_call("reduce_scatter", _reduce_scatter(sums))
    final_sums([0], sums, recv, 0)
    g_big = _comm_call("reduce_share", _reduce_share([reds[0]], 0)) + reds[1:]

    pieces = []
    for idx in range(10):
        pieces.append(jnp.stack([small_grads[0][idx], small_grads[1][idx]]))
    pack = jnp.concatenate([_rows128(p) for p in pieces], axis=0)
    red = _allreduce_small(pack)
    shapes = [p.shape for p in pieces]
    outs = []
    row = 0
    for shp in shapes:
        nrow = math.prod(shp) // 128
        outs.append(red[row:row + nrow].reshape(shp))
        row += nrow
    (g_nmpre, g_nmpost, g_nfpre, g_nfpost, g_bgate, g_rel, g_wg, g_pscale, g_cb, g_cw) = outs
    g_nmpre, g_nmpost, g_nfpre, g_nfpost = [a.reshape(DEPTH, D_MODEL)
                                            for a in (g_nmpre, g_nmpost, g_nfpre, g_nfpost)]
    g_bgate = g_bgate.reshape(DEPTH, 2 * D_MODEL)
    g_rel = g_rel[:, :, :N_REL]
    g_pscale = g_pscale.reshape(DEPTH, POOL_W)
    g_cb = g_cb.reshape(DEPTH, 2 * D_FF)
    ncw = conv_w.shape[2]
    chip = 2 * lax.axis_index("x") + lax.axis_index("y")
    g_cw = lax.dynamic_slice_in_dim(g_cw, chip * ncw, ncw, axis=2)

    grads = dict(norm_mix_pre=g_nmpre, w_in=g_big[0], b_gate=g_bgate, rel_bias=g_rel,
                 w_attn_out=g_big[1], w_pool_group=g_wg, pool_scale=g_pscale, w_pool_out=g_big[2],
                 w_o=g_big[3], norm_mix_post=g_nmpost, norm_ffn_pre=g_nfpre, w_up=g_big[4],
                 conv_w=g_cw, conv_b=g_cb, w_down=g_big[5], norm_ffn_post=g_nfpost)
    weights = dict(norm_mix_pre=norm_mix_pre, w_in=w_in, b_gate=b_gate, rel_bias=rel_bias,
                   w_attn_out=w_attn_out, w_pool_group=w_pool_group, pool_scale=pool_scale,
                   w_pool_out=w_pool_out, w_o=w_o, norm_mix_post=norm_mix_post,
                   norm_ffn_pre=norm_ffn_pre, w_up=w_up, conv_w=conv_w, conv_b=conv_b,
                   w_down=w_down, norm_ffn_post=norm_ffn_post)
    moms = dict(norm_mix_pre=(m_norm_mix_pre, v_norm_mix_pre), w_in=(m_w_in, v_w_in),
                b_gate=(m_b_gate, v_b_gate), rel_bias=(m_rel_bias, v_rel_bias),
                w_attn_out=(m_w_attn_out, v_w_attn_out),
                w_pool_group=(m_w_pool_group, v_w_pool_group),
                pool_scale=(m_pool_scale, v_pool_scale), w_pool_out=(m_w_pool_out, v_w_pool_out),
                w_o=(m_w_o, v_w_o), norm_mix_post=(m_norm_mix_post, v_norm_mix_post),
                norm_ffn_pre=(m_norm_ffn_pre, v_norm_ffn_pre), w_up=(m_w_up, v_w_up),
                conv_w=(m_conv_w, v_conv_w), conv_b=(m_conv_b, v_conv_b),
                w_down=(m_w_down, v_w_down), norm_ffn_post=(m_norm_ffn_post, v_norm_ffn_post))
    order = list(weights.keys())

    delta, new_m, new_v = {}, {}, {}
    small_names = [nm for nm in order if nm not in names]
    for nm in names:
        delta[nm], new_m[nm], new_v[nm] = _adamw("adamw_" + nm, weights[nm], grads[nm], *moms[nm])

    def pack_small(get):
        flat = [get(nm).reshape(-1) for nm in small_names]
        total = sum(f.shape[0] for f in flat)
        padded = -(-total // 1024) * 1024
        flat.append(jnp.zeros((padded - total,), F32))
        return jnp.concatenate(flat).reshape(1, padded // 128, 128)

    d_s, m_s, v_s = _adamw(
        "adamw_small", pack_small(lambda nm: weights[nm]), pack_small(lambda nm: grads[nm]),
        pack_small(lambda nm: moms[nm][0]) , pack_small(lambda nm: moms[nm][1]))
    off = 0
    for nm in small_names:
        size = math.prod(weights[nm].shape)
        for dst, src in ((delta, d_s), (new_m, m_s), (new_v, v_s)):
            dst[nm] = src.reshape(-1)[off:off + size].reshape(weights[nm].shape)
        off += size

    return (loss, grad_x, *[grads[nm] for nm in order], *[delta[nm] for nm in order],
            *[new_m[nm] for nm in order], *[new_v[nm] for nm in order])
```

```python
import functools
import math

import jax
import jax.numpy as jnp
from jax import lax
from jax.experimental import pallas as pl
from jax.experimental.pallas import tpu as pltpu

F32 = jnp.float32
BF16 = jnp.bfloat16
MESH = pl.DeviceIdType.MESH

D_MODEL = 1024
DEPTH = 2
CHUNK = 64
BAND_CHUNKS = 9
BAND = BAND_CHUNKS * CHUNK
HEADS = 8
HEAD_DIM = 64
ATTN_W = HEADS * HEAD_DIM
POOL_WINDOWS = (2, 4, 8, 16)
POOL_W = 512
POOL_GD = 128
MAX_REL = 256
N_REL = 2 * MAX_REL + 1
D_FF = 2816
IN_W = 3 * ATTN_W + POOL_W + 2 * D_MODEL
EPS = 1e-6
ATTN_SCALE = HEAD_DIM ** -0.5
BAND_PAD = 640
BIAS_LANES = BAND_PAD
N_CHIPS = 4

ADAM_LR = 0.001
ADAM_B1 = 0.9
ADAM_B2 = 0.999
ADAM_EPS = 1e-08
ADAM_WD = 0.01
ADAM_STEP = 10

VMEM_LIMIT_V7X = 56 * 1024 * 1024
TOK = 512
ATT_BLK = 8 * CHUNK
FF_COL = 256
FF_TOK = 1024
HALO = 32


def _cparams(*sem):
    return pltpu.CompilerParams(dimension_semantics=sem, vmem_limit_bytes=VMEM_LIMIT_V7X)


def _sds(shape, dtype):
    return jax.ShapeDtypeStruct(shape, dtype)


class _Comm:
    def __init__(self, ins, outs, copies, n_sems, alias=None):
        self.ins, self.outs, self.copies, self.n_sems = list(ins), list(outs), copies, n_sems
        self.alias = dict(alias or {})


class _SemsFrom:
    def __init__(self, sems, start):
        self.sems, self.start = sems, start

    @property
    def at(self):
        return self

    def __getitem__(self, i):
        return self.sems.at[self.start + i]


def _both(a, b):
    na, nao = len(a.ins), len(a.outs)

    def copies(cin, cout, ssem, rsem):
        return (a.copies(cin[:na], cout[:nao], ssem, rsem)
                + b.copies(cin[na:], cout[nao:], _SemsFrom(ssem, a.n_sems), _SemsFrom(rsem, a.n_sems)))

    alias = dict(a.alias)
    alias.update({na + i: nao + o for i, o in b.alias.items()})
    return _Comm(a.ins + b.ins, a.outs + b.outs, copies, a.n_sems + b.n_sems, alias)


def _pcall(body, name, grid, in_specs, out_specs, out_shape, scratch_shapes, compiler_params, args,
           comm=None, aliases=None):
    single = not isinstance(out_shape, (list, tuple))
    out_specs = [out_specs] if single else list(out_specs)
    out_shape = [out_shape] if single else list(out_shape)
    n_in, n_out = len(in_specs), len(out_specs)
    aliases = dict(aliases or {})
    if comm is None:
        res = pl.pallas_call(
            body, name=name, grid=grid, in_specs=list(in_specs), out_specs=out_specs,
            out_shape=out_shape, scratch_shapes=list(scratch_shapes),
            input_output_aliases=aliases, compiler_params=compiler_params)(*args)
        return (res[0] if single else res), None
    ci, co = len(comm.ins), len(comm.outs)

    def hosted(*refs):
        main_in, cin = refs[:n_in], refs[n_in:n_in + ci]
        main_out = refs[n_in + ci:n_in + ci + n_out]
        cout = refs[n_in + ci + n_out:n_in + ci + n_out + co]
        rest = refs[n_in + ci + n_out + co:]
        copies = comm.copies(cin, cout, rest[-2], rest[-1])
        ids = [pl.program_id(a) for a in range(len(grid))]
        first = functools.reduce(jnp.logical_and, [i == 0 for i in ids])
        last = functools.reduce(jnp.logical_and, [i == g - 1 for i, g in zip(ids, grid)])

        @pl.when(first)
        def _():
            for cp in copies:
                cp.start()

        body(*main_in, *main_out, *rest[:-2])

        @pl.when(last)
        def _():
            for cp in copies:
                cp.wait()

    for i, o in comm.alias.items():
        aliases[n_in + i] = n_out + o
    hbm = pl.BlockSpec(memory_space=pl.ANY)
    sems = pltpu.SemaphoreType.DMA((comm.n_sems,))
    res = pl.pallas_call(
        hosted, name=name, grid=grid, in_specs=list(in_specs) + [hbm] * ci,
        out_specs=out_specs + [hbm] * co, out_shape=out_shape + comm.outs,
        scratch_shapes=list(scratch_shapes) + [sems, sems],
        input_output_aliases=aliases, compiler_params=compiler_params)(*args, *comm.ins)
    return (res[0] if single else list(res[:n_out])), list(res[n_out:])


def _comm_call(name, comm):
    ci = len(comm.ins)

    def body(*refs):
        copies = comm.copies(refs[:ci], refs[ci:-2], refs[-2], refs[-1])
        for cp in copies:
            cp.start()
        for cp in copies:
            cp.wait()

    hbm = pl.BlockSpec(memory_space=pl.ANY)
    sems = pltpu.SemaphoreType.DMA((comm.n_sems,))
    return list(pl.pallas_call(
        body, name=name, in_specs=[hbm] * ci, out_specs=[hbm] * len(comm.outs),
        out_shape=comm.outs, scratch_shapes=[sems, sems],
        input_output_aliases=comm.alias)(*comm.ins))


def _matmul(name, a, b, a_spec, b_spec, o_spec, out_shape, grid, contract, nk, acc_shape,
            fill=None, comm=None):
    def body(*refs):
        a_ref, b_ref = refs[0], refs[1]
        o_ref = refs[2 if fill is None else 3]
        scratch = refs[(3 if fill is None else 4):]
        part = lax.dot_general(a_ref[...], b_ref[...], (contract, ((), ())),
                               preferred_element_type=F32)
        if nk == 1:
            o_ref[...] = part.astype(o_ref.dtype)
        else:
            acc_ref = scratch[0]
            k = pl.program_id(2)

            @pl.when(k == 0)
            def _():
                acc_ref[...] = part

            @pl.when(k > 0)
            def _():
                acc_ref[...] += part

            @pl.when(k == nk - 1)
            def _():
                o_ref[...] = acc_ref[...].astype(o_ref.dtype)

    scratch = [] if nk == 1 else [pltpu.VMEM(acc_shape, F32)]
    in_specs, args, aliases = [a_spec, b_spec], [a, b], {}
    if fill is not None:
        in_specs.append(pl.BlockSpec(memory_space=pl.ANY))
        args.append(fill)
        aliases = {2: 0}
    out, moved = _pcall(body, name, grid, in_specs, o_spec, out_shape, scratch,
                        _cparams("parallel", "parallel", "arbitrary"), args, comm, aliases)
    return out if comm is None else (out, moved)


NN = ((1,), (0,))
NT = ((1,), (1,))
TN = ((0,), (0,))


def _tm(t):
    return min(t, 1024)


def _col_block_spec(a, rows, nb, row_col):
    if a.ndim == 2:
        return pl.BlockSpec((rows, nb), row_col)

    def halves(*ids):
        r, c = row_col(*ids)
        return c // 2, r, c % 2

    return pl.BlockSpec((None, rows, nb), halves)


def _mm_nn_blocked(name, a, w, l, out_dtype):
    t, k = a.shape
    nb = w.shape[3]
    tm = _tm(t)
    return _matmul(
        name, a, w,
        pl.BlockSpec((tm, k), lambda i, n, kk: (i, 0)),
        pl.BlockSpec((None, None, k, nb), lambda i, n, kk: (l, n, 0, 0)),
        pl.BlockSpec((tm, nb), lambda i, n, kk: (i, n)),
        _sds((t, N_CHIPS * nb), out_dtype), (t // tm, N_CHIPS, 1), NN, 1, None)


def _mm_nt_blocked(name, a, w, l, out_dtype, comm=None):
    t = a.shape[-2]
    k, nb = w.shape[2], w.shape[3]
    tm = _tm(t)
    return _matmul(
        name, a, w,
        _col_block_spec(a, tm, nb, lambda i, n, kk: (i, kk)),
        pl.BlockSpec((None, None, k, nb), lambda i, n, kk: (l, kk, 0, 0)),
        pl.BlockSpec((tm, k), lambda i, n, kk: (i, 0)),
        _sds((t, k), out_dtype), (t // tm, 1, N_CHIPS), NT, N_CHIPS, (tm, k), comm=comm)


def _mm_tn_blocked(name, a, g, l, fill):
    t, k = a.shape
    nb = g.shape[-1] * (g.ndim - 1) // N_CHIPS
    tt = _tm(t)
    nt = t // tt
    return _matmul(
        name, a, g,
        pl.BlockSpec((tt, k), lambda n, j, kk: (kk, 0)),
        _col_block_spec(g, tt, nb, lambda n, j, kk: (kk, n)),
        pl.BlockSpec((None, None, k, nb), lambda n, j, kk: (l, n, 0, 0)),
        _sds((DEPTH, N_CHIPS, k, nb), BF16), (N_CHIPS, 1, nt), TN, nt, (k, nb), fill)


def _proj_pieces(rows, dqkv_first):
    def piece(col):
        if dqkv_first:
            return pl.BlockSpec((rows, ATTN_W), lambda i, kk: (i, col))
        return pl.BlockSpec((rows, ATTN_W), lambda n, kk: (kk, col))
    return [piece(0), piece(1), piece(2), piece(0)]


def _proj_dx(name, dqkv, du, dgates, w, l, comm=None):
    t = du.shape[0]
    k, nb = w.shape[2], w.shape[3]
    tm = _tm(t)

    def body(dq_ref, dk_ref, dv_ref, du_ref, dg_ref, w_ref, o_ref, acc_ref):
        kk = pl.program_id(1)

        def mm(a):
            return lax.dot_general(a, w_ref[...], (NT, ((), ())), preferred_element_type=F32)

        @pl.when(kk == 0)
        def _():
            acc_ref[...] = mm(jnp.concatenate([dq_ref[...], dk_ref[...]], axis=1))

        @pl.when(kk == 1)
        def _():
            acc_ref[...] += mm(jnp.concatenate([dv_ref[...], du_ref[...]], axis=1))

        @pl.when(kk >= 2)
        def _():
            acc_ref[...] += mm(dg_ref[...])

        @pl.when(kk == N_CHIPS - 1)
        def _():
            o_ref[...] = acc_ref[...]

    out, moved = _pcall(
        body, name, (t // tm, N_CHIPS),
        _proj_pieces(tm, True)
        + [pl.BlockSpec((tm, nb), lambda i, kk: (i, jnp.maximum(kk - 2, 0))),
           pl.BlockSpec((None, None, k, nb), lambda i, kk: (l, kk, 0, 0))],
        pl.BlockSpec((tm, k), lambda i, kk: (i, 0)), _sds((t, k), F32),
        [pltpu.VMEM((tm, k), F32)], _cparams("arbitrary", "arbitrary"),
        (dqkv, dqkv, dqkv, du, dgates, w), comm)
    return out if comm is None else (out, moved)


def _proj_dw(name, h, dqkv, du, dgates, l, fill):
    t, k = h.shape
    nb = dgates.shape[1] // 2
    tt = _tm(t)
    nt = t // tt

    def body(*refs):
        h_ref, dq_ref, dk_ref, dv_ref, du_ref, dg_ref = refs[:6]
        o_ref, acc_ref = refs[-2], refs[-1]
        n, kk = pl.program_id(0), pl.program_id(1)

        def update(g):
            part = lax.dot_general(h_ref[...], g, (TN, ((), ())), preferred_element_type=F32)

            @pl.when(kk == 0)
            def _():
                acc_ref[...] = part

            @pl.when(kk > 0)
            def _():
                acc_ref[...] += part

        @pl.when(n == 0)
        def _():
            update(jnp.concatenate([dq_ref[...], dk_ref[...]], axis=1))

        @pl.when(n == 1)
        def _():
            update(jnp.concatenate([dv_ref[...], du_ref[...]], axis=1))

        @pl.when(n >= 2)
        def _():
            update(dg_ref[...])

        @pl.when(kk == nt - 1)
        def _():
            o_ref[...] = acc_ref[...].astype(BF16)

    in_specs = ([pl.BlockSpec((tt, k), lambda n, kk: (kk, 0))] + _proj_pieces(tt, False)
                + [pl.BlockSpec((tt, nb), lambda n, kk: (kk, jnp.maximum(n - 2, 0)))])
    args, aliases = [h, dqkv, dqkv, dqkv, du, dgates], {}
    if fill is not None:
        in_specs.append(pl.BlockSpec(memory_space=pl.ANY))
        args.append(fill)
        aliases = {6: 0}
    return pl.pallas_call(
        body, name=name, grid=(N_CHIPS, nt), in_specs=in_specs,
        out_specs=pl.BlockSpec((None, None, k, nb), lambda n, kk: (l, n, 0, 0)),
        out_shape=_sds((DEPTH, N_CHIPS, k, nb), BF16),
        scratch_shapes=[pltpu.VMEM((k, nb), F32)], input_output_aliases=aliases,
        compiler_params=_cparams("parallel", "arbitrary"))(*args)


def _narrow_nn(name, a, w, l):
    t, k = a.shape
    nb = w.shape[3]
    tm = _tm(t)

    def body(a_ref, w_ref, o_ref):
        av = a_ref[...]
        for j in range(N_CHIPS):
            o_ref[:, j * nb:(j + 1) * nb] = jnp.dot(
                av, w_ref[j], preferred_element_type=F32).astype(BF16)

    return pl.pallas_call(
        body, name=name, grid=(t // tm,),
        in_specs=[pl.BlockSpec((tm, k), lambda i: (i, 0)),
                  pl.BlockSpec((None, N_CHIPS, k, nb), lambda i: (l, 0, 0, 0))],
        out_specs=pl.BlockSpec((tm, N_CHIPS * nb), lambda i: (i, 0)),
        out_shape=_sds((t, N_CHIPS * nb), BF16), compiler_params=_cparams("parallel"))(a, w)


def _narrow_nt(name, a, w, l):
    t = a.shape[0]
    k, nb = w.shape[2], w.shape[3]
    tm = _tm(t)

    def body(a_ref, w_ref, o_ref):
        acc = lax.dot_general(a_ref[:, 0:nb], w_ref[0], (NT, ((), ())), preferred_element_type=F32)
        for j in range(1, N_CHIPS):
            acc = acc + lax.dot_general(a_ref[:, j * nb:(j + 1) * nb], w_ref[j], (NT, ((), ())),
                                        preferred_element_type=F32)
        o_ref[...] = acc.astype(BF16)

    return pl.pallas_call(
        body, name=name, grid=(t // tm,),
        in_specs=[pl.BlockSpec((tm, N_CHIPS * nb), lambda i: (i, 0)),
                  pl.BlockSpec((None, N_CHIPS, k, nb), lambda i: (l, 0, 0, 0))],
        out_specs=pl.BlockSpec((tm, k), lambda i: (i, 0)),
        out_shape=_sds((t, k), BF16), compiler_params=_cparams("parallel"))(a, w)


def _narrow_tn(name, a, g, l, fill):
    t, k = a.shape
    nb = g.shape[1] // N_CHIPS
    tt = _tm(t)
    nt = t // tt

    def body(*refs):
        a_ref, g_ref, o_ref, acc_ref = refs[0], refs[1], refs[-2], refs[-1]
        i = pl.program_id(0)
        part = lax.dot_general(a_ref[...], g_ref[...], (TN, ((), ())), preferred_element_type=F32)

        @pl.when(i == 0)
        def _():
            acc_ref[...] = part

        @pl.when(i > 0)
        def _():
            acc_ref[...] += part

        @pl.when(i == nt - 1)
        def _():
            for j in range(N_CHIPS):
                o_ref[j] = acc_ref[:, j * nb:(j + 1) * nb].astype(BF16)

    in_specs = [pl.BlockSpec((tt, k), lambda i: (i, 0)),
                pl.BlockSpec((tt, N_CHIPS * nb), lambda i: (i, 0))]
    args, aliases = [a, g], {}
    if fill is not None:
        in_specs.append(pl.BlockSpec(memory_space=pl.ANY))
        args.append(fill)
        aliases = {2: 0}
    return pl.pallas_call(
        body, name=name, grid=(nt,), in_specs=in_specs,
        out_specs=pl.BlockSpec((None, N_CHIPS, k, nb), lambda i: (l, 0, 0, 0)),
        out_shape=_sds((DEPTH, N_CHIPS, k, nb), BF16),
        scratch_shapes=[pltpu.VMEM((k, N_CHIPS * nb), F32)], input_output_aliases=aliases,
        compiler_params=_cparams("arbitrary"))(*args)


def _mm_nn(name, a, w, l, tk, out_dtype):
    t, k = a.shape
    n = w.shape[2]
    tm = _tm(t)
    nk = k // tk
    return _matmul(
        name, a, w,
        pl.BlockSpec((tm, tk), lambda i, j, kk: (i, kk)),
        pl.BlockSpec((None, tk, n), lambda i, j, kk: (l, kk, 0)),
        pl.BlockSpec((tm, n), lambda i, j, kk: (i, 0)),
        _sds((t, n), out_dtype), (t // tm, 1, nk), NN, nk, (tm, n))


def _mm_nt(name, a, w, l, tn, out_dtype, comm=None):
    t, n = a.shape
    k = w.shape[1]
    tm = _tm(t)
    return _matmul(
        name, a, w,
        pl.BlockSpec((tm, n), lambda i, j, kk: (i, 0)),
        pl.BlockSpec((None, tn, n), lambda i, j, kk: (l, j, 0)),
        pl.BlockSpec((tm, tn), lambda i, j, kk: (i, j)),
        _sds((t, k), out_dtype), (t // tm, k // tn, 1), NT, 1, None, comm=comm)


def _mm_tn(name, a, g, tko, l, fill):
    t, k = a.shape
    n = g.shape[1]
    tt = _tm(t)
    nt = t // tt
    return _matmul(
        name, a, g,
        pl.BlockSpec((tt, tko), lambda i, j, kk: (kk, i)),
        pl.BlockSpec((tt, n), lambda i, j, kk: (kk, 0)),
        pl.BlockSpec((None, tko, n), lambda i, j, kk: (l, i, 0)),
        _sds((DEPTH, k, n), BF16), (k // tko, 1, nt), TN, nt, (tko, n), fill)


def _row_spec(width, col=0):
    return pl.BlockSpec((TOK, width), lambda i: (i, col))


def _vec_spec(width):
    return pl.BlockSpec((1, width), lambda i: (0, 0))


def _rms(x):
    return lax.rsqrt(jnp.mean(x * x, axis=-1, keepdims=True) + EPS)


def _norm_fwd(name, x, g):
    t = x.shape[0]

    def body(x_ref, g_ref, h_ref):
        xv = x_ref[...]
        h_ref[...] = (xv * _rms(xv) * g_ref[...]).astype(BF16)

    return pl.pallas_call(
        body, name=name, grid=(t // TOK,), in_specs=[_row_spec(D_MODEL), _vec_spec(D_MODEL)],
        out_specs=_row_spec(D_MODEL), out_shape=_sds((t, D_MODEL), BF16),
        compiler_params=_cparams("parallel"))(x, g)


def _norm_residual_fwd(name, xres, m, g, comm=None):
    t = xres.shape[0]

    def body(x_ref, m_ref, g_ref, o_ref):
        mv = m_ref[...]
        o_ref[...] = x_ref[...] + mv * _rms(mv) * g_ref[...]

    out, moved = _pcall(
        body, name, (t // TOK,),
        [_row_spec(D_MODEL), _row_spec(D_MODEL), _vec_spec(D_MODEL)],
        _row_spec(D_MODEL), _sds((t, D_MODEL), F32), [], _cparams("arbitrary"),
        (xres, m, g), comm)
    return out if comm is None else (out, moved)


def _norm_post_bwd(name, dxo, m, g):
    t = dxo.shape[0]

    def body(d_ref, m_ref, g_ref, dm_ref, dg_ref):
        mv = m_ref[...]
        dv = d_ref[...]
        r = _rms(mv)
        n = mv * r
        dn = dv * g_ref[...]
        dm_ref[...] = (r * (dn - n * jnp.mean(dn * n, axis=-1, keepdims=True))).astype(BF16)
        part = jnp.sum(dv * n, axis=0, keepdims=True)

        @pl.when(pl.program_id(0) == 0)
        def _():
            dg_ref[...] = part

        @pl.when(pl.program_id(0) > 0)
        def _():
            dg_ref[...] += part

    return pl.pallas_call(
        body, name=name, grid=(t // TOK,),
        in_specs=[_row_spec(D_MODEL), _row_spec(D_MODEL), _vec_spec(D_MODEL)],
        out_specs=[_row_spec(D_MODEL), _vec_spec(D_MODEL)],
        out_shape=[_sds((t, D_MODEL), BF16), _sds((1, D_MODEL), F32)],
        compiler_params=_cparams("arbitrary"))(dxo, m, g)


def _norm_pre_bwd(name, dh, xin, dxo, g, comm=None):
    t = dh.shape[0]

    def body(dh_ref, x_ref, d_ref, g_ref, dx_ref, dg_ref):
        xv = x_ref[...]
        dhv = dh_ref[...]
        r = _rms(xv)
        n = xv * r
        dn = dhv * g_ref[...]
        dx_ref[...] = d_ref[...] + r * (dn - n * jnp.mean(dn * n, axis=-1, keepdims=True))
        part = jnp.sum(dhv * n, axis=0, keepdims=True)

        @pl.when(pl.program_id(0) == 0)
        def _():
            dg_ref[...] = part

        @pl.when(pl.program_id(0) > 0)
        def _():
            dg_ref[...] += part

    out, moved = _pcall(
        body, name, (t // TOK,),
        [_row_spec(D_MODEL), _row_spec(D_MODEL), _row_spec(D_MODEL), _vec_spec(D_MODEL)],
        [_row_spec(D_MODEL), _vec_spec(D_MODEL)],
        [_sds((t, D_MODEL), F32), _sds((1, D_MODEL), F32)], [], _cparams("arbitrary"),
        (dh, xin, dxo, g), comm)
    return out if comm is None else (*out, moved)


def _loss_head(y, target):
    t = y.shape[0]

    def body(y_ref, t_ref, dy_ref, l_ref):
        e = y_ref[...] - t_ref[...]
        dy_ref[...] = e * (1.0 / D_MODEL)
        part = jnp.sum(jnp.sum(e * e, axis=0, keepdims=True), axis=1, keepdims=True)

        @pl.when(pl.program_id(0) == 0)
        def _():
            l_ref[...] = part

        @pl.when(pl.program_id(0) > 0)
        def _():
            l_ref[...] += part

    dy, sq = pl.pallas_call(
        body, name="loss_head", grid=(t // TOK,),
        in_specs=[_row_spec(D_MODEL), _row_spec(D_MODEL)],
        out_specs=[_row_spec(D_MODEL), pl.BlockSpec((1, 1), lambda i: (0, 0))],
        out_shape=[_sds((t, D_MODEL), F32), _sds((1, 1), F32)],
        compiler_params=_cparams("arbitrary"))(y, target)
    return dy, sq[0, 0] * (0.5 / D_MODEL)


def _gate_fwd(name, proj, b_gate, ya, yb):
    t = proj.shape[0]

    def body(ga_ref, gb_ref, b_ref, ya_ref, yb_ref, z_ref):
        sa = jax.nn.sigmoid(ga_ref[...].astype(F32) + b_ref[:, :D_MODEL])
        sb = jax.nn.sigmoid(gb_ref[...].astype(F32) + b_ref[:, D_MODEL:])
        z_ref[...] = (sa * ya_ref[...].astype(F32) + sb * yb_ref[...].astype(F32)).astype(BF16)

    return pl.pallas_call(
        body, name=name, grid=(t // TOK,),
        in_specs=[_row_spec(D_MODEL, 2), _row_spec(D_MODEL, 3), _vec_spec(2 * D_MODEL),
                  _row_spec(D_MODEL), _row_spec(D_MODEL)],
        out_specs=_row_spec(D_MODEL), out_shape=_sds((t, D_MODEL), BF16),
        compiler_params=_cparams("parallel"))(proj, proj, b_gate, ya, yb)


def _gate_bwd(name, dz, proj, b_gate, ya, yb):
    t = proj.shape[0]

    def body(dz_ref, ga_ref, gb_ref, b_ref, ya_ref, yb_ref, dya_ref, dyb_ref, dg_ref, db_ref):
        dzv = dz_ref[...].astype(F32)
        sa = jax.nn.sigmoid(ga_ref[...].astype(F32) + b_ref[:, :D_MODEL])
        sb = jax.nn.sigmoid(gb_ref[...].astype(F32) + b_ref[:, D_MODEL:])
        dya_ref[...] = (dzv * sa).astype(BF16)
        dyb_ref[...] = (dzv * sb).astype(BF16)
        dga = dzv * ya_ref[...].astype(F32) * sa * (1.0 - sa)
        dgb = dzv * yb_ref[...].astype(F32) * sb * (1.0 - sb)
        dg_ref[:, :D_MODEL] = dga.astype(BF16)
        dg_ref[:, D_MODEL:] = dgb.astype(BF16)
        pa = jnp.sum(dga, axis=0, keepdims=True)
        pb = jnp.sum(dgb, axis=0, keepdims=True)

        @pl.when(pl.program_id(0) == 0)
        def _():
            db_ref[:, :D_MODEL] = pa
            db_ref[:, D_MODEL:] = pb

        @pl.when(pl.program_id(0) > 0)
        def _():
            db_ref[:, :D_MODEL] += pa
            db_ref[:, D_MODEL:] += pb

    return pl.pallas_call(
        body, name=name, grid=(t // TOK,),
        in_specs=[_row_spec(D_MODEL), _row_spec(D_MODEL, 2), _row_spec(D_MODEL, 3),
                  _vec_spec(2 * D_MODEL), _row_spec(D_MODEL), _row_spec(D_MODEL)],
        out_specs=[_row_spec(D_MODEL), _row_spec(D_MODEL), _row_spec(2 * D_MODEL),
                   _vec_spec(2 * D_MODEL)],
        out_shape=[_sds((t, D_MODEL), BF16), _sds((t, D_MODEL), BF16),
                   _sds((t, 2 * D_MODEL), BF16), _sds((1, 2 * D_MODEL), F32)],
        compiler_params=_cparams("arbitrary"))(dz, proj, proj, b_gate, ya, yb)


def _head_masks():
    lane = lax.broadcasted_iota(jnp.int32, (1, 2 * HEAD_DIM), 1)
    return lane < HEAD_DIM


BAND_ROWS = 2 * ATT_BLK + CHUNK


def _fill_band(band, prev_ref, cur_ref):
    band[0:ATT_BLK, :] = prev_ref[...]
    band[ATT_BLK:2 * ATT_BLK, :] = cur_ref[...]
    band[2 * ATT_BLK:, :] = jnp.zeros((CHUNK, ATTN_W), BF16)


def _pair_rows(x2, low):
    zero = jnp.zeros_like(x2)
    return jnp.concatenate([jnp.where(low, x2, zero), jnp.where(low, zero, x2)], axis=0)


def _pair_diag(o2, low):
    return jnp.where(low, o2[0:CHUNK, :], o2[CHUNK:, :])


N_PAIRS = HEADS // 2
SM_STRIP = 32
N_STRIPS = BAND_PAD // SM_STRIP
NEG = -1e30


def _fold8(x, op):
    return op(op(x[0:8], x[8:16]), op(x[16:24], x[24:32]))


def _strip(k):
    return pl.ds(pl.multiple_of(k * SM_STRIP, SM_STRIP), SM_STRIP)


def _band_probs(k2, qcat, bias_t, first_key):
    kpos = lax.broadcasted_iota(jnp.int32, (BAND_PAD, 1), 0)
    st = lax.dot_general(k2, qcat, (NT, ((), ())), preferred_element_type=F32)
    st = jnp.where(kpos + first_key >= 0, st + bias_t, NEG)
    e = jnp.exp(st - jnp.max(st, axis=0, keepdims=True))
    return e * (1.0 / jnp.sum(e, axis=0, keepdims=True))


def _band_softmax_stats(st_ref, b_ref, first_key, dp_ref):
    rowi = lax.broadcasted_iota(jnp.int32, (SM_STRIP, 128), 0)

    def scores(k, mx):
        rows = _strip(k)
        live = (rowi + (k * SM_STRIP + first_key)) >= 0
        out = []
        for hp in range(N_PAIRS):
            x = jnp.where(live, st_ref[hp, rows, :] + b_ref[hp, rows, :], NEG)
            st_ref[hp, rows, :] = x
            out.append(jnp.maximum(mx[hp], _fold8(x, jnp.maximum)))
        return tuple(out)

    mx = lax.fori_loop(0, N_STRIPS, scores, (jnp.full((8, 128), NEG, F32),) * N_PAIRS, unroll=2)
    top = [jnp.max(m, axis=0, keepdims=True) for m in mx]

    def sums(k, acc):
        rows = _strip(k)
        ls, eds = [], []
        for hp in range(N_PAIRS):
            e = jnp.exp(st_ref[hp, rows, :] - top[hp])
            ls.append(acc[hp] + _fold8(e, jnp.add))
            eds.append(acc[N_PAIRS + hp] + _fold8(e * dp_ref[hp, rows, :], jnp.add))
        return tuple(ls + eds)

    acc = lax.fori_loop(0, N_STRIPS, sums, (jnp.zeros((8, 128), F32),) * (2 * N_PAIRS), unroll=2)
    inv = [1.0 / jnp.sum(a, axis=0, keepdims=True) for a in acc[:N_PAIRS]]
    delta = [jnp.sum(a, axis=0, keepdims=True) * i for a, i in zip(acc[N_PAIRS:], inv)]
    return top, inv, delta


def _attn_specs(nblk):
    cur = lambda col: pl.BlockSpec((ATT_BLK, ATTN_W), lambda s: (jnp.minimum(s, nblk - 1), col))
    prev = lambda col: pl.BlockSpec(
        (ATT_BLK, ATTN_W), lambda s: (jnp.maximum(jnp.minimum(s, nblk - 1) - 1, 0), col))
    return cur, prev


def _attn_fwd(name, proj, bias, comm=None):
    t = proj.shape[0]
    nblk = t // ATT_BLK
    cur, prev = _attn_specs(nblk)

    def body(q_ref, kp_ref, kc_ref, vp_ref, vc_ref, b_ref, o_ref, kband, vband):
        s = pl.program_id(0)
        _fill_band(kband, kp_ref, kc_ref)
        _fill_band(vband, vp_ref, vc_ref)
        low = _head_masks()

        def chunk(ci, carry):
            r0 = pl.multiple_of(ci * CHUNK, CHUNK)
            for hp in range(N_PAIRS):
                cols = slice(hp * 128, (hp + 1) * 128)
                qcat = _pair_rows(q_ref[pl.ds(r0, CHUNK), cols] * ATTN_SCALE, low)
                p = _band_probs(kband[pl.ds(r0, BAND_PAD), cols], qcat, b_ref[hp],
                                (s * 8 - 8 + ci) * CHUNK)
                o2 = lax.dot_general(p.astype(BF16), vband[pl.ds(r0, BAND_PAD), cols],
                                     (TN, ((), ())), preferred_element_type=F32)
                o_ref[pl.ds(r0, CHUNK), cols] = _pair_diag(o2, low).astype(BF16)
            return carry

        lax.fori_loop(0, 8, chunk, 0)

    out, moved = _pcall(
        body, name, (nblk,),
        [cur(0), prev(1), cur(1), prev(2), cur(2),
         pl.BlockSpec((N_PAIRS, BAND_PAD, 128), lambda s: (0, 0, 0))],
        pl.BlockSpec((ATT_BLK, ATTN_W), lambda s: (s, 0)), _sds((t, ATTN_W), BF16),
        [pltpu.VMEM((BAND_ROWS, ATTN_W), BF16), pltpu.VMEM((BAND_ROWS, ATTN_W), BF16)],
        _cparams("arbitrary"), (proj, proj, proj, proj, proj, bias), comm)
    return out if comm is None else (out, moved)


def _attn_bwd(name, proj, datt, bias, comm=None):
    t = proj.shape[0]
    nblk = t // ATT_BLK
    cur, prev = _attn_specs(nblk)
    late = pl.BlockSpec((ATT_BLK, 3 * ATTN_W), lambda s: (jnp.maximum(s - 1, 0), 0))

    def body(q_ref, kp_ref, kc_ref, vp_ref, vc_ref, do_ref, b_ref,
             dqkv_ref, db_ref, kband, vband, dkacc, dvacc,
             st_ref, dp_ref, pb_ref, dsb_ref, qc_ref, dc_ref, dq_ref, dq_held):
        s = pl.program_id(0)

        @pl.when(s == 0)
        def _():
            dkacc[...] = jnp.zeros_like(dkacc)
            dvacc[...] = jnp.zeros_like(dvacc)
            db_ref[...] = jnp.zeros_like(db_ref)
            dq_ref[...] = jnp.zeros_like(dq_ref)

        @pl.when(s < nblk)
        def _():
            _fill_band(kband, kp_ref, kc_ref)
            _fill_band(vband, vp_ref, vc_ref)
            low = _head_masks()

            def chunk(ci, carry):
                r0 = pl.multiple_of(ci * CHUNK, CHUNK)
                for hp in range(N_PAIRS):
                    cols = slice(hp * 128, (hp + 1) * 128)
                    qc_ref[hp] = _pair_rows(q_ref[pl.ds(r0, CHUNK), cols] * ATTN_SCALE, low)
                    dc_ref[hp] = _pair_rows(do_ref[pl.ds(r0, CHUNK), cols], low)
                    st_ref[hp] = lax.dot_general(kband[pl.ds(r0, BAND_PAD), cols], qc_ref[hp],
                                                 (NT, ((), ())), preferred_element_type=F32)
                    dp_ref[hp] = lax.dot_general(vband[pl.ds(r0, BAND_PAD), cols], dc_ref[hp],
                                                 (NT, ((), ())), preferred_element_type=F32)
                top, inv, delta = _band_softmax_stats(st_ref, b_ref, (s * 8 - 8 + ci) * CHUNK,
                                                      dp_ref)

                def grads(k, c):
                    rows = _strip(k)
                    for hp in range(N_PAIRS):
                        p = jnp.exp(st_ref[hp, rows, :] - top[hp]) * inv[hp]
                        ds = p * (dp_ref[hp, rows, :] - delta[hp])
                        db_ref[hp, rows, :] += ds
                        dsb_ref[hp, rows, :] = ds.astype(BF16)
                        pb_ref[hp, rows, :] = p.astype(BF16)
                    return c

                lax.fori_loop(0, N_STRIPS, grads, 0, unroll=2)
                for hp in range(N_PAIRS):
                    cols = slice(hp * 128, (hp + 1) * 128)
                    dq2 = lax.dot_general(dsb_ref[hp], kband[pl.ds(r0, BAND_PAD), cols],
                                          (TN, ((), ())), preferred_element_type=F32)
                    dq_ref[pl.ds(r0, CHUNK), cols] = (_pair_diag(dq2, low) * ATTN_SCALE).astype(BF16)
                    dkacc[pl.ds(r0, BAND_PAD), cols] += jnp.dot(dsb_ref[hp], qc_ref[hp],
                                                               preferred_element_type=F32)
                    dvacc[pl.ds(r0, BAND_PAD), cols] += jnp.dot(pb_ref[hp], dc_ref[hp],
                                                               preferred_element_type=F32)
                return carry

            dq_held[...] = dq_ref[...]
            lax.fori_loop(0, 8, chunk, 0)

        @pl.when(s == nblk)
        def _():
            dq_held[...] = dq_ref[...]

        dqkv_ref[:, 0:ATTN_W] = dq_held[...]
        dqkv_ref[:, ATTN_W:2 * ATTN_W] = dkacc[0:ATT_BLK, :].astype(BF16)
        dqkv_ref[:, 2 * ATTN_W:] = dvacc[0:ATT_BLK, :].astype(BF16)
        dkacc[0:ATT_BLK, :] = dkacc[ATT_BLK:2 * ATT_BLK, :]
        dvacc[0:ATT_BLK, :] = dvacc[ATT_BLK:2 * ATT_BLK, :]
        dkacc[ATT_BLK:, :] = jnp.zeros((ATT_BLK + CHUNK, ATTN_W), F32)
        dvacc[ATT_BLK:, :] = jnp.zeros((ATT_BLK + CHUNK, ATTN_W), F32)

    outs, moved = _pcall(
        body, name, (nblk + 1,),
        [cur(0), prev(1), cur(1), prev(2), cur(2),
         pl.BlockSpec((ATT_BLK, ATTN_W), lambda s: (jnp.minimum(s, nblk - 1), 0)),
         pl.BlockSpec((HEADS // 2, BAND_PAD, 128), lambda s: (0, 0, 0))],
        [late, pl.BlockSpec((HEADS // 2, BAND_PAD, 128), lambda s: (0, 0, 0))],
        [_sds((t, 3 * ATTN_W), BF16), _sds((HEADS // 2, BAND_PAD, 128), F32)],
        [pltpu.VMEM((BAND_ROWS, ATTN_W), BF16), pltpu.VMEM((BAND_ROWS, ATTN_W), BF16),
         pltpu.VMEM((BAND_ROWS, ATTN_W), F32), pltpu.VMEM((BAND_ROWS, ATTN_W), F32),
         pltpu.VMEM((N_PAIRS, BAND_PAD, 128), F32), pltpu.VMEM((N_PAIRS, BAND_PAD, 128), F32),
         pltpu.VMEM((N_PAIRS, BAND_PAD, 128), BF16), pltpu.VMEM((N_PAIRS, BAND_PAD, 128), BF16),
         pltpu.VMEM((N_PAIRS, 2 * CHUNK, 128), BF16), pltpu.VMEM((N_PAIRS, 2 * CHUNK, 128), BF16),
         pltpu.VMEM((ATT_BLK, ATTN_W), BF16), pltpu.VMEM((ATT_BLK, ATTN_W), BF16)],
        _cparams("arbitrary"), (proj, proj, proj, proj, proj, datt, bias), comm)
    return outs if comm is None else (*outs, moved)


def _diag_onehot(rel_rows):
    d0 = lax.broadcasted_iota(jnp.int32, (BIAS_LANES, BIAS_LANES), 0)
    d1 = lax.broadcasted_iota(jnp.int32, (BIAS_LANES, BIAS_LANES), 1)
    m, n = (d0, d1) if rel_rows else (d1, d0)
    hit = (m == jnp.minimum(BAND - 1 + MAX_REL - n, 2 * MAX_REL)) & (n < BAND + CHUNK - 1)
    return jnp.where(hit, 1.0, 0.0).astype(F32)


def _bias_table(name, rel_bias_l):
    rel_pad = jnp.pad(rel_bias_l, ((0, 0), (0, BIAS_LANES - N_REL)))

    def body(r_ref, o_ref):
        diag = jnp.dot(r_ref[...], _diag_onehot(True), preferred_element_type=F32,
                       precision=lax.Precision.HIGHEST)
        rowid = lax.broadcasted_iota(jnp.int32, (8, BIAS_LANES), 0)
        lane = lax.broadcasted_iota(jnp.int32, (8, BIAS_LANES), 1)
        for h in range(HEADS):
            d8 = jnp.broadcast_to(diag[h:h + 1, :], (8, BIAS_LANES))
            slab0 = pltpu.roll(d8, BIAS_LANES - CHUNK + 1, axis=1)
            for b in range(1, 8):
                slab0 = jnp.where(rowid == b, pltpu.roll(d8, BIAS_LANES - CHUNK + 1 + b, axis=1),
                                  slab0)
            for a in range(8):
                slab = slab0 if a == 0 else pltpu.roll(slab0, 8 * a, axis=1)
                o_ref[h * CHUNK + 8 * a:h * CHUNK + 8 * a + 8, :] = jnp.where(lane < BAND, slab, NEG)

    tab = pl.pallas_call(
        body, name=name,
        in_specs=[pl.BlockSpec(memory_space=pltpu.VMEM)],
        out_specs=pl.BlockSpec(memory_space=pltpu.VMEM),
        out_shape=_sds((HEADS * CHUNK, BIAS_LANES), F32),
    )(rel_pad)
    tab = tab.reshape(HEADS // 2, 2, CHUNK, BIAS_LANES)
    return jnp.transpose(tab, (0, 3, 1, 2)).reshape(HEADS // 2, BIAS_LANES, 2 * CHUNK)


def _bias_fold(name, dbias_t):
    rows = HEADS * CHUNK
    dbias = jnp.transpose(dbias_t.reshape(HEADS // 2, BIAS_LANES, 2, CHUNK), (0, 2, 3, 1))

    def body(d_ref, o_ref):
        rowid = lax.broadcasted_iota(jnp.int32, (8, BIAS_LANES), 0)
        diags = []
        for h in range(HEADS):
            acc = d_ref[h * CHUNK + 56:h * CHUNK + 64, :]
            for a in range(7):
                slab = d_ref[h * CHUNK + 8 * a:h * CHUNK + 8 * a + 8, :]
                acc = acc + pltpu.roll(slab, 56 - 8 * a, axis=1)
            tot = jnp.where(rowid == 7, acc, 0.0)
            for b in range(7):
                tot = tot + jnp.where(rowid == b, pltpu.roll(acc, 7 - b, axis=1), 0.0)
            diags.append(jnp.sum(tot, axis=0, keepdims=True))
        diag = jnp.concatenate(diags, axis=0)
        o_ref[...] = jnp.dot(diag, _diag_onehot(False), preferred_element_type=F32,
                             precision=lax.Precision.HIGHEST)

    return pl.pallas_call(
        body, name=name,
        in_specs=[pl.BlockSpec(memory_space=pltpu.VMEM)],
        out_specs=pl.BlockSpec(memory_space=pltpu.VMEM),
        out_shape=_sds((HEADS, BIAS_LANES), F32),
    )(dbias.reshape(rows, BIAS_LANES))


def _inv_counts(i):
    trow = lax.broadcasted_iota(jnp.int32, (TOK + HALO, 1), 0) + i * TOK
    return [1.0 / jnp.minimum(trow + 1, w).astype(F32) for w in POOL_WINDOWS]


def _pool_fwd(name, proj, wg, scale, comm=None):
    t = proj.shape[0]
    hb = TOK // HALO

    def body(u_ref, up_ref, wg_ref, sc_ref, pooled_ref, mixed_ref, b0, b1, b2, b3):
        i = pl.program_id(0)
        halo = up_ref[...].astype(F32)
        b0[0:HALO, :] = jnp.where(i == 0, jnp.zeros_like(halo), halo)
        b0[HALO:, :] = u_ref[...].astype(F32)
        n = TOK + HALO
        b1[8:n, :] = b0[8:n, :] + b0[7:n - 1, :]
        b2[16:n, 128:] = b1[16:n, 128:] + b1[14:n - 2, 128:]
        b3[24:n, 256:] = b2[24:n, 256:] + b2[20:n - 4, 256:]
        wins = [b1[HALO:n, 0:128], b2[HALO:n, 128:256], b3[HALO:n, 256:384],
                b3[HALO:n, 384:512] + b3[HALO - 8:n - 8, 384:512]]
        inv = _inv_counts(i)
        for g in range(4):
            cols = slice(g * POOL_GD, (g + 1) * POOL_GD)
            pooled = (wins[g] * inv[g][0:TOK] - b0[HALO:n, cols]).astype(BF16)
            pooled_ref[:, cols] = pooled
            pre = jnp.dot(pooled, wg_ref[g], preferred_element_type=F32)
            mixed_ref[:, cols] = (pre * sc_ref[:, cols]).astype(BF16)

    buf = pltpu.VMEM((TOK + HALO, POOL_W), F32)
    outs, moved = _pcall(
        body, name, (t // TOK,),
        [_row_spec(POOL_W, 3),
         pl.BlockSpec((HALO, POOL_W), lambda i: (jnp.maximum(i * hb - 1, 0), 3)),
         pl.BlockSpec((4, POOL_GD, POOL_GD), lambda i: (0, 0, 0)), _vec_spec(POOL_W)],
        [_row_spec(POOL_W), _row_spec(POOL_W)],
        [_sds((t, POOL_W), BF16), _sds((t, POOL_W), BF16)], [buf, buf, buf, buf],
        _cparams("arbitrary"), (proj, proj, wg, scale), comm)
    return outs if comm is None else (*outs, moved)


def _pool_bwd(name, dmixed, pooled, wg, scale, comm=None):
    t = dmixed.shape[0]
    nt = t // TOK
    hb = TOK // HALO

    def body(dm_ref, dmn_ref, p_ref, wg_ref, sc_ref, du_ref, dwg_ref, dsc_ref, c0, c1, c2, c3):
        i = pl.program_id(0)

        @pl.when(i == 0)
        def _():
            dwg_ref[...] = jnp.zeros_like(dwg_ref)
            dsc_ref[...] = jnp.zeros_like(dsc_ref)

        n = TOK + HALO
        inv = _inv_counts(i)
        dmv = dm_ref[...].astype(F32)
        dmn = dmn_ref[...].astype(F32)
        dmn = jnp.where(i == nt - 1, jnp.zeros_like(dmn), dmn)
        for g in range(4):
            cols = slice(g * POOL_GD, (g + 1) * POOL_GD)
            scg = sc_ref[:, cols]
            pg = p_ref[:, cols]
            dpre = (dmv[:, cols] * scg).astype(BF16)
            dpre_n = (dmn[:, cols] * scg).astype(BF16)
            pre = jnp.dot(pg, wg_ref[g], preferred_element_type=F32)
            dsc_ref[:, cols] += jnp.sum(dmv[:, cols] * pre, axis=0, keepdims=True)
            dwg_ref[g] += lax.dot_general(pg, dpre, (TN, ((), ())), preferred_element_type=F32)
            dpool = lax.dot_general(dpre, wg_ref[g], (NT, ((), ())), preferred_element_type=F32)
            dpool_n = lax.dot_general(dpre_n, wg_ref[g], (NT, ((), ())),
                                      preferred_element_type=F32)
            c0[0:TOK, cols] = dpool
            c0[TOK:n, cols] = dpool_n
            c1[0:TOK, cols] = dpool * inv[g][0:TOK]
            c1[TOK:n, cols] = dpool_n * inv[g][TOK:n]
        c2[0:n - 8, :] = c1[0:n - 8, :] + c1[1:n - 7, :]
        c3[0:n - 16, 128:] = c2[0:n - 16, 128:] + c2[2:n - 14, 128:]
        c1[0:n - 24, 256:] = c3[0:n - 24, 256:] + c3[4:n - 20, 256:]
        wins = [c2[0:TOK, 0:128], c3[0:TOK, 128:256], c1[0:TOK, 256:384],
                c1[0:TOK, 384:512] + c1[8:TOK + 8, 384:512]]
        for g in range(4):
            cols = slice(g * POOL_GD, (g + 1) * POOL_GD)
            du_ref[:, cols] = (wins[g] - c0[0:TOK, cols]).astype(BF16)

    buf = pltpu.VMEM((TOK + HALO, POOL_W), F32)
    outs, moved = _pcall(
        body, name, (nt,),
        [_row_spec(POOL_W),
         pl.BlockSpec((HALO, POOL_W), lambda i: (jnp.minimum((i + 1) * hb, nt * hb - 1), 0)),
         _row_spec(POOL_W), pl.BlockSpec((4, POOL_GD, POOL_GD), lambda i: (0, 0, 0)),
         _vec_spec(POOL_W)],
        [_row_spec(POOL_W), pl.BlockSpec((4, POOL_GD, POOL_GD), lambda i: (0, 0, 0)),
         _vec_spec(POOL_W)],
        [_sds((t, POOL_W), BF16), _sds((4, POOL_GD, POOL_GD), F32), _sds((1, POOL_W), F32)],
        [buf, buf, buf, buf], _cparams("arbitrary"), (dmixed, dmixed, pooled, wg, scale), comm)
    return outs if comm is None else (*outs, moved)


GELU_C = math.sqrt(2.0 / math.pi)


GELU_K = 0.044715


def _gelu_parts(x):
    x2 = x * x
    s = 0.5 + 0.5 * jnp.tanh(x * (GELU_C + (GELU_C * GELU_K) * x2))
    return x * s, s, x2


def _gelu(x):
    return _gelu_parts(x)[0]


def _gelu_and_grad(x):
    g, s, x2 = _gelu_parts(x)
    return g, s + g * (1.0 - s) * ((2 * GELU_C) + (6 * GELU_C * GELU_K) * x2)


def _taps(buf, r, rows):
    a = buf[pl.ds(r, rows + 8), :]
    return a[8:], pltpu.roll(a, 1, axis=0)[8:], pltpu.roll(a, 2, axis=0)[8:]


def _conv(taps, w_ref, b_ref):
    return b_ref[...] + w_ref[2:3, :] * taps[0] + w_ref[1:2, :] * taps[1] + w_ref[0:1, :] * taps[2]


def _stage(dst, prev_ref, cur_ref, next_ref, first, last):
    rows = cur_ref.shape[0]
    h = prev_ref[...].astype(F32)
    dst[0:8, :] = jnp.where(first, jnp.zeros_like(h), h)
    dst[8:8 + rows, :] = cur_ref[...].astype(F32)
    if next_ref is not None:
        h = next_ref[...].astype(F32)
        dst[8 + rows:, :] = jnp.where(last, jnp.zeros_like(h), h)


FWD_STRIP = 32
BWD_STRIP = 16


def _ffn_gate_fwd(name, hu, conv_w, conv_b, comm=None):
    t = hu.shape[0]
    ncol = D_FF // FF_COL
    hb = FF_TOK // 8

    def tile(off):
        return pl.BlockSpec((FF_TOK, FF_COL), lambda i, j: (i, j + off))

    def halo(off):
        return pl.BlockSpec((8, FF_COL), lambda i, j: (jnp.maximum(i * hb - 1, 0), j + off))

    def wspec(off):
        return pl.BlockSpec((3, FF_COL), lambda i, j: (0, j + off))

    def bspec(off):
        return pl.BlockSpec((1, FF_COL), lambda i, j: (0, j + off))

    def body(v_ref, vp_ref, g_ref, gp_ref, wv_ref, wg_ref, bv_ref, bg_ref, a_ref, vb, gb):
        first = pl.program_id(0) == 0
        _stage(vb, vp_ref, v_ref, None, first, None)
        _stage(gb, gp_ref, g_ref, None, first, None)

        def strip(k, carry):
            r = pl.multiple_of(k * FWD_STRIP, FWD_STRIP)
            val = _conv(_taps(vb, r, FWD_STRIP), wv_ref, bv_ref)
            gate = _conv(_taps(gb, r, FWD_STRIP), wg_ref, bg_ref)
            a_ref[pl.ds(r, FWD_STRIP), :] = (_gelu(gate) * val).astype(BF16)
            return carry

        lax.fori_loop(0, FF_TOK // FWD_STRIP, strip, 0)

    buf = pltpu.VMEM((FF_TOK + 8, FF_COL), F32)
    out, moved = _pcall(
        body, name, (t // FF_TOK, ncol),
        [tile(0), halo(0), tile(ncol), halo(ncol), wspec(0), wspec(ncol), bspec(0), bspec(ncol)],
        pl.BlockSpec((FF_TOK, FF_COL), lambda i, j: (i, j)), _sds((t, D_FF), BF16), [buf, buf],
        _cparams("arbitrary", "arbitrary"),
        (hu, hu, hu, hu, conv_w, conv_w, conv_b, conv_b), comm)
    return out if comm is None else (out, moved)


def _ffn_gate_bwd(name, da, hu, conv_w, conv_b, comm=None):
    t = hu.shape[0]
    nt = t // FF_TOK
    ncol = D_FF // FF_COL
    hb = FF_TOK // 8
    ext = FF_TOK + 8

    def tile(off):
        return pl.BlockSpec((FF_TOK, FF_COL), lambda j, i: (i, j + off))

    def prev(off):
        return pl.BlockSpec((8, FF_COL), lambda j, i: (jnp.maximum(i * hb - 1, 0), j + off))

    def nxt(off):
        return pl.BlockSpec((8, FF_COL), lambda j, i: (jnp.minimum((i + 1) * hb, nt * hb - 1), j + off))

    def wspec(off):
        return pl.BlockSpec((3, FF_COL), lambda j, i: (0, j + off))

    def bspec(off):
        return pl.BlockSpec((1, FF_COL), lambda j, i: (0, j + off))

    def body(da_ref, dan_ref, v_ref, vp_ref, vn_ref, g_ref, gp_ref, gn_ref,
             wv_ref, wg_ref, bv_ref, bg_ref, dh_ref, dwv_ref, dwg_ref, vb, gb, dab):
        i = pl.program_id(1)
        first, last = i == 0, i == nt - 1

        @pl.when(first)
        def _():
            dwv_ref[...] = jnp.zeros_like(dwv_ref)
            dwg_ref[...] = jnp.zeros_like(dwg_ref)

        _stage(vb, vp_ref, v_ref, vn_ref, first, last)
        _stage(gb, gp_ref, g_ref, gn_ref, first, last)
        dab[0:FF_TOK, :] = da_ref[...].astype(F32)
        h = dan_ref[...].astype(F32)
        dab[FF_TOK:, :] = jnp.where(last, jnp.zeros_like(h), h)

        def grads(r, rows):
            tv, tg = _taps(vb, r, rows), _taps(gb, r, rows)
            gate = _conv(tg, wg_ref, bg_ref)
            dav = dab[pl.ds(r, rows), :]
            g, dg = _gelu_and_grad(gate)
            dval = dav * g
            dgate = dav * _conv(tv, wv_ref, bv_ref) * dg
            return dval, dgate, tv, tg

        def fold(x):
            return x[0:8] + x[8:16]

        def strip(k, carry):
            r = pl.multiple_of(FF_TOK - BWD_STRIP - k * BWD_STRIP, BWD_STRIP)
            dval, dgate, tv, tg = grads(r, BWD_STRIP)
            new = (dval[0:8], dgate[0:8])
            for half, (d, nxt_rows, taps, w_ref, dw_ref) in enumerate((
                    (dval, carry[0], tv, wv_ref, dwv_ref), (dgate, carry[1], tg, wg_ref, dwg_ref))):
                e = jnp.concatenate([d, nxt_rows], axis=0)
                dh = (w_ref[2:3, :] * d
                      + w_ref[1:2, :] * pltpu.roll(e, BWD_STRIP + 7, axis=0)[0:BWD_STRIP]
                      + w_ref[0:1, :] * pltpu.roll(e, BWD_STRIP + 6, axis=0)[0:BWD_STRIP])
                dh_ref[half, pl.ds(r, BWD_STRIP), :] = dh.astype(BF16)
                dw_ref[0:8, :] += fold(d * taps[2])
                dw_ref[8:16, :] += fold(d * taps[1])
                dw_ref[16:24, :] += fold(d * taps[0])
                dw_ref[24:32, :] += fold(d)
            return new

        dval, dgate, _, _ = grads(FF_TOK, 8)
        lax.fori_loop(0, FF_TOK // BWD_STRIP, strip, (dval, dgate))

        @pl.when(last)
        def _():
            for dw_ref in (dwv_ref, dwg_ref):
                for q in range(4):
                    dw_ref[8 * q:8 * q + 1, :] = jnp.sum(dw_ref[8 * q:8 * q + 8, :], axis=0,
                                                         keepdims=True)

    hbuf = pltpu.VMEM((FF_TOK + 16, FF_COL), F32)
    acc = pl.BlockSpec((32, FF_COL), lambda j, i: (0, j))
    (dhu, dwv, dwg), moved = _pcall(
        body, name, (ncol, nt),
        [tile(0), nxt(0), tile(0), prev(0), nxt(0), tile(ncol), prev(ncol), nxt(ncol),
         wspec(0), wspec(ncol), bspec(0), bspec(ncol)],
        [pl.BlockSpec((2, FF_TOK, FF_COL), lambda j, i: (0, i, j)), acc, acc],
        [_sds((2, t, D_FF), BF16), _sds((32, D_FF), F32), _sds((32, D_FF), F32)],
        [hbuf, hbuf, pltpu.VMEM((ext, FF_COL), F32)], _cparams("arbitrary", "arbitrary"),
        (da, da, hu, hu, hu, hu, hu, hu, conv_w, conv_w, conv_b, conv_b), comm)
    dconv = jnp.concatenate([dwv, dwg], axis=1).reshape(4, 8, 2 * D_FF)[:, 0]
    return (dhu, dconv) if comm is None else (dhu, dconv, moved)


def _mesh_pos():
    x, y, c = lax.axis_index("x"), lax.axis_index("y"), lax.axis_index("c")
    return x, y, c, [(1 - x, y), (x, 1 - y), (1 - x, 1 - y)]


def _any_specs(n):
    return [pl.BlockSpec(memory_space=pl.ANY)] * n


def _remote(src, dst, send_sems, recv_sems, i, dev):
    return pltpu.make_async_remote_copy(src_ref=src, dst_ref=dst, send_sem=send_sems.at[i],
                                        recv_sem=recv_sems.at[i], device_id=dev,
                                        device_id_type=MESH)


def _mine(c, rows):
    return pl.ds(pl.multiple_of(c * (rows // 2), 16), rows // 2)


def _gather_send(shards, conv_shard, gathered, l):
    nbig = len(shards)
    with_conv = conv_shard is not None
    if gathered is None:
        ins = list(shards) + ([conv_shard] if with_conv else [])
        outs = [_sds((DEPTH, N_CHIPS) + s.shape[1:], s.dtype) for s in ins]
        alias = {}
    else:
        ins = list(shards) + list(gathered)
        outs = [_sds(g.shape, g.dtype) for g in gathered]
        alias = {nbig + k: k for k in range(nbig)}

    def copies(cin, cout, ssem, rsem):
        x, y, c, chips = _mesh_pos()
        me = 2 * x + y
        out = []
        for k in range(nbig):
            rows = shards[k].shape[1]
            for j, (cx, cy) in enumerate(chips):
                out.append(_remote(cin[k].at[l, _mine(c, rows)], cout[k].at[l, me, _mine(c, rows)],
                                   ssem, rsem, 4 * k + j, (cx, cy, c)))
            out.append(_remote(cin[k].at[l], cout[k].at[l, me], ssem, rsem, 4 * k + 3,
                               (x, y, 1 - c)))
        if with_conv:
            base = 4 * nbig
            for j, (cx, cy) in enumerate(chips):
                out.append(_remote(cin[nbig].at[c], cout[nbig].at[c, me], ssem, rsem, base + j,
                                   (cx, cy, c)))
            for ll in range(DEPTH):
                out.append(_remote(cin[nbig].at[ll], cout[nbig].at[ll, me], ssem, rsem,
                                   base + 3 + ll, (x, y, 1 - c)))
        return out

    return _Comm(ins, outs, copies, 4 * nbig + 5, alias)


def _gather_forward(gathered, nbig, rows, l):
    with_conv = len(gathered) > nbig
    alias = {k: k for k in range(len(gathered))}

    def copies(cin, cout, ssem, rsem):
        x, y, c, chips = _mesh_pos()
        out = []
        for k in range(nbig):
            for j, (cx, cy) in enumerate(chips):
                blk = cout[k].at[l, 2 * cx + cy, _mine(c, rows[k])]
                out.append(_remote(blk, blk, ssem, rsem, 3 * k + j, (x, y, 1 - c)))
        if with_conv:
            for j, (cx, cy) in enumerate(chips):
                blk = cout[nbig].at[c, 2 * cx + cy]
                out.append(_remote(blk, blk, ssem, rsem, 3 * nbig + j, (x, y, 1 - c)))
        return out

    return _Comm(gathered, [_sds(g.shape, g.dtype) for g in gathered], copies, 3 * nbig + 3, alias)


def _reduce_swap(grads, l):
    def copies(cin, cout, ssem, rsem):
        x, y, c, _ = _mesh_pos()
        return [_remote(cin[k].at[l, :, _mine(1 - c, g.shape[2])], cout[k], ssem, rsem, k,
                        (x, y, 1 - c)) for k, g in enumerate(grads)]

    outs = [_sds((N_CHIPS, g.shape[2] // 2, g.shape[3]), g.dtype) for g in grads]
    return _Comm(grads, outs, copies, len(grads))


def _reduce_scatter(sums):
    def copies(cin, cout, ssem, rsem):
        x, y, c, chips = _mesh_pos()
        return [_remote(cin[k].at[2 * cx + cy], cout[k].at[j], ssem, rsem, 3 * k + j, (cx, cy, c))
                for k in range(len(sums)) for j, (cx, cy) in enumerate(chips)]

    outs = [_sds((3,) + s.shape[1:], s.dtype) for s in sums]
    return _Comm(sums, outs, copies, 3 * len(sums))


def _reduce_share(reds, l):
    def copies(cin, cout, ssem, rsem):
        x, y, c, _ = _mesh_pos()
        out = []
        for k, r in enumerate(reds):
            half = cout[k].at[l, _mine(c, r.shape[1])]
            out.append(_remote(half, half, ssem, rsem, k, (x, y, 1 - c)))
        return out

    return _Comm(reds, [_sds(r.shape, r.dtype) for r in reds], copies, len(reds),
                 {k: k for k in range(len(reds))})


def _allgather_weights(shards):
    n = len(shards)

    def body(*refs):
        ins, outs = refs[:n], refs[n:2 * n]
        send_sems, recv_sems = refs[2 * n:]
        x, y, c, chips = _mesh_pos()
        me = 2 * x + y
        started = []
        own = []
        for k in range(n):
            for l in range(2):
                cp = pltpu.make_async_remote_copy(
                    src_ref=ins[k].at[l], dst_ref=outs[k].at[l, me],
                    send_sem=send_sems.at[k, 6 + l], recv_sem=recv_sems.at[k, 6 + l],
                    device_id=(x, y, 1 - c), device_id_type=MESH)
                cp.start()
                own.append(cp)
            for j, (cx, cy) in enumerate(chips):
                cp = pltpu.make_async_remote_copy(
                    src_ref=ins[k].at[c], dst_ref=outs[k].at[c, me],
                    send_sem=send_sems.at[k, j], recv_sem=recv_sems.at[k, j],
                    device_id=(cx, cy, c), device_id_type=MESH)
                cp.start()
                started.append(cp)
        for k in range(n):
            for j, (cx, cy) in enumerate(chips):
                landed = outs[k].at[c, 2 * cx + cy]
                pltpu.make_async_remote_copy(
                    src_ref=ins[k].at[c], dst_ref=landed,
                    send_sem=send_sems.at[k, j], recv_sem=recv_sems.at[k, j],
                    device_id=(cx, cy, c), device_id_type=MESH).wait_recv()
                fw = pltpu.make_async_remote_copy(
                    src_ref=landed, dst_ref=landed,
                    send_sem=send_sems.at[k, 3 + j], recv_sem=recv_sems.at[k, 3 + j],
                    device_id=(x, y, 1 - c), device_id_type=MESH)
                fw.start()
                started.append(fw)
        for k in range(n):
            for j, (cx, cy) in enumerate(chips):
                theirs = outs[k].at[1 - c, 2 * cx + cy]
                pltpu.make_async_remote_copy(
                    src_ref=theirs, dst_ref=theirs,
                    send_sem=send_sems.at[k, 3 + j], recv_sem=recv_sems.at[k, 3 + j],
                    device_id=(x, y, 1 - c), device_id_type=MESH).wait_recv()
        for cp in started:
            cp.wait_send()
        for cp in own:
            cp.wait()

    return pl.pallas_call(
        body, name="allgather_weights",
        in_specs=_any_specs(n), out_specs=_any_specs(n),
        out_shape=[_sds((2, N_CHIPS) + s.shape[1:], s.dtype) for s in shards],
        scratch_shapes=[pltpu.SemaphoreType.DMA((n, 8)), pltpu.SemaphoreType.DMA((n, 8))],
    )(*shards)


def _swap_layers(grads):
    n = len(grads)

    def body(*refs):
        ins, outs = refs[:n], refs[n:2 * n]
        send_sems, recv_sems = refs[2 * n:]
        x, y, c, _ = _mesh_pos()
        cps = []
        for k in range(n):
            cp = pltpu.make_async_remote_copy(
                src_ref=ins[k].at[1 - c], dst_ref=outs[k],
                send_sem=send_sems.at[k], recv_sem=recv_sems.at[k],
                device_id=(x, y, 1 - c), device_id_type=MESH)
            cp.start()
            cps.append(cp)
        for cp in cps:
            cp.wait()

    return pl.pallas_call(
        body, name="swap_layers",
        in_specs=_any_specs(n), out_specs=_any_specs(n),
        out_shape=[_sds(g.shape[1:], g.dtype) for g in grads],
        scratch_shapes=[pltpu.SemaphoreType.DMA((n,)), pltpu.SemaphoreType.DMA((n,))],
    )(*grads)


def _scatter_blocks(sums):
    n = len(sums)

    def body(*refs):
        ins, outs = refs[:n], refs[n:2 * n]
        send_sems, recv_sems = refs[2 * n:]
        x, y, c, chips = _mesh_pos()
        cps = []
        for k in range(n):
            for j, (cx, cy) in enumerate(chips):
                cp = pltpu.make_async_remote_copy(
                    src_ref=ins[k].at[2 * cx + cy], dst_ref=outs[k].at[j],
                    send_sem=send_sems.at[k, j], recv_sem=recv_sems.at[k, j],
                    device_id=(cx, cy, c), device_id_type=MESH)
                cp.start()
                cps.append(cp)
        for cp in cps:
            cp.wait()

    return pl.pallas_call(
        body, name="scatter_blocks",
        in_specs=_any_specs(n), out_specs=_any_specs(n),
        out_shape=[_sds((3,) + s.shape[1:], s.dtype) for s in sums],
        scratch_shapes=[pltpu.SemaphoreType.DMA((n, 3)), pltpu.SemaphoreType.DMA((n, 3))],
    )(*sums)


def _exchange_reduced(reds):
    n = len(reds)

    def body(*refs):
        outs = refs[n:2 * n]
        send_sems, recv_sems = refs[2 * n:]
        x, y, c, _ = _mesh_pos()
        cps = []
        for k in range(n):
            cp = pltpu.make_async_remote_copy(
                src_ref=outs[k].at[c], dst_ref=outs[k].at[c],
                send_sem=send_sems.at[k], recv_sem=recv_sems.at[k],
                device_id=(x, y, 1 - c), device_id_type=MESH)
            cp.start()
            cps.append(cp)
        for k in range(n):
            pltpu.make_async_remote_copy(
                src_ref=outs[k].at[c], dst_ref=outs[k].at[1 - c],
                send_sem=send_sems.at[k], recv_sem=recv_sems.at[k],
                device_id=(x, y, 1 - c), device_id_type=MESH).wait_recv()
        for cp in cps:
            cp.wait_send()

    return pl.pallas_call(
        body, name="exchange_reduced",
        in_specs=_any_specs(n), out_specs=_any_specs(n),
        out_shape=[_sds(r.shape, r.dtype) for r in reds],
        input_output_aliases={k: k for k in range(n)},
        scratch_shapes=[pltpu.SemaphoreType.DMA((n,)), pltpu.SemaphoreType.DMA((n,))],
    )(*reds)


def _allreduce_small(pack):
    n = pack.shape[0]

    def body(x_ref, o_ref, gbuf, send_sems, recv_sems):
        x, y, c, chips = _mesh_pos()
        sibling = (x, y, 1 - c)

        def slot(px, py, pc):
            return gbuf.at[4 * px + 2 * py + pc]

        def copy(k, block, to, src=None):
            return pltpu.make_async_remote_copy(
                src_ref=slot(*block) if src is None else src, dst_ref=slot(*block),
                send_sem=send_sems.at[k], recv_sem=recv_sems.at[k],
                device_id=to, device_id_type=MESH)

        me = (x, y, c)
        first = [copy(0, me, sibling, src=x_ref)]
        first += [copy(1 + j, me, (*chip, c), src=x_ref) for j, chip in enumerate(chips)]
        for cp in first:
            cp.start()
        gbuf[4 * x + 2 * y + c] = x_ref[...]
        passed = [copy(4 + j, (*chip, c), sibling) for j, chip in enumerate(chips)]
        for j, chip in enumerate(chips):
            copy(1 + j, (*chip, c), me).wait_recv()
            passed[j].start()
        copy(0, sibling, me).wait_recv()
        for j, chip in enumerate(chips):
            copy(4 + j, (*chip, 1 - c), me).wait_recv()
        for cp in first + passed:
            cp.wait_send()
        acc = gbuf[0]
        for d in range(1, 8):
            acc = acc + gbuf[d]
        o_ref[...] = acc

    return pl.pallas_call(
        body, name="allreduce_small",
        in_specs=[pl.BlockSpec(memory_space=pltpu.VMEM)],
        out_specs=pl.BlockSpec(memory_space=pltpu.VMEM),
        out_shape=_sds((n, 128), F32),
        scratch_shapes=[pltpu.VMEM((8, n, 128), F32), pltpu.SemaphoreType.DMA((7,)),
                        pltpu.SemaphoreType.DMA((7,))],
        compiler_params=pltpu.CompilerParams(vmem_limit_bytes=VMEM_LIMIT_V7X),
    )(pack)


def _core_index():
    return jnp.reshape(lax.axis_index("c"), (1,)).astype(jnp.int32)


def _chip_index():
    return jnp.reshape(2 * lax.axis_index("x") + lax.axis_index("y"), (1,)).astype(jnp.int32)


def _chip_sum(name, stacked, sib, l):
    _, nb, r, cdim = stacked.shape
    hr = r // 2

    def body(c_ref, a_ref, b_ref, o_ref):
        o_ref[...] = (a_ref[...].astype(F32) + b_ref[...].astype(F32)).astype(BF16)

    return pl.pallas_call(
        body, name=name,
        grid_spec=pltpu.PrefetchScalarGridSpec(
            num_scalar_prefetch=1, grid=(nb,),
            in_specs=[pl.BlockSpec((None, None, hr, cdim), lambda j, cr: (l, j, cr[0], 0)),
                      pl.BlockSpec((None, hr, cdim), lambda j, cr: (j, 0, 0))],
            out_specs=pl.BlockSpec((None, hr, cdim), lambda j, cr: (j, 0, 0))),
        out_shape=_sds((nb, hr, cdim), BF16),
        compiler_params=_cparams("parallel"))(_core_index(), stacked, sib)


def _final_sum(name, sums, recv, l, fill):
    _, hr, cdim = sums.shape
    tr = hr // 2

    def body(m_ref, a_ref, b_ref, *rest):
        acc = a_ref[...].astype(F32)
        for j in range(3):
            acc = acc + b_ref[j].astype(F32)
        rest[-1][...] = acc

    in_specs = [pl.BlockSpec((None, tr, cdim), lambda i, mr: (mr[0], i, 0)),
                pl.BlockSpec((3, tr, cdim), lambda i, mr: (0, i, 0))]
    args = [jnp.concatenate([_chip_index(), _core_index()]), sums, recv]
    aliases = {}
    if fill is not None:
        in_specs.append(pl.BlockSpec(memory_space=pl.ANY))
        args.append(fill)
        aliases = {3: 0}
    return pl.pallas_call(
        body, name=name,
        grid_spec=pltpu.PrefetchScalarGridSpec(
            num_scalar_prefetch=1, grid=(2,), in_specs=in_specs,
            out_specs=pl.BlockSpec((None, tr, cdim), lambda i, mr: (l, 2 * mr[1] + i, 0))),
        out_shape=_sds((DEPTH, 2 * hr, cdim), F32), input_output_aliases=aliases,
        compiler_params=_cparams("parallel"))(*args)


def _adamw(name, w, g, m, v):
    nl, r, cdim = w.shape
    tr = r // 4 if r % 32 == 0 else r
    c1 = 1.0 - ADAM_B1 ** ADAM_STEP
    c2 = 1.0 - ADAM_B2 ** ADAM_STEP

    def body(w_ref, g_ref, m_ref, v_ref, d_ref, nm_ref, nv_ref):
        gv = g_ref[...]
        nm = ADAM_B1 * m_ref[...] + (1.0 - ADAM_B1) * gv
        nv = ADAM_B2 * v_ref[...] + (1.0 - ADAM_B2) * (gv * gv)
        nm_ref[...] = nm
        nv_ref[...] = nv
        d_ref[...] = -ADAM_LR * ((nm / c1) / (jnp.sqrt(nv / c2) + ADAM_EPS) + ADAM_WD * w_ref[...])

    spec = pl.BlockSpec((None, tr, cdim), lambda l, i: (l, i, 0))
    out = _sds(w.shape, F32)
    return pl.pallas_call(
        body, name=name, grid=(nl, r // tr),
        in_specs=[spec] * 4, out_specs=[spec] * 3, out_shape=[out] * 3,
        compiler_params=_cparams("parallel", "parallel"))(w, g, m, v)


def _rows128(a):
    return a.reshape(-1, 128)


def kernel(x, norm_mix_pre, w_in, b_gate, rel_bias, w_attn_out, w_pool_group, pool_scale, w_pool_out, w_o, norm_mix_post, norm_ffn_pre, w_up, conv_w, conv_b, w_down, norm_ffn_post, loss_target, m_norm_mix_pre, m_w_in, m_b_gate, m_rel_bias, m_w_attn_out, m_w_pool_group, m_pool_scale, m_w_pool_out, m_w_o, m_norm_mix_post, m_norm_ffn_pre, m_w_up, m_conv_w, m_conv_b, m_w_down, m_norm_ffn_post, v_norm_mix_pre, v_w_in, v_b_gate, v_rel_bias, v_w_attn_out, v_w_pool_group, v_pool_scale, v_w_pool_out, v_w_o, v_norm_mix_post, v_norm_ffn_pre, v_w_up, v_conv_w, v_conv_b, v_w_down, v_norm_ffn_post):
    t = x.shape[1]
    xs = x.reshape(t, D_MODEL)
    target = loss_target.reshape(t, D_MODEL)

    names = ["w_in", "w_attn_out", "w_pool_out", "w_o", "w_up", "w_down"]
    shards = [w.astype(BF16) for w in (w_in, w_attn_out, w_pool_out, w_o, w_up, w_down)]
    rows = [s.shape[1] for s in shards]
    nbig = len(shards)
    g = _comm_call("gather0_send", _gather_send(shards[:1], conv_w, None, 0))
    g = _comm_call("gather0_forward", _gather_forward(g, 1, rows[:1], 0))
    cw_full = jnp.transpose(g[1], (0, 2, 1, 3)).reshape(DEPTH, 3, 2 * D_FF)
    g = g[:1]
    wg_bf = w_pool_group.astype(BF16)

    def views(gathered):
        win_g, wao_g, wpo_g, wo_g, wup_g, wdn_g = gathered
        return (win_g, wao_g, wpo_g, wo_g.reshape(DEPTH, D_MODEL, D_MODEL), wup_g,
                wdn_g.reshape(DEPTH, D_FF, D_MODEL))

    saved = []
    xcur = xs
    for l in range(DEPTH):
        tag = f"l{l}_"
        bias = _bias_table(tag + "bias_table", rel_bias[l])
        h = _norm_fwd(tag + "norm_mix_pre", xcur, norm_mix_pre[l:l + 1])
        proj = _mm_nn_blocked(tag + "proj", h, g[0], l, BF16)
        if l == 0:
            att, rest = _attn_fwd(tag + "attn_fwd", proj, bias,
                                  _gather_send(shards[1:], None, None, 0))
            pooled, mixed, rest = _pool_fwd(tag + "pool_fwd", proj, wg_bf[l], pool_scale[l:l + 1],
                                            _gather_forward(rest, nbig - 1, rows[1:], 0))
            g = g + rest
        else:
            att = _attn_fwd(tag + "attn_fwd", proj, bias)
            pooled, mixed = _pool_fwd(tag + "pool_fwd", proj, wg_bf[l], pool_scale[l:l + 1])
        win_g, wao_g, wpo_g, wo_full, wup_g, wdn_full = views(g)
        ya = _narrow_nn(tag + "attn_out", att, wao_g, l)
        yb = _narrow_nn(tag + "pool_out", mixed, wpo_g, l)
        z = _gate_fwd(tag + "gate_fwd", proj, b_gate[l:l + 1], ya, yb)
        mix = _mm_nn(tag + "mix", z, wo_full, l, D_MODEL, F32)
        x1 = _norm_residual_fwd(tag + "norm_mix_post", xcur, mix, norm_mix_post[l:l + 1])
        h2 = _norm_fwd(tag + "norm_ffn_pre", x1, norm_ffn_pre[l:l + 1])
        hu = _mm_nn_blocked(tag + "ffn_up", h2, wup_g, l, BF16)
        if l == 0:
            a, g = _ffn_gate_fwd(tag + "ffn_gate_fwd", hu, cw_full[l], conv_b[l:l + 1],
                                 _gather_send(shards, None, g, 1))
            wdn_full = views(g)[5]
        else:
            a = _ffn_gate_fwd(tag + "ffn_gate_fwd", hu, cw_full[l], conv_b[l:l + 1])
        f = _mm_nn(tag + "ffn_down", a, wdn_full, l, D_FF // 2, F32)
        if l == 0:
            x2, g = _norm_residual_fwd(tag + "norm_ffn_post", x1, f, norm_ffn_post[l:l + 1],
                                       _gather_forward(g, nbig, rows, 1))
        else:
            x2 = _norm_residual_fwd(tag + "norm_ffn_post", x1, f, norm_ffn_post[l:l + 1])
        saved.append(dict(x=xcur, h=h, proj=proj, att=att, pooled=pooled, mixed=mixed, ya=ya,
                          yb=yb, z=z, mix=mix, x1=x1, h2=h2, hu=hu, a=a, f=f, bias=bias))
        xcur = x2
    win_g, wao_g, wpo_g, wo_full, wup_g, wdn_full = views(g)

    dy, loss_local = _loss_head(xcur, target)
    loss = lax.psum(loss_local, ("x", "y", "c"))

    dx = dy
    dws = dict.fromkeys(names)
    reds = [None] * nbig
    small_grads = [None] * DEPTH
    ffn = [4, 5]
    outs3 = [1, 2, 3]

    def blocks(ks):
        return [dws[names[k]].reshape(DEPTH, N_CHIPS, rows[k], -1) for k in ks]

    def chip_sums(ks, sib, l):
        return [_chip_sum(f"chip_sum{l}_" + names[k], b, s, l)
                for k, b, s in zip(ks, blocks(ks), sib)]

    def final_sums(ks, sums, recv, l):
        for k, s, r in zip(ks, sums, recv):
            reds[k] = _final_sum(f"final_sum{l}_" + names[k], s, r, l, reds[k])

    for l in reversed(range(DEPTH)):
        tag = f"l{l}_"
        sv = saved[l]
        every = list(range(nbig))
        df, d_nfpost = _norm_post_bwd(tag + "norm_ffn_post_bwd", dx, sv["f"], norm_ffn_post[l:l + 1])
        if l == 0:
            da, sib = _mm_nt(tag + "ffn_down_dx", df, wdn_full, l, D_FF // 2, BF16,
                             _reduce_swap(blocks(every), 1))
            sums = chip_sums(every, sib, 1)
        else:
            da = _mm_nt(tag + "ffn_down_dx", df, wdn_full, l, D_FF // 2, BF16)
        dws["w_down"] = _mm_tn(tag + "ffn_down_dw", sv["a"], df, D_FF // 2, l, dws["w_down"])
        if l == 0:
            dhu, dconv, recv = _ffn_gate_bwd(tag + "ffn_gate_bwd", da, sv["hu"], cw_full[l],
                                             conv_b[l:l + 1], _reduce_scatter(sums))
            final_sums(every, sums, recv, 1)
            dh2, reds = _mm_nt_blocked(tag + "ffn_up_dx", dhu, wup_g, l, F32,
                                       _reduce_share(reds, 1))
        else:
            dhu, dconv = _ffn_gate_bwd(tag + "ffn_gate_bwd", da, sv["hu"], cw_full[l],
                                       conv_b[l:l + 1])
            dh2 = _mm_nt_blocked(tag + "ffn_up_dx", dhu, wup_g, l, F32)
        dws["w_up"] = _mm_tn_blocked(tag + "ffn_up_dw", sv["h2"], dhu, l, dws["w_up"])
        if l == 0:
            dx1, d_nfpre, sib = _norm_pre_bwd(tag + "norm_ffn_pre_bwd", dh2, sv["x1"], dx,
                                              norm_ffn_pre[l:l + 1], _reduce_swap(blocks(ffn), 0))
            sums = chip_sums(ffn, sib, 0)
        else:
            dx1, d_nfpre = _norm_pre_bwd(tag + "norm_ffn_pre_bwd", dh2, sv["x1"], dx,
                                         norm_ffn_pre[l:l + 1])
        dmix, d_nmpost = _norm_post_bwd(tag + "norm_mix_post_bwd", dx1, sv["mix"], norm_mix_post[l:l + 1])
        dz = _mm_nt(tag + "mix_dx", dmix, wo_full, l, D_MODEL, BF16)
        dws["w_o"] = _mm_tn(tag + "mix_dw", sv["z"], dmix, D_MODEL, l, dws["w_o"])
        dya, dyb, dgates, d_bgate = _gate_bwd(tag + "gate_bwd", dz, sv["proj"], b_gate[l:l + 1],
                                              sv["ya"], sv["yb"])
        datt = _narrow_nt(tag + "attn_out_dx", dya, wao_g, l)
        dws["w_attn_out"] = _narrow_tn(tag + "attn_out_dw", sv["att"], dya, l, dws["w_attn_out"])
        dmixed = _narrow_nt(tag + "pool_out_dx", dyb, wpo_g, l)
        dws["w_pool_out"] = _narrow_tn(tag + "pool_out_dw", sv["mixed"], dyb, l, dws["w_pool_out"])
        if l == 0:
            du, d_wg, d_pscale, sib = _pool_bwd(tag + "pool_bwd", dmixed, sv["pooled"], wg_bf[l],
                                                pool_scale[l:l + 1], _reduce_swap(blocks(outs3), 0))
            sums3 = chip_sums(outs3, sib, 0)
            dqkv, dbias, recv = _attn_bwd(
                tag + "attn_bwd", sv["proj"], datt, sv["bias"],
                _both(_reduce_scatter(sums), _reduce_scatter(sums3)))
            final_sums(ffn, sums, recv[:len(ffn)], 0)
            final_sums(outs3, sums3, recv[len(ffn):], 0)
        else:
            du, d_wg, d_pscale = _pool_bwd(tag + "pool_bwd", dmixed, sv["pooled"], wg_bf[l],
                                           pool_scale[l:l + 1])
            dqkv, dbias = _attn_bwd(tag + "attn_bwd", sv["proj"], datt, sv["bias"])
        d_rel = _bias_fold(tag + "bias_fold", dbias)
        if l == 0:
            dh, shared = _proj_dx(tag + "proj_dx", dqkv, du, dgates, win_g, l,
                                  _reduce_share([reds[k] for k in ffn + outs3], 0))
            for k, r in zip(ffn + outs3, shared):
                reds[k] = r
        else:
            dh = _proj_dx(tag + "proj_dx", dqkv, du, dgates, win_g, l)
        dws["w_in"] = _proj_dw(tag + "proj_dw", sv["h"], dqkv, du, dgates, l, dws["w_in"])
        dx, d_nmpre = _norm_pre_bwd(tag + "norm_mix_pre_bwd", dh, sv["x"], dx1, norm_mix_pre[l:l + 1])
        small_grads[l] = [d_nmpre, d_nmpost, d_nfpre, d_nfpost, d_bgate, d_rel, d_wg, d_pscale,
                          dconv[3:4], dconv[0:3]]

    grad_x = dx.reshape(x.shape)

    sib = _comm_call("reduce_swap", _reduce_swap(blocks([0]), 0))
    sums = chip_sums([0], sib, 0)
    recv = _comm_call("reduce_scatter", _reduce_scatter(sums))
    final_sums([0], sums, recv, 0)
    g_big = _comm_call("reduce_share", _reduce_share([reds[0]], 0)) + reds[1:]

    pieces = []
    for idx in range(10):
        pieces.append(jnp.stack([small_grads[0][idx], small_grads[1][idx]]))
    pack = jnp.concatenate([_rows128(p) for p in pieces], axis=0)
    red = _allreduce_small(pack)
    shapes = [p.shape for p in pieces]
    outs = []
    row = 0
    for shp in shapes:
        nrow = math.prod(shp) // 128
        outs.append(red[row:row + nrow].reshape(shp))
        row += nrow
    (g_nmpre, g_nmpost, g_nfpre, g_nfpost, g_bgate, g_rel, g_wg, g_pscale, g_cb, g_cw) = outs
    g_nmpre, g_nmpost, g_nfpre, g_nfpost = [a.reshape(DEPTH, D_MODEL)
                                            for a in (g_nmpre, g_nmpost, g_nfpre, g_nfpost)]
    g_bgate = g_bgate.reshape(DEPTH, 2 * D_MODEL)
    g_rel = g_rel[:, :, :N_REL]
    g_pscale = g_pscale.reshape(DEPTH, POOL_W)
    g_cb = g_cb.reshape(DEPTH, 2 * D_FF)
    ncw = conv_w.shape[2]
    chip = 2 * lax.axis_index("x") + lax.axis_index("y")
    g_cw = lax.dynamic_slice_in_dim(g_cw, chip * ncw, ncw, axis=2)

    grads = dict(norm_mix_pre=g_nmpre, w_in=g_big[0], b_gate=g_bgate, rel_bias=g_rel,
                 w_attn_out=g_big[1], w_pool_group=g_wg, pool_scale=g_pscale, w_pool_out=g_big[2],
                 w_o=g_big[3], norm_mix_post=g_nmpost, norm_ffn_pre=g_nfpre, w_up=g_big[4],
                 conv_w=g_cw, conv_b=g_cb, w_down=g_big[5], norm_ffn_post=g_nfpost)
    weights = dict(norm_mix_pre=norm_mix_pre, w_in=w_in, b_gate=b_gate, rel_bias=rel_bias,
                   w_attn_out=w_attn_out, w_pool_group=w_pool_group, pool_scale=pool_scale,
                   w_pool_out=w_pool_out, w_o=w_o, norm_mix_post=norm_mix_post,
                   norm_ffn_pre=norm_ffn_pre, w_up=w_up, conv_w=conv_w, conv_b=conv_b,
                   w_down=w_down, norm_ffn_post=norm_ffn_post)
    moms = dict(norm_mix_pre=(m_norm_mix_pre, v_norm_mix_pre), w_in=(m_w_in, v_w_in),
                b_gate=(m_b_gate, v_b_gate), rel_bias=(m_rel_bias, v_rel_bias),
                w_attn_out=(m_w_attn_out, v_w_attn_out),
                w_pool_group=(m_w_pool_group, v_w_pool_group),
                pool_scale=(m_pool_scale, v_pool_scale), w_pool_out=(m_w_pool_out, v_w_pool_out),
                w_o=(m_w_o, v_w_o), norm_mix_post=(m_norm_mix_post, v_norm_mix_post),
                norm_ffn_pre=(m_norm_ffn_pre, v_norm_ffn_pre), w_up=(m_w_up, v_w_up),
                conv_w=(m_conv_w, v_conv_w), conv_b=(m_conv_b, v_conv_b),
                w_down=(m_w_down, v_w_down), norm_ffn_post=(m_norm_ffn_post, v_norm_ffn_post))
    order = list(weights.keys())

    delta, new_m, new_v = {}, {}, {}
    small_names = [nm for nm in order if nm not in names]
    for nm in names:
        delta[nm], new_m[nm], new_v[nm] = _adamw("adamw_" + nm, weights[nm], grads[nm], *moms[nm])

    def pack_small(get):
        flat = [get(nm).reshape(-1) for nm in small_names]
        total = sum(f.shape[0] for f in flat)
        padded = -(-total // 1024) * 1024
        flat.append(jnp.zeros((padded - total,), F32))
        return jnp.concatenate(flat).reshape(1, padded // 128, 128)

    d_s, m_s, v_s = _adamw(
        "adamw_small", pack_small(lambda nm: weights[nm]), pack_small(lambda nm: grads[nm]),
        pack_small(lambda nm: moms[nm][0]) , pack_small(lambda nm: moms[nm][1]))
    off = 0
    for nm in small_names:
        size = math.prod(weights[nm].shape)
        for dst, src in ((delta, d_s), (new_m, m_s), (new_v, v_s)):
            dst[nm] = src.reshape(-1)[off:off + size].reshape(weights[nm].shape)
        off += size

    return (loss, grad_x, *[grads[nm] for nm in order], *[delta[nm] for nm in order],
            *[new_m[nm] for nm in order], *[new_v[nm] for nm in order])
```

```python
import functools
import math

import jax
import jax.numpy as jnp
from jax import lax
from jax.experimental import pallas as pl
from jax.experimental.pallas import tpu as pltpu

F32 = jnp.float32
BF16 = jnp.bfloat16
MESH = pl.DeviceIdType.MESH

D_MODEL = 1024
DEPTH = 2
CHUNK = 64
BAND_CHUNKS = 9
BAND = BAND_CHUNKS * CHUNK
HEADS = 8
HEAD_DIM = 64
ATTN_W = HEADS * HEAD_DIM
POOL_WINDOWS = (2, 4, 8, 16)
POOL_W = 512
POOL_GD = 128
MAX_REL = 256
N_REL = 2 * MAX_REL + 1
D_FF = 2816
IN_W = 3 * ATTN_W + POOL_W + 2 * D_MODEL
EPS = 1e-6
ATTN_SCALE = HEAD_DIM ** -0.5
BAND_PAD = 640
BIAS_LANES = BAND_PAD
N_CHIPS = 4

ADAM_LR = 0.001
ADAM_B1 = 0.9
ADAM_B2 = 0.999
ADAM_EPS = 1e-08
ADAM_WD = 0.01
ADAM_STEP = 10

VMEM_LIMIT_V7X = 56 * 1024 * 1024
TOK = 512
ATT_BLK = 8 * CHUNK
FF_COL = 256
FF_TOK = 1024
HALO = 32


def _cparams(*sem):
    return pltpu.CompilerParams(dimension_semantics=sem, vmem_limit_bytes=VMEM_LIMIT_V7X)


def _sds(shape, dtype):
    return jax.ShapeDtypeStruct(shape, dtype)


class _Comm:
    def __init__(self, ins, outs, copies, n_sems, alias=None):
        self.ins, self.outs, self.copies, self.n_sems = list(ins), list(outs), copies, n_sems
        self.alias = dict(alias or {})


class _SemsFrom:
    def __init__(self, sems, start):
        self.sems, self.start = sems, start

    @property
    def at(self):
        return self

    def __getitem__(self, i):
        return self.sems.at[self.start + i]


def _both(a, b):
    na, nao = len(a.ins), len(a.outs)

    def copies(cin, cout, ssem, rsem):
        return (a.copies(cin[:na], cout[:nao], ssem, rsem)
                + b.copies(cin[na:], cout[nao:], _SemsFrom(ssem, a.n_sems), _SemsFrom(rsem, a.n_sems)))

    alias = dict(a.alias)
    alias.update({na + i: nao + o for i, o in b.alias.items()})
    return _Comm(a.ins + b.ins, a.outs + b.outs, copies, a.n_sems + b.n_sems, alias)


def _pcall(body, name, grid, in_specs, out_specs, out_shape, scratch_shapes, compiler_params, args,
           comm=None, aliases=None):
    single = not isinstance(out_shape, (list, tuple))
    out_specs = [out_specs] if single else list(out_specs)
    out_shape = [out_shape] if single else list(out_shape)
    n_in, n_out = len(in_specs), len(out_specs)
    aliases = dict(aliases or {})
    if comm is None:
        res = pl.pallas_call(
            body, name=name, grid=grid, in_specs=list(in_specs), out_specs=out_specs,
            out_shape=out_shape, scratch_shapes=list(scratch_shapes),
            input_output_aliases=aliases, compiler_params=compiler_params)(*args)
        return (res[0] if single else res), None
    ci, co = len(comm.ins), len(comm.outs)

    def hosted(*refs):
        main_in, cin = refs[:n_in], refs[n_in:n_in + ci]
        main_out = refs[n_in + ci:n_in + ci + n_out]
        cout = refs[n_in + ci + n_out:n_in + ci + n_out + co]
        rest = refs[n_in + ci + n_out + co:]
        copies = comm.copies(cin, cout, rest[-2], rest[-1])
        ids = [pl.program_id(a) for a in range(len(grid))]
        first = functools.reduce(jnp.logical_and, [i == 0 for i in ids])
        last = functools.reduce(jnp.logical_and, [i == g - 1 for i, g in zip(ids, grid)])

        @pl.when(first)
        def _():
            for cp in copies:
                cp.start()

        body(*main_in, *main_out, *rest[:-2])

        @pl.when(last)
        def _():
            for cp in copies:
                cp.wait()

    for i, o in comm.alias.items():
        aliases[n_in + i] = n_out + o
    hbm = pl.BlockSpec(memory_space=pl.ANY)
    sems = pltpu.SemaphoreType.DMA((comm.n_sems,))
    res = pl.pallas_call(
        hosted, name=name, grid=grid, in_specs=list(in_specs) + [hbm] * ci,
        out_specs=out_specs + [hbm] * co, out_shape=out_shape + comm.outs,
        scratch_shapes=list(scratch_shapes) + [sems, sems],
        input_output_aliases=aliases, compiler_params=compiler_params)(*args, *comm.ins)
    return (res[0] if single else list(res[:n_out])), list(res[n_out:])


def _comm_call(name, comm):
    ci = len(comm.ins)

    def body(*refs):
        copies = comm.copies(refs[:ci], refs[ci:-2], refs[-2], refs[-1])
        for cp in copies:
            cp.start()
        for cp in copies:
            cp.wait()

    hbm = pl.BlockSpec(memory_space=pl.ANY)
    sems = pltpu.SemaphoreType.DMA((comm.n_sems,))
    return list(pl.pallas_call(
        body, name=name, in_specs=[hbm] * ci, out_specs=[hbm] * len(comm.outs),
        out_shape=comm.outs, scratch_shapes=[sems, sems],
        input_output_aliases=comm.alias)(*comm.ins))


def _matmul(name, a, b, a_spec, b_spec, o_spec, out_shape, grid, contract, nk, acc_shape,
            fill=None, comm=None):
    def body(*refs):
        a_ref, b_ref = refs[0], refs[1]
        o_ref = refs[2 if fill is None else 3]
        scratch = refs[(3 if fill is None else 4):]
        part = lax.dot_general(a_ref[...], b_ref[...], (contract, ((), ())),
                               preferred_element_type=F32)
        if nk == 1:
            o_ref[...] = part.astype(o_ref.dtype)
        else:
            acc_ref = scratch[0]
            k = pl.program_id(2)

            @pl.when(k == 0)
            def _():
                acc_ref[...] = part

            @pl.when(k > 0)
            def _():
                acc_ref[...] += part

            @pl.when(k == nk - 1)
            def _():
                o_ref[...] = acc_ref[...].astype(o_ref.dtype)

    scratch = [] if nk == 1 else [pltpu.VMEM(acc_shape, F32)]
    in_specs, args, aliases = [a_spec, b_spec], [a, b], {}
    if fill is not None:
        in_specs.append(pl.BlockSpec(memory_space=pl.ANY))
        args.append(fill)
        aliases = {2: 0}
    out, moved = _pcall(body, name, grid, in_specs, o_spec, out_shape, scratch,
                        _cparams("parallel", "parallel", "arbitrary"), args, comm, aliases)
    return out if comm is None else (out, moved)


NN = ((1,), (0,))
NT = ((1,), (1,))
TN = ((0,), (0,))


def _tm(t):
    return min(t, 1024)


def _col_block_spec(a, rows, nb, row_col):
    if a.ndim == 2:
        return pl.BlockSpec((rows, nb), row_col)

    def halves(*ids):
        r, c = row_col(*ids)
        return c // 2, r, c % 2

    return pl.BlockSpec((None, rows, nb), halves)


def _mm_nn_blocked(name, a, w, l, out_dtype):
    t, k = a.shape
    nb = w.shape[3]
    tm = _tm(t)
    return _matmul(
        name, a, w,
        pl.BlockSpec((tm, k), lambda i, n, kk: (i, 0)),
        pl.BlockSpec((None, None, k, nb), lambda i, n, kk: (l, n, 0, 0)),
        pl.BlockSpec((tm, nb), lambda i, n, kk: (i, n)),
        _sds((t, N_CHIPS * nb), out_dtype), (t // tm, N_CHIPS, 1), NN, 1, None)


def _mm_nt_blocked(name, a, w, l, out_dtype, comm=None):
    t = a.shape[-2]
    k, nb = w.shape[2], w.shape[3]
    tm = _tm(t)
    return _matmul(
        name, a, w,
        _col_block_spec(a, tm, nb, lambda i, n, kk: (i, kk)),
        pl.BlockSpec((None, None, k, nb), lambda i, n, kk: (l, kk, 0, 0)),
        pl.BlockSpec((tm, k), lambda i, n, kk: (i, 0)),
        _sds((t, k), out_dtype), (t // tm, 1, N_CHIPS), NT, N_CHIPS, (tm, k), comm=comm)


def _mm_tn_blocked(name, a, g, l, fill):
    t, k = a.shape
    nb = g.shape[-1] * (g.ndim - 1) // N_CHIPS
    tt = _tm(t)
    nt = t // tt
    return _matmul(
        name, a, g,
        pl.BlockSpec((tt, k), lambda n, j, kk: (kk, 0)),
        _col_block_spec(g, tt, nb, lambda n, j, kk: (kk, n)),
        pl.BlockSpec((None, None, k, nb), lambda n, j, kk: (l, n, 0, 0)),
        _sds((DEPTH, N_CHIPS, k, nb), BF16), (N_CHIPS, 1, nt), TN, nt, (k, nb), fill)


def _proj_pieces(rows, dqkv_first):
    def piece(col):
        if dqkv_first:
            return pl.BlockSpec((rows, ATTN_W), lambda i, kk: (i, col))
        return pl.BlockSpec((rows, ATTN_W), lambda n, kk: (kk, col))
    return [piece(0), piece(1), piece(2), piece(0)]


def _proj_dx(name, dqkv, du, dgates, w, l, comm=None):
    t = du.shape[0]
    k, nb = w.shape[2], w.shape[3]
    tm = _tm(t)

    def body(dq_ref, dk_ref, dv_ref, du_ref, dg_ref, w_ref, o_ref, acc_ref):
        kk = pl.program_id(1)

        def mm(a):
            return lax.dot_general(a, w_ref[...], (NT, ((), ())), preferred_element_type=F32)

        @pl.when(kk == 0)
        def _():
            acc_ref[...] = mm(jnp.concatenate([dq_ref[...], dk_ref[...]], axis=1))

        @pl.when(kk == 1)
        def _():
            acc_ref[...] += mm(jnp.concatenate([dv_ref[...], du_ref[...]], axis=1))

        @pl.when(kk >= 2)
        def _():
            acc_ref[...] += mm(dg_ref[...])

        @pl.when(kk == N_CHIPS - 1)
        def _():
            o_ref[...] = acc_ref[...]

    out, moved = _pcall(
        body, name, (t // tm, N_CHIPS),
        _proj_pieces(tm, True)
        + [pl.BlockSpec((tm, nb), lambda i, kk: (i, jnp.maximum(kk - 2, 0))),
           pl.BlockSpec((None, None, k, nb), lambda i, kk: (l, kk, 0, 0))],
        pl.BlockSpec((tm, k), lambda i, kk: (i, 0)), _sds((t, k), F32),
        [pltpu.VMEM((tm, k), F32)], _cparams("arbitrary", "arbitrary"),
        (dqkv, dqkv, dqkv, du, dgates, w), comm)
    return out if comm is None else (out, moved)


def _proj_dw(name, h, dqkv, du, dgates, l, fill):
    t, k = h.shape
    nb = dgates.shape[1] // 2
    tt = _tm(t)
    nt = t // tt

    def body(*refs):
        h_ref, dq_ref, dk_ref, dv_ref, du_ref, dg_ref = refs[:6]
        o_ref, acc_ref = refs[-2], refs[-1]
        n, kk = pl.program_id(0), pl.program_id(1)

        def update(g):
            part = lax.dot_general(h_ref[...], g, (TN, ((), ())), preferred_element_type=F32)

            @pl.when(kk == 0)
            def _():
                acc_ref[...] = part

            @pl.when(kk > 0)
            def _():
                acc_ref[...] += part

        @pl.when(n == 0)
        def _():
            update(jnp.concatenate([dq_ref[...], dk_ref[...]], axis=1))

        @pl.when(n == 1)
        def _():
            update(jnp.concatenate([dv_ref[...], du_ref[...]], axis=1))

        @pl.when(n >= 2)
        def _():
            update(dg_ref[...])

        @pl.when(kk == nt - 1)
        def _():
            o_ref[...] = acc_ref[...].astype(BF16)

    in_specs = ([pl.BlockSpec((tt, k), lambda n, kk: (kk, 0))] + _proj_pieces(tt, False)
                + [pl.BlockSpec((tt, nb), lambda n, kk: (kk, jnp.maximum(n - 2, 0)))])
    args, aliases = [h, dqkv, dqkv, dqkv, du, dgates], {}
    if fill is not None:
        in_specs.append(pl.BlockSpec(memory_space=pl.ANY))
        args.append(fill)
        aliases = {6: 0}
    return pl.pallas_call(
        body, name=name, grid=(N_CHIPS, nt), in_specs=in_specs,
        out_specs=pl.BlockSpec((None, None, k, nb), lambda n, kk: (l, n, 0, 0)),
        out_shape=_sds((DEPTH, N_CHIPS, k, nb), BF16),
        scratch_shapes=[pltpu.VMEM((k, nb), F32)], input_output_aliases=aliases,
        compiler_params=_cparams("parallel", "arbitrary"))(*args)


def _narrow_nn(name, a, w, l):
    t, k = a.shape
    nb = w.shape[3]
    tm = _tm(t)

    def body(a_ref, w_ref, o_ref):
        av = a_ref[...]
        for j in range(N_CHIPS):
            o_ref[:, j * nb:(j + 1) * nb] = jnp.dot(
                av, w_ref[j], preferred_element_type=F32).astype(BF16)

    return pl.pallas_call(
        body, name=name, grid=(t // tm,),
        in_specs=[pl.BlockSpec((tm, k), lambda i: (i, 0)),
                  pl.BlockSpec((None, N_CHIPS, k, nb), lambda i: (l, 0, 0, 0))],
        out_specs=pl.BlockSpec((tm, N_CHIPS * nb), lambda i: (i, 0)),
        out_shape=_sds((t, N_CHIPS * nb), BF16), compiler_params=_cparams("parallel"))(a, w)


def _narrow_nt(name, a, w, l):
    t = a.shape[0]
    k, nb = w.shape[2], w.shape[3]
    tm = _tm(t)

    def body(a_ref, w_ref, o_ref):
        acc = lax.dot_general(a_ref[:, 0:nb], w_ref[0], (NT, ((), ())), preferred_element_type=F32)
        for j in range(1, N_CHIPS):
            acc = acc + lax.dot_general(a_ref[:, j * nb:(j + 1) * nb], w_ref[j], (NT, ((), ())),
                                        preferred_element_type=F32)
        o_ref[...] = acc.astype(BF16)

    return pl.pallas_call(
        body, name=name, grid=(t // tm,),
        in_specs=[pl.BlockSpec((tm, N_CHIPS * nb), lambda i: (i, 0)),
                  pl.BlockSpec((None, N_CHIPS, k, nb), lambda i: (l, 0, 0, 0))],
        out_specs=pl.BlockSpec((tm, k), lambda i: (i, 0)),
        out_shape=_sds((t, k), BF16), compiler_params=_cparams("parallel"))(a, w)


def _narrow_tn(name, a, g, l, fill):
    t, k = a.shape
    nb = g.shape[1] // N_CHIPS
    tt = _tm(t)
    nt = t // tt

    def body(*refs):
        a_ref, g_ref, o_ref, acc_ref = refs[0], refs[1], refs[-2], refs[-1]
        i = pl.program_id(0)
        part = lax.dot_general(a_ref[...], g_ref[...], (TN, ((), ())), preferred_element_type=F32)

        @pl.when(i == 0)
        def _():
            acc_ref[...] = part

        @pl.when(i > 0)
        def _():
            acc_ref[...] += part

        @pl.when(i == nt - 1)
        def _():
            for j in range(N_CHIPS):
                o_ref[j] = acc_ref[:, j * nb:(j + 1) * nb].astype(BF16)

    in_specs = [pl.BlockSpec((tt, k), lambda i: (i, 0)),
                pl.BlockSpec((tt, N_CHIPS * nb), lambda i: (i, 0))]
    args, aliases = [a, g], {}
    if fill is not None:
        in_specs.append(pl.BlockSpec(memory_space=pl.ANY))
        args.append(fill)
        aliases = {2: 0}
    return pl.pallas_call(
        body, name=name, grid=(nt,), in_specs=in_specs,
        out_specs=pl.BlockSpec((None, N_CHIPS, k, nb), lambda i: (l, 0, 0, 0)),
        out_shape=_sds((DEPTH, N_CHIPS, k, nb), BF16),
        scratch_shapes=[pltpu.VMEM((k, N_CHIPS * nb), F32)], input_output_aliases=aliases,
        compiler_params=_cparams("arbitrary"))(*args)


def _mm_nn(name, a, w, l, tk, out_dtype):
    t, k = a.shape
    n = w.shape[2]
    tm = _tm(t)
    nk = k // tk
    return _matmul(
        name, a, w,
        pl.BlockSpec((tm, tk), lambda i, j, kk: (i, kk)),
        pl.BlockSpec((None, tk, n), lambda i, j, kk: (l, kk, 0)),
        pl.BlockSpec((tm, n), lambda i, j, kk: (i, 0)),
        _sds((t, n), out_dtype), (t // tm, 1, nk), NN, nk, (tm, n))


def _mm_nt(name, a, w, l, tn, out_dtype, comm=None):
    t, n = a.shape
    k = w.shape[1]
    tm = _tm(t)
    return _matmul(
        name, a, w,
        pl.BlockSpec((tm, n), lambda i, j, kk: (i, 0)),
        pl.BlockSpec((None, tn, n), lambda i, j, kk: (l, j, 0)),
        pl.BlockSpec((tm, tn), lambda i, j, kk: (i, j)),
        _sds((t, k), out_dtype), (t // tm, k // tn, 1), NT, 1, None, comm=comm)


def _mm_tn(name, a, g, tko, l, fill):
    t, k = a.shape
    n = g.shape[1]
    tt = _tm(t)
    nt = t // tt
    return _matmul(
        name, a, g,
        pl.BlockSpec((tt, tko), lambda i, j, kk: (kk, i)),
        pl.BlockSpec((tt, n), lambda i, j, kk: (kk, 0)),
        pl.BlockSpec((None, tko, n), lambda i, j, kk: (l, i, 0)),
        _sds((DEPTH, k, n), BF16), (k // tko, 1, nt), TN, nt, (tko, n), fill)


def _row_spec(width, col=0):
    return pl.BlockSpec((TOK, width), lambda i: (i, col))


def _vec_spec(width):
    return pl.BlockSpec((1, width), lambda i: (0, 0))


def _rms(x):
    return lax.rsqrt(jnp.mean(x * x, axis=-1, keepdims=True) + EPS)


def _norm_fwd(name, x, g):
    t = x.shape[0]

    def body(x_ref, g_ref, h_ref):
        xv = x_ref[...]
        h_ref[...] = (xv * _rms(xv) * g_ref[...]).astype(BF16)

    return pl.pallas_call(
        body, name=name, grid=(t // TOK,), in_specs=[_row_spec(D_MODEL), _vec_spec(D_MODEL)],
        out_specs=_row_spec(D_MODEL), out_shape=_sds((t, D_MODEL), BF16),
        compiler_params=_cparams("parallel"))(x, g)


ROWS = 16


def _rows(k):
    return pl.ds(pl.multiple_of(k * ROWS, ROWS), ROWS)


def _fold_rows(x):
    return x[0:8] + x[8:16]


def _accumulate(ref, part):
    total = jnp.sum(part, axis=0, keepdims=True)

    @pl.when(pl.program_id(0) == 0)
    def _():
        ref[...] = total

    @pl.when(pl.program_id(0) > 0)
    def _():
        ref[...] += total


def _norm_bwd_rows(d, mv, g):
    r = _rms(mv)
    n = mv * r
    dn = d * g
    return r * (dn - n * jnp.mean(dn * n, axis=-1, keepdims=True)), d * n


def _post_pre_fwd(name, xres, m, g_post, g_pre, comm=None):
    t = xres.shape[0]

    def body(x_ref, m_ref, gp_ref, gn_ref, x1_ref, h_ref):
        def strip(k, c):
            rows = _rows(k)
            mv = m_ref[rows, :]
            x1 = x_ref[rows, :] + mv * _rms(mv) * gp_ref[...]
            x1_ref[rows, :] = x1
            h_ref[rows, :] = (x1 * _rms(x1) * gn_ref[...]).astype(BF16)
            return c

        lax.fori_loop(0, TOK // ROWS, strip, 0)

    outs, moved = _pcall(
        body, name, (t // TOK,),
        [_row_spec(D_MODEL), _row_spec(D_MODEL), _vec_spec(D_MODEL), _vec_spec(D_MODEL)],
        [_row_spec(D_MODEL), _row_spec(D_MODEL)],
        [_sds((t, D_MODEL), F32), _sds((t, D_MODEL), BF16)], [], _cparams("arbitrary"),
        (xres, m, g_post, g_pre), comm)
    return outs if comm is None else (*outs, moved)


def _tail(name, xres, m, g_post, target):
    t = xres.shape[0]

    def body(x_ref, m_ref, g_ref, t_ref, dy_ref, dm_ref, dg_ref, l_ref):
        def strip(k, carry):
            rows = _rows(k)
            mv = m_ref[rows, :]
            e = x_ref[rows, :] + mv * _rms(mv) * g_ref[...] - t_ref[rows, :]
            dy = e * (1.0 / D_MODEL)
            dy_ref[rows, :] = dy
            dm, dgn = _norm_bwd_rows(dy, mv, g_ref[...])
            dm_ref[rows, :] = dm.astype(BF16)
            return carry[0] + _fold_rows(dgn), carry[1] + _fold_rows(e * e)

        zero = jnp.zeros((8, D_MODEL), F32)
        dg, sq = lax.fori_loop(0, TOK // ROWS, strip, (zero, zero))
        _accumulate(dg_ref, dg)
        _accumulate(l_ref, jnp.sum(sq, axis=1, keepdims=True))

    dy, dm, dg, sq = pl.pallas_call(
        body, name=name, grid=(t // TOK,),
        in_specs=[_row_spec(D_MODEL), _row_spec(D_MODEL), _vec_spec(D_MODEL), _row_spec(D_MODEL)],
        out_specs=[_row_spec(D_MODEL), _row_spec(D_MODEL), _vec_spec(D_MODEL),
                   pl.BlockSpec((1, 1), lambda i: (0, 0))],
        out_shape=[_sds((t, D_MODEL), F32), _sds((t, D_MODEL), BF16), _sds((1, D_MODEL), F32),
                   _sds((1, 1), F32)],
        compiler_params=_cparams("arbitrary"))(xres, m, g_post, target)
    return dy, dm, dg, sq[0, 0] * (0.5 / D_MODEL)


def _pre_post_bwd(name, dh, xin, dxo, g_pre, m, g_post, comm=None):
    t = dh.shape[0]

    def body(dh_ref, x_ref, d_ref, gq_ref, m_ref, gp_ref, dx_ref, dgq_ref, dm_ref, dgp_ref):
        def strip(k, carry):
            rows = _rows(k)
            dxin, dgq = _norm_bwd_rows(dh_ref[rows, :], x_ref[rows, :], gq_ref[...])
            dx = d_ref[rows, :] + dxin
            dx_ref[rows, :] = dx
            dm, dgp = _norm_bwd_rows(dx, m_ref[rows, :], gp_ref[...])
            dm_ref[rows, :] = dm.astype(BF16)
            return carry[0] + _fold_rows(dgq), carry[1] + _fold_rows(dgp)

        zero = jnp.zeros((8, D_MODEL), F32)
        dgq, dgp = lax.fori_loop(0, TOK // ROWS, strip, (zero, zero))
        _accumulate(dgq_ref, dgq)
        _accumulate(dgp_ref, dgp)

    outs, moved = _pcall(
        body, name, (t // TOK,),
        [_row_spec(D_MODEL), _row_spec(D_MODEL), _row_spec(D_MODEL), _vec_spec(D_MODEL),
         _row_spec(D_MODEL), _vec_spec(D_MODEL)],
        [_row_spec(D_MODEL), _vec_spec(D_MODEL), _row_spec(D_MODEL), _vec_spec(D_MODEL)],
        [_sds((t, D_MODEL), F32), _sds((1, D_MODEL), F32), _sds((t, D_MODEL), BF16),
         _sds((1, D_MODEL), F32)], [], _cparams("arbitrary"),
        (dh, xin, dxo, g_pre, m, g_post), comm)
    return outs if comm is None else (*outs, moved)


def _norm_pre_bwd(name, dh, xin, dxo, g, comm=None):
    t = dh.shape[0]

    def body(dh_ref, x_ref, d_ref, g_ref, dx_ref, dg_ref):
        xv = x_ref[...]
        dhv = dh_ref[...]
        r = _rms(xv)
        n = xv * r
        dn = dhv * g_ref[...]
        dx_ref[...] = d_ref[...] + r * (dn - n * jnp.mean(dn * n, axis=-1, keepdims=True))
        part = jnp.sum(dhv * n, axis=0, keepdims=True)

        @pl.when(pl.program_id(0) == 0)
        def _():
            dg_ref[...] = part

        @pl.when(pl.program_id(0) > 0)
        def _():
            dg_ref[...] += part

    out, moved = _pcall(
        body, name, (t // TOK,),
        [_row_spec(D_MODEL), _row_spec(D_MODEL), _row_spec(D_MODEL), _vec_spec(D_MODEL)],
        [_row_spec(D_MODEL), _vec_spec(D_MODEL)],
        [_sds((t, D_MODEL), F32), _sds((1, D_MODEL), F32)], [], _cparams("arbitrary"),
        (dh, xin, dxo, g), comm)
    return out if comm is None else (*out, moved)


def _gate_fwd(name, proj, b_gate, ya, yb):
    t = proj.shape[0]

    def body(ga_ref, gb_ref, b_ref, ya_ref, yb_ref, z_ref):
        sa = jax.nn.sigmoid(ga_ref[...].astype(F32) + b_ref[:, :D_MODEL])
        sb = jax.nn.sigmoid(gb_ref[...].astype(F32) + b_ref[:, D_MODEL:])
        z_ref[...] = (sa * ya_ref[...].astype(F32) + sb * yb_ref[...].astype(F32)).astype(BF16)

    return pl.pallas_call(
        body, name=name, grid=(t // TOK,),
        in_specs=[_row_spec(D_MODEL, 2), _row_spec(D_MODEL, 3), _vec_spec(2 * D_MODEL),
                  _row_spec(D_MODEL), _row_spec(D_MODEL)],
        out_specs=_row_spec(D_MODEL), out_shape=_sds((t, D_MODEL), BF16),
        compiler_params=_cparams("parallel"))(proj, proj, b_gate, ya, yb)


def _gate_bwd(name, dz, proj, b_gate, ya, yb):
    t = proj.shape[0]

    def body(dz_ref, ga_ref, gb_ref, b_ref, ya_ref, yb_ref, dya_ref, dyb_ref, dg_ref, db_ref):
        dzv = dz_ref[...].astype(F32)
        sa = jax.nn.sigmoid(ga_ref[...].astype(F32) + b_ref[:, :D_MODEL])
        sb = jax.nn.sigmoid(gb_ref[...].astype(F32) + b_ref[:, D_MODEL:])
        dya_ref[...] = (dzv * sa).astype(BF16)
        dyb_ref[...] = (dzv * sb).astype(BF16)
        dga = dzv * ya_ref[...].astype(F32) * sa * (1.0 - sa)
        dgb = dzv * yb_ref[...].astype(F32) * sb * (1.0 - sb)
        dg_ref[:, :D_MODEL] = dga.astype(BF16)
        dg_ref[:, D_MODEL:] = dgb.astype(BF16)
        pa = jnp.sum(dga, axis=0, keepdims=True)
        pb = jnp.sum(dgb, axis=0, keepdims=True)

        @pl.when(pl.program_id(0) == 0)
        def _():
            db_ref[:, :D_MODEL] = pa
            db_ref[:, D_MODEL:] = pb

        @pl.when(pl.program_id(0) > 0)
        def _():
            db_ref[:, :D_MODEL] += pa
            db_ref[:, D_MODEL:] += pb

    return pl.pallas_call(
        body, name=name, grid=(t // TOK,),
        in_specs=[_row_spec(D_MODEL), _row_spec(D_MODEL, 2), _row_spec(D_MODEL, 3),
                  _vec_spec(2 * D_MODEL), _row_spec(D_MODEL), _row_spec(D_MODEL)],
        out_specs=[_row_spec(D_MODEL), _row_spec(D_MODEL), _row_spec(2 * D_MODEL),
                   _vec_spec(2 * D_MODEL)],
        out_shape=[_sds((t, D_MODEL), BF16), _sds((t, D_MODEL), BF16),
                   _sds((t, 2 * D_MODEL), BF16), _sds((1, 2 * D_MODEL), F32)],
        compiler_params=_cparams("arbitrary"))(dz, proj, proj, b_gate, ya, yb)


def _head_masks():
    lane = lax.broadcasted_iota(jnp.int32, (1, 2 * HEAD_DIM), 1)
    return lane < HEAD_DIM


BAND_ROWS = 2 * ATT_BLK + CHUNK


def _fill_band(band, prev_ref, cur_ref):
    band[0:ATT_BLK, :] = prev_ref[...]
    band[ATT_BLK:2 * ATT_BLK, :] = cur_ref[...]
    band[2 * ATT_BLK:, :] = jnp.zeros((CHUNK, ATTN_W), BF16)


def _pair_rows(x2, low):
    zero = jnp.zeros_like(x2)
    return jnp.concatenate([jnp.where(low, x2, zero), jnp.where(low, zero, x2)], axis=0)


def _pair_diag(o2, low):
    return jnp.where(low, o2[0:CHUNK, :], o2[CHUNK:, :])


N_PAIRS = HEADS // 2
SM_STRIP = 32
N_STRIPS = BAND_PAD // SM_STRIP
NEG = -1e30


def _fold8(x, op):
    return op(op(x[0:8], x[8:16]), op(x[16:24], x[24:32]))


def _strip(k):
    return pl.ds(pl.multiple_of(k * SM_STRIP, SM_STRIP), SM_STRIP)


def _band_probs(k2, qcat, bias_t, first_key):
    kpos = lax.broadcasted_iota(jnp.int32, (BAND_PAD, 1), 0)
    st = lax.dot_general(k2, qcat, (NT, ((), ())), preferred_element_type=F32)
    st = jnp.where(kpos + first_key >= 0, st + bias_t, NEG)
    e = jnp.exp(st - jnp.max(st, axis=0, keepdims=True))
    return e * (1.0 / jnp.sum(e, axis=0, keepdims=True))


def _band_softmax_stats(st_ref, b_ref, first_key, dp_ref):
    rowi = lax.broadcasted_iota(jnp.int32, (SM_STRIP, 128), 0)

    def scores(k, mx):
        rows = _strip(k)
        live = (rowi + (k * SM_STRIP + first_key)) >= 0
        out = []
        for hp in range(N_PAIRS):
            x = jnp.where(live, st_ref[hp, rows, :] + b_ref[hp, rows, :], NEG)
            st_ref[hp, rows, :] = x
            out.append(jnp.maximum(mx[hp], _fold8(x, jnp.maximum)))
        return tuple(out)

    mx = lax.fori_loop(0, N_STRIPS, scores, (jnp.full((8, 128), NEG, F32),) * N_PAIRS, unroll=2)
    top = [jnp.max(m, axis=0, keepdims=True) for m in mx]

    def sums(k, acc):
        rows = _strip(k)
        ls, eds = [], []
        for hp in range(N_PAIRS):
            e = jnp.exp(st_ref[hp, rows, :] - top[hp])
            ls.append(acc[hp] + _fold8(e, jnp.add))
            eds.append(acc[N_PAIRS + hp] + _fold8(e * dp_ref[hp, rows, :], jnp.add))
        return tuple(ls + eds)

    acc = lax.fori_loop(0, N_STRIPS, sums, (jnp.zeros((8, 128), F32),) * (2 * N_PAIRS), unroll=2)
    inv = [1.0 / jnp.sum(a, axis=0, keepdims=True) for a in acc[:N_PAIRS]]
    delta = [jnp.sum(a, axis=0, keepdims=True) * i for a, i in zip(acc[N_PAIRS:], inv)]
    return top, inv, delta


def _attn_specs(nblk):
    cur = lambda col: pl.BlockSpec((ATT_BLK, ATTN_W), lambda s: (jnp.minimum(s, nblk - 1), col))
    prev = lambda col: pl.BlockSpec(
        (ATT_BLK, ATTN_W), lambda s: (jnp.maximum(jnp.minimum(s, nblk - 1) - 1, 0), col))
    return cur, prev


def _attn_fwd(name, proj, bias, comm=None):
    t = proj.shape[0]
    nblk = t // ATT_BLK
    cur, prev = _attn_specs(nblk)

    def body(q_ref, kp_ref, kc_ref, vp_ref, vc_ref, b_ref, o_ref, kband, vband):
        s = pl.program_id(0)
        _fill_band(kband, kp_ref, kc_ref)
        _fill_band(vband, vp_ref, vc_ref)
        low = _head_masks()

        def chunk(ci, carry):
            r0 = pl.multiple_of(ci * CHUNK, CHUNK)
            for hp in range(N_PAIRS):
                cols = slice(hp * 128, (hp + 1) * 128)
                qcat = _pair_rows(q_ref[pl.ds(r0, CHUNK), cols] * ATTN_SCALE, low)
                p = _band_probs(kband[pl.ds(r0, BAND_PAD), cols], qcat, b_ref[hp],
                                (s * 8 - 8 + ci) * CHUNK)
                o2 = lax.dot_general(p.astype(BF16), vband[pl.ds(r0, BAND_PAD), cols],
                                     (TN, ((), ())), preferred_element_type=F32)
                o_ref[pl.ds(r0, CHUNK), cols] = _pair_diag(o2, low).astype(BF16)
            return carry

        lax.fori_loop(0, 8, chunk, 0)

    out, moved = _pcall(
        body, name, (nblk,),
        [cur(0), prev(1), cur(1), prev(2), cur(2),
         pl.BlockSpec((N_PAIRS, BAND_PAD, 128), lambda s: (0, 0, 0))],
        pl.BlockSpec((ATT_BLK, ATTN_W), lambda s: (s, 0)), _sds((t, ATTN_W), BF16),
        [pltpu.VMEM((BAND_ROWS, ATTN_W), BF16), pltpu.VMEM((BAND_ROWS, ATTN_W), BF16)],
        _cparams("arbitrary"), (proj, proj, proj, proj, proj, bias), comm)
    return out if comm is None else (out, moved)


def _attn_bwd(name, proj, datt, bias, comm=None):
    t = proj.shape[0]
    nblk = t // ATT_BLK
    cur, prev = _attn_specs(nblk)
    late = pl.BlockSpec((ATT_BLK, 3 * ATTN_W), lambda s: (jnp.maximum(s - 1, 0), 0))

    def body(q_ref, kp_ref, kc_ref, vp_ref, vc_ref, do_ref, b_ref,
             dqkv_ref, db_ref, kband, vband, dkacc, dvacc,
             st_ref, dp_ref, pb_ref, dsb_ref, qc_ref, dc_ref, dq_ref, dq_held):
        s = pl.program_id(0)

        @pl.when(s == 0)
        def _():
            dkacc[...] = jnp.zeros_like(dkacc)
            dvacc[...] = jnp.zeros_like(dvacc)
            db_ref[...] = jnp.zeros_like(db_ref)
            dq_ref[...] = jnp.zeros_like(dq_ref)

        @pl.when(s < nblk)
        def _():
            _fill_band(kband, kp_ref, kc_ref)
            _fill_band(vband, vp_ref, vc_ref)
            low = _head_masks()

            def chunk(ci, carry):
                r0 = pl.multiple_of(ci * CHUNK, CHUNK)
                for hp in range(N_PAIRS):
                    cols = slice(hp * 128, (hp + 1) * 128)
                    qc_ref[hp] = _pair_rows(q_ref[pl.ds(r0, CHUNK), cols] * ATTN_SCALE, low)
                    dc_ref[hp] = _pair_rows(do_ref[pl.ds(r0, CHUNK), cols], low)
                    st_ref[hp] = lax.dot_general(kband[pl.ds(r0, BAND_PAD), cols], qc_ref[hp],
                                                 (NT, ((), ())), preferred_element_type=F32)
                    dp_ref[hp] = lax.dot_general(vband[pl.ds(r0, BAND_PAD), cols], dc_ref[hp],
                                                 (NT, ((), ())), preferred_element_type=F32)
                top, inv, delta = _band_softmax_stats(st_ref, b_ref, (s * 8 - 8 + ci) * CHUNK,
                                                      dp_ref)

                def grads(k, c):
                    rows = _strip(k)
                    for hp in range(N_PAIRS):
                        p = jnp.exp(st_ref[hp, rows, :] - top[hp]) * inv[hp]
                        ds = p * (dp_ref[hp, rows, :] - delta[hp])
                        db_ref[hp, rows, :] += ds
                        dsb_ref[hp, rows, :] = ds.astype(BF16)
                        pb_ref[hp, rows, :] = p.astype(BF16)
                    return c

                lax.fori_loop(0, N_STRIPS, grads, 0, unroll=2)
                for hp in range(N_PAIRS):
                    cols = slice(hp * 128, (hp + 1) * 128)
                    dq2 = lax.dot_general(dsb_ref[hp], kband[pl.ds(r0, BAND_PAD), cols],
                                          (TN, ((), ())), preferred_element_type=F32)
                    dq_ref[pl.ds(r0, CHUNK), cols] = (_pair_diag(dq2, low) * ATTN_SCALE).astype(BF16)
                    dkacc[pl.ds(r0, BAND_PAD), cols] += jnp.dot(dsb_ref[hp], qc_ref[hp],
                                                               preferred_element_type=F32)
                    dvacc[pl.ds(r0, BAND_PAD), cols] += jnp.dot(pb_ref[hp], dc_ref[hp],
                                                               preferred_element_type=F32)
                return carry

            dq_held[...] = dq_ref[...]
            lax.fori_loop(0, 8, chunk, 0)

        @pl.when(s == nblk)
        def _():
            dq_held[...] = dq_ref[...]

        dqkv_ref[:, 0:ATTN_W] = dq_held[...]
        dqkv_ref[:, ATTN_W:2 * ATTN_W] = dkacc[0:ATT_BLK, :].astype(BF16)
        dqkv_ref[:, 2 * ATTN_W:] = dvacc[0:ATT_BLK, :].astype(BF16)
        dkacc[0:ATT_BLK, :] = dkacc[ATT_BLK:2 * ATT_BLK, :]
        dvacc[0:ATT_BLK, :] = dvacc[ATT_BLK:2 * ATT_BLK, :]
        dkacc[ATT_BLK:, :] = jnp.zeros((ATT_BLK + CHUNK, ATTN_W), F32)
        dvacc[ATT_BLK:, :] = jnp.zeros((ATT_BLK + CHUNK, ATTN_W), F32)

    outs, moved = _pcall(
        body, name, (nblk + 1,),
        [cur(0), prev(1), cur(1), prev(2), cur(2),
         pl.BlockSpec((ATT_BLK, ATTN_W), lambda s: (jnp.minimum(s, nblk - 1), 0)),
         pl.BlockSpec((HEADS // 2, BAND_PAD, 128), lambda s: (0, 0, 0))],
        [late, pl.BlockSpec((HEADS // 2, BAND_PAD, 128), lambda s: (0, 0, 0))],
        [_sds((t, 3 * ATTN_W), BF16), _sds((HEADS // 2, BAND_PAD, 128), F32)],
        [pltpu.VMEM((BAND_ROWS, ATTN_W), BF16), pltpu.VMEM((BAND_ROWS, ATTN_W), BF16),
         pltpu.VMEM((BAND_ROWS, ATTN_W), F32), pltpu.VMEM((BAND_ROWS, ATTN_W), F32),
         pltpu.VMEM((N_PAIRS, BAND_PAD, 128), F32), pltpu.VMEM((N_PAIRS, BAND_PAD, 128), F32),
         pltpu.VMEM((N_PAIRS, BAND_PAD, 128), BF16), pltpu.VMEM((N_PAIRS, BAND_PAD, 128), BF16),
         pltpu.VMEM((N_PAIRS, 2 * CHUNK, 128), BF16), pltpu.VMEM((N_PAIRS, 2 * CHUNK, 128), BF16),
         pltpu.VMEM((ATT_BLK, ATTN_W), BF16), pltpu.VMEM((ATT_BLK, ATTN_W), BF16)],
        _cparams("arbitrary"), (proj, proj, proj, proj, proj, datt, bias), comm)
    return outs if comm is None else (*outs, moved)


def _diag_onehot(rel_rows):
    d0 = lax.broadcasted_iota(jnp.int32, (BIAS_LANES, BIAS_LANES), 0)
    d1 = lax.broadcasted_iota(jnp.int32, (BIAS_LANES, BIAS_LANES), 1)
    m, n = (d0, d1) if rel_rows else (d1, d0)
    hit = (m == jnp.minimum(BAND - 1 + MAX_REL - n, 2 * MAX_REL)) & (n < BAND + CHUNK - 1)
    return jnp.where(hit, 1.0, 0.0).astype(F32)


def _bias_table(name, rel_bias_l):
    rel_pad = jnp.pad(rel_bias_l, ((0, 0), (0, BIAS_LANES - N_REL)))

    def body(r_ref, o_ref):
        diag = jnp.dot(r_ref[...], _diag_onehot(True), preferred_element_type=F32,
                       precision=lax.Precision.HIGHEST)
        rowid = lax.broadcasted_iota(jnp.int32, (8, BIAS_LANES), 0)
        lane = lax.broadcasted_iota(jnp.int32, (8, BIAS_LANES), 1)
        for h in range(HEADS):
            d8 = jnp.broadcast_to(diag[h:h + 1, :], (8, BIAS_LANES))
            slab0 = pltpu.roll(d8, BIAS_LANES - CHUNK + 1, axis=1)
            for b in range(1, 8):
                slab0 = jnp.where(rowid == b, pltpu.roll(d8, BIAS_LANES - CHUNK + 1 + b, axis=1),
                                  slab0)
            for a in range(8):
                slab = slab0 if a == 0 else pltpu.roll(slab0, 8 * a, axis=1)
                o_ref[h * CHUNK + 8 * a:h * CHUNK + 8 * a + 8, :] = jnp.where(lane < BAND, slab, NEG)

    tab = pl.pallas_call(
        body, name=name,
        in_specs=[pl.BlockSpec(memory_space=pltpu.VMEM)],
        out_specs=pl.BlockSpec(memory_space=pltpu.VMEM),
        out_shape=_sds((HEADS * CHUNK, BIAS_LANES), F32),
    )(rel_pad)
    tab = tab.reshape(HEADS // 2, 2, CHUNK, BIAS_LANES)
    return jnp.transpose(tab, (0, 3, 1, 2)).reshape(HEADS // 2, BIAS_LANES, 2 * CHUNK)


def _bias_fold(name, dbias_t):
    rows = HEADS * CHUNK
    dbias = jnp.transpose(dbias_t.reshape(HEADS // 2, BIAS_LANES, 2, CHUNK), (0, 2, 3, 1))

    def body(d_ref, o_ref):
        rowid = lax.broadcasted_iota(jnp.int32, (8, BIAS_LANES), 0)
        diags = []
        for h in range(HEADS):
            acc = d_ref[h * CHUNK + 56:h * CHUNK + 64, :]
            for a in range(7):
                slab = d_ref[h * CHUNK + 8 * a:h * CHUNK + 8 * a + 8, :]
                acc = acc + pltpu.roll(slab, 56 - 8 * a, axis=1)
            tot = jnp.where(rowid == 7, acc, 0.0)
            for b in range(7):
                tot = tot + jnp.where(rowid == b, pltpu.roll(acc, 7 - b, axis=1), 0.0)
            diags.append(jnp.sum(tot, axis=0, keepdims=True))
        diag = jnp.concatenate(diags, axis=0)
        o_ref[...] = jnp.dot(diag, _diag_onehot(False), preferred_element_type=F32,
                             precision=lax.Precision.HIGHEST)

    return pl.pallas_call(
        body, name=name,
        in_specs=[pl.BlockSpec(memory_space=pltpu.VMEM)],
        out_specs=pl.BlockSpec(memory_space=pltpu.VMEM),
        out_shape=_sds((HEADS, BIAS_LANES), F32),
    )(dbias.reshape(rows, BIAS_LANES))


def _inv_counts(i):
    trow = lax.broadcasted_iota(jnp.int32, (TOK + HALO, 1), 0) + i * TOK
    return [1.0 / jnp.minimum(trow + 1, w).astype(F32) for w in POOL_WINDOWS]


def _pool_fwd(name, proj, wg, scale, comm=None):
    t = proj.shape[0]
    hb = TOK // HALO

    def body(u_ref, up_ref, wg_ref, sc_ref, pooled_ref, mixed_ref, b0, b1, b2, b3):
        i = pl.program_id(0)
        halo = up_ref[...].astype(F32)
        b0[0:HALO, :] = jnp.where(i == 0, jnp.zeros_like(halo), halo)
        b0[HALO:, :] = u_ref[...].astype(F32)
        n = TOK + HALO
        b1[8:n, :] = b0[8:n, :] + b0[7:n - 1, :]
        b2[16:n, 128:] = b1[16:n, 128:] + b1[14:n - 2, 128:]
        b3[24:n, 256:] = b2[24:n, 256:] + b2[20:n - 4, 256:]
        wins = [b1[HALO:n, 0:128], b2[HALO:n, 128:256], b3[HALO:n, 256:384],
                b3[HALO:n, 384:512] + b3[HALO - 8:n - 8, 384:512]]
        inv = _inv_counts(i)
        for g in range(4):
            cols = slice(g * POOL_GD, (g + 1) * POOL_GD)
            pooled = (wins[g] * inv[g][0:TOK] - b0[HALO:n, cols]).astype(BF16)
            pooled_ref[:, cols] = pooled
            pre = jnp.dot(pooled, wg_ref[g], preferred_element_type=F32)
            mixed_ref[:, cols] = (pre * sc_ref[:, cols]).astype(BF16)

    buf = pltpu.VMEM((TOK + HALO, POOL_W), F32)
    outs, moved = _pcall(
        body, name, (t // TOK,),
        [_row_spec(POOL_W, 3),
         pl.BlockSpec((HALO, POOL_W), lambda i: (jnp.maximum(i * hb - 1, 0), 3)),
         pl.BlockSpec((4, POOL_GD, POOL_GD), lambda i: (0, 0, 0)), _vec_spec(POOL_W)],
        [_row_spec(POOL_W), _row_spec(POOL_W)],
        [_sds((t, POOL_W), BF16), _sds((t, POOL_W), BF16)], [buf, buf, buf, buf],
        _cparams("arbitrary"), (proj, proj, wg, scale), comm)
    return outs if comm is None else (*outs, moved)


def _pool_bwd(name, dmixed, pooled, wg, scale, comm=None):
    t = dmixed.shape[0]
    nt = t // TOK
    hb = TOK // HALO

    def body(dm_ref, dmn_ref, p_ref, wg_ref, sc_ref, du_ref, dwg_ref, dsc_ref, c0, c1, c2, c3):
        i = pl.program_id(0)

        @pl.when(i == 0)
        def _():
            dwg_ref[...] = jnp.zeros_like(dwg_ref)
            dsc_ref[...] = jnp.zeros_like(dsc_ref)

        n = TOK + HALO
        inv = _inv_counts(i)
        dmv = dm_ref[...].astype(F32)
        dmn = dmn_ref[...].astype(F32)
        dmn = jnp.where(i == nt - 1, jnp.zeros_like(dmn), dmn)
        for g in range(4):
            cols = slice(g * POOL_GD, (g + 1) * POOL_GD)
            scg = sc_ref[:, cols]
            pg = p_ref[:, cols]
            dpre = (dmv[:, cols] * scg).astype(BF16)
            dpre_n = (dmn[:, cols] * scg).astype(BF16)
            pre = jnp.dot(pg, wg_ref[g], preferred_element_type=F32)
            dsc_ref[:, cols] += jnp.sum(dmv[:, cols] * pre, axis=0, keepdims=True)
            dwg_ref[g] += lax.dot_general(pg, dpre, (TN, ((), ())), preferred_element_type=F32)
            dpool = lax.dot_general(dpre, wg_ref[g], (NT, ((), ())), preferred_element_type=F32)
            dpool_n = lax.dot_general(dpre_n, wg_ref[g], (NT, ((), ())),
                                      preferred_element_type=F32)
            c0[0:TOK, cols] = dpool
            c0[TOK:n, cols] = dpool_n
            c1[0:TOK, cols] = dpool * inv[g][0:TOK]
            c1[TOK:n, cols] = dpool_n * inv[g][TOK:n]
        c2[0:n - 8, :] = c1[0:n - 8, :] + c1[1:n - 7, :]
        c3[0:n - 16, 128:] = c2[0:n - 16, 128:] + c2[2:n - 14, 128:]
        c1[0:n - 24, 256:] = c3[0:n - 24, 256:] + c3[4:n - 20, 256:]
        wins = [c2[0:TOK, 0:128], c3[0:TOK, 128:256], c1[0:TOK, 256:384],
                c1[0:TOK, 384:512] + c1[8:TOK + 8, 384:512]]
        for g in range(4):
            cols = slice(g * POOL_GD, (g + 1) * POOL_GD)
            du_ref[:, cols] = (wins[g] - c0[0:TOK, cols]).astype(BF16)

    buf = pltpu.VMEM((TOK + HALO, POOL_W), F32)
    outs, moved = _pcall(
        body, name, (nt,),
        [_row_spec(POOL_W),
         pl.BlockSpec((HALO, POOL_W), lambda i: (jnp.minimum((i + 1) * hb, nt * hb - 1), 0)),
         _row_spec(POOL_W), pl.BlockSpec((4, POOL_GD, POOL_GD), lambda i: (0, 0, 0)),
         _vec_spec(POOL_W)],
        [_row_spec(POOL_W), pl.BlockSpec((4, POOL_GD, POOL_GD), lambda i: (0, 0, 0)),
         _vec_spec(POOL_W)],
        [_sds((t, POOL_W), BF16), _sds((4, POOL_GD, POOL_GD), F32), _sds((1, POOL_W), F32)],
        [buf, buf, buf, buf], _cparams("arbitrary"), (dmixed, dmixed, pooled, wg, scale), comm)
    return outs if comm is None else (*outs, moved)


GELU_C = math.sqrt(2.0 / math.pi)


GELU_K = 0.044715


def _gelu_parts(x):
    x2 = x * x
    s = 0.5 + 0.5 * jnp.tanh(x * (GELU_C + (GELU_C * GELU_K) * x2))
    return x * s, s, x2


def _gelu(x):
    return _gelu_parts(x)[0]


def _gelu_and_grad(x):
    g, s, x2 = _gelu_parts(x)
    return g, s + g * (1.0 - s) * ((2 * GELU_C) + (6 * GELU_C * GELU_K) * x2)


def _taps(buf, r, rows):
    a = buf[pl.ds(r, rows + 8), :]
    return a[8:], pltpu.roll(a, 1, axis=0)[8:], pltpu.roll(a, 2, axis=0)[8:]


def _conv(taps, w_ref, b_ref):
    return b_ref[...] + w_ref[2:3, :] * taps[0] + w_ref[1:2, :] * taps[1] + w_ref[0:1, :] * taps[2]


def _stage(dst, prev_ref, cur_ref, next_ref, first, last):
    rows = cur_ref.shape[0]
    h = prev_ref[...].astype(F32)
    dst[0:8, :] = jnp.where(first, jnp.zeros_like(h), h)
    dst[8:8 + rows, :] = cur_ref[...].astype(F32)
    if next_ref is not None:
        h = next_ref[...].astype(F32)
        dst[8 + rows:, :] = jnp.where(last, jnp.zeros_like(h), h)


FWD_STRIP = 32
BWD_STRIP = 16


def _ffn_gate_fwd(name, hu, conv_w, conv_b, comm=None):
    t = hu.shape[0]
    ncol = D_FF // FF_COL
    hb = FF_TOK // 8

    def tile(off):
        return pl.BlockSpec((FF_TOK, FF_COL), lambda i, j: (i, j + off))

    def halo(off):
        return pl.BlockSpec((8, FF_COL), lambda i, j: (jnp.maximum(i * hb - 1, 0), j + off))

    def wspec(off):
        return pl.BlockSpec((3, FF_COL), lambda i, j: (0, j + off))

    def bspec(off):
        return pl.BlockSpec((1, FF_COL), lambda i, j: (0, j + off))

    def body(v_ref, vp_ref, g_ref, gp_ref, wv_ref, wg_ref, bv_ref, bg_ref, a_ref, vb, gb):
        first = pl.program_id(0) == 0
        _stage(vb, vp_ref, v_ref, None, first, None)
        _stage(gb, gp_ref, g_ref, None, first, None)

        def strip(k, carry):
            r = pl.multiple_of(k * FWD_STRIP, FWD_STRIP)
            val = _conv(_taps(vb, r, FWD_STRIP), wv_ref, bv_ref)
            gate = _conv(_taps(gb, r, FWD_STRIP), wg_ref, bg_ref)
            a_ref[pl.ds(r, FWD_STRIP), :] = (_gelu(gate) * val).astype(BF16)
            return carry

        lax.fori_loop(0, FF_TOK // FWD_STRIP, strip, 0)

    buf = pltpu.VMEM((FF_TOK + 8, FF_COL), F32)
    out, moved = _pcall(
        body, name, (t // FF_TOK, ncol),
        [tile(0), halo(0), tile(ncol), halo(ncol), wspec(0), wspec(ncol), bspec(0), bspec(ncol)],
        pl.BlockSpec((FF_TOK, FF_COL), lambda i, j: (i, j)), _sds((t, D_FF), BF16), [buf, buf],
        _cparams("arbitrary", "arbitrary"),
        (hu, hu, hu, hu, conv_w, conv_w, conv_b, conv_b), comm)
    return out if comm is None else (out, moved)


def _ffn_gate_bwd(name, da, hu, conv_w, conv_b, comm=None):
    t = hu.shape[0]
    nt = t // FF_TOK
    ncol = D_FF // FF_COL
    hb = FF_TOK // 8
    ext = FF_TOK + 8

    def tile(off):
        return pl.BlockSpec((FF_TOK, FF_COL), lambda j, i: (i, j + off))

    def prev(off):
        return pl.BlockSpec((8, FF_COL), lambda j, i: (jnp.maximum(i * hb - 1, 0), j + off))

    def nxt(off):
        return pl.BlockSpec((8, FF_COL), lambda j, i: (jnp.minimum((i + 1) * hb, nt * hb - 1), j + off))

    def wspec(off):
        return pl.BlockSpec((3, FF_COL), lambda j, i: (0, j + off))

    def bspec(off):
        return pl.BlockSpec((1, FF_COL), lambda j, i: (0, j + off))

    def body(da_ref, dan_ref, v_ref, vp_ref, vn_ref, g_ref, gp_ref, gn_ref,
             wv_ref, wg_ref, bv_ref, bg_ref, dh_ref, dwv_ref, dwg_ref, vb, gb, dab):
        i = pl.program_id(1)
        first, last = i == 0, i == nt - 1

        @pl.when(first)
        def _():
            dwv_ref[...] = jnp.zeros_like(dwv_ref)
            dwg_ref[...] = jnp.zeros_like(dwg_ref)

        _stage(vb, vp_ref, v_ref, vn_ref, first, last)
        _stage(gb, gp_ref, g_ref, gn_ref, first, last)
        dab[0:FF_TOK, :] = da_ref[...].astype(F32)
        h = dan_ref[...].astype(F32)
        dab[FF_TOK:, :] = jnp.where(last, jnp.zeros_like(h), h)

        def grads(r, rows):
            tv, tg = _taps(vb, r, rows), _taps(gb, r, rows)
            gate = _conv(tg, wg_ref, bg_ref)
            dav = dab[pl.ds(r, rows), :]
            g, dg = _gelu_and_grad(gate)
            dval = dav * g
            dgate = dav * _conv(tv, wv_ref, bv_ref) * dg
            return dval, dgate, tv, tg

        def fold(x):
            return x[0:8] + x[8:16]

        def strip(k, carry):
            r = pl.multiple_of(FF_TOK - BWD_STRIP - k * BWD_STRIP, BWD_STRIP)
            dval, dgate, tv, tg = grads(r, BWD_STRIP)
            new = (dval[0:8], dgate[0:8])
            for half, (d, nxt_rows, taps, w_ref, dw_ref) in enumerate((
                    (dval, carry[0], tv, wv_ref, dwv_ref), (dgate, carry[1], tg, wg_ref, dwg_ref))):
                e = jnp.concatenate([d, nxt_rows], axis=0)
                dh = (w_ref[2:3, :] * d
                      + w_ref[1:2, :] * pltpu.roll(e, BWD_STRIP + 7, axis=0)[0:BWD_STRIP]
                      + w_ref[0:1, :] * pltpu.roll(e, BWD_STRIP + 6, axis=0)[0:BWD_STRIP])
                dh_ref[half, pl.ds(r, BWD_STRIP), :] = dh.astype(BF16)
                dw_ref[0:8, :] += fold(d * taps[2])
                dw_ref[8:16, :] += fold(d * taps[1])
                dw_ref[16:24, :] += fold(d * taps[0])
                dw_ref[24:32, :] += fold(d)
            return new

        dval, dgate, _, _ = grads(FF_TOK, 8)
        lax.fori_loop(0, FF_TOK // BWD_STRIP, strip, (dval, dgate))

        @pl.when(last)
        def _():
            for dw_ref in (dwv_ref, dwg_ref):
                for q in range(4):
                    dw_ref[8 * q:8 * q + 1, :] = jnp.sum(dw_ref[8 * q:8 * q + 8, :], axis=0,
                                                         keepdims=True)

    hbuf = pltpu.VMEM((FF_TOK + 16, FF_COL), F32)
    acc = pl.BlockSpec((32, FF_COL), lambda j, i: (0, j))
    (dhu, dwv, dwg), moved = _pcall(
        body, name, (ncol, nt),
        [tile(0), nxt(0), tile(0), prev(0), nxt(0), tile(ncol), prev(ncol), nxt(ncol),
         wspec(0), wspec(ncol), bspec(0), bspec(ncol)],
        [pl.BlockSpec((2, FF_TOK, FF_COL), lambda j, i: (0, i, j)), acc, acc],
        [_sds((2, t, D_FF), BF16), _sds((32, D_FF), F32), _sds((32, D_FF), F32)],
        [hbuf, hbuf, pltpu.VMEM((ext, FF_COL), F32)], _cparams("arbitrary", "arbitrary"),
        (da, da, hu, hu, hu, hu, hu, hu, conv_w, conv_w, conv_b, conv_b), comm)
    dconv = jnp.concatenate([dwv, dwg], axis=1).reshape(4, 8, 2 * D_FF)[:, 0]
    return (dhu, dconv) if comm is None else (dhu, dconv, moved)


def _mesh_pos():
    x, y, c = lax.axis_index("x"), lax.axis_index("y"), lax.axis_index("c")
    return x, y, c, [(1 - x, y), (x, 1 - y), (1 - x, 1 - y)]


def _any_specs(n):
    return [pl.BlockSpec(memory_space=pl.ANY)] * n


def _remote(src, dst, send_sems, recv_sems, i, dev):
    return pltpu.make_async_remote_copy(src_ref=src, dst_ref=dst, send_sem=send_sems.at[i],
                                        recv_sem=recv_sems.at[i], device_id=dev,
                                        device_id_type=MESH)


def _mine(c, rows):
    return pl.ds(pl.multiple_of(c * (rows // 2), 16), rows // 2)


def _gather_send(shards, conv_shard, gathered, l):
    nbig = len(shards)
    with_conv = conv_shard is not None
    if gathered is None:
        ins = list(shards) + ([conv_shard] if with_conv else [])
        outs = [_sds((DEPTH, N_CHIPS) + s.shape[1:], s.dtype) for s in ins]
        alias = {}
    else:
        ins = list(shards) + list(gathered)
        outs = [_sds(g.shape, g.dtype) for g in gathered]
        alias = {nbig + k: k for k in range(nbig)}

    def copies(cin, cout, ssem, rsem):
        x, y, c, chips = _mesh_pos()
        me = 2 * x + y
        out = []
        for k in range(nbig):
            rows = shards[k].shape[1]
            for j, (cx, cy) in enumerate(chips):
                out.append(_remote(cin[k].at[l, _mine(c, rows)], cout[k].at[l, me, _mine(c, rows)],
                                   ssem, rsem, 4 * k + j, (cx, cy, c)))
            out.append(_remote(cin[k].at[l], cout[k].at[l, me], ssem, rsem, 4 * k + 3,
                               (x, y, 1 - c)))
        if with_conv:
            base = 4 * nbig
            for j, (cx, cy) in enumerate(chips):
                out.append(_remote(cin[nbig].at[c], cout[nbig].at[c, me], ssem, rsem, base + j,
                                   (cx, cy, c)))
            for ll in range(DEPTH):
                out.append(_remote(cin[nbig].at[ll], cout[nbig].at[ll, me], ssem, rsem,
                                   base + 3 + ll, (x, y, 1 - c)))
        return out

    return _Comm(ins, outs, copies, 4 * nbig + 5, alias)


def _gather_forward(gathered, nbig, rows, l):
    with_conv = len(gathered) > nbig
    alias = {k: k for k in range(len(gathered))}

    def copies(cin, cout, ssem, rsem):
        x, y, c, chips = _mesh_pos()
        out = []
        for k in range(nbig):
            for j, (cx, cy) in enumerate(chips):
                blk = cout[k].at[l, 2 * cx + cy, _mine(c, rows[k])]
                out.append(_remote(blk, blk, ssem, rsem, 3 * k + j, (x, y, 1 - c)))
        if with_conv:
            for j, (cx, cy) in enumerate(chips):
                blk = cout[nbig].at[c, 2 * cx + cy]
                out.append(_remote(blk, blk, ssem, rsem, 3 * nbig + j, (x, y, 1 - c)))
        return out

    return _Comm(gathered, [_sds(g.shape, g.dtype) for g in gathered], copies, 3 * nbig + 3, alias)


def _reduce_swap(grads, l):
    def copies(cin, cout, ssem, rsem):
        x, y, c, _ = _mesh_pos()
        return [_remote(cin[k].at[l, :, _mine(1 - c, g.shape[2])], cout[k], ssem, rsem, k,
                        (x, y, 1 - c)) for k, g in enumerate(grads)]

    outs = [_sds((N_CHIPS, g.shape[2] // 2, g.shape[3]), g.dtype) for g in grads]
    return _Comm(grads, outs, copies, len(grads))


def _reduce_scatter(sums):
    def copies(cin, cout, ssem, rsem):
        x, y, c, chips = _mesh_pos()
        return [_remote(cin[k].at[2 * cx + cy], cout[k].at[j], ssem, rsem, 3 * k + j, (cx, cy, c))
                for k in range(len(sums)) for j, (cx, cy) in enumerate(chips)]

    outs = [_sds((3,) + s.shape[1:], s.dtype) for s in sums]
    return _Comm(sums, outs, copies, 3 * len(sums))


def _reduce_share(reds, l):
    def copies(cin, cout, ssem, rsem):
        x, y, c, _ = _mesh_pos()
        out = []
        for k, r in enumerate(reds):
            half = cout[k].at[l, _mine(c, r.shape[1])]
            out.append(_remote(half, half, ssem, rsem, k, (x, y, 1 - c)))
        return out

    return _Comm(reds, [_sds(r.shape, r.dtype) for r in reds], copies, len(reds),
                 {k: k for k in range(len(reds))})


def _allgather_weights(shards):
    n = len(shards)

    def body(*refs):
        ins, outs = refs[:n], refs[n:2 * n]
        send_sems, recv_sems = refs[2 * n:]
        x, y, c, chips = _mesh_pos()
        me = 2 * x + y
        started = []
        own = []
        for k in range(n):
            for l in range(2):
                cp = pltpu.make_async_remote_copy(
                    src_ref=ins[k].at[l], dst_ref=outs[k].at[l, me],
                    send_sem=send_sems.at[k, 6 + l], recv_sem=recv_sems.at[k, 6 + l],
                    device_id=(x, y, 1 - c), device_id_type=MESH)
                cp.start()
                own.append(cp)
            for j, (cx, cy) in enumerate(chips):
                cp = pltpu.make_async_remote_copy(
                    src_ref=ins[k].at[c], dst_ref=outs[k].at[c, me],
                    send_sem=send_sems.at[k, j], recv_sem=recv_sems.at[k, j],
                    device_id=(cx, cy, c), device_id_type=MESH)
                cp.start()
                started.append(cp)
        for k in range(n):
            for j, (cx, cy) in enumerate(chips):
                landed = outs[k].at[c, 2 * cx + cy]
                pltpu.make_async_remote_copy(
                    src_ref=ins[k].at[c], dst_ref=landed,
                    send_sem=send_sems.at[k, j], recv_sem=recv_sems.at[k, j],
                    device_id=(cx, cy, c), device_id_type=MESH).wait_recv()
                fw = pltpu.make_async_remote_copy(
                    src_ref=landed, dst_ref=landed,
                    send_sem=send_sems.at[k, 3 + j], recv_sem=recv_sems.at[k, 3 + j],
                    device_id=(x, y, 1 - c), device_id_type=MESH)
                fw.start()
                started.append(fw)
        for k in range(n):
            for j, (cx, cy) in enumerate(chips):
                theirs = outs[k].at[1 - c, 2 * cx + cy]
                pltpu.make_async_remote_copy(
                    src_ref=theirs, dst_ref=theirs,
                    send_sem=send_sems.at[k, 3 + j], recv_sem=recv_sems.at[k, 3 + j],
                    device_id=(x, y, 1 - c), device_id_type=MESH).wait_recv()
        for cp in started:
            cp.wait_send()
        for cp in own:
            cp.wait()

    return pl.pallas_call(
        body, name="allgather_weights",
        in_specs=_any_specs(n), out_specs=_any_specs(n),
        out_shape=[_sds((2, N_CHIPS) + s.shape[1:], s.dtype) for s in shards],
        scratch_shapes=[pltpu.SemaphoreType.DMA((n, 8)), pltpu.SemaphoreType.DMA((n, 8))],
    )(*shards)


def _swap_layers(grads):
    n = len(grads)

    def body(*refs):
        ins, outs = refs[:n], refs[n:2 * n]
        send_sems, recv_sems = refs[2 * n:]
        x, y, c, _ = _mesh_pos()
        cps = []
        for k in range(n):
            cp = pltpu.make_async_remote_copy(
                src_ref=ins[k].at[1 - c], dst_ref=outs[k],
                send_sem=send_sems.at[k], recv_sem=recv_sems.at[k],
                device_id=(x, y, 1 - c), device_id_type=MESH)
            cp.start()
            cps.append(cp)
        for cp in cps:
            cp.wait()

    return pl.pallas_call(
        body, name="swap_layers",
        in_specs=_any_specs(n), out_specs=_any_specs(n),
        out_shape=[_sds(g.shape[1:], g.dtype) for g in grads],
        scratch_shapes=[pltpu.SemaphoreType.DMA((n,)), pltpu.SemaphoreType.DMA((n,))],
    )(*grads)


def _scatter_blocks(sums):
    n = len(sums)

    def body(*refs):
        ins, outs = refs[:n], refs[n:2 * n]
        send_sems, recv_sems = refs[2 * n:]
        x, y, c, chips = _mesh_pos()
        cps = []
        for k in range(n):
            for j, (cx, cy) in enumerate(chips):
                cp = pltpu.make_async_remote_copy(
                    src_ref=ins[k].at[2 * cx + cy], dst_ref=outs[k].at[j],
                    send_sem=send_sems.at[k, j], recv_sem=recv_sems.at[k, j],
                    device_id=(cx, cy, c), device_id_type=MESH)
                cp.start()
                cps.append(cp)
        for cp in cps:
            cp.wait()

    return pl.pallas_call(
        body, name="scatter_blocks",
        in_specs=_any_specs(n), out_specs=_any_specs(n),
        out_shape=[_sds((3,) + s.shape[1:], s.dtype) for s in sums],
        scratch_shapes=[pltpu.SemaphoreType.DMA((n, 3)), pltpu.SemaphoreType.DMA((n, 3))],
    )(*sums)


def _exchange_reduced(reds):
    n = len(reds)

    def body(*refs):
        outs = refs[n:2 * n]
        send_sems, recv_sems = refs[2 * n:]
        x, y, c, _ = _mesh_pos()
        cps = []
        for k in range(n):
            cp = pltpu.make_async_remote_copy(
                src_ref=outs[k].at[c], dst_ref=outs[k].at[c],
                send_sem=send_sems.at[k], recv_sem=recv_sems.at[k],
                device_id=(x, y, 1 - c), device_id_type=MESH)
            cp.start()
            cps.append(cp)
        for k in range(n):
            pltpu.make_async_remote_copy(
                src_ref=outs[k].at[c], dst_ref=outs[k].at[1 - c],
                send_sem=send_sems.at[k], recv_sem=recv_sems.at[k],
                device_id=(x, y, 1 - c), device_id_type=MESH).wait_recv()
        for cp in cps:
            cp.wait_send()

    return pl.pallas_call(
        body, name="exchange_reduced",
        in_specs=_any_specs(n), out_specs=_any_specs(n),
        out_shape=[_sds(r.shape, r.dtype) for r in reds],
        input_output_aliases={k: k for k in range(n)},
        scratch_shapes=[pltpu.SemaphoreType.DMA((n,)), pltpu.SemaphoreType.DMA((n,))],
    )(*reds)


def _allreduce_small(pack):
    n = pack.shape[0]

    def body(x_ref, o_ref, gbuf, send_sems, recv_sems):
        x, y, c, chips = _mesh_pos()
        sibling = (x, y, 1 - c)

        def slot(px, py, pc):
            return gbuf.at[4 * px + 2 * py + pc]

        def copy(k, block, to, src=None):
            return pltpu.make_async_remote_copy(
                src_ref=slot(*block) if src is None else src, dst_ref=slot(*block),
                send_sem=send_sems.at[k], recv_sem=recv_sems.at[k],
                device_id=to, device_id_type=MESH)

        me = (x, y, c)
        first = [copy(0, me, sibling, src=x_ref)]
        first += [copy(1 + j, me, (*chip, c), src=x_ref) for j, chip in enumerate(chips)]
        for cp in first:
            cp.start()
        gbuf[4 * x + 2 * y + c] = x_ref[...]
        passed = [copy(4 + j, (*chip, c), sibling) for j, chip in enumerate(chips)]
        for j, chip in enumerate(chips):
            copy(1 + j, (*chip, c), me).wait_recv()
            passed[j].start()
        copy(0, sibling, me).wait_recv()
        for j, chip in enumerate(chips):
            copy(4 + j, (*chip, 1 - c), me).wait_recv()
        for cp in first + passed:
            cp.wait_send()
        acc = gbuf[0]
        for d in range(1, 8):
            acc = acc + gbuf[d]
        o_ref[...] = acc

    return pl.pallas_call(
        body, name="allreduce_small",
        in_specs=[pl.BlockSpec(memory_space=pltpu.VMEM)],
        out_specs=pl.BlockSpec(memory_space=pltpu.VMEM),
        out_shape=_sds((n, 128), F32),
        scratch_shapes=[pltpu.VMEM((8, n, 128), F32), pltpu.SemaphoreType.DMA((7,)),
                        pltpu.SemaphoreType.DMA((7,))],
        compiler_params=pltpu.CompilerParams(vmem_limit_bytes=VMEM_LIMIT_V7X),
    )(pack)


def _core_index():
    return jnp.reshape(lax.axis_index("c"), (1,)).astype(jnp.int32)


def _chip_index():
    return jnp.reshape(2 * lax.axis_index("x") + lax.axis_index("y"), (1,)).astype(jnp.int32)


def _chip_sum(name, stacked, sib, l):
    _, nb, r, cdim = stacked.shape
    hr = r // 2

    def body(c_ref, a_ref, b_ref, o_ref):
        o_ref[...] = (a_ref[...].astype(F32) + b_ref[...].astype(F32)).astype(BF16)

    return pl.pallas_call(
        body, name=name,
        grid_spec=pltpu.PrefetchScalarGridSpec(
            num_scalar_prefetch=1, grid=(nb,),
            in_specs=[pl.BlockSpec((None, None, hr, cdim), lambda j, cr: (l, j, cr[0], 0)),
                      pl.BlockSpec((None, hr, cdim), lambda j, cr: (j, 0, 0))],
            out_specs=pl.BlockSpec((None, hr, cdim), lambda j, cr: (j, 0, 0))),
        out_shape=_sds((nb, hr, cdim), BF16),
        compiler_params=_cparams("parallel"))(_core_index(), stacked, sib)


def _final_sum(name, sums, recv, l, fill):
    _, hr, cdim = sums.shape
    tr = hr // 2

    def body(m_ref, a_ref, b_ref, *rest):
        acc = a_ref[...].astype(F32)
        for j in range(3):
            acc = acc + b_ref[j].astype(F32)
        rest[-1][...] = acc

    in_specs = [pl.BlockSpec((None, tr, cdim), lambda i, mr: (mr[0], i, 0)),
                pl.BlockSpec((3, tr, cdim), lambda i, mr: (0, i, 0))]
    args = [jnp.concatenate([_chip_index(), _core_index()]), sums, recv]
    aliases = {}
    if fill is not None:
        in_specs.append(pl.BlockSpec(memory_space=pl.ANY))
        args.append(fill)
        aliases = {3: 0}
    return pl.pallas_call(
        body, name=name,
        grid_spec=pltpu.PrefetchScalarGridSpec(
            num_scalar_prefetch=1, grid=(2,), in_specs=in_specs,
            out_specs=pl.BlockSpec((None, tr, cdim), lambda i, mr: (l, 2 * mr[1] + i, 0))),
        out_shape=_sds((DEPTH, 2 * hr, cdim), F32), input_output_aliases=aliases,
        compiler_params=_cparams("parallel"))(*args)


def _adamw(name, w, g, m, v):
    nl, r, cdim = w.shape
    tr = r // 4 if r % 32 == 0 else r
    c1 = 1.0 - ADAM_B1 ** ADAM_STEP
    c2 = 1.0 - ADAM_B2 ** ADAM_STEP

    def body(w_ref, g_ref, m_ref, v_ref, d_ref, nm_ref, nv_ref):
        gv = g_ref[...]
        nm = ADAM_B1 * m_ref[...] + (1.0 - ADAM_B1) * gv
        nv = ADAM_B2 * v_ref[...] + (1.0 - ADAM_B2) * (gv * gv)
        nm_ref[...] = nm
        nv_ref[...] = nv
        d_ref[...] = -ADAM_LR * ((nm / c1) / (jnp.sqrt(nv / c2) + ADAM_EPS) + ADAM_WD * w_ref[...])

    spec = pl.BlockSpec((None, tr, cdim), lambda l, i: (l, i, 0))
    out = _sds(w.shape, F32)
    return pl.pallas_call(
        body, name=name, grid=(nl, r // tr),
        in_specs=[spec] * 4, out_specs=[spec] * 3, out_shape=[out] * 3,
        compiler_params=_cparams("parallel", "parallel"))(w, g, m, v)


def _rows128(a):
    return a.reshape(-1, 128)


def kernel(x, norm_mix_pre, w_in, b_gate, rel_bias, w_attn_out, w_pool_group, pool_scale, w_pool_out, w_o, norm_mix_post, norm_ffn_pre, w_up, conv_w, conv_b, w_down, norm_ffn_post, loss_target, m_norm_mix_pre, m_w_in, m_b_gate, m_rel_bias, m_w_attn_out, m_w_pool_group, m_pool_scale, m_w_pool_out, m_w_o, m_norm_mix_post, m_norm_ffn_pre, m_w_up, m_conv_w, m_conv_b, m_w_down, m_norm_ffn_post, v_norm_mix_pre, v_w_in, v_b_gate, v_rel_bias, v_w_attn_out, v_w_pool_group, v_pool_scale, v_w_pool_out, v_w_o, v_norm_mix_post, v_norm_ffn_pre, v_w_up, v_conv_w, v_conv_b, v_w_down, v_norm_ffn_post):
    t = x.shape[1]
    xs = x.reshape(t, D_MODEL)
    target = loss_target.reshape(t, D_MODEL)

    names = ["w_in", "w_attn_out", "w_pool_out", "w_o", "w_up", "w_down"]
    shards = [w.astype(BF16) for w in (w_in, w_attn_out, w_pool_out, w_o, w_up, w_down)]
    rows = [s.shape[1] for s in shards]
    nbig = len(shards)
    g = _comm_call("gather0_send", _gather_send(shards[:1], conv_w, None, 0))
    g = _comm_call("gather0_forward", _gather_forward(g, 1, rows[:1], 0))
    cw_full = jnp.transpose(g[1], (0, 2, 1, 3)).reshape(DEPTH, 3, 2 * D_FF)
    g = g[:1]
    wg_bf = w_pool_group.astype(BF16)

    def views(gathered):
        win_g, wao_g, wpo_g, wo_g, wup_g, wdn_g = gathered
        return (win_g, wao_g, wpo_g, wo_g.reshape(DEPTH, D_MODEL, D_MODEL), wup_g,
                wdn_g.reshape(DEPTH, D_FF, D_MODEL))

    saved = []
    xcur = xs
    h = _norm_fwd("l0_norm_mix_pre", xs, norm_mix_pre[0:1])
    for l in range(DEPTH):
        tag = f"l{l}_"
        bias = _bias_table(tag + "bias_table", rel_bias[l])
        proj = _mm_nn_blocked(tag + "proj", h, g[0], l, BF16)
        if l == 0:
            att, rest = _attn_fwd(tag + "attn_fwd", proj, bias,
                                  _gather_send(shards[1:], None, None, 0))
            pooled, mixed, rest = _pool_fwd(tag + "pool_fwd", proj, wg_bf[l], pool_scale[l:l + 1],
                                            _gather_forward(rest, nbig - 1, rows[1:], 0))
            g = g + rest
        else:
            att = _attn_fwd(tag + "attn_fwd", proj, bias)
            pooled, mixed = _pool_fwd(tag + "pool_fwd", proj, wg_bf[l], pool_scale[l:l + 1])
        win_g, wao_g, wpo_g, wo_full, wup_g, wdn_full = views(g)
        ya = _narrow_nn(tag + "attn_out", att, wao_g, l)
        yb = _narrow_nn(tag + "pool_out", mixed, wpo_g, l)
        z = _gate_fwd(tag + "gate_fwd", proj, b_gate[l:l + 1], ya, yb)
        mix = _mm_nn(tag + "mix", z, wo_full, l, D_MODEL, F32)
        x1, h2 = _post_pre_fwd(tag + "norm_mix_post", xcur, mix, norm_mix_post[l:l + 1],
                               norm_ffn_pre[l:l + 1])
        hu = _mm_nn_blocked(tag + "ffn_up", h2, wup_g, l, BF16)
        if l == 0:
            a, g = _ffn_gate_fwd(tag + "ffn_gate_fwd", hu, cw_full[l], conv_b[l:l + 1],
                                 _gather_send(shards, None, g, 1))
            wdn_full = views(g)[5]
        else:
            a = _ffn_gate_fwd(tag + "ffn_gate_fwd", hu, cw_full[l], conv_b[l:l + 1])
        f = _mm_nn(tag + "ffn_down", a, wdn_full, l, D_FF // 2, F32)
        saved.append(dict(x=xcur, h=h, proj=proj, att=att, pooled=pooled, mixed=mixed, ya=ya,
                          yb=yb, z=z, mix=mix, x1=x1, h2=h2, hu=hu, a=a, f=f, bias=bias))
        if l == 0:
            xcur, h, g = _post_pre_fwd(tag + "norm_ffn_post", x1, f, norm_ffn_post[l:l + 1],
                                       norm_mix_pre[l + 1:l + 2], _gather_forward(g, nbig, rows, 1))
        elif l < DEPTH - 1:
            xcur, h = _post_pre_fwd(tag + "norm_ffn_post", x1, f, norm_ffn_post[l:l + 1],
                                    norm_mix_pre[l + 1:l + 2])
    win_g, wao_g, wpo_g, wo_full, wup_g, wdn_full = views(g)

    dy, df, d_nfpost, loss_local = _tail("tail", saved[-1]["x1"], saved[-1]["f"],
                                         norm_ffn_post[DEPTH - 1:DEPTH], target)
    loss = lax.psum(loss_local, ("x", "y", "c"))

    dx = dy
    dws = dict.fromkeys(names)
    reds = [None] * nbig
    small_grads = [None] * DEPTH
    ffn = [4, 5]
    outs3 = [1, 2, 3]

    def blocks(ks):
        return [dws[names[k]].reshape(DEPTH, N_CHIPS, rows[k], -1) for k in ks]

    def chip_sums(ks, sib, l):
        return [_chip_sum(f"chip_sum{l}_" + names[k], b, s, l)
                for k, b, s in zip(ks, blocks(ks), sib)]

    def final_sums(ks, sums, recv, l):
        for k, s, r in zip(ks, sums, recv):
            reds[k] = _final_sum(f"final_sum{l}_" + names[k], s, r, l, reds[k])

    for l in reversed(range(DEPTH)):
        tag = f"l{l}_"
        sv = saved[l]
        every = list(range(nbig))
        if l == 0:
            da, sib = _mm_nt(tag + "ffn_down_dx", df, wdn_full, l, D_FF // 2, BF16,
                             _reduce_swap(blocks(every), 1))
            sums = chip_sums(every, sib, 1)
        else:
            da = _mm_nt(tag + "ffn_down_dx", df, wdn_full, l, D_FF // 2, BF16)
        dws["w_down"] = _mm_tn(tag + "ffn_down_dw", sv["a"], df, D_FF // 2, l, dws["w_down"])
        if l == 0:
            dhu, dconv, recv = _ffn_gate_bwd(tag + "ffn_gate_bwd", da, sv["hu"], cw_full[l],
                                             conv_b[l:l + 1], _reduce_scatter(sums))
            final_sums(every, sums, recv, 1)
            dh2, reds = _mm_nt_blocked(tag + "ffn_up_dx", dhu, wup_g, l, F32,
                                       _reduce_share(reds, 1))
        else:
            dhu, dconv = _ffn_gate_bwd(tag + "ffn_gate_bwd", da, sv["hu"], cw_full[l],
                                       conv_b[l:l + 1])
            dh2 = _mm_nt_blocked(tag + "ffn_up_dx", dhu, wup_g, l, F32)
        dws["w_up"] = _mm_tn_blocked(tag + "ffn_up_dw", sv["h2"], dhu, l, dws["w_up"])
        if l == 0:
            dx1, d_nfpre, dmix, d_nmpost, sib = _pre_post_bwd(
                tag + "norm_ffn_pre_bwd", dh2, sv["x1"], dx, norm_ffn_pre[l:l + 1], sv["mix"],
                norm_mix_post[l:l + 1], _reduce_swap(blocks(ffn), 0))
            sums = chip_sums(ffn, sib, 0)
        else:
            dx1, d_nfpre, dmix, d_nmpost = _pre_post_bwd(
                tag + "norm_ffn_pre_bwd", dh2, sv["x1"], dx, norm_ffn_pre[l:l + 1], sv["mix"],
                norm_mix_post[l:l + 1])
        dz = _mm_nt(tag + "mix_dx", dmix, wo_full, l, D_MODEL, BF16)
        dws["w_o"] = _mm_tn(tag + "mix_dw", sv["z"], dmix, D_MODEL, l, dws["w_o"])
        dya, dyb, dgates, d_bgate = _gate_bwd(tag + "gate_bwd", dz, sv["proj"], b_gate[l:l + 1],
                                              sv["ya"], sv["yb"])
        datt = _narrow_nt(tag + "attn_out_dx", dya, wao_g, l)
        dws["w_attn_out"] = _narrow_tn(tag + "attn_out_dw", sv["att"], dya, l, dws["w_attn_out"])
        dmixed = _narrow_nt(tag + "pool_out_dx", dyb, wpo_g, l)
        dws["w_pool_out"] = _narrow_tn(tag + "pool_out_dw", sv["mixed"], dyb, l, dws["w_pool_out"])
        if l == 0:
            du, d_wg, d_pscale, sib = _pool_bwd(tag + "pool_bwd", dmixed, sv["pooled"], wg_bf[l],
                                                pool_scale[l:l + 1], _reduce_swap(blocks(outs3), 0))
            sums3 = chip_sums(outs3, sib, 0)
            dqkv, dbias, recv = _attn_bwd(
                tag + "attn_bwd", sv["proj"], datt, sv["bias"],
                _both(_reduce_scatter(sums), _reduce_scatter(sums3)))
            final_sums(ffn, sums, recv[:len(ffn)], 0)
            final_sums(outs3, sums3, recv[len(ffn):], 0)
        else:
            du, d_wg, d_pscale = _pool_bwd(tag + "pool_bwd", dmixed, sv["pooled"], wg_bf[l],
                                           pool_scale[l:l + 1])
            dqkv, dbias = _attn_bwd(tag + "attn_bwd", sv["proj"], datt, sv["bias"])
        d_rel = _bias_fold(tag + "bias_fold", dbias)
        if l == 0:
            dh, shared = _proj_dx(tag + "proj_dx", dqkv, du, dgates, win_g, l,
                                  _reduce_share([reds[k] for k in ffn + outs3], 0))
            for k, r in zip(ffn + outs3, shared):
                reds[k] = r
        else:
            dh = _proj_dx(tag + "proj_dx", dqkv, du, dgates, win_g, l)
        dws["w_in"] = _proj_dw(tag + "proj_dw", sv["h"], dqkv, du, dgates, l, dws["w_in"])
        small_grads[l] = [None, d_nmpost, d_nfpre, d_nfpost, d_bgate, d_rel, d_wg, d_pscale,
                          dconv[3:4], dconv[0:3]]
        if l > 0:
            dx, small_grads[l][0], df, d_nfpost = _pre_post_bwd(
                tag + "norm_mix_pre_bwd", dh, sv["x"], dx1, norm_mix_pre[l:l + 1],
                saved[l - 1]["f"], norm_ffn_post[l - 1:l])
        else:
            dx, small_grads[l][0] = _norm_pre_bwd(tag + "norm_mix_pre_bwd", dh, sv["x"], dx1,
                                                  norm_mix_pre[l:l + 1])

    grad_x = dx.reshape(x.shape)

    sib = _comm_call("reduce_swap", _reduce_swap(blocks([0]), 0))
    sums = chip_sums([0], sib, 0)
    recv = _comm_call("reduce_scatter", _reduce_scatter(sums))
    final_sums([0], sums, recv, 0)
    g_big = _comm_call("reduce_share", _reduce_share([reds[0]], 0)) + reds[1:]

    pieces = []
    for idx in range(10):
        pieces.append(jnp.stack([small_grads[0][idx], small_grads[1][idx]]))
    pack = jnp.concatenate([_rows128(p) for p in pieces], axis=0)
    red = _allreduce_small(pack)
    shapes = [p.shape for p in pieces]
    outs = []
    row = 0
    for shp in shapes:
        nrow = math.prod(shp) // 128
        outs.append(red[row:row + nrow].reshape(shp))
        row += nrow
    (g_nmpre, g_nmpost, g_nfpre, g_nfpost, g_bgate, g_rel, g_wg, g_pscale, g_cb, g_cw) = outs
    g_nmpre, g_nmpost, g_nfpre, g_nfpost = [a.reshape(DEPTH, D_MODEL)
                                            for a in (g_nmpre, g_nmpost, g_nfpre, g_nfpost)]
    g_bgate = g_bgate.reshape(DEPTH, 2 * D_MODEL)
    g_rel = g_rel[:, :, :N_REL]
    g_pscale = g_pscale.reshape(DEPTH, POOL_W)
    g_cb = g_cb.reshape(DEPTH, 2 * D_FF)
    ncw = conv_w.shape[2]
    chip = 2 * lax.axis_index("x") + lax.axis_index("y")
    g_cw = lax.dynamic_slice_in_dim(g_cw, chip * ncw, ncw, axis=2)

    grads = dict(norm_mix_pre=g_nmpre, w_in=g_big[0], b_gate=g_bgate, rel_bias=g_rel,
                 w_attn_out=g_big[1], w_pool_group=g_wg, pool_scale=g_pscale, w_pool_out=g_big[2],
                 w_o=g_big[3], norm_mix_post=g_nmpost, norm_ffn_pre=g_nfpre, w_up=g_big[4],
                 conv_w=g_cw, conv_b=g_cb, w_down=g_big[5], norm_ffn_post=g_nfpost)
    weights = dict(norm_mix_pre=norm_mix_pre, w_in=w_in, b_gate=b_gate, rel_bias=rel_bias,
                   w_attn_out=w_attn_out, w_pool_group=w_pool_group, pool_scale=pool_scale,
                   w_pool_out=w_pool_out, w_o=w_o, norm_mix_post=norm_mix_post,
                   norm_ffn_pre=norm_ffn_pre, w_up=w_up, conv_w=conv_w, conv_b=conv_b,
                   w_down=w_down, norm_ffn_post=norm_ffn_post)
    moms = dict(norm_mix_pre=(m_norm_mix_pre, v_norm_mix_pre), w_in=(m_w_in, v_w_in),
                b_gate=(m_b_gate, v_b_gate), rel_bias=(m_rel_bias, v_rel_bias),
                w_attn_out=(m_w_attn_out, v_w_attn_out),
                w_pool_group=(m_w_pool_group, v_w_pool_group),
                pool_scale=(m_pool_scale, v_pool_scale), w_pool_out=(m_w_pool_out, v_w_pool_out),
                w_o=(m_w_o, v_w_o), norm_mix_post=(m_norm_mix_post, v_norm_mix_post),
                norm_ffn_pre=(m_norm_ffn_pre, v_norm_ffn_pre), w_up=(m_w_up, v_w_up),
                conv_w=(m_conv_w, v_conv_w), conv_b=(m_conv_b, v_conv_b),
                w_down=(m_w_down, v_w_down), norm_ffn_post=(m_norm_ffn_post, v_norm_ffn_post))
    order = list(weights.keys())

    delta, new_m, new_v = {}, {}, {}
    small_names = [nm for nm in order if nm not in names]
    for nm in names:
        delta[nm], new_m[nm], new_v[nm] = _adamw("adamw_" + nm, weights[nm], grads[nm], *moms[nm])

    def pack_small(get):
        flat = [get(nm).reshape(-1) for nm in small_names]
        total = sum(f.shape[0] for f in flat)
        padded = -(-total // 1024) * 1024
        flat.append(jnp.zeros((padded - total,), F32))
        return jnp.concatenate(flat).reshape(1, padded // 128, 128)

    d_s, m_s, v_s = _adamw(
        "adamw_small", pack_small(lambda nm: weights[nm]), pack_small(lambda nm: grads[nm]),
        pack_small(lambda nm: moms[nm][0]) , pack_small(lambda nm: moms[nm][1]))
    off = 0
    for nm in small_names:
        size = math.prod(weights[nm].shape)
        for dst, src in ((delta, d_s), (new_m, m_s), (new_v, v_s)):
            dst[nm] = src.reshape(-1)[off:off + size].reshape(weights[nm].shape)
        off += size

    return (loss, grad_x, *[grads[nm] for nm in order], *[delta[nm] for nm in order],
            *[new_m[nm] for nm in order], *[new_v[nm] for nm in order])
```

```python
import functools
import math

import jax
import jax.numpy as jnp
from jax import lax
from jax.experimental import pallas as pl
from jax.experimental.pallas import tpu as pltpu

F32 = jnp.float32
BF16 = jnp.bfloat16
MESH = pl.DeviceIdType.MESH

D_MODEL = 1024
DEPTH = 2
CHUNK = 64
BAND_CHUNKS = 9
BAND = BAND_CHUNKS * CHUNK
HEADS = 8
HEAD_DIM = 64
ATTN_W = HEADS * HEAD_DIM
POOL_WINDOWS = (2, 4, 8, 16)
POOL_W = 512
POOL_GD = 128
MAX_REL = 256
N_REL = 2 * MAX_REL + 1
D_FF = 2816
IN_W = 3 * ATTN_W + POOL_W + 2 * D_MODEL
EPS = 1e-6
ATTN_SCALE = HEAD_DIM ** -0.5
BAND_PAD = 640
BIAS_LANES = BAND_PAD
N_CHIPS = 4

ADAM_LR = 0.001
ADAM_B1 = 0.9
ADAM_B2 = 0.999
ADAM_EPS = 1e-08
ADAM_WD = 0.01
ADAM_STEP = 10

VMEM_LIMIT_V7X = 56 * 1024 * 1024
TOK = 512
ATT_BLK = 8 * CHUNK
FF_COL = 256
FF_TOK = 1024
HALO = 32


def _cparams(*sem):
    return pltpu.CompilerParams(dimension_semantics=sem, vmem_limit_bytes=VMEM_LIMIT_V7X)


def _sds(shape, dtype):
    return jax.ShapeDtypeStruct(shape, dtype)


class _Comm:
    def __init__(self, ins, outs, copies, n_sems, alias=None):
        self.ins, self.outs, self.copies, self.n_sems = list(ins), list(outs), copies, n_sems
        self.alias = dict(alias or {})


class _SemsFrom:
    def __init__(self, sems, start):
        self.sems, self.start = sems, start

    @property
    def at(self):
        return self

    def __getitem__(self, i):
        return self.sems.at[self.start + i]


def _both(a, b):
    na, nao = len(a.ins), len(a.outs)

    def copies(cin, cout, ssem, rsem):
        return (a.copies(cin[:na], cout[:nao], ssem, rsem)
                + b.copies(cin[na:], cout[nao:], _SemsFrom(ssem, a.n_sems), _SemsFrom(rsem, a.n_sems)))

    alias = dict(a.alias)
    alias.update({na + i: nao + o for i, o in b.alias.items()})
    return _Comm(a.ins + b.ins, a.outs + b.outs, copies, a.n_sems + b.n_sems, alias)


def _pcall(body, name, grid, in_specs, out_specs, out_shape, scratch_shapes, compiler_params, args,
           comm=None, aliases=None):
    single = not isinstance(out_shape, (list, tuple))
    out_specs = [out_specs] if single else list(out_specs)
    out_shape = [out_shape] if single else list(out_shape)
    n_in, n_out = len(in_specs), len(out_specs)
    aliases = dict(aliases or {})
    if comm is None:
        res = pl.pallas_call(
            body, name=name, grid=grid, in_specs=list(in_specs), out_specs=out_specs,
            out_shape=out_shape, scratch_shapes=list(scratch_shapes),
            input_output_aliases=aliases, compiler_params=compiler_params)(*args)
        return (res[0] if single else res), None
    ci, co = len(comm.ins), len(comm.outs)

    def hosted(*refs):
        main_in, cin = refs[:n_in], refs[n_in:n_in + ci]
        main_out = refs[n_in + ci:n_in + ci + n_out]
        cout = refs[n_in + ci + n_out:n_in + ci + n_out + co]
        rest = refs[n_in + ci + n_out + co:]
        copies = comm.copies(cin, cout, rest[-2], rest[-1])
        ids = [pl.program_id(a) for a in range(len(grid))]
        first = functools.reduce(jnp.logical_and, [i == 0 for i in ids])
        last = functools.reduce(jnp.logical_and, [i == g - 1 for i, g in zip(ids, grid)])

        @pl.when(first)
        def _():
            for cp in copies:
                cp.start()

        body(*main_in, *main_out, *rest[:-2])

        @pl.when(last)
        def _():
            for cp in copies:
                cp.wait()

    for i, o in comm.alias.items():
        aliases[n_in + i] = n_out + o
    hbm = pl.BlockSpec(memory_space=pl.ANY)
    sems = pltpu.SemaphoreType.DMA((comm.n_sems,))
    res = pl.pallas_call(
        hosted, name=name, grid=grid, in_specs=list(in_specs) + [hbm] * ci,
        out_specs=out_specs + [hbm] * co, out_shape=out_shape + comm.outs,
        scratch_shapes=list(scratch_shapes) + [sems, sems],
        input_output_aliases=aliases, compiler_params=compiler_params)(*args, *comm.ins)
    return (res[0] if single else list(res[:n_out])), list(res[n_out:])


def _comm_call(name, comm):
    ci = len(comm.ins)

    def body(*refs):
        copies = comm.copies(refs[:ci], refs[ci:-2], refs[-2], refs[-1])
        for cp in copies:
            cp.start()
        for cp in copies:
            cp.wait()

    hbm = pl.BlockSpec(memory_space=pl.ANY)
    sems = pltpu.SemaphoreType.DMA((comm.n_sems,))
    return list(pl.pallas_call(
        body, name=name, in_specs=[hbm] * ci, out_specs=[hbm] * len(comm.outs),
        out_shape=comm.outs, scratch_shapes=[sems, sems],
        input_output_aliases=comm.alias)(*comm.ins))


def _matmul(name, a, b, a_spec, b_spec, o_spec, out_shape, grid, contract, nk, acc_shape,
            fill=None, comm=None):
    def body(*refs):
        a_ref, b_ref = refs[0], refs[1]
        o_ref = refs[2 if fill is None else 3]
        scratch = refs[(3 if fill is None else 4):]
        part = lax.dot_general(a_ref[...], b_ref[...], (contract, ((), ())),
                               preferred_element_type=F32)
        if nk == 1:
            o_ref[...] = part.astype(o_ref.dtype)
        else:
            acc_ref = scratch[0]
            k = pl.program_id(2)

            @pl.when(k == 0)
            def _():
                acc_ref[...] = part

            @pl.when(k > 0)
            def _():
                acc_ref[...] += part

            @pl.when(k == nk - 1)
            def _():
                o_ref[...] = acc_ref[...].astype(o_ref.dtype)

    scratch = [] if nk == 1 else [pltpu.VMEM(acc_shape, F32)]
    in_specs, args, aliases = [a_spec, b_spec], [a, b], {}
    if fill is not None:
        in_specs.append(pl.BlockSpec(memory_space=pl.ANY))
        args.append(fill)
        aliases = {2: 0}
    out, moved = _pcall(body, name, grid, in_specs, o_spec, out_shape, scratch,
                        _cparams("parallel", "parallel", "arbitrary"), args, comm, aliases)
    return out if comm is None else (out, moved)


NN = ((1,), (0,))
NT = ((1,), (1,))
TN = ((0,), (0,))


def _tm(t):
    return min(t, 1024)


def _col_block_spec(a, rows, nb, row_col):
    if a.ndim == 2:
        return pl.BlockSpec((rows, nb), row_col)

    def halves(*ids):
        r, c = row_col(*ids)
        return c // 2, r, c % 2

    return pl.BlockSpec((None, rows, nb), halves)


def _mm_nn_blocked(name, a, w, l, out_dtype):
    t, k = a.shape
    nb = w.shape[3]
    tm = _tm(t)
    return _matmul(
        name, a, w,
        pl.BlockSpec((tm, k), lambda i, n, kk: (i, 0)),
        pl.BlockSpec((None, None, k, nb), lambda i, n, kk: (l, n, 0, 0)),
        pl.BlockSpec((tm, nb), lambda i, n, kk: (i, n)),
        _sds((t, N_CHIPS * nb), out_dtype), (t // tm, N_CHIPS, 1), NN, 1, None)


def _mm_nt_blocked(name, a, w, l, out_dtype, comm=None):
    t = a.shape[-2]
    k, nb = w.shape[2], w.shape[3]
    tm = _tm(t)
    return _matmul(
        name, a, w,
        _col_block_spec(a, tm, nb, lambda i, n, kk: (i, kk)),
        pl.BlockSpec((None, None, k, nb), lambda i, n, kk: (l, kk, 0, 0)),
        pl.BlockSpec((tm, k), lambda i, n, kk: (i, 0)),
        _sds((t, k), out_dtype), (t // tm, 1, N_CHIPS), NT, N_CHIPS, (tm, k), comm=comm)


def _mm_tn_blocked(name, a, g, l, fill):
    t, k = a.shape
    nb = g.shape[-1] * (g.ndim - 1) // N_CHIPS
    tt = _tm(t)
    nt = t // tt
    return _matmul(
        name, a, g,
        pl.BlockSpec((tt, k), lambda n, j, kk: (kk, 0)),
        _col_block_spec(g, tt, nb, lambda n, j, kk: (kk, n)),
        pl.BlockSpec((None, None, k, nb), lambda n, j, kk: (l, n, 0, 0)),
        _sds((DEPTH, N_CHIPS, k, nb), BF16), (N_CHIPS, 1, nt), TN, nt, (k, nb), fill)


def _proj_pieces(rows, dqkv_first):
    def piece(col):
        if dqkv_first:
            return pl.BlockSpec((rows, ATTN_W), lambda i, kk: (i, col))
        return pl.BlockSpec((rows, ATTN_W), lambda n, kk: (kk, col))
    return [piece(0), piece(1), piece(2), piece(0)]


def _proj_dx(name, dqkv, du, dgates, w, l, comm=None):
    t = du.shape[0]
    k, nb = w.shape[2], w.shape[3]
    tm = _tm(t)

    def body(dq_ref, dk_ref, dv_ref, du_ref, dg_ref, w_ref, o_ref, acc_ref):
        kk = pl.program_id(1)

        def mm(a):
            return lax.dot_general(a, w_ref[...], (NT, ((), ())), preferred_element_type=F32)

        @pl.when(kk == 0)
        def _():
            acc_ref[...] = mm(jnp.concatenate([dq_ref[...], dk_ref[...]], axis=1))

        @pl.when(kk == 1)
        def _():
            acc_ref[...] += mm(jnp.concatenate([dv_ref[...], du_ref[...]], axis=1))

        @pl.when(kk >= 2)
        def _():
            acc_ref[...] += mm(dg_ref[...])

        @pl.when(kk == N_CHIPS - 1)
        def _():
            o_ref[...] = acc_ref[...]

    out, moved = _pcall(
        body, name, (t // tm, N_CHIPS),
        _proj_pieces(tm, True)
        + [pl.BlockSpec((tm, nb), lambda i, kk: (i, jnp.maximum(kk - 2, 0))),
           pl.BlockSpec((None, None, k, nb), lambda i, kk: (l, kk, 0, 0))],
        pl.BlockSpec((tm, k), lambda i, kk: (i, 0)), _sds((t, k), F32),
        [pltpu.VMEM((tm, k), F32)], _cparams("arbitrary", "arbitrary"),
        (dqkv, dqkv, dqkv, du, dgates, w), comm)
    return out if comm is None else (out, moved)


def _proj_dw(name, h, dqkv, du, dgates, l, fill):
    t, k = h.shape
    nb = dgates.shape[1] // 2
    tt = _tm(t)
    nt = t // tt

    def body(*refs):
        h_ref, dq_ref, dk_ref, dv_ref, du_ref, dg_ref = refs[:6]
        o_ref, acc_ref = refs[-2], refs[-1]
        n, kk = pl.program_id(0), pl.program_id(1)

        def update(g):
            part = lax.dot_general(h_ref[...], g, (TN, ((), ())), preferred_element_type=F32)

            @pl.when(kk == 0)
            def _():
                acc_ref[...] = part

            @pl.when(kk > 0)
            def _():
                acc_ref[...] += part

        @pl.when(n == 0)
        def _():
            update(jnp.concatenate([dq_ref[...], dk_ref[...]], axis=1))

        @pl.when(n == 1)
        def _():
            update(jnp.concatenate([dv_ref[...], du_ref[...]], axis=1))

        @pl.when(n >= 2)
        def _():
            update(dg_ref[...])

        @pl.when(kk == nt - 1)
        def _():
            o_ref[...] = acc_ref[...].astype(BF16)

    in_specs = ([pl.BlockSpec((tt, k), lambda n, kk: (kk, 0))] + _proj_pieces(tt, False)
                + [pl.BlockSpec((tt, nb), lambda n, kk: (kk, jnp.maximum(n - 2, 0)))])
    args, aliases = [h, dqkv, dqkv, dqkv, du, dgates], {}
    if fill is not None:
        in_specs.append(pl.BlockSpec(memory_space=pl.ANY))
        args.append(fill)
        aliases = {6: 0}
    return pl.pallas_call(
        body, name=name, grid=(N_CHIPS, nt), in_specs=in_specs,
        out_specs=pl.BlockSpec((None, None, k, nb), lambda n, kk: (l, n, 0, 0)),
        out_shape=_sds((DEPTH, N_CHIPS, k, nb), BF16),
        scratch_shapes=[pltpu.VMEM((k, nb), F32)], input_output_aliases=aliases,
        compiler_params=_cparams("parallel", "arbitrary"))(*args)


def _narrow_nn(name, a, w, l):
    t, k = a.shape
    nb = w.shape[3]
    tm = _tm(t)

    def body(a_ref, w_ref, o_ref):
        av = a_ref[...]
        for j in range(N_CHIPS):
            o_ref[:, j * nb:(j + 1) * nb] = jnp.dot(
                av, w_ref[j], preferred_element_type=F32).astype(BF16)

    return pl.pallas_call(
        body, name=name, grid=(t // tm,),
        in_specs=[pl.BlockSpec((tm, k), lambda i: (i, 0)),
                  pl.BlockSpec((None, N_CHIPS, k, nb), lambda i: (l, 0, 0, 0))],
        out_specs=pl.BlockSpec((tm, N_CHIPS * nb), lambda i: (i, 0)),
        out_shape=_sds((t, N_CHIPS * nb), BF16), compiler_params=_cparams("parallel"))(a, w)


def _narrow_nt(name, a, w, l):
    t = a.shape[0]
    k, nb = w.shape[2], w.shape[3]
    tm = _tm(t)

    def body(a_ref, w_ref, o_ref):
        acc = lax.dot_general(a_ref[:, 0:nb], w_ref[0], (NT, ((), ())), preferred_element_type=F32)
        for j in range(1, N_CHIPS):
            acc = acc + lax.dot_general(a_ref[:, j * nb:(j + 1) * nb], w_ref[j], (NT, ((), ())),
                                        preferred_element_type=F32)
        o_ref[...] = acc.astype(BF16)

    return pl.pallas_call(
        body, name=name, grid=(t // tm,),
        in_specs=[pl.BlockSpec((tm, N_CHIPS * nb), lambda i: (i, 0)),
                  pl.BlockSpec((None, N_CHIPS, k, nb), lambda i: (l, 0, 0, 0))],
        out_specs=pl.BlockSpec((tm, k), lambda i: (i, 0)),
        out_shape=_sds((t, k), BF16), compiler_params=_cparams("parallel"))(a, w)


def _narrow_tn(name, a, g, l, fill):
    t, k = a.shape
    nb = g.shape[1] // N_CHIPS
    tt = _tm(t)
    nt = t // tt

    def body(*refs):
        a_ref, g_ref, o_ref, acc_ref = refs[0], refs[1], refs[-2], refs[-1]
        i = pl.program_id(0)
        part = lax.dot_general(a_ref[...], g_ref[...], (TN, ((), ())), preferred_element_type=F32)

        @pl.when(i == 0)
        def _():
            acc_ref[...] = part

        @pl.when(i > 0)
        def _():
            acc_ref[...] += part

        @pl.when(i == nt - 1)
        def _():
            for j in range(N_CHIPS):
                o_ref[j] = acc_ref[:, j * nb:(j + 1) * nb].astype(BF16)

    in_specs = [pl.BlockSpec((tt, k), lambda i: (i, 0)),
                pl.BlockSpec((tt, N_CHIPS * nb), lambda i: (i, 0))]
    args, aliases = [a, g], {}
    if fill is not None:
        in_specs.append(pl.BlockSpec(memory_space=pl.ANY))
        args.append(fill)
        aliases = {2: 0}
    return pl.pallas_call(
        body, name=name, grid=(nt,), in_specs=in_specs,
        out_specs=pl.BlockSpec((None, N_CHIPS, k, nb), lambda i: (l, 0, 0, 0)),
        out_shape=_sds((DEPTH, N_CHIPS, k, nb), BF16),
        scratch_shapes=[pltpu.VMEM((k, N_CHIPS * nb), F32)], input_output_aliases=aliases,
        compiler_params=_cparams("arbitrary"))(*args)


def _mm_nn(name, a, w, l, tk, out_dtype):
    t, k = a.shape
    n = w.shape[2]
    tm = _tm(t)
    nk = k // tk
    return _matmul(
        name, a, w,
        pl.BlockSpec((tm, tk), lambda i, j, kk: (i, kk)),
        pl.BlockSpec((None, tk, n), lambda i, j, kk: (l, kk, 0)),
        pl.BlockSpec((tm, n), lambda i, j, kk: (i, 0)),
        _sds((t, n), out_dtype), (t // tm, 1, nk), NN, nk, (tm, n))


def _mm_nt(name, a, w, l, tn, out_dtype, comm=None):
    t, n = a.shape
    k = w.shape[1]
    tm = _tm(t)
    return _matmul(
        name, a, w,
        pl.BlockSpec((tm, n), lambda i, j, kk: (i, 0)),
        pl.BlockSpec((None, tn, n), lambda i, j, kk: (l, j, 0)),
        pl.BlockSpec((tm, tn), lambda i, j, kk: (i, j)),
        _sds((t, k), out_dtype), (t // tm, k // tn, 1), NT, 1, None, comm=comm)


def _mm_tn(name, a, g, tko, l, fill):
    t, k = a.shape
    n = g.shape[1]
    tt = _tm(t)
    nt = t // tt
    return _matmul(
        name, a, g,
        pl.BlockSpec((tt, tko), lambda i, j, kk: (kk, i)),
        pl.BlockSpec((tt, n), lambda i, j, kk: (kk, 0)),
        pl.BlockSpec((None, tko, n), lambda i, j, kk: (l, i, 0)),
        _sds((DEPTH, k, n), BF16), (k // tko, 1, nt), TN, nt, (tko, n), fill)


def _row_spec(width, col=0):
    return pl.BlockSpec((TOK, width), lambda i: (i, col))


def _vec_spec(width):
    return pl.BlockSpec((1, width), lambda i: (0, 0))


def _rms(x):
    return lax.rsqrt(jnp.mean(x * x, axis=-1, keepdims=True) + EPS)


def _norm_fwd(name, x, g):
    t = x.shape[0]

    def body(x_ref, g_ref, h_ref):
        xv = x_ref[...]
        h_ref[...] = (xv * _rms(xv) * g_ref[...]).astype(BF16)

    return pl.pallas_call(
        body, name=name, grid=(t // TOK,), in_specs=[_row_spec(D_MODEL), _vec_spec(D_MODEL)],
        out_specs=_row_spec(D_MODEL), out_shape=_sds((t, D_MODEL), BF16),
        compiler_params=_cparams("parallel"))(x, g)


ROWS = 16
ROW_UNROLL = 8


def _rows(k):
    return pl.ds(pl.multiple_of(k * ROWS, ROWS), ROWS)


def _strips(step, init):
    def group(j, carry):
        for u in range(ROW_UNROLL):
            carry = step(j * ROW_UNROLL + u, carry)
        return carry

    return lax.fori_loop(0, TOK // (ROWS * ROW_UNROLL), group, init)


def _fold_rows(x):
    return x[0:8] + x[8:16]


def _accumulate(ref, part):
    total = jnp.sum(part, axis=0, keepdims=True)

    @pl.when(pl.program_id(0) == 0)
    def _():
        ref[...] = total

    @pl.when(pl.program_id(0) > 0)
    def _():
        ref[...] += total


def _norm_bwd_rows(d, mv, g):
    r = _rms(mv)
    n = mv * r
    dn = d * g
    return r * (dn - n * jnp.mean(dn * n, axis=-1, keepdims=True)), d * n


def _post_pre_fwd(name, xres, m, g_post, g_pre, comm=None):
    t = xres.shape[0]

    def body(x_ref, m_ref, gp_ref, gn_ref, x1_ref, h_ref):
        def strip(k, c):
            rows = _rows(k)
            mv = m_ref[rows, :]
            x1 = x_ref[rows, :] + mv * _rms(mv) * gp_ref[...]
            x1_ref[rows, :] = x1
            h_ref[rows, :] = (x1 * _rms(x1) * gn_ref[...]).astype(BF16)
            return c

        _strips(strip, 0)

    outs, moved = _pcall(
        body, name, (t // TOK,),
        [_row_spec(D_MODEL), _row_spec(D_MODEL), _vec_spec(D_MODEL), _vec_spec(D_MODEL)],
        [_row_spec(D_MODEL), _row_spec(D_MODEL)],
        [_sds((t, D_MODEL), F32), _sds((t, D_MODEL), BF16)], [], _cparams("arbitrary"),
        (xres, m, g_post, g_pre), comm)
    return outs if comm is None else (*outs, moved)


def _tail(name, xres, m, g_post, target):
    t = xres.shape[0]

    def body(x_ref, m_ref, g_ref, t_ref, dy_ref, dm_ref, dg_ref, l_ref):
        def strip(k, carry):
            rows = _rows(k)
            mv = m_ref[rows, :]
            e = x_ref[rows, :] + mv * _rms(mv) * g_ref[...] - t_ref[rows, :]
            dy = e * (1.0 / D_MODEL)
            dy_ref[rows, :] = dy
            dm, dgn = _norm_bwd_rows(dy, mv, g_ref[...])
            dm_ref[rows, :] = dm.astype(BF16)
            return carry[0] + _fold_rows(dgn), carry[1] + _fold_rows(e * e)

        zero = jnp.zeros((8, D_MODEL), F32)
        dg, sq = _strips(strip, (zero, zero))
        _accumulate(dg_ref, dg)
        _accumulate(l_ref, jnp.sum(sq, axis=1, keepdims=True))

    dy, dm, dg, sq = pl.pallas_call(
        body, name=name, grid=(t // TOK,),
        in_specs=[_row_spec(D_MODEL), _row_spec(D_MODEL), _vec_spec(D_MODEL), _row_spec(D_MODEL)],
        out_specs=[_row_spec(D_MODEL), _row_spec(D_MODEL), _vec_spec(D_MODEL),
                   pl.BlockSpec((1, 1), lambda i: (0, 0))],
        out_shape=[_sds((t, D_MODEL), F32), _sds((t, D_MODEL), BF16), _sds((1, D_MODEL), F32),
                   _sds((1, 1), F32)],
        compiler_params=_cparams("arbitrary"))(xres, m, g_post, target)
    return dy, dm, dg, sq[0, 0] * (0.5 / D_MODEL)


def _pre_post_bwd(name, dh, xin, dxo, g_pre, m, g_post, comm=None):
    t = dh.shape[0]

    def body(dh_ref, x_ref, d_ref, gq_ref, m_ref, gp_ref, dx_ref, dgq_ref, dm_ref, dgp_ref):
        def strip(k, carry):
            rows = _rows(k)
            dxin, dgq = _norm_bwd_rows(dh_ref[rows, :], x_ref[rows, :], gq_ref[...])
            dx = d_ref[rows, :] + dxin
            dx_ref[rows, :] = dx
            dm, dgp = _norm_bwd_rows(dx, m_ref[rows, :], gp_ref[...])
            dm_ref[rows, :] = dm.astype(BF16)
            return carry[0] + _fold_rows(dgq), carry[1] + _fold_rows(dgp)

        zero = jnp.zeros((8, D_MODEL), F32)
        dgq, dgp = _strips(strip, (zero, zero))
        _accumulate(dgq_ref, dgq)
        _accumulate(dgp_ref, dgp)

    outs, moved = _pcall(
        body, name, (t // TOK,),
        [_row_spec(D_MODEL), _row_spec(D_MODEL), _row_spec(D_MODEL), _vec_spec(D_MODEL),
         _row_spec(D_MODEL), _vec_spec(D_MODEL)],
        [_row_spec(D_MODEL), _vec_spec(D_MODEL), _row_spec(D_MODEL), _vec_spec(D_MODEL)],
        [_sds((t, D_MODEL), F32), _sds((1, D_MODEL), F32), _sds((t, D_MODEL), BF16),
         _sds((1, D_MODEL), F32)], [], _cparams("arbitrary"),
        (dh, xin, dxo, g_pre, m, g_post), comm)
    return outs if comm is None else (*outs, moved)


def _norm_pre_bwd(name, dh, xin, dxo, g, comm=None):
    t = dh.shape[0]

    def body(dh_ref, x_ref, d_ref, g_ref, dx_ref, dg_ref):
        xv = x_ref[...]
        dhv = dh_ref[...]
        r = _rms(xv)
        n = xv * r
        dn = dhv * g_ref[...]
        dx_ref[...] = d_ref[...] + r * (dn - n * jnp.mean(dn * n, axis=-1, keepdims=True))
        part = jnp.sum(dhv * n, axis=0, keepdims=True)

        @pl.when(pl.program_id(0) == 0)
        def _():
            dg_ref[...] = part

        @pl.when(pl.program_id(0) > 0)
        def _():
            dg_ref[...] += part

    out, moved = _pcall(
        body, name, (t // TOK,),
        [_row_spec(D_MODEL), _row_spec(D_MODEL), _row_spec(D_MODEL), _vec_spec(D_MODEL)],
        [_row_spec(D_MODEL), _vec_spec(D_MODEL)],
        [_sds((t, D_MODEL), F32), _sds((1, D_MODEL), F32)], [], _cparams("arbitrary"),
        (dh, xin, dxo, g), comm)
    return out if comm is None else (*out, moved)


def _gate_fwd(name, proj, b_gate, ya, yb):
    t = proj.shape[0]

    def body(ga_ref, gb_ref, b_ref, ya_ref, yb_ref, z_ref):
        sa = jax.nn.sigmoid(ga_ref[...].astype(F32) + b_ref[:, :D_MODEL])
        sb = jax.nn.sigmoid(gb_ref[...].astype(F32) + b_ref[:, D_MODEL:])
        z_ref[...] = (sa * ya_ref[...].astype(F32) + sb * yb_ref[...].astype(F32)).astype(BF16)

    return pl.pallas_call(
        body, name=name, grid=(t // TOK,),
        in_specs=[_row_spec(D_MODEL, 2), _row_spec(D_MODEL, 3), _vec_spec(2 * D_MODEL),
                  _row_spec(D_MODEL), _row_spec(D_MODEL)],
        out_specs=_row_spec(D_MODEL), out_shape=_sds((t, D_MODEL), BF16),
        compiler_params=_cparams("parallel"))(proj, proj, b_gate, ya, yb)


def _gate_bwd(name, dz, proj, b_gate, ya, yb):
    t = proj.shape[0]

    def body(dz_ref, ga_ref, gb_ref, b_ref, ya_ref, yb_ref, dya_ref, dyb_ref, dg_ref, db_ref):
        dzv = dz_ref[...].astype(F32)
        sa = jax.nn.sigmoid(ga_ref[...].astype(F32) + b_ref[:, :D_MODEL])
        sb = jax.nn.sigmoid(gb_ref[...].astype(F32) + b_ref[:, D_MODEL:])
        dya_ref[...] = (dzv * sa).astype(BF16)
        dyb_ref[...] = (dzv * sb).astype(BF16)
        dga = dzv * ya_ref[...].astype(F32) * sa * (1.0 - sa)
        dgb = dzv * yb_ref[...].astype(F32) * sb * (1.0 - sb)
        dg_ref[:, :D_MODEL] = dga.astype(BF16)
        dg_ref[:, D_MODEL:] = dgb.astype(BF16)
        pa = jnp.sum(dga, axis=0, keepdims=True)
        pb = jnp.sum(dgb, axis=0, keepdims=True)

        @pl.when(pl.program_id(0) == 0)
        def _():
            db_ref[:, :D_MODEL] = pa
            db_ref[:, D_MODEL:] = pb

        @pl.when(pl.program_id(0) > 0)
        def _():
            db_ref[:, :D_MODEL] += pa
            db_ref[:, D_MODEL:] += pb

    return pl.pallas_call(
        body, name=name, grid=(t // TOK,),
        in_specs=[_row_spec(D_MODEL), _row_spec(D_MODEL, 2), _row_spec(D_MODEL, 3),
                  _vec_spec(2 * D_MODEL), _row_spec(D_MODEL), _row_spec(D_MODEL)],
        out_specs=[_row_spec(D_MODEL), _row_spec(D_MODEL), _row_spec(2 * D_MODEL),
                   _vec_spec(2 * D_MODEL)],
        out_shape=[_sds((t, D_MODEL), BF16), _sds((t, D_MODEL), BF16),
                   _sds((t, 2 * D_MODEL), BF16), _sds((1, 2 * D_MODEL), F32)],
        compiler_params=_cparams("arbitrary"))(dz, proj, proj, b_gate, ya, yb)


def _head_masks():
    lane = lax.broadcasted_iota(jnp.int32, (1, 2 * HEAD_DIM), 1)
    return lane < HEAD_DIM


BAND_ROWS = 2 * ATT_BLK + CHUNK


def _fill_band(band, prev_ref, cur_ref):
    band[0:ATT_BLK, :] = prev_ref[...]
    band[ATT_BLK:2 * ATT_BLK, :] = cur_ref[...]
    band[2 * ATT_BLK:, :] = jnp.zeros((CHUNK, ATTN_W), BF16)


def _pair_rows(x2, low):
    zero = jnp.zeros_like(x2)
    return jnp.concatenate([jnp.where(low, x2, zero), jnp.where(low, zero, x2)], axis=0)


def _pair_diag(o2, low):
    return jnp.where(low, o2[0:CHUNK, :], o2[CHUNK:, :])


N_PAIRS = HEADS // 2
SM_STRIP = 32
N_STRIPS = BAND_PAD // SM_STRIP
NEG = -1e30


def _fold8(x, op):
    return op(op(x[0:8], x[8:16]), op(x[16:24], x[24:32]))


def _strip(k):
    return pl.ds(pl.multiple_of(k * SM_STRIP, SM_STRIP), SM_STRIP)


def _band_probs(k2, qcat, bias_t, first_key):
    kpos = lax.broadcasted_iota(jnp.int32, (BAND_PAD, 1), 0)
    st = lax.dot_general(k2, qcat, (NT, ((), ())), preferred_element_type=F32)
    st = jnp.where(kpos + first_key >= 0, st + bias_t, NEG)
    e = jnp.exp(st - jnp.max(st, axis=0, keepdims=True))
    return e * (1.0 / jnp.sum(e, axis=0, keepdims=True))


def _band_softmax_stats(st_ref, b_ref, first_key, dp_ref):
    rowi = lax.broadcasted_iota(jnp.int32, (SM_STRIP, 128), 0)

    def scores(k, mx):
        rows = _strip(k)
        live = (rowi + (k * SM_STRIP + first_key)) >= 0
        out = []
        for hp in range(N_PAIRS):
            x = jnp.where(live, st_ref[hp, rows, :] + b_ref[hp, rows, :], NEG)
            st_ref[hp, rows, :] = x
            out.append(jnp.maximum(mx[hp], _fold8(x, jnp.maximum)))
        return tuple(out)

    mx = lax.fori_loop(0, N_STRIPS, scores, (jnp.full((8, 128), NEG, F32),) * N_PAIRS, unroll=2)
    top = [jnp.max(m, axis=0, keepdims=True) for m in mx]

    def sums(k, acc):
        rows = _strip(k)
        ls, eds = [], []
        for hp in range(N_PAIRS):
            e = jnp.exp(st_ref[hp, rows, :] - top[hp])
            ls.append(acc[hp] + _fold8(e, jnp.add))
            eds.append(acc[N_PAIRS + hp] + _fold8(e * dp_ref[hp, rows, :], jnp.add))
        return tuple(ls + eds)

    acc = lax.fori_loop(0, N_STRIPS, sums, (jnp.zeros((8, 128), F32),) * (2 * N_PAIRS), unroll=2)
    inv = [1.0 / jnp.sum(a, axis=0, keepdims=True) for a in acc[:N_PAIRS]]
    delta = [jnp.sum(a, axis=0, keepdims=True) * i for a, i in zip(acc[N_PAIRS:], inv)]
    return top, inv, delta


def _attn_specs(nblk):
    cur = lambda col: pl.BlockSpec((ATT_BLK, ATTN_W), lambda s: (jnp.minimum(s, nblk - 1), col))
    prev = lambda col: pl.BlockSpec(
        (ATT_BLK, ATTN_W), lambda s: (jnp.maximum(jnp.minimum(s, nblk - 1) - 1, 0), col))
    return cur, prev


def _attn_fwd(name, proj, bias, comm=None):
    t = proj.shape[0]
    nblk = t // ATT_BLK
    cur, prev = _attn_specs(nblk)

    def body(q_ref, kp_ref, kc_ref, vp_ref, vc_ref, b_ref, o_ref, kband, vband):
        s = pl.program_id(0)
        _fill_band(kband, kp_ref, kc_ref)
        _fill_band(vband, vp_ref, vc_ref)
        low = _head_masks()

        def chunk(ci, carry):
            r0 = pl.multiple_of(ci * CHUNK, CHUNK)
            for hp in range(N_PAIRS):
                cols = slice(hp * 128, (hp + 1) * 128)
                qcat = _pair_rows(q_ref[pl.ds(r0, CHUNK), cols] * ATTN_SCALE, low)
                p = _band_probs(kband[pl.ds(r0, BAND_PAD), cols], qcat, b_ref[hp],
                                (s * 8 - 8 + ci) * CHUNK)
                o2 = lax.dot_general(p.astype(BF16), vband[pl.ds(r0, BAND_PAD), cols],
                                     (TN, ((), ())), preferred_element_type=F32)
                o_ref[pl.ds(r0, CHUNK), cols] = _pair_diag(o2, low).astype(BF16)
            return carry

        lax.fori_loop(0, 8, chunk, 0)

    out, moved = _pcall(
        body, name, (nblk,),
        [cur(0), prev(1), cur(1), prev(2), cur(2),
         pl.BlockSpec((N_PAIRS, BAND_PAD, 128), lambda s: (0, 0, 0))],
        pl.BlockSpec((ATT_BLK, ATTN_W), lambda s: (s, 0)), _sds((t, ATTN_W), BF16),
        [pltpu.VMEM((BAND_ROWS, ATTN_W), BF16), pltpu.VMEM((BAND_ROWS, ATTN_W), BF16)],
        _cparams("arbitrary"), (proj, proj, proj, proj, proj, bias), comm)
    return out if comm is None else (out, moved)


def _attn_bwd(name, proj, datt, bias, comm=None):
    t = proj.shape[0]
    nblk = t // ATT_BLK
    cur, prev = _attn_specs(nblk)
    late = pl.BlockSpec((ATT_BLK, 3 * ATTN_W), lambda s: (jnp.maximum(s - 1, 0), 0))

    def body(q_ref, kp_ref, kc_ref, vp_ref, vc_ref, do_ref, b_ref,
             dqkv_ref, db_ref, kband, vband, dkacc, dvacc,
             st_ref, dp_ref, pb_ref, dsb_ref, qc_ref, dc_ref, dq_ref, dq_held):
        s = pl.program_id(0)

        @pl.when(s == 0)
        def _():
            dkacc[...] = jnp.zeros_like(dkacc)
            dvacc[...] = jnp.zeros_like(dvacc)
            db_ref[...] = jnp.zeros_like(db_ref)
            dq_ref[...] = jnp.zeros_like(dq_ref)

        @pl.when(s < nblk)
        def _():
            _fill_band(kband, kp_ref, kc_ref)
            _fill_band(vband, vp_ref, vc_ref)
            low = _head_masks()

            def chunk(ci, carry):
                r0 = pl.multiple_of(ci * CHUNK, CHUNK)
                for hp in range(N_PAIRS):
                    cols = slice(hp * 128, (hp + 1) * 128)
                    qc_ref[hp] = _pair_rows(q_ref[pl.ds(r0, CHUNK), cols] * ATTN_SCALE, low)
                    dc_ref[hp] = _pair_rows(do_ref[pl.ds(r0, CHUNK), cols], low)
                    st_ref[hp] = lax.dot_general(kband[pl.ds(r0, BAND_PAD), cols], qc_ref[hp],
                                                 (NT, ((), ())), preferred_element_type=F32)
                    dp_ref[hp] = lax.dot_general(vband[pl.ds(r0, BAND_PAD), cols], dc_ref[hp],
                                                 (NT, ((), ())), preferred_element_type=F32)
                top, inv, delta = _band_softmax_stats(st_ref, b_ref, (s * 8 - 8 + ci) * CHUNK,
                                                      dp_ref)

                def grads(k, c):
                    rows = _strip(k)
                    for hp in range(N_PAIRS):
                        p = jnp.exp(st_ref[hp, rows, :] - top[hp]) * inv[hp]
                        ds = p * (dp_ref[hp, rows, :] - delta[hp])
                        db_ref[hp, rows, :] += ds
                        dsb_ref[hp, rows, :] = ds.astype(BF16)
                        pb_ref[hp, rows, :] = p.astype(BF16)
                    return c

                lax.fori_loop(0, N_STRIPS, grads, 0, unroll=2)
                for hp in range(N_PAIRS):
                    cols = slice(hp * 128, (hp + 1) * 128)
                    dq2 = lax.dot_general(dsb_ref[hp], kband[pl.ds(r0, BAND_PAD), cols],
                                          (TN, ((), ())), preferred_element_type=F32)
                    dq_ref[pl.ds(r0, CHUNK), cols] = (_pair_diag(dq2, low) * ATTN_SCALE).astype(BF16)
                    dkacc[pl.ds(r0, BAND_PAD), cols] += jnp.dot(dsb_ref[hp], qc_ref[hp],
                                                               preferred_element_type=F32)
                    dvacc[pl.ds(r0, BAND_PAD), cols] += jnp.dot(pb_ref[hp], dc_ref[hp],
                                                               preferred_element_type=F32)
                return carry

            dq_held[...] = dq_ref[...]
            lax.fori_loop(0, 8, chunk, 0)

        @pl.when(s == nblk)
        def _():
            dq_held[...] = dq_ref[...]

        dqkv_ref[:, 0:ATTN_W] = dq_held[...]
        dqkv_ref[:, ATTN_W:2 * ATTN_W] = dkacc[0:ATT_BLK, :].astype(BF16)
        dqkv_ref[:, 2 * ATTN_W:] = dvacc[0:ATT_BLK, :].astype(BF16)
        dkacc[0:ATT_BLK, :] = dkacc[ATT_BLK:2 * ATT_BLK, :]
        dvacc[0:ATT_BLK, :] = dvacc[ATT_BLK:2 * ATT_BLK, :]
        dkacc[ATT_BLK:, :] = jnp.zeros((ATT_BLK + CHUNK, ATTN_W), F32)
        dvacc[ATT_BLK:, :] = jnp.zeros((ATT_BLK + CHUNK, ATTN_W), F32)

    outs, moved = _pcall(
        body, name, (nblk + 1,),
        [cur(0), prev(1), cur(1), prev(2), cur(2),
         pl.BlockSpec((ATT_BLK, ATTN_W), lambda s: (jnp.minimum(s, nblk - 1), 0)),
         pl.BlockSpec((HEADS // 2, BAND_PAD, 128), lambda s: (0, 0, 0))],
        [late, pl.BlockSpec((HEADS // 2, BAND_PAD, 128), lambda s: (0, 0, 0))],
        [_sds((t, 3 * ATTN_W), BF16), _sds((HEADS // 2, BAND_PAD, 128), F32)],
        [pltpu.VMEM((BAND_ROWS, ATTN_W), BF16), pltpu.VMEM((BAND_ROWS, ATTN_W), BF16),
         pltpu.VMEM((BAND_ROWS, ATTN_W), F32), pltpu.VMEM((BAND_ROWS, ATTN_W), F32),
         pltpu.VMEM((N_PAIRS, BAND_PAD, 128), F32), pltpu.VMEM((N_PAIRS, BAND_PAD, 128), F32),
         pltpu.VMEM((N_PAIRS, BAND_PAD, 128), BF16), pltpu.VMEM((N_PAIRS, BAND_PAD, 128), BF16),
         pltpu.VMEM((N_PAIRS, 2 * CHUNK, 128), BF16), pltpu.VMEM((N_PAIRS, 2 * CHUNK, 128), BF16),
         pltpu.VMEM((ATT_BLK, ATTN_W), BF16), pltpu.VMEM((ATT_BLK, ATTN_W), BF16)],
        _cparams("arbitrary"), (proj, proj, proj, proj, proj, datt, bias), comm)
    return outs if comm is None else (*outs, moved)


def _diag_onehot(rel_rows):
    d0 = lax.broadcasted_iota(jnp.int32, (BIAS_LANES, BIAS_LANES), 0)
    d1 = lax.broadcasted_iota(jnp.int32, (BIAS_LANES, BIAS_LANES), 1)
    m, n = (d0, d1) if rel_rows else (d1, d0)
    hit = (m == jnp.minimum(BAND - 1 + MAX_REL - n, 2 * MAX_REL)) & (n < BAND + CHUNK - 1)
    return jnp.where(hit, 1.0, 0.0).astype(F32)


def _bias_table(name, rel_bias_l):
    rel_pad = jnp.pad(rel_bias_l, ((0, 0), (0, BIAS_LANES - N_REL)))

    def body(r_ref, o_ref):
        diag = jnp.dot(r_ref[...], _diag_onehot(True), preferred_element_type=F32,
                       precision=lax.Precision.HIGHEST)
        rowid = lax.broadcasted_iota(jnp.int32, (8, BIAS_LANES), 0)
        lane = lax.broadcasted_iota(jnp.int32, (8, BIAS_LANES), 1)
        for h in range(HEADS):
            d8 = jnp.broadcast_to(diag[h:h + 1, :], (8, BIAS_LANES))
            slab0 = pltpu.roll(d8, BIAS_LANES - CHUNK + 1, axis=1)
            for b in range(1, 8):
                slab0 = jnp.where(rowid == b, pltpu.roll(d8, BIAS_LANES - CHUNK + 1 + b, axis=1),
                                  slab0)
            for a in range(8):
                slab = slab0 if a == 0 else pltpu.roll(slab0, 8 * a, axis=1)
                o_ref[h * CHUNK + 8 * a:h * CHUNK + 8 * a + 8, :] = jnp.where(lane < BAND, slab, NEG)

    tab = pl.pallas_call(
        body, name=name,
        in_specs=[pl.BlockSpec(memory_space=pltpu.VMEM)],
        out_specs=pl.BlockSpec(memory_space=pltpu.VMEM),
        out_shape=_sds((HEADS * CHUNK, BIAS_LANES), F32),
    )(rel_pad)
    tab = tab.reshape(HEADS // 2, 2, CHUNK, BIAS_LANES)
    return jnp.transpose(tab, (0, 3, 1, 2)).reshape(HEADS // 2, BIAS_LANES, 2 * CHUNK)


def _bias_fold(name, dbias_t):
    rows = HEADS * CHUNK
    dbias = jnp.transpose(dbias_t.reshape(HEADS // 2, BIAS_LANES, 2, CHUNK), (0, 2, 3, 1))

    def body(d_ref, o_ref):
        rowid = lax.broadcasted_iota(jnp.int32, (8, BIAS_LANES), 0)
        diags = []
        for h in range(HEADS):
            acc = d_ref[h * CHUNK + 56:h * CHUNK + 64, :]
            for a in range(7):
                slab = d_ref[h * CHUNK + 8 * a:h * CHUNK + 8 * a + 8, :]
                acc = acc + pltpu.roll(slab, 56 - 8 * a, axis=1)
            tot = jnp.where(rowid == 7, acc, 0.0)
            for b in range(7):
                tot = tot + jnp.where(rowid == b, pltpu.roll(acc, 7 - b, axis=1), 0.0)
            diags.append(jnp.sum(tot, axis=0, keepdims=True))
        diag = jnp.concatenate(diags, axis=0)
        o_ref[...] = jnp.dot(diag, _diag_onehot(False), preferred_element_type=F32,
                             precision=lax.Precision.HIGHEST)

    return pl.pallas_call(
        body, name=name,
        in_specs=[pl.BlockSpec(memory_space=pltpu.VMEM)],
        out_specs=pl.BlockSpec(memory_space=pltpu.VMEM),
        out_shape=_sds((HEADS, BIAS_LANES), F32),
    )(dbias.reshape(rows, BIAS_LANES))


def _inv_counts(i):
    trow = lax.broadcasted_iota(jnp.int32, (TOK + HALO, 1), 0) + i * TOK
    return [1.0 / jnp.minimum(trow + 1, w).astype(F32) for w in POOL_WINDOWS]


def _pool_fwd(name, proj, wg, scale, comm=None):
    t = proj.shape[0]
    hb = TOK // HALO

    def body(u_ref, up_ref, wg_ref, sc_ref, pooled_ref, mixed_ref, b0, b1, b2, b3):
        i = pl.program_id(0)
        halo = up_ref[...].astype(F32)
        b0[0:HALO, :] = jnp.where(i == 0, jnp.zeros_like(halo), halo)
        b0[HALO:, :] = u_ref[...].astype(F32)
        n = TOK + HALO
        b1[8:n, :] = b0[8:n, :] + b0[7:n - 1, :]
        b2[16:n, 128:] = b1[16:n, 128:] + b1[14:n - 2, 128:]
        b3[24:n, 256:] = b2[24:n, 256:] + b2[20:n - 4, 256:]
        wins = [b1[HALO:n, 0:128], b2[HALO:n, 128:256], b3[HALO:n, 256:384],
                b3[HALO:n, 384:512] + b3[HALO - 8:n - 8, 384:512]]
        inv = _inv_counts(i)
        for g in range(4):
            cols = slice(g * POOL_GD, (g + 1) * POOL_GD)
            pooled = (wins[g] * inv[g][0:TOK] - b0[HALO:n, cols]).astype(BF16)
            pooled_ref[:, cols] = pooled
            pre = jnp.dot(pooled, wg_ref[g], preferred_element_type=F32)
            mixed_ref[:, cols] = (pre * sc_ref[:, cols]).astype(BF16)

    buf = pltpu.VMEM((TOK + HALO, POOL_W), F32)
    outs, moved = _pcall(
        body, name, (t // TOK,),
        [_row_spec(POOL_W, 3),
         pl.BlockSpec((HALO, POOL_W), lambda i: (jnp.maximum(i * hb - 1, 0), 3)),
         pl.BlockSpec((4, POOL_GD, POOL_GD), lambda i: (0, 0, 0)), _vec_spec(POOL_W)],
        [_row_spec(POOL_W), _row_spec(POOL_W)],
        [_sds((t, POOL_W), BF16), _sds((t, POOL_W), BF16)], [buf, buf, buf, buf],
        _cparams("arbitrary"), (proj, proj, wg, scale), comm)
    return outs if comm is None else (*outs, moved)


def _pool_bwd(name, dmixed, pooled, wg, scale, comm=None):
    t = dmixed.shape[0]
    nt = t // TOK
    hb = TOK // HALO

    def body(dm_ref, dmn_ref, p_ref, wg_ref, sc_ref, du_ref, dwg_ref, dsc_ref, c0, c1, c2, c3):
        i = pl.program_id(0)

        @pl.when(i == 0)
        def _():
            dwg_ref[...] = jnp.zeros_like(dwg_ref)
            dsc_ref[...] = jnp.zeros_like(dsc_ref)

        n = TOK + HALO
        inv = _inv_counts(i)
        dmv = dm_ref[...].astype(F32)
        dmn = dmn_ref[...].astype(F32)
        dmn = jnp.where(i == nt - 1, jnp.zeros_like(dmn), dmn)
        for g in range(4):
            cols = slice(g * POOL_GD, (g + 1) * POOL_GD)
            scg = sc_ref[:, cols]
            pg = p_ref[:, cols]
            dpre = (dmv[:, cols] * scg).astype(BF16)
            dpre_n = (dmn[:, cols] * scg).astype(BF16)
            pre = jnp.dot(pg, wg_ref[g], preferred_element_type=F32)
            dsc_ref[:, cols] += jnp.sum(dmv[:, cols] * pre, axis=0, keepdims=True)
            dwg_ref[g] += lax.dot_general(pg, dpre, (TN, ((), ())), preferred_element_type=F32)
            dpool = lax.dot_general(dpre, wg_ref[g], (NT, ((), ())), preferred_element_type=F32)
            dpool_n = lax.dot_general(dpre_n, wg_ref[g], (NT, ((), ())),
                                      preferred_element_type=F32)
            c0[0:TOK, cols] = dpool
            c0[TOK:n, cols] = dpool_n
            c1[0:TOK, cols] = dpool * inv[g][0:TOK]
            c1[TOK:n, cols] = dpool_n * inv[g][TOK:n]
        c2[0:n - 8, :] = c1[0:n - 8, :] + c1[1:n - 7, :]
        c3[0:n - 16, 128:] = c2[0:n - 16, 128:] + c2[2:n - 14, 128:]
        c1[0:n - 24, 256:] = c3[0:n - 24, 256:] + c3[4:n - 20, 256:]
        wins = [c2[0:TOK, 0:128], c3[0:TOK, 128:256], c1[0:TOK, 256:384],
                c1[0:TOK, 384:512] + c1[8:TOK + 8, 384:512]]
        for g in range(4):
            cols = slice(g * POOL_GD, (g + 1) * POOL_GD)
            du_ref[:, cols] = (wins[g] - c0[0:TOK, cols]).astype(BF16)

    buf = pltpu.VMEM((TOK + HALO, POOL_W), F32)
    outs, moved = _pcall(
        body, name, (nt,),
        [_row_spec(POOL_W),
         pl.BlockSpec((HALO, POOL_W), lambda i: (jnp.minimum((i + 1) * hb, nt * hb - 1), 0)),
         _row_spec(POOL_W), pl.BlockSpec((4, POOL_GD, POOL_GD), lambda i: (0, 0, 0)),
         _vec_spec(POOL_W)],
        [_row_spec(POOL_W), pl.BlockSpec((4, POOL_GD, POOL_GD), lambda i: (0, 0, 0)),
         _vec_spec(POOL_W)],
        [_sds((t, POOL_W), BF16), _sds((4, POOL_GD, POOL_GD), F32), _sds((1, POOL_W), F32)],
        [buf, buf, buf, buf], _cparams("arbitrary"), (dmixed, dmixed, pooled, wg, scale), comm)
    return outs if comm is None else (*outs, moved)


GELU_C = math.sqrt(2.0 / math.pi)


GELU_K = 0.044715


def _gelu_parts(x):
    x2 = x * x
    s = 0.5 + 0.5 * jnp.tanh(x * (GELU_C + (GELU_C * GELU_K) * x2))
    return x * s, s, x2


def _gelu(x):
    return _gelu_parts(x)[0]


def _gelu_and_grad(x):
    g, s, x2 = _gelu_parts(x)
    return g, s + g * (1.0 - s) * ((2 * GELU_C) + (6 * GELU_C * GELU_K) * x2)


def _taps(buf, r, rows):
    a = buf[pl.ds(r, rows + 8), :]
    return a[8:], pltpu.roll(a, 1, axis=0)[8:], pltpu.roll(a, 2, axis=0)[8:]


def _conv(taps, w_ref, b_ref):
    return b_ref[...] + w_ref[2:3, :] * taps[0] + w_ref[1:2, :] * taps[1] + w_ref[0:1, :] * taps[2]


def _stage(dst, prev_ref, cur_ref, next_ref, first, last):
    rows = cur_ref.shape[0]
    h = prev_ref[...].astype(F32)
    dst[0:8, :] = jnp.where(first, jnp.zeros_like(h), h)
    dst[8:8 + rows, :] = cur_ref[...].astype(F32)
    if next_ref is not None:
        h = next_ref[...].astype(F32)
        dst[8 + rows:, :] = jnp.where(last, jnp.zeros_like(h), h)


FWD_STRIP = 32
BWD_STRIP = 16


def _ffn_gate_fwd(name, hu, conv_w, conv_b, comm=None):
    t = hu.shape[0]
    ncol = D_FF // FF_COL
    hb = FF_TOK // 8

    def tile(off):
        return pl.BlockSpec((FF_TOK, FF_COL), lambda i, j: (i, j + off))

    def halo(off):
        return pl.BlockSpec((8, FF_COL), lambda i, j: (jnp.maximum(i * hb - 1, 0), j + off))

    def wspec(off):
        return pl.BlockSpec((3, FF_COL), lambda i, j: (0, j + off))

    def bspec(off):
        return pl.BlockSpec((1, FF_COL), lambda i, j: (0, j + off))

    def body(v_ref, vp_ref, g_ref, gp_ref, wv_ref, wg_ref, bv_ref, bg_ref, a_ref, vb, gb):
        first = pl.program_id(0) == 0
        _stage(vb, vp_ref, v_ref, None, first, None)
        _stage(gb, gp_ref, g_ref, None, first, None)

        def strip(k, carry):
            r = pl.multiple_of(k * FWD_STRIP, FWD_STRIP)
            val = _conv(_taps(vb, r, FWD_STRIP), wv_ref, bv_ref)
            gate = _conv(_taps(gb, r, FWD_STRIP), wg_ref, bg_ref)
            a_ref[pl.ds(r, FWD_STRIP), :] = (_gelu(gate) * val).astype(BF16)
            return carry

        lax.fori_loop(0, FF_TOK // FWD_STRIP, strip, 0)

    buf = pltpu.VMEM((FF_TOK + 8, FF_COL), F32)
    out, moved = _pcall(
        body, name, (t // FF_TOK, ncol),
        [tile(0), halo(0), tile(ncol), halo(ncol), wspec(0), wspec(ncol), bspec(0), bspec(ncol)],
        pl.BlockSpec((FF_TOK, FF_COL), lambda i, j: (i, j)), _sds((t, D_FF), BF16), [buf, buf],
        _cparams("arbitrary", "arbitrary"),
        (hu, hu, hu, hu, conv_w, conv_w, conv_b, conv_b), comm)
    return out if comm is None else (out, moved)


def _ffn_gate_bwd(name, da, hu, conv_w, conv_b, comm=None):
    t = hu.shape[0]
    nt = t // FF_TOK
    ncol = D_FF // FF_COL
    hb = FF_TOK // 8
    ext = FF_TOK + 8

    def tile(off):
        return pl.BlockSpec((FF_TOK, FF_COL), lambda j, i: (i, j + off))

    def prev(off):
        return pl.BlockSpec((8, FF_COL), lambda j, i: (jnp.maximum(i * hb - 1, 0), j + off))

    def nxt(off):
        return pl.BlockSpec((8, FF_COL), lambda j, i: (jnp.minimum((i + 1) * hb, nt * hb - 1), j + off))

    def wspec(off):
        return pl.BlockSpec((3, FF_COL), lambda j, i: (0, j + off))

    def bspec(off):
        return pl.BlockSpec((1, FF_COL), lambda j, i: (0, j + off))

    def body(da_ref, dan_ref, v_ref, vp_ref, vn_ref, g_ref, gp_ref, gn_ref,
             wv_ref, wg_ref, bv_ref, bg_ref, dh_ref, dwv_ref, dwg_ref, vb, gb, dab):
        i = pl.program_id(1)
        first, last = i == 0, i == nt - 1

        @pl.when(first)
        def _():
            dwv_ref[...] = jnp.zeros_like(dwv_ref)
            dwg_ref[...] = jnp.zeros_like(dwg_ref)

        _stage(vb, vp_ref, v_ref, vn_ref, first, last)
        _stage(gb, gp_ref, g_ref, gn_ref, first, last)
        dab[0:FF_TOK, :] = da_ref[...].astype(F32)
        h = dan_ref[...].astype(F32)
        dab[FF_TOK:, :] = jnp.where(last, jnp.zeros_like(h), h)

        def grads(r, rows):
            tv, tg = _taps(vb, r, rows), _taps(gb, r, rows)
            gate = _conv(tg, wg_ref, bg_ref)
            dav = dab[pl.ds(r, rows), :]
            g, dg = _gelu_and_grad(gate)
            dval = dav * g
            dgate = dav * _conv(tv, wv_ref, bv_ref) * dg
            return dval, dgate, tv, tg

        def fold(x):
            return x[0:8] + x[8:16]

        def strip(k, carry):
            r = pl.multiple_of(FF_TOK - BWD_STRIP - k * BWD_STRIP, BWD_STRIP)
            dval, dgate, tv, tg = grads(r, BWD_STRIP)
            new = (dval[0:8], dgate[0:8])
            for half, (d, nxt_rows, taps, w_ref, dw_ref) in enumerate((
                    (dval, carry[0], tv, wv_ref, dwv_ref), (dgate, carry[1], tg, wg_ref, dwg_ref))):
                e = jnp.concatenate([d, nxt_rows], axis=0)
                dh = (w_ref[2:3, :] * d
                      + w_ref[1:2, :] * pltpu.roll(e, BWD_STRIP + 7, axis=0)[0:BWD_STRIP]
                      + w_ref[0:1, :] * pltpu.roll(e, BWD_STRIP + 6, axis=0)[0:BWD_STRIP])
                dh_ref[half, pl.ds(r, BWD_STRIP), :] = dh.astype(BF16)
                dw_ref[0:8, :] += fold(d * taps[2])
                dw_ref[8:16, :] += fold(d * taps[1])
                dw_ref[16:24, :] += fold(d * taps[0])
                dw_ref[24:32, :] += fold(d)
            return new

        dval, dgate, _, _ = grads(FF_TOK, 8)
        lax.fori_loop(0, FF_TOK // BWD_STRIP, strip, (dval, dgate))

        @pl.when(last)
        def _():
            for dw_ref in (dwv_ref, dwg_ref):
                for q in range(4):
                    dw_ref[8 * q:8 * q + 1, :] = jnp.sum(dw_ref[8 * q:8 * q + 8, :], axis=0,
                                                         keepdims=True)

    hbuf = pltpu.VMEM((FF_TOK + 16, FF_COL), F32)
    acc = pl.BlockSpec((32, FF_COL), lambda j, i: (0, j))
    (dhu, dwv, dwg), moved = _pcall(
        body, name, (ncol, nt),
        [tile(0), nxt(0), tile(0), prev(0), nxt(0), tile(ncol), prev(ncol), nxt(ncol),
         wspec(0), wspec(ncol), bspec(0), bspec(ncol)],
        [pl.BlockSpec((2, FF_TOK, FF_COL), lambda j, i: (0, i, j)), acc, acc],
        [_sds((2, t, D_FF), BF16), _sds((32, D_FF), F32), _sds((32, D_FF), F32)],
        [hbuf, hbuf, pltpu.VMEM((ext, FF_COL), F32)], _cparams("arbitrary", "arbitrary"),
        (da, da, hu, hu, hu, hu, hu, hu, conv_w, conv_w, conv_b, conv_b), comm)
    dconv = jnp.concatenate([dwv, dwg], axis=1).reshape(4, 8, 2 * D_FF)[:, 0]
    return (dhu, dconv) if comm is None else (dhu, dconv, moved)


def _mesh_pos():
    x, y, c = lax.axis_index("x"), lax.axis_index("y"), lax.axis_index("c")
    return x, y, c, [(1 - x, y), (x, 1 - y), (1 - x, 1 - y)]


def _any_specs(n):
    return [pl.BlockSpec(memory_space=pl.ANY)] * n


def _remote(src, dst, send_sems, recv_sems, i, dev):
    return pltpu.make_async_remote_copy(src_ref=src, dst_ref=dst, send_sem=send_sems.at[i],
                                        recv_sem=recv_sems.at[i], device_id=dev,
                                        device_id_type=MESH)


def _mine(c, rows):
    return pl.ds(pl.multiple_of(c * (rows // 2), 16), rows // 2)


def _gather_send(shards, conv_shard, gathered, l):
    nbig = len(shards)
    with_conv = conv_shard is not None
    if gathered is None:
        ins = list(shards) + ([conv_shard] if with_conv else [])
        outs = [_sds((DEPTH, N_CHIPS) + s.shape[1:], s.dtype) for s in ins]
        alias = {}
    else:
        ins = list(shards) + list(gathered)
        outs = [_sds(g.shape, g.dtype) for g in gathered]
        alias = {nbig + k: k for k in range(nbig)}

    def copies(cin, cout, ssem, rsem):
        x, y, c, chips = _mesh_pos()
        me = 2 * x + y
        out = []
        for k in range(nbig):
            rows = shards[k].shape[1]
            for j, (cx, cy) in enumerate(chips):
                out.append(_remote(cin[k].at[l, _mine(c, rows)], cout[k].at[l, me, _mine(c, rows)],
                                   ssem, rsem, 4 * k + j, (cx, cy, c)))
            out.append(_remote(cin[k].at[l], cout[k].at[l, me], ssem, rsem, 4 * k + 3,
                               (x, y, 1 - c)))
        if with_conv:
            base = 4 * nbig
            for j, (cx, cy) in enumerate(chips):
                out.append(_remote(cin[nbig].at[c], cout[nbig].at[c, me], ssem, rsem, base + j,
                                   (cx, cy, c)))
            for ll in range(DEPTH):
                out.append(_remote(cin[nbig].at[ll], cout[nbig].at[ll, me], ssem, rsem,
                                   base + 3 + ll, (x, y, 1 - c)))
        return out

    return _Comm(ins, outs, copies, 4 * nbig + 5, alias)


def _gather_forward(gathered, nbig, rows, l):
    with_conv = len(gathered) > nbig
    alias = {k: k for k in range(len(gathered))}

    def copies(cin, cout, ssem, rsem):
        x, y, c, chips = _mesh_pos()
        out = []
        for k in range(nbig):
            for j, (cx, cy) in enumerate(chips):
                blk = cout[k].at[l, 2 * cx + cy, _mine(c, rows[k])]
                out.append(_remote(blk, blk, ssem, rsem, 3 * k + j, (x, y, 1 - c)))
        if with_conv:
            for j, (cx, cy) in enumerate(chips):
                blk = cout[nbig].at[c, 2 * cx + cy]
                out.append(_remote(blk, blk, ssem, rsem, 3 * nbig + j, (x, y, 1 - c)))
        return out

    return _Comm(gathered, [_sds(g.shape, g.dtype) for g in gathered], copies, 3 * nbig + 3, alias)


def _reduce_swap(grads, l):
    def copies(cin, cout, ssem, rsem):
        x, y, c, _ = _mesh_pos()
        return [_remote(cin[k].at[l, :, _mine(1 - c, g.shape[2])], cout[k], ssem, rsem, k,
                        (x, y, 1 - c)) for k, g in enumerate(grads)]

    outs = [_sds((N_CHIPS, g.shape[2] // 2, g.shape[3]), g.dtype) for g in grads]
    return _Comm(grads, outs, copies, len(grads))


def _reduce_scatter(sums):
    def copies(cin, cout, ssem, rsem):
        x, y, c, chips = _mesh_pos()
        return [_remote(cin[k].at[2 * cx + cy], cout[k].at[j], ssem, rsem, 3 * k + j, (cx, cy, c))
                for k in range(len(sums)) for j, (cx, cy) in enumerate(chips)]

    outs = [_sds((3,) + s.shape[1:], s.dtype) for s in sums]
    return _Comm(sums, outs, copies, 3 * len(sums))


def _reduce_share(reds, l):
    def copies(cin, cout, ssem, rsem):
        x, y, c, _ = _mesh_pos()
        out = []
        for k, r in enumerate(reds):
            half = cout[k].at[l, _mine(c, r.shape[1])]
            out.append(_remote(half, half, ssem, rsem, k, (x, y, 1 - c)))
        return out

    return _Comm(reds, [_sds(r.shape, r.dtype) for r in reds], copies, len(reds),
                 {k: k for k in range(len(reds))})


def _allgather_weights(shards):
    n = len(shards)

    def body(*refs):
        ins, outs = refs[:n], refs[n:2 * n]
        send_sems, recv_sems = refs[2 * n:]
        x, y, c, chips = _mesh_pos()
        me = 2 * x + y
        started = []
        own = []
        for k in range(n):
            for l in range(2):
                cp = pltpu.make_async_remote_copy(
                    src_ref=ins[k].at[l], dst_ref=outs[k].at[l, me],
                    send_sem=send_sems.at[k, 6 + l], recv_sem=recv_sems.at[k, 6 + l],
                    device_id=(x, y, 1 - c), device_id_type=MESH)
                cp.start()
                own.append(cp)
            for j, (cx, cy) in enumerate(chips):
                cp = pltpu.make_async_remote_copy(
                    src_ref=ins[k].at[c], dst_ref=outs[k].at[c, me],
                    send_sem=send_sems.at[k, j], recv_sem=recv_sems.at[k, j],
                    device_id=(cx, cy, c), device_id_type=MESH)
                cp.start()
                started.append(cp)
        for k in range(n):
            for j, (cx, cy) in enumerate(chips):
                landed = outs[k].at[c, 2 * cx + cy]
                pltpu.make_async_remote_copy(
                    src_ref=ins[k].at[c], dst_ref=landed,
                    send_sem=send_sems.at[k, j], recv_sem=recv_sems.at[k, j],
                    device_id=(cx, cy, c), device_id_type=MESH).wait_recv()
                fw = pltpu.make_async_remote_copy(
                    src_ref=landed, dst_ref=landed,
                    send_sem=send_sems.at[k, 3 + j], recv_sem=recv_sems.at[k, 3 + j],
                    device_id=(x, y, 1 - c), device_id_type=MESH)
                fw.start()
                started.append(fw)
        for k in range(n):
            for j, (cx, cy) in enumerate(chips):
                theirs = outs[k].at[1 - c, 2 * cx + cy]
                pltpu.make_async_remote_copy(
                    src_ref=theirs, dst_ref=theirs,
                    send_sem=send_sems.at[k, 3 + j], recv_sem=recv_sems.at[k, 3 + j],
                    device_id=(x, y, 1 - c), device_id_type=MESH).wait_recv()
        for cp in started:
            cp.wait_send()
        for cp in own:
            cp.wait()

    return pl.pallas_call(
        body, name="allgather_weights",
        in_specs=_any_specs(n), out_specs=_any_specs(n),
        out_shape=[_sds((2, N_CHIPS) + s.shape[1:], s.dtype) for s in shards],
        scratch_shapes=[pltpu.SemaphoreType.DMA((n, 8)), pltpu.SemaphoreType.DMA((n, 8))],
    )(*shards)


def _swap_layers(grads):
    n = len(grads)

    def body(*refs):
        ins, outs = refs[:n], refs[n:2 * n]
        send_sems, recv_sems = refs[2 * n:]
        x, y, c, _ = _mesh_pos()
        cps = []
        for k in range(n):
            cp = pltpu.make_async_remote_copy(
                src_ref=ins[k].at[1 - c], dst_ref=outs[k],
                send_sem=send_sems.at[k], recv_sem=recv_sems.at[k],
                device_id=(x, y, 1 - c), device_id_type=MESH)
            cp.start()
            cps.append(cp)
        for cp in cps:
            cp.wait()

    return pl.pallas_call(
        body, name="swap_layers",
        in_specs=_any_specs(n), out_specs=_any_specs(n),
        out_shape=[_sds(g.shape[1:], g.dtype) for g in grads],
        scratch_shapes=[pltpu.SemaphoreType.DMA((n,)), pltpu.SemaphoreType.DMA((n,))],
    )(*grads)


def _scatter_blocks(sums):
    n = len(sums)

    def body(*refs):
        ins, outs = refs[:n], refs[n:2 * n]
        send_sems, recv_sems = refs[2 * n:]
        x, y, c, chips = _mesh_pos()
        cps = []
        for k in range(n):
            for j, (cx, cy) in enumerate(chips):
                cp = pltpu.make_async_remote_copy(
                    src_ref=ins[k].at[2 * cx + cy], dst_ref=outs[k].at[j],
                    send_sem=send_sems.at[k, j], recv_sem=recv_sems.at[k, j],
                    device_id=(cx, cy, c), device_id_type=MESH)
                cp.start()
                cps.append(cp)
        for cp in cps:
            cp.wait()

    return pl.pallas_call(
        body, name="scatter_blocks",
        in_specs=_any_specs(n), out_specs=_any_specs(n),
        out_shape=[_sds((3,) + s.shape[1:], s.dtype) for s in sums],
        scratch_shapes=[pltpu.SemaphoreType.DMA((n, 3)), pltpu.SemaphoreType.DMA((n, 3))],
    )(*sums)


def _exchange_reduced(reds):
    n = len(reds)

    def body(*refs):
        outs = refs[n:2 * n]
        send_sems, recv_sems = refs[2 * n:]
        x, y, c, _ = _mesh_pos()
        cps = []
        for k in range(n):
            cp = pltpu.make_async_remote_copy(
                src_ref=outs[k].at[c], dst_ref=outs[k].at[c],
                send_sem=send_sems.at[k], recv_sem=recv_sems.at[k],
                device_id=(x, y, 1 - c), device_id_type=MESH)
            cp.start()
            cps.append(cp)
        for k in range(n):
            pltpu.make_async_remote_copy(
                src_ref=outs[k].at[c], dst_ref=outs[k].at[1 - c],
                send_sem=send_sems.at[k], recv_sem=recv_sems.at[k],
                device_id=(x, y, 1 - c), device_id_type=MESH).wait_recv()
        for cp in cps:
            cp.wait_send()

    return pl.pallas_call(
        body, name="exchange_reduced",
        in_specs=_any_specs(n), out_specs=_any_specs(n),
        out_shape=[_sds(r.shape, r.dtype) for r in reds],
        input_output_aliases={k: k for k in range(n)},
        scratch_shapes=[pltpu.SemaphoreType.DMA((n,)), pltpu.SemaphoreType.DMA((n,))],
    )(*reds)


def _allreduce_small(pack):
    n = pack.shape[0]

    def body(x_ref, o_ref, gbuf, send_sems, recv_sems):
        x, y, c, chips = _mesh_pos()
        sibling = (x, y, 1 - c)

        def slot(px, py, pc):
            return gbuf.at[4 * px + 2 * py + pc]

        def copy(k, block, to, src=None):
            return pltpu.make_async_remote_copy(
                src_ref=slot(*block) if src is None else src, dst_ref=slot(*block),
                send_sem=send_sems.at[k], recv_sem=recv_sems.at[k],
                device_id=to, device_id_type=MESH)

        me = (x, y, c)
        first = [copy(0, me, sibling, src=x_ref)]
        first += [copy(1 + j, me, (*chip, c), src=x_ref) for j, chip in enumerate(chips)]
        for cp in first:
            cp.start()
        gbuf[4 * x + 2 * y + c] = x_ref[...]
        passed = [copy(4 + j, (*chip, c), sibling) for j, chip in enumerate(chips)]
        for j, chip in enumerate(chips):
            copy(1 + j, (*chip, c), me).wait_recv()
            passed[j].start()
        copy(0, sibling, me).wait_recv()
        for j, chip in enumerate(chips):
            copy(4 + j, (*chip, 1 - c), me).wait_recv()
        for cp in first + passed:
            cp.wait_send()
        acc = gbuf[0]
        for d in range(1, 8):
            acc = acc + gbuf[d]
        o_ref[...] = acc

    return pl.pallas_call(
        body, name="allreduce_small",
        in_specs=[pl.BlockSpec(memory_space=pltpu.VMEM)],
        out_specs=pl.BlockSpec(memory_space=pltpu.VMEM),
        out_shape=_sds((n, 128), F32),
        scratch_shapes=[pltpu.VMEM((8, n, 128), F32), pltpu.SemaphoreType.DMA((7,)),
                        pltpu.SemaphoreType.DMA((7,))],
        compiler_params=pltpu.CompilerParams(vmem_limit_bytes=VMEM_LIMIT_V7X),
    )(pack)


def _core_index():
    return jnp.reshape(lax.axis_index("c"), (1,)).astype(jnp.int32)


def _chip_index():
    return jnp.reshape(2 * lax.axis_index("x") + lax.axis_index("y"), (1,)).astype(jnp.int32)


def _chip_sum(name, stacked, sib, l):
    _, nb, r, cdim = stacked.shape
    hr = r // 2

    def body(c_ref, a_ref, b_ref, o_ref):
        o_ref[...] = (a_ref[...].astype(F32) + b_ref[...].astype(F32)).astype(BF16)

    return pl.pallas_call(
        body, name=name,
        grid_spec=pltpu.PrefetchScalarGridSpec(
            num_scalar_prefetch=1, grid=(nb,),
            in_specs=[pl.BlockSpec((None, None, hr, cdim), lambda j, cr: (l, j, cr[0], 0)),
                      pl.BlockSpec((None, hr, cdim), lambda j, cr: (j, 0, 0))],
            out_specs=pl.BlockSpec((None, hr, cdim), lambda j, cr: (j, 0, 0))),
        out_shape=_sds((nb, hr, cdim), BF16),
        compiler_params=_cparams("parallel"))(_core_index(), stacked, sib)


def _final_sum(name, sums, recv, l, fill):
    _, hr, cdim = sums.shape
    tr = hr // 2

    def body(m_ref, a_ref, b_ref, *rest):
        acc = a_ref[...].astype(F32)
        for j in range(3):
            acc = acc + b_ref[j].astype(F32)
        rest[-1][...] = acc

    in_specs = [pl.BlockSpec((None, tr, cdim), lambda i, mr: (mr[0], i, 0)),
                pl.BlockSpec((3, tr, cdim), lambda i, mr: (0, i, 0))]
    args = [jnp.concatenate([_chip_index(), _core_index()]), sums, recv]
    aliases = {}
    if fill is not None:
        in_specs.append(pl.BlockSpec(memory_space=pl.ANY))
        args.append(fill)
        aliases = {3: 0}
    return pl.pallas_call(
        body, name=name,
        grid_spec=pltpu.PrefetchScalarGridSpec(
            num_scalar_prefetch=1, grid=(2,), in_specs=in_specs,
            out_specs=pl.BlockSpec((None, tr, cdim), lambda i, mr: (l, 2 * mr[1] + i, 0))),
        out_shape=_sds((DEPTH, 2 * hr, cdim), F32), input_output_aliases=aliases,
        compiler_params=_cparams("parallel"))(*args)


def _adamw(name, w, g, m, v):
    nl, r, cdim = w.shape
    tr = r // 4 if r % 32 == 0 else r
    c1 = 1.0 - ADAM_B1 ** ADAM_STEP
    c2 = 1.0 - ADAM_B2 ** ADAM_STEP

    def body(w_ref, g_ref, m_ref, v_ref, d_ref, nm_ref, nv_ref):
        gv = g_ref[...]
        nm = ADAM_B1 * m_ref[...] + (1.0 - ADAM_B1) * gv
        nv = ADAM_B2 * v_ref[...] + (1.0 - ADAM_B2) * (gv * gv)
        nm_ref[...] = nm
        nv_ref[...] = nv
        d_ref[...] = -ADAM_LR * ((nm / c1) / (jnp.sqrt(nv / c2) + ADAM_EPS) + ADAM_WD * w_ref[...])

    spec = pl.BlockSpec((None, tr, cdim), lambda l, i: (l, i, 0))
    out = _sds(w.shape, F32)
    return pl.pallas_call(
        body, name=name, grid=(nl, r // tr),
        in_specs=[spec] * 4, out_specs=[spec] * 3, out_shape=[out] * 3,
        compiler_params=_cparams("parallel", "parallel"))(w, g, m, v)


def _rows128(a):
    return a.reshape(-1, 128)


def kernel(x, norm_mix_pre, w_in, b_gate, rel_bias, w_attn_out, w_pool_group, pool_scale, w_pool_out, w_o, norm_mix_post, norm_ffn_pre, w_up, conv_w, conv_b, w_down, norm_ffn_post, loss_target, m_norm_mix_pre, m_w_in, m_b_gate, m_rel_bias, m_w_attn_out, m_w_pool_group, m_pool_scale, m_w_pool_out, m_w_o, m_norm_mix_post, m_norm_ffn_pre, m_w_up, m_conv_w, m_conv_b, m_w_down, m_norm_ffn_post, v_norm_mix_pre, v_w_in, v_b_gate, v_rel_bias, v_w_attn_out, v_w_pool_group, v_pool_scale, v_w_pool_out, v_w_o, v_norm_mix_post, v_norm_ffn_pre, v_w_up, v_conv_w, v_conv_b, v_w_down, v_norm_ffn_post):
    t = x.shape[1]
    xs = x.reshape(t, D_MODEL)
    target = loss_target.reshape(t, D_MODEL)

    names = ["w_in", "w_attn_out", "w_pool_out", "w_o", "w_up", "w_down"]
    shards = [w.astype(BF16) for w in (w_in, w_attn_out, w_pool_out, w_o, w_up, w_down)]
    rows = [s.shape[1] for s in shards]
    nbig = len(shards)
    g = _comm_call("gather0_send", _gather_send(shards[:1], conv_w, None, 0))
    g = _comm_call("gather0_forward", _gather_forward(g, 1, rows[:1], 0))
    cw_full = jnp.transpose(g[1], (0, 2, 1, 3)).reshape(DEPTH, 3, 2 * D_FF)
    g = g[:1]
    wg_bf = w_pool_group.astype(BF16)

    def views(gathered):
        win_g, wao_g, wpo_g, wo_g, wup_g, wdn_g = gathered
        return (win_g, wao_g, wpo_g, wo_g.reshape(DEPTH, D_MODEL, D_MODEL), wup_g,
                wdn_g.reshape(DEPTH, D_FF, D_MODEL))

    saved = []
    xcur = xs
    h = _norm_fwd("l0_norm_mix_pre", xs, norm_mix_pre[0:1])
    for l in range(DEPTH):
        tag = f"l{l}_"
        bias = _bias_table(tag + "bias_table", rel_bias[l])
        proj = _mm_nn_blocked(tag + "proj", h, g[0], l, BF16)
        if l == 0:
            att, rest = _attn_fwd(tag + "attn_fwd", proj, bias,
                                  _gather_send(shards[1:], None, None, 0))
            pooled, mixed, rest = _pool_fwd(tag + "pool_fwd", proj, wg_bf[l], pool_scale[l:l + 1],
                                            _gather_forward(rest, nbig - 1, rows[1:], 0))
            g = g + rest
        else:
            att = _attn_fwd(tag + "attn_fwd", proj, bias)
            pooled, mixed = _pool_fwd(tag + "pool_fwd", proj, wg_bf[l], pool_scale[l:l + 1])
        win_g, wao_g, wpo_g, wo_full, wup_g, wdn_full = views(g)
        ya = _narrow_nn(tag + "attn_out", att, wao_g, l)
        yb = _narrow_nn(tag + "pool_out", mixed, wpo_g, l)
        z = _gate_fwd(tag + "gate_fwd", proj, b_gate[l:l + 1], ya, yb)
        mix = _mm_nn(tag + "mix", z, wo_full, l, D_MODEL, F32)
        x1, h2 = _post_pre_fwd(tag + "norm_mix_post", xcur, mix, norm_mix_post[l:l + 1],
                               norm_ffn_pre[l:l + 1])
        hu = _mm_nn_blocked(tag + "ffn_up", h2, wup_g, l, BF16)
        if l == 0:
            a, g = _ffn_gate_fwd(tag + "ffn_gate_fwd", hu, cw_full[l], conv_b[l:l + 1],
                                 _gather_send(shards, None, g, 1))
            wdn_full = views(g)[5]
        else:
            a = _ffn_gate_fwd(tag + "ffn_gate_fwd", hu, cw_full[l], conv_b[l:l + 1])
        f = _mm_nn(tag + "ffn_down", a, wdn_full, l, D_FF // 2, F32)
        saved.append(dict(x=xcur, h=h, proj=proj, att=att, pooled=pooled, mixed=mixed, ya=ya,
                          yb=yb, z=z, mix=mix, x1=x1, h2=h2, hu=hu, a=a, f=f, bias=bias))
        if l == 0:
            xcur, h, g = _post_pre_fwd(tag + "norm_ffn_post", x1, f, norm_ffn_post[l:l + 1],
                                       norm_mix_pre[l + 1:l + 2], _gather_forward(g, nbig, rows, 1))
        elif l < DEPTH - 1:
            xcur, h = _post_pre_fwd(tag + "norm_ffn_post", x1, f, norm_ffn_post[l:l + 1],
                                    norm_mix_pre[l + 1:l + 2])
    win_g, wao_g, wpo_g, wo_full, wup_g, wdn_full = views(g)

    dy, df, d_nfpost, loss_local = _tail("tail", saved[-1]["x1"], saved[-1]["f"],
                                         norm_ffn_post[DEPTH - 1:DEPTH], target)
    loss = lax.psum(loss_local, ("x", "y", "c"))

    dx = dy
    dws = dict.fromkeys(names)
    reds = [None] * nbig
    small_grads = [None] * DEPTH
    ffn = [4, 5]
    outs3 = [1, 2, 3]

    def blocks(ks):
        return [dws[names[k]].reshape(DEPTH, N_CHIPS, rows[k], -1) for k in ks]

    def chip_sums(ks, sib, l):
        return [_chip_sum(f"chip_sum{l}_" + names[k], b, s, l)
                for k, b, s in zip(ks, blocks(ks), sib)]

    def final_sums(ks, sums, recv, l):
        for k, s, r in zip(ks, sums, recv):
            reds[k] = _final_sum(f"final_sum{l}_" + names[k], s, r, l, reds[k])

    for l in reversed(range(DEPTH)):
        tag = f"l{l}_"
        sv = saved[l]
        every = list(range(nbig))
        if l == 0:
            da, sib = _mm_nt(tag + "ffn_down_dx", df, wdn_full, l, D_FF // 2, BF16,
                             _reduce_swap(blocks(every), 1))
            sums = chip_sums(every, sib, 1)
        else:
            da = _mm_nt(tag + "ffn_down_dx", df, wdn_full, l, D_FF // 2, BF16)
        dws["w_down"] = _mm_tn(tag + "ffn_down_dw", sv["a"], df, D_FF // 2, l, dws["w_down"])
        if l == 0:
            dhu, dconv, recv = _ffn_gate_bwd(tag + "ffn_gate_bwd", da, sv["hu"], cw_full[l],
                                             conv_b[l:l + 1], _reduce_scatter(sums))
            final_sums(every, sums, recv, 1)
            dh2, reds = _mm_nt_blocked(tag + "ffn_up_dx", dhu, wup_g, l, F32,
                                       _reduce_share(reds, 1))
        else:
            dhu, dconv = _ffn_gate_bwd(tag + "ffn_gate_bwd", da, sv["hu"], cw_full[l],
                                       conv_b[l:l + 1])
            dh2 = _mm_nt_blocked(tag + "ffn_up_dx", dhu, wup_g, l, F32)
        dws["w_up"] = _mm_tn_blocked(tag + "ffn_up_dw", sv["h2"], dhu, l, dws["w_up"])
        if l == 0:
            dx1, d_nfpre, dmix, d_nmpost, sib = _pre_post_bwd(
                tag + "norm_ffn_pre_bwd", dh2, sv["x1"], dx, norm_ffn_pre[l:l + 1], sv["mix"],
                norm_mix_post[l:l + 1], _reduce_swap(blocks(ffn), 0))
            sums = chip_sums(ffn, sib, 0)
        else:
            dx1, d_nfpre, dmix, d_nmpost = _pre_post_bwd(
                tag + "norm_ffn_pre_bwd", dh2, sv["x1"], dx, norm_ffn_pre[l:l + 1], sv["mix"],
                norm_mix_post[l:l + 1])
        dz = _mm_nt(tag + "mix_dx", dmix, wo_full, l, D_MODEL, BF16)
        dws["w_o"] = _mm_tn(tag + "mix_dw", sv["z"], dmix, D_MODEL, l, dws["w_o"])
        dya, dyb, dgates, d_bgate = _gate_bwd(tag + "gate_bwd", dz, sv["proj"], b_gate[l:l + 1],
                                              sv["ya"], sv["yb"])
        datt = _narrow_nt(tag + "attn_out_dx", dya, wao_g, l)
        dws["w_attn_out"] = _narrow_tn(tag + "attn_out_dw", sv["att"], dya, l, dws["w_attn_out"])
        dmixed = _narrow_nt(tag + "pool_out_dx", dyb, wpo_g, l)
        dws["w_pool_out"] = _narrow_tn(tag + "pool_out_dw", sv["mixed"], dyb, l, dws["w_pool_out"])
        if l == 0:
            du, d_wg, d_pscale, sib = _pool_bwd(tag + "pool_bwd", dmixed, sv["pooled"], wg_bf[l],
                                                pool_scale[l:l + 1], _reduce_swap(blocks(outs3), 0))
            sums3 = chip_sums(outs3, sib, 0)
            dqkv, dbias, recv = _attn_bwd(
                tag + "attn_bwd", sv["proj"], datt, sv["bias"],
                _both(_reduce_scatter(sums), _reduce_scatter(sums3)))
            final_sums(ffn, sums, recv[:len(ffn)], 0)
            final_sums(outs3, sums3, recv[len(ffn):], 0)
        else:
            du, d_wg, d_pscale = _pool_bwd(tag + "pool_bwd", dmixed, sv["pooled"], wg_bf[l],
                                           pool_scale[l:l + 1])
            dqkv, dbias = _attn_bwd(tag + "attn_bwd", sv["proj"], datt, sv["bias"])
        d_rel = _bias_fold(tag + "bias_fold", dbias)
        if l == 0:
            dh, shared = _proj_dx(tag + "proj_dx", dqkv, du, dgates, win_g, l,
                                  _reduce_share([reds[k] for k in ffn + outs3], 0))
            for k, r in zip(ffn + outs3, shared):
                reds[k] = r
        else:
            dh = _proj_dx(tag + "proj_dx", dqkv, du, dgates, win_g, l)
        dws["w_in"] = _proj_dw(tag + "proj_dw", sv["h"], dqkv, du, dgates, l, dws["w_in"])
        small_grads[l] = [None, d_nmpost, d_nfpre, d_nfpost, d_bgate, d_rel, d_wg, d_pscale,
                          dconv[3:4], dconv[0:3]]
        if l > 0:
            dx, small_grads[l][0], df, d_nfpost = _pre_post_bwd(
                tag + "norm_mix_pre_bwd", dh, sv["x"], dx1, norm_mix_pre[l:l + 1],
                saved[l - 1]["f"], norm_ffn_post[l - 1:l])
        else:
            dx, small_grads[l][0] = _norm_pre_bwd(tag + "norm_mix_pre_bwd", dh, sv["x"], dx1,
                                                  norm_mix_pre[l:l + 1])

    grad_x = dx.reshape(x.shape)

    sib = _comm_call("reduce_swap", _reduce_swap(blocks([0]), 0))
    sums = chip_sums([0], sib, 0)
    recv = _comm_call("reduce_scatter", _reduce_scatter(sums))
    final_sums([0], sums, recv, 0)
    g_big = _comm_call("reduce_share", _reduce_share([reds[0]], 0)) + reds[1:]

    pieces = []
    for idx in range(10):
        pieces.append(jnp.stack([small_grads[0][idx], small_grads[1][idx]]))
    pack = jnp.concatenate([_rows128(p) for p in pieces], axis=0)
    red = _allreduce_small(pack)
    shapes = [p.shape for p in pieces]
    outs = []
    row = 0
    for shp in shapes:
        nrow = math.prod(shp) // 128
        outs.append(red[row:row + nrow].reshape(shp))
        row += nrow
    (g_nmpre, g_nmpost, g_nfpre, g_nfpost, g_bgate, g_rel, g_wg, g_pscale, g_cb, g_cw) = outs
    g_nmpre, g_nmpost, g_nfpre, g_nfpost = [a.reshape(DEPTH, D_MODEL)
                                            for a in (g_nmpre, g_nmpost, g_nfpre, g_nfpost)]
    g_bgate = g_bgate.reshape(DEPTH, 2 * D_MODEL)
    g_rel = g_rel[:, :, :N_REL]
    g_pscale = g_pscale.reshape(DEPTH, POOL_W)
    g_cb = g_cb.reshape(DEPTH, 2 * D_FF)
    ncw = conv_w.shape[2]
    chip = 2 * lax.axis_index("x") + lax.axis_index("y")
    g_cw = lax.dynamic_slice_in_dim(g_cw, chip * ncw, ncw, axis=2)

    grads = dict(norm_mix_pre=g_nmpre, w_in=g_big[0], b_gate=g_bgate, rel_bias=g_rel,
                 w_attn_out=g_big[1], w_pool_group=g_wg, pool_scale=g_pscale, w_pool_out=g_big[2],
                 w_o=g_big[3], norm_mix_post=g_nmpost, norm_ffn_pre=g_nfpre, w_up=g_big[4],
                 conv_w=g_cw, conv_b=g_cb, w_down=g_big[5], norm_ffn_post=g_nfpost)
    weights = dict(norm_mix_pre=norm_mix_pre, w_in=w_in, b_gate=b_gate, rel_bias=rel_bias,
                   w_attn_out=w_attn_out, w_pool_group=w_pool_group, pool_scale=pool_scale,
                   w_pool_out=w_pool_out, w_o=w_o, norm_mix_post=norm_mix_post,
                   norm_ffn_pre=norm_ffn_pre, w_up=w_up, conv_w=conv_w, conv_b=conv_b,
                   w_down=w_down, norm_ffn_post=norm_ffn_post)
    moms = dict(norm_mix_pre=(m_norm_mix_pre, v_norm_mix_pre), w_in=(m_w_in, v_w_in),
                b_gate=(m_b_gate, v_b_gate), rel_bias=(m_rel_bias, v_rel_bias),
                w_attn_out=(m_w_attn_out, v_w_attn_out),
                w_pool_group=(m_w_pool_group, v_w_pool_group),
                pool_scale=(m_pool_scale, v_pool_scale), w_pool_out=(m_w_pool_out, v_w_pool_out),
                w_o=(m_w_o, v_w_o), norm_mix_post=(m_norm_mix_post, v_norm_mix_post),
                norm_ffn_pre=(m_norm_ffn_pre, v_norm_ffn_pre), w_up=(m_w_up, v_w_up),
                conv_w=(m_conv_w, v_conv_w), conv_b=(m_conv_b, v_conv_b),
                w_down=(m_w_down, v_w_down), norm_ffn_post=(m_norm_ffn_post, v_norm_ffn_post))
    order = list(weights.keys())

    delta, new_m, new_v = {}, {}, {}
    small_names = [nm for nm in order if nm not in names]
    for nm in names:
        delta[nm], new_m[nm], new_v[nm] = _adamw("adamw_" + nm, weights[nm], grads[nm], *moms[nm])

    def pack_small(get):
        flat = [get(nm).reshape(-1) for nm in small_names]
        total = sum(f.shape[0] for f in flat)
        padded = -(-total // 1024) * 1024
        flat.append(jnp.zeros((padded - total,), F32))
        return jnp.concatenate(flat).reshape(1, padded // 128, 128)

    d_s, m_s, v_s = _adamw(
        "adamw_small", pack_small(lambda nm: weights[nm]), pack_small(lambda nm: grads[nm]),
        pack_small(lambda nm: moms[nm][0]) , pack_small(lambda nm: moms[nm][1]))
    off = 0
    for nm in small_names:
        size = math.prod(weights[nm].shape)
        for dst, src in ((delta, d_s), (new_m, m_s), (new_v, v_s)):
            dst[nm] = src.reshape(-1)[off:off + size].reshape(weights[nm].shape)
        off += size

    return (loss, grad_x, *[grads[nm] for nm in order], *[delta[nm] for nm in order],
            *[new_m[nm] for nm in order], *[new_v[nm] for nm in order])
```

```python
import functools
import math

import jax
import jax.numpy as jnp
from jax import lax
from jax.experimental import pallas as pl
from jax.experimental.pallas import tpu as pltpu

F32 = jnp.float32
BF16 = jnp.bfloat16
MESH = pl.DeviceIdType.MESH

D_MODEL = 1024
DEPTH = 2
CHUNK = 64
BAND_CHUNKS = 9
BAND = BAND_CHUNKS * CHUNK
HEADS = 8
HEAD_DIM = 64
ATTN_W = HEADS * HEAD_DIM
POOL_WINDOWS = (2, 4, 8, 16)
POOL_W = 512
POOL_GD = 128
MAX_REL = 256
N_REL = 2 * MAX_REL + 1
D_FF = 2816
IN_W = 3 * ATTN_W + POOL_W + 2 * D_MODEL
EPS = 1e-6
ATTN_SCALE = HEAD_DIM ** -0.5
BAND_PAD = 640
BIAS_LANES = BAND_PAD
N_CHIPS = 4

ADAM_LR = 0.001
ADAM_B1 = 0.9
ADAM_B2 = 0.999
ADAM_EPS = 1e-08
ADAM_WD = 0.01
ADAM_STEP = 10

VMEM_LIMIT_V7X = 56 * 1024 * 1024
TOK = 512
ATT_BLK = 8 * CHUNK
FF_COL = 256
FF_TOK = 1024
HALO = 32


def _cparams(*sem):
    return pltpu.CompilerParams(dimension_semantics=sem, vmem_limit_bytes=VMEM_LIMIT_V7X)


def _sds(shape, dtype):
    return jax.ShapeDtypeStruct(shape, dtype)


class _Comm:
    def __init__(self, ins, outs, copies, n_sems, alias=None):
        self.ins, self.outs, self.copies, self.n_sems = list(ins), list(outs), copies, n_sems
        self.alias = dict(alias or {})


class _SemsFrom:
    def __init__(self, sems, start):
        self.sems, self.start = sems, start

    @property
    def at(self):
        return self

    def __getitem__(self, i):
        return self.sems.at[self.start + i]


def _both(a, b):
    na, nao = len(a.ins), len(a.outs)

    def copies(cin, cout, ssem, rsem):
        return (a.copies(cin[:na], cout[:nao], ssem, rsem)
                + b.copies(cin[na:], cout[nao:], _SemsFrom(ssem, a.n_sems), _SemsFrom(rsem, a.n_sems)))

    alias = dict(a.alias)
    alias.update({na + i: nao + o for i, o in b.alias.items()})
    return _Comm(a.ins + b.ins, a.outs + b.outs, copies, a.n_sems + b.n_sems, alias)


def _pcall(body, name, grid, in_specs, out_specs, out_shape, scratch_shapes, compiler_params, args,
           comm=None, aliases=None):
    single = not isinstance(out_shape, (list, tuple))
    out_specs = [out_specs] if single else list(out_specs)
    out_shape = [out_shape] if single else list(out_shape)
    n_in, n_out = len(in_specs), len(out_specs)
    aliases = dict(aliases or {})
    if comm is None:
        res = pl.pallas_call(
            body, name=name, grid=grid, in_specs=list(in_specs), out_specs=out_specs,
            out_shape=out_shape, scratch_shapes=list(scratch_shapes),
            input_output_aliases=aliases, compiler_params=compiler_params)(*args)
        return (res[0] if single else res), None
    ci, co = len(comm.ins), len(comm.outs)

    def hosted(*refs):
        main_in, cin = refs[:n_in], refs[n_in:n_in + ci]
        main_out = refs[n_in + ci:n_in + ci + n_out]
        cout = refs[n_in + ci + n_out:n_in + ci + n_out + co]
        rest = refs[n_in + ci + n_out + co:]
        copies = comm.copies(cin, cout, rest[-2], rest[-1])
        ids = [pl.program_id(a) for a in range(len(grid))]
        first = functools.reduce(jnp.logical_and, [i == 0 for i in ids])
        last = functools.reduce(jnp.logical_and, [i == g - 1 for i, g in zip(ids, grid)])

        @pl.when(first)
        def _():
            for cp in copies:
                cp.start()

        body(*main_in, *main_out, *rest[:-2])

        @pl.when(last)
        def _():
            for cp in copies:
                cp.wait()

    for i, o in comm.alias.items():
        aliases[n_in + i] = n_out + o
    hbm = pl.BlockSpec(memory_space=pl.ANY)
    sems = pltpu.SemaphoreType.DMA((comm.n_sems,))
    res = pl.pallas_call(
        hosted, name=name, grid=grid, in_specs=list(in_specs) + [hbm] * ci,
        out_specs=out_specs + [hbm] * co, out_shape=out_shape + comm.outs,
        scratch_shapes=list(scratch_shapes) + [sems, sems],
        input_output_aliases=aliases, compiler_params=compiler_params)(*args, *comm.ins)
    return (res[0] if single else list(res[:n_out])), list(res[n_out:])


def _comm_call(name, comm):
    ci = len(comm.ins)

    def body(*refs):
        copies = comm.copies(refs[:ci], refs[ci:-2], refs[-2], refs[-1])
        for cp in copies:
            cp.start()
        for cp in copies:
            cp.wait()

    hbm = pl.BlockSpec(memory_space=pl.ANY)
    sems = pltpu.SemaphoreType.DMA((comm.n_sems,))
    return list(pl.pallas_call(
        body, name=name, in_specs=[hbm] * ci, out_specs=[hbm] * len(comm.outs),
        out_shape=comm.outs, scratch_shapes=[sems, sems],
        input_output_aliases=comm.alias)(*comm.ins))


def _matmul(name, a, b, a_spec, b_spec, o_spec, out_shape, grid, contract, nk, acc_shape,
            fill=None, comm=None):
    def body(*refs):
        a_ref, b_ref = refs[0], refs[1]
        o_ref = refs[2 if fill is None else 3]
        scratch = refs[(3 if fill is None else 4):]
        part = lax.dot_general(a_ref[...], b_ref[...], (contract, ((), ())),
                               preferred_element_type=F32)
        if nk == 1:
            o_ref[...] = part.astype(o_ref.dtype)
        else:
            acc_ref = scratch[0]
            k = pl.program_id(2)

            @pl.when(k == 0)
            def _():
                acc_ref[...] = part

            @pl.when(k > 0)
            def _():
                acc_ref[...] += part

            @pl.when(k == nk - 1)
            def _():
                o_ref[...] = acc_ref[...].astype(o_ref.dtype)

    scratch = [] if nk == 1 else [pltpu.VMEM(acc_shape, F32)]
    in_specs, args, aliases = [a_spec, b_spec], [a, b], {}
    if fill is not None:
        in_specs.append(pl.BlockSpec(memory_space=pl.ANY))
        args.append(fill)
        aliases = {2: 0}
    out, moved = _pcall(body, name, grid, in_specs, o_spec, out_shape, scratch,
                        _cparams("parallel", "parallel", "arbitrary"), args, comm, aliases)
    return out if comm is None else (out, moved)


NN = ((1,), (0,))
NT = ((1,), (1,))
TN = ((0,), (0,))


def _tm(t):
    return min(t, 1024)


def _col_block_spec(a, rows, nb, row_col):
    if a.ndim == 2:
        return pl.BlockSpec((rows, nb), row_col)

    def halves(*ids):
        r, c = row_col(*ids)
        return c // 2, r, c % 2

    return pl.BlockSpec((None, rows, nb), halves)


def _mm_nn_blocked(name, a, w, l, out_dtype):
    t, k = a.shape
    nb = w.shape[3]
    tm = _tm(t)
    return _matmul(
        name, a, w,
        pl.BlockSpec((tm, k), lambda i, n, kk: (i, 0)),
        pl.BlockSpec((None, None, k, nb), lambda i, n, kk: (l, n, 0, 0)),
        pl.BlockSpec((tm, nb), lambda i, n, kk: (i, n)),
        _sds((t, N_CHIPS * nb), out_dtype), (t // tm, N_CHIPS, 1), NN, 1, None)


def _mm_nt_blocked(name, a, w, l, out_dtype, comm=None):
    t = a.shape[-2]
    k, nb = w.shape[2], w.shape[3]
    tm = _tm(t)
    return _matmul(
        name, a, w,
        _col_block_spec(a, tm, nb, lambda i, n, kk: (i, kk)),
        pl.BlockSpec((None, None, k, nb), lambda i, n, kk: (l, kk, 0, 0)),
        pl.BlockSpec((tm, k), lambda i, n, kk: (i, 0)),
        _sds((t, k), out_dtype), (t // tm, 1, N_CHIPS), NT, N_CHIPS, (tm, k), comm=comm)


def _mm_tn_blocked(name, a, g, l, fill):
    t, k = a.shape
    nb = g.shape[-1] * (g.ndim - 1) // N_CHIPS
    tt = _tm(t)
    nt = t // tt
    return _matmul(
        name, a, g,
        pl.BlockSpec((tt, k), lambda n, j, kk: (kk, 0)),
        _col_block_spec(g, tt, nb, lambda n, j, kk: (kk, n)),
        pl.BlockSpec((None, None, k, nb), lambda n, j, kk: (l, n, 0, 0)),
        _sds((DEPTH, N_CHIPS, k, nb), BF16), (N_CHIPS, 1, nt), TN, nt, (k, nb), fill)


def _proj_pieces(rows, dqkv_first):
    def piece(col):
        if dqkv_first:
            return pl.BlockSpec((rows, ATTN_W), lambda i, kk: (i, col))
        return pl.BlockSpec((rows, ATTN_W), lambda n, kk: (kk, col))
    return [piece(0), piece(1), piece(2), piece(0)]


def _proj_dx(name, dqkv, du, dgates, w, l, comm=None):
    t = du.shape[0]
    k, nb = w.shape[2], w.shape[3]
    tm = _tm(t)

    def body(dq_ref, dk_ref, dv_ref, du_ref, dg_ref, w_ref, o_ref, acc_ref):
        kk = pl.program_id(1)

        def mm(a):
            return lax.dot_general(a, w_ref[...], (NT, ((), ())), preferred_element_type=F32)

        @pl.when(kk == 0)
        def _():
            acc_ref[...] = mm(jnp.concatenate([dq_ref[...], dk_ref[...]], axis=1))

        @pl.when(kk == 1)
        def _():
            acc_ref[...] += mm(jnp.concatenate([dv_ref[...], du_ref[...]], axis=1))

        @pl.when(kk >= 2)
        def _():
            acc_ref[...] += mm(dg_ref[...])

        @pl.when(kk == N_CHIPS - 1)
        def _():
            o_ref[...] = acc_ref[...]

    out, moved = _pcall(
        body, name, (t // tm, N_CHIPS),
        _proj_pieces(tm, True)
        + [pl.BlockSpec((tm, nb), lambda i, kk: (i, jnp.maximum(kk - 2, 0))),
           pl.BlockSpec((None, None, k, nb), lambda i, kk: (l, kk, 0, 0))],
        pl.BlockSpec((tm, k), lambda i, kk: (i, 0)), _sds((t, k), F32),
        [pltpu.VMEM((tm, k), F32)], _cparams("arbitrary", "arbitrary"),
        (dqkv, dqkv, dqkv, du, dgates, w), comm)
    return out if comm is None else (out, moved)


def _proj_dw(name, h, dqkv, du, dgates, l, fill):
    t, k = h.shape
    nb = dgates.shape[1] // 2
    tt = _tm(t)
    nt = t // tt

    def body(*refs):
        h_ref, dq_ref, dk_ref, dv_ref, du_ref, dg_ref = refs[:6]
        o_ref, acc_ref = refs[-2], refs[-1]
        n, kk = pl.program_id(0), pl.program_id(1)

        def update(g):
            part = lax.dot_general(h_ref[...], g, (TN, ((), ())), preferred_element_type=F32)

            @pl.when(kk == 0)
            def _():
                acc_ref[...] = part

            @pl.when(kk > 0)
            def _():
                acc_ref[...] += part

        @pl.when(n == 0)
        def _():
            update(jnp.concatenate([dq_ref[...], dk_ref[...]], axis=1))

        @pl.when(n == 1)
        def _():
            update(jnp.concatenate([dv_ref[...], du_ref[...]], axis=1))

        @pl.when(n >= 2)
        def _():
            update(dg_ref[...])

        @pl.when(kk == nt - 1)
        def _():
            o_ref[...] = acc_ref[...].astype(BF16)

    in_specs = ([pl.BlockSpec((tt, k), lambda n, kk: (kk, 0))] + _proj_pieces(tt, False)
                + [pl.BlockSpec((tt, nb), lambda n, kk: (kk, jnp.maximum(n - 2, 0)))])
    args, aliases = [h, dqkv, dqkv, dqkv, du, dgates], {}
    if fill is not None:
        in_specs.append(pl.BlockSpec(memory_space=pl.ANY))
        args.append(fill)
        aliases = {6: 0}
    return pl.pallas_call(
        body, name=name, grid=(N_CHIPS, nt), in_specs=in_specs,
        out_specs=pl.BlockSpec((None, None, k, nb), lambda n, kk: (l, n, 0, 0)),
        out_shape=_sds((DEPTH, N_CHIPS, k, nb), BF16),
        scratch_shapes=[pltpu.VMEM((k, nb), F32)], input_output_aliases=aliases,
        compiler_params=_cparams("parallel", "arbitrary"))(*args)


def _narrow_nn(name, a, w, l):
    t, k = a.shape
    nb = w.shape[3]
    tm = _tm(t)

    def body(a_ref, w_ref, o_ref):
        av = a_ref[...]
        for j in range(N_CHIPS):
            o_ref[:, j * nb:(j + 1) * nb] = jnp.dot(
                av, w_ref[j], preferred_element_type=F32).astype(BF16)

    return pl.pallas_call(
        body, name=name, grid=(t // tm,),
        in_specs=[pl.BlockSpec((tm, k), lambda i: (i, 0)),
                  pl.BlockSpec((None, N_CHIPS, k, nb), lambda i: (l, 0, 0, 0))],
        out_specs=pl.BlockSpec((tm, N_CHIPS * nb), lambda i: (i, 0)),
        out_shape=_sds((t, N_CHIPS * nb), BF16), compiler_params=_cparams("parallel"))(a, w)


def _narrow_nt(name, a, w, l):
    t = a.shape[0]
    k, nb = w.shape[2], w.shape[3]
    tm = _tm(t)

    def body(a_ref, w_ref, o_ref):
        acc = lax.dot_general(a_ref[:, 0:nb], w_ref[0], (NT, ((), ())), preferred_element_type=F32)
        for j in range(1, N_CHIPS):
            acc = acc + lax.dot_general(a_ref[:, j * nb:(j + 1) * nb], w_ref[j], (NT, ((), ())),
                                        preferred_element_type=F32)
        o_ref[...] = acc.astype(BF16)

    return pl.pallas_call(
        body, name=name, grid=(t // tm,),
        in_specs=[pl.BlockSpec((tm, N_CHIPS * nb), lambda i: (i, 0)),
                  pl.BlockSpec((None, N_CHIPS, k, nb), lambda i: (l, 0, 0, 0))],
        out_specs=pl.BlockSpec((tm, k), lambda i: (i, 0)),
        out_shape=_sds((t, k), BF16), compiler_params=_cparams("parallel"))(a, w)


def _narrow_tn(name, a, g, l, fill):
    t, k = a.shape
    nb = g.shape[1] // N_CHIPS
    tt = _tm(t)
    nt = t // tt

    def body(*refs):
        a_ref, g_ref, o_ref, acc_ref = refs[0], refs[1], refs[-2], refs[-1]
        i = pl.program_id(0)
        part = lax.dot_general(a_ref[...], g_ref[...], (TN, ((), ())), preferred_element_type=F32)

        @pl.when(i == 0)
        def _():
            acc_ref[...] = part

        @pl.when(i > 0)
        def _():
            acc_ref[...] += part

        @pl.when(i == nt - 1)
        def _():
            for j in range(N_CHIPS):
                o_ref[j] = acc_ref[:, j * nb:(j + 1) * nb].astype(BF16)

    in_specs = [pl.BlockSpec((tt, k), lambda i: (i, 0)),
                pl.BlockSpec((tt, N_CHIPS * nb), lambda i: (i, 0))]
    args, aliases = [a, g], {}
    if fill is not None:
        in_specs.append(pl.BlockSpec(memory_space=pl.ANY))
        args.append(fill)
        aliases = {2: 0}
    return pl.pallas_call(
        body, name=name, grid=(nt,), in_specs=in_specs,
        out_specs=pl.BlockSpec((None, N_CHIPS, k, nb), lambda i: (l, 0, 0, 0)),
        out_shape=_sds((DEPTH, N_CHIPS, k, nb), BF16),
        scratch_shapes=[pltpu.VMEM((k, N_CHIPS * nb), F32)], input_output_aliases=aliases,
        compiler_params=_cparams("arbitrary"))(*args)


def _mm_nn(name, a, w, l, tk, out_dtype):
    t, k = a.shape
    n = w.shape[2]
    tm = _tm(t)
    nk = k // tk
    return _matmul(
        name, a, w,
        pl.BlockSpec((tm, tk), lambda i, j, kk: (i, kk)),
        pl.BlockSpec((None, tk, n), lambda i, j, kk: (l, kk, 0)),
        pl.BlockSpec((tm, n), lambda i, j, kk: (i, 0)),
        _sds((t, n), out_dtype), (t // tm, 1, nk), NN, nk, (tm, n))


def _mm_nt(name, a, w, l, tn, out_dtype, comm=None):
    t, n = a.shape
    k = w.shape[1]
    tm = _tm(t)
    return _matmul(
        name, a, w,
        pl.BlockSpec((tm, n), lambda i, j, kk: (i, 0)),
        pl.BlockSpec((None, tn, n), lambda i, j, kk: (l, j, 0)),
        pl.BlockSpec((tm, tn), lambda i, j, kk: (i, j)),
        _sds((t, k), out_dtype), (t // tm, k // tn, 1), NT, 1, None, comm=comm)


def _mm_tn(name, a, g, tko, l, fill):
    t, k = a.shape
    n = g.shape[1]
    tt = _tm(t)
    nt = t // tt
    return _matmul(
        name, a, g,
        pl.BlockSpec((tt, tko), lambda i, j, kk: (kk, i)),
        pl.BlockSpec((tt, n), lambda i, j, kk: (kk, 0)),
        pl.BlockSpec((None, tko, n), lambda i, j, kk: (l, i, 0)),
        _sds((DEPTH, k, n), BF16), (k // tko, 1, nt), TN, nt, (tko, n), fill)


def _row_spec(width, col=0):
    return pl.BlockSpec((TOK, width), lambda i: (i, col))


def _vec_spec(width):
    return pl.BlockSpec((1, width), lambda i: (0, 0))


def _rms(x):
    return lax.rsqrt(jnp.mean(x * x, axis=-1, keepdims=True) + EPS)


def _norm_fwd(name, x, g):
    t = x.shape[0]

    def body(x_ref, g_ref, h_ref):
        xv = x_ref[...]
        h_ref[...] = (xv * _rms(xv) * g_ref[...]).astype(BF16)

    return pl.pallas_call(
        body, name=name, grid=(t // TOK,), in_specs=[_row_spec(D_MODEL), _vec_spec(D_MODEL)],
        out_specs=_row_spec(D_MODEL), out_shape=_sds((t, D_MODEL), BF16),
        compiler_params=_cparams("parallel"))(x, g)


ROWS = 16
ROW_UNROLL = 8


def _rows(k):
    return pl.ds(pl.multiple_of(k * ROWS, ROWS), ROWS)


def _strips(step, init):
    def group(j, carry):
        for u in range(ROW_UNROLL):
            carry = step(j * ROW_UNROLL + u, carry)
        return carry

    return lax.fori_loop(0, TOK // (ROWS * ROW_UNROLL), group, init)


def _fold_rows(x):
    return x[0:8] + x[8:16]


def _accumulate(ref, part):
    total = jnp.sum(part, axis=0, keepdims=True)

    @pl.when(pl.program_id(0) == 0)
    def _():
        ref[...] = total

    @pl.when(pl.program_id(0) > 0)
    def _():
        ref[...] += total


def _norm_bwd_rows(d, mv, g):
    r = _rms(mv)
    n = mv * r
    dn = d * g
    return r * (dn - n * jnp.mean(dn * n, axis=-1, keepdims=True)), d * n


def _post_pre_fwd(name, xres, m, g_post, g_pre, comm=None):
    t = xres.shape[0]

    def body(x_ref, m_ref, gp_ref, gn_ref, x1_ref, h_ref):
        def strip(k, c):
            rows = _rows(k)
            mv = m_ref[rows, :]
            x1 = x_ref[rows, :] + mv * _rms(mv) * gp_ref[...]
            x1_ref[rows, :] = x1
            h_ref[rows, :] = (x1 * _rms(x1) * gn_ref[...]).astype(BF16)
            return c

        _strips(strip, 0)

    outs, moved = _pcall(
        body, name, (t // TOK,),
        [_row_spec(D_MODEL), _row_spec(D_MODEL), _vec_spec(D_MODEL), _vec_spec(D_MODEL)],
        [_row_spec(D_MODEL), _row_spec(D_MODEL)],
        [_sds((t, D_MODEL), F32), _sds((t, D_MODEL), BF16)], [], _cparams("arbitrary"),
        (xres, m, g_post, g_pre), comm)
    return outs if comm is None else (*outs, moved)


def _tail(name, xres, m, g_post, target):
    t = xres.shape[0]

    def body(x_ref, m_ref, g_ref, t_ref, dy_ref, dm_ref, dg_ref, l_ref):
        def strip(k, carry):
            rows = _rows(k)
            mv = m_ref[rows, :]
            e = x_ref[rows, :] + mv * _rms(mv) * g_ref[...] - t_ref[rows, :]
            dy = e * (1.0 / D_MODEL)
            dy_ref[rows, :] = dy
            dm, dgn = _norm_bwd_rows(dy, mv, g_ref[...])
            dm_ref[rows, :] = dm.astype(BF16)
            return carry[0] + _fold_rows(dgn), carry[1] + _fold_rows(e * e)

        zero = jnp.zeros((8, D_MODEL), F32)
        dg, sq = _strips(strip, (zero, zero))
        _accumulate(dg_ref, dg)
        _accumulate(l_ref, jnp.sum(sq, axis=1, keepdims=True))

    dy, dm, dg, sq = pl.pallas_call(
        body, name=name, grid=(t // TOK,),
        in_specs=[_row_spec(D_MODEL), _row_spec(D_MODEL), _vec_spec(D_MODEL), _row_spec(D_MODEL)],
        out_specs=[_row_spec(D_MODEL), _row_spec(D_MODEL), _vec_spec(D_MODEL),
                   pl.BlockSpec((1, 1), lambda i: (0, 0))],
        out_shape=[_sds((t, D_MODEL), F32), _sds((t, D_MODEL), BF16), _sds((1, D_MODEL), F32),
                   _sds((1, 1), F32)],
        compiler_params=_cparams("arbitrary"))(xres, m, g_post, target)
    return dy, dm, dg, sq[0, 0] * (0.5 / D_MODEL)


def _pre_post_bwd(name, dh, xin, dxo, g_pre, m, g_post, comm=None):
    t = dh.shape[0]

    def body(dh_ref, x_ref, d_ref, gq_ref, m_ref, gp_ref, dx_ref, dgq_ref, dm_ref, dgp_ref):
        def strip(k, carry):
            rows = _rows(k)
            dxin, dgq = _norm_bwd_rows(dh_ref[rows, :], x_ref[rows, :], gq_ref[...])
            dx = d_ref[rows, :] + dxin
            dx_ref[rows, :] = dx
            dm, dgp = _norm_bwd_rows(dx, m_ref[rows, :], gp_ref[...])
            dm_ref[rows, :] = dm.astype(BF16)
            return carry[0] + _fold_rows(dgq), carry[1] + _fold_rows(dgp)

        zero = jnp.zeros((8, D_MODEL), F32)
        dgq, dgp = _strips(strip, (zero, zero))
        _accumulate(dgq_ref, dgq)
        _accumulate(dgp_ref, dgp)

    outs, moved = _pcall(
        body, name, (t // TOK,),
        [_row_spec(D_MODEL), _row_spec(D_MODEL), _row_spec(D_MODEL), _vec_spec(D_MODEL),
         _row_spec(D_MODEL), _vec_spec(D_MODEL)],
        [_row_spec(D_MODEL), _vec_spec(D_MODEL), _row_spec(D_MODEL), _vec_spec(D_MODEL)],
        [_sds((t, D_MODEL), F32), _sds((1, D_MODEL), F32), _sds((t, D_MODEL), BF16),
         _sds((1, D_MODEL), F32)], [], _cparams("arbitrary"),
        (dh, xin, dxo, g_pre, m, g_post), comm)
    return outs if comm is None else (*outs, moved)


def _norm_pre_bwd(name, dh, xin, dxo, g, comm=None):
    t = dh.shape[0]

    def body(dh_ref, x_ref, d_ref, g_ref, dx_ref, dg_ref):
        xv = x_ref[...]
        dhv = dh_ref[...]
        r = _rms(xv)
        n = xv * r
        dn = dhv * g_ref[...]
        dx_ref[...] = d_ref[...] + r * (dn - n * jnp.mean(dn * n, axis=-1, keepdims=True))
        part = jnp.sum(dhv * n, axis=0, keepdims=True)

        @pl.when(pl.program_id(0) == 0)
        def _():
            dg_ref[...] = part

        @pl.when(pl.program_id(0) > 0)
        def _():
            dg_ref[...] += part

    out, moved = _pcall(
        body, name, (t // TOK,),
        [_row_spec(D_MODEL), _row_spec(D_MODEL), _row_spec(D_MODEL), _vec_spec(D_MODEL)],
        [_row_spec(D_MODEL), _vec_spec(D_MODEL)],
        [_sds((t, D_MODEL), F32), _sds((1, D_MODEL), F32)], [], _cparams("arbitrary"),
        (dh, xin, dxo, g), comm)
    return out if comm is None else (*out, moved)


def _gate_fwd(name, proj, b_gate, ya, yb):
    t = proj.shape[0]

    def body(ga_ref, gb_ref, b_ref, ya_ref, yb_ref, z_ref):
        sa = jax.nn.sigmoid(ga_ref[...].astype(F32) + b_ref[:, :D_MODEL])
        sb = jax.nn.sigmoid(gb_ref[...].astype(F32) + b_ref[:, D_MODEL:])
        z_ref[...] = (sa * ya_ref[...].astype(F32) + sb * yb_ref[...].astype(F32)).astype(BF16)

    return pl.pallas_call(
        body, name=name, grid=(t // TOK,),
        in_specs=[_row_spec(D_MODEL, 2), _row_spec(D_MODEL, 3), _vec_spec(2 * D_MODEL),
                  _row_spec(D_MODEL), _row_spec(D_MODEL)],
        out_specs=_row_spec(D_MODEL), out_shape=_sds((t, D_MODEL), BF16),
        compiler_params=_cparams("parallel"))(proj, proj, b_gate, ya, yb)


def _gate_bwd(name, dz, proj, b_gate, ya, yb):
    t = proj.shape[0]

    def body(dz_ref, ga_ref, gb_ref, b_ref, ya_ref, yb_ref, dya_ref, dyb_ref, dg_ref, db_ref):
        dzv = dz_ref[...].astype(F32)
        sa = jax.nn.sigmoid(ga_ref[...].astype(F32) + b_ref[:, :D_MODEL])
        sb = jax.nn.sigmoid(gb_ref[...].astype(F32) + b_ref[:, D_MODEL:])
        dya_ref[...] = (dzv * sa).astype(BF16)
        dyb_ref[...] = (dzv * sb).astype(BF16)
        dga = dzv * ya_ref[...].astype(F32) * sa * (1.0 - sa)
        dgb = dzv * yb_ref[...].astype(F32) * sb * (1.0 - sb)
        dg_ref[:, :D_MODEL] = dga.astype(BF16)
        dg_ref[:, D_MODEL:] = dgb.astype(BF16)
        pa = jnp.sum(dga, axis=0, keepdims=True)
        pb = jnp.sum(dgb, axis=0, keepdims=True)

        @pl.when(pl.program_id(0) == 0)
        def _():
            db_ref[:, :D_MODEL] = pa
            db_ref[:, D_MODEL:] = pb

        @pl.when(pl.program_id(0) > 0)
        def _():
            db_ref[:, :D_MODEL] += pa
            db_ref[:, D_MODEL:] += pb

    return pl.pallas_call(
        body, name=name, grid=(t // TOK,),
        in_specs=[_row_spec(D_MODEL), _row_spec(D_MODEL, 2), _row_spec(D_MODEL, 3),
                  _vec_spec(2 * D_MODEL), _row_spec(D_MODEL), _row_spec(D_MODEL)],
        out_specs=[_row_spec(D_MODEL), _row_spec(D_MODEL), _row_spec(2 * D_MODEL),
                   _vec_spec(2 * D_MODEL)],
        out_shape=[_sds((t, D_MODEL), BF16), _sds((t, D_MODEL), BF16),
                   _sds((t, 2 * D_MODEL), BF16), _sds((1, 2 * D_MODEL), F32)],
        compiler_params=_cparams("arbitrary"))(dz, proj, proj, b_gate, ya, yb)


def _head_masks():
    lane = lax.broadcasted_iota(jnp.int32, (1, 2 * HEAD_DIM), 1)
    return lane < HEAD_DIM


BAND_ROWS = 2 * ATT_BLK + CHUNK


def _fill_band(band, prev_ref, cur_ref):
    band[0:ATT_BLK, :] = prev_ref[...]
    band[ATT_BLK:2 * ATT_BLK, :] = cur_ref[...]
    band[2 * ATT_BLK:, :] = jnp.zeros((CHUNK, ATTN_W), BF16)


def _pair_rows(x2, low):
    zero = jnp.zeros_like(x2)
    return jnp.concatenate([jnp.where(low, x2, zero), jnp.where(low, zero, x2)], axis=0)


def _pair_diag(o2, low):
    return jnp.where(low, o2[0:CHUNK, :], o2[CHUNK:, :])


N_PAIRS = HEADS // 2
SM_STRIP = 32
N_STRIPS = BAND_PAD // SM_STRIP
NEG = -1e30


def _fold8(x, op):
    return op(op(x[0:8], x[8:16]), op(x[16:24], x[24:32]))


def _strip(k):
    return pl.ds(pl.multiple_of(k * SM_STRIP, SM_STRIP), SM_STRIP)


def _band_probs(k2, qcat, bias_t, first_key):
    kpos = lax.broadcasted_iota(jnp.int32, (BAND_PAD, 1), 0)
    st = lax.dot_general(k2, qcat, (NT, ((), ())), preferred_element_type=F32)
    st = jnp.where(kpos + first_key >= 0, st + bias_t, NEG)
    e = jnp.exp(st - jnp.max(st, axis=0, keepdims=True))
    return e * (1.0 / jnp.sum(e, axis=0, keepdims=True))


def _band_softmax_stats(st_ref, b_ref, first_key, dp_ref):
    rowi = lax.broadcasted_iota(jnp.int32, (SM_STRIP, 128), 0)

    def scores(k, mx):
        rows = _strip(k)
        live = (rowi + (k * SM_STRIP + first_key)) >= 0
        out = []
        for hp in range(N_PAIRS):
            x = jnp.where(live, st_ref[hp, rows, :] + b_ref[hp, rows, :], NEG)
            st_ref[hp, rows, :] = x
            out.append(jnp.maximum(mx[hp], _fold8(x, jnp.maximum)))
        return tuple(out)

    mx = lax.fori_loop(0, N_STRIPS, scores, (jnp.full((8, 128), NEG, F32),) * N_PAIRS, unroll=2)
    top = [jnp.max(m, axis=0, keepdims=True) for m in mx]

    def sums(k, acc):
        rows = _strip(k)
        ls, eds = [], []
        for hp in range(N_PAIRS):
            e = jnp.exp(st_ref[hp, rows, :] - top[hp])
            ls.append(acc[hp] + _fold8(e, jnp.add))
            eds.append(acc[N_PAIRS + hp] + _fold8(e * dp_ref[hp, rows, :], jnp.add))
        return tuple(ls + eds)

    acc = lax.fori_loop(0, N_STRIPS, sums, (jnp.zeros((8, 128), F32),) * (2 * N_PAIRS), unroll=2)
    inv = [1.0 / jnp.sum(a, axis=0, keepdims=True) for a in acc[:N_PAIRS]]
    delta = [jnp.sum(a, axis=0, keepdims=True) * i for a, i in zip(acc[N_PAIRS:], inv)]
    return top, inv, delta


def _attn_specs(nblk):
    cur = lambda col: pl.BlockSpec((ATT_BLK, ATTN_W), lambda s: (jnp.minimum(s, nblk - 1), col))
    prev = lambda col: pl.BlockSpec(
        (ATT_BLK, ATTN_W), lambda s: (jnp.maximum(jnp.minimum(s, nblk - 1) - 1, 0), col))
    return cur, prev


def _attn_fwd(name, proj, bias, comm=None):
    t = proj.shape[0]
    nblk = t // ATT_BLK
    cur, prev = _attn_specs(nblk)

    def body(q_ref, kp_ref, kc_ref, vp_ref, vc_ref, b_ref, o_ref, kband, vband):
        s = pl.program_id(0)
        _fill_band(kband, kp_ref, kc_ref)
        _fill_band(vband, vp_ref, vc_ref)
        low = _head_masks()

        def chunk(ci, carry):
            r0 = pl.multiple_of(ci * CHUNK, CHUNK)
            for hp in range(N_PAIRS):
                cols = slice(hp * 128, (hp + 1) * 128)
                qcat = _pair_rows(q_ref[pl.ds(r0, CHUNK), cols] * ATTN_SCALE, low)
                p = _band_probs(kband[pl.ds(r0, BAND_PAD), cols], qcat, b_ref[hp],
                                (s * 8 - 8 + ci) * CHUNK)
                o2 = lax.dot_general(p.astype(BF16), vband[pl.ds(r0, BAND_PAD), cols],
                                     (TN, ((), ())), preferred_element_type=F32)
                o_ref[pl.ds(r0, CHUNK), cols] = _pair_diag(o2, low).astype(BF16)
            return carry

        lax.fori_loop(0, 8, chunk, 0)

    out, moved = _pcall(
        body, name, (nblk,),
        [cur(0), prev(1), cur(1), prev(2), cur(2),
         pl.BlockSpec((N_PAIRS, BAND_PAD, 128), lambda s: (0, 0, 0))],
        pl.BlockSpec((ATT_BLK, ATTN_W), lambda s: (s, 0)), _sds((t, ATTN_W), BF16),
        [pltpu.VMEM((BAND_ROWS, ATTN_W), BF16), pltpu.VMEM((BAND_ROWS, ATTN_W), BF16)],
        _cparams("arbitrary"), (proj, proj, proj, proj, proj, bias), comm)
    return out if comm is None else (out, moved)


def _attn_bwd(name, proj, datt, bias, comm=None):
    t = proj.shape[0]
    nblk = t // ATT_BLK
    cur, prev = _attn_specs(nblk)
    late = pl.BlockSpec((ATT_BLK, 3 * ATTN_W), lambda s: (jnp.maximum(s - 1, 0), 0))

    def body(q_ref, kp_ref, kc_ref, vp_ref, vc_ref, do_ref, b_ref,
             dqkv_ref, db_ref, kband, vband, dkacc, dvacc,
             st_ref, dp_ref, pb_ref, dsb_ref, qc_ref, dc_ref, dq_ref, dq_held):
        s = pl.program_id(0)

        @pl.when(s == 0)
        def _():
            dkacc[...] = jnp.zeros_like(dkacc)
            dvacc[...] = jnp.zeros_like(dvacc)
            db_ref[...] = jnp.zeros_like(db_ref)
            dq_ref[...] = jnp.zeros_like(dq_ref)

        @pl.when(s < nblk)
        def _():
            _fill_band(kband, kp_ref, kc_ref)
            _fill_band(vband, vp_ref, vc_ref)
            low = _head_masks()

            def chunk(ci, carry):
                r0 = pl.multiple_of(ci * CHUNK, CHUNK)
                for hp in range(N_PAIRS):
                    cols = slice(hp * 128, (hp + 1) * 128)
                    qc_ref[hp] = _pair_rows(q_ref[pl.ds(r0, CHUNK), cols] * ATTN_SCALE, low)
                    dc_ref[hp] = _pair_rows(do_ref[pl.ds(r0, CHUNK), cols], low)
                    st_ref[hp] = lax.dot_general(kband[pl.ds(r0, BAND_PAD), cols], qc_ref[hp],
                                                 (NT, ((), ())), preferred_element_type=F32)
                    dp_ref[hp] = lax.dot_general(vband[pl.ds(r0, BAND_PAD), cols], dc_ref[hp],
                                                 (NT, ((), ())), preferred_element_type=F32)
                top, inv, delta = _band_softmax_stats(st_ref, b_ref, (s * 8 - 8 + ci) * CHUNK,
                                                      dp_ref)

                def grads(k, c):
                    rows = _strip(k)
                    for hp in range(N_PAIRS):
                        p = jnp.exp(st_ref[hp, rows, :] - top[hp]) * inv[hp]
                        ds = p * (dp_ref[hp, rows, :] - delta[hp])
                        db_ref[hp, rows, :] += ds
                        dsb_ref[hp, rows, :] = ds.astype(BF16)
                        pb_ref[hp, rows, :] = p.astype(BF16)
                    return c

                lax.fori_loop(0, N_STRIPS, grads, 0, unroll=2)
                for hp in range(N_PAIRS):
                    cols = slice(hp * 128, (hp + 1) * 128)
                    dq2 = lax.dot_general(dsb_ref[hp], kband[pl.ds(r0, BAND_PAD), cols],
                                          (TN, ((), ())), preferred_element_type=F32)
                    dq_ref[pl.ds(r0, CHUNK), cols] = (_pair_diag(dq2, low) * ATTN_SCALE).astype(BF16)
                    dkacc[pl.ds(r0, BAND_PAD), cols] += jnp.dot(dsb_ref[hp], qc_ref[hp],
                                                               preferred_element_type=F32)
                    dvacc[pl.ds(r0, BAND_PAD), cols] += jnp.dot(pb_ref[hp], dc_ref[hp],
                                                               preferred_element_type=F32)
                return carry

            dq_held[...] = dq_ref[...]
            lax.fori_loop(0, 8, chunk, 0)

        @pl.when(s == nblk)
        def _():
            dq_held[...] = dq_ref[...]

        dqkv_ref[:, 0:ATTN_W] = dq_held[...]
        dqkv_ref[:, ATTN_W:2 * ATTN_W] = dkacc[0:ATT_BLK, :].astype(BF16)
        dqkv_ref[:, 2 * ATTN_W:] = dvacc[0:ATT_BLK, :].astype(BF16)
        dkacc[0:ATT_BLK, :] = dkacc[ATT_BLK:2 * ATT_BLK, :]
        dvacc[0:ATT_BLK, :] = dvacc[ATT_BLK:2 * ATT_BLK, :]
        dkacc[ATT_BLK:, :] = jnp.zeros((ATT_BLK + CHUNK, ATTN_W), F32)
        dvacc[ATT_BLK:, :] = jnp.zeros((ATT_BLK + CHUNK, ATTN_W), F32)

    outs, moved = _pcall(
        body, name, (nblk + 1,),
        [cur(0), prev(1), cur(1), prev(2), cur(2),
         pl.BlockSpec((ATT_BLK, ATTN_W), lambda s: (jnp.minimum(s, nblk - 1), 0)),
         pl.BlockSpec((HEADS // 2, BAND_PAD, 128), lambda s: (0, 0, 0))],
        [late, pl.BlockSpec((HEADS // 2, BAND_PAD, 128), lambda s: (0, 0, 0))],
        [_sds((t, 3 * ATTN_W), BF16), _sds((HEADS // 2, BAND_PAD, 128), F32)],
        [pltpu.VMEM((BAND_ROWS, ATTN_W), BF16), pltpu.VMEM((BAND_ROWS, ATTN_W), BF16),
         pltpu.VMEM((BAND_ROWS, ATTN_W), F32), pltpu.VMEM((BAND_ROWS, ATTN_W), F32),
         pltpu.VMEM((N_PAIRS, BAND_PAD, 128), F32), pltpu.VMEM((N_PAIRS, BAND_PAD, 128), F32),
         pltpu.VMEM((N_PAIRS, BAND_PAD, 128), BF16), pltpu.VMEM((N_PAIRS, BAND_PAD, 128), BF16),
         pltpu.VMEM((N_PAIRS, 2 * CHUNK, 128), BF16), pltpu.VMEM((N_PAIRS, 2 * CHUNK, 128), BF16),
         pltpu.VMEM((ATT_BLK, ATTN_W), BF16), pltpu.VMEM((ATT_BLK, ATTN_W), BF16)],
        _cparams("arbitrary"), (proj, proj, proj, proj, proj, datt, bias), comm)
    return outs if comm is None else (*outs, moved)


def _diag_onehot(rel_rows):
    d0 = lax.broadcasted_iota(jnp.int32, (BIAS_LANES, BIAS_LANES), 0)
    d1 = lax.broadcasted_iota(jnp.int32, (BIAS_LANES, BIAS_LANES), 1)
    m, n = (d0, d1) if rel_rows else (d1, d0)
    hit = (m == jnp.minimum(BAND - 1 + MAX_REL - n, 2 * MAX_REL)) & (n < BAND + CHUNK - 1)
    return jnp.where(hit, 1.0, 0.0).astype(F32)


def _bias_table(name, rel_bias_l):
    rel_pad = jnp.pad(rel_bias_l, ((0, 0), (0, BIAS_LANES - N_REL)))

    def body(r_ref, o_ref):
        diag = jnp.dot(r_ref[...], _diag_onehot(True), preferred_element_type=F32,
                       precision=lax.Precision.HIGHEST)
        rowid = lax.broadcasted_iota(jnp.int32, (8, BIAS_LANES), 0)
        lane = lax.broadcasted_iota(jnp.int32, (8, BIAS_LANES), 1)
        for h in range(HEADS):
            d8 = jnp.broadcast_to(diag[h:h + 1, :], (8, BIAS_LANES))
            slab0 = pltpu.roll(d8, BIAS_LANES - CHUNK + 1, axis=1)
            for b in range(1, 8):
                slab0 = jnp.where(rowid == b, pltpu.roll(d8, BIAS_LANES - CHUNK + 1 + b, axis=1),
                                  slab0)
            for a in range(8):
                slab = slab0 if a == 0 else pltpu.roll(slab0, 8 * a, axis=1)
                o_ref[h * CHUNK + 8 * a:h * CHUNK + 8 * a + 8, :] = jnp.where(lane < BAND, slab, NEG)

    tab = pl.pallas_call(
        body, name=name,
        in_specs=[pl.BlockSpec(memory_space=pltpu.VMEM)],
        out_specs=pl.BlockSpec(memory_space=pltpu.VMEM),
        out_shape=_sds((HEADS * CHUNK, BIAS_LANES), F32),
    )(rel_pad)
    tab = tab.reshape(HEADS // 2, 2, CHUNK, BIAS_LANES)
    return jnp.transpose(tab, (0, 3, 1, 2)).reshape(HEADS // 2, BIAS_LANES, 2 * CHUNK)


def _bias_fold(name, dbias_t):
    rows = HEADS * CHUNK
    dbias = jnp.transpose(dbias_t.reshape(HEADS // 2, BIAS_LANES, 2, CHUNK), (0, 2, 3, 1))

    def body(d_ref, o_ref):
        rowid = lax.broadcasted_iota(jnp.int32, (8, BIAS_LANES), 0)
        diags = []
        for h in range(HEADS):
            acc = d_ref[h * CHUNK + 56:h * CHUNK + 64, :]
            for a in range(7):
                slab = d_ref[h * CHUNK + 8 * a:h * CHUNK + 8 * a + 8, :]
                acc = acc + pltpu.roll(slab, 56 - 8 * a, axis=1)
            tot = jnp.where(rowid == 7, acc, 0.0)
            for b in range(7):
                tot = tot + jnp.where(rowid == b, pltpu.roll(acc, 7 - b, axis=1), 0.0)
            diags.append(jnp.sum(tot, axis=0, keepdims=True))
        diag = jnp.concatenate(diags, axis=0)
        o_ref[...] = jnp.dot(diag, _diag_onehot(False), preferred_element_type=F32,
                             precision=lax.Precision.HIGHEST)

    return pl.pallas_call(
        body, name=name,
        in_specs=[pl.BlockSpec(memory_space=pltpu.VMEM)],
        out_specs=pl.BlockSpec(memory_space=pltpu.VMEM),
        out_shape=_sds((HEADS, BIAS_LANES), F32),
    )(dbias.reshape(rows, BIAS_LANES))


def _inv_counts(i):
    trow = lax.broadcasted_iota(jnp.int32, (TOK + HALO, 1), 0) + i * TOK
    return [1.0 / jnp.minimum(trow + 1, w).astype(F32) for w in POOL_WINDOWS]


def _pool_fwd(name, proj, wg, scale, comm=None):
    t = proj.shape[0]
    hb = TOK // HALO

    def body(u_ref, up_ref, wg_ref, sc_ref, pooled_ref, mixed_ref, b0, b1, b2, b3):
        i = pl.program_id(0)
        halo = up_ref[...].astype(F32)
        b0[0:HALO, :] = jnp.where(i == 0, jnp.zeros_like(halo), halo)
        b0[HALO:, :] = u_ref[...].astype(F32)
        n = TOK + HALO
        b1[8:n, :] = b0[8:n, :] + b0[7:n - 1, :]
        b2[16:n, 128:] = b1[16:n, 128:] + b1[14:n - 2, 128:]
        b3[24:n, 256:] = b2[24:n, 256:] + b2[20:n - 4, 256:]
        wins = [b1[HALO:n, 0:128], b2[HALO:n, 128:256], b3[HALO:n, 256:384],
                b3[HALO:n, 384:512] + b3[HALO - 8:n - 8, 384:512]]
        inv = _inv_counts(i)
        for g in range(4):
            cols = slice(g * POOL_GD, (g + 1) * POOL_GD)
            pooled = (wins[g] * inv[g][0:TOK] - b0[HALO:n, cols]).astype(BF16)
            pooled_ref[:, cols] = pooled
            pre = jnp.dot(pooled, wg_ref[g], preferred_element_type=F32)
            mixed_ref[:, cols] = (pre * sc_ref[:, cols]).astype(BF16)

    buf = pltpu.VMEM((TOK + HALO, POOL_W), F32)
    outs, moved = _pcall(
        body, name, (t // TOK,),
        [_row_spec(POOL_W, 3),
         pl.BlockSpec((HALO, POOL_W), lambda i: (jnp.maximum(i * hb - 1, 0), 3)),
         pl.BlockSpec((4, POOL_GD, POOL_GD), lambda i: (0, 0, 0)), _vec_spec(POOL_W)],
        [_row_spec(POOL_W), _row_spec(POOL_W)],
        [_sds((t, POOL_W), BF16), _sds((t, POOL_W), BF16)], [buf, buf, buf, buf],
        _cparams("arbitrary"), (proj, proj, wg, scale), comm)
    return outs if comm is None else (*outs, moved)


def _pool_bwd(name, dmixed, pooled, wg, scale, comm=None):
    t = dmixed.shape[0]
    nt = t // TOK
    hb = TOK // HALO

    def body(dm_ref, dmn_ref, p_ref, wg_ref, sc_ref, du_ref, dwg_ref, dsc_ref, c0, c1, c2, c3):
        i = pl.program_id(0)

        @pl.when(i == 0)
        def _():
            dwg_ref[...] = jnp.zeros_like(dwg_ref)
            dsc_ref[...] = jnp.zeros_like(dsc_ref)

        n = TOK + HALO
        inv = _inv_counts(i)
        dmv = dm_ref[...].astype(F32)
        dmn = dmn_ref[...].astype(F32)
        dmn = jnp.where(i == nt - 1, jnp.zeros_like(dmn), dmn)
        for g in range(4):
            cols = slice(g * POOL_GD, (g + 1) * POOL_GD)
            scg = sc_ref[:, cols]
            pg = p_ref[:, cols]
            dpre = (dmv[:, cols] * scg).astype(BF16)
            dpre_n = (dmn[:, cols] * scg).astype(BF16)
            pre = jnp.dot(pg, wg_ref[g], preferred_element_type=F32)
            dsc_ref[:, cols] += jnp.sum(dmv[:, cols] * pre, axis=0, keepdims=True)
            dwg_ref[g] += lax.dot_general(pg, dpre, (TN, ((), ())), preferred_element_type=F32)
            dpool = lax.dot_general(dpre, wg_ref[g], (NT, ((), ())), preferred_element_type=F32)
            dpool_n = lax.dot_general(dpre_n, wg_ref[g], (NT, ((), ())),
                                      preferred_element_type=F32)
            c0[0:TOK, cols] = dpool
            c0[TOK:n, cols] = dpool_n
            c1[0:TOK, cols] = dpool * inv[g][0:TOK]
            c1[TOK:n, cols] = dpool_n * inv[g][TOK:n]
        c2[0:n - 8, :] = c1[0:n - 8, :] + c1[1:n - 7, :]
        c3[0:n - 16, 128:] = c2[0:n - 16, 128:] + c2[2:n - 14, 128:]
        c1[0:n - 24, 256:] = c3[0:n - 24, 256:] + c3[4:n - 20, 256:]
        wins = [c2[0:TOK, 0:128], c3[0:TOK, 128:256], c1[0:TOK, 256:384],
                c1[0:TOK, 384:512] + c1[8:TOK + 8, 384:512]]
        for g in range(4):
            cols = slice(g * POOL_GD, (g + 1) * POOL_GD)
            du_ref[:, cols] = (wins[g] - c0[0:TOK, cols]).astype(BF16)

    buf = pltpu.VMEM((TOK + HALO, POOL_W), F32)
    outs, moved = _pcall(
        body, name, (nt,),
        [_row_spec(POOL_W),
         pl.BlockSpec((HALO, POOL_W), lambda i: (jnp.minimum((i + 1) * hb, nt * hb - 1), 0)),
         _row_spec(POOL_W), pl.BlockSpec((4, POOL_GD, POOL_GD), lambda i: (0, 0, 0)),
         _vec_spec(POOL_W)],
        [_row_spec(POOL_W), pl.BlockSpec((4, POOL_GD, POOL_GD), lambda i: (0, 0, 0)),
         _vec_spec(POOL_W)],
        [_sds((t, POOL_W), BF16), _sds((4, POOL_GD, POOL_GD), F32), _sds((1, POOL_W), F32)],
        [buf, buf, buf, buf], _cparams("arbitrary"), (dmixed, dmixed, pooled, wg, scale), comm)
    return outs if comm is None else (*outs, moved)


GELU_C = math.sqrt(2.0 / math.pi)


GELU_K = 0.044715


def _gelu_parts(x):
    x2 = x * x
    s = 0.5 + 0.5 * jnp.tanh(x * (GELU_C + (GELU_C * GELU_K) * x2))
    return x * s, s, x2


def _gelu(x):
    return _gelu_parts(x)[0]


def _gelu_and_grad(x):
    g, s, x2 = _gelu_parts(x)
    return g, s + g * (1.0 - s) * ((2 * GELU_C) + (6 * GELU_C * GELU_K) * x2)


def _taps(buf, r, rows):
    a = buf[pl.ds(r, rows + 8), :]
    return a[8:], pltpu.roll(a, 1, axis=0)[8:], pltpu.roll(a, 2, axis=0)[8:]


def _conv(taps, w_ref, b_ref):
    return b_ref[...] + w_ref[2:3, :] * taps[0] + w_ref[1:2, :] * taps[1] + w_ref[0:1, :] * taps[2]


def _stage(dst, prev_ref, cur_ref, next_ref, first, last):
    rows = cur_ref.shape[0]
    h = prev_ref[...].astype(F32)
    dst[0:8, :] = jnp.where(first, jnp.zeros_like(h), h)
    dst[8:8 + rows, :] = cur_ref[...].astype(F32)
    if next_ref is not None:
        h = next_ref[...].astype(F32)
        dst[8 + rows:, :] = jnp.where(last, jnp.zeros_like(h), h)


FWD_STRIP = 32
BWD_STRIP = 16


def _ffn_gate_fwd(name, hu, conv_w, conv_b, comm=None):
    t = hu.shape[0]
    ncol = D_FF // FF_COL
    hb = FF_TOK // 8

    def tile(off):
        return pl.BlockSpec((FF_TOK, FF_COL), lambda i, j: (i, j + off))

    def halo(off):
        return pl.BlockSpec((8, FF_COL), lambda i, j: (jnp.maximum(i * hb - 1, 0), j + off))

    def wspec(off):
        return pl.BlockSpec((3, FF_COL), lambda i, j: (0, j + off))

    def bspec(off):
        return pl.BlockSpec((1, FF_COL), lambda i, j: (0, j + off))

    def body(v_ref, vp_ref, g_ref, gp_ref, wv_ref, wg_ref, bv_ref, bg_ref, a_ref, vb, gb):
        first = pl.program_id(0) == 0
        _stage(vb, vp_ref, v_ref, None, first, None)
        _stage(gb, gp_ref, g_ref, None, first, None)

        def strip(k, carry):
            r = pl.multiple_of(k * FWD_STRIP, FWD_STRIP)
            val = _conv(_taps(vb, r, FWD_STRIP), wv_ref, bv_ref)
            gate = _conv(_taps(gb, r, FWD_STRIP), wg_ref, bg_ref)
            a_ref[pl.ds(r, FWD_STRIP), :] = (_gelu(gate) * val).astype(BF16)
            return carry

        lax.fori_loop(0, FF_TOK // FWD_STRIP, strip, 0)

    buf = pltpu.VMEM((FF_TOK + 8, FF_COL), F32)
    out, moved = _pcall(
        body, name, (t // FF_TOK, ncol),
        [tile(0), halo(0), tile(ncol), halo(ncol), wspec(0), wspec(ncol), bspec(0), bspec(ncol)],
        pl.BlockSpec((FF_TOK, FF_COL), lambda i, j: (i, j)), _sds((t, D_FF), BF16), [buf, buf],
        _cparams("arbitrary", "arbitrary"),
        (hu, hu, hu, hu, conv_w, conv_w, conv_b, conv_b), comm)
    return out if comm is None else (out, moved)


def _ffn_gate_bwd(name, da, hu, conv_w, conv_b, comm=None):
    t = hu.shape[0]
    nt = t // FF_TOK
    ncol = D_FF // FF_COL
    hb = FF_TOK // 8
    ext = FF_TOK + 8

    def tile(off):
        return pl.BlockSpec((FF_TOK, FF_COL), lambda j, i: (i, j + off))

    def prev(off):
        return pl.BlockSpec((8, FF_COL), lambda j, i: (jnp.maximum(i * hb - 1, 0), j + off))

    def nxt(off):
        return pl.BlockSpec((8, FF_COL), lambda j, i: (jnp.minimum((i + 1) * hb, nt * hb - 1), j + off))

    def wspec(off):
        return pl.BlockSpec((3, FF_COL), lambda j, i: (0, j + off))

    def bspec(off):
        return pl.BlockSpec((1, FF_COL), lambda j, i: (0, j + off))

    def body(da_ref, dan_ref, v_ref, vp_ref, vn_ref, g_ref, gp_ref, gn_ref,
             wv_ref, wg_ref, bv_ref, bg_ref, dh_ref, dwv_ref, dwg_ref, vb, gb, dab):
        i = pl.program_id(1)
        first, last = i == 0, i == nt - 1

        @pl.when(first)
        def _():
            dwv_ref[...] = jnp.zeros_like(dwv_ref)
            dwg_ref[...] = jnp.zeros_like(dwg_ref)

        _stage(vb, vp_ref, v_ref, vn_ref, first, last)
        _stage(gb, gp_ref, g_ref, gn_ref, first, last)
        dab[0:FF_TOK, :] = da_ref[...].astype(F32)
        h = dan_ref[...].astype(F32)
        dab[FF_TOK:, :] = jnp.where(last, jnp.zeros_like(h), h)

        def grads(r, rows):
            tv, tg = _taps(vb, r, rows), _taps(gb, r, rows)
            gate = _conv(tg, wg_ref, bg_ref)
            dav = dab[pl.ds(r, rows), :]
            g, dg = _gelu_and_grad(gate)
            dval = dav * g
            dgate = dav * _conv(tv, wv_ref, bv_ref) * dg
            return dval, dgate, tv, tg

        def fold(x):
            return x[0:8] + x[8:16]

        def strip(k, carry):
            r = pl.multiple_of(FF_TOK - BWD_STRIP - k * BWD_STRIP, BWD_STRIP)
            dval, dgate, tv, tg = grads(r, BWD_STRIP)
            new = (dval[0:8], dgate[0:8])
            for half, (d, nxt_rows, taps, w_ref, dw_ref) in enumerate((
                    (dval, carry[0], tv, wv_ref, dwv_ref), (dgate, carry[1], tg, wg_ref, dwg_ref))):
                e = jnp.concatenate([d, nxt_rows], axis=0)
                dh = (w_ref[2:3, :] * d
                      + w_ref[1:2, :] * pltpu.roll(e, BWD_STRIP + 7, axis=0)[0:BWD_STRIP]
                      + w_ref[0:1, :] * pltpu.roll(e, BWD_STRIP + 6, axis=0)[0:BWD_STRIP])
                dh_ref[half, pl.ds(r, BWD_STRIP), :] = dh.astype(BF16)
                dw_ref[0:8, :] += fold(d * taps[2])
                dw_ref[8:16, :] += fold(d * taps[1])
                dw_ref[16:24, :] += fold(d * taps[0])
                dw_ref[24:32, :] += fold(d)
            return new

        dval, dgate, _, _ = grads(FF_TOK, 8)
        lax.fori_loop(0, FF_TOK // BWD_STRIP, strip, (dval, dgate))

        @pl.when(last)
        def _():
            for dw_ref in (dwv_ref, dwg_ref):
                for q in range(4):
                    dw_ref[8 * q:8 * q + 1, :] = jnp.sum(dw_ref[8 * q:8 * q + 8, :], axis=0,
                                                         keepdims=True)

    hbuf = pltpu.VMEM((FF_TOK + 16, FF_COL), F32)
    acc = pl.BlockSpec((32, FF_COL), lambda j, i: (0, j))
    (dhu, dwv, dwg), moved = _pcall(
        body, name, (ncol, nt),
        [tile(0), nxt(0), tile(0), prev(0), nxt(0), tile(ncol), prev(ncol), nxt(ncol),
         wspec(0), wspec(ncol), bspec(0), bspec(ncol)],
        [pl.BlockSpec((2, FF_TOK, FF_COL), lambda j, i: (0, i, j)), acc, acc],
        [_sds((2, t, D_FF), BF16), _sds((32, D_FF), F32), _sds((32, D_FF), F32)],
        [hbuf, hbuf, pltpu.VMEM((ext, FF_COL), F32)], _cparams("arbitrary", "arbitrary"),
        (da, da, hu, hu, hu, hu, hu, hu, conv_w, conv_w, conv_b, conv_b), comm)
    dconv = jnp.concatenate([dwv, dwg], axis=1).reshape(4, 8, 2 * D_FF)[:, 0]
    return (dhu, dconv) if comm is None else (dhu, dconv, moved)


def _mesh_pos():
    x, y, c = lax.axis_index("x"), lax.axis_index("y"), lax.axis_index("c")
    return x, y, c, [(1 - x, y), (x, 1 - y), (1 - x, 1 - y)]


def _any_specs(n):
    return [pl.BlockSpec(memory_space=pl.ANY)] * n


def _remote(src, dst, send_sems, recv_sems, i, dev):
    return pltpu.make_async_remote_copy(src_ref=src, dst_ref=dst, send_sem=send_sems.at[i],
                                        recv_sem=recv_sems.at[i], device_id=dev,
                                        device_id_type=MESH)


def _mine(c, rows):
    return pl.ds(pl.multiple_of(c * (rows // 2), 16), rows // 2)


def _gather_send(shards, conv_shard, gathered, l):
    nbig = len(shards)
    with_conv = conv_shard is not None
    if gathered is None:
        ins = list(shards) + ([conv_shard] if with_conv else [])
        outs = [_sds((DEPTH, N_CHIPS) + s.shape[1:], s.dtype) for s in ins]
        alias = {}
    else:
        ins = list(shards) + list(gathered)
        outs = [_sds(g.shape, g.dtype) for g in gathered]
        alias = {nbig + k: k for k in range(nbig)}

    def copies(cin, cout, ssem, rsem):
        x, y, c, chips = _mesh_pos()
        me = 2 * x + y
        out = []
        for k in range(nbig):
            rows = shards[k].shape[1]
            for j, (cx, cy) in enumerate(chips):
                out.append(_remote(cin[k].at[l, _mine(c, rows)], cout[k].at[l, me, _mine(c, rows)],
                                   ssem, rsem, 4 * k + j, (cx, cy, c)))
            out.append(_remote(cin[k].at[l], cout[k].at[l, me], ssem, rsem, 4 * k + 3,
                               (x, y, 1 - c)))
        if with_conv:
            base = 4 * nbig
            for j, (cx, cy) in enumerate(chips):
                out.append(_remote(cin[nbig].at[c], cout[nbig].at[c, me], ssem, rsem, base + j,
                                   (cx, cy, c)))
            for ll in range(DEPTH):
                out.append(_remote(cin[nbig].at[ll], cout[nbig].at[ll, me], ssem, rsem,
                                   base + 3 + ll, (x, y, 1 - c)))
        return out

    return _Comm(ins, outs, copies, 4 * nbig + 5, alias)


def _gather_forward(gathered, nbig, rows, l):
    with_conv = len(gathered) > nbig
    alias = {k: k for k in range(len(gathered))}

    def copies(cin, cout, ssem, rsem):
        x, y, c, chips = _mesh_pos()
        out = []
        for k in range(nbig):
            for j, (cx, cy) in enumerate(chips):
                blk = cout[k].at[l, 2 * cx + cy, _mine(c, rows[k])]
                out.append(_remote(blk, blk, ssem, rsem, 3 * k + j, (x, y, 1 - c)))
        if with_conv:
            for j, (cx, cy) in enumerate(chips):
                blk = cout[nbig].at[c, 2 * cx + cy]
                out.append(_remote(blk, blk, ssem, rsem, 3 * nbig + j, (x, y, 1 - c)))
        return out

    return _Comm(gathered, [_sds(g.shape, g.dtype) for g in gathered], copies, 3 * nbig + 3, alias)


def _reduce_swap(grads, l):
    def copies(cin, cout, ssem, rsem):
        x, y, c, _ = _mesh_pos()
        return [_remote(cin[k].at[l, :, _mine(1 - c, g.shape[2])], cout[k], ssem, rsem, k,
                        (x, y, 1 - c)) for k, g in enumerate(grads)]

    outs = [_sds((N_CHIPS, g.shape[2] // 2, g.shape[3]), g.dtype) for g in grads]
    return _Comm(grads, outs, copies, len(grads))


def _reduce_scatter(sums):
    def copies(cin, cout, ssem, rsem):
        x, y, c, chips = _mesh_pos()
        return [_remote(cin[k].at[2 * cx + cy], cout[k].at[j], ssem, rsem, 3 * k + j, (cx, cy, c))
                for k in range(len(sums)) for j, (cx, cy) in enumerate(chips)]

    outs = [_sds((3,) + s.shape[1:], s.dtype) for s in sums]
    return _Comm(sums, outs, copies, 3 * len(sums))


def _reduce_share(reds, l):
    def copies(cin, cout, ssem, rsem):
        x, y, c, _ = _mesh_pos()
        out = []
        for k, r in enumerate(reds):
            half = cout[k].at[l, _mine(c, r.shape[1])]
            out.append(_remote(half, half, ssem, rsem, k, (x, y, 1 - c)))
        return out

    return _Comm(reds, [_sds(r.shape, r.dtype) for r in reds], copies, len(reds),
                 {k: k for k in range(len(reds))})


def _allgather_weights(shards):
    n = len(shards)

    def body(*refs):
        ins, outs = refs[:n], refs[n:2 * n]
        send_sems, recv_sems = refs[2 * n:]
        x, y, c, chips = _mesh_pos()
        me = 2 * x + y
        started = []
        own = []
        for k in range(n):
            for l in range(2):
                cp = pltpu.make_async_remote_copy(
                    src_ref=ins[k].at[l], dst_ref=outs[k].at[l, me],
                    send_sem=send_sems.at[k, 6 + l], recv_sem=recv_sems.at[k, 6 + l],
                    device_id=(x, y, 1 - c), device_id_type=MESH)
                cp.start()
                own.append(cp)
            for j, (cx, cy) in enumerate(chips):
                cp = pltpu.make_async_remote_copy(
                    src_ref=ins[k].at[c], dst_ref=outs[k].at[c, me],
                    send_sem=send_sems.at[k, j], recv_sem=recv_sems.at[k, j],
                    device_id=(cx, cy, c), device_id_type=MESH)
                cp.start()
                started.append(cp)
        for k in range(n):
            for j, (cx, cy) in enumerate(chips):
                landed = outs[k].at[c, 2 * cx + cy]
                pltpu.make_async_remote_copy(
                    src_ref=ins[k].at[c], dst_ref=landed,
                    send_sem=send_sems.at[k, j], recv_sem=recv_sems.at[k, j],
                    device_id=(cx, cy, c), device_id_type=MESH).wait_recv()
                fw = pltpu.make_async_remote_copy(
                    src_ref=landed, dst_ref=landed,
                    send_sem=send_sems.at[k, 3 + j], recv_sem=recv_sems.at[k, 3 + j],
                    device_id=(x, y, 1 - c), device_id_type=MESH)
                fw.start()
                started.append(fw)
        for k in range(n):
            for j, (cx, cy) in enumerate(chips):
                theirs = outs[k].at[1 - c, 2 * cx + cy]
                pltpu.make_async_remote_copy(
                    src_ref=theirs, dst_ref=theirs,
                    send_sem=send_sems.at[k, 3 + j], recv_sem=recv_sems.at[k, 3 + j],
                    device_id=(x, y, 1 - c), device_id_type=MESH).wait_recv()
        for cp in started:
            cp.wait_send()
        for cp in own:
            cp.wait()

    return pl.pallas_call(
        body, name="allgather_weights",
        in_specs=_any_specs(n), out_specs=_any_specs(n),
        out_shape=[_sds((2, N_CHIPS) + s.shape[1:], s.dtype) for s in shards],
        scratch_shapes=[pltpu.SemaphoreType.DMA((n, 8)), pltpu.SemaphoreType.DMA((n, 8))],
    )(*shards)


def _swap_layers(grads):
    n = len(grads)

    def body(*refs):
        ins, outs = refs[:n], refs[n:2 * n]
        send_sems, recv_sems = refs[2 * n:]
        x, y, c, _ = _mesh_pos()
        cps = []
        for k in range(n):
            cp = pltpu.make_async_remote_copy(
                src_ref=ins[k].at[1 - c], dst_ref=outs[k],
                send_sem=send_sems.at[k], recv_sem=recv_sems.at[k],
                device_id=(x, y, 1 - c), device_id_type=MESH)
            cp.start()
            cps.append(cp)
        for cp in cps:
            cp.wait()

    return pl.pallas_call(
        body, name="swap_layers",
        in_specs=_any_specs(n), out_specs=_any_specs(n),
        out_shape=[_sds(g.shape[1:], g.dtype) for g in grads],
        scratch_shapes=[pltpu.SemaphoreType.DMA((n,)), pltpu.SemaphoreType.DMA((n,))],
    )(*grads)


def _scatter_blocks(sums):
    n = len(sums)

    def body(*refs):
        ins, outs = refs[:n], refs[n:2 * n]
        send_sems, recv_sems = refs[2 * n:]
        x, y, c, chips = _mesh_pos()
        cps = []
        for k in range(n):
            for j, (cx, cy) in enumerate(chips):
                cp = pltpu.make_async_remote_copy(
                    src_ref=ins[k].at[2 * cx + cy], dst_ref=outs[k].at[j],
                    send_sem=send_sems.at[k, j], recv_sem=recv_sems.at[k, j],
                    device_id=(cx, cy, c), device_id_type=MESH)
                cp.start()
                cps.append(cp)
        for cp in cps:
            cp.wait()

    return pl.pallas_call(
        body, name="scatter_blocks",
        in_specs=_any_specs(n), out_specs=_any_specs(n),
        out_shape=[_sds((3,) + s.shape[1:], s.dtype) for s in sums],
        scratch_shapes=[pltpu.SemaphoreType.DMA((n, 3)), pltpu.SemaphoreType.DMA((n, 3))],
    )(*sums)


def _exchange_reduced(reds):
    n = len(reds)

    def body(*refs):
        outs = refs[n:2 * n]
        send_sems, recv_sems = refs[2 * n:]
        x, y, c, _ = _mesh_pos()
        cps = []
        for k in range(n):
            cp = pltpu.make_async_remote_copy(
                src_ref=outs[k].at[c], dst_ref=outs[k].at[c],
                send_sem=send_sems.at[k], recv_sem=recv_sems.at[k],
                device_id=(x, y, 1 - c), device_id_type=MESH)
            cp.start()
            cps.append(cp)
        for k in range(n):
            pltpu.make_async_remote_copy(
                src_ref=outs[k].at[c], dst_ref=outs[k].at[1 - c],
                send_sem=send_sems.at[k], recv_sem=recv_sems.at[k],
                device_id=(x, y, 1 - c), device_id_type=MESH).wait_recv()
        for cp in cps:
            cp.wait_send()

    return pl.pallas_call(
        body, name="exchange_reduced",
        in_specs=_any_specs(n), out_specs=_any_specs(n),
        out_shape=[_sds(r.shape, r.dtype) for r in reds],
        input_output_aliases={k: k for k in range(n)},
        scratch_shapes=[pltpu.SemaphoreType.DMA((n,)), pltpu.SemaphoreType.DMA((n,))],
    )(*reds)


def _allreduce_small(per_layer):
    kinds = len(per_layer[0])
    shapes = [a.shape[1:] if a.shape[0] == 1 else a.shape for a in per_layer[0]]

    def body(*refs):
        ins = refs[:DEPTH * kinds]
        outs = refs[DEPTH * kinds:(DEPTH + 1) * kinds]
        gbufs = refs[(DEPTH + 1) * kinds:(DEPTH + 2) * kinds]
        send_sems, recv_sems = refs[-2], refs[-1]
        x, y, c, chips = _mesh_pos()
        sibling = (x, y, 1 - c)

        def copy(k, i, block, to):
            px, py, pc = block
            slot = gbufs[k].at[4 * px + 2 * py + pc]
            return _remote(slot, slot, send_sems, recv_sems, 7 * k + i, to)

        me = (x, y, c)
        first, passed = [], []
        for k in range(kinds):
            for l in range(DEPTH):
                a = ins[l * kinds + k]
                if per_layer[l][k].shape[0] == 1:
                    gbufs[k][4 * x + 2 * y + c, l:l + 1] = a[...]
                else:
                    gbufs[k][4 * x + 2 * y + c, l] = a[...]
            first.append(copy(k, 0, me, sibling))
            first += [copy(k, 1 + j, me, (*chip, c)) for j, chip in enumerate(chips)]
            passed += [copy(k, 4 + j, (*chip, c), sibling) for j, chip in enumerate(chips)]
        for cp in first:
            cp.start()
        for k in range(kinds):
            for j, chip in enumerate(chips):
                copy(k, 1 + j, (*chip, c), me).wait_recv()
                passed[3 * k + j].start()
        for k in range(kinds):
            copy(k, 0, sibling, me).wait_recv()
            for j, chip in enumerate(chips):
                copy(k, 4 + j, (*chip, 1 - c), me).wait_recv()
        for cp in first + passed:
            cp.wait_send()
        for k in range(kinds):
            acc = gbufs[k][0]
            for d in range(1, 8):
                acc = acc + gbufs[k][d]
            outs[k][...] = acc

    vmem = pl.BlockSpec(memory_space=pltpu.VMEM)
    return pl.pallas_call(
        body, name="allreduce_small",
        in_specs=[vmem] * (DEPTH * kinds), out_specs=[vmem] * kinds,
        out_shape=[_sds((DEPTH,) + s, F32) for s in shapes],
        scratch_shapes=[pltpu.VMEM((8, DEPTH) + s, F32) for s in shapes]
        + [pltpu.SemaphoreType.DMA((7 * kinds,)), pltpu.SemaphoreType.DMA((7 * kinds,))],
        compiler_params=pltpu.CompilerParams(vmem_limit_bytes=VMEM_LIMIT_V7X),
    )(*per_layer[0], *per_layer[1])


def _adamw_small(ws, gs, ms, vs):
    n = len(ws)
    c1 = 1.0 - ADAM_B1 ** ADAM_STEP
    c2 = 1.0 - ADAM_B2 ** ADAM_STEP

    def body(*refs):
        for i in range(n):
            w_ref, g_ref, m_ref, v_ref = (refs[j * n + i] for j in range(4))
            d_ref, nm_ref, nv_ref = (refs[(4 + j) * n + i] for j in range(3))
            gv = g_ref[...]
            nm = ADAM_B1 * m_ref[...] + (1.0 - ADAM_B1) * gv
            nv = ADAM_B2 * v_ref[...] + (1.0 - ADAM_B2) * (gv * gv)
            nm_ref[...] = nm
            nv_ref[...] = nv
            d_ref[...] = -ADAM_LR * ((nm / c1) / (jnp.sqrt(nv / c2) + ADAM_EPS)
                                     + ADAM_WD * w_ref[...])

    vmem = pl.BlockSpec(memory_space=pltpu.VMEM)
    outs = pl.pallas_call(
        body, name="adamw_small", in_specs=[vmem] * (4 * n), out_specs=[vmem] * (3 * n),
        out_shape=[_sds(w.shape, F32) for w in ws] * 3,
        compiler_params=pltpu.CompilerParams(vmem_limit_bytes=VMEM_LIMIT_V7X),
    )(*ws, *gs, *ms, *vs)
    return outs[:n], outs[n:2 * n], outs[2 * n:]


def _core_index():
    return jnp.reshape(lax.axis_index("c"), (1,)).astype(jnp.int32)


def _chip_index():
    return jnp.reshape(2 * lax.axis_index("x") + lax.axis_index("y"), (1,)).astype(jnp.int32)


def _chip_sum(name, stacked, sib, l):
    _, nb, r, cdim = stacked.shape
    hr = r // 2

    def body(c_ref, a_ref, b_ref, o_ref):
        o_ref[...] = (a_ref[...].astype(F32) + b_ref[...].astype(F32)).astype(BF16)

    return pl.pallas_call(
        body, name=name,
        grid_spec=pltpu.PrefetchScalarGridSpec(
            num_scalar_prefetch=1, grid=(nb,),
            in_specs=[pl.BlockSpec((None, None, hr, cdim), lambda j, cr: (l, j, cr[0], 0)),
                      pl.BlockSpec((None, hr, cdim), lambda j, cr: (j, 0, 0))],
            out_specs=pl.BlockSpec((None, hr, cdim), lambda j, cr: (j, 0, 0))),
        out_shape=_sds((nb, hr, cdim), BF16),
        compiler_params=_cparams("parallel"))(_core_index(), stacked, sib)


def _final_sum(name, sums, recv, l, fill):
    _, hr, cdim = sums.shape
    tr = hr // 2

    def body(m_ref, a_ref, b_ref, *rest):
        acc = a_ref[...].astype(F32)
        for j in range(3):
            acc = acc + b_ref[j].astype(F32)
        rest[-1][...] = acc

    in_specs = [pl.BlockSpec((None, tr, cdim), lambda i, mr: (mr[0], i, 0)),
                pl.BlockSpec((3, tr, cdim), lambda i, mr: (0, i, 0))]
    args = [jnp.concatenate([_chip_index(), _core_index()]), sums, recv]
    aliases = {}
    if fill is not None:
        in_specs.append(pl.BlockSpec(memory_space=pl.ANY))
        args.append(fill)
        aliases = {3: 0}
    return pl.pallas_call(
        body, name=name,
        grid_spec=pltpu.PrefetchScalarGridSpec(
            num_scalar_prefetch=1, grid=(2,), in_specs=in_specs,
            out_specs=pl.BlockSpec((None, tr, cdim), lambda i, mr: (l, 2 * mr[1] + i, 0))),
        out_shape=_sds((DEPTH, 2 * hr, cdim), F32), input_output_aliases=aliases,
        compiler_params=_cparams("parallel"))(*args)


def _adamw(name, w, g, m, v):
    nl, r, cdim = w.shape
    tr = r // 4 if r % 32 == 0 else r
    c1 = 1.0 - ADAM_B1 ** ADAM_STEP
    c2 = 1.0 - ADAM_B2 ** ADAM_STEP

    def body(w_ref, g_ref, m_ref, v_ref, d_ref, nm_ref, nv_ref):
        gv = g_ref[...]
        nm = ADAM_B1 * m_ref[...] + (1.0 - ADAM_B1) * gv
        nv = ADAM_B2 * v_ref[...] + (1.0 - ADAM_B2) * (gv * gv)
        nm_ref[...] = nm
        nv_ref[...] = nv
        d_ref[...] = -ADAM_LR * ((nm / c1) / (jnp.sqrt(nv / c2) + ADAM_EPS) + ADAM_WD * w_ref[...])

    spec = pl.BlockSpec((None, tr, cdim), lambda l, i: (l, i, 0))
    out = _sds(w.shape, F32)
    return pl.pallas_call(
        body, name=name, grid=(nl, r // tr),
        in_specs=[spec] * 4, out_specs=[spec] * 3, out_shape=[out] * 3,
        compiler_params=_cparams("parallel", "parallel"))(w, g, m, v)


def _rows128(a):
    return a.reshape(-1, 128)


def kernel(x, norm_mix_pre, w_in, b_gate, rel_bias, w_attn_out, w_pool_group, pool_scale, w_pool_out, w_o, norm_mix_post, norm_ffn_pre, w_up, conv_w, conv_b, w_down, norm_ffn_post, loss_target, m_norm_mix_pre, m_w_in, m_b_gate, m_rel_bias, m_w_attn_out, m_w_pool_group, m_pool_scale, m_w_pool_out, m_w_o, m_norm_mix_post, m_norm_ffn_pre, m_w_up, m_conv_w, m_conv_b, m_w_down, m_norm_ffn_post, v_norm_mix_pre, v_w_in, v_b_gate, v_rel_bias, v_w_attn_out, v_w_pool_group, v_pool_scale, v_w_pool_out, v_w_o, v_norm_mix_post, v_norm_ffn_pre, v_w_up, v_conv_w, v_conv_b, v_w_down, v_norm_ffn_post):
    t = x.shape[1]
    xs = x.reshape(t, D_MODEL)
    target = loss_target.reshape(t, D_MODEL)

    names = ["w_in", "w_attn_out", "w_pool_out", "w_o", "w_up", "w_down"]
    shards = [w.astype(BF16) for w in (w_in, w_attn_out, w_pool_out, w_o, w_up, w_down)]
    rows = [s.shape[1] for s in shards]
    nbig = len(shards)
    g = _comm_call("gather0_send", _gather_send(shards[:1], conv_w, None, 0))
    g = _comm_call("gather0_forward", _gather_forward(g, 1, rows[:1], 0))
    cw_full = jnp.transpose(g[1], (0, 2, 1, 3)).reshape(DEPTH, 3, 2 * D_FF)
    g = g[:1]
    wg_bf = w_pool_group.astype(BF16)

    def views(gathered):
        win_g, wao_g, wpo_g, wo_g, wup_g, wdn_g = gathered
        return (win_g, wao_g, wpo_g, wo_g.reshape(DEPTH, D_MODEL, D_MODEL), wup_g,
                wdn_g.reshape(DEPTH, D_FF, D_MODEL))

    saved = []
    xcur = xs
    h = _norm_fwd("l0_norm_mix_pre", xs, norm_mix_pre[0:1])
    for l in range(DEPTH):
        tag = f"l{l}_"
        bias = _bias_table(tag + "bias_table", rel_bias[l])
        proj = _mm_nn_blocked(tag + "proj", h, g[0], l, BF16)
        if l == 0:
            att, rest = _attn_fwd(tag + "attn_fwd", proj, bias,
                                  _gather_send(shards[1:], None, None, 0))
            pooled, mixed, rest = _pool_fwd(tag + "pool_fwd", proj, wg_bf[l], pool_scale[l:l + 1],
                                            _gather_forward(rest, nbig - 1, rows[1:], 0))
            g = g + rest
        else:
            att = _attn_fwd(tag + "attn_fwd", proj, bias)
            pooled, mixed = _pool_fwd(tag + "pool_fwd", proj, wg_bf[l], pool_scale[l:l + 1])
        win_g, wao_g, wpo_g, wo_full, wup_g, wdn_full = views(g)
        ya = _narrow_nn(tag + "attn_out", att, wao_g, l)
        yb = _narrow_nn(tag + "pool_out", mixed, wpo_g, l)
        z = _gate_fwd(tag + "gate_fwd", proj, b_gate[l:l + 1], ya, yb)
        mix = _mm_nn(tag + "mix", z, wo_full, l, D_MODEL, F32)
        x1, h2 = _post_pre_fwd(tag + "norm_mix_post", xcur, mix, norm_mix_post[l:l + 1],
                               norm_ffn_pre[l:l + 1])
        hu = _mm_nn_blocked(tag + "ffn_up", h2, wup_g, l, BF16)
        if l == 0:
            a, g = _ffn_gate_fwd(tag + "ffn_gate_fwd", hu, cw_full[l], conv_b[l:l + 1],
                                 _gather_send(shards, None, g, 1))
            wdn_full = views(g)[5]
        else:
            a = _ffn_gate_fwd(tag + "ffn_gate_fwd", hu, cw_full[l], conv_b[l:l + 1])
        f = _mm_nn(tag + "ffn_down", a, wdn_full, l, D_FF // 2, F32)
        saved.append(dict(x=xcur, h=h, proj=proj, att=att, pooled=pooled, mixed=mixed, ya=ya,
                          yb=yb, z=z, mix=mix, x1=x1, h2=h2, hu=hu, a=a, f=f, bias=bias))
        if l == 0:
            xcur, h, g = _post_pre_fwd(tag + "norm_ffn_post", x1, f, norm_ffn_post[l:l + 1],
                                       norm_mix_pre[l + 1:l + 2], _gather_forward(g, nbig, rows, 1))
        elif l < DEPTH - 1:
            xcur, h = _post_pre_fwd(tag + "norm_ffn_post", x1, f, norm_ffn_post[l:l + 1],
                                    norm_mix_pre[l + 1:l + 2])
    win_g, wao_g, wpo_g, wo_full, wup_g, wdn_full = views(g)

    dy, df, d_nfpost, loss_local = _tail("tail", saved[-1]["x1"], saved[-1]["f"],
                                         norm_ffn_post[DEPTH - 1:DEPTH], target)
    loss = lax.psum(loss_local, ("x", "y", "c"))

    dx = dy
    dws = dict.fromkeys(names)
    reds = [None] * nbig
    small_grads = [None] * DEPTH
    ffn = [4, 5]
    outs3 = [1, 2, 3]

    def blocks(ks):
        return [dws[names[k]].reshape(DEPTH, N_CHIPS, rows[k], -1) for k in ks]

    def chip_sums(ks, sib, l):
        return [_chip_sum(f"chip_sum{l}_" + names[k], b, s, l)
                for k, b, s in zip(ks, blocks(ks), sib)]

    def final_sums(ks, sums, recv, l):
        for k, s, r in zip(ks, sums, recv):
            reds[k] = _final_sum(f"final_sum{l}_" + names[k], s, r, l, reds[k])

    for l in reversed(range(DEPTH)):
        tag = f"l{l}_"
        sv = saved[l]
        every = list(range(nbig))
        if l == 0:
            da, sib = _mm_nt(tag + "ffn_down_dx", df, wdn_full, l, D_FF // 2, BF16,
                             _reduce_swap(blocks(every), 1))
            sums = chip_sums(every, sib, 1)
        else:
            da = _mm_nt(tag + "ffn_down_dx", df, wdn_full, l, D_FF // 2, BF16)
        dws["w_down"] = _mm_tn(tag + "ffn_down_dw", sv["a"], df, D_FF // 2, l, dws["w_down"])
        if l == 0:
            dhu, dconv, recv = _ffn_gate_bwd(tag + "ffn_gate_bwd", da, sv["hu"], cw_full[l],
                                             conv_b[l:l + 1], _reduce_scatter(sums))
            final_sums(every, sums, recv, 1)
            dh2, reds = _mm_nt_blocked(tag + "ffn_up_dx", dhu, wup_g, l, F32,
                                       _reduce_share(reds, 1))
        else:
            dhu, dconv = _ffn_gate_bwd(tag + "ffn_gate_bwd", da, sv["hu"], cw_full[l],
                                       conv_b[l:l + 1])
            dh2 = _mm_nt_blocked(tag + "ffn_up_dx", dhu, wup_g, l, F32)
        dws["w_up"] = _mm_tn_blocked(tag + "ffn_up_dw", sv["h2"], dhu, l, dws["w_up"])
        if l == 0:
            dx1, d_nfpre, dmix, d_nmpost, sib = _pre_post_bwd(
                tag + "norm_ffn_pre_bwd", dh2, sv["x1"], dx, norm_ffn_pre[l:l + 1], sv["mix"],
                norm_mix_post[l:l + 1], _reduce_swap(blocks(ffn), 0))
            sums = chip_sums(ffn, sib, 0)
        else:
            dx1, d_nfpre, dmix, d_nmpost = _pre_post_bwd(
                tag + "norm_ffn_pre_bwd", dh2, sv["x1"], dx, norm_ffn_pre[l:l + 1], sv["mix"],
                norm_mix_post[l:l + 1])
        dz = _mm_nt(tag + "mix_dx", dmix, wo_full, l, D_MODEL, BF16)
        dws["w_o"] = _mm_tn(tag + "mix_dw", sv["z"], dmix, D_MODEL, l, dws["w_o"])
        dya, dyb, dgates, d_bgate = _gate_bwd(tag + "gate_bwd", dz, sv["proj"], b_gate[l:l + 1],
                                              sv["ya"], sv["yb"])
        datt = _narrow_nt(tag + "attn_out_dx", dya, wao_g, l)
        dws["w_attn_out"] = _narrow_tn(tag + "attn_out_dw", sv["att"], dya, l, dws["w_attn_out"])
        dmixed = _narrow_nt(tag + "pool_out_dx", dyb, wpo_g, l)
        dws["w_pool_out"] = _narrow_tn(tag + "pool_out_dw", sv["mixed"], dyb, l, dws["w_pool_out"])
        if l == 0:
            du, d_wg, d_pscale, sib = _pool_bwd(tag + "pool_bwd", dmixed, sv["pooled"], wg_bf[l],
                                                pool_scale[l:l + 1], _reduce_swap(blocks(outs3), 0))
            sums3 = chip_sums(outs3, sib, 0)
            dqkv, dbias, recv = _attn_bwd(
                tag + "attn_bwd", sv["proj"], datt, sv["bias"],
                _both(_reduce_scatter(sums), _reduce_scatter(sums3)))
            final_sums(ffn, sums, recv[:len(ffn)], 0)
            final_sums(outs3, sums3, recv[len(ffn):], 0)
        else:
            du, d_wg, d_pscale = _pool_bwd(tag + "pool_bwd", dmixed, sv["pooled"], wg_bf[l],
                                           pool_scale[l:l + 1])
            dqkv, dbias = _attn_bwd(tag + "attn_bwd", sv["proj"], datt, sv["bias"])
        d_rel = _bias_fold(tag + "bias_fold", dbias)
        if l == 0:
            dh, shared = _proj_dx(tag + "proj_dx", dqkv, du, dgates, win_g, l,
                                  _reduce_share([reds[k] for k in ffn + outs3], 0))
            for k, r in zip(ffn + outs3, shared):
                reds[k] = r
        else:
            dh = _proj_dx(tag + "proj_dx", dqkv, du, dgates, win_g, l)
        dws["w_in"] = _proj_dw(tag + "proj_dw", sv["h"], dqkv, du, dgates, l, dws["w_in"])
        small_grads[l] = [None, d_nmpost, d_nfpre, d_nfpost, d_bgate, d_rel, d_wg, d_pscale, dconv]
        if l > 0:
            dx, small_grads[l][0], df, d_nfpost = _pre_post_bwd(
                tag + "norm_mix_pre_bwd", dh, sv["x"], dx1, norm_mix_pre[l:l + 1],
                saved[l - 1]["f"], norm_ffn_post[l - 1:l])
        else:
            dx, small_grads[l][0] = _norm_pre_bwd(tag + "norm_mix_pre_bwd", dh, sv["x"], dx1,
                                                  norm_mix_pre[l:l + 1])

    grad_x = dx.reshape(x.shape)

    sib = _comm_call("reduce_swap", _reduce_swap(blocks([0]), 0))
    sums = chip_sums([0], sib, 0)
    recv = _comm_call("reduce_scatter", _reduce_scatter(sums))
    final_sums([0], sums, recv, 0)
    g_big = _comm_call("reduce_share", _reduce_share([reds[0]], 0)) + reds[1:]

    (g_nmpre, g_nmpost, g_nfpre, g_nfpost, g_bgate, g_rel, g_wg, g_pscale,
     g_conv) = _allreduce_small(small_grads)
    g_rel = g_rel[:, :, :N_REL]
    g_cb = g_conv[:, 3]
    ncw = conv_w.shape[2]
    chip = 2 * lax.axis_index("x") + lax.axis_index("y")
    g_cw = lax.dynamic_slice_in_dim(g_conv[:, 0:3], chip * ncw, ncw, axis=2)

    grads = dict(norm_mix_pre=g_nmpre, w_in=g_big[0], b_gate=g_bgate, rel_bias=g_rel,
                 w_attn_out=g_big[1], w_pool_group=g_wg, pool_scale=g_pscale, w_pool_out=g_big[2],
                 w_o=g_big[3], norm_mix_post=g_nmpost, norm_ffn_pre=g_nfpre, w_up=g_big[4],
                 conv_w=g_cw, conv_b=g_cb, w_down=g_big[5], norm_ffn_post=g_nfpost)
    weights = dict(norm_mix_pre=norm_mix_pre, w_in=w_in, b_gate=b_gate, rel_bias=rel_bias,
                   w_attn_out=w_attn_out, w_pool_group=w_pool_group, pool_scale=pool_scale,
                   w_pool_out=w_pool_out, w_o=w_o, norm_mix_post=norm_mix_post,
                   norm_ffn_pre=norm_ffn_pre, w_up=w_up, conv_w=conv_w, conv_b=conv_b,
                   w_down=w_down, norm_ffn_post=norm_ffn_post)
    moms = dict(norm_mix_pre=(m_norm_mix_pre, v_norm_mix_pre), w_in=(m_w_in, v_w_in),
                b_gate=(m_b_gate, v_b_gate), rel_bias=(m_rel_bias, v_rel_bias),
                w_attn_out=(m_w_attn_out, v_w_attn_out),
                w_pool_group=(m_w_pool_group, v_w_pool_group),
                pool_scale=(m_pool_scale, v_pool_scale), w_pool_out=(m_w_pool_out, v_w_pool_out),
                w_o=(m_w_o, v_w_o), norm_mix_post=(m_norm_mix_post, v_norm_mix_post),
                norm_ffn_pre=(m_norm_ffn_pre, v_norm_ffn_pre), w_up=(m_w_up, v_w_up),
                conv_w=(m_conv_w, v_conv_w), conv_b=(m_conv_b, v_conv_b),
                w_down=(m_w_down, v_w_down), norm_ffn_post=(m_norm_ffn_post, v_norm_ffn_post))
    order = list(weights.keys())

    delta, new_m, new_v = {}, {}, {}
    small_names = [nm for nm in order if nm not in names]
    for nm in names:
        delta[nm], new_m[nm], new_v[nm] = _adamw("adamw_" + nm, weights[nm], grads[nm], *moms[nm])
    d_s, m_s, v_s = _adamw_small([weights[nm] for nm in small_names],
                                 [grads[nm] for nm in small_names],
                                 [moms[nm][0] for nm in small_names],
                                 [moms[nm][1] for nm in small_names])
    for i, nm in enumerate(small_names):
        delta[nm], new_m[nm], new_v[nm] = d_s[i], m_s[i], v_s[i]

    return (loss, grad_x, *[grads[nm] for nm in order], *[delta[nm] for nm in order],
            *[new_m[nm] for nm in order], *[new_v[nm] for nm in order])
```

```python
import functools
import math

import jax
import jax.numpy as jnp
from jax import lax
from jax.experimental import pallas as pl
from jax.experimental.pallas import tpu as pltpu

F32 = jnp.float32
BF16 = jnp.bfloat16
MESH = pl.DeviceIdType.MESH

D_MODEL = 1024
DEPTH = 2
CHUNK = 64
BAND_CHUNKS = 9
BAND = BAND_CHUNKS * CHUNK
HEADS = 8
HEAD_DIM = 64
ATTN_W = HEADS * HEAD_DIM
POOL_WINDOWS = (2, 4, 8, 16)
POOL_W = 512
POOL_GD = 128
MAX_REL = 256
N_REL = 2 * MAX_REL + 1
D_FF = 2816
IN_W = 3 * ATTN_W + POOL_W + 2 * D_MODEL
EPS = 1e-6
ATTN_SCALE = HEAD_DIM ** -0.5
BAND_PAD = 640
BIAS_LANES = BAND_PAD
N_CHIPS = 4

ADAM_LR = 0.001
ADAM_B1 = 0.9
ADAM_B2 = 0.999
ADAM_EPS = 1e-08
ADAM_WD = 0.01
ADAM_STEP = 10

VMEM_LIMIT_V7X = 56 * 1024 * 1024
TOK = 512
ATT_BLK = 8 * CHUNK
FF_COL = 256
FF_TOK = 1024
HALO = 32


def _cparams(*sem):
    return pltpu.CompilerParams(dimension_semantics=sem, vmem_limit_bytes=VMEM_LIMIT_V7X)


def _sds(shape, dtype):
    return jax.ShapeDtypeStruct(shape, dtype)


class _Comm:
    def __init__(self, ins, outs, copies, n_sems, alias=None):
        self.ins, self.outs, self.copies, self.n_sems = list(ins), list(outs), copies, n_sems
        self.alias = dict(alias or {})


class _SemsFrom:
    def __init__(self, sems, start):
        self.sems, self.start = sems, start

    @property
    def at(self):
        return self

    def __getitem__(self, i):
        return self.sems.at[self.start + i]


def _both(a, b):
    na, nao = len(a.ins), len(a.outs)

    def copies(cin, cout, ssem, rsem):
        return (a.copies(cin[:na], cout[:nao], ssem, rsem)
                + b.copies(cin[na:], cout[nao:], _SemsFrom(ssem, a.n_sems), _SemsFrom(rsem, a.n_sems)))

    alias = dict(a.alias)
    alias.update({na + i: nao + o for i, o in b.alias.items()})
    return _Comm(a.ins + b.ins, a.outs + b.outs, copies, a.n_sems + b.n_sems, alias)


def _pcall(body, name, grid, in_specs, out_specs, out_shape, scratch_shapes, compiler_params, args,
           comm=None, aliases=None):
    single = not isinstance(out_shape, (list, tuple))
    out_specs = [out_specs] if single else list(out_specs)
    out_shape = [out_shape] if single else list(out_shape)
    n_in, n_out = len(in_specs), len(out_specs)
    aliases = dict(aliases or {})
    if comm is None:
        res = pl.pallas_call(
            body, name=name, grid=grid, in_specs=list(in_specs), out_specs=out_specs,
            out_shape=out_shape, scratch_shapes=list(scratch_shapes),
            input_output_aliases=aliases, compiler_params=compiler_params)(*args)
        return (res[0] if single else res), None
    ci, co = len(comm.ins), len(comm.outs)

    def hosted(*refs):
        main_in, cin = refs[:n_in], refs[n_in:n_in + ci]
        main_out = refs[n_in + ci:n_in + ci + n_out]
        cout = refs[n_in + ci + n_out:n_in + ci + n_out + co]
        rest = refs[n_in + ci + n_out + co:]
        copies = comm.copies(cin, cout, rest[-2], rest[-1])
        ids = [pl.program_id(a) for a in range(len(grid))]
        first = functools.reduce(jnp.logical_and, [i == 0 for i in ids])
        last = functools.reduce(jnp.logical_and, [i == g - 1 for i, g in zip(ids, grid)])

        @pl.when(first)
        def _():
            for cp in copies:
                cp.start()

        body(*main_in, *main_out, *rest[:-2])

        @pl.when(last)
        def _():
            for cp in copies:
                cp.wait()

    for i, o in comm.alias.items():
        aliases[n_in + i] = n_out + o
    hbm = pl.BlockSpec(memory_space=pl.ANY)
    sems = pltpu.SemaphoreType.DMA((comm.n_sems,))
    res = pl.pallas_call(
        hosted, name=name, grid=grid, in_specs=list(in_specs) + [hbm] * ci,
        out_specs=out_specs + [hbm] * co, out_shape=out_shape + comm.outs,
        scratch_shapes=list(scratch_shapes) + [sems, sems],
        input_output_aliases=aliases, compiler_params=compiler_params)(*args, *comm.ins)
    return (res[0] if single else list(res[:n_out])), list(res[n_out:])


def _comm_call(name, comm):
    ci = len(comm.ins)

    def body(*refs):
        copies = comm.copies(refs[:ci], refs[ci:-2], refs[-2], refs[-1])
        for cp in copies:
            cp.start()
        for cp in copies:
            cp.wait()

    hbm = pl.BlockSpec(memory_space=pl.ANY)
    sems = pltpu.SemaphoreType.DMA((comm.n_sems,))
    return list(pl.pallas_call(
        body, name=name, in_specs=[hbm] * ci, out_specs=[hbm] * len(comm.outs),
        out_shape=comm.outs, scratch_shapes=[sems, sems],
        input_output_aliases=comm.alias)(*comm.ins))


def _matmul(name, a, b, a_spec, b_spec, o_spec, out_shape, grid, contract, nk, acc_shape,
            fill=None, comm=None):
    def body(*refs):
        a_ref, b_ref = refs[0], refs[1]
        o_ref = refs[2 if fill is None else 3]
        scratch = refs[(3 if fill is None else 4):]
        part = lax.dot_general(a_ref[...], b_ref[...], (contract, ((), ())),
                               preferred_element_type=F32)
        if nk == 1:
            o_ref[...] = part.astype(o_ref.dtype)
        else:
            acc_ref = scratch[0]
            k = pl.program_id(2)

            @pl.when(k == 0)
            def _():
                acc_ref[...] = part

            @pl.when(k > 0)
            def _():
                acc_ref[...] += part

            @pl.when(k == nk - 1)
            def _():
                o_ref[...] = acc_ref[...].astype(o_ref.dtype)

    scratch = [] if nk == 1 else [pltpu.VMEM(acc_shape, F32)]
    in_specs, args, aliases = [a_spec, b_spec], [a, b], {}
    if fill is not None:
        in_specs.append(pl.BlockSpec(memory_space=pl.ANY))
        args.append(fill)
        aliases = {2: 0}
    out, moved = _pcall(body, name, grid, in_specs, o_spec, out_shape, scratch,
                        _cparams("parallel", "parallel", "arbitrary"), args, comm, aliases)
    return out if comm is None else (out, moved)


NN = ((1,), (0,))
NT = ((1,), (1,))
TN = ((0,), (0,))


def _tm(t):
    return min(t, 1024)


def _tt(t):
    return min(t, 2048)


def _col_block_spec(a, rows, nb, row_col):
    if a.ndim == 2:
        return pl.BlockSpec((rows, nb), row_col)

    def halves(*ids):
        r, c = row_col(*ids)
        return c // 2, r, c % 2

    return pl.BlockSpec((None, rows, nb), halves)


def _mm_nn_blocked(name, a, w, l, out_dtype):
    t, k = a.shape
    nb = w.shape[3]
    tm = _tm(t)
    return _matmul(
        name, a, w,
        pl.BlockSpec((tm, k), lambda i, n, kk: (i, 0)),
        pl.BlockSpec((None, None, k, nb), lambda i, n, kk: (l, n, 0, 0)),
        pl.BlockSpec((tm, nb), lambda i, n, kk: (i, n)),
        _sds((t, N_CHIPS * nb), out_dtype), (t // tm, N_CHIPS, 1), NN, 1, None)


def _mm_nt_blocked(name, a, w, l, out_dtype, comm=None):
    t = a.shape[-2]
    k, nb = w.shape[2], w.shape[3]
    tm = _tm(t)
    return _matmul(
        name, a, w,
        _col_block_spec(a, tm, nb, lambda i, n, kk: (i, kk)),
        pl.BlockSpec((None, None, k, nb), lambda i, n, kk: (l, kk, 0, 0)),
        pl.BlockSpec((tm, k), lambda i, n, kk: (i, 0)),
        _sds((t, k), out_dtype), (t // tm, 1, N_CHIPS), NT, N_CHIPS, (tm, k), comm=comm)


def _mm_tn_blocked(name, a, g, l, fill):
    t, k = a.shape
    nb = g.shape[-1] * (g.ndim - 1) // N_CHIPS
    tt = _tt(t)
    nt = t // tt
    return _matmul(
        name, a, g,
        pl.BlockSpec((tt, k), lambda n, j, kk: (kk, 0)),
        _col_block_spec(g, tt, nb, lambda n, j, kk: (kk, n)),
        pl.BlockSpec((None, None, k, nb), lambda n, j, kk: (l, n, 0, 0)),
        _sds((DEPTH, N_CHIPS, k, nb), BF16), (N_CHIPS, 1, nt), TN, nt, (k, nb), fill)


def _proj_pieces(rows, dqkv_first):
    def piece(col):
        if dqkv_first:
            return pl.BlockSpec((rows, ATTN_W), lambda i, kk: (i, col))
        return pl.BlockSpec((rows, ATTN_W), lambda n, kk: (kk, col))
    return [piece(0), piece(1), piece(2), piece(0)]


def _proj_dx(name, dqkv, du, dgates, w, l, comm=None):
    t = du.shape[0]
    k, nb = w.shape[2], w.shape[3]
    tm = _tm(t)

    def body(dq_ref, dk_ref, dv_ref, du_ref, dg_ref, w_ref, o_ref, acc_ref):
        kk = pl.program_id(1)

        def mm(a):
            return lax.dot_general(a, w_ref[...], (NT, ((), ())), preferred_element_type=F32)

        @pl.when(kk == 0)
        def _():
            acc_ref[...] = mm(jnp.concatenate([dq_ref[...], dk_ref[...]], axis=1))

        @pl.when(kk == 1)
        def _():
            acc_ref[...] += mm(jnp.concatenate([dv_ref[...], du_ref[...]], axis=1))

        @pl.when(kk >= 2)
        def _():
            acc_ref[...] += mm(dg_ref[...])

        @pl.when(kk == N_CHIPS - 1)
        def _():
            o_ref[...] = acc_ref[...]

    out, moved = _pcall(
        body, name, (t // tm, N_CHIPS),
        _proj_pieces(tm, True)
        + [pl.BlockSpec((tm, nb), lambda i, kk: (i, jnp.maximum(kk - 2, 0))),
           pl.BlockSpec((None, None, k, nb), lambda i, kk: (l, kk, 0, 0))],
        pl.BlockSpec((tm, k), lambda i, kk: (i, 0)), _sds((t, k), F32),
        [pltpu.VMEM((tm, k), F32)], _cparams("arbitrary", "arbitrary"),
        (dqkv, dqkv, dqkv, du, dgates, w), comm)
    return out if comm is None else (out, moved)


def _proj_dw(name, h, dqkv, du, dgates, l, fill):
    t, k = h.shape
    nb = dgates.shape[1] // 2
    tt = _tm(t)
    nt = t // tt

    def body(*refs):
        h_ref, dq_ref, dk_ref, dv_ref, du_ref, dg_ref = refs[:6]
        o_ref, acc_ref = refs[-2], refs[-1]
        n, kk = pl.program_id(0), pl.program_id(1)

        def update(g):
            part = lax.dot_general(h_ref[...], g, (TN, ((), ())), preferred_element_type=F32)

            @pl.when(kk == 0)
            def _():
                acc_ref[...] = part

            @pl.when(kk > 0)
            def _():
                acc_ref[...] += part

        @pl.when(n == 0)
        def _():
            update(jnp.concatenate([dq_ref[...], dk_ref[...]], axis=1))

        @pl.when(n == 1)
        def _():
            update(jnp.concatenate([dv_ref[...], du_ref[...]], axis=1))

        @pl.when(n >= 2)
        def _():
            update(dg_ref[...])

        @pl.when(kk == nt - 1)
        def _():
            o_ref[...] = acc_ref[...].astype(BF16)

    in_specs = ([pl.BlockSpec((tt, k), lambda n, kk: (kk, 0))] + _proj_pieces(tt, False)
                + [pl.BlockSpec((tt, nb), lambda n, kk: (kk, jnp.maximum(n - 2, 0)))])
    args, aliases = [h, dqkv, dqkv, dqkv, du, dgates], {}
    if fill is not None:
        in_specs.append(pl.BlockSpec(memory_space=pl.ANY))
        args.append(fill)
        aliases = {6: 0}
    return pl.pallas_call(
        body, name=name, grid=(N_CHIPS, nt), in_specs=in_specs,
        out_specs=pl.BlockSpec((None, None, k, nb), lambda n, kk: (l, n, 0, 0)),
        out_shape=_sds((DEPTH, N_CHIPS, k, nb), BF16),
        scratch_shapes=[pltpu.VMEM((k, nb), F32)], input_output_aliases=aliases,
        compiler_params=_cparams("parallel", "arbitrary"))(*args)


def _narrow_nn(name, a, w, l):
    t, k = a.shape
    nb = w.shape[3]
    tm = _tm(t)

    def body(a_ref, w_ref, o_ref):
        av = a_ref[...]
        for j in range(N_CHIPS):
            o_ref[:, j * nb:(j + 1) * nb] = jnp.dot(
                av, w_ref[j], preferred_element_type=F32).astype(BF16)

    return pl.pallas_call(
        body, name=name, grid=(t // tm,),
        in_specs=[pl.BlockSpec((tm, k), lambda i: (i, 0)),
                  pl.BlockSpec((None, N_CHIPS, k, nb), lambda i: (l, 0, 0, 0))],
        out_specs=pl.BlockSpec((tm, N_CHIPS * nb), lambda i: (i, 0)),
        out_shape=_sds((t, N_CHIPS * nb), BF16), compiler_params=_cparams("parallel"))(a, w)


def _narrow_nt(name, a, w, l):
    t = a.shape[0]
    k, nb = w.shape[2], w.shape[3]
    tm = _tm(t)

    def body(a_ref, w_ref, o_ref):
        acc = lax.dot_general(a_ref[:, 0:nb], w_ref[0], (NT, ((), ())), preferred_element_type=F32)
        for j in range(1, N_CHIPS):
            acc = acc + lax.dot_general(a_ref[:, j * nb:(j + 1) * nb], w_ref[j], (NT, ((), ())),
                                        preferred_element_type=F32)
        o_ref[...] = acc.astype(BF16)

    return pl.pallas_call(
        body, name=name, grid=(t // tm,),
        in_specs=[pl.BlockSpec((tm, N_CHIPS * nb), lambda i: (i, 0)),
                  pl.BlockSpec((None, N_CHIPS, k, nb), lambda i: (l, 0, 0, 0))],
        out_specs=pl.BlockSpec((tm, k), lambda i: (i, 0)),
        out_shape=_sds((t, k), BF16), compiler_params=_cparams("parallel"))(a, w)


def _narrow_tn(name, a, g, l, fill):
    t, k = a.shape
    nb = g.shape[1] // N_CHIPS
    tt = _tm(t)
    nt = t // tt

    def body(*refs):
        a_ref, g_ref, o_ref, acc_ref = refs[0], refs[1], refs[-2], refs[-1]
        i = pl.program_id(0)
        part = lax.dot_general(a_ref[...], g_ref[...], (TN, ((), ())), preferred_element_type=F32)

        @pl.when(i == 0)
        def _():
            acc_ref[...] = part

        @pl.when(i > 0)
        def _():
            acc_ref[...] += part

        @pl.when(i == nt - 1)
        def _():
            for j in range(N_CHIPS):
                o_ref[j] = acc_ref[:, j * nb:(j + 1) * nb].astype(BF16)

    in_specs = [pl.BlockSpec((tt, k), lambda i: (i, 0)),
                pl.BlockSpec((tt, N_CHIPS * nb), lambda i: (i, 0))]
    args, aliases = [a, g], {}
    if fill is not None:
        in_specs.append(pl.BlockSpec(memory_space=pl.ANY))
        args.append(fill)
        aliases = {2: 0}
    return pl.pallas_call(
        body, name=name, grid=(nt,), in_specs=in_specs,
        out_specs=pl.BlockSpec((None, N_CHIPS, k, nb), lambda i: (l, 0, 0, 0)),
        out_shape=_sds((DEPTH, N_CHIPS, k, nb), BF16),
        scratch_shapes=[pltpu.VMEM((k, N_CHIPS * nb), F32)], input_output_aliases=aliases,
        compiler_params=_cparams("arbitrary"))(*args)


def _mm_nn(name, a, w, l, tk, out_dtype):
    t, k = a.shape
    n = w.shape[2]
    tm = _tm(t)
    nk = k // tk
    return _matmul(
        name, a, w,
        pl.BlockSpec((tm, tk), lambda i, j, kk: (i, kk)),
        pl.BlockSpec((None, tk, n), lambda i, j, kk: (l, kk, 0)),
        pl.BlockSpec((tm, n), lambda i, j, kk: (i, 0)),
        _sds((t, n), out_dtype), (t // tm, 1, nk), NN, nk, (tm, n))


def _mm_nt(name, a, w, l, tn, out_dtype, comm=None):
    t, n = a.shape
    k = w.shape[1]
    tm = _tm(t)
    return _matmul(
        name, a, w,
        pl.BlockSpec((tm, n), lambda i, j, kk: (i, 0)),
        pl.BlockSpec((None, tn, n), lambda i, j, kk: (l, j, 0)),
        pl.BlockSpec((tm, tn), lambda i, j, kk: (i, j)),
        _sds((t, k), out_dtype), (t // tm, k // tn, 1), NT, 1, None, comm=comm)


def _mm_tn(name, a, g, tko, l, fill):
    t, k = a.shape
    n = g.shape[1]
    tt = _tt(t)
    nt = t // tt
    return _matmul(
        name, a, g,
        pl.BlockSpec((tt, tko), lambda i, j, kk: (kk, i)),
        pl.BlockSpec((tt, n), lambda i, j, kk: (kk, 0)),
        pl.BlockSpec((None, tko, n), lambda i, j, kk: (l, i, 0)),
        _sds((DEPTH, k, n), BF16), (k // tko, 1, nt), TN, nt, (tko, n), fill)


def _row_spec(width, col=0):
    return pl.BlockSpec((TOK, width), lambda i: (i, col))


def _vec_spec(width):
    return pl.BlockSpec((1, width), lambda i: (0, 0))


def _rms(x):
    return lax.rsqrt(jnp.mean(x * x, axis=-1, keepdims=True) + EPS)


def _norm_fwd(name, x, g):
    t = x.shape[0]

    def body(x_ref, g_ref, h_ref):
        xv = x_ref[...]
        h_ref[...] = (xv * _rms(xv) * g_ref[...]).astype(BF16)

    return pl.pallas_call(
        body, name=name, grid=(t // TOK,), in_specs=[_row_spec(D_MODEL), _vec_spec(D_MODEL)],
        out_specs=_row_spec(D_MODEL), out_shape=_sds((t, D_MODEL), BF16),
        compiler_params=_cparams("parallel"))(x, g)


ROWS = 16
ROW_UNROLL = 8


def _rows(k):
    return pl.ds(pl.multiple_of(k * ROWS, ROWS), ROWS)


def _strips(step, init):
    def group(j, carry):
        for u in range(ROW_UNROLL):
            carry = step(j * ROW_UNROLL + u, carry)
        return carry

    return lax.fori_loop(0, TOK // (ROWS * ROW_UNROLL), group, init)


def _fold_rows(x):
    return x[0:8] + x[8:16]


def _accumulate(ref, part):
    total = jnp.sum(part, axis=0, keepdims=True)

    @pl.when(pl.program_id(0) == 0)
    def _():
        ref[...] = total

    @pl.when(pl.program_id(0) > 0)
    def _():
        ref[...] += total


def _norm_bwd_rows(d, mv, g):
    r = _rms(mv)
    n = mv * r
    dn = d * g
    return r * (dn - n * jnp.mean(dn * n, axis=-1, keepdims=True)), d * n


def _post_pre_fwd(name, xres, m, g_post, g_pre, comm=None):
    t = xres.shape[0]

    def body(x_ref, m_ref, gp_ref, gn_ref, x1_ref, h_ref):
        def strip(k, c):
            rows = _rows(k)
            mv = m_ref[rows, :]
            x1 = x_ref[rows, :] + mv * _rms(mv) * gp_ref[...]
            x1_ref[rows, :] = x1
            h_ref[rows, :] = (x1 * _rms(x1) * gn_ref[...]).astype(BF16)
            return c

        _strips(strip, 0)

    outs, moved = _pcall(
        body, name, (t // TOK,),
        [_row_spec(D_MODEL), _row_spec(D_MODEL), _vec_spec(D_MODEL), _vec_spec(D_MODEL)],
        [_row_spec(D_MODEL), _row_spec(D_MODEL)],
        [_sds((t, D_MODEL), F32), _sds((t, D_MODEL), BF16)], [], _cparams("arbitrary"),
        (xres, m, g_post, g_pre), comm)
    return outs if comm is None else (*outs, moved)


def _tail(name, xres, m, g_post, target):
    t = xres.shape[0]

    def body(x_ref, m_ref, g_ref, t_ref, dy_ref, dm_ref, dg_ref, l_ref):
        def strip(k, carry):
            rows = _rows(k)
            mv = m_ref[rows, :]
            e = x_ref[rows, :] + mv * _rms(mv) * g_ref[...] - t_ref[rows, :]
            dy = e * (1.0 / D_MODEL)
            dy_ref[rows, :] = dy
            dm, dgn = _norm_bwd_rows(dy, mv, g_ref[...])
            dm_ref[rows, :] = dm.astype(BF16)
            return carry[0] + _fold_rows(dgn), carry[1] + _fold_rows(e * e)

        zero = jnp.zeros((8, D_MODEL), F32)
        dg, sq = _strips(strip, (zero, zero))
        _accumulate(dg_ref, dg)
        _accumulate(l_ref, jnp.sum(sq, axis=1, keepdims=True))

    dy, dm, dg, sq = pl.pallas_call(
        body, name=name, grid=(t // TOK,),
        in_specs=[_row_spec(D_MODEL), _row_spec(D_MODEL), _vec_spec(D_MODEL), _row_spec(D_MODEL)],
        out_specs=[_row_spec(D_MODEL), _row_spec(D_MODEL), _vec_spec(D_MODEL),
                   pl.BlockSpec((1, 1), lambda i: (0, 0))],
        out_shape=[_sds((t, D_MODEL), F32), _sds((t, D_MODEL), BF16), _sds((1, D_MODEL), F32),
                   _sds((1, 1), F32)],
        compiler_params=_cparams("arbitrary"))(xres, m, g_post, target)
    return dy, dm, dg, sq[0, 0] * (0.5 / D_MODEL)


def _pre_post_bwd(name, dh, xin, dxo, g_pre, m, g_post, comm=None):
    t = dh.shape[0]

    def body(dh_ref, x_ref, d_ref, gq_ref, m_ref, gp_ref, dx_ref, dgq_ref, dm_ref, dgp_ref):
        def strip(k, carry):
            rows = _rows(k)
            dxin, dgq = _norm_bwd_rows(dh_ref[rows, :], x_ref[rows, :], gq_ref[...])
            dx = d_ref[rows, :] + dxin
            dx_ref[rows, :] = dx
            dm, dgp = _norm_bwd_rows(dx, m_ref[rows, :], gp_ref[...])
            dm_ref[rows, :] = dm.astype(BF16)
            return carry[0] + _fold_rows(dgq), carry[1] + _fold_rows(dgp)

        zero = jnp.zeros((8, D_MODEL), F32)
        dgq, dgp = _strips(strip, (zero, zero))
        _accumulate(dgq_ref, dgq)
        _accumulate(dgp_ref, dgp)

    outs, moved = _pcall(
        body, name, (t // TOK,),
        [_row_spec(D_MODEL), _row_spec(D_MODEL), _row_spec(D_MODEL), _vec_spec(D_MODEL),
         _row_spec(D_MODEL), _vec_spec(D_MODEL)],
        [_row_spec(D_MODEL), _vec_spec(D_MODEL), _row_spec(D_MODEL), _vec_spec(D_MODEL)],
        [_sds((t, D_MODEL), F32), _sds((1, D_MODEL), F32), _sds((t, D_MODEL), BF16),
         _sds((1, D_MODEL), F32)], [], _cparams("arbitrary"),
        (dh, xin, dxo, g_pre, m, g_post), comm)
    return outs if comm is None else (*outs, moved)


def _norm_pre_bwd(name, dh, xin, dxo, g, comm=None):
    t = dh.shape[0]

    def body(dh_ref, x_ref, d_ref, g_ref, dx_ref, dg_ref):
        xv = x_ref[...]
        dhv = dh_ref[...]
        r = _rms(xv)
        n = xv * r
        dn = dhv * g_ref[...]
        dx_ref[...] = d_ref[...] + r * (dn - n * jnp.mean(dn * n, axis=-1, keepdims=True))
        part = jnp.sum(dhv * n, axis=0, keepdims=True)

        @pl.when(pl.program_id(0) == 0)
        def _():
            dg_ref[...] = part

        @pl.when(pl.program_id(0) > 0)
        def _():
            dg_ref[...] += part

    out, moved = _pcall(
        body, name, (t // TOK,),
        [_row_spec(D_MODEL), _row_spec(D_MODEL), _row_spec(D_MODEL), _vec_spec(D_MODEL)],
        [_row_spec(D_MODEL), _vec_spec(D_MODEL)],
        [_sds((t, D_MODEL), F32), _sds((1, D_MODEL), F32)], [], _cparams("arbitrary"),
        (dh, xin, dxo, g), comm)
    return out if comm is None else (*out, moved)


def _gate_fwd(name, proj, b_gate, ya, yb):
    t = proj.shape[0]

    def body(ga_ref, gb_ref, b_ref, ya_ref, yb_ref, z_ref):
        def strip(k, c):
            rows = _rows(k)
            sa = jax.nn.sigmoid(ga_ref[rows, :].astype(F32) + b_ref[:, :D_MODEL])
            sb = jax.nn.sigmoid(gb_ref[rows, :].astype(F32) + b_ref[:, D_MODEL:])
            z_ref[rows, :] = (sa * ya_ref[rows, :].astype(F32)
                              + sb * yb_ref[rows, :].astype(F32)).astype(BF16)
            return c

        _strips(strip, 0)

    return pl.pallas_call(
        body, name=name, grid=(t // TOK,),
        in_specs=[_row_spec(D_MODEL, 2), _row_spec(D_MODEL, 3), _vec_spec(2 * D_MODEL),
                  _row_spec(D_MODEL), _row_spec(D_MODEL)],
        out_specs=_row_spec(D_MODEL), out_shape=_sds((t, D_MODEL), BF16),
        compiler_params=_cparams("parallel"))(proj, proj, b_gate, ya, yb)


def _gate_bwd(name, dz, proj, b_gate, ya, yb):
    t = proj.shape[0]

    def body(dz_ref, ga_ref, gb_ref, b_ref, ya_ref, yb_ref, dya_ref, dyb_ref, dg_ref, db_ref):
        def strip(k, carry):
            rows = _rows(k)
            dzv = dz_ref[rows, :].astype(F32)
            sa = jax.nn.sigmoid(ga_ref[rows, :].astype(F32) + b_ref[:, :D_MODEL])
            sb = jax.nn.sigmoid(gb_ref[rows, :].astype(F32) + b_ref[:, D_MODEL:])
            dya_ref[rows, :] = (dzv * sa).astype(BF16)
            dyb_ref[rows, :] = (dzv * sb).astype(BF16)
            dga = dzv * ya_ref[rows, :].astype(F32) * sa * (1.0 - sa)
            dgb = dzv * yb_ref[rows, :].astype(F32) * sb * (1.0 - sb)
            dg_ref[rows, :D_MODEL] = dga.astype(BF16)
            dg_ref[rows, D_MODEL:] = dgb.astype(BF16)
            return carry[0] + _fold_rows(dga), carry[1] + _fold_rows(dgb)

        zero = jnp.zeros((8, D_MODEL), F32)
        pa, pb = _strips(strip, (zero, zero))
        _accumulate(db_ref.at[:, :D_MODEL], pa)
        _accumulate(db_ref.at[:, D_MODEL:], pb)

    return pl.pallas_call(
        body, name=name, grid=(t // TOK,),
        in_specs=[_row_spec(D_MODEL), _row_spec(D_MODEL, 2), _row_spec(D_MODEL, 3),
                  _vec_spec(2 * D_MODEL), _row_spec(D_MODEL), _row_spec(D_MODEL)],
        out_specs=[_row_spec(D_MODEL), _row_spec(D_MODEL), _row_spec(2 * D_MODEL),
                   _vec_spec(2 * D_MODEL)],
        out_shape=[_sds((t, D_MODEL), BF16), _sds((t, D_MODEL), BF16),
                   _sds((t, 2 * D_MODEL), BF16), _sds((1, 2 * D_MODEL), F32)],
        compiler_params=_cparams("arbitrary"))(dz, proj, proj, b_gate, ya, yb)


def _head_masks():
    lane = lax.broadcasted_iota(jnp.int32, (1, 2 * HEAD_DIM), 1)
    return lane < HEAD_DIM


BAND_ROWS = 2 * ATT_BLK + CHUNK


def _fill_band(band, prev_ref, cur_ref):
    band[0:ATT_BLK, :] = prev_ref[...]
    band[ATT_BLK:2 * ATT_BLK, :] = cur_ref[...]
    band[2 * ATT_BLK:, :] = jnp.zeros((CHUNK, ATTN_W), BF16)


def _pair_rows(x2, low):
    zero = jnp.zeros_like(x2)
    return jnp.concatenate([jnp.where(low, x2, zero), jnp.where(low, zero, x2)], axis=0)


def _pair_diag(o2, low):
    return jnp.where(low, o2[0:CHUNK, :], o2[CHUNK:, :])


N_PAIRS = HEADS // 2
SM_STRIP = 32
N_STRIPS = BAND_PAD // SM_STRIP
NEG = -1e30


def _fold8(x, op):
    return op(op(x[0:8], x[8:16]), op(x[16:24], x[24:32]))


def _strip(k):
    return pl.ds(pl.multiple_of(k * SM_STRIP, SM_STRIP), SM_STRIP)


def _band_probs(k2, qcat, bias_t, first_key):
    kpos = lax.broadcasted_iota(jnp.int32, (BAND_PAD, 1), 0)
    st = lax.dot_general(k2, qcat, (NT, ((), ())), preferred_element_type=F32)
    st = jnp.where(kpos + first_key >= 0, st + bias_t, NEG)
    e = jnp.exp(st - jnp.max(st, axis=0, keepdims=True))
    return e * (1.0 / jnp.sum(e, axis=0, keepdims=True))


def _band_softmax_stats(st_ref, b_ref, first_key, dp_ref):
    rowi = lax.broadcasted_iota(jnp.int32, (SM_STRIP, 128), 0)

    def scores(k, mx):
        rows = _strip(k)
        live = (rowi + (k * SM_STRIP + first_key)) >= 0
        out = []
        for hp in range(N_PAIRS):
            x = jnp.where(live, st_ref[hp, rows, :] + b_ref[hp, rows, :], NEG)
            st_ref[hp, rows, :] = x
            out.append(jnp.maximum(mx[hp], _fold8(x, jnp.maximum)))
        return tuple(out)

    mx = lax.fori_loop(0, N_STRIPS, scores, (jnp.full((8, 128), NEG, F32),) * N_PAIRS, unroll=2)
    top = [jnp.max(m, axis=0, keepdims=True) for m in mx]

    def sums(k, acc):
        rows = _strip(k)
        ls, eds = [], []
        for hp in range(N_PAIRS):
            e = jnp.exp(st_ref[hp, rows, :] - top[hp])
            ls.append(acc[hp] + _fold8(e, jnp.add))
            eds.append(acc[N_PAIRS + hp] + _fold8(e * dp_ref[hp, rows, :], jnp.add))
        return tuple(ls + eds)

    acc = lax.fori_loop(0, N_STRIPS, sums, (jnp.zeros((8, 128), F32),) * (2 * N_PAIRS), unroll=2)
    inv = [1.0 / jnp.sum(a, axis=0, keepdims=True) for a in acc[:N_PAIRS]]
    delta = [jnp.sum(a, axis=0, keepdims=True) * i for a, i in zip(acc[N_PAIRS:], inv)]
    return top, inv, delta


def _attn_specs(nblk):
    cur = lambda col: pl.BlockSpec((ATT_BLK, ATTN_W), lambda s: (jnp.minimum(s, nblk - 1), col))
    prev = lambda col: pl.BlockSpec(
        (ATT_BLK, ATTN_W), lambda s: (jnp.maximum(jnp.minimum(s, nblk - 1) - 1, 0), col))
    return cur, prev


def _attn_fwd(name, proj, bias, comm=None):
    t = proj.shape[0]
    nblk = t // ATT_BLK
    cur, prev = _attn_specs(nblk)

    def body(q_ref, kp_ref, kc_ref, vp_ref, vc_ref, b_ref, o_ref, kband, vband):
        s = pl.program_id(0)
        _fill_band(kband, kp_ref, kc_ref)
        _fill_band(vband, vp_ref, vc_ref)
        low = _head_masks()

        def chunk(ci, carry):
            r0 = pl.multiple_of(ci * CHUNK, CHUNK)
            for hp in range(N_PAIRS):
                cols = slice(hp * 128, (hp + 1) * 128)
                qcat = _pair_rows(q_ref[pl.ds(r0, CHUNK), cols] * ATTN_SCALE, low)
                p = _band_probs(kband[pl.ds(r0, BAND_PAD), cols], qcat, b_ref[hp],
                                (s * 8 - 8 + ci) * CHUNK)
                o2 = lax.dot_general(p.astype(BF16), vband[pl.ds(r0, BAND_PAD), cols],
                                     (TN, ((), ())), preferred_element_type=F32)
                o_ref[pl.ds(r0, CHUNK), cols] = _pair_diag(o2, low).astype(BF16)
            return carry

        lax.fori_loop(0, 8, chunk, 0)

    out, moved = _pcall(
        body, name, (nblk,),
        [cur(0), prev(1), cur(1), prev(2), cur(2),
         pl.BlockSpec((N_PAIRS, BAND_PAD, 128), lambda s: (0, 0, 0))],
        pl.BlockSpec((ATT_BLK, ATTN_W), lambda s: (s, 0)), _sds((t, ATTN_W), BF16),
        [pltpu.VMEM((BAND_ROWS, ATTN_W), BF16), pltpu.VMEM((BAND_ROWS, ATTN_W), BF16)],
        _cparams("arbitrary"), (proj, proj, proj, proj, proj, bias), comm)
    return out if comm is None else (out, moved)


def _attn_bwd(name, proj, datt, bias, comm=None):
    t = proj.shape[0]
    nblk = t // ATT_BLK
    cur, prev = _attn_specs(nblk)
    late = pl.BlockSpec((ATT_BLK, 3 * ATTN_W), lambda s: (jnp.maximum(s - 1, 0), 0))

    def body(q_ref, kp_ref, kc_ref, vp_ref, vc_ref, do_ref, b_ref,
             dqkv_ref, db_ref, kband, vband, dkacc, dvacc,
             st_ref, dp_ref, pb_ref, dsb_ref, qc_ref, dc_ref, dq_ref, dq_held):
        s = pl.program_id(0)

        @pl.when(s == 0)
        def _():
            dkacc[...] = jnp.zeros_like(dkacc)
            dvacc[...] = jnp.zeros_like(dvacc)
            db_ref[...] = jnp.zeros_like(db_ref)
            dq_ref[...] = jnp.zeros_like(dq_ref)

        @pl.when(s < nblk)
        def _():
            _fill_band(kband, kp_ref, kc_ref)
            _fill_band(vband, vp_ref, vc_ref)
            low = _head_masks()

            def chunk(ci, carry):
                r0 = pl.multiple_of(ci * CHUNK, CHUNK)
                for hp in range(N_PAIRS):
                    cols = slice(hp * 128, (hp + 1) * 128)
                    qc_ref[hp] = _pair_rows(q_ref[pl.ds(r0, CHUNK), cols] * ATTN_SCALE, low)
                    dc_ref[hp] = _pair_rows(do_ref[pl.ds(r0, CHUNK), cols], low)
                    st_ref[hp] = lax.dot_general(kband[pl.ds(r0, BAND_PAD), cols], qc_ref[hp],
                                                 (NT, ((), ())), preferred_element_type=F32)
                    dp_ref[hp] = lax.dot_general(vband[pl.ds(r0, BAND_PAD), cols], dc_ref[hp],
                                                 (NT, ((), ())), preferred_element_type=F32)
                top, inv, delta = _band_softmax_stats(st_ref, b_ref, (s * 8 - 8 + ci) * CHUNK,
                                                      dp_ref)

                def grads(k, c):
                    rows = _strip(k)
                    for hp in range(N_PAIRS):
                        p = jnp.exp(st_ref[hp, rows, :] - top[hp]) * inv[hp]
                        ds = p * (dp_ref[hp, rows, :] - delta[hp])
                        db_ref[hp, rows, :] += ds
                        dsb_ref[hp, rows, :] = ds.astype(BF16)
                        pb_ref[hp, rows, :] = p.astype(BF16)
                    return c

                lax.fori_loop(0, N_STRIPS, grads, 0, unroll=2)
                for hp in range(N_PAIRS):
                    cols = slice(hp * 128, (hp + 1) * 128)
                    dq2 = lax.dot_general(dsb_ref[hp], kband[pl.ds(r0, BAND_PAD), cols],
                                          (TN, ((), ())), preferred_element_type=F32)
                    dq_ref[pl.ds(r0, CHUNK), cols] = (_pair_diag(dq2, low) * ATTN_SCALE).astype(BF16)
                    dkacc[pl.ds(r0, BAND_PAD), cols] += jnp.dot(dsb_ref[hp], qc_ref[hp],
                                                               preferred_element_type=F32)
                    dvacc[pl.ds(r0, BAND_PAD), cols] += jnp.dot(pb_ref[hp], dc_ref[hp],
                                                               preferred_element_type=F32)
                return carry

            dq_held[...] = dq_ref[...]
            lax.fori_loop(0, 8, chunk, 0)

        @pl.when(s == nblk)
        def _():
            dq_held[...] = dq_ref[...]

        dqkv_ref[:, 0:ATTN_W] = dq_held[...]
        dqkv_ref[:, ATTN_W:2 * ATTN_W] = dkacc[0:ATT_BLK, :].astype(BF16)
        dqkv_ref[:, 2 * ATTN_W:] = dvacc[0:ATT_BLK, :].astype(BF16)
        dkacc[0:ATT_BLK, :] = dkacc[ATT_BLK:2 * ATT_BLK, :]
        dvacc[0:ATT_BLK, :] = dvacc[ATT_BLK:2 * ATT_BLK, :]
        dkacc[ATT_BLK:, :] = jnp.zeros((ATT_BLK + CHUNK, ATTN_W), F32)
        dvacc[ATT_BLK:, :] = jnp.zeros((ATT_BLK + CHUNK, ATTN_W), F32)

    outs, moved = _pcall(
        body, name, (nblk + 1,),
        [cur(0), prev(1), cur(1), prev(2), cur(2),
         pl.BlockSpec((ATT_BLK, ATTN_W), lambda s: (jnp.minimum(s, nblk - 1), 0)),
         pl.BlockSpec((HEADS // 2, BAND_PAD, 128), lambda s: (0, 0, 0))],
        [late, pl.BlockSpec((HEADS // 2, BAND_PAD, 128), lambda s: (0, 0, 0))],
        [_sds((t, 3 * ATTN_W), BF16), _sds((HEADS // 2, BAND_PAD, 128), F32)],
        [pltpu.VMEM((BAND_ROWS, ATTN_W), BF16), pltpu.VMEM((BAND_ROWS, ATTN_W), BF16),
         pltpu.VMEM((BAND_ROWS, ATTN_W), F32), pltpu.VMEM((BAND_ROWS, ATTN_W), F32),
         pltpu.VMEM((N_PAIRS, BAND_PAD, 128), F32), pltpu.VMEM((N_PAIRS, BAND_PAD, 128), F32),
         pltpu.VMEM((N_PAIRS, BAND_PAD, 128), BF16), pltpu.VMEM((N_PAIRS, BAND_PAD, 128), BF16),
         pltpu.VMEM((N_PAIRS, 2 * CHUNK, 128), BF16), pltpu.VMEM((N_PAIRS, 2 * CHUNK, 128), BF16),
         pltpu.VMEM((ATT_BLK, ATTN_W), BF16), pltpu.VMEM((ATT_BLK, ATTN_W), BF16)],
        _cparams("arbitrary"), (proj, proj, proj, proj, proj, datt, bias), comm)
    return outs if comm is None else (*outs, moved)


def _diag_onehot(rel_rows):
    d0 = lax.broadcasted_iota(jnp.int32, (BIAS_LANES, BIAS_LANES), 0)
    d1 = lax.broadcasted_iota(jnp.int32, (BIAS_LANES, BIAS_LANES), 1)
    m, n = (d0, d1) if rel_rows else (d1, d0)
    hit = (m == jnp.minimum(BAND - 1 + MAX_REL - n, 2 * MAX_REL)) & (n < BAND + CHUNK - 1)
    return jnp.where(hit, 1.0, 0.0).astype(F32)


def _bias_table(name, rel_bias_l):
    rel_pad = jnp.pad(rel_bias_l, ((0, 0), (0, BIAS_LANES - N_REL)))

    def body(r_ref, o_ref):
        diag = jnp.dot(r_ref[...], _diag_onehot(True), preferred_element_type=F32,
                       precision=lax.Precision.HIGHEST)
        rowid = lax.broadcasted_iota(jnp.int32, (8, BIAS_LANES), 0)
        lane = lax.broadcasted_iota(jnp.int32, (8, BIAS_LANES), 1)
        for h in range(HEADS):
            d8 = jnp.broadcast_to(diag[h:h + 1, :], (8, BIAS_LANES))
            slab0 = pltpu.roll(d8, BIAS_LANES - CHUNK + 1, axis=1)
            for b in range(1, 8):
                slab0 = jnp.where(rowid == b, pltpu.roll(d8, BIAS_LANES - CHUNK + 1 + b, axis=1),
                                  slab0)
            for a in range(8):
                slab = slab0 if a == 0 else pltpu.roll(slab0, 8 * a, axis=1)
                o_ref[h * CHUNK + 8 * a:h * CHUNK + 8 * a + 8, :] = jnp.where(lane < BAND, slab, NEG)

    tab = pl.pallas_call(
        body, name=name,
        in_specs=[pl.BlockSpec(memory_space=pltpu.VMEM)],
        out_specs=pl.BlockSpec(memory_space=pltpu.VMEM),
        out_shape=_sds((HEADS * CHUNK, BIAS_LANES), F32),
    )(rel_pad)
    tab = tab.reshape(HEADS // 2, 2, CHUNK, BIAS_LANES)
    return jnp.transpose(tab, (0, 3, 1, 2)).reshape(HEADS // 2, BIAS_LANES, 2 * CHUNK)


def _bias_fold(name, dbias_t):
    rows = HEADS * CHUNK
    dbias = jnp.transpose(dbias_t.reshape(HEADS // 2, BIAS_LANES, 2, CHUNK), (0, 2, 3, 1))

    def body(d_ref, o_ref):
        rowid = lax.broadcasted_iota(jnp.int32, (8, BIAS_LANES), 0)
        diags = []
        for h in range(HEADS):
            acc = d_ref[h * CHUNK + 56:h * CHUNK + 64, :]
            for a in range(7):
                slab = d_ref[h * CHUNK + 8 * a:h * CHUNK + 8 * a + 8, :]
                acc = acc + pltpu.roll(slab, 56 - 8 * a, axis=1)
            tot = jnp.where(rowid == 7, acc, 0.0)
            for b in range(7):
                tot = tot + jnp.where(rowid == b, pltpu.roll(acc, 7 - b, axis=1), 0.0)
            diags.append(jnp.sum(tot, axis=0, keepdims=True))
        diag = jnp.concatenate(diags, axis=0)
        o_ref[...] = jnp.dot(diag, _diag_onehot(False), preferred_element_type=F32,
                             precision=lax.Precision.HIGHEST)

    return pl.pallas_call(
        body, name=name,
        in_specs=[pl.BlockSpec(memory_space=pltpu.VMEM)],
        out_specs=pl.BlockSpec(memory_space=pltpu.VMEM),
        out_shape=_sds((HEADS, BIAS_LANES), F32),
    )(dbias.reshape(rows, BIAS_LANES))


def _inv_counts(i):
    trow = lax.broadcasted_iota(jnp.int32, (TOK + HALO, 1), 0) + i * TOK
    return [1.0 / jnp.minimum(trow + 1, w).astype(F32) for w in POOL_WINDOWS]


def _pool_fwd(name, proj, wg, scale, comm=None):
    t = proj.shape[0]
    hb = TOK // HALO

    def body(u_ref, up_ref, wg_ref, sc_ref, pooled_ref, mixed_ref, b0, b1, b2, b3):
        i = pl.program_id(0)
        halo = up_ref[...].astype(F32)
        b0[0:HALO, :] = jnp.where(i == 0, jnp.zeros_like(halo), halo)
        b0[HALO:, :] = u_ref[...].astype(F32)
        n = TOK + HALO
        b1[8:n, :] = b0[8:n, :] + b0[7:n - 1, :]
        b2[16:n, 128:] = b1[16:n, 128:] + b1[14:n - 2, 128:]
        b3[24:n, 256:] = b2[24:n, 256:] + b2[20:n - 4, 256:]
        wins = [b1[HALO:n, 0:128], b2[HALO:n, 128:256], b3[HALO:n, 256:384],
                b3[HALO:n, 384:512] + b3[HALO - 8:n - 8, 384:512]]
        inv = _inv_counts(i)
        for g in range(4):
            cols = slice(g * POOL_GD, (g + 1) * POOL_GD)
            pooled = (wins[g] * inv[g][0:TOK] - b0[HALO:n, cols]).astype(BF16)
            pooled_ref[:, cols] = pooled
            pre = jnp.dot(pooled, wg_ref[g], preferred_element_type=F32)
            mixed_ref[:, cols] = (pre * sc_ref[:, cols]).astype(BF16)

    buf = pltpu.VMEM((TOK + HALO, POOL_W), F32)
    outs, moved = _pcall(
        body, name, (t // TOK,),
        [_row_spec(POOL_W, 3),
         pl.BlockSpec((HALO, POOL_W), lambda i: (jnp.maximum(i * hb - 1, 0), 3)),
         pl.BlockSpec((4, POOL_GD, POOL_GD), lambda i: (0, 0, 0)), _vec_spec(POOL_W)],
        [_row_spec(POOL_W), _row_spec(POOL_W)],
        [_sds((t, POOL_W), BF16), _sds((t, POOL_W), BF16)], [buf, buf, buf, buf],
        _cparams("arbitrary"), (proj, proj, wg, scale), comm)
    return outs if comm is None else (*outs, moved)


def _pool_bwd(name, dmixed, pooled, wg, scale, comm=None):
    t = dmixed.shape[0]
    nt = t // TOK
    hb = TOK // HALO

    def body(dm_ref, dmn_ref, p_ref, wg_ref, sc_ref, du_ref, dwg_ref, dsc_ref, c0, c1, c2, c3):
        i = pl.program_id(0)

        @pl.when(i == 0)
        def _():
            dwg_ref[...] = jnp.zeros_like(dwg_ref)
            dsc_ref[...] = jnp.zeros_like(dsc_ref)

        n = TOK + HALO
        inv = _inv_counts(i)
        dmv = dm_ref[...].astype(F32)
        dmn = dmn_ref[...].astype(F32)
        dmn = jnp.where(i == nt - 1, jnp.zeros_like(dmn), dmn)
        for g in range(4):
            cols = slice(g * POOL_GD, (g + 1) * POOL_GD)
            scg = sc_ref[:, cols]
            pg = p_ref[:, cols]
            dpre = (dmv[:, cols] * scg).astype(BF16)
            dpre_n = (dmn[:, cols] * scg).astype(BF16)
            pre = jnp.dot(pg, wg_ref[g], preferred_element_type=F32)
            dsc_ref[:, cols] += jnp.sum(dmv[:, cols] * pre, axis=0, keepdims=True)
            dwg_ref[g] += lax.dot_general(pg, dpre, (TN, ((), ())), preferred_element_type=F32)
            dpool = lax.dot_general(dpre, wg_ref[g], (NT, ((), ())), preferred_element_type=F32)
            dpool_n = lax.dot_general(dpre_n, wg_ref[g], (NT, ((), ())),
                                      preferred_element_type=F32)
            c0[0:TOK, cols] = dpool
            c0[TOK:n, cols] = dpool_n
            c1[0:TOK, cols] = dpool * inv[g][0:TOK]
            c1[TOK:n, cols] = dpool_n * inv[g][TOK:n]
        c2[0:n - 8, :] = c1[0:n - 8, :] + c1[1:n - 7, :]
        c3[0:n - 16, 128:] = c2[0:n - 16, 128:] + c2[2:n - 14, 128:]
        c1[0:n - 24, 256:] = c3[0:n - 24, 256:] + c3[4:n - 20, 256:]
        wins = [c2[0:TOK, 0:128], c3[0:TOK, 128:256], c1[0:TOK, 256:384],
                c1[0:TOK, 384:512] + c1[8:TOK + 8, 384:512]]
        for g in range(4):
            cols = slice(g * POOL_GD, (g + 1) * POOL_GD)
            du_ref[:, cols] = (wins[g] - c0[0:TOK, cols]).astype(BF16)

    buf = pltpu.VMEM((TOK + HALO, POOL_W), F32)
    outs, moved = _pcall(
        body, name, (nt,),
        [_row_spec(POOL_W),
         pl.BlockSpec((HALO, POOL_W), lambda i: (jnp.minimum((i + 1) * hb, nt * hb - 1), 0)),
         _row_spec(POOL_W), pl.BlockSpec((4, POOL_GD, POOL_GD), lambda i: (0, 0, 0)),
         _vec_spec(POOL_W)],
        [_row_spec(POOL_W), pl.BlockSpec((4, POOL_GD, POOL_GD), lambda i: (0, 0, 0)),
         _vec_spec(POOL_W)],
        [_sds((t, POOL_W), BF16), _sds((4, POOL_GD, POOL_GD), F32), _sds((1, POOL_W), F32)],
        [buf, buf, buf, buf], _cparams("arbitrary"), (dmixed, dmixed, pooled, wg, scale), comm)
    return outs if comm is None else (*outs, moved)


GELU_C = math.sqrt(2.0 / math.pi)


GELU_K = 0.044715


def _gelu_parts(x):
    x2 = x * x
    s = 0.5 + 0.5 * jnp.tanh(x * (GELU_C + (GELU_C * GELU_K) * x2))
    return x * s, s, x2


def _gelu(x):
    return _gelu_parts(x)[0]


def _gelu_and_grad(x):
    g, s, x2 = _gelu_parts(x)
    return g, s + g * (1.0 - s) * ((2 * GELU_C) + (6 * GELU_C * GELU_K) * x2)


def _taps(buf, r, rows):
    a = buf[pl.ds(r, rows + 8), :]
    return a[8:], pltpu.roll(a, 1, axis=0)[8:], pltpu.roll(a, 2, axis=0)[8:]


def _conv(taps, w_ref, b_ref):
    return b_ref[...] + w_ref[2:3, :] * taps[0] + w_ref[1:2, :] * taps[1] + w_ref[0:1, :] * taps[2]


def _stage(dst, prev_ref, cur_ref, next_ref, first, last):
    rows = cur_ref.shape[0]
    h = prev_ref[...].astype(F32)
    dst[0:8, :] = jnp.where(first, jnp.zeros_like(h), h)
    dst[8:8 + rows, :] = cur_ref[...].astype(F32)
    if next_ref is not None:
        h = next_ref[...].astype(F32)
        dst[8 + rows:, :] = jnp.where(last, jnp.zeros_like(h), h)


FWD_STRIP = 32
BWD_STRIP = 16


def _ffn_gate_fwd(name, hu, conv_w, conv_b, comm=None):
    t = hu.shape[0]
    ncol = D_FF // FF_COL
    hb = FF_TOK // 8

    def tile(off):
        return pl.BlockSpec((FF_TOK, FF_COL), lambda i, j: (i, j + off))

    def halo(off):
        return pl.BlockSpec((8, FF_COL), lambda i, j: (jnp.maximum(i * hb - 1, 0), j + off))

    def wspec(off):
        return pl.BlockSpec((3, FF_COL), lambda i, j: (0, j + off))

    def bspec(off):
        return pl.BlockSpec((1, FF_COL), lambda i, j: (0, j + off))

    def body(v_ref, vp_ref, g_ref, gp_ref, wv_ref, wg_ref, bv_ref, bg_ref, a_ref, vb, gb):
        first = pl.program_id(0) == 0
        _stage(vb, vp_ref, v_ref, None, first, None)
        _stage(gb, gp_ref, g_ref, None, first, None)

        def strip(k, carry):
            r = pl.multiple_of(k * FWD_STRIP, FWD_STRIP)
            val = _conv(_taps(vb, r, FWD_STRIP), wv_ref, bv_ref)
            gate = _conv(_taps(gb, r, FWD_STRIP), wg_ref, bg_ref)
            a_ref[pl.ds(r, FWD_STRIP), :] = (_gelu(gate) * val).astype(BF16)
            return carry

        lax.fori_loop(0, FF_TOK // FWD_STRIP, strip, 0)

    buf = pltpu.VMEM((FF_TOK + 8, FF_COL), F32)
    out, moved = _pcall(
        body, name, (t // FF_TOK, ncol),
        [tile(0), halo(0), tile(ncol), halo(ncol), wspec(0), wspec(ncol), bspec(0), bspec(ncol)],
        pl.BlockSpec((FF_TOK, FF_COL), lambda i, j: (i, j)), _sds((t, D_FF), BF16), [buf, buf],
        _cparams("arbitrary", "arbitrary"),
        (hu, hu, hu, hu, conv_w, conv_w, conv_b, conv_b), comm)
    return out if comm is None else (out, moved)


def _ffn_gate_bwd(name, da, hu, conv_w, conv_b, comm=None):
    t = hu.shape[0]
    nt = t // FF_TOK
    ncol = D_FF // FF_COL
    hb = FF_TOK // 8
    ext = FF_TOK + 8

    def tile(off):
        return pl.BlockSpec((FF_TOK, FF_COL), lambda j, i: (i, j + off))

    def prev(off):
        return pl.BlockSpec((8, FF_COL), lambda j, i: (jnp.maximum(i * hb - 1, 0), j + off))

    def nxt(off):
        return pl.BlockSpec((8, FF_COL), lambda j, i: (jnp.minimum((i + 1) * hb, nt * hb - 1), j + off))

    def wspec(off):
        return pl.BlockSpec((3, FF_COL), lambda j, i: (0, j + off))

    def bspec(off):
        return pl.BlockSpec((1, FF_COL), lambda j, i: (0, j + off))

    def body(da_ref, dan_ref, v_ref, vp_ref, vn_ref, g_ref, gp_ref, gn_ref,
             wv_ref, wg_ref, bv_ref, bg_ref, dh_ref, dwv_ref, dwg_ref, vb, gb, dab):
        i = pl.program_id(1)
        first, last = i == 0, i == nt - 1

        @pl.when(first)
        def _():
            dwv_ref[...] = jnp.zeros_like(dwv_ref)
            dwg_ref[...] = jnp.zeros_like(dwg_ref)

        _stage(vb, vp_ref, v_ref, vn_ref, first, last)
        _stage(gb, gp_ref, g_ref, gn_ref, first, last)
        dab[0:FF_TOK, :] = da_ref[...].astype(F32)
        h = dan_ref[...].astype(F32)
        dab[FF_TOK:, :] = jnp.where(last, jnp.zeros_like(h), h)

        def grads(r, rows):
            tv, tg = _taps(vb, r, rows), _taps(gb, r, rows)
            gate = _conv(tg, wg_ref, bg_ref)
            dav = dab[pl.ds(r, rows), :]
            g, dg = _gelu_and_grad(gate)
            dval = dav * g
            dgate = dav * _conv(tv, wv_ref, bv_ref) * dg
            return dval, dgate, tv, tg

        def fold(x):
            return x[0:8] + x[8:16]

        def strip(k, carry):
            r = pl.multiple_of(FF_TOK - BWD_STRIP - k * BWD_STRIP, BWD_STRIP)
            dval, dgate, tv, tg = grads(r, BWD_STRIP)
            new = (dval[0:8], dgate[0:8])
            for half, (d, nxt_rows, taps, w_ref, dw_ref) in enumerate((
                    (dval, carry[0], tv, wv_ref, dwv_ref), (dgate, carry[1], tg, wg_ref, dwg_ref))):
                e = jnp.concatenate([d, nxt_rows], axis=0)
                dh = (w_ref[2:3, :] * d
                      + w_ref[1:2, :] * pltpu.roll(e, BWD_STRIP + 7, axis=0)[0:BWD_STRIP]
                      + w_ref[0:1, :] * pltpu.roll(e, BWD_STRIP + 6, axis=0)[0:BWD_STRIP])
                dh_ref[half, pl.ds(r, BWD_STRIP), :] = dh.astype(BF16)
                dw_ref[0:8, :] += fold(d * taps[2])
                dw_ref[8:16, :] += fold(d * taps[1])
                dw_ref[16:24, :] += fold(d * taps[0])
                dw_ref[24:32, :] += fold(d)
            return new

        dval, dgate, _, _ = grads(FF_TOK, 8)
        lax.fori_loop(0, FF_TOK // BWD_STRIP, strip, (dval, dgate))

        @pl.when(last)
        def _():
            for dw_ref in (dwv_ref, dwg_ref):
                for q in range(4):
                    dw_ref[8 * q:8 * q + 1, :] = jnp.sum(dw_ref[8 * q:8 * q + 8, :], axis=0,
                                                         keepdims=True)

    hbuf = pltpu.VMEM((FF_TOK + 16, FF_COL), F32)
    acc = pl.BlockSpec((32, FF_COL), lambda j, i: (0, j))
    (dhu, dwv, dwg), moved = _pcall(
        body, name, (ncol, nt),
        [tile(0), nxt(0), tile(0), prev(0), nxt(0), tile(ncol), prev(ncol), nxt(ncol),
         wspec(0), wspec(ncol), bspec(0), bspec(ncol)],
        [pl.BlockSpec((2, FF_TOK, FF_COL), lambda j, i: (0, i, j)), acc, acc],
        [_sds((2, t, D_FF), BF16), _sds((32, D_FF), F32), _sds((32, D_FF), F32)],
        [hbuf, hbuf, pltpu.VMEM((ext, FF_COL), F32)], _cparams("arbitrary", "arbitrary"),
        (da, da, hu, hu, hu, hu, hu, hu, conv_w, conv_w, conv_b, conv_b), comm)
    dconv = jnp.concatenate([dwv, dwg], axis=1).reshape(4, 8, 2 * D_FF)[:, 0]
    return (dhu, dconv) if comm is None else (dhu, dconv, moved)


def _mesh_pos():
    x, y, c = lax.axis_index("x"), lax.axis_index("y"), lax.axis_index("c")
    return x, y, c, [(1 - x, y), (x, 1 - y), (1 - x, 1 - y)]


def _remote(src, dst, send_sems, recv_sems, i, dev):
    return pltpu.make_async_remote_copy(src_ref=src, dst_ref=dst, send_sem=send_sems.at[i],
                                        recv_sem=recv_sems.at[i], device_id=dev,
                                        device_id_type=MESH)


def _mine(c, rows):
    return pl.ds(pl.multiple_of(c * (rows // 2), 16), rows // 2)


def _gather_send(shards, conv_shard, gathered, l):
    nbig = len(shards)
    with_conv = conv_shard is not None
    if gathered is None:
        ins = list(shards) + ([conv_shard] if with_conv else [])
        outs = [_sds((DEPTH, N_CHIPS) + s.shape[1:], s.dtype) for s in ins]
        alias = {}
    else:
        ins = list(shards) + list(gathered)
        outs = [_sds(g.shape, g.dtype) for g in gathered]
        alias = {nbig + k: k for k in range(nbig)}

    def copies(cin, cout, ssem, rsem):
        x, y, c, chips = _mesh_pos()
        me = 2 * x + y
        out = []
        for k in range(nbig):
            rows = shards[k].shape[1]
            for j, (cx, cy) in enumerate(chips):
                out.append(_remote(cin[k].at[l, _mine(c, rows)], cout[k].at[l, me, _mine(c, rows)],
                                   ssem, rsem, 4 * k + j, (cx, cy, c)))
            out.append(_remote(cin[k].at[l], cout[k].at[l, me], ssem, rsem, 4 * k + 3,
                               (x, y, 1 - c)))
        if with_conv:
            base = 4 * nbig
            for j, (cx, cy) in enumerate(chips):
                out.append(_remote(cin[nbig].at[c], cout[nbig].at[c, me], ssem, rsem, base + j,
                                   (cx, cy, c)))
            for ll in range(DEPTH):
                out.append(_remote(cin[nbig].at[ll], cout[nbig].at[ll, me], ssem, rsem,
                                   base + 3 + ll, (x, y, 1 - c)))
        return out

    return _Comm(ins, outs, copies, 4 * nbig + 5, alias)


def _gather_forward(gathered, nbig, rows, l):
    with_conv = len(gathered) > nbig
    alias = {k: k for k in range(len(gathered))}

    def copies(cin, cout, ssem, rsem):
        x, y, c, chips = _mesh_pos()
        out = []
        for k in range(nbig):
            for j, (cx, cy) in enumerate(chips):
                blk = cout[k].at[l, 2 * cx + cy, _mine(c, rows[k])]
                out.append(_remote(blk, blk, ssem, rsem, 3 * k + j, (x, y, 1 - c)))
        if with_conv:
            for j, (cx, cy) in enumerate(chips):
                blk = cout[nbig].at[c, 2 * cx + cy]
                out.append(_remote(blk, blk, ssem, rsem, 3 * nbig + j, (x, y, 1 - c)))
        return out

    return _Comm(gathered, [_sds(g.shape, g.dtype) for g in gathered], copies, 3 * nbig + 3, alias)


def _reduce_swap(grads, l):
    def copies(cin, cout, ssem, rsem):
        x, y, c, _ = _mesh_pos()
        return [_remote(cin[k].at[l, :, _mine(1 - c, g.shape[2])], cout[k], ssem, rsem, k,
                        (x, y, 1 - c)) for k, g in enumerate(grads)]

    outs = [_sds((N_CHIPS, g.shape[2] // 2, g.shape[3]), g.dtype) for g in grads]
    return _Comm(grads, outs, copies, len(grads))


def _reduce_scatter(sums):
    def copies(cin, cout, ssem, rsem):
        x, y, c, chips = _mesh_pos()
        return [_remote(cin[k].at[2 * cx + cy], cout[k].at[j], ssem, rsem, 3 * k + j, (cx, cy, c))
                for k in range(len(sums)) for j, (cx, cy) in enumerate(chips)]

    outs = [_sds((3,) + s.shape[1:], s.dtype) for s in sums]
    return _Comm(sums, outs, copies, 3 * len(sums))


def _reduce_share(reds, l):
    def copies(cin, cout, ssem, rsem):
        x, y, c, _ = _mesh_pos()
        out = []
        for k, r in enumerate(reds):
            half = cout[k].at[l, _mine(c, r.shape[1])]
            out.append(_remote(half, half, ssem, rsem, k, (x, y, 1 - c)))
        return out

    return _Comm(reds, [_sds(r.shape, r.dtype) for r in reds], copies, len(reds),
                 {k: k for k in range(len(reds))})


def _allreduce_small(per_layer):
    kinds = len(per_layer[0])
    shapes = [a.shape[1:] if a.shape[0] == 1 else a.shape for a in per_layer[0]]

    def body(*refs):
        ins = refs[:DEPTH * kinds]
        outs = refs[DEPTH * kinds:(DEPTH + 1) * kinds]
        gbufs = refs[(DEPTH + 1) * kinds:(DEPTH + 2) * kinds]
        send_sems, recv_sems = refs[-2], refs[-1]
        x, y, c, chips = _mesh_pos()
        sibling = (x, y, 1 - c)

        def copy(k, i, block, to):
            px, py, pc = block
            slot = gbufs[k].at[4 * px + 2 * py + pc]
            return _remote(slot, slot, send_sems, recv_sems, 7 * k + i, to)

        me = (x, y, c)
        first, passed = [], []
        for k in range(kinds):
            for l in range(DEPTH):
                a = ins[l * kinds + k]
                if per_layer[l][k].shape[0] == 1:
                    gbufs[k][4 * x + 2 * y + c, l:l + 1] = a[...]
                else:
                    gbufs[k][4 * x + 2 * y + c, l] = a[...]
            first.append(copy(k, 0, me, sibling))
            first += [copy(k, 1 + j, me, (*chip, c)) for j, chip in enumerate(chips)]
            passed += [copy(k, 4 + j, (*chip, c), sibling) for j, chip in enumerate(chips)]
        for cp in first:
            cp.start()
        for k in range(kinds):
            for j, chip in enumerate(chips):
                copy(k, 1 + j, (*chip, c), me).wait_recv()
                passed[3 * k + j].start()
        for k in range(kinds):
            copy(k, 0, sibling, me).wait_recv()
            for j, chip in enumerate(chips):
                copy(k, 4 + j, (*chip, 1 - c), me).wait_recv()
        for cp in first + passed:
            cp.wait_send()
        for k in range(kinds):
            acc = gbufs[k][0]
            for d in range(1, 8):
                acc = acc + gbufs[k][d]
            outs[k][...] = acc

    vmem = pl.BlockSpec(memory_space=pltpu.VMEM)
    return pl.pallas_call(
        body, name="allreduce_small",
        in_specs=[vmem] * (DEPTH * kinds), out_specs=[vmem] * kinds,
        out_shape=[_sds((DEPTH,) + s, F32) for s in shapes],
        scratch_shapes=[pltpu.VMEM((8, DEPTH) + s, F32) for s in shapes]
        + [pltpu.SemaphoreType.DMA((7 * kinds,)), pltpu.SemaphoreType.DMA((7 * kinds,))],
        compiler_params=pltpu.CompilerParams(vmem_limit_bytes=VMEM_LIMIT_V7X),
    )(*per_layer[0], *per_layer[1])


def _adamw_small(ws, gs, ms, vs):
    n = len(ws)
    c1 = 1.0 - ADAM_B1 ** ADAM_STEP
    c2 = 1.0 - ADAM_B2 ** ADAM_STEP

    def body(*refs):
        for i in range(n):
            w_ref, g_ref, m_ref, v_ref = (refs[j * n + i] for j in range(4))
            d_ref, nm_ref, nv_ref = (refs[(4 + j) * n + i] for j in range(3))
            gv = g_ref[...]
            nm = ADAM_B1 * m_ref[...] + (1.0 - ADAM_B1) * gv
            nv = ADAM_B2 * v_ref[...] + (1.0 - ADAM_B2) * (gv * gv)
            nm_ref[...] = nm
            nv_ref[...] = nv
            d_ref[...] = -ADAM_LR * ((nm / c1) / (jnp.sqrt(nv / c2) + ADAM_EPS)
                                     + ADAM_WD * w_ref[...])

    vmem = pl.BlockSpec(memory_space=pltpu.VMEM)
    outs = pl.pallas_call(
        body, name="adamw_small", in_specs=[vmem] * (4 * n), out_specs=[vmem] * (3 * n),
        out_shape=[_sds(w.shape, F32) for w in ws] * 3,
        compiler_params=pltpu.CompilerParams(vmem_limit_bytes=VMEM_LIMIT_V7X),
    )(*ws, *gs, *ms, *vs)
    return outs[:n], outs[n:2 * n], outs[2 * n:]


def _core_index():
    return jnp.reshape(lax.axis_index("c"), (1,)).astype(jnp.int32)


def _chip_index():
    return jnp.reshape(2 * lax.axis_index("x") + lax.axis_index("y"), (1,)).astype(jnp.int32)


def _chip_sums(name, stacked, sibs, l):
    n = len(stacked)
    dims = [(s.shape[2] // 2, s.shape[3]) for s in stacked]

    def body(c_ref, *refs):
        for k in range(n):
            a_ref, b_ref, o_ref = refs[k], refs[n + k], refs[2 * n + k]
            o_ref[...] = (a_ref[...].astype(F32) + b_ref[...].astype(F32)).astype(BF16)

    return pl.pallas_call(
        body, name=name,
        grid_spec=pltpu.PrefetchScalarGridSpec(
            num_scalar_prefetch=1, grid=(N_CHIPS,),
            in_specs=[pl.BlockSpec((None, None, hr, cd), lambda j, cr: (l, j, cr[0], 0))
                      for hr, cd in dims]
            + [pl.BlockSpec((None, hr, cd), lambda j, cr: (j, 0, 0)) for hr, cd in dims],
            out_specs=[pl.BlockSpec((None, hr, cd), lambda j, cr: (j, 0, 0)) for hr, cd in dims]),
        out_shape=[_sds((N_CHIPS, hr, cd), BF16) for hr, cd in dims],
        compiler_params=_cparams("parallel"))(_core_index(), *stacked, *sibs)


def _final_sums(name, sums, recvs, l, fills):
    n = len(sums)
    dims = [(s.shape[1] // 2, s.shape[2]) for s in sums]
    filled = fills[0] is not None

    def body(m_ref, *refs):
        outs = refs[-n:]
        for k in range(n):
            acc = refs[k][...].astype(F32)
            for j in range(3):
                acc = acc + refs[n + k][j].astype(F32)
            outs[k][...] = acc

    in_specs = ([pl.BlockSpec((None, tr, cd), lambda i, mr: (mr[0], i, 0)) for tr, cd in dims]
                + [pl.BlockSpec((3, tr, cd), lambda i, mr: (0, i, 0)) for tr, cd in dims])
    args = [jnp.concatenate([_chip_index(), _core_index()]), *sums, *recvs]
    aliases = {}
    if filled:
        in_specs += [pl.BlockSpec(memory_space=pl.ANY)] * n
        args += list(fills)
        aliases = {1 + 2 * n + k: k for k in range(n)}
    return pl.pallas_call(
        body, name=name,
        grid_spec=pltpu.PrefetchScalarGridSpec(
            num_scalar_prefetch=1, grid=(2,), in_specs=in_specs,
            out_specs=[pl.BlockSpec((None, tr, cd), lambda i, mr: (l, 2 * mr[1] + i, 0))
                       for tr, cd in dims]),
        out_shape=[_sds((DEPTH, 4 * tr, cd), F32) for tr, cd in dims],
        input_output_aliases=aliases,
        compiler_params=_cparams("parallel"))(*args)


def _adamw(name, w, g, m, v):
    nl, r, cdim = w.shape
    tr = r // 4 if r % 32 == 0 else r
    c1 = 1.0 - ADAM_B1 ** ADAM_STEP
    c2 = 1.0 - ADAM_B2 ** ADAM_STEP

    def body(w_ref, g_ref, m_ref, v_ref, d_ref, nm_ref, nv_ref):
        gv = g_ref[...]
        nm = ADAM_B1 * m_ref[...] + (1.0 - ADAM_B1) * gv
        nv = ADAM_B2 * v_ref[...] + (1.0 - ADAM_B2) * (gv * gv)
        nm_ref[...] = nm
        nv_ref[...] = nv
        d_ref[...] = -ADAM_LR * ((nm / c1) / (jnp.sqrt(nv / c2) + ADAM_EPS) + ADAM_WD * w_ref[...])

    spec = pl.BlockSpec((None, tr, cdim), lambda l, i: (l, i, 0))
    out = _sds(w.shape, F32)
    return pl.pallas_call(
        body, name=name, grid=(nl, r // tr),
        in_specs=[spec] * 4, out_specs=[spec] * 3, out_shape=[out] * 3,
        compiler_params=_cparams("parallel", "parallel"))(w, g, m, v)


def kernel(x, norm_mix_pre, w_in, b_gate, rel_bias, w_attn_out, w_pool_group, pool_scale, w_pool_out, w_o, norm_mix_post, norm_ffn_pre, w_up, conv_w, conv_b, w_down, norm_ffn_post, loss_target, m_norm_mix_pre, m_w_in, m_b_gate, m_rel_bias, m_w_attn_out, m_w_pool_group, m_pool_scale, m_w_pool_out, m_w_o, m_norm_mix_post, m_norm_ffn_pre, m_w_up, m_conv_w, m_conv_b, m_w_down, m_norm_ffn_post, v_norm_mix_pre, v_w_in, v_b_gate, v_rel_bias, v_w_attn_out, v_w_pool_group, v_pool_scale, v_w_pool_out, v_w_o, v_norm_mix_post, v_norm_ffn_pre, v_w_up, v_conv_w, v_conv_b, v_w_down, v_norm_ffn_post):
    t = x.shape[1]
    xs = x.reshape(t, D_MODEL)
    target = loss_target.reshape(t, D_MODEL)

    names = ["w_in", "w_attn_out", "w_pool_out", "w_o", "w_up", "w_down"]
    shards = [w.astype(BF16) for w in (w_in, w_attn_out, w_pool_out, w_o, w_up, w_down)]
    rows = [s.shape[1] for s in shards]
    nbig = len(shards)
    g = _comm_call("gather0_send", _gather_send(shards[:1], conv_w, None, 0))
    g = _comm_call("gather0_forward", _gather_forward(g, 1, rows[:1], 0))
    cw_full = jnp.transpose(g[1], (0, 2, 1, 3)).reshape(DEPTH, 3, 2 * D_FF)
    g = g[:1]
    wg_bf = w_pool_group.astype(BF16)

    def views(gathered):
        win_g, wao_g, wpo_g, wo_g, wup_g, wdn_g = gathered
        return (win_g, wao_g, wpo_g, wo_g.reshape(DEPTH, D_MODEL, D_MODEL), wup_g,
                wdn_g.reshape(DEPTH, D_FF, D_MODEL))

    saved = []
    xcur = xs
    h = _norm_fwd("l0_norm_mix_pre", xs, norm_mix_pre[0:1])
    for l in range(DEPTH):
        tag = f"l{l}_"
        bias = _bias_table(tag + "bias_table", rel_bias[l])
        proj = _mm_nn_blocked(tag + "proj", h, g[0], l, BF16)
        if l == 0:
            att, rest = _attn_fwd(tag + "attn_fwd", proj, bias,
                                  _gather_send(shards[1:], None, None, 0))
            pooled, mixed, rest = _pool_fwd(tag + "pool_fwd", proj, wg_bf[l], pool_scale[l:l + 1],
                                            _gather_forward(rest, nbig - 1, rows[1:], 0))
            g = g + rest
        else:
            att = _attn_fwd(tag + "attn_fwd", proj, bias)
            pooled, mixed = _pool_fwd(tag + "pool_fwd", proj, wg_bf[l], pool_scale[l:l + 1])
        win_g, wao_g, wpo_g, wo_full, wup_g, wdn_full = views(g)
        ya = _narrow_nn(tag + "attn_out", att, wao_g, l)
        yb = _narrow_nn(tag + "pool_out", mixed, wpo_g, l)
        z = _gate_fwd(tag + "gate_fwd", proj, b_gate[l:l + 1], ya, yb)
        mix = _mm_nn(tag + "mix", z, wo_full, l, D_MODEL, F32)
        x1, h2 = _post_pre_fwd(tag + "norm_mix_post", xcur, mix, norm_mix_post[l:l + 1],
                               norm_ffn_pre[l:l + 1])
        hu = _mm_nn_blocked(tag + "ffn_up", h2, wup_g, l, BF16)
        if l == 0:
            a, g = _ffn_gate_fwd(tag + "ffn_gate_fwd", hu, cw_full[l], conv_b[l:l + 1],
                                 _gather_send(shards, None, g, 1))
            wdn_full = views(g)[5]
        else:
            a = _ffn_gate_fwd(tag + "ffn_gate_fwd", hu, cw_full[l], conv_b[l:l + 1])
        f = _mm_nn(tag + "ffn_down", a, wdn_full, l, D_FF // 2, F32)
        saved.append(dict(x=xcur, h=h, proj=proj, att=att, pooled=pooled, mixed=mixed, ya=ya,
                          yb=yb, z=z, mix=mix, x1=x1, h2=h2, hu=hu, a=a, f=f, bias=bias))
        if l == 0:
            xcur, h, g = _post_pre_fwd(tag + "norm_ffn_post", x1, f, norm_ffn_post[l:l + 1],
                                       norm_mix_pre[l + 1:l + 2], _gather_forward(g, nbig, rows, 1))
        elif l < DEPTH - 1:
            xcur, h = _post_pre_fwd(tag + "norm_ffn_post", x1, f, norm_ffn_post[l:l + 1],
                                    norm_mix_pre[l + 1:l + 2])
    win_g, wao_g, wpo_g, wo_full, wup_g, wdn_full = views(g)

    dy, df, d_nfpost, loss_local = _tail("tail", saved[-1]["x1"], saved[-1]["f"],
                                         norm_ffn_post[DEPTH - 1:DEPTH], target)
    loss = lax.psum(loss_local, ("x", "y", "c"))

    dx = dy
    dws = dict.fromkeys(names)
    reds = [None] * nbig
    small_grads = [None] * DEPTH
    ffn = [4, 5]
    outs3 = [1, 2, 3]

    def blocks(ks):
        return [dws[names[k]].reshape(DEPTH, N_CHIPS, rows[k], -1) for k in ks]

    def chip_sums(ks, sib, l):
        return _chip_sums(f"chip_sums{l}_" + names[ks[0]], blocks(ks), sib, l)

    def final_sums(ks, sums, recv, l):
        outs = _final_sums(f"final_sums{l}_" + names[ks[0]], sums, recv, l, [reds[k] for k in ks])
        for k, r in zip(ks, outs):
            reds[k] = r

    for l in reversed(range(DEPTH)):
        tag = f"l{l}_"
        sv = saved[l]
        every = list(range(nbig))
        if l == 0:
            da, sib = _mm_nt(tag + "ffn_down_dx", df, wdn_full, l, D_FF // 2, BF16,
                             _reduce_swap(blocks(every), 1))
            sums = chip_sums(every, sib, 1)
        else:
            da = _mm_nt(tag + "ffn_down_dx", df, wdn_full, l, D_FF // 2, BF16)
        dws["w_down"] = _mm_tn(tag + "ffn_down_dw", sv["a"], df, D_FF // 2, l, dws["w_down"])
        if l == 0:
            dhu, dconv, recv = _ffn_gate_bwd(tag + "ffn_gate_bwd", da, sv["hu"], cw_full[l],
                                             conv_b[l:l + 1], _reduce_scatter(sums))
            final_sums(every, sums, recv, 1)
            dh2, reds = _mm_nt_blocked(tag + "ffn_up_dx", dhu, wup_g, l, F32,
                                       _reduce_share(reds, 1))
        else:
            dhu, dconv = _ffn_gate_bwd(tag + "ffn_gate_bwd", da, sv["hu"], cw_full[l],
                                       conv_b[l:l + 1])
            dh2 = _mm_nt_blocked(tag + "ffn_up_dx", dhu, wup_g, l, F32)
        dws["w_up"] = _mm_tn_blocked(tag + "ffn_up_dw", sv["h2"], dhu, l, dws["w_up"])
        if l == 0:
            dx1, d_nfpre, dmix, d_nmpost, sib = _pre_post_bwd(
                tag + "norm_ffn_pre_bwd", dh2, sv["x1"], dx, norm_ffn_pre[l:l + 1], sv["mix"],
                norm_mix_post[l:l + 1], _reduce_swap(blocks(ffn), 0))
            sums = chip_sums(ffn, sib, 0)
        else:
            dx1, d_nfpre, dmix, d_nmpost = _pre_post_bwd(
                tag + "norm_ffn_pre_bwd", dh2, sv["x1"], dx, norm_ffn_pre[l:l + 1], sv["mix"],
                norm_mix_post[l:l + 1])
        dz = _mm_nt(tag + "mix_dx", dmix, wo_full, l, D_MODEL, BF16)
        dws["w_o"] = _mm_tn(tag + "mix_dw", sv["z"], dmix, D_MODEL, l, dws["w_o"])
        dya, dyb, dgates, d_bgate = _gate_bwd(tag + "gate_bwd", dz, sv["proj"], b_gate[l:l + 1],
                                              sv["ya"], sv["yb"])
        datt = _narrow_nt(tag + "attn_out_dx", dya, wao_g, l)
        dws["w_attn_out"] = _narrow_tn(tag + "attn_out_dw", sv["att"], dya, l, dws["w_attn_out"])
        dmixed = _narrow_nt(tag + "pool_out_dx", dyb, wpo_g, l)
        dws["w_pool_out"] = _narrow_tn(tag + "pool_out_dw", sv["mixed"], dyb, l, dws["w_pool_out"])
        if l == 0:
            du, d_wg, d_pscale, sib = _pool_bwd(tag + "pool_bwd", dmixed, sv["pooled"], wg_bf[l],
                                                pool_scale[l:l + 1], _reduce_swap(blocks(outs3), 0))
            sums3 = chip_sums(outs3, sib, 0)
            dqkv, dbias, recv = _attn_bwd(
                tag + "attn_bwd", sv["proj"], datt, sv["bias"],
                _both(_reduce_scatter(sums), _reduce_scatter(sums3)))
            final_sums(ffn, sums, recv[:len(ffn)], 0)
            final_sums(outs3, sums3, recv[len(ffn):], 0)
        else:
            du, d_wg, d_pscale = _pool_bwd(tag + "pool_bwd", dmixed, sv["pooled"], wg_bf[l],
                                           pool_scale[l:l + 1])
            dqkv, dbias = _attn_bwd(tag + "attn_bwd", sv["proj"], datt, sv["bias"])
        d_rel = _bias_fold(tag + "bias_fold", dbias)
        if l == 0:
            dh, shared = _proj_dx(tag + "proj_dx", dqkv, du, dgates, win_g, l,
                                  _reduce_share([reds[k] for k in ffn + outs3], 0))
            for k, r in zip(ffn + outs3, shared):
                reds[k] = r
        else:
            dh = _proj_dx(tag + "proj_dx", dqkv, du, dgates, win_g, l)
        dws["w_in"] = _proj_dw(tag + "proj_dw", sv["h"], dqkv, du, dgates, l, dws["w_in"])
        small_grads[l] = [None, d_nmpost, d_nfpre, d_nfpost, d_bgate, d_rel, d_wg, d_pscale, dconv]
        if l > 0:
            dx, small_grads[l][0], df, d_nfpost = _pre_post_bwd(
                tag + "norm_mix_pre_bwd", dh, sv["x"], dx1, norm_mix_pre[l:l + 1],
                saved[l - 1]["f"], norm_ffn_post[l - 1:l])
        else:
            dx, small_grads[l][0] = _norm_pre_bwd(tag + "norm_mix_pre_bwd", dh, sv["x"], dx1,
                                                  norm_mix_pre[l:l + 1])

    grad_x = dx.reshape(x.shape)

    sib = _comm_call("reduce_swap", _reduce_swap(blocks([0]), 0))
    sums = chip_sums([0], sib, 0)
    recv = _comm_call("reduce_scatter", _reduce_scatter(sums))
    final_sums([0], sums, recv, 0)
    g_big = _comm_call("reduce_share", _reduce_share([reds[0]], 0)) + reds[1:]

    (g_nmpre, g_nmpost, g_nfpre, g_nfpost, g_bgate, g_rel, g_wg, g_pscale,
     g_conv) = _allreduce_small(small_grads)
    g_rel = g_rel[:, :, :N_REL]
    g_cb = g_conv[:, 3]
    ncw = conv_w.shape[2]
    chip = 2 * lax.axis_index("x") + lax.axis_index("y")
    g_cw = lax.dynamic_slice_in_dim(g_conv[:, 0:3], chip * ncw, ncw, axis=2)

    grads = dict(norm_mix_pre=g_nmpre, w_in=g_big[0], b_gate=g_bgate, rel_bias=g_rel,
                 w_attn_out=g_big[1], w_pool_group=g_wg, pool_scale=g_pscale, w_pool_out=g_big[2],
                 w_o=g_big[3], norm_mix_post=g_nmpost, norm_ffn_pre=g_nfpre, w_up=g_big[4],
                 conv_w=g_cw, conv_b=g_cb, w_down=g_big[5], norm_ffn_post=g_nfpost)
    weights = dict(norm_mix_pre=norm_mix_pre, w_in=w_in, b_gate=b_gate, rel_bias=rel_bias,
                   w_attn_out=w_attn_out, w_pool_group=w_pool_group, pool_scale=pool_scale,
                   w_pool_out=w_pool_out, w_o=w_o, norm_mix_post=norm_mix_post,
                   norm_ffn_pre=norm_ffn_pre, w_up=w_up, conv_w=conv_w, conv_b=conv_b,
                   w_down=w_down, norm_ffn_post=norm_ffn_post)
    moms = dict(norm_mix_pre=(m_norm_mix_pre, v_norm_mix_pre), w_in=(m_w_in, v_w_in),
                b_gate=(m_b_gate, v_b_gate), rel_bias=(m_rel_bias, v_rel_bias),
                w_attn_out=(m_w_attn_out, v_w_attn_out),
                w_pool_group=(m_w_pool_group, v_w_pool_group),
                pool_scale=(m_pool_scale, v_pool_scale), w_pool_out=(m_w_pool_out, v_w_pool_out),
                w_o=(m_w_o, v_w_o), norm_mix_post=(m_norm_mix_post, v_norm_mix_post),
                norm_ffn_pre=(m_norm_ffn_pre, v_norm_ffn_pre), w_up=(m_w_up, v_w_up),
                conv_w=(m_conv_w, v_conv_w), conv_b=(m_conv_b, v_conv_b),
                w_down=(m_w_down, v_w_down), norm_ffn_post=(m_norm_ffn_post, v_norm_ffn_post))
    order = list(weights.keys())

    delta, new_m, new_v = {}, {}, {}
    small_names = [nm for nm in order if nm not in names]
    for nm in names:
        delta[nm], new_m[nm], new_v[nm] = _adamw("adamw_" + nm, weights[nm], grads[nm], *moms[nm])
    d_s, m_s, v_s = _adamw_small([weights[nm] for nm in small_names],
                                 [grads[nm] for nm in small_names],
                                 [moms[nm][0] for nm in small_names],
                                 [moms[nm][1] for nm in small_names])
    for i, nm in enumerate(small_names):
        delta[nm], new_m[nm], new_v[nm] = d_s[i], m_s[i], v_s[i]

    return (loss, grad_x, *[grads[nm] for nm in order], *[delta[nm] for nm in order],
            *[new_m[nm] for nm in order], *[new_v[nm] for nm in order])
```

```python
import functools
import math

import jax
import jax.numpy as jnp
from jax import lax
from jax.experimental import pallas as pl
from jax.experimental.pallas import tpu as pltpu

F32 = jnp.float32
BF16 = jnp.bfloat16
MESH = pl.DeviceIdType.MESH

D_MODEL = 1024
DEPTH = 2
CHUNK = 64
BAND_CHUNKS = 9
BAND = BAND_CHUNKS * CHUNK
HEADS = 8
HEAD_DIM = 64
ATTN_W = HEADS * HEAD_DIM
POOL_WINDOWS = (2, 4, 8, 16)
POOL_W = 512
POOL_GD = 128
MAX_REL = 256
N_REL = 2 * MAX_REL + 1
D_FF = 2816
IN_W = 3 * ATTN_W + POOL_W + 2 * D_MODEL
EPS = 1e-6
ATTN_SCALE = HEAD_DIM ** -0.5
BAND_PAD = 640
BIAS_LANES = BAND_PAD
N_CHIPS = 4

ADAM_LR = 0.001
ADAM_B1 = 0.9
ADAM_B2 = 0.999
ADAM_EPS = 1e-08
ADAM_WD = 0.01
ADAM_STEP = 10

VMEM_LIMIT_V7X = 56 * 1024 * 1024
TOK = 512
ATT_BLK = 8 * CHUNK
FF_COL = 256
FF_TOK = 1024
HALO = 32


def _cparams(*sem):
    return pltpu.CompilerParams(dimension_semantics=sem, vmem_limit_bytes=VMEM_LIMIT_V7X)


def _sds(shape, dtype):
    return jax.ShapeDtypeStruct(shape, dtype)


class _Comm:
    def __init__(self, ins, outs, copies, n_sems, alias=None):
        self.ins, self.outs, self.copies, self.n_sems = list(ins), list(outs), copies, n_sems
        self.alias = dict(alias or {})


class _SemsFrom:
    def __init__(self, sems, start):
        self.sems, self.start = sems, start

    @property
    def at(self):
        return self

    def __getitem__(self, i):
        return self.sems.at[self.start + i]


def _both(a, b):
    na, nao = len(a.ins), len(a.outs)

    def copies(cin, cout, ssem, rsem):
        return (a.copies(cin[:na], cout[:nao], ssem, rsem)
                + b.copies(cin[na:], cout[nao:], _SemsFrom(ssem, a.n_sems), _SemsFrom(rsem, a.n_sems)))

    alias = dict(a.alias)
    alias.update({na + i: nao + o for i, o in b.alias.items()})
    return _Comm(a.ins + b.ins, a.outs + b.outs, copies, a.n_sems + b.n_sems, alias)


def _pcall(body, name, grid, in_specs, out_specs, out_shape, scratch_shapes, compiler_params, args,
           comm=None, aliases=None):
    single = not isinstance(out_shape, (list, tuple))
    out_specs = [out_specs] if single else list(out_specs)
    out_shape = [out_shape] if single else list(out_shape)
    n_in, n_out = len(in_specs), len(out_specs)
    aliases = dict(aliases or {})
    if comm is None:
        res = pl.pallas_call(
            body, name=name, grid=grid, in_specs=list(in_specs), out_specs=out_specs,
            out_shape=out_shape, scratch_shapes=list(scratch_shapes),
            input_output_aliases=aliases, compiler_params=compiler_params)(*args)
        return (res[0] if single else res), None
    ci, co = len(comm.ins), len(comm.outs)

    def hosted(*refs):
        main_in, cin = refs[:n_in], refs[n_in:n_in + ci]
        main_out = refs[n_in + ci:n_in + ci + n_out]
        cout = refs[n_in + ci + n_out:n_in + ci + n_out + co]
        rest = refs[n_in + ci + n_out + co:]
        copies = comm.copies(cin, cout, rest[-2], rest[-1])
        ids = [pl.program_id(a) for a in range(len(grid))]
        first = functools.reduce(jnp.logical_and, [i == 0 for i in ids])
        last = functools.reduce(jnp.logical_and, [i == g - 1 for i, g in zip(ids, grid)])

        @pl.when(first)
        def _():
            for cp in copies:
                cp.start()

        body(*main_in, *main_out, *rest[:-2])

        @pl.when(last)
        def _():
            for cp in copies:
                cp.wait()

    for i, o in comm.alias.items():
        aliases[n_in + i] = n_out + o
    hbm = pl.BlockSpec(memory_space=pl.ANY)
    sems = pltpu.SemaphoreType.DMA((comm.n_sems,))
    res = pl.pallas_call(
        hosted, name=name, grid=grid, in_specs=list(in_specs) + [hbm] * ci,
        out_specs=out_specs + [hbm] * co, out_shape=out_shape + comm.outs,
        scratch_shapes=list(scratch_shapes) + [sems, sems],
        input_output_aliases=aliases, compiler_params=compiler_params)(*args, *comm.ins)
    return (res[0] if single else list(res[:n_out])), list(res[n_out:])


def _comm_call(name, comm):
    ci = len(comm.ins)

    def body(*refs):
        copies = comm.copies(refs[:ci], refs[ci:-2], refs[-2], refs[-1])
        for cp in copies:
            cp.start()
        for cp in copies:
            cp.wait()

    hbm = pl.BlockSpec(memory_space=pl.ANY)
    sems = pltpu.SemaphoreType.DMA((comm.n_sems,))
    return list(pl.pallas_call(
        body, name=name, in_specs=[hbm] * ci, out_specs=[hbm] * len(comm.outs),
        out_shape=comm.outs, scratch_shapes=[sems, sems],
        input_output_aliases=comm.alias)(*comm.ins))


def _matmul(name, a, b, a_spec, b_spec, o_spec, out_shape, grid, contract, nk, acc_shape,
            fill=None, comm=None):
    def body(*refs):
        a_ref, b_ref = refs[0], refs[1]
        o_ref = refs[2 if fill is None else 3]
        scratch = refs[(3 if fill is None else 4):]
        part = lax.dot_general(a_ref[...], b_ref[...], (contract, ((), ())),
                               preferred_element_type=F32)
        if nk == 1:
            o_ref[...] = part.astype(o_ref.dtype)
        else:
            acc_ref = scratch[0]
            k = pl.program_id(2)

            @pl.when(k == 0)
            def _():
                acc_ref[...] = part

            @pl.when(k > 0)
            def _():
                acc_ref[...] += part

            @pl.when(k == nk - 1)
            def _():
                o_ref[...] = acc_ref[...].astype(o_ref.dtype)

    scratch = [] if nk == 1 else [pltpu.VMEM(acc_shape, F32)]
    in_specs, args, aliases = [a_spec, b_spec], [a, b], {}
    if fill is not None:
        in_specs.append(pl.BlockSpec(memory_space=pl.ANY))
        args.append(fill)
        aliases = {2: 0}
    out, moved = _pcall(body, name, grid, in_specs, o_spec, out_shape, scratch,
                        _cparams("parallel", "parallel", "arbitrary"), args, comm, aliases)
    return out if comm is None else (out, moved)


NN = ((1,), (0,))
NT = ((1,), (1,))
TN = ((0,), (0,))


def _tm(t):
    return min(t, 1024)


def _tt(t):
    return min(t, 2048)


def _col_block_spec(a, rows, nb, row_col):
    if a.ndim == 2:
        return pl.BlockSpec((rows, nb), row_col)

    def halves(*ids):
        r, c = row_col(*ids)
        return c // 2, r, c % 2

    return pl.BlockSpec((None, rows, nb), halves)


def _mm_nn_blocked(name, a, w, l, out_dtype):
    t, k = a.shape
    nb = w.shape[3]
    tm = _tm(t)
    return _matmul(
        name, a, w,
        pl.BlockSpec((tm, k), lambda i, n, kk: (i, 0)),
        pl.BlockSpec((None, None, k, nb), lambda i, n, kk: (l, n, 0, 0)),
        pl.BlockSpec((tm, nb), lambda i, n, kk: (i, n)),
        _sds((t, N_CHIPS * nb), out_dtype), (t // tm, N_CHIPS, 1), NN, 1, None)


def _mm_nt_blocked(name, a, w, l, out_dtype, comm=None):
    t = a.shape[-2]
    k, nb = w.shape[2], w.shape[3]
    tm = _tm(t)
    return _matmul(
        name, a, w,
        _col_block_spec(a, tm, nb, lambda i, n, kk: (i, kk)),
        pl.BlockSpec((None, None, k, nb), lambda i, n, kk: (l, kk, 0, 0)),
        pl.BlockSpec((tm, k), lambda i, n, kk: (i, 0)),
        _sds((t, k), out_dtype), (t // tm, 1, N_CHIPS), NT, N_CHIPS, (tm, k), comm=comm)


def _mm_tn_blocked(name, a, g, l, fill):
    t, k = a.shape
    nb = g.shape[-1] * (g.ndim - 1) // N_CHIPS
    tt = _tt(t)
    nt = t // tt
    return _matmul(
        name, a, g,
        pl.BlockSpec((tt, k), lambda n, j, kk: (kk, 0)),
        _col_block_spec(g, tt, nb, lambda n, j, kk: (kk, n)),
        pl.BlockSpec((None, None, k, nb), lambda n, j, kk: (l, n, 0, 0)),
        _sds((DEPTH, N_CHIPS, k, nb), BF16), (N_CHIPS, 1, nt), TN, nt, (k, nb), fill)


def _proj_pieces(rows, dqkv_first):
    def piece(col):
        if dqkv_first:
            return pl.BlockSpec((rows, ATTN_W), lambda i, kk: (i, col))
        return pl.BlockSpec((rows, ATTN_W), lambda n, kk: (kk, col))
    return [piece(0), piece(1), piece(2), piece(0)]


def _proj_dx(name, dqkv, du, dgates, w, l, comm=None):
    t = du.shape[0]
    k, nb = w.shape[2], w.shape[3]
    tm = _tm(t)

    def body(dq_ref, dk_ref, dv_ref, du_ref, dg_ref, w_ref, o_ref, acc_ref):
        kk = pl.program_id(1)

        def mm(a):
            return lax.dot_general(a, w_ref[...], (NT, ((), ())), preferred_element_type=F32)

        @pl.when(kk == 0)
        def _():
            acc_ref[...] = mm(jnp.concatenate([dq_ref[...], dk_ref[...]], axis=1))

        @pl.when(kk == 1)
        def _():
            acc_ref[...] += mm(jnp.concatenate([dv_ref[...], du_ref[...]], axis=1))

        @pl.when(kk >= 2)
        def _():
            acc_ref[...] += mm(dg_ref[...])

        @pl.when(kk == N_CHIPS - 1)
        def _():
            o_ref[...] = acc_ref[...]

    out, moved = _pcall(
        body, name, (t // tm, N_CHIPS),
        _proj_pieces(tm, True)
        + [pl.BlockSpec((tm, nb), lambda i, kk: (i, jnp.maximum(kk - 2, 0))),
           pl.BlockSpec((None, None, k, nb), lambda i, kk: (l, kk, 0, 0))],
        pl.BlockSpec((tm, k), lambda i, kk: (i, 0)), _sds((t, k), F32),
        [pltpu.VMEM((tm, k), F32)], _cparams("arbitrary", "arbitrary"),
        (dqkv, dqkv, dqkv, du, dgates, w), comm)
    return out if comm is None else (out, moved)


def _proj_dw(name, h, dqkv, du, dgates, l, fill):
    t, k = h.shape
    nb = dgates.shape[1] // 2
    tt = _tm(t)
    nt = t // tt

    def body(*refs):
        h_ref, dq_ref, dk_ref, dv_ref, du_ref, dg_ref = refs[:6]
        o_ref, acc_ref = refs[-2], refs[-1]
        n, kk = pl.program_id(0), pl.program_id(1)

        def update(g):
            part = lax.dot_general(h_ref[...], g, (TN, ((), ())), preferred_element_type=F32)

            @pl.when(kk == 0)
            def _():
                acc_ref[...] = part

            @pl.when(kk > 0)
            def _():
                acc_ref[...] += part

        @pl.when(n == 0)
        def _():
            update(jnp.concatenate([dq_ref[...], dk_ref[...]], axis=1))

        @pl.when(n == 1)
        def _():
            update(jnp.concatenate([dv_ref[...], du_ref[...]], axis=1))

        @pl.when(n >= 2)
        def _():
            update(dg_ref[...])

        @pl.when(kk == nt - 1)
        def _():
            o_ref[...] = acc_ref[...].astype(BF16)

    in_specs = ([pl.BlockSpec((tt, k), lambda n, kk: (kk, 0))] + _proj_pieces(tt, False)
                + [pl.BlockSpec((tt, nb), lambda n, kk: (kk, jnp.maximum(n - 2, 0)))])
    args, aliases = [h, dqkv, dqkv, dqkv, du, dgates], {}
    if fill is not None:
        in_specs.append(pl.BlockSpec(memory_space=pl.ANY))
        args.append(fill)
        aliases = {6: 0}
    return pl.pallas_call(
        body, name=name, grid=(N_CHIPS, nt), in_specs=in_specs,
        out_specs=pl.BlockSpec((None, None, k, nb), lambda n, kk: (l, n, 0, 0)),
        out_shape=_sds((DEPTH, N_CHIPS, k, nb), BF16),
        scratch_shapes=[pltpu.VMEM((k, nb), F32)], input_output_aliases=aliases,
        compiler_params=_cparams("parallel", "arbitrary"))(*args)


def _narrow_nn(name, a, w, l):
    t, k = a.shape
    nb = w.shape[3]
    tm = _tm(t)

    def body(a_ref, w_ref, o_ref):
        av = a_ref[...]
        for j in range(N_CHIPS):
            o_ref[:, j * nb:(j + 1) * nb] = jnp.dot(
                av, w_ref[j], preferred_element_type=F32).astype(BF16)

    return pl.pallas_call(
        body, name=name, grid=(t // tm,),
        in_specs=[pl.BlockSpec((tm, k), lambda i: (i, 0)),
                  pl.BlockSpec((None, N_CHIPS, k, nb), lambda i: (l, 0, 0, 0))],
        out_specs=pl.BlockSpec((tm, N_CHIPS * nb), lambda i: (i, 0)),
        out_shape=_sds((t, N_CHIPS * nb), BF16), compiler_params=_cparams("parallel"))(a, w)


def _narrow_nt(name, a, w, l):
    t = a.shape[0]
    k, nb = w.shape[2], w.shape[3]
    tm = _tm(t)

    def body(a_ref, w_ref, o_ref):
        acc = lax.dot_general(a_ref[:, 0:nb], w_ref[0], (NT, ((), ())), preferred_element_type=F32)
        for j in range(1, N_CHIPS):
            acc = acc + lax.dot_general(a_ref[:, j * nb:(j + 1) * nb], w_ref[j], (NT, ((), ())),
                                        preferred_element_type=F32)
        o_ref[...] = acc.astype(BF16)

    return pl.pallas_call(
        body, name=name, grid=(t // tm,),
        in_specs=[pl.BlockSpec((tm, N_CHIPS * nb), lambda i: (i, 0)),
                  pl.BlockSpec((None, N_CHIPS, k, nb), lambda i: (l, 0, 0, 0))],
        out_specs=pl.BlockSpec((tm, k), lambda i: (i, 0)),
        out_shape=_sds((t, k), BF16), compiler_params=_cparams("parallel"))(a, w)


def _narrow_tn(name, a, g, l, fill):
    t, k = a.shape
    nb = g.shape[1] // N_CHIPS
    tt = _tm(t)
    nt = t // tt

    def body(*refs):
        a_ref, g_ref, o_ref, acc_ref = refs[0], refs[1], refs[-2], refs[-1]
        i = pl.program_id(0)
        part = lax.dot_general(a_ref[...], g_ref[...], (TN, ((), ())), preferred_element_type=F32)

        @pl.when(i == 0)
        def _():
            acc_ref[...] = part

        @pl.when(i > 0)
        def _():
            acc_ref[...] += part

        @pl.when(i == nt - 1)
        def _():
            for j in range(N_CHIPS):
                o_ref[j] = acc_ref[:, j * nb:(j + 1) * nb].astype(BF16)

    in_specs = [pl.BlockSpec((tt, k), lambda i: (i, 0)),
                pl.BlockSpec((tt, N_CHIPS * nb), lambda i: (i, 0))]
    args, aliases = [a, g], {}
    if fill is not None:
        in_specs.append(pl.BlockSpec(memory_space=pl.ANY))
        args.append(fill)
        aliases = {2: 0}
    return pl.pallas_call(
        body, name=name, grid=(nt,), in_specs=in_specs,
        out_specs=pl.BlockSpec((None, N_CHIPS, k, nb), lambda i: (l, 0, 0, 0)),
        out_shape=_sds((DEPTH, N_CHIPS, k, nb), BF16),
        scratch_shapes=[pltpu.VMEM((k, N_CHIPS * nb), F32)], input_output_aliases=aliases,
        compiler_params=_cparams("arbitrary"))(*args)


def _mm_nn(name, a, w, l, tk, out_dtype):
    t, k = a.shape
    n = w.shape[2]
    tm = _tm(t)
    nk = k // tk
    return _matmul(
        name, a, w,
        pl.BlockSpec((tm, tk), lambda i, j, kk: (i, kk)),
        pl.BlockSpec((None, tk, n), lambda i, j, kk: (l, kk, 0)),
        pl.BlockSpec((tm, n), lambda i, j, kk: (i, 0)),
        _sds((t, n), out_dtype), (t // tm, 1, nk), NN, nk, (tm, n))


def _mm_nt(name, a, w, l, tn, out_dtype, comm=None):
    t, n = a.shape
    k = w.shape[1]
    tm = _tm(t)
    return _matmul(
        name, a, w,
        pl.BlockSpec((tm, n), lambda i, j, kk: (i, 0)),
        pl.BlockSpec((None, tn, n), lambda i, j, kk: (l, j, 0)),
        pl.BlockSpec((tm, tn), lambda i, j, kk: (i, j)),
        _sds((t, k), out_dtype), (t // tm, k // tn, 1), NT, 1, None, comm=comm)


def _mm_tn(name, a, g, tko, l, fill):
    t, k = a.shape
    n = g.shape[1]
    tt = _tt(t)
    nt = t // tt
    return _matmul(
        name, a, g,
        pl.BlockSpec((tt, tko), lambda i, j, kk: (kk, i)),
        pl.BlockSpec((tt, n), lambda i, j, kk: (kk, 0)),
        pl.BlockSpec((None, tko, n), lambda i, j, kk: (l, i, 0)),
        _sds((DEPTH, k, n), BF16), (k // tko, 1, nt), TN, nt, (tko, n), fill)


def _row_spec(width, col=0):
    return pl.BlockSpec((TOK, width), lambda i: (i, col))


def _vec_spec(width):
    return pl.BlockSpec((1, width), lambda i: (0, 0))


def _rms(x):
    return lax.rsqrt(jnp.mean(x * x, axis=-1, keepdims=True) + EPS)


def _norm_fwd(name, x, g, comm=None):
    t = x.shape[0]

    def body(x_ref, g_ref, h_ref):
        xv = x_ref[...]
        h_ref[...] = (xv * _rms(xv) * g_ref[...]).astype(BF16)

    out, moved = _pcall(body, name, (t // TOK,), [_row_spec(D_MODEL), _vec_spec(D_MODEL)],
                        _row_spec(D_MODEL), _sds((t, D_MODEL), BF16), [], _cparams("arbitrary"),
                        (x, g), comm)
    return out if comm is None else (out, moved)


ROWS = 16
ROW_UNROLL = 8


def _rows(k):
    return pl.ds(pl.multiple_of(k * ROWS, ROWS), ROWS)


def _strips(step, init):
    def group(j, carry):
        for u in range(ROW_UNROLL):
            carry = step(j * ROW_UNROLL + u, carry)
        return carry

    return lax.fori_loop(0, TOK // (ROWS * ROW_UNROLL), group, init)


def _fold_rows(x):
    return x[0:8] + x[8:16]


def _accumulate(ref, part):
    total = jnp.sum(part, axis=0, keepdims=True)

    @pl.when(pl.program_id(0) == 0)
    def _():
        ref[...] = total

    @pl.when(pl.program_id(0) > 0)
    def _():
        ref[...] += total


def _norm_bwd_rows(d, mv, g):
    r = _rms(mv)
    n = mv * r
    dn = d * g
    return r * (dn - n * jnp.mean(dn * n, axis=-1, keepdims=True)), d * n


def _post_pre_fwd(name, xres, m, g_post, g_pre, comm=None):
    t = xres.shape[0]

    def body(x_ref, m_ref, gp_ref, gn_ref, x1_ref, h_ref):
        def strip(k, c):
            rows = _rows(k)
            mv = m_ref[rows, :]
            x1 = x_ref[rows, :] + mv * _rms(mv) * gp_ref[...]
            x1_ref[rows, :] = x1
            h_ref[rows, :] = (x1 * _rms(x1) * gn_ref[...]).astype(BF16)
            return c

        _strips(strip, 0)

    outs, moved = _pcall(
        body, name, (t // TOK,),
        [_row_spec(D_MODEL), _row_spec(D_MODEL), _vec_spec(D_MODEL), _vec_spec(D_MODEL)],
        [_row_spec(D_MODEL), _row_spec(D_MODEL)],
        [_sds((t, D_MODEL), F32), _sds((t, D_MODEL), BF16)], [], _cparams("arbitrary"),
        (xres, m, g_post, g_pre), comm)
    return outs if comm is None else (*outs, moved)


def _tail(name, xres, m, g_post, target):
    t = xres.shape[0]

    def body(x_ref, m_ref, g_ref, t_ref, dy_ref, dm_ref, dg_ref, l_ref):
        def strip(k, carry):
            rows = _rows(k)
            mv = m_ref[rows, :]
            e = x_ref[rows, :] + mv * _rms(mv) * g_ref[...] - t_ref[rows, :]
            dy = e * (1.0 / D_MODEL)
            dy_ref[rows, :] = dy
            dm, dgn = _norm_bwd_rows(dy, mv, g_ref[...])
            dm_ref[rows, :] = dm.astype(BF16)
            return carry[0] + _fold_rows(dgn), carry[1] + _fold_rows(e * e)

        zero = jnp.zeros((8, D_MODEL), F32)
        dg, sq = _strips(strip, (zero, zero))
        _accumulate(dg_ref, dg)
        _accumulate(l_ref, jnp.sum(sq, axis=1, keepdims=True))

    dy, dm, dg, sq = pl.pallas_call(
        body, name=name, grid=(t // TOK,),
        in_specs=[_row_spec(D_MODEL), _row_spec(D_MODEL), _vec_spec(D_MODEL), _row_spec(D_MODEL)],
        out_specs=[_row_spec(D_MODEL), _row_spec(D_MODEL), _vec_spec(D_MODEL),
                   pl.BlockSpec((1, 1), lambda i: (0, 0))],
        out_shape=[_sds((t, D_MODEL), F32), _sds((t, D_MODEL), BF16), _sds((1, D_MODEL), F32),
                   _sds((1, 1), F32)],
        compiler_params=_cparams("arbitrary"))(xres, m, g_post, target)
    return dy, dm, dg, sq[0, 0] * (0.5 / D_MODEL)


def _pre_post_bwd(name, dh, xin, dxo, g_pre, m, g_post, comm=None):
    t = dh.shape[0]

    def body(dh_ref, x_ref, d_ref, gq_ref, m_ref, gp_ref, dx_ref, dgq_ref, dm_ref, dgp_ref):
        def strip(k, carry):
            rows = _rows(k)
            dxin, dgq = _norm_bwd_rows(dh_ref[rows, :], x_ref[rows, :], gq_ref[...])
            dx = d_ref[rows, :] + dxin
            dx_ref[rows, :] = dx
            dm, dgp = _norm_bwd_rows(dx, m_ref[rows, :], gp_ref[...])
            dm_ref[rows, :] = dm.astype(BF16)
            return carry[0] + _fold_rows(dgq), carry[1] + _fold_rows(dgp)

        zero = jnp.zeros((8, D_MODEL), F32)
        dgq, dgp = _strips(strip, (zero, zero))
        _accumulate(dgq_ref, dgq)
        _accumulate(dgp_ref, dgp)

    outs, moved = _pcall(
        body, name, (t // TOK,),
        [_row_spec(D_MODEL), _row_spec(D_MODEL), _row_spec(D_MODEL), _vec_spec(D_MODEL),
         _row_spec(D_MODEL), _vec_spec(D_MODEL)],
        [_row_spec(D_MODEL), _vec_spec(D_MODEL), _row_spec(D_MODEL), _vec_spec(D_MODEL)],
        [_sds((t, D_MODEL), F32), _sds((1, D_MODEL), F32), _sds((t, D_MODEL), BF16),
         _sds((1, D_MODEL), F32)], [], _cparams("arbitrary"),
        (dh, xin, dxo, g_pre, m, g_post), comm)
    return outs if comm is None else (*outs, moved)


def _norm_pre_bwd(name, dh, xin, dxo, g, comm=None):
    t = dh.shape[0]

    def body(dh_ref, x_ref, d_ref, g_ref, dx_ref, dg_ref):
        xv = x_ref[...]
        dhv = dh_ref[...]
        r = _rms(xv)
        n = xv * r
        dn = dhv * g_ref[...]
        dx_ref[...] = d_ref[...] + r * (dn - n * jnp.mean(dn * n, axis=-1, keepdims=True))
        part = jnp.sum(dhv * n, axis=0, keepdims=True)

        @pl.when(pl.program_id(0) == 0)
        def _():
            dg_ref[...] = part

        @pl.when(pl.program_id(0) > 0)
        def _():
            dg_ref[...] += part

    out, moved = _pcall(
        body, name, (t // TOK,),
        [_row_spec(D_MODEL), _row_spec(D_MODEL), _row_spec(D_MODEL), _vec_spec(D_MODEL)],
        [_row_spec(D_MODEL), _vec_spec(D_MODEL)],
        [_sds((t, D_MODEL), F32), _sds((1, D_MODEL), F32)], [], _cparams("arbitrary"),
        (dh, xin, dxo, g), comm)
    return out if comm is None else (*out, moved)


def _gate_fwd(name, proj, b_gate, ya, yb):
    t = proj.shape[0]

    def body(ga_ref, gb_ref, b_ref, ya_ref, yb_ref, z_ref):
        def strip(k, c):
            rows = _rows(k)
            sa = jax.nn.sigmoid(ga_ref[rows, :].astype(F32) + b_ref[:, :D_MODEL])
            sb = jax.nn.sigmoid(gb_ref[rows, :].astype(F32) + b_ref[:, D_MODEL:])
            z_ref[rows, :] = (sa * ya_ref[rows, :].astype(F32)
                              + sb * yb_ref[rows, :].astype(F32)).astype(BF16)
            return c

        _strips(strip, 0)

    return pl.pallas_call(
        body, name=name, grid=(t // TOK,),
        in_specs=[_row_spec(D_MODEL, 2), _row_spec(D_MODEL, 3), _vec_spec(2 * D_MODEL),
                  _row_spec(D_MODEL), _row_spec(D_MODEL)],
        out_specs=_row_spec(D_MODEL), out_shape=_sds((t, D_MODEL), BF16),
        compiler_params=_cparams("parallel"))(proj, proj, b_gate, ya, yb)


def _gate_bwd(name, dz, proj, b_gate, ya, yb):
    t = proj.shape[0]

    def body(dz_ref, ga_ref, gb_ref, b_ref, ya_ref, yb_ref, dya_ref, dyb_ref, dg_ref, db_ref):
        def strip(k, carry):
            rows = _rows(k)
            dzv = dz_ref[rows, :].astype(F32)
            sa = jax.nn.sigmoid(ga_ref[rows, :].astype(F32) + b_ref[:, :D_MODEL])
            sb = jax.nn.sigmoid(gb_ref[rows, :].astype(F32) + b_ref[:, D_MODEL:])
            dya_ref[rows, :] = (dzv * sa).astype(BF16)
            dyb_ref[rows, :] = (dzv * sb).astype(BF16)
            dga = dzv * ya_ref[rows, :].astype(F32) * sa * (1.0 - sa)
            dgb = dzv * yb_ref[rows, :].astype(F32) * sb * (1.0 - sb)
            dg_ref[rows, :D_MODEL] = dga.astype(BF16)
            dg_ref[rows, D_MODEL:] = dgb.astype(BF16)
            return carry[0] + _fold_rows(dga), carry[1] + _fold_rows(dgb)

        zero = jnp.zeros((8, D_MODEL), F32)
        pa, pb = _strips(strip, (zero, zero))
        _accumulate(db_ref.at[:, :D_MODEL], pa)
        _accumulate(db_ref.at[:, D_MODEL:], pb)

    return pl.pallas_call(
        body, name=name, grid=(t // TOK,),
        in_specs=[_row_spec(D_MODEL), _row_spec(D_MODEL, 2), _row_spec(D_MODEL, 3),
                  _vec_spec(2 * D_MODEL), _row_spec(D_MODEL), _row_spec(D_MODEL)],
        out_specs=[_row_spec(D_MODEL), _row_spec(D_MODEL), _row_spec(2 * D_MODEL),
                   _vec_spec(2 * D_MODEL)],
        out_shape=[_sds((t, D_MODEL), BF16), _sds((t, D_MODEL), BF16),
                   _sds((t, 2 * D_MODEL), BF16), _sds((1, 2 * D_MODEL), F32)],
        compiler_params=_cparams("arbitrary"))(dz, proj, proj, b_gate, ya, yb)


def _head_masks():
    lane = lax.broadcasted_iota(jnp.int32, (1, 2 * HEAD_DIM), 1)
    return lane < HEAD_DIM


BAND_ROWS = 2 * ATT_BLK + CHUNK


def _fill_band(band, prev_ref, cur_ref):
    band[0:ATT_BLK, :] = prev_ref[...]
    band[ATT_BLK:2 * ATT_BLK, :] = cur_ref[...]
    band[2 * ATT_BLK:, :] = jnp.zeros((CHUNK, ATTN_W), BF16)


def _pair_rows(x2, low):
    zero = jnp.zeros_like(x2)
    return jnp.concatenate([jnp.where(low, x2, zero), jnp.where(low, zero, x2)], axis=0)


def _pair_diag(o2, low):
    return jnp.where(low, o2[0:CHUNK, :], o2[CHUNK:, :])


N_PAIRS = HEADS // 2
SM_STRIP = 32
N_STRIPS = BAND_PAD // SM_STRIP
NEG = -1e30


def _fold8(x, op):
    return op(op(x[0:8], x[8:16]), op(x[16:24], x[24:32]))


def _strip(k):
    return pl.ds(pl.multiple_of(k * SM_STRIP, SM_STRIP), SM_STRIP)


def _band_probs(k2, qcat, bias_t, first_key):
    kpos = lax.broadcasted_iota(jnp.int32, (BAND_PAD, 1), 0)
    st = lax.dot_general(k2, qcat, (NT, ((), ())), preferred_element_type=F32)
    st = jnp.where(kpos + first_key >= 0, st + bias_t, NEG)
    e = jnp.exp(st - jnp.max(st, axis=0, keepdims=True))
    return e * (1.0 / jnp.sum(e, axis=0, keepdims=True))


def _band_softmax_stats(st_ref, b_ref, first_key, dp_ref):
    rowi = lax.broadcasted_iota(jnp.int32, (SM_STRIP, 128), 0)

    def scores(k, mx):
        rows = _strip(k)
        live = (rowi + (k * SM_STRIP + first_key)) >= 0
        out = []
        for hp in range(N_PAIRS):
            x = jnp.where(live, st_ref[hp, rows, :] + b_ref[hp, rows, :], NEG)
            st_ref[hp, rows, :] = x
            out.append(jnp.maximum(mx[hp], _fold8(x, jnp.maximum)))
        return tuple(out)

    mx = lax.fori_loop(0, N_STRIPS, scores, (jnp.full((8, 128), NEG, F32),) * N_PAIRS, unroll=2)
    top = [jnp.max(m, axis=0, keepdims=True) for m in mx]

    def sums(k, acc):
        rows = _strip(k)
        ls, eds = [], []
        for hp in range(N_PAIRS):
            e = jnp.exp(st_ref[hp, rows, :] - top[hp])
            ls.append(acc[hp] + _fold8(e, jnp.add))
            eds.append(acc[N_PAIRS + hp] + _fold8(e * dp_ref[hp, rows, :], jnp.add))
        return tuple(ls + eds)

    acc = lax.fori_loop(0, N_STRIPS, sums, (jnp.zeros((8, 128), F32),) * (2 * N_PAIRS), unroll=2)
    inv = [1.0 / jnp.sum(a, axis=0, keepdims=True) for a in acc[:N_PAIRS]]
    delta = [jnp.sum(a, axis=0, keepdims=True) * i for a, i in zip(acc[N_PAIRS:], inv)]
    return top, inv, delta


def _attn_specs(nblk):
    cur = lambda col: pl.BlockSpec((ATT_BLK, ATTN_W), lambda s: (jnp.minimum(s, nblk - 1), col))
    prev = lambda col: pl.BlockSpec(
        (ATT_BLK, ATTN_W), lambda s: (jnp.maximum(jnp.minimum(s, nblk - 1) - 1, 0), col))
    return cur, prev


def _attn_fwd(name, proj, bias, comm=None):
    t = proj.shape[0]
    nblk = t // ATT_BLK
    cur, prev = _attn_specs(nblk)

    def body(q_ref, kp_ref, kc_ref, vp_ref, vc_ref, b_ref, o_ref, kband, vband):
        s = pl.program_id(0)
        _fill_band(kband, kp_ref, kc_ref)
        _fill_band(vband, vp_ref, vc_ref)
        low = _head_masks()

        def chunk(ci, carry):
            r0 = pl.multiple_of(ci * CHUNK, CHUNK)
            for hp in range(N_PAIRS):
                cols = slice(hp * 128, (hp + 1) * 128)
                qcat = _pair_rows(q_ref[pl.ds(r0, CHUNK), cols] * ATTN_SCALE, low)
                p = _band_probs(kband[pl.ds(r0, BAND_PAD), cols], qcat, b_ref[hp],
                                (s * 8 - 8 + ci) * CHUNK)
                o2 = lax.dot_general(p.astype(BF16), vband[pl.ds(r0, BAND_PAD), cols],
                                     (TN, ((), ())), preferred_element_type=F32)
                o_ref[pl.ds(r0, CHUNK), cols] = _pair_diag(o2, low).astype(BF16)
            return carry

        lax.fori_loop(0, 8, chunk, 0)

    out, moved = _pcall(
        body, name, (nblk,),
        [cur(0), prev(1), cur(1), prev(2), cur(2),
         pl.BlockSpec((N_PAIRS, BAND_PAD, 128), lambda s: (0, 0, 0))],
        pl.BlockSpec((ATT_BLK, ATTN_W), lambda s: (s, 0)), _sds((t, ATTN_W), BF16),
        [pltpu.VMEM((BAND_ROWS, ATTN_W), BF16), pltpu.VMEM((BAND_ROWS, ATTN_W), BF16)],
        _cparams("arbitrary"), (proj, proj, proj, proj, proj, bias), comm)
    return out if comm is None else (out, moved)


def _attn_bwd(name, proj, datt, bias, comm=None):
    t = proj.shape[0]
    nblk = t // ATT_BLK
    cur, prev = _attn_specs(nblk)
    late = pl.BlockSpec((ATT_BLK, 3 * ATTN_W), lambda s: (jnp.maximum(s - 1, 0), 0))

    def body(q_ref, kp_ref, kc_ref, vp_ref, vc_ref, do_ref, b_ref,
             dqkv_ref, db_ref, kband, vband, dkacc, dvacc,
             st_ref, dp_ref, pb_ref, dsb_ref, qc_ref, dc_ref, dq_ref, dq_held):
        s = pl.program_id(0)

        @pl.when(s == 0)
        def _():
            dkacc[...] = jnp.zeros_like(dkacc)
            dvacc[...] = jnp.zeros_like(dvacc)
            db_ref[...] = jnp.zeros_like(db_ref)
            dq_ref[...] = jnp.zeros_like(dq_ref)

        @pl.when(s < nblk)
        def _():
            _fill_band(kband, kp_ref, kc_ref)
            _fill_band(vband, vp_ref, vc_ref)
            low = _head_masks()

            def chunk(ci, carry):
                r0 = pl.multiple_of(ci * CHUNK, CHUNK)
                for hp in range(N_PAIRS):
                    cols = slice(hp * 128, (hp + 1) * 128)
                    qc_ref[hp] = _pair_rows(q_ref[pl.ds(r0, CHUNK), cols] * ATTN_SCALE, low)
                    dc_ref[hp] = _pair_rows(do_ref[pl.ds(r0, CHUNK), cols], low)
                    st_ref[hp] = lax.dot_general(kband[pl.ds(r0, BAND_PAD), cols], qc_ref[hp],
                                                 (NT, ((), ())), preferred_element_type=F32)
                    dp_ref[hp] = lax.dot_general(vband[pl.ds(r0, BAND_PAD), cols], dc_ref[hp],
                                                 (NT, ((), ())), preferred_element_type=F32)
                top, inv, delta = _band_softmax_stats(st_ref, b_ref, (s * 8 - 8 + ci) * CHUNK,
                                                      dp_ref)

                def grads(k, c):
                    rows = _strip(k)
                    for hp in range(N_PAIRS):
                        p = jnp.exp(st_ref[hp, rows, :] - top[hp]) * inv[hp]
                        ds = p * (dp_ref[hp, rows, :] - delta[hp])
                        db_ref[hp, rows, :] += ds
                        dsb_ref[hp, rows, :] = ds.astype(BF16)
                        pb_ref[hp, rows, :] = p.astype(BF16)
                    return c

                lax.fori_loop(0, N_STRIPS, grads, 0, unroll=2)
                for hp in range(N_PAIRS):
                    cols = slice(hp * 128, (hp + 1) * 128)
                    dq2 = lax.dot_general(dsb_ref[hp], kband[pl.ds(r0, BAND_PAD), cols],
                                          (TN, ((), ())), preferred_element_type=F32)
                    dq_ref[pl.ds(r0, CHUNK), cols] = (_pair_diag(dq2, low) * ATTN_SCALE).astype(BF16)
                    dkacc[pl.ds(r0, BAND_PAD), cols] += jnp.dot(dsb_ref[hp], qc_ref[hp],
                                                               preferred_element_type=F32)
                    dvacc[pl.ds(r0, BAND_PAD), cols] += jnp.dot(pb_ref[hp], dc_ref[hp],
                                                               preferred_element_type=F32)
                return carry

            dq_held[...] = dq_ref[...]
            lax.fori_loop(0, 8, chunk, 0)

        @pl.when(s == nblk)
        def _():
            dq_held[...] = dq_ref[...]

        dqkv_ref[:, 0:ATTN_W] = dq_held[...]
        dqkv_ref[:, ATTN_W:2 * ATTN_W] = dkacc[0:ATT_BLK, :].astype(BF16)
        dqkv_ref[:, 2 * ATTN_W:] = dvacc[0:ATT_BLK, :].astype(BF16)
        dkacc[0:ATT_BLK, :] = dkacc[ATT_BLK:2 * ATT_BLK, :]
        dvacc[0:ATT_BLK, :] = dvacc[ATT_BLK:2 * ATT_BLK, :]
        dkacc[ATT_BLK:, :] = jnp.zeros((ATT_BLK + CHUNK, ATTN_W), F32)
        dvacc[ATT_BLK:, :] = jnp.zeros((ATT_BLK + CHUNK, ATTN_W), F32)

    outs, moved = _pcall(
        body, name, (nblk + 1,),
        [cur(0), prev(1), cur(1), prev(2), cur(2),
         pl.BlockSpec((ATT_BLK, ATTN_W), lambda s: (jnp.minimum(s, nblk - 1), 0)),
         pl.BlockSpec((HEADS // 2, BAND_PAD, 128), lambda s: (0, 0, 0))],
        [late, pl.BlockSpec((HEADS // 2, BAND_PAD, 128), lambda s: (0, 0, 0))],
        [_sds((t, 3 * ATTN_W), BF16), _sds((HEADS // 2, BAND_PAD, 128), F32)],
        [pltpu.VMEM((BAND_ROWS, ATTN_W), BF16), pltpu.VMEM((BAND_ROWS, ATTN_W), BF16),
         pltpu.VMEM((BAND_ROWS, ATTN_W), F32), pltpu.VMEM((BAND_ROWS, ATTN_W), F32),
         pltpu.VMEM((N_PAIRS, BAND_PAD, 128), F32), pltpu.VMEM((N_PAIRS, BAND_PAD, 128), F32),
         pltpu.VMEM((N_PAIRS, BAND_PAD, 128), BF16), pltpu.VMEM((N_PAIRS, BAND_PAD, 128), BF16),
         pltpu.VMEM((N_PAIRS, 2 * CHUNK, 128), BF16), pltpu.VMEM((N_PAIRS, 2 * CHUNK, 128), BF16),
         pltpu.VMEM((ATT_BLK, ATTN_W), BF16), pltpu.VMEM((ATT_BLK, ATTN_W), BF16)],
        _cparams("arbitrary"), (proj, proj, proj, proj, proj, datt, bias), comm)
    return outs if comm is None else (*outs, moved)


def _diag_onehot(rel_rows):
    d0 = lax.broadcasted_iota(jnp.int32, (BIAS_LANES, BIAS_LANES), 0)
    d1 = lax.broadcasted_iota(jnp.int32, (BIAS_LANES, BIAS_LANES), 1)
    m, n = (d0, d1) if rel_rows else (d1, d0)
    hit = (m == jnp.minimum(BAND - 1 + MAX_REL - n, 2 * MAX_REL)) & (n < BAND + CHUNK - 1)
    return jnp.where(hit, 1.0, 0.0).astype(F32)


def _bias_table(name, rel_bias_l):
    rel_pad = jnp.pad(rel_bias_l, ((0, 0), (0, BIAS_LANES - N_REL)))

    def body(r_ref, o_ref):
        diag = jnp.dot(r_ref[...], _diag_onehot(True), preferred_element_type=F32,
                       precision=lax.Precision.HIGHEST)
        rowid = lax.broadcasted_iota(jnp.int32, (8, BIAS_LANES), 0)
        lane = lax.broadcasted_iota(jnp.int32, (8, BIAS_LANES), 1)
        for h in range(HEADS):
            d8 = jnp.broadcast_to(diag[h:h + 1, :], (8, BIAS_LANES))
            slab0 = pltpu.roll(d8, BIAS_LANES - CHUNK + 1, axis=1)
            for b in range(1, 8):
                slab0 = jnp.where(rowid == b, pltpu.roll(d8, BIAS_LANES - CHUNK + 1 + b, axis=1),
                                  slab0)
            for a in range(8):
                slab = slab0 if a == 0 else pltpu.roll(slab0, 8 * a, axis=1)
                o_ref[h * CHUNK + 8 * a:h * CHUNK + 8 * a + 8, :] = jnp.where(lane < BAND, slab, NEG)

    tab = pl.pallas_call(
        body, name=name,
        in_specs=[pl.BlockSpec(memory_space=pltpu.VMEM)],
        out_specs=pl.BlockSpec(memory_space=pltpu.VMEM),
        out_shape=_sds((HEADS * CHUNK, BIAS_LANES), F32),
    )(rel_pad)
    tab = tab.reshape(HEADS // 2, 2, CHUNK, BIAS_LANES)
    return jnp.transpose(tab, (0, 3, 1, 2)).reshape(HEADS // 2, BIAS_LANES, 2 * CHUNK)


def _bias_fold(name, dbias_t):
    rows = HEADS * CHUNK
    dbias = jnp.transpose(dbias_t.reshape(HEADS // 2, BIAS_LANES, 2, CHUNK), (0, 2, 3, 1))

    def body(d_ref, o_ref):
        rowid = lax.broadcasted_iota(jnp.int32, (8, BIAS_LANES), 0)
        diags = []
        for h in range(HEADS):
            acc = d_ref[h * CHUNK + 56:h * CHUNK + 64, :]
            for a in range(7):
                slab = d_ref[h * CHUNK + 8 * a:h * CHUNK + 8 * a + 8, :]
                acc = acc + pltpu.roll(slab, 56 - 8 * a, axis=1)
            tot = jnp.where(rowid == 7, acc, 0.0)
            for b in range(7):
                tot = tot + jnp.where(rowid == b, pltpu.roll(acc, 7 - b, axis=1), 0.0)
            diags.append(jnp.sum(tot, axis=0, keepdims=True))
        diag = jnp.concatenate(diags, axis=0)
        o_ref[...] = jnp.dot(diag, _diag_onehot(False), preferred_element_type=F32,
                             precision=lax.Precision.HIGHEST)

    return pl.pallas_call(
        body, name=name,
        in_specs=[pl.BlockSpec(memory_space=pltpu.VMEM)],
        out_specs=pl.BlockSpec(memory_space=pltpu.VMEM),
        out_shape=_sds((HEADS, BIAS_LANES), F32),
    )(dbias.reshape(rows, BIAS_LANES))


def _inv_counts(i):
    trow = lax.broadcasted_iota(jnp.int32, (TOK + HALO, 1), 0) + i * TOK
    return [1.0 / jnp.minimum(trow + 1, w).astype(F32) for w in POOL_WINDOWS]


def _pool_fwd(name, proj, wg, scale, comm=None):
    t = proj.shape[0]
    hb = TOK // HALO

    def body(u_ref, up_ref, wg_ref, sc_ref, pooled_ref, mixed_ref, b0, b1, b2, b3):
        i = pl.program_id(0)
        halo = up_ref[...].astype(F32)
        b0[0:HALO, :] = jnp.where(i == 0, jnp.zeros_like(halo), halo)
        b0[HALO:, :] = u_ref[...].astype(F32)
        n = TOK + HALO
        b1[8:n, :] = b0[8:n, :] + b0[7:n - 1, :]
        b2[16:n, 128:] = b1[16:n, 128:] + b1[14:n - 2, 128:]
        b3[24:n, 256:] = b2[24:n, 256:] + b2[20:n - 4, 256:]
        wins = [b1[HALO:n, 0:128], b2[HALO:n, 128:256], b3[HALO:n, 256:384],
                b3[HALO:n, 384:512] + b3[HALO - 8:n - 8, 384:512]]
        inv = _inv_counts(i)
        for g in range(4):
            cols = slice(g * POOL_GD, (g + 1) * POOL_GD)
            pooled = (wins[g] * inv[g][0:TOK] - b0[HALO:n, cols]).astype(BF16)
            pooled_ref[:, cols] = pooled
            pre = jnp.dot(pooled, wg_ref[g], preferred_element_type=F32)
            mixed_ref[:, cols] = (pre * sc_ref[:, cols]).astype(BF16)

    buf = pltpu.VMEM((TOK + HALO, POOL_W), F32)
    outs, moved = _pcall(
        body, name, (t // TOK,),
        [_row_spec(POOL_W, 3),
         pl.BlockSpec((HALO, POOL_W), lambda i: (jnp.maximum(i * hb - 1, 0), 3)),
         pl.BlockSpec((4, POOL_GD, POOL_GD), lambda i: (0, 0, 0)), _vec_spec(POOL_W)],
        [_row_spec(POOL_W), _row_spec(POOL_W)],
        [_sds((t, POOL_W), BF16), _sds((t, POOL_W), BF16)], [buf, buf, buf, buf],
        _cparams("arbitrary"), (proj, proj, wg, scale), comm)
    return outs if comm is None else (*outs, moved)


def _pool_bwd(name, dmixed, pooled, wg, scale, comm=None):
    t = dmixed.shape[0]
    nt = t // TOK
    hb = TOK // HALO

    def body(dm_ref, dmn_ref, p_ref, wg_ref, sc_ref, du_ref, dwg_ref, dsc_ref, c0, c1, c2, c3):
        i = pl.program_id(0)

        @pl.when(i == 0)
        def _():
            dwg_ref[...] = jnp.zeros_like(dwg_ref)
            dsc_ref[...] = jnp.zeros_like(dsc_ref)

        n = TOK + HALO
        inv = _inv_counts(i)
        dmv = dm_ref[...].astype(F32)
        dmn = dmn_ref[...].astype(F32)
        dmn = jnp.where(i == nt - 1, jnp.zeros_like(dmn), dmn)
        for g in range(4):
            cols = slice(g * POOL_GD, (g + 1) * POOL_GD)
            scg = sc_ref[:, cols]
            pg = p_ref[:, cols]
            dpre = (dmv[:, cols] * scg).astype(BF16)
            dpre_n = (dmn[:, cols] * scg).astype(BF16)
            pre = jnp.dot(pg, wg_ref[g], preferred_element_type=F32)
            dsc_ref[:, cols] += jnp.sum(dmv[:, cols] * pre, axis=0, keepdims=True)
            dwg_ref[g] += lax.dot_general(pg, dpre, (TN, ((), ())), preferred_element_type=F32)
            dpool = lax.dot_general(dpre, wg_ref[g], (NT, ((), ())), preferred_element_type=F32)
            dpool_n = lax.dot_general(dpre_n, wg_ref[g], (NT, ((), ())),
                                      preferred_element_type=F32)
            c0[0:TOK, cols] = dpool
            c0[TOK:n, cols] = dpool_n
            c1[0:TOK, cols] = dpool * inv[g][0:TOK]
            c1[TOK:n, cols] = dpool_n * inv[g][TOK:n]
        c2[0:n - 8, :] = c1[0:n - 8, :] + c1[1:n - 7, :]
        c3[0:n - 16, 128:] = c2[0:n - 16, 128:] + c2[2:n - 14, 128:]
        c1[0:n - 24, 256:] = c3[0:n - 24, 256:] + c3[4:n - 20, 256:]
        wins = [c2[0:TOK, 0:128], c3[0:TOK, 128:256], c1[0:TOK, 256:384],
                c1[0:TOK, 384:512] + c1[8:TOK + 8, 384:512]]
        for g in range(4):
            cols = slice(g * POOL_GD, (g + 1) * POOL_GD)
            du_ref[:, cols] = (wins[g] - c0[0:TOK, cols]).astype(BF16)

    buf = pltpu.VMEM((TOK + HALO, POOL_W), F32)
    outs, moved = _pcall(
        body, name, (nt,),
        [_row_spec(POOL_W),
         pl.BlockSpec((HALO, POOL_W), lambda i: (jnp.minimum((i + 1) * hb, nt * hb - 1), 0)),
         _row_spec(POOL_W), pl.BlockSpec((4, POOL_GD, POOL_GD), lambda i: (0, 0, 0)),
         _vec_spec(POOL_W)],
        [_row_spec(POOL_W), pl.BlockSpec((4, POOL_GD, POOL_GD), lambda i: (0, 0, 0)),
         _vec_spec(POOL_W)],
        [_sds((t, POOL_W), BF16), _sds((4, POOL_GD, POOL_GD), F32), _sds((1, POOL_W), F32)],
        [buf, buf, buf, buf], _cparams("arbitrary"), (dmixed, dmixed, pooled, wg, scale), comm)
    return outs if comm is None else (*outs, moved)


GELU_C = math.sqrt(2.0 / math.pi)


GELU_K = 0.044715


def _gelu_parts(x):
    x2 = x * x
    s = 0.5 + 0.5 * jnp.tanh(x * (GELU_C + (GELU_C * GELU_K) * x2))
    return x * s, s, x2


def _gelu(x):
    return _gelu_parts(x)[0]


def _gelu_and_grad(x):
    g, s, x2 = _gelu_parts(x)
    return g, s + g * (1.0 - s) * ((2 * GELU_C) + (6 * GELU_C * GELU_K) * x2)


def _taps(buf, r, rows):
    a = buf[pl.ds(r, rows + 8), :]
    return a[8:], pltpu.roll(a, 1, axis=0)[8:], pltpu.roll(a, 2, axis=0)[8:]


def _conv(taps, w_ref, b_ref):
    return b_ref[...] + w_ref[2:3, :] * taps[0] + w_ref[1:2, :] * taps[1] + w_ref[0:1, :] * taps[2]


def _stage(dst, prev_ref, cur_ref, next_ref, first, last):
    rows = cur_ref.shape[0]
    h = prev_ref[...].astype(F32)
    dst[0:8, :] = jnp.where(first, jnp.zeros_like(h), h)
    dst[8:8 + rows, :] = cur_ref[...].astype(F32)
    if next_ref is not None:
        h = next_ref[...].astype(F32)
        dst[8 + rows:, :] = jnp.where(last, jnp.zeros_like(h), h)


FWD_STRIP = 32
BWD_STRIP = 16


def _ffn_gate_fwd(name, hu, conv_w, conv_b, comm=None):
    t = hu.shape[0]
    ncol = D_FF // FF_COL
    hb = FF_TOK // 8

    def tile(off):
        return pl.BlockSpec((FF_TOK, FF_COL), lambda i, j: (i, j + off))

    def halo(off):
        return pl.BlockSpec((8, FF_COL), lambda i, j: (jnp.maximum(i * hb - 1, 0), j + off))

    def wspec(off):
        return pl.BlockSpec((3, FF_COL), lambda i, j: (0, j + off))

    def bspec(off):
        return pl.BlockSpec((1, FF_COL), lambda i, j: (0, j + off))

    def body(v_ref, vp_ref, g_ref, gp_ref, wv_ref, wg_ref, bv_ref, bg_ref, a_ref, vb, gb):
        first = pl.program_id(0) == 0
        _stage(vb, vp_ref, v_ref, None, first, None)
        _stage(gb, gp_ref, g_ref, None, first, None)

        def strip(k, carry):
            for u in range(2):
                r = pl.multiple_of((2 * k + u) * FWD_STRIP, FWD_STRIP)
                val = _conv(_taps(vb, r, FWD_STRIP), wv_ref, bv_ref)
                gate = _conv(_taps(gb, r, FWD_STRIP), wg_ref, bg_ref)
                a_ref[pl.ds(r, FWD_STRIP), :] = (_gelu(gate) * val).astype(BF16)
            return carry

        lax.fori_loop(0, FF_TOK // (2 * FWD_STRIP), strip, 0)

    buf = pltpu.VMEM((FF_TOK + 8, FF_COL), F32)
    out, moved = _pcall(
        body, name, (t // FF_TOK, ncol),
        [tile(0), halo(0), tile(ncol), halo(ncol), wspec(0), wspec(ncol), bspec(0), bspec(ncol)],
        pl.BlockSpec((FF_TOK, FF_COL), lambda i, j: (i, j)), _sds((t, D_FF), BF16), [buf, buf],
        _cparams("arbitrary", "arbitrary"),
        (hu, hu, hu, hu, conv_w, conv_w, conv_b, conv_b), comm)
    return out if comm is None else (out, moved)


def _ffn_gate_bwd(name, da, hu, conv_w, conv_b, comm=None):
    t = hu.shape[0]
    nt = t // FF_TOK
    ncol = D_FF // FF_COL
    hb = FF_TOK // 8
    ext = FF_TOK + 8

    def tile(off):
        return pl.BlockSpec((FF_TOK, FF_COL), lambda j, i: (i, j + off))

    def prev(off):
        return pl.BlockSpec((8, FF_COL), lambda j, i: (jnp.maximum(i * hb - 1, 0), j + off))

    def nxt(off):
        return pl.BlockSpec((8, FF_COL), lambda j, i: (jnp.minimum((i + 1) * hb, nt * hb - 1), j + off))

    def wspec(off):
        return pl.BlockSpec((3, FF_COL), lambda j, i: (0, j + off))

    def bspec(off):
        return pl.BlockSpec((1, FF_COL), lambda j, i: (0, j + off))

    def body(da_ref, dan_ref, v_ref, vp_ref, vn_ref, g_ref, gp_ref, gn_ref,
             wv_ref, wg_ref, bv_ref, bg_ref, dh_ref, dwv_ref, dwg_ref, vb, gb, dab):
        i = pl.program_id(1)
        first, last = i == 0, i == nt - 1

        @pl.when(first)
        def _():
            dwv_ref[...] = jnp.zeros_like(dwv_ref)
            dwg_ref[...] = jnp.zeros_like(dwg_ref)

        _stage(vb, vp_ref, v_ref, vn_ref, first, last)
        _stage(gb, gp_ref, g_ref, gn_ref, first, last)
        dab[0:FF_TOK, :] = da_ref[...].astype(F32)
        h = dan_ref[...].astype(F32)
        dab[FF_TOK:, :] = jnp.where(last, jnp.zeros_like(h), h)

        def grads(r, rows):
            tv, tg = _taps(vb, r, rows), _taps(gb, r, rows)
            gate = _conv(tg, wg_ref, bg_ref)
            dav = dab[pl.ds(r, rows), :]
            g, dg = _gelu_and_grad(gate)
            dval = dav * g
            dgate = dav * _conv(tv, wv_ref, bv_ref) * dg
            return dval, dgate, tv, tg

        def fold(x):
            return x[0:8] + x[8:16]

        def strip(k, carry):
            r = pl.multiple_of(FF_TOK - BWD_STRIP - k * BWD_STRIP, BWD_STRIP)
            dval, dgate, tv, tg = grads(r, BWD_STRIP)
            new = (dval[0:8], dgate[0:8])
            for half, (d, nxt_rows, taps, w_ref, dw_ref) in enumerate((
                    (dval, carry[0], tv, wv_ref, dwv_ref), (dgate, carry[1], tg, wg_ref, dwg_ref))):
                e = jnp.concatenate([d, nxt_rows], axis=0)
                dh = (w_ref[2:3, :] * d
                      + w_ref[1:2, :] * pltpu.roll(e, BWD_STRIP + 7, axis=0)[0:BWD_STRIP]
                      + w_ref[0:1, :] * pltpu.roll(e, BWD_STRIP + 6, axis=0)[0:BWD_STRIP])
                dh_ref[half, pl.ds(r, BWD_STRIP), :] = dh.astype(BF16)
                dw_ref[0:8, :] += fold(d * taps[2])
                dw_ref[8:16, :] += fold(d * taps[1])
                dw_ref[16:24, :] += fold(d * taps[0])
                dw_ref[24:32, :] += fold(d)
            return new

        dval, dgate, _, _ = grads(FF_TOK, 8)
        lax.fori_loop(0, FF_TOK // BWD_STRIP, strip, (dval, dgate))

        @pl.when(last)
        def _():
            for dw_ref in (dwv_ref, dwg_ref):
                for q in range(4):
                    dw_ref[8 * q:8 * q + 1, :] = jnp.sum(dw_ref[8 * q:8 * q + 8, :], axis=0,
                                                         keepdims=True)

    hbuf = pltpu.VMEM((FF_TOK + 16, FF_COL), F32)
    acc = pl.BlockSpec((32, FF_COL), lambda j, i: (0, j))
    (dhu, dwv, dwg), moved = _pcall(
        body, name, (ncol, nt),
        [tile(0), nxt(0), tile(0), prev(0), nxt(0), tile(ncol), prev(ncol), nxt(ncol),
         wspec(0), wspec(ncol), bspec(0), bspec(ncol)],
        [pl.BlockSpec((2, FF_TOK, FF_COL), lambda j, i: (0, i, j)), acc, acc],
        [_sds((2, t, D_FF), BF16), _sds((32, D_FF), F32), _sds((32, D_FF), F32)],
        [hbuf, hbuf, pltpu.VMEM((ext, FF_COL), F32)], _cparams("arbitrary", "arbitrary"),
        (da, da, hu, hu, hu, hu, hu, hu, conv_w, conv_w, conv_b, conv_b), comm)
    dconv = jnp.concatenate([dwv, dwg], axis=1).reshape(4, 8, 2 * D_FF)[:, 0]
    return (dhu, dconv) if comm is None else (dhu, dconv, moved)


def _mesh_pos():
    x, y, c = lax.axis_index("x"), lax.axis_index("y"), lax.axis_index("c")
    return x, y, c, [(1 - x, y), (x, 1 - y), (1 - x, 1 - y)]


def _remote(src, dst, send_sems, recv_sems, i, dev):
    return pltpu.make_async_remote_copy(src_ref=src, dst_ref=dst, send_sem=send_sems.at[i],
                                        recv_sem=recv_sems.at[i], device_id=dev,
                                        device_id_type=MESH)


def _mine(c, rows):
    return pl.ds(pl.multiple_of(c * (rows // 2), 16), rows // 2)


def _gather_send(shards, conv_shard, gathered, l):
    nbig = len(shards)
    with_conv = conv_shard is not None
    if gathered is None:
        ins = list(shards) + ([conv_shard] if with_conv else [])
        outs = [_sds((DEPTH, N_CHIPS) + s.shape[1:], s.dtype) for s in ins]
        alias = {}
    else:
        ins = list(shards) + list(gathered)
        outs = [_sds(g.shape, g.dtype) for g in gathered]
        alias = {nbig + k: k for k in range(nbig)}

    def copies(cin, cout, ssem, rsem):
        x, y, c, chips = _mesh_pos()
        me = 2 * x + y
        out = []
        for k in range(nbig):
            rows = shards[k].shape[1]
            for j, (cx, cy) in enumerate(chips):
                out.append(_remote(cin[k].at[l, _mine(c, rows)], cout[k].at[l, me, _mine(c, rows)],
                                   ssem, rsem, 4 * k + j, (cx, cy, c)))
            out.append(_remote(cin[k].at[l], cout[k].at[l, me], ssem, rsem, 4 * k + 3,
                               (x, y, 1 - c)))
        if with_conv:
            base = 4 * nbig
            for j, (cx, cy) in enumerate(chips):
                out.append(_remote(cin[nbig].at[c], cout[nbig].at[c, me], ssem, rsem, base + j,
                                   (cx, cy, c)))
            for ll in range(DEPTH):
                out.append(_remote(cin[nbig].at[ll], cout[nbig].at[ll, me], ssem, rsem,
                                   base + 3 + ll, (x, y, 1 - c)))
        return out

    return _Comm(ins, outs, copies, 4 * nbig + 5, alias)


def _gather_forward(gathered, nbig, rows, l):
    with_conv = len(gathered) > nbig
    alias = {k: k for k in range(len(gathered))}

    def copies(cin, cout, ssem, rsem):
        x, y, c, chips = _mesh_pos()
        out = []
        for k in range(nbig):
            for j, (cx, cy) in enumerate(chips):
                blk = cout[k].at[l, 2 * cx + cy, _mine(c, rows[k])]
                out.append(_remote(blk, blk, ssem, rsem, 3 * k + j, (x, y, 1 - c)))
        if with_conv:
            for j, (cx, cy) in enumerate(chips):
                blk = cout[nbig].at[c, 2 * cx + cy]
                out.append(_remote(blk, blk, ssem, rsem, 3 * nbig + j, (x, y, 1 - c)))
        return out

    return _Comm(gathered, [_sds(g.shape, g.dtype) for g in gathered], copies, 3 * nbig + 3, alias)


def _reduce_swap(grads, l):
    def copies(cin, cout, ssem, rsem):
        x, y, c, _ = _mesh_pos()
        return [_remote(cin[k].at[l, :, _mine(1 - c, g.shape[2])], cout[k], ssem, rsem, k,
                        (x, y, 1 - c)) for k, g in enumerate(grads)]

    outs = [_sds((N_CHIPS, g.shape[2] // 2, g.shape[3]), g.dtype) for g in grads]
    return _Comm(grads, outs, copies, len(grads))


def _reduce_scatter(sums):
    def copies(cin, cout, ssem, rsem):
        x, y, c, chips = _mesh_pos()
        return [_remote(cin[k].at[2 * cx + cy], cout[k].at[j], ssem, rsem, 3 * k + j, (cx, cy, c))
                for k in range(len(sums)) for j, (cx, cy) in enumerate(chips)]

    outs = [_sds((3,) + s.shape[1:], s.dtype) for s in sums]
    return _Comm(sums, outs, copies, 3 * len(sums))


def _reduce_share(reds, l):
    def copies(cin, cout, ssem, rsem):
        x, y, c, _ = _mesh_pos()
        out = []
        for k, r in enumerate(reds):
            half = cout[k].at[l, _mine(c, r.shape[1])]
            out.append(_remote(half, half, ssem, rsem, k, (x, y, 1 - c)))
        return out

    return _Comm(reds, [_sds(r.shape, r.dtype) for r in reds], copies, len(reds),
                 {k: k for k in range(len(reds))})


def _allreduce_small(per_layer):
    kinds = len(per_layer[0])
    shapes = [a.shape[1:] if a.shape[0] == 1 else a.shape for a in per_layer[0]]

    def body(*refs):
        ins = refs[:DEPTH * kinds]
        outs = refs[DEPTH * kinds:(DEPTH + 1) * kinds]
        gbufs = refs[(DEPTH + 1) * kinds:(DEPTH + 2) * kinds]
        send_sems, recv_sems = refs[-2], refs[-1]
        x, y, c, chips = _mesh_pos()
        sibling = (x, y, 1 - c)

        def copy(k, i, block, to):
            px, py, pc = block
            slot = gbufs[k].at[4 * px + 2 * py + pc]
            return _remote(slot, slot, send_sems, recv_sems, 7 * k + i, to)

        me = (x, y, c)
        first, passed = [], []
        for k in range(kinds):
            for l in range(DEPTH):
                a = ins[l * kinds + k]
                if per_layer[l][k].shape[0] == 1:
                    gbufs[k][4 * x + 2 * y + c, l:l + 1] = a[...]
                else:
                    gbufs[k][4 * x + 2 * y + c, l] = a[...]
            first.append(copy(k, 0, me, sibling))
            first += [copy(k, 1 + j, me, (*chip, c)) for j, chip in enumerate(chips)]
            passed += [copy(k, 4 + j, (*chip, c), sibling) for j, chip in enumerate(chips)]
        for cp in first:
            cp.start()
        for k in range(kinds):
            for j, chip in enumerate(chips):
                copy(k, 1 + j, (*chip, c), me).wait_recv()
                passed[3 * k + j].start()
        for k in range(kinds):
            copy(k, 0, sibling, me).wait_recv()
            for j, chip in enumerate(chips):
                copy(k, 4 + j, (*chip, 1 - c), me).wait_recv()
        for cp in first + passed:
            cp.wait_send()
        for k in range(kinds):
            acc = gbufs[k][0]
            for d in range(1, 8):
                acc = acc + gbufs[k][d]
            outs[k][...] = acc

    vmem = pl.BlockSpec(memory_space=pltpu.VMEM)
    return pl.pallas_call(
        body, name="allreduce_small",
        in_specs=[vmem] * (DEPTH * kinds), out_specs=[vmem] * kinds,
        out_shape=[_sds((DEPTH,) + s, F32) for s in shapes],
        scratch_shapes=[pltpu.VMEM((8, DEPTH) + s, F32) for s in shapes]
        + [pltpu.SemaphoreType.DMA((7 * kinds,)), pltpu.SemaphoreType.DMA((7 * kinds,))],
        compiler_params=pltpu.CompilerParams(vmem_limit_bytes=VMEM_LIMIT_V7X),
    )(*per_layer[0], *per_layer[1])


def _adamw_small(ws, gs, ms, vs):
    n = len(ws)
    c1 = 1.0 - ADAM_B1 ** ADAM_STEP
    c2 = 1.0 - ADAM_B2 ** ADAM_STEP

    def body(*refs):
        for i in range(n):
            w_ref, g_ref, m_ref, v_ref = (refs[j * n + i] for j in range(4))
            d_ref, nm_ref, nv_ref = (refs[(4 + j) * n + i] for j in range(3))
            gv = g_ref[...]
            nm = ADAM_B1 * m_ref[...] + (1.0 - ADAM_B1) * gv
            nv = ADAM_B2 * v_ref[...] + (1.0 - ADAM_B2) * (gv * gv)
            nm_ref[...] = nm
            nv_ref[...] = nv
            d_ref[...] = -ADAM_LR * ((nm / c1) / (jnp.sqrt(nv / c2) + ADAM_EPS)
                                     + ADAM_WD * w_ref[...])

    vmem = pl.BlockSpec(memory_space=pltpu.VMEM)
    outs = pl.pallas_call(
        body, name="adamw_small", in_specs=[vmem] * (4 * n), out_specs=[vmem] * (3 * n),
        out_shape=[_sds(w.shape, F32) for w in ws] * 3,
        compiler_params=pltpu.CompilerParams(vmem_limit_bytes=VMEM_LIMIT_V7X),
    )(*ws, *gs, *ms, *vs)
    return outs[:n], outs[n:2 * n], outs[2 * n:]


def _core_index():
    return jnp.reshape(lax.axis_index("c"), (1,)).astype(jnp.int32)


def _chip_index():
    return jnp.reshape(2 * lax.axis_index("x") + lax.axis_index("y"), (1,)).astype(jnp.int32)


def _chip_sums(name, stacked, sibs, l):
    n = len(stacked)
    dims = [(s.shape[2] // 2, s.shape[3]) for s in stacked]

    def body(c_ref, *refs):
        for k in range(n):
            a_ref, b_ref, o_ref = refs[k], refs[n + k], refs[2 * n + k]
            o_ref[...] = (a_ref[...].astype(F32) + b_ref[...].astype(F32)).astype(BF16)

    return pl.pallas_call(
        body, name=name,
        grid_spec=pltpu.PrefetchScalarGridSpec(
            num_scalar_prefetch=1, grid=(N_CHIPS,),
            in_specs=[pl.BlockSpec((None, None, hr, cd), lambda j, cr: (l, j, cr[0], 0))
                      for hr, cd in dims]
            + [pl.BlockSpec((None, hr, cd), lambda j, cr: (j, 0, 0)) for hr, cd in dims],
            out_specs=[pl.BlockSpec((None, hr, cd), lambda j, cr: (j, 0, 0)) for hr, cd in dims]),
        out_shape=[_sds((N_CHIPS, hr, cd), BF16) for hr, cd in dims],
        compiler_params=_cparams("parallel"))(_core_index(), *stacked, *sibs)


def _final_sums(name, sums, recvs, l, fills):
    n = len(sums)
    dims = [(s.shape[1] // 2, s.shape[2]) for s in sums]
    filled = fills[0] is not None

    def body(m_ref, *refs):
        outs = refs[-n:]
        for k in range(n):
            acc = refs[k][...].astype(F32)
            for j in range(3):
                acc = acc + refs[n + k][j].astype(F32)
            outs[k][...] = acc

    in_specs = ([pl.BlockSpec((None, tr, cd), lambda i, mr: (mr[0], i, 0)) for tr, cd in dims]
                + [pl.BlockSpec((3, tr, cd), lambda i, mr: (0, i, 0)) for tr, cd in dims])
    args = [jnp.concatenate([_chip_index(), _core_index()]), *sums, *recvs]
    aliases = {}
    if filled:
        in_specs += [pl.BlockSpec(memory_space=pl.ANY)] * n
        args += list(fills)
        aliases = {1 + 2 * n + k: k for k in range(n)}
    return pl.pallas_call(
        body, name=name,
        grid_spec=pltpu.PrefetchScalarGridSpec(
            num_scalar_prefetch=1, grid=(2,), in_specs=in_specs,
            out_specs=[pl.BlockSpec((None, tr, cd), lambda i, mr: (l, 2 * mr[1] + i, 0))
                       for tr, cd in dims]),
        out_shape=[_sds((DEPTH, 4 * tr, cd), F32) for tr, cd in dims],
        input_output_aliases=aliases,
        compiler_params=_cparams("parallel"))(*args)


def _adamw(name, w, g, m, v, comm=None):
    nl, r, cdim = w.shape
    tr = r // 4 if r % 32 == 0 else r
    c1 = 1.0 - ADAM_B1 ** ADAM_STEP
    c2 = 1.0 - ADAM_B2 ** ADAM_STEP

    def body(w_ref, g_ref, m_ref, v_ref, d_ref, nm_ref, nv_ref):
        gv = g_ref[...]
        nm = ADAM_B1 * m_ref[...] + (1.0 - ADAM_B1) * gv
        nv = ADAM_B2 * v_ref[...] + (1.0 - ADAM_B2) * (gv * gv)
        nm_ref[...] = nm
        nv_ref[...] = nv
        d_ref[...] = -ADAM_LR * ((nm / c1) / (jnp.sqrt(nv / c2) + ADAM_EPS) + ADAM_WD * w_ref[...])

    spec = pl.BlockSpec((None, tr, cdim), lambda l, i: (l, i, 0))
    out = _sds(w.shape, F32)
    outs, moved = _pcall(body, name, (nl, r // tr), [spec] * 4, [spec] * 3, [out] * 3, [],
                         _cparams("arbitrary", "arbitrary"), (w, g, m, v), comm)
    return outs if comm is None else (*outs, moved)


def kernel(x, norm_mix_pre, w_in, b_gate, rel_bias, w_attn_out, w_pool_group, pool_scale, w_pool_out, w_o, norm_mix_post, norm_ffn_pre, w_up, conv_w, conv_b, w_down, norm_ffn_post, loss_target, m_norm_mix_pre, m_w_in, m_b_gate, m_rel_bias, m_w_attn_out, m_w_pool_group, m_pool_scale, m_w_pool_out, m_w_o, m_norm_mix_post, m_norm_ffn_pre, m_w_up, m_conv_w, m_conv_b, m_w_down, m_norm_ffn_post, v_norm_mix_pre, v_w_in, v_b_gate, v_rel_bias, v_w_attn_out, v_w_pool_group, v_pool_scale, v_w_pool_out, v_w_o, v_norm_mix_post, v_norm_ffn_pre, v_w_up, v_conv_w, v_conv_b, v_w_down, v_norm_ffn_post):
    t = x.shape[1]
    xs = x.reshape(t, D_MODEL)
    target = loss_target.reshape(t, D_MODEL)

    names = ["w_in", "w_attn_out", "w_pool_out", "w_o", "w_up", "w_down"]
    shards = [w.astype(BF16) for w in (w_in, w_attn_out, w_pool_out, w_o, w_up, w_down)]
    rows = [s.shape[1] for s in shards]
    nbig = len(shards)
    h, g = _norm_fwd("l0_norm_mix_pre", x.reshape(t, D_MODEL), norm_mix_pre[0:1],
                     _gather_send(shards[:1], conv_w, None, 0))
    g = _comm_call("gather0_forward", _gather_forward(g, 1, rows[:1], 0))
    cw_full = jnp.transpose(g[1], (0, 2, 1, 3)).reshape(DEPTH, 3, 2 * D_FF)
    g = g[:1]
    wg_bf = w_pool_group.astype(BF16)

    def views(gathered):
        win_g, wao_g, wpo_g, wo_g, wup_g, wdn_g = gathered
        return (win_g, wao_g, wpo_g, wo_g.reshape(DEPTH, D_MODEL, D_MODEL), wup_g,
                wdn_g.reshape(DEPTH, D_FF, D_MODEL))

    saved = []
    xcur = xs
    for l in range(DEPTH):
        tag = f"l{l}_"
        bias = _bias_table(tag + "bias_table", rel_bias[l])
        proj = _mm_nn_blocked(tag + "proj", h, g[0], l, BF16)
        if l == 0:
            att, rest = _attn_fwd(tag + "attn_fwd", proj, bias,
                                  _gather_send(shards[1:], None, None, 0))
            pooled, mixed, rest = _pool_fwd(tag + "pool_fwd", proj, wg_bf[l], pool_scale[l:l + 1],
                                            _gather_forward(rest, nbig - 1, rows[1:], 0))
            g = g + rest
        else:
            att = _attn_fwd(tag + "attn_fwd", proj, bias)
            pooled, mixed = _pool_fwd(tag + "pool_fwd", proj, wg_bf[l], pool_scale[l:l + 1])
        win_g, wao_g, wpo_g, wo_full, wup_g, wdn_full = views(g)
        ya = _narrow_nn(tag + "attn_out", att, wao_g, l)
        yb = _narrow_nn(tag + "pool_out", mixed, wpo_g, l)
        z = _gate_fwd(tag + "gate_fwd", proj, b_gate[l:l + 1], ya, yb)
        mix = _mm_nn(tag + "mix", z, wo_full, l, D_MODEL, F32)
        x1, h2 = _post_pre_fwd(tag + "norm_mix_post", xcur, mix, norm_mix_post[l:l + 1],
                               norm_ffn_pre[l:l + 1])
        hu = _mm_nn_blocked(tag + "ffn_up", h2, wup_g, l, BF16)
        if l == 0:
            a, g = _ffn_gate_fwd(tag + "ffn_gate_fwd", hu, cw_full[l], conv_b[l:l + 1],
                                 _gather_send(shards, None, g, 1))
            wdn_full = views(g)[5]
        else:
            a = _ffn_gate_fwd(tag + "ffn_gate_fwd", hu, cw_full[l], conv_b[l:l + 1])
        f = _mm_nn(tag + "ffn_down", a, wdn_full, l, D_FF // 2, F32)
        saved.append(dict(x=xcur, h=h, proj=proj, att=att, pooled=pooled, mixed=mixed, ya=ya,
                          yb=yb, z=z, mix=mix, x1=x1, h2=h2, hu=hu, a=a, f=f, bias=bias))
        if l == 0:
            xcur, h, g = _post_pre_fwd(tag + "norm_ffn_post", x1, f, norm_ffn_post[l:l + 1],
                                       norm_mix_pre[l + 1:l + 2], _gather_forward(g, nbig, rows, 1))
        elif l < DEPTH - 1:
            xcur, h = _post_pre_fwd(tag + "norm_ffn_post", x1, f, norm_ffn_post[l:l + 1],
                                    norm_mix_pre[l + 1:l + 2])
    win_g, wao_g, wpo_g, wo_full, wup_g, wdn_full = views(g)

    dy, df, d_nfpost, loss_local = _tail("tail", saved[-1]["x1"], saved[-1]["f"],
                                         norm_ffn_post[DEPTH - 1:DEPTH], target)
    loss = lax.psum(loss_local, ("x", "y", "c"))

    dx = dy
    dws = dict.fromkeys(names)
    reds = [None] * nbig
    small_grads = [None] * DEPTH
    ffn = [4, 5]
    outs3 = [1, 2, 3]

    def blocks(ks):
        return [dws[names[k]].reshape(DEPTH, N_CHIPS, rows[k], -1) for k in ks]

    def chip_sums(ks, sib, l):
        return _chip_sums(f"chip_sums{l}_" + names[ks[0]], blocks(ks), sib, l)

    def final_sums(ks, sums, recv, l):
        outs = _final_sums(f"final_sums{l}_" + names[ks[0]], sums, recv, l, [reds[k] for k in ks])
        for k, r in zip(ks, outs):
            reds[k] = r

    for l in reversed(range(DEPTH)):
        tag = f"l{l}_"
        sv = saved[l]
        every = list(range(nbig))
        if l == 0:
            da, sib = _mm_nt(tag + "ffn_down_dx", df, wdn_full, l, D_FF // 2, BF16,
                             _reduce_swap(blocks(every), 1))
            sums = chip_sums(every, sib, 1)
        else:
            da = _mm_nt(tag + "ffn_down_dx", df, wdn_full, l, D_FF // 2, BF16)
        dws["w_down"] = _mm_tn(tag + "ffn_down_dw", sv["a"], df, D_FF // 2, l, dws["w_down"])
        if l == 0:
            dhu, dconv, recv = _ffn_gate_bwd(tag + "ffn_gate_bwd", da, sv["hu"], cw_full[l],
                                             conv_b[l:l + 1], _reduce_scatter(sums))
            final_sums(every, sums, recv, 1)
            dh2, reds = _mm_nt_blocked(tag + "ffn_up_dx", dhu, wup_g, l, F32,
                                       _reduce_share(reds, 1))
        else:
            dhu, dconv = _ffn_gate_bwd(tag + "ffn_gate_bwd", da, sv["hu"], cw_full[l],
                                       conv_b[l:l + 1])
            dh2 = _mm_nt_blocked(tag + "ffn_up_dx", dhu, wup_g, l, F32)
        dws["w_up"] = _mm_tn_blocked(tag + "ffn_up_dw", sv["h2"], dhu, l, dws["w_up"])
        if l == 0:
            dx1, d_nfpre, dmix, d_nmpost, sib = _pre_post_bwd(
                tag + "norm_ffn_pre_bwd", dh2, sv["x1"], dx, norm_ffn_pre[l:l + 1], sv["mix"],
                norm_mix_post[l:l + 1], _reduce_swap(blocks(ffn), 0))
            sums = chip_sums(ffn, sib, 0)
        else:
            dx1, d_nfpre, dmix, d_nmpost = _pre_post_bwd(
                tag + "norm_ffn_pre_bwd", dh2, sv["x1"], dx, norm_ffn_pre[l:l + 1], sv["mix"],
                norm_mix_post[l:l + 1])
        dz = _mm_nt(tag + "mix_dx", dmix, wo_full, l, D_MODEL, BF16)
        dws["w_o"] = _mm_tn(tag + "mix_dw", sv["z"], dmix, D_MODEL, l, dws["w_o"])
        dya, dyb, dgates, d_bgate = _gate_bwd(tag + "gate_bwd", dz, sv["proj"], b_gate[l:l + 1],
                                              sv["ya"], sv["yb"])
        datt = _narrow_nt(tag + "attn_out_dx", dya, wao_g, l)
        dws["w_attn_out"] = _narrow_tn(tag + "attn_out_dw", sv["att"], dya, l, dws["w_attn_out"])
        dmixed = _narrow_nt(tag + "pool_out_dx", dyb, wpo_g, l)
        dws["w_pool_out"] = _narrow_tn(tag + "pool_out_dw", sv["mixed"], dyb, l, dws["w_pool_out"])
        if l == 0:
            du, d_wg, d_pscale, sib = _pool_bwd(tag + "pool_bwd", dmixed, sv["pooled"], wg_bf[l],
                                                pool_scale[l:l + 1], _reduce_swap(blocks(outs3), 0))
            sums3 = chip_sums(outs3, sib, 0)
            dqkv, dbias, recv = _attn_bwd(
                tag + "attn_bwd", sv["proj"], datt, sv["bias"],
                _both(_reduce_scatter(sums), _reduce_scatter(sums3)))
            final_sums(ffn, sums, recv[:len(ffn)], 0)
            final_sums(outs3, sums3, recv[len(ffn):], 0)
        else:
            du, d_wg, d_pscale = _pool_bwd(tag + "pool_bwd", dmixed, sv["pooled"], wg_bf[l],
                                           pool_scale[l:l + 1])
            dqkv, dbias = _attn_bwd(tag + "attn_bwd", sv["proj"], datt, sv["bias"])
        d_rel = _bias_fold(tag + "bias_fold", dbias)
        if l == 0:
            dh, shared = _proj_dx(tag + "proj_dx", dqkv, du, dgates, win_g, l,
                                  _reduce_share([reds[k] for k in ffn + outs3], 0))
            for k, r in zip(ffn + outs3, shared):
                reds[k] = r
        else:
            dh = _proj_dx(tag + "proj_dx", dqkv, du, dgates, win_g, l)
        dws["w_in"] = _proj_dw(tag + "proj_dw", sv["h"], dqkv, du, dgates, l, dws["w_in"])
        small_grads[l] = [None, d_nmpost, d_nfpre, d_nfpost, d_bgate, d_rel, d_wg, d_pscale, dconv]
        if l > 0:
            dx, small_grads[l][0], df, d_nfpost = _pre_post_bwd(
                tag + "norm_mix_pre_bwd", dh, sv["x"], dx1, norm_mix_pre[l:l + 1],
                saved[l - 1]["f"], norm_ffn_post[l - 1:l])
        else:
            dx, small_grads[l][0], sib = _norm_pre_bwd(
                tag + "norm_mix_pre_bwd", dh, sv["x"], dx1, norm_mix_pre[l:l + 1],
                _reduce_swap(blocks([0]), 0))

    grad_x = dx.reshape(x.shape)

    delta, new_m, new_v = {}, {}, {}
    sums = chip_sums([0], sib, 0)
    delta["w_up"], new_m["w_up"], new_v["w_up"], recv = _adamw(
        "adamw_w_up", w_up, reds[4], m_w_up, v_w_up, _reduce_scatter(sums))
    final_sums([0], sums, recv, 0)
    delta["w_down"], new_m["w_down"], new_v["w_down"], shared = _adamw(
        "adamw_w_down", w_down, reds[5], m_w_down, v_w_down, _reduce_share([reds[0]], 0))
    g_big = shared + reds[1:]

    (g_nmpre, g_nmpost, g_nfpre, g_nfpost, g_bgate, g_rel, g_wg, g_pscale,
     g_conv) = _allreduce_small(small_grads)
    g_rel = g_rel[:, :, :N_REL]
    g_cb = g_conv[:, 3]
    ncw = conv_w.shape[2]
    chip = 2 * lax.axis_index("x") + lax.axis_index("y")
    g_cw = lax.dynamic_slice_in_dim(g_conv[:, 0:3], chip * ncw, ncw, axis=2)

    grads = dict(norm_mix_pre=g_nmpre, w_in=g_big[0], b_gate=g_bgate, rel_bias=g_rel,
                 w_attn_out=g_big[1], w_pool_group=g_wg, pool_scale=g_pscale, w_pool_out=g_big[2],
                 w_o=g_big[3], norm_mix_post=g_nmpost, norm_ffn_pre=g_nfpre, w_up=g_big[4],
                 conv_w=g_cw, conv_b=g_cb, w_down=g_big[5], norm_ffn_post=g_nfpost)
    weights = dict(norm_mix_pre=norm_mix_pre, w_in=w_in, b_gate=b_gate, rel_bias=rel_bias,
                   w_attn_out=w_attn_out, w_pool_group=w_pool_group, pool_scale=pool_scale,
                   w_pool_out=w_pool_out, w_o=w_o, norm_mix_post=norm_mix_post,
                   norm_ffn_pre=norm_ffn_pre, w_up=w_up, conv_w=conv_w, conv_b=conv_b,
                   w_down=w_down, norm_ffn_post=norm_ffn_post)
    moms = dict(norm_mix_pre=(m_norm_mix_pre, v_norm_mix_pre), w_in=(m_w_in, v_w_in),
                b_gate=(m_b_gate, v_b_gate), rel_bias=(m_rel_bias, v_rel_bias),
                w_attn_out=(m_w_attn_out, v_w_attn_out),
                w_pool_group=(m_w_pool_group, v_w_pool_group),
                pool_scale=(m_pool_scale, v_pool_scale), w_pool_out=(m_w_pool_out, v_w_pool_out),
                w_o=(m_w_o, v_w_o), norm_mix_post=(m_norm_mix_post, v_norm_mix_post),
                norm_ffn_pre=(m_norm_ffn_pre, v_norm_ffn_pre), w_up=(m_w_up, v_w_up),
                conv_w=(m_conv_w, v_conv_w), conv_b=(m_conv_b, v_conv_b),
                w_down=(m_w_down, v_w_down), norm_ffn_post=(m_norm_ffn_post, v_norm_ffn_post))
    order = list(weights.keys())

    small_names = [nm for nm in order if nm not in names]
    for nm in names:
        if nm not in delta:
            delta[nm], new_m[nm], new_v[nm] = _adamw("adamw_" + nm, weights[nm], grads[nm],
                                                     *moms[nm])
    d_s, m_s, v_s = _adamw_small([weights[nm] for nm in small_names],
                                 [grads[nm] for nm in small_names],
                                 [moms[nm][0] for nm in small_names],
                                 [moms[nm][1] for nm in small_names])
    for i, nm in enumerate(small_names):
        delta[nm], new_m[nm], new_v[nm] = d_s[i], m_s[i], v_s[i]

    return (loss, grad_x, *[grads[nm] for nm in order], *[delta[nm] for nm in order],
            *[new_m[nm] for nm in order], *[new_v[nm] for nm in order])
```

```python
import functools
import math

import jax
import jax.numpy as jnp
from jax import lax
from jax.experimental import pallas as pl
from jax.experimental.pallas import tpu as pltpu

F32 = jnp.float32
BF16 = jnp.bfloat16
MESH = pl.DeviceIdType.MESH

D_MODEL = 1024
DEPTH = 2
CHUNK = 64
BAND_CHUNKS = 9
BAND = BAND_CHUNKS * CHUNK
HEADS = 8
HEAD_DIM = 64
ATTN_W = HEADS * HEAD_DIM
POOL_WINDOWS = (2, 4, 8, 16)
POOL_W = 512
POOL_GD = 128
MAX_REL = 256
N_REL = 2 * MAX_REL + 1
D_FF = 2816
IN_W = 3 * ATTN_W + POOL_W + 2 * D_MODEL
EPS = 1e-6
ATTN_SCALE = HEAD_DIM ** -0.5
BAND_PAD = 640
BIAS_LANES = BAND_PAD
N_CHIPS = 4

ADAM_LR = 0.001
ADAM_B1 = 0.9
ADAM_B2 = 0.999
ADAM_EPS = 1e-08
ADAM_WD = 0.01
ADAM_STEP = 10

VMEM_LIMIT_V7X = 56 * 1024 * 1024
TOK = 512
ATT_BLK = 8 * CHUNK
FF_COL = 256
FF_TOK = 1024
HALO = 32


def _cparams(*sem):
    return pltpu.CompilerParams(dimension_semantics=sem, vmem_limit_bytes=VMEM_LIMIT_V7X)


def _sds(shape, dtype):
    return jax.ShapeDtypeStruct(shape, dtype)


class _Comm:
    def __init__(self, ins, outs, copies, n_sems, alias=None):
        self.ins, self.outs, self.copies, self.n_sems = list(ins), list(outs), copies, n_sems
        self.alias = dict(alias or {})


class _SemsFrom:
    def __init__(self, sems, start):
        self.sems, self.start = sems, start

    @property
    def at(self):
        return self

    def __getitem__(self, i):
        return self.sems.at[self.start + i]


def _both(a, b):
    na, nao = len(a.ins), len(a.outs)

    def copies(cin, cout, ssem, rsem):
        return (a.copies(cin[:na], cout[:nao], ssem, rsem)
                + b.copies(cin[na:], cout[nao:], _SemsFrom(ssem, a.n_sems), _SemsFrom(rsem, a.n_sems)))

    alias = dict(a.alias)
    alias.update({na + i: nao + o for i, o in b.alias.items()})
    return _Comm(a.ins + b.ins, a.outs + b.outs, copies, a.n_sems + b.n_sems, alias)


def _pcall(body, name, grid, in_specs, out_specs, out_shape, scratch_shapes, compiler_params, args,
           comm=None, aliases=None):
    single = not isinstance(out_shape, (list, tuple))
    out_specs = [out_specs] if single else list(out_specs)
    out_shape = [out_shape] if single else list(out_shape)
    n_in, n_out = len(in_specs), len(out_specs)
    aliases = dict(aliases or {})
    if comm is None:
        res = pl.pallas_call(
            body, name=name, grid=grid, in_specs=list(in_specs), out_specs=out_specs,
            out_shape=out_shape, scratch_shapes=list(scratch_shapes),
            input_output_aliases=aliases, compiler_params=compiler_params)(*args)
        return (res[0] if single else res), None
    ci, co = len(comm.ins), len(comm.outs)

    def hosted(*refs):
        main_in, cin = refs[:n_in], refs[n_in:n_in + ci]
        main_out = refs[n_in + ci:n_in + ci + n_out]
        cout = refs[n_in + ci + n_out:n_in + ci + n_out + co]
        rest = refs[n_in + ci + n_out + co:]
        copies = comm.copies(cin, cout, rest[-2], rest[-1])
        ids = [pl.program_id(a) for a in range(len(grid))]
        first = functools.reduce(jnp.logical_and, [i == 0 for i in ids])
        last = functools.reduce(jnp.logical_and, [i == g - 1 for i, g in zip(ids, grid)])

        @pl.when(first)
        def _():
            for cp in copies:
                cp.start()

        body(*main_in, *main_out, *rest[:-2])

        @pl.when(last)
        def _():
            for cp in copies:
                cp.wait()

    for i, o in comm.alias.items():
        aliases[n_in + i] = n_out + o
    hbm = pl.BlockSpec(memory_space=pl.ANY)
    sems = pltpu.SemaphoreType.DMA((comm.n_sems,))
    res = pl.pallas_call(
        hosted, name=name, grid=grid, in_specs=list(in_specs) + [hbm] * ci,
        out_specs=out_specs + [hbm] * co, out_shape=out_shape + comm.outs,
        scratch_shapes=list(scratch_shapes) + [sems, sems],
        input_output_aliases=aliases, compiler_params=compiler_params)(*args, *comm.ins)
    return (res[0] if single else list(res[:n_out])), list(res[n_out:])


def _comm_call(name, comm):
    ci = len(comm.ins)

    def body(*refs):
        copies = comm.copies(refs[:ci], refs[ci:-2], refs[-2], refs[-1])
        for cp in copies:
            cp.start()
        for cp in copies:
            cp.wait()

    hbm = pl.BlockSpec(memory_space=pl.ANY)
    sems = pltpu.SemaphoreType.DMA((comm.n_sems,))
    return list(pl.pallas_call(
        body, name=name, in_specs=[hbm] * ci, out_specs=[hbm] * len(comm.outs),
        out_shape=comm.outs, scratch_shapes=[sems, sems],
        input_output_aliases=comm.alias)(*comm.ins))


def _matmul(name, a, b, a_spec, b_spec, o_spec, out_shape, grid, contract, nk, acc_shape,
            fill=None, comm=None):
    def body(*refs):
        a_ref, b_ref = refs[0], refs[1]
        o_ref = refs[2 if fill is None else 3]
        scratch = refs[(3 if fill is None else 4):]
        part = lax.dot_general(a_ref[...], b_ref[...], (contract, ((), ())),
                               preferred_element_type=F32)
        if nk == 1:
            o_ref[...] = part.astype(o_ref.dtype)
        else:
            acc_ref = scratch[0]
            k = pl.program_id(2)

            @pl.when(k == 0)
            def _():
                acc_ref[...] = part

            @pl.when(k > 0)
            def _():
                acc_ref[...] += part

            @pl.when(k == nk - 1)
            def _():
                o_ref[...] = acc_ref[...].astype(o_ref.dtype)

    scratch = [] if nk == 1 else [pltpu.VMEM(acc_shape, F32)]
    in_specs, args, aliases = [a_spec, b_spec], [a, b], {}
    if fill is not None:
        in_specs.append(pl.BlockSpec(memory_space=pl.ANY))
        args.append(fill)
        aliases = {2: 0}
    out, moved = _pcall(body, name, grid, in_specs, o_spec, out_shape, scratch,
                        _cparams("parallel", "parallel", "arbitrary"), args, comm, aliases)
    return out if comm is None else (out, moved)


NN = ((1,), (0,))
NT = ((1,), (1,))
TN = ((0,), (0,))


def _tm(t):
    return min(t, 1024)


def _tt(t):
    return min(t, 2048)


def _col_block_spec(a, rows, nb, row_col):
    if a.ndim == 2:
        return pl.BlockSpec((rows, nb), row_col)

    def halves(*ids):
        r, c = row_col(*ids)
        return c // 2, r, c % 2

    return pl.BlockSpec((None, rows, nb), halves)


def _mm_nn_blocked(name, a, w, l, out_dtype, comm=None):
    t, k = a.shape
    nb = w.shape[3]
    tm = _tm(t)
    return _matmul(
        name, a, w,
        pl.BlockSpec((tm, k), lambda i, n, kk: (i, 0)),
        pl.BlockSpec((None, None, k, nb), lambda i, n, kk: (l, n, 0, 0)),
        pl.BlockSpec((tm, nb), lambda i, n, kk: (i, n)),
        _sds((t, N_CHIPS * nb), out_dtype), (t // tm, N_CHIPS, 1), NN, 1, None, comm=comm)


def _mm_nt_blocked(name, a, w, l, out_dtype, comm=None):
    t = a.shape[-2]
    k, nb = w.shape[2], w.shape[3]
    tm = _tm(t)
    return _matmul(
        name, a, w,
        _col_block_spec(a, tm, nb, lambda i, n, kk: (i, kk)),
        pl.BlockSpec((None, None, k, nb), lambda i, n, kk: (l, kk, 0, 0)),
        pl.BlockSpec((tm, k), lambda i, n, kk: (i, 0)),
        _sds((t, k), out_dtype), (t // tm, 1, N_CHIPS), NT, N_CHIPS, (tm, k), comm=comm)


def _mm_tn_blocked(name, a, g, l, fill):
    t, k = a.shape
    nb = g.shape[-1] * (g.ndim - 1) // N_CHIPS
    tt = _tt(t)
    nt = t // tt
    return _matmul(
        name, a, g,
        pl.BlockSpec((tt, k), lambda n, j, kk: (kk, 0)),
        _col_block_spec(g, tt, nb, lambda n, j, kk: (kk, n)),
        pl.BlockSpec((None, None, k, nb), lambda n, j, kk: (l, n, 0, 0)),
        _sds((DEPTH, N_CHIPS, k, nb), BF16), (N_CHIPS, 1, nt), TN, nt, (k, nb), fill)


def _proj_pieces(rows, dqkv_first):
    def piece(col):
        if dqkv_first:
            return pl.BlockSpec((rows, ATTN_W), lambda i, kk: (i, col))
        return pl.BlockSpec((rows, ATTN_W), lambda n, kk: (kk, col))
    return [piece(0), piece(1), piece(2), piece(0)]


def _proj_dx(name, dqkv, du, dgates, w, l, comm=None):
    t = du.shape[0]
    k, nb = w.shape[2], w.shape[3]
    tm = _tm(t)

    def body(dq_ref, dk_ref, dv_ref, du_ref, dg_ref, w_ref, o_ref, acc_ref):
        kk = pl.program_id(1)

        def mm(a):
            return lax.dot_general(a, w_ref[...], (NT, ((), ())), preferred_element_type=F32)

        @pl.when(kk == 0)
        def _():
            acc_ref[...] = mm(jnp.concatenate([dq_ref[...], dk_ref[...]], axis=1))

        @pl.when(kk == 1)
        def _():
            acc_ref[...] += mm(jnp.concatenate([dv_ref[...], du_ref[...]], axis=1))

        @pl.when(kk >= 2)
        def _():
            acc_ref[...] += mm(dg_ref[...])

        @pl.when(kk == N_CHIPS - 1)
        def _():
            o_ref[...] = acc_ref[...]

    out, moved = _pcall(
        body, name, (t // tm, N_CHIPS),
        _proj_pieces(tm, True)
        + [pl.BlockSpec((tm, nb), lambda i, kk: (i, jnp.maximum(kk - 2, 0))),
           pl.BlockSpec((None, None, k, nb), lambda i, kk: (l, kk, 0, 0))],
        pl.BlockSpec((tm, k), lambda i, kk: (i, 0)), _sds((t, k), F32),
        [pltpu.VMEM((tm, k), F32)], _cparams("arbitrary", "arbitrary"),
        (dqkv, dqkv, dqkv, du, dgates, w), comm)
    return out if comm is None else (out, moved)


def _proj_dw(name, h, dqkv, du, dgates, l, fill):
    t, k = h.shape
    nb = dgates.shape[1] // 2
    tt = _tm(t)
    nt = t // tt

    def body(*refs):
        h_ref, dq_ref, dk_ref, dv_ref, du_ref, dg_ref = refs[:6]
        o_ref, acc_ref = refs[-2], refs[-1]
        n, kk = pl.program_id(0), pl.program_id(1)

        def update(g):
            part = lax.dot_general(h_ref[...], g, (TN, ((), ())), preferred_element_type=F32)

            @pl.when(kk == 0)
            def _():
                acc_ref[...] = part

            @pl.when(kk > 0)
            def _():
                acc_ref[...] += part

        @pl.when(n == 0)
        def _():
            update(jnp.concatenate([dq_ref[...], dk_ref[...]], axis=1))

        @pl.when(n == 1)
        def _():
            update(jnp.concatenate([dv_ref[...], du_ref[...]], axis=1))

        @pl.when(n >= 2)
        def _():
            update(dg_ref[...])

        @pl.when(kk == nt - 1)
        def _():
            o_ref[...] = acc_ref[...].astype(BF16)

    in_specs = ([pl.BlockSpec((tt, k), lambda n, kk: (kk, 0))] + _proj_pieces(tt, False)
                + [pl.BlockSpec((tt, nb), lambda n, kk: (kk, jnp.maximum(n - 2, 0)))])
    args, aliases = [h, dqkv, dqkv, dqkv, du, dgates], {}
    if fill is not None:
        in_specs.append(pl.BlockSpec(memory_space=pl.ANY))
        args.append(fill)
        aliases = {6: 0}
    return pl.pallas_call(
        body, name=name, grid=(N_CHIPS, nt), in_specs=in_specs,
        out_specs=pl.BlockSpec((None, None, k, nb), lambda n, kk: (l, n, 0, 0)),
        out_shape=_sds((DEPTH, N_CHIPS, k, nb), BF16),
        scratch_shapes=[pltpu.VMEM((k, nb), F32)], input_output_aliases=aliases,
        compiler_params=_cparams("parallel", "arbitrary"))(*args)


def _narrow_nn(name, a, w, l):
    t, k = a.shape
    nb = w.shape[3]
    tm = _tm(t)

    def body(a_ref, w_ref, o_ref):
        av = a_ref[...]
        for j in range(N_CHIPS):
            o_ref[:, j * nb:(j + 1) * nb] = jnp.dot(
                av, w_ref[j], preferred_element_type=F32).astype(BF16)

    return pl.pallas_call(
        body, name=name, grid=(t // tm,),
        in_specs=[pl.BlockSpec((tm, k), lambda i: (i, 0)),
                  pl.BlockSpec((None, N_CHIPS, k, nb), lambda i: (l, 0, 0, 0))],
        out_specs=pl.BlockSpec((tm, N_CHIPS * nb), lambda i: (i, 0)),
        out_shape=_sds((t, N_CHIPS * nb), BF16), compiler_params=_cparams("parallel"))(a, w)


def _narrow_nt(name, a, w, l):
    t = a.shape[0]
    k, nb = w.shape[2], w.shape[3]
    tm = _tm(t)

    def body(a_ref, w_ref, o_ref):
        acc = lax.dot_general(a_ref[:, 0:nb], w_ref[0], (NT, ((), ())), preferred_element_type=F32)
        for j in range(1, N_CHIPS):
            acc = acc + lax.dot_general(a_ref[:, j * nb:(j + 1) * nb], w_ref[j], (NT, ((), ())),
                                        preferred_element_type=F32)
        o_ref[...] = acc.astype(BF16)

    return pl.pallas_call(
        body, name=name, grid=(t // tm,),
        in_specs=[pl.BlockSpec((tm, N_CHIPS * nb), lambda i: (i, 0)),
                  pl.BlockSpec((None, N_CHIPS, k, nb), lambda i: (l, 0, 0, 0))],
        out_specs=pl.BlockSpec((tm, k), lambda i: (i, 0)),
        out_shape=_sds((t, k), BF16), compiler_params=_cparams("parallel"))(a, w)


def _narrow_tn(name, a, g, l, fill):
    t, k = a.shape
    nb = g.shape[1] // N_CHIPS
    tt = _tm(t)
    nt = t // tt

    def body(*refs):
        a_ref, g_ref, o_ref, acc_ref = refs[0], refs[1], refs[-2], refs[-1]
        i = pl.program_id(0)
        part = lax.dot_general(a_ref[...], g_ref[...], (TN, ((), ())), preferred_element_type=F32)

        @pl.when(i == 0)
        def _():
            acc_ref[...] = part

        @pl.when(i > 0)
        def _():
            acc_ref[...] += part

        @pl.when(i == nt - 1)
        def _():
            for j in range(N_CHIPS):
                o_ref[j] = acc_ref[:, j * nb:(j + 1) * nb].astype(BF16)

    in_specs = [pl.BlockSpec((tt, k), lambda i: (i, 0)),
                pl.BlockSpec((tt, N_CHIPS * nb), lambda i: (i, 0))]
    args, aliases = [a, g], {}
    if fill is not None:
        in_specs.append(pl.BlockSpec(memory_space=pl.ANY))
        args.append(fill)
        aliases = {2: 0}
    return pl.pallas_call(
        body, name=name, grid=(nt,), in_specs=in_specs,
        out_specs=pl.BlockSpec((None, N_CHIPS, k, nb), lambda i: (l, 0, 0, 0)),
        out_shape=_sds((DEPTH, N_CHIPS, k, nb), BF16),
        scratch_shapes=[pltpu.VMEM((k, N_CHIPS * nb), F32)], input_output_aliases=aliases,
        compiler_params=_cparams("arbitrary"))(*args)


def _mm_nn(name, a, w, l, tk, out_dtype):
    t, k = a.shape
    n = w.shape[2]
    tm = _tm(t)
    nk = k // tk
    return _matmul(
        name, a, w,
        pl.BlockSpec((tm, tk), lambda i, j, kk: (i, kk)),
        pl.BlockSpec((None, tk, n), lambda i, j, kk: (l, kk, 0)),
        pl.BlockSpec((tm, n), lambda i, j, kk: (i, 0)),
        _sds((t, n), out_dtype), (t // tm, 1, nk), NN, nk, (tm, n))


def _mm_nt(name, a, w, l, tn, out_dtype, comm=None):
    t, n = a.shape
    k = w.shape[1]
    tm = _tm(t)
    return _matmul(
        name, a, w,
        pl.BlockSpec((tm, n), lambda i, j, kk: (i, 0)),
        pl.BlockSpec((None, tn, n), lambda i, j, kk: (l, j, 0)),
        pl.BlockSpec((tm, tn), lambda i, j, kk: (i, j)),
        _sds((t, k), out_dtype), (t // tm, k // tn, 1), NT, 1, None, comm=comm)


def _mm_tn(name, a, g, tko, l, fill):
    t, k = a.shape
    n = g.shape[1]
    tt = _tt(t)
    nt = t // tt
    return _matmul(
        name, a, g,
        pl.BlockSpec((tt, tko), lambda i, j, kk: (kk, i)),
        pl.BlockSpec((tt, n), lambda i, j, kk: (kk, 0)),
        pl.BlockSpec((None, tko, n), lambda i, j, kk: (l, i, 0)),
        _sds((DEPTH, k, n), BF16), (k // tko, 1, nt), TN, nt, (tko, n), fill)


def _row_spec(width, col=0):
    return pl.BlockSpec((TOK, width), lambda i: (i, col))


def _vec_spec(width):
    return pl.BlockSpec((1, width), lambda i: (0, 0))


def _rms(x):
    return lax.rsqrt(jnp.mean(x * x, axis=-1, keepdims=True) + EPS)


def _norm_fwd(name, x, g, comm=None):
    t = x.shape[0]

    def body(x_ref, g_ref, h_ref):
        xv = x_ref[...]
        h_ref[...] = (xv * _rms(xv) * g_ref[...]).astype(BF16)

    out, moved = _pcall(body, name, (t // TOK,), [_row_spec(D_MODEL), _vec_spec(D_MODEL)],
                        _row_spec(D_MODEL), _sds((t, D_MODEL), BF16), [], _cparams("arbitrary"),
                        (x, g), comm)
    return out if comm is None else (out, moved)


ROWS = 16
ROW_UNROLL = 8


def _rows(k):
    return pl.ds(pl.multiple_of(k * ROWS, ROWS), ROWS)


def _strips(step, init):
    def group(j, carry):
        for u in range(ROW_UNROLL):
            carry = step(j * ROW_UNROLL + u, carry)
        return carry

    return lax.fori_loop(0, TOK // (ROWS * ROW_UNROLL), group, init)


def _fold_rows(x):
    return x[0:8] + x[8:16]


def _accumulate(ref, part):
    total = jnp.sum(part, axis=0, keepdims=True)

    @pl.when(pl.program_id(0) == 0)
    def _():
        ref[...] = total

    @pl.when(pl.program_id(0) > 0)
    def _():
        ref[...] += total


def _norm_bwd_rows(d, mv, g):
    r = _rms(mv)
    n = mv * r
    dn = d * g
    return r * (dn - n * jnp.mean(dn * n, axis=-1, keepdims=True)), d * n


def _post_pre_fwd(name, xres, m, g_post, g_pre, comm=None):
    t = xres.shape[0]

    def body(x_ref, m_ref, gp_ref, gn_ref, x1_ref, h_ref):
        def strip(k, c):
            rows = _rows(k)
            mv = m_ref[rows, :]
            x1 = x_ref[rows, :] + mv * _rms(mv) * gp_ref[...]
            x1_ref[rows, :] = x1
            h_ref[rows, :] = (x1 * _rms(x1) * gn_ref[...]).astype(BF16)
            return c

        _strips(strip, 0)

    outs, moved = _pcall(
        body, name, (t // TOK,),
        [_row_spec(D_MODEL), _row_spec(D_MODEL), _vec_spec(D_MODEL), _vec_spec(D_MODEL)],
        [_row_spec(D_MODEL), _row_spec(D_MODEL)],
        [_sds((t, D_MODEL), F32), _sds((t, D_MODEL), BF16)], [], _cparams("arbitrary"),
        (xres, m, g_post, g_pre), comm)
    return outs if comm is None else (*outs, moved)


def _tail(name, xres, m, g_post, target):
    t = xres.shape[0]

    def body(x_ref, m_ref, g_ref, t_ref, dy_ref, dm_ref, dg_ref, l_ref):
        def strip(k, carry):
            rows = _rows(k)
            mv = m_ref[rows, :]
            e = x_ref[rows, :] + mv * _rms(mv) * g_ref[...] - t_ref[rows, :]
            dy = e * (1.0 / D_MODEL)
            dy_ref[rows, :] = dy
            dm, dgn = _norm_bwd_rows(dy, mv, g_ref[...])
            dm_ref[rows, :] = dm.astype(BF16)
            return carry[0] + _fold_rows(dgn), carry[1] + _fold_rows(e * e)

        zero = jnp.zeros((8, D_MODEL), F32)
        dg, sq = _strips(strip, (zero, zero))
        _accumulate(dg_ref, dg)
        _accumulate(l_ref, jnp.sum(sq, axis=1, keepdims=True))

    dy, dm, dg, sq = pl.pallas_call(
        body, name=name, grid=(t // TOK,),
        in_specs=[_row_spec(D_MODEL), _row_spec(D_MODEL), _vec_spec(D_MODEL), _row_spec(D_MODEL)],
        out_specs=[_row_spec(D_MODEL), _row_spec(D_MODEL), _vec_spec(D_MODEL),
                   pl.BlockSpec((1, 1), lambda i: (0, 0))],
        out_shape=[_sds((t, D_MODEL), F32), _sds((t, D_MODEL), BF16), _sds((1, D_MODEL), F32),
                   _sds((1, 1), F32)],
        compiler_params=_cparams("arbitrary"))(xres, m, g_post, target)
    return dy, dm, dg, sq[0, 0] * (0.5 / D_MODEL)


def _pre_post_bwd(name, dh, xin, dxo, g_pre, m, g_post, comm=None):
    t = dh.shape[0]

    def body(dh_ref, x_ref, d_ref, gq_ref, m_ref, gp_ref, dx_ref, dgq_ref, dm_ref, dgp_ref):
        def strip(k, carry):
            rows = _rows(k)
            dxin, dgq = _norm_bwd_rows(dh_ref[rows, :], x_ref[rows, :], gq_ref[...])
            dx = d_ref[rows, :] + dxin
            dx_ref[rows, :] = dx
            dm, dgp = _norm_bwd_rows(dx, m_ref[rows, :], gp_ref[...])
            dm_ref[rows, :] = dm.astype(BF16)
            return carry[0] + _fold_rows(dgq), carry[1] + _fold_rows(dgp)

        zero = jnp.zeros((8, D_MODEL), F32)
        dgq, dgp = _strips(strip, (zero, zero))
        _accumulate(dgq_ref, dgq)
        _accumulate(dgp_ref, dgp)

    outs, moved = _pcall(
        body, name, (t // TOK,),
        [_row_spec(D_MODEL), _row_spec(D_MODEL), _row_spec(D_MODEL), _vec_spec(D_MODEL),
         _row_spec(D_MODEL), _vec_spec(D_MODEL)],
        [_row_spec(D_MODEL), _vec_spec(D_MODEL), _row_spec(D_MODEL), _vec_spec(D_MODEL)],
        [_sds((t, D_MODEL), F32), _sds((1, D_MODEL), F32), _sds((t, D_MODEL), BF16),
         _sds((1, D_MODEL), F32)], [], _cparams("arbitrary"),
        (dh, xin, dxo, g_pre, m, g_post), comm)
    return outs if comm is None else (*outs, moved)


def _norm_pre_bwd(name, dh, xin, dxo, g, comm=None):
    t = dh.shape[0]

    def body(dh_ref, x_ref, d_ref, g_ref, dx_ref, dg_ref):
        xv = x_ref[...]
        dhv = dh_ref[...]
        r = _rms(xv)
        n = xv * r
        dn = dhv * g_ref[...]
        dx_ref[...] = d_ref[...] + r * (dn - n * jnp.mean(dn * n, axis=-1, keepdims=True))
        part = jnp.sum(dhv * n, axis=0, keepdims=True)

        @pl.when(pl.program_id(0) == 0)
        def _():
            dg_ref[...] = part

        @pl.when(pl.program_id(0) > 0)
        def _():
            dg_ref[...] += part

    out, moved = _pcall(
        body, name, (t // TOK,),
        [_row_spec(D_MODEL), _row_spec(D_MODEL), _row_spec(D_MODEL), _vec_spec(D_MODEL)],
        [_row_spec(D_MODEL), _vec_spec(D_MODEL)],
        [_sds((t, D_MODEL), F32), _sds((1, D_MODEL), F32)], [], _cparams("arbitrary"),
        (dh, xin, dxo, g), comm)
    return out if comm is None else (*out, moved)


def _gate_fwd(name, proj, b_gate, ya, yb):
    t = proj.shape[0]

    def body(ga_ref, gb_ref, b_ref, ya_ref, yb_ref, z_ref):
        def strip(k, c):
            rows = _rows(k)
            sa = jax.nn.sigmoid(ga_ref[rows, :].astype(F32) + b_ref[:, :D_MODEL])
            sb = jax.nn.sigmoid(gb_ref[rows, :].astype(F32) + b_ref[:, D_MODEL:])
            z_ref[rows, :] = (sa * ya_ref[rows, :].astype(F32)
                              + sb * yb_ref[rows, :].astype(F32)).astype(BF16)
            return c

        _strips(strip, 0)

    return pl.pallas_call(
        body, name=name, grid=(t // TOK,),
        in_specs=[_row_spec(D_MODEL, 2), _row_spec(D_MODEL, 3), _vec_spec(2 * D_MODEL),
                  _row_spec(D_MODEL), _row_spec(D_MODEL)],
        out_specs=_row_spec(D_MODEL), out_shape=_sds((t, D_MODEL), BF16),
        compiler_params=_cparams("parallel"))(proj, proj, b_gate, ya, yb)


def _gate_bwd(name, dz, proj, b_gate, ya, yb):
    t = proj.shape[0]

    def body(dz_ref, ga_ref, gb_ref, b_ref, ya_ref, yb_ref, dya_ref, dyb_ref, dg_ref, db_ref):
        def strip(k, carry):
            rows = _rows(k)
            dzv = dz_ref[rows, :].astype(F32)
            sa = jax.nn.sigmoid(ga_ref[rows, :].astype(F32) + b_ref[:, :D_MODEL])
            sb = jax.nn.sigmoid(gb_ref[rows, :].astype(F32) + b_ref[:, D_MODEL:])
            dya_ref[rows, :] = (dzv * sa).astype(BF16)
            dyb_ref[rows, :] = (dzv * sb).astype(BF16)
            dga = dzv * ya_ref[rows, :].astype(F32) * sa * (1.0 - sa)
            dgb = dzv * yb_ref[rows, :].astype(F32) * sb * (1.0 - sb)
            dg_ref[rows, :D_MODEL] = dga.astype(BF16)
            dg_ref[rows, D_MODEL:] = dgb.astype(BF16)
            return carry[0] + _fold_rows(dga), carry[1] + _fold_rows(dgb)

        zero = jnp.zeros((8, D_MODEL), F32)
        pa, pb = _strips(strip, (zero, zero))
        _accumulate(db_ref.at[:, :D_MODEL], pa)
        _accumulate(db_ref.at[:, D_MODEL:], pb)

    return pl.pallas_call(
        body, name=name, grid=(t // TOK,),
        in_specs=[_row_spec(D_MODEL), _row_spec(D_MODEL, 2), _row_spec(D_MODEL, 3),
                  _vec_spec(2 * D_MODEL), _row_spec(D_MODEL), _row_spec(D_MODEL)],
        out_specs=[_row_spec(D_MODEL), _row_spec(D_MODEL), _row_spec(2 * D_MODEL),
                   _vec_spec(2 * D_MODEL)],
        out_shape=[_sds((t, D_MODEL), BF16), _sds((t, D_MODEL), BF16),
                   _sds((t, 2 * D_MODEL), BF16), _sds((1, 2 * D_MODEL), F32)],
        compiler_params=_cparams("arbitrary"))(dz, proj, proj, b_gate, ya, yb)


def _head_masks():
    lane = lax.broadcasted_iota(jnp.int32, (1, 2 * HEAD_DIM), 1)
    return lane < HEAD_DIM


BAND_ROWS = 2 * ATT_BLK + CHUNK


def _fill_band(band, prev_ref, cur_ref):
    band[0:ATT_BLK, :] = prev_ref[...]
    band[ATT_BLK:2 * ATT_BLK, :] = cur_ref[...]
    band[2 * ATT_BLK:, :] = jnp.zeros((CHUNK, ATTN_W), BF16)


def _pair_rows(x2, low):
    zero = jnp.zeros_like(x2)
    return jnp.concatenate([jnp.where(low, x2, zero), jnp.where(low, zero, x2)], axis=0)


def _pair_diag(o2, low):
    return jnp.where(low, o2[0:CHUNK, :], o2[CHUNK:, :])


N_PAIRS = HEADS // 2
SM_STRIP = 32
N_STRIPS = BAND_PAD // SM_STRIP
NEG = -1e30


def _fold8(x, op):
    return op(op(x[0:8], x[8:16]), op(x[16:24], x[24:32]))


def _strip(k):
    return pl.ds(pl.multiple_of(k * SM_STRIP, SM_STRIP), SM_STRIP)


def _band_probs(k2, qcat, bias_t, first_key):
    kpos = lax.broadcasted_iota(jnp.int32, (BAND_PAD, 1), 0)
    st = lax.dot_general(k2, qcat, (NT, ((), ())), preferred_element_type=F32)
    st = jnp.where(kpos + first_key >= 0, st + bias_t, NEG)
    e = jnp.exp(st - jnp.max(st, axis=0, keepdims=True))
    return e * (1.0 / jnp.sum(e, axis=0, keepdims=True))


def _band_softmax_stats(st_ref, b_ref, first_key, dp_ref):
    rowi = lax.broadcasted_iota(jnp.int32, (SM_STRIP, 128), 0)

    def scores(k, mx):
        rows = _strip(k)
        live = (rowi + (k * SM_STRIP + first_key)) >= 0
        out = []
        for hp in range(N_PAIRS):
            x = jnp.where(live, st_ref[hp, rows, :] + b_ref[hp, rows, :], NEG)
            st_ref[hp, rows, :] = x
            out.append(jnp.maximum(mx[hp], _fold8(x, jnp.maximum)))
        return tuple(out)

    mx = lax.fori_loop(0, N_STRIPS, scores, (jnp.full((8, 128), NEG, F32),) * N_PAIRS, unroll=2)
    top = [jnp.max(m, axis=0, keepdims=True) for m in mx]

    def sums(k, acc):
        rows = _strip(k)
        ls, eds = [], []
        for hp in range(N_PAIRS):
            e = jnp.exp(st_ref[hp, rows, :] - top[hp])
            ls.append(acc[hp] + _fold8(e, jnp.add))
            eds.append(acc[N_PAIRS + hp] + _fold8(e * dp_ref[hp, rows, :], jnp.add))
        return tuple(ls + eds)

    acc = lax.fori_loop(0, N_STRIPS, sums, (jnp.zeros((8, 128), F32),) * (2 * N_PAIRS), unroll=2)
    inv = [1.0 / jnp.sum(a, axis=0, keepdims=True) for a in acc[:N_PAIRS]]
    delta = [jnp.sum(a, axis=0, keepdims=True) * i for a, i in zip(acc[N_PAIRS:], inv)]
    return top, inv, delta


def _attn_specs(nblk):
    cur = lambda col: pl.BlockSpec((ATT_BLK, ATTN_W), lambda s: (jnp.minimum(s, nblk - 1), col))
    prev = lambda col: pl.BlockSpec(
        (ATT_BLK, ATTN_W), lambda s: (jnp.maximum(jnp.minimum(s, nblk - 1) - 1, 0), col))
    return cur, prev


def _attn_fwd(name, proj, bias, comm=None):
    t = proj.shape[0]
    nblk = t // ATT_BLK
    cur, prev = _attn_specs(nblk)

    def body(q_ref, kp_ref, kc_ref, vp_ref, vc_ref, b_ref, o_ref, kband, vband):
        s = pl.program_id(0)
        _fill_band(kband, kp_ref, kc_ref)
        _fill_band(vband, vp_ref, vc_ref)
        low = _head_masks()

        def chunk(ci, carry):
            r0 = pl.multiple_of(ci * CHUNK, CHUNK)
            for hp in range(N_PAIRS):
                cols = slice(hp * 128, (hp + 1) * 128)
                qcat = _pair_rows(q_ref[pl.ds(r0, CHUNK), cols] * ATTN_SCALE, low)
                p = _band_probs(kband[pl.ds(r0, BAND_PAD), cols], qcat, b_ref[hp],
                                (s * 8 - 8 + ci) * CHUNK)
                o2 = lax.dot_general(p.astype(BF16), vband[pl.ds(r0, BAND_PAD), cols],
                                     (TN, ((), ())), preferred_element_type=F32)
                o_ref[pl.ds(r0, CHUNK), cols] = _pair_diag(o2, low).astype(BF16)
            return carry

        lax.fori_loop(0, 8, chunk, 0)

    out, moved = _pcall(
        body, name, (nblk,),
        [cur(0), prev(1), cur(1), prev(2), cur(2),
         pl.BlockSpec((N_PAIRS, BAND_PAD, 128), lambda s: (0, 0, 0))],
        pl.BlockSpec((ATT_BLK, ATTN_W), lambda s: (s, 0)), _sds((t, ATTN_W), BF16),
        [pltpu.VMEM((BAND_ROWS, ATTN_W), BF16), pltpu.VMEM((BAND_ROWS, ATTN_W), BF16)],
        _cparams("arbitrary"), (proj, proj, proj, proj, proj, bias), comm)
    return out if comm is None else (out, moved)


def _attn_bwd(name, proj, datt, bias, comm=None):
    t = proj.shape[0]
    nblk = t // ATT_BLK
    cur, prev = _attn_specs(nblk)
    late = pl.BlockSpec((ATT_BLK, 3 * ATTN_W), lambda s: (jnp.maximum(s - 1, 0), 0))

    def body(q_ref, kp_ref, kc_ref, vp_ref, vc_ref, do_ref, b_ref,
             dqkv_ref, db_ref, kband, vband, dkacc, dvacc,
             st_ref, dp_ref, pb_ref, dsb_ref, qc_ref, dc_ref, dq_ref, dq_held):
        s = pl.program_id(0)

        @pl.when(s == 0)
        def _():
            dkacc[...] = jnp.zeros_like(dkacc)
            dvacc[...] = jnp.zeros_like(dvacc)
            db_ref[...] = jnp.zeros_like(db_ref)
            dq_ref[...] = jnp.zeros_like(dq_ref)

        @pl.when(s < nblk)
        def _():
            _fill_band(kband, kp_ref, kc_ref)
            _fill_band(vband, vp_ref, vc_ref)
            low = _head_masks()

            def chunk(ci, carry):
                r0 = pl.multiple_of(ci * CHUNK, CHUNK)
                for hp in range(N_PAIRS):
                    cols = slice(hp * 128, (hp + 1) * 128)
                    qc_ref[hp] = _pair_rows(q_ref[pl.ds(r0, CHUNK), cols] * ATTN_SCALE, low)
                    dc_ref[hp] = _pair_rows(do_ref[pl.ds(r0, CHUNK), cols], low)
                    st_ref[hp] = lax.dot_general(kband[pl.ds(r0, BAND_PAD), cols], qc_ref[hp],
                                                 (NT, ((), ())), preferred_element_type=F32)
                    dp_ref[hp] = lax.dot_general(vband[pl.ds(r0, BAND_PAD), cols], dc_ref[hp],
                                                 (NT, ((), ())), preferred_element_type=F32)
                top, inv, delta = _band_softmax_stats(st_ref, b_ref, (s * 8 - 8 + ci) * CHUNK,
                                                      dp_ref)

                def grads(k, c):
                    rows = _strip(k)
                    for hp in range(N_PAIRS):
                        p = jnp.exp(st_ref[hp, rows, :] - top[hp]) * inv[hp]
                        ds = p * (dp_ref[hp, rows, :] - delta[hp])
                        db_ref[hp, rows, :] += ds
                        dsb_ref[hp, rows, :] = ds.astype(BF16)
                        pb_ref[hp, rows, :] = p.astype(BF16)
                    return c

                lax.fori_loop(0, N_STRIPS, grads, 0, unroll=2)
                for hp in range(N_PAIRS):
                    cols = slice(hp * 128, (hp + 1) * 128)
                    dq2 = lax.dot_general(dsb_ref[hp], kband[pl.ds(r0, BAND_PAD), cols],
                                          (TN, ((), ())), preferred_element_type=F32)
                    dq_ref[pl.ds(r0, CHUNK), cols] = (_pair_diag(dq2, low) * ATTN_SCALE).astype(BF16)
                    dkacc[pl.ds(r0, BAND_PAD), cols] += jnp.dot(dsb_ref[hp], qc_ref[hp],
                                                               preferred_element_type=F32)
                    dvacc[pl.ds(r0, BAND_PAD), cols] += jnp.dot(pb_ref[hp], dc_ref[hp],
                                                               preferred_element_type=F32)
                return carry

            dq_held[...] = dq_ref[...]
            lax.fori_loop(0, 8, chunk, 0)

        @pl.when(s == nblk)
        def _():
            dq_held[...] = dq_ref[...]

        dqkv_ref[:, 0:ATTN_W] = dq_held[...]
        dqkv_ref[:, ATTN_W:2 * ATTN_W] = dkacc[0:ATT_BLK, :].astype(BF16)
        dqkv_ref[:, 2 * ATTN_W:] = dvacc[0:ATT_BLK, :].astype(BF16)
        dkacc[0:ATT_BLK, :] = dkacc[ATT_BLK:2 * ATT_BLK, :]
        dvacc[0:ATT_BLK, :] = dvacc[ATT_BLK:2 * ATT_BLK, :]
        dkacc[ATT_BLK:, :] = jnp.zeros((ATT_BLK + CHUNK, ATTN_W), F32)
        dvacc[ATT_BLK:, :] = jnp.zeros((ATT_BLK + CHUNK, ATTN_W), F32)

    outs, moved = _pcall(
        body, name, (nblk + 1,),
        [cur(0), prev(1), cur(1), prev(2), cur(2),
         pl.BlockSpec((ATT_BLK, ATTN_W), lambda s: (jnp.minimum(s, nblk - 1), 0)),
         pl.BlockSpec((HEADS // 2, BAND_PAD, 128), lambda s: (0, 0, 0))],
        [late, pl.BlockSpec((HEADS // 2, BAND_PAD, 128), lambda s: (0, 0, 0))],
        [_sds((t, 3 * ATTN_W), BF16), _sds((HEADS // 2, BAND_PAD, 128), F32)],
        [pltpu.VMEM((BAND_ROWS, ATTN_W), BF16), pltpu.VMEM((BAND_ROWS, ATTN_W), BF16),
         pltpu.VMEM((BAND_ROWS, ATTN_W), F32), pltpu.VMEM((BAND_ROWS, ATTN_W), F32),
         pltpu.VMEM((N_PAIRS, BAND_PAD, 128), F32), pltpu.VMEM((N_PAIRS, BAND_PAD, 128), F32),
         pltpu.VMEM((N_PAIRS, BAND_PAD, 128), BF16), pltpu.VMEM((N_PAIRS, BAND_PAD, 128), BF16),
         pltpu.VMEM((N_PAIRS, 2 * CHUNK, 128), BF16), pltpu.VMEM((N_PAIRS, 2 * CHUNK, 128), BF16),
         pltpu.VMEM((ATT_BLK, ATTN_W), BF16), pltpu.VMEM((ATT_BLK, ATTN_W), BF16)],
        _cparams("arbitrary"), (proj, proj, proj, proj, proj, datt, bias), comm)
    return outs if comm is None else (*outs, moved)


def _diag_onehot(rel_rows):
    d0 = lax.broadcasted_iota(jnp.int32, (BIAS_LANES, BIAS_LANES), 0)
    d1 = lax.broadcasted_iota(jnp.int32, (BIAS_LANES, BIAS_LANES), 1)
    m, n = (d0, d1) if rel_rows else (d1, d0)
    hit = (m == jnp.minimum(BAND - 1 + MAX_REL - n, 2 * MAX_REL)) & (n < BAND + CHUNK - 1)
    return jnp.where(hit, 1.0, 0.0).astype(F32)


def _bias_table(name, rel_bias_l):
    rel_pad = jnp.pad(rel_bias_l, ((0, 0), (0, BIAS_LANES - N_REL)))

    def body(r_ref, o_ref):
        diag = jnp.dot(r_ref[...], _diag_onehot(True), preferred_element_type=F32,
                       precision=lax.Precision.HIGHEST)
        rowid = lax.broadcasted_iota(jnp.int32, (8, BIAS_LANES), 0)
        lane = lax.broadcasted_iota(jnp.int32, (8, BIAS_LANES), 1)
        for h in range(HEADS):
            d8 = jnp.broadcast_to(diag[h:h + 1, :], (8, BIAS_LANES))
            slab0 = pltpu.roll(d8, BIAS_LANES - CHUNK + 1, axis=1)
            for b in range(1, 8):
                slab0 = jnp.where(rowid == b, pltpu.roll(d8, BIAS_LANES - CHUNK + 1 + b, axis=1),
                                  slab0)
            for a in range(8):
                slab = slab0 if a == 0 else pltpu.roll(slab0, 8 * a, axis=1)
                o_ref[h * CHUNK + 8 * a:h * CHUNK + 8 * a + 8, :] = jnp.where(lane < BAND, slab, NEG)

    tab = pl.pallas_call(
        body, name=name,
        in_specs=[pl.BlockSpec(memory_space=pltpu.VMEM)],
        out_specs=pl.BlockSpec(memory_space=pltpu.VMEM),
        out_shape=_sds((HEADS * CHUNK, BIAS_LANES), F32),
    )(rel_pad)
    tab = tab.reshape(HEADS // 2, 2, CHUNK, BIAS_LANES)
    return jnp.transpose(tab, (0, 3, 1, 2)).reshape(HEADS // 2, BIAS_LANES, 2 * CHUNK)


def _bias_fold(name, dbias_t):
    rows = HEADS * CHUNK
    dbias = jnp.transpose(dbias_t.reshape(HEADS // 2, BIAS_LANES, 2, CHUNK), (0, 2, 3, 1))

    def body(d_ref, o_ref):
        rowid = lax.broadcasted_iota(jnp.int32, (8, BIAS_LANES), 0)
        diags = []
        for h in range(HEADS):
            acc = d_ref[h * CHUNK + 56:h * CHUNK + 64, :]
            for a in range(7):
                slab = d_ref[h * CHUNK + 8 * a:h * CHUNK + 8 * a + 8, :]
                acc = acc + pltpu.roll(slab, 56 - 8 * a, axis=1)
            tot = jnp.where(rowid == 7, acc, 0.0)
            for b in range(7):
                tot = tot + jnp.where(rowid == b, pltpu.roll(acc, 7 - b, axis=1), 0.0)
            diags.append(jnp.sum(tot, axis=0, keepdims=True))
        diag = jnp.concatenate(diags, axis=0)
        o_ref[...] = jnp.dot(diag, _diag_onehot(False), preferred_element_type=F32,
                             precision=lax.Precision.HIGHEST)

    return pl.pallas_call(
        body, name=name,
        in_specs=[pl.BlockSpec(memory_space=pltpu.VMEM)],
        out_specs=pl.BlockSpec(memory_space=pltpu.VMEM),
        out_shape=_sds((HEADS, BIAS_LANES), F32),
    )(dbias.reshape(rows, BIAS_LANES))


def _inv_counts(i):
    trow = lax.broadcasted_iota(jnp.int32, (TOK + HALO, 1), 0) + i * TOK
    return [1.0 / jnp.minimum(trow + 1, w).astype(F32) for w in POOL_WINDOWS]


def _pool_fwd(name, proj, wg, scale, comm=None):
    t = proj.shape[0]
    hb = TOK // HALO

    def body(u_ref, up_ref, wg_ref, sc_ref, pooled_ref, mixed_ref, b0, b1, b2, b3):
        i = pl.program_id(0)
        halo = up_ref[...].astype(F32)
        b0[0:HALO, :] = jnp.where(i == 0, jnp.zeros_like(halo), halo)
        b0[HALO:, :] = u_ref[...].astype(F32)
        n = TOK + HALO
        b1[8:n, :] = b0[8:n, :] + b0[7:n - 1, :]
        b2[16:n, 128:] = b1[16:n, 128:] + b1[14:n - 2, 128:]
        b3[24:n, 256:] = b2[24:n, 256:] + b2[20:n - 4, 256:]
        wins = [b1[HALO:n, 0:128], b2[HALO:n, 128:256], b3[HALO:n, 256:384],
                b3[HALO:n, 384:512] + b3[HALO - 8:n - 8, 384:512]]
        inv = _inv_counts(i)
        for g in range(4):
            cols = slice(g * POOL_GD, (g + 1) * POOL_GD)
            pooled = (wins[g] * inv[g][0:TOK] - b0[HALO:n, cols]).astype(BF16)
            pooled_ref[:, cols] = pooled
            pre = jnp.dot(pooled, wg_ref[g], preferred_element_type=F32)
            mixed_ref[:, cols] = (pre * sc_ref[:, cols]).astype(BF16)

    buf = pltpu.VMEM((TOK + HALO, POOL_W), F32)
    outs, moved = _pcall(
        body, name, (t // TOK,),
        [_row_spec(POOL_W, 3),
         pl.BlockSpec((HALO, POOL_W), lambda i: (jnp.maximum(i * hb - 1, 0), 3)),
         pl.BlockSpec((4, POOL_GD, POOL_GD), lambda i: (0, 0, 0)), _vec_spec(POOL_W)],
        [_row_spec(POOL_W), _row_spec(POOL_W)],
        [_sds((t, POOL_W), BF16), _sds((t, POOL_W), BF16)], [buf, buf, buf, buf],
        _cparams("arbitrary"), (proj, proj, wg, scale), comm)
    return outs if comm is None else (*outs, moved)


def _pool_bwd(name, dmixed, pooled, wg, scale, comm=None):
    t = dmixed.shape[0]
    nt = t // TOK
    hb = TOK // HALO

    def body(dm_ref, dmn_ref, p_ref, wg_ref, sc_ref, du_ref, dwg_ref, dsc_ref, c0, c1, c2, c3):
        i = pl.program_id(0)

        @pl.when(i == 0)
        def _():
            dwg_ref[...] = jnp.zeros_like(dwg_ref)
            dsc_ref[...] = jnp.zeros_like(dsc_ref)

        n = TOK + HALO
        inv = _inv_counts(i)
        dmv = dm_ref[...].astype(F32)
        dmn = dmn_ref[...].astype(F32)
        dmn = jnp.where(i == nt - 1, jnp.zeros_like(dmn), dmn)
        for g in range(4):
            cols = slice(g * POOL_GD, (g + 1) * POOL_GD)
            scg = sc_ref[:, cols]
            pg = p_ref[:, cols]
            dpre = (dmv[:, cols] * scg).astype(BF16)
            dpre_n = (dmn[:, cols] * scg).astype(BF16)
            pre = jnp.dot(pg, wg_ref[g], preferred_element_type=F32)
            dsc_ref[:, cols] += jnp.sum(dmv[:, cols] * pre, axis=0, keepdims=True)
            dwg_ref[g] += lax.dot_general(pg, dpre, (TN, ((), ())), preferred_element_type=F32)
            dpool = lax.dot_general(dpre, wg_ref[g], (NT, ((), ())), preferred_element_type=F32)
            dpool_n = lax.dot_general(dpre_n, wg_ref[g], (NT, ((), ())),
                                      preferred_element_type=F32)
            c0[0:TOK, cols] = dpool
            c0[TOK:n, cols] = dpool_n
            c1[0:TOK, cols] = dpool * inv[g][0:TOK]
            c1[TOK:n, cols] = dpool_n * inv[g][TOK:n]
        c2[0:n - 8, :] = c1[0:n - 8, :] + c1[1:n - 7, :]
        c3[0:n - 16, 128:] = c2[0:n - 16, 128:] + c2[2:n - 14, 128:]
        c1[0:n - 24, 256:] = c3[0:n - 24, 256:] + c3[4:n - 20, 256:]
        wins = [c2[0:TOK, 0:128], c3[0:TOK, 128:256], c1[0:TOK, 256:384],
                c1[0:TOK, 384:512] + c1[8:TOK + 8, 384:512]]
        for g in range(4):
            cols = slice(g * POOL_GD, (g + 1) * POOL_GD)
            du_ref[:, cols] = (wins[g] - c0[0:TOK, cols]).astype(BF16)

    buf = pltpu.VMEM((TOK + HALO, POOL_W), F32)
    outs, moved = _pcall(
        body, name, (nt,),
        [_row_spec(POOL_W),
         pl.BlockSpec((HALO, POOL_W), lambda i: (jnp.minimum((i + 1) * hb, nt * hb - 1), 0)),
         _row_spec(POOL_W), pl.BlockSpec((4, POOL_GD, POOL_GD), lambda i: (0, 0, 0)),
         _vec_spec(POOL_W)],
        [_row_spec(POOL_W), pl.BlockSpec((4, POOL_GD, POOL_GD), lambda i: (0, 0, 0)),
         _vec_spec(POOL_W)],
        [_sds((t, POOL_W), BF16), _sds((4, POOL_GD, POOL_GD), F32), _sds((1, POOL_W), F32)],
        [buf, buf, buf, buf], _cparams("arbitrary"), (dmixed, dmixed, pooled, wg, scale), comm)
    return outs if comm is None else (*outs, moved)


GELU_C = math.sqrt(2.0 / math.pi)


GELU_K = 0.044715


def _gelu_parts(x):
    x2 = x * x
    s = 0.5 + 0.5 * jnp.tanh(x * (GELU_C + (GELU_C * GELU_K) * x2))
    return x * s, s, x2


def _gelu(x):
    return _gelu_parts(x)[0]


def _gelu_and_grad(x):
    g, s, x2 = _gelu_parts(x)
    return g, s + g * (1.0 - s) * ((2 * GELU_C) + (6 * GELU_C * GELU_K) * x2)


def _taps(buf, r, rows):
    a = buf[pl.ds(r, rows + 8), :]
    return a[8:], pltpu.roll(a, 1, axis=0)[8:], pltpu.roll(a, 2, axis=0)[8:]


def _conv(taps, w_ref, b_ref):
    return b_ref[...] + w_ref[2:3, :] * taps[0] + w_ref[1:2, :] * taps[1] + w_ref[0:1, :] * taps[2]


def _stage(dst, prev_ref, cur_ref, next_ref, first, last):
    rows = cur_ref.shape[0]
    h = prev_ref[...].astype(F32)
    dst[0:8, :] = jnp.where(first, jnp.zeros_like(h), h)
    dst[8:8 + rows, :] = cur_ref[...].astype(F32)
    if next_ref is not None:
        h = next_ref[...].astype(F32)
        dst[8 + rows:, :] = jnp.where(last, jnp.zeros_like(h), h)


FWD_STRIP = 32
BWD_STRIP = 16


def _ffn_gate_fwd(name, hu, conv_w, conv_b, comm=None):
    t = hu.shape[0]
    ncol = D_FF // FF_COL
    hb = FF_TOK // 8

    def tile(off):
        return pl.BlockSpec((FF_TOK, FF_COL), lambda i, j: (i, j + off))

    def halo(off):
        return pl.BlockSpec((8, FF_COL), lambda i, j: (jnp.maximum(i * hb - 1, 0), j + off))

    def wspec(off):
        return pl.BlockSpec((3, FF_COL), lambda i, j: (0, j + off))

    def bspec(off):
        return pl.BlockSpec((1, FF_COL), lambda i, j: (0, j + off))

    def body(v_ref, vp_ref, g_ref, gp_ref, wv_ref, wg_ref, bv_ref, bg_ref, a_ref, vb, gb):
        first = pl.program_id(0) == 0
        _stage(vb, vp_ref, v_ref, None, first, None)
        _stage(gb, gp_ref, g_ref, None, first, None)

        def strip(k, carry):
            for u in range(2):
                r = pl.multiple_of((2 * k + u) * FWD_STRIP, FWD_STRIP)
                val = _conv(_taps(vb, r, FWD_STRIP), wv_ref, bv_ref)
                gate = _conv(_taps(gb, r, FWD_STRIP), wg_ref, bg_ref)
                a_ref[pl.ds(r, FWD_STRIP), :] = (_gelu(gate) * val).astype(BF16)
            return carry

        lax.fori_loop(0, FF_TOK // (2 * FWD_STRIP), strip, 0)

    buf = pltpu.VMEM((FF_TOK + 8, FF_COL), F32)
    out, moved = _pcall(
        body, name, (t // FF_TOK, ncol),
        [tile(0), halo(0), tile(ncol), halo(ncol), wspec(0), wspec(ncol), bspec(0), bspec(ncol)],
        pl.BlockSpec((FF_TOK, FF_COL), lambda i, j: (i, j)), _sds((t, D_FF), BF16), [buf, buf],
        _cparams("arbitrary", "arbitrary"),
        (hu, hu, hu, hu, conv_w, conv_w, conv_b, conv_b), comm)
    return out if comm is None else (out, moved)


def _ffn_gate_bwd(name, da, hu, conv_w, conv_b, comm=None):
    t = hu.shape[0]
    nt = t // FF_TOK
    ncol = D_FF // FF_COL
    hb = FF_TOK // 8
    ext = FF_TOK + 8

    def tile(off):
        return pl.BlockSpec((FF_TOK, FF_COL), lambda j, i: (i, j + off))

    def prev(off):
        return pl.BlockSpec((8, FF_COL), lambda j, i: (jnp.maximum(i * hb - 1, 0), j + off))

    def nxt(off):
        return pl.BlockSpec((8, FF_COL), lambda j, i: (jnp.minimum((i + 1) * hb, nt * hb - 1), j + off))

    def wspec(off):
        return pl.BlockSpec((3, FF_COL), lambda j, i: (0, j + off))

    def bspec(off):
        return pl.BlockSpec((1, FF_COL), lambda j, i: (0, j + off))

    def body(da_ref, dan_ref, v_ref, vp_ref, vn_ref, g_ref, gp_ref, gn_ref,
             wv_ref, wg_ref, bv_ref, bg_ref, dh_ref, dwv_ref, dwg_ref, vb, gb, dab):
        i = pl.program_id(1)
        first, last = i == 0, i == nt - 1

        @pl.when(first)
        def _():
            dwv_ref[...] = jnp.zeros_like(dwv_ref)
            dwg_ref[...] = jnp.zeros_like(dwg_ref)

        _stage(vb, vp_ref, v_ref, vn_ref, first, last)
        _stage(gb, gp_ref, g_ref, gn_ref, first, last)
        dab[0:FF_TOK, :] = da_ref[...].astype(F32)
        h = dan_ref[...].astype(F32)
        dab[FF_TOK:, :] = jnp.where(last, jnp.zeros_like(h), h)

        def grads(r, rows):
            tv, tg = _taps(vb, r, rows), _taps(gb, r, rows)
            gate = _conv(tg, wg_ref, bg_ref)
            dav = dab[pl.ds(r, rows), :]
            g, dg = _gelu_and_grad(gate)
            dval = dav * g
            dgate = dav * _conv(tv, wv_ref, bv_ref) * dg
            return dval, dgate, tv, tg

        def fold(x):
            return x[0:8] + x[8:16]

        def strip(k, carry):
            r = pl.multiple_of(FF_TOK - BWD_STRIP - k * BWD_STRIP, BWD_STRIP)
            dval, dgate, tv, tg = grads(r, BWD_STRIP)
            new = (dval[0:8], dgate[0:8])
            for half, (d, nxt_rows, taps, w_ref, dw_ref) in enumerate((
                    (dval, carry[0], tv, wv_ref, dwv_ref), (dgate, carry[1], tg, wg_ref, dwg_ref))):
                e = jnp.concatenate([d, nxt_rows], axis=0)
                dh = (w_ref[2:3, :] * d
                      + w_ref[1:2, :] * pltpu.roll(e, BWD_STRIP + 7, axis=0)[0:BWD_STRIP]
                      + w_ref[0:1, :] * pltpu.roll(e, BWD_STRIP + 6, axis=0)[0:BWD_STRIP])
                dh_ref[half, pl.ds(r, BWD_STRIP), :] = dh.astype(BF16)
                dw_ref[0:8, :] += fold(d * taps[2])
                dw_ref[8:16, :] += fold(d * taps[1])
                dw_ref[16:24, :] += fold(d * taps[0])
                dw_ref[24:32, :] += fold(d)
            return new

        dval, dgate, _, _ = grads(FF_TOK, 8)
        lax.fori_loop(0, FF_TOK // BWD_STRIP, strip, (dval, dgate))

        @pl.when(last)
        def _():
            for dw_ref in (dwv_ref, dwg_ref):
                for q in range(4):
                    dw_ref[8 * q:8 * q + 1, :] = jnp.sum(dw_ref[8 * q:8 * q + 8, :], axis=0,
                                                         keepdims=True)

    hbuf = pltpu.VMEM((FF_TOK + 16, FF_COL), F32)
    acc = pl.BlockSpec((32, FF_COL), lambda j, i: (0, j))
    (dhu, dwv, dwg), moved = _pcall(
        body, name, (ncol, nt),
        [tile(0), nxt(0), tile(0), prev(0), nxt(0), tile(ncol), prev(ncol), nxt(ncol),
         wspec(0), wspec(ncol), bspec(0), bspec(ncol)],
        [pl.BlockSpec((2, FF_TOK, FF_COL), lambda j, i: (0, i, j)), acc, acc],
        [_sds((2, t, D_FF), BF16), _sds((32, D_FF), F32), _sds((32, D_FF), F32)],
        [hbuf, hbuf, pltpu.VMEM((ext, FF_COL), F32)], _cparams("arbitrary", "arbitrary"),
        (da, da, hu, hu, hu, hu, hu, hu, conv_w, conv_w, conv_b, conv_b), comm)
    dconv = jnp.concatenate([dwv, dwg], axis=1).reshape(4, 8, 2 * D_FF)[:, 0]
    return (dhu, dconv) if comm is None else (dhu, dconv, moved)


def _mesh_pos():
    x, y, c = lax.axis_index("x"), lax.axis_index("y"), lax.axis_index("c")
    return x, y, c, [(1 - x, y), (x, 1 - y), (1 - x, 1 - y)]


def _remote(src, dst, send_sems, recv_sems, i, dev):
    return pltpu.make_async_remote_copy(src_ref=src, dst_ref=dst, send_sem=send_sems.at[i],
                                        recv_sem=recv_sems.at[i], device_id=dev,
                                        device_id_type=MESH)


def _mine(c, rows):
    return pl.ds(pl.multiple_of(c * (rows // 2), 16), rows // 2)


def _gather_send(shards, conv_shard, gathered, l):
    nbig = len(shards)
    with_conv = conv_shard is not None
    if gathered is None:
        ins = list(shards) + ([conv_shard] if with_conv else [])
        outs = [_sds((DEPTH, N_CHIPS) + s.shape[1:], s.dtype) for s in ins]
        alias = {}
    else:
        ins = list(shards) + list(gathered)
        outs = [_sds(g.shape, g.dtype) for g in gathered]
        alias = {nbig + k: k for k in range(nbig)}

    def copies(cin, cout, ssem, rsem):
        x, y, c, chips = _mesh_pos()
        me = 2 * x + y
        out = []
        for k in range(nbig):
            rows = shards[k].shape[1]
            for j, (cx, cy) in enumerate(chips):
                out.append(_remote(cin[k].at[l, _mine(c, rows)], cout[k].at[l, me, _mine(c, rows)],
                                   ssem, rsem, 4 * k + j, (cx, cy, c)))
            out.append(_remote(cin[k].at[l], cout[k].at[l, me], ssem, rsem, 4 * k + 3,
                               (x, y, 1 - c)))
        if with_conv:
            base = 4 * nbig
            for j, (cx, cy) in enumerate(chips):
                out.append(_remote(cin[nbig].at[c], cout[nbig].at[c, me], ssem, rsem, base + j,
                                   (cx, cy, c)))
            for ll in range(DEPTH):
                out.append(_remote(cin[nbig].at[ll], cout[nbig].at[ll, me], ssem, rsem,
                                   base + 3 + ll, (x, y, 1 - c)))
        return out

    return _Comm(ins, outs, copies, 4 * nbig + 5, alias)


def _gather_forward(gathered, nbig, rows, l):
    with_conv = len(gathered) > nbig
    alias = {k: k for k in range(len(gathered))}

    def copies(cin, cout, ssem, rsem):
        x, y, c, chips = _mesh_pos()
        out = []
        for k in range(nbig):
            for j, (cx, cy) in enumerate(chips):
                blk = cout[k].at[l, 2 * cx + cy, _mine(c, rows[k])]
                out.append(_remote(blk, blk, ssem, rsem, 3 * k + j, (x, y, 1 - c)))
        if with_conv:
            for j, (cx, cy) in enumerate(chips):
                blk = cout[nbig].at[c, 2 * cx + cy]
                out.append(_remote(blk, blk, ssem, rsem, 3 * nbig + j, (x, y, 1 - c)))
        return out

    return _Comm(gathered, [_sds(g.shape, g.dtype) for g in gathered], copies, 3 * nbig + 3, alias)


def _reduce_swap(grads, l):
    def copies(cin, cout, ssem, rsem):
        x, y, c, _ = _mesh_pos()
        return [_remote(cin[k].at[l, :, _mine(1 - c, g.shape[2])], cout[k], ssem, rsem, k,
                        (x, y, 1 - c)) for k, g in enumerate(grads)]

    outs = [_sds((N_CHIPS, g.shape[2] // 2, g.shape[3]), g.dtype) for g in grads]
    return _Comm(grads, outs, copies, len(grads))


def _reduce_scatter(sums):
    def copies(cin, cout, ssem, rsem):
        x, y, c, chips = _mesh_pos()
        return [_remote(cin[k].at[2 * cx + cy], cout[k].at[j], ssem, rsem, 3 * k + j, (cx, cy, c))
                for k in range(len(sums)) for j, (cx, cy) in enumerate(chips)]

    outs = [_sds((3,) + s.shape[1:], s.dtype) for s in sums]
    return _Comm(sums, outs, copies, 3 * len(sums))


def _reduce_share(reds, l):
    def copies(cin, cout, ssem, rsem):
        x, y, c, _ = _mesh_pos()
        out = []
        for k, r in enumerate(reds):
            half = cout[k].at[l, _mine(c, r.shape[1])]
            out.append(_remote(half, half, ssem, rsem, k, (x, y, 1 - c)))
        return out

    return _Comm(reds, [_sds(r.shape, r.dtype) for r in reds], copies, len(reds),
                 {k: k for k in range(len(reds))})


def _allreduce_small(per_layer):
    kinds = len(per_layer[0])
    shapes = [a.shape[1:] if a.shape[0] == 1 else a.shape for a in per_layer[0]]

    def body(*refs):
        ins = refs[:DEPTH * kinds]
        outs = refs[DEPTH * kinds:(DEPTH + 1) * kinds]
        gbufs = refs[(DEPTH + 1) * kinds:(DEPTH + 2) * kinds]
        send_sems, recv_sems = refs[-2], refs[-1]
        x, y, c, chips = _mesh_pos()
        sibling = (x, y, 1 - c)

        def copy(k, i, block, to):
            px, py, pc = block
            slot = gbufs[k].at[4 * px + 2 * py + pc]
            return _remote(slot, slot, send_sems, recv_sems, 7 * k + i, to)

        me = (x, y, c)
        first, passed = [], []
        for k in range(kinds):
            for l in range(DEPTH):
                a = ins[l * kinds + k]
                if per_layer[l][k].shape[0] == 1:
                    gbufs[k][4 * x + 2 * y + c, l:l + 1] = a[...]
                else:
                    gbufs[k][4 * x + 2 * y + c, l] = a[...]
            first.append(copy(k, 0, me, sibling))
            first += [copy(k, 1 + j, me, (*chip, c)) for j, chip in enumerate(chips)]
            passed += [copy(k, 4 + j, (*chip, c), sibling) for j, chip in enumerate(chips)]
        for cp in first:
            cp.start()
        for k in range(kinds):
            for j, chip in enumerate(chips):
                copy(k, 1 + j, (*chip, c), me).wait_recv()
                passed[3 * k + j].start()
        for k in range(kinds):
            copy(k, 0, sibling, me).wait_recv()
            for j, chip in enumerate(chips):
                copy(k, 4 + j, (*chip, 1 - c), me).wait_recv()
        for cp in first + passed:
            cp.wait_send()
        for k in range(kinds):
            acc = gbufs[k][0]
            for d in range(1, 8):
                acc = acc + gbufs[k][d]
            outs[k][...] = acc

    vmem = pl.BlockSpec(memory_space=pltpu.VMEM)
    return pl.pallas_call(
        body, name="allreduce_small",
        in_specs=[vmem] * (DEPTH * kinds), out_specs=[vmem] * kinds,
        out_shape=[_sds((DEPTH,) + s, F32) for s in shapes],
        scratch_shapes=[pltpu.VMEM((8, DEPTH) + s, F32) for s in shapes]
        + [pltpu.SemaphoreType.DMA((7 * kinds,)), pltpu.SemaphoreType.DMA((7 * kinds,))],
        compiler_params=pltpu.CompilerParams(vmem_limit_bytes=VMEM_LIMIT_V7X),
    )(*per_layer[0], *per_layer[1])


def _adamw_small(ws, gs, ms, vs):
    n = len(ws)
    c1 = 1.0 - ADAM_B1 ** ADAM_STEP
    c2 = 1.0 - ADAM_B2 ** ADAM_STEP

    def body(*refs):
        for i in range(n):
            w_ref, g_ref, m_ref, v_ref = (refs[j * n + i] for j in range(4))
            d_ref, nm_ref, nv_ref = (refs[(4 + j) * n + i] for j in range(3))
            gv = g_ref[...]
            nm = ADAM_B1 * m_ref[...] + (1.0 - ADAM_B1) * gv
            nv = ADAM_B2 * v_ref[...] + (1.0 - ADAM_B2) * (gv * gv)
            nm_ref[...] = nm
            nv_ref[...] = nv
            d_ref[...] = -ADAM_LR * ((nm / c1) / (jnp.sqrt(nv / c2) + ADAM_EPS)
                                     + ADAM_WD * w_ref[...])

    vmem = pl.BlockSpec(memory_space=pltpu.VMEM)
    outs = pl.pallas_call(
        body, name="adamw_small", in_specs=[vmem] * (4 * n), out_specs=[vmem] * (3 * n),
        out_shape=[_sds(w.shape, F32) for w in ws] * 3,
        compiler_params=pltpu.CompilerParams(vmem_limit_bytes=VMEM_LIMIT_V7X),
    )(*ws, *gs, *ms, *vs)
    return outs[:n], outs[n:2 * n], outs[2 * n:]


def _core_index():
    return jnp.reshape(lax.axis_index("c"), (1,)).astype(jnp.int32)


def _chip_index():
    return jnp.reshape(2 * lax.axis_index("x") + lax.axis_index("y"), (1,)).astype(jnp.int32)


def _chip_sums(name, stacked, sibs, l):
    n = len(stacked)
    dims = [(s.shape[2] // 2, s.shape[3]) for s in stacked]

    def body(c_ref, *refs):
        for k in range(n):
            a_ref, b_ref, o_ref = refs[k], refs[n + k], refs[2 * n + k]
            o_ref[...] = (a_ref[...].astype(F32) + b_ref[...].astype(F32)).astype(BF16)

    return pl.pallas_call(
        body, name=name,
        grid_spec=pltpu.PrefetchScalarGridSpec(
            num_scalar_prefetch=1, grid=(N_CHIPS,),
            in_specs=[pl.BlockSpec((None, None, hr, cd), lambda j, cr: (l, j, cr[0], 0))
                      for hr, cd in dims]
            + [pl.BlockSpec((None, hr, cd), lambda j, cr: (j, 0, 0)) for hr, cd in dims],
            out_specs=[pl.BlockSpec((None, hr, cd), lambda j, cr: (j, 0, 0)) for hr, cd in dims]),
        out_shape=[_sds((N_CHIPS, hr, cd), BF16) for hr, cd in dims],
        compiler_params=_cparams("parallel"))(_core_index(), *stacked, *sibs)


def _final_sums(name, sums, recvs, l, fills):
    n = len(sums)
    dims = [(s.shape[1] // 2, s.shape[2]) for s in sums]
    filled = fills[0] is not None

    def body(m_ref, *refs):
        outs = refs[-n:]
        for k in range(n):
            acc = refs[k][...].astype(F32)
            for j in range(3):
                acc = acc + refs[n + k][j].astype(F32)
            outs[k][...] = acc

    in_specs = ([pl.BlockSpec((None, tr, cd), lambda i, mr: (mr[0], i, 0)) for tr, cd in dims]
                + [pl.BlockSpec((3, tr, cd), lambda i, mr: (0, i, 0)) for tr, cd in dims])
    args = [jnp.concatenate([_chip_index(), _core_index()]), *sums, *recvs]
    aliases = {}
    if filled:
        in_specs += [pl.BlockSpec(memory_space=pl.ANY)] * n
        args += list(fills)
        aliases = {1 + 2 * n + k: k for k in range(n)}
    return pl.pallas_call(
        body, name=name,
        grid_spec=pltpu.PrefetchScalarGridSpec(
            num_scalar_prefetch=1, grid=(2,), in_specs=in_specs,
            out_specs=[pl.BlockSpec((None, tr, cd), lambda i, mr: (l, 2 * mr[1] + i, 0))
                       for tr, cd in dims]),
        out_shape=[_sds((DEPTH, 4 * tr, cd), F32) for tr, cd in dims],
        input_output_aliases=aliases,
        compiler_params=_cparams("parallel"))(*args)


def _adamw(name, w, g, m, v, comm=None):
    nl, r, cdim = w.shape
    tr = r // 4 if r % 32 == 0 else r
    c1 = 1.0 - ADAM_B1 ** ADAM_STEP
    c2 = 1.0 - ADAM_B2 ** ADAM_STEP

    def body(w_ref, g_ref, m_ref, v_ref, d_ref, nm_ref, nv_ref):
        gv = g_ref[...]
        nm = ADAM_B1 * m_ref[...] + (1.0 - ADAM_B1) * gv
        nv = ADAM_B2 * v_ref[...] + (1.0 - ADAM_B2) * (gv * gv)
        nm_ref[...] = nm
        nv_ref[...] = nv
        d_ref[...] = -ADAM_LR * ((nm / c1) / (jnp.sqrt(nv / c2) + ADAM_EPS) + ADAM_WD * w_ref[...])

    spec = pl.BlockSpec((None, tr, cdim), lambda l, i: (l, i, 0))
    out = _sds(w.shape, F32)
    outs, moved = _pcall(body, name, (nl, r // tr), [spec] * 4, [spec] * 3, [out] * 3, [],
                         _cparams("arbitrary", "arbitrary"), (w, g, m, v), comm)
    return outs if comm is None else (*outs, moved)


def kernel(x, norm_mix_pre, w_in, b_gate, rel_bias, w_attn_out, w_pool_group, pool_scale, w_pool_out, w_o, norm_mix_post, norm_ffn_pre, w_up, conv_w, conv_b, w_down, norm_ffn_post, loss_target, m_norm_mix_pre, m_w_in, m_b_gate, m_rel_bias, m_w_attn_out, m_w_pool_group, m_pool_scale, m_w_pool_out, m_w_o, m_norm_mix_post, m_norm_ffn_pre, m_w_up, m_conv_w, m_conv_b, m_w_down, m_norm_ffn_post, v_norm_mix_pre, v_w_in, v_b_gate, v_rel_bias, v_w_attn_out, v_w_pool_group, v_pool_scale, v_w_pool_out, v_w_o, v_norm_mix_post, v_norm_ffn_pre, v_w_up, v_conv_w, v_conv_b, v_w_down, v_norm_ffn_post):
    t = x.shape[1]
    xs = x.reshape(t, D_MODEL)
    target = loss_target.reshape(t, D_MODEL)

    names = ["w_in", "w_attn_out", "w_pool_out", "w_o", "w_up", "w_down"]
    shards = [w.astype(BF16) for w in (w_in, w_attn_out, w_pool_out, w_o, w_up, w_down)]
    rows = [s.shape[1] for s in shards]
    nbig = len(shards)
    h, g = _norm_fwd("l0_norm_mix_pre", x.reshape(t, D_MODEL), norm_mix_pre[0:1],
                     _gather_send(shards[:1], conv_w, None, 0))
    g = _comm_call("gather0_forward", _gather_forward(g, 1, rows[:1], 0))
    cw_full = jnp.transpose(g[1], (0, 2, 1, 3)).reshape(DEPTH, 3, 2 * D_FF)
    g = g[:1]
    wg_bf = w_pool_group.astype(BF16)

    def views(gathered):
        win_g, wao_g, wpo_g, wo_g, wup_g, wdn_g = gathered
        return (win_g, wao_g, wpo_g, wo_g.reshape(DEPTH, D_MODEL, D_MODEL), wup_g,
                wdn_g.reshape(DEPTH, D_FF, D_MODEL))

    saved = []
    xcur = xs
    for l in range(DEPTH):
        tag = f"l{l}_"
        bias = _bias_table(tag + "bias_table", rel_bias[l])
        proj = _mm_nn_blocked(tag + "proj", h, g[0], l, BF16)
        if l == 0:
            att, rest = _attn_fwd(tag + "attn_fwd", proj, bias,
                                  _gather_send(shards[1:], None, None, 0))
            pooled, mixed, rest = _pool_fwd(tag + "pool_fwd", proj, wg_bf[l], pool_scale[l:l + 1],
                                            _gather_forward(rest, nbig - 1, rows[1:], 0))
            g = g + rest
        else:
            att = _attn_fwd(tag + "attn_fwd", proj, bias)
            pooled, mixed = _pool_fwd(tag + "pool_fwd", proj, wg_bf[l], pool_scale[l:l + 1])
        win_g, wao_g, wpo_g, wo_full, wup_g, wdn_full = views(g)
        ya = _narrow_nn(tag + "attn_out", att, wao_g, l)
        yb = _narrow_nn(tag + "pool_out", mixed, wpo_g, l)
        z = _gate_fwd(tag + "gate_fwd", proj, b_gate[l:l + 1], ya, yb)
        mix = _mm_nn(tag + "mix", z, wo_full, l, D_MODEL, F32)
        x1, h2 = _post_pre_fwd(tag + "norm_mix_post", xcur, mix, norm_mix_post[l:l + 1],
                               norm_ffn_pre[l:l + 1])
        if l == 0:
            hu, mixing = _mm_nn_blocked(tag + "ffn_up", h2, wup_g, l, BF16,
                                        _gather_send(shards[:4], None, g[:4], 1))
            a, ffn_g = _ffn_gate_fwd(tag + "ffn_gate_fwd", hu, cw_full[l], conv_b[l:l + 1],
                                     _gather_send(shards[4:], None, g[4:], 1))
            g = mixing + ffn_g
            wdn_full = views(g)[5]
        else:
            hu = _mm_nn_blocked(tag + "ffn_up", h2, wup_g, l, BF16)
            a = _ffn_gate_fwd(tag + "ffn_gate_fwd", hu, cw_full[l], conv_b[l:l + 1])
        f = _mm_nn(tag + "ffn_down", a, wdn_full, l, D_FF // 2, F32)
        saved.append(dict(x=xcur, h=h, proj=proj, att=att, pooled=pooled, mixed=mixed, ya=ya,
                          yb=yb, z=z, mix=mix, x1=x1, h2=h2, hu=hu, a=a, f=f, bias=bias))
        if l == 0:
            xcur, h, g = _post_pre_fwd(tag + "norm_ffn_post", x1, f, norm_ffn_post[l:l + 1],
                                       norm_mix_pre[l + 1:l + 2], _gather_forward(g, nbig, rows, 1))
        elif l < DEPTH - 1:
            xcur, h = _post_pre_fwd(tag + "norm_ffn_post", x1, f, norm_ffn_post[l:l + 1],
                                    norm_mix_pre[l + 1:l + 2])
    win_g, wao_g, wpo_g, wo_full, wup_g, wdn_full = views(g)

    dy, df, d_nfpost, loss_local = _tail("tail", saved[-1]["x1"], saved[-1]["f"],
                                         norm_ffn_post[DEPTH - 1:DEPTH], target)
    loss = lax.psum(loss_local, ("x", "y", "c"))

    dx = dy
    dws = dict.fromkeys(names)
    reds = [None] * nbig
    small_grads = [None] * DEPTH
    ffn = [4, 5]
    outs3 = [1, 2, 3]

    def blocks(ks):
        return [dws[names[k]].reshape(DEPTH, N_CHIPS, rows[k], -1) for k in ks]

    def chip_sums(ks, sib, l):
        return _chip_sums(f"chip_sums{l}_" + names[ks[0]], blocks(ks), sib, l)

    def final_sums(ks, sums, recv, l):
        outs = _final_sums(f"final_sums{l}_" + names[ks[0]], sums, recv, l, [reds[k] for k in ks])
        for k, r in zip(ks, outs):
            reds[k] = r

    for l in reversed(range(DEPTH)):
        tag = f"l{l}_"
        sv = saved[l]
        every = list(range(nbig))
        if l == 0:
            da, sib = _mm_nt(tag + "ffn_down_dx", df, wdn_full, l, D_FF // 2, BF16,
                             _reduce_swap(blocks(every), 1))
            sums = chip_sums(every, sib, 1)
        else:
            da = _mm_nt(tag + "ffn_down_dx", df, wdn_full, l, D_FF // 2, BF16)
        dws["w_down"] = _mm_tn(tag + "ffn_down_dw", sv["a"], df, D_FF // 2, l, dws["w_down"])
        if l == 0:
            dhu, dconv, recv = _ffn_gate_bwd(tag + "ffn_gate_bwd", da, sv["hu"], cw_full[l],
                                             conv_b[l:l + 1], _reduce_scatter(sums))
            final_sums(every, sums, recv, 1)
            dh2, reds = _mm_nt_blocked(tag + "ffn_up_dx", dhu, wup_g, l, F32,
                                       _reduce_share(reds, 1))
        else:
            dhu, dconv = _ffn_gate_bwd(tag + "ffn_gate_bwd", da, sv["hu"], cw_full[l],
                                       conv_b[l:l + 1])
            dh2 = _mm_nt_blocked(tag + "ffn_up_dx", dhu, wup_g, l, F32)
        dws["w_up"] = _mm_tn_blocked(tag + "ffn_up_dw", sv["h2"], dhu, l, dws["w_up"])
        if l == 0:
            dx1, d_nfpre, dmix, d_nmpost, sib = _pre_post_bwd(
                tag + "norm_ffn_pre_bwd", dh2, sv["x1"], dx, norm_ffn_pre[l:l + 1], sv["mix"],
                norm_mix_post[l:l + 1], _reduce_swap(blocks(ffn), 0))
            sums = chip_sums(ffn, sib, 0)
        else:
            dx1, d_nfpre, dmix, d_nmpost = _pre_post_bwd(
                tag + "norm_ffn_pre_bwd", dh2, sv["x1"], dx, norm_ffn_pre[l:l + 1], sv["mix"],
                norm_mix_post[l:l + 1])
        dz = _mm_nt(tag + "mix_dx", dmix, wo_full, l, D_MODEL, BF16)
        dws["w_o"] = _mm_tn(tag + "mix_dw", sv["z"], dmix, D_MODEL, l, dws["w_o"])
        dya, dyb, dgates, d_bgate = _gate_bwd(tag + "gate_bwd", dz, sv["proj"], b_gate[l:l + 1],
                                              sv["ya"], sv["yb"])
        datt = _narrow_nt(tag + "attn_out_dx", dya, wao_g, l)
        dws["w_attn_out"] = _narrow_tn(tag + "attn_out_dw", sv["att"], dya, l, dws["w_attn_out"])
        dmixed = _narrow_nt(tag + "pool_out_dx", dyb, wpo_g, l)
        dws["w_pool_out"] = _narrow_tn(tag + "pool_out_dw", sv["mixed"], dyb, l, dws["w_pool_out"])
        if l == 0:
            du, d_wg, d_pscale, sib = _pool_bwd(tag + "pool_bwd", dmixed, sv["pooled"], wg_bf[l],
                                                pool_scale[l:l + 1], _reduce_swap(blocks(outs3), 0))
            sums3 = chip_sums(outs3, sib, 0)
            dqkv, dbias, recv = _attn_bwd(
                tag + "attn_bwd", sv["proj"], datt, sv["bias"],
                _both(_reduce_scatter(sums), _reduce_scatter(sums3)))
            final_sums(ffn, sums, recv[:len(ffn)], 0)
            final_sums(outs3, sums3, recv[len(ffn):], 0)
        else:
            du, d_wg, d_pscale = _pool_bwd(tag + "pool_bwd", dmixed, sv["pooled"], wg_bf[l],
                                           pool_scale[l:l + 1])
            dqkv, dbias = _attn_bwd(tag + "attn_bwd", sv["proj"], datt, sv["bias"])
        d_rel = _bias_fold(tag + "bias_fold", dbias)
        if l == 0:
            dh, shared = _proj_dx(tag + "proj_dx", dqkv, du, dgates, win_g, l,
                                  _reduce_share([reds[k] for k in ffn + outs3], 0))
            for k, r in zip(ffn + outs3, shared):
                reds[k] = r
        else:
            dh = _proj_dx(tag + "proj_dx", dqkv, du, dgates, win_g, l)
        dws["w_in"] = _proj_dw(tag + "proj_dw", sv["h"], dqkv, du, dgates, l, dws["w_in"])
        small_grads[l] = [None, d_nmpost, d_nfpre, d_nfpost, d_bgate, d_rel, d_wg, d_pscale, dconv]
        if l > 0:
            dx, small_grads[l][0], df, d_nfpost = _pre_post_bwd(
                tag + "norm_mix_pre_bwd", dh, sv["x"], dx1, norm_mix_pre[l:l + 1],
                saved[l - 1]["f"], norm_ffn_post[l - 1:l])
        else:
            dx, small_grads[l][0], sib = _norm_pre_bwd(
                tag + "norm_mix_pre_bwd", dh, sv["x"], dx1, norm_mix_pre[l:l + 1],
                _reduce_swap(blocks([0]), 0))

    grad_x = dx.reshape(x.shape)

    delta, new_m, new_v = {}, {}, {}
    sums = chip_sums([0], sib, 0)
    delta["w_up"], new_m["w_up"], new_v["w_up"], recv = _adamw(
        "adamw_w_up", w_up, reds[4], m_w_up, v_w_up, _reduce_scatter(sums))
    final_sums([0], sums, recv, 0)
    delta["w_down"], new_m["w_down"], new_v["w_down"], shared = _adamw(
        "adamw_w_down", w_down, reds[5], m_w_down, v_w_down, _reduce_share([reds[0]], 0))
    g_big = shared + reds[1:]

    (g_nmpre, g_nmpost, g_nfpre, g_nfpost, g_bgate, g_rel, g_wg, g_pscale,
     g_conv) = _allreduce_small(small_grads)
    g_rel = g_rel[:, :, :N_REL]
    g_cb = g_conv[:, 3]
    ncw = conv_w.shape[2]
    chip = 2 * lax.axis_index("x") + lax.axis_index("y")
    g_cw = lax.dynamic_slice_in_dim(g_conv[:, 0:3], chip * ncw, ncw, axis=2)

    grads = dict(norm_mix_pre=g_nmpre, w_in=g_big[0], b_gate=g_bgate, rel_bias=g_rel,
                 w_attn_out=g_big[1], w_pool_group=g_wg, pool_scale=g_pscale, w_pool_out=g_big[2],
                 w_o=g_big[3], norm_mix_post=g_nmpost, norm_ffn_pre=g_nfpre, w_up=g_big[4],
                 conv_w=g_cw, conv_b=g_cb, w_down=g_big[5], norm_ffn_post=g_nfpost)
    weights = dict(norm_mix_pre=norm_mix_pre, w_in=w_in, b_gate=b_gate, rel_bias=rel_bias,
                   w_attn_out=w_attn_out, w_pool_group=w_pool_group, pool_scale=pool_scale,
                   w_pool_out=w_pool_out, w_o=w_o, norm_mix_post=norm_mix_post,
                   norm_ffn_pre=norm_ffn_pre, w_up=w_up, conv_w=conv_w, conv_b=conv_b,
                   w_down=w_down, norm_ffn_post=norm_ffn_post)
    moms = dict(norm_mix_pre=(m_norm_mix_pre, v_norm_mix_pre), w_in=(m_w_in, v_w_in),
                b_gate=(m_b_gate, v_b_gate), rel_bias=(m_rel_bias, v_rel_bias),
                w_attn_out=(m_w_attn_out, v_w_attn_out),
                w_pool_group=(m_w_pool_group, v_w_pool_group),
                pool_scale=(m_pool_scale, v_pool_scale), w_pool_out=(m_w_pool_out, v_w_pool_out),
                w_o=(m_w_o, v_w_o), norm_mix_post=(m_norm_mix_post, v_norm_mix_post),
                norm_ffn_pre=(m_norm_ffn_pre, v_norm_ffn_pre), w_up=(m_w_up, v_w_up),
                conv_w=(m_conv_w, v_conv_w), conv_b=(m_conv_b, v_conv_b),
                w_down=(m_w_down, v_w_down), norm_ffn_post=(m_norm_ffn_post, v_norm_ffn_post))
    order = list(weights.keys())

    small_names = [nm for nm in order if nm not in names]
    for nm in names:
        if nm not in delta:
            delta[nm], new_m[nm], new_v[nm] = _adamw("adamw_" + nm, weights[nm], grads[nm],
                                                     *moms[nm])
    d_s, m_s, v_s = _adamw_small([weights[nm] for nm in small_names],
                                 [grads[nm] for nm in small_names],
                                 [moms[nm][0] for nm in small_names],
                                 [moms[nm][1] for nm in small_names])
    for i, nm in enumerate(small_names):
        delta[nm], new_m[nm], new_v[nm] = d_s[i], m_s[i], v_s[i]

    return (loss, grad_x, *[grads[nm] for nm in order], *[delta[nm] for nm in order],
            *[new_m[nm] for nm in order], *[new_v[nm] for nm in order])
```

```python
import functools
import math

import jax
import jax.numpy as jnp
from jax import lax
from jax.experimental import pallas as pl
from jax.experimental.pallas import tpu as pltpu

F32 = jnp.float32
BF16 = jnp.bfloat16
MESH = pl.DeviceIdType.MESH

D_MODEL = 1024
DEPTH = 2
CHUNK = 64
BAND_CHUNKS = 9
BAND = BAND_CHUNKS * CHUNK
HEADS = 8
HEAD_DIM = 64
ATTN_W = HEADS * HEAD_DIM
POOL_WINDOWS = (2, 4, 8, 16)
POOL_W = 512
POOL_GD = 128
MAX_REL = 256
N_REL = 2 * MAX_REL + 1
D_FF = 2816
IN_W = 3 * ATTN_W + POOL_W + 2 * D_MODEL
EPS = 1e-6
ATTN_SCALE = HEAD_DIM ** -0.5
BAND_PAD = 640
BIAS_LANES = BAND_PAD
N_CHIPS = 4

ADAM_LR = 0.001
ADAM_B1 = 0.9
ADAM_B2 = 0.999
ADAM_EPS = 1e-08
ADAM_WD = 0.01
ADAM_STEP = 10

VMEM_LIMIT_V7X = 56 * 1024 * 1024
TOK = 512
ATT_BLK = 8 * CHUNK
FF_COL = 256
FF_TOK = 1024
HALO = 32


def _cparams(*sem):
    return pltpu.CompilerParams(dimension_semantics=sem, vmem_limit_bytes=VMEM_LIMIT_V7X)


def _sds(shape, dtype):
    return jax.ShapeDtypeStruct(shape, dtype)


class _Comm:
    def __init__(self, ins, outs, copies, n_sems, alias=None):
        self.ins, self.outs, self.copies, self.n_sems = list(ins), list(outs), copies, n_sems
        self.alias = dict(alias or {})


class _SemsFrom:
    def __init__(self, sems, start):
        self.sems, self.start = sems, start

    @property
    def at(self):
        return self

    def __getitem__(self, i):
        return self.sems.at[self.start + i]


def _both(a, b):
    na, nao = len(a.ins), len(a.outs)

    def copies(cin, cout, ssem, rsem):
        return (a.copies(cin[:na], cout[:nao], ssem, rsem)
                + b.copies(cin[na:], cout[nao:], _SemsFrom(ssem, a.n_sems), _SemsFrom(rsem, a.n_sems)))

    alias = dict(a.alias)
    alias.update({na + i: nao + o for i, o in b.alias.items()})
    return _Comm(a.ins + b.ins, a.outs + b.outs, copies, a.n_sems + b.n_sems, alias)


def _pcall(body, name, grid, in_specs, out_specs, out_shape, scratch_shapes, compiler_params, args,
           comm=None, aliases=None):
    single = not isinstance(out_shape, (list, tuple))
    out_specs = [out_specs] if single else list(out_specs)
    out_shape = [out_shape] if single else list(out_shape)
    n_in, n_out = len(in_specs), len(out_specs)
    aliases = dict(aliases or {})
    if comm is None:
        res = pl.pallas_call(
            body, name=name, grid=grid, in_specs=list(in_specs), out_specs=out_specs,
            out_shape=out_shape, scratch_shapes=list(scratch_shapes),
            input_output_aliases=aliases, compiler_params=compiler_params)(*args)
        return (res[0] if single else res), None
    ci, co = len(comm.ins), len(comm.outs)

    def hosted(*refs):
        main_in, cin = refs[:n_in], refs[n_in:n_in + ci]
        main_out = refs[n_in + ci:n_in + ci + n_out]
        cout = refs[n_in + ci + n_out:n_in + ci + n_out + co]
        rest = refs[n_in + ci + n_out + co:]
        copies = comm.copies(cin, cout, rest[-2], rest[-1])
        ids = [pl.program_id(a) for a in range(len(grid))]
        first = functools.reduce(jnp.logical_and, [i == 0 for i in ids])
        last = functools.reduce(jnp.logical_and, [i == g - 1 for i, g in zip(ids, grid)])

        @pl.when(first)
        def _():
            for cp in copies:
                cp.start()

        body(*main_in, *main_out, *rest[:-2])

        @pl.when(last)
        def _():
            for cp in copies:
                cp.wait()

    for i, o in comm.alias.items():
        aliases[n_in + i] = n_out + o
    hbm = pl.BlockSpec(memory_space=pl.ANY)
    sems = pltpu.SemaphoreType.DMA((comm.n_sems,))
    res = pl.pallas_call(
        hosted, name=name, grid=grid, in_specs=list(in_specs) + [hbm] * ci,
        out_specs=out_specs + [hbm] * co, out_shape=out_shape + comm.outs,
        scratch_shapes=list(scratch_shapes) + [sems, sems],
        input_output_aliases=aliases, compiler_params=compiler_params)(*args, *comm.ins)
    return (res[0] if single else list(res[:n_out])), list(res[n_out:])


def _comm_call(name, comm):
    ci = len(comm.ins)

    def body(*refs):
        copies = comm.copies(refs[:ci], refs[ci:-2], refs[-2], refs[-1])
        for cp in copies:
            cp.start()
        for cp in copies:
            cp.wait()

    hbm = pl.BlockSpec(memory_space=pl.ANY)
    sems = pltpu.SemaphoreType.DMA((comm.n_sems,))
    return list(pl.pallas_call(
        body, name=name, in_specs=[hbm] * ci, out_specs=[hbm] * len(comm.outs),
        out_shape=comm.outs, scratch_shapes=[sems, sems],
        input_output_aliases=comm.alias)(*comm.ins))


def _matmul(name, a, b, a_spec, b_spec, o_spec, out_shape, grid, contract, nk, acc_shape,
            fill=None, comm=None):
    def body(*refs):
        a_ref, b_ref = refs[0], refs[1]
        o_ref = refs[2 if fill is None else 3]
        scratch = refs[(3 if fill is None else 4):]
        part = lax.dot_general(a_ref[...], b_ref[...], (contract, ((), ())),
                               preferred_element_type=F32)
        if nk == 1:
            o_ref[...] = part.astype(o_ref.dtype)
        else:
            acc_ref = scratch[0]
            k = pl.program_id(2)

            @pl.when(k == 0)
            def _():
                acc_ref[...] = part

            @pl.when(k > 0)
            def _():
                acc_ref[...] += part

            @pl.when(k == nk - 1)
            def _():
                o_ref[...] = acc_ref[...].astype(o_ref.dtype)

    scratch = [] if nk == 1 else [pltpu.VMEM(acc_shape, F32)]
    in_specs, args, aliases = [a_spec, b_spec], [a, b], {}
    if fill is not None:
        in_specs.append(pl.BlockSpec(memory_space=pl.ANY))
        args.append(fill)
        aliases = {2: 0}
    out, moved = _pcall(body, name, grid, in_specs, o_spec, out_shape, scratch,
                        _cparams("parallel", "parallel", "arbitrary"), args, comm, aliases)
    return out if comm is None else (out, moved)


NN = ((1,), (0,))
NT = ((1,), (1,))
TN = ((0,), (0,))


def _tm(t):
    return min(t, 1024)


def _tt(t):
    return min(t, 2048)


def _col_block_spec(a, rows, nb, row_col):
    if a.ndim == 2:
        return pl.BlockSpec((rows, nb), row_col)

    def halves(*ids):
        r, c = row_col(*ids)
        return c // 2, r, c % 2

    return pl.BlockSpec((None, rows, nb), halves)


def _mm_nn_blocked(name, a, w, l, out_dtype, comm=None):
    t, k = a.shape
    nb = w.shape[3]
    tm = _tm(t)
    return _matmul(
        name, a, w,
        pl.BlockSpec((tm, k), lambda i, n, kk: (i, 0)),
        pl.BlockSpec((None, None, k, nb), lambda i, n, kk: (l, n, 0, 0)),
        pl.BlockSpec((tm, nb), lambda i, n, kk: (i, n)),
        _sds((t, N_CHIPS * nb), out_dtype), (t // tm, N_CHIPS, 1), NN, 1, None, comm=comm)


def _mm_nt_blocked(name, a, w, l, out_dtype, comm=None):
    t = a.shape[-2]
    k, nb = w.shape[2], w.shape[3]
    tm = _tm(t)
    return _matmul(
        name, a, w,
        _col_block_spec(a, tm, nb, lambda i, n, kk: (i, kk)),
        pl.BlockSpec((None, None, k, nb), lambda i, n, kk: (l, kk, 0, 0)),
        pl.BlockSpec((tm, k), lambda i, n, kk: (i, 0)),
        _sds((t, k), out_dtype), (t // tm, 1, N_CHIPS), NT, N_CHIPS, (tm, k), comm=comm)


def _mm_tn_blocked(name, a, g, l, fill):
    t, k = a.shape
    nb = g.shape[-1] * (g.ndim - 1) // N_CHIPS
    tt = _tt(t)
    nt = t // tt
    return _matmul(
        name, a, g,
        pl.BlockSpec((tt, k), lambda n, j, kk: (kk, 0)),
        _col_block_spec(g, tt, nb, lambda n, j, kk: (kk, n)),
        pl.BlockSpec((None, None, k, nb), lambda n, j, kk: (l, n, 0, 0)),
        _sds((DEPTH, N_CHIPS, k, nb), BF16), (N_CHIPS, 1, nt), TN, nt, (k, nb), fill)


def _proj_pieces(rows, dqkv_first):
    def piece(col):
        if dqkv_first:
            return pl.BlockSpec((rows, ATTN_W), lambda i, kk: (i, col))
        return pl.BlockSpec((rows, ATTN_W), lambda n, kk: (kk, col))
    return [piece(0), piece(1), piece(2), piece(0)]


def _proj_dx(name, dqkv, du, dgates, w, l, comm=None):
    t = du.shape[0]
    k, nb = w.shape[2], w.shape[3]
    tm = _tm(t)

    def body(dq_ref, dk_ref, dv_ref, du_ref, dg_ref, w_ref, o_ref, acc_ref):
        kk = pl.program_id(1)

        def mm(a):
            return lax.dot_general(a, w_ref[...], (NT, ((), ())), preferred_element_type=F32)

        @pl.when(kk == 0)
        def _():
            acc_ref[...] = mm(jnp.concatenate([dq_ref[...], dk_ref[...]], axis=1))

        @pl.when(kk == 1)
        def _():
            acc_ref[...] += mm(jnp.concatenate([dv_ref[...], du_ref[...]], axis=1))

        @pl.when(kk >= 2)
        def _():
            acc_ref[...] += mm(dg_ref[...])

        @pl.when(kk == N_CHIPS - 1)
        def _():
            o_ref[...] = acc_ref[...]

    out, moved = _pcall(
        body, name, (t // tm, N_CHIPS),
        _proj_pieces(tm, True)
        + [pl.BlockSpec((tm, nb), lambda i, kk: (i, jnp.maximum(kk - 2, 0))),
           pl.BlockSpec((None, None, k, nb), lambda i, kk: (l, kk, 0, 0))],
        pl.BlockSpec((tm, k), lambda i, kk: (i, 0)), _sds((t, k), F32),
        [pltpu.VMEM((tm, k), F32)], _cparams("arbitrary", "arbitrary"),
        (dqkv, dqkv, dqkv, du, dgates, w), comm)
    return out if comm is None else (out, moved)


def _proj_dw(name, h, dqkv, du, dgates, l, fill):
    t, k = h.shape
    nb = dgates.shape[1] // 2
    tt = _tm(t)
    nt = t // tt

    def body(*refs):
        h_ref, dq_ref, dk_ref, dv_ref, du_ref, dg_ref = refs[:6]
        o_ref, acc_ref = refs[-2], refs[-1]
        n, kk = pl.program_id(0), pl.program_id(1)

        def update(g):
            part = lax.dot_general(h_ref[...], g, (TN, ((), ())), preferred_element_type=F32)

            @pl.when(kk == 0)
            def _():
                acc_ref[...] = part

            @pl.when(kk > 0)
            def _():
                acc_ref[...] += part

        @pl.when(n == 0)
        def _():
            update(jnp.concatenate([dq_ref[...], dk_ref[...]], axis=1))

        @pl.when(n == 1)
        def _():
            update(jnp.concatenate([dv_ref[...], du_ref[...]], axis=1))

        @pl.when(n >= 2)
        def _():
            update(dg_ref[...])

        @pl.when(kk == nt - 1)
        def _():
            o_ref[...] = acc_ref[...].astype(BF16)

    in_specs = ([pl.BlockSpec((tt, k), lambda n, kk: (kk, 0))] + _proj_pieces(tt, False)
                + [pl.BlockSpec((tt, nb), lambda n, kk: (kk, jnp.maximum(n - 2, 0)))])
    args, aliases = [h, dqkv, dqkv, dqkv, du, dgates], {}
    if fill is not None:
        in_specs.append(pl.BlockSpec(memory_space=pl.ANY))
        args.append(fill)
        aliases = {6: 0}
    return pl.pallas_call(
        body, name=name, grid=(N_CHIPS, nt), in_specs=in_specs,
        out_specs=pl.BlockSpec((None, None, k, nb), lambda n, kk: (l, n, 0, 0)),
        out_shape=_sds((DEPTH, N_CHIPS, k, nb), BF16),
        scratch_shapes=[pltpu.VMEM((k, nb), F32)], input_output_aliases=aliases,
        compiler_params=_cparams("parallel", "arbitrary"))(*args)


def _narrow_nn(name, a, w, l):
    t, k = a.shape
    nb = w.shape[3]
    tm = _tm(t)

    def body(a_ref, w_ref, o_ref):
        av = a_ref[...]
        for j in range(N_CHIPS):
            o_ref[:, j * nb:(j + 1) * nb] = jnp.dot(
                av, w_ref[j], preferred_element_type=F32).astype(BF16)

    return pl.pallas_call(
        body, name=name, grid=(t // tm,),
        in_specs=[pl.BlockSpec((tm, k), lambda i: (i, 0)),
                  pl.BlockSpec((None, N_CHIPS, k, nb), lambda i: (l, 0, 0, 0))],
        out_specs=pl.BlockSpec((tm, N_CHIPS * nb), lambda i: (i, 0)),
        out_shape=_sds((t, N_CHIPS * nb), BF16), compiler_params=_cparams("parallel"))(a, w)


def _narrow_nt(name, a, w, l):
    t = a.shape[0]
    k, nb = w.shape[2], w.shape[3]
    tm = _tm(t)

    def body(a_ref, w_ref, o_ref):
        acc = lax.dot_general(a_ref[:, 0:nb], w_ref[0], (NT, ((), ())), preferred_element_type=F32)
        for j in range(1, N_CHIPS):
            acc = acc + lax.dot_general(a_ref[:, j * nb:(j + 1) * nb], w_ref[j], (NT, ((), ())),
                                        preferred_element_type=F32)
        o_ref[...] = acc.astype(BF16)

    return pl.pallas_call(
        body, name=name, grid=(t // tm,),
        in_specs=[pl.BlockSpec((tm, N_CHIPS * nb), lambda i: (i, 0)),
                  pl.BlockSpec((None, N_CHIPS, k, nb), lambda i: (l, 0, 0, 0))],
        out_specs=pl.BlockSpec((tm, k), lambda i: (i, 0)),
        out_shape=_sds((t, k), BF16), compiler_params=_cparams("parallel"))(a, w)


def _narrow_tn(name, a, g, l, fill):
    t, k = a.shape
    nb = g.shape[1] // N_CHIPS
    tt = _tm(t)
    nt = t // tt

    def body(*refs):
        a_ref, g_ref, o_ref, acc_ref = refs[0], refs[1], refs[-2], refs[-1]
        i = pl.program_id(0)
        part = lax.dot_general(a_ref[...], g_ref[...], (TN, ((), ())), preferred_element_type=F32)

        @pl.when(i == 0)
        def _():
            acc_ref[...] = part

        @pl.when(i > 0)
        def _():
            acc_ref[...] += part

        @pl.when(i == nt - 1)
        def _():
            for j in range(N_CHIPS):
                o_ref[j] = acc_ref[:, j * nb:(j + 1) * nb].astype(BF16)

    in_specs = [pl.BlockSpec((tt, k), lambda i: (i, 0)),
                pl.BlockSpec((tt, N_CHIPS * nb), lambda i: (i, 0))]
    args, aliases = [a, g], {}
    if fill is not None:
        in_specs.append(pl.BlockSpec(memory_space=pl.ANY))
        args.append(fill)
        aliases = {2: 0}
    return pl.pallas_call(
        body, name=name, grid=(nt,), in_specs=in_specs,
        out_specs=pl.BlockSpec((None, N_CHIPS, k, nb), lambda i: (l, 0, 0, 0)),
        out_shape=_sds((DEPTH, N_CHIPS, k, nb), BF16),
        scratch_shapes=[pltpu.VMEM((k, N_CHIPS * nb), F32)], input_output_aliases=aliases,
        compiler_params=_cparams("arbitrary"))(*args)


def _mm_nn(name, a, w, l, tk, out_dtype):
    t, k = a.shape
    n = w.shape[2]
    tm = _tm(t)
    nk = k // tk
    return _matmul(
        name, a, w,
        pl.BlockSpec((tm, tk), lambda i, j, kk: (i, kk)),
        pl.BlockSpec((None, tk, n), lambda i, j, kk: (l, kk, 0)),
        pl.BlockSpec((tm, n), lambda i, j, kk: (i, 0)),
        _sds((t, n), out_dtype), (t // tm, 1, nk), NN, nk, (tm, n))


def _mm_nt(name, a, w, l, tn, out_dtype, comm=None):
    t, n = a.shape
    k = w.shape[1]
    tm = _tm(t)
    return _matmul(
        name, a, w,
        pl.BlockSpec((tm, n), lambda i, j, kk: (i, 0)),
        pl.BlockSpec((None, tn, n), lambda i, j, kk: (l, j, 0)),
        pl.BlockSpec((tm, tn), lambda i, j, kk: (i, j)),
        _sds((t, k), out_dtype), (t // tm, k // tn, 1), NT, 1, None, comm=comm)


def _mm_tn(name, a, g, tko, l, fill):
    t, k = a.shape
    n = g.shape[1]
    tt = _tt(t)
    nt = t // tt
    return _matmul(
        name, a, g,
        pl.BlockSpec((tt, tko), lambda i, j, kk: (kk, i)),
        pl.BlockSpec((tt, n), lambda i, j, kk: (kk, 0)),
        pl.BlockSpec((None, tko, n), lambda i, j, kk: (l, i, 0)),
        _sds((DEPTH, k, n), BF16), (k // tko, 1, nt), TN, nt, (tko, n), fill)


def _row_spec(width, col=0):
    return pl.BlockSpec((TOK, width), lambda i: (i, col))


def _vec_spec(width):
    return pl.BlockSpec((1, width), lambda i: (0, 0))


def _rms(x):
    return lax.rsqrt(jnp.mean(x * x, axis=-1, keepdims=True) + EPS)


def _norm_fwd(name, x, g, comm=None):
    t = x.shape[0]

    def body(x_ref, g_ref, h_ref):
        xv = x_ref[...]
        h_ref[...] = (xv * _rms(xv) * g_ref[...]).astype(BF16)

    out, moved = _pcall(body, name, (t // TOK,), [_row_spec(D_MODEL), _vec_spec(D_MODEL)],
                        _row_spec(D_MODEL), _sds((t, D_MODEL), BF16), [], _cparams("arbitrary"),
                        (x, g), comm)
    return out if comm is None else (out, moved)


ROWS = 16
ROW_UNROLL = 8


def _rows(k):
    return pl.ds(pl.multiple_of(k * ROWS, ROWS), ROWS)


def _strips(step, init):
    def group(j, carry):
        for u in range(ROW_UNROLL):
            carry = step(j * ROW_UNROLL + u, carry)
        return carry

    return lax.fori_loop(0, TOK // (ROWS * ROW_UNROLL), group, init)


def _fold_rows(x):
    return x[0:8] + x[8:16]


def _accumulate(ref, part):
    total = jnp.sum(part, axis=0, keepdims=True)

    @pl.when(pl.program_id(0) == 0)
    def _():
        ref[...] = total

    @pl.when(pl.program_id(0) > 0)
    def _():
        ref[...] += total


def _norm_bwd_rows(d, mv, g):
    r = _rms(mv)
    n = mv * r
    dn = d * g
    return r * (dn - n * jnp.mean(dn * n, axis=-1, keepdims=True)), d * n


def _post_pre_fwd(name, xres, m, g_post, g_pre, comm=None):
    t = xres.shape[0]

    def body(x_ref, m_ref, gp_ref, gn_ref, x1_ref, h_ref):
        def strip(k, c):
            rows = _rows(k)
            mv = m_ref[rows, :]
            x1 = x_ref[rows, :] + mv * _rms(mv) * gp_ref[...]
            x1_ref[rows, :] = x1
            h_ref[rows, :] = (x1 * _rms(x1) * gn_ref[...]).astype(BF16)
            return c

        _strips(strip, 0)

    outs, moved = _pcall(
        body, name, (t // TOK,),
        [_row_spec(D_MODEL), _row_spec(D_MODEL), _vec_spec(D_MODEL), _vec_spec(D_MODEL)],
        [_row_spec(D_MODEL), _row_spec(D_MODEL)],
        [_sds((t, D_MODEL), F32), _sds((t, D_MODEL), BF16)], [], _cparams("arbitrary"),
        (xres, m, g_post, g_pre), comm)
    return outs if comm is None else (*outs, moved)


def _tail(name, xres, m, g_post, target):
    t = xres.shape[0]

    def body(x_ref, m_ref, g_ref, t_ref, dy_ref, dm_ref, dg_ref, l_ref):
        def strip(k, carry):
            rows = _rows(k)
            mv = m_ref[rows, :]
            e = x_ref[rows, :] + mv * _rms(mv) * g_ref[...] - t_ref[rows, :]
            dy = e * (1.0 / D_MODEL)
            dy_ref[rows, :] = dy
            dm, dgn = _norm_bwd_rows(dy, mv, g_ref[...])
            dm_ref[rows, :] = dm.astype(BF16)
            return carry[0] + _fold_rows(dgn), carry[1] + _fold_rows(e * e)

        zero = jnp.zeros((8, D_MODEL), F32)
        dg, sq = _strips(strip, (zero, zero))
        _accumulate(dg_ref, dg)
        _accumulate(l_ref, jnp.sum(sq, axis=1, keepdims=True))

    dy, dm, dg, sq = pl.pallas_call(
        body, name=name, grid=(t // TOK,),
        in_specs=[_row_spec(D_MODEL), _row_spec(D_MODEL), _vec_spec(D_MODEL), _row_spec(D_MODEL)],
        out_specs=[_row_spec(D_MODEL), _row_spec(D_MODEL), _vec_spec(D_MODEL),
                   pl.BlockSpec((1, 1), lambda i: (0, 0))],
        out_shape=[_sds((t, D_MODEL), F32), _sds((t, D_MODEL), BF16), _sds((1, D_MODEL), F32),
                   _sds((1, 1), F32)],
        compiler_params=_cparams("arbitrary"))(xres, m, g_post, target)
    return dy, dm, dg, sq[0, 0] * (0.5 / D_MODEL)


def _pre_post_bwd(name, dh, xin, dxo, g_pre, m, g_post, comm=None):
    t = dh.shape[0]

    def body(dh_ref, x_ref, d_ref, gq_ref, m_ref, gp_ref, dx_ref, dgq_ref, dm_ref, dgp_ref):
        def strip(k, carry):
            rows = _rows(k)
            dxin, dgq = _norm_bwd_rows(dh_ref[rows, :], x_ref[rows, :], gq_ref[...])
            dx = d_ref[rows, :] + dxin
            dx_ref[rows, :] = dx
            dm, dgp = _norm_bwd_rows(dx, m_ref[rows, :], gp_ref[...])
            dm_ref[rows, :] = dm.astype(BF16)
            return carry[0] + _fold_rows(dgq), carry[1] + _fold_rows(dgp)

        zero = jnp.zeros((8, D_MODEL), F32)
        dgq, dgp = _strips(strip, (zero, zero))
        _accumulate(dgq_ref, dgq)
        _accumulate(dgp_ref, dgp)

    outs, moved = _pcall(
        body, name, (t // TOK,),
        [_row_spec(D_MODEL), _row_spec(D_MODEL), _row_spec(D_MODEL), _vec_spec(D_MODEL),
         _row_spec(D_MODEL), _vec_spec(D_MODEL)],
        [_row_spec(D_MODEL), _vec_spec(D_MODEL), _row_spec(D_MODEL), _vec_spec(D_MODEL)],
        [_sds((t, D_MODEL), F32), _sds((1, D_MODEL), F32), _sds((t, D_MODEL), BF16),
         _sds((1, D_MODEL), F32)], [], _cparams("arbitrary"),
        (dh, xin, dxo, g_pre, m, g_post), comm)
    return outs if comm is None else (*outs, moved)


def _norm_pre_bwd(name, dh, xin, dxo, g, comm=None):
    t = dh.shape[0]

    def body(dh_ref, x_ref, d_ref, g_ref, dx_ref, dg_ref):
        xv = x_ref[...]
        dhv = dh_ref[...]
        r = _rms(xv)
        n = xv * r
        dn = dhv * g_ref[...]
        dx_ref[...] = d_ref[...] + r * (dn - n * jnp.mean(dn * n, axis=-1, keepdims=True))
        part = jnp.sum(dhv * n, axis=0, keepdims=True)

        @pl.when(pl.program_id(0) == 0)
        def _():
            dg_ref[...] = part

        @pl.when(pl.program_id(0) > 0)
        def _():
            dg_ref[...] += part

    out, moved = _pcall(
        body, name, (t // TOK,),
        [_row_spec(D_MODEL), _row_spec(D_MODEL), _row_spec(D_MODEL), _vec_spec(D_MODEL)],
        [_row_spec(D_MODEL), _vec_spec(D_MODEL)],
        [_sds((t, D_MODEL), F32), _sds((1, D_MODEL), F32)], [], _cparams("arbitrary"),
        (dh, xin, dxo, g), comm)
    return out if comm is None else (*out, moved)


def _gate_fwd(name, proj, b_gate, ya, yb):
    t = proj.shape[0]

    def body(ga_ref, gb_ref, b_ref, ya_ref, yb_ref, z_ref):
        def strip(k, c):
            rows = _rows(k)
            sa = jax.nn.sigmoid(ga_ref[rows, :].astype(F32) + b_ref[:, :D_MODEL])
            sb = jax.nn.sigmoid(gb_ref[rows, :].astype(F32) + b_ref[:, D_MODEL:])
            z_ref[rows, :] = (sa * ya_ref[rows, :].astype(F32)
                              + sb * yb_ref[rows, :].astype(F32)).astype(BF16)
            return c

        _strips(strip, 0)

    return pl.pallas_call(
        body, name=name, grid=(t // TOK,),
        in_specs=[_row_spec(D_MODEL, 2), _row_spec(D_MODEL, 3), _vec_spec(2 * D_MODEL),
                  _row_spec(D_MODEL), _row_spec(D_MODEL)],
        out_specs=_row_spec(D_MODEL), out_shape=_sds((t, D_MODEL), BF16),
        compiler_params=_cparams("parallel"))(proj, proj, b_gate, ya, yb)


def _gate_bwd(name, dz, proj, b_gate, ya, yb):
    t = proj.shape[0]

    def body(dz_ref, ga_ref, gb_ref, b_ref, ya_ref, yb_ref, dya_ref, dyb_ref, dg_ref, db_ref):
        def strip(k, carry):
            rows = _rows(k)
            dzv = dz_ref[rows, :].astype(F32)
            sa = jax.nn.sigmoid(ga_ref[rows, :].astype(F32) + b_ref[:, :D_MODEL])
            sb = jax.nn.sigmoid(gb_ref[rows, :].astype(F32) + b_ref[:, D_MODEL:])
            dya_ref[rows, :] = (dzv * sa).astype(BF16)
            dyb_ref[rows, :] = (dzv * sb).astype(BF16)
            dga = dzv * ya_ref[rows, :].astype(F32) * sa * (1.0 - sa)
            dgb = dzv * yb_ref[rows, :].astype(F32) * sb * (1.0 - sb)
            dg_ref[rows, :D_MODEL] = dga.astype(BF16)
            dg_ref[rows, D_MODEL:] = dgb.astype(BF16)
            return carry[0] + _fold_rows(dga), carry[1] + _fold_rows(dgb)

        zero = jnp.zeros((8, D_MODEL), F32)
        pa, pb = _strips(strip, (zero, zero))
        _accumulate(db_ref.at[:, :D_MODEL], pa)
        _accumulate(db_ref.at[:, D_MODEL:], pb)

    return pl.pallas_call(
        body, name=name, grid=(t // TOK,),
        in_specs=[_row_spec(D_MODEL), _row_spec(D_MODEL, 2), _row_spec(D_MODEL, 3),
                  _vec_spec(2 * D_MODEL), _row_spec(D_MODEL), _row_spec(D_MODEL)],
        out_specs=[_row_spec(D_MODEL), _row_spec(D_MODEL), _row_spec(2 * D_MODEL),
                   _vec_spec(2 * D_MODEL)],
        out_shape=[_sds((t, D_MODEL), BF16), _sds((t, D_MODEL), BF16),
                   _sds((t, 2 * D_MODEL), BF16), _sds((1, 2 * D_MODEL), F32)],
        compiler_params=_cparams("arbitrary"))(dz, proj, proj, b_gate, ya, yb)


def _head_masks():
    lane = lax.broadcasted_iota(jnp.int32, (1, 2 * HEAD_DIM), 1)
    return lane < HEAD_DIM


BAND_ROWS = 2 * ATT_BLK + CHUNK


def _fill_band(band, prev_ref, cur_ref):
    band[0:ATT_BLK, :] = prev_ref[...]
    band[ATT_BLK:2 * ATT_BLK, :] = cur_ref[...]
    band[2 * ATT_BLK:, :] = jnp.zeros((CHUNK, ATTN_W), BF16)


def _pair_rows(x2, low):
    zero = jnp.zeros_like(x2)
    return jnp.concatenate([jnp.where(low, x2, zero), jnp.where(low, zero, x2)], axis=0)


def _pair_diag(o2, low):
    return jnp.where(low, o2[0:CHUNK, :], o2[CHUNK:, :])


N_PAIRS = HEADS // 2
SM_STRIP = 32
N_STRIPS = BAND_PAD // SM_STRIP
NEG = -1e30


def _fold8(x, op):
    return op(op(x[0:8], x[8:16]), op(x[16:24], x[24:32]))


def _strip(k):
    return pl.ds(pl.multiple_of(k * SM_STRIP, SM_STRIP), SM_STRIP)


def _band_probs(k2, qcat, bias_t, first_key):
    kpos = lax.broadcasted_iota(jnp.int32, (BAND_PAD, 1), 0)
    st = lax.dot_general(k2, qcat, (NT, ((), ())), preferred_element_type=F32)
    st = jnp.where(kpos + first_key >= 0, st + bias_t, NEG)
    e = jnp.exp(st - jnp.max(st, axis=0, keepdims=True))
    return e * (1.0 / jnp.sum(e, axis=0, keepdims=True))


def _band_softmax_stats(st_ref, b_ref, first_key, dp_ref):
    rowi = lax.broadcasted_iota(jnp.int32, (SM_STRIP, 128), 0)

    def scores(k, mx):
        rows = _strip(k)
        live = (rowi + (k * SM_STRIP + first_key)) >= 0
        out = []
        for hp in range(N_PAIRS):
            x = jnp.where(live, st_ref[hp, rows, :] + b_ref[hp, rows, :], NEG)
            st_ref[hp, rows, :] = x
            out.append(jnp.maximum(mx[hp], _fold8(x, jnp.maximum)))
        return tuple(out)

    mx = lax.fori_loop(0, N_STRIPS, scores, (jnp.full((8, 128), NEG, F32),) * N_PAIRS, unroll=2)
    top = [jnp.max(m, axis=0, keepdims=True) for m in mx]

    def sums(k, acc):
        rows = _strip(k)
        ls, eds = [], []
        for hp in range(N_PAIRS):
            e = jnp.exp(st_ref[hp, rows, :] - top[hp])
            ls.append(acc[hp] + _fold8(e, jnp.add))
            eds.append(acc[N_PAIRS + hp] + _fold8(e * dp_ref[hp, rows, :], jnp.add))
        return tuple(ls + eds)

    acc = lax.fori_loop(0, N_STRIPS, sums, (jnp.zeros((8, 128), F32),) * (2 * N_PAIRS), unroll=2)
    inv = [1.0 / jnp.sum(a, axis=0, keepdims=True) for a in acc[:N_PAIRS]]
    delta = [jnp.sum(a, axis=0, keepdims=True) * i for a, i in zip(acc[N_PAIRS:], inv)]
    return top, inv, delta


def _attn_specs(nblk):
    cur = lambda col: pl.BlockSpec((ATT_BLK, ATTN_W), lambda s: (jnp.minimum(s, nblk - 1), col))
    prev = lambda col: pl.BlockSpec(
        (ATT_BLK, ATTN_W), lambda s: (jnp.maximum(jnp.minimum(s, nblk - 1) - 1, 0), col))
    return cur, prev


def _attn_fwd(name, proj, bias, comm=None):
    t = proj.shape[0]
    nblk = t // ATT_BLK
    cur, prev = _attn_specs(nblk)

    def body(q_ref, kp_ref, kc_ref, vp_ref, vc_ref, b_ref, o_ref, kband, vband):
        s = pl.program_id(0)
        _fill_band(kband, kp_ref, kc_ref)
        _fill_band(vband, vp_ref, vc_ref)
        low = _head_masks()

        def chunk(ci, carry):
            r0 = pl.multiple_of(ci * CHUNK, CHUNK)
            for hp in range(N_PAIRS):
                cols = slice(hp * 128, (hp + 1) * 128)
                qcat = _pair_rows(q_ref[pl.ds(r0, CHUNK), cols] * ATTN_SCALE, low)
                p = _band_probs(kband[pl.ds(r0, BAND_PAD), cols], qcat, b_ref[hp],
                                (s * 8 - 8 + ci) * CHUNK)
                o2 = lax.dot_general(p.astype(BF16), vband[pl.ds(r0, BAND_PAD), cols],
                                     (TN, ((), ())), preferred_element_type=F32)
                o_ref[pl.ds(r0, CHUNK), cols] = _pair_diag(o2, low).astype(BF16)
            return carry

        lax.fori_loop(0, 8, chunk, 0)

    out, moved = _pcall(
        body, name, (nblk,),
        [cur(0), prev(1), cur(1), prev(2), cur(2),
         pl.BlockSpec((N_PAIRS, BAND_PAD, 128), lambda s: (0, 0, 0))],
        pl.BlockSpec((ATT_BLK, ATTN_W), lambda s: (s, 0)), _sds((t, ATTN_W), BF16),
        [pltpu.VMEM((BAND_ROWS, ATTN_W), BF16), pltpu.VMEM((BAND_ROWS, ATTN_W), BF16)],
        _cparams("arbitrary"), (proj, proj, proj, proj, proj, bias), comm)
    return out if comm is None else (out, moved)


def _attn_bwd(name, proj, datt, bias, comm=None):
    t = proj.shape[0]
    nblk = t // ATT_BLK
    cur, prev = _attn_specs(nblk)
    late = pl.BlockSpec((ATT_BLK, 3 * ATTN_W), lambda s: (jnp.maximum(s - 1, 0), 0))

    def body(q_ref, kp_ref, kc_ref, vp_ref, vc_ref, do_ref, b_ref,
             dqkv_ref, db_ref, kband, vband, dkacc, dvacc,
             st_ref, dp_ref, pb_ref, dsb_ref, qc_ref, dc_ref, dq_ref, dq_held):
        s = pl.program_id(0)

        @pl.when(s == 0)
        def _():
            dkacc[...] = jnp.zeros_like(dkacc)
            dvacc[...] = jnp.zeros_like(dvacc)
            db_ref[...] = jnp.zeros_like(db_ref)
            dq_ref[...] = jnp.zeros_like(dq_ref)

        @pl.when(s < nblk)
        def _():
            _fill_band(kband, kp_ref, kc_ref)
            _fill_band(vband, vp_ref, vc_ref)
            low = _head_masks()

            def chunk(ci, carry):
                r0 = pl.multiple_of(ci * CHUNK, CHUNK)
                for hp in range(N_PAIRS):
                    cols = slice(hp * 128, (hp + 1) * 128)
                    qc_ref[hp] = _pair_rows(q_ref[pl.ds(r0, CHUNK), cols] * ATTN_SCALE, low)
                    dc_ref[hp] = _pair_rows(do_ref[pl.ds(r0, CHUNK), cols], low)
                    st_ref[hp] = lax.dot_general(kband[pl.ds(r0, BAND_PAD), cols], qc_ref[hp],
                                                 (NT, ((), ())), preferred_element_type=F32)
                    dp_ref[hp] = lax.dot_general(vband[pl.ds(r0, BAND_PAD), cols], dc_ref[hp],
                                                 (NT, ((), ())), preferred_element_type=F32)
                top, inv, delta = _band_softmax_stats(st_ref, b_ref, (s * 8 - 8 + ci) * CHUNK,
                                                      dp_ref)

                def grads(k, c):
                    rows = _strip(k)
                    for hp in range(N_PAIRS):
                        p = jnp.exp(st_ref[hp, rows, :] - top[hp]) * inv[hp]
                        ds = p * (dp_ref[hp, rows, :] - delta[hp])
                        db_ref[hp, rows, :] += ds
                        dsb_ref[hp, rows, :] = ds.astype(BF16)
                        pb_ref[hp, rows, :] = p.astype(BF16)
                    return c

                lax.fori_loop(0, N_STRIPS, grads, 0, unroll=2)
                for hp in range(N_PAIRS):
                    cols = slice(hp * 128, (hp + 1) * 128)
                    dq2 = lax.dot_general(dsb_ref[hp], kband[pl.ds(r0, BAND_PAD), cols],
                                          (TN, ((), ())), preferred_element_type=F32)
                    dq_ref[pl.ds(r0, CHUNK), cols] = (_pair_diag(dq2, low) * ATTN_SCALE).astype(BF16)
                    dkacc[pl.ds(r0, BAND_PAD), cols] += jnp.dot(dsb_ref[hp], qc_ref[hp],
                                                               preferred_element_type=F32)
                    dvacc[pl.ds(r0, BAND_PAD), cols] += jnp.dot(pb_ref[hp], dc_ref[hp],
                                                               preferred_element_type=F32)
                return carry

            dq_held[...] = dq_ref[...]
            lax.fori_loop(0, 8, chunk, 0)

        @pl.when(s == nblk)
        def _():
            dq_held[...] = dq_ref[...]

        dqkv_ref[:, 0:ATTN_W] = dq_held[...]
        dqkv_ref[:, ATTN_W:2 * ATTN_W] = dkacc[0:ATT_BLK, :].astype(BF16)
        dqkv_ref[:, 2 * ATTN_W:] = dvacc[0:ATT_BLK, :].astype(BF16)
        dkacc[0:ATT_BLK, :] = dkacc[ATT_BLK:2 * ATT_BLK, :]
        dvacc[0:ATT_BLK, :] = dvacc[ATT_BLK:2 * ATT_BLK, :]
        dkacc[ATT_BLK:, :] = jnp.zeros((ATT_BLK + CHUNK, ATTN_W), F32)
        dvacc[ATT_BLK:, :] = jnp.zeros((ATT_BLK + CHUNK, ATTN_W), F32)

    outs, moved = _pcall(
        body, name, (nblk + 1,),
        [cur(0), prev(1), cur(1), prev(2), cur(2),
         pl.BlockSpec((ATT_BLK, ATTN_W), lambda s: (jnp.minimum(s, nblk - 1), 0)),
         pl.BlockSpec((HEADS // 2, BAND_PAD, 128), lambda s: (0, 0, 0))],
        [late, pl.BlockSpec((HEADS // 2, BAND_PAD, 128), lambda s: (0, 0, 0))],
        [_sds((t, 3 * ATTN_W), BF16), _sds((HEADS // 2, BAND_PAD, 128), F32)],
        [pltpu.VMEM((BAND_ROWS, ATTN_W), BF16), pltpu.VMEM((BAND_ROWS, ATTN_W), BF16),
         pltpu.VMEM((BAND_ROWS, ATTN_W), F32), pltpu.VMEM((BAND_ROWS, ATTN_W), F32),
         pltpu.VMEM((N_PAIRS, BAND_PAD, 128), F32), pltpu.VMEM((N_PAIRS, BAND_PAD, 128), F32),
         pltpu.VMEM((N_PAIRS, BAND_PAD, 128), BF16), pltpu.VMEM((N_PAIRS, BAND_PAD, 128), BF16),
         pltpu.VMEM((N_PAIRS, 2 * CHUNK, 128), BF16), pltpu.VMEM((N_PAIRS, 2 * CHUNK, 128), BF16),
         pltpu.VMEM((ATT_BLK, ATTN_W), BF16), pltpu.VMEM((ATT_BLK, ATTN_W), BF16)],
        _cparams("arbitrary"), (proj, proj, proj, proj, proj, datt, bias), comm)
    return outs if comm is None else (*outs, moved)


def _diag_onehot(rel_rows):
    d0 = lax.broadcasted_iota(jnp.int32, (BIAS_LANES, BIAS_LANES), 0)
    d1 = lax.broadcasted_iota(jnp.int32, (BIAS_LANES, BIAS_LANES), 1)
    m, n = (d0, d1) if rel_rows else (d1, d0)
    hit = (m == jnp.minimum(BAND - 1 + MAX_REL - n, 2 * MAX_REL)) & (n < BAND + CHUNK - 1)
    return jnp.where(hit, 1.0, 0.0).astype(F32)


def _bias_table(name, rel_bias_l):
    rel_pad = jnp.pad(rel_bias_l, ((0, 0), (0, BIAS_LANES - N_REL)))

    def body(r_ref, o_ref):
        diag = jnp.dot(r_ref[...], _diag_onehot(True), preferred_element_type=F32,
                       precision=lax.Precision.HIGHEST)
        rowid = lax.broadcasted_iota(jnp.int32, (8, BIAS_LANES), 0)
        lane = lax.broadcasted_iota(jnp.int32, (8, BIAS_LANES), 1)
        for h in range(HEADS):
            d8 = jnp.broadcast_to(diag[h:h + 1, :], (8, BIAS_LANES))
            slab0 = pltpu.roll(d8, BIAS_LANES - CHUNK + 1, axis=1)
            for b in range(1, 8):
                slab0 = jnp.where(rowid == b, pltpu.roll(d8, BIAS_LANES - CHUNK + 1 + b, axis=1),
                                  slab0)
            for a in range(8):
                slab = slab0 if a == 0 else pltpu.roll(slab0, 8 * a, axis=1)
                o_ref[h * CHUNK + 8 * a:h * CHUNK + 8 * a + 8, :] = jnp.where(lane < BAND, slab, NEG)

    tab = pl.pallas_call(
        body, name=name,
        in_specs=[pl.BlockSpec(memory_space=pltpu.VMEM)],
        out_specs=pl.BlockSpec(memory_space=pltpu.VMEM),
        out_shape=_sds((HEADS * CHUNK, BIAS_LANES), F32),
    )(rel_pad)
    tab = tab.reshape(HEADS // 2, 2, CHUNK, BIAS_LANES)
    return jnp.transpose(tab, (0, 3, 1, 2)).reshape(HEADS // 2, BIAS_LANES, 2 * CHUNK)


def _bias_fold(name, dbias_t):
    rows = HEADS * CHUNK
    dbias = jnp.transpose(dbias_t.reshape(HEADS // 2, BIAS_LANES, 2, CHUNK), (0, 2, 3, 1))

    def body(d_ref, o_ref):
        rowid = lax.broadcasted_iota(jnp.int32, (8, BIAS_LANES), 0)
        diags = []
        for h in range(HEADS):
            acc = d_ref[h * CHUNK + 56:h * CHUNK + 64, :]
            for a in range(7):
                slab = d_ref[h * CHUNK + 8 * a:h * CHUNK + 8 * a + 8, :]
                acc = acc + pltpu.roll(slab, 56 - 8 * a, axis=1)
            tot = jnp.where(rowid == 7, acc, 0.0)
            for b in range(7):
                tot = tot + jnp.where(rowid == b, pltpu.roll(acc, 7 - b, axis=1), 0.0)
            diags.append(jnp.sum(tot, axis=0, keepdims=True))
        diag = jnp.concatenate(diags, axis=0)
        o_ref[...] = jnp.dot(diag, _diag_onehot(False), preferred_element_type=F32,
                             precision=lax.Precision.HIGHEST)

    return pl.pallas_call(
        body, name=name,
        in_specs=[pl.BlockSpec(memory_space=pltpu.VMEM)],
        out_specs=pl.BlockSpec(memory_space=pltpu.VMEM),
        out_shape=_sds((HEADS, BIAS_LANES), F32),
    )(dbias.reshape(rows, BIAS_LANES))


def _inv_counts(i):
    trow = lax.broadcasted_iota(jnp.int32, (TOK + HALO, 1), 0) + i * TOK
    return [1.0 / jnp.minimum(trow + 1, w).astype(F32) for w in POOL_WINDOWS]


def _pool_fwd(name, proj, wg, scale, comm=None):
    t = proj.shape[0]
    hb = TOK // HALO

    def body(u_ref, up_ref, wg_ref, sc_ref, pooled_ref, mixed_ref, b0, b1, b2, b3):
        i = pl.program_id(0)
        halo = up_ref[...].astype(F32)
        b0[0:HALO, :] = jnp.where(i == 0, jnp.zeros_like(halo), halo)
        b0[HALO:, :] = u_ref[...].astype(F32)
        n = TOK + HALO
        b1[8:n, :] = b0[8:n, :] + b0[7:n - 1, :]
        b2[16:n, 128:] = b1[16:n, 128:] + b1[14:n - 2, 128:]
        b3[24:n, 256:] = b2[24:n, 256:] + b2[20:n - 4, 256:]
        wins = [b1[HALO:n, 0:128], b2[HALO:n, 128:256], b3[HALO:n, 256:384],
                b3[HALO:n, 384:512] + b3[HALO - 8:n - 8, 384:512]]
        inv = _inv_counts(i)
        for g in range(4):
            cols = slice(g * POOL_GD, (g + 1) * POOL_GD)
            pooled = (wins[g] * inv[g][0:TOK] - b0[HALO:n, cols]).astype(BF16)
            pooled_ref[:, cols] = pooled
            pre = jnp.dot(pooled, wg_ref[g], preferred_element_type=F32)
            mixed_ref[:, cols] = (pre * sc_ref[:, cols]).astype(BF16)

    buf = pltpu.VMEM((TOK + HALO, POOL_W), F32)
    outs, moved = _pcall(
        body, name, (t // TOK,),
        [_row_spec(POOL_W, 3),
         pl.BlockSpec((HALO, POOL_W), lambda i: (jnp.maximum(i * hb - 1, 0), 3)),
         pl.BlockSpec((4, POOL_GD, POOL_GD), lambda i: (0, 0, 0)), _vec_spec(POOL_W)],
        [_row_spec(POOL_W), _row_spec(POOL_W)],
        [_sds((t, POOL_W), BF16), _sds((t, POOL_W), BF16)], [buf, buf, buf, buf],
        _cparams("arbitrary"), (proj, proj, wg, scale), comm)
    return outs if comm is None else (*outs, moved)


def _pool_bwd(name, dmixed, pooled, wg, scale, comm=None):
    t = dmixed.shape[0]
    nt = t // TOK
    hb = TOK // HALO

    def body(dm_ref, dmn_ref, p_ref, wg_ref, sc_ref, du_ref, dwg_ref, dsc_ref, c0, c1, c2, c3):
        i = pl.program_id(0)

        @pl.when(i == 0)
        def _():
            dwg_ref[...] = jnp.zeros_like(dwg_ref)
            dsc_ref[...] = jnp.zeros_like(dsc_ref)

        n = TOK + HALO
        inv = _inv_counts(i)
        dmv = dm_ref[...].astype(F32)
        dmn = dmn_ref[...].astype(F32)
        dmn = jnp.where(i == nt - 1, jnp.zeros_like(dmn), dmn)
        for g in range(4):
            cols = slice(g * POOL_GD, (g + 1) * POOL_GD)
            scg = sc_ref[:, cols]
            pg = p_ref[:, cols]
            dpre = (dmv[:, cols] * scg).astype(BF16)
            dpre_n = (dmn[:, cols] * scg).astype(BF16)
            pre = jnp.dot(pg, wg_ref[g], preferred_element_type=F32)
            dsc_ref[:, cols] += jnp.sum(dmv[:, cols] * pre, axis=0, keepdims=True)
            dwg_ref[g] += lax.dot_general(pg, dpre, (TN, ((), ())), preferred_element_type=F32)
            dpool = lax.dot_general(dpre, wg_ref[g], (NT, ((), ())), preferred_element_type=F32)
            dpool_n = lax.dot_general(dpre_n, wg_ref[g], (NT, ((), ())),
                                      preferred_element_type=F32)
            c0[0:TOK, cols] = dpool
            c0[TOK:n, cols] = dpool_n
            c1[0:TOK, cols] = dpool * inv[g][0:TOK]
            c1[TOK:n, cols] = dpool_n * inv[g][TOK:n]
        c2[0:n - 8, :] = c1[0:n - 8, :] + c1[1:n - 7, :]
        c3[0:n - 16, 128:] = c2[0:n - 16, 128:] + c2[2:n - 14, 128:]
        c1[0:n - 24, 256:] = c3[0:n - 24, 256:] + c3[4:n - 20, 256:]
        wins = [c2[0:TOK, 0:128], c3[0:TOK, 128:256], c1[0:TOK, 256:384],
                c1[0:TOK, 384:512] + c1[8:TOK + 8, 384:512]]
        for g in range(4):
            cols = slice(g * POOL_GD, (g + 1) * POOL_GD)
            du_ref[:, cols] = (wins[g] - c0[0:TOK, cols]).astype(BF16)

    buf = pltpu.VMEM((TOK + HALO, POOL_W), F32)
    outs, moved = _pcall(
        body, name, (nt,),
        [_row_spec(POOL_W),
         pl.BlockSpec((HALO, POOL_W), lambda i: (jnp.minimum((i + 1) * hb, nt * hb - 1), 0)),
         _row_spec(POOL_W), pl.BlockSpec((4, POOL_GD, POOL_GD), lambda i: (0, 0, 0)),
         _vec_spec(POOL_W)],
        [_row_spec(POOL_W), pl.BlockSpec((4, POOL_GD, POOL_GD), lambda i: (0, 0, 0)),
         _vec_spec(POOL_W)],
        [_sds((t, POOL_W), BF16), _sds((4, POOL_GD, POOL_GD), F32), _sds((1, POOL_W), F32)],
        [buf, buf, buf, buf], _cparams("arbitrary"), (dmixed, dmixed, pooled, wg, scale), comm)
    return outs if comm is None else (*outs, moved)


GELU_C = math.sqrt(2.0 / math.pi)


GELU_K = 0.044715


def _gelu_parts(x):
    x2 = x * x
    s = 0.5 + 0.5 * jnp.tanh(x * (GELU_C + (GELU_C * GELU_K) * x2))
    return x * s, s, x2


def _gelu(x):
    return _gelu_parts(x)[0]


def _gelu_and_grad(x):
    g, s, x2 = _gelu_parts(x)
    return g, s + g * (1.0 - s) * ((2 * GELU_C) + (6 * GELU_C * GELU_K) * x2)


def _taps(buf, r, rows):
    a = buf[pl.ds(r, rows + 8), :]
    return a[8:], pltpu.roll(a, 1, axis=0)[8:], pltpu.roll(a, 2, axis=0)[8:]


def _conv(taps, w_ref, b_ref):
    return b_ref[...] + w_ref[2:3, :] * taps[0] + w_ref[1:2, :] * taps[1] + w_ref[0:1, :] * taps[2]


def _stage(dst, prev_ref, cur_ref, next_ref, first, last):
    rows = cur_ref.shape[0]
    h = prev_ref[...].astype(F32)
    dst[0:8, :] = jnp.where(first, jnp.zeros_like(h), h)
    dst[8:8 + rows, :] = cur_ref[...].astype(F32)
    if next_ref is not None:
        h = next_ref[...].astype(F32)
        dst[8 + rows:, :] = jnp.where(last, jnp.zeros_like(h), h)


FWD_STRIP = 32
BWD_STRIP = 16


def _ffn_gate_fwd(name, hu, conv_w, conv_b, comm=None):
    t = hu.shape[0]
    ncol = D_FF // FF_COL
    hb = FF_TOK // 8

    def tile(off):
        return pl.BlockSpec((FF_TOK, FF_COL), lambda i, j: (i, j + off))

    def halo(off):
        return pl.BlockSpec((8, FF_COL), lambda i, j: (jnp.maximum(i * hb - 1, 0), j + off))

    def wspec(off):
        return pl.BlockSpec((3, FF_COL), lambda i, j: (0, j + off))

    def bspec(off):
        return pl.BlockSpec((1, FF_COL), lambda i, j: (0, j + off))

    def body(v_ref, vp_ref, g_ref, gp_ref, wv_ref, wg_ref, bv_ref, bg_ref, a_ref, hc_ref, vb, gb):
        first = pl.program_id(0) == 0
        _stage(vb, vp_ref, v_ref, None, first, None)
        _stage(gb, gp_ref, g_ref, None, first, None)

        def strip(k, carry):
            for u in range(2):
                r = pl.multiple_of((2 * k + u) * FWD_STRIP, FWD_STRIP)
                val = _conv(_taps(vb, r, FWD_STRIP), wv_ref, bv_ref)
                gate = _conv(_taps(gb, r, FWD_STRIP), wg_ref, bg_ref)
                a_ref[pl.ds(r, FWD_STRIP), :] = (_gelu(gate) * val).astype(BF16)
                hc_ref[0, pl.ds(r, FWD_STRIP), :] = val.astype(BF16)
                hc_ref[1, pl.ds(r, FWD_STRIP), :] = gate.astype(BF16)
            return carry

        lax.fori_loop(0, FF_TOK // (2 * FWD_STRIP), strip, 0)

    buf = pltpu.VMEM((FF_TOK + 8, FF_COL), F32)
    outs, moved = _pcall(
        body, name, (t // FF_TOK, ncol),
        [tile(0), halo(0), tile(ncol), halo(ncol), wspec(0), wspec(ncol), bspec(0), bspec(ncol)],
        [pl.BlockSpec((FF_TOK, FF_COL), lambda i, j: (i, j)),
         pl.BlockSpec((2, FF_TOK, FF_COL), lambda i, j: (0, i, j))],
        [_sds((t, D_FF), BF16), _sds((2, t, D_FF), BF16)], [buf, buf],
        _cparams("arbitrary", "arbitrary"),
        (hu, hu, hu, hu, conv_w, conv_w, conv_b, conv_b), comm)
    return outs if comm is None else (*outs, moved)


def _ffn_gate_bwd(name, da, hu, hc, conv_w, comm=None):
    t = hu.shape[0]
    nt = t // FF_TOK
    ncol = D_FF // FF_COL
    hb = FF_TOK // 8

    def tile(off):
        return pl.BlockSpec((FF_TOK, FF_COL), lambda j, i: (i, j + off))

    def nxt_rows(i):
        return jnp.minimum((i + 1) * hb, nt * hb - 1)

    def wspec(off):
        return pl.BlockSpec((3, FF_COL), lambda j, i: (0, j + off))

    def body(da_ref, dan_ref, v_ref, g_ref, hc_ref, hcn_ref, wv_ref, wg_ref,
             dh_ref, dwv_ref, dwg_ref):
        i = pl.program_id(1)
        first, last = i == 0, i == nt - 1

        @pl.when(first)
        def _():
            dwv_ref[...] = jnp.zeros_like(dwv_ref)
            dwg_ref[...] = jnp.zeros_like(dwg_ref)

        def grads(dav, val, gate):
            g, dg = _gelu_and_grad(gate.astype(F32))
            dav = dav.astype(F32)
            return dav * g, dav * val.astype(F32) * dg

        def fold(x):
            return x[0:8] + x[8:16]

        def strip(j, carry):
            for u in range(2):
                carry = one_strip(2 * j + u, carry)
            return carry

        def one_strip(k, carry):
            r = pl.multiple_of(FF_TOK - BWD_STRIP - k * BWD_STRIP, BWD_STRIP)
            rows = pl.ds(r, BWD_STRIP)
            dval, dgate = grads(da_ref[rows, :], hc_ref[0, rows, :], hc_ref[1, rows, :])
            new = (dval[0:8], dgate[0:8])
            for half, (d, below, h_ref, w_ref, dw_ref) in enumerate((
                    (dval, carry[0], v_ref, wv_ref, dwv_ref),
                    (dgate, carry[1], g_ref, wg_ref, dwg_ref))):
                e = jnp.concatenate([d, below], axis=0)
                e1 = pltpu.roll(e, BWD_STRIP + 7, axis=0)[0:BWD_STRIP]
                e2 = pltpu.roll(e, BWD_STRIP + 6, axis=0)[0:BWD_STRIP]
                dh = w_ref[2:3, :] * d + w_ref[1:2, :] * e1 + w_ref[0:1, :] * e2
                dh_ref[half, rows, :] = dh.astype(BF16)
                huv = h_ref[rows, :].astype(F32)
                dw_ref[0:8, :] += fold(e2 * huv)
                dw_ref[8:16, :] += fold(e1 * huv)
                dw_ref[16:24, :] += fold(d * huv)
                dw_ref[24:32, :] += fold(d)
            return new

        dan = dan_ref[...]
        dan = jnp.where(last, jnp.zeros_like(dan), dan)
        lax.fori_loop(0, FF_TOK // (2 * BWD_STRIP), strip, grads(dan, hcn_ref[0], hcn_ref[1]))

        @pl.when(last)
        def _():
            for dw_ref in (dwv_ref, dwg_ref):
                for q in range(4):
                    dw_ref[8 * q:8 * q + 1, :] = jnp.sum(dw_ref[8 * q:8 * q + 8, :], axis=0,
                                                         keepdims=True)

    acc = pl.BlockSpec((32, FF_COL), lambda j, i: (0, j))
    (dhu, dwv, dwg), moved = _pcall(
        body, name, (ncol, nt),
        [tile(0), pl.BlockSpec((8, FF_COL), lambda j, i: (nxt_rows(i), j)),
         tile(0), tile(ncol),
         pl.BlockSpec((2, FF_TOK, FF_COL), lambda j, i: (0, i, j)),
         pl.BlockSpec((2, 8, FF_COL), lambda j, i: (0, nxt_rows(i), j)),
         wspec(0), wspec(ncol)],
        [pl.BlockSpec((2, FF_TOK, FF_COL), lambda j, i: (0, i, j)), acc, acc],
        [_sds((2, t, D_FF), BF16), _sds((32, D_FF), F32), _sds((32, D_FF), F32)],
        [], _cparams("arbitrary", "arbitrary"),
        (da, da, hu, hu, hc, hc, conv_w, conv_w), comm)
    dconv = jnp.concatenate([dwv, dwg], axis=1).reshape(4, 8, 2 * D_FF)[:, 0]
    return (dhu, dconv) if comm is None else (dhu, dconv, moved)


def _mesh_pos():
    x, y, c = lax.axis_index("x"), lax.axis_index("y"), lax.axis_index("c")
    return x, y, c, [(1 - x, y), (x, 1 - y), (1 - x, 1 - y)]


def _remote(src, dst, send_sems, recv_sems, i, dev):
    return pltpu.make_async_remote_copy(src_ref=src, dst_ref=dst, send_sem=send_sems.at[i],
                                        recv_sem=recv_sems.at[i], device_id=dev,
                                        device_id_type=MESH)


def _mine(c, rows):
    return pl.ds(pl.multiple_of(c * (rows // 2), 16), rows // 2)


def _gather_send(shards, conv_shard, gathered, l):
    nbig = len(shards)
    with_conv = conv_shard is not None
    if gathered is None:
        ins = list(shards) + ([conv_shard] if with_conv else [])
        outs = [_sds((DEPTH, N_CHIPS) + s.shape[1:], s.dtype) for s in ins]
        alias = {}
    else:
        ins = list(shards) + list(gathered)
        outs = [_sds(g.shape, g.dtype) for g in gathered]
        alias = {nbig + k: k for k in range(nbig)}

    def copies(cin, cout, ssem, rsem):
        x, y, c, chips = _mesh_pos()
        me = 2 * x + y
        out = []
        for k in range(nbig):
            rows = shards[k].shape[1]
            for j, (cx, cy) in enumerate(chips):
                out.append(_remote(cin[k].at[l, _mine(c, rows)], cout[k].at[l, me, _mine(c, rows)],
                                   ssem, rsem, 4 * k + j, (cx, cy, c)))
            out.append(_remote(cin[k].at[l], cout[k].at[l, me], ssem, rsem, 4 * k + 3,
                               (x, y, 1 - c)))
        if with_conv:
            base = 4 * nbig
            for j, (cx, cy) in enumerate(chips):
                out.append(_remote(cin[nbig].at[c], cout[nbig].at[c, me], ssem, rsem, base + j,
                                   (cx, cy, c)))
            for ll in range(DEPTH):
                out.append(_remote(cin[nbig].at[ll], cout[nbig].at[ll, me], ssem, rsem,
                                   base + 3 + ll, (x, y, 1 - c)))
        return out

    return _Comm(ins, outs, copies, 4 * nbig + 5, alias)


def _gather_forward(gathered, nbig, rows, l):
    with_conv = len(gathered) > nbig
    alias = {k: k for k in range(len(gathered))}

    def copies(cin, cout, ssem, rsem):
        x, y, c, chips = _mesh_pos()
        out = []
        for k in range(nbig):
            for j, (cx, cy) in enumerate(chips):
                blk = cout[k].at[l, 2 * cx + cy, _mine(c, rows[k])]
                out.append(_remote(blk, blk, ssem, rsem, 3 * k + j, (x, y, 1 - c)))
        if with_conv:
            for j, (cx, cy) in enumerate(chips):
                blk = cout[nbig].at[c, 2 * cx + cy]
                out.append(_remote(blk, blk, ssem, rsem, 3 * nbig + j, (x, y, 1 - c)))
        return out

    return _Comm(gathered, [_sds(g.shape, g.dtype) for g in gathered], copies, 3 * nbig + 3, alias)


def _reduce_swap(grads, l):
    def copies(cin, cout, ssem, rsem):
        x, y, c, _ = _mesh_pos()
        return [_remote(cin[k].at[l, :, _mine(1 - c, g.shape[2])], cout[k], ssem, rsem, k,
                        (x, y, 1 - c)) for k, g in enumerate(grads)]

    outs = [_sds((N_CHIPS, g.shape[2] // 2, g.shape[3]), g.dtype) for g in grads]
    return _Comm(grads, outs, copies, len(grads))


def _reduce_scatter(sums):
    def copies(cin, cout, ssem, rsem):
        x, y, c, chips = _mesh_pos()
        return [_remote(cin[k].at[2 * cx + cy], cout[k].at[j], ssem, rsem, 3 * k + j, (cx, cy, c))
                for k in range(len(sums)) for j, (cx, cy) in enumerate(chips)]

    outs = [_sds((3,) + s.shape[1:], s.dtype) for s in sums]
    return _Comm(sums, outs, copies, 3 * len(sums))


def _reduce_share(reds, l):
    def copies(cin, cout, ssem, rsem):
        x, y, c, _ = _mesh_pos()
        out = []
        for k, r in enumerate(reds):
            half = cout[k].at[l, _mine(c, r.shape[1])]
            out.append(_remote(half, half, ssem, rsem, k, (x, y, 1 - c)))
        return out

    return _Comm(reds, [_sds(r.shape, r.dtype) for r in reds], copies, len(reds),
                 {k: k for k in range(len(reds))})


def _allreduce_small(per_layer):
    kinds = len(per_layer[0])
    shapes = [a.shape[1:] if a.shape[0] == 1 else a.shape for a in per_layer[0]]

    def body(*refs):
        ins = refs[:DEPTH * kinds]
        outs = refs[DEPTH * kinds:(DEPTH + 1) * kinds]
        gbufs = refs[(DEPTH + 1) * kinds:(DEPTH + 2) * kinds]
        send_sems, recv_sems = refs[-2], refs[-1]
        x, y, c, chips = _mesh_pos()
        sibling = (x, y, 1 - c)

        def copy(k, i, block, to):
            px, py, pc = block
            slot = gbufs[k].at[4 * px + 2 * py + pc]
            return _remote(slot, slot, send_sems, recv_sems, 7 * k + i, to)

        me = (x, y, c)
        first, passed = [], []
        for k in range(kinds):
            for l in range(DEPTH):
                a = ins[l * kinds + k]
                if per_layer[l][k].shape[0] == 1:
                    gbufs[k][4 * x + 2 * y + c, l:l + 1] = a[...]
                else:
                    gbufs[k][4 * x + 2 * y + c, l] = a[...]
            first.append(copy(k, 0, me, sibling))
            first += [copy(k, 1 + j, me, (*chip, c)) for j, chip in enumerate(chips)]
            passed += [copy(k, 4 + j, (*chip, c), sibling) for j, chip in enumerate(chips)]
        for cp in first:
            cp.start()
        for k in range(kinds):
            for j, chip in enumerate(chips):
                copy(k, 1 + j, (*chip, c), me).wait_recv()
                passed[3 * k + j].start()
        for k in range(kinds):
            copy(k, 0, sibling, me).wait_recv()
            for j, chip in enumerate(chips):
                copy(k, 4 + j, (*chip, 1 - c), me).wait_recv()
        for cp in first + passed:
            cp.wait_send()
        for k in range(kinds):
            acc = gbufs[k][0]
            for d in range(1, 8):
                acc = acc + gbufs[k][d]
            outs[k][...] = acc

    vmem = pl.BlockSpec(memory_space=pltpu.VMEM)
    return pl.pallas_call(
        body, name="allreduce_small",
        in_specs=[vmem] * (DEPTH * kinds), out_specs=[vmem] * kinds,
        out_shape=[_sds((DEPTH,) + s, F32) for s in shapes],
        scratch_shapes=[pltpu.VMEM((8, DEPTH) + s, F32) for s in shapes]
        + [pltpu.SemaphoreType.DMA((7 * kinds,)), pltpu.SemaphoreType.DMA((7 * kinds,))],
        compiler_params=pltpu.CompilerParams(vmem_limit_bytes=VMEM_LIMIT_V7X),
    )(*per_layer[0], *per_layer[1])


def _adamw_small(ws, gs, ms, vs):
    n = len(ws)
    c1 = 1.0 - ADAM_B1 ** ADAM_STEP
    c2 = 1.0 - ADAM_B2 ** ADAM_STEP

    def body(*refs):
        for i in range(n):
            w_ref, g_ref, m_ref, v_ref = (refs[j * n + i] for j in range(4))
            d_ref, nm_ref, nv_ref = (refs[(4 + j) * n + i] for j in range(3))
            gv = g_ref[...]
            nm = ADAM_B1 * m_ref[...] + (1.0 - ADAM_B1) * gv
            nv = ADAM_B2 * v_ref[...] + (1.0 - ADAM_B2) * (gv * gv)
            nm_ref[...] = nm
            nv_ref[...] = nv
            d_ref[...] = -ADAM_LR * ((nm / c1) / (jnp.sqrt(nv / c2) + ADAM_EPS)
                                     + ADAM_WD * w_ref[...])

    vmem = pl.BlockSpec(memory_space=pltpu.VMEM)
    outs = pl.pallas_call(
        body, name="adamw_small", in_specs=[vmem] * (4 * n), out_specs=[vmem] * (3 * n),
        out_shape=[_sds(w.shape, F32) for w in ws] * 3,
        compiler_params=pltpu.CompilerParams(vmem_limit_bytes=VMEM_LIMIT_V7X),
    )(*ws, *gs, *ms, *vs)
    return outs[:n], outs[n:2 * n], outs[2 * n:]


def _core_index():
    return jnp.reshape(lax.axis_index("c"), (1,)).astype(jnp.int32)


def _chip_index():
    return jnp.reshape(2 * lax.axis_index("x") + lax.axis_index("y"), (1,)).astype(jnp.int32)


def _chip_sums(name, stacked, sibs, l):
    n = len(stacked)
    dims = [(s.shape[2] // 2, s.shape[3]) for s in stacked]

    def body(c_ref, *refs):
        for k in range(n):
            a_ref, b_ref, o_ref = refs[k], refs[n + k], refs[2 * n + k]
            o_ref[...] = (a_ref[...].astype(F32) + b_ref[...].astype(F32)).astype(BF16)

    return pl.pallas_call(
        body, name=name,
        grid_spec=pltpu.PrefetchScalarGridSpec(
            num_scalar_prefetch=1, grid=(N_CHIPS,),
            in_specs=[pl.BlockSpec((None, None, hr, cd), lambda j, cr: (l, j, cr[0], 0))
                      for hr, cd in dims]
            + [pl.BlockSpec((None, hr, cd), lambda j, cr: (j, 0, 0)) for hr, cd in dims],
            out_specs=[pl.BlockSpec((None, hr, cd), lambda j, cr: (j, 0, 0)) for hr, cd in dims]),
        out_shape=[_sds((N_CHIPS, hr, cd), BF16) for hr, cd in dims],
        compiler_params=_cparams("parallel"))(_core_index(), *stacked, *sibs)


def _final_sums(name, sums, recvs, l, fills):
    n = len(sums)
    dims = [(s.shape[1] // 2, s.shape[2]) for s in sums]
    filled = fills[0] is not None

    def body(m_ref, *refs):
        outs = refs[-n:]
        for k in range(n):
            acc = refs[k][...].astype(F32)
            for j in range(3):
                acc = acc + refs[n + k][j].astype(F32)
            outs[k][...] = acc

    in_specs = ([pl.BlockSpec((None, tr, cd), lambda i, mr: (mr[0], i, 0)) for tr, cd in dims]
                + [pl.BlockSpec((3, tr, cd), lambda i, mr: (0, i, 0)) for tr, cd in dims])
    args = [jnp.concatenate([_chip_index(), _core_index()]), *sums, *recvs]
    aliases = {}
    if filled:
        in_specs += [pl.BlockSpec(memory_space=pl.ANY)] * n
        args += list(fills)
        aliases = {1 + 2 * n + k: k for k in range(n)}
    return pl.pallas_call(
        body, name=name,
        grid_spec=pltpu.PrefetchScalarGridSpec(
            num_scalar_prefetch=1, grid=(2,), in_specs=in_specs,
            out_specs=[pl.BlockSpec((None, tr, cd), lambda i, mr: (l, 2 * mr[1] + i, 0))
                       for tr, cd in dims]),
        out_shape=[_sds((DEPTH, 4 * tr, cd), F32) for tr, cd in dims],
        input_output_aliases=aliases,
        compiler_params=_cparams("parallel"))(*args)


def _adamw(name, w, g, m, v, comm=None):
    nl, r, cdim = w.shape
    tr = r // 4 if r % 32 == 0 else r
    c1 = 1.0 - ADAM_B1 ** ADAM_STEP
    c2 = 1.0 - ADAM_B2 ** ADAM_STEP

    def body(w_ref, g_ref, m_ref, v_ref, d_ref, nm_ref, nv_ref):
        gv = g_ref[...]
        nm = ADAM_B1 * m_ref[...] + (1.0 - ADAM_B1) * gv
        nv = ADAM_B2 * v_ref[...] + (1.0 - ADAM_B2) * (gv * gv)
        nm_ref[...] = nm
        nv_ref[...] = nv
        d_ref[...] = -ADAM_LR * ((nm / c1) / (jnp.sqrt(nv / c2) + ADAM_EPS) + ADAM_WD * w_ref[...])

    spec = pl.BlockSpec((None, tr, cdim), lambda l, i: (l, i, 0))
    out = _sds(w.shape, F32)
    outs, moved = _pcall(body, name, (nl, r // tr), [spec] * 4, [spec] * 3, [out] * 3, [],
                         _cparams("arbitrary", "arbitrary"), (w, g, m, v), comm)
    return outs if comm is None else (*outs, moved)


def kernel(x, norm_mix_pre, w_in, b_gate, rel_bias, w_attn_out, w_pool_group, pool_scale, w_pool_out, w_o, norm_mix_post, norm_ffn_pre, w_up, conv_w, conv_b, w_down, norm_ffn_post, loss_target, m_norm_mix_pre, m_w_in, m_b_gate, m_rel_bias, m_w_attn_out, m_w_pool_group, m_pool_scale, m_w_pool_out, m_w_o, m_norm_mix_post, m_norm_ffn_pre, m_w_up, m_conv_w, m_conv_b, m_w_down, m_norm_ffn_post, v_norm_mix_pre, v_w_in, v_b_gate, v_rel_bias, v_w_attn_out, v_w_pool_group, v_pool_scale, v_w_pool_out, v_w_o, v_norm_mix_post, v_norm_ffn_pre, v_w_up, v_conv_w, v_conv_b, v_w_down, v_norm_ffn_post):
    t = x.shape[1]
    xs = x.reshape(t, D_MODEL)
    target = loss_target.reshape(t, D_MODEL)

    names = ["w_in", "w_attn_out", "w_pool_out", "w_o", "w_up", "w_down"]
    shards = [w.astype(BF16) for w in (w_in, w_attn_out, w_pool_out, w_o, w_up, w_down)]
    rows = [s.shape[1] for s in shards]
    nbig = len(shards)
    h, g = _norm_fwd("l0_norm_mix_pre", x.reshape(t, D_MODEL), norm_mix_pre[0:1],
                     _gather_send(shards[:1], conv_w, None, 0))
    g = _comm_call("gather0_forward", _gather_forward(g, 1, rows[:1], 0))
    cw_full = jnp.transpose(g[1], (0, 2, 1, 3)).reshape(DEPTH, 3, 2 * D_FF)
    g = g[:1]
    wg_bf = w_pool_group.astype(BF16)

    def views(gathered):
        win_g, wao_g, wpo_g, wo_g, wup_g, wdn_g = gathered
        return (win_g, wao_g, wpo_g, wo_g.reshape(DEPTH, D_MODEL, D_MODEL), wup_g,
                wdn_g.reshape(DEPTH, D_FF, D_MODEL))

    saved = []
    xcur = xs
    for l in range(DEPTH):
        tag = f"l{l}_"
        bias = _bias_table(tag + "bias_table", rel_bias[l])
        proj = _mm_nn_blocked(tag + "proj", h, g[0], l, BF16)
        if l == 0:
            att, rest = _attn_fwd(tag + "attn_fwd", proj, bias,
                                  _gather_send(shards[1:], None, None, 0))
            pooled, mixed, rest = _pool_fwd(tag + "pool_fwd", proj, wg_bf[l], pool_scale[l:l + 1],
                                            _gather_forward(rest, nbig - 1, rows[1:], 0))
            g = g + rest
        else:
            att = _attn_fwd(tag + "attn_fwd", proj, bias)
            pooled, mixed = _pool_fwd(tag + "pool_fwd", proj, wg_bf[l], pool_scale[l:l + 1])
        win_g, wao_g, wpo_g, wo_full, wup_g, wdn_full = views(g)
        ya = _narrow_nn(tag + "attn_out", att, wao_g, l)
        yb = _narrow_nn(tag + "pool_out", mixed, wpo_g, l)
        z = _gate_fwd(tag + "gate_fwd", proj, b_gate[l:l + 1], ya, yb)
        mix = _mm_nn(tag + "mix", z, wo_full, l, D_MODEL, F32)
        x1, h2 = _post_pre_fwd(tag + "norm_mix_post", xcur, mix, norm_mix_post[l:l + 1],
                               norm_ffn_pre[l:l + 1])
        if l == 0:
            hu, mixing = _mm_nn_blocked(tag + "ffn_up", h2, wup_g, l, BF16,
                                        _gather_send(shards[:4], None, g[:4], 1))
            a, hc, ffn_g = _ffn_gate_fwd(tag + "ffn_gate_fwd", hu, cw_full[l], conv_b[l:l + 1],
                                         _gather_send(shards[4:], None, g[4:], 1))
            g = mixing + ffn_g
            wdn_full = views(g)[5]
        else:
            hu = _mm_nn_blocked(tag + "ffn_up", h2, wup_g, l, BF16)
            a, hc = _ffn_gate_fwd(tag + "ffn_gate_fwd", hu, cw_full[l], conv_b[l:l + 1])
        f = _mm_nn(tag + "ffn_down", a, wdn_full, l, D_FF // 2, F32)
        saved.append(dict(x=xcur, h=h, proj=proj, att=att, pooled=pooled, mixed=mixed, ya=ya,
                          yb=yb, z=z, mix=mix, x1=x1, h2=h2, hu=hu, hc=hc, a=a, f=f, bias=bias))
        if l == 0:
            xcur, h, g = _post_pre_fwd(tag + "norm_ffn_post", x1, f, norm_ffn_post[l:l + 1],
                                       norm_mix_pre[l + 1:l + 2], _gather_forward(g, nbig, rows, 1))
        elif l < DEPTH - 1:
            xcur, h = _post_pre_fwd(tag + "norm_ffn_post", x1, f, norm_ffn_post[l:l + 1],
                                    norm_mix_pre[l + 1:l + 2])
    win_g, wao_g, wpo_g, wo_full, wup_g, wdn_full = views(g)

    dy, df, d_nfpost, loss_local = _tail("tail", saved[-1]["x1"], saved[-1]["f"],
                                         norm_ffn_post[DEPTH - 1:DEPTH], target)
    loss = lax.psum(loss_local, ("x", "y", "c"))

    dx = dy
    dws = dict.fromkeys(names)
    reds = [None] * nbig
    small_grads = [None] * DEPTH
    ffn = [4, 5]
    outs3 = [1, 2, 3]

    def blocks(ks):
        return [dws[names[k]].reshape(DEPTH, N_CHIPS, rows[k], -1) for k in ks]

    def chip_sums(ks, sib, l):
        return _chip_sums(f"chip_sums{l}_" + names[ks[0]], blocks(ks), sib, l)

    def final_sums(ks, sums, recv, l):
        outs = _final_sums(f"final_sums{l}_" + names[ks[0]], sums, recv, l, [reds[k] for k in ks])
        for k, r in zip(ks, outs):
            reds[k] = r

    for l in reversed(range(DEPTH)):
        tag = f"l{l}_"
        sv = saved[l]
        every = list(range(nbig))
        if l == 0:
            da, sib = _mm_nt(tag + "ffn_down_dx", df, wdn_full, l, D_FF // 2, BF16,
                             _reduce_swap(blocks(every), 1))
            sums = chip_sums(every, sib, 1)
        else:
            da = _mm_nt(tag + "ffn_down_dx", df, wdn_full, l, D_FF // 2, BF16)
        dws["w_down"] = _mm_tn(tag + "ffn_down_dw", sv["a"], df, D_FF // 2, l, dws["w_down"])
        if l == 0:
            dhu, dconv, recv = _ffn_gate_bwd(tag + "ffn_gate_bwd", da, sv["hu"], sv["hc"],
                                             cw_full[l], _reduce_scatter(sums))
            final_sums(every, sums, recv, 1)
            dh2, reds = _mm_nt_blocked(tag + "ffn_up_dx", dhu, wup_g, l, F32,
                                       _reduce_share(reds, 1))
        else:
            dhu, dconv = _ffn_gate_bwd(tag + "ffn_gate_bwd", da, sv["hu"], sv["hc"], cw_full[l])
            dh2 = _mm_nt_blocked(tag + "ffn_up_dx", dhu, wup_g, l, F32)
        dws["w_up"] = _mm_tn_blocked(tag + "ffn_up_dw", sv["h2"], dhu, l, dws["w_up"])
        if l == 0:
            dx1, d_nfpre, dmix, d_nmpost, sib = _pre_post_bwd(
                tag + "norm_ffn_pre_bwd", dh2, sv["x1"], dx, norm_ffn_pre[l:l + 1], sv["mix"],
                norm_mix_post[l:l + 1], _reduce_swap(blocks(ffn), 0))
            sums = chip_sums(ffn, sib, 0)
        else:
            dx1, d_nfpre, dmix, d_nmpost = _pre_post_bwd(
                tag + "norm_ffn_pre_bwd", dh2, sv["x1"], dx, norm_ffn_pre[l:l + 1], sv["mix"],
                norm_mix_post[l:l + 1])
        dz = _mm_nt(tag + "mix_dx", dmix, wo_full, l, D_MODEL, BF16)
        dws["w_o"] = _mm_tn(tag + "mix_dw", sv["z"], dmix, D_MODEL, l, dws["w_o"])
        dya, dyb, dgates, d_bgate = _gate_bwd(tag + "gate_bwd", dz, sv["proj"], b_gate[l:l + 1],
                                              sv["ya"], sv["yb"])
        datt = _narrow_nt(tag + "attn_out_dx", dya, wao_g, l)
        dws["w_attn_out"] = _narrow_tn(tag + "attn_out_dw", sv["att"], dya, l, dws["w_attn_out"])
        dmixed = _narrow_nt(tag + "pool_out_dx", dyb, wpo_g, l)
        dws["w_pool_out"] = _narrow_tn(tag + "pool_out_dw", sv["mixed"], dyb, l, dws["w_pool_out"])
        if l == 0:
            du, d_wg, d_pscale, sib = _pool_bwd(tag + "pool_bwd", dmixed, sv["pooled"], wg_bf[l],
                                                pool_scale[l:l + 1], _reduce_swap(blocks(outs3), 0))
            sums3 = chip_sums(outs3, sib, 0)
            dqkv, dbias, recv = _attn_bwd(
                tag + "attn_bwd", sv["proj"], datt, sv["bias"],
                _both(_reduce_scatter(sums), _reduce_scatter(sums3)))
            final_sums(ffn, sums, recv[:len(ffn)], 0)
            final_sums(outs3, sums3, recv[len(ffn):], 0)
        else:
            du, d_wg, d_pscale = _pool_bwd(tag + "pool_bwd", dmixed, sv["pooled"], wg_bf[l],
                                           pool_scale[l:l + 1])
            dqkv, dbias = _attn_bwd(tag + "attn_bwd", sv["proj"], datt, sv["bias"])
        d_rel = _bias_fold(tag + "bias_fold", dbias)
        if l == 0:
            dh, shared = _proj_dx(tag + "proj_dx", dqkv, du, dgates, win_g, l,
                                  _reduce_share([reds[k] for k in ffn + outs3], 0))
            for k, r in zip(ffn + outs3, shared):
                reds[k] = r
        else:
            dh = _proj_dx(tag + "proj_dx", dqkv, du, dgates, win_g, l)
        dws["w_in"] = _proj_dw(tag + "proj_dw", sv["h"], dqkv, du, dgates, l, dws["w_in"])
        small_grads[l] = [None, d_nmpost, d_nfpre, d_nfpost, d_bgate, d_rel, d_wg, d_pscale, dconv]
        if l > 0:
            dx, small_grads[l][0], df, d_nfpost = _pre_post_bwd(
                tag + "norm_mix_pre_bwd", dh, sv["x"], dx1, norm_mix_pre[l:l + 1],
                saved[l - 1]["f"], norm_ffn_post[l - 1:l])
        else:
            dx, small_grads[l][0], sib = _norm_pre_bwd(
                tag + "norm_mix_pre_bwd", dh, sv["x"], dx1, norm_mix_pre[l:l + 1],
                _reduce_swap(blocks([0]), 0))

    grad_x = dx.reshape(x.shape)

    delta, new_m, new_v = {}, {}, {}
    sums = chip_sums([0], sib, 0)
    delta["w_up"], new_m["w_up"], new_v["w_up"], recv = _adamw(
        "adamw_w_up", w_up, reds[4], m_w_up, v_w_up, _reduce_scatter(sums))
    final_sums([0], sums, recv, 0)
    delta["w_down"], new_m["w_down"], new_v["w_down"], shared = _adamw(
        "adamw_w_down", w_down, reds[5], m_w_down, v_w_down, _reduce_share([reds[0]], 0))
    g_big = shared + reds[1:]

    (g_nmpre, g_nmpost, g_nfpre, g_nfpost, g_bgate, g_rel, g_wg, g_pscale,
     g_conv) = _allreduce_small(small_grads)
    g_rel = g_rel[:, :, :N_REL]
    g_cb = g_conv[:, 3]
    ncw = conv_w.shape[2]
    chip = 2 * lax.axis_index("x") + lax.axis_index("y")
    g_cw = lax.dynamic_slice_in_dim(g_conv[:, 0:3], chip * ncw, ncw, axis=2)

    grads = dict(norm_mix_pre=g_nmpre, w_in=g_big[0], b_gate=g_bgate, rel_bias=g_rel,
                 w_attn_out=g_big[1], w_pool_group=g_wg, pool_scale=g_pscale, w_pool_out=g_big[2],
                 w_o=g_big[3], norm_mix_post=g_nmpost, norm_ffn_pre=g_nfpre, w_up=g_big[4],
                 conv_w=g_cw, conv_b=g_cb, w_down=g_big[5], norm_ffn_post=g_nfpost)
    weights = dict(norm_mix_pre=norm_mix_pre, w_in=w_in, b_gate=b_gate, rel_bias=rel_bias,
                   w_attn_out=w_attn_out, w_pool_group=w_pool_group, pool_scale=pool_scale,
                   w_pool_out=w_pool_out, w_o=w_o, norm_mix_post=norm_mix_post,
                   norm_ffn_pre=norm_ffn_pre, w_up=w_up, conv_w=conv_w, conv_b=conv_b,
                   w_down=w_down, norm_ffn_post=norm_ffn_post)
    moms = dict(norm_mix_pre=(m_norm_mix_pre, v_norm_mix_pre), w_in=(m_w_in, v_w_in),
                b_gate=(m_b_gate, v_b_gate), rel_bias=(m_rel_bias, v_rel_bias),
                w_attn_out=(m_w_attn_out, v_w_attn_out),
                w_pool_group=(m_w_pool_group, v_w_pool_group),
                pool_scale=(m_pool_scale, v_pool_scale), w_pool_out=(m_w_pool_out, v_w_pool_out),
                w_o=(m_w_o, v_w_o), norm_mix_post=(m_norm_mix_post, v_norm_mix_post),
                norm_ffn_pre=(m_norm_ffn_pre, v_norm_ffn_pre), w_up=(m_w_up, v_w_up),
                conv_w=(m_conv_w, v_conv_w), conv_b=(m_conv_b, v_conv_b),
                w_down=(m_w_down, v_w_down), norm_ffn_post=(m_norm_ffn_post, v_norm_ffn_post))
    order = list(weights.keys())

    small_names = [nm for nm in order if nm not in names]
    for nm in names:
        if nm not in delta:
            delta[nm], new_m[nm], new_v[nm] = _adamw("adamw_" + nm, weights[nm], grads[nm],
                                                     *moms[nm])
    d_s, m_s, v_s = _adamw_small([weights[nm] for nm in small_names],
                                 [grads[nm] for nm in small_names],
                                 [moms[nm][0] for nm in small_names],
                                 [moms[nm][1] for nm in small_names])
    for i, nm in enumerate(small_names):
        delta[nm], new_m[nm], new_v[nm] = d_s[i], m_s[i], v_s[i]

    return (loss, grad_x, *[grads[nm] for nm in order], *[delta[nm] for nm in order],
            *[new_m[nm] for nm in order], *[new_v[nm] for nm in order])
```

```python
import functools
import math

import jax
import jax.numpy as jnp
from jax import lax
from jax.experimental import pallas as pl
from jax.experimental.pallas import tpu as pltpu

F32 = jnp.float32
BF16 = jnp.bfloat16
MESH = pl.DeviceIdType.MESH

D_MODEL = 1024
DEPTH = 2
CHUNK = 64
BAND_CHUNKS = 9
BAND = BAND_CHUNKS * CHUNK
HEADS = 8
HEAD_DIM = 64
ATTN_W = HEADS * HEAD_DIM
POOL_WINDOWS = (2, 4, 8, 16)
POOL_W = 512
POOL_GD = 128
MAX_REL = 256
N_REL = 2 * MAX_REL + 1
D_FF = 2816
IN_W = 3 * ATTN_W + POOL_W + 2 * D_MODEL
EPS = 1e-6
ATTN_SCALE = HEAD_DIM ** -0.5
BAND_PAD = 640
BIAS_LANES = BAND_PAD
N_CHIPS = 4

ADAM_LR = 0.001
ADAM_B1 = 0.9
ADAM_B2 = 0.999
ADAM_EPS = 1e-08
ADAM_WD = 0.01
ADAM_STEP = 10

VMEM_LIMIT_V7X = 56 * 1024 * 1024
TOK = 512
ATT_BLK = 8 * CHUNK
FF_COL = 256
FF_TOK = 1024
HALO = 32


def _cparams(*sem):
    return pltpu.CompilerParams(dimension_semantics=sem, vmem_limit_bytes=VMEM_LIMIT_V7X)


def _sds(shape, dtype):
    return jax.ShapeDtypeStruct(shape, dtype)


class _Comm:
    def __init__(self, ins, outs, copies, n_sems, alias=None):
        self.ins, self.outs, self.copies, self.n_sems = list(ins), list(outs), copies, n_sems
        self.alias = dict(alias or {})


class _SemsFrom:
    def __init__(self, sems, start):
        self.sems, self.start = sems, start

    @property
    def at(self):
        return self

    def __getitem__(self, i):
        return self.sems.at[self.start + i]


def _both(a, b):
    na, nao = len(a.ins), len(a.outs)

    def copies(cin, cout, ssem, rsem):
        return (a.copies(cin[:na], cout[:nao], ssem, rsem)
                + b.copies(cin[na:], cout[nao:], _SemsFrom(ssem, a.n_sems), _SemsFrom(rsem, a.n_sems)))

    alias = dict(a.alias)
    alias.update({na + i: nao + o for i, o in b.alias.items()})
    return _Comm(a.ins + b.ins, a.outs + b.outs, copies, a.n_sems + b.n_sems, alias)


def _pcall(body, name, grid, in_specs, out_specs, out_shape, scratch_shapes, compiler_params, args,
           comm=None, aliases=None):
    single = not isinstance(out_shape, (list, tuple))
    out_specs = [out_specs] if single else list(out_specs)
    out_shape = [out_shape] if single else list(out_shape)
    n_in, n_out = len(in_specs), len(out_specs)
    aliases = dict(aliases or {})
    if comm is None:
        res = pl.pallas_call(
            body, name=name, grid=grid, in_specs=list(in_specs), out_specs=out_specs,
            out_shape=out_shape, scratch_shapes=list(scratch_shapes),
            input_output_aliases=aliases, compiler_params=compiler_params)(*args)
        return (res[0] if single else res), None
    ci, co = len(comm.ins), len(comm.outs)

    def hosted(*refs):
        main_in, cin = refs[:n_in], refs[n_in:n_in + ci]
        main_out = refs[n_in + ci:n_in + ci + n_out]
        cout = refs[n_in + ci + n_out:n_in + ci + n_out + co]
        rest = refs[n_in + ci + n_out + co:]
        copies = comm.copies(cin, cout, rest[-2], rest[-1])
        ids = [pl.program_id(a) for a in range(len(grid))]
        first = functools.reduce(jnp.logical_and, [i == 0 for i in ids])
        last = functools.reduce(jnp.logical_and, [i == g - 1 for i, g in zip(ids, grid)])

        @pl.when(first)
        def _():
            for cp in copies:
                cp.start()

        body(*main_in, *main_out, *rest[:-2])

        @pl.when(last)
        def _():
            for cp in copies:
                cp.wait()

    for i, o in comm.alias.items():
        aliases[n_in + i] = n_out + o
    hbm = pl.BlockSpec(memory_space=pl.ANY)
    sems = pltpu.SemaphoreType.DMA((comm.n_sems,))
    res = pl.pallas_call(
        hosted, name=name, grid=grid, in_specs=list(in_specs) + [hbm] * ci,
        out_specs=out_specs + [hbm] * co, out_shape=out_shape + comm.outs,
        scratch_shapes=list(scratch_shapes) + [sems, sems],
        input_output_aliases=aliases, compiler_params=compiler_params)(*args, *comm.ins)
    return (res[0] if single else list(res[:n_out])), list(res[n_out:])


def _comm_call(name, comm):
    ci = len(comm.ins)

    def body(*refs):
        copies = comm.copies(refs[:ci], refs[ci:-2], refs[-2], refs[-1])
        for cp in copies:
            cp.start()
        for cp in copies:
            cp.wait()

    hbm = pl.BlockSpec(memory_space=pl.ANY)
    sems = pltpu.SemaphoreType.DMA((comm.n_sems,))
    return list(pl.pallas_call(
        body, name=name, in_specs=[hbm] * ci, out_specs=[hbm] * len(comm.outs),
        out_shape=comm.outs, scratch_shapes=[sems, sems],
        input_output_aliases=comm.alias)(*comm.ins))


def _matmul(name, a, b, a_spec, b_spec, o_spec, out_shape, grid, contract, nk, acc_shape,
            fill=None, comm=None):
    def body(*refs):
        a_ref, b_ref = refs[0], refs[1]
        o_ref = refs[2 if fill is None else 3]
        scratch = refs[(3 if fill is None else 4):]
        part = lax.dot_general(a_ref[...], b_ref[...], (contract, ((), ())),
                               preferred_element_type=F32)
        if nk == 1:
            o_ref[...] = part.astype(o_ref.dtype)
        else:
            acc_ref = scratch[0]
            k = pl.program_id(2)

            @pl.when(k == 0)
            def _():
                acc_ref[...] = part

            @pl.when(k > 0)
            def _():
                acc_ref[...] += part

            @pl.when(k == nk - 1)
            def _():
                o_ref[...] = acc_ref[...].astype(o_ref.dtype)

    scratch = [] if nk == 1 else [pltpu.VMEM(acc_shape, F32)]
    in_specs, args, aliases = [a_spec, b_spec], [a, b], {}
    if fill is not None:
        in_specs.append(pl.BlockSpec(memory_space=pl.ANY))
        args.append(fill)
        aliases = {2: 0}
    out, moved = _pcall(body, name, grid, in_specs, o_spec, out_shape, scratch,
                        _cparams("parallel", "parallel", "arbitrary"), args, comm, aliases)
    return out if comm is None else (out, moved)


NN = ((1,), (0,))
NT = ((1,), (1,))
TN = ((0,), (0,))


def _tm(t):
    return min(t, 1024)


def _tt(t):
    return min(t, 2048)


def _col_block_spec(a, rows, nb, row_col):
    if a.ndim == 2:
        return pl.BlockSpec((rows, nb), row_col)

    def halves(*ids):
        r, c = row_col(*ids)
        return c // 2, r, c % 2

    return pl.BlockSpec((None, rows, nb), halves)


def _mm_nn_blocked(name, a, w, l, out_dtype, comm=None):
    t, k = a.shape
    nb = w.shape[3]
    tm = _tm(t)
    return _matmul(
        name, a, w,
        pl.BlockSpec((tm, k), lambda i, n, kk: (i, 0)),
        pl.BlockSpec((None, None, k, nb), lambda i, n, kk: (l, n, 0, 0)),
        pl.BlockSpec((tm, nb), lambda i, n, kk: (i, n)),
        _sds((t, N_CHIPS * nb), out_dtype), (t // tm, N_CHIPS, 1), NN, 1, None, comm=comm)


def _mm_nt_blocked(name, a, w, l, out_dtype, comm=None):
    t = a.shape[-2]
    k, nb = w.shape[2], w.shape[3]
    tm = _tm(t)
    return _matmul(
        name, a, w,
        _col_block_spec(a, tm, nb, lambda i, n, kk: (i, kk)),
        pl.BlockSpec((None, None, k, nb), lambda i, n, kk: (l, kk, 0, 0)),
        pl.BlockSpec((tm, k), lambda i, n, kk: (i, 0)),
        _sds((t, k), out_dtype), (t // tm, 1, N_CHIPS), NT, N_CHIPS, (tm, k), comm=comm)


def _mm_tn_blocked(name, a, g, l, fill):
    t, k = a.shape
    nb = g.shape[-1] * (g.ndim - 1) // N_CHIPS
    tt = _tt(t)
    nt = t // tt
    return _matmul(
        name, a, g,
        pl.BlockSpec((tt, k), lambda n, j, kk: (kk, 0)),
        _col_block_spec(g, tt, nb, lambda n, j, kk: (kk, n)),
        pl.BlockSpec((None, None, k, nb), lambda n, j, kk: (l, n, 0, 0)),
        _sds((DEPTH, N_CHIPS, k, nb), BF16), (N_CHIPS, 1, nt), TN, nt, (k, nb), fill)


def _proj_pieces(rows, dqkv_first):
    def piece(col):
        if dqkv_first:
            return pl.BlockSpec((rows, ATTN_W), lambda i, kk: (i, col))
        return pl.BlockSpec((rows, ATTN_W), lambda n, kk: (kk, col))
    return [piece(0), piece(1), piece(2), piece(0)]


def _proj_dx(name, dqkv, du, dgates, w, l, comm=None):
    t = du.shape[0]
    k, nb = w.shape[2], w.shape[3]
    tm = _tm(t)

    def body(dq_ref, dk_ref, dv_ref, du_ref, dg_ref, w_ref, o_ref, acc_ref):
        kk = pl.program_id(1)

        def mm(a):
            return lax.dot_general(a, w_ref[...], (NT, ((), ())), preferred_element_type=F32)

        @pl.when(kk == 0)
        def _():
            acc_ref[...] = mm(jnp.concatenate([dq_ref[...], dk_ref[...]], axis=1))

        @pl.when(kk == 1)
        def _():
            acc_ref[...] += mm(jnp.concatenate([dv_ref[...], du_ref[...]], axis=1))

        @pl.when(kk >= 2)
        def _():
            acc_ref[...] += mm(dg_ref[...])

        @pl.when(kk == N_CHIPS - 1)
        def _():
            o_ref[...] = acc_ref[...]

    out, moved = _pcall(
        body, name, (t // tm, N_CHIPS),
        _proj_pieces(tm, True)
        + [pl.BlockSpec((tm, nb), lambda i, kk: (i, jnp.maximum(kk - 2, 0))),
           pl.BlockSpec((None, None, k, nb), lambda i, kk: (l, kk, 0, 0))],
        pl.BlockSpec((tm, k), lambda i, kk: (i, 0)), _sds((t, k), F32),
        [pltpu.VMEM((tm, k), F32)], _cparams("arbitrary", "arbitrary"),
        (dqkv, dqkv, dqkv, du, dgates, w), comm)
    return out if comm is None else (out, moved)


def _proj_dw(name, h, dqkv, du, dgates, l, fill):
    t, k = h.shape
    nb = dgates.shape[1] // 2
    tt = _tm(t)
    nt = t // tt

    def body(*refs):
        h_ref, dq_ref, dk_ref, dv_ref, du_ref, dg_ref = refs[:6]
        o_ref, acc_ref = refs[-2], refs[-1]
        n, kk = pl.program_id(0), pl.program_id(1)

        def update(g):
            part = lax.dot_general(h_ref[...], g, (TN, ((), ())), preferred_element_type=F32)

            @pl.when(kk == 0)
            def _():
                acc_ref[...] = part

            @pl.when(kk > 0)
            def _():
                acc_ref[...] += part

        @pl.when(n == 0)
        def _():
            update(jnp.concatenate([dq_ref[...], dk_ref[...]], axis=1))

        @pl.when(n == 1)
        def _():
            update(jnp.concatenate([dv_ref[...], du_ref[...]], axis=1))

        @pl.when(n >= 2)
        def _():
            update(dg_ref[...])

        @pl.when(kk == nt - 1)
        def _():
            o_ref[...] = acc_ref[...].astype(BF16)

    in_specs = ([pl.BlockSpec((tt, k), lambda n, kk: (kk, 0))] + _proj_pieces(tt, False)
                + [pl.BlockSpec((tt, nb), lambda n, kk: (kk, jnp.maximum(n - 2, 0)))])
    args, aliases = [h, dqkv, dqkv, dqkv, du, dgates], {}
    if fill is not None:
        in_specs.append(pl.BlockSpec(memory_space=pl.ANY))
        args.append(fill)
        aliases = {6: 0}
    return pl.pallas_call(
        body, name=name, grid=(N_CHIPS, nt), in_specs=in_specs,
        out_specs=pl.BlockSpec((None, None, k, nb), lambda n, kk: (l, n, 0, 0)),
        out_shape=_sds((DEPTH, N_CHIPS, k, nb), BF16),
        scratch_shapes=[pltpu.VMEM((k, nb), F32)], input_output_aliases=aliases,
        compiler_params=_cparams("parallel", "arbitrary"))(*args)


def _narrow_nn(name, a, w, l):
    t, k = a.shape
    nb = w.shape[3]
    tm = _tm(t)

    def body(a_ref, w_ref, o_ref):
        av = a_ref[...]
        for j in range(N_CHIPS):
            o_ref[:, j * nb:(j + 1) * nb] = jnp.dot(
                av, w_ref[j], preferred_element_type=F32).astype(BF16)

    return pl.pallas_call(
        body, name=name, grid=(t // tm,),
        in_specs=[pl.BlockSpec((tm, k), lambda i: (i, 0)),
                  pl.BlockSpec((None, N_CHIPS, k, nb), lambda i: (l, 0, 0, 0))],
        out_specs=pl.BlockSpec((tm, N_CHIPS * nb), lambda i: (i, 0)),
        out_shape=_sds((t, N_CHIPS * nb), BF16), compiler_params=_cparams("parallel"))(a, w)


def _narrow_nt(name, a, w, l):
    t = a.shape[0]
    k, nb = w.shape[2], w.shape[3]
    tm = _tm(t)

    def body(a_ref, w_ref, o_ref):
        acc = lax.dot_general(a_ref[:, 0:nb], w_ref[0], (NT, ((), ())), preferred_element_type=F32)
        for j in range(1, N_CHIPS):
            acc = acc + lax.dot_general(a_ref[:, j * nb:(j + 1) * nb], w_ref[j], (NT, ((), ())),
                                        preferred_element_type=F32)
        o_ref[...] = acc.astype(BF16)

    return pl.pallas_call(
        body, name=name, grid=(t // tm,),
        in_specs=[pl.BlockSpec((tm, N_CHIPS * nb), lambda i: (i, 0)),
                  pl.BlockSpec((None, N_CHIPS, k, nb), lambda i: (l, 0, 0, 0))],
        out_specs=pl.BlockSpec((tm, k), lambda i: (i, 0)),
        out_shape=_sds((t, k), BF16), compiler_params=_cparams("parallel"))(a, w)


def _narrow_tn(name, a, g, l, fill):
    t, k = a.shape
    nb = g.shape[1] // N_CHIPS
    tt = _tm(t)
    nt = t // tt

    def body(*refs):
        a_ref, g_ref, o_ref, acc_ref = refs[0], refs[1], refs[-2], refs[-1]
        i = pl.program_id(0)
        part = lax.dot_general(a_ref[...], g_ref[...], (TN, ((), ())), preferred_element_type=F32)

        @pl.when(i == 0)
        def _():
            acc_ref[...] = part

        @pl.when(i > 0)
        def _():
            acc_ref[...] += part

        @pl.when(i == nt - 1)
        def _():
            for j in range(N_CHIPS):
                o_ref[j] = acc_ref[:, j * nb:(j + 1) * nb].astype(BF16)

    in_specs = [pl.BlockSpec((tt, k), lambda i: (i, 0)),
                pl.BlockSpec((tt, N_CHIPS * nb), lambda i: (i, 0))]
    args, aliases = [a, g], {}
    if fill is not None:
        in_specs.append(pl.BlockSpec(memory_space=pl.ANY))
        args.append(fill)
        aliases = {2: 0}
    return pl.pallas_call(
        body, name=name, grid=(nt,), in_specs=in_specs,
        out_specs=pl.BlockSpec((None, N_CHIPS, k, nb), lambda i: (l, 0, 0, 0)),
        out_shape=_sds((DEPTH, N_CHIPS, k, nb), BF16),
        scratch_shapes=[pltpu.VMEM((k, N_CHIPS * nb), F32)], input_output_aliases=aliases,
        compiler_params=_cparams("arbitrary"))(*args)


def _mm_nn(name, a, w, l, tk, out_dtype):
    t, k = a.shape
    n = w.shape[2]
    tm = _tm(t)
    nk = k // tk
    return _matmul(
        name, a, w,
        pl.BlockSpec((tm, tk), lambda i, j, kk: (i, kk)),
        pl.BlockSpec((None, tk, n), lambda i, j, kk: (l, kk, 0)),
        pl.BlockSpec((tm, n), lambda i, j, kk: (i, 0)),
        _sds((t, n), out_dtype), (t // tm, 1, nk), NN, nk, (tm, n))


def _mm_nt(name, a, w, l, tn, out_dtype, comm=None):
    t, n = a.shape
    k = w.shape[1]
    tm = _tm(t)
    return _matmul(
        name, a, w,
        pl.BlockSpec((tm, n), lambda i, j, kk: (i, 0)),
        pl.BlockSpec((None, tn, n), lambda i, j, kk: (l, j, 0)),
        pl.BlockSpec((tm, tn), lambda i, j, kk: (i, j)),
        _sds((t, k), out_dtype), (t // tm, k // tn, 1), NT, 1, None, comm=comm)


def _mm_tn(name, a, g, tko, l, fill):
    t, k = a.shape
    n = g.shape[1]
    tt = _tt(t)
    nt = t // tt
    return _matmul(
        name, a, g,
        pl.BlockSpec((tt, tko), lambda i, j, kk: (kk, i)),
        pl.BlockSpec((tt, n), lambda i, j, kk: (kk, 0)),
        pl.BlockSpec((None, tko, n), lambda i, j, kk: (l, i, 0)),
        _sds((DEPTH, k, n), BF16), (k // tko, 1, nt), TN, nt, (tko, n), fill)


def _row_spec(width, col=0):
    return pl.BlockSpec((TOK, width), lambda i: (i, col))


def _vec_spec(width):
    return pl.BlockSpec((1, width), lambda i: (0, 0))


def _rms(x):
    return lax.rsqrt(jnp.mean(x * x, axis=-1, keepdims=True) + EPS)


def _norm_fwd(name, x, g, comm=None):
    t = x.shape[0]

    def body(x_ref, g_ref, h_ref):
        xv = x_ref[...]
        h_ref[...] = (xv * _rms(xv) * g_ref[...]).astype(BF16)

    out, moved = _pcall(body, name, (t // TOK,), [_row_spec(D_MODEL), _vec_spec(D_MODEL)],
                        _row_spec(D_MODEL), _sds((t, D_MODEL), BF16), [], _cparams("arbitrary"),
                        (x, g), comm)
    return out if comm is None else (out, moved)


ROWS = 16
ROW_UNROLL = 8


def _rows(k):
    return pl.ds(pl.multiple_of(k * ROWS, ROWS), ROWS)


def _strips(step, init):
    def group(j, carry):
        for u in range(ROW_UNROLL):
            carry = step(j * ROW_UNROLL + u, carry)
        return carry

    return lax.fori_loop(0, TOK // (ROWS * ROW_UNROLL), group, init)


def _fold_rows(x):
    return x[0:8] + x[8:16]


def _accumulate(ref, part):
    total = jnp.sum(part, axis=0, keepdims=True)

    @pl.when(pl.program_id(0) == 0)
    def _():
        ref[...] = total

    @pl.when(pl.program_id(0) > 0)
    def _():
        ref[...] += total


def _norm_bwd_rows(d, mv, g):
    r = _rms(mv)
    n = mv * r
    dn = d * g
    return r * (dn - n * jnp.mean(dn * n, axis=-1, keepdims=True)), d * n


def _post_pre_fwd(name, xres, m, g_post, g_pre, comm=None):
    t = xres.shape[0]

    def body(x_ref, m_ref, gp_ref, gn_ref, x1_ref, h_ref):
        def strip(k, c):
            rows = _rows(k)
            mv = m_ref[rows, :]
            x1 = x_ref[rows, :] + mv * _rms(mv) * gp_ref[...]
            x1_ref[rows, :] = x1
            h_ref[rows, :] = (x1 * _rms(x1) * gn_ref[...]).astype(BF16)
            return c

        _strips(strip, 0)

    outs, moved = _pcall(
        body, name, (t // TOK,),
        [_row_spec(D_MODEL), _row_spec(D_MODEL), _vec_spec(D_MODEL), _vec_spec(D_MODEL)],
        [_row_spec(D_MODEL), _row_spec(D_MODEL)],
        [_sds((t, D_MODEL), F32), _sds((t, D_MODEL), BF16)], [], _cparams("arbitrary"),
        (xres, m, g_post, g_pre), comm)
    return outs if comm is None else (*outs, moved)


def _tail(name, xres, m, g_post, target):
    t = xres.shape[0]

    def body(x_ref, m_ref, g_ref, t_ref, dy_ref, dm_ref, dg_ref, l_ref):
        def strip(k, carry):
            rows = _rows(k)
            mv = m_ref[rows, :]
            e = x_ref[rows, :] + mv * _rms(mv) * g_ref[...] - t_ref[rows, :]
            dy = e * (1.0 / D_MODEL)
            dy_ref[rows, :] = dy
            dm, dgn = _norm_bwd_rows(dy, mv, g_ref[...])
            dm_ref[rows, :] = dm.astype(BF16)
            return carry[0] + _fold_rows(dgn), carry[1] + _fold_rows(e * e)

        zero = jnp.zeros((8, D_MODEL), F32)
        dg, sq = _strips(strip, (zero, zero))
        _accumulate(dg_ref, dg)
        _accumulate(l_ref, jnp.sum(sq, axis=1, keepdims=True))

    dy, dm, dg, sq = pl.pallas_call(
        body, name=name, grid=(t // TOK,),
        in_specs=[_row_spec(D_MODEL), _row_spec(D_MODEL), _vec_spec(D_MODEL), _row_spec(D_MODEL)],
        out_specs=[_row_spec(D_MODEL), _row_spec(D_MODEL), _vec_spec(D_MODEL),
                   pl.BlockSpec((1, 1), lambda i: (0, 0))],
        out_shape=[_sds((t, D_MODEL), F32), _sds((t, D_MODEL), BF16), _sds((1, D_MODEL), F32),
                   _sds((1, 1), F32)],
        compiler_params=_cparams("arbitrary"))(xres, m, g_post, target)
    return dy, dm, dg, sq[0, 0] * (0.5 / D_MODEL)


def _pre_post_bwd(name, dh, xin, dxo, g_pre, m, g_post, comm=None):
    t = dh.shape[0]

    def body(dh_ref, x_ref, d_ref, gq_ref, m_ref, gp_ref, dx_ref, dgq_ref, dm_ref, dgp_ref):
        def strip(k, carry):
            rows = _rows(k)
            dxin, dgq = _norm_bwd_rows(dh_ref[rows, :], x_ref[rows, :], gq_ref[...])
            dx = d_ref[rows, :] + dxin
            dx_ref[rows, :] = dx
            dm, dgp = _norm_bwd_rows(dx, m_ref[rows, :], gp_ref[...])
            dm_ref[rows, :] = dm.astype(BF16)
            return carry[0] + _fold_rows(dgq), carry[1] + _fold_rows(dgp)

        zero = jnp.zeros((8, D_MODEL), F32)
        dgq, dgp = _strips(strip, (zero, zero))
        _accumulate(dgq_ref, dgq)
        _accumulate(dgp_ref, dgp)

    outs, moved = _pcall(
        body, name, (t // TOK,),
        [_row_spec(D_MODEL), _row_spec(D_MODEL), _row_spec(D_MODEL), _vec_spec(D_MODEL),
         _row_spec(D_MODEL), _vec_spec(D_MODEL)],
        [_row_spec(D_MODEL), _vec_spec(D_MODEL), _row_spec(D_MODEL), _vec_spec(D_MODEL)],
        [_sds((t, D_MODEL), F32), _sds((1, D_MODEL), F32), _sds((t, D_MODEL), BF16),
         _sds((1, D_MODEL), F32)], [], _cparams("arbitrary"),
        (dh, xin, dxo, g_pre, m, g_post), comm)
    return outs if comm is None else (*outs, moved)


def _norm_pre_bwd(name, dh, xin, dxo, g, comm=None):
    t = dh.shape[0]

    def body(dh_ref, x_ref, d_ref, g_ref, dx_ref, dg_ref):
        xv = x_ref[...]
        dhv = dh_ref[...]
        r = _rms(xv)
        n = xv * r
        dn = dhv * g_ref[...]
        dx_ref[...] = d_ref[...] + r * (dn - n * jnp.mean(dn * n, axis=-1, keepdims=True))
        part = jnp.sum(dhv * n, axis=0, keepdims=True)

        @pl.when(pl.program_id(0) == 0)
        def _():
            dg_ref[...] = part

        @pl.when(pl.program_id(0) > 0)
        def _():
            dg_ref[...] += part

    out, moved = _pcall(
        body, name, (t // TOK,),
        [_row_spec(D_MODEL), _row_spec(D_MODEL), _row_spec(D_MODEL), _vec_spec(D_MODEL)],
        [_row_spec(D_MODEL), _vec_spec(D_MODEL)],
        [_sds((t, D_MODEL), F32), _sds((1, D_MODEL), F32)], [], _cparams("arbitrary"),
        (dh, xin, dxo, g), comm)
    return out if comm is None else (*out, moved)


def _gate_fwd(name, proj, b_gate, ya, yb):
    t = proj.shape[0]

    def body(ga_ref, gb_ref, b_ref, ya_ref, yb_ref, z_ref):
        def strip(k, c):
            rows = _rows(k)
            sa = jax.nn.sigmoid(ga_ref[rows, :].astype(F32) + b_ref[:, :D_MODEL])
            sb = jax.nn.sigmoid(gb_ref[rows, :].astype(F32) + b_ref[:, D_MODEL:])
            z_ref[rows, :] = (sa * ya_ref[rows, :].astype(F32)
                              + sb * yb_ref[rows, :].astype(F32)).astype(BF16)
            return c

        _strips(strip, 0)

    return pl.pallas_call(
        body, name=name, grid=(t // TOK,),
        in_specs=[_row_spec(D_MODEL, 2), _row_spec(D_MODEL, 3), _vec_spec(2 * D_MODEL),
                  _row_spec(D_MODEL), _row_spec(D_MODEL)],
        out_specs=_row_spec(D_MODEL), out_shape=_sds((t, D_MODEL), BF16),
        compiler_params=_cparams("parallel"))(proj, proj, b_gate, ya, yb)


def _gate_bwd(name, dz, proj, b_gate, ya, yb):
    t = proj.shape[0]

    def body(dz_ref, ga_ref, gb_ref, b_ref, ya_ref, yb_ref, dya_ref, dyb_ref, dg_ref, db_ref):
        def strip(k, carry):
            rows = _rows(k)
            dzv = dz_ref[rows, :].astype(F32)
            sa = jax.nn.sigmoid(ga_ref[rows, :].astype(F32) + b_ref[:, :D_MODEL])
            sb = jax.nn.sigmoid(gb_ref[rows, :].astype(F32) + b_ref[:, D_MODEL:])
            dya_ref[rows, :] = (dzv * sa).astype(BF16)
            dyb_ref[rows, :] = (dzv * sb).astype(BF16)
            dga = dzv * ya_ref[rows, :].astype(F32) * sa * (1.0 - sa)
            dgb = dzv * yb_ref[rows, :].astype(F32) * sb * (1.0 - sb)
            dg_ref[rows, :D_MODEL] = dga.astype(BF16)
            dg_ref[rows, D_MODEL:] = dgb.astype(BF16)
            return carry[0] + _fold_rows(dga), carry[1] + _fold_rows(dgb)

        zero = jnp.zeros((8, D_MODEL), F32)
        pa, pb = _strips(strip, (zero, zero))
        _accumulate(db_ref.at[:, :D_MODEL], pa)
        _accumulate(db_ref.at[:, D_MODEL:], pb)

    return pl.pallas_call(
        body, name=name, grid=(t // TOK,),
        in_specs=[_row_spec(D_MODEL), _row_spec(D_MODEL, 2), _row_spec(D_MODEL, 3),
                  _vec_spec(2 * D_MODEL), _row_spec(D_MODEL), _row_spec(D_MODEL)],
        out_specs=[_row_spec(D_MODEL), _row_spec(D_MODEL), _row_spec(2 * D_MODEL),
                   _vec_spec(2 * D_MODEL)],
        out_shape=[_sds((t, D_MODEL), BF16), _sds((t, D_MODEL), BF16),
                   _sds((t, 2 * D_MODEL), BF16), _sds((1, 2 * D_MODEL), F32)],
        compiler_params=_cparams("arbitrary"))(dz, proj, proj, b_gate, ya, yb)


def _head_masks():
    lane = lax.broadcasted_iota(jnp.int32, (1, 2 * HEAD_DIM), 1)
    return lane < HEAD_DIM


BAND_ROWS = 2 * ATT_BLK + CHUNK


def _fill_band(band, prev_ref, cur_ref):
    band[0:ATT_BLK, :] = prev_ref[...]
    band[ATT_BLK:2 * ATT_BLK, :] = cur_ref[...]
    band[2 * ATT_BLK:, :] = jnp.zeros((CHUNK, ATTN_W), BF16)


def _pair_rows(x2, low):
    zero = jnp.zeros_like(x2)
    return jnp.concatenate([jnp.where(low, x2, zero), jnp.where(low, zero, x2)], axis=0)


def _pair_diag(o2, low):
    return jnp.where(low, o2[0:CHUNK, :], o2[CHUNK:, :])


N_PAIRS = HEADS // 2
SM_STRIP = 32
N_STRIPS = BAND_PAD // SM_STRIP
NEG = -1e30


def _fold8(x, op):
    return op(op(x[0:8], x[8:16]), op(x[16:24], x[24:32]))


def _strip(k):
    return pl.ds(pl.multiple_of(k * SM_STRIP, SM_STRIP), SM_STRIP)


def _band_probs(k2, qcat, bias_t, first_key):
    kpos = lax.broadcasted_iota(jnp.int32, (BAND_PAD, 1), 0)
    st = lax.dot_general(k2, qcat, (NT, ((), ())), preferred_element_type=F32)
    st = jnp.where(kpos + first_key >= 0, st + bias_t, NEG)
    e = jnp.exp(st - jnp.max(st, axis=0, keepdims=True))
    return e * (1.0 / jnp.sum(e, axis=0, keepdims=True))


def _band_softmax_stats(st_ref, b_ref, first_key, dp_ref):
    rowi = lax.broadcasted_iota(jnp.int32, (SM_STRIP, 128), 0)

    def scores(k, mx):
        rows = _strip(k)
        live = (rowi + (k * SM_STRIP + first_key)) >= 0
        out = []
        for hp in range(N_PAIRS):
            x = jnp.where(live, st_ref[hp, rows, :] + b_ref[hp, rows, :], NEG)
            st_ref[hp, rows, :] = x
            out.append(jnp.maximum(mx[hp], _fold8(x, jnp.maximum)))
        return tuple(out)

    mx = lax.fori_loop(0, N_STRIPS, scores, (jnp.full((8, 128), NEG, F32),) * N_PAIRS, unroll=2)
    top = [jnp.max(m, axis=0, keepdims=True) for m in mx]

    def sums(k, acc):
        rows = _strip(k)
        ls, eds = [], []
        for hp in range(N_PAIRS):
            e = jnp.exp(st_ref[hp, rows, :] - top[hp])
            ls.append(acc[hp] + _fold8(e, jnp.add))
            eds.append(acc[N_PAIRS + hp] + _fold8(e * dp_ref[hp, rows, :], jnp.add))
        return tuple(ls + eds)

    acc = lax.fori_loop(0, N_STRIPS, sums, (jnp.zeros((8, 128), F32),) * (2 * N_PAIRS), unroll=2)
    inv = [1.0 / jnp.sum(a, axis=0, keepdims=True) for a in acc[:N_PAIRS]]
    delta = [jnp.sum(a, axis=0, keepdims=True) * i for a, i in zip(acc[N_PAIRS:], inv)]
    return top, inv, delta


def _attn_specs(nblk):
    cur = lambda col: pl.BlockSpec((ATT_BLK, ATTN_W), lambda s: (jnp.minimum(s, nblk - 1), col))
    prev = lambda col: pl.BlockSpec(
        (ATT_BLK, ATTN_W), lambda s: (jnp.maximum(jnp.minimum(s, nblk - 1) - 1, 0), col))
    return cur, prev


def _attn_fwd(name, proj, bias, comm=None):
    t = proj.shape[0]
    nblk = t // ATT_BLK
    cur, prev = _attn_specs(nblk)

    def body(q_ref, kp_ref, kc_ref, vp_ref, vc_ref, b_ref, o_ref, p_ref, kband, vband):
        s = pl.program_id(0)
        _fill_band(kband, kp_ref, kc_ref)
        _fill_band(vband, vp_ref, vc_ref)
        low = _head_masks()

        def chunk(ci, carry):
            r0 = pl.multiple_of(ci * CHUNK, CHUNK)
            for hp in range(N_PAIRS):
                cols = slice(hp * 128, (hp + 1) * 128)
                qcat = _pair_rows(q_ref[pl.ds(r0, CHUNK), cols] * ATTN_SCALE, low)
                p = _band_probs(kband[pl.ds(r0, BAND_PAD), cols], qcat, b_ref[hp],
                                (s * 8 - 8 + ci) * CHUNK).astype(BF16)
                p_ref[ci, hp] = p
                o2 = lax.dot_general(p, vband[pl.ds(r0, BAND_PAD), cols],
                                     (TN, ((), ())), preferred_element_type=F32)
                o_ref[pl.ds(r0, CHUNK), cols] = _pair_diag(o2, low).astype(BF16)
            return carry

        lax.fori_loop(0, 8, chunk, 0)

    outs, moved = _pcall(
        body, name, (nblk,),
        [cur(0), prev(1), cur(1), prev(2), cur(2),
         pl.BlockSpec((N_PAIRS, BAND_PAD, 128), lambda s: (0, 0, 0))],
        [pl.BlockSpec((ATT_BLK, ATTN_W), lambda s: (s, 0)),
         pl.BlockSpec((8, N_PAIRS, BAND_PAD, 128), lambda s: (s, 0, 0, 0))],
        [_sds((t, ATTN_W), BF16), _sds((t // CHUNK, N_PAIRS, BAND_PAD, 128), BF16)],
        [pltpu.VMEM((BAND_ROWS, ATTN_W), BF16), pltpu.VMEM((BAND_ROWS, ATTN_W), BF16)],
        _cparams("arbitrary"), (proj, proj, proj, proj, proj, bias), comm)
    return outs if comm is None else (*outs, moved)


def _attn_bwd(name, proj, datt, probs, comm=None):
    t = proj.shape[0]
    nblk = t // ATT_BLK
    cur, prev = _attn_specs(nblk)
    late = pl.BlockSpec((ATT_BLK, 3 * ATTN_W), lambda s: (jnp.maximum(s - 1, 0), 0))

    def body(q_ref, kp_ref, kc_ref, vp_ref, vc_ref, do_ref, p_ref,
             dqkv_ref, db_ref, kband, vband, dkacc, dvacc,
             dp_ref, dsb_ref, qc_ref, dc_ref, dq_ref, dq_held):
        s = pl.program_id(0)

        @pl.when(s == 0)
        def _():
            dkacc[...] = jnp.zeros_like(dkacc)
            dvacc[...] = jnp.zeros_like(dvacc)
            db_ref[...] = jnp.zeros_like(db_ref)
            dq_ref[...] = jnp.zeros_like(dq_ref)

        @pl.when(s < nblk)
        def _():
            _fill_band(kband, kp_ref, kc_ref)
            _fill_band(vband, vp_ref, vc_ref)
            low = _head_masks()

            def chunk(ci, carry):
                r0 = pl.multiple_of(ci * CHUNK, CHUNK)
                for hp in range(N_PAIRS):
                    cols = slice(hp * 128, (hp + 1) * 128)
                    qc_ref[hp] = _pair_rows(q_ref[pl.ds(r0, CHUNK), cols] * ATTN_SCALE, low)
                    dc_ref[hp] = _pair_rows(do_ref[pl.ds(r0, CHUNK), cols], low)
                    dp_ref[hp] = lax.dot_general(vband[pl.ds(r0, BAND_PAD), cols], dc_ref[hp],
                                                 (NT, ((), ())), preferred_element_type=F32)

                def sums(k, acc):
                    rows = _strip(k)
                    return tuple(acc[hp] + _fold8(p_ref[ci, hp, rows, :].astype(F32)
                                                  * dp_ref[hp, rows, :], jnp.add)
                                 for hp in range(N_PAIRS))

                acc = lax.fori_loop(0, N_STRIPS, sums, (jnp.zeros((8, 128), F32),) * N_PAIRS)
                delta = [jnp.sum(a, axis=0, keepdims=True) for a in acc]

                def grads(k, c):
                    rows = _strip(k)
                    for hp in range(N_PAIRS):
                        ds = (p_ref[ci, hp, rows, :].astype(F32)
                              * (dp_ref[hp, rows, :] - delta[hp]))
                        db_ref[hp, rows, :] += ds
                        dsb_ref[hp, rows, :] = ds.astype(BF16)
                    return c

                lax.fori_loop(0, N_STRIPS, grads, 0)
                for hp in range(N_PAIRS):
                    cols = slice(hp * 128, (hp + 1) * 128)
                    dq2 = lax.dot_general(dsb_ref[hp], kband[pl.ds(r0, BAND_PAD), cols],
                                          (TN, ((), ())), preferred_element_type=F32)
                    dq_ref[pl.ds(r0, CHUNK), cols] = (_pair_diag(dq2, low) * ATTN_SCALE).astype(BF16)
                    dkacc[pl.ds(r0, BAND_PAD), cols] += jnp.dot(dsb_ref[hp], qc_ref[hp],
                                                               preferred_element_type=F32)
                    dvacc[pl.ds(r0, BAND_PAD), cols] += jnp.dot(p_ref[ci, hp], dc_ref[hp],
                                                               preferred_element_type=F32)
                return carry

            dq_held[...] = dq_ref[...]
            lax.fori_loop(0, 8, chunk, 0)

        @pl.when(s == nblk)
        def _():
            dq_held[...] = dq_ref[...]

        dqkv_ref[:, 0:ATTN_W] = dq_held[...]
        dqkv_ref[:, ATTN_W:2 * ATTN_W] = dkacc[0:ATT_BLK, :].astype(BF16)
        dqkv_ref[:, 2 * ATTN_W:] = dvacc[0:ATT_BLK, :].astype(BF16)
        dkacc[0:ATT_BLK, :] = dkacc[ATT_BLK:2 * ATT_BLK, :]
        dvacc[0:ATT_BLK, :] = dvacc[ATT_BLK:2 * ATT_BLK, :]
        dkacc[ATT_BLK:, :] = jnp.zeros((ATT_BLK + CHUNK, ATTN_W), F32)
        dvacc[ATT_BLK:, :] = jnp.zeros((ATT_BLK + CHUNK, ATTN_W), F32)

    outs, moved = _pcall(
        body, name, (nblk + 1,),
        [cur(0), prev(1), cur(1), prev(2), cur(2),
         pl.BlockSpec((ATT_BLK, ATTN_W), lambda s: (jnp.minimum(s, nblk - 1), 0)),
         pl.BlockSpec((8, N_PAIRS, BAND_PAD, 128), lambda s: (jnp.minimum(s, nblk - 1), 0, 0, 0))],
        [late, pl.BlockSpec((HEADS // 2, BAND_PAD, 128), lambda s: (0, 0, 0))],
        [_sds((t, 3 * ATTN_W), BF16), _sds((HEADS // 2, BAND_PAD, 128), F32)],
        [pltpu.VMEM((BAND_ROWS, ATTN_W), BF16), pltpu.VMEM((BAND_ROWS, ATTN_W), BF16),
         pltpu.VMEM((BAND_ROWS, ATTN_W), F32), pltpu.VMEM((BAND_ROWS, ATTN_W), F32),
         pltpu.VMEM((N_PAIRS, BAND_PAD, 128), F32), pltpu.VMEM((N_PAIRS, BAND_PAD, 128), BF16),
         pltpu.VMEM((N_PAIRS, 2 * CHUNK, 128), BF16), pltpu.VMEM((N_PAIRS, 2 * CHUNK, 128), BF16),
         pltpu.VMEM((ATT_BLK, ATTN_W), BF16), pltpu.VMEM((ATT_BLK, ATTN_W), BF16)],
        _cparams("arbitrary"), (proj, proj, proj, proj, proj, datt, probs), comm)
    return outs if comm is None else (*outs, moved)


def _diag_onehot(rel_rows):
    d0 = lax.broadcasted_iota(jnp.int32, (BIAS_LANES, BIAS_LANES), 0)
    d1 = lax.broadcasted_iota(jnp.int32, (BIAS_LANES, BIAS_LANES), 1)
    m, n = (d0, d1) if rel_rows else (d1, d0)
    hit = (m == jnp.minimum(BAND - 1 + MAX_REL - n, 2 * MAX_REL)) & (n < BAND + CHUNK - 1)
    return jnp.where(hit, 1.0, 0.0).astype(F32)


def _bias_table(name, rel_bias_l):
    rel_pad = jnp.pad(rel_bias_l, ((0, 0), (0, BIAS_LANES - N_REL)))

    def body(r_ref, o_ref):
        diag = jnp.dot(r_ref[...], _diag_onehot(True), preferred_element_type=F32,
                       precision=lax.Precision.HIGHEST)
        rowid = lax.broadcasted_iota(jnp.int32, (8, BIAS_LANES), 0)
        lane = lax.broadcasted_iota(jnp.int32, (8, BIAS_LANES), 1)
        for h in range(HEADS):
            d8 = jnp.broadcast_to(diag[h:h + 1, :], (8, BIAS_LANES))
            slab0 = pltpu.roll(d8, BIAS_LANES - CHUNK + 1, axis=1)
            for b in range(1, 8):
                slab0 = jnp.where(rowid == b, pltpu.roll(d8, BIAS_LANES - CHUNK + 1 + b, axis=1),
                                  slab0)
            for a in range(8):
                slab = slab0 if a == 0 else pltpu.roll(slab0, 8 * a, axis=1)
                o_ref[h * CHUNK + 8 * a:h * CHUNK + 8 * a + 8, :] = jnp.where(lane < BAND, slab, NEG)

    tab = pl.pallas_call(
        body, name=name,
        in_specs=[pl.BlockSpec(memory_space=pltpu.VMEM)],
        out_specs=pl.BlockSpec(memory_space=pltpu.VMEM),
        out_shape=_sds((HEADS * CHUNK, BIAS_LANES), F32),
    )(rel_pad)
    tab = tab.reshape(HEADS // 2, 2, CHUNK, BIAS_LANES)
    return jnp.transpose(tab, (0, 3, 1, 2)).reshape(HEADS // 2, BIAS_LANES, 2 * CHUNK)


def _bias_fold(name, dbias_t):
    rows = HEADS * CHUNK
    dbias = jnp.transpose(dbias_t.reshape(HEADS // 2, BIAS_LANES, 2, CHUNK), (0, 2, 3, 1))

    def body(d_ref, o_ref):
        rowid = lax.broadcasted_iota(jnp.int32, (8, BIAS_LANES), 0)
        diags = []
        for h in range(HEADS):
            acc = d_ref[h * CHUNK + 56:h * CHUNK + 64, :]
            for a in range(7):
                slab = d_ref[h * CHUNK + 8 * a:h * CHUNK + 8 * a + 8, :]
                acc = acc + pltpu.roll(slab, 56 - 8 * a, axis=1)
            tot = jnp.where(rowid == 7, acc, 0.0)
            for b in range(7):
                tot = tot + jnp.where(rowid == b, pltpu.roll(acc, 7 - b, axis=1), 0.0)
            diags.append(jnp.sum(tot, axis=0, keepdims=True))
        diag = jnp.concatenate(diags, axis=0)
        o_ref[...] = jnp.dot(diag, _diag_onehot(False), preferred_element_type=F32,
                             precision=lax.Precision.HIGHEST)

    return pl.pallas_call(
        body, name=name,
        in_specs=[pl.BlockSpec(memory_space=pltpu.VMEM)],
        out_specs=pl.BlockSpec(memory_space=pltpu.VMEM),
        out_shape=_sds((HEADS, BIAS_LANES), F32),
    )(dbias.reshape(rows, BIAS_LANES))


def _inv_counts(i):
    trow = lax.broadcasted_iota(jnp.int32, (TOK + HALO, 1), 0) + i * TOK
    return [1.0 / jnp.minimum(trow + 1, w).astype(F32) for w in POOL_WINDOWS]


def _pool_fwd(name, proj, wg, scale, comm=None):
    t = proj.shape[0]
    hb = TOK // HALO

    def body(u_ref, up_ref, wg_ref, sc_ref, pooled_ref, mixed_ref, b0, b1, b2, b3):
        i = pl.program_id(0)
        halo = up_ref[...].astype(F32)
        b0[0:HALO, :] = jnp.where(i == 0, jnp.zeros_like(halo), halo)
        b0[HALO:, :] = u_ref[...].astype(F32)
        n = TOK + HALO
        b1[8:n, :] = b0[8:n, :] + b0[7:n - 1, :]
        b2[16:n, 128:] = b1[16:n, 128:] + b1[14:n - 2, 128:]
        b3[24:n, 256:] = b2[24:n, 256:] + b2[20:n - 4, 256:]
        wins = [b1[HALO:n, 0:128], b2[HALO:n, 128:256], b3[HALO:n, 256:384],
                b3[HALO:n, 384:512] + b3[HALO - 8:n - 8, 384:512]]
        inv = _inv_counts(i)
        for g in range(4):
            cols = slice(g * POOL_GD, (g + 1) * POOL_GD)
            pooled = (wins[g] * inv[g][0:TOK] - b0[HALO:n, cols]).astype(BF16)
            pooled_ref[:, cols] = pooled
            pre = jnp.dot(pooled, wg_ref[g], preferred_element_type=F32)
            mixed_ref[:, cols] = (pre * sc_ref[:, cols]).astype(BF16)

    buf = pltpu.VMEM((TOK + HALO, POOL_W), F32)
    outs, moved = _pcall(
        body, name, (t // TOK,),
        [_row_spec(POOL_W, 3),
         pl.BlockSpec((HALO, POOL_W), lambda i: (jnp.maximum(i * hb - 1, 0), 3)),
         pl.BlockSpec((4, POOL_GD, POOL_GD), lambda i: (0, 0, 0)), _vec_spec(POOL_W)],
        [_row_spec(POOL_W), _row_spec(POOL_W)],
        [_sds((t, POOL_W), BF16), _sds((t, POOL_W), BF16)], [buf, buf, buf, buf],
        _cparams("arbitrary"), (proj, proj, wg, scale), comm)
    return outs if comm is None else (*outs, moved)


def _pool_bwd(name, dmixed, pooled, wg, scale, comm=None):
    t = dmixed.shape[0]
    nt = t // TOK
    hb = TOK // HALO

    def body(dm_ref, dmn_ref, p_ref, wg_ref, sc_ref, du_ref, dwg_ref, dsc_ref, c0, c1, c2, c3):
        i = pl.program_id(0)

        @pl.when(i == 0)
        def _():
            dwg_ref[...] = jnp.zeros_like(dwg_ref)
            dsc_ref[...] = jnp.zeros_like(dsc_ref)

        n = TOK + HALO
        inv = _inv_counts(i)
        dmv = dm_ref[...].astype(F32)
        dmn = dmn_ref[...].astype(F32)
        dmn = jnp.where(i == nt - 1, jnp.zeros_like(dmn), dmn)
        for g in range(4):
            cols = slice(g * POOL_GD, (g + 1) * POOL_GD)
            scg = sc_ref[:, cols]
            pg = p_ref[:, cols]
            dpre = (dmv[:, cols] * scg).astype(BF16)
            dpre_n = (dmn[:, cols] * scg).astype(BF16)
            pre = jnp.dot(pg, wg_ref[g], preferred_element_type=F32)
            dsc_ref[:, cols] += jnp.sum(dmv[:, cols] * pre, axis=0, keepdims=True)
            dwg_ref[g] += lax.dot_general(pg, dpre, (TN, ((), ())), preferred_element_type=F32)
            dpool = lax.dot_general(dpre, wg_ref[g], (NT, ((), ())), preferred_element_type=F32)
            dpool_n = lax.dot_general(dpre_n, wg_ref[g], (NT, ((), ())),
                                      preferred_element_type=F32)
            c0[0:TOK, cols] = dpool
            c0[TOK:n, cols] = dpool_n
            c1[0:TOK, cols] = dpool * inv[g][0:TOK]
            c1[TOK:n, cols] = dpool_n * inv[g][TOK:n]
        c2[0:n - 8, :] = c1[0:n - 8, :] + c1[1:n - 7, :]
        c3[0:n - 16, 128:] = c2[0:n - 16, 128:] + c2[2:n - 14, 128:]
        c1[0:n - 24, 256:] = c3[0:n - 24, 256:] + c3[4:n - 20, 256:]
        wins = [c2[0:TOK, 0:128], c3[0:TOK, 128:256], c1[0:TOK, 256:384],
                c1[0:TOK, 384:512] + c1[8:TOK + 8, 384:512]]
        for g in range(4):
            cols = slice(g * POOL_GD, (g + 1) * POOL_GD)
            du_ref[:, cols] = (wins[g] - c0[0:TOK, cols]).astype(BF16)

    buf = pltpu.VMEM((TOK + HALO, POOL_W), F32)
    outs, moved = _pcall(
        body, name, (nt,),
        [_row_spec(POOL_W),
         pl.BlockSpec((HALO, POOL_W), lambda i: (jnp.minimum((i + 1) * hb, nt * hb - 1), 0)),
         _row_spec(POOL_W), pl.BlockSpec((4, POOL_GD, POOL_GD), lambda i: (0, 0, 0)),
         _vec_spec(POOL_W)],
        [_row_spec(POOL_W), pl.BlockSpec((4, POOL_GD, POOL_GD), lambda i: (0, 0, 0)),
         _vec_spec(POOL_W)],
        [_sds((t, POOL_W), BF16), _sds((4, POOL_GD, POOL_GD), F32), _sds((1, POOL_W), F32)],
        [buf, buf, buf, buf], _cparams("arbitrary"), (dmixed, dmixed, pooled, wg, scale), comm)
    return outs if comm is None else (*outs, moved)


GELU_C = math.sqrt(2.0 / math.pi)


GELU_K = 0.044715


def _gelu_parts(x):
    x2 = x * x
    s = 0.5 + 0.5 * jnp.tanh(x * (GELU_C + (GELU_C * GELU_K) * x2))
    return x * s, s, x2


def _gelu(x):
    return _gelu_parts(x)[0]


def _gelu_and_grad(x):
    g, s, x2 = _gelu_parts(x)
    return g, s + g * (1.0 - s) * ((2 * GELU_C) + (6 * GELU_C * GELU_K) * x2)


def _taps(buf, r, rows):
    a = buf[pl.ds(r, rows + 8), :]
    return a[8:], pltpu.roll(a, 1, axis=0)[8:], pltpu.roll(a, 2, axis=0)[8:]


def _conv(taps, w_ref, b_ref):
    return b_ref[...] + w_ref[2:3, :] * taps[0] + w_ref[1:2, :] * taps[1] + w_ref[0:1, :] * taps[2]


def _stage(dst, prev_ref, cur_ref, next_ref, first, last):
    rows = cur_ref.shape[0]
    h = prev_ref[...].astype(F32)
    dst[0:8, :] = jnp.where(first, jnp.zeros_like(h), h)
    dst[8:8 + rows, :] = cur_ref[...].astype(F32)
    if next_ref is not None:
        h = next_ref[...].astype(F32)
        dst[8 + rows:, :] = jnp.where(last, jnp.zeros_like(h), h)


FWD_STRIP = 32
BWD_STRIP = 16


def _ffn_gate_fwd(name, hu, conv_w, conv_b, comm=None):
    t = hu.shape[0]
    ncol = D_FF // FF_COL
    hb = FF_TOK // 8

    def tile(off):
        return pl.BlockSpec((FF_TOK, FF_COL), lambda i, j: (i, j + off))

    def halo(off):
        return pl.BlockSpec((8, FF_COL), lambda i, j: (jnp.maximum(i * hb - 1, 0), j + off))

    def wspec(off):
        return pl.BlockSpec((3, FF_COL), lambda i, j: (0, j + off))

    def bspec(off):
        return pl.BlockSpec((1, FF_COL), lambda i, j: (0, j + off))

    def body(v_ref, vp_ref, g_ref, gp_ref, wv_ref, wg_ref, bv_ref, bg_ref, a_ref, hc_ref, vb, gb):
        first = pl.program_id(0) == 0
        _stage(vb, vp_ref, v_ref, None, first, None)
        _stage(gb, gp_ref, g_ref, None, first, None)

        def strip(k, carry):
            for u in range(2):
                r = pl.multiple_of((2 * k + u) * FWD_STRIP, FWD_STRIP)
                val = _conv(_taps(vb, r, FWD_STRIP), wv_ref, bv_ref)
                gate = _conv(_taps(gb, r, FWD_STRIP), wg_ref, bg_ref)
                a_ref[pl.ds(r, FWD_STRIP), :] = (_gelu(gate) * val).astype(BF16)
                hc_ref[0, pl.ds(r, FWD_STRIP), :] = val.astype(BF16)
                hc_ref[1, pl.ds(r, FWD_STRIP), :] = gate.astype(BF16)
            return carry

        lax.fori_loop(0, FF_TOK // (2 * FWD_STRIP), strip, 0)

    buf = pltpu.VMEM((FF_TOK + 8, FF_COL), F32)
    outs, moved = _pcall(
        body, name, (t // FF_TOK, ncol),
        [tile(0), halo(0), tile(ncol), halo(ncol), wspec(0), wspec(ncol), bspec(0), bspec(ncol)],
        [pl.BlockSpec((FF_TOK, FF_COL), lambda i, j: (i, j)),
         pl.BlockSpec((2, FF_TOK, FF_COL), lambda i, j: (0, i, j))],
        [_sds((t, D_FF), BF16), _sds((2, t, D_FF), BF16)], [buf, buf],
        _cparams("arbitrary", "arbitrary"),
        (hu, hu, hu, hu, conv_w, conv_w, conv_b, conv_b), comm)
    return outs if comm is None else (*outs, moved)


def _ffn_gate_bwd(name, da, hu, hc, conv_w, comm=None):
    t = hu.shape[0]
    nt = t // FF_TOK
    ncol = D_FF // FF_COL
    hb = FF_TOK // 8

    def tile(off):
        return pl.BlockSpec((FF_TOK, FF_COL), lambda j, i: (i, j + off))

    def nxt_rows(i):
        return jnp.minimum((i + 1) * hb, nt * hb - 1)

    def wspec(off):
        return pl.BlockSpec((3, FF_COL), lambda j, i: (0, j + off))

    def body(da_ref, dan_ref, v_ref, g_ref, hc_ref, hcn_ref, wv_ref, wg_ref,
             dh_ref, dwv_ref, dwg_ref):
        i = pl.program_id(1)
        first, last = i == 0, i == nt - 1

        @pl.when(first)
        def _():
            dwv_ref[...] = jnp.zeros_like(dwv_ref)
            dwg_ref[...] = jnp.zeros_like(dwg_ref)

        def grads(dav, val, gate):
            g, dg = _gelu_and_grad(gate.astype(F32))
            dav = dav.astype(F32)
            return dav * g, dav * val.astype(F32) * dg

        def fold(x):
            return x[0:8] + x[8:16]

        def strip(j, carry):
            for u in range(2):
                carry = one_strip(2 * j + u, carry)
            return carry

        def one_strip(k, carry):
            r = pl.multiple_of(FF_TOK - BWD_STRIP - k * BWD_STRIP, BWD_STRIP)
            rows = pl.ds(r, BWD_STRIP)
            dval, dgate = grads(da_ref[rows, :], hc_ref[0, rows, :], hc_ref[1, rows, :])
            new = (dval[0:8], dgate[0:8])
            for half, (d, below, h_ref, w_ref, dw_ref) in enumerate((
                    (dval, carry[0], v_ref, wv_ref, dwv_ref),
                    (dgate, carry[1], g_ref, wg_ref, dwg_ref))):
                e = jnp.concatenate([d, below], axis=0)
                e1 = pltpu.roll(e, BWD_STRIP + 7, axis=0)[0:BWD_STRIP]
                e2 = pltpu.roll(e, BWD_STRIP + 6, axis=0)[0:BWD_STRIP]
                dh = w_ref[2:3, :] * d + w_ref[1:2, :] * e1 + w_ref[0:1, :] * e2
                dh_ref[half, rows, :] = dh.astype(BF16)
                huv = h_ref[rows, :].astype(F32)
                dw_ref[0:8, :] += fold(e2 * huv)
                dw_ref[8:16, :] += fold(e1 * huv)
                dw_ref[16:24, :] += fold(d * huv)
                dw_ref[24:32, :] += fold(d)
            return new

        dan = dan_ref[...]
        dan = jnp.where(last, jnp.zeros_like(dan), dan)
        lax.fori_loop(0, FF_TOK // (2 * BWD_STRIP), strip, grads(dan, hcn_ref[0], hcn_ref[1]))

        @pl.when(last)
        def _():
            for dw_ref in (dwv_ref, dwg_ref):
                for q in range(4):
                    dw_ref[8 * q:8 * q + 1, :] = jnp.sum(dw_ref[8 * q:8 * q + 8, :], axis=0,
                                                         keepdims=True)

    acc = pl.BlockSpec((32, FF_COL), lambda j, i: (0, j))
    (dhu, dwv, dwg), moved = _pcall(
        body, name, (ncol, nt),
        [tile(0), pl.BlockSpec((8, FF_COL), lambda j, i: (nxt_rows(i), j)),
         tile(0), tile(ncol),
         pl.BlockSpec((2, FF_TOK, FF_COL), lambda j, i: (0, i, j)),
         pl.BlockSpec((2, 8, FF_COL), lambda j, i: (0, nxt_rows(i), j)),
         wspec(0), wspec(ncol)],
        [pl.BlockSpec((2, FF_TOK, FF_COL), lambda j, i: (0, i, j)), acc, acc],
        [_sds((2, t, D_FF), BF16), _sds((32, D_FF), F32), _sds((32, D_FF), F32)],
        [], _cparams("arbitrary", "arbitrary"),
        (da, da, hu, hu, hc, hc, conv_w, conv_w), comm)
    dconv = jnp.concatenate([dwv, dwg], axis=1).reshape(4, 8, 2 * D_FF)[:, 0]
    return (dhu, dconv) if comm is None else (dhu, dconv, moved)


def _mesh_pos():
    x, y, c = lax.axis_index("x"), lax.axis_index("y"), lax.axis_index("c")
    return x, y, c, [(1 - x, y), (x, 1 - y), (1 - x, 1 - y)]


def _remote(src, dst, send_sems, recv_sems, i, dev):
    return pltpu.make_async_remote_copy(src_ref=src, dst_ref=dst, send_sem=send_sems.at[i],
                                        recv_sem=recv_sems.at[i], device_id=dev,
                                        device_id_type=MESH)


def _mine(c, rows):
    return pl.ds(pl.multiple_of(c * (rows // 2), 16), rows // 2)


def _gather_send(shards, conv_shard, gathered, l):
    nbig = len(shards)
    with_conv = conv_shard is not None
    if gathered is None:
        ins = list(shards) + ([conv_shard] if with_conv else [])
        outs = [_sds((DEPTH, N_CHIPS) + s.shape[1:], s.dtype) for s in ins]
        alias = {}
    else:
        ins = list(shards) + list(gathered)
        outs = [_sds(g.shape, g.dtype) for g in gathered]
        alias = {nbig + k: k for k in range(nbig)}

    def copies(cin, cout, ssem, rsem):
        x, y, c, chips = _mesh_pos()
        me = 2 * x + y
        out = []
        for k in range(nbig):
            rows = shards[k].shape[1]
            for j, (cx, cy) in enumerate(chips):
                out.append(_remote(cin[k].at[l, _mine(c, rows)], cout[k].at[l, me, _mine(c, rows)],
                                   ssem, rsem, 4 * k + j, (cx, cy, c)))
            out.append(_remote(cin[k].at[l], cout[k].at[l, me], ssem, rsem, 4 * k + 3,
                               (x, y, 1 - c)))
        if with_conv:
            base = 4 * nbig
            for j, (cx, cy) in enumerate(chips):
                out.append(_remote(cin[nbig].at[c], cout[nbig].at[c, me], ssem, rsem, base + j,
                                   (cx, cy, c)))
            for ll in range(DEPTH):
                out.append(_remote(cin[nbig].at[ll], cout[nbig].at[ll, me], ssem, rsem,
                                   base + 3 + ll, (x, y, 1 - c)))
        return out

    return _Comm(ins, outs, copies, 4 * nbig + 5, alias)


def _gather_forward(gathered, nbig, rows, l):
    with_conv = len(gathered) > nbig
    alias = {k: k for k in range(len(gathered))}

    def copies(cin, cout, ssem, rsem):
        x, y, c, chips = _mesh_pos()
        out = []
        for k in range(nbig):
            for j, (cx, cy) in enumerate(chips):
                blk = cout[k].at[l, 2 * cx + cy, _mine(c, rows[k])]
                out.append(_remote(blk, blk, ssem, rsem, 3 * k + j, (x, y, 1 - c)))
        if with_conv:
            for j, (cx, cy) in enumerate(chips):
                blk = cout[nbig].at[c, 2 * cx + cy]
                out.append(_remote(blk, blk, ssem, rsem, 3 * nbig + j, (x, y, 1 - c)))
        return out

    return _Comm(gathered, [_sds(g.shape, g.dtype) for g in gathered], copies, 3 * nbig + 3, alias)


def _reduce_swap(grads, l):
    def copies(cin, cout, ssem, rsem):
        x, y, c, _ = _mesh_pos()
        return [_remote(cin[k].at[l, :, _mine(1 - c, g.shape[2])], cout[k], ssem, rsem, k,
                        (x, y, 1 - c)) for k, g in enumerate(grads)]

    outs = [_sds((N_CHIPS, g.shape[2] // 2, g.shape[3]), g.dtype) for g in grads]
    return _Comm(grads, outs, copies, len(grads))


def _reduce_scatter(sums):
    def copies(cin, cout, ssem, rsem):
        x, y, c, chips = _mesh_pos()
        return [_remote(cin[k].at[2 * cx + cy], cout[k].at[j], ssem, rsem, 3 * k + j, (cx, cy, c))
                for k in range(len(sums)) for j, (cx, cy) in enumerate(chips)]

    outs = [_sds((3,) + s.shape[1:], s.dtype) for s in sums]
    return _Comm(sums, outs, copies, 3 * len(sums))


def _reduce_share(reds, l):
    def copies(cin, cout, ssem, rsem):
        x, y, c, _ = _mesh_pos()
        out = []
        for k, r in enumerate(reds):
            half = cout[k].at[l, _mine(c, r.shape[1])]
            out.append(_remote(half, half, ssem, rsem, k, (x, y, 1 - c)))
        return out

    return _Comm(reds, [_sds(r.shape, r.dtype) for r in reds], copies, len(reds),
                 {k: k for k in range(len(reds))})


def _allreduce_small(per_layer):
    kinds = len(per_layer[0])
    shapes = [a.shape[1:] if a.shape[0] == 1 else a.shape for a in per_layer[0]]

    def body(*refs):
        ins = refs[:DEPTH * kinds]
        outs = refs[DEPTH * kinds:(DEPTH + 1) * kinds]
        gbufs = refs[(DEPTH + 1) * kinds:(DEPTH + 2) * kinds]
        send_sems, recv_sems = refs[-2], refs[-1]
        x, y, c, chips = _mesh_pos()
        sibling = (x, y, 1 - c)

        def copy(k, i, block, to):
            px, py, pc = block
            slot = gbufs[k].at[4 * px + 2 * py + pc]
            return _remote(slot, slot, send_sems, recv_sems, 7 * k + i, to)

        me = (x, y, c)
        first, passed = [], []
        for k in range(kinds):
            for l in range(DEPTH):
                a = ins[l * kinds + k]
                if per_layer[l][k].shape[0] == 1:
                    gbufs[k][4 * x + 2 * y + c, l:l + 1] = a[...]
                else:
                    gbufs[k][4 * x + 2 * y + c, l] = a[...]
            first.append(copy(k, 0, me, sibling))
            first += [copy(k, 1 + j, me, (*chip, c)) for j, chip in enumerate(chips)]
            passed += [copy(k, 4 + j, (*chip, c), sibling) for j, chip in enumerate(chips)]
        for cp in first:
            cp.start()
        for k in range(kinds):
            for j, chip in enumerate(chips):
                copy(k, 1 + j, (*chip, c), me).wait_recv()
                passed[3 * k + j].start()
        for k in range(kinds):
            copy(k, 0, sibling, me).wait_recv()
            for j, chip in enumerate(chips):
                copy(k, 4 + j, (*chip, 1 - c), me).wait_recv()
        for cp in first + passed:
            cp.wait_send()
        for k in range(kinds):
            acc = gbufs[k][0]
            for d in range(1, 8):
                acc = acc + gbufs[k][d]
            outs[k][...] = acc

    vmem = pl.BlockSpec(memory_space=pltpu.VMEM)
    return pl.pallas_call(
        body, name="allreduce_small",
        in_specs=[vmem] * (DEPTH * kinds), out_specs=[vmem] * kinds,
        out_shape=[_sds((DEPTH,) + s, F32) for s in shapes],
        scratch_shapes=[pltpu.VMEM((8, DEPTH) + s, F32) for s in shapes]
        + [pltpu.SemaphoreType.DMA((7 * kinds,)), pltpu.SemaphoreType.DMA((7 * kinds,))],
        compiler_params=pltpu.CompilerParams(vmem_limit_bytes=VMEM_LIMIT_V7X),
    )(*per_layer[0], *per_layer[1])


def _adamw_small(ws, gs, ms, vs):
    n = len(ws)
    c1 = 1.0 - ADAM_B1 ** ADAM_STEP
    c2 = 1.0 - ADAM_B2 ** ADAM_STEP

    def body(*refs):
        for i in range(n):
            w_ref, g_ref, m_ref, v_ref = (refs[j * n + i] for j in range(4))
            d_ref, nm_ref, nv_ref = (refs[(4 + j) * n + i] for j in range(3))
            gv = g_ref[...]
            nm = ADAM_B1 * m_ref[...] + (1.0 - ADAM_B1) * gv
            nv = ADAM_B2 * v_ref[...] + (1.0 - ADAM_B2) * (gv * gv)
            nm_ref[...] = nm
            nv_ref[...] = nv
            d_ref[...] = -ADAM_LR * ((nm / c1) / (jnp.sqrt(nv / c2) + ADAM_EPS)
                                     + ADAM_WD * w_ref[...])

    vmem = pl.BlockSpec(memory_space=pltpu.VMEM)
    outs = pl.pallas_call(
        body, name="adamw_small", in_specs=[vmem] * (4 * n), out_specs=[vmem] * (3 * n),
        out_shape=[_sds(w.shape, F32) for w in ws] * 3,
        compiler_params=pltpu.CompilerParams(vmem_limit_bytes=VMEM_LIMIT_V7X),
    )(*ws, *gs, *ms, *vs)
    return outs[:n], outs[n:2 * n], outs[2 * n:]


def _core_index():
    return jnp.reshape(lax.axis_index("c"), (1,)).astype(jnp.int32)


def _chip_index():
    return jnp.reshape(2 * lax.axis_index("x") + lax.axis_index("y"), (1,)).astype(jnp.int32)


def _chip_sums(name, stacked, sibs, l):
    n = len(stacked)
    dims = [(s.shape[2] // 2, s.shape[3]) for s in stacked]

    def body(c_ref, *refs):
        for k in range(n):
            a_ref, b_ref, o_ref = refs[k], refs[n + k], refs[2 * n + k]
            o_ref[...] = (a_ref[...].astype(F32) + b_ref[...].astype(F32)).astype(BF16)

    return pl.pallas_call(
        body, name=name,
        grid_spec=pltpu.PrefetchScalarGridSpec(
            num_scalar_prefetch=1, grid=(N_CHIPS,),
            in_specs=[pl.BlockSpec((None, None, hr, cd), lambda j, cr: (l, j, cr[0], 0))
                      for hr, cd in dims]
            + [pl.BlockSpec((None, hr, cd), lambda j, cr: (j, 0, 0)) for hr, cd in dims],
            out_specs=[pl.BlockSpec((None, hr, cd), lambda j, cr: (j, 0, 0)) for hr, cd in dims]),
        out_shape=[_sds((N_CHIPS, hr, cd), BF16) for hr, cd in dims],
        compiler_params=_cparams("parallel"))(_core_index(), *stacked, *sibs)


def _final_sums(name, sums, recvs, l, fills):
    n = len(sums)
    dims = [(s.shape[1] // 2, s.shape[2]) for s in sums]
    filled = fills[0] is not None

    def body(m_ref, *refs):
        outs = refs[-n:]
        for k in range(n):
            acc = refs[k][...].astype(F32)
            for j in range(3):
                acc = acc + refs[n + k][j].astype(F32)
            outs[k][...] = acc

    in_specs = ([pl.BlockSpec((None, tr, cd), lambda i, mr: (mr[0], i, 0)) for tr, cd in dims]
                + [pl.BlockSpec((3, tr, cd), lambda i, mr: (0, i, 0)) for tr, cd in dims])
    args = [jnp.concatenate([_chip_index(), _core_index()]), *sums, *recvs]
    aliases = {}
    if filled:
        in_specs += [pl.BlockSpec(memory_space=pl.ANY)] * n
        args += list(fills)
        aliases = {1 + 2 * n + k: k for k in range(n)}
    return pl.pallas_call(
        body, name=name,
        grid_spec=pltpu.PrefetchScalarGridSpec(
            num_scalar_prefetch=1, grid=(2,), in_specs=in_specs,
            out_specs=[pl.BlockSpec((None, tr, cd), lambda i, mr: (l, 2 * mr[1] + i, 0))
                       for tr, cd in dims]),
        out_shape=[_sds((DEPTH, 4 * tr, cd), F32) for tr, cd in dims],
        input_output_aliases=aliases,
        compiler_params=_cparams("parallel"))(*args)


def _adamw(name, w, g, m, v, comm=None):
    nl, r, cdim = w.shape
    tr = r // 4 if r % 32 == 0 else r
    c1 = 1.0 - ADAM_B1 ** ADAM_STEP
    c2 = 1.0 - ADAM_B2 ** ADAM_STEP

    def body(w_ref, g_ref, m_ref, v_ref, d_ref, nm_ref, nv_ref):
        gv = g_ref[...]
        nm = ADAM_B1 * m_ref[...] + (1.0 - ADAM_B1) * gv
        nv = ADAM_B2 * v_ref[...] + (1.0 - ADAM_B2) * (gv * gv)
        nm_ref[...] = nm
        nv_ref[...] = nv
        d_ref[...] = -ADAM_LR * ((nm / c1) / (jnp.sqrt(nv / c2) + ADAM_EPS) + ADAM_WD * w_ref[...])

    spec = pl.BlockSpec((None, tr, cdim), lambda l, i: (l, i, 0))
    out = _sds(w.shape, F32)
    outs, moved = _pcall(body, name, (nl, r // tr), [spec] * 4, [spec] * 3, [out] * 3, [],
                         _cparams("arbitrary", "arbitrary"), (w, g, m, v), comm)
    return outs if comm is None else (*outs, moved)


def kernel(x, norm_mix_pre, w_in, b_gate, rel_bias, w_attn_out, w_pool_group, pool_scale, w_pool_out, w_o, norm_mix_post, norm_ffn_pre, w_up, conv_w, conv_b, w_down, norm_ffn_post, loss_target, m_norm_mix_pre, m_w_in, m_b_gate, m_rel_bias, m_w_attn_out, m_w_pool_group, m_pool_scale, m_w_pool_out, m_w_o, m_norm_mix_post, m_norm_ffn_pre, m_w_up, m_conv_w, m_conv_b, m_w_down, m_norm_ffn_post, v_norm_mix_pre, v_w_in, v_b_gate, v_rel_bias, v_w_attn_out, v_w_pool_group, v_pool_scale, v_w_pool_out, v_w_o, v_norm_mix_post, v_norm_ffn_pre, v_w_up, v_conv_w, v_conv_b, v_w_down, v_norm_ffn_post):
    t = x.shape[1]
    xs = x.reshape(t, D_MODEL)
    target = loss_target.reshape(t, D_MODEL)

    names = ["w_in", "w_attn_out", "w_pool_out", "w_o", "w_up", "w_down"]
    shards = [w.astype(BF16) for w in (w_in, w_attn_out, w_pool_out, w_o, w_up, w_down)]
    rows = [s.shape[1] for s in shards]
    nbig = len(shards)
    h, g = _norm_fwd("l0_norm_mix_pre", x.reshape(t, D_MODEL), norm_mix_pre[0:1],
                     _gather_send(shards[:1], conv_w, None, 0))
    g = _comm_call("gather0_forward", _gather_forward(g, 1, rows[:1], 0))
    cw_full = jnp.transpose(g[1], (0, 2, 1, 3)).reshape(DEPTH, 3, 2 * D_FF)
    g = g[:1]
    wg_bf = w_pool_group.astype(BF16)

    def views(gathered):
        win_g, wao_g, wpo_g, wo_g, wup_g, wdn_g = gathered
        return (win_g, wao_g, wpo_g, wo_g.reshape(DEPTH, D_MODEL, D_MODEL), wup_g,
                wdn_g.reshape(DEPTH, D_FF, D_MODEL))

    saved = []
    xcur = xs
    for l in range(DEPTH):
        tag = f"l{l}_"
        bias = _bias_table(tag + "bias_table", rel_bias[l])
        proj = _mm_nn_blocked(tag + "proj", h, g[0], l, BF16)
        if l == 0:
            att, probs, rest = _attn_fwd(tag + "attn_fwd", proj, bias,
                                         _gather_send(shards[1:], None, None, 0))
            pooled, mixed, rest = _pool_fwd(tag + "pool_fwd", proj, wg_bf[l], pool_scale[l:l + 1],
                                            _gather_forward(rest, nbig - 1, rows[1:], 0))
            g = g + rest
        else:
            att, probs = _attn_fwd(tag + "attn_fwd", proj, bias)
            pooled, mixed = _pool_fwd(tag + "pool_fwd", proj, wg_bf[l], pool_scale[l:l + 1])
        win_g, wao_g, wpo_g, wo_full, wup_g, wdn_full = views(g)
        ya = _narrow_nn(tag + "attn_out", att, wao_g, l)
        yb = _narrow_nn(tag + "pool_out", mixed, wpo_g, l)
        z = _gate_fwd(tag + "gate_fwd", proj, b_gate[l:l + 1], ya, yb)
        mix = _mm_nn(tag + "mix", z, wo_full, l, D_MODEL, F32)
        x1, h2 = _post_pre_fwd(tag + "norm_mix_post", xcur, mix, norm_mix_post[l:l + 1],
                               norm_ffn_pre[l:l + 1])
        if l == 0:
            hu, mixing = _mm_nn_blocked(tag + "ffn_up", h2, wup_g, l, BF16,
                                        _gather_send(shards[:4], None, g[:4], 1))
            a, hc, ffn_g = _ffn_gate_fwd(tag + "ffn_gate_fwd", hu, cw_full[l], conv_b[l:l + 1],
                                         _gather_send(shards[4:], None, g[4:], 1))
            g = mixing + ffn_g
            wdn_full = views(g)[5]
        else:
            hu = _mm_nn_blocked(tag + "ffn_up", h2, wup_g, l, BF16)
            a, hc = _ffn_gate_fwd(tag + "ffn_gate_fwd", hu, cw_full[l], conv_b[l:l + 1])
        f = _mm_nn(tag + "ffn_down", a, wdn_full, l, D_FF // 2, F32)
        saved.append(dict(x=xcur, h=h, proj=proj, att=att, pooled=pooled, mixed=mixed, ya=ya,
                          yb=yb, z=z, mix=mix, x1=x1, h2=h2, hu=hu, hc=hc, a=a, f=f, probs=probs))
        if l == 0:
            xcur, h, g = _post_pre_fwd(tag + "norm_ffn_post", x1, f, norm_ffn_post[l:l + 1],
                                       norm_mix_pre[l + 1:l + 2], _gather_forward(g, nbig, rows, 1))
        elif l < DEPTH - 1:
            xcur, h = _post_pre_fwd(tag + "norm_ffn_post", x1, f, norm_ffn_post[l:l + 1],
                                    norm_mix_pre[l + 1:l + 2])
    win_g, wao_g, wpo_g, wo_full, wup_g, wdn_full = views(g)

    dy, df, d_nfpost, loss_local = _tail("tail", saved[-1]["x1"], saved[-1]["f"],
                                         norm_ffn_post[DEPTH - 1:DEPTH], target)
    loss = lax.psum(loss_local, ("x", "y", "c"))

    dx = dy
    dws = dict.fromkeys(names)
    reds = [None] * nbig
    small_grads = [None] * DEPTH
    ffn = [4, 5]
    outs3 = [1, 2, 3]

    def blocks(ks):
        return [dws[names[k]].reshape(DEPTH, N_CHIPS, rows[k], -1) for k in ks]

    def chip_sums(ks, sib, l):
        return _chip_sums(f"chip_sums{l}_" + names[ks[0]], blocks(ks), sib, l)

    def final_sums(ks, sums, recv, l):
        outs = _final_sums(f"final_sums{l}_" + names[ks[0]], sums, recv, l, [reds[k] for k in ks])
        for k, r in zip(ks, outs):
            reds[k] = r

    for l in reversed(range(DEPTH)):
        tag = f"l{l}_"
        sv = saved[l]
        every = list(range(nbig))
        if l == 0:
            da, sib = _mm_nt(tag + "ffn_down_dx", df, wdn_full, l, D_FF // 2, BF16,
                             _reduce_swap(blocks(every), 1))
            sums = chip_sums(every, sib, 1)
        else:
            da = _mm_nt(tag + "ffn_down_dx", df, wdn_full, l, D_FF // 2, BF16)
        dws["w_down"] = _mm_tn(tag + "ffn_down_dw", sv["a"], df, D_FF // 2, l, dws["w_down"])
        if l == 0:
            dhu, dconv, recv = _ffn_gate_bwd(tag + "ffn_gate_bwd", da, sv["hu"], sv["hc"],
                                             cw_full[l], _reduce_scatter(sums))
            final_sums(every, sums, recv, 1)
            dh2, reds = _mm_nt_blocked(tag + "ffn_up_dx", dhu, wup_g, l, F32,
                                       _reduce_share(reds, 1))
        else:
            dhu, dconv = _ffn_gate_bwd(tag + "ffn_gate_bwd", da, sv["hu"], sv["hc"], cw_full[l])
            dh2 = _mm_nt_blocked(tag + "ffn_up_dx", dhu, wup_g, l, F32)
        dws["w_up"] = _mm_tn_blocked(tag + "ffn_up_dw", sv["h2"], dhu, l, dws["w_up"])
        if l == 0:
            dx1, d_nfpre, dmix, d_nmpost, sib = _pre_post_bwd(
                tag + "norm_ffn_pre_bwd", dh2, sv["x1"], dx, norm_ffn_pre[l:l + 1], sv["mix"],
                norm_mix_post[l:l + 1], _reduce_swap(blocks(ffn), 0))
            sums = chip_sums(ffn, sib, 0)
        else:
            dx1, d_nfpre, dmix, d_nmpost = _pre_post_bwd(
                tag + "norm_ffn_pre_bwd", dh2, sv["x1"], dx, norm_ffn_pre[l:l + 1], sv["mix"],
                norm_mix_post[l:l + 1])
        dz = _mm_nt(tag + "mix_dx", dmix, wo_full, l, D_MODEL, BF16)
        dws["w_o"] = _mm_tn(tag + "mix_dw", sv["z"], dmix, D_MODEL, l, dws["w_o"])
        dya, dyb, dgates, d_bgate = _gate_bwd(tag + "gate_bwd", dz, sv["proj"], b_gate[l:l + 1],
                                              sv["ya"], sv["yb"])
        datt = _narrow_nt(tag + "attn_out_dx", dya, wao_g, l)
        dws["w_attn_out"] = _narrow_tn(tag + "attn_out_dw", sv["att"], dya, l, dws["w_attn_out"])
        dmixed = _narrow_nt(tag + "pool_out_dx", dyb, wpo_g, l)
        dws["w_pool_out"] = _narrow_tn(tag + "pool_out_dw", sv["mixed"], dyb, l, dws["w_pool_out"])
        if l == 0:
            du, d_wg, d_pscale, sib = _pool_bwd(tag + "pool_bwd", dmixed, sv["pooled"], wg_bf[l],
                                                pool_scale[l:l + 1], _reduce_swap(blocks(outs3), 0))
            sums3 = chip_sums(outs3, sib, 0)
            dqkv, dbias, recv = _attn_bwd(
                tag + "attn_bwd", sv["proj"], datt, sv["probs"],
                _both(_reduce_scatter(sums), _reduce_scatter(sums3)))
            final_sums(ffn, sums, recv[:len(ffn)], 0)
            final_sums(outs3, sums3, recv[len(ffn):], 0)
        else:
            du, d_wg, d_pscale = _pool_bwd(tag + "pool_bwd", dmixed, sv["pooled"], wg_bf[l],
                                           pool_scale[l:l + 1])
            dqkv, dbias = _attn_bwd(tag + "attn_bwd", sv["proj"], datt, sv["probs"])
        d_rel = _bias_fold(tag + "bias_fold", dbias)
        if l == 0:
            dh, shared = _proj_dx(tag + "proj_dx", dqkv, du, dgates, win_g, l,
                                  _reduce_share([reds[k] for k in ffn + outs3], 0))
            for k, r in zip(ffn + outs3, shared):
                reds[k] = r
        else:
            dh = _proj_dx(tag + "proj_dx", dqkv, du, dgates, win_g, l)
        dws["w_in"] = _proj_dw(tag + "proj_dw", sv["h"], dqkv, du, dgates, l, dws["w_in"])
        small_grads[l] = [None, d_nmpost, d_nfpre, d_nfpost, d_bgate, d_rel, d_wg, d_pscale, dconv]
        if l > 0:
            dx, small_grads[l][0], df, d_nfpost = _pre_post_bwd(
                tag + "norm_mix_pre_bwd", dh, sv["x"], dx1, norm_mix_pre[l:l + 1],
                saved[l - 1]["f"], norm_ffn_post[l - 1:l])
        else:
            dx, small_grads[l][0], sib = _norm_pre_bwd(
                tag + "norm_mix_pre_bwd", dh, sv["x"], dx1, norm_mix_pre[l:l + 1],
                _reduce_swap(blocks([0]), 0))

    grad_x = dx.reshape(x.shape)

    delta, new_m, new_v = {}, {}, {}
    sums = chip_sums([0], sib, 0)
    delta["w_up"], new_m["w_up"], new_v["w_up"], recv = _adamw(
        "adamw_w_up", w_up, reds[4], m_w_up, v_w_up, _reduce_scatter(sums))
    final_sums([0], sums, recv, 0)
    delta["w_down"], new_m["w_down"], new_v["w_down"], shared = _adamw(
        "adamw_w_down", w_down, reds[5], m_w_down, v_w_down, _reduce_share([reds[0]], 0))
    g_big = shared + reds[1:]

    (g_nmpre, g_nmpost, g_nfpre, g_nfpost, g_bgate, g_rel, g_wg, g_pscale,
     g_conv) = _allreduce_small(small_grads)
    g_rel = g_rel[:, :, :N_REL]
    g_cb = g_conv[:, 3]
    ncw = conv_w.shape[2]
    chip = 2 * lax.axis_index("x") + lax.axis_index("y")
    g_cw = lax.dynamic_slice_in_dim(g_conv[:, 0:3], chip * ncw, ncw, axis=2)

    grads = dict(norm_mix_pre=g_nmpre, w_in=g_big[0], b_gate=g_bgate, rel_bias=g_rel,
                 w_attn_out=g_big[1], w_pool_group=g_wg, pool_scale=g_pscale, w_pool_out=g_big[2],
                 w_o=g_big[3], norm_mix_post=g_nmpost, norm_ffn_pre=g_nfpre, w_up=g_big[4],
                 conv_w=g_cw, conv_b=g_cb, w_down=g_big[5], norm_ffn_post=g_nfpost)
    weights = dict(norm_mix_pre=norm_mix_pre, w_in=w_in, b_gate=b_gate, rel_bias=rel_bias,
                   w_attn_out=w_attn_out, w_pool_group=w_pool_group, pool_scale=pool_scale,
                   w_pool_out=w_pool_out, w_o=w_o, norm_mix_post=norm_mix_post,
                   norm_ffn_pre=norm_ffn_pre, w_up=w_up, conv_w=conv_w, conv_b=conv_b,
                   w_down=w_down, norm_ffn_post=norm_ffn_post)
    moms = dict(norm_mix_pre=(m_norm_mix_pre, v_norm_mix_pre), w_in=(m_w_in, v_w_in),
                b_gate=(m_b_gate, v_b_gate), rel_bias=(m_rel_bias, v_rel_bias),
                w_attn_out=(m_w_attn_out, v_w_attn_out),
                w_pool_group=(m_w_pool_group, v_w_pool_group),
                pool_scale=(m_pool_scale, v_pool_scale), w_pool_out=(m_w_pool_out, v_w_pool_out),
                w_o=(m_w_o, v_w_o), norm_mix_post=(m_norm_mix_post, v_norm_mix_post),
                norm_ffn_pre=(m_norm_ffn_pre, v_norm_ffn_pre), w_up=(m_w_up, v_w_up),
                conv_w=(m_conv_w, v_conv_w), conv_b=(m_conv_b, v_conv_b),
                w_down=(m_w_down, v_w_down), norm_ffn_post=(m_norm_ffn_post, v_norm_ffn_post))
    order = list(weights.keys())

    small_names = [nm for nm in order if nm not in names]
    for nm in names:
        if nm not in delta:
            delta[nm], new_m[nm], new_v[nm] = _adamw("adamw_" + nm, weights[nm], grads[nm],
                                                     *moms[nm])
    d_s, m_s, v_s = _adamw_small([weights[nm] for nm in small_names],
                                 [grads[nm] for nm in small_names],
                                 [moms[nm][0] for nm in small_names],
                                 [moms[nm][1] for nm in small_names])
    for i, nm in enumerate(small_names):
        delta[nm], new_m[nm], new_v[nm] = d_s[i], m_s[i], v_s[i]

    return (loss, grad_x, *[grads[nm] for nm in order], *[delta[nm] for nm in order],
            *[new_m[nm] for nm in order], *[new_v[nm] for nm in order])
```

```python
import functools
import math

import jax
import jax.numpy as jnp
from jax import lax
from jax.experimental import pallas as pl
from jax.experimental.pallas import tpu as pltpu

F32 = jnp.float32
BF16 = jnp.bfloat16
MESH = pl.DeviceIdType.MESH

D_MODEL = 1024
DEPTH = 2
CHUNK = 64
BAND_CHUNKS = 9
BAND = BAND_CHUNKS * CHUNK
HEADS = 8
HEAD_DIM = 64
ATTN_W = HEADS * HEAD_DIM
POOL_WINDOWS = (2, 4, 8, 16)
POOL_W = 512
POOL_GD = 128
MAX_REL = 256
N_REL = 2 * MAX_REL + 1
D_FF = 2816
IN_W = 3 * ATTN_W + POOL_W + 2 * D_MODEL
EPS = 1e-6
ATTN_SCALE = HEAD_DIM ** -0.5
BAND_PAD = 640
BIAS_LANES = BAND_PAD
N_CHIPS = 4

ADAM_LR = 0.001
ADAM_B1 = 0.9
ADAM_B2 = 0.999
ADAM_EPS = 1e-08
ADAM_WD = 0.01
ADAM_STEP = 10

VMEM_LIMIT_V7X = 56 * 1024 * 1024
TOK = 512
ATT_BLK = 8 * CHUNK
FF_COL = 256
FF_TOK = 1024
HALO = 32


def _cparams(*sem):
    return pltpu.CompilerParams(dimension_semantics=sem, vmem_limit_bytes=VMEM_LIMIT_V7X)


def _sds(shape, dtype):
    return jax.ShapeDtypeStruct(shape, dtype)


class _Comm:
    def __init__(self, ins, outs, copies, n_sems, alias=None):
        self.ins, self.outs, self.copies, self.n_sems = list(ins), list(outs), copies, n_sems
        self.alias = dict(alias or {})


class _SemsFrom:
    def __init__(self, sems, start):
        self.sems, self.start = sems, start

    @property
    def at(self):
        return self

    def __getitem__(self, i):
        return self.sems.at[self.start + i]


def _both(a, b):
    na, nao = len(a.ins), len(a.outs)

    def copies(cin, cout, ssem, rsem):
        return (a.copies(cin[:na], cout[:nao], ssem, rsem)
                + b.copies(cin[na:], cout[nao:], _SemsFrom(ssem, a.n_sems), _SemsFrom(rsem, a.n_sems)))

    alias = dict(a.alias)
    alias.update({na + i: nao + o for i, o in b.alias.items()})
    return _Comm(a.ins + b.ins, a.outs + b.outs, copies, a.n_sems + b.n_sems, alias)


def _pcall(body, name, grid, in_specs, out_specs, out_shape, scratch_shapes, compiler_params, args,
           comm=None, aliases=None):
    single = not isinstance(out_shape, (list, tuple))
    out_specs = [out_specs] if single else list(out_specs)
    out_shape = [out_shape] if single else list(out_shape)
    n_in, n_out = len(in_specs), len(out_specs)
    aliases = dict(aliases or {})
    if comm is None:
        res = pl.pallas_call(
            body, name=name, grid=grid, in_specs=list(in_specs), out_specs=out_specs,
            out_shape=out_shape, scratch_shapes=list(scratch_shapes),
            input_output_aliases=aliases, compiler_params=compiler_params)(*args)
        return (res[0] if single else res), None
    ci, co = len(comm.ins), len(comm.outs)

    def hosted(*refs):
        main_in, cin = refs[:n_in], refs[n_in:n_in + ci]
        main_out = refs[n_in + ci:n_in + ci + n_out]
        cout = refs[n_in + ci + n_out:n_in + ci + n_out + co]
        rest = refs[n_in + ci + n_out + co:]
        copies = comm.copies(cin, cout, rest[-2], rest[-1])
        ids = [pl.program_id(a) for a in range(len(grid))]
        first = functools.reduce(jnp.logical_and, [i == 0 for i in ids])
        last = functools.reduce(jnp.logical_and, [i == g - 1 for i, g in zip(ids, grid)])

        @pl.when(first)
        def _():
            for cp in copies:
                cp.start()

        body(*main_in, *main_out, *rest[:-2])

        @pl.when(last)
        def _():
            for cp in copies:
                cp.wait()

    for i, o in comm.alias.items():
        aliases[n_in + i] = n_out + o
    hbm = pl.BlockSpec(memory_space=pl.ANY)
    sems = pltpu.SemaphoreType.DMA((comm.n_sems,))
    res = pl.pallas_call(
        hosted, name=name, grid=grid, in_specs=list(in_specs) + [hbm] * ci,
        out_specs=out_specs + [hbm] * co, out_shape=out_shape + comm.outs,
        scratch_shapes=list(scratch_shapes) + [sems, sems],
        input_output_aliases=aliases, compiler_params=compiler_params)(*args, *comm.ins)
    return (res[0] if single else list(res[:n_out])), list(res[n_out:])


def _comm_call(name, comm):
    ci = len(comm.ins)

    def body(*refs):
        copies = comm.copies(refs[:ci], refs[ci:-2], refs[-2], refs[-1])
        for cp in copies:
            cp.start()
        for cp in copies:
            cp.wait()

    hbm = pl.BlockSpec(memory_space=pl.ANY)
    sems = pltpu.SemaphoreType.DMA((comm.n_sems,))
    return list(pl.pallas_call(
        body, name=name, in_specs=[hbm] * ci, out_specs=[hbm] * len(comm.outs),
        out_shape=comm.outs, scratch_shapes=[sems, sems],
        input_output_aliases=comm.alias)(*comm.ins))


def _matmul(name, a, b, a_spec, b_spec, o_spec, out_shape, grid, contract, nk, acc_shape,
            fill=None, comm=None):
    def body(*refs):
        a_ref, b_ref = refs[0], refs[1]
        o_ref = refs[2 if fill is None else 3]
        scratch = refs[(3 if fill is None else 4):]
        part = lax.dot_general(a_ref[...], b_ref[...], (contract, ((), ())),
                               preferred_element_type=F32)
        if nk == 1:
            o_ref[...] = part.astype(o_ref.dtype)
        else:
            acc_ref = scratch[0]
            k = pl.program_id(2)

            @pl.when(k == 0)
            def _():
                acc_ref[...] = part

            @pl.when(k > 0)
            def _():
                acc_ref[...] += part

            @pl.when(k == nk - 1)
            def _():
                o_ref[...] = acc_ref[...].astype(o_ref.dtype)

    scratch = [] if nk == 1 else [pltpu.VMEM(acc_shape, F32)]
    in_specs, args, aliases = [a_spec, b_spec], [a, b], {}
    if fill is not None:
        in_specs.append(pl.BlockSpec(memory_space=pl.ANY))
        args.append(fill)
        aliases = {2: 0}
    out, moved = _pcall(body, name, grid, in_specs, o_spec, out_shape, scratch,
                        _cparams("parallel", "parallel", "arbitrary"), args, comm, aliases)
    return out if comm is None else (out, moved)


NN = ((1,), (0,))
NT = ((1,), (1,))
TN = ((0,), (0,))


def _tm(t):
    return min(t, 1024)


def _tt(t):
    return min(t, 2048)


def _col_block_spec(a, rows, nb, row_col):
    if a.ndim == 2:
        return pl.BlockSpec((rows, nb), row_col)

    def halves(*ids):
        r, c = row_col(*ids)
        return c // 2, r, c % 2

    return pl.BlockSpec((None, rows, nb), halves)


def _mm_nn_blocked(name, a, w, l, out_dtype, comm=None):
    t, k = a.shape
    nb = w.shape[3]
    tm = _tm(t)
    return _matmul(
        name, a, w,
        pl.BlockSpec((tm, k), lambda i, n, kk: (i, 0)),
        pl.BlockSpec((None, None, k, nb), lambda i, n, kk: (l, n, 0, 0)),
        pl.BlockSpec((tm, nb), lambda i, n, kk: (i, n)),
        _sds((t, N_CHIPS * nb), out_dtype), (t // tm, N_CHIPS, 1), NN, 1, None, comm=comm)


def _mm_nt_blocked(name, a, w, l, out_dtype, comm=None):
    t = a.shape[-2]
    k, nb = w.shape[2], w.shape[3]
    tm = _tm(t)
    return _matmul(
        name, a, w,
        _col_block_spec(a, tm, nb, lambda i, n, kk: (i, kk)),
        pl.BlockSpec((None, None, k, nb), lambda i, n, kk: (l, kk, 0, 0)),
        pl.BlockSpec((tm, k), lambda i, n, kk: (i, 0)),
        _sds((t, k), out_dtype), (t // tm, 1, N_CHIPS), NT, N_CHIPS, (tm, k), comm=comm)


def _mm_tn_blocked(name, a, g, l, fill):
    t, k = a.shape
    nb = g.shape[-1] * (g.ndim - 1) // N_CHIPS
    tt = _tt(t)
    nt = t // tt
    return _matmul(
        name, a, g,
        pl.BlockSpec((tt, k), lambda n, j, kk: (kk, 0)),
        _col_block_spec(g, tt, nb, lambda n, j, kk: (kk, n)),
        pl.BlockSpec((None, None, k, nb), lambda n, j, kk: (l, n, 0, 0)),
        _sds((DEPTH, N_CHIPS, k, nb), BF16), (N_CHIPS, 1, nt), TN, nt, (k, nb), fill)


def _proj_pieces(rows, dqkv_first):
    def piece(col):
        if dqkv_first:
            return pl.BlockSpec((rows, ATTN_W), lambda i, kk: (i, col))
        return pl.BlockSpec((rows, ATTN_W), lambda n, kk: (kk, col))
    return [piece(0), piece(1), piece(2), piece(0)]


def _proj_dx(name, dqkv, du, dgates, w, l, comm=None):
    t = du.shape[0]
    k, nb = w.shape[2], w.shape[3]
    tm = _tm(t)

    def body(dq_ref, dk_ref, dv_ref, du_ref, dg_ref, w_ref, o_ref, acc_ref):
        kk = pl.program_id(1)

        def mm(a):
            return lax.dot_general(a, w_ref[...], (NT, ((), ())), preferred_element_type=F32)

        @pl.when(kk == 0)
        def _():
            acc_ref[...] = mm(jnp.concatenate([dq_ref[...], dk_ref[...]], axis=1))

        @pl.when(kk == 1)
        def _():
            acc_ref[...] += mm(jnp.concatenate([dv_ref[...], du_ref[...]], axis=1))

        @pl.when(kk >= 2)
        def _():
            acc_ref[...] += mm(dg_ref[...])

        @pl.when(kk == N_CHIPS - 1)
        def _():
            o_ref[...] = acc_ref[...]

    out, moved = _pcall(
        body, name, (t // tm, N_CHIPS),
        _proj_pieces(tm, True)
        + [pl.BlockSpec((tm, nb), lambda i, kk: (i, jnp.maximum(kk - 2, 0))),
           pl.BlockSpec((None, None, k, nb), lambda i, kk: (l, kk, 0, 0))],
        pl.BlockSpec((tm, k), lambda i, kk: (i, 0)), _sds((t, k), F32),
        [pltpu.VMEM((tm, k), F32)], _cparams("arbitrary", "arbitrary"),
        (dqkv, dqkv, dqkv, du, dgates, w), comm)
    return out if comm is None else (out, moved)


def _proj_dw(name, h, dqkv, du, dgates, l, fill):
    t, k = h.shape
    nb = dgates.shape[1] // 2
    tt = _tm(t)
    nt = t // tt

    def body(*refs):
        h_ref, dq_ref, dk_ref, dv_ref, du_ref, dg_ref = refs[:6]
        o_ref, acc_ref = refs[-2], refs[-1]
        n, kk = pl.program_id(0), pl.program_id(1)

        def update(g):
            part = lax.dot_general(h_ref[...], g, (TN, ((), ())), preferred_element_type=F32)

            @pl.when(kk == 0)
            def _():
                acc_ref[...] = part

            @pl.when(kk > 0)
            def _():
                acc_ref[...] += part

        @pl.when(n == 0)
        def _():
            update(jnp.concatenate([dq_ref[...], dk_ref[...]], axis=1))

        @pl.when(n == 1)
        def _():
            update(jnp.concatenate([dv_ref[...], du_ref[...]], axis=1))

        @pl.when(n >= 2)
        def _():
            update(dg_ref[...])

        @pl.when(kk == nt - 1)
        def _():
            o_ref[...] = acc_ref[...].astype(BF16)

    in_specs = ([pl.BlockSpec((tt, k), lambda n, kk: (kk, 0))] + _proj_pieces(tt, False)
                + [pl.BlockSpec((tt, nb), lambda n, kk: (kk, jnp.maximum(n - 2, 0)))])
    args, aliases = [h, dqkv, dqkv, dqkv, du, dgates], {}
    if fill is not None:
        in_specs.append(pl.BlockSpec(memory_space=pl.ANY))
        args.append(fill)
        aliases = {6: 0}
    return pl.pallas_call(
        body, name=name, grid=(N_CHIPS, nt), in_specs=in_specs,
        out_specs=pl.BlockSpec((None, None, k, nb), lambda n, kk: (l, n, 0, 0)),
        out_shape=_sds((DEPTH, N_CHIPS, k, nb), BF16),
        scratch_shapes=[pltpu.VMEM((k, nb), F32)], input_output_aliases=aliases,
        compiler_params=_cparams("parallel", "arbitrary"))(*args)


def _narrow_nn(name, a, w, l):
    t, k = a.shape
    nb = w.shape[3]
    tm = _tm(t)

    def body(a_ref, w_ref, o_ref):
        av = a_ref[...]
        for j in range(N_CHIPS):
            o_ref[:, j * nb:(j + 1) * nb] = jnp.dot(
                av, w_ref[j], preferred_element_type=F32).astype(BF16)

    return pl.pallas_call(
        body, name=name, grid=(t // tm,),
        in_specs=[pl.BlockSpec((tm, k), lambda i: (i, 0)),
                  pl.BlockSpec((None, N_CHIPS, k, nb), lambda i: (l, 0, 0, 0))],
        out_specs=pl.BlockSpec((tm, N_CHIPS * nb), lambda i: (i, 0)),
        out_shape=_sds((t, N_CHIPS * nb), BF16), compiler_params=_cparams("parallel"))(a, w)


def _narrow_nt(name, a, w, l):
    t = a.shape[0]
    k, nb = w.shape[2], w.shape[3]
    tm = _tm(t)

    def body(a_ref, w_ref, o_ref):
        acc = lax.dot_general(a_ref[:, 0:nb], w_ref[0], (NT, ((), ())), preferred_element_type=F32)
        for j in range(1, N_CHIPS):
            acc = acc + lax.dot_general(a_ref[:, j * nb:(j + 1) * nb], w_ref[j], (NT, ((), ())),
                                        preferred_element_type=F32)
        o_ref[...] = acc.astype(BF16)

    return pl.pallas_call(
        body, name=name, grid=(t // tm,),
        in_specs=[pl.BlockSpec((tm, N_CHIPS * nb), lambda i: (i, 0)),
                  pl.BlockSpec((None, N_CHIPS, k, nb), lambda i: (l, 0, 0, 0))],
        out_specs=pl.BlockSpec((tm, k), lambda i: (i, 0)),
        out_shape=_sds((t, k), BF16), compiler_params=_cparams("parallel"))(a, w)


def _narrow_tn(name, a, g, l, fill):
    t, k = a.shape
    nb = g.shape[1] // N_CHIPS
    tt = _tm(t)
    nt = t // tt

    def body(*refs):
        a_ref, g_ref, o_ref, acc_ref = refs[0], refs[1], refs[-2], refs[-1]
        i = pl.program_id(0)
        part = lax.dot_general(a_ref[...], g_ref[...], (TN, ((), ())), preferred_element_type=F32)

        @pl.when(i == 0)
        def _():
            acc_ref[...] = part

        @pl.when(i > 0)
        def _():
            acc_ref[...] += part

        @pl.when(i == nt - 1)
        def _():
            for j in range(N_CHIPS):
                o_ref[j] = acc_ref[:, j * nb:(j + 1) * nb].astype(BF16)

    in_specs = [pl.BlockSpec((tt, k), lambda i: (i, 0)),
                pl.BlockSpec((tt, N_CHIPS * nb), lambda i: (i, 0))]
    args, aliases = [a, g], {}
    if fill is not None:
        in_specs.append(pl.BlockSpec(memory_space=pl.ANY))
        args.append(fill)
        aliases = {2: 0}
    return pl.pallas_call(
        body, name=name, grid=(nt,), in_specs=in_specs,
        out_specs=pl.BlockSpec((None, N_CHIPS, k, nb), lambda i: (l, 0, 0, 0)),
        out_shape=_sds((DEPTH, N_CHIPS, k, nb), BF16),
        scratch_shapes=[pltpu.VMEM((k, N_CHIPS * nb), F32)], input_output_aliases=aliases,
        compiler_params=_cparams("arbitrary"))(*args)


def _mm_nn(name, a, w, l, tk, out_dtype):
    t, k = a.shape
    n = w.shape[2]
    tm = _tm(t)
    nk = k // tk
    return _matmul(
        name, a, w,
        pl.BlockSpec((tm, tk), lambda i, j, kk: (i, kk)),
        pl.BlockSpec((None, tk, n), lambda i, j, kk: (l, kk, 0)),
        pl.BlockSpec((tm, n), lambda i, j, kk: (i, 0)),
        _sds((t, n), out_dtype), (t // tm, 1, nk), NN, nk, (tm, n))


def _mm_nt(name, a, w, l, tn, out_dtype, comm=None):
    t, n = a.shape
    k = w.shape[1]
    tm = _tm(t)
    return _matmul(
        name, a, w,
        pl.BlockSpec((tm, n), lambda i, j, kk: (i, 0)),
        pl.BlockSpec((None, tn, n), lambda i, j, kk: (l, j, 0)),
        pl.BlockSpec((tm, tn), lambda i, j, kk: (i, j)),
        _sds((t, k), out_dtype), (t // tm, k // tn, 1), NT, 1, None, comm=comm)


def _mm_tn(name, a, g, tko, l, fill):
    t, k = a.shape
    n = g.shape[1]
    tt = _tt(t)
    nt = t // tt
    return _matmul(
        name, a, g,
        pl.BlockSpec((tt, tko), lambda i, j, kk: (kk, i)),
        pl.BlockSpec((tt, n), lambda i, j, kk: (kk, 0)),
        pl.BlockSpec((None, tko, n), lambda i, j, kk: (l, i, 0)),
        _sds((DEPTH, k, n), BF16), (k // tko, 1, nt), TN, nt, (tko, n), fill)


def _row_spec(width, col=0):
    return pl.BlockSpec((TOK, width), lambda i: (i, col))


def _vec_spec(width):
    return pl.BlockSpec((1, width), lambda i: (0, 0))


def _rms(x):
    return lax.rsqrt(jnp.mean(x * x, axis=-1, keepdims=True) + EPS)


def _norm_fwd(name, x, g, comm=None):
    t = x.shape[0]

    def body(x_ref, g_ref, h_ref):
        xv = x_ref[...]
        h_ref[...] = (xv * _rms(xv) * g_ref[...]).astype(BF16)

    out, moved = _pcall(body, name, (t // TOK,), [_row_spec(D_MODEL), _vec_spec(D_MODEL)],
                        _row_spec(D_MODEL), _sds((t, D_MODEL), BF16), [], _cparams("arbitrary"),
                        (x, g), comm)
    return out if comm is None else (out, moved)


ROWS = 16
ROW_UNROLL = 8


def _rows(k):
    return pl.ds(pl.multiple_of(k * ROWS, ROWS), ROWS)


def _strips(step, init):
    def group(j, carry):
        for u in range(ROW_UNROLL):
            carry = step(j * ROW_UNROLL + u, carry)
        return carry

    return lax.fori_loop(0, TOK // (ROWS * ROW_UNROLL), group, init)


def _fold_rows(x):
    return x[0:8] + x[8:16]


def _accumulate(ref, part):
    total = jnp.sum(part, axis=0, keepdims=True)

    @pl.when(pl.program_id(0) == 0)
    def _():
        ref[...] = total

    @pl.when(pl.program_id(0) > 0)
    def _():
        ref[...] += total


def _norm_bwd_rows(d, mv, g):
    r = _rms(mv)
    n = mv * r
    dn = d * g
    return r * (dn - n * jnp.mean(dn * n, axis=-1, keepdims=True)), d * n


def _post_pre_fwd(name, xres, m, g_post, g_pre, comm=None):
    t = xres.shape[0]

    def body(x_ref, m_ref, gp_ref, gn_ref, x1_ref, h_ref):
        def strip(k, c):
            rows = _rows(k)
            mv = m_ref[rows, :]
            x1 = x_ref[rows, :] + mv * _rms(mv) * gp_ref[...]
            x1_ref[rows, :] = x1
            h_ref[rows, :] = (x1 * _rms(x1) * gn_ref[...]).astype(BF16)
            return c

        _strips(strip, 0)

    outs, moved = _pcall(
        body, name, (t // TOK,),
        [_row_spec(D_MODEL), _row_spec(D_MODEL), _vec_spec(D_MODEL), _vec_spec(D_MODEL)],
        [_row_spec(D_MODEL), _row_spec(D_MODEL)],
        [_sds((t, D_MODEL), F32), _sds((t, D_MODEL), BF16)], [], _cparams("arbitrary"),
        (xres, m, g_post, g_pre), comm)
    return outs if comm is None else (*outs, moved)


def _tail(name, xres, m, g_post, target):
    t = xres.shape[0]

    def body(x_ref, m_ref, g_ref, t_ref, dy_ref, dm_ref, dg_ref, l_ref):
        def strip(k, carry):
            rows = _rows(k)
            mv = m_ref[rows, :]
            e = x_ref[rows, :] + mv * _rms(mv) * g_ref[...] - t_ref[rows, :]
            dy = e * (1.0 / D_MODEL)
            dy_ref[rows, :] = dy
            dm, dgn = _norm_bwd_rows(dy, mv, g_ref[...])
            dm_ref[rows, :] = dm.astype(BF16)
            return carry[0] + _fold_rows(dgn), carry[1] + _fold_rows(e * e)

        zero = jnp.zeros((8, D_MODEL), F32)
        dg, sq = _strips(strip, (zero, zero))
        _accumulate(dg_ref, dg)
        _accumulate(l_ref, jnp.sum(sq, axis=1, keepdims=True))

    dy, dm, dg, sq = pl.pallas_call(
        body, name=name, grid=(t // TOK,),
        in_specs=[_row_spec(D_MODEL), _row_spec(D_MODEL), _vec_spec(D_MODEL), _row_spec(D_MODEL)],
        out_specs=[_row_spec(D_MODEL), _row_spec(D_MODEL), _vec_spec(D_MODEL),
                   pl.BlockSpec((1, 1), lambda i: (0, 0))],
        out_shape=[_sds((t, D_MODEL), F32), _sds((t, D_MODEL), BF16), _sds((1, D_MODEL), F32),
                   _sds((1, 1), F32)],
        compiler_params=_cparams("arbitrary"))(xres, m, g_post, target)
    return dy, dm, dg, sq[0, 0] * (0.5 / D_MODEL)


def _pre_post_bwd(name, dh, xin, dxo, g_pre, m, g_post, comm=None):
    t = dh.shape[0]

    def body(dh_ref, x_ref, d_ref, gq_ref, m_ref, gp_ref, dx_ref, dgq_ref, dm_ref, dgp_ref):
        def strip(k, carry):
            rows = _rows(k)
            dxin, dgq = _norm_bwd_rows(dh_ref[rows, :], x_ref[rows, :], gq_ref[...])
            dx = d_ref[rows, :] + dxin
            dx_ref[rows, :] = dx
            dm, dgp = _norm_bwd_rows(dx, m_ref[rows, :], gp_ref[...])
            dm_ref[rows, :] = dm.astype(BF16)
            return carry[0] + _fold_rows(dgq), carry[1] + _fold_rows(dgp)

        zero = jnp.zeros((8, D_MODEL), F32)
        dgq, dgp = _strips(strip, (zero, zero))
        _accumulate(dgq_ref, dgq)
        _accumulate(dgp_ref, dgp)

    outs, moved = _pcall(
        body, name, (t // TOK,),
        [_row_spec(D_MODEL), _row_spec(D_MODEL), _row_spec(D_MODEL), _vec_spec(D_MODEL),
         _row_spec(D_MODEL), _vec_spec(D_MODEL)],
        [_row_spec(D_MODEL), _vec_spec(D_MODEL), _row_spec(D_MODEL), _vec_spec(D_MODEL)],
        [_sds((t, D_MODEL), F32), _sds((1, D_MODEL), F32), _sds((t, D_MODEL), BF16),
         _sds((1, D_MODEL), F32)], [], _cparams("arbitrary"),
        (dh, xin, dxo, g_pre, m, g_post), comm)
    return outs if comm is None else (*outs, moved)


def _norm_pre_bwd(name, dh, xin, dxo, g, comm=None):
    t = dh.shape[0]

    def body(dh_ref, x_ref, d_ref, g_ref, dx_ref, dg_ref):
        xv = x_ref[...]
        dhv = dh_ref[...]
        r = _rms(xv)
        n = xv * r
        dn = dhv * g_ref[...]
        dx_ref[...] = d_ref[...] + r * (dn - n * jnp.mean(dn * n, axis=-1, keepdims=True))
        part = jnp.sum(dhv * n, axis=0, keepdims=True)

        @pl.when(pl.program_id(0) == 0)
        def _():
            dg_ref[...] = part

        @pl.when(pl.program_id(0) > 0)
        def _():
            dg_ref[...] += part

    out, moved = _pcall(
        body, name, (t // TOK,),
        [_row_spec(D_MODEL), _row_spec(D_MODEL), _row_spec(D_MODEL), _vec_spec(D_MODEL)],
        [_row_spec(D_MODEL), _vec_spec(D_MODEL)],
        [_sds((t, D_MODEL), F32), _sds((1, D_MODEL), F32)], [], _cparams("arbitrary"),
        (dh, xin, dxo, g), comm)
    return out if comm is None else (*out, moved)


def _gate_fwd(name, proj, b_gate, ya, yb):
    t = proj.shape[0]

    def body(ga_ref, gb_ref, b_ref, ya_ref, yb_ref, z_ref):
        def strip(k, c):
            rows = _rows(k)
            sa = jax.nn.sigmoid(ga_ref[rows, :].astype(F32) + b_ref[:, :D_MODEL])
            sb = jax.nn.sigmoid(gb_ref[rows, :].astype(F32) + b_ref[:, D_MODEL:])
            z_ref[rows, :] = (sa * ya_ref[rows, :].astype(F32)
                              + sb * yb_ref[rows, :].astype(F32)).astype(BF16)
            return c

        _strips(strip, 0)

    return pl.pallas_call(
        body, name=name, grid=(t // TOK,),
        in_specs=[_row_spec(D_MODEL, 2), _row_spec(D_MODEL, 3), _vec_spec(2 * D_MODEL),
                  _row_spec(D_MODEL), _row_spec(D_MODEL)],
        out_specs=_row_spec(D_MODEL), out_shape=_sds((t, D_MODEL), BF16),
        compiler_params=_cparams("parallel"))(proj, proj, b_gate, ya, yb)


def _gate_bwd(name, dz, proj, b_gate, ya, yb):
    t = proj.shape[0]

    def body(dz_ref, ga_ref, gb_ref, b_ref, ya_ref, yb_ref, dya_ref, dyb_ref, dg_ref, db_ref):
        def strip(k, carry):
            rows = _rows(k)
            dzv = dz_ref[rows, :].astype(F32)
            sa = jax.nn.sigmoid(ga_ref[rows, :].astype(F32) + b_ref[:, :D_MODEL])
            sb = jax.nn.sigmoid(gb_ref[rows, :].astype(F32) + b_ref[:, D_MODEL:])
            dya_ref[rows, :] = (dzv * sa).astype(BF16)
            dyb_ref[rows, :] = (dzv * sb).astype(BF16)
            dga = dzv * ya_ref[rows, :].astype(F32) * sa * (1.0 - sa)
            dgb = dzv * yb_ref[rows, :].astype(F32) * sb * (1.0 - sb)
            dg_ref[rows, :D_MODEL] = dga.astype(BF16)
            dg_ref[rows, D_MODEL:] = dgb.astype(BF16)
            return carry[0] + _fold_rows(dga), carry[1] + _fold_rows(dgb)

        zero = jnp.zeros((8, D_MODEL), F32)
        pa, pb = _strips(strip, (zero, zero))
        _accumulate(db_ref.at[:, :D_MODEL], pa)
        _accumulate(db_ref.at[:, D_MODEL:], pb)

    return pl.pallas_call(
        body, name=name, grid=(t // TOK,),
        in_specs=[_row_spec(D_MODEL), _row_spec(D_MODEL, 2), _row_spec(D_MODEL, 3),
                  _vec_spec(2 * D_MODEL), _row_spec(D_MODEL), _row_spec(D_MODEL)],
        out_specs=[_row_spec(D_MODEL), _row_spec(D_MODEL), _row_spec(2 * D_MODEL),
                   _vec_spec(2 * D_MODEL)],
        out_shape=[_sds((t, D_MODEL), BF16), _sds((t, D_MODEL), BF16),
                   _sds((t, 2 * D_MODEL), BF16), _sds((1, 2 * D_MODEL), F32)],
        compiler_params=_cparams("arbitrary"))(dz, proj, proj, b_gate, ya, yb)


def _head_masks():
    lane = lax.broadcasted_iota(jnp.int32, (1, 2 * HEAD_DIM), 1)
    return lane < HEAD_DIM


BAND_ROWS = 2 * ATT_BLK + CHUNK


def _fill_band(band, prev_ref, cur_ref):
    band[0:ATT_BLK, :] = prev_ref[...]
    band[ATT_BLK:2 * ATT_BLK, :] = cur_ref[...]
    band[2 * ATT_BLK:, :] = jnp.zeros((CHUNK, ATTN_W), BF16)


def _pair_rows(x2, low):
    zero = jnp.zeros_like(x2)
    return jnp.concatenate([jnp.where(low, x2, zero), jnp.where(low, zero, x2)], axis=0)


def _pair_diag(o2, low):
    return jnp.where(low, o2[0:CHUNK, :], o2[CHUNK:, :])


N_PAIRS = HEADS // 2
SM_STRIP = 32
N_STRIPS = BAND_PAD // SM_STRIP
NEG = -1e30


def _strip(k):
    return pl.ds(pl.multiple_of(k * SM_STRIP, SM_STRIP), SM_STRIP)


def _band_probs(k2, qcat, bias_t, first_key):
    kpos = lax.broadcasted_iota(jnp.int32, (BAND_PAD, 1), 0)
    st = lax.dot_general(k2, qcat, (NT, ((), ())), preferred_element_type=F32)
    st = jnp.where(kpos + first_key >= 0, st + bias_t, NEG)
    e = jnp.exp(st - jnp.max(st, axis=0, keepdims=True))
    return e * (1.0 / jnp.sum(e, axis=0, keepdims=True))


def _attn_specs(nblk):
    cur = lambda col: pl.BlockSpec((ATT_BLK, ATTN_W), lambda s: (jnp.minimum(s, nblk - 1), col))
    prev = lambda col: pl.BlockSpec(
        (ATT_BLK, ATTN_W), lambda s: (jnp.maximum(jnp.minimum(s, nblk - 1) - 1, 0), col))
    return cur, prev


def _attn_fwd(name, proj, bias, comm=None):
    t = proj.shape[0]
    nblk = t // ATT_BLK
    cur, prev = _attn_specs(nblk)

    def body(q_ref, kp_ref, kc_ref, vp_ref, vc_ref, b_ref, o_ref, p_ref, kband, vband):
        s = pl.program_id(0)
        _fill_band(kband, kp_ref, kc_ref)
        _fill_band(vband, vp_ref, vc_ref)
        low = _head_masks()

        def chunk(ci, carry):
            r0 = pl.multiple_of(ci * CHUNK, CHUNK)
            for hp in range(N_PAIRS):
                cols = slice(hp * 128, (hp + 1) * 128)
                qcat = _pair_rows(q_ref[pl.ds(r0, CHUNK), cols] * ATTN_SCALE, low)
                p = _band_probs(kband[pl.ds(r0, BAND_PAD), cols], qcat, b_ref[hp],
                                (s * 8 - 8 + ci) * CHUNK).astype(BF16)
                p_ref[ci, hp] = p
                o2 = lax.dot_general(p, vband[pl.ds(r0, BAND_PAD), cols],
                                     (TN, ((), ())), preferred_element_type=F32)
                o_ref[pl.ds(r0, CHUNK), cols] = _pair_diag(o2, low).astype(BF16)
            return carry

        lax.fori_loop(0, 8, chunk, 0)

    outs, moved = _pcall(
        body, name, (nblk,),
        [cur(0), prev(1), cur(1), prev(2), cur(2),
         pl.BlockSpec((N_PAIRS, BAND_PAD, 128), lambda s: (0, 0, 0))],
        [pl.BlockSpec((ATT_BLK, ATTN_W), lambda s: (s, 0)),
         pl.BlockSpec((8, N_PAIRS, BAND_PAD, 128), lambda s: (s, 0, 0, 0))],
        [_sds((t, ATTN_W), BF16), _sds((t // CHUNK, N_PAIRS, BAND_PAD, 128), BF16)],
        [pltpu.VMEM((BAND_ROWS, ATTN_W), BF16), pltpu.VMEM((BAND_ROWS, ATTN_W), BF16)],
        _cparams("arbitrary"), (proj, proj, proj, proj, proj, bias), comm)
    return outs if comm is None else (*outs, moved)


def _attn_bwd(name, proj, datt, att, probs, comm=None):
    t = proj.shape[0]
    nblk = t // ATT_BLK
    cur, prev = _attn_specs(nblk)
    late = pl.BlockSpec((ATT_BLK, 3 * ATTN_W), lambda s: (jnp.maximum(s - 1, 0), 0))

    def body(q_ref, kp_ref, kc_ref, vp_ref, vc_ref, do_ref, o_ref, p_ref,
             dqkv_ref, db_ref, kband, vband, dkacc, dvacc,
             dp_ref, dsb_ref, qc_ref, dc_ref, dq_ref, dq_held):
        s = pl.program_id(0)

        @pl.when(s == 0)
        def _():
            dkacc[...] = jnp.zeros_like(dkacc)
            dvacc[...] = jnp.zeros_like(dvacc)
            db_ref[...] = jnp.zeros_like(db_ref)
            dq_ref[...] = jnp.zeros_like(dq_ref)

        @pl.when(s < nblk)
        def _():
            _fill_band(kband, kp_ref, kc_ref)
            _fill_band(vband, vp_ref, vc_ref)
            low = _head_masks()

            def chunk(ci, carry):
                r0 = pl.multiple_of(ci * CHUNK, CHUNK)
                delta = []
                for hp in range(N_PAIRS):
                    cols = slice(hp * 128, (hp + 1) * 128)
                    qc_ref[hp] = _pair_rows(q_ref[pl.ds(r0, CHUNK), cols] * ATTN_SCALE, low)
                    docat = _pair_rows(do_ref[pl.ds(r0, CHUNK), cols], low)
                    dc_ref[hp] = docat
                    dp_ref[hp] = lax.dot_general(vband[pl.ds(r0, BAND_PAD), cols], docat,
                                                 (NT, ((), ())), preferred_element_type=F32)
                    o2 = o_ref[pl.ds(r0, CHUNK), cols].astype(F32)
                    prod = docat.astype(F32) * jnp.concatenate([o2, o2], axis=0)
                    delta.append(lax.dot_general(
                        jnp.ones((8, 128), F32), prod, (NT, ((), ())), preferred_element_type=F32,
                        precision=lax.Precision.HIGHEST)[0:1])

                def grads(k, c):
                    rows = _strip(k)
                    for hp in range(N_PAIRS):
                        ds = (p_ref[ci, hp, rows, :].astype(F32)
                              * (dp_ref[hp, rows, :] - delta[hp]))
                        db_ref[hp, rows, :] += ds
                        dsb_ref[hp, rows, :] = ds.astype(BF16)
                    return c

                lax.fori_loop(0, N_STRIPS, grads, 0)
                for hp in range(N_PAIRS):
                    cols = slice(hp * 128, (hp + 1) * 128)
                    dq2 = lax.dot_general(dsb_ref[hp], kband[pl.ds(r0, BAND_PAD), cols],
                                          (TN, ((), ())), preferred_element_type=F32)
                    dq_ref[pl.ds(r0, CHUNK), cols] = (_pair_diag(dq2, low) * ATTN_SCALE).astype(BF16)
                    dkacc[pl.ds(r0, BAND_PAD), cols] += jnp.dot(dsb_ref[hp], qc_ref[hp],
                                                               preferred_element_type=F32)
                    dvacc[pl.ds(r0, BAND_PAD), cols] += jnp.dot(p_ref[ci, hp], dc_ref[hp],
                                                               preferred_element_type=F32)
                return carry

            dq_held[...] = dq_ref[...]
            lax.fori_loop(0, 8, chunk, 0)

        @pl.when(s == nblk)
        def _():
            dq_held[...] = dq_ref[...]

        dqkv_ref[:, 0:ATTN_W] = dq_held[...]
        dqkv_ref[:, ATTN_W:2 * ATTN_W] = dkacc[0:ATT_BLK, :].astype(BF16)
        dqkv_ref[:, 2 * ATTN_W:] = dvacc[0:ATT_BLK, :].astype(BF16)
        dkacc[0:ATT_BLK, :] = dkacc[ATT_BLK:2 * ATT_BLK, :]
        dvacc[0:ATT_BLK, :] = dvacc[ATT_BLK:2 * ATT_BLK, :]
        dkacc[ATT_BLK:, :] = jnp.zeros((ATT_BLK + CHUNK, ATTN_W), F32)
        dvacc[ATT_BLK:, :] = jnp.zeros((ATT_BLK + CHUNK, ATTN_W), F32)

    outs, moved = _pcall(
        body, name, (nblk + 1,),
        [cur(0), prev(1), cur(1), prev(2), cur(2),
         pl.BlockSpec((ATT_BLK, ATTN_W), lambda s: (jnp.minimum(s, nblk - 1), 0)),
         pl.BlockSpec((ATT_BLK, ATTN_W), lambda s: (jnp.minimum(s, nblk - 1), 0)),
         pl.BlockSpec((8, N_PAIRS, BAND_PAD, 128), lambda s: (jnp.minimum(s, nblk - 1), 0, 0, 0))],
        [late, pl.BlockSpec((HEADS // 2, BAND_PAD, 128), lambda s: (0, 0, 0))],
        [_sds((t, 3 * ATTN_W), BF16), _sds((HEADS // 2, BAND_PAD, 128), F32)],
        [pltpu.VMEM((BAND_ROWS, ATTN_W), BF16), pltpu.VMEM((BAND_ROWS, ATTN_W), BF16),
         pltpu.VMEM((BAND_ROWS, ATTN_W), F32), pltpu.VMEM((BAND_ROWS, ATTN_W), F32),
         pltpu.VMEM((N_PAIRS, BAND_PAD, 128), F32), pltpu.VMEM((N_PAIRS, BAND_PAD, 128), BF16),
         pltpu.VMEM((N_PAIRS, 2 * CHUNK, 128), BF16), pltpu.VMEM((N_PAIRS, 2 * CHUNK, 128), BF16),
         pltpu.VMEM((ATT_BLK, ATTN_W), BF16), pltpu.VMEM((ATT_BLK, ATTN_W), BF16)],
        _cparams("arbitrary"), (proj, proj, proj, proj, proj, datt, att, probs), comm)
    return outs if comm is None else (*outs, moved)


def _diag_onehot(rel_rows):
    d0 = lax.broadcasted_iota(jnp.int32, (BIAS_LANES, BIAS_LANES), 0)
    d1 = lax.broadcasted_iota(jnp.int32, (BIAS_LANES, BIAS_LANES), 1)
    m, n = (d0, d1) if rel_rows else (d1, d0)
    hit = (m == jnp.minimum(BAND - 1 + MAX_REL - n, 2 * MAX_REL)) & (n < BAND + CHUNK - 1)
    return jnp.where(hit, 1.0, 0.0).astype(F32)


def _bias_table(name, rel_bias_l):
    rel_pad = jnp.pad(rel_bias_l, ((0, 0), (0, BIAS_LANES - N_REL)))

    def body(r_ref, o_ref):
        diag = jnp.dot(r_ref[...], _diag_onehot(True), preferred_element_type=F32,
                       precision=lax.Precision.HIGHEST)
        rowid = lax.broadcasted_iota(jnp.int32, (8, BIAS_LANES), 0)
        lane = lax.broadcasted_iota(jnp.int32, (8, BIAS_LANES), 1)
        for h in range(HEADS):
            d8 = jnp.broadcast_to(diag[h:h + 1, :], (8, BIAS_LANES))
            slab0 = pltpu.roll(d8, BIAS_LANES - CHUNK + 1, axis=1)
            for b in range(1, 8):
                slab0 = jnp.where(rowid == b, pltpu.roll(d8, BIAS_LANES - CHUNK + 1 + b, axis=1),
                                  slab0)
            for a in range(8):
                slab = slab0 if a == 0 else pltpu.roll(slab0, 8 * a, axis=1)
                o_ref[h * CHUNK + 8 * a:h * CHUNK + 8 * a + 8, :] = jnp.where(lane < BAND, slab, NEG)

    tab = pl.pallas_call(
        body, name=name,
        in_specs=[pl.BlockSpec(memory_space=pltpu.VMEM)],
        out_specs=pl.BlockSpec(memory_space=pltpu.VMEM),
        out_shape=_sds((HEADS * CHUNK, BIAS_LANES), F32),
    )(rel_pad)
    tab = tab.reshape(HEADS // 2, 2, CHUNK, BIAS_LANES)
    return jnp.transpose(tab, (0, 3, 1, 2)).reshape(HEADS // 2, BIAS_LANES, 2 * CHUNK)


def _bias_fold(name, dbias_t):
    rows = HEADS * CHUNK
    dbias = jnp.transpose(dbias_t.reshape(HEADS // 2, BIAS_LANES, 2, CHUNK), (0, 2, 3, 1))

    def body(d_ref, o_ref):
        rowid = lax.broadcasted_iota(jnp.int32, (8, BIAS_LANES), 0)
        diags = []
        for h in range(HEADS):
            acc = d_ref[h * CHUNK + 56:h * CHUNK + 64, :]
            for a in range(7):
                slab = d_ref[h * CHUNK + 8 * a:h * CHUNK + 8 * a + 8, :]
                acc = acc + pltpu.roll(slab, 56 - 8 * a, axis=1)
            tot = jnp.where(rowid == 7, acc, 0.0)
            for b in range(7):
                tot = tot + jnp.where(rowid == b, pltpu.roll(acc, 7 - b, axis=1), 0.0)
            diags.append(jnp.sum(tot, axis=0, keepdims=True))
        diag = jnp.concatenate(diags, axis=0)
        o_ref[...] = jnp.dot(diag, _diag_onehot(False), preferred_element_type=F32,
                             precision=lax.Precision.HIGHEST)

    return pl.pallas_call(
        body, name=name,
        in_specs=[pl.BlockSpec(memory_space=pltpu.VMEM)],
        out_specs=pl.BlockSpec(memory_space=pltpu.VMEM),
        out_shape=_sds((HEADS, BIAS_LANES), F32),
    )(dbias.reshape(rows, BIAS_LANES))


def _inv_counts(i):
    trow = lax.broadcasted_iota(jnp.int32, (TOK + HALO, 1), 0) + i * TOK
    return [1.0 / jnp.minimum(trow + 1, w).astype(F32) for w in POOL_WINDOWS]


def _pool_fwd(name, proj, wg, scale, comm=None):
    t = proj.shape[0]
    hb = TOK // HALO

    def body(u_ref, up_ref, wg_ref, sc_ref, pooled_ref, mixed_ref, b0, b1, b2, b3):
        i = pl.program_id(0)
        halo = up_ref[...].astype(F32)
        b0[0:HALO, :] = jnp.where(i == 0, jnp.zeros_like(halo), halo)
        b0[HALO:, :] = u_ref[...].astype(F32)
        n = TOK + HALO
        b1[8:n, :] = b0[8:n, :] + b0[7:n - 1, :]
        b2[16:n, 128:] = b1[16:n, 128:] + b1[14:n - 2, 128:]
        b3[24:n, 256:] = b2[24:n, 256:] + b2[20:n - 4, 256:]
        wins = [b1[HALO:n, 0:128], b2[HALO:n, 128:256], b3[HALO:n, 256:384],
                b3[HALO:n, 384:512] + b3[HALO - 8:n - 8, 384:512]]
        inv = _inv_counts(i)
        for g in range(4):
            cols = slice(g * POOL_GD, (g + 1) * POOL_GD)
            pooled = (wins[g] * inv[g][0:TOK] - b0[HALO:n, cols]).astype(BF16)
            pooled_ref[:, cols] = pooled
            pre = jnp.dot(pooled, wg_ref[g], preferred_element_type=F32)
            mixed_ref[:, cols] = (pre * sc_ref[:, cols]).astype(BF16)

    buf = pltpu.VMEM((TOK + HALO, POOL_W), F32)
    outs, moved = _pcall(
        body, name, (t // TOK,),
        [_row_spec(POOL_W, 3),
         pl.BlockSpec((HALO, POOL_W), lambda i: (jnp.maximum(i * hb - 1, 0), 3)),
         pl.BlockSpec((4, POOL_GD, POOL_GD), lambda i: (0, 0, 0)), _vec_spec(POOL_W)],
        [_row_spec(POOL_W), _row_spec(POOL_W)],
        [_sds((t, POOL_W), BF16), _sds((t, POOL_W), BF16)], [buf, buf, buf, buf],
        _cparams("arbitrary"), (proj, proj, wg, scale), comm)
    return outs if comm is None else (*outs, moved)


def _pool_bwd(name, dmixed, pooled, wg, scale, comm=None):
    t = dmixed.shape[0]
    nt = t // TOK
    hb = TOK // HALO

    def body(dm_ref, dmn_ref, p_ref, wg_ref, sc_ref, du_ref, dwg_ref, dsc_ref, c0, c1, c2, c3):
        i = pl.program_id(0)

        @pl.when(i == 0)
        def _():
            dwg_ref[...] = jnp.zeros_like(dwg_ref)
            dsc_ref[...] = jnp.zeros_like(dsc_ref)

        n = TOK + HALO
        inv = _inv_counts(i)
        dmv = dm_ref[...].astype(F32)
        dmn = dmn_ref[...].astype(F32)
        dmn = jnp.where(i == nt - 1, jnp.zeros_like(dmn), dmn)
        for g in range(4):
            cols = slice(g * POOL_GD, (g + 1) * POOL_GD)
            scg = sc_ref[:, cols]
            pg = p_ref[:, cols]
            dpre = (dmv[:, cols] * scg).astype(BF16)
            dpre_n = (dmn[:, cols] * scg).astype(BF16)
            pre = jnp.dot(pg, wg_ref[g], preferred_element_type=F32)
            dsc_ref[:, cols] += jnp.sum(dmv[:, cols] * pre, axis=0, keepdims=True)
            dwg_ref[g] += lax.dot_general(pg, dpre, (TN, ((), ())), preferred_element_type=F32)
            dpool = lax.dot_general(dpre, wg_ref[g], (NT, ((), ())), preferred_element_type=F32)
            dpool_n = lax.dot_general(dpre_n, wg_ref[g], (NT, ((), ())),
                                      preferred_element_type=F32)
            c0[0:TOK, cols] = dpool
            c0[TOK:n, cols] = dpool_n
            c1[0:TOK, cols] = dpool * inv[g][0:TOK]
            c1[TOK:n, cols] = dpool_n * inv[g][TOK:n]
        c2[0:n - 8, :] = c1[0:n - 8, :] + c1[1:n - 7, :]
        c3[0:n - 16, 128:] = c2[0:n - 16, 128:] + c2[2:n - 14, 128:]
        c1[0:n - 24, 256:] = c3[0:n - 24, 256:] + c3[4:n - 20, 256:]
        wins = [c2[0:TOK, 0:128], c3[0:TOK, 128:256], c1[0:TOK, 256:384],
                c1[0:TOK, 384:512] + c1[8:TOK + 8, 384:512]]
        for g in range(4):
            cols = slice(g * POOL_GD, (g + 1) * POOL_GD)
            du_ref[:, cols] = (wins[g] - c0[0:TOK, cols]).astype(BF16)

    buf = pltpu.VMEM((TOK + HALO, POOL_W), F32)
    outs, moved = _pcall(
        body, name, (nt,),
        [_row_spec(POOL_W),
         pl.BlockSpec((HALO, POOL_W), lambda i: (jnp.minimum((i + 1) * hb, nt * hb - 1), 0)),
         _row_spec(POOL_W), pl.BlockSpec((4, POOL_GD, POOL_GD), lambda i: (0, 0, 0)),
         _vec_spec(POOL_W)],
        [_row_spec(POOL_W), pl.BlockSpec((4, POOL_GD, POOL_GD), lambda i: (0, 0, 0)),
         _vec_spec(POOL_W)],
        [_sds((t, POOL_W), BF16), _sds((4, POOL_GD, POOL_GD), F32), _sds((1, POOL_W), F32)],
        [buf, buf, buf, buf], _cparams("arbitrary"), (dmixed, dmixed, pooled, wg, scale), comm)
    return outs if comm is None else (*outs, moved)


GELU_C = math.sqrt(2.0 / math.pi)


GELU_K = 0.044715


def _gelu_parts(x):
    x2 = x * x
    s = 0.5 + 0.5 * jnp.tanh(x * (GELU_C + (GELU_C * GELU_K) * x2))
    return x * s, s, x2


def _gelu(x):
    return _gelu_parts(x)[0]


def _gelu_and_grad(x):
    g, s, x2 = _gelu_parts(x)
    return g, s + g * (1.0 - s) * ((2 * GELU_C) + (6 * GELU_C * GELU_K) * x2)


def _taps(buf, r, rows):
    a = buf[pl.ds(r, rows + 8), :]
    return a[8:], pltpu.roll(a, 1, axis=0)[8:], pltpu.roll(a, 2, axis=0)[8:]


def _conv(taps, w_ref, b_ref):
    return b_ref[...] + w_ref[2:3, :] * taps[0] + w_ref[1:2, :] * taps[1] + w_ref[0:1, :] * taps[2]


def _stage(dst, prev_ref, cur_ref, next_ref, first, last):
    rows = cur_ref.shape[0]
    h = prev_ref[...].astype(F32)
    dst[0:8, :] = jnp.where(first, jnp.zeros_like(h), h)
    dst[8:8 + rows, :] = cur_ref[...].astype(F32)
    if next_ref is not None:
        h = next_ref[...].astype(F32)
        dst[8 + rows:, :] = jnp.where(last, jnp.zeros_like(h), h)


FWD_STRIP = 32
BWD_STRIP = 16


def _ffn_gate_fwd(name, hu, conv_w, conv_b, comm=None):
    t = hu.shape[0]
    ncol = D_FF // FF_COL
    hb = FF_TOK // 8

    def tile(off):
        return pl.BlockSpec((FF_TOK, FF_COL), lambda i, j: (i, j + off))

    def halo(off):
        return pl.BlockSpec((8, FF_COL), lambda i, j: (jnp.maximum(i * hb - 1, 0), j + off))

    def wspec(off):
        return pl.BlockSpec((3, FF_COL), lambda i, j: (0, j + off))

    def bspec(off):
        return pl.BlockSpec((1, FF_COL), lambda i, j: (0, j + off))

    def body(v_ref, vp_ref, g_ref, gp_ref, wv_ref, wg_ref, bv_ref, bg_ref, a_ref, hc_ref, vb, gb):
        first = pl.program_id(0) == 0
        _stage(vb, vp_ref, v_ref, None, first, None)
        _stage(gb, gp_ref, g_ref, None, first, None)

        def strip(k, carry):
            for u in range(2):
                r = pl.multiple_of((2 * k + u) * FWD_STRIP, FWD_STRIP)
                val = _conv(_taps(vb, r, FWD_STRIP), wv_ref, bv_ref)
                gate = _conv(_taps(gb, r, FWD_STRIP), wg_ref, bg_ref)
                a_ref[pl.ds(r, FWD_STRIP), :] = (_gelu(gate) * val).astype(BF16)
                hc_ref[0, pl.ds(r, FWD_STRIP), :] = val.astype(BF16)
                hc_ref[1, pl.ds(r, FWD_STRIP), :] = gate.astype(BF16)
            return carry

        lax.fori_loop(0, FF_TOK // (2 * FWD_STRIP), strip, 0)

    buf = pltpu.VMEM((FF_TOK + 8, FF_COL), F32)
    outs, moved = _pcall(
        body, name, (t // FF_TOK, ncol),
        [tile(0), halo(0), tile(ncol), halo(ncol), wspec(0), wspec(ncol), bspec(0), bspec(ncol)],
        [pl.BlockSpec((FF_TOK, FF_COL), lambda i, j: (i, j)),
         pl.BlockSpec((2, FF_TOK, FF_COL), lambda i, j: (0, i, j))],
        [_sds((t, D_FF), BF16), _sds((2, t, D_FF), BF16)], [buf, buf],
        _cparams("arbitrary", "arbitrary"),
        (hu, hu, hu, hu, conv_w, conv_w, conv_b, conv_b), comm)
    return outs if comm is None else (*outs, moved)


def _ffn_gate_bwd(name, da, hu, hc, conv_w, comm=None):
    t = hu.shape[0]
    nt = t // FF_TOK
    ncol = D_FF // FF_COL
    hb = FF_TOK // 8

    def tile(off):
        return pl.BlockSpec((FF_TOK, FF_COL), lambda j, i: (i, j + off))

    def nxt_rows(i):
        return jnp.minimum((i + 1) * hb, nt * hb - 1)

    def wspec(off):
        return pl.BlockSpec((3, FF_COL), lambda j, i: (0, j + off))

    def body(da_ref, dan_ref, v_ref, g_ref, hc_ref, hcn_ref, wv_ref, wg_ref,
             dh_ref, dwv_ref, dwg_ref):
        i = pl.program_id(1)
        first, last = i == 0, i == nt - 1

        @pl.when(first)
        def _():
            dwv_ref[...] = jnp.zeros_like(dwv_ref)
            dwg_ref[...] = jnp.zeros_like(dwg_ref)

        def grads(dav, val, gate):
            g, dg = _gelu_and_grad(gate.astype(F32))
            dav = dav.astype(F32)
            return dav * g, dav * val.astype(F32) * dg

        def fold(x):
            return x[0:8] + x[8:16]

        def strip(j, carry):
            for u in range(2):
                carry = one_strip(2 * j + u, carry)
            return carry

        def one_strip(k, carry):
            r = pl.multiple_of(FF_TOK - BWD_STRIP - k * BWD_STRIP, BWD_STRIP)
            rows = pl.ds(r, BWD_STRIP)
            dval, dgate = grads(da_ref[rows, :], hc_ref[0, rows, :], hc_ref[1, rows, :])
            new = (dval[0:8], dgate[0:8])
            for half, (d, below, h_ref, w_ref, dw_ref) in enumerate((
                    (dval, carry[0], v_ref, wv_ref, dwv_ref),
                    (dgate, carry[1], g_ref, wg_ref, dwg_ref))):
                e = jnp.concatenate([d, below], axis=0)
                e1 = pltpu.roll(e, BWD_STRIP + 7, axis=0)[0:BWD_STRIP]
                e2 = pltpu.roll(e, BWD_STRIP + 6, axis=0)[0:BWD_STRIP]
                dh = w_ref[2:3, :] * d + w_ref[1:2, :] * e1 + w_ref[0:1, :] * e2
                dh_ref[half, rows, :] = dh.astype(BF16)
                huv = h_ref[rows, :].astype(F32)
                dw_ref[0:8, :] += fold(e2 * huv)
                dw_ref[8:16, :] += fold(e1 * huv)
                dw_ref[16:24, :] += fold(d * huv)
                dw_ref[24:32, :] += fold(d)
            return new

        dan = dan_ref[...]
        dan = jnp.where(last, jnp.zeros_like(dan), dan)
        lax.fori_loop(0, FF_TOK // (2 * BWD_STRIP), strip, grads(dan, hcn_ref[0], hcn_ref[1]))

        @pl.when(last)
        def _():
            for dw_ref in (dwv_ref, dwg_ref):
                for q in range(4):
                    dw_ref[8 * q:8 * q + 1, :] = jnp.sum(dw_ref[8 * q:8 * q + 8, :], axis=0,
                                                         keepdims=True)

    acc = pl.BlockSpec((32, FF_COL), lambda j, i: (0, j))
    (dhu, dwv, dwg), moved = _pcall(
        body, name, (ncol, nt),
        [tile(0), pl.BlockSpec((8, FF_COL), lambda j, i: (nxt_rows(i), j)),
         tile(0), tile(ncol),
         pl.BlockSpec((2, FF_TOK, FF_COL), lambda j, i: (0, i, j)),
         pl.BlockSpec((2, 8, FF_COL), lambda j, i: (0, nxt_rows(i), j)),
         wspec(0), wspec(ncol)],
        [pl.BlockSpec((2, FF_TOK, FF_COL), lambda j, i: (0, i, j)), acc, acc],
        [_sds((2, t, D_FF), BF16), _sds((32, D_FF), F32), _sds((32, D_FF), F32)],
        [], _cparams("arbitrary", "arbitrary"),
        (da, da, hu, hu, hc, hc, conv_w, conv_w), comm)
    dconv = jnp.concatenate([dwv, dwg], axis=1).reshape(4, 8, 2 * D_FF)[:, 0]
    return (dhu, dconv) if comm is None else (dhu, dconv, moved)


def _mesh_pos():
    x, y, c = lax.axis_index("x"), lax.axis_index("y"), lax.axis_index("c")
    return x, y, c, [(1 - x, y), (x, 1 - y), (1 - x, 1 - y)]


def _remote(src, dst, send_sems, recv_sems, i, dev):
    return pltpu.make_async_remote_copy(src_ref=src, dst_ref=dst, send_sem=send_sems.at[i],
                                        recv_sem=recv_sems.at[i], device_id=dev,
                                        device_id_type=MESH)


def _mine(c, rows):
    return pl.ds(pl.multiple_of(c * (rows // 2), 16), rows // 2)


def _gather_send(shards, conv_shard, gathered, l):
    nbig = len(shards)
    with_conv = conv_shard is not None
    if gathered is None:
        ins = list(shards) + ([conv_shard] if with_conv else [])
        outs = [_sds((DEPTH, N_CHIPS) + s.shape[1:], s.dtype) for s in ins]
        alias = {}
    else:
        ins = list(shards) + list(gathered)
        outs = [_sds(g.shape, g.dtype) for g in gathered]
        alias = {nbig + k: k for k in range(nbig)}

    def copies(cin, cout, ssem, rsem):
        x, y, c, chips = _mesh_pos()
        me = 2 * x + y
        out = []
        for k in range(nbig):
            rows = shards[k].shape[1]
            for j, (cx, cy) in enumerate(chips):
                out.append(_remote(cin[k].at[l, _mine(c, rows)], cout[k].at[l, me, _mine(c, rows)],
                                   ssem, rsem, 4 * k + j, (cx, cy, c)))
            out.append(_remote(cin[k].at[l], cout[k].at[l, me], ssem, rsem, 4 * k + 3,
                               (x, y, 1 - c)))
        if with_conv:
            base = 4 * nbig
            for j, (cx, cy) in enumerate(chips):
                out.append(_remote(cin[nbig].at[c], cout[nbig].at[c, me], ssem, rsem, base + j,
                                   (cx, cy, c)))
            for ll in range(DEPTH):
                out.append(_remote(cin[nbig].at[ll], cout[nbig].at[ll, me], ssem, rsem,
                                   base + 3 + ll, (x, y, 1 - c)))
        return out

    return _Comm(ins, outs, copies, 4 * nbig + 5, alias)


def _gather_forward(gathered, nbig, rows, l):
    with_conv = len(gathered) > nbig
    alias = {k: k for k in range(len(gathered))}

    def copies(cin, cout, ssem, rsem):
        x, y, c, chips = _mesh_pos()
        out = []
        for k in range(nbig):
            for j, (cx, cy) in enumerate(chips):
                blk = cout[k].at[l, 2 * cx + cy, _mine(c, rows[k])]
                out.append(_remote(blk, blk, ssem, rsem, 3 * k + j, (x, y, 1 - c)))
        if with_conv:
            for j, (cx, cy) in enumerate(chips):
                blk = cout[nbig].at[c, 2 * cx + cy]
                out.append(_remote(blk, blk, ssem, rsem, 3 * nbig + j, (x, y, 1 - c)))
        return out

    return _Comm(gathered, [_sds(g.shape, g.dtype) for g in gathered], copies, 3 * nbig + 3, alias)


def _reduce_swap(grads, l):
    def copies(cin, cout, ssem, rsem):
        x, y, c, _ = _mesh_pos()
        return [_remote(cin[k].at[l, :, _mine(1 - c, g.shape[2])], cout[k], ssem, rsem, k,
                        (x, y, 1 - c)) for k, g in enumerate(grads)]

    outs = [_sds((N_CHIPS, g.shape[2] // 2, g.shape[3]), g.dtype) for g in grads]
    return _Comm(grads, outs, copies, len(grads))


def _reduce_scatter(sums):
    def copies(cin, cout, ssem, rsem):
        x, y, c, chips = _mesh_pos()
        return [_remote(cin[k].at[2 * cx + cy], cout[k].at[j], ssem, rsem, 3 * k + j, (cx, cy, c))
                for k in range(len(sums)) for j, (cx, cy) in enumerate(chips)]

    outs = [_sds((3,) + s.shape[1:], s.dtype) for s in sums]
    return _Comm(sums, outs, copies, 3 * len(sums))


def _reduce_share(reds, l):
    def copies(cin, cout, ssem, rsem):
        x, y, c, _ = _mesh_pos()
        out = []
        for k, r in enumerate(reds):
            half = cout[k].at[l, _mine(c, r.shape[1])]
            out.append(_remote(half, half, ssem, rsem, k, (x, y, 1 - c)))
        return out

    return _Comm(reds, [_sds(r.shape, r.dtype) for r in reds], copies, len(reds),
                 {k: k for k in range(len(reds))})


def _allreduce_small(per_layer):
    kinds = len(per_layer[0])
    shapes = [a.shape[1:] if a.shape[0] == 1 else a.shape for a in per_layer[0]]

    def body(*refs):
        ins = refs[:DEPTH * kinds]
        outs = refs[DEPTH * kinds:(DEPTH + 1) * kinds]
        gbufs = refs[(DEPTH + 1) * kinds:(DEPTH + 2) * kinds]
        send_sems, recv_sems = refs[-2], refs[-1]
        x, y, c, chips = _mesh_pos()
        sibling = (x, y, 1 - c)

        def copy(k, i, block, to):
            px, py, pc = block
            slot = gbufs[k].at[4 * px + 2 * py + pc]
            return _remote(slot, slot, send_sems, recv_sems, 7 * k + i, to)

        me = (x, y, c)
        first, passed = [], []
        for k in range(kinds):
            for l in range(DEPTH):
                a = ins[l * kinds + k]
                if per_layer[l][k].shape[0] == 1:
                    gbufs[k][4 * x + 2 * y + c, l:l + 1] = a[...]
                else:
                    gbufs[k][4 * x + 2 * y + c, l] = a[...]
            first.append(copy(k, 0, me, sibling))
            first += [copy(k, 1 + j, me, (*chip, c)) for j, chip in enumerate(chips)]
            passed += [copy(k, 4 + j, (*chip, c), sibling) for j, chip in enumerate(chips)]
        for cp in first:
            cp.start()
        for k in range(kinds):
            for j, chip in enumerate(chips):
                copy(k, 1 + j, (*chip, c), me).wait_recv()
                passed[3 * k + j].start()
        for k in range(kinds):
            copy(k, 0, sibling, me).wait_recv()
            for j, chip in enumerate(chips):
                copy(k, 4 + j, (*chip, 1 - c), me).wait_recv()
        for cp in first + passed:
            cp.wait_send()
        for k in range(kinds):
            acc = gbufs[k][0]
            for d in range(1, 8):
                acc = acc + gbufs[k][d]
            outs[k][...] = acc

    vmem = pl.BlockSpec(memory_space=pltpu.VMEM)
    return pl.pallas_call(
        body, name="allreduce_small",
        in_specs=[vmem] * (DEPTH * kinds), out_specs=[vmem] * kinds,
        out_shape=[_sds((DEPTH,) + s, F32) for s in shapes],
        scratch_shapes=[pltpu.VMEM((8, DEPTH) + s, F32) for s in shapes]
        + [pltpu.SemaphoreType.DMA((7 * kinds,)), pltpu.SemaphoreType.DMA((7 * kinds,))],
        compiler_params=pltpu.CompilerParams(vmem_limit_bytes=VMEM_LIMIT_V7X),
    )(*per_layer[0], *per_layer[1])


def _adamw_small(ws, gs, ms, vs):
    n = len(ws)
    c1 = 1.0 - ADAM_B1 ** ADAM_STEP
    c2 = 1.0 - ADAM_B2 ** ADAM_STEP

    def body(*refs):
        for i in range(n):
            w_ref, g_ref, m_ref, v_ref = (refs[j * n + i] for j in range(4))
            d_ref, nm_ref, nv_ref = (refs[(4 + j) * n + i] for j in range(3))
            gv = g_ref[...]
            nm = ADAM_B1 * m_ref[...] + (1.0 - ADAM_B1) * gv
            nv = ADAM_B2 * v_ref[...] + (1.0 - ADAM_B2) * (gv * gv)
            nm_ref[...] = nm
            nv_ref[...] = nv
            d_ref[...] = -ADAM_LR * ((nm / c1) / (jnp.sqrt(nv / c2) + ADAM_EPS)
                                     + ADAM_WD * w_ref[...])

    vmem = pl.BlockSpec(memory_space=pltpu.VMEM)
    outs = pl.pallas_call(
        body, name="adamw_small", in_specs=[vmem] * (4 * n), out_specs=[vmem] * (3 * n),
        out_shape=[_sds(w.shape, F32) for w in ws] * 3,
        compiler_params=pltpu.CompilerParams(vmem_limit_bytes=VMEM_LIMIT_V7X),
    )(*ws, *gs, *ms, *vs)
    return outs[:n], outs[n:2 * n], outs[2 * n:]


def _core_index():
    return jnp.reshape(lax.axis_index("c"), (1,)).astype(jnp.int32)


def _chip_index():
    return jnp.reshape(2 * lax.axis_index("x") + lax.axis_index("y"), (1,)).astype(jnp.int32)


def _chip_sums(name, stacked, sibs, l):
    n = len(stacked)
    dims = [(s.shape[2] // 2, s.shape[3]) for s in stacked]

    def body(c_ref, *refs):
        for k in range(n):
            a_ref, b_ref, o_ref = refs[k], refs[n + k], refs[2 * n + k]
            o_ref[...] = (a_ref[...].astype(F32) + b_ref[...].astype(F32)).astype(BF16)

    return pl.pallas_call(
        body, name=name,
        grid_spec=pltpu.PrefetchScalarGridSpec(
            num_scalar_prefetch=1, grid=(N_CHIPS,),
            in_specs=[pl.BlockSpec((None, None, hr, cd), lambda j, cr: (l, j, cr[0], 0))
                      for hr, cd in dims]
            + [pl.BlockSpec((None, hr, cd), lambda j, cr: (j, 0, 0)) for hr, cd in dims],
            out_specs=[pl.BlockSpec((None, hr, cd), lambda j, cr: (j, 0, 0)) for hr, cd in dims]),
        out_shape=[_sds((N_CHIPS, hr, cd), BF16) for hr, cd in dims],
        compiler_params=_cparams("parallel"))(_core_index(), *stacked, *sibs)


def _final_sums(name, sums, recvs, l, fills):
    n = len(sums)
    dims = [(s.shape[1] // 2, s.shape[2]) for s in sums]
    filled = fills[0] is not None

    def body(m_ref, *refs):
        outs = refs[-n:]
        for k in range(n):
            acc = refs[k][...].astype(F32)
            for j in range(3):
                acc = acc + refs[n + k][j].astype(F32)
            outs[k][...] = acc

    in_specs = ([pl.BlockSpec((None, tr, cd), lambda i, mr: (mr[0], i, 0)) for tr, cd in dims]
                + [pl.BlockSpec((3, tr, cd), lambda i, mr: (0, i, 0)) for tr, cd in dims])
    args = [jnp.concatenate([_chip_index(), _core_index()]), *sums, *recvs]
    aliases = {}
    if filled:
        in_specs += [pl.BlockSpec(memory_space=pl.ANY)] * n
        args += list(fills)
        aliases = {1 + 2 * n + k: k for k in range(n)}
    return pl.pallas_call(
        body, name=name,
        grid_spec=pltpu.PrefetchScalarGridSpec(
            num_scalar_prefetch=1, grid=(2,), in_specs=in_specs,
            out_specs=[pl.BlockSpec((None, tr, cd), lambda i, mr: (l, 2 * mr[1] + i, 0))
                       for tr, cd in dims]),
        out_shape=[_sds((DEPTH, 4 * tr, cd), F32) for tr, cd in dims],
        input_output_aliases=aliases,
        compiler_params=_cparams("parallel"))(*args)


def _adamw(name, w, g, m, v, comm=None):
    nl, r, cdim = w.shape
    tr = r // 4 if r % 32 == 0 else r
    c1 = 1.0 - ADAM_B1 ** ADAM_STEP
    c2 = 1.0 - ADAM_B2 ** ADAM_STEP

    def body(w_ref, g_ref, m_ref, v_ref, d_ref, nm_ref, nv_ref):
        gv = g_ref[...]
        nm = ADAM_B1 * m_ref[...] + (1.0 - ADAM_B1) * gv
        nv = ADAM_B2 * v_ref[...] + (1.0 - ADAM_B2) * (gv * gv)
        nm_ref[...] = nm
        nv_ref[...] = nv
        d_ref[...] = -ADAM_LR * ((nm / c1) / (jnp.sqrt(nv / c2) + ADAM_EPS) + ADAM_WD * w_ref[...])

    spec = pl.BlockSpec((None, tr, cdim), lambda l, i: (l, i, 0))
    out = _sds(w.shape, F32)
    outs, moved = _pcall(body, name, (nl, r // tr), [spec] * 4, [spec] * 3, [out] * 3, [],
                         _cparams("arbitrary", "arbitrary"), (w, g, m, v), comm)
    return outs if comm is None else (*outs, moved)


def kernel(x, norm_mix_pre, w_in, b_gate, rel_bias, w_attn_out, w_pool_group, pool_scale, w_pool_out, w_o, norm_mix_post, norm_ffn_pre, w_up, conv_w, conv_b, w_down, norm_ffn_post, loss_target, m_norm_mix_pre, m_w_in, m_b_gate, m_rel_bias, m_w_attn_out, m_w_pool_group, m_pool_scale, m_w_pool_out, m_w_o, m_norm_mix_post, m_norm_ffn_pre, m_w_up, m_conv_w, m_conv_b, m_w_down, m_norm_ffn_post, v_norm_mix_pre, v_w_in, v_b_gate, v_rel_bias, v_w_attn_out, v_w_pool_group, v_pool_scale, v_w_pool_out, v_w_o, v_norm_mix_post, v_norm_ffn_pre, v_w_up, v_conv_w, v_conv_b, v_w_down, v_norm_ffn_post):
    t = x.shape[1]
    xs = x.reshape(t, D_MODEL)
    target = loss_target.reshape(t, D_MODEL)

    names = ["w_in", "w_attn_out", "w_pool_out", "w_o", "w_up", "w_down"]
    shards = [w.astype(BF16) for w in (w_in, w_attn_out, w_pool_out, w_o, w_up, w_down)]
    rows = [s.shape[1] for s in shards]
    nbig = len(shards)
    h, g = _norm_fwd("l0_norm_mix_pre", x.reshape(t, D_MODEL), norm_mix_pre[0:1],
                     _gather_send(shards[:1], conv_w, None, 0))
    g = _comm_call("gather0_forward", _gather_forward(g, 1, rows[:1], 0))
    cw_full = jnp.transpose(g[1], (0, 2, 1, 3)).reshape(DEPTH, 3, 2 * D_FF)
    g = g[:1]
    wg_bf = w_pool_group.astype(BF16)

    def views(gathered):
        win_g, wao_g, wpo_g, wo_g, wup_g, wdn_g = gathered
        return (win_g, wao_g, wpo_g, wo_g.reshape(DEPTH, D_MODEL, D_MODEL), wup_g,
                wdn_g.reshape(DEPTH, D_FF, D_MODEL))

    saved = []
    xcur = xs
    for l in range(DEPTH):
        tag = f"l{l}_"
        bias = _bias_table(tag + "bias_table", rel_bias[l])
        proj = _mm_nn_blocked(tag + "proj", h, g[0], l, BF16)
        if l == 0:
            att, probs, rest = _attn_fwd(tag + "attn_fwd", proj, bias,
                                         _gather_send(shards[1:], None, None, 0))
            pooled, mixed, rest = _pool_fwd(tag + "pool_fwd", proj, wg_bf[l], pool_scale[l:l + 1],
                                            _gather_forward(rest, nbig - 1, rows[1:], 0))
            g = g + rest
        else:
            att, probs = _attn_fwd(tag + "attn_fwd", proj, bias)
            pooled, mixed = _pool_fwd(tag + "pool_fwd", proj, wg_bf[l], pool_scale[l:l + 1])
        win_g, wao_g, wpo_g, wo_full, wup_g, wdn_full = views(g)
        ya = _narrow_nn(tag + "attn_out", att, wao_g, l)
        yb = _narrow_nn(tag + "pool_out", mixed, wpo_g, l)
        z = _gate_fwd(tag + "gate_fwd", proj, b_gate[l:l + 1], ya, yb)
        mix = _mm_nn(tag + "mix", z, wo_full, l, D_MODEL, F32)
        x1, h2 = _post_pre_fwd(tag + "norm_mix_post", xcur, mix, norm_mix_post[l:l + 1],
                               norm_ffn_pre[l:l + 1])
        if l == 0:
            hu, mixing = _mm_nn_blocked(tag + "ffn_up", h2, wup_g, l, BF16,
                                        _gather_send(shards[:4], None, g[:4], 1))
            a, hc, ffn_g = _ffn_gate_fwd(tag + "ffn_gate_fwd", hu, cw_full[l], conv_b[l:l + 1],
                                         _gather_send(shards[4:], None, g[4:], 1))
            g = mixing + ffn_g
            wdn_full = views(g)[5]
        else:
            hu = _mm_nn_blocked(tag + "ffn_up", h2, wup_g, l, BF16)
            a, hc = _ffn_gate_fwd(tag + "ffn_gate_fwd", hu, cw_full[l], conv_b[l:l + 1])
        f = _mm_nn(tag + "ffn_down", a, wdn_full, l, D_FF // 2, F32)
        saved.append(dict(x=xcur, h=h, proj=proj, att=att, pooled=pooled, mixed=mixed, ya=ya,
                          yb=yb, z=z, mix=mix, x1=x1, h2=h2, hu=hu, hc=hc, a=a, f=f, probs=probs))
        if l == 0:
            xcur, h, g = _post_pre_fwd(tag + "norm_ffn_post", x1, f, norm_ffn_post[l:l + 1],
                                       norm_mix_pre[l + 1:l + 2], _gather_forward(g, nbig, rows, 1))
        elif l < DEPTH - 1:
            xcur, h = _post_pre_fwd(tag + "norm_ffn_post", x1, f, norm_ffn_post[l:l + 1],
                                    norm_mix_pre[l + 1:l + 2])
    win_g, wao_g, wpo_g, wo_full, wup_g, wdn_full = views(g)

    dy, df, d_nfpost, loss_local = _tail("tail", saved[-1]["x1"], saved[-1]["f"],
                                         norm_ffn_post[DEPTH - 1:DEPTH], target)
    loss = lax.psum(loss_local, ("x", "y", "c"))

    dx = dy
    dws = dict.fromkeys(names)
    reds = [None] * nbig
    small_grads = [None] * DEPTH
    ffn = [4, 5]
    outs3 = [1, 2, 3]

    def blocks(ks):
        return [dws[names[k]].reshape(DEPTH, N_CHIPS, rows[k], -1) for k in ks]

    def chip_sums(ks, sib, l):
        return _chip_sums(f"chip_sums{l}_" + names[ks[0]], blocks(ks), sib, l)

    def final_sums(ks, sums, recv, l):
        outs = _final_sums(f"final_sums{l}_" + names[ks[0]], sums, recv, l, [reds[k] for k in ks])
        for k, r in zip(ks, outs):
            reds[k] = r

    for l in reversed(range(DEPTH)):
        tag = f"l{l}_"
        sv = saved[l]
        every = list(range(nbig))
        if l == 0:
            da, sib = _mm_nt(tag + "ffn_down_dx", df, wdn_full, l, D_FF // 2, BF16,
                             _reduce_swap(blocks(every), 1))
            sums = chip_sums(every, sib, 1)
        else:
            da = _mm_nt(tag + "ffn_down_dx", df, wdn_full, l, D_FF // 2, BF16)
        dws["w_down"] = _mm_tn(tag + "ffn_down_dw", sv["a"], df, D_FF // 2, l, dws["w_down"])
        if l == 0:
            dhu, dconv, recv = _ffn_gate_bwd(tag + "ffn_gate_bwd", da, sv["hu"], sv["hc"],
                                             cw_full[l], _reduce_scatter(sums))
            final_sums(every, sums, recv, 1)
            dh2, reds = _mm_nt_blocked(tag + "ffn_up_dx", dhu, wup_g, l, F32,
                                       _reduce_share(reds, 1))
        else:
            dhu, dconv = _ffn_gate_bwd(tag + "ffn_gate_bwd", da, sv["hu"], sv["hc"], cw_full[l])
            dh2 = _mm_nt_blocked(tag + "ffn_up_dx", dhu, wup_g, l, F32)
        dws["w_up"] = _mm_tn_blocked(tag + "ffn_up_dw", sv["h2"], dhu, l, dws["w_up"])
        if l == 0:
            dx1, d_nfpre, dmix, d_nmpost, sib = _pre_post_bwd(
                tag + "norm_ffn_pre_bwd", dh2, sv["x1"], dx, norm_ffn_pre[l:l + 1], sv["mix"],
                norm_mix_post[l:l + 1], _reduce_swap(blocks(ffn), 0))
            sums = chip_sums(ffn, sib, 0)
        else:
            dx1, d_nfpre, dmix, d_nmpost = _pre_post_bwd(
                tag + "norm_ffn_pre_bwd", dh2, sv["x1"], dx, norm_ffn_pre[l:l + 1], sv["mix"],
                norm_mix_post[l:l + 1])
        dz = _mm_nt(tag + "mix_dx", dmix, wo_full, l, D_MODEL, BF16)
        dws["w_o"] = _mm_tn(tag + "mix_dw", sv["z"], dmix, D_MODEL, l, dws["w_o"])
        dya, dyb, dgates, d_bgate = _gate_bwd(tag + "gate_bwd", dz, sv["proj"], b_gate[l:l + 1],
                                              sv["ya"], sv["yb"])
        datt = _narrow_nt(tag + "attn_out_dx", dya, wao_g, l)
        dws["w_attn_out"] = _narrow_tn(tag + "attn_out_dw", sv["att"], dya, l, dws["w_attn_out"])
        dmixed = _narrow_nt(tag + "pool_out_dx", dyb, wpo_g, l)
        dws["w_pool_out"] = _narrow_tn(tag + "pool_out_dw", sv["mixed"], dyb, l, dws["w_pool_out"])
        if l == 0:
            du, d_wg, d_pscale, sib = _pool_bwd(tag + "pool_bwd", dmixed, sv["pooled"], wg_bf[l],
                                                pool_scale[l:l + 1], _reduce_swap(blocks(outs3), 0))
            sums3 = chip_sums(outs3, sib, 0)
            dqkv, dbias, recv = _attn_bwd(
                tag + "attn_bwd", sv["proj"], datt, sv["att"], sv["probs"],
                _both(_reduce_scatter(sums), _reduce_scatter(sums3)))
            final_sums(ffn, sums, recv[:len(ffn)], 0)
            final_sums(outs3, sums3, recv[len(ffn):], 0)
        else:
            du, d_wg, d_pscale = _pool_bwd(tag + "pool_bwd", dmixed, sv["pooled"], wg_bf[l],
                                           pool_scale[l:l + 1])
            dqkv, dbias = _attn_bwd(tag + "attn_bwd", sv["proj"], datt, sv["att"], sv["probs"])
        d_rel = _bias_fold(tag + "bias_fold", dbias)
        if l == 0:
            dh, shared = _proj_dx(tag + "proj_dx", dqkv, du, dgates, win_g, l,
                                  _reduce_share([reds[k] for k in ffn + outs3], 0))
            for k, r in zip(ffn + outs3, shared):
                reds[k] = r
        else:
            dh = _proj_dx(tag + "proj_dx", dqkv, du, dgates, win_g, l)
        dws["w_in"] = _proj_dw(tag + "proj_dw", sv["h"], dqkv, du, dgates, l, dws["w_in"])
        small_grads[l] = [None, d_nmpost, d_nfpre, d_nfpost, d_bgate, d_rel, d_wg, d_pscale, dconv]
        if l > 0:
            dx, small_grads[l][0], df, d_nfpost = _pre_post_bwd(
                tag + "norm_mix_pre_bwd", dh, sv["x"], dx1, norm_mix_pre[l:l + 1],
                saved[l - 1]["f"], norm_ffn_post[l - 1:l])
        else:
            dx, small_grads[l][0], sib = _norm_pre_bwd(
                tag + "norm_mix_pre_bwd", dh, sv["x"], dx1, norm_mix_pre[l:l + 1],
                _reduce_swap(blocks([0]), 0))

    grad_x = dx.reshape(x.shape)

    delta, new_m, new_v = {}, {}, {}
    sums = chip_sums([0], sib, 0)
    delta["w_up"], new_m["w_up"], new_v["w_up"], recv = _adamw(
        "adamw_w_up", w_up, reds[4], m_w_up, v_w_up, _reduce_scatter(sums))
    final_sums([0], sums, recv, 0)
    delta["w_down"], new_m["w_down"], new_v["w_down"], shared = _adamw(
        "adamw_w_down", w_down, reds[5], m_w_down, v_w_down, _reduce_share([reds[0]], 0))
    g_big = shared + reds[1:]

    (g_nmpre, g_nmpost, g_nfpre, g_nfpost, g_bgate, g_rel, g_wg, g_pscale,
     g_conv) = _allreduce_small(small_grads)
    g_rel = g_rel[:, :, :N_REL]
    g_cb = g_conv[:, 3]
    ncw = conv_w.shape[2]
    chip = 2 * lax.axis_index("x") + lax.axis_index("y")
    g_cw = lax.dynamic_slice_in_dim(g_conv[:, 0:3], chip * ncw, ncw, axis=2)

    grads = dict(norm_mix_pre=g_nmpre, w_in=g_big[0], b_gate=g_bgate, rel_bias=g_rel,
                 w_attn_out=g_big[1], w_pool_group=g_wg, pool_scale=g_pscale, w_pool_out=g_big[2],
                 w_o=g_big[3], norm_mix_post=g_nmpost, norm_ffn_pre=g_nfpre, w_up=g_big[4],
                 conv_w=g_cw, conv_b=g_cb, w_down=g_big[5], norm_ffn_post=g_nfpost)
    weights = dict(norm_mix_pre=norm_mix_pre, w_in=w_in, b_gate=b_gate, rel_bias=rel_bias,
                   w_attn_out=w_attn_out, w_pool_group=w_pool_group, pool_scale=pool_scale,
                   w_pool_out=w_pool_out, w_o=w_o, norm_mix_post=norm_mix_post,
                   norm_ffn_pre=norm_ffn_pre, w_up=w_up, conv_w=conv_w, conv_b=conv_b,
                   w_down=w_down, norm_ffn_post=norm_ffn_post)
    moms = dict(norm_mix_pre=(m_norm_mix_pre, v_norm_mix_pre), w_in=(m_w_in, v_w_in),
                b_gate=(m_b_gate, v_b_gate), rel_bias=(m_rel_bias, v_rel_bias),
                w_attn_out=(m_w_attn_out, v_w_attn_out),
                w_pool_group=(m_w_pool_group, v_w_pool_group),
                pool_scale=(m_pool_scale, v_pool_scale), w_pool_out=(m_w_pool_out, v_w_pool_out),
                w_o=(m_w_o, v_w_o), norm_mix_post=(m_norm_mix_post, v_norm_mix_post),
                norm_ffn_pre=(m_norm_ffn_pre, v_norm_ffn_pre), w_up=(m_w_up, v_w_up),
                conv_w=(m_conv_w, v_conv_w), conv_b=(m_conv_b, v_conv_b),
                w_down=(m_w_down, v_w_down), norm_ffn_post=(m_norm_ffn_post, v_norm_ffn_post))
    order = list(weights.keys())

    small_names = [nm for nm in order if nm not in names]
    for nm in names:
        if nm not in delta:
            delta[nm], new_m[nm], new_v[nm] = _adamw("adamw_" + nm, weights[nm], grads[nm],
                                                     *moms[nm])
    d_s, m_s, v_s = _adamw_small([weights[nm] for nm in small_names],
                                 [grads[nm] for nm in small_names],
                                 [moms[nm][0] for nm in small_names],
                                 [moms[nm][1] for nm in small_names])
    for i, nm in enumerate(small_names):
        delta[nm], new_m[nm], new_v[nm] = d_s[i], m_s[i], v_s[i]

    return (loss, grad_x, *[grads[nm] for nm in order], *[delta[nm] for nm in order],
            *[new_m[nm] for nm in order], *[new_v[nm] for nm in order])
```

```python
import functools
import math

import jax
import jax.numpy as jnp
from jax import lax
from jax.experimental import pallas as pl
from jax.experimental.pallas import tpu as pltpu

F32 = jnp.float32
BF16 = jnp.bfloat16
MESH = pl.DeviceIdType.MESH

D_MODEL = 1024
DEPTH = 2
CHUNK = 64
BAND_CHUNKS = 9
BAND = BAND_CHUNKS * CHUNK
HEADS = 8
HEAD_DIM = 64
ATTN_W = HEADS * HEAD_DIM
POOL_WINDOWS = (2, 4, 8, 16)
POOL_W = 512
POOL_GD = 128
MAX_REL = 256
N_REL = 2 * MAX_REL + 1
D_FF = 2816
IN_W = 3 * ATTN_W + POOL_W + 2 * D_MODEL
EPS = 1e-6
ATTN_SCALE = HEAD_DIM ** -0.5
BAND_PAD = 640
BIAS_LANES = BAND_PAD
N_CHIPS = 4

ADAM_LR = 0.001
ADAM_B1 = 0.9
ADAM_B2 = 0.999
ADAM_EPS = 1e-08
ADAM_WD = 0.01
ADAM_STEP = 10

VMEM_LIMIT_V7X = 56 * 1024 * 1024
TOK = 512
ATT_BLK = 8 * CHUNK
FF_COL = 256
FF_TOK = 1024
HALO = 32


def _cparams(*sem):
    return pltpu.CompilerParams(dimension_semantics=sem, vmem_limit_bytes=VMEM_LIMIT_V7X)


def _sds(shape, dtype):
    return jax.ShapeDtypeStruct(shape, dtype)


class _Comm:
    def __init__(self, ins, outs, copies, n_sems, alias=None):
        self.ins, self.outs, self.copies, self.n_sems = list(ins), list(outs), copies, n_sems
        self.alias = dict(alias or {})


class _SemsFrom:
    def __init__(self, sems, start):
        self.sems, self.start = sems, start

    @property
    def at(self):
        return self

    def __getitem__(self, i):
        return self.sems.at[self.start + i]


def _both(a, b):
    na, nao = len(a.ins), len(a.outs)

    def copies(cin, cout, ssem, rsem):
        return (a.copies(cin[:na], cout[:nao], ssem, rsem)
                + b.copies(cin[na:], cout[nao:], _SemsFrom(ssem, a.n_sems), _SemsFrom(rsem, a.n_sems)))

    alias = dict(a.alias)
    alias.update({na + i: nao + o for i, o in b.alias.items()})
    return _Comm(a.ins + b.ins, a.outs + b.outs, copies, a.n_sems + b.n_sems, alias)


def _pcall(body, name, grid, in_specs, out_specs, out_shape, scratch_shapes, compiler_params, args,
           comm=None, aliases=None):
    single = not isinstance(out_shape, (list, tuple))
    out_specs = [out_specs] if single else list(out_specs)
    out_shape = [out_shape] if single else list(out_shape)
    n_in, n_out = len(in_specs), len(out_specs)
    aliases = dict(aliases or {})
    if comm is None:
        res = pl.pallas_call(
            body, name=name, grid=grid, in_specs=list(in_specs), out_specs=out_specs,
            out_shape=out_shape, scratch_shapes=list(scratch_shapes),
            input_output_aliases=aliases, compiler_params=compiler_params)(*args)
        return (res[0] if single else res), None
    ci, co = len(comm.ins), len(comm.outs)

    def hosted(*refs):
        main_in, cin = refs[:n_in], refs[n_in:n_in + ci]
        main_out = refs[n_in + ci:n_in + ci + n_out]
        cout = refs[n_in + ci + n_out:n_in + ci + n_out + co]
        rest = refs[n_in + ci + n_out + co:]
        copies = comm.copies(cin, cout, rest[-2], rest[-1])
        ids = [pl.program_id(a) for a in range(len(grid))]
        first = functools.reduce(jnp.logical_and, [i == 0 for i in ids])
        last = functools.reduce(jnp.logical_and, [i == g - 1 for i, g in zip(ids, grid)])

        @pl.when(first)
        def _():
            for cp in copies:
                cp.start()

        body(*main_in, *main_out, *rest[:-2])

        @pl.when(last)
        def _():
            for cp in copies:
                cp.wait()

    for i, o in comm.alias.items():
        aliases[n_in + i] = n_out + o
    hbm = pl.BlockSpec(memory_space=pl.ANY)
    sems = pltpu.SemaphoreType.DMA((comm.n_sems,))
    res = pl.pallas_call(
        hosted, name=name, grid=grid, in_specs=list(in_specs) + [hbm] * ci,
        out_specs=out_specs + [hbm] * co, out_shape=out_shape + comm.outs,
        scratch_shapes=list(scratch_shapes) + [sems, sems],
        input_output_aliases=aliases, compiler_params=compiler_params)(*args, *comm.ins)
    return (res[0] if single else list(res[:n_out])), list(res[n_out:])


def _comm_call(name, comm):
    ci = len(comm.ins)

    def body(*refs):
        copies = comm.copies(refs[:ci], refs[ci:-2], refs[-2], refs[-1])
        for cp in copies:
            cp.start()
        for cp in copies:
            cp.wait()

    hbm = pl.BlockSpec(memory_space=pl.ANY)
    sems = pltpu.SemaphoreType.DMA((comm.n_sems,))
    return list(pl.pallas_call(
        body, name=name, in_specs=[hbm] * ci, out_specs=[hbm] * len(comm.outs),
        out_shape=comm.outs, scratch_shapes=[sems, sems],
        input_output_aliases=comm.alias)(*comm.ins))


def _matmul(name, a, b, a_spec, b_spec, o_spec, out_shape, grid, contract, nk, acc_shape,
            fill=None, comm=None):
    def body(*refs):
        a_ref, b_ref = refs[0], refs[1]
        o_ref = refs[2 if fill is None else 3]
        scratch = refs[(3 if fill is None else 4):]
        part = lax.dot_general(a_ref[...], b_ref[...], (contract, ((), ())),
                               preferred_element_type=F32)
        if nk == 1:
            o_ref[...] = part.astype(o_ref.dtype)
        else:
            acc_ref = scratch[0]
            k = pl.program_id(2)

            @pl.when(k == 0)
            def _():
                acc_ref[...] = part

            @pl.when(k > 0)
            def _():
                acc_ref[...] += part

            @pl.when(k == nk - 1)
            def _():
                o_ref[...] = acc_ref[...].astype(o_ref.dtype)

    scratch = [] if nk == 1 else [pltpu.VMEM(acc_shape, F32)]
    in_specs, args, aliases = [a_spec, b_spec], [a, b], {}
    if fill is not None:
        in_specs.append(pl.BlockSpec(memory_space=pl.ANY))
        args.append(fill)
        aliases = {2: 0}
    out, moved = _pcall(body, name, grid, in_specs, o_spec, out_shape, scratch,
                        _cparams("parallel", "parallel", "arbitrary"), args, comm, aliases)
    return out if comm is None else (out, moved)


NN = ((1,), (0,))
NT = ((1,), (1,))
TN = ((0,), (0,))


def _tm(t):
    return min(t, 1024)


def _tt(t):
    return min(t, 2048)


def _col_block_spec(a, rows, nb, row_col):
    if a.ndim == 2:
        return pl.BlockSpec((rows, nb), row_col)

    def halves(*ids):
        r, c = row_col(*ids)
        return c // 2, r, c % 2

    return pl.BlockSpec((None, rows, nb), halves)


def _mm_nn_blocked(name, a, w, l, out_dtype, comm=None):
    t, k = a.shape
    nb = w.shape[3]
    tm = _tm(t)
    return _matmul(
        name, a, w,
        pl.BlockSpec((tm, k), lambda i, n, kk: (i, 0)),
        pl.BlockSpec((None, None, k, nb), lambda i, n, kk: (l, n, 0, 0)),
        pl.BlockSpec((tm, nb), lambda i, n, kk: (i, n)),
        _sds((t, N_CHIPS * nb), out_dtype), (t // tm, N_CHIPS, 1), NN, 1, None, comm=comm)


def _mm_nt_blocked(name, a, w, l, out_dtype, comm=None):
    t = a.shape[-2]
    k, nb = w.shape[2], w.shape[3]
    tm = _tm(t)
    return _matmul(
        name, a, w,
        _col_block_spec(a, tm, nb, lambda i, n, kk: (i, kk)),
        pl.BlockSpec((None, None, k, nb), lambda i, n, kk: (l, kk, 0, 0)),
        pl.BlockSpec((tm, k), lambda i, n, kk: (i, 0)),
        _sds((t, k), out_dtype), (t // tm, 1, N_CHIPS), NT, N_CHIPS, (tm, k), comm=comm)


def _mm_tn_blocked(name, a, g, l, fill):
    t, k = a.shape
    nb = g.shape[-1] * (g.ndim - 1) // N_CHIPS
    tt = _tt(t)
    nt = t // tt
    return _matmul(
        name, a, g,
        pl.BlockSpec((tt, k), lambda n, j, kk: (kk, 0)),
        _col_block_spec(g, tt, nb, lambda n, j, kk: (kk, n)),
        pl.BlockSpec((None, None, k, nb), lambda n, j, kk: (l, n, 0, 0)),
        _sds((DEPTH, N_CHIPS, k, nb), BF16), (N_CHIPS, 1, nt), TN, nt, (k, nb), fill)


def _proj_pieces(rows, dqkv_first):
    def piece(col):
        if dqkv_first:
            return pl.BlockSpec((rows, ATTN_W), lambda i, kk: (i, col))
        return pl.BlockSpec((rows, ATTN_W), lambda n, kk: (kk, col))
    return [piece(0), piece(1), piece(2), piece(0)]


def _proj_dx(name, dqkv, du, dgates, w, l, comm=None):
    t = du.shape[0]
    k, nb = w.shape[2], w.shape[3]
    tm = _tm(t)

    def body(dq_ref, dk_ref, dv_ref, du_ref, dg_ref, w_ref, o_ref, acc_ref):
        kk = pl.program_id(1)

        def mm(a):
            return lax.dot_general(a, w_ref[...], (NT, ((), ())), preferred_element_type=F32)

        @pl.when(kk == 0)
        def _():
            acc_ref[...] = mm(jnp.concatenate([dq_ref[...], dk_ref[...]], axis=1))

        @pl.when(kk == 1)
        def _():
            acc_ref[...] += mm(jnp.concatenate([dv_ref[...], du_ref[...]], axis=1))

        @pl.when(kk >= 2)
        def _():
            acc_ref[...] += mm(dg_ref[...])

        @pl.when(kk == N_CHIPS - 1)
        def _():
            o_ref[...] = acc_ref[...]

    out, moved = _pcall(
        body, name, (t // tm, N_CHIPS),
        _proj_pieces(tm, True)
        + [pl.BlockSpec((tm, nb), lambda i, kk: (i, jnp.maximum(kk - 2, 0))),
           pl.BlockSpec((None, None, k, nb), lambda i, kk: (l, kk, 0, 0))],
        pl.BlockSpec((tm, k), lambda i, kk: (i, 0)), _sds((t, k), F32),
        [pltpu.VMEM((tm, k), F32)], _cparams("arbitrary", "arbitrary"),
        (dqkv, dqkv, dqkv, du, dgates, w), comm)
    return out if comm is None else (out, moved)


def _proj_dw(name, h, dqkv, du, dgates, l, fill):
    t, k = h.shape
    nb = dgates.shape[1] // 2
    tt = _tm(t)
    nt = t // tt

    def body(*refs):
        h_ref, dq_ref, dk_ref, dv_ref, du_ref, dg_ref = refs[:6]
        o_ref, acc_ref = refs[-2], refs[-1]
        n, kk = pl.program_id(0), pl.program_id(1)

        def update(g):
            part = lax.dot_general(h_ref[...], g, (TN, ((), ())), preferred_element_type=F32)

            @pl.when(kk == 0)
            def _():
                acc_ref[...] = part

            @pl.when(kk > 0)
            def _():
                acc_ref[...] += part

        @pl.when(n == 0)
        def _():
            update(jnp.concatenate([dq_ref[...], dk_ref[...]], axis=1))

        @pl.when(n == 1)
        def _():
            update(jnp.concatenate([dv_ref[...], du_ref[...]], axis=1))

        @pl.when(n >= 2)
        def _():
            update(dg_ref[...])

        @pl.when(kk == nt - 1)
        def _():
            o_ref[...] = acc_ref[...].astype(BF16)

    in_specs = ([pl.BlockSpec((tt, k), lambda n, kk: (kk, 0))] + _proj_pieces(tt, False)
                + [pl.BlockSpec((tt, nb), lambda n, kk: (kk, jnp.maximum(n - 2, 0)))])
    args, aliases = [h, dqkv, dqkv, dqkv, du, dgates], {}
    if fill is not None:
        in_specs.append(pl.BlockSpec(memory_space=pl.ANY))
        args.append(fill)
        aliases = {6: 0}
    return pl.pallas_call(
        body, name=name, grid=(N_CHIPS, nt), in_specs=in_specs,
        out_specs=pl.BlockSpec((None, None, k, nb), lambda n, kk: (l, n, 0, 0)),
        out_shape=_sds((DEPTH, N_CHIPS, k, nb), BF16),
        scratch_shapes=[pltpu.VMEM((k, nb), F32)], input_output_aliases=aliases,
        compiler_params=_cparams("parallel", "arbitrary"))(*args)


def _narrow_nn(name, a, w, l):
    t, k = a.shape
    nb = w.shape[3]
    tm = _tm(t)

    def body(a_ref, w_ref, o_ref):
        av = a_ref[...]
        for j in range(N_CHIPS):
            o_ref[:, j * nb:(j + 1) * nb] = jnp.dot(
                av, w_ref[j], preferred_element_type=F32).astype(BF16)

    return pl.pallas_call(
        body, name=name, grid=(t // tm,),
        in_specs=[pl.BlockSpec((tm, k), lambda i: (i, 0)),
                  pl.BlockSpec((None, N_CHIPS, k, nb), lambda i: (l, 0, 0, 0))],
        out_specs=pl.BlockSpec((tm, N_CHIPS * nb), lambda i: (i, 0)),
        out_shape=_sds((t, N_CHIPS * nb), BF16), compiler_params=_cparams("parallel"))(a, w)


def _narrow_nt(name, a, w, l):
    t = a.shape[0]
    k, nb = w.shape[2], w.shape[3]
    tm = _tm(t)

    def body(a_ref, w_ref, o_ref):
        acc = lax.dot_general(a_ref[:, 0:nb], w_ref[0], (NT, ((), ())), preferred_element_type=F32)
        for j in range(1, N_CHIPS):
            acc = acc + lax.dot_general(a_ref[:, j * nb:(j + 1) * nb], w_ref[j], (NT, ((), ())),
                                        preferred_element_type=F32)
        o_ref[...] = acc.astype(BF16)

    return pl.pallas_call(
        body, name=name, grid=(t // tm,),
        in_specs=[pl.BlockSpec((tm, N_CHIPS * nb), lambda i: (i, 0)),
                  pl.BlockSpec((None, N_CHIPS, k, nb), lambda i: (l, 0, 0, 0))],
        out_specs=pl.BlockSpec((tm, k), lambda i: (i, 0)),
        out_shape=_sds((t, k), BF16), compiler_params=_cparams("parallel"))(a, w)


def _narrow_tn(name, a, g, l, fill):
    t, k = a.shape
    nb = g.shape[1] // N_CHIPS
    tt = _tm(t)
    nt = t // tt

    def body(*refs):
        a_ref, g_ref, o_ref, acc_ref = refs[0], refs[1], refs[-2], refs[-1]
        i = pl.program_id(0)
        part = lax.dot_general(a_ref[...], g_ref[...], (TN, ((), ())), preferred_element_type=F32)

        @pl.when(i == 0)
        def _():
            acc_ref[...] = part

        @pl.when(i > 0)
        def _():
            acc_ref[...] += part

        @pl.when(i == nt - 1)
        def _():
            for j in range(N_CHIPS):
                o_ref[j] = acc_ref[:, j * nb:(j + 1) * nb].astype(BF16)

    in_specs = [pl.BlockSpec((tt, k), lambda i: (i, 0)),
                pl.BlockSpec((tt, N_CHIPS * nb), lambda i: (i, 0))]
    args, aliases = [a, g], {}
    if fill is not None:
        in_specs.append(pl.BlockSpec(memory_space=pl.ANY))
        args.append(fill)
        aliases = {2: 0}
    return pl.pallas_call(
        body, name=name, grid=(nt,), in_specs=in_specs,
        out_specs=pl.BlockSpec((None, N_CHIPS, k, nb), lambda i: (l, 0, 0, 0)),
        out_shape=_sds((DEPTH, N_CHIPS, k, nb), BF16),
        scratch_shapes=[pltpu.VMEM((k, N_CHIPS * nb), F32)], input_output_aliases=aliases,
        compiler_params=_cparams("arbitrary"))(*args)


def _mm_nn(name, a, w, l, tk, out_dtype):
    t, k = a.shape
    n = w.shape[2]
    tm = _tm(t)
    nk = k // tk
    return _matmul(
        name, a, w,
        pl.BlockSpec((tm, tk), lambda i, j, kk: (i, kk)),
        pl.BlockSpec((None, tk, n), lambda i, j, kk: (l, kk, 0)),
        pl.BlockSpec((tm, n), lambda i, j, kk: (i, 0)),
        _sds((t, n), out_dtype), (t // tm, 1, nk), NN, nk, (tm, n))


def _mm_nt(name, a, w, l, tn, out_dtype, comm=None):
    t, n = a.shape
    k = w.shape[1]
    tm = _tm(t)
    return _matmul(
        name, a, w,
        pl.BlockSpec((tm, n), lambda i, j, kk: (i, 0)),
        pl.BlockSpec((None, tn, n), lambda i, j, kk: (l, j, 0)),
        pl.BlockSpec((tm, tn), lambda i, j, kk: (i, j)),
        _sds((t, k), out_dtype), (t // tm, k // tn, 1), NT, 1, None, comm=comm)


def _mm_tn(name, a, g, tko, l, fill):
    t, k = a.shape
    n = g.shape[1]
    tt = _tt(t)
    nt = t // tt
    return _matmul(
        name, a, g,
        pl.BlockSpec((tt, tko), lambda i, j, kk: (kk, i)),
        pl.BlockSpec((tt, n), lambda i, j, kk: (kk, 0)),
        pl.BlockSpec((None, tko, n), lambda i, j, kk: (l, i, 0)),
        _sds((DEPTH, k, n), BF16), (k // tko, 1, nt), TN, nt, (tko, n), fill)


def _row_spec(width, col=0):
    return pl.BlockSpec((TOK, width), lambda i: (i, col))


def _vec_spec(width):
    return pl.BlockSpec((1, width), lambda i: (0, 0))


def _rms(x):
    return lax.rsqrt(jnp.mean(x * x, axis=-1, keepdims=True) + EPS)


def _norm_fwd(name, x, g, comm=None):
    t = x.shape[0]

    def body(x_ref, g_ref, h_ref):
        xv = x_ref[...]
        h_ref[...] = (xv * _rms(xv) * g_ref[...]).astype(BF16)

    out, moved = _pcall(body, name, (t // TOK,), [_row_spec(D_MODEL), _vec_spec(D_MODEL)],
                        _row_spec(D_MODEL), _sds((t, D_MODEL), BF16), [], _cparams("arbitrary"),
                        (x, g), comm)
    return out if comm is None else (out, moved)


ROWS = 16
ROW_UNROLL = 8


def _rows(k):
    return pl.ds(pl.multiple_of(k * ROWS, ROWS), ROWS)


def _strips(step, init):
    def group(j, carry):
        for u in range(ROW_UNROLL):
            carry = step(j * ROW_UNROLL + u, carry)
        return carry

    return lax.fori_loop(0, TOK // (ROWS * ROW_UNROLL), group, init)


def _fold_rows(x):
    return x[0:8] + x[8:16]


def _accumulate(ref, part):
    total = jnp.sum(part, axis=0, keepdims=True)

    @pl.when(pl.program_id(0) == 0)
    def _():
        ref[...] = total

    @pl.when(pl.program_id(0) > 0)
    def _():
        ref[...] += total


def _norm_bwd_rows(d, mv, g):
    r = _rms(mv)
    n = mv * r
    dn = d * g
    return r * (dn - n * jnp.mean(dn * n, axis=-1, keepdims=True)), d * n


def _post_pre_fwd(name, xres, m, g_post, g_pre, comm=None):
    t = xres.shape[0]

    def body(x_ref, m_ref, gp_ref, gn_ref, x1_ref, h_ref):
        def strip(k, c):
            rows = _rows(k)
            mv = m_ref[rows, :]
            x1 = x_ref[rows, :] + mv * _rms(mv) * gp_ref[...]
            x1_ref[rows, :] = x1
            h_ref[rows, :] = (x1 * _rms(x1) * gn_ref[...]).astype(BF16)
            return c

        _strips(strip, 0)

    outs, moved = _pcall(
        body, name, (t // TOK,),
        [_row_spec(D_MODEL), _row_spec(D_MODEL), _vec_spec(D_MODEL), _vec_spec(D_MODEL)],
        [_row_spec(D_MODEL), _row_spec(D_MODEL)],
        [_sds((t, D_MODEL), F32), _sds((t, D_MODEL), BF16)], [], _cparams("arbitrary"),
        (xres, m, g_post, g_pre), comm)
    return outs if comm is None else (*outs, moved)


def _tail(name, xres, m, g_post, target):
    t = xres.shape[0]

    def body(x_ref, m_ref, g_ref, t_ref, dy_ref, dm_ref, dg_ref, l_ref):
        def strip(k, carry):
            rows = _rows(k)
            mv = m_ref[rows, :]
            e = x_ref[rows, :] + mv * _rms(mv) * g_ref[...] - t_ref[rows, :]
            dy = e * (1.0 / D_MODEL)
            dy_ref[rows, :] = dy
            dm, dgn = _norm_bwd_rows(dy, mv, g_ref[...])
            dm_ref[rows, :] = dm.astype(BF16)
            return carry[0] + _fold_rows(dgn), carry[1] + _fold_rows(e * e)

        zero = jnp.zeros((8, D_MODEL), F32)
        dg, sq = _strips(strip, (zero, zero))
        _accumulate(dg_ref, dg)
        _accumulate(l_ref, jnp.sum(sq, axis=1, keepdims=True))

    dy, dm, dg, sq = pl.pallas_call(
        body, name=name, grid=(t // TOK,),
        in_specs=[_row_spec(D_MODEL), _row_spec(D_MODEL), _vec_spec(D_MODEL), _row_spec(D_MODEL)],
        out_specs=[_row_spec(D_MODEL), _row_spec(D_MODEL), _vec_spec(D_MODEL),
                   pl.BlockSpec((1, 1), lambda i: (0, 0))],
        out_shape=[_sds((t, D_MODEL), F32), _sds((t, D_MODEL), BF16), _sds((1, D_MODEL), F32),
                   _sds((1, 1), F32)],
        compiler_params=_cparams("arbitrary"))(xres, m, g_post, target)
    return dy, dm, dg, sq[0, 0] * (0.5 / D_MODEL)


def _pre_post_bwd(name, dh, xin, dxo, g_pre, m, g_post, comm=None):
    t = dh.shape[0]

    def body(dh_ref, x_ref, d_ref, gq_ref, m_ref, gp_ref, dx_ref, dgq_ref, dm_ref, dgp_ref):
        def strip(k, carry):
            rows = _rows(k)
            dxin, dgq = _norm_bwd_rows(dh_ref[rows, :], x_ref[rows, :], gq_ref[...])
            dx = d_ref[rows, :] + dxin
            dx_ref[rows, :] = dx
            dm, dgp = _norm_bwd_rows(dx, m_ref[rows, :], gp_ref[...])
            dm_ref[rows, :] = dm.astype(BF16)
            return carry[0] + _fold_rows(dgq), carry[1] + _fold_rows(dgp)

        zero = jnp.zeros((8, D_MODEL), F32)
        dgq, dgp = _strips(strip, (zero, zero))
        _accumulate(dgq_ref, dgq)
        _accumulate(dgp_ref, dgp)

    outs, moved = _pcall(
        body, name, (t // TOK,),
        [_row_spec(D_MODEL), _row_spec(D_MODEL), _row_spec(D_MODEL), _vec_spec(D_MODEL),
         _row_spec(D_MODEL), _vec_spec(D_MODEL)],
        [_row_spec(D_MODEL), _vec_spec(D_MODEL), _row_spec(D_MODEL), _vec_spec(D_MODEL)],
        [_sds((t, D_MODEL), F32), _sds((1, D_MODEL), F32), _sds((t, D_MODEL), BF16),
         _sds((1, D_MODEL), F32)], [], _cparams("arbitrary"),
        (dh, xin, dxo, g_pre, m, g_post), comm)
    return outs if comm is None else (*outs, moved)


def _norm_pre_bwd(name, dh, xin, dxo, g, comm=None):
    t = dh.shape[0]

    def body(dh_ref, x_ref, d_ref, g_ref, dx_ref, dg_ref):
        xv = x_ref[...]
        dhv = dh_ref[...]
        r = _rms(xv)
        n = xv * r
        dn = dhv * g_ref[...]
        dx_ref[...] = d_ref[...] + r * (dn - n * jnp.mean(dn * n, axis=-1, keepdims=True))
        part = jnp.sum(dhv * n, axis=0, keepdims=True)

        @pl.when(pl.program_id(0) == 0)
        def _():
            dg_ref[...] = part

        @pl.when(pl.program_id(0) > 0)
        def _():
            dg_ref[...] += part

    out, moved = _pcall(
        body, name, (t // TOK,),
        [_row_spec(D_MODEL), _row_spec(D_MODEL), _row_spec(D_MODEL), _vec_spec(D_MODEL)],
        [_row_spec(D_MODEL), _vec_spec(D_MODEL)],
        [_sds((t, D_MODEL), F32), _sds((1, D_MODEL), F32)], [], _cparams("arbitrary"),
        (dh, xin, dxo, g), comm)
    return out if comm is None else (*out, moved)


def _gate_fwd(name, proj, b_gate, ya, yb):
    t = proj.shape[0]

    def body(ga_ref, gb_ref, b_ref, ya_ref, yb_ref, z_ref):
        def strip(k, c):
            rows = _rows(k)
            sa = jax.nn.sigmoid(ga_ref[rows, :].astype(F32) + b_ref[:, :D_MODEL])
            sb = jax.nn.sigmoid(gb_ref[rows, :].astype(F32) + b_ref[:, D_MODEL:])
            z_ref[rows, :] = (sa * ya_ref[rows, :].astype(F32)
                              + sb * yb_ref[rows, :].astype(F32)).astype(BF16)
            return c

        _strips(strip, 0)

    return pl.pallas_call(
        body, name=name, grid=(t // TOK,),
        in_specs=[_row_spec(D_MODEL, 2), _row_spec(D_MODEL, 3), _vec_spec(2 * D_MODEL),
                  _row_spec(D_MODEL), _row_spec(D_MODEL)],
        out_specs=_row_spec(D_MODEL), out_shape=_sds((t, D_MODEL), BF16),
        compiler_params=_cparams("parallel"))(proj, proj, b_gate, ya, yb)


def _gate_bwd(name, dz, proj, b_gate, ya, yb):
    t = proj.shape[0]

    def body(dz_ref, ga_ref, gb_ref, b_ref, ya_ref, yb_ref, dya_ref, dyb_ref, dg_ref, db_ref):
        def strip(k, carry):
            rows = _rows(k)
            dzv = dz_ref[rows, :].astype(F32)
            sa = jax.nn.sigmoid(ga_ref[rows, :].astype(F32) + b_ref[:, :D_MODEL])
            sb = jax.nn.sigmoid(gb_ref[rows, :].astype(F32) + b_ref[:, D_MODEL:])
            dya_ref[rows, :] = (dzv * sa).astype(BF16)
            dyb_ref[rows, :] = (dzv * sb).astype(BF16)
            dga = dzv * ya_ref[rows, :].astype(F32) * sa * (1.0 - sa)
            dgb = dzv * yb_ref[rows, :].astype(F32) * sb * (1.0 - sb)
            dg_ref[rows, :D_MODEL] = dga.astype(BF16)
            dg_ref[rows, D_MODEL:] = dgb.astype(BF16)
            return carry[0] + _fold_rows(dga), carry[1] + _fold_rows(dgb)

        zero = jnp.zeros((8, D_MODEL), F32)
        pa, pb = _strips(strip, (zero, zero))
        _accumulate(db_ref.at[:, :D_MODEL], pa)
        _accumulate(db_ref.at[:, D_MODEL:], pb)

    return pl.pallas_call(
        body, name=name, grid=(t // TOK,),
        in_specs=[_row_spec(D_MODEL), _row_spec(D_MODEL, 2), _row_spec(D_MODEL, 3),
                  _vec_spec(2 * D_MODEL), _row_spec(D_MODEL), _row_spec(D_MODEL)],
        out_specs=[_row_spec(D_MODEL), _row_spec(D_MODEL), _row_spec(2 * D_MODEL),
                   _vec_spec(2 * D_MODEL)],
        out_shape=[_sds((t, D_MODEL), BF16), _sds((t, D_MODEL), BF16),
                   _sds((t, 2 * D_MODEL), BF16), _sds((1, 2 * D_MODEL), F32)],
        compiler_params=_cparams("arbitrary"))(dz, proj, proj, b_gate, ya, yb)


def _head_masks():
    lane = lax.broadcasted_iota(jnp.int32, (1, 2 * HEAD_DIM), 1)
    return lane < HEAD_DIM


BAND_ROWS = 2 * ATT_BLK + CHUNK


def _fill_band(band, prev_ref, cur_ref):
    band[0:ATT_BLK, :] = prev_ref[...]
    band[ATT_BLK:2 * ATT_BLK, :] = cur_ref[...]
    band[2 * ATT_BLK:, :] = jnp.zeros((CHUNK, ATTN_W), BF16)


def _pair_rows(x2, low):
    zero = jnp.zeros_like(x2)
    return jnp.concatenate([jnp.where(low, x2, zero), jnp.where(low, zero, x2)], axis=0)


def _pair_diag(o2, low):
    return jnp.where(low, o2[0:CHUNK, :], o2[CHUNK:, :])


N_PAIRS = HEADS // 2
SM_STRIP = 32
N_STRIPS = BAND_PAD // SM_STRIP
NEG = -1e30


def _fold8(x, op):
    return op(op(x[0:8], x[8:16]), op(x[16:24], x[24:32]))


def _strip(k):
    return pl.ds(pl.multiple_of(k * SM_STRIP, SM_STRIP), SM_STRIP)


def _band_probs(k2, qcat, bias_t, first_key):
    kpos = lax.broadcasted_iota(jnp.int32, (BAND_PAD, 1), 0)
    st = lax.dot_general(k2, qcat, (NT, ((), ())), preferred_element_type=F32)
    st = jnp.where(kpos + first_key >= 0, st + bias_t, NEG)
    e = jnp.exp(st - jnp.max(st, axis=0, keepdims=True))
    return e * (1.0 / jnp.sum(e, axis=0, keepdims=True))


def _attn_specs(nblk):
    cur = lambda col: pl.BlockSpec((ATT_BLK, ATTN_W), lambda s: (jnp.minimum(s, nblk - 1), col))
    prev = lambda col: pl.BlockSpec(
        (ATT_BLK, ATTN_W), lambda s: (jnp.maximum(jnp.minimum(s, nblk - 1) - 1, 0), col))
    return cur, prev


def _attn_fwd(name, proj, bias, comm=None):
    t = proj.shape[0]
    nblk = t // ATT_BLK
    cur, prev = _attn_specs(nblk)

    def body(q_ref, kp_ref, kc_ref, vp_ref, vc_ref, b_ref, o_ref, p_ref, kband, vband):
        s = pl.program_id(0)
        _fill_band(kband, kp_ref, kc_ref)
        _fill_band(vband, vp_ref, vc_ref)
        low = _head_masks()

        def chunk(ci, carry):
            r0 = pl.multiple_of(ci * CHUNK, CHUNK)
            for hp in range(N_PAIRS):
                cols = slice(hp * 128, (hp + 1) * 128)
                qcat = _pair_rows(q_ref[pl.ds(r0, CHUNK), cols] * ATTN_SCALE, low)
                p = _band_probs(kband[pl.ds(r0, BAND_PAD), cols], qcat, b_ref[hp],
                                (s * 8 - 8 + ci) * CHUNK).astype(BF16)
                p_ref[ci, hp] = p
                o2 = lax.dot_general(p, vband[pl.ds(r0, BAND_PAD), cols],
                                     (TN, ((), ())), preferred_element_type=F32)
                o_ref[pl.ds(r0, CHUNK), cols] = _pair_diag(o2, low).astype(BF16)
            return carry

        lax.fori_loop(0, 8, chunk, 0)

    outs, moved = _pcall(
        body, name, (nblk,),
        [cur(0), prev(1), cur(1), prev(2), cur(2),
         pl.BlockSpec((N_PAIRS, BAND_PAD, 128), lambda s: (0, 0, 0))],
        [pl.BlockSpec((ATT_BLK, ATTN_W), lambda s: (s, 0)),
         pl.BlockSpec((8, N_PAIRS, BAND_PAD, 128), lambda s: (s, 0, 0, 0))],
        [_sds((t, ATTN_W), BF16), _sds((t // CHUNK, N_PAIRS, BAND_PAD, 128), BF16)],
        [pltpu.VMEM((BAND_ROWS, ATTN_W), BF16), pltpu.VMEM((BAND_ROWS, ATTN_W), BF16)],
        _cparams("arbitrary"), (proj, proj, proj, proj, proj, bias), comm)
    return outs if comm is None else (*outs, moved)


def _attn_bwd(name, proj, datt, probs, comm=None):
    t = proj.shape[0]
    nblk = t // ATT_BLK
    cur, prev = _attn_specs(nblk)
    late = pl.BlockSpec((ATT_BLK, 3 * ATTN_W), lambda s: (jnp.maximum(s - 1, 0), 0))

    def body(q_ref, kp_ref, kc_ref, vp_ref, vc_ref, do_ref, p_ref,
             dqkv_ref, db_ref, kband, vband, dkacc, dvacc,
             dp_ref, dsb_ref, qc_ref, dc_ref, dq_ref, dq_held):
        s = pl.program_id(0)

        @pl.when(s == 0)
        def _():
            dkacc[...] = jnp.zeros_like(dkacc)
            dvacc[...] = jnp.zeros_like(dvacc)
            db_ref[...] = jnp.zeros_like(db_ref)
            dq_ref[...] = jnp.zeros_like(dq_ref)

        @pl.when(s < nblk)
        def _():
            _fill_band(kband, kp_ref, kc_ref)
            _fill_band(vband, vp_ref, vc_ref)
            low = _head_masks()

            def chunk(ci, carry):
                r0 = pl.multiple_of(ci * CHUNK, CHUNK)
                for hp in range(N_PAIRS):
                    cols = slice(hp * 128, (hp + 1) * 128)
                    qc_ref[hp] = _pair_rows(q_ref[pl.ds(r0, CHUNK), cols] * ATTN_SCALE, low)
                    dc_ref[hp] = _pair_rows(do_ref[pl.ds(r0, CHUNK), cols], low)
                    dp_ref[hp] = lax.dot_general(vband[pl.ds(r0, BAND_PAD), cols], dc_ref[hp],
                                                 (NT, ((), ())), preferred_element_type=F32)

                def sums(k, acc):
                    rows = _strip(k)
                    return tuple(acc[hp] + _fold8(p_ref[ci, hp, rows, :].astype(F32)
                                                  * dp_ref[hp, rows, :], jnp.add)
                                 for hp in range(N_PAIRS))

                acc = lax.fori_loop(0, N_STRIPS, sums, (jnp.zeros((8, 128), F32),) * N_PAIRS)
                delta = [jnp.sum(a, axis=0, keepdims=True) for a in acc]

                def grads(k, c):
                    rows = _strip(k)
                    for hp in range(N_PAIRS):
                        ds = (p_ref[ci, hp, rows, :].astype(F32)
                              * (dp_ref[hp, rows, :] - delta[hp]))
                        db_ref[hp, rows, :] += ds
                        dsb_ref[hp, rows, :] = ds.astype(BF16)
                    return c

                lax.fori_loop(0, N_STRIPS, grads, 0)
                for hp in range(N_PAIRS):
                    cols = slice(hp * 128, (hp + 1) * 128)
                    dq2 = lax.dot_general(dsb_ref[hp], kband[pl.ds(r0, BAND_PAD), cols],
                                          (TN, ((), ())), preferred_element_type=F32)
                    dq_ref[pl.ds(r0, CHUNK), cols] = (_pair_diag(dq2, low) * ATTN_SCALE).astype(BF16)
                    dkacc[pl.ds(r0, BAND_PAD), cols] += jnp.dot(dsb_ref[hp], qc_ref[hp],
                                                               preferred_element_type=F32)
                    dvacc[pl.ds(r0, BAND_PAD), cols] += jnp.dot(p_ref[ci, hp], dc_ref[hp],
                                                               preferred_element_type=F32)
                return carry

            dq_held[...] = dq_ref[...]
            lax.fori_loop(0, 8, chunk, 0)

        @pl.when(s == nblk)
        def _():
            dq_held[...] = dq_ref[...]

        dqkv_ref[:, 0:ATTN_W] = dq_held[...]
        dqkv_ref[:, ATTN_W:2 * ATTN_W] = dkacc[0:ATT_BLK, :].astype(BF16)
        dqkv_ref[:, 2 * ATTN_W:] = dvacc[0:ATT_BLK, :].astype(BF16)
        dkacc[0:ATT_BLK, :] = dkacc[ATT_BLK:2 * ATT_BLK, :]
        dvacc[0:ATT_BLK, :] = dvacc[ATT_BLK:2 * ATT_BLK, :]
        dkacc[ATT_BLK:, :] = jnp.zeros((ATT_BLK + CHUNK, ATTN_W), F32)
        dvacc[ATT_BLK:, :] = jnp.zeros((ATT_BLK + CHUNK, ATTN_W), F32)

    outs, moved = _pcall(
        body, name, (nblk + 1,),
        [cur(0), prev(1), cur(1), prev(2), cur(2),
         pl.BlockSpec((ATT_BLK, ATTN_W), lambda s: (jnp.minimum(s, nblk - 1), 0)),
         pl.BlockSpec((8, N_PAIRS, BAND_PAD, 128), lambda s: (jnp.minimum(s, nblk - 1), 0, 0, 0))],
        [late, pl.BlockSpec((HEADS // 2, BAND_PAD, 128), lambda s: (0, 0, 0))],
        [_sds((t, 3 * ATTN_W), BF16), _sds((HEADS // 2, BAND_PAD, 128), F32)],
        [pltpu.VMEM((BAND_ROWS, ATTN_W), BF16), pltpu.VMEM((BAND_ROWS, ATTN_W), BF16),
         pltpu.VMEM((BAND_ROWS, ATTN_W), F32), pltpu.VMEM((BAND_ROWS, ATTN_W), F32),
         pltpu.VMEM((N_PAIRS, BAND_PAD, 128), F32), pltpu.VMEM((N_PAIRS, BAND_PAD, 128), BF16),
         pltpu.VMEM((N_PAIRS, 2 * CHUNK, 128), BF16), pltpu.VMEM((N_PAIRS, 2 * CHUNK, 128), BF16),
         pltpu.VMEM((ATT_BLK, ATTN_W), BF16), pltpu.VMEM((ATT_BLK, ATTN_W), BF16)],
        _cparams("arbitrary"), (proj, proj, proj, proj, proj, datt, probs), comm)
    return outs if comm is None else (*outs, moved)


def _diag_onehot(rel_rows):
    d0 = lax.broadcasted_iota(jnp.int32, (BIAS_LANES, BIAS_LANES), 0)
    d1 = lax.broadcasted_iota(jnp.int32, (BIAS_LANES, BIAS_LANES), 1)
    m, n = (d0, d1) if rel_rows else (d1, d0)
    hit = (m == jnp.minimum(BAND - 1 + MAX_REL - n, 2 * MAX_REL)) & (n < BAND + CHUNK - 1)
    return jnp.where(hit, 1.0, 0.0).astype(F32)


def _bias_table(name, rel_bias_l):
    rel_pad = jnp.pad(rel_bias_l, ((0, 0), (0, BIAS_LANES - N_REL)))

    def body(r_ref, o_ref):
        diag = jnp.dot(r_ref[...], _diag_onehot(True), preferred_element_type=F32,
                       precision=lax.Precision.HIGHEST)
        rowid = lax.broadcasted_iota(jnp.int32, (8, BIAS_LANES), 0)
        lane = lax.broadcasted_iota(jnp.int32, (8, BIAS_LANES), 1)
        for h in range(HEADS):
            d8 = jnp.broadcast_to(diag[h:h + 1, :], (8, BIAS_LANES))
            slab0 = pltpu.roll(d8, BIAS_LANES - CHUNK + 1, axis=1)
            for b in range(1, 8):
                slab0 = jnp.where(rowid == b, pltpu.roll(d8, BIAS_LANES - CHUNK + 1 + b, axis=1),
                                  slab0)
            for a in range(8):
                slab = slab0 if a == 0 else pltpu.roll(slab0, 8 * a, axis=1)
                o_ref[h * CHUNK + 8 * a:h * CHUNK + 8 * a + 8, :] = jnp.where(lane < BAND, slab, NEG)

    tab = pl.pallas_call(
        body, name=name,
        in_specs=[pl.BlockSpec(memory_space=pltpu.VMEM)],
        out_specs=pl.BlockSpec(memory_space=pltpu.VMEM),
        out_shape=_sds((HEADS * CHUNK, BIAS_LANES), F32),
    )(rel_pad)
    tab = tab.reshape(HEADS // 2, 2, CHUNK, BIAS_LANES)
    return jnp.transpose(tab, (0, 3, 1, 2)).reshape(HEADS // 2, BIAS_LANES, 2 * CHUNK)


def _bias_fold(name, dbias_t):
    rows = HEADS * CHUNK
    dbias = jnp.transpose(dbias_t.reshape(HEADS // 2, BIAS_LANES, 2, CHUNK), (0, 2, 3, 1))

    def body(d_ref, o_ref):
        rowid = lax.broadcasted_iota(jnp.int32, (8, BIAS_LANES), 0)
        diags = []
        for h in range(HEADS):
            acc = d_ref[h * CHUNK + 56:h * CHUNK + 64, :]
            for a in range(7):
                slab = d_ref[h * CHUNK + 8 * a:h * CHUNK + 8 * a + 8, :]
                acc = acc + pltpu.roll(slab, 56 - 8 * a, axis=1)
            tot = jnp.where(rowid == 7, acc, 0.0)
            for b in range(7):
                tot = tot + jnp.where(rowid == b, pltpu.roll(acc, 7 - b, axis=1), 0.0)
            diags.append(jnp.sum(tot, axis=0, keepdims=True))
        diag = jnp.concatenate(diags, axis=0)
        o_ref[...] = jnp.dot(diag, _diag_onehot(False), preferred_element_type=F32,
                             precision=lax.Precision.HIGHEST)

    return pl.pallas_call(
        body, name=name,
        in_specs=[pl.BlockSpec(memory_space=pltpu.VMEM)],
        out_specs=pl.BlockSpec(memory_space=pltpu.VMEM),
        out_shape=_sds((HEADS, BIAS_LANES), F32),
    )(dbias.reshape(rows, BIAS_LANES))


def _inv_counts(i):
    trow = lax.broadcasted_iota(jnp.int32, (TOK + HALO, 1), 0) + i * TOK
    return [1.0 / jnp.minimum(trow + 1, w).astype(F32) for w in POOL_WINDOWS]


def _pool_fwd(name, proj, wg, scale, comm=None):
    t = proj.shape[0]
    hb = TOK // HALO

    def body(u_ref, up_ref, wg_ref, sc_ref, pooled_ref, mixed_ref, b0, b1, b2, b3):
        i = pl.program_id(0)
        halo = up_ref[...].astype(F32)
        b0[0:HALO, :] = jnp.where(i == 0, jnp.zeros_like(halo), halo)
        b0[HALO:, :] = u_ref[...].astype(F32)
        n = TOK + HALO
        b1[8:n, :] = b0[8:n, :] + b0[7:n - 1, :]
        b2[16:n, 128:] = b1[16:n, 128:] + b1[14:n - 2, 128:]
        b3[24:n, 256:] = b2[24:n, 256:] + b2[20:n - 4, 256:]
        wins = [b1[HALO:n, 0:128], b2[HALO:n, 128:256], b3[HALO:n, 256:384],
                b3[HALO:n, 384:512] + b3[HALO - 8:n - 8, 384:512]]
        inv = _inv_counts(i)
        for g in range(4):
            cols = slice(g * POOL_GD, (g + 1) * POOL_GD)
            pooled = (wins[g] * inv[g][0:TOK] - b0[HALO:n, cols]).astype(BF16)
            pooled_ref[:, cols] = pooled
            pre = jnp.dot(pooled, wg_ref[g], preferred_element_type=F32)
            mixed_ref[:, cols] = (pre * sc_ref[:, cols]).astype(BF16)

    buf = pltpu.VMEM((TOK + HALO, POOL_W), F32)
    outs, moved = _pcall(
        body, name, (t // TOK,),
        [_row_spec(POOL_W, 3),
         pl.BlockSpec((HALO, POOL_W), lambda i: (jnp.maximum(i * hb - 1, 0), 3)),
         pl.BlockSpec((4, POOL_GD, POOL_GD), lambda i: (0, 0, 0)), _vec_spec(POOL_W)],
        [_row_spec(POOL_W), _row_spec(POOL_W)],
        [_sds((t, POOL_W), BF16), _sds((t, POOL_W), BF16)], [buf, buf, buf, buf],
        _cparams("arbitrary"), (proj, proj, wg, scale), comm)
    return outs if comm is None else (*outs, moved)


def _pool_bwd(name, dmixed, pooled, wg, scale, comm=None):
    t = dmixed.shape[0]
    nt = t // TOK
    hb = TOK // HALO

    def body(dm_ref, dmn_ref, p_ref, wg_ref, sc_ref, du_ref, dwg_ref, dsc_ref, c0, c1, c2, c3):
        i = pl.program_id(0)

        @pl.when(i == 0)
        def _():
            dwg_ref[...] = jnp.zeros_like(dwg_ref)
            dsc_ref[...] = jnp.zeros_like(dsc_ref)

        n = TOK + HALO
        inv = _inv_counts(i)
        dmv = dm_ref[...].astype(F32)
        dmn = dmn_ref[...].astype(F32)
        dmn = jnp.where(i == nt - 1, jnp.zeros_like(dmn), dmn)
        for g in range(4):
            cols = slice(g * POOL_GD, (g + 1) * POOL_GD)
            scg = sc_ref[:, cols]
            pg = p_ref[:, cols]
            dpre = (dmv[:, cols] * scg).astype(BF16)
            dpre_n = (dmn[:, cols] * scg).astype(BF16)
            pre = jnp.dot(pg, wg_ref[g], preferred_element_type=F32)
            dsc_ref[:, cols] += jnp.sum(dmv[:, cols] * pre, axis=0, keepdims=True)
            dwg_ref[g] += lax.dot_general(pg, dpre, (TN, ((), ())), preferred_element_type=F32)
            dpool = lax.dot_general(dpre, wg_ref[g], (NT, ((), ())), preferred_element_type=F32)
            dpool_n = lax.dot_general(dpre_n, wg_ref[g], (NT, ((), ())),
                                      preferred_element_type=F32)
            c0[0:TOK, cols] = dpool
            c0[TOK:n, cols] = dpool_n
            c1[0:TOK, cols] = dpool * inv[g][0:TOK]
            c1[TOK:n, cols] = dpool_n * inv[g][TOK:n]
        c2[0:n - 8, :] = c1[0:n - 8, :] + c1[1:n - 7, :]
        c3[0:n - 16, 128:] = c2[0:n - 16, 128:] + c2[2:n - 14, 128:]
        c1[0:n - 24, 256:] = c3[0:n - 24, 256:] + c3[4:n - 20, 256:]
        wins = [c2[0:TOK, 0:128], c3[0:TOK, 128:256], c1[0:TOK, 256:384],
                c1[0:TOK, 384:512] + c1[8:TOK + 8, 384:512]]
        for g in range(4):
            cols = slice(g * POOL_GD, (g + 1) * POOL_GD)
            du_ref[:, cols] = (wins[g] - c0[0:TOK, cols]).astype(BF16)

    buf = pltpu.VMEM((TOK + HALO, POOL_W), F32)
    outs, moved = _pcall(
        body, name, (nt,),
        [_row_spec(POOL_W),
         pl.BlockSpec((HALO, POOL_W), lambda i: (jnp.minimum((i + 1) * hb, nt * hb - 1), 0)),
         _row_spec(POOL_W), pl.BlockSpec((4, POOL_GD, POOL_GD), lambda i: (0, 0, 0)),
         _vec_spec(POOL_W)],
        [_row_spec(POOL_W), pl.BlockSpec((4, POOL_GD, POOL_GD), lambda i: (0, 0, 0)),
         _vec_spec(POOL_W)],
        [_sds((t, POOL_W), BF16), _sds((4, POOL_GD, POOL_GD), F32), _sds((1, POOL_W), F32)],
        [buf, buf, buf, buf], _cparams("arbitrary"), (dmixed, dmixed, pooled, wg, scale), comm)
    return outs if comm is None else (*outs, moved)


GELU_C = math.sqrt(2.0 / math.pi)


GELU_K = 0.044715


def _gelu_parts(x):
    x2 = x * x
    s = 0.5 + 0.5 * jnp.tanh(x * (GELU_C + (GELU_C * GELU_K) * x2))
    return x * s, s, x2


def _gelu(x):
    return _gelu_parts(x)[0]


def _gelu_and_grad(x):
    g, s, x2 = _gelu_parts(x)
    return g, s + g * (1.0 - s) * ((2 * GELU_C) + (6 * GELU_C * GELU_K) * x2)


def _taps(buf, r, rows):
    a = buf[pl.ds(r, rows + 8), :]
    return a[8:], pltpu.roll(a, 1, axis=0)[8:], pltpu.roll(a, 2, axis=0)[8:]


def _conv(taps, w_ref, b_ref):
    return b_ref[...] + w_ref[2:3, :] * taps[0] + w_ref[1:2, :] * taps[1] + w_ref[0:1, :] * taps[2]


def _stage(dst, prev_ref, cur_ref, next_ref, first, last):
    rows = cur_ref.shape[0]
    h = prev_ref[...].astype(F32)
    dst[0:8, :] = jnp.where(first, jnp.zeros_like(h), h)
    dst[8:8 + rows, :] = cur_ref[...].astype(F32)
    if next_ref is not None:
        h = next_ref[...].astype(F32)
        dst[8 + rows:, :] = jnp.where(last, jnp.zeros_like(h), h)


FWD_STRIP = 32
BWD_STRIP = 16


def _ffn_gate_fwd(name, hu, conv_w, conv_b, comm=None):
    t = hu.shape[0]
    ncol = D_FF // FF_COL
    hb = FF_TOK // 8

    def tile(off):
        return pl.BlockSpec((FF_TOK, FF_COL), lambda i, j: (i, j + off))

    def halo(off):
        return pl.BlockSpec((8, FF_COL), lambda i, j: (jnp.maximum(i * hb - 1, 0), j + off))

    def wspec(off):
        return pl.BlockSpec((3, FF_COL), lambda i, j: (0, j + off))

    def bspec(off):
        return pl.BlockSpec((1, FF_COL), lambda i, j: (0, j + off))

    def body(v_ref, vp_ref, g_ref, gp_ref, wv_ref, wg_ref, bv_ref, bg_ref, a_ref, hc_ref, vb, gb):
        first = pl.program_id(0) == 0
        _stage(vb, vp_ref, v_ref, None, first, None)
        _stage(gb, gp_ref, g_ref, None, first, None)

        def strip(k, carry):
            for u in range(2):
                r = pl.multiple_of((2 * k + u) * FWD_STRIP, FWD_STRIP)
                val = _conv(_taps(vb, r, FWD_STRIP), wv_ref, bv_ref)
                gate = _conv(_taps(gb, r, FWD_STRIP), wg_ref, bg_ref)
                a_ref[pl.ds(r, FWD_STRIP), :] = (_gelu(gate) * val).astype(BF16)
                hc_ref[0, pl.ds(r, FWD_STRIP), :] = val.astype(BF16)
                hc_ref[1, pl.ds(r, FWD_STRIP), :] = gate.astype(BF16)
            return carry

        lax.fori_loop(0, FF_TOK // (2 * FWD_STRIP), strip, 0)

    buf = pltpu.VMEM((FF_TOK + 8, FF_COL), F32)
    outs, moved = _pcall(
        body, name, (t // FF_TOK, ncol),
        [tile(0), halo(0), tile(ncol), halo(ncol), wspec(0), wspec(ncol), bspec(0), bspec(ncol)],
        [pl.BlockSpec((FF_TOK, FF_COL), lambda i, j: (i, j)),
         pl.BlockSpec((2, FF_TOK, FF_COL), lambda i, j: (0, i, j))],
        [_sds((t, D_FF), BF16), _sds((2, t, D_FF), BF16)], [buf, buf],
        _cparams("arbitrary", "arbitrary"),
        (hu, hu, hu, hu, conv_w, conv_w, conv_b, conv_b), comm)
    return outs if comm is None else (*outs, moved)


def _ffn_gate_bwd(name, da, hu, hc, conv_w, comm=None):
    t = hu.shape[0]
    nt = t // FF_TOK
    ncol = D_FF // FF_COL
    hb = FF_TOK // 8

    def tile(off):
        return pl.BlockSpec((FF_TOK, FF_COL), lambda j, i: (i, j + off))

    def nxt_rows(i):
        return jnp.minimum((i + 1) * hb, nt * hb - 1)

    def wspec(off):
        return pl.BlockSpec((3, FF_COL), lambda j, i: (0, j + off))

    def body(da_ref, dan_ref, v_ref, g_ref, hc_ref, hcn_ref, wv_ref, wg_ref,
             dh_ref, dwv_ref, dwg_ref):
        i = pl.program_id(1)
        first, last = i == 0, i == nt - 1

        @pl.when(first)
        def _():
            dwv_ref[...] = jnp.zeros_like(dwv_ref)
            dwg_ref[...] = jnp.zeros_like(dwg_ref)

        def grads(dav, val, gate):
            g, dg = _gelu_and_grad(gate.astype(F32))
            dav = dav.astype(F32)
            return dav * g, dav * val.astype(F32) * dg

        def fold(x):
            return x[0:8] + x[8:16]

        def strip(j, carry):
            for u in range(2):
                carry = one_strip(2 * j + u, carry)
            return carry

        def one_strip(k, carry):
            r = pl.multiple_of(FF_TOK - BWD_STRIP - k * BWD_STRIP, BWD_STRIP)
            rows = pl.ds(r, BWD_STRIP)
            dval, dgate = grads(da_ref[rows, :], hc_ref[0, rows, :], hc_ref[1, rows, :])
            new = (dval[0:8], dgate[0:8])
            for half, (d, below, h_ref, w_ref, dw_ref) in enumerate((
                    (dval, carry[0], v_ref, wv_ref, dwv_ref),
                    (dgate, carry[1], g_ref, wg_ref, dwg_ref))):
                e = jnp.concatenate([d, below], axis=0)
                e1 = pltpu.roll(e, BWD_STRIP + 7, axis=0)[0:BWD_STRIP]
                e2 = pltpu.roll(e, BWD_STRIP + 6, axis=0)[0:BWD_STRIP]
                dh = w_ref[2:3, :] * d + w_ref[1:2, :] * e1 + w_ref[0:1, :] * e2
                dh_ref[half, rows, :] = dh.astype(BF16)
                huv = h_ref[rows, :].astype(F32)
                dw_ref[0:8, :] += fold(e2 * huv)
                dw_ref[8:16, :] += fold(e1 * huv)
                dw_ref[16:24, :] += fold(d * huv)
                dw_ref[24:32, :] += fold(d)
            return new

        dan = dan_ref[...]
        dan = jnp.where(last, jnp.zeros_like(dan), dan)
        lax.fori_loop(0, FF_TOK // (2 * BWD_STRIP), strip, grads(dan, hcn_ref[0], hcn_ref[1]))

        @pl.when(last)
        def _():
            for dw_ref in (dwv_ref, dwg_ref):
                for q in range(4):
                    dw_ref[8 * q:8 * q + 1, :] = jnp.sum(dw_ref[8 * q:8 * q + 8, :], axis=0,
                                                         keepdims=True)

    acc = pl.BlockSpec((32, FF_COL), lambda j, i: (0, j))
    (dhu, dwv, dwg), moved = _pcall(
        body, name, (ncol, nt),
        [tile(0), pl.BlockSpec((8, FF_COL), lambda j, i: (nxt_rows(i), j)),
         tile(0), tile(ncol),
         pl.BlockSpec((2, FF_TOK, FF_COL), lambda j, i: (0, i, j)),
         pl.BlockSpec((2, 8, FF_COL), lambda j, i: (0, nxt_rows(i), j)),
         wspec(0), wspec(ncol)],
        [pl.BlockSpec((2, FF_TOK, FF_COL), lambda j, i: (0, i, j)), acc, acc],
        [_sds((2, t, D_FF), BF16), _sds((32, D_FF), F32), _sds((32, D_FF), F32)],
        [], _cparams("arbitrary", "arbitrary"),
        (da, da, hu, hu, hc, hc, conv_w, conv_w), comm)
    dconv = jnp.concatenate([dwv, dwg], axis=1).reshape(4, 8, 2 * D_FF)[:, 0]
    return (dhu, dconv) if comm is None else (dhu, dconv, moved)


def _mesh_pos():
    x, y, c = lax.axis_index("x"), lax.axis_index("y"), lax.axis_index("c")
    return x, y, c, [(1 - x, y), (x, 1 - y), (1 - x, 1 - y)]


def _remote(src, dst, send_sems, recv_sems, i, dev):
    return pltpu.make_async_remote_copy(src_ref=src, dst_ref=dst, send_sem=send_sems.at[i],
                                        recv_sem=recv_sems.at[i], device_id=dev,
                                        device_id_type=MESH)


def _mine(c, rows):
    return pl.ds(pl.multiple_of(c * (rows // 2), 16), rows // 2)


def _gather_send(shards, conv_shard, gathered, l):
    nbig = len(shards)
    with_conv = conv_shard is not None
    if gathered is None:
        ins = list(shards) + ([conv_shard] if with_conv else [])
        outs = [_sds((DEPTH, N_CHIPS) + s.shape[1:], s.dtype) for s in ins]
        alias = {}
    else:
        ins = list(shards) + list(gathered)
        outs = [_sds(g.shape, g.dtype) for g in gathered]
        alias = {nbig + k: k for k in range(nbig)}

    def copies(cin, cout, ssem, rsem):
        x, y, c, chips = _mesh_pos()
        me = 2 * x + y
        out = []
        for k in range(nbig):
            rows = shards[k].shape[1]
            for j, (cx, cy) in enumerate(chips):
                out.append(_remote(cin[k].at[l, _mine(c, rows)], cout[k].at[l, me, _mine(c, rows)],
                                   ssem, rsem, 4 * k + j, (cx, cy, c)))
            out.append(_remote(cin[k].at[l], cout[k].at[l, me], ssem, rsem, 4 * k + 3,
                               (x, y, 1 - c)))
        if with_conv:
            base = 4 * nbig
            for j, (cx, cy) in enumerate(chips):
                out.append(_remote(cin[nbig].at[c], cout[nbig].at[c, me], ssem, rsem, base + j,
                                   (cx, cy, c)))
            for ll in range(DEPTH):
                out.append(_remote(cin[nbig].at[ll], cout[nbig].at[ll, me], ssem, rsem,
                                   base + 3 + ll, (x, y, 1 - c)))
        return out

    return _Comm(ins, outs, copies, 4 * nbig + 5, alias)


def _gather_forward(gathered, nbig, rows, l):
    with_conv = len(gathered) > nbig
    alias = {k: k for k in range(len(gathered))}

    def copies(cin, cout, ssem, rsem):
        x, y, c, chips = _mesh_pos()
        out = []
        for k in range(nbig):
            for j, (cx, cy) in enumerate(chips):
                blk = cout[k].at[l, 2 * cx + cy, _mine(c, rows[k])]
                out.append(_remote(blk, blk, ssem, rsem, 3 * k + j, (x, y, 1 - c)))
        if with_conv:
            for j, (cx, cy) in enumerate(chips):
                blk = cout[nbig].at[c, 2 * cx + cy]
                out.append(_remote(blk, blk, ssem, rsem, 3 * nbig + j, (x, y, 1 - c)))
        return out

    return _Comm(gathered, [_sds(g.shape, g.dtype) for g in gathered], copies, 3 * nbig + 3, alias)


def _reduce_swap(grads, l):
    def copies(cin, cout, ssem, rsem):
        x, y, c, _ = _mesh_pos()
        return [_remote(cin[k].at[l, :, _mine(1 - c, g.shape[2])], cout[k], ssem, rsem, k,
                        (x, y, 1 - c)) for k, g in enumerate(grads)]

    outs = [_sds((N_CHIPS, g.shape[2] // 2, g.shape[3]), g.dtype) for g in grads]
    return _Comm(grads, outs, copies, len(grads))


def _reduce_scatter(sums):
    def copies(cin, cout, ssem, rsem):
        x, y, c, chips = _mesh_pos()
        return [_remote(cin[k].at[2 * cx + cy], cout[k].at[j], ssem, rsem, 3 * k + j, (cx, cy, c))
                for k in range(len(sums)) for j, (cx, cy) in enumerate(chips)]

    outs = [_sds((3,) + s.shape[1:], s.dtype) for s in sums]
    return _Comm(sums, outs, copies, 3 * len(sums))


def _reduce_share(reds, l):
    def copies(cin, cout, ssem, rsem):
        x, y, c, _ = _mesh_pos()
        out = []
        for k, r in enumerate(reds):
            half = cout[k].at[l, _mine(c, r.shape[1])]
            out.append(_remote(half, half, ssem, rsem, k, (x, y, 1 - c)))
        return out

    return _Comm(reds, [_sds(r.shape, r.dtype) for r in reds], copies, len(reds),
                 {k: k for k in range(len(reds))})


def _allreduce_small(per_layer):
    kinds = len(per_layer[0])
    shapes = [a.shape[1:] if a.shape[0] == 1 else a.shape for a in per_layer[0]]

    def body(*refs):
        ins = refs[:DEPTH * kinds]
        outs = refs[DEPTH * kinds:(DEPTH + 1) * kinds]
        gbufs = refs[(DEPTH + 1) * kinds:(DEPTH + 2) * kinds]
        send_sems, recv_sems = refs[-2], refs[-1]
        x, y, c, chips = _mesh_pos()
        sibling = (x, y, 1 - c)

        def copy(k, i, block, to):
            px, py, pc = block
            slot = gbufs[k].at[4 * px + 2 * py + pc]
            return _remote(slot, slot, send_sems, recv_sems, 7 * k + i, to)

        me = (x, y, c)
        first, passed = [], []
        for k in range(kinds):
            for l in range(DEPTH):
                a = ins[l * kinds + k]
                if per_layer[l][k].shape[0] == 1:
                    gbufs[k][4 * x + 2 * y + c, l:l + 1] = a[...]
                else:
                    gbufs[k][4 * x + 2 * y + c, l] = a[...]
            first.append(copy(k, 0, me, sibling))
            first += [copy(k, 1 + j, me, (*chip, c)) for j, chip in enumerate(chips)]
            passed += [copy(k, 4 + j, (*chip, c), sibling) for j, chip in enumerate(chips)]
        for cp in first:
            cp.start()
        for k in range(kinds):
            for j, chip in enumerate(chips):
                copy(k, 1 + j, (*chip, c), me).wait_recv()
                passed[3 * k + j].start()
        for k in range(kinds):
            copy(k, 0, sibling, me).wait_recv()
            for j, chip in enumerate(chips):
                copy(k, 4 + j, (*chip, 1 - c), me).wait_recv()
        for cp in first + passed:
            cp.wait_send()
        for k in range(kinds):
            acc = gbufs[k][0]
            for d in range(1, 8):
                acc = acc + gbufs[k][d]
            outs[k][...] = acc

    vmem = pl.BlockSpec(memory_space=pltpu.VMEM)
    return pl.pallas_call(
        body, name="allreduce_small",
        in_specs=[vmem] * (DEPTH * kinds), out_specs=[vmem] * kinds,
        out_shape=[_sds((DEPTH,) + s, F32) for s in shapes],
        scratch_shapes=[pltpu.VMEM((8, DEPTH) + s, F32) for s in shapes]
        + [pltpu.SemaphoreType.DMA((7 * kinds,)), pltpu.SemaphoreType.DMA((7 * kinds,))],
        compiler_params=pltpu.CompilerParams(vmem_limit_bytes=VMEM_LIMIT_V7X),
    )(*per_layer[0], *per_layer[1])


def _adamw_small(ws, gs, ms, vs):
    n = len(ws)
    c1 = 1.0 - ADAM_B1 ** ADAM_STEP
    c2 = 1.0 - ADAM_B2 ** ADAM_STEP

    def body(*refs):
        for i in range(n):
            w_ref, g_ref, m_ref, v_ref = (refs[j * n + i] for j in range(4))
            d_ref, nm_ref, nv_ref = (refs[(4 + j) * n + i] for j in range(3))
            gv = g_ref[...]
            nm = ADAM_B1 * m_ref[...] + (1.0 - ADAM_B1) * gv
            nv = ADAM_B2 * v_ref[...] + (1.0 - ADAM_B2) * (gv * gv)
            nm_ref[...] = nm
            nv_ref[...] = nv
            d_ref[...] = -ADAM_LR * ((nm / c1) / (jnp.sqrt(nv / c2) + ADAM_EPS)
                                     + ADAM_WD * w_ref[...])

    vmem = pl.BlockSpec(memory_space=pltpu.VMEM)
    outs = pl.pallas_call(
        body, name="adamw_small", in_specs=[vmem] * (4 * n), out_specs=[vmem] * (3 * n),
        out_shape=[_sds(w.shape, F32) for w in ws] * 3,
        compiler_params=pltpu.CompilerParams(vmem_limit_bytes=VMEM_LIMIT_V7X),
    )(*ws, *gs, *ms, *vs)
    return outs[:n], outs[n:2 * n], outs[2 * n:]


def _core_index():
    return jnp.reshape(lax.axis_index("c"), (1,)).astype(jnp.int32)


def _chip_index():
    return jnp.reshape(2 * lax.axis_index("x") + lax.axis_index("y"), (1,)).astype(jnp.int32)


def _chip_sums(name, stacked, sibs, l):
    n = len(stacked)
    dims = [(s.shape[2] // 2, s.shape[3]) for s in stacked]

    def body(c_ref, *refs):
        for k in range(n):
            a_ref, b_ref, o_ref = refs[k], refs[n + k], refs[2 * n + k]
            o_ref[...] = (a_ref[...].astype(F32) + b_ref[...].astype(F32)).astype(BF16)

    return pl.pallas_call(
        body, name=name,
        grid_spec=pltpu.PrefetchScalarGridSpec(
            num_scalar_prefetch=1, grid=(N_CHIPS,),
            in_specs=[pl.BlockSpec((None, None, hr, cd), lambda j, cr: (l, j, cr[0], 0))
                      for hr, cd in dims]
            + [pl.BlockSpec((None, hr, cd), lambda j, cr: (j, 0, 0)) for hr, cd in dims],
            out_specs=[pl.BlockSpec((None, hr, cd), lambda j, cr: (j, 0, 0)) for hr, cd in dims]),
        out_shape=[_sds((N_CHIPS, hr, cd), BF16) for hr, cd in dims],
        compiler_params=_cparams("parallel"))(_core_index(), *stacked, *sibs)


def _final_sums(name, sums, recvs, l, fills):
    n = len(sums)
    dims = [(s.shape[1] // 2, s.shape[2]) for s in sums]
    filled = fills[0] is not None

    def body(m_ref, *refs):
        outs = refs[-n:]
        for k in range(n):
            acc = refs[k][...].astype(F32)
            for j in range(3):
                acc = acc + refs[n + k][j].astype(F32)
            outs[k][...] = acc

    in_specs = ([pl.BlockSpec((None, tr, cd), lambda i, mr: (mr[0], i, 0)) for tr, cd in dims]
                + [pl.BlockSpec((3, tr, cd), lambda i, mr: (0, i, 0)) for tr, cd in dims])
    args = [jnp.concatenate([_chip_index(), _core_index()]), *sums, *recvs]
    aliases = {}
    if filled:
        in_specs += [pl.BlockSpec(memory_space=pl.ANY)] * n
        args += list(fills)
        aliases = {1 + 2 * n + k: k for k in range(n)}
    return pl.pallas_call(
        body, name=name,
        grid_spec=pltpu.PrefetchScalarGridSpec(
            num_scalar_prefetch=1, grid=(2,), in_specs=in_specs,
            out_specs=[pl.BlockSpec((None, tr, cd), lambda i, mr: (l, 2 * mr[1] + i, 0))
                       for tr, cd in dims]),
        out_shape=[_sds((DEPTH, 4 * tr, cd), F32) for tr, cd in dims],
        input_output_aliases=aliases,
        compiler_params=_cparams("parallel"))(*args)


def _adamw(name, w, g, m, v, comm=None):
    nl, r, cdim = w.shape
    tr = r // 4 if r % 32 == 0 else r
    c1 = 1.0 - ADAM_B1 ** ADAM_STEP
    c2 = 1.0 - ADAM_B2 ** ADAM_STEP

    def body(w_ref, g_ref, m_ref, v_ref, d_ref, nm_ref, nv_ref):
        gv = g_ref[...]
        nm = ADAM_B1 * m_ref[...] + (1.0 - ADAM_B1) * gv
        nv = ADAM_B2 * v_ref[...] + (1.0 - ADAM_B2) * (gv * gv)
        nm_ref[...] = nm
        nv_ref[...] = nv
        d_ref[...] = -ADAM_LR * ((nm / c1) / (jnp.sqrt(nv / c2) + ADAM_EPS) + ADAM_WD * w_ref[...])

    spec = pl.BlockSpec((None, tr, cdim), lambda l, i: (l, i, 0))
    out = _sds(w.shape, F32)
    outs, moved = _pcall(body, name, (nl, r // tr), [spec] * 4, [spec] * 3, [out] * 3, [],
                         _cparams("arbitrary", "arbitrary"), (w, g, m, v), comm)
    return outs if comm is None else (*outs, moved)


def kernel(x, norm_mix_pre, w_in, b_gate, rel_bias, w_attn_out, w_pool_group, pool_scale, w_pool_out, w_o, norm_mix_post, norm_ffn_pre, w_up, conv_w, conv_b, w_down, norm_ffn_post, loss_target, m_norm_mix_pre, m_w_in, m_b_gate, m_rel_bias, m_w_attn_out, m_w_pool_group, m_pool_scale, m_w_pool_out, m_w_o, m_norm_mix_post, m_norm_ffn_pre, m_w_up, m_conv_w, m_conv_b, m_w_down, m_norm_ffn_post, v_norm_mix_pre, v_w_in, v_b_gate, v_rel_bias, v_w_attn_out, v_w_pool_group, v_pool_scale, v_w_pool_out, v_w_o, v_norm_mix_post, v_norm_ffn_pre, v_w_up, v_conv_w, v_conv_b, v_w_down, v_norm_ffn_post):
    t = x.shape[1]
    xs = x.reshape(t, D_MODEL)
    target = loss_target.reshape(t, D_MODEL)

    names = ["w_in", "w_attn_out", "w_pool_out", "w_o", "w_up", "w_down"]
    shards = [w.astype(BF16) for w in (w_in, w_attn_out, w_pool_out, w_o, w_up, w_down)]
    rows = [s.shape[1] for s in shards]
    nbig = len(shards)
    h, g = _norm_fwd("l0_norm_mix_pre", x.reshape(t, D_MODEL), norm_mix_pre[0:1],
                     _gather_send(shards[:1], conv_w, None, 0))
    g = _comm_call("gather0_forward", _gather_forward(g, 1, rows[:1], 0))
    cw_full = jnp.transpose(g[1], (0, 2, 1, 3)).reshape(DEPTH, 3, 2 * D_FF)
    g = g[:1]
    wg_bf = w_pool_group.astype(BF16)

    def views(gathered):
        win_g, wao_g, wpo_g, wo_g, wup_g, wdn_g = gathered
        return (win_g, wao_g, wpo_g, wo_g.reshape(DEPTH, D_MODEL, D_MODEL), wup_g,
                wdn_g.reshape(DEPTH, D_FF, D_MODEL))

    saved = []
    xcur = xs
    for l in range(DEPTH):
        tag = f"l{l}_"
        bias = _bias_table(tag + "bias_table", rel_bias[l])
        proj = _mm_nn_blocked(tag + "proj", h, g[0], l, BF16)
        if l == 0:
            att, probs, rest = _attn_fwd(tag + "attn_fwd", proj, bias,
                                         _gather_send(shards[1:], None, None, 0))
            pooled, mixed, rest = _pool_fwd(tag + "pool_fwd", proj, wg_bf[l], pool_scale[l:l + 1],
                                            _gather_forward(rest, nbig - 1, rows[1:], 0))
            g = g + rest
        else:
            att, probs = _attn_fwd(tag + "attn_fwd", proj, bias)
            pooled, mixed = _pool_fwd(tag + "pool_fwd", proj, wg_bf[l], pool_scale[l:l + 1])
        win_g, wao_g, wpo_g, wo_full, wup_g, wdn_full = views(g)
        ya = _narrow_nn(tag + "attn_out", att, wao_g, l)
        yb = _narrow_nn(tag + "pool_out", mixed, wpo_g, l)
        z = _gate_fwd(tag + "gate_fwd", proj, b_gate[l:l + 1], ya, yb)
        mix = _mm_nn(tag + "mix", z, wo_full, l, D_MODEL, F32)
        x1, h2 = _post_pre_fwd(tag + "norm_mix_post", xcur, mix, norm_mix_post[l:l + 1],
                               norm_ffn_pre[l:l + 1])
        if l == 0:
            hu, mixing = _mm_nn_blocked(tag + "ffn_up", h2, wup_g, l, BF16,
                                        _gather_send(shards[:4], None, g[:4], 1))
            a, hc, ffn_g = _ffn_gate_fwd(tag + "ffn_gate_fwd", hu, cw_full[l], conv_b[l:l + 1],
                                         _gather_send(shards[4:], None, g[4:], 1))
            g = mixing + ffn_g
            wdn_full = views(g)[5]
        else:
            hu = _mm_nn_blocked(tag + "ffn_up", h2, wup_g, l, BF16)
            a, hc = _ffn_gate_fwd(tag + "ffn_gate_fwd", hu, cw_full[l], conv_b[l:l + 1])
        f = _mm_nn(tag + "ffn_down", a, wdn_full, l, D_FF, F32)
        saved.append(dict(x=xcur, h=h, proj=proj, att=att, pooled=pooled, mixed=mixed, ya=ya,
                          yb=yb, z=z, mix=mix, x1=x1, h2=h2, hu=hu, hc=hc, a=a, f=f, probs=probs))
        if l == 0:
            xcur, h, g = _post_pre_fwd(tag + "norm_ffn_post", x1, f, norm_ffn_post[l:l + 1],
                                       norm_mix_pre[l + 1:l + 2], _gather_forward(g, nbig, rows, 1))
        elif l < DEPTH - 1:
            xcur, h = _post_pre_fwd(tag + "norm_ffn_post", x1, f, norm_ffn_post[l:l + 1],
                                    norm_mix_pre[l + 1:l + 2])
    win_g, wao_g, wpo_g, wo_full, wup_g, wdn_full = views(g)

    dy, df, d_nfpost, loss_local = _tail("tail", saved[-1]["x1"], saved[-1]["f"],
                                         norm_ffn_post[DEPTH - 1:DEPTH], target)
    loss = lax.psum(loss_local, ("x", "y", "c"))

    dx = dy
    dws = dict.fromkeys(names)
    reds = [None] * nbig
    small_grads = [None] * DEPTH
    ffn = [4, 5]
    outs3 = [1, 2, 3]

    def blocks(ks):
        return [dws[names[k]].reshape(DEPTH, N_CHIPS, rows[k], -1) for k in ks]

    def chip_sums(ks, sib, l):
        return _chip_sums(f"chip_sums{l}_" + names[ks[0]], blocks(ks), sib, l)

    def final_sums(ks, sums, recv, l):
        outs = _final_sums(f"final_sums{l}_" + names[ks[0]], sums, recv, l, [reds[k] for k in ks])
        for k, r in zip(ks, outs):
            reds[k] = r

    for l in reversed(range(DEPTH)):
        tag = f"l{l}_"
        sv = saved[l]
        every = list(range(nbig))
        if l == 0:
            da, sib = _mm_nt(tag + "ffn_down_dx", df, wdn_full, l, D_FF // 2, BF16,
                             _reduce_swap(blocks(every), 1))
            sums = chip_sums(every, sib, 1)
        else:
            da = _mm_nt(tag + "ffn_down_dx", df, wdn_full, l, D_FF // 2, BF16)
        dws["w_down"] = _mm_tn(tag + "ffn_down_dw", sv["a"], df, D_FF // 2, l, dws["w_down"])
        if l == 0:
            dhu, dconv, recv = _ffn_gate_bwd(tag + "ffn_gate_bwd", da, sv["hu"], sv["hc"],
                                             cw_full[l], _reduce_scatter(sums))
            final_sums(every, sums, recv, 1)
            dh2, reds = _mm_nt_blocked(tag + "ffn_up_dx", dhu, wup_g, l, F32,
                                       _reduce_share(reds, 1))
        else:
            dhu, dconv = _ffn_gate_bwd(tag + "ffn_gate_bwd", da, sv["hu"], sv["hc"], cw_full[l])
            dh2 = _mm_nt_blocked(tag + "ffn_up_dx", dhu, wup_g, l, F32)
        dws["w_up"] = _mm_tn_blocked(tag + "ffn_up_dw", sv["h2"], dhu, l, dws["w_up"])
        if l == 0:
            dx1, d_nfpre, dmix, d_nmpost, sib = _pre_post_bwd(
                tag + "norm_ffn_pre_bwd", dh2, sv["x1"], dx, norm_ffn_pre[l:l + 1], sv["mix"],
                norm_mix_post[l:l + 1], _reduce_swap(blocks(ffn), 0))
            sums = chip_sums(ffn, sib, 0)
        else:
            dx1, d_nfpre, dmix, d_nmpost = _pre_post_bwd(
                tag + "norm_ffn_pre_bwd", dh2, sv["x1"], dx, norm_ffn_pre[l:l + 1], sv["mix"],
                norm_mix_post[l:l + 1])
        dz = _mm_nt(tag + "mix_dx", dmix, wo_full, l, D_MODEL, BF16)
        dws["w_o"] = _mm_tn(tag + "mix_dw", sv["z"], dmix, D_MODEL, l, dws["w_o"])
        dya, dyb, dgates, d_bgate = _gate_bwd(tag + "gate_bwd", dz, sv["proj"], b_gate[l:l + 1],
                                              sv["ya"], sv["yb"])
        datt = _narrow_nt(tag + "attn_out_dx", dya, wao_g, l)
        dws["w_attn_out"] = _narrow_tn(tag + "attn_out_dw", sv["att"], dya, l, dws["w_attn_out"])
        dmixed = _narrow_nt(tag + "pool_out_dx", dyb, wpo_g, l)
        dws["w_pool_out"] = _narrow_tn(tag + "pool_out_dw", sv["mixed"], dyb, l, dws["w_pool_out"])
        if l == 0:
            du, d_wg, d_pscale, sib = _pool_bwd(tag + "pool_bwd", dmixed, sv["pooled"], wg_bf[l],
                                                pool_scale[l:l + 1], _reduce_swap(blocks(outs3), 0))
            sums3 = chip_sums(outs3, sib, 0)
            dqkv, dbias, recv = _attn_bwd(
                tag + "attn_bwd", sv["proj"], datt, sv["probs"],
                _both(_reduce_scatter(sums), _reduce_scatter(sums3)))
            final_sums(ffn, sums, recv[:len(ffn)], 0)
            final_sums(outs3, sums3, recv[len(ffn):], 0)
        else:
            du, d_wg, d_pscale = _pool_bwd(tag + "pool_bwd", dmixed, sv["pooled"], wg_bf[l],
                                           pool_scale[l:l + 1])
            dqkv, dbias = _attn_bwd(tag + "attn_bwd", sv["proj"], datt, sv["probs"])
        d_rel = _bias_fold(tag + "bias_fold", dbias)
        if l == 0:
            dh, shared = _proj_dx(tag + "proj_dx", dqkv, du, dgates, win_g, l,
                                  _reduce_share([reds[k] for k in ffn + outs3], 0))
            for k, r in zip(ffn + outs3, shared):
                reds[k] = r
        else:
            dh = _proj_dx(tag + "proj_dx", dqkv, du, dgates, win_g, l)
        dws["w_in"] = _proj_dw(tag + "proj_dw", sv["h"], dqkv, du, dgates, l, dws["w_in"])
        small_grads[l] = [None, d_nmpost, d_nfpre, d_nfpost, d_bgate, d_rel, d_wg, d_pscale, dconv]
        if l > 0:
            dx, small_grads[l][0], df, d_nfpost = _pre_post_bwd(
                tag + "norm_mix_pre_bwd", dh, sv["x"], dx1, norm_mix_pre[l:l + 1],
                saved[l - 1]["f"], norm_ffn_post[l - 1:l])
        else:
            dx, small_grads[l][0], sib = _norm_pre_bwd(
                tag + "norm_mix_pre_bwd", dh, sv["x"], dx1, norm_mix_pre[l:l + 1],
                _reduce_swap(blocks([0]), 0))

    grad_x = dx.reshape(x.shape)

    delta, new_m, new_v = {}, {}, {}
    sums = chip_sums([0], sib, 0)
    delta["w_up"], new_m["w_up"], new_v["w_up"], recv = _adamw(
        "adamw_w_up", w_up, reds[4], m_w_up, v_w_up, _reduce_scatter(sums))
    final_sums([0], sums, recv, 0)
    delta["w_down"], new_m["w_down"], new_v["w_down"], shared = _adamw(
        "adamw_w_down", w_down, reds[5], m_w_down, v_w_down, _reduce_share([reds[0]], 0))
    g_big = shared + reds[1:]

    (g_nmpre, g_nmpost, g_nfpre, g_nfpost, g_bgate, g_rel, g_wg, g_pscale,
     g_conv) = _allreduce_small(small_grads)
    g_rel = g_rel[:, :, :N_REL]
    g_cb = g_conv[:, 3]
    ncw = conv_w.shape[2]
    chip = 2 * lax.axis_index("x") + lax.axis_index("y")
    g_cw = lax.dynamic_slice_in_dim(g_conv[:, 0:3], chip * ncw, ncw, axis=2)

    grads = dict(norm_mix_pre=g_nmpre, w_in=g_big[0], b_gate=g_bgate, rel_bias=g_rel,
                 w_attn_out=g_big[1], w_pool_group=g_wg, pool_scale=g_pscale, w_pool_out=g_big[2],
                 w_o=g_big[3], norm_mix_post=g_nmpost, norm_ffn_pre=g_nfpre, w_up=g_big[4],
                 conv_w=g_cw, conv_b=g_cb, w_down=g_big[5], norm_ffn_post=g_nfpost)
    weights = dict(norm_mix_pre=norm_mix_pre, w_in=w_in, b_gate=b_gate, rel_bias=rel_bias,
                   w_attn_out=w_attn_out, w_pool_group=w_pool_group, pool_scale=pool_scale,
                   w_pool_out=w_pool_out, w_o=w_o, norm_mix_post=norm_mix_post,
                   norm_ffn_pre=norm_ffn_pre, w_up=w_up, conv_w=conv_w, conv_b=conv_b,
                   w_down=w_down, norm_ffn_post=norm_ffn_post)
    moms = dict(norm_mix_pre=(m_norm_mix_pre, v_norm_mix_pre), w_in=(m_w_in, v_w_in),
                b_gate=(m_b_gate, v_b_gate), rel_bias=(m_rel_bias, v_rel_bias),
                w_attn_out=(m_w_attn_out, v_w_attn_out),
                w_pool_group=(m_w_pool_group, v_w_pool_group),
                pool_scale=(m_pool_scale, v_pool_scale), w_pool_out=(m_w_pool_out, v_w_pool_out),
                w_o=(m_w_o, v_w_o), norm_mix_post=(m_norm_mix_post, v_norm_mix_post),
                norm_ffn_pre=(m_norm_ffn_pre, v_norm_ffn_pre), w_up=(m_w_up, v_w_up),
                conv_w=(m_conv_w, v_conv_w), conv_b=(m_conv_b, v_conv_b),
                w_down=(m_w_down, v_w_down), norm_ffn_post=(m_norm_ffn_post, v_norm_ffn_post))
    order = list(weights.keys())

    small_names = [nm for nm in order if nm not in names]
    for nm in names:
        if nm not in delta:
            delta[nm], new_m[nm], new_v[nm] = _adamw("adamw_" + nm, weights[nm], grads[nm],
                                                     *moms[nm])
    d_s, m_s, v_s = _adamw_small([weights[nm] for nm in small_names],
                                 [grads[nm] for nm in small_names],
                                 [moms[nm][0] for nm in small_names],
                                 [moms[nm][1] for nm in small_names])
    for i, nm in enumerate(small_names):
        delta[nm], new_m[nm], new_v[nm] = d_s[i], m_s[i], v_s[i]

    return (loss, grad_x, *[grads[nm] for nm in order], *[delta[nm] for nm in order],
            *[new_m[nm] for nm in order], *[new_v[nm] for nm in order])
```

```python
import functools
import math

import jax
import jax.numpy as jnp
from jax import lax
from jax.experimental import pallas as pl
from jax.experimental.pallas import tpu as pltpu

F32 = jnp.float32
BF16 = jnp.bfloat16
MESH = pl.DeviceIdType.MESH

D_MODEL = 1024
DEPTH = 2
CHUNK = 64
BAND_CHUNKS = 9
BAND = BAND_CHUNKS * CHUNK
HEADS = 8
HEAD_DIM = 64
ATTN_W = HEADS * HEAD_DIM
POOL_WINDOWS = (2, 4, 8, 16)
POOL_W = 512
POOL_GD = 128
MAX_REL = 256
N_REL = 2 * MAX_REL + 1
D_FF = 2816
IN_W = 3 * ATTN_W + POOL_W + 2 * D_MODEL
EPS = 1e-6
ATTN_SCALE = HEAD_DIM ** -0.5
BAND_PAD = 640
BIAS_LANES = BAND_PAD
N_CHIPS = 4

ADAM_LR = 0.001
ADAM_B1 = 0.9
ADAM_B2 = 0.999
ADAM_EPS = 1e-08
ADAM_WD = 0.01
ADAM_STEP = 10

VMEM_LIMIT_V7X = 56 * 1024 * 1024
TOK = 512
ATT_BLK = 8 * CHUNK
FF_COL = 256
FF_TOK = 1024
HALO = 32


def _cparams(*sem):
    return pltpu.CompilerParams(dimension_semantics=sem, vmem_limit_bytes=VMEM_LIMIT_V7X)


def _sds(shape, dtype):
    return jax.ShapeDtypeStruct(shape, dtype)


class _Comm:
    def __init__(self, ins, outs, copies, n_sems, alias=None):
        self.ins, self.outs, self.copies, self.n_sems = list(ins), list(outs), copies, n_sems
        self.alias = dict(alias or {})


class _SemsFrom:
    def __init__(self, sems, start):
        self.sems, self.start = sems, start

    @property
    def at(self):
        return self

    def __getitem__(self, i):
        return self.sems.at[self.start + i]


def _both(a, b):
    na, nao = len(a.ins), len(a.outs)

    def copies(cin, cout, ssem, rsem):
        return (a.copies(cin[:na], cout[:nao], ssem, rsem)
                + b.copies(cin[na:], cout[nao:], _SemsFrom(ssem, a.n_sems), _SemsFrom(rsem, a.n_sems)))

    alias = dict(a.alias)
    alias.update({na + i: nao + o for i, o in b.alias.items()})
    return _Comm(a.ins + b.ins, a.outs + b.outs, copies, a.n_sems + b.n_sems, alias)


def _pcall(body, name, grid, in_specs, out_specs, out_shape, scratch_shapes, compiler_params, args,
           comm=None, aliases=None):
    single = not isinstance(out_shape, (list, tuple))
    out_specs = [out_specs] if single else list(out_specs)
    out_shape = [out_shape] if single else list(out_shape)
    n_in, n_out = len(in_specs), len(out_specs)
    aliases = dict(aliases or {})
    if comm is None:
        res = pl.pallas_call(
            body, name=name, grid=grid, in_specs=list(in_specs), out_specs=out_specs,
            out_shape=out_shape, scratch_shapes=list(scratch_shapes),
            input_output_aliases=aliases, compiler_params=compiler_params)(*args)
        return (res[0] if single else res), None
    ci, co = len(comm.ins), len(comm.outs)

    def hosted(*refs):
        main_in, cin = refs[:n_in], refs[n_in:n_in + ci]
        main_out = refs[n_in + ci:n_in + ci + n_out]
        cout = refs[n_in + ci + n_out:n_in + ci + n_out + co]
        rest = refs[n_in + ci + n_out + co:]
        copies = comm.copies(cin, cout, rest[-2], rest[-1])
        ids = [pl.program_id(a) for a in range(len(grid))]
        first = functools.reduce(jnp.logical_and, [i == 0 for i in ids])
        last = functools.reduce(jnp.logical_and, [i == g - 1 for i, g in zip(ids, grid)])

        @pl.when(first)
        def _():
            for cp in copies:
                cp.start()

        body(*main_in, *main_out, *rest[:-2])

        @pl.when(last)
        def _():
            for cp in copies:
                cp.wait()

    for i, o in comm.alias.items():
        aliases[n_in + i] = n_out + o
    hbm = pl.BlockSpec(memory_space=pl.ANY)
    sems = pltpu.SemaphoreType.DMA((comm.n_sems,))
    res = pl.pallas_call(
        hosted, name=name, grid=grid, in_specs=list(in_specs) + [hbm] * ci,
        out_specs=out_specs + [hbm] * co, out_shape=out_shape + comm.outs,
        scratch_shapes=list(scratch_shapes) + [sems, sems],
        input_output_aliases=aliases, compiler_params=compiler_params)(*args, *comm.ins)
    return (res[0] if single else list(res[:n_out])), list(res[n_out:])


def _comm_call(name, comm):
    ci = len(comm.ins)

    def body(*refs):
        copies = comm.copies(refs[:ci], refs[ci:-2], refs[-2], refs[-1])
        for cp in copies:
            cp.start()
        for cp in copies:
            cp.wait()

    hbm = pl.BlockSpec(memory_space=pl.ANY)
    sems = pltpu.SemaphoreType.DMA((comm.n_sems,))
    return list(pl.pallas_call(
        body, name=name, in_specs=[hbm] * ci, out_specs=[hbm] * len(comm.outs),
        out_shape=comm.outs, scratch_shapes=[sems, sems],
        input_output_aliases=comm.alias)(*comm.ins))


def _matmul(name, a, b, a_spec, b_spec, o_spec, out_shape, grid, contract, nk, acc_shape,
            fill=None, comm=None):
    in_place = out_shape.dtype == F32

    def body(*refs):
        a_ref, b_ref = refs[0], refs[1]
        o_ref = refs[2 if fill is None else 3]
        scratch = refs[(3 if fill is None else 4):]
        part = lax.dot_general(a_ref[...], b_ref[...], (contract, ((), ())),
                               preferred_element_type=F32)
        if nk == 1:
            o_ref[...] = part.astype(o_ref.dtype)
        else:
            acc_ref = o_ref if in_place else scratch[0]
            k = pl.program_id(2)

            @pl.when(k == 0)
            def _():
                acc_ref[...] = part

            @pl.when(k > 0)
            def _():
                acc_ref[...] += part

            if not in_place:
                @pl.when(k == nk - 1)
                def _():
                    o_ref[...] = acc_ref[...].astype(o_ref.dtype)

    scratch = [] if nk == 1 or in_place else [pltpu.VMEM(acc_shape, F32)]
    in_specs, args, aliases = [a_spec, b_spec], [a, b], {}
    if fill is not None:
        in_specs.append(pl.BlockSpec(memory_space=pl.ANY))
        args.append(fill)
        aliases = {2: 0}
    out, moved = _pcall(body, name, grid, in_specs, o_spec, out_shape, scratch,
                        _cparams("parallel", "parallel", "arbitrary"), args, comm, aliases)
    return out if comm is None else (out, moved)


NN = ((1,), (0,))
NT = ((1,), (1,))
TN = ((0,), (0,))


def _tm(t):
    return min(t, 1024)


def _tt(t):
    return min(t, 2048)


def _col_block_spec(a, rows, nb, row_col):
    if a.ndim == 2:
        return pl.BlockSpec((rows, nb), row_col)

    def halves(*ids):
        r, c = row_col(*ids)
        return c // 2, r, c % 2

    return pl.BlockSpec((None, rows, nb), halves)


def _mm_nn_blocked(name, a, w, l, out_dtype, comm=None):
    t, k = a.shape
    nb = w.shape[3]
    tm = _tm(t)
    return _matmul(
        name, a, w,
        pl.BlockSpec((tm, k), lambda i, n, kk: (i, 0)),
        pl.BlockSpec((None, None, k, nb), lambda i, n, kk: (l, n, 0, 0)),
        pl.BlockSpec((tm, nb), lambda i, n, kk: (i, n)),
        _sds((t, N_CHIPS * nb), out_dtype), (t // tm, N_CHIPS, 1), NN, 1, None, comm=comm)


def _mm_nt_blocked(name, a, w, l, out_dtype, comm=None):
    t = a.shape[-2]
    k, nb = w.shape[2], w.shape[3]
    tm = _tm(t)
    return _matmul(
        name, a, w,
        _col_block_spec(a, tm, nb, lambda i, n, kk: (i, kk)),
        pl.BlockSpec((None, None, k, nb), lambda i, n, kk: (l, kk, 0, 0)),
        pl.BlockSpec((tm, k), lambda i, n, kk: (i, 0)),
        _sds((t, k), out_dtype), (t // tm, 1, N_CHIPS), NT, N_CHIPS, (tm, k), comm=comm)


def _mm_tn_blocked(name, a, g, l, fill):
    t, k = a.shape
    nb = g.shape[-1] * (g.ndim - 1) // N_CHIPS
    tt = _tt(t)
    nt = t // tt
    return _matmul(
        name, a, g,
        pl.BlockSpec((tt, k), lambda n, j, kk: (kk, 0)),
        _col_block_spec(g, tt, nb, lambda n, j, kk: (kk, n)),
        pl.BlockSpec((None, None, k, nb), lambda n, j, kk: (l, n, 0, 0)),
        _sds((DEPTH, N_CHIPS, k, nb), BF16), (N_CHIPS, 1, nt), TN, nt, (k, nb), fill)


def _proj_pieces(rows, dqkv_first):
    def piece(col):
        if dqkv_first:
            return pl.BlockSpec((rows, ATTN_W), lambda i, kk: (i, col))
        return pl.BlockSpec((rows, ATTN_W), lambda n, kk: (kk, col))
    return [piece(0), piece(1), piece(2), piece(0)]


def _proj_dx(name, dqkv, du, dgates, w, l, comm=None):
    t = du.shape[0]
    k, nb = w.shape[2], w.shape[3]
    tm = _tm(t)

    def body(dq_ref, dk_ref, dv_ref, du_ref, dg_ref, w_ref, o_ref):
        kk = pl.program_id(1)

        def mm(a):
            return lax.dot_general(a, w_ref[...], (NT, ((), ())), preferred_element_type=F32)

        @pl.when(kk == 0)
        def _():
            o_ref[...] = mm(jnp.concatenate([dq_ref[...], dk_ref[...]], axis=1))

        @pl.when(kk == 1)
        def _():
            o_ref[...] += mm(jnp.concatenate([dv_ref[...], du_ref[...]], axis=1))

        @pl.when(kk >= 2)
        def _():
            o_ref[...] += mm(dg_ref[...])

    out, moved = _pcall(
        body, name, (t // tm, N_CHIPS),
        _proj_pieces(tm, True)
        + [pl.BlockSpec((tm, nb), lambda i, kk: (i, jnp.maximum(kk - 2, 0))),
           pl.BlockSpec((None, None, k, nb), lambda i, kk: (l, kk, 0, 0))],
        pl.BlockSpec((tm, k), lambda i, kk: (i, 0)), _sds((t, k), F32),
        [], _cparams("arbitrary", "arbitrary"),
        (dqkv, dqkv, dqkv, du, dgates, w), comm)
    return out if comm is None else (out, moved)


def _proj_dw(name, h, dqkv, du, dgates, l, fill):
    t, k = h.shape
    nb = dgates.shape[1] // 2
    tt = _tm(t)
    nt = t // tt

    def body(*refs):
        h_ref, dq_ref, dk_ref, dv_ref, du_ref, dg_ref = refs[:6]
        o_ref, acc_ref = refs[-2], refs[-1]
        n, kk = pl.program_id(0), pl.program_id(1)

        def update(g):
            part = lax.dot_general(h_ref[...], g, (TN, ((), ())), preferred_element_type=F32)

            @pl.when(kk == 0)
            def _():
                acc_ref[...] = part

            @pl.when(kk > 0)
            def _():
                acc_ref[...] += part

        @pl.when(n == 0)
        def _():
            update(jnp.concatenate([dq_ref[...], dk_ref[...]], axis=1))

        @pl.when(n == 1)
        def _():
            update(jnp.concatenate([dv_ref[...], du_ref[...]], axis=1))

        @pl.when(n >= 2)
        def _():
            update(dg_ref[...])

        @pl.when(kk == nt - 1)
        def _():
            o_ref[...] = acc_ref[...].astype(BF16)

    in_specs = ([pl.BlockSpec((tt, k), lambda n, kk: (kk, 0))] + _proj_pieces(tt, False)
                + [pl.BlockSpec((tt, nb), lambda n, kk: (kk, jnp.maximum(n - 2, 0)))])
    args, aliases = [h, dqkv, dqkv, dqkv, du, dgates], {}
    if fill is not None:
        in_specs.append(pl.BlockSpec(memory_space=pl.ANY))
        args.append(fill)
        aliases = {6: 0}
    return pl.pallas_call(
        body, name=name, grid=(N_CHIPS, nt), in_specs=in_specs,
        out_specs=pl.BlockSpec((None, None, k, nb), lambda n, kk: (l, n, 0, 0)),
        out_shape=_sds((DEPTH, N_CHIPS, k, nb), BF16),
        scratch_shapes=[pltpu.VMEM((k, nb), F32)], input_output_aliases=aliases,
        compiler_params=_cparams("parallel", "arbitrary"))(*args)


def _narrow_nn(name, a, w, l):
    t, k = a.shape
    nb = w.shape[3]
    tm = _tm(t)

    def body(a_ref, w_ref, o_ref):
        av = a_ref[...]
        for j in range(N_CHIPS):
            o_ref[:, j * nb:(j + 1) * nb] = jnp.dot(
                av, w_ref[j], preferred_element_type=F32).astype(BF16)

    return pl.pallas_call(
        body, name=name, grid=(t // tm,),
        in_specs=[pl.BlockSpec((tm, k), lambda i: (i, 0)),
                  pl.BlockSpec((None, N_CHIPS, k, nb), lambda i: (l, 0, 0, 0))],
        out_specs=pl.BlockSpec((tm, N_CHIPS * nb), lambda i: (i, 0)),
        out_shape=_sds((t, N_CHIPS * nb), BF16), compiler_params=_cparams("parallel"))(a, w)


def _narrow_nt(name, a, w, l):
    t = a.shape[0]
    k, nb = w.shape[2], w.shape[3]
    tm = _tm(t)

    def body(a_ref, w_ref, o_ref):
        acc = lax.dot_general(a_ref[:, 0:nb], w_ref[0], (NT, ((), ())), preferred_element_type=F32)
        for j in range(1, N_CHIPS):
            acc = acc + lax.dot_general(a_ref[:, j * nb:(j + 1) * nb], w_ref[j], (NT, ((), ())),
                                        preferred_element_type=F32)
        o_ref[...] = acc.astype(BF16)

    return pl.pallas_call(
        body, name=name, grid=(t // tm,),
        in_specs=[pl.BlockSpec((tm, N_CHIPS * nb), lambda i: (i, 0)),
                  pl.BlockSpec((None, N_CHIPS, k, nb), lambda i: (l, 0, 0, 0))],
        out_specs=pl.BlockSpec((tm, k), lambda i: (i, 0)),
        out_shape=_sds((t, k), BF16), compiler_params=_cparams("parallel"))(a, w)


def _narrow_tn(name, a, g, l, fill):
    t, k = a.shape
    nb = g.shape[1] // N_CHIPS
    tt = _tm(t)
    nt = t // tt

    def body(*refs):
        a_ref, g_ref, o_ref, acc_ref = refs[0], refs[1], refs[-2], refs[-1]
        i = pl.program_id(0)
        part = lax.dot_general(a_ref[...], g_ref[...], (TN, ((), ())), preferred_element_type=F32)

        @pl.when(i == 0)
        def _():
            acc_ref[...] = part

        @pl.when(i > 0)
        def _():
            acc_ref[...] += part

        @pl.when(i == nt - 1)
        def _():
            for j in range(N_CHIPS):
                o_ref[j] = acc_ref[:, j * nb:(j + 1) * nb].astype(BF16)

    in_specs = [pl.BlockSpec((tt, k), lambda i: (i, 0)),
                pl.BlockSpec((tt, N_CHIPS * nb), lambda i: (i, 0))]
    args, aliases = [a, g], {}
    if fill is not None:
        in_specs.append(pl.BlockSpec(memory_space=pl.ANY))
        args.append(fill)
        aliases = {2: 0}
    return pl.pallas_call(
        body, name=name, grid=(nt,), in_specs=in_specs,
        out_specs=pl.BlockSpec((None, N_CHIPS, k, nb), lambda i: (l, 0, 0, 0)),
        out_shape=_sds((DEPTH, N_CHIPS, k, nb), BF16),
        scratch_shapes=[pltpu.VMEM((k, N_CHIPS * nb), F32)], input_output_aliases=aliases,
        compiler_params=_cparams("arbitrary"))(*args)


def _mm_nn(name, a, w, l, tk, out_dtype):
    t, k = a.shape
    n = w.shape[2]
    tm = _tm(t)
    nk = k // tk
    return _matmul(
        name, a, w,
        pl.BlockSpec((tm, tk), lambda i, j, kk: (i, kk)),
        pl.BlockSpec((None, tk, n), lambda i, j, kk: (l, kk, 0)),
        pl.BlockSpec((tm, n), lambda i, j, kk: (i, 0)),
        _sds((t, n), out_dtype), (t // tm, 1, nk), NN, nk, (tm, n))


def _mm_nt(name, a, w, l, tn, out_dtype, comm=None):
    t, n = a.shape
    k = w.shape[1]
    tm = _tm(t)
    return _matmul(
        name, a, w,
        pl.BlockSpec((tm, n), lambda i, j, kk: (i, 0)),
        pl.BlockSpec((None, tn, n), lambda i, j, kk: (l, j, 0)),
        pl.BlockSpec((tm, tn), lambda i, j, kk: (i, j)),
        _sds((t, k), out_dtype), (t // tm, k // tn, 1), NT, 1, None, comm=comm)


def _mm_tn(name, a, g, tko, l, fill):
    t, k = a.shape
    n = g.shape[1]
    tt = _tt(t)
    nt = t // tt
    return _matmul(
        name, a, g,
        pl.BlockSpec((tt, tko), lambda i, j, kk: (kk, i)),
        pl.BlockSpec((tt, n), lambda i, j, kk: (kk, 0)),
        pl.BlockSpec((None, tko, n), lambda i, j, kk: (l, i, 0)),
        _sds((DEPTH, k, n), BF16), (k // tko, 1, nt), TN, nt, (tko, n), fill)


def _row_spec(width, col=0):
    return pl.BlockSpec((TOK, width), lambda i: (i, col))


def _vec_spec(width):
    return pl.BlockSpec((1, width), lambda i: (0, 0))


def _rms(x):
    return lax.rsqrt(jnp.mean(x * x, axis=-1, keepdims=True) + EPS)


def _norm_fwd(name, x, g, comm=None):
    t = x.shape[0]

    def body(x_ref, g_ref, h_ref):
        xv = x_ref[...]
        h_ref[...] = (xv * _rms(xv) * g_ref[...]).astype(BF16)

    out, moved = _pcall(body, name, (t // TOK,), [_row_spec(D_MODEL), _vec_spec(D_MODEL)],
                        _row_spec(D_MODEL), _sds((t, D_MODEL), BF16), [], _cparams("arbitrary"),
                        (x, g), comm)
    return out if comm is None else (out, moved)


ROWS = 16
ROW_UNROLL = 8


def _rows(k):
    return pl.ds(pl.multiple_of(k * ROWS, ROWS), ROWS)


def _strips(step, init):
    def group(j, carry):
        for u in range(ROW_UNROLL):
            carry = step(j * ROW_UNROLL + u, carry)
        return carry

    return lax.fori_loop(0, TOK // (ROWS * ROW_UNROLL), group, init)


def _fold_rows(x):
    return x[0:8] + x[8:16]


def _accumulate(ref, part):
    total = jnp.sum(part, axis=0, keepdims=True)

    @pl.when(pl.program_id(0) == 0)
    def _():
        ref[...] = total

    @pl.when(pl.program_id(0) > 0)
    def _():
        ref[...] += total


def _norm_bwd_rows(d, mv, g):
    r = _rms(mv)
    n = mv * r
    dn = d * g
    return r * (dn - n * jnp.mean(dn * n, axis=-1, keepdims=True)), d * n


def _post_pre_fwd(name, xres, m, g_post, g_pre, comm=None):
    t = xres.shape[0]

    def body(x_ref, m_ref, gp_ref, gn_ref, x1_ref, h_ref):
        def strip(k, c):
            rows = _rows(k)
            mv = m_ref[rows, :]
            x1 = x_ref[rows, :] + mv * _rms(mv) * gp_ref[...]
            x1_ref[rows, :] = x1
            h_ref[rows, :] = (x1 * _rms(x1) * gn_ref[...]).astype(BF16)
            return c

        _strips(strip, 0)

    outs, moved = _pcall(
        body, name, (t // TOK,),
        [_row_spec(D_MODEL), _row_spec(D_MODEL), _vec_spec(D_MODEL), _vec_spec(D_MODEL)],
        [_row_spec(D_MODEL), _row_spec(D_MODEL)],
        [_sds((t, D_MODEL), F32), _sds((t, D_MODEL), BF16)], [], _cparams("arbitrary"),
        (xres, m, g_post, g_pre), comm)
    return outs if comm is None else (*outs, moved)


def _tail(name, xres, m, g_post, target):
    t = xres.shape[0]

    def body(x_ref, m_ref, g_ref, t_ref, dy_ref, dm_ref, dg_ref, l_ref):
        def strip(k, carry):
            rows = _rows(k)
            mv = m_ref[rows, :]
            e = x_ref[rows, :] + mv * _rms(mv) * g_ref[...] - t_ref[rows, :]
            dy = e * (1.0 / D_MODEL)
            dy_ref[rows, :] = dy
            dm, dgn = _norm_bwd_rows(dy, mv, g_ref[...])
            dm_ref[rows, :] = dm.astype(BF16)
            return carry[0] + _fold_rows(dgn), carry[1] + _fold_rows(e * e)

        zero = jnp.zeros((8, D_MODEL), F32)
        dg, sq = _strips(strip, (zero, zero))
        _accumulate(dg_ref, dg)
        _accumulate(l_ref, jnp.sum(sq, axis=1, keepdims=True))

    dy, dm, dg, sq = pl.pallas_call(
        body, name=name, grid=(t // TOK,),
        in_specs=[_row_spec(D_MODEL), _row_spec(D_MODEL), _vec_spec(D_MODEL), _row_spec(D_MODEL)],
        out_specs=[_row_spec(D_MODEL), _row_spec(D_MODEL), _vec_spec(D_MODEL),
                   pl.BlockSpec((1, 1), lambda i: (0, 0))],
        out_shape=[_sds((t, D_MODEL), F32), _sds((t, D_MODEL), BF16), _sds((1, D_MODEL), F32),
                   _sds((1, 1), F32)],
        compiler_params=_cparams("arbitrary"))(xres, m, g_post, target)
    return dy, dm, dg, sq[0, 0] * (0.5 / D_MODEL)


def _pre_post_bwd(name, dh, xin, dxo, g_pre, m, g_post, comm=None):
    t = dh.shape[0]

    def body(dh_ref, x_ref, d_ref, gq_ref, m_ref, gp_ref, dx_ref, dgq_ref, dm_ref, dgp_ref):
        def strip(k, carry):
            rows = _rows(k)
            dxin, dgq = _norm_bwd_rows(dh_ref[rows, :], x_ref[rows, :], gq_ref[...])
            dx = d_ref[rows, :] + dxin
            dx_ref[rows, :] = dx
            dm, dgp = _norm_bwd_rows(dx, m_ref[rows, :], gp_ref[...])
            dm_ref[rows, :] = dm.astype(BF16)
            return carry[0] + _fold_rows(dgq), carry[1] + _fold_rows(dgp)

        zero = jnp.zeros((8, D_MODEL), F32)
        dgq, dgp = _strips(strip, (zero, zero))
        _accumulate(dgq_ref, dgq)
        _accumulate(dgp_ref, dgp)

    outs, moved = _pcall(
        body, name, (t // TOK,),
        [_row_spec(D_MODEL), _row_spec(D_MODEL), _row_spec(D_MODEL), _vec_spec(D_MODEL),
         _row_spec(D_MODEL), _vec_spec(D_MODEL)],
        [_row_spec(D_MODEL), _vec_spec(D_MODEL), _row_spec(D_MODEL), _vec_spec(D_MODEL)],
        [_sds((t, D_MODEL), F32), _sds((1, D_MODEL), F32), _sds((t, D_MODEL), BF16),
         _sds((1, D_MODEL), F32)], [], _cparams("arbitrary"),
        (dh, xin, dxo, g_pre, m, g_post), comm)
    return outs if comm is None else (*outs, moved)


def _norm_pre_bwd(name, dh, xin, dxo, g, comm=None):
    t = dh.shape[0]

    def body(dh_ref, x_ref, d_ref, g_ref, dx_ref, dg_ref):
        xv = x_ref[...]
        dhv = dh_ref[...]
        r = _rms(xv)
        n = xv * r
        dn = dhv * g_ref[...]
        dx_ref[...] = d_ref[...] + r * (dn - n * jnp.mean(dn * n, axis=-1, keepdims=True))
        part = jnp.sum(dhv * n, axis=0, keepdims=True)

        @pl.when(pl.program_id(0) == 0)
        def _():
            dg_ref[...] = part

        @pl.when(pl.program_id(0) > 0)
        def _():
            dg_ref[...] += part

    out, moved = _pcall(
        body, name, (t // TOK,),
        [_row_spec(D_MODEL), _row_spec(D_MODEL), _row_spec(D_MODEL), _vec_spec(D_MODEL)],
        [_row_spec(D_MODEL), _vec_spec(D_MODEL)],
        [_sds((t, D_MODEL), F32), _sds((1, D_MODEL), F32)], [], _cparams("arbitrary"),
        (dh, xin, dxo, g), comm)
    return out if comm is None else (*out, moved)


def _gate_fwd(name, proj, b_gate, ya, yb):
    t = proj.shape[0]

    def body(ga_ref, gb_ref, b_ref, ya_ref, yb_ref, z_ref):
        def strip(k, c):
            rows = _rows(k)
            sa = jax.nn.sigmoid(ga_ref[rows, :].astype(F32) + b_ref[:, :D_MODEL])
            sb = jax.nn.sigmoid(gb_ref[rows, :].astype(F32) + b_ref[:, D_MODEL:])
            z_ref[rows, :] = (sa * ya_ref[rows, :].astype(F32)
                              + sb * yb_ref[rows, :].astype(F32)).astype(BF16)
            return c

        _strips(strip, 0)

    return pl.pallas_call(
        body, name=name, grid=(t // TOK,),
        in_specs=[_row_spec(D_MODEL, 2), _row_spec(D_MODEL, 3), _vec_spec(2 * D_MODEL),
                  _row_spec(D_MODEL), _row_spec(D_MODEL)],
        out_specs=_row_spec(D_MODEL), out_shape=_sds((t, D_MODEL), BF16),
        compiler_params=_cparams("parallel"))(proj, proj, b_gate, ya, yb)


def _gate_bwd(name, dz, proj, b_gate, ya, yb):
    t = proj.shape[0]

    def body(dz_ref, ga_ref, gb_ref, b_ref, ya_ref, yb_ref, dya_ref, dyb_ref, dg_ref, db_ref):
        def strip(k, carry):
            rows = _rows(k)
            dzv = dz_ref[rows, :].astype(F32)
            sa = jax.nn.sigmoid(ga_ref[rows, :].astype(F32) + b_ref[:, :D_MODEL])
            sb = jax.nn.sigmoid(gb_ref[rows, :].astype(F32) + b_ref[:, D_MODEL:])
            dya_ref[rows, :] = (dzv * sa).astype(BF16)
            dyb_ref[rows, :] = (dzv * sb).astype(BF16)
            dga = dzv * ya_ref[rows, :].astype(F32) * sa * (1.0 - sa)
            dgb = dzv * yb_ref[rows, :].astype(F32) * sb * (1.0 - sb)
            dg_ref[rows, :D_MODEL] = dga.astype(BF16)
            dg_ref[rows, D_MODEL:] = dgb.astype(BF16)
            return carry[0] + _fold_rows(dga), carry[1] + _fold_rows(dgb)

        zero = jnp.zeros((8, D_MODEL), F32)
        pa, pb = _strips(strip, (zero, zero))
        _accumulate(db_ref.at[:, :D_MODEL], pa)
        _accumulate(db_ref.at[:, D_MODEL:], pb)

    return pl.pallas_call(
        body, name=name, grid=(t // TOK,),
        in_specs=[_row_spec(D_MODEL), _row_spec(D_MODEL, 2), _row_spec(D_MODEL, 3),
                  _vec_spec(2 * D_MODEL), _row_spec(D_MODEL), _row_spec(D_MODEL)],
        out_specs=[_row_spec(D_MODEL), _row_spec(D_MODEL), _row_spec(2 * D_MODEL),
                   _vec_spec(2 * D_MODEL)],
        out_shape=[_sds((t, D_MODEL), BF16), _sds((t, D_MODEL), BF16),
                   _sds((t, 2 * D_MODEL), BF16), _sds((1, 2 * D_MODEL), F32)],
        compiler_params=_cparams("arbitrary"))(dz, proj, proj, b_gate, ya, yb)


def _head_masks():
    lane = lax.broadcasted_iota(jnp.int32, (1, 2 * HEAD_DIM), 1)
    return lane < HEAD_DIM


BAND_ROWS = 2 * ATT_BLK + CHUNK


def _fill_band(band, prev_ref, cur_ref):
    band[0:ATT_BLK, :] = prev_ref[...]
    band[ATT_BLK:2 * ATT_BLK, :] = cur_ref[...]
    band[2 * ATT_BLK:, :] = jnp.zeros((CHUNK, ATTN_W), BF16)


def _pair_rows(x2, low):
    zero = jnp.zeros_like(x2)
    return jnp.concatenate([jnp.where(low, x2, zero), jnp.where(low, zero, x2)], axis=0)


def _pair_diag(o2, low):
    return jnp.where(low, o2[0:CHUNK, :], o2[CHUNK:, :])


N_PAIRS = HEADS // 2
SM_STRIP = 32
N_STRIPS = BAND_PAD // SM_STRIP
NEG = -1e30


def _fold8(x, op):
    return op(op(x[0:8], x[8:16]), op(x[16:24], x[24:32]))


def _strip(k):
    return pl.ds(pl.multiple_of(k * SM_STRIP, SM_STRIP), SM_STRIP)


def _band_probs(k2, qcat, bias_t, first_key):
    kpos = lax.broadcasted_iota(jnp.int32, (BAND_PAD, 1), 0)
    st = lax.dot_general(k2, qcat, (NT, ((), ())), preferred_element_type=F32)
    st = jnp.where(kpos + first_key >= 0, st + bias_t, NEG)
    e = jnp.exp(st - jnp.max(st, axis=0, keepdims=True))
    return e * (1.0 / jnp.sum(e, axis=0, keepdims=True))


def _attn_specs(nblk):
    cur = lambda col: pl.BlockSpec((ATT_BLK, ATTN_W), lambda s: (jnp.minimum(s, nblk - 1), col))
    prev = lambda col: pl.BlockSpec(
        (ATT_BLK, ATTN_W), lambda s: (jnp.maximum(jnp.minimum(s, nblk - 1) - 1, 0), col))
    return cur, prev


def _attn_fwd(name, proj, bias, comm=None):
    t = proj.shape[0]
    nblk = t // ATT_BLK
    cur, prev = _attn_specs(nblk)

    def body(q_ref, kp_ref, kc_ref, vp_ref, vc_ref, b_ref, o_ref, p_ref, kband, vband):
        s = pl.program_id(0)
        _fill_band(kband, kp_ref, kc_ref)
        _fill_band(vband, vp_ref, vc_ref)
        low = _head_masks()

        def chunk(ci, carry):
            r0 = pl.multiple_of(ci * CHUNK, CHUNK)
            for hp in range(N_PAIRS):
                cols = slice(hp * 128, (hp + 1) * 128)
                qcat = _pair_rows(q_ref[pl.ds(r0, CHUNK), cols] * ATTN_SCALE, low)
                p = _band_probs(kband[pl.ds(r0, BAND_PAD), cols], qcat, b_ref[hp],
                                (s * 8 - 8 + ci) * CHUNK).astype(BF16)
                p_ref[ci, hp] = p
                o2 = lax.dot_general(p, vband[pl.ds(r0, BAND_PAD), cols],
                                     (TN, ((), ())), preferred_element_type=F32)
                o_ref[pl.ds(r0, CHUNK), cols] = _pair_diag(o2, low).astype(BF16)
            return carry

        lax.fori_loop(0, 8, chunk, 0)

    outs, moved = _pcall(
        body, name, (nblk,),
        [cur(0), prev(1), cur(1), prev(2), cur(2),
         pl.BlockSpec((N_PAIRS, BAND_PAD, 128), lambda s: (0, 0, 0))],
        [pl.BlockSpec((ATT_BLK, ATTN_W), lambda s: (s, 0)),
         pl.BlockSpec((8, N_PAIRS, BAND_PAD, 128), lambda s: (s, 0, 0, 0))],
        [_sds((t, ATTN_W), BF16), _sds((t // CHUNK, N_PAIRS, BAND_PAD, 128), BF16)],
        [pltpu.VMEM((BAND_ROWS, ATTN_W), BF16), pltpu.VMEM((BAND_ROWS, ATTN_W), BF16)],
        _cparams("arbitrary"), (proj, proj, proj, proj, proj, bias), comm)
    return outs if comm is None else (*outs, moved)


def _attn_bwd(name, proj, datt, probs, comm=None):
    t = proj.shape[0]
    nblk = t // ATT_BLK
    cur, prev = _attn_specs(nblk)
    late = pl.BlockSpec((ATT_BLK, 3 * ATTN_W), lambda s: (jnp.maximum(s - 1, 0), 0))

    def body(q_ref, kp_ref, kc_ref, vp_ref, vc_ref, do_ref, p_ref,
             dqkv_ref, db_ref, kband, vband, dkacc, dvacc,
             dp_ref, dsb_ref, qc_ref, dc_ref, dq_ref, dq_held):
        s = pl.program_id(0)

        @pl.when(s == 0)
        def _():
            dkacc[...] = jnp.zeros_like(dkacc)
            dvacc[...] = jnp.zeros_like(dvacc)
            db_ref[...] = jnp.zeros_like(db_ref)
            dq_ref[...] = jnp.zeros_like(dq_ref)

        @pl.when(s < nblk)
        def _():
            _fill_band(kband, kp_ref, kc_ref)
            _fill_band(vband, vp_ref, vc_ref)
            low = _head_masks()

            def chunk(ci, carry):
                r0 = pl.multiple_of(ci * CHUNK, CHUNK)
                for hp in range(N_PAIRS):
                    cols = slice(hp * 128, (hp + 1) * 128)
                    qc_ref[hp] = _pair_rows(q_ref[pl.ds(r0, CHUNK), cols] * ATTN_SCALE, low)
                    dc_ref[hp] = _pair_rows(do_ref[pl.ds(r0, CHUNK), cols], low)
                    dp_ref[hp] = lax.dot_general(vband[pl.ds(r0, BAND_PAD), cols], dc_ref[hp],
                                                 (NT, ((), ())), preferred_element_type=F32)

                def sums(k, acc):
                    rows = _strip(k)
                    return tuple(acc[hp] + _fold8(p_ref[ci, hp, rows, :].astype(F32)
                                                  * dp_ref[hp, rows, :], jnp.add)
                                 for hp in range(N_PAIRS))

                acc = lax.fori_loop(0, N_STRIPS, sums, (jnp.zeros((8, 128), F32),) * N_PAIRS)
                delta = [jnp.sum(a, axis=0, keepdims=True) for a in acc]

                def grads(k, c):
                    rows = _strip(k)
                    for hp in range(N_PAIRS):
                        ds = (p_ref[ci, hp, rows, :].astype(F32)
                              * (dp_ref[hp, rows, :] - delta[hp]))
                        db_ref[hp, rows, :] += ds
                        dsb_ref[hp, rows, :] = ds.astype(BF16)
                    return c

                lax.fori_loop(0, N_STRIPS, grads, 0)
                for hp in range(N_PAIRS):
                    cols = slice(hp * 128, (hp + 1) * 128)
                    dq2 = lax.dot_general(dsb_ref[hp], kband[pl.ds(r0, BAND_PAD), cols],
                                          (TN, ((), ())), preferred_element_type=F32)
                    dq_ref[pl.ds(r0, CHUNK), cols] = (_pair_diag(dq2, low) * ATTN_SCALE).astype(BF16)
                    dkacc[pl.ds(r0, BAND_PAD), cols] += jnp.dot(dsb_ref[hp], qc_ref[hp],
                                                               preferred_element_type=F32)
                    dvacc[pl.ds(r0, BAND_PAD), cols] += jnp.dot(p_ref[ci, hp], dc_ref[hp],
                                                               preferred_element_type=F32)
                return carry

            dq_held[...] = dq_ref[...]
            lax.fori_loop(0, 8, chunk, 0)

        @pl.when(s == nblk)
        def _():
            dq_held[...] = dq_ref[...]

        dqkv_ref[:, 0:ATTN_W] = dq_held[...]
        dqkv_ref[:, ATTN_W:2 * ATTN_W] = dkacc[0:ATT_BLK, :].astype(BF16)
        dqkv_ref[:, 2 * ATTN_W:] = dvacc[0:ATT_BLK, :].astype(BF16)
        dkacc[0:ATT_BLK, :] = dkacc[ATT_BLK:2 * ATT_BLK, :]
        dvacc[0:ATT_BLK, :] = dvacc[ATT_BLK:2 * ATT_BLK, :]
        dkacc[ATT_BLK:, :] = jnp.zeros((ATT_BLK + CHUNK, ATTN_W), F32)
        dvacc[ATT_BLK:, :] = jnp.zeros((ATT_BLK + CHUNK, ATTN_W), F32)

    outs, moved = _pcall(
        body, name, (nblk + 1,),
        [cur(0), prev(1), cur(1), prev(2), cur(2),
         pl.BlockSpec((ATT_BLK, ATTN_W), lambda s: (jnp.minimum(s, nblk - 1), 0)),
         pl.BlockSpec((8, N_PAIRS, BAND_PAD, 128), lambda s: (jnp.minimum(s, nblk - 1), 0, 0, 0))],
        [late, pl.BlockSpec((HEADS // 2, BAND_PAD, 128), lambda s: (0, 0, 0))],
        [_sds((t, 3 * ATTN_W), BF16), _sds((HEADS // 2, BAND_PAD, 128), F32)],
        [pltpu.VMEM((BAND_ROWS, ATTN_W), BF16), pltpu.VMEM((BAND_ROWS, ATTN_W), BF16),
         pltpu.VMEM((BAND_ROWS, ATTN_W), F32), pltpu.VMEM((BAND_ROWS, ATTN_W), F32),
         pltpu.VMEM((N_PAIRS, BAND_PAD, 128), F32), pltpu.VMEM((N_PAIRS, BAND_PAD, 128), BF16),
         pltpu.VMEM((N_PAIRS, 2 * CHUNK, 128), BF16), pltpu.VMEM((N_PAIRS, 2 * CHUNK, 128), BF16),
         pltpu.VMEM((ATT_BLK, ATTN_W), BF16), pltpu.VMEM((ATT_BLK, ATTN_W), BF16)],
        _cparams("arbitrary"), (proj, proj, proj, proj, proj, datt, probs), comm)
    return outs if comm is None else (*outs, moved)


def _diag_onehot(rel_rows):
    d0 = lax.broadcasted_iota(jnp.int32, (BIAS_LANES, BIAS_LANES), 0)
    d1 = lax.broadcasted_iota(jnp.int32, (BIAS_LANES, BIAS_LANES), 1)
    m, n = (d0, d1) if rel_rows else (d1, d0)
    hit = (m == jnp.minimum(BAND - 1 + MAX_REL - n, 2 * MAX_REL)) & (n < BAND + CHUNK - 1)
    return jnp.where(hit, 1.0, 0.0).astype(F32)


def _bias_table(name, rel_bias_l):
    rel_pad = jnp.pad(rel_bias_l, ((0, 0), (0, BIAS_LANES - N_REL)))

    def body(r_ref, o_ref):
        diag = jnp.dot(r_ref[...], _diag_onehot(True), preferred_element_type=F32,
                       precision=lax.Precision.HIGHEST)
        rowid = lax.broadcasted_iota(jnp.int32, (8, BIAS_LANES), 0)
        lane = lax.broadcasted_iota(jnp.int32, (8, BIAS_LANES), 1)
        for h in range(HEADS):
            d8 = jnp.broadcast_to(diag[h:h + 1, :], (8, BIAS_LANES))
            slab0 = pltpu.roll(d8, BIAS_LANES - CHUNK + 1, axis=1)
            for b in range(1, 8):
                slab0 = jnp.where(rowid == b, pltpu.roll(d8, BIAS_LANES - CHUNK + 1 + b, axis=1),
                                  slab0)
            for a in range(8):
                slab = slab0 if a == 0 else pltpu.roll(slab0, 8 * a, axis=1)
                o_ref[h * CHUNK + 8 * a:h * CHUNK + 8 * a + 8, :] = jnp.where(lane < BAND, slab, NEG)

    tab = pl.pallas_call(
        body, name=name,
        in_specs=[pl.BlockSpec(memory_space=pltpu.VMEM)],
        out_specs=pl.BlockSpec(memory_space=pltpu.VMEM),
        out_shape=_sds((HEADS * CHUNK, BIAS_LANES), F32),
    )(rel_pad)
    tab = tab.reshape(HEADS // 2, 2, CHUNK, BIAS_LANES)
    return jnp.transpose(tab, (0, 3, 1, 2)).reshape(HEADS // 2, BIAS_LANES, 2 * CHUNK)


def _bias_fold(name, dbias_t):
    rows = HEADS * CHUNK
    dbias = jnp.transpose(dbias_t.reshape(HEADS // 2, BIAS_LANES, 2, CHUNK), (0, 2, 3, 1))

    def body(d_ref, o_ref):
        rowid = lax.broadcasted_iota(jnp.int32, (8, BIAS_LANES), 0)
        diags = []
        for h in range(HEADS):
            acc = d_ref[h * CHUNK + 56:h * CHUNK + 64, :]
            for a in range(7):
                slab = d_ref[h * CHUNK + 8 * a:h * CHUNK + 8 * a + 8, :]
                acc = acc + pltpu.roll(slab, 56 - 8 * a, axis=1)
            tot = jnp.where(rowid == 7, acc, 0.0)
            for b in range(7):
                tot = tot + jnp.where(rowid == b, pltpu.roll(acc, 7 - b, axis=1), 0.0)
            diags.append(jnp.sum(tot, axis=0, keepdims=True))
        diag = jnp.concatenate(diags, axis=0)
        o_ref[...] = jnp.dot(diag, _diag_onehot(False), preferred_element_type=F32,
                             precision=lax.Precision.HIGHEST)

    return pl.pallas_call(
        body, name=name,
        in_specs=[pl.BlockSpec(memory_space=pltpu.VMEM)],
        out_specs=pl.BlockSpec(memory_space=pltpu.VMEM),
        out_shape=_sds((HEADS, BIAS_LANES), F32),
    )(dbias.reshape(rows, BIAS_LANES))


def _inv_counts(i):
    trow = lax.broadcasted_iota(jnp.int32, (TOK + HALO, 1), 0) + i * TOK
    return [1.0 / jnp.minimum(trow + 1, w).astype(F32) for w in POOL_WINDOWS]


def _pool_fwd(name, proj, wg, scale, comm=None):
    t = proj.shape[0]
    hb = TOK // HALO

    def body(u_ref, up_ref, wg_ref, sc_ref, pooled_ref, mixed_ref, b0, b1, b2, b3):
        i = pl.program_id(0)
        halo = up_ref[...].astype(F32)
        b0[0:HALO, :] = jnp.where(i == 0, jnp.zeros_like(halo), halo)
        b0[HALO:, :] = u_ref[...].astype(F32)
        n = TOK + HALO
        b1[8:n, :] = b0[8:n, :] + b0[7:n - 1, :]
        b2[16:n, 128:] = b1[16:n, 128:] + b1[14:n - 2, 128:]
        b3[24:n, 256:] = b2[24:n, 256:] + b2[20:n - 4, 256:]
        wins = [b1[HALO:n, 0:128], b2[HALO:n, 128:256], b3[HALO:n, 256:384],
                b3[HALO:n, 384:512] + b3[HALO - 8:n - 8, 384:512]]
        inv = _inv_counts(i)
        for g in range(4):
            cols = slice(g * POOL_GD, (g + 1) * POOL_GD)
            pooled = (wins[g] * inv[g][0:TOK] - b0[HALO:n, cols]).astype(BF16)
            pooled_ref[:, cols] = pooled
            pre = jnp.dot(pooled, wg_ref[g], preferred_element_type=F32)
            mixed_ref[:, cols] = (pre * sc_ref[:, cols]).astype(BF16)

    buf = pltpu.VMEM((TOK + HALO, POOL_W), F32)
    outs, moved = _pcall(
        body, name, (t // TOK,),
        [_row_spec(POOL_W, 3),
         pl.BlockSpec((HALO, POOL_W), lambda i: (jnp.maximum(i * hb - 1, 0), 3)),
         pl.BlockSpec((4, POOL_GD, POOL_GD), lambda i: (0, 0, 0)), _vec_spec(POOL_W)],
        [_row_spec(POOL_W), _row_spec(POOL_W)],
        [_sds((t, POOL_W), BF16), _sds((t, POOL_W), BF16)], [buf, buf, buf, buf],
        _cparams("arbitrary"), (proj, proj, wg, scale), comm)
    return outs if comm is None else (*outs, moved)


def _pool_bwd(name, dmixed, pooled, wg, scale, comm=None):
    t = dmixed.shape[0]
    nt = t // TOK
    hb = TOK // HALO

    def body(dm_ref, dmn_ref, p_ref, wg_ref, sc_ref, du_ref, dwg_ref, dsc_ref, c0, c1, c2, c3):
        i = pl.program_id(0)

        @pl.when(i == 0)
        def _():
            dwg_ref[...] = jnp.zeros_like(dwg_ref)
            dsc_ref[...] = jnp.zeros_like(dsc_ref)

        n = TOK + HALO
        inv = _inv_counts(i)
        dmv = dm_ref[...].astype(F32)
        dmn = dmn_ref[...].astype(F32)
        dmn = jnp.where(i == nt - 1, jnp.zeros_like(dmn), dmn)
        for g in range(4):
            cols = slice(g * POOL_GD, (g + 1) * POOL_GD)
            scg = sc_ref[:, cols]
            pg = p_ref[:, cols]
            dpre = (dmv[:, cols] * scg).astype(BF16)
            dpre_n = (dmn[:, cols] * scg).astype(BF16)
            pre = jnp.dot(pg, wg_ref[g], preferred_element_type=F32)
            dsc_ref[:, cols] += jnp.sum(dmv[:, cols] * pre, axis=0, keepdims=True)
            dwg_ref[g] += lax.dot_general(pg, dpre, (TN, ((), ())), preferred_element_type=F32)
            dpool = lax.dot_general(dpre, wg_ref[g], (NT, ((), ())), preferred_element_type=F32)
            dpool_n = lax.dot_general(dpre_n, wg_ref[g], (NT, ((), ())),
                                      preferred_element_type=F32)
            c0[0:TOK, cols] = dpool
            c0[TOK:n, cols] = dpool_n
            c1[0:TOK, cols] = dpool * inv[g][0:TOK]
            c1[TOK:n, cols] = dpool_n * inv[g][TOK:n]
        c2[0:n - 8, :] = c1[0:n - 8, :] + c1[1:n - 7, :]
        c3[0:n - 16, 128:] = c2[0:n - 16, 128:] + c2[2:n - 14, 128:]
        c1[0:n - 24, 256:] = c3[0:n - 24, 256:] + c3[4:n - 20, 256:]
        wins = [c2[0:TOK, 0:128], c3[0:TOK, 128:256], c1[0:TOK, 256:384],
                c1[0:TOK, 384:512] + c1[8:TOK + 8, 384:512]]
        for g in range(4):
            cols = slice(g * POOL_GD, (g + 1) * POOL_GD)
            du_ref[:, cols] = (wins[g] - c0[0:TOK, cols]).astype(BF16)

    buf = pltpu.VMEM((TOK + HALO, POOL_W), F32)
    outs, moved = _pcall(
        body, name, (nt,),
        [_row_spec(POOL_W),
         pl.BlockSpec((HALO, POOL_W), lambda i: (jnp.minimum((i + 1) * hb, nt * hb - 1), 0)),
         _row_spec(POOL_W), pl.BlockSpec((4, POOL_GD, POOL_GD), lambda i: (0, 0, 0)),
         _vec_spec(POOL_W)],
        [_row_spec(POOL_W), pl.BlockSpec((4, POOL_GD, POOL_GD), lambda i: (0, 0, 0)),
         _vec_spec(POOL_W)],
        [_sds((t, POOL_W), BF16), _sds((4, POOL_GD, POOL_GD), F32), _sds((1, POOL_W), F32)],
        [buf, buf, buf, buf], _cparams("arbitrary"), (dmixed, dmixed, pooled, wg, scale), comm)
    return outs if comm is None else (*outs, moved)


GELU_C = math.sqrt(2.0 / math.pi)


GELU_K = 0.044715


def _gelu_parts(x):
    x2 = x * x
    s = 0.5 + 0.5 * jnp.tanh(x * (GELU_C + (GELU_C * GELU_K) * x2))
    return x * s, s, x2


def _gelu(x):
    return _gelu_parts(x)[0]


def _gelu_and_grad(x):
    g, s, x2 = _gelu_parts(x)
    return g, s + g * (1.0 - s) * ((2 * GELU_C) + (6 * GELU_C * GELU_K) * x2)


def _taps(buf, r, rows):
    a = buf[pl.ds(r, rows + 8), :]
    return a[8:], pltpu.roll(a, 1, axis=0)[8:], pltpu.roll(a, 2, axis=0)[8:]


def _conv(taps, w_ref, b_ref):
    return b_ref[...] + w_ref[2:3, :] * taps[0] + w_ref[1:2, :] * taps[1] + w_ref[0:1, :] * taps[2]


def _stage(dst, prev_ref, cur_ref, next_ref, first, last):
    rows = cur_ref.shape[0]
    h = prev_ref[...].astype(F32)
    dst[0:8, :] = jnp.where(first, jnp.zeros_like(h), h)
    dst[8:8 + rows, :] = cur_ref[...].astype(F32)
    if next_ref is not None:
        h = next_ref[...].astype(F32)
        dst[8 + rows:, :] = jnp.where(last, jnp.zeros_like(h), h)


FWD_STRIP = 32
BWD_STRIP = 16


def _ffn_gate_fwd(name, hu, conv_w, conv_b, comm=None):
    t = hu.shape[0]
    ncol = D_FF // FF_COL
    hb = FF_TOK // 8

    def tile(off):
        return pl.BlockSpec((FF_TOK, FF_COL), lambda i, j: (i, j + off))

    def halo(off):
        return pl.BlockSpec((8, FF_COL), lambda i, j: (jnp.maximum(i * hb - 1, 0), j + off))

    def wspec(off):
        return pl.BlockSpec((3, FF_COL), lambda i, j: (0, j + off))

    def bspec(off):
        return pl.BlockSpec((1, FF_COL), lambda i, j: (0, j + off))

    def body(v_ref, vp_ref, g_ref, gp_ref, wv_ref, wg_ref, bv_ref, bg_ref, a_ref, hc_ref, vb, gb):
        first = pl.program_id(0) == 0
        _stage(vb, vp_ref, v_ref, None, first, None)
        _stage(gb, gp_ref, g_ref, None, first, None)

        def strip(k, carry):
            for u in range(2):
                r = pl.multiple_of((2 * k + u) * FWD_STRIP, FWD_STRIP)
                val = _conv(_taps(vb, r, FWD_STRIP), wv_ref, bv_ref)
                gate = _conv(_taps(gb, r, FWD_STRIP), wg_ref, bg_ref)
                a_ref[pl.ds(r, FWD_STRIP), :] = (_gelu(gate) * val).astype(BF16)
                hc_ref[0, pl.ds(r, FWD_STRIP), :] = val.astype(BF16)
                hc_ref[1, pl.ds(r, FWD_STRIP), :] = gate.astype(BF16)
            return carry

        lax.fori_loop(0, FF_TOK // (2 * FWD_STRIP), strip, 0)

    buf = pltpu.VMEM((FF_TOK + 8, FF_COL), F32)
    outs, moved = _pcall(
        body, name, (t // FF_TOK, ncol),
        [tile(0), halo(0), tile(ncol), halo(ncol), wspec(0), wspec(ncol), bspec(0), bspec(ncol)],
        [pl.BlockSpec((FF_TOK, FF_COL), lambda i, j: (i, j)),
         pl.BlockSpec((2, FF_TOK, FF_COL), lambda i, j: (0, i, j))],
        [_sds((t, D_FF), BF16), _sds((2, t, D_FF), BF16)], [buf, buf],
        _cparams("arbitrary", "arbitrary"),
        (hu, hu, hu, hu, conv_w, conv_w, conv_b, conv_b), comm)
    return outs if comm is None else (*outs, moved)


def _ffn_gate_bwd(name, da, hu, hc, conv_w, comm=None):
    t = hu.shape[0]
    nt = t // FF_TOK
    ncol = D_FF // FF_COL
    hb = FF_TOK // 8

    def tile(off):
        return pl.BlockSpec((FF_TOK, FF_COL), lambda j, i: (i, j + off))

    def nxt_rows(i):
        return jnp.minimum((i + 1) * hb, nt * hb - 1)

    def wspec(off):
        return pl.BlockSpec((3, FF_COL), lambda j, i: (0, j + off))

    def body(da_ref, dan_ref, v_ref, g_ref, hc_ref, hcn_ref, wv_ref, wg_ref,
             dh_ref, dwv_ref, dwg_ref):
        i = pl.program_id(1)
        first, last = i == 0, i == nt - 1

        @pl.when(first)
        def _():
            dwv_ref[...] = jnp.zeros_like(dwv_ref)
            dwg_ref[...] = jnp.zeros_like(dwg_ref)

        def grads(dav, val, gate):
            g, dg = _gelu_and_grad(gate.astype(F32))
            dav = dav.astype(F32)
            return dav * g, dav * val.astype(F32) * dg

        def fold(x):
            return x[0:8] + x[8:16]

        def strip(j, carry):
            for u in range(2):
                carry = one_strip(2 * j + u, carry)
            return carry

        def one_strip(k, carry):
            r = pl.multiple_of(FF_TOK - BWD_STRIP - k * BWD_STRIP, BWD_STRIP)
            rows = pl.ds(r, BWD_STRIP)
            dval, dgate = grads(da_ref[rows, :], hc_ref[0, rows, :], hc_ref[1, rows, :])
            new = (dval[0:8], dgate[0:8])
            for half, (d, below, h_ref, w_ref, dw_ref) in enumerate((
                    (dval, carry[0], v_ref, wv_ref, dwv_ref),
                    (dgate, carry[1], g_ref, wg_ref, dwg_ref))):
                e = jnp.concatenate([d, below], axis=0)
                e1 = pltpu.roll(e, BWD_STRIP + 7, axis=0)[0:BWD_STRIP]
                e2 = pltpu.roll(e, BWD_STRIP + 6, axis=0)[0:BWD_STRIP]
                dh = w_ref[2:3, :] * d + w_ref[1:2, :] * e1 + w_ref[0:1, :] * e2
                dh_ref[half, rows, :] = dh.astype(BF16)
                huv = h_ref[rows, :].astype(F32)
                dw_ref[0:8, :] += fold(e2 * huv)
                dw_ref[8:16, :] += fold(e1 * huv)
                dw_ref[16:24, :] += fold(d * huv)
                dw_ref[24:32, :] += fold(d)
            return new

        dan = dan_ref[...]
        dan = jnp.where(last, jnp.zeros_like(dan), dan)
        lax.fori_loop(0, FF_TOK // (2 * BWD_STRIP), strip, grads(dan, hcn_ref[0], hcn_ref[1]))

        @pl.when(last)
        def _():
            for dw_ref in (dwv_ref, dwg_ref):
                for q in range(4):
                    dw_ref[8 * q:8 * q + 1, :] = jnp.sum(dw_ref[8 * q:8 * q + 8, :], axis=0,
                                                         keepdims=True)

    acc = pl.BlockSpec((32, FF_COL), lambda j, i: (0, j))
    (dhu, dwv, dwg), moved = _pcall(
        body, name, (ncol, nt),
        [tile(0), pl.BlockSpec((8, FF_COL), lambda j, i: (nxt_rows(i), j)),
         tile(0), tile(ncol),
         pl.BlockSpec((2, FF_TOK, FF_COL), lambda j, i: (0, i, j)),
         pl.BlockSpec((2, 8, FF_COL), lambda j, i: (0, nxt_rows(i), j)),
         wspec(0), wspec(ncol)],
        [pl.BlockSpec((2, FF_TOK, FF_COL), lambda j, i: (0, i, j)), acc, acc],
        [_sds((2, t, D_FF), BF16), _sds((32, D_FF), F32), _sds((32, D_FF), F32)],
        [], _cparams("arbitrary", "arbitrary"),
        (da, da, hu, hu, hc, hc, conv_w, conv_w), comm)
    dconv = jnp.concatenate([dwv, dwg], axis=1).reshape(4, 8, 2 * D_FF)[:, 0]
    return (dhu, dconv) if comm is None else (dhu, dconv, moved)


def _mesh_pos():
    x, y, c = lax.axis_index("x"), lax.axis_index("y"), lax.axis_index("c")
    return x, y, c, [(1 - x, y), (x, 1 - y), (1 - x, 1 - y)]


def _remote(src, dst, send_sems, recv_sems, i, dev):
    return pltpu.make_async_remote_copy(src_ref=src, dst_ref=dst, send_sem=send_sems.at[i],
                                        recv_sem=recv_sems.at[i], device_id=dev,
                                        device_id_type=MESH)


def _mine(c, rows):
    return pl.ds(pl.multiple_of(c * (rows // 2), 16), rows // 2)


def _gather_send(shards, conv_shard, gathered, l):
    nbig = len(shards)
    with_conv = conv_shard is not None
    if gathered is None:
        ins = list(shards) + ([conv_shard] if with_conv else [])
        outs = [_sds((DEPTH, N_CHIPS) + s.shape[1:], s.dtype) for s in ins]
        alias = {}
    else:
        ins = list(shards) + list(gathered)
        outs = [_sds(g.shape, g.dtype) for g in gathered]
        alias = {nbig + k: k for k in range(nbig)}

    def copies(cin, cout, ssem, rsem):
        x, y, c, chips = _mesh_pos()
        me = 2 * x + y
        out = []
        for k in range(nbig):
            rows = shards[k].shape[1]
            for j, (cx, cy) in enumerate(chips):
                out.append(_remote(cin[k].at[l, _mine(c, rows)], cout[k].at[l, me, _mine(c, rows)],
                                   ssem, rsem, 4 * k + j, (cx, cy, c)))
            out.append(_remote(cin[k].at[l], cout[k].at[l, me], ssem, rsem, 4 * k + 3,
                               (x, y, 1 - c)))
        if with_conv:
            base = 4 * nbig
            for j, (cx, cy) in enumerate(chips):
                out.append(_remote(cin[nbig].at[c], cout[nbig].at[c, me], ssem, rsem, base + j,
                                   (cx, cy, c)))
            for ll in range(DEPTH):
                out.append(_remote(cin[nbig].at[ll], cout[nbig].at[ll, me], ssem, rsem,
                                   base + 3 + ll, (x, y, 1 - c)))
        return out

    return _Comm(ins, outs, copies, 4 * nbig + 5, alias)


def _gather_forward(gathered, nbig, rows, l):
    with_conv = len(gathered) > nbig
    alias = {k: k for k in range(len(gathered))}

    def copies(cin, cout, ssem, rsem):
        x, y, c, chips = _mesh_pos()
        out = []
        for k in range(nbig):
            for j, (cx, cy) in enumerate(chips):
                blk = cout[k].at[l, 2 * cx + cy, _mine(c, rows[k])]
                out.append(_remote(blk, blk, ssem, rsem, 3 * k + j, (x, y, 1 - c)))
        if with_conv:
            for j, (cx, cy) in enumerate(chips):
                blk = cout[nbig].at[c, 2 * cx + cy]
                out.append(_remote(blk, blk, ssem, rsem, 3 * nbig + j, (x, y, 1 - c)))
        return out

    return _Comm(gathered, [_sds(g.shape, g.dtype) for g in gathered], copies, 3 * nbig + 3, alias)


def _reduce_swap(grads, l):
    def copies(cin, cout, ssem, rsem):
        x, y, c, _ = _mesh_pos()
        return [_remote(cin[k].at[l, :, _mine(1 - c, g.shape[2])], cout[k], ssem, rsem, k,
                        (x, y, 1 - c)) for k, g in enumerate(grads)]

    outs = [_sds((N_CHIPS, g.shape[2] // 2, g.shape[3]), g.dtype) for g in grads]
    return _Comm(grads, outs, copies, len(grads))


def _reduce_scatter(sums):
    def copies(cin, cout, ssem, rsem):
        x, y, c, chips = _mesh_pos()
        return [_remote(cin[k].at[2 * cx + cy], cout[k].at[j], ssem, rsem, 3 * k + j, (cx, cy, c))
                for k in range(len(sums)) for j, (cx, cy) in enumerate(chips)]

    outs = [_sds((3,) + s.shape[1:], s.dtype) for s in sums]
    return _Comm(sums, outs, copies, 3 * len(sums))


def _reduce_share(reds, l):
    def copies(cin, cout, ssem, rsem):
        x, y, c, _ = _mesh_pos()
        out = []
        for k, r in enumerate(reds):
            half = cout[k].at[l, _mine(c, r.shape[1])]
            out.append(_remote(half, half, ssem, rsem, k, (x, y, 1 - c)))
        return out

    return _Comm(reds, [_sds(r.shape, r.dtype) for r in reds], copies, len(reds),
                 {k: k for k in range(len(reds))})


def _allreduce_small(per_layer):
    kinds = len(per_layer[0])
    shapes = [a.shape[1:] if a.shape[0] == 1 else a.shape for a in per_layer[0]]

    def body(*refs):
        ins = refs[:DEPTH * kinds]
        outs = refs[DEPTH * kinds:(DEPTH + 1) * kinds]
        gbufs = refs[(DEPTH + 1) * kinds:(DEPTH + 2) * kinds]
        send_sems, recv_sems = refs[-2], refs[-1]
        x, y, c, chips = _mesh_pos()
        sibling = (x, y, 1 - c)

        def copy(k, i, block, to):
            px, py, pc = block
            slot = gbufs[k].at[4 * px + 2 * py + pc]
            return _remote(slot, slot, send_sems, recv_sems, 7 * k + i, to)

        me = (x, y, c)
        first, passed = [], []
        for k in range(kinds):
            for l in range(DEPTH):
                a = ins[l * kinds + k]
                if per_layer[l][k].shape[0] == 1:
                    gbufs[k][4 * x + 2 * y + c, l:l + 1] = a[...]
                else:
                    gbufs[k][4 * x + 2 * y + c, l] = a[...]
            first.append(copy(k, 0, me, sibling))
            first += [copy(k, 1 + j, me, (*chip, c)) for j, chip in enumerate(chips)]
            passed += [copy(k, 4 + j, (*chip, c), sibling) for j, chip in enumerate(chips)]
        for cp in first:
            cp.start()
        for k in range(kinds):
            for j, chip in enumerate(chips):
                copy(k, 1 + j, (*chip, c), me).wait_recv()
                passed[3 * k + j].start()
        for k in range(kinds):
            copy(k, 0, sibling, me).wait_recv()
            for j, chip in enumerate(chips):
                copy(k, 4 + j, (*chip, 1 - c), me).wait_recv()
        for cp in first + passed:
            cp.wait_send()
        for k in range(kinds):
            acc = gbufs[k][0]
            for d in range(1, 8):
                acc = acc + gbufs[k][d]
            outs[k][...] = acc

    vmem = pl.BlockSpec(memory_space=pltpu.VMEM)
    return pl.pallas_call(
        body, name="allreduce_small",
        in_specs=[vmem] * (DEPTH * kinds), out_specs=[vmem] * kinds,
        out_shape=[_sds((DEPTH,) + s, F32) for s in shapes],
        scratch_shapes=[pltpu.VMEM((8, DEPTH) + s, F32) for s in shapes]
        + [pltpu.SemaphoreType.DMA((7 * kinds,)), pltpu.SemaphoreType.DMA((7 * kinds,))],
        compiler_params=pltpu.CompilerParams(vmem_limit_bytes=VMEM_LIMIT_V7X),
    )(*per_layer[0], *per_layer[1])


def _adamw_small(ws, gs, ms, vs):
    n = len(ws)
    c1 = 1.0 - ADAM_B1 ** ADAM_STEP
    c2 = 1.0 - ADAM_B2 ** ADAM_STEP

    def body(*refs):
        for i in range(n):
            w_ref, g_ref, m_ref, v_ref = (refs[j * n + i] for j in range(4))
            d_ref, nm_ref, nv_ref = (refs[(4 + j) * n + i] for j in range(3))
            gv = g_ref[...]
            nm = ADAM_B1 * m_ref[...] + (1.0 - ADAM_B1) * gv
            nv = ADAM_B2 * v_ref[...] + (1.0 - ADAM_B2) * (gv * gv)
            nm_ref[...] = nm
            nv_ref[...] = nv
            d_ref[...] = -ADAM_LR * ((nm / c1) / (jnp.sqrt(nv / c2) + ADAM_EPS)
                                     + ADAM_WD * w_ref[...])

    vmem = pl.BlockSpec(memory_space=pltpu.VMEM)
    outs = pl.pallas_call(
        body, name="adamw_small", in_specs=[vmem] * (4 * n), out_specs=[vmem] * (3 * n),
        out_shape=[_sds(w.shape, F32) for w in ws] * 3,
        compiler_params=pltpu.CompilerParams(vmem_limit_bytes=VMEM_LIMIT_V7X),
    )(*ws, *gs, *ms, *vs)
    return outs[:n], outs[n:2 * n], outs[2 * n:]


def _core_index():
    return jnp.reshape(lax.axis_index("c"), (1,)).astype(jnp.int32)


def _chip_index():
    return jnp.reshape(2 * lax.axis_index("x") + lax.axis_index("y"), (1,)).astype(jnp.int32)


def _chip_sums(name, stacked, sibs, l):
    n = len(stacked)
    dims = [(s.shape[2] // 2, s.shape[3]) for s in stacked]

    def body(c_ref, *refs):
        for k in range(n):
            a_ref, b_ref, o_ref = refs[k], refs[n + k], refs[2 * n + k]
            o_ref[...] = (a_ref[...].astype(F32) + b_ref[...].astype(F32)).astype(BF16)

    return pl.pallas_call(
        body, name=name,
        grid_spec=pltpu.PrefetchScalarGridSpec(
            num_scalar_prefetch=1, grid=(N_CHIPS,),
            in_specs=[pl.BlockSpec((None, None, hr, cd), lambda j, cr: (l, j, cr[0], 0))
                      for hr, cd in dims]
            + [pl.BlockSpec((None, hr, cd), lambda j, cr: (j, 0, 0)) for hr, cd in dims],
            out_specs=[pl.BlockSpec((None, hr, cd), lambda j, cr: (j, 0, 0)) for hr, cd in dims]),
        out_shape=[_sds((N_CHIPS, hr, cd), BF16) for hr, cd in dims],
        compiler_params=_cparams("parallel"))(_core_index(), *stacked, *sibs)


def _final_sums(name, sums, recvs, l, fills):
    n = len(sums)
    dims = [(s.shape[1] // 2, s.shape[2]) for s in sums]
    filled = fills[0] is not None

    def body(m_ref, *refs):
        outs = refs[-n:]
        for k in range(n):
            acc = refs[k][...].astype(F32)
            for j in range(3):
                acc = acc + refs[n + k][j].astype(F32)
            outs[k][...] = acc

    in_specs = ([pl.BlockSpec((None, tr, cd), lambda i, mr: (mr[0], i, 0)) for tr, cd in dims]
                + [pl.BlockSpec((3, tr, cd), lambda i, mr: (0, i, 0)) for tr, cd in dims])
    args = [jnp.concatenate([_chip_index(), _core_index()]), *sums, *recvs]
    aliases = {}
    if filled:
        in_specs += [pl.BlockSpec(memory_space=pl.ANY)] * n
        args += list(fills)
        aliases = {1 + 2 * n + k: k for k in range(n)}
    return pl.pallas_call(
        body, name=name,
        grid_spec=pltpu.PrefetchScalarGridSpec(
            num_scalar_prefetch=1, grid=(2,), in_specs=in_specs,
            out_specs=[pl.BlockSpec((None, tr, cd), lambda i, mr: (l, 2 * mr[1] + i, 0))
                       for tr, cd in dims]),
        out_shape=[_sds((DEPTH, 4 * tr, cd), F32) for tr, cd in dims],
        input_output_aliases=aliases,
        compiler_params=_cparams("parallel"))(*args)


def _adamw(name, w, g, m, v, comm=None):
    nl, r, cdim = w.shape
    tr = r // 4 if r % 32 == 0 else r
    c1 = 1.0 - ADAM_B1 ** ADAM_STEP
    c2 = 1.0 - ADAM_B2 ** ADAM_STEP

    def body(w_ref, g_ref, m_ref, v_ref, d_ref, nm_ref, nv_ref):
        gv = g_ref[...]
        nm = ADAM_B1 * m_ref[...] + (1.0 - ADAM_B1) * gv
        nv = ADAM_B2 * v_ref[...] + (1.0 - ADAM_B2) * (gv * gv)
        nm_ref[...] = nm
        nv_ref[...] = nv
        d_ref[...] = -ADAM_LR * ((nm / c1) / (jnp.sqrt(nv / c2) + ADAM_EPS) + ADAM_WD * w_ref[...])

    spec = pl.BlockSpec((None, tr, cdim), lambda l, i: (l, i, 0))
    out = _sds(w.shape, F32)
    outs, moved = _pcall(body, name, (nl, r // tr), [spec] * 4, [spec] * 3, [out] * 3, [],
                         _cparams("arbitrary", "arbitrary"), (w, g, m, v), comm)
    return outs if comm is None else (*outs, moved)


def kernel(x, norm_mix_pre, w_in, b_gate, rel_bias, w_attn_out, w_pool_group, pool_scale, w_pool_out, w_o, norm_mix_post, norm_ffn_pre, w_up, conv_w, conv_b, w_down, norm_ffn_post, loss_target, m_norm_mix_pre, m_w_in, m_b_gate, m_rel_bias, m_w_attn_out, m_w_pool_group, m_pool_scale, m_w_pool_out, m_w_o, m_norm_mix_post, m_norm_ffn_pre, m_w_up, m_conv_w, m_conv_b, m_w_down, m_norm_ffn_post, v_norm_mix_pre, v_w_in, v_b_gate, v_rel_bias, v_w_attn_out, v_w_pool_group, v_pool_scale, v_w_pool_out, v_w_o, v_norm_mix_post, v_norm_ffn_pre, v_w_up, v_conv_w, v_conv_b, v_w_down, v_norm_ffn_post):
    t = x.shape[1]
    xs = x.reshape(t, D_MODEL)
    target = loss_target.reshape(t, D_MODEL)

    names = ["w_in", "w_attn_out", "w_pool_out", "w_o", "w_up", "w_down"]
    shards = [w.astype(BF16) for w in (w_in, w_attn_out, w_pool_out, w_o, w_up, w_down)]
    rows = [s.shape[1] for s in shards]
    nbig = len(shards)
    h, g = _norm_fwd("l0_norm_mix_pre", x.reshape(t, D_MODEL), norm_mix_pre[0:1],
                     _gather_send(shards[:1], conv_w, None, 0))
    g = _comm_call("gather0_forward", _gather_forward(g, 1, rows[:1], 0))
    cw_full = jnp.transpose(g[1], (0, 2, 1, 3)).reshape(DEPTH, 3, 2 * D_FF)
    g = g[:1]
    wg_bf = w_pool_group.astype(BF16)

    def views(gathered):
        win_g, wao_g, wpo_g, wo_g, wup_g, wdn_g = gathered
        return (win_g, wao_g, wpo_g, wo_g.reshape(DEPTH, D_MODEL, D_MODEL), wup_g,
                wdn_g.reshape(DEPTH, D_FF, D_MODEL))

    saved = []
    xcur = xs
    for l in range(DEPTH):
        tag = f"l{l}_"
        bias = _bias_table(tag + "bias_table", rel_bias[l])
        proj = _mm_nn_blocked(tag + "proj", h, g[0], l, BF16)
        if l == 0:
            att, probs, rest = _attn_fwd(tag + "attn_fwd", proj, bias,
                                         _gather_send(shards[1:], None, None, 0))
            pooled, mixed, rest = _pool_fwd(tag + "pool_fwd", proj, wg_bf[l], pool_scale[l:l + 1],
                                            _gather_forward(rest, nbig - 1, rows[1:], 0))
            g = g + rest
        else:
            att, probs = _attn_fwd(tag + "attn_fwd", proj, bias)
            pooled, mixed = _pool_fwd(tag + "pool_fwd", proj, wg_bf[l], pool_scale[l:l + 1])
        win_g, wao_g, wpo_g, wo_full, wup_g, wdn_full = views(g)
        ya = _narrow_nn(tag + "attn_out", att, wao_g, l)
        yb = _narrow_nn(tag + "pool_out", mixed, wpo_g, l)
        z = _gate_fwd(tag + "gate_fwd", proj, b_gate[l:l + 1], ya, yb)
        mix = _mm_nn(tag + "mix", z, wo_full, l, D_MODEL, F32)
        x1, h2 = _post_pre_fwd(tag + "norm_mix_post", xcur, mix, norm_mix_post[l:l + 1],
                               norm_ffn_pre[l:l + 1])
        if l == 0:
            hu, mixing = _mm_nn_blocked(tag + "ffn_up", h2, wup_g, l, BF16,
                                        _gather_send(shards[:4], None, g[:4], 1))
            a, hc, ffn_g = _ffn_gate_fwd(tag + "ffn_gate_fwd", hu, cw_full[l], conv_b[l:l + 1],
                                         _gather_send(shards[4:], None, g[4:], 1))
            g = mixing + ffn_g
            wdn_full = views(g)[5]
        else:
            hu = _mm_nn_blocked(tag + "ffn_up", h2, wup_g, l, BF16)
            a, hc = _ffn_gate_fwd(tag + "ffn_gate_fwd", hu, cw_full[l], conv_b[l:l + 1])
        f = _mm_nn(tag + "ffn_down", a, wdn_full, l, D_FF, F32)
        saved.append(dict(x=xcur, h=h, proj=proj, att=att, pooled=pooled, mixed=mixed, ya=ya,
                          yb=yb, z=z, mix=mix, x1=x1, h2=h2, hu=hu, hc=hc, a=a, f=f, probs=probs))
        if l == 0:
            xcur, h, g = _post_pre_fwd(tag + "norm_ffn_post", x1, f, norm_ffn_post[l:l + 1],
                                       norm_mix_pre[l + 1:l + 2], _gather_forward(g, nbig, rows, 1))
        elif l < DEPTH - 1:
            xcur, h = _post_pre_fwd(tag + "norm_ffn_post", x1, f, norm_ffn_post[l:l + 1],
                                    norm_mix_pre[l + 1:l + 2])
    win_g, wao_g, wpo_g, wo_full, wup_g, wdn_full = views(g)

    dy, df, d_nfpost, loss_local = _tail("tail", saved[-1]["x1"], saved[-1]["f"],
                                         norm_ffn_post[DEPTH - 1:DEPTH], target)
    loss = lax.psum(loss_local, ("x", "y", "c"))

    dx = dy
    dws = dict.fromkeys(names)
    reds = [None] * nbig
    small_grads = [None] * DEPTH
    ffn = [4, 5]
    outs3 = [1, 2, 3]

    def blocks(ks):
        return [dws[names[k]].reshape(DEPTH, N_CHIPS, rows[k], -1) for k in ks]

    def chip_sums(ks, sib, l):
        return _chip_sums(f"chip_sums{l}_" + names[ks[0]], blocks(ks), sib, l)

    def final_sums(ks, sums, recv, l):
        outs = _final_sums(f"final_sums{l}_" + names[ks[0]], sums, recv, l, [reds[k] for k in ks])
        for k, r in zip(ks, outs):
            reds[k] = r

    for l in reversed(range(DEPTH)):
        tag = f"l{l}_"
        sv = saved[l]
        every = list(range(nbig))
        if l == 0:
            da, sib = _mm_nt(tag + "ffn_down_dx", df, wdn_full, l, D_FF // 2, BF16,
                             _reduce_swap(blocks(every), 1))
            sums = chip_sums(every, sib, 1)
        else:
            da = _mm_nt(tag + "ffn_down_dx", df, wdn_full, l, D_FF // 2, BF16)
        dws["w_down"] = _mm_tn(tag + "ffn_down_dw", sv["a"], df, D_FF // 2, l, dws["w_down"])
        if l == 0:
            dhu, dconv, recv = _ffn_gate_bwd(tag + "ffn_gate_bwd", da, sv["hu"], sv["hc"],
                                             cw_full[l], _reduce_scatter(sums))
            final_sums(every, sums, recv, 1)
            dh2, reds = _mm_nt_blocked(tag + "ffn_up_dx", dhu, wup_g, l, F32,
                                       _reduce_share(reds, 1))
        else:
            dhu, dconv = _ffn_gate_bwd(tag + "ffn_gate_bwd", da, sv["hu"], sv["hc"], cw_full[l])
            dh2 = _mm_nt_blocked(tag + "ffn_up_dx", dhu, wup_g, l, F32)
        dws["w_up"] = _mm_tn_blocked(tag + "ffn_up_dw", sv["h2"], dhu, l, dws["w_up"])
        if l == 0:
            dx1, d_nfpre, dmix, d_nmpost, sib = _pre_post_bwd(
                tag + "norm_ffn_pre_bwd", dh2, sv["x1"], dx, norm_ffn_pre[l:l + 1], sv["mix"],
                norm_mix_post[l:l + 1], _reduce_swap(blocks(ffn), 0))
            sums = chip_sums(ffn, sib, 0)
        else:
            dx1, d_nfpre, dmix, d_nmpost = _pre_post_bwd(
                tag + "norm_ffn_pre_bwd", dh2, sv["x1"], dx, norm_ffn_pre[l:l + 1], sv["mix"],
                norm_mix_post[l:l + 1])
        dz = _mm_nt(tag + "mix_dx", dmix, wo_full, l, D_MODEL, BF16)
        dws["w_o"] = _mm_tn(tag + "mix_dw", sv["z"], dmix, D_MODEL, l, dws["w_o"])
        dya, dyb, dgates, d_bgate = _gate_bwd(tag + "gate_bwd", dz, sv["proj"], b_gate[l:l + 1],
                                              sv["ya"], sv["yb"])
        datt = _narrow_nt(tag + "attn_out_dx", dya, wao_g, l)
        dws["w_attn_out"] = _narrow_tn(tag + "attn_out_dw", sv["att"], dya, l, dws["w_attn_out"])
        dmixed = _narrow_nt(tag + "pool_out_dx", dyb, wpo_g, l)
        dws["w_pool_out"] = _narrow_tn(tag + "pool_out_dw", sv["mixed"], dyb, l, dws["w_pool_out"])
        if l == 0:
            du, d_wg, d_pscale, sib = _pool_bwd(tag + "pool_bwd", dmixed, sv["pooled"], wg_bf[l],
                                                pool_scale[l:l + 1], _reduce_swap(blocks(outs3), 0))
            sums3 = chip_sums(outs3, sib, 0)
            dqkv, dbias, recv = _attn_bwd(
                tag + "attn_bwd", sv["proj"], datt, sv["probs"],
                _both(_reduce_scatter(sums), _reduce_scatter(sums3)))
            final_sums(ffn, sums, recv[:len(ffn)], 0)
            final_sums(outs3, sums3, recv[len(ffn):], 0)
        else:
            du, d_wg, d_pscale = _pool_bwd(tag + "pool_bwd", dmixed, sv["pooled"], wg_bf[l],
                                           pool_scale[l:l + 1])
            dqkv, dbias = _attn_bwd(tag + "attn_bwd", sv["proj"], datt, sv["probs"])
        d_rel = _bias_fold(tag + "bias_fold", dbias)
        if l == 0:
            dh, shared = _proj_dx(tag + "proj_dx", dqkv, du, dgates, win_g, l,
                                  _reduce_share([reds[k] for k in ffn + outs3], 0))
            for k, r in zip(ffn + outs3, shared):
                reds[k] = r
        else:
            dh = _proj_dx(tag + "proj_dx", dqkv, du, dgates, win_g, l)
        dws["w_in"] = _proj_dw(tag + "proj_dw", sv["h"], dqkv, du, dgates, l, dws["w_in"])
        small_grads[l] = [None, d_nmpost, d_nfpre, d_nfpost, d_bgate, d_rel, d_wg, d_pscale, dconv]
        if l > 0:
            dx, small_grads[l][0], df, d_nfpost = _pre_post_bwd(
                tag + "norm_mix_pre_bwd", dh, sv["x"], dx1, norm_mix_pre[l:l + 1],
                saved[l - 1]["f"], norm_ffn_post[l - 1:l])
        else:
            dx, small_grads[l][0], sib = _norm_pre_bwd(
                tag + "norm_mix_pre_bwd", dh, sv["x"], dx1, norm_mix_pre[l:l + 1],
                _reduce_swap(blocks([0]), 0))

    grad_x = dx.reshape(x.shape)

    delta, new_m, new_v = {}, {}, {}
    sums = chip_sums([0], sib, 0)
    delta["w_up"], new_m["w_up"], new_v["w_up"], recv = _adamw(
        "adamw_w_up", w_up, reds[4], m_w_up, v_w_up, _reduce_scatter(sums))
    final_sums([0], sums, recv, 0)
    delta["w_down"], new_m["w_down"], new_v["w_down"], shared = _adamw(
        "adamw_w_down", w_down, reds[5], m_w_down, v_w_down, _reduce_share([reds[0]], 0))
    g_big = shared + reds[1:]

    (g_nmpre, g_nmpost, g_nfpre, g_nfpost, g_bgate, g_rel, g_wg, g_pscale,
     g_conv) = _allreduce_small(small_grads)
    g_rel = g_rel[:, :, :N_REL]
    g_cb = g_conv[:, 3]
    ncw = conv_w.shape[2]
    chip = 2 * lax.axis_index("x") + lax.axis_index("y")
    g_cw = lax.dynamic_slice_in_dim(g_conv[:, 0:3], chip * ncw, ncw, axis=2)

    grads = dict(norm_mix_pre=g_nmpre, w_in=g_big[0], b_gate=g_bgate, rel_bias=g_rel,
                 w_attn_out=g_big[1], w_pool_group=g_wg, pool_scale=g_pscale, w_pool_out=g_big[2],
                 w_o=g_big[3], norm_mix_post=g_nmpost, norm_ffn_pre=g_nfpre, w_up=g_big[4],
                 conv_w=g_cw, conv_b=g_cb, w_down=g_big[5], norm_ffn_post=g_nfpost)
    weights = dict(norm_mix_pre=norm_mix_pre, w_in=w_in, b_gate=b_gate, rel_bias=rel_bias,
                   w_attn_out=w_attn_out, w_pool_group=w_pool_group, pool_scale=pool_scale,
                   w_pool_out=w_pool_out, w_o=w_o, norm_mix_post=norm_mix_post,
                   norm_ffn_pre=norm_ffn_pre, w_up=w_up, conv_w=conv_w, conv_b=conv_b,
                   w_down=w_down, norm_ffn_post=norm_ffn_post)
    moms = dict(norm_mix_pre=(m_norm_mix_pre, v_norm_mix_pre), w_in=(m_w_in, v_w_in),
                b_gate=(m_b_gate, v_b_gate), rel_bias=(m_rel_bias, v_rel_bias),
                w_attn_out=(m_w_attn_out, v_w_attn_out),
                w_pool_group=(m_w_pool_group, v_w_pool_group),
                pool_scale=(m_pool_scale, v_pool_scale), w_pool_out=(m_w_pool_out, v_w_pool_out),
                w_o=(m_w_o, v_w_o), norm_mix_post=(m_norm_mix_post, v_norm_mix_post),
                norm_ffn_pre=(m_norm_ffn_pre, v_norm_ffn_pre), w_up=(m_w_up, v_w_up),
                conv_w=(m_conv_w, v_conv_w), conv_b=(m_conv_b, v_conv_b),
                w_down=(m_w_down, v_w_down), norm_ffn_post=(m_norm_ffn_post, v_norm_ffn_post))
    order = list(weights.keys())

    small_names = [nm for nm in order if nm not in names]
    for nm in names:
        if nm not in delta:
            delta[nm], new_m[nm], new_v[nm] = _adamw("adamw_" + nm, weights[nm], grads[nm],
                                                     *moms[nm])
    d_s, m_s, v_s = _adamw_small([weights[nm] for nm in small_names],
                                 [grads[nm] for nm in small_names],
                                 [moms[nm][0] for nm in small_names],
                                 [moms[nm][1] for nm in small_names])
    for i, nm in enumerate(small_names):
        delta[nm], new_m[nm], new_v[nm] = d_s[i], m_s[i], v_s[i]

    return (loss, grad_x, *[grads[nm] for nm in order], *[delta[nm] for nm in order],
            *[new_m[nm] for nm in order], *[new_v[nm] for nm in order])
```

```python
import functools
import math

import jax
import jax.numpy as jnp
from jax import lax
from jax.experimental import pallas as pl
from jax.experimental.pallas import tpu as pltpu

F32 = jnp.float32
BF16 = jnp.bfloat16
MESH = pl.DeviceIdType.MESH

D_MODEL = 1024
DEPTH = 2
CHUNK = 64
BAND_CHUNKS = 9
BAND = BAND_CHUNKS * CHUNK
HEADS = 8
HEAD_DIM = 64
ATTN_W = HEADS * HEAD_DIM
POOL_WINDOWS = (2, 4, 8, 16)
POOL_W = 512
POOL_GD = 128
MAX_REL = 256
N_REL = 2 * MAX_REL + 1
D_FF = 2816
IN_W = 3 * ATTN_W + POOL_W + 2 * D_MODEL
EPS = 1e-6
ATTN_SCALE = HEAD_DIM ** -0.5
BAND_PAD = 640
BIAS_LANES = BAND_PAD
N_CHIPS = 4

ADAM_LR = 0.001
ADAM_B1 = 0.9
ADAM_B2 = 0.999
ADAM_EPS = 1e-08
ADAM_WD = 0.01
ADAM_STEP = 10

VMEM_LIMIT_V7X = 56 * 1024 * 1024
TOK = 512
ATT_BLK = 8 * CHUNK
FF_COL = 256
FF_TOK = 1024
HALO = 32


def _cparams(*sem):
    return pltpu.CompilerParams(dimension_semantics=sem, vmem_limit_bytes=VMEM_LIMIT_V7X)


def _sds(shape, dtype):
    return jax.ShapeDtypeStruct(shape, dtype)


class _Comm:
    def __init__(self, ins, outs, copies, n_sems, alias=None):
        self.ins, self.outs, self.copies, self.n_sems = list(ins), list(outs), copies, n_sems
        self.alias = dict(alias or {})


class _SemsFrom:
    def __init__(self, sems, start):
        self.sems, self.start = sems, start

    @property
    def at(self):
        return self

    def __getitem__(self, i):
        return self.sems.at[self.start + i]


def _both(a, b):
    na, nao = len(a.ins), len(a.outs)

    def copies(cin, cout, ssem, rsem):
        return (a.copies(cin[:na], cout[:nao], ssem, rsem)
                + b.copies(cin[na:], cout[nao:], _SemsFrom(ssem, a.n_sems), _SemsFrom(rsem, a.n_sems)))

    alias = dict(a.alias)
    alias.update({na + i: nao + o for i, o in b.alias.items()})
    return _Comm(a.ins + b.ins, a.outs + b.outs, copies, a.n_sems + b.n_sems, alias)


def _pcall(body, name, grid, in_specs, out_specs, out_shape, scratch_shapes, compiler_params, args,
           comm=None, aliases=None):
    single = not isinstance(out_shape, (list, tuple))
    out_specs = [out_specs] if single else list(out_specs)
    out_shape = [out_shape] if single else list(out_shape)
    n_in, n_out = len(in_specs), len(out_specs)
    aliases = dict(aliases or {})
    if comm is None:
        res = pl.pallas_call(
            body, name=name, grid=grid, in_specs=list(in_specs), out_specs=out_specs,
            out_shape=out_shape, scratch_shapes=list(scratch_shapes),
            input_output_aliases=aliases, compiler_params=compiler_params)(*args)
        return (res[0] if single else res), None
    ci, co = len(comm.ins), len(comm.outs)

    def hosted(*refs):
        main_in, cin = refs[:n_in], refs[n_in:n_in + ci]
        main_out = refs[n_in + ci:n_in + ci + n_out]
        cout = refs[n_in + ci + n_out:n_in + ci + n_out + co]
        rest = refs[n_in + ci + n_out + co:]
        copies = comm.copies(cin, cout, rest[-2], rest[-1])
        ids = [pl.program_id(a) for a in range(len(grid))]
        first = functools.reduce(jnp.logical_and, [i == 0 for i in ids])
        last = functools.reduce(jnp.logical_and, [i == g - 1 for i, g in zip(ids, grid)])

        @pl.when(first)
        def _():
            for cp in copies:
                cp.start()

        body(*main_in, *main_out, *rest[:-2])

        @pl.when(last)
        def _():
            for cp in copies:
                cp.wait()

    for i, o in comm.alias.items():
        aliases[n_in + i] = n_out + o
    hbm = pl.BlockSpec(memory_space=pl.ANY)
    sems = pltpu.SemaphoreType.DMA((comm.n_sems,))
    res = pl.pallas_call(
        hosted, name=name, grid=grid, in_specs=list(in_specs) + [hbm] * ci,
        out_specs=out_specs + [hbm] * co, out_shape=out_shape + comm.outs,
        scratch_shapes=list(scratch_shapes) + [sems, sems],
        input_output_aliases=aliases, compiler_params=compiler_params)(*args, *comm.ins)
    return (res[0] if single else list(res[:n_out])), list(res[n_out:])


def _comm_call(name, comm):
    ci = len(comm.ins)

    def body(*refs):
        copies = comm.copies(refs[:ci], refs[ci:-2], refs[-2], refs[-1])
        for cp in copies:
            cp.start()
        for cp in copies:
            cp.wait()

    hbm = pl.BlockSpec(memory_space=pl.ANY)
    sems = pltpu.SemaphoreType.DMA((comm.n_sems,))
    return list(pl.pallas_call(
        body, name=name, in_specs=[hbm] * ci, out_specs=[hbm] * len(comm.outs),
        out_shape=comm.outs, scratch_shapes=[sems, sems],
        input_output_aliases=comm.alias)(*comm.ins))


def _matmul(name, a, b, a_spec, b_spec, o_spec, out_shape, grid, contract, nk, acc_shape,
            fill=None, comm=None):
    in_place = out_shape.dtype == F32

    def body(*refs):
        a_ref, b_ref = refs[0], refs[1]
        o_ref = refs[2 if fill is None else 3]
        scratch = refs[(3 if fill is None else 4):]
        part = lax.dot_general(a_ref[...], b_ref[...], (contract, ((), ())),
                               preferred_element_type=F32)
        if nk == 1:
            o_ref[...] = part.astype(o_ref.dtype)
        else:
            acc_ref = o_ref if in_place else scratch[0]
            k = pl.program_id(2)

            @pl.when(k == 0)
            def _():
                acc_ref[...] = part

            @pl.when(k > 0)
            def _():
                acc_ref[...] += part

            if not in_place:
                @pl.when(k == nk - 1)
                def _():
                    o_ref[...] = acc_ref[...].astype(o_ref.dtype)

    scratch = [] if nk == 1 or in_place else [pltpu.VMEM(acc_shape, F32)]
    in_specs, args, aliases = [a_spec, b_spec], [a, b], {}
    if fill is not None:
        in_specs.append(pl.BlockSpec(memory_space=pl.ANY))
        args.append(fill)
        aliases = {2: 0}
    out, moved = _pcall(body, name, grid, in_specs, o_spec, out_shape, scratch,
                        _cparams("parallel", "parallel", "arbitrary"), args, comm, aliases)
    return out if comm is None else (out, moved)


NN = ((1,), (0,))
NT = ((1,), (1,))
TN = ((0,), (0,))


def _tm(t):
    return min(t, 1024)


def _tt(t):
    return min(t, 2048)


def _col_block_spec(a, rows, nb, row_col):
    if a.ndim == 2:
        return pl.BlockSpec((rows, nb), row_col)

    def halves(*ids):
        r, c = row_col(*ids)
        return c // 2, r, c % 2

    return pl.BlockSpec((None, rows, nb), halves)


def _mm_nn_blocked(name, a, w, l, out_dtype, comm=None):
    t, k = a.shape
    nb = w.shape[3]
    tm = _tm(t)
    return _matmul(
        name, a, w,
        pl.BlockSpec((tm, k), lambda i, n, kk: (i, 0)),
        pl.BlockSpec((None, None, k, nb), lambda i, n, kk: (l, n, 0, 0)),
        pl.BlockSpec((tm, nb), lambda i, n, kk: (i, n)),
        _sds((t, N_CHIPS * nb), out_dtype), (t // tm, N_CHIPS, 1), NN, 1, None, comm=comm)


def _mm_nt_blocked(name, a, w, l, out_dtype, comm=None):
    t = a.shape[-2]
    k, nb = w.shape[2], w.shape[3]
    tm = _tm(t)
    return _matmul(
        name, a, w,
        _col_block_spec(a, tm, nb, lambda i, n, kk: (i, kk)),
        pl.BlockSpec((None, None, k, nb), lambda i, n, kk: (l, kk, 0, 0)),
        pl.BlockSpec((tm, k), lambda i, n, kk: (i, 0)),
        _sds((t, k), out_dtype), (t // tm, 1, N_CHIPS), NT, N_CHIPS, (tm, k), comm=comm)


def _mm_tn_blocked(name, a, g, l, fill):
    t, k = a.shape
    nb = g.shape[-1] * (g.ndim - 1) // N_CHIPS
    tt = _tt(t)
    nt = t // tt
    return _matmul(
        name, a, g,
        pl.BlockSpec((tt, k), lambda n, j, kk: (kk, 0)),
        _col_block_spec(g, tt, nb, lambda n, j, kk: (kk, n)),
        pl.BlockSpec((None, None, k, nb), lambda n, j, kk: (l, n, 0, 0)),
        _sds((DEPTH, N_CHIPS, k, nb), BF16), (N_CHIPS, 1, nt), TN, nt, (k, nb), fill)


def _proj_pieces(rows, dqkv_first):
    def piece(col):
        if dqkv_first:
            return pl.BlockSpec((rows, ATTN_W), lambda i, kk: (i, col))
        return pl.BlockSpec((rows, ATTN_W), lambda n, kk: (kk, col))
    return [piece(0), piece(1), piece(2), piece(0)]


def _proj_dx(name, dqkv, du, dgates, w, l, comm=None):
    t = du.shape[0]
    k, nb = w.shape[2], w.shape[3]
    tm = _tm(t)

    def body(dq_ref, dk_ref, dv_ref, du_ref, dg_ref, w_ref, o_ref):
        kk = pl.program_id(1)

        def mm(a):
            return lax.dot_general(a, w_ref[...], (NT, ((), ())), preferred_element_type=F32)

        @pl.when(kk == 0)
        def _():
            o_ref[...] = mm(jnp.concatenate([dq_ref[...], dk_ref[...]], axis=1))

        @pl.when(kk == 1)
        def _():
            o_ref[...] += mm(jnp.concatenate([dv_ref[...], du_ref[...]], axis=1))

        @pl.when(kk >= 2)
        def _():
            o_ref[...] += mm(dg_ref[...])

    out, moved = _pcall(
        body, name, (t // tm, N_CHIPS),
        _proj_pieces(tm, True)
        + [pl.BlockSpec((tm, nb), lambda i, kk: (i, jnp.maximum(kk - 2, 0))),
           pl.BlockSpec((None, None, k, nb), lambda i, kk: (l, kk, 0, 0))],
        pl.BlockSpec((tm, k), lambda i, kk: (i, 0)), _sds((t, k), F32),
        [], _cparams("arbitrary", "arbitrary"),
        (dqkv, dqkv, dqkv, du, dgates, w), comm)
    return out if comm is None else (out, moved)


def _proj_dw(name, h, dqkv, du, dgates, l, fill):
    t, k = h.shape
    nb = dgates.shape[1] // 2
    tt = _tm(t)
    nt = t // tt

    def body(*refs):
        h_ref, dq_ref, dk_ref, dv_ref, du_ref, dg_ref = refs[:6]
        o_ref, acc_ref = refs[-2], refs[-1]
        n, kk = pl.program_id(0), pl.program_id(1)

        def update(g):
            part = lax.dot_general(h_ref[...], g, (TN, ((), ())), preferred_element_type=F32)

            @pl.when(kk == 0)
            def _():
                acc_ref[...] = part

            @pl.when(kk > 0)
            def _():
                acc_ref[...] += part

        @pl.when(n == 0)
        def _():
            update(jnp.concatenate([dq_ref[...], dk_ref[...]], axis=1))

        @pl.when(n == 1)
        def _():
            update(jnp.concatenate([dv_ref[...], du_ref[...]], axis=1))

        @pl.when(n >= 2)
        def _():
            update(dg_ref[...])

        @pl.when(kk == nt - 1)
        def _():
            o_ref[...] = acc_ref[...].astype(BF16)

    in_specs = ([pl.BlockSpec((tt, k), lambda n, kk: (kk, 0))] + _proj_pieces(tt, False)
                + [pl.BlockSpec((tt, nb), lambda n, kk: (kk, jnp.maximum(n - 2, 0)))])
    args, aliases = [h, dqkv, dqkv, dqkv, du, dgates], {}
    if fill is not None:
        in_specs.append(pl.BlockSpec(memory_space=pl.ANY))
        args.append(fill)
        aliases = {6: 0}
    return pl.pallas_call(
        body, name=name, grid=(N_CHIPS, nt), in_specs=in_specs,
        out_specs=pl.BlockSpec((None, None, k, nb), lambda n, kk: (l, n, 0, 0)),
        out_shape=_sds((DEPTH, N_CHIPS, k, nb), BF16),
        scratch_shapes=[pltpu.VMEM((k, nb), F32)], input_output_aliases=aliases,
        compiler_params=_cparams("parallel", "arbitrary"))(*args)


def _narrow_nn(name, a, w, l):
    t, k = a.shape
    nb = w.shape[3]
    tm = _tm(t)

    def body(a_ref, w_ref, o_ref):
        av = a_ref[...]
        for j in range(N_CHIPS):
            o_ref[:, j * nb:(j + 1) * nb] = jnp.dot(
                av, w_ref[j], preferred_element_type=F32).astype(BF16)

    return pl.pallas_call(
        body, name=name, grid=(t // tm,),
        in_specs=[pl.BlockSpec((tm, k), lambda i: (i, 0)),
                  pl.BlockSpec((None, N_CHIPS, k, nb), lambda i: (l, 0, 0, 0))],
        out_specs=pl.BlockSpec((tm, N_CHIPS * nb), lambda i: (i, 0)),
        out_shape=_sds((t, N_CHIPS * nb), BF16), compiler_params=_cparams("parallel"))(a, w)


def _narrow_nt(name, a, w, l):
    t = a.shape[0]
    k, nb = w.shape[2], w.shape[3]
    tm = _tm(t)

    def body(a_ref, w_ref, o_ref):
        acc = lax.dot_general(a_ref[:, 0:nb], w_ref[0], (NT, ((), ())), preferred_element_type=F32)
        for j in range(1, N_CHIPS):
            acc = acc + lax.dot_general(a_ref[:, j * nb:(j + 1) * nb], w_ref[j], (NT, ((), ())),
                                        preferred_element_type=F32)
        o_ref[...] = acc.astype(BF16)

    return pl.pallas_call(
        body, name=name, grid=(t // tm,),
        in_specs=[pl.BlockSpec((tm, N_CHIPS * nb), lambda i: (i, 0)),
                  pl.BlockSpec((None, N_CHIPS, k, nb), lambda i: (l, 0, 0, 0))],
        out_specs=pl.BlockSpec((tm, k), lambda i: (i, 0)),
        out_shape=_sds((t, k), BF16), compiler_params=_cparams("parallel"))(a, w)


def _narrow_tn(name, a, g, l, fill):
    t, k = a.shape
    nb = g.shape[1] // N_CHIPS
    tt = _tm(t)
    nt = t // tt

    def body(*refs):
        a_ref, g_ref, o_ref, acc_ref = refs[0], refs[1], refs[-2], refs[-1]
        i = pl.program_id(0)
        part = lax.dot_general(a_ref[...], g_ref[...], (TN, ((), ())), preferred_element_type=F32)

        @pl.when(i == 0)
        def _():
            acc_ref[...] = part

        @pl.when(i > 0)
        def _():
            acc_ref[...] += part

        @pl.when(i == nt - 1)
        def _():
            for j in range(N_CHIPS):
                o_ref[j] = acc_ref[:, j * nb:(j + 1) * nb].astype(BF16)

    in_specs = [pl.BlockSpec((tt, k), lambda i: (i, 0)),
                pl.BlockSpec((tt, N_CHIPS * nb), lambda i: (i, 0))]
    args, aliases = [a, g], {}
    if fill is not None:
        in_specs.append(pl.BlockSpec(memory_space=pl.ANY))
        args.append(fill)
        aliases = {2: 0}
    return pl.pallas_call(
        body, name=name, grid=(nt,), in_specs=in_specs,
        out_specs=pl.BlockSpec((None, N_CHIPS, k, nb), lambda i: (l, 0, 0, 0)),
        out_shape=_sds((DEPTH, N_CHIPS, k, nb), BF16),
        scratch_shapes=[pltpu.VMEM((k, N_CHIPS * nb), F32)], input_output_aliases=aliases,
        compiler_params=_cparams("arbitrary"))(*args)


def _mm_nn(name, a, w, l, tk, out_dtype):
    t, k = a.shape
    n = w.shape[2]
    tm = _tm(t)
    nk = k // tk
    return _matmul(
        name, a, w,
        pl.BlockSpec((tm, tk), lambda i, j, kk: (i, kk)),
        pl.BlockSpec((None, tk, n), lambda i, j, kk: (l, kk, 0)),
        pl.BlockSpec((tm, n), lambda i, j, kk: (i, 0)),
        _sds((t, n), out_dtype), (t // tm, 1, nk), NN, nk, (tm, n))


def _mm_nt(name, a, w, l, tn, out_dtype, comm=None):
    t, n = a.shape
    k = w.shape[1]
    tm = _tm(t)
    return _matmul(
        name, a, w,
        pl.BlockSpec((tm, n), lambda i, j, kk: (i, 0)),
        pl.BlockSpec((None, tn, n), lambda i, j, kk: (l, j, 0)),
        pl.BlockSpec((tm, tn), lambda i, j, kk: (i, j)),
        _sds((t, k), out_dtype), (t // tm, k // tn, 1), NT, 1, None, comm=comm)


def _mm_tn(name, a, g, tko, l, fill):
    t, k = a.shape
    n = g.shape[1]
    tt = _tt(t)
    nt = t // tt
    return _matmul(
        name, a, g,
        pl.BlockSpec((tt, tko), lambda i, j, kk: (kk, i)),
        pl.BlockSpec((tt, n), lambda i, j, kk: (kk, 0)),
        pl.BlockSpec((None, tko, n), lambda i, j, kk: (l, i, 0)),
        _sds((DEPTH, k, n), BF16), (k // tko, 1, nt), TN, nt, (tko, n), fill)


def _row_spec(width, col=0):
    return pl.BlockSpec((TOK, width), lambda i: (i, col))


def _vec_spec(width):
    return pl.BlockSpec((1, width), lambda i: (0, 0))


def _rms(x):
    return lax.rsqrt(jnp.mean(x * x, axis=-1, keepdims=True) + EPS)


def _norm_fwd(name, x, g, comm=None):
    t = x.shape[0]

    def body(x_ref, g_ref, h_ref):
        xv = x_ref[...]
        h_ref[...] = (xv * _rms(xv) * g_ref[...]).astype(BF16)

    out, moved = _pcall(body, name, (t // TOK,), [_row_spec(D_MODEL), _vec_spec(D_MODEL)],
                        _row_spec(D_MODEL), _sds((t, D_MODEL), BF16), [], _cparams("arbitrary"),
                        (x, g), comm)
    return out if comm is None else (out, moved)


ROWS = 16
ROW_UNROLL = 8


def _rows(k):
    return pl.ds(pl.multiple_of(k * ROWS, ROWS), ROWS)


def _strips(step, init):
    def group(j, carry):
        for u in range(ROW_UNROLL):
            carry = step(j * ROW_UNROLL + u, carry)
        return carry

    return lax.fori_loop(0, TOK // (ROWS * ROW_UNROLL), group, init)


def _fold_rows(x):
    return x[0:8] + x[8:16]


def _accumulate(ref, part):
    total = jnp.sum(part, axis=0, keepdims=True)

    @pl.when(pl.program_id(0) == 0)
    def _():
        ref[...] = total

    @pl.when(pl.program_id(0) > 0)
    def _():
        ref[...] += total


def _norm_bwd_rows(d, mv, g):
    r = _rms(mv)
    n = mv * r
    dn = d * g
    return r * (dn - n * jnp.mean(dn * n, axis=-1, keepdims=True)), d * n


def _post_pre_fwd(name, xres, m, g_post, g_pre, comm=None):
    t = xres.shape[0]

    def body(x_ref, m_ref, gp_ref, gn_ref, x1_ref, h_ref):
        def strip(k, c):
            rows = _rows(k)
            mv = m_ref[rows, :]
            x1 = x_ref[rows, :] + mv * _rms(mv) * gp_ref[...]
            x1_ref[rows, :] = x1
            h_ref[rows, :] = (x1 * _rms(x1) * gn_ref[...]).astype(BF16)
            return c

        _strips(strip, 0)

    outs, moved = _pcall(
        body, name, (t // TOK,),
        [_row_spec(D_MODEL), _row_spec(D_MODEL), _vec_spec(D_MODEL), _vec_spec(D_MODEL)],
        [_row_spec(D_MODEL), _row_spec(D_MODEL)],
        [_sds((t, D_MODEL), F32), _sds((t, D_MODEL), BF16)], [], _cparams("arbitrary"),
        (xres, m, g_post, g_pre), comm)
    return outs if comm is None else (*outs, moved)


def _tail(name, xres, m, g_post, target):
    t = xres.shape[0]

    def body(x_ref, m_ref, g_ref, t_ref, dy_ref, dm_ref, dg_ref, l_ref):
        def strip(k, carry):
            rows = _rows(k)
            mv = m_ref[rows, :]
            e = x_ref[rows, :] + mv * _rms(mv) * g_ref[...] - t_ref[rows, :]
            dy = e * (1.0 / D_MODEL)
            dy_ref[rows, :] = dy
            dm, dgn = _norm_bwd_rows(dy, mv, g_ref[...])
            dm_ref[rows, :] = dm.astype(BF16)
            return carry[0] + _fold_rows(dgn), carry[1] + _fold_rows(e * e)

        zero = jnp.zeros((8, D_MODEL), F32)
        dg, sq = _strips(strip, (zero, zero))
        _accumulate(dg_ref, dg)
        _accumulate(l_ref, jnp.sum(sq, axis=1, keepdims=True))

    dy, dm, dg, sq = pl.pallas_call(
        body, name=name, grid=(t // TOK,),
        in_specs=[_row_spec(D_MODEL), _row_spec(D_MODEL), _vec_spec(D_MODEL), _row_spec(D_MODEL)],
        out_specs=[_row_spec(D_MODEL), _row_spec(D_MODEL), _vec_spec(D_MODEL),
                   pl.BlockSpec((1, 1), lambda i: (0, 0))],
        out_shape=[_sds((t, D_MODEL), F32), _sds((t, D_MODEL), BF16), _sds((1, D_MODEL), F32),
                   _sds((1, 1), F32)],
        compiler_params=_cparams("arbitrary"))(xres, m, g_post, target)
    return dy, dm, dg, sq[0, 0] * (0.5 / D_MODEL)


def _pre_post_bwd(name, dh, xin, dxo, g_pre, m, g_post, comm=None):
    t = dh.shape[0]

    def body(dh_ref, x_ref, d_ref, gq_ref, m_ref, gp_ref, dx_ref, dgq_ref, dm_ref, dgp_ref):
        def strip(k, carry):
            rows = _rows(k)
            dxin, dgq = _norm_bwd_rows(dh_ref[rows, :], x_ref[rows, :], gq_ref[...])
            dx = d_ref[rows, :] + dxin
            dx_ref[rows, :] = dx
            dm, dgp = _norm_bwd_rows(dx, m_ref[rows, :], gp_ref[...])
            dm_ref[rows, :] = dm.astype(BF16)
            return carry[0] + _fold_rows(dgq), carry[1] + _fold_rows(dgp)

        zero = jnp.zeros((8, D_MODEL), F32)
        dgq, dgp = _strips(strip, (zero, zero))
        _accumulate(dgq_ref, dgq)
        _accumulate(dgp_ref, dgp)

    outs, moved = _pcall(
        body, name, (t // TOK,),
        [_row_spec(D_MODEL), _row_spec(D_MODEL), _row_spec(D_MODEL), _vec_spec(D_MODEL),
         _row_spec(D_MODEL), _vec_spec(D_MODEL)],
        [_row_spec(D_MODEL), _vec_spec(D_MODEL), _row_spec(D_MODEL), _vec_spec(D_MODEL)],
        [_sds((t, D_MODEL), F32), _sds((1, D_MODEL), F32), _sds((t, D_MODEL), BF16),
         _sds((1, D_MODEL), F32)], [], _cparams("arbitrary"),
        (dh, xin, dxo, g_pre, m, g_post), comm)
    return outs if comm is None else (*outs, moved)


def _norm_pre_bwd(name, dh, xin, dxo, g, comm=None):
    t = dh.shape[0]

    def body(dh_ref, x_ref, d_ref, g_ref, dx_ref, dg_ref):
        xv = x_ref[...]
        dhv = dh_ref[...]
        r = _rms(xv)
        n = xv * r
        dn = dhv * g_ref[...]
        dx_ref[...] = d_ref[...] + r * (dn - n * jnp.mean(dn * n, axis=-1, keepdims=True))
        part = jnp.sum(dhv * n, axis=0, keepdims=True)

        @pl.when(pl.program_id(0) == 0)
        def _():
            dg_ref[...] = part

        @pl.when(pl.program_id(0) > 0)
        def _():
            dg_ref[...] += part

    out, moved = _pcall(
        body, name, (t // TOK,),
        [_row_spec(D_MODEL), _row_spec(D_MODEL), _row_spec(D_MODEL), _vec_spec(D_MODEL)],
        [_row_spec(D_MODEL), _vec_spec(D_MODEL)],
        [_sds((t, D_MODEL), F32), _sds((1, D_MODEL), F32)], [], _cparams("arbitrary"),
        (dh, xin, dxo, g), comm)
    return out if comm is None else (*out, moved)


def _gate_fwd(name, proj, b_gate, ya, yb):
    t = proj.shape[0]

    def body(ga_ref, gb_ref, b_ref, ya_ref, yb_ref, z_ref):
        def strip(k, c):
            rows = _rows(k)
            sa = jax.nn.sigmoid(ga_ref[rows, :].astype(F32) + b_ref[:, :D_MODEL])
            sb = jax.nn.sigmoid(gb_ref[rows, :].astype(F32) + b_ref[:, D_MODEL:])
            z_ref[rows, :] = (sa * ya_ref[rows, :].astype(F32)
                              + sb * yb_ref[rows, :].astype(F32)).astype(BF16)
            return c

        _strips(strip, 0)

    return pl.pallas_call(
        body, name=name, grid=(t // TOK,),
        in_specs=[_row_spec(D_MODEL, 2), _row_spec(D_MODEL, 3), _vec_spec(2 * D_MODEL),
                  _row_spec(D_MODEL), _row_spec(D_MODEL)],
        out_specs=_row_spec(D_MODEL), out_shape=_sds((t, D_MODEL), BF16),
        compiler_params=_cparams("parallel"))(proj, proj, b_gate, ya, yb)


def _gate_bwd(name, dz, proj, b_gate, ya, yb):
    t = proj.shape[0]

    def body(dz_ref, ga_ref, gb_ref, b_ref, ya_ref, yb_ref, dya_ref, dyb_ref, dg_ref, db_ref):
        def strip(k, carry):
            rows = _rows(k)
            dzv = dz_ref[rows, :].astype(F32)
            sa = jax.nn.sigmoid(ga_ref[rows, :].astype(F32) + b_ref[:, :D_MODEL])
            sb = jax.nn.sigmoid(gb_ref[rows, :].astype(F32) + b_ref[:, D_MODEL:])
            dya_ref[rows, :] = (dzv * sa).astype(BF16)
            dyb_ref[rows, :] = (dzv * sb).astype(BF16)
            dga = dzv * ya_ref[rows, :].astype(F32) * sa * (1.0 - sa)
            dgb = dzv * yb_ref[rows, :].astype(F32) * sb * (1.0 - sb)
            dg_ref[rows, :D_MODEL] = dga.astype(BF16)
            dg_ref[rows, D_MODEL:] = dgb.astype(BF16)
            return carry[0] + _fold_rows(dga), carry[1] + _fold_rows(dgb)

        zero = jnp.zeros((8, D_MODEL), F32)
        pa, pb = _strips(strip, (zero, zero))
        _accumulate(db_ref.at[:, :D_MODEL], pa)
        _accumulate(db_ref.at[:, D_MODEL:], pb)

    return pl.pallas_call(
        body, name=name, grid=(t // TOK,),
        in_specs=[_row_spec(D_MODEL), _row_spec(D_MODEL, 2), _row_spec(D_MODEL, 3),
                  _vec_spec(2 * D_MODEL), _row_spec(D_MODEL), _row_spec(D_MODEL)],
        out_specs=[_row_spec(D_MODEL), _row_spec(D_MODEL), _row_spec(2 * D_MODEL),
                   _vec_spec(2 * D_MODEL)],
        out_shape=[_sds((t, D_MODEL), BF16), _sds((t, D_MODEL), BF16),
                   _sds((t, 2 * D_MODEL), BF16), _sds((1, 2 * D_MODEL), F32)],
        compiler_params=_cparams("arbitrary"))(dz, proj, proj, b_gate, ya, yb)


def _head_masks():
    lane = lax.broadcasted_iota(jnp.int32, (1, 2 * HEAD_DIM), 1)
    return lane < HEAD_DIM


BAND_ROWS = 2 * ATT_BLK + CHUNK


def _fill_band(band, prev_ref, cur_ref):
    band[0:ATT_BLK, :] = prev_ref[...]
    band[ATT_BLK:2 * ATT_BLK, :] = cur_ref[...]
    band[2 * ATT_BLK:, :] = jnp.zeros((CHUNK, ATTN_W), BF16)


def _pair_rows(x2, low):
    zero = jnp.zeros_like(x2)
    return jnp.concatenate([jnp.where(low, x2, zero), jnp.where(low, zero, x2)], axis=0)


def _pair_diag(o2, low):
    return jnp.where(low, o2[0:CHUNK, :], o2[CHUNK:, :])


N_PAIRS = HEADS // 2
SM_STRIP = 32
N_STRIPS = BAND_PAD // SM_STRIP
NEG = -1e30


def _fold8(x, op):
    return op(op(x[0:8], x[8:16]), op(x[16:24], x[24:32]))


def _strip(k):
    return pl.ds(pl.multiple_of(k * SM_STRIP, SM_STRIP), SM_STRIP)


def _band_probs(k2, qcat, bias_t, first_key):
    kpos = lax.broadcasted_iota(jnp.int32, (BAND_PAD, 1), 0)
    st = lax.dot_general(k2, qcat, (NT, ((), ())), preferred_element_type=F32)
    st = jnp.where(kpos + first_key >= 0, st + bias_t, NEG)
    e = jnp.exp(st - jnp.max(st, axis=0, keepdims=True))
    return e * (1.0 / jnp.sum(e, axis=0, keepdims=True))


def _attn_specs(nblk):
    cur = lambda col: pl.BlockSpec((ATT_BLK, ATTN_W), lambda s: (jnp.minimum(s, nblk - 1), col))
    prev = lambda col: pl.BlockSpec(
        (ATT_BLK, ATTN_W), lambda s: (jnp.maximum(jnp.minimum(s, nblk - 1) - 1, 0), col))
    return cur, prev


def _attn_fwd(name, proj, bias, comm=None):
    t = proj.shape[0]
    nblk = t // ATT_BLK
    cur, prev = _attn_specs(nblk)

    def body(q_ref, kp_ref, kc_ref, vp_ref, vc_ref, b_ref, o_ref, p_ref, kband, vband):
        s = pl.program_id(0)
        _fill_band(kband, kp_ref, kc_ref)
        _fill_band(vband, vp_ref, vc_ref)
        low = _head_masks()

        def chunk(ci, carry):
            r0 = pl.multiple_of(ci * CHUNK, CHUNK)
            for hp in range(N_PAIRS):
                cols = slice(hp * 128, (hp + 1) * 128)
                qcat = _pair_rows(q_ref[pl.ds(r0, CHUNK), cols] * ATTN_SCALE, low)
                p = _band_probs(kband[pl.ds(r0, BAND_PAD), cols], qcat, b_ref[hp],
                                (s * 8 - 8 + ci) * CHUNK).astype(BF16)
                p_ref[ci, hp] = p
                o2 = lax.dot_general(p, vband[pl.ds(r0, BAND_PAD), cols],
                                     (TN, ((), ())), preferred_element_type=F32)
                o_ref[pl.ds(r0, CHUNK), cols] = _pair_diag(o2, low).astype(BF16)
            return carry

        lax.fori_loop(0, 8, chunk, 0)

    outs, moved = _pcall(
        body, name, (nblk,),
        [cur(0), prev(1), cur(1), prev(2), cur(2),
         pl.BlockSpec((N_PAIRS, BAND_PAD, 128), lambda s: (0, 0, 0))],
        [pl.BlockSpec((ATT_BLK, ATTN_W), lambda s: (s, 0)),
         pl.BlockSpec((8, N_PAIRS, BAND_PAD, 128), lambda s: (s, 0, 0, 0))],
        [_sds((t, ATTN_W), BF16), _sds((t // CHUNK, N_PAIRS, BAND_PAD, 128), BF16)],
        [pltpu.VMEM((BAND_ROWS, ATTN_W), BF16), pltpu.VMEM((BAND_ROWS, ATTN_W), BF16)],
        _cparams("arbitrary"), (proj, proj, proj, proj, proj, bias), comm)
    return outs if comm is None else (*outs, moved)


def _attn_bwd(name, proj, datt, probs, comm=None):
    t = proj.shape[0]
    nblk = t // ATT_BLK
    cur, prev = _attn_specs(nblk)
    late = pl.BlockSpec((ATT_BLK, 3 * ATTN_W), lambda s: (jnp.maximum(s - 1, 0), 0))

    def body(q_ref, kp_ref, kc_ref, vp_ref, vc_ref, do_ref, p_ref,
             dqkv_ref, db_ref, kband, vband, dkacc, dvacc,
             dp_ref, dsb_ref, qc_ref, dc_ref, dq_ref, dq_held):
        s = pl.program_id(0)

        @pl.when(s == 0)
        def _():
            dkacc[...] = jnp.zeros_like(dkacc)
            dvacc[...] = jnp.zeros_like(dvacc)
            db_ref[...] = jnp.zeros_like(db_ref)
            dq_ref[...] = jnp.zeros_like(dq_ref)

        @pl.when(s < nblk)
        def _():
            _fill_band(kband, kp_ref, kc_ref)
            _fill_band(vband, vp_ref, vc_ref)
            low = _head_masks()

            def chunk(ci, carry):
                r0 = pl.multiple_of(ci * CHUNK, CHUNK)
                for hp in range(N_PAIRS):
                    cols = slice(hp * 128, (hp + 1) * 128)
                    qc_ref[hp] = _pair_rows(q_ref[pl.ds(r0, CHUNK), cols] * ATTN_SCALE, low)
                    dc_ref[hp] = _pair_rows(do_ref[pl.ds(r0, CHUNK), cols], low)
                    dp_ref[hp] = lax.dot_general(vband[pl.ds(r0, BAND_PAD), cols], dc_ref[hp],
                                                 (NT, ((), ())), preferred_element_type=F32)

                def sums(k, acc):
                    rows = _strip(k)
                    return tuple(acc[hp] + _fold8(p_ref[ci, hp, rows, :].astype(F32)
                                                  * dp_ref[hp, rows, :], jnp.add)
                                 for hp in range(N_PAIRS))

                acc = lax.fori_loop(0, N_STRIPS, sums, (jnp.zeros((8, 128), F32),) * N_PAIRS)
                delta = [jnp.sum(a, axis=0, keepdims=True) for a in acc]

                def grads(k, c):
                    rows = _strip(k)
                    for hp in range(N_PAIRS):
                        ds = (p_ref[ci, hp, rows, :].astype(F32)
                              * (dp_ref[hp, rows, :] - delta[hp]))
                        db_ref[hp, rows, :] += ds
                        dsb_ref[hp, rows, :] = ds.astype(BF16)
                    return c

                lax.fori_loop(0, N_STRIPS, grads, 0)
                for hp in range(N_PAIRS):
                    cols = slice(hp * 128, (hp + 1) * 128)
                    dq2 = lax.dot_general(dsb_ref[hp], kband[pl.ds(r0, BAND_PAD), cols],
                                          (TN, ((), ())), preferred_element_type=F32)
                    dq_ref[pl.ds(r0, CHUNK), cols] = (_pair_diag(dq2, low) * ATTN_SCALE).astype(BF16)
                    dkacc[pl.ds(r0, BAND_PAD), cols] += jnp.dot(dsb_ref[hp], qc_ref[hp],
                                                               preferred_element_type=F32)
                    dvacc[pl.ds(r0, BAND_PAD), cols] += jnp.dot(p_ref[ci, hp], dc_ref[hp],
                                                               preferred_element_type=F32)
                return carry

            dq_held[...] = dq_ref[...]
            lax.fori_loop(0, 8, chunk, 0)

        @pl.when(s == nblk)
        def _():
            dq_held[...] = dq_ref[...]

        dqkv_ref[:, 0:ATTN_W] = dq_held[...]
        dqkv_ref[:, ATTN_W:2 * ATTN_W] = dkacc[0:ATT_BLK, :].astype(BF16)
        dqkv_ref[:, 2 * ATTN_W:] = dvacc[0:ATT_BLK, :].astype(BF16)
        dkacc[0:ATT_BLK, :] = dkacc[ATT_BLK:2 * ATT_BLK, :]
        dvacc[0:ATT_BLK, :] = dvacc[ATT_BLK:2 * ATT_BLK, :]
        dkacc[ATT_BLK:, :] = jnp.zeros((ATT_BLK + CHUNK, ATTN_W), F32)
        dvacc[ATT_BLK:, :] = jnp.zeros((ATT_BLK + CHUNK, ATTN_W), F32)

    outs, moved = _pcall(
        body, name, (nblk + 1,),
        [cur(0), prev(1), cur(1), prev(2), cur(2),
         pl.BlockSpec((ATT_BLK, ATTN_W), lambda s: (jnp.minimum(s, nblk - 1), 0)),
         pl.BlockSpec((8, N_PAIRS, BAND_PAD, 128), lambda s: (jnp.minimum(s, nblk - 1), 0, 0, 0))],
        [late, pl.BlockSpec((HEADS // 2, BAND_PAD, 128), lambda s: (0, 0, 0))],
        [_sds((t, 3 * ATTN_W), BF16), _sds((HEADS // 2, BAND_PAD, 128), F32)],
        [pltpu.VMEM((BAND_ROWS, ATTN_W), BF16), pltpu.VMEM((BAND_ROWS, ATTN_W), BF16),
         pltpu.VMEM((BAND_ROWS, ATTN_W), F32), pltpu.VMEM((BAND_ROWS, ATTN_W), F32),
         pltpu.VMEM((N_PAIRS, BAND_PAD, 128), F32), pltpu.VMEM((N_PAIRS, BAND_PAD, 128), BF16),
         pltpu.VMEM((N_PAIRS, 2 * CHUNK, 128), BF16), pltpu.VMEM((N_PAIRS, 2 * CHUNK, 128), BF16),
         pltpu.VMEM((ATT_BLK, ATTN_W), BF16), pltpu.VMEM((ATT_BLK, ATTN_W), BF16)],
        _cparams("arbitrary"), (proj, proj, proj, proj, proj, datt, probs), comm)
    return outs if comm is None else (*outs, moved)


def _diag_onehot(rel_rows):
    d0 = lax.broadcasted_iota(jnp.int32, (BIAS_LANES, BIAS_LANES), 0)
    d1 = lax.broadcasted_iota(jnp.int32, (BIAS_LANES, BIAS_LANES), 1)
    m, n = (d0, d1) if rel_rows else (d1, d0)
    hit = (m == jnp.minimum(BAND - 1 + MAX_REL - n, 2 * MAX_REL)) & (n < BAND + CHUNK - 1)
    return jnp.where(hit, 1.0, 0.0).astype(F32)


def _bias_table(name, rel_bias_l):
    rel_pad = jnp.pad(rel_bias_l, ((0, 0), (0, BIAS_LANES - N_REL)))

    def body(r_ref, o_ref):
        diag = jnp.dot(r_ref[...], _diag_onehot(True), preferred_element_type=F32,
                       precision=lax.Precision.HIGHEST)
        rowid = lax.broadcasted_iota(jnp.int32, (8, BIAS_LANES), 0)
        lane = lax.broadcasted_iota(jnp.int32, (8, BIAS_LANES), 1)
        for h in range(HEADS):
            d8 = jnp.broadcast_to(diag[h:h + 1, :], (8, BIAS_LANES))
            slab0 = pltpu.roll(d8, BIAS_LANES - CHUNK + 1, axis=1)
            for b in range(1, 8):
                slab0 = jnp.where(rowid == b, pltpu.roll(d8, BIAS_LANES - CHUNK + 1 + b, axis=1),
                                  slab0)
            for a in range(8):
                slab = slab0 if a == 0 else pltpu.roll(slab0, 8 * a, axis=1)
                o_ref[h * CHUNK + 8 * a:h * CHUNK + 8 * a + 8, :] = jnp.where(lane < BAND, slab, NEG)

    tab = pl.pallas_call(
        body, name=name,
        in_specs=[pl.BlockSpec(memory_space=pltpu.VMEM)],
        out_specs=pl.BlockSpec(memory_space=pltpu.VMEM),
        out_shape=_sds((HEADS * CHUNK, BIAS_LANES), F32),
    )(rel_pad)
    tab = tab.reshape(HEADS // 2, 2, CHUNK, BIAS_LANES)
    return jnp.transpose(tab, (0, 3, 1, 2)).reshape(HEADS // 2, BIAS_LANES, 2 * CHUNK)


def _bias_fold(name, dbias_t):
    rows = HEADS * CHUNK
    dbias = jnp.transpose(dbias_t.reshape(HEADS // 2, BIAS_LANES, 2, CHUNK), (0, 2, 3, 1))

    def body(d_ref, o_ref):
        rowid = lax.broadcasted_iota(jnp.int32, (8, BIAS_LANES), 0)
        diags = []
        for h in range(HEADS):
            acc = d_ref[h * CHUNK + 56:h * CHUNK + 64, :]
            for a in range(7):
                slab = d_ref[h * CHUNK + 8 * a:h * CHUNK + 8 * a + 8, :]
                acc = acc + pltpu.roll(slab, 56 - 8 * a, axis=1)
            tot = jnp.where(rowid == 7, acc, 0.0)
            for b in range(7):
                tot = tot + jnp.where(rowid == b, pltpu.roll(acc, 7 - b, axis=1), 0.0)
            diags.append(jnp.sum(tot, axis=0, keepdims=True))
        diag = jnp.concatenate(diags, axis=0)
        o_ref[...] = jnp.dot(diag, _diag_onehot(False), preferred_element_type=F32,
                             precision=lax.Precision.HIGHEST)

    return pl.pallas_call(
        body, name=name,
        in_specs=[pl.BlockSpec(memory_space=pltpu.VMEM)],
        out_specs=pl.BlockSpec(memory_space=pltpu.VMEM),
        out_shape=_sds((HEADS, BIAS_LANES), F32),
    )(dbias.reshape(rows, BIAS_LANES))


def _inv_counts(i):
    trow = lax.broadcasted_iota(jnp.int32, (TOK + HALO, 1), 0) + i * TOK
    return [1.0 / jnp.minimum(trow + 1, w).astype(F32) for w in POOL_WINDOWS]


def _pool_fwd(name, proj, wg, scale, comm=None):
    t = proj.shape[0]
    hb = TOK // HALO

    def body(u_ref, up_ref, wg_ref, sc_ref, pooled_ref, mixed_ref, b0, b1, b2, b3):
        i = pl.program_id(0)
        halo = up_ref[...].astype(F32)
        b0[0:HALO, :] = jnp.where(i == 0, jnp.zeros_like(halo), halo)
        b0[HALO:, :] = u_ref[...].astype(F32)
        n = TOK + HALO
        b1[8:n, :] = b0[8:n, :] + b0[7:n - 1, :]
        b2[16:n, 128:] = b1[16:n, 128:] + b1[14:n - 2, 128:]
        b3[24:n, 256:] = b2[24:n, 256:] + b2[20:n - 4, 256:]
        wins = [b1[HALO:n, 0:128], b2[HALO:n, 128:256], b3[HALO:n, 256:384],
                b3[HALO:n, 384:512] + b3[HALO - 8:n - 8, 384:512]]
        inv = _inv_counts(i)
        for g in range(4):
            cols = slice(g * POOL_GD, (g + 1) * POOL_GD)
            pooled = (wins[g] * inv[g][0:TOK] - b0[HALO:n, cols]).astype(BF16)
            pooled_ref[:, cols] = pooled
            pre = jnp.dot(pooled, wg_ref[g], preferred_element_type=F32)
            mixed_ref[:, cols] = (pre * sc_ref[:, cols]).astype(BF16)

    buf = pltpu.VMEM((TOK + HALO, POOL_W), F32)
    outs, moved = _pcall(
        body, name, (t // TOK,),
        [_row_spec(POOL_W, 3),
         pl.BlockSpec((HALO, POOL_W), lambda i: (jnp.maximum(i * hb - 1, 0), 3)),
         pl.BlockSpec((4, POOL_GD, POOL_GD), lambda i: (0, 0, 0)), _vec_spec(POOL_W)],
        [_row_spec(POOL_W), _row_spec(POOL_W)],
        [_sds((t, POOL_W), BF16), _sds((t, POOL_W), BF16)], [buf, buf, buf, buf],
        _cparams("arbitrary"), (proj, proj, wg, scale), comm)
    return outs if comm is None else (*outs, moved)


def _pool_bwd(name, dmixed, pooled, wg, scale, comm=None):
    t = dmixed.shape[0]
    nt = t // TOK
    hb = TOK // HALO

    def body(dm_ref, dmn_ref, p_ref, wg_ref, sc_ref, du_ref, dwg_ref, dsc_ref, c0, c1, c2, c3):
        i = pl.program_id(0)

        @pl.when(i == 0)
        def _():
            dwg_ref[...] = jnp.zeros_like(dwg_ref)
            dsc_ref[...] = jnp.zeros_like(dsc_ref)

        n = TOK + HALO
        inv = _inv_counts(i)
        dmv = dm_ref[...].astype(F32)
        dmn = dmn_ref[...].astype(F32)
        dmn = jnp.where(i == nt - 1, jnp.zeros_like(dmn), dmn)
        for g in range(4):
            cols = slice(g * POOL_GD, (g + 1) * POOL_GD)
            scg = sc_ref[:, cols]
            pg = p_ref[:, cols]
            dpre = (dmv[:, cols] * scg).astype(BF16)
            dpre_n = (dmn[:, cols] * scg).astype(BF16)
            pre = jnp.dot(pg, wg_ref[g], preferred_element_type=F32)
            dsc_ref[:, cols] += jnp.sum(dmv[:, cols] * pre, axis=0, keepdims=True)
            dwg_ref[g] += lax.dot_general(pg, dpre, (TN, ((), ())), preferred_element_type=F32)
            dpool = lax.dot_general(dpre, wg_ref[g], (NT, ((), ())), preferred_element_type=F32)
            dpool_n = lax.dot_general(dpre_n, wg_ref[g], (NT, ((), ())),
                                      preferred_element_type=F32)
            c0[0:TOK, cols] = dpool
            c0[TOK:n, cols] = dpool_n
            c1[0:TOK, cols] = dpool * inv[g][0:TOK]
            c1[TOK:n, cols] = dpool_n * inv[g][TOK:n]
        c2[0:n - 8, :] = c1[0:n - 8, :] + c1[1:n - 7, :]
        c3[0:n - 16, 128:] = c2[0:n - 16, 128:] + c2[2:n - 14, 128:]
        c1[0:n - 24, 256:] = c3[0:n - 24, 256:] + c3[4:n - 20, 256:]
        wins = [c2[0:TOK, 0:128], c3[0:TOK, 128:256], c1[0:TOK, 256:384],
                c1[0:TOK, 384:512] + c1[8:TOK + 8, 384:512]]
        for g in range(4):
            cols = slice(g * POOL_GD, (g + 1) * POOL_GD)
            du_ref[:, cols] = (wins[g] - c0[0:TOK, cols]).astype(BF16)

    buf = pltpu.VMEM((TOK + HALO, POOL_W), F32)
    outs, moved = _pcall(
        body, name, (nt,),
        [_row_spec(POOL_W),
         pl.BlockSpec((HALO, POOL_W), lambda i: (jnp.minimum((i + 1) * hb, nt * hb - 1), 0)),
         _row_spec(POOL_W), pl.BlockSpec((4, POOL_GD, POOL_GD), lambda i: (0, 0, 0)),
         _vec_spec(POOL_W)],
        [_row_spec(POOL_W), pl.BlockSpec((4, POOL_GD, POOL_GD), lambda i: (0, 0, 0)),
         _vec_spec(POOL_W)],
        [_sds((t, POOL_W), BF16), _sds((4, POOL_GD, POOL_GD), F32), _sds((1, POOL_W), F32)],
        [buf, buf, buf, buf], _cparams("arbitrary"), (dmixed, dmixed, pooled, wg, scale), comm)
    return outs if comm is None else (*outs, moved)


GELU_C = math.sqrt(2.0 / math.pi)


GELU_K = 0.044715


def _gelu_parts(x):
    x2 = x * x
    s = 0.5 + 0.5 * jnp.tanh(x * (GELU_C + (GELU_C * GELU_K) * x2))
    return x * s, s, x2


def _gelu(x):
    return _gelu_parts(x)[0]


def _gelu_and_grad(x):
    g, s, x2 = _gelu_parts(x)
    return g, s + g * (1.0 - s) * ((2 * GELU_C) + (6 * GELU_C * GELU_K) * x2)


def _taps(buf, r, rows):
    a = buf[pl.ds(r, rows + 8), :]
    return a[8:], pltpu.roll(a, 1, axis=0)[8:], pltpu.roll(a, 2, axis=0)[8:]


def _conv(taps, w_ref, b_ref):
    return b_ref[...] + w_ref[2:3, :] * taps[0] + w_ref[1:2, :] * taps[1] + w_ref[0:1, :] * taps[2]


def _stage(dst, prev_ref, cur_ref, next_ref, first, last):
    rows = cur_ref.shape[0]
    h = prev_ref[...].astype(F32)
    dst[0:8, :] = jnp.where(first, jnp.zeros_like(h), h)
    dst[8:8 + rows, :] = cur_ref[...].astype(F32)
    if next_ref is not None:
        h = next_ref[...].astype(F32)
        dst[8 + rows:, :] = jnp.where(last, jnp.zeros_like(h), h)


FWD_STRIP = 32
BWD_STRIP = 16


def _ffn_gate_fwd(name, hu, conv_w, conv_b, comm=None):
    t = hu.shape[0]
    ncol = D_FF // FF_COL
    hb = FF_TOK // 8

    def tile(off):
        return pl.BlockSpec((FF_TOK, FF_COL), lambda i, j: (i, j + off))

    def halo(off):
        return pl.BlockSpec((8, FF_COL), lambda i, j: (jnp.maximum(i * hb - 1, 0), j + off))

    def wspec(off):
        return pl.BlockSpec((3, FF_COL), lambda i, j: (0, j + off))

    def bspec(off):
        return pl.BlockSpec((1, FF_COL), lambda i, j: (0, j + off))

    def body(v_ref, vp_ref, g_ref, gp_ref, wv_ref, wg_ref, bv_ref, bg_ref, a_ref, hc_ref, vb, gb):
        first = pl.program_id(0) == 0
        _stage(vb, vp_ref, v_ref, None, first, None)
        _stage(gb, gp_ref, g_ref, None, first, None)

        def strip(k, carry):
            for u in range(2):
                r = pl.multiple_of((2 * k + u) * FWD_STRIP, FWD_STRIP)
                val = _conv(_taps(vb, r, FWD_STRIP), wv_ref, bv_ref)
                gate = _conv(_taps(gb, r, FWD_STRIP), wg_ref, bg_ref)
                a_ref[pl.ds(r, FWD_STRIP), :] = (_gelu(gate) * val).astype(BF16)
                hc_ref[0, pl.ds(r, FWD_STRIP), :] = val.astype(BF16)
                hc_ref[1, pl.ds(r, FWD_STRIP), :] = gate.astype(BF16)
            return carry

        lax.fori_loop(0, FF_TOK // (2 * FWD_STRIP), strip, 0)

    buf = pltpu.VMEM((FF_TOK + 8, FF_COL), F32)
    outs, moved = _pcall(
        body, name, (t // FF_TOK, ncol),
        [tile(0), halo(0), tile(ncol), halo(ncol), wspec(0), wspec(ncol), bspec(0), bspec(ncol)],
        [pl.BlockSpec((FF_TOK, FF_COL), lambda i, j: (i, j)),
         pl.BlockSpec((2, FF_TOK, FF_COL), lambda i, j: (0, i, j))],
        [_sds((t, D_FF), BF16), _sds((2, t, D_FF), BF16)], [buf, buf],
        _cparams("arbitrary", "arbitrary"),
        (hu, hu, hu, hu, conv_w, conv_w, conv_b, conv_b), comm)
    return outs if comm is None else (*outs, moved)


def _ffn_gate_bwd(name, da, hu, hc, conv_w, comm=None):
    t = hu.shape[0]
    nt = t // FF_TOK
    ncol = D_FF // FF_COL
    hb = FF_TOK // 8

    def tile(off):
        return pl.BlockSpec((FF_TOK, FF_COL), lambda j, i: (i, j + off))

    def nxt_rows(i):
        return jnp.minimum((i + 1) * hb, nt * hb - 1)

    def wspec(off):
        return pl.BlockSpec((3, FF_COL), lambda j, i: (0, j + off))

    def body(da_ref, dan_ref, v_ref, g_ref, hc_ref, hcn_ref, wv_ref, wg_ref,
             dh_ref, dwv_ref, dwg_ref):
        i = pl.program_id(1)
        first, last = i == 0, i == nt - 1

        @pl.when(first)
        def _():
            dwv_ref[...] = jnp.zeros_like(dwv_ref)
            dwg_ref[...] = jnp.zeros_like(dwg_ref)

        def grads(dav, val, gate):
            g, dg = _gelu_and_grad(gate.astype(F32))
            dav = dav.astype(F32)
            return dav * g, dav * val.astype(F32) * dg

        def fold(x):
            return x[0:8] + x[8:16]

        def strip(j, carry):
            for u in range(2):
                carry = one_strip(2 * j + u, carry)
            return carry

        def one_strip(k, carry):
            r = pl.multiple_of(FF_TOK - BWD_STRIP - k * BWD_STRIP, BWD_STRIP)
            rows = pl.ds(r, BWD_STRIP)
            dval, dgate = grads(da_ref[rows, :], hc_ref[0, rows, :], hc_ref[1, rows, :])
            new = (dval[0:8], dgate[0:8])
            for half, (d, below, h_ref, w_ref, dw_ref) in enumerate((
                    (dval, carry[0], v_ref, wv_ref, dwv_ref),
                    (dgate, carry[1], g_ref, wg_ref, dwg_ref))):
                e = jnp.concatenate([d, below], axis=0)
                e1 = pltpu.roll(e, BWD_STRIP + 7, axis=0)[0:BWD_STRIP]
                e2 = pltpu.roll(e, BWD_STRIP + 6, axis=0)[0:BWD_STRIP]
                dh = w_ref[2:3, :] * d + w_ref[1:2, :] * e1 + w_ref[0:1, :] * e2
                dh_ref[half, rows, :] = dh.astype(BF16)
                huv = h_ref[rows, :].astype(F32)
                dw_ref[0:8, :] += fold(e2 * huv)
                dw_ref[8:16, :] += fold(e1 * huv)
                dw_ref[16:24, :] += fold(d * huv)
                dw_ref[24:32, :] += fold(d)
            return new

        dan = dan_ref[...]
        dan = jnp.where(last, jnp.zeros_like(dan), dan)
        lax.fori_loop(0, FF_TOK // (2 * BWD_STRIP), strip, grads(dan, hcn_ref[0], hcn_ref[1]))

        @pl.when(last)
        def _():
            for dw_ref in (dwv_ref, dwg_ref):
                for q in range(4):
                    dw_ref[8 * q:8 * q + 1, :] = jnp.sum(dw_ref[8 * q:8 * q + 8, :], axis=0,
                                                         keepdims=True)

    acc = pl.BlockSpec((32, FF_COL), lambda j, i: (0, j))
    (dhu, dwv, dwg), moved = _pcall(
        body, name, (ncol, nt),
        [tile(0), pl.BlockSpec((8, FF_COL), lambda j, i: (nxt_rows(i), j)),
         tile(0), tile(ncol),
         pl.BlockSpec((2, FF_TOK, FF_COL), lambda j, i: (0, i, j)),
         pl.BlockSpec((2, 8, FF_COL), lambda j, i: (0, nxt_rows(i), j)),
         wspec(0), wspec(ncol)],
        [pl.BlockSpec((2, FF_TOK, FF_COL), lambda j, i: (0, i, j)), acc, acc],
        [_sds((2, t, D_FF), BF16), _sds((32, D_FF), F32), _sds((32, D_FF), F32)],
        [], _cparams("arbitrary", "arbitrary"),
        (da, da, hu, hu, hc, hc, conv_w, conv_w), comm)
    dconv = jnp.concatenate([dwv, dwg], axis=1).reshape(4, 8, 2 * D_FF)[:, 0]
    return (dhu, dconv) if comm is None else (dhu, dconv, moved)


def _mesh_pos():
    x, y, c = lax.axis_index("x"), lax.axis_index("y"), lax.axis_index("c")
    return x, y, c, [(1 - x, y), (x, 1 - y), (1 - x, 1 - y)]


def _remote(src, dst, send_sems, recv_sems, i, dev):
    return pltpu.make_async_remote_copy(src_ref=src, dst_ref=dst, send_sem=send_sems.at[i],
                                        recv_sem=recv_sems.at[i], device_id=dev,
                                        device_id_type=MESH)


def _mine(c, rows):
    return pl.ds(pl.multiple_of(c * (rows // 2), 16), rows // 2)


def _gather_send(shards, conv_shard, gathered, l):
    nbig = len(shards)
    with_conv = conv_shard is not None
    if gathered is None:
        ins = list(shards) + ([conv_shard] if with_conv else [])
        outs = [_sds((DEPTH, N_CHIPS) + s.shape[1:], s.dtype) for s in ins]
        alias = {}
    else:
        ins = list(shards) + list(gathered)
        outs = [_sds(g.shape, g.dtype) for g in gathered]
        alias = {nbig + k: k for k in range(nbig)}

    def copies(cin, cout, ssem, rsem):
        x, y, c, chips = _mesh_pos()
        me = 2 * x + y
        out = []
        for k in range(nbig):
            rows = shards[k].shape[1]
            for j, (cx, cy) in enumerate(chips):
                out.append(_remote(cin[k].at[l, _mine(c, rows)], cout[k].at[l, me, _mine(c, rows)],
                                   ssem, rsem, 4 * k + j, (cx, cy, c)))
            out.append(_remote(cin[k].at[l], cout[k].at[l, me], ssem, rsem, 4 * k + 3,
                               (x, y, 1 - c)))
        if with_conv:
            base = 4 * nbig
            for j, (cx, cy) in enumerate(chips):
                out.append(_remote(cin[nbig].at[c], cout[nbig].at[c, me], ssem, rsem, base + j,
                                   (cx, cy, c)))
            for ll in range(DEPTH):
                out.append(_remote(cin[nbig].at[ll], cout[nbig].at[ll, me], ssem, rsem,
                                   base + 3 + ll, (x, y, 1 - c)))
        return out

    return _Comm(ins, outs, copies, 4 * nbig + 5, alias)


def _gather_forward(gathered, nbig, rows, l):
    with_conv = len(gathered) > nbig
    alias = {k: k for k in range(len(gathered))}

    def copies(cin, cout, ssem, rsem):
        x, y, c, chips = _mesh_pos()
        out = []
        for k in range(nbig):
            for j, (cx, cy) in enumerate(chips):
                blk = cout[k].at[l, 2 * cx + cy, _mine(c, rows[k])]
                out.append(_remote(blk, blk, ssem, rsem, 3 * k + j, (x, y, 1 - c)))
        if with_conv:
            for j, (cx, cy) in enumerate(chips):
                blk = cout[nbig].at[c, 2 * cx + cy]
                out.append(_remote(blk, blk, ssem, rsem, 3 * nbig + j, (x, y, 1 - c)))
        return out

    return _Comm(gathered, [_sds(g.shape, g.dtype) for g in gathered], copies, 3 * nbig + 3, alias)


def _reduce_swap(grads, l):
    def copies(cin, cout, ssem, rsem):
        x, y, c, _ = _mesh_pos()
        return [_remote(cin[k].at[l, :, _mine(1 - c, g.shape[2])], cout[k], ssem, rsem, k,
                        (x, y, 1 - c)) for k, g in enumerate(grads)]

    outs = [_sds((N_CHIPS, g.shape[2] // 2, g.shape[3]), g.dtype) for g in grads]
    return _Comm(grads, outs, copies, len(grads))


def _reduce_scatter(sums):
    def copies(cin, cout, ssem, rsem):
        x, y, c, chips = _mesh_pos()
        return [_remote(cin[k].at[2 * cx + cy], cout[k].at[j], ssem, rsem, 3 * k + j, (cx, cy, c))
                for k in range(len(sums)) for j, (cx, cy) in enumerate(chips)]

    outs = [_sds((3,) + s.shape[1:], s.dtype) for s in sums]
    return _Comm(sums, outs, copies, 3 * len(sums))


def _reduce_share(reds, l):
    def copies(cin, cout, ssem, rsem):
        x, y, c, _ = _mesh_pos()
        out = []
        for k, r in enumerate(reds):
            half = cout[k].at[l, _mine(c, r.shape[1])]
            out.append(_remote(half, half, ssem, rsem, k, (x, y, 1 - c)))
        return out

    return _Comm(reds, [_sds(r.shape, r.dtype) for r in reds], copies, len(reds),
                 {k: k for k in range(len(reds))})


def _allreduce_small(per_layer):
    kinds = len(per_layer[0])
    shapes = [a.shape[1:] if a.shape[0] == 1 else a.shape for a in per_layer[0]]

    def body(*refs):
        ins = refs[:DEPTH * kinds]
        outs = refs[DEPTH * kinds:(DEPTH + 1) * kinds]
        gbufs = refs[(DEPTH + 1) * kinds:(DEPTH + 2) * kinds]
        send_sems, recv_sems = refs[-2], refs[-1]
        x, y, c, chips = _mesh_pos()
        sibling = (x, y, 1 - c)

        def copy(k, i, block, to):
            px, py, pc = block
            slot = gbufs[k].at[4 * px + 2 * py + pc]
            return _remote(slot, slot, send_sems, recv_sems, 7 * k + i, to)

        me = (x, y, c)
        first, passed = [], []
        for k in range(kinds):
            for l in range(DEPTH):
                a = ins[l * kinds + k]
                if per_layer[l][k].shape[0] == 1:
                    gbufs[k][4 * x + 2 * y + c, l:l + 1] = a[...]
                else:
                    gbufs[k][4 * x + 2 * y + c, l] = a[...]
            first.append(copy(k, 0, me, sibling))
            first += [copy(k, 1 + j, me, (*chip, c)) for j, chip in enumerate(chips)]
            passed += [copy(k, 4 + j, (*chip, c), sibling) for j, chip in enumerate(chips)]
        for cp in first:
            cp.start()
        for k in range(kinds):
            for j, chip in enumerate(chips):
                copy(k, 1 + j, (*chip, c), me).wait_recv()
                passed[3 * k + j].start()
        for k in range(kinds):
            copy(k, 0, sibling, me).wait_recv()
            for j, chip in enumerate(chips):
                copy(k, 4 + j, (*chip, 1 - c), me).wait_recv()
        for cp in first + passed:
            cp.wait_send()
        for k in range(kinds):
            acc = gbufs[k][0]
            for d in range(1, 8):
                acc = acc + gbufs[k][d]
            outs[k][...] = acc

    vmem = pl.BlockSpec(memory_space=pltpu.VMEM)
    return pl.pallas_call(
        body, name="allreduce_small",
        in_specs=[vmem] * (DEPTH * kinds), out_specs=[vmem] * kinds,
        out_shape=[_sds((DEPTH,) + s, F32) for s in shapes],
        scratch_shapes=[pltpu.VMEM((8, DEPTH) + s, F32) for s in shapes]
        + [pltpu.SemaphoreType.DMA((7 * kinds,)), pltpu.SemaphoreType.DMA((7 * kinds,))],
        compiler_params=pltpu.CompilerParams(vmem_limit_bytes=VMEM_LIMIT_V7X),
    )(*per_layer[0], *per_layer[1])


def _adamw_small(ws, gs, ms, vs):
    n = len(ws)
    c1 = 1.0 - ADAM_B1 ** ADAM_STEP
    c2 = 1.0 - ADAM_B2 ** ADAM_STEP

    def body(*refs):
        for i in range(n):
            w_ref, g_ref, m_ref, v_ref = (refs[j * n + i] for j in range(4))
            d_ref, nm_ref, nv_ref = (refs[(4 + j) * n + i] for j in range(3))
            gv = g_ref[...]
            nm = ADAM_B1 * m_ref[...] + (1.0 - ADAM_B1) * gv
            nv = ADAM_B2 * v_ref[...] + (1.0 - ADAM_B2) * (gv * gv)
            nm_ref[...] = nm
            nv_ref[...] = nv
            d_ref[...] = -ADAM_LR * ((nm / c1) / (jnp.sqrt(nv / c2) + ADAM_EPS)
                                     + ADAM_WD * w_ref[...])

    vmem = pl.BlockSpec(memory_space=pltpu.VMEM)
    outs = pl.pallas_call(
        body, name="adamw_small", in_specs=[vmem] * (4 * n), out_specs=[vmem] * (3 * n),
        out_shape=[_sds(w.shape, F32) for w in ws] * 3,
        compiler_params=pltpu.CompilerParams(vmem_limit_bytes=VMEM_LIMIT_V7X),
    )(*ws, *gs, *ms, *vs)
    return outs[:n], outs[n:2 * n], outs[2 * n:]


def _core_index():
    return jnp.reshape(lax.axis_index("c"), (1,)).astype(jnp.int32)


def _chip_index():
    return jnp.reshape(2 * lax.axis_index("x") + lax.axis_index("y"), (1,)).astype(jnp.int32)


def _chip_sums(name, stacked, sibs, l):
    n = len(stacked)
    dims = [(s.shape[2] // 2, s.shape[3]) for s in stacked]

    def body(c_ref, *refs):
        for k in range(n):
            a_ref, b_ref, o_ref = refs[k], refs[n + k], refs[2 * n + k]
            o_ref[...] = (a_ref[...].astype(F32) + b_ref[...].astype(F32)).astype(BF16)

    return pl.pallas_call(
        body, name=name,
        grid_spec=pltpu.PrefetchScalarGridSpec(
            num_scalar_prefetch=1, grid=(N_CHIPS,),
            in_specs=[pl.BlockSpec((None, None, hr, cd), lambda j, cr: (l, j, cr[0], 0))
                      for hr, cd in dims]
            + [pl.BlockSpec((None, hr, cd), lambda j, cr: (j, 0, 0)) for hr, cd in dims],
            out_specs=[pl.BlockSpec((None, hr, cd), lambda j, cr: (j, 0, 0)) for hr, cd in dims]),
        out_shape=[_sds((N_CHIPS, hr, cd), BF16) for hr, cd in dims],
        compiler_params=_cparams("parallel"))(_core_index(), *stacked, *sibs)


def _final_sums(name, sums, recvs, l, fills):
    n = len(sums)
    dims = [(s.shape[1] // 2, s.shape[2]) for s in sums]
    filled = fills[0] is not None

    def body(m_ref, *refs):
        outs = refs[-n:]
        for k in range(n):
            acc = refs[k][...].astype(F32)
            for j in range(3):
                acc = acc + refs[n + k][j].astype(F32)
            outs[k][...] = acc

    in_specs = ([pl.BlockSpec((None, tr, cd), lambda i, mr: (mr[0], i, 0)) for tr, cd in dims]
                + [pl.BlockSpec((3, tr, cd), lambda i, mr: (0, i, 0)) for tr, cd in dims])
    args = [jnp.concatenate([_chip_index(), _core_index()]), *sums, *recvs]
    aliases = {}
    if filled:
        in_specs += [pl.BlockSpec(memory_space=pl.ANY)] * n
        args += list(fills)
        aliases = {1 + 2 * n + k: k for k in range(n)}
    return pl.pallas_call(
        body, name=name,
        grid_spec=pltpu.PrefetchScalarGridSpec(
            num_scalar_prefetch=1, grid=(2,), in_specs=in_specs,
            out_specs=[pl.BlockSpec((None, tr, cd), lambda i, mr: (l, 2 * mr[1] + i, 0))
                       for tr, cd in dims]),
        out_shape=[_sds((DEPTH, 4 * tr, cd), F32) for tr, cd in dims],
        input_output_aliases=aliases,
        compiler_params=_cparams("parallel"))(*args)


def _adamw(name, w, g, m, v):
    nl, r, cdim = w.shape
    tr = r // 4 if r % 32 == 0 else r
    c1 = 1.0 - ADAM_B1 ** ADAM_STEP
    c2 = 1.0 - ADAM_B2 ** ADAM_STEP

    def body(w_ref, g_ref, m_ref, v_ref, d_ref, nm_ref, nv_ref, go_ref):
        gv = g_ref[...]
        go_ref[...] = gv
        nm = ADAM_B1 * m_ref[...] + (1.0 - ADAM_B1) * gv
        nv = ADAM_B2 * v_ref[...] + (1.0 - ADAM_B2) * (gv * gv)
        nm_ref[...] = nm
        nv_ref[...] = nv
        d_ref[...] = -ADAM_LR * ((nm / c1) / (jnp.sqrt(nv / c2) + ADAM_EPS) + ADAM_WD * w_ref[...])

    spec = pl.BlockSpec((None, tr, cdim), lambda l, i: (l, i, 0))
    out = _sds(w.shape, F32)
    return pl.pallas_call(
        body, name=name, grid=(nl, r // tr), in_specs=[spec] * 4, out_specs=[spec] * 4,
        out_shape=[out] * 4, compiler_params=_cparams("parallel", "parallel"))(w, g, m, v)


def kernel(x, norm_mix_pre, w_in, b_gate, rel_bias, w_attn_out, w_pool_group, pool_scale, w_pool_out, w_o, norm_mix_post, norm_ffn_pre, w_up, conv_w, conv_b, w_down, norm_ffn_post, loss_target, m_norm_mix_pre, m_w_in, m_b_gate, m_rel_bias, m_w_attn_out, m_w_pool_group, m_pool_scale, m_w_pool_out, m_w_o, m_norm_mix_post, m_norm_ffn_pre, m_w_up, m_conv_w, m_conv_b, m_w_down, m_norm_ffn_post, v_norm_mix_pre, v_w_in, v_b_gate, v_rel_bias, v_w_attn_out, v_w_pool_group, v_pool_scale, v_w_pool_out, v_w_o, v_norm_mix_post, v_norm_ffn_pre, v_w_up, v_conv_w, v_conv_b, v_w_down, v_norm_ffn_post):
    t = x.shape[1]
    xs = x.reshape(t, D_MODEL)
    target = loss_target.reshape(t, D_MODEL)

    names = ["w_in", "w_attn_out", "w_pool_out", "w_o", "w_up", "w_down"]
    shards = [w.astype(BF16) for w in (w_in, w_attn_out, w_pool_out, w_o, w_up, w_down)]
    rows = [s.shape[1] for s in shards]
    nbig = len(shards)
    h, g = _norm_fwd("l0_norm_mix_pre", x.reshape(t, D_MODEL), norm_mix_pre[0:1],
                     _gather_send(shards[:1], conv_w, None, 0))
    g = _comm_call("gather0_forward", _gather_forward(g, 1, rows[:1], 0))
    cw_full = jnp.transpose(g[1], (0, 2, 1, 3)).reshape(DEPTH, 3, 2 * D_FF)
    g = g[:1]
    wg_bf = w_pool_group.astype(BF16)

    def views(gathered):
        win_g, wao_g, wpo_g, wo_g, wup_g, wdn_g = gathered
        return (win_g, wao_g, wpo_g, wo_g.reshape(DEPTH, D_MODEL, D_MODEL), wup_g,
                wdn_g.reshape(DEPTH, D_FF, D_MODEL))

    saved = []
    xcur = xs
    for l in range(DEPTH):
        tag = f"l{l}_"
        bias = _bias_table(tag + "bias_table", rel_bias[l])
        proj = _mm_nn_blocked(tag + "proj", h, g[0], l, BF16)
        if l == 0:
            att, probs, rest = _attn_fwd(tag + "attn_fwd", proj, bias,
                                         _gather_send(shards[1:], None, None, 0))
            pooled, mixed, rest = _pool_fwd(tag + "pool_fwd", proj, wg_bf[l], pool_scale[l:l + 1],
                                            _gather_forward(rest, nbig - 1, rows[1:], 0))
            g = g + rest
        else:
            att, probs = _attn_fwd(tag + "attn_fwd", proj, bias)
            pooled, mixed = _pool_fwd(tag + "pool_fwd", proj, wg_bf[l], pool_scale[l:l + 1])
        win_g, wao_g, wpo_g, wo_full, wup_g, wdn_full = views(g)
        ya = _narrow_nn(tag + "attn_out", att, wao_g, l)
        yb = _narrow_nn(tag + "pool_out", mixed, wpo_g, l)
        z = _gate_fwd(tag + "gate_fwd", proj, b_gate[l:l + 1], ya, yb)
        mix = _mm_nn(tag + "mix", z, wo_full, l, D_MODEL, F32)
        x1, h2 = _post_pre_fwd(tag + "norm_mix_post", xcur, mix, norm_mix_post[l:l + 1],
                               norm_ffn_pre[l:l + 1])
        if l == 0:
            hu, mixing = _mm_nn_blocked(tag + "ffn_up", h2, wup_g, l, BF16,
                                        _gather_send(shards[:4], None, g[:4], 1))
            a, hc, ffn_g = _ffn_gate_fwd(tag + "ffn_gate_fwd", hu, cw_full[l], conv_b[l:l + 1],
                                         _gather_send(shards[4:], None, g[4:], 1))
            g = mixing + ffn_g
            wdn_full = views(g)[5]
        else:
            hu = _mm_nn_blocked(tag + "ffn_up", h2, wup_g, l, BF16)
            a, hc = _ffn_gate_fwd(tag + "ffn_gate_fwd", hu, cw_full[l], conv_b[l:l + 1])
        f = _mm_nn(tag + "ffn_down", a, wdn_full, l, D_FF, F32)
        saved.append(dict(x=xcur, h=h, proj=proj, att=att, pooled=pooled, mixed=mixed, ya=ya,
                          yb=yb, z=z, mix=mix, x1=x1, h2=h2, hu=hu, hc=hc, a=a, f=f, probs=probs))
        if l == 0:
            xcur, h, g = _post_pre_fwd(tag + "norm_ffn_post", x1, f, norm_ffn_post[l:l + 1],
                                       norm_mix_pre[l + 1:l + 2], _gather_forward(g, nbig, rows, 1))
        elif l < DEPTH - 1:
            xcur, h = _post_pre_fwd(tag + "norm_ffn_post", x1, f, norm_ffn_post[l:l + 1],
                                    norm_mix_pre[l + 1:l + 2])
    win_g, wao_g, wpo_g, wo_full, wup_g, wdn_full = views(g)

    dy, df, d_nfpost, loss_local = _tail("tail", saved[-1]["x1"], saved[-1]["f"],
                                         norm_ffn_post[DEPTH - 1:DEPTH], target)
    loss = lax.psum(loss_local, ("x", "y", "c"))

    dx = dy
    dws = dict.fromkeys(names)
    reds = [None] * nbig
    small_grads = [None] * DEPTH
    ffn = [4, 5]
    outs3 = [1, 2, 3]

    def blocks(ks):
        return [dws[names[k]].reshape(DEPTH, N_CHIPS, rows[k], -1) for k in ks]

    def chip_sums(ks, sib, l):
        return _chip_sums(f"chip_sums{l}_" + names[ks[0]], blocks(ks), sib, l)

    def final_sums(ks, sums, recv, l):
        outs = _final_sums(f"final_sums{l}_" + names[ks[0]], sums, recv, l, [reds[k] for k in ks])
        for k, r in zip(ks, outs):
            reds[k] = r

    for l in reversed(range(DEPTH)):
        tag = f"l{l}_"
        sv = saved[l]
        every = list(range(nbig))
        if l == 0:
            da, sib = _mm_nt(tag + "ffn_down_dx", df, wdn_full, l, D_FF // 2, BF16,
                             _reduce_swap(blocks(every), 1))
            sums = chip_sums(every, sib, 1)
        else:
            da = _mm_nt(tag + "ffn_down_dx", df, wdn_full, l, D_FF // 2, BF16)
        dws["w_down"] = _mm_tn(tag + "ffn_down_dw", sv["a"], df, D_FF // 2, l, dws["w_down"])
        if l == 0:
            dhu, dconv, recv = _ffn_gate_bwd(tag + "ffn_gate_bwd", da, sv["hu"], sv["hc"],
                                             cw_full[l], _reduce_scatter(sums))
            final_sums(every, sums, recv, 1)
            dh2, reds = _mm_nt_blocked(tag + "ffn_up_dx", dhu, wup_g, l, F32,
                                       _reduce_share(reds, 1))
        else:
            dhu, dconv = _ffn_gate_bwd(tag + "ffn_gate_bwd", da, sv["hu"], sv["hc"], cw_full[l])
            dh2 = _mm_nt_blocked(tag + "ffn_up_dx", dhu, wup_g, l, F32)
        dws["w_up"] = _mm_tn_blocked(tag + "ffn_up_dw", sv["h2"], dhu, l, dws["w_up"])
        if l == 0:
            dx1, d_nfpre, dmix, d_nmpost, sib = _pre_post_bwd(
                tag + "norm_ffn_pre_bwd", dh2, sv["x1"], dx, norm_ffn_pre[l:l + 1], sv["mix"],
                norm_mix_post[l:l + 1], _reduce_swap(blocks(ffn), 0))
            sums = chip_sums(ffn, sib, 0)
        else:
            dx1, d_nfpre, dmix, d_nmpost = _pre_post_bwd(
                tag + "norm_ffn_pre_bwd", dh2, sv["x1"], dx, norm_ffn_pre[l:l + 1], sv["mix"],
                norm_mix_post[l:l + 1])
        dz = _mm_nt(tag + "mix_dx", dmix, wo_full, l, D_MODEL, BF16)
        dws["w_o"] = _mm_tn(tag + "mix_dw", sv["z"], dmix, D_MODEL, l, dws["w_o"])
        dya, dyb, dgates, d_bgate = _gate_bwd(tag + "gate_bwd", dz, sv["proj"], b_gate[l:l + 1],
                                              sv["ya"], sv["yb"])
        datt = _narrow_nt(tag + "attn_out_dx", dya, wao_g, l)
        dws["w_attn_out"] = _narrow_tn(tag + "attn_out_dw", sv["att"], dya, l, dws["w_attn_out"])
        dmixed = _narrow_nt(tag + "pool_out_dx", dyb, wpo_g, l)
        dws["w_pool_out"] = _narrow_tn(tag + "pool_out_dw", sv["mixed"], dyb, l, dws["w_pool_out"])
        if l == 0:
            du, d_wg, d_pscale, sib = _pool_bwd(tag + "pool_bwd", dmixed, sv["pooled"], wg_bf[l],
                                                pool_scale[l:l + 1], _reduce_swap(blocks(outs3), 0))
            sums3 = chip_sums(outs3, sib, 0)
            dqkv, dbias, recv = _attn_bwd(
                tag + "attn_bwd", sv["proj"], datt, sv["probs"],
                _both(_reduce_scatter(sums), _reduce_scatter(sums3)))
            final_sums(ffn, sums, recv[:len(ffn)], 0)
            final_sums(outs3, sums3, recv[len(ffn):], 0)
        else:
            du, d_wg, d_pscale = _pool_bwd(tag + "pool_bwd", dmixed, sv["pooled"], wg_bf[l],
                                           pool_scale[l:l + 1])
            dqkv, dbias = _attn_bwd(tag + "attn_bwd", sv["proj"], datt, sv["probs"])
        d_rel = _bias_fold(tag + "bias_fold", dbias)
        if l == 0:
            dh, shared = _proj_dx(tag + "proj_dx", dqkv, du, dgates, win_g, l,
                                  _reduce_share([reds[k] for k in ffn + outs3], 0))
            for k, r in zip(ffn + outs3, shared):
                reds[k] = r
        else:
            dh = _proj_dx(tag + "proj_dx", dqkv, du, dgates, win_g, l)
        dws["w_in"] = _proj_dw(tag + "proj_dw", sv["h"], dqkv, du, dgates, l, dws["w_in"])
        small_grads[l] = [None, d_nmpost, d_nfpre, d_nfpost, d_bgate, d_rel, d_wg, d_pscale, dconv]
        if l > 0:
            dx, small_grads[l][0], df, d_nfpost = _pre_post_bwd(
                tag + "norm_mix_pre_bwd", dh, sv["x"], dx1, norm_mix_pre[l:l + 1],
                saved[l - 1]["f"], norm_ffn_post[l - 1:l])
        else:
            dx, small_grads[l][0] = _norm_pre_bwd(tag + "norm_mix_pre_bwd", dh, sv["x"], dx1,
                                                  norm_mix_pre[l:l + 1])

    grad_x = dx.reshape(x.shape)

    delta, new_m, new_v = {}, {}, {}
    sib = _comm_call("reduce_swap", _reduce_swap(blocks([0]), 0))
    sums = chip_sums([0], sib, 0)
    recv = _comm_call("reduce_scatter", _reduce_scatter(sums))
    final_sums([0], sums, recv, 0)
    g_big = _comm_call("reduce_share", _reduce_share([reds[0]], 0)) + reds[1:]

    (g_nmpre, g_nmpost, g_nfpre, g_nfpost, g_bgate, g_rel, g_wg, g_pscale,
     g_conv) = _allreduce_small(small_grads)
    g_rel = g_rel[:, :, :N_REL]
    g_cb = g_conv[:, 3]
    ncw = conv_w.shape[2]
    chip = 2 * lax.axis_index("x") + lax.axis_index("y")
    g_cw = lax.dynamic_slice_in_dim(g_conv[:, 0:3], chip * ncw, ncw, axis=2)

    grads = dict(norm_mix_pre=g_nmpre, w_in=g_big[0], b_gate=g_bgate, rel_bias=g_rel,
                 w_attn_out=g_big[1], w_pool_group=g_wg, pool_scale=g_pscale, w_pool_out=g_big[2],
                 w_o=g_big[3], norm_mix_post=g_nmpost, norm_ffn_pre=g_nfpre, w_up=g_big[4],
                 conv_w=g_cw, conv_b=g_cb, w_down=g_big[5], norm_ffn_post=g_nfpost)
    weights = dict(norm_mix_pre=norm_mix_pre, w_in=w_in, b_gate=b_gate, rel_bias=rel_bias,
                   w_attn_out=w_attn_out, w_pool_group=w_pool_group, pool_scale=pool_scale,
                   w_pool_out=w_pool_out, w_o=w_o, norm_mix_post=norm_mix_post,
                   norm_ffn_pre=norm_ffn_pre, w_up=w_up, conv_w=conv_w, conv_b=conv_b,
                   w_down=w_down, norm_ffn_post=norm_ffn_post)
    moms = dict(norm_mix_pre=(m_norm_mix_pre, v_norm_mix_pre), w_in=(m_w_in, v_w_in),
                b_gate=(m_b_gate, v_b_gate), rel_bias=(m_rel_bias, v_rel_bias),
                w_attn_out=(m_w_attn_out, v_w_attn_out),
                w_pool_group=(m_w_pool_group, v_w_pool_group),
                pool_scale=(m_pool_scale, v_pool_scale), w_pool_out=(m_w_pool_out, v_w_pool_out),
                w_o=(m_w_o, v_w_o), norm_mix_post=(m_norm_mix_post, v_norm_mix_post),
                norm_ffn_pre=(m_norm_ffn_pre, v_norm_ffn_pre), w_up=(m_w_up, v_w_up),
                conv_w=(m_conv_w, v_conv_w), conv_b=(m_conv_b, v_conv_b),
                w_down=(m_w_down, v_w_down), norm_ffn_post=(m_norm_ffn_post, v_norm_ffn_post))
    order = list(weights.keys())

    small_names = [nm for nm in order if nm not in names]
    for nm in names:
        delta[nm], new_m[nm], new_v[nm], grads[nm] = _adamw("adamw_" + nm, weights[nm], grads[nm],
                                                            *moms[nm])
    d_s, m_s, v_s = _adamw_small([weights[nm] for nm in small_names],
                                 [grads[nm] for nm in small_names],
                                 [moms[nm][0] for nm in small_names],
                                 [moms[nm][1] for nm in small_names])
    for i, nm in enumerate(small_names):
        delta[nm], new_m[nm], new_v[nm] = d_s[i], m_s[i], v_s[i]

    return (loss, grad_x, *[grads[nm] for nm in order], *[delta[nm] for nm in order],
            *[new_m[nm] for nm in order], *[new_v[nm] for nm in order])
```

```python
import functools
import math

import jax
import jax.numpy as jnp
from jax import lax
from jax.experimental import pallas as pl
from jax.experimental.pallas import tpu as pltpu

F32 = jnp.float32
BF16 = jnp.bfloat16
MESH = pl.DeviceIdType.MESH

D_MODEL = 1024
DEPTH = 2
CHUNK = 64
BAND_CHUNKS = 9
BAND = BAND_CHUNKS * CHUNK
HEADS = 8
HEAD_DIM = 64
ATTN_W = HEADS * HEAD_DIM
POOL_WINDOWS = (2, 4, 8, 16)
POOL_W = 512
POOL_GD = 128
MAX_REL = 256
N_REL = 2 * MAX_REL + 1
D_FF = 2816
IN_W = 3 * ATTN_W + POOL_W + 2 * D_MODEL
EPS = 1e-6
ATTN_SCALE = HEAD_DIM ** -0.5
BAND_PAD = 640
BIAS_LANES = BAND_PAD
N_CHIPS = 4

ADAM_LR = 0.001
ADAM_B1 = 0.9
ADAM_B2 = 0.999
ADAM_EPS = 1e-08
ADAM_WD = 0.01
ADAM_STEP = 10

VMEM_LIMIT_V7X = 56 * 1024 * 1024
TOK = 512
ATT_BLK = 8 * CHUNK
FF_COL = 256
FF_TOK = 1024
HALO = 32


def _cparams(*sem):
    return pltpu.CompilerParams(dimension_semantics=sem, vmem_limit_bytes=VMEM_LIMIT_V7X)


def _sds(shape, dtype):
    return jax.ShapeDtypeStruct(shape, dtype)


class _Comm:
    def __init__(self, ins, outs, copies, n_sems, alias=None):
        self.ins, self.outs, self.copies, self.n_sems = list(ins), list(outs), copies, n_sems
        self.alias = dict(alias or {})


class _SemsFrom:
    def __init__(self, sems, start):
        self.sems, self.start = sems, start

    @property
    def at(self):
        return self

    def __getitem__(self, i):
        return self.sems.at[self.start + i]


def _both(a, b):
    na, nao = len(a.ins), len(a.outs)

    def copies(cin, cout, ssem, rsem):
        return (a.copies(cin[:na], cout[:nao], ssem, rsem)
                + b.copies(cin[na:], cout[nao:], _SemsFrom(ssem, a.n_sems), _SemsFrom(rsem, a.n_sems)))

    alias = dict(a.alias)
    alias.update({na + i: nao + o for i, o in b.alias.items()})
    return _Comm(a.ins + b.ins, a.outs + b.outs, copies, a.n_sems + b.n_sems, alias)


def _pcall(body, name, grid, in_specs, out_specs, out_shape, scratch_shapes, compiler_params, args,
           comm=None, aliases=None):
    single = not isinstance(out_shape, (list, tuple))
    out_specs = [out_specs] if single else list(out_specs)
    out_shape = [out_shape] if single else list(out_shape)
    n_in, n_out = len(in_specs), len(out_specs)
    aliases = dict(aliases or {})
    if comm is None:
        res = pl.pallas_call(
            body, name=name, grid=grid, in_specs=list(in_specs), out_specs=out_specs,
            out_shape=out_shape, scratch_shapes=list(scratch_shapes),
            input_output_aliases=aliases, compiler_params=compiler_params)(*args)
        return (res[0] if single else res), None
    ci, co = len(comm.ins), len(comm.outs)

    def hosted(*refs):
        main_in, cin = refs[:n_in], refs[n_in:n_in + ci]
        main_out = refs[n_in + ci:n_in + ci + n_out]
        cout = refs[n_in + ci + n_out:n_in + ci + n_out + co]
        rest = refs[n_in + ci + n_out + co:]
        copies = comm.copies(cin, cout, rest[-2], rest[-1])
        ids = [pl.program_id(a) for a in range(len(grid))]
        first = functools.reduce(jnp.logical_and, [i == 0 for i in ids])
        last = functools.reduce(jnp.logical_and, [i == g - 1 for i, g in zip(ids, grid)])

        @pl.when(first)
        def _():
            for cp in copies:
                cp.start()

        body(*main_in, *main_out, *rest[:-2])

        @pl.when(last)
        def _():
            for cp in copies:
                cp.wait()

    for i, o in comm.alias.items():
        aliases[n_in + i] = n_out + o
    hbm = pl.BlockSpec(memory_space=pl.ANY)
    sems = pltpu.SemaphoreType.DMA((comm.n_sems,))
    res = pl.pallas_call(
        hosted, name=name, grid=grid, in_specs=list(in_specs) + [hbm] * ci,
        out_specs=out_specs + [hbm] * co, out_shape=out_shape + comm.outs,
        scratch_shapes=list(scratch_shapes) + [sems, sems],
        input_output_aliases=aliases, compiler_params=compiler_params)(*args, *comm.ins)
    return (res[0] if single else list(res[:n_out])), list(res[n_out:])


def _comm_call(name, comm):
    ci = len(comm.ins)

    def body(*refs):
        copies = comm.copies(refs[:ci], refs[ci:-2], refs[-2], refs[-1])
        for cp in copies:
            cp.start()
        for cp in copies:
            cp.wait()

    hbm = pl.BlockSpec(memory_space=pl.ANY)
    sems = pltpu.SemaphoreType.DMA((comm.n_sems,))
    return list(pl.pallas_call(
        body, name=name, in_specs=[hbm] * ci, out_specs=[hbm] * len(comm.outs),
        out_shape=comm.outs, scratch_shapes=[sems, sems],
        input_output_aliases=comm.alias)(*comm.ins))


def _matmul(name, a, b, a_spec, b_spec, o_spec, out_shape, grid, contract, nk, acc_shape,
            fill=None, comm=None):
    in_place = out_shape.dtype == F32

    def body(*refs):
        a_ref, b_ref = refs[0], refs[1]
        o_ref = refs[2 if fill is None else 3]
        scratch = refs[(3 if fill is None else 4):]
        part = lax.dot_general(a_ref[...], b_ref[...], (contract, ((), ())),
                               preferred_element_type=F32)
        if nk == 1:
            o_ref[...] = part.astype(o_ref.dtype)
        else:
            acc_ref = o_ref if in_place else scratch[0]
            k = pl.program_id(2)

            @pl.when(k == 0)
            def _():
                acc_ref[...] = part

            @pl.when(k > 0)
            def _():
                acc_ref[...] += part

            if not in_place:
                @pl.when(k == nk - 1)
                def _():
                    o_ref[...] = acc_ref[...].astype(o_ref.dtype)

    scratch = [] if nk == 1 or in_place else [pltpu.VMEM(acc_shape, F32)]
    in_specs, args, aliases = [a_spec, b_spec], [a, b], {}
    if fill is not None:
        in_specs.append(pl.BlockSpec(memory_space=pl.ANY))
        args.append(fill)
        aliases = {2: 0}
    out, moved = _pcall(body, name, grid, in_specs, o_spec, out_shape, scratch,
                        _cparams("parallel", "parallel", "arbitrary"), args, comm, aliases)
    return out if comm is None else (out, moved)


NN = ((1,), (0,))
NT = ((1,), (1,))
TN = ((0,), (0,))


def _tm(t):
    return min(t, 1024)


def _tt(t):
    return min(t, 2048)


def _col_block_spec(a, rows, nb, row_col):
    if a.ndim == 2:
        return pl.BlockSpec((rows, nb), row_col)

    def halves(*ids):
        r, c = row_col(*ids)
        return c // 2, r, c % 2

    return pl.BlockSpec((None, rows, nb), halves)


def _mm_nn_blocked(name, a, w, l, out_dtype, comm=None):
    t, k = a.shape
    nb = w.shape[3]
    tm = _tm(t)
    return _matmul(
        name, a, w,
        pl.BlockSpec((tm, k), lambda i, n, kk: (i, 0)),
        pl.BlockSpec((None, None, k, nb), lambda i, n, kk: (l, n, 0, 0)),
        pl.BlockSpec((tm, nb), lambda i, n, kk: (i, n)),
        _sds((t, N_CHIPS * nb), out_dtype), (t // tm, N_CHIPS, 1), NN, 1, None, comm=comm)


def _mm_nt_blocked(name, a, w, l, out_dtype, comm=None):
    t = a.shape[-2]
    k, nb = w.shape[2], w.shape[3]
    tm = _tm(t)
    return _matmul(
        name, a, w,
        _col_block_spec(a, tm, nb, lambda i, n, kk: (i, kk)),
        pl.BlockSpec((None, None, k, nb), lambda i, n, kk: (l, kk, 0, 0)),
        pl.BlockSpec((tm, k), lambda i, n, kk: (i, 0)),
        _sds((t, k), out_dtype), (t // tm, 1, N_CHIPS), NT, N_CHIPS, (tm, k), comm=comm)


def _mm_tn_blocked(name, a, g, l, fill):
    t, k = a.shape
    nb = g.shape[-1] * (g.ndim - 1) // N_CHIPS
    tt = _tt(t)
    nt = t // tt
    return _matmul(
        name, a, g,
        pl.BlockSpec((tt, k), lambda n, j, kk: (kk, 0)),
        _col_block_spec(g, tt, nb, lambda n, j, kk: (kk, n)),
        pl.BlockSpec((None, None, k, nb), lambda n, j, kk: (l, n, 0, 0)),
        _sds((DEPTH, N_CHIPS, k, nb), BF16), (N_CHIPS, 1, nt), TN, nt, (k, nb), fill)


def _proj_pieces(rows, dqkv_first):
    def piece(col):
        if dqkv_first:
            return pl.BlockSpec((rows, ATTN_W), lambda i, kk: (i, col))
        return pl.BlockSpec((rows, ATTN_W), lambda n, kk: (kk, col))
    return [piece(0), piece(1), piece(2), piece(0)]


def _proj_dx(name, dqkv, du, dgates, w, l, comm=None):
    t = du.shape[0]
    k, nb = w.shape[2], w.shape[3]
    tm = _tm(t)

    def body(dq_ref, dk_ref, dv_ref, du_ref, dg_ref, w_ref, o_ref):
        kk = pl.program_id(1)

        def mm(a):
            return lax.dot_general(a, w_ref[...], (NT, ((), ())), preferred_element_type=F32)

        @pl.when(kk == 0)
        def _():
            o_ref[...] = mm(jnp.concatenate([dq_ref[...], dk_ref[...]], axis=1))

        @pl.when(kk == 1)
        def _():
            o_ref[...] += mm(jnp.concatenate([dv_ref[...], du_ref[...]], axis=1))

        @pl.when(kk >= 2)
        def _():
            o_ref[...] += mm(dg_ref[...])

    out, moved = _pcall(
        body, name, (t // tm, N_CHIPS),
        _proj_pieces(tm, True)
        + [pl.BlockSpec((tm, nb), lambda i, kk: (i, jnp.maximum(kk - 2, 0))),
           pl.BlockSpec((None, None, k, nb), lambda i, kk: (l, kk, 0, 0))],
        pl.BlockSpec((tm, k), lambda i, kk: (i, 0)), _sds((t, k), F32),
        [], _cparams("arbitrary", "arbitrary"),
        (dqkv, dqkv, dqkv, du, dgates, w), comm)
    return out if comm is None else (out, moved)


def _proj_dw(name, h, dqkv, du, dgates, l, fill):
    t, k = h.shape
    nb = dgates.shape[1] // 2
    tt = _tm(t)
    nt = t // tt

    def body(*refs):
        h_ref, dq_ref, dk_ref, dv_ref, du_ref, dg_ref = refs[:6]
        o_ref, acc_ref = refs[-2], refs[-1]
        n, kk = pl.program_id(0), pl.program_id(1)

        def update(g):
            part = lax.dot_general(h_ref[...], g, (TN, ((), ())), preferred_element_type=F32)

            @pl.when(kk == 0)
            def _():
                acc_ref[...] = part

            @pl.when(kk > 0)
            def _():
                acc_ref[...] += part

        @pl.when(n == 0)
        def _():
            update(jnp.concatenate([dq_ref[...], dk_ref[...]], axis=1))

        @pl.when(n == 1)
        def _():
            update(jnp.concatenate([dv_ref[...], du_ref[...]], axis=1))

        @pl.when(n >= 2)
        def _():
            update(dg_ref[...])

        @pl.when(kk == nt - 1)
        def _():
            o_ref[...] = acc_ref[...].astype(BF16)

    in_specs = ([pl.BlockSpec((tt, k), lambda n, kk: (kk, 0))] + _proj_pieces(tt, False)
                + [pl.BlockSpec((tt, nb), lambda n, kk: (kk, jnp.maximum(n - 2, 0)))])
    args, aliases = [h, dqkv, dqkv, dqkv, du, dgates], {}
    if fill is not None:
        in_specs.append(pl.BlockSpec(memory_space=pl.ANY))
        args.append(fill)
        aliases = {6: 0}
    return pl.pallas_call(
        body, name=name, grid=(N_CHIPS, nt), in_specs=in_specs,
        out_specs=pl.BlockSpec((None, None, k, nb), lambda n, kk: (l, n, 0, 0)),
        out_shape=_sds((DEPTH, N_CHIPS, k, nb), BF16),
        scratch_shapes=[pltpu.VMEM((k, nb), F32)], input_output_aliases=aliases,
        compiler_params=_cparams("parallel", "arbitrary"))(*args)


def _narrow_nn(name, a, w, l):
    t, k = a.shape
    nb = w.shape[3]
    tm = _tm(t)

    def body(a_ref, w_ref, o_ref):
        av = a_ref[...]
        for j in range(N_CHIPS):
            o_ref[:, j * nb:(j + 1) * nb] = jnp.dot(
                av, w_ref[j], preferred_element_type=F32).astype(BF16)

    return pl.pallas_call(
        body, name=name, grid=(t // tm,),
        in_specs=[pl.BlockSpec((tm, k), lambda i: (i, 0)),
                  pl.BlockSpec((None, N_CHIPS, k, nb), lambda i: (l, 0, 0, 0))],
        out_specs=pl.BlockSpec((tm, N_CHIPS * nb), lambda i: (i, 0)),
        out_shape=_sds((t, N_CHIPS * nb), BF16), compiler_params=_cparams("parallel"))(a, w)


def _narrow_nt(name, a, w, l):
    t = a.shape[0]
    k, nb = w.shape[2], w.shape[3]
    tm = _tm(t)

    def body(a_ref, w_ref, o_ref):
        acc = lax.dot_general(a_ref[:, 0:nb], w_ref[0], (NT, ((), ())), preferred_element_type=F32)
        for j in range(1, N_CHIPS):
            acc = acc + lax.dot_general(a_ref[:, j * nb:(j + 1) * nb], w_ref[j], (NT, ((), ())),
                                        preferred_element_type=F32)
        o_ref[...] = acc.astype(BF16)

    return pl.pallas_call(
        body, name=name, grid=(t // tm,),
        in_specs=[pl.BlockSpec((tm, N_CHIPS * nb), lambda i: (i, 0)),
                  pl.BlockSpec((None, N_CHIPS, k, nb), lambda i: (l, 0, 0, 0))],
        out_specs=pl.BlockSpec((tm, k), lambda i: (i, 0)),
        out_shape=_sds((t, k), BF16), compiler_params=_cparams("parallel"))(a, w)


def _narrow_tn(name, a, g, l, fill):
    t, k = a.shape
    nb = g.shape[1] // N_CHIPS
    tt = _tm(t)
    nt = t // tt

    def body(*refs):
        a_ref, g_ref, o_ref, acc_ref = refs[0], refs[1], refs[-2], refs[-1]
        i = pl.program_id(0)
        part = lax.dot_general(a_ref[...], g_ref[...], (TN, ((), ())), preferred_element_type=F32)

        @pl.when(i == 0)
        def _():
            acc_ref[...] = part

        @pl.when(i > 0)
        def _():
            acc_ref[...] += part

        @pl.when(i == nt - 1)
        def _():
            for j in range(N_CHIPS):
                o_ref[j] = acc_ref[:, j * nb:(j + 1) * nb].astype(BF16)

    in_specs = [pl.BlockSpec((tt, k), lambda i: (i, 0)),
                pl.BlockSpec((tt, N_CHIPS * nb), lambda i: (i, 0))]
    args, aliases = [a, g], {}
    if fill is not None:
        in_specs.append(pl.BlockSpec(memory_space=pl.ANY))
        args.append(fill)
        aliases = {2: 0}
    return pl.pallas_call(
        body, name=name, grid=(nt,), in_specs=in_specs,
        out_specs=pl.BlockSpec((None, N_CHIPS, k, nb), lambda i: (l, 0, 0, 0)),
        out_shape=_sds((DEPTH, N_CHIPS, k, nb), BF16),
        scratch_shapes=[pltpu.VMEM((k, N_CHIPS * nb), F32)], input_output_aliases=aliases,
        compiler_params=_cparams("arbitrary"))(*args)


def _mm_nn(name, a, w, l, tk, out_dtype):
    t, k = a.shape
    n = w.shape[2]
    tm = _tm(t)
    nk = k // tk
    return _matmul(
        name, a, w,
        pl.BlockSpec((tm, tk), lambda i, j, kk: (i, kk)),
        pl.BlockSpec((None, tk, n), lambda i, j, kk: (l, kk, 0)),
        pl.BlockSpec((tm, n), lambda i, j, kk: (i, 0)),
        _sds((t, n), out_dtype), (t // tm, 1, nk), NN, nk, (tm, n))


def _mm_nt(name, a, w, l, tn, out_dtype, comm=None):
    t, n = a.shape
    k = w.shape[1]
    tm = _tm(t)
    return _matmul(
        name, a, w,
        pl.BlockSpec((tm, n), lambda i, j, kk: (i, 0)),
        pl.BlockSpec((None, tn, n), lambda i, j, kk: (l, j, 0)),
        pl.BlockSpec((tm, tn), lambda i, j, kk: (i, j)),
        _sds((t, k), out_dtype), (t // tm, k // tn, 1), NT, 1, None, comm=comm)


def _mm_tn(name, a, g, tko, l, fill):
    t, k = a.shape
    n = g.shape[1]
    tt = _tt(t)
    nt = t // tt
    return _matmul(
        name, a, g,
        pl.BlockSpec((tt, tko), lambda i, j, kk: (kk, i)),
        pl.BlockSpec((tt, n), lambda i, j, kk: (kk, 0)),
        pl.BlockSpec((None, tko, n), lambda i, j, kk: (l, i, 0)),
        _sds((DEPTH, k, n), BF16), (k // tko, 1, nt), TN, nt, (tko, n), fill)


def _row_spec(width, col=0):
    return pl.BlockSpec((TOK, width), lambda i: (i, col))


def _vec_spec(width):
    return pl.BlockSpec((1, width), lambda i: (0, 0))


def _rms(x):
    return lax.rsqrt(jnp.mean(x * x, axis=-1, keepdims=True) + EPS)


def _norm_fwd(name, x, g, comm=None):
    t = x.shape[0]

    def body(x_ref, g_ref, h_ref):
        xv = x_ref[...]
        h_ref[...] = (xv * _rms(xv) * g_ref[...]).astype(BF16)

    out, moved = _pcall(body, name, (t // TOK,), [_row_spec(D_MODEL), _vec_spec(D_MODEL)],
                        _row_spec(D_MODEL), _sds((t, D_MODEL), BF16), [], _cparams("arbitrary"),
                        (x, g), comm)
    return out if comm is None else (out, moved)


ROWS = 16
ROW_UNROLL = 8


def _rows(k):
    return pl.ds(pl.multiple_of(k * ROWS, ROWS), ROWS)


def _strips(step, init):
    def group(j, carry):
        for u in range(ROW_UNROLL):
            carry = step(j * ROW_UNROLL + u, carry)
        return carry

    return lax.fori_loop(0, TOK // (ROWS * ROW_UNROLL), group, init)


def _fold_rows(x):
    return x[0:8] + x[8:16]


def _accumulate(ref, part):
    total = jnp.sum(part, axis=0, keepdims=True)

    @pl.when(pl.program_id(0) == 0)
    def _():
        ref[...] = total

    @pl.when(pl.program_id(0) > 0)
    def _():
        ref[...] += total


def _norm_bwd_rows(d, mv, g):
    r = _rms(mv)
    n = mv * r
    dn = d * g
    return r * (dn - n * jnp.mean(dn * n, axis=-1, keepdims=True)), d * n


def _post_pre_fwd(name, xres, m, g_post, g_pre, comm=None):
    t = xres.shape[0]

    def body(x_ref, m_ref, gp_ref, gn_ref, x1_ref, h_ref):
        def strip(k, c):
            rows = _rows(k)
            mv = m_ref[rows, :]
            x1 = x_ref[rows, :] + mv * _rms(mv) * gp_ref[...]
            x1_ref[rows, :] = x1
            h_ref[rows, :] = (x1 * _rms(x1) * gn_ref[...]).astype(BF16)
            return c

        _strips(strip, 0)

    outs, moved = _pcall(
        body, name, (t // TOK,),
        [_row_spec(D_MODEL), _row_spec(D_MODEL), _vec_spec(D_MODEL), _vec_spec(D_MODEL)],
        [_row_spec(D_MODEL), _row_spec(D_MODEL)],
        [_sds((t, D_MODEL), F32), _sds((t, D_MODEL), BF16)], [], _cparams("arbitrary"),
        (xres, m, g_post, g_pre), comm)
    return outs if comm is None else (*outs, moved)


def _tail(name, xres, m, g_post, target):
    t = xres.shape[0]

    def body(x_ref, m_ref, g_ref, t_ref, dy_ref, dm_ref, dg_ref, l_ref):
        def strip(k, carry):
            rows = _rows(k)
            mv = m_ref[rows, :]
            e = x_ref[rows, :] + mv * _rms(mv) * g_ref[...] - t_ref[rows, :]
            dy = e * (1.0 / D_MODEL)
            dy_ref[rows, :] = dy
            dm, dgn = _norm_bwd_rows(dy, mv, g_ref[...])
            dm_ref[rows, :] = dm.astype(BF16)
            return carry[0] + _fold_rows(dgn), carry[1] + _fold_rows(e * e)

        zero = jnp.zeros((8, D_MODEL), F32)
        dg, sq = _strips(strip, (zero, zero))
        _accumulate(dg_ref, dg)
        _accumulate(l_ref, jnp.sum(sq, axis=1, keepdims=True))

    dy, dm, dg, sq = pl.pallas_call(
        body, name=name, grid=(t // TOK,),
        in_specs=[_row_spec(D_MODEL), _row_spec(D_MODEL), _vec_spec(D_MODEL), _row_spec(D_MODEL)],
        out_specs=[_row_spec(D_MODEL), _row_spec(D_MODEL), _vec_spec(D_MODEL),
                   pl.BlockSpec((1, 1), lambda i: (0, 0))],
        out_shape=[_sds((t, D_MODEL), F32), _sds((t, D_MODEL), BF16), _sds((1, D_MODEL), F32),
                   _sds((1, 1), F32)],
        compiler_params=_cparams("arbitrary"))(xres, m, g_post, target)
    return dy, dm, dg, sq[0, 0] * (0.5 / D_MODEL)


def _pre_post_bwd(name, dh, xin, dxo, g_pre, m, g_post, comm=None):
    t = dh.shape[0]

    def body(dh_ref, x_ref, d_ref, gq_ref, m_ref, gp_ref, dx_ref, dgq_ref, dm_ref, dgp_ref):
        def strip(k, carry):
            rows = _rows(k)
            dxin, dgq = _norm_bwd_rows(dh_ref[rows, :], x_ref[rows, :], gq_ref[...])
            dx = d_ref[rows, :] + dxin
            dx_ref[rows, :] = dx
            dm, dgp = _norm_bwd_rows(dx, m_ref[rows, :], gp_ref[...])
            dm_ref[rows, :] = dm.astype(BF16)
            return carry[0] + _fold_rows(dgq), carry[1] + _fold_rows(dgp)

        zero = jnp.zeros((8, D_MODEL), F32)
        dgq, dgp = _strips(strip, (zero, zero))
        _accumulate(dgq_ref, dgq)
        _accumulate(dgp_ref, dgp)

    outs, moved = _pcall(
        body, name, (t // TOK,),
        [_row_spec(D_MODEL), _row_spec(D_MODEL), _row_spec(D_MODEL), _vec_spec(D_MODEL),
         _row_spec(D_MODEL), _vec_spec(D_MODEL)],
        [_row_spec(D_MODEL), _vec_spec(D_MODEL), _row_spec(D_MODEL), _vec_spec(D_MODEL)],
        [_sds((t, D_MODEL), F32), _sds((1, D_MODEL), F32), _sds((t, D_MODEL), BF16),
         _sds((1, D_MODEL), F32)], [], _cparams("arbitrary"),
        (dh, xin, dxo, g_pre, m, g_post), comm)
    return outs if comm is None else (*outs, moved)


def _norm_pre_bwd(name, dh, xin, dxo, g, comm=None):
    t = dh.shape[0]

    def body(dh_ref, x_ref, d_ref, g_ref, dx_ref, dg_ref):
        xv = x_ref[...]
        dhv = dh_ref[...]
        r = _rms(xv)
        n = xv * r
        dn = dhv * g_ref[...]
        dx_ref[...] = d_ref[...] + r * (dn - n * jnp.mean(dn * n, axis=-1, keepdims=True))
        part = jnp.sum(dhv * n, axis=0, keepdims=True)

        @pl.when(pl.program_id(0) == 0)
        def _():
            dg_ref[...] = part

        @pl.when(pl.program_id(0) > 0)
        def _():
            dg_ref[...] += part

    out, moved = _pcall(
        body, name, (t // TOK,),
        [_row_spec(D_MODEL), _row_spec(D_MODEL), _row_spec(D_MODEL), _vec_spec(D_MODEL)],
        [_row_spec(D_MODEL), _vec_spec(D_MODEL)],
        [_sds((t, D_MODEL), F32), _sds((1, D_MODEL), F32)], [], _cparams("arbitrary"),
        (dh, xin, dxo, g), comm)
    return out if comm is None else (*out, moved)


def _gate_fwd(name, proj, b_gate, ya, yb):
    t = proj.shape[0]

    def body(ga_ref, gb_ref, b_ref, ya_ref, yb_ref, z_ref):
        def strip(k, c):
            rows = _rows(k)
            sa = jax.nn.sigmoid(ga_ref[rows, :].astype(F32) + b_ref[:, :D_MODEL])
            sb = jax.nn.sigmoid(gb_ref[rows, :].astype(F32) + b_ref[:, D_MODEL:])
            z_ref[rows, :] = (sa * ya_ref[rows, :].astype(F32)
                              + sb * yb_ref[rows, :].astype(F32)).astype(BF16)
            return c

        _strips(strip, 0)

    return pl.pallas_call(
        body, name=name, grid=(t // TOK,),
        in_specs=[_row_spec(D_MODEL, 2), _row_spec(D_MODEL, 3), _vec_spec(2 * D_MODEL),
                  _row_spec(D_MODEL), _row_spec(D_MODEL)],
        out_specs=_row_spec(D_MODEL), out_shape=_sds((t, D_MODEL), BF16),
        compiler_params=_cparams("parallel"))(proj, proj, b_gate, ya, yb)


def _gate_bwd(name, dz, proj, b_gate, ya, yb):
    t = proj.shape[0]

    def body(dz_ref, ga_ref, gb_ref, b_ref, ya_ref, yb_ref, dya_ref, dyb_ref, dg_ref, db_ref):
        def strip(k, carry):
            rows = _rows(k)
            dzv = dz_ref[rows, :].astype(F32)
            sa = jax.nn.sigmoid(ga_ref[rows, :].astype(F32) + b_ref[:, :D_MODEL])
            sb = jax.nn.sigmoid(gb_ref[rows, :].astype(F32) + b_ref[:, D_MODEL:])
            dya_ref[rows, :] = (dzv * sa).astype(BF16)
            dyb_ref[rows, :] = (dzv * sb).astype(BF16)
            dga = dzv * ya_ref[rows, :].astype(F32) * sa * (1.0 - sa)
            dgb = dzv * yb_ref[rows, :].astype(F32) * sb * (1.0 - sb)
            dg_ref[rows, :D_MODEL] = dga.astype(BF16)
            dg_ref[rows, D_MODEL:] = dgb.astype(BF16)
            return carry[0] + _fold_rows(dga), carry[1] + _fold_rows(dgb)

        zero = jnp.zeros((8, D_MODEL), F32)
        pa, pb = _strips(strip, (zero, zero))
        _accumulate(db_ref.at[:, :D_MODEL], pa)
        _accumulate(db_ref.at[:, D_MODEL:], pb)

    return pl.pallas_call(
        body, name=name, grid=(t // TOK,),
        in_specs=[_row_spec(D_MODEL), _row_spec(D_MODEL, 2), _row_spec(D_MODEL, 3),
                  _vec_spec(2 * D_MODEL), _row_spec(D_MODEL), _row_spec(D_MODEL)],
        out_specs=[_row_spec(D_MODEL), _row_spec(D_MODEL), _row_spec(2 * D_MODEL),
                   _vec_spec(2 * D_MODEL)],
        out_shape=[_sds((t, D_MODEL), BF16), _sds((t, D_MODEL), BF16),
                   _sds((t, 2 * D_MODEL), BF16), _sds((1, 2 * D_MODEL), F32)],
        compiler_params=_cparams("arbitrary"))(dz, proj, proj, b_gate, ya, yb)


def _head_masks():
    lane = lax.broadcasted_iota(jnp.int32, (1, 2 * HEAD_DIM), 1)
    return lane < HEAD_DIM


BAND_ROWS = 2 * ATT_BLK + CHUNK


def _fill_band(band, prev_ref, cur_ref):
    band[0:ATT_BLK, :] = prev_ref[...]
    band[ATT_BLK:2 * ATT_BLK, :] = cur_ref[...]
    band[2 * ATT_BLK:, :] = jnp.zeros((CHUNK, ATTN_W), BF16)


def _pair_rows(x2, low):
    zero = jnp.zeros_like(x2)
    return jnp.concatenate([jnp.where(low, x2, zero), jnp.where(low, zero, x2)], axis=0)


def _pair_diag(o2, low):
    return jnp.where(low, o2[0:CHUNK, :], o2[CHUNK:, :])


N_PAIRS = HEADS // 2
SM_STRIP = 32
N_STRIPS = BAND_PAD // SM_STRIP
NEG = -1e30


def _fold8(x, op):
    return op(op(x[0:8], x[8:16]), op(x[16:24], x[24:32]))


def _strip(k):
    return pl.ds(pl.multiple_of(k * SM_STRIP, SM_STRIP), SM_STRIP)


def _band_probs(k2, qcat, bias_t, first_key):
    kpos = lax.broadcasted_iota(jnp.int32, (BAND_PAD, 1), 0)
    st = lax.dot_general(k2, qcat, (NT, ((), ())), preferred_element_type=F32)
    st = jnp.where(kpos + first_key >= 0, st + bias_t, NEG)
    e = jnp.exp(st - jnp.max(st, axis=0, keepdims=True))
    return e * (1.0 / jnp.sum(e, axis=0, keepdims=True))


def _attn_specs(nblk):
    cur = lambda col: pl.BlockSpec((ATT_BLK, ATTN_W), lambda s: (jnp.minimum(s, nblk - 1), col))
    prev = lambda col: pl.BlockSpec(
        (ATT_BLK, ATTN_W), lambda s: (jnp.maximum(jnp.minimum(s, nblk - 1) - 1, 0), col))
    return cur, prev


def _attn_fwd(name, proj, bias, comm=None):
    t = proj.shape[0]
    nblk = t // ATT_BLK
    cur, prev = _attn_specs(nblk)

    def body(q_ref, kp_ref, kc_ref, vp_ref, vc_ref, b_ref, o_ref, p_ref, kband, vband):
        s = pl.program_id(0)
        _fill_band(kband, kp_ref, kc_ref)
        _fill_band(vband, vp_ref, vc_ref)
        low = _head_masks()

        def chunk(ci, carry):
            r0 = pl.multiple_of(ci * CHUNK, CHUNK)
            for hp in range(N_PAIRS):
                cols = slice(hp * 128, (hp + 1) * 128)
                qcat = _pair_rows(q_ref[pl.ds(r0, CHUNK), cols] * ATTN_SCALE, low)
                p = _band_probs(kband[pl.ds(r0, BAND_PAD), cols], qcat, b_ref[hp],
                                (s * 8 - 8 + ci) * CHUNK).astype(BF16)
                p_ref[ci, hp] = p
                o2 = lax.dot_general(p, vband[pl.ds(r0, BAND_PAD), cols],
                                     (TN, ((), ())), preferred_element_type=F32)
                o_ref[pl.ds(r0, CHUNK), cols] = _pair_diag(o2, low).astype(BF16)
            return carry

        lax.fori_loop(0, 8, chunk, 0)

    outs, moved = _pcall(
        body, name, (nblk,),
        [cur(0), prev(1), cur(1), prev(2), cur(2),
         pl.BlockSpec((N_PAIRS, BAND_PAD, 128), lambda s: (0, 0, 0))],
        [pl.BlockSpec((ATT_BLK, ATTN_W), lambda s: (s, 0)),
         pl.BlockSpec((8, N_PAIRS, BAND_PAD, 128), lambda s: (s, 0, 0, 0))],
        [_sds((t, ATTN_W), BF16), _sds((t // CHUNK, N_PAIRS, BAND_PAD, 128), BF16)],
        [pltpu.VMEM((BAND_ROWS, ATTN_W), BF16), pltpu.VMEM((BAND_ROWS, ATTN_W), BF16)],
        _cparams("arbitrary"), (proj, proj, proj, proj, proj, bias), comm)
    return outs if comm is None else (*outs, moved)


def _attn_bwd(name, proj, datt, probs, comm=None):
    t = proj.shape[0]
    nblk = t // ATT_BLK
    cur, prev = _attn_specs(nblk)
    late = pl.BlockSpec((ATT_BLK, 3 * ATTN_W), lambda s: (jnp.maximum(s - 1, 0), 0))

    def body(q_ref, kp_ref, kc_ref, vp_ref, vc_ref, do_ref, p_ref,
             dqkv_ref, db_ref, kband, vband, dkacc, dvacc,
             dp_ref, dsb_ref, qc_ref, dc_ref, dq_ref, dq_held):
        s = pl.program_id(0)

        @pl.when(s == 0)
        def _():
            dkacc[...] = jnp.zeros_like(dkacc)
            dvacc[...] = jnp.zeros_like(dvacc)
            db_ref[...] = jnp.zeros_like(db_ref)
            dq_ref[...] = jnp.zeros_like(dq_ref)

        @pl.when(s < nblk)
        def _():
            _fill_band(kband, kp_ref, kc_ref)
            _fill_band(vband, vp_ref, vc_ref)
            low = _head_masks()

            def chunk(ci, carry):
                r0 = pl.multiple_of(ci * CHUNK, CHUNK)
                for hp in range(N_PAIRS):
                    cols = slice(hp * 128, (hp + 1) * 128)
                    qc_ref[hp] = _pair_rows(q_ref[pl.ds(r0, CHUNK), cols] * ATTN_SCALE, low)
                    dc_ref[hp] = _pair_rows(do_ref[pl.ds(r0, CHUNK), cols], low)
                    dp_ref[hp] = lax.dot_general(vband[pl.ds(r0, BAND_PAD), cols], dc_ref[hp],
                                                 (NT, ((), ())), preferred_element_type=F32)

                def sums(j, acc):
                    for u in range(2):
                        rows = _strip(2 * j + u)
                        acc = tuple(acc[hp] + _fold8(p_ref[ci, hp, rows, :].astype(F32)
                                                     * dp_ref[hp, rows, :], jnp.add)
                                    for hp in range(N_PAIRS))
                    return acc

                acc = lax.fori_loop(0, N_STRIPS // 2, sums, (jnp.zeros((8, 128), F32),) * N_PAIRS)
                delta = [jnp.sum(a, axis=0, keepdims=True) for a in acc]

                def grads(j, c):
                    for u in range(2):
                        rows = _strip(2 * j + u)
                        for hp in range(N_PAIRS):
                            ds = (p_ref[ci, hp, rows, :].astype(F32)
                                  * (dp_ref[hp, rows, :] - delta[hp]))
                            db_ref[hp, rows, :] += ds
                            dsb_ref[hp, rows, :] = ds.astype(BF16)
                    return c

                lax.fori_loop(0, N_STRIPS // 2, grads, 0)
                for hp in range(N_PAIRS):
                    cols = slice(hp * 128, (hp + 1) * 128)
                    dq2 = lax.dot_general(dsb_ref[hp], kband[pl.ds(r0, BAND_PAD), cols],
                                          (TN, ((), ())), preferred_element_type=F32)
                    dq_ref[pl.ds(r0, CHUNK), cols] = (_pair_diag(dq2, low) * ATTN_SCALE).astype(BF16)
                    dkacc[pl.ds(r0, BAND_PAD), cols] += jnp.dot(dsb_ref[hp], qc_ref[hp],
                                                               preferred_element_type=F32)
                    dvacc[pl.ds(r0, BAND_PAD), cols] += jnp.dot(p_ref[ci, hp], dc_ref[hp],
                                                               preferred_element_type=F32)
                return carry

            dq_held[...] = dq_ref[...]
            lax.fori_loop(0, 8, chunk, 0)

        @pl.when(s == nblk)
        def _():
            dq_held[...] = dq_ref[...]

        dqkv_ref[:, 0:ATTN_W] = dq_held[...]
        dqkv_ref[:, ATTN_W:2 * ATTN_W] = dkacc[0:ATT_BLK, :].astype(BF16)
        dqkv_ref[:, 2 * ATTN_W:] = dvacc[0:ATT_BLK, :].astype(BF16)
        dkacc[0:ATT_BLK, :] = dkacc[ATT_BLK:2 * ATT_BLK, :]
        dvacc[0:ATT_BLK, :] = dvacc[ATT_BLK:2 * ATT_BLK, :]
        dkacc[ATT_BLK:, :] = jnp.zeros((ATT_BLK + CHUNK, ATTN_W), F32)
        dvacc[ATT_BLK:, :] = jnp.zeros((ATT_BLK + CHUNK, ATTN_W), F32)

    outs, moved = _pcall(
        body, name, (nblk + 1,),
        [cur(0), prev(1), cur(1), prev(2), cur(2),
         pl.BlockSpec((ATT_BLK, ATTN_W), lambda s: (jnp.minimum(s, nblk - 1), 0)),
         pl.BlockSpec((8, N_PAIRS, BAND_PAD, 128), lambda s: (jnp.minimum(s, nblk - 1), 0, 0, 0))],
        [late, pl.BlockSpec((HEADS // 2, BAND_PAD, 128), lambda s: (0, 0, 0))],
        [_sds((t, 3 * ATTN_W), BF16), _sds((HEADS // 2, BAND_PAD, 128), F32)],
        [pltpu.VMEM((BAND_ROWS, ATTN_W), BF16), pltpu.VMEM((BAND_ROWS, ATTN_W), BF16),
         pltpu.VMEM((BAND_ROWS, ATTN_W), F32), pltpu.VMEM((BAND_ROWS, ATTN_W), F32),
         pltpu.VMEM((N_PAIRS, BAND_PAD, 128), F32), pltpu.VMEM((N_PAIRS, BAND_PAD, 128), BF16),
         pltpu.VMEM((N_PAIRS, 2 * CHUNK, 128), BF16), pltpu.VMEM((N_PAIRS, 2 * CHUNK, 128), BF16),
         pltpu.VMEM((ATT_BLK, ATTN_W), BF16), pltpu.VMEM((ATT_BLK, ATTN_W), BF16)],
        _cparams("arbitrary"), (proj, proj, proj, proj, proj, datt, probs), comm)
    return outs if comm is None else (*outs, moved)


def _diag_onehot(rel_rows):
    d0 = lax.broadcasted_iota(jnp.int32, (BIAS_LANES, BIAS_LANES), 0)
    d1 = lax.broadcasted_iota(jnp.int32, (BIAS_LANES, BIAS_LANES), 1)
    m, n = (d0, d1) if rel_rows else (d1, d0)
    hit = (m == jnp.minimum(BAND - 1 + MAX_REL - n, 2 * MAX_REL)) & (n < BAND + CHUNK - 1)
    return jnp.where(hit, 1.0, 0.0).astype(F32)


def _bias_table(name, rel_bias_l):
    rel_pad = jnp.pad(rel_bias_l, ((0, 0), (0, BIAS_LANES - N_REL)))

    def body(r_ref, o_ref):
        diag = jnp.dot(r_ref[...], _diag_onehot(True), preferred_element_type=F32,
                       precision=lax.Precision.HIGHEST)
        rowid = lax.broadcasted_iota(jnp.int32, (8, BIAS_LANES), 0)
        lane = lax.broadcasted_iota(jnp.int32, (8, BIAS_LANES), 1)
        for h in range(HEADS):
            d8 = jnp.broadcast_to(diag[h:h + 1, :], (8, BIAS_LANES))
            slab0 = pltpu.roll(d8, BIAS_LANES - CHUNK + 1, axis=1)
            for b in range(1, 8):
                slab0 = jnp.where(rowid == b, pltpu.roll(d8, BIAS_LANES - CHUNK + 1 + b, axis=1),
                                  slab0)
            for a in range(8):
                slab = slab0 if a == 0 else pltpu.roll(slab0, 8 * a, axis=1)
                o_ref[h * CHUNK + 8 * a:h * CHUNK + 8 * a + 8, :] = jnp.where(lane < BAND, slab, NEG)

    tab = pl.pallas_call(
        body, name=name,
        in_specs=[pl.BlockSpec(memory_space=pltpu.VMEM)],
        out_specs=pl.BlockSpec(memory_space=pltpu.VMEM),
        out_shape=_sds((HEADS * CHUNK, BIAS_LANES), F32),
    )(rel_pad)
    tab = tab.reshape(HEADS // 2, 2, CHUNK, BIAS_LANES)
    return jnp.transpose(tab, (0, 3, 1, 2)).reshape(HEADS // 2, BIAS_LANES, 2 * CHUNK)


def _bias_fold(name, dbias_t):
    rows = HEADS * CHUNK
    dbias = jnp.transpose(dbias_t.reshape(HEADS // 2, BIAS_LANES, 2, CHUNK), (0, 2, 3, 1))

    def body(d_ref, o_ref):
        rowid = lax.broadcasted_iota(jnp.int32, (8, BIAS_LANES), 0)
        diags = []
        for h in range(HEADS):
            acc = d_ref[h * CHUNK + 56:h * CHUNK + 64, :]
            for a in range(7):
                slab = d_ref[h * CHUNK + 8 * a:h * CHUNK + 8 * a + 8, :]
                acc = acc + pltpu.roll(slab, 56 - 8 * a, axis=1)
            tot = jnp.where(rowid == 7, acc, 0.0)
            for b in range(7):
                tot = tot + jnp.where(rowid == b, pltpu.roll(acc, 7 - b, axis=1), 0.0)
            diags.append(jnp.sum(tot, axis=0, keepdims=True))
        diag = jnp.concatenate(diags, axis=0)
        o_ref[...] = jnp.dot(diag, _diag_onehot(False), preferred_element_type=F32,
                             precision=lax.Precision.HIGHEST)

    return pl.pallas_call(
        body, name=name,
        in_specs=[pl.BlockSpec(memory_space=pltpu.VMEM)],
        out_specs=pl.BlockSpec(memory_space=pltpu.VMEM),
        out_shape=_sds((HEADS, BIAS_LANES), F32),
    )(dbias.reshape(rows, BIAS_LANES))


def _inv_counts(i):
    trow = lax.broadcasted_iota(jnp.int32, (TOK + HALO, 1), 0) + i * TOK
    return [1.0 / jnp.minimum(trow + 1, w).astype(F32) for w in POOL_WINDOWS]


def _pool_fwd(name, proj, wg, scale, comm=None):
    t = proj.shape[0]
    hb = TOK // HALO

    def body(u_ref, up_ref, wg_ref, sc_ref, pooled_ref, mixed_ref, b0, b1, b2, b3):
        i = pl.program_id(0)
        halo = up_ref[...].astype(F32)
        b0[0:HALO, :] = jnp.where(i == 0, jnp.zeros_like(halo), halo)
        b0[HALO:, :] = u_ref[...].astype(F32)
        n = TOK + HALO
        b1[8:n, :] = b0[8:n, :] + b0[7:n - 1, :]
        b2[16:n, 128:] = b1[16:n, 128:] + b1[14:n - 2, 128:]
        b3[24:n, 256:] = b2[24:n, 256:] + b2[20:n - 4, 256:]
        wins = [b1[HALO:n, 0:128], b2[HALO:n, 128:256], b3[HALO:n, 256:384],
                b3[HALO:n, 384:512] + b3[HALO - 8:n - 8, 384:512]]
        inv = _inv_counts(i)
        for g in range(4):
            cols = slice(g * POOL_GD, (g + 1) * POOL_GD)
            pooled = (wins[g] * inv[g][0:TOK] - b0[HALO:n, cols]).astype(BF16)
            pooled_ref[:, cols] = pooled
            pre = jnp.dot(pooled, wg_ref[g], preferred_element_type=F32)
            mixed_ref[:, cols] = (pre * sc_ref[:, cols]).astype(BF16)

    buf = pltpu.VMEM((TOK + HALO, POOL_W), F32)
    outs, moved = _pcall(
        body, name, (t // TOK,),
        [_row_spec(POOL_W, 3),
         pl.BlockSpec((HALO, POOL_W), lambda i: (jnp.maximum(i * hb - 1, 0), 3)),
         pl.BlockSpec((4, POOL_GD, POOL_GD), lambda i: (0, 0, 0)), _vec_spec(POOL_W)],
        [_row_spec(POOL_W), _row_spec(POOL_W)],
        [_sds((t, POOL_W), BF16), _sds((t, POOL_W), BF16)], [buf, buf, buf, buf],
        _cparams("arbitrary"), (proj, proj, wg, scale), comm)
    return outs if comm is None else (*outs, moved)


def _pool_bwd(name, dmixed, pooled, wg, scale, comm=None):
    t = dmixed.shape[0]
    nt = t // TOK
    hb = TOK // HALO

    def body(dm_ref, dmn_ref, p_ref, wg_ref, sc_ref, du_ref, dwg_ref, dsc_ref, c0, c1, c2, c3):
        i = pl.program_id(0)

        @pl.when(i == 0)
        def _():
            dwg_ref[...] = jnp.zeros_like(dwg_ref)
            dsc_ref[...] = jnp.zeros_like(dsc_ref)

        n = TOK + HALO
        inv = _inv_counts(i)
        dmv = dm_ref[...].astype(F32)
        dmn = dmn_ref[...].astype(F32)
        dmn = jnp.where(i == nt - 1, jnp.zeros_like(dmn), dmn)
        for g in range(4):
            cols = slice(g * POOL_GD, (g + 1) * POOL_GD)
            scg = sc_ref[:, cols]
            pg = p_ref[:, cols]
            dpre = (dmv[:, cols] * scg).astype(BF16)
            dpre_n = (dmn[:, cols] * scg).astype(BF16)
            pre = jnp.dot(pg, wg_ref[g], preferred_element_type=F32)
            dsc_ref[:, cols] += jnp.sum(dmv[:, cols] * pre, axis=0, keepdims=True)
            dwg_ref[g] += lax.dot_general(pg, dpre, (TN, ((), ())), preferred_element_type=F32)
            dpool = lax.dot_general(dpre, wg_ref[g], (NT, ((), ())), preferred_element_type=F32)
            dpool_n = lax.dot_general(dpre_n, wg_ref[g], (NT, ((), ())),
                                      preferred_element_type=F32)
            c0[0:TOK, cols] = dpool
            c0[TOK:n, cols] = dpool_n
            c1[0:TOK, cols] = dpool * inv[g][0:TOK]
            c1[TOK:n, cols] = dpool_n * inv[g][TOK:n]
        c2[0:n - 8, :] = c1[0:n - 8, :] + c1[1:n - 7, :]
        c3[0:n - 16, 128:] = c2[0:n - 16, 128:] + c2[2:n - 14, 128:]
        c1[0:n - 24, 256:] = c3[0:n - 24, 256:] + c3[4:n - 20, 256:]
        wins = [c2[0:TOK, 0:128], c3[0:TOK, 128:256], c1[0:TOK, 256:384],
                c1[0:TOK, 384:512] + c1[8:TOK + 8, 384:512]]
        for g in range(4):
            cols = slice(g * POOL_GD, (g + 1) * POOL_GD)
            du_ref[:, cols] = (wins[g] - c0[0:TOK, cols]).astype(BF16)

    buf = pltpu.VMEM((TOK + HALO, POOL_W), F32)
    outs, moved = _pcall(
        body, name, (nt,),
        [_row_spec(POOL_W),
         pl.BlockSpec((HALO, POOL_W), lambda i: (jnp.minimum((i + 1) * hb, nt * hb - 1), 0)),
         _row_spec(POOL_W), pl.BlockSpec((4, POOL_GD, POOL_GD), lambda i: (0, 0, 0)),
         _vec_spec(POOL_W)],
        [_row_spec(POOL_W), pl.BlockSpec((4, POOL_GD, POOL_GD), lambda i: (0, 0, 0)),
         _vec_spec(POOL_W)],
        [_sds((t, POOL_W), BF16), _sds((4, POOL_GD, POOL_GD), F32), _sds((1, POOL_W), F32)],
        [buf, buf, buf, buf], _cparams("arbitrary"), (dmixed, dmixed, pooled, wg, scale), comm)
    return outs if comm is None else (*outs, moved)


GELU_C = math.sqrt(2.0 / math.pi)


GELU_K = 0.044715


def _gelu_parts(x):
    x2 = x * x
    s = 0.5 + 0.5 * jnp.tanh(x * (GELU_C + (GELU_C * GELU_K) * x2))
    return x * s, s, x2


def _gelu(x):
    return _gelu_parts(x)[0]


def _gelu_and_grad(x):
    g, s, x2 = _gelu_parts(x)
    return g, s + g * (1.0 - s) * ((2 * GELU_C) + (6 * GELU_C * GELU_K) * x2)


def _taps(buf, r, rows):
    a = buf[pl.ds(r, rows + 8), :]
    return a[8:], pltpu.roll(a, 1, axis=0)[8:], pltpu.roll(a, 2, axis=0)[8:]


def _conv(taps, w_ref, b_ref):
    return b_ref[...] + w_ref[2:3, :] * taps[0] + w_ref[1:2, :] * taps[1] + w_ref[0:1, :] * taps[2]


def _stage(dst, prev_ref, cur_ref, next_ref, first, last):
    rows = cur_ref.shape[0]
    h = prev_ref[...].astype(F32)
    dst[0:8, :] = jnp.where(first, jnp.zeros_like(h), h)
    dst[8:8 + rows, :] = cur_ref[...].astype(F32)
    if next_ref is not None:
        h = next_ref[...].astype(F32)
        dst[8 + rows:, :] = jnp.where(last, jnp.zeros_like(h), h)


FWD_STRIP = 32
BWD_STRIP = 16


def _ffn_gate_fwd(name, hu, conv_w, conv_b, comm=None):
    t = hu.shape[0]
    ncol = D_FF // FF_COL
    hb = FF_TOK // 8

    def tile(off):
        return pl.BlockSpec((FF_TOK, FF_COL), lambda i, j: (i, j + off))

    def halo(off):
        return pl.BlockSpec((8, FF_COL), lambda i, j: (jnp.maximum(i * hb - 1, 0), j + off))

    def wspec(off):
        return pl.BlockSpec((3, FF_COL), lambda i, j: (0, j + off))

    def bspec(off):
        return pl.BlockSpec((1, FF_COL), lambda i, j: (0, j + off))

    def body(v_ref, vp_ref, g_ref, gp_ref, wv_ref, wg_ref, bv_ref, bg_ref, a_ref, hc_ref, vb, gb):
        first = pl.program_id(0) == 0
        _stage(vb, vp_ref, v_ref, None, first, None)
        _stage(gb, gp_ref, g_ref, None, first, None)

        def strip(k, carry):
            for u in range(2):
                r = pl.multiple_of((2 * k + u) * FWD_STRIP, FWD_STRIP)
                val = _conv(_taps(vb, r, FWD_STRIP), wv_ref, bv_ref)
                gate = _conv(_taps(gb, r, FWD_STRIP), wg_ref, bg_ref)
                a_ref[pl.ds(r, FWD_STRIP), :] = (_gelu(gate) * val).astype(BF16)
                hc_ref[0, pl.ds(r, FWD_STRIP), :] = val.astype(BF16)
                hc_ref[1, pl.ds(r, FWD_STRIP), :] = gate.astype(BF16)
            return carry

        lax.fori_loop(0, FF_TOK // (2 * FWD_STRIP), strip, 0)

    buf = pltpu.VMEM((FF_TOK + 8, FF_COL), F32)
    outs, moved = _pcall(
        body, name, (t // FF_TOK, ncol),
        [tile(0), halo(0), tile(ncol), halo(ncol), wspec(0), wspec(ncol), bspec(0), bspec(ncol)],
        [pl.BlockSpec((FF_TOK, FF_COL), lambda i, j: (i, j)),
         pl.BlockSpec((2, FF_TOK, FF_COL), lambda i, j: (0, i, j))],
        [_sds((t, D_FF), BF16), _sds((2, t, D_FF), BF16)], [buf, buf],
        _cparams("arbitrary", "arbitrary"),
        (hu, hu, hu, hu, conv_w, conv_w, conv_b, conv_b), comm)
    return outs if comm is None else (*outs, moved)


def _ffn_gate_bwd(name, da, hu, hc, conv_w, comm=None):
    t = hu.shape[0]
    nt = t // FF_TOK
    ncol = D_FF // FF_COL
    hb = FF_TOK // 8

    def tile(off):
        return pl.BlockSpec((FF_TOK, FF_COL), lambda j, i: (i, j + off))

    def nxt_rows(i):
        return jnp.minimum((i + 1) * hb, nt * hb - 1)

    def wspec(off):
        return pl.BlockSpec((3, FF_COL), lambda j, i: (0, j + off))

    def body(da_ref, dan_ref, v_ref, g_ref, hc_ref, hcn_ref, wv_ref, wg_ref,
             dh_ref, dwv_ref, dwg_ref):
        i = pl.program_id(1)
        first, last = i == 0, i == nt - 1

        @pl.when(first)
        def _():
            dwv_ref[...] = jnp.zeros_like(dwv_ref)
            dwg_ref[...] = jnp.zeros_like(dwg_ref)

        def grads(dav, val, gate):
            g, dg = _gelu_and_grad(gate.astype(F32))
            dav = dav.astype(F32)
            return dav * g, dav * val.astype(F32) * dg

        def fold(x):
            return x[0:8] + x[8:16]

        def strip(j, carry):
            for u in range(2):
                carry = one_strip(2 * j + u, carry)
            return carry

        def one_strip(k, carry):
            r = pl.multiple_of(FF_TOK - BWD_STRIP - k * BWD_STRIP, BWD_STRIP)
            rows = pl.ds(r, BWD_STRIP)
            dval, dgate = grads(da_ref[rows, :], hc_ref[0, rows, :], hc_ref[1, rows, :])
            new = (dval[0:8], dgate[0:8])
            for half, (d, below, h_ref, w_ref, dw_ref) in enumerate((
                    (dval, carry[0], v_ref, wv_ref, dwv_ref),
                    (dgate, carry[1], g_ref, wg_ref, dwg_ref))):
                e = jnp.concatenate([d, below], axis=0)
                e1 = pltpu.roll(e, BWD_STRIP + 7, axis=0)[0:BWD_STRIP]
                e2 = pltpu.roll(e, BWD_STRIP + 6, axis=0)[0:BWD_STRIP]
                dh = w_ref[2:3, :] * d + w_ref[1:2, :] * e1 + w_ref[0:1, :] * e2
                dh_ref[half, rows, :] = dh.astype(BF16)
                huv = h_ref[rows, :].astype(F32)
                dw_ref[0:8, :] += fold(e2 * huv)
                dw_ref[8:16, :] += fold(e1 * huv)
                dw_ref[16:24, :] += fold(d * huv)
                dw_ref[24:32, :] += fold(d)
            return new

        dan = dan_ref[...]
        dan = jnp.where(last, jnp.zeros_like(dan), dan)
        lax.fori_loop(0, FF_TOK // (2 * BWD_STRIP), strip, grads(dan, hcn_ref[0], hcn_ref[1]))

        @pl.when(last)
        def _():
            for dw_ref in (dwv_ref, dwg_ref):
                for q in range(4):
                    dw_ref[8 * q:8 * q + 1, :] = jnp.sum(dw_ref[8 * q:8 * q + 8, :], axis=0,
                                                         keepdims=True)

    acc = pl.BlockSpec((32, FF_COL), lambda j, i: (0, j))
    (dhu, dwv, dwg), moved = _pcall(
        body, name, (ncol, nt),
        [tile(0), pl.BlockSpec((8, FF_COL), lambda j, i: (nxt_rows(i), j)),
         tile(0), tile(ncol),
         pl.BlockSpec((2, FF_TOK, FF_COL), lambda j, i: (0, i, j)),
         pl.BlockSpec((2, 8, FF_COL), lambda j, i: (0, nxt_rows(i), j)),
         wspec(0), wspec(ncol)],
        [pl.BlockSpec((2, FF_TOK, FF_COL), lambda j, i: (0, i, j)), acc, acc],
        [_sds((2, t, D_FF), BF16), _sds((32, D_FF), F32), _sds((32, D_FF), F32)],
        [], _cparams("arbitrary", "arbitrary"),
        (da, da, hu, hu, hc, hc, conv_w, conv_w), comm)
    dconv = jnp.concatenate([dwv, dwg], axis=1).reshape(4, 8, 2 * D_FF)[:, 0]
    return (dhu, dconv) if comm is None else (dhu, dconv, moved)


def _mesh_pos():
    x, y, c = lax.axis_index("x"), lax.axis_index("y"), lax.axis_index("c")
    return x, y, c, [(1 - x, y), (x, 1 - y), (1 - x, 1 - y)]


def _remote(src, dst, send_sems, recv_sems, i, dev):
    return pltpu.make_async_remote_copy(src_ref=src, dst_ref=dst, send_sem=send_sems.at[i],
                                        recv_sem=recv_sems.at[i], device_id=dev,
                                        device_id_type=MESH)


def _mine(c, rows):
    return pl.ds(pl.multiple_of(c * (rows // 2), 16), rows // 2)


def _gather_send(shards, conv_shard, gathered, l):
    nbig = len(shards)
    with_conv = conv_shard is not None
    if gathered is None:
        ins = list(shards) + ([conv_shard] if with_conv else [])
        outs = [_sds((DEPTH, N_CHIPS) + s.shape[1:], s.dtype) for s in ins]
        alias = {}
    else:
        ins = list(shards) + list(gathered)
        outs = [_sds(g.shape, g.dtype) for g in gathered]
        alias = {nbig + k: k for k in range(nbig)}

    def copies(cin, cout, ssem, rsem):
        x, y, c, chips = _mesh_pos()
        me = 2 * x + y
        out = []
        for k in range(nbig):
            rows = shards[k].shape[1]
            for j, (cx, cy) in enumerate(chips):
                out.append(_remote(cin[k].at[l, _mine(c, rows)], cout[k].at[l, me, _mine(c, rows)],
                                   ssem, rsem, 4 * k + j, (cx, cy, c)))
            out.append(_remote(cin[k].at[l], cout[k].at[l, me], ssem, rsem, 4 * k + 3,
                               (x, y, 1 - c)))
        if with_conv:
            base = 4 * nbig
            for j, (cx, cy) in enumerate(chips):
                out.append(_remote(cin[nbig].at[c], cout[nbig].at[c, me], ssem, rsem, base + j,
                                   (cx, cy, c)))
            for ll in range(DEPTH):
                out.append(_remote(cin[nbig].at[ll], cout[nbig].at[ll, me], ssem, rsem,
                                   base + 3 + ll, (x, y, 1 - c)))
        return out

    return _Comm(ins, outs, copies, 4 * nbig + 5, alias)


def _gather_forward(gathered, nbig, rows, l):
    with_conv = len(gathered) > nbig
    alias = {k: k for k in range(len(gathered))}

    def copies(cin, cout, ssem, rsem):
        x, y, c, chips = _mesh_pos()
        out = []
        for k in range(nbig):
            for j, (cx, cy) in enumerate(chips):
                blk = cout[k].at[l, 2 * cx + cy, _mine(c, rows[k])]
                out.append(_remote(blk, blk, ssem, rsem, 3 * k + j, (x, y, 1 - c)))
        if with_conv:
            for j, (cx, cy) in enumerate(chips):
                blk = cout[nbig].at[c, 2 * cx + cy]
                out.append(_remote(blk, blk, ssem, rsem, 3 * nbig + j, (x, y, 1 - c)))
        return out

    return _Comm(gathered, [_sds(g.shape, g.dtype) for g in gathered], copies, 3 * nbig + 3, alias)


def _reduce_swap(grads, l):
    def copies(cin, cout, ssem, rsem):
        x, y, c, _ = _mesh_pos()
        return [_remote(cin[k].at[l, :, _mine(1 - c, g.shape[2])], cout[k], ssem, rsem, k,
                        (x, y, 1 - c)) for k, g in enumerate(grads)]

    outs = [_sds((N_CHIPS, g.shape[2] // 2, g.shape[3]), g.dtype) for g in grads]
    return _Comm(grads, outs, copies, len(grads))


def _reduce_scatter(sums):
    def copies(cin, cout, ssem, rsem):
        x, y, c, chips = _mesh_pos()
        return [_remote(cin[k].at[2 * cx + cy], cout[k].at[j], ssem, rsem, 3 * k + j, (cx, cy, c))
                for k in range(len(sums)) for j, (cx, cy) in enumerate(chips)]

    outs = [_sds((3,) + s.shape[1:], s.dtype) for s in sums]
    return _Comm(sums, outs, copies, 3 * len(sums))


def _reduce_share(reds, l):
    def copies(cin, cout, ssem, rsem):
        x, y, c, _ = _mesh_pos()
        out = []
        for k, r in enumerate(reds):
            half = cout[k].at[l, _mine(c, r.shape[1])]
            out.append(_remote(half, half, ssem, rsem, k, (x, y, 1 - c)))
        return out

    return _Comm(reds, [_sds(r.shape, r.dtype) for r in reds], copies, len(reds),
                 {k: k for k in range(len(reds))})


def _allreduce_small(per_layer):
    kinds = len(per_layer[0])
    shapes = [a.shape[1:] if a.shape[0] == 1 else a.shape for a in per_layer[0]]

    def body(*refs):
        ins = refs[:DEPTH * kinds]
        outs = refs[DEPTH * kinds:(DEPTH + 1) * kinds]
        gbufs = refs[(DEPTH + 1) * kinds:(DEPTH + 2) * kinds]
        send_sems, recv_sems = refs[-2], refs[-1]
        x, y, c, chips = _mesh_pos()
        sibling = (x, y, 1 - c)

        def copy(k, i, block, to):
            px, py, pc = block
            slot = gbufs[k].at[4 * px + 2 * py + pc]
            return _remote(slot, slot, send_sems, recv_sems, 7 * k + i, to)

        me = (x, y, c)
        first, passed = [], []
        for k in range(kinds):
            for l in range(DEPTH):
                a = ins[l * kinds + k]
                if per_layer[l][k].shape[0] == 1:
                    gbufs[k][4 * x + 2 * y + c, l:l + 1] = a[...]
                else:
                    gbufs[k][4 * x + 2 * y + c, l] = a[...]
            first.append(copy(k, 0, me, sibling))
            first += [copy(k, 1 + j, me, (*chip, c)) for j, chip in enumerate(chips)]
            passed += [copy(k, 4 + j, (*chip, c), sibling) for j, chip in enumerate(chips)]
        for cp in first:
            cp.start()
        for k in range(kinds):
            for j, chip in enumerate(chips):
                copy(k, 1 + j, (*chip, c), me).wait_recv()
                passed[3 * k + j].start()
        for k in range(kinds):
            copy(k, 0, sibling, me).wait_recv()
            for j, chip in enumerate(chips):
                copy(k, 4 + j, (*chip, 1 - c), me).wait_recv()
        for cp in first + passed:
            cp.wait_send()
        for k in range(kinds):
            acc = gbufs[k][0]
            for d in range(1, 8):
                acc = acc + gbufs[k][d]
            outs[k][...] = acc

    vmem = pl.BlockSpec(memory_space=pltpu.VMEM)
    return pl.pallas_call(
        body, name="allreduce_small",
        in_specs=[vmem] * (DEPTH * kinds), out_specs=[vmem] * kinds,
        out_shape=[_sds((DEPTH,) + s, F32) for s in shapes],
        scratch_shapes=[pltpu.VMEM((8, DEPTH) + s, F32) for s in shapes]
        + [pltpu.SemaphoreType.DMA((7 * kinds,)), pltpu.SemaphoreType.DMA((7 * kinds,))],
        compiler_params=pltpu.CompilerParams(vmem_limit_bytes=VMEM_LIMIT_V7X),
    )(*per_layer[0], *per_layer[1])


def _adamw_small(ws, gs, ms, vs):
    n = len(ws)
    c1 = 1.0 - ADAM_B1 ** ADAM_STEP
    c2 = 1.0 - ADAM_B2 ** ADAM_STEP

    def body(*refs):
        for i in range(n):
            w_ref, g_ref, m_ref, v_ref = (refs[j * n + i] for j in range(4))
            d_ref, nm_ref, nv_ref = (refs[(4 + j) * n + i] for j in range(3))
            gv = g_ref[...]
            nm = ADAM_B1 * m_ref[...] + (1.0 - ADAM_B1) * gv
            nv = ADAM_B2 * v_ref[...] + (1.0 - ADAM_B2) * (gv * gv)
            nm_ref[...] = nm
            nv_ref[...] = nv
            d_ref[...] = -ADAM_LR * ((nm / c1) / (jnp.sqrt(nv / c2) + ADAM_EPS)
                                     + ADAM_WD * w_ref[...])

    vmem = pl.BlockSpec(memory_space=pltpu.VMEM)
    outs = pl.pallas_call(
        body, name="adamw_small", in_specs=[vmem] * (4 * n), out_specs=[vmem] * (3 * n),
        out_shape=[_sds(w.shape, F32) for w in ws] * 3,
        compiler_params=pltpu.CompilerParams(vmem_limit_bytes=VMEM_LIMIT_V7X),
    )(*ws, *gs, *ms, *vs)
    return outs[:n], outs[n:2 * n], outs[2 * n:]


def _core_index():
    return jnp.reshape(lax.axis_index("c"), (1,)).astype(jnp.int32)


def _chip_index():
    return jnp.reshape(2 * lax.axis_index("x") + lax.axis_index("y"), (1,)).astype(jnp.int32)


def _chip_sums(name, stacked, sibs, l):
    n = len(stacked)
    dims = [(s.shape[2] // 2, s.shape[3]) for s in stacked]

    def body(c_ref, *refs):
        for k in range(n):
            a_ref, b_ref, o_ref = refs[k], refs[n + k], refs[2 * n + k]
            o_ref[...] = (a_ref[...].astype(F32) + b_ref[...].astype(F32)).astype(BF16)

    return pl.pallas_call(
        body, name=name,
        grid_spec=pltpu.PrefetchScalarGridSpec(
            num_scalar_prefetch=1, grid=(N_CHIPS,),
            in_specs=[pl.BlockSpec((None, None, hr, cd), lambda j, cr: (l, j, cr[0], 0))
                      for hr, cd in dims]
            + [pl.BlockSpec((None, hr, cd), lambda j, cr: (j, 0, 0)) for hr, cd in dims],
            out_specs=[pl.BlockSpec((None, hr, cd), lambda j, cr: (j, 0, 0)) for hr, cd in dims]),
        out_shape=[_sds((N_CHIPS, hr, cd), BF16) for hr, cd in dims],
        compiler_params=_cparams("parallel"))(_core_index(), *stacked, *sibs)


def _final_sums(name, sums, recvs, l, fills):
    n = len(sums)
    dims = [(s.shape[1] // 2, s.shape[2]) for s in sums]
    filled = fills[0] is not None

    def body(m_ref, *refs):
        outs = refs[-n:]
        for k in range(n):
            acc = refs[k][...].astype(F32)
            for j in range(3):
                acc = acc + refs[n + k][j].astype(F32)
            outs[k][...] = acc

    in_specs = ([pl.BlockSpec((None, tr, cd), lambda i, mr: (mr[0], i, 0)) for tr, cd in dims]
                + [pl.BlockSpec((3, tr, cd), lambda i, mr: (0, i, 0)) for tr, cd in dims])
    args = [jnp.concatenate([_chip_index(), _core_index()]), *sums, *recvs]
    aliases = {}
    if filled:
        in_specs += [pl.BlockSpec(memory_space=pl.ANY)] * n
        args += list(fills)
        aliases = {1 + 2 * n + k: k for k in range(n)}
    return pl.pallas_call(
        body, name=name,
        grid_spec=pltpu.PrefetchScalarGridSpec(
            num_scalar_prefetch=1, grid=(2,), in_specs=in_specs,
            out_specs=[pl.BlockSpec((None, tr, cd), lambda i, mr: (l, 2 * mr[1] + i, 0))
                       for tr, cd in dims]),
        out_shape=[_sds((DEPTH, 4 * tr, cd), F32) for tr, cd in dims],
        input_output_aliases=aliases,
        compiler_params=_cparams("parallel"))(*args)


def _adamw(name, w, g, m, v):
    nl, r, cdim = w.shape
    tr = r // 4 if r % 32 == 0 else r
    c1 = 1.0 - ADAM_B1 ** ADAM_STEP
    c2 = 1.0 - ADAM_B2 ** ADAM_STEP

    def body(w_ref, g_ref, m_ref, v_ref, d_ref, nm_ref, nv_ref, go_ref):
        gv = g_ref[...]
        go_ref[...] = gv
        nm = ADAM_B1 * m_ref[...] + (1.0 - ADAM_B1) * gv
        nv = ADAM_B2 * v_ref[...] + (1.0 - ADAM_B2) * (gv * gv)
        nm_ref[...] = nm
        nv_ref[...] = nv
        d_ref[...] = -ADAM_LR * ((nm / c1) / (jnp.sqrt(nv / c2) + ADAM_EPS) + ADAM_WD * w_ref[...])

    spec = pl.BlockSpec((None, tr, cdim), lambda l, i: (l, i, 0))
    out = _sds(w.shape, F32)
    return pl.pallas_call(
        body, name=name, grid=(nl, r // tr), in_specs=[spec] * 4, out_specs=[spec] * 4,
        out_shape=[out] * 4, compiler_params=_cparams("parallel", "parallel"))(w, g, m, v)


def kernel(x, norm_mix_pre, w_in, b_gate, rel_bias, w_attn_out, w_pool_group, pool_scale, w_pool_out, w_o, norm_mix_post, norm_ffn_pre, w_up, conv_w, conv_b, w_down, norm_ffn_post, loss_target, m_norm_mix_pre, m_w_in, m_b_gate, m_rel_bias, m_w_attn_out, m_w_pool_group, m_pool_scale, m_w_pool_out, m_w_o, m_norm_mix_post, m_norm_ffn_pre, m_w_up, m_conv_w, m_conv_b, m_w_down, m_norm_ffn_post, v_norm_mix_pre, v_w_in, v_b_gate, v_rel_bias, v_w_attn_out, v_w_pool_group, v_pool_scale, v_w_pool_out, v_w_o, v_norm_mix_post, v_norm_ffn_pre, v_w_up, v_conv_w, v_conv_b, v_w_down, v_norm_ffn_post):
    t = x.shape[1]
    xs = x.reshape(t, D_MODEL)
    target = loss_target.reshape(t, D_MODEL)

    names = ["w_in", "w_attn_out", "w_pool_out", "w_o", "w_up", "w_down"]
    shards = [w.astype(BF16) for w in (w_in, w_attn_out, w_pool_out, w_o, w_up, w_down)]
    rows = [s.shape[1] for s in shards]
    nbig = len(shards)
    h, g = _norm_fwd("l0_norm_mix_pre", x.reshape(t, D_MODEL), norm_mix_pre[0:1],
                     _gather_send(shards[:1], conv_w, None, 0))
    g = _comm_call("gather0_forward", _gather_forward(g, 1, rows[:1], 0))
    cw_full = jnp.transpose(g[1], (0, 2, 1, 3)).reshape(DEPTH, 3, 2 * D_FF)
    g = g[:1]
    wg_bf = w_pool_group.astype(BF16)

    def views(gathered):
        win_g, wao_g, wpo_g, wo_g, wup_g, wdn_g = gathered
        return (win_g, wao_g, wpo_g, wo_g.reshape(DEPTH, D_MODEL, D_MODEL), wup_g,
                wdn_g.reshape(DEPTH, D_FF, D_MODEL))

    saved = []
    xcur = xs
    for l in range(DEPTH):
        tag = f"l{l}_"
        bias = _bias_table(tag + "bias_table", rel_bias[l])
        proj = _mm_nn_blocked(tag + "proj", h, g[0], l, BF16)
        if l == 0:
            att, probs, rest = _attn_fwd(tag + "attn_fwd", proj, bias,
                                         _gather_send(shards[1:], None, None, 0))
            pooled, mixed, rest = _pool_fwd(tag + "pool_fwd", proj, wg_bf[l], pool_scale[l:l + 1],
                                            _gather_forward(rest, nbig - 1, rows[1:], 0))
            g = g + rest
        else:
            att, probs = _attn_fwd(tag + "attn_fwd", proj, bias)
            pooled, mixed = _pool_fwd(tag + "pool_fwd", proj, wg_bf[l], pool_scale[l:l + 1])
        win_g, wao_g, wpo_g, wo_full, wup_g, wdn_full = views(g)
        ya = _narrow_nn(tag + "attn_out", att, wao_g, l)
        yb = _narrow_nn(tag + "pool_out", mixed, wpo_g, l)
        z = _gate_fwd(tag + "gate_fwd", proj, b_gate[l:l + 1], ya, yb)
        mix = _mm_nn(tag + "mix", z, wo_full, l, D_MODEL, F32)
        x1, h2 = _post_pre_fwd(tag + "norm_mix_post", xcur, mix, norm_mix_post[l:l + 1],
                               norm_ffn_pre[l:l + 1])
        if l == 0:
            hu, mixing = _mm_nn_blocked(tag + "ffn_up", h2, wup_g, l, BF16,
                                        _gather_send(shards[:4], None, g[:4], 1))
            a, hc, ffn_g = _ffn_gate_fwd(tag + "ffn_gate_fwd", hu, cw_full[l], conv_b[l:l + 1],
                                         _gather_send(shards[4:], None, g[4:], 1))
            g = mixing + ffn_g
            wdn_full = views(g)[5]
        else:
            hu = _mm_nn_blocked(tag + "ffn_up", h2, wup_g, l, BF16)
            a, hc = _ffn_gate_fwd(tag + "ffn_gate_fwd", hu, cw_full[l], conv_b[l:l + 1])
        f = _mm_nn(tag + "ffn_down", a, wdn_full, l, D_FF, F32)
        saved.append(dict(x=xcur, h=h, proj=proj, att=att, pooled=pooled, mixed=mixed, ya=ya,
                          yb=yb, z=z, mix=mix, x1=x1, h2=h2, hu=hu, hc=hc, a=a, f=f, probs=probs))
        if l == 0:
            xcur, h, g = _post_pre_fwd(tag + "norm_ffn_post", x1, f, norm_ffn_post[l:l + 1],
                                       norm_mix_pre[l + 1:l + 2], _gather_forward(g, nbig, rows, 1))
        elif l < DEPTH - 1:
            xcur, h = _post_pre_fwd(tag + "norm_ffn_post", x1, f, norm_ffn_post[l:l + 1],
                                    norm_mix_pre[l + 1:l + 2])
    win_g, wao_g, wpo_g, wo_full, wup_g, wdn_full = views(g)

    dy, df, d_nfpost, loss_local = _tail("tail", saved[-1]["x1"], saved[-1]["f"],
                                         norm_ffn_post[DEPTH - 1:DEPTH], target)
    loss = lax.psum(loss_local, ("x", "y", "c"))

    dx = dy
    dws = dict.fromkeys(names)
    reds = [None] * nbig
    small_grads = [None] * DEPTH
    ffn = [4, 5]
    outs3 = [1, 2, 3]

    def blocks(ks):
        return [dws[names[k]].reshape(DEPTH, N_CHIPS, rows[k], -1) for k in ks]

    def chip_sums(ks, sib, l):
        return _chip_sums(f"chip_sums{l}_" + names[ks[0]], blocks(ks), sib, l)

    def final_sums(ks, sums, recv, l):
        outs = _final_sums(f"final_sums{l}_" + names[ks[0]], sums, recv, l, [reds[k] for k in ks])
        for k, r in zip(ks, outs):
            reds[k] = r

    for l in reversed(range(DEPTH)):
        tag = f"l{l}_"
        sv = saved[l]
        every = list(range(nbig))
        if l == 0:
            da, sib = _mm_nt(tag + "ffn_down_dx", df, wdn_full, l, D_FF // 2, BF16,
                             _reduce_swap(blocks(every), 1))
            sums = chip_sums(every, sib, 1)
        else:
            da = _mm_nt(tag + "ffn_down_dx", df, wdn_full, l, D_FF // 2, BF16)
        dws["w_down"] = _mm_tn(tag + "ffn_down_dw", sv["a"], df, D_FF // 2, l, dws["w_down"])
        if l == 0:
            dhu, dconv, recv = _ffn_gate_bwd(tag + "ffn_gate_bwd", da, sv["hu"], sv["hc"],
                                             cw_full[l], _reduce_scatter(sums))
            final_sums(every, sums, recv, 1)
            dh2, reds = _mm_nt_blocked(tag + "ffn_up_dx", dhu, wup_g, l, F32,
                                       _reduce_share(reds, 1))
        else:
            dhu, dconv = _ffn_gate_bwd(tag + "ffn_gate_bwd", da, sv["hu"], sv["hc"], cw_full[l])
            dh2 = _mm_nt_blocked(tag + "ffn_up_dx", dhu, wup_g, l, F32)
        dws["w_up"] = _mm_tn_blocked(tag + "ffn_up_dw", sv["h2"], dhu, l, dws["w_up"])
        if l == 0:
            dx1, d_nfpre, dmix, d_nmpost, sib = _pre_post_bwd(
                tag + "norm_ffn_pre_bwd", dh2, sv["x1"], dx, norm_ffn_pre[l:l + 1], sv["mix"],
                norm_mix_post[l:l + 1], _reduce_swap(blocks(ffn), 0))
            sums = chip_sums(ffn, sib, 0)
        else:
            dx1, d_nfpre, dmix, d_nmpost = _pre_post_bwd(
                tag + "norm_ffn_pre_bwd", dh2, sv["x1"], dx, norm_ffn_pre[l:l + 1], sv["mix"],
                norm_mix_post[l:l + 1])
        dz = _mm_nt(tag + "mix_dx", dmix, wo_full, l, D_MODEL, BF16)
        dws["w_o"] = _mm_tn(tag + "mix_dw", sv["z"], dmix, D_MODEL, l, dws["w_o"])
        dya, dyb, dgates, d_bgate = _gate_bwd(tag + "gate_bwd", dz, sv["proj"], b_gate[l:l + 1],
                                              sv["ya"], sv["yb"])
        datt = _narrow_nt(tag + "attn_out_dx", dya, wao_g, l)
        dws["w_attn_out"] = _narrow_tn(tag + "attn_out_dw", sv["att"], dya, l, dws["w_attn_out"])
        dmixed = _narrow_nt(tag + "pool_out_dx", dyb, wpo_g, l)
        dws["w_pool_out"] = _narrow_tn(tag + "pool_out_dw", sv["mixed"], dyb, l, dws["w_pool_out"])
        if l == 0:
            du, d_wg, d_pscale, sib = _pool_bwd(tag + "pool_bwd", dmixed, sv["pooled"], wg_bf[l],
                                                pool_scale[l:l + 1], _reduce_swap(blocks(outs3), 0))
            sums3 = chip_sums(outs3, sib, 0)
            dqkv, dbias, recv = _attn_bwd(
                tag + "attn_bwd", sv["proj"], datt, sv["probs"],
                _both(_reduce_scatter(sums), _reduce_scatter(sums3)))
            final_sums(ffn, sums, recv[:len(ffn)], 0)
            final_sums(outs3, sums3, recv[len(ffn):], 0)
        else:
            du, d_wg, d_pscale = _pool_bwd(tag + "pool_bwd", dmixed, sv["pooled"], wg_bf[l],
                                           pool_scale[l:l + 1])
            dqkv, dbias = _attn_bwd(tag + "attn_bwd", sv["proj"], datt, sv["probs"])
        d_rel = _bias_fold(tag + "bias_fold", dbias)
        if l == 0:
            dh, shared = _proj_dx(tag + "proj_dx", dqkv, du, dgates, win_g, l,
                                  _reduce_share([reds[k] for k in ffn + outs3], 0))
            for k, r in zip(ffn + outs3, shared):
                reds[k] = r
        else:
            dh = _proj_dx(tag + "proj_dx", dqkv, du, dgates, win_g, l)
        dws["w_in"] = _proj_dw(tag + "proj_dw", sv["h"], dqkv, du, dgates, l, dws["w_in"])
        small_grads[l] = [None, d_nmpost, d_nfpre, d_nfpost, d_bgate, d_rel, d_wg, d_pscale, dconv]
        if l > 0:
            dx, small_grads[l][0], df, d_nfpost = _pre_post_bwd(
                tag + "norm_mix_pre_bwd", dh, sv["x"], dx1, norm_mix_pre[l:l + 1],
                saved[l - 1]["f"], norm_ffn_post[l - 1:l])
        else:
            dx, small_grads[l][0] = _norm_pre_bwd(tag + "norm_mix_pre_bwd", dh, sv["x"], dx1,
                                                  norm_mix_pre[l:l + 1])

    grad_x = dx.reshape(x.shape)

    delta, new_m, new_v = {}, {}, {}
    sib = _comm_call("reduce_swap", _reduce_swap(blocks([0]), 0))
    sums = chip_sums([0], sib, 0)
    recv = _comm_call("reduce_scatter", _reduce_scatter(sums))
    final_sums([0], sums, recv, 0)
    g_big = _comm_call("reduce_share", _reduce_share([reds[0]], 0)) + reds[1:]

    (g_nmpre, g_nmpost, g_nfpre, g_nfpost, g_bgate, g_rel, g_wg, g_pscale,
     g_conv) = _allreduce_small(small_grads)
    g_rel = g_rel[:, :, :N_REL]
    g_cb = g_conv[:, 3]
    ncw = conv_w.shape[2]
    chip = 2 * lax.axis_index("x") + lax.axis_index("y")
    g_cw = lax.dynamic_slice_in_dim(g_conv[:, 0:3], chip * ncw, ncw, axis=2)

    grads = dict(norm_mix_pre=g_nmpre, w_in=g_big[0], b_gate=g_bgate, rel_bias=g_rel,
                 w_attn_out=g_big[1], w_pool_group=g_wg, pool_scale=g_pscale, w_pool_out=g_big[2],
                 w_o=g_big[3], norm_mix_post=g_nmpost, norm_ffn_pre=g_nfpre, w_up=g_big[4],
                 conv_w=g_cw, conv_b=g_cb, w_down=g_big[5], norm_ffn_post=g_nfpost)
    weights = dict(norm_mix_pre=norm_mix_pre, w_in=w_in, b_gate=b_gate, rel_bias=rel_bias,
                   w_attn_out=w_attn_out, w_pool_group=w_pool_group, pool_scale=pool_scale,
                   w_pool_out=w_pool_out, w_o=w_o, norm_mix_post=norm_mix_post,
                   norm_ffn_pre=norm_ffn_pre, w_up=w_up, conv_w=conv_w, conv_b=conv_b,
                   w_down=w_down, norm_ffn_post=norm_ffn_post)
    moms = dict(norm_mix_pre=(m_norm_mix_pre, v_norm_mix_pre), w_in=(m_w_in, v_w_in),
                b_gate=(m_b_gate, v_b_gate), rel_bias=(m_rel_bias, v_rel_bias),
                w_attn_out=(m_w_attn_out, v_w_attn_out),
                w_pool_group=(m_w_pool_group, v_w_pool_group),
                pool_scale=(m_pool_scale, v_pool_scale), w_pool_out=(m_w_pool_out, v_w_pool_out),
                w_o=(m_w_o, v_w_o), norm_mix_post=(m_norm_mix_post, v_norm_mix_post),
                norm_ffn_pre=(m_norm_ffn_pre, v_norm_ffn_pre), w_up=(m_w_up, v_w_up),
                conv_w=(m_conv_w, v_conv_w), conv_b=(m_conv_b, v_conv_b),
                w_down=(m_w_down, v_w_down), norm_ffn_post=(m_norm_ffn_post, v_norm_ffn_post))
    order = list(weights.keys())

    small_names = [nm for nm in order if nm not in names]
    for nm in names:
        delta[nm], new_m[nm], new_v[nm], grads[nm] = _adamw("adamw_" + nm, weights[nm], grads[nm],
                                                            *moms[nm])
    d_s, m_s, v_s = _adamw_small([weights[nm] for nm in small_names],
                                 [grads[nm] for nm in small_names],
                                 [moms[nm][0] for nm in small_names],
                                 [moms[nm][1] for nm in small_names])
    for i, nm in enumerate(small_names):
        delta[nm], new_m[nm], new_v[nm] = d_s[i], m_s[i], v_s[i]

    return (loss, grad_x, *[grads[nm] for nm in order], *[delta[nm] for nm in order],
            *[new_m[nm] for nm in order], *[new_v[nm] for nm in order])
```

```python
import functools
import math

import jax
import jax.numpy as jnp
from jax import lax
from jax.experimental import pallas as pl
from jax.experimental.pallas import tpu as pltpu

F32 = jnp.float32
BF16 = jnp.bfloat16
MESH = pl.DeviceIdType.MESH

D_MODEL = 1024
DEPTH = 2
CHUNK = 64
BAND_CHUNKS = 9
BAND = BAND_CHUNKS * CHUNK
HEADS = 8
HEAD_DIM = 64
ATTN_W = HEADS * HEAD_DIM
POOL_WINDOWS = (2, 4, 8, 16)
POOL_W = 512
POOL_GD = 128
MAX_REL = 256
N_REL = 2 * MAX_REL + 1
D_FF = 2816
IN_W = 3 * ATTN_W + POOL_W + 2 * D_MODEL
EPS = 1e-6
ATTN_SCALE = HEAD_DIM ** -0.5
BAND_PAD = 640
BIAS_LANES = BAND_PAD
N_CHIPS = 4

ADAM_LR = 0.001
ADAM_B1 = 0.9
ADAM_B2 = 0.999
ADAM_EPS = 1e-08
ADAM_WD = 0.01
ADAM_STEP = 10

VMEM_LIMIT_V7X = 56 * 1024 * 1024
TOK = 512
ATT_BLK = 8 * CHUNK
FF_COL = 256
FF_TOK = 1024
HALO = 32


def _cparams(*sem):
    return pltpu.CompilerParams(dimension_semantics=sem, vmem_limit_bytes=VMEM_LIMIT_V7X)


def _sds(shape, dtype):
    return jax.ShapeDtypeStruct(shape, dtype)


class _Comm:
    def __init__(self, ins, outs, copies, n_sems, alias=None):
        self.ins, self.outs, self.copies, self.n_sems = list(ins), list(outs), copies, n_sems
        self.alias = dict(alias or {})


class _SemsFrom:
    def __init__(self, sems, start):
        self.sems, self.start = sems, start

    @property
    def at(self):
        return self

    def __getitem__(self, i):
        return self.sems.at[self.start + i]


def _both(a, b):
    na, nao = len(a.ins), len(a.outs)

    def copies(cin, cout, ssem, rsem):
        return (a.copies(cin[:na], cout[:nao], ssem, rsem)
                + b.copies(cin[na:], cout[nao:], _SemsFrom(ssem, a.n_sems), _SemsFrom(rsem, a.n_sems)))

    alias = dict(a.alias)
    alias.update({na + i: nao + o for i, o in b.alias.items()})
    return _Comm(a.ins + b.ins, a.outs + b.outs, copies, a.n_sems + b.n_sems, alias)


def _pcall(body, name, grid, in_specs, out_specs, out_shape, scratch_shapes, compiler_params, args,
           comm=None, aliases=None):
    single = not isinstance(out_shape, (list, tuple))
    out_specs = [out_specs] if single else list(out_specs)
    out_shape = [out_shape] if single else list(out_shape)
    n_in, n_out = len(in_specs), len(out_specs)
    aliases = dict(aliases or {})
    if comm is None:
        res = pl.pallas_call(
            body, name=name, grid=grid, in_specs=list(in_specs), out_specs=out_specs,
            out_shape=out_shape, scratch_shapes=list(scratch_shapes),
            input_output_aliases=aliases, compiler_params=compiler_params)(*args)
        return (res[0] if single else res), None
    ci, co = len(comm.ins), len(comm.outs)

    def hosted(*refs):
        main_in, cin = refs[:n_in], refs[n_in:n_in + ci]
        main_out = refs[n_in + ci:n_in + ci + n_out]
        cout = refs[n_in + ci + n_out:n_in + ci + n_out + co]
        rest = refs[n_in + ci + n_out + co:]
        copies = comm.copies(cin, cout, rest[-2], rest[-1])
        ids = [pl.program_id(a) for a in range(len(grid))]
        first = functools.reduce(jnp.logical_and, [i == 0 for i in ids])
        last = functools.reduce(jnp.logical_and, [i == g - 1 for i, g in zip(ids, grid)])

        @pl.when(first)
        def _():
            for cp in copies:
                cp.start()

        body(*main_in, *main_out, *rest[:-2])

        @pl.when(last)
        def _():
            for cp in copies:
                cp.wait()

    for i, o in comm.alias.items():
        aliases[n_in + i] = n_out + o
    hbm = pl.BlockSpec(memory_space=pl.ANY)
    sems = pltpu.SemaphoreType.DMA((comm.n_sems,))
    res = pl.pallas_call(
        hosted, name=name, grid=grid, in_specs=list(in_specs) + [hbm] * ci,
        out_specs=out_specs + [hbm] * co, out_shape=out_shape + comm.outs,
        scratch_shapes=list(scratch_shapes) + [sems, sems],
        input_output_aliases=aliases, compiler_params=compiler_params)(*args, *comm.ins)
    return (res[0] if single else list(res[:n_out])), list(res[n_out:])


def _comm_call(name, comm):
    ci = len(comm.ins)

    def body(*refs):
        copies = comm.copies(refs[:ci], refs[ci:-2], refs[-2], refs[-1])
        for cp in copies:
            cp.start()
        for cp in copies:
            cp.wait()

    hbm = pl.BlockSpec(memory_space=pl.ANY)
    sems = pltpu.SemaphoreType.DMA((comm.n_sems,))
    return list(pl.pallas_call(
        body, name=name, in_specs=[hbm] * ci, out_specs=[hbm] * len(comm.outs),
        out_shape=comm.outs, scratch_shapes=[sems, sems],
        input_output_aliases=comm.alias)(*comm.ins))


def _matmul(name, a, b, a_spec, b_spec, o_spec, out_shape, grid, contract, nk, acc_shape,
            fill=None, comm=None):
    in_place = out_shape.dtype == F32

    def body(*refs):
        a_ref, b_ref = refs[0], refs[1]
        o_ref = refs[2 if fill is None else 3]
        scratch = refs[(3 if fill is None else 4):]
        part = lax.dot_general(a_ref[...], b_ref[...], (contract, ((), ())),
                               preferred_element_type=F32)
        if nk == 1:
            o_ref[...] = part.astype(o_ref.dtype)
        else:
            acc_ref = o_ref if in_place else scratch[0]
            k = pl.program_id(2)

            @pl.when(k == 0)
            def _():
                acc_ref[...] = part

            @pl.when(k > 0)
            def _():
                acc_ref[...] += part

            if not in_place:
                @pl.when(k == nk - 1)
                def _():
                    o_ref[...] = acc_ref[...].astype(o_ref.dtype)

    scratch = [] if nk == 1 or in_place else [pltpu.VMEM(acc_shape, F32)]
    in_specs, args, aliases = [a_spec, b_spec], [a, b], {}
    if fill is not None:
        in_specs.append(pl.BlockSpec(memory_space=pl.ANY))
        args.append(fill)
        aliases = {2: 0}
    out, moved = _pcall(body, name, grid, in_specs, o_spec, out_shape, scratch,
                        _cparams("parallel", "parallel", "arbitrary"), args, comm, aliases)
    return out if comm is None else (out, moved)


NN = ((1,), (0,))
NT = ((1,), (1,))
TN = ((0,), (0,))


def _tm(t):
    return min(t, 1024)


def _tt(t):
    return min(t, 2048)


def _col_block_spec(a, rows, nb, row_col):
    if a.ndim == 2:
        return pl.BlockSpec((rows, nb), row_col)

    def halves(*ids):
        r, c = row_col(*ids)
        return c // 2, r, c % 2

    return pl.BlockSpec((None, rows, nb), halves)


def _mm_nn_blocked(name, a, w, l, out_dtype, comm=None):
    t, k = a.shape
    nb = w.shape[3]
    tm = _tm(t)
    return _matmul(
        name, a, w,
        pl.BlockSpec((tm, k), lambda i, n, kk: (i, 0)),
        pl.BlockSpec((None, None, k, nb), lambda i, n, kk: (l, n, 0, 0)),
        pl.BlockSpec((tm, nb), lambda i, n, kk: (i, n)),
        _sds((t, N_CHIPS * nb), out_dtype), (t // tm, N_CHIPS, 1), NN, 1, None, comm=comm)


def _mm_nt_blocked(name, a, w, l, out_dtype, comm=None):
    t = a.shape[-2]
    k, nb = w.shape[2], w.shape[3]
    tm = _tm(t)
    return _matmul(
        name, a, w,
        _col_block_spec(a, tm, nb, lambda i, n, kk: (i, kk)),
        pl.BlockSpec((None, None, k, nb), lambda i, n, kk: (l, kk, 0, 0)),
        pl.BlockSpec((tm, k), lambda i, n, kk: (i, 0)),
        _sds((t, k), out_dtype), (t // tm, 1, N_CHIPS), NT, N_CHIPS, (tm, k), comm=comm)


def _mm_tn_blocked(name, a, g, l, fill):
    t, k = a.shape
    nb = g.shape[-1] * (g.ndim - 1) // N_CHIPS
    tt = _tt(t)
    nt = t // tt
    return _matmul(
        name, a, g,
        pl.BlockSpec((tt, k), lambda n, j, kk: (kk, 0)),
        _col_block_spec(g, tt, nb, lambda n, j, kk: (kk, n)),
        pl.BlockSpec((None, None, k, nb), lambda n, j, kk: (l, n, 0, 0)),
        _sds((DEPTH, N_CHIPS, k, nb), BF16), (N_CHIPS, 1, nt), TN, nt, (k, nb), fill)


def _proj_pieces(rows, dqkv_first):
    def piece(col):
        if dqkv_first:
            return pl.BlockSpec((rows, ATTN_W), lambda i, kk: (i, col))
        return pl.BlockSpec((rows, ATTN_W), lambda n, kk: (kk, col))
    return [piece(0), piece(1), piece(2), piece(0)]


def _proj_dx(name, dqkv, du, dgates, w, l, comm=None):
    t = du.shape[0]
    k, nb = w.shape[2], w.shape[3]
    tm = _tm(t)

    def body(dq_ref, dk_ref, dv_ref, du_ref, dg_ref, w_ref, o_ref):
        kk = pl.program_id(1)

        def mm(a):
            return lax.dot_general(a, w_ref[...], (NT, ((), ())), preferred_element_type=F32)

        @pl.when(kk == 0)
        def _():
            o_ref[...] = mm(jnp.concatenate([dq_ref[...], dk_ref[...]], axis=1))

        @pl.when(kk == 1)
        def _():
            o_ref[...] += mm(jnp.concatenate([dv_ref[...], du_ref[...]], axis=1))

        @pl.when(kk >= 2)
        def _():
            o_ref[...] += mm(dg_ref[...])

    out, moved = _pcall(
        body, name, (t // tm, N_CHIPS),
        _proj_pieces(tm, True)
        + [pl.BlockSpec((tm, nb), lambda i, kk: (i, jnp.maximum(kk - 2, 0))),
           pl.BlockSpec((None, None, k, nb), lambda i, kk: (l, kk, 0, 0))],
        pl.BlockSpec((tm, k), lambda i, kk: (i, 0)), _sds((t, k), F32),
        [], _cparams("arbitrary", "arbitrary"),
        (dqkv, dqkv, dqkv, du, dgates, w), comm)
    return out if comm is None else (out, moved)


def _proj_dw(name, h, dqkv, du, dgates, l, fill):
    t, k = h.shape
    nb = dgates.shape[1] // 2
    tt = _tm(t)
    nt = t // tt

    def body(*refs):
        h_ref, dq_ref, dk_ref, dv_ref, du_ref, dg_ref = refs[:6]
        o_ref, acc_ref = refs[-2], refs[-1]
        n, kk = pl.program_id(0), pl.program_id(1)

        def update(g):
            part = lax.dot_general(h_ref[...], g, (TN, ((), ())), preferred_element_type=F32)

            @pl.when(kk == 0)
            def _():
                acc_ref[...] = part

            @pl.when(kk > 0)
            def _():
                acc_ref[...] += part

        @pl.when(n == 0)
        def _():
            update(jnp.concatenate([dq_ref[...], dk_ref[...]], axis=1))

        @pl.when(n == 1)
        def _():
            update(jnp.concatenate([dv_ref[...], du_ref[...]], axis=1))

        @pl.when(n >= 2)
        def _():
            update(dg_ref[...])

        @pl.when(kk == nt - 1)
        def _():
            o_ref[...] = acc_ref[...].astype(BF16)

    in_specs = ([pl.BlockSpec((tt, k), lambda n, kk: (kk, 0))] + _proj_pieces(tt, False)
                + [pl.BlockSpec((tt, nb), lambda n, kk: (kk, jnp.maximum(n - 2, 0)))])
    args, aliases = [h, dqkv, dqkv, dqkv, du, dgates], {}
    if fill is not None:
        in_specs.append(pl.BlockSpec(memory_space=pl.ANY))
        args.append(fill)
        aliases = {6: 0}
    return pl.pallas_call(
        body, name=name, grid=(N_CHIPS, nt), in_specs=in_specs,
        out_specs=pl.BlockSpec((None, None, k, nb), lambda n, kk: (l, n, 0, 0)),
        out_shape=_sds((DEPTH, N_CHIPS, k, nb), BF16),
        scratch_shapes=[pltpu.VMEM((k, nb), F32)], input_output_aliases=aliases,
        compiler_params=_cparams("parallel", "arbitrary"))(*args)


def _narrow_nn(name, a, w, l):
    t, k = a.shape
    nb = w.shape[3]
    tm = _tm(t)

    def body(a_ref, w_ref, o_ref):
        av = a_ref[...]
        for j in range(N_CHIPS):
            o_ref[:, j * nb:(j + 1) * nb] = jnp.dot(
                av, w_ref[j], preferred_element_type=F32).astype(BF16)

    return pl.pallas_call(
        body, name=name, grid=(t // tm,),
        in_specs=[pl.BlockSpec((tm, k), lambda i: (i, 0)),
                  pl.BlockSpec((None, N_CHIPS, k, nb), lambda i: (l, 0, 0, 0))],
        out_specs=pl.BlockSpec((tm, N_CHIPS * nb), lambda i: (i, 0)),
        out_shape=_sds((t, N_CHIPS * nb), BF16), compiler_params=_cparams("parallel"))(a, w)


def _narrow_nt(name, a, w, l):
    t = a.shape[0]
    k, nb = w.shape[2], w.shape[3]
    tm = _tm(t)

    def body(a_ref, w_ref, o_ref):
        acc = lax.dot_general(a_ref[:, 0:nb], w_ref[0], (NT, ((), ())), preferred_element_type=F32)
        for j in range(1, N_CHIPS):
            acc = acc + lax.dot_general(a_ref[:, j * nb:(j + 1) * nb], w_ref[j], (NT, ((), ())),
                                        preferred_element_type=F32)
        o_ref[...] = acc.astype(BF16)

    return pl.pallas_call(
        body, name=name, grid=(t // tm,),
        in_specs=[pl.BlockSpec((tm, N_CHIPS * nb), lambda i: (i, 0)),
                  pl.BlockSpec((None, N_CHIPS, k, nb), lambda i: (l, 0, 0, 0))],
        out_specs=pl.BlockSpec((tm, k), lambda i: (i, 0)),
        out_shape=_sds((t, k), BF16), compiler_params=_cparams("parallel"))(a, w)


def _narrow_tn(name, a, g, l, fill):
    t, k = a.shape
    nb = g.shape[1] // N_CHIPS
    tt = _tm(t)
    nt = t // tt

    def body(*refs):
        a_ref, g_ref, o_ref, acc_ref = refs[0], refs[1], refs[-2], refs[-1]
        i = pl.program_id(0)
        part = lax.dot_general(a_ref[...], g_ref[...], (TN, ((), ())), preferred_element_type=F32)

        @pl.when(i == 0)
        def _():
            acc_ref[...] = part

        @pl.when(i > 0)
        def _():
            acc_ref[...] += part

        @pl.when(i == nt - 1)
        def _():
            for j in range(N_CHIPS):
                o_ref[j] = acc_ref[:, j * nb:(j + 1) * nb].astype(BF16)

    in_specs = [pl.BlockSpec((tt, k), lambda i: (i, 0)),
                pl.BlockSpec((tt, N_CHIPS * nb), lambda i: (i, 0))]
    args, aliases = [a, g], {}
    if fill is not None:
        in_specs.append(pl.BlockSpec(memory_space=pl.ANY))
        args.append(fill)
        aliases = {2: 0}
    return pl.pallas_call(
        body, name=name, grid=(nt,), in_specs=in_specs,
        out_specs=pl.BlockSpec((None, N_CHIPS, k, nb), lambda i: (l, 0, 0, 0)),
        out_shape=_sds((DEPTH, N_CHIPS, k, nb), BF16),
        scratch_shapes=[pltpu.VMEM((k, N_CHIPS * nb), F32)], input_output_aliases=aliases,
        compiler_params=_cparams("arbitrary"))(*args)


def _mm_nn(name, a, w, l, tk, out_dtype):
    t, k = a.shape
    n = w.shape[2]
    tm = _tm(t)
    nk = k // tk
    return _matmul(
        name, a, w,
        pl.BlockSpec((tm, tk), lambda i, j, kk: (i, kk)),
        pl.BlockSpec((None, tk, n), lambda i, j, kk: (l, kk, 0)),
        pl.BlockSpec((tm, n), lambda i, j, kk: (i, 0)),
        _sds((t, n), out_dtype), (t // tm, 1, nk), NN, nk, (tm, n))


def _mm_nt(name, a, w, l, tn, out_dtype, comm=None):
    t, n = a.shape
    k = w.shape[1]
    tm = _tm(t)
    return _matmul(
        name, a, w,
        pl.BlockSpec((tm, n), lambda i, j, kk: (i, 0)),
        pl.BlockSpec((None, tn, n), lambda i, j, kk: (l, j, 0)),
        pl.BlockSpec((tm, tn), lambda i, j, kk: (i, j)),
        _sds((t, k), out_dtype), (t // tm, k // tn, 1), NT, 1, None, comm=comm)


def _mm_tn(name, a, g, tko, l, fill):
    t, k = a.shape
    n = g.shape[1]
    tt = _tt(t)
    nt = t // tt
    return _matmul(
        name, a, g,
        pl.BlockSpec((tt, tko), lambda i, j, kk: (kk, i)),
        pl.BlockSpec((tt, n), lambda i, j, kk: (kk, 0)),
        pl.BlockSpec((None, tko, n), lambda i, j, kk: (l, i, 0)),
        _sds((DEPTH, k, n), BF16), (k // tko, 1, nt), TN, nt, (tko, n), fill)


def _row_spec(width, col=0):
    return pl.BlockSpec((TOK, width), lambda i: (i, col))


def _vec_spec(width):
    return pl.BlockSpec((1, width), lambda i: (0, 0))


def _rms(x):
    return lax.rsqrt(jnp.mean(x * x, axis=-1, keepdims=True) + EPS)


def _norm_fwd(name, x, g, comm=None):
    t = x.shape[0]

    def body(x_ref, g_ref, h_ref):
        xv = x_ref[...]
        h_ref[...] = (xv * _rms(xv) * g_ref[...]).astype(BF16)

    out, moved = _pcall(body, name, (t // TOK,), [_row_spec(D_MODEL), _vec_spec(D_MODEL)],
                        _row_spec(D_MODEL), _sds((t, D_MODEL), BF16), [], _cparams("arbitrary"),
                        (x, g), comm)
    return out if comm is None else (out, moved)


ROWS = 16
ROW_UNROLL = 8


def _rows(k):
    return pl.ds(pl.multiple_of(k * ROWS, ROWS), ROWS)


def _strips(step, init):
    def group(j, carry):
        for u in range(ROW_UNROLL):
            carry = step(j * ROW_UNROLL + u, carry)
        return carry

    return lax.fori_loop(0, TOK // (ROWS * ROW_UNROLL), group, init)


def _fold_rows(x):
    return x[0:8] + x[8:16]


def _accumulate(ref, part):
    total = jnp.sum(part, axis=0, keepdims=True)

    @pl.when(pl.program_id(0) == 0)
    def _():
        ref[...] = total

    @pl.when(pl.program_id(0) > 0)
    def _():
        ref[...] += total


def _norm_bwd_rows(d, mv, g):
    r = _rms(mv)
    n = mv * r
    dn = d * g
    return r * (dn - n * jnp.mean(dn * n, axis=-1, keepdims=True)), d * n


def _post_pre_fwd(name, xres, m, g_post, g_pre, comm=None):
    t = xres.shape[0]

    def body(x_ref, m_ref, gp_ref, gn_ref, x1_ref, h_ref):
        def strip(k, c):
            rows = _rows(k)
            mv = m_ref[rows, :]
            x1 = x_ref[rows, :] + mv * _rms(mv) * gp_ref[...]
            x1_ref[rows, :] = x1
            h_ref[rows, :] = (x1 * _rms(x1) * gn_ref[...]).astype(BF16)
            return c

        _strips(strip, 0)

    outs, moved = _pcall(
        body, name, (t // TOK,),
        [_row_spec(D_MODEL), _row_spec(D_MODEL), _vec_spec(D_MODEL), _vec_spec(D_MODEL)],
        [_row_spec(D_MODEL), _row_spec(D_MODEL)],
        [_sds((t, D_MODEL), F32), _sds((t, D_MODEL), BF16)], [], _cparams("arbitrary"),
        (xres, m, g_post, g_pre), comm)
    return outs if comm is None else (*outs, moved)


def _tail(name, xres, m, g_post, target):
    t = xres.shape[0]

    def body(x_ref, m_ref, g_ref, t_ref, dy_ref, dm_ref, dg_ref, l_ref):
        def strip(k, carry):
            rows = _rows(k)
            mv = m_ref[rows, :]
            e = x_ref[rows, :] + mv * _rms(mv) * g_ref[...] - t_ref[rows, :]
            dy = e * (1.0 / D_MODEL)
            dy_ref[rows, :] = dy
            dm, dgn = _norm_bwd_rows(dy, mv, g_ref[...])
            dm_ref[rows, :] = dm.astype(BF16)
            return carry[0] + _fold_rows(dgn), carry[1] + _fold_rows(e * e)

        zero = jnp.zeros((8, D_MODEL), F32)
        dg, sq = _strips(strip, (zero, zero))
        _accumulate(dg_ref, dg)
        _accumulate(l_ref, jnp.sum(sq, axis=1, keepdims=True))

    dy, dm, dg, sq = pl.pallas_call(
        body, name=name, grid=(t // TOK,),
        in_specs=[_row_spec(D_MODEL), _row_spec(D_MODEL), _vec_spec(D_MODEL), _row_spec(D_MODEL)],
        out_specs=[_row_spec(D_MODEL), _row_spec(D_MODEL), _vec_spec(D_MODEL),
                   pl.BlockSpec((1, 1), lambda i: (0, 0))],
        out_shape=[_sds((t, D_MODEL), F32), _sds((t, D_MODEL), BF16), _sds((1, D_MODEL), F32),
                   _sds((1, 1), F32)],
        compiler_params=_cparams("arbitrary"))(xres, m, g_post, target)
    return dy, dm, dg, sq[0, 0] * (0.5 / D_MODEL)


def _pre_post_bwd(name, dh, xin, dxo, g_pre, m, g_post, comm=None):
    t = dh.shape[0]

    def body(dh_ref, x_ref, d_ref, gq_ref, m_ref, gp_ref, dx_ref, dgq_ref, dm_ref, dgp_ref):
        def strip(k, carry):
            rows = _rows(k)
            dxin, dgq = _norm_bwd_rows(dh_ref[rows, :], x_ref[rows, :], gq_ref[...])
            dx = d_ref[rows, :] + dxin
            dx_ref[rows, :] = dx
            dm, dgp = _norm_bwd_rows(dx, m_ref[rows, :], gp_ref[...])
            dm_ref[rows, :] = dm.astype(BF16)
            return carry[0] + _fold_rows(dgq), carry[1] + _fold_rows(dgp)

        zero = jnp.zeros((8, D_MODEL), F32)
        dgq, dgp = _strips(strip, (zero, zero))
        _accumulate(dgq_ref, dgq)
        _accumulate(dgp_ref, dgp)

    outs, moved = _pcall(
        body, name, (t // TOK,),
        [_row_spec(D_MODEL), _row_spec(D_MODEL), _row_spec(D_MODEL), _vec_spec(D_MODEL),
         _row_spec(D_MODEL), _vec_spec(D_MODEL)],
        [_row_spec(D_MODEL), _vec_spec(D_MODEL), _row_spec(D_MODEL), _vec_spec(D_MODEL)],
        [_sds((t, D_MODEL), F32), _sds((1, D_MODEL), F32), _sds((t, D_MODEL), BF16),
         _sds((1, D_MODEL), F32)], [], _cparams("arbitrary"),
        (dh, xin, dxo, g_pre, m, g_post), comm)
    return outs if comm is None else (*outs, moved)


def _norm_pre_bwd(name, dh, xin, dxo, g, comm=None):
    t = dh.shape[0]

    def body(dh_ref, x_ref, d_ref, g_ref, dx_ref, dg_ref):
        xv = x_ref[...]
        dhv = dh_ref[...]
        r = _rms(xv)
        n = xv * r
        dn = dhv * g_ref[...]
        dx_ref[...] = d_ref[...] + r * (dn - n * jnp.mean(dn * n, axis=-1, keepdims=True))
        part = jnp.sum(dhv * n, axis=0, keepdims=True)

        @pl.when(pl.program_id(0) == 0)
        def _():
            dg_ref[...] = part

        @pl.when(pl.program_id(0) > 0)
        def _():
            dg_ref[...] += part

    out, moved = _pcall(
        body, name, (t // TOK,),
        [_row_spec(D_MODEL), _row_spec(D_MODEL), _row_spec(D_MODEL), _vec_spec(D_MODEL)],
        [_row_spec(D_MODEL), _vec_spec(D_MODEL)],
        [_sds((t, D_MODEL), F32), _sds((1, D_MODEL), F32)], [], _cparams("arbitrary"),
        (dh, xin, dxo, g), comm)
    return out if comm is None else (*out, moved)


def _gate_fwd(name, proj, b_gate, ya, yb):
    t = proj.shape[0]

    def body(ga_ref, gb_ref, b_ref, ya_ref, yb_ref, z_ref):
        def strip(k, c):
            rows = _rows(k)
            sa = jax.nn.sigmoid(ga_ref[rows, :].astype(F32) + b_ref[:, :D_MODEL])
            sb = jax.nn.sigmoid(gb_ref[rows, :].astype(F32) + b_ref[:, D_MODEL:])
            z_ref[rows, :] = (sa * ya_ref[rows, :].astype(F32)
                              + sb * yb_ref[rows, :].astype(F32)).astype(BF16)
            return c

        _strips(strip, 0)

    return pl.pallas_call(
        body, name=name, grid=(t // TOK,),
        in_specs=[_row_spec(D_MODEL, 2), _row_spec(D_MODEL, 3), _vec_spec(2 * D_MODEL),
                  _row_spec(D_MODEL), _row_spec(D_MODEL)],
        out_specs=_row_spec(D_MODEL), out_shape=_sds((t, D_MODEL), BF16),
        compiler_params=_cparams("parallel"))(proj, proj, b_gate, ya, yb)


def _gate_bwd(name, dz, proj, b_gate, ya, yb):
    t = proj.shape[0]

    def body(dz_ref, ga_ref, gb_ref, b_ref, ya_ref, yb_ref, dya_ref, dyb_ref, dg_ref, db_ref):
        def strip(k, carry):
            rows = _rows(k)
            dzv = dz_ref[rows, :].astype(F32)
            sa = jax.nn.sigmoid(ga_ref[rows, :].astype(F32) + b_ref[:, :D_MODEL])
            sb = jax.nn.sigmoid(gb_ref[rows, :].astype(F32) + b_ref[:, D_MODEL:])
            dya_ref[rows, :] = (dzv * sa).astype(BF16)
            dyb_ref[rows, :] = (dzv * sb).astype(BF16)
            dga = dzv * ya_ref[rows, :].astype(F32) * sa * (1.0 - sa)
            dgb = dzv * yb_ref[rows, :].astype(F32) * sb * (1.0 - sb)
            dg_ref[rows, :D_MODEL] = dga.astype(BF16)
            dg_ref[rows, D_MODEL:] = dgb.astype(BF16)
            return carry[0] + _fold_rows(dga), carry[1] + _fold_rows(dgb)

        zero = jnp.zeros((8, D_MODEL), F32)
        pa, pb = _strips(strip, (zero, zero))
        _accumulate(db_ref.at[:, :D_MODEL], pa)
        _accumulate(db_ref.at[:, D_MODEL:], pb)

    return pl.pallas_call(
        body, name=name, grid=(t // TOK,),
        in_specs=[_row_spec(D_MODEL), _row_spec(D_MODEL, 2), _row_spec(D_MODEL, 3),
                  _vec_spec(2 * D_MODEL), _row_spec(D_MODEL), _row_spec(D_MODEL)],
        out_specs=[_row_spec(D_MODEL), _row_spec(D_MODEL), _row_spec(2 * D_MODEL),
                   _vec_spec(2 * D_MODEL)],
        out_shape=[_sds((t, D_MODEL), BF16), _sds((t, D_MODEL), BF16),
                   _sds((t, 2 * D_MODEL), BF16), _sds((1, 2 * D_MODEL), F32)],
        compiler_params=_cparams("arbitrary"))(dz, proj, proj, b_gate, ya, yb)


def _head_masks():
    lane = lax.broadcasted_iota(jnp.int32, (1, 2 * HEAD_DIM), 1)
    return lane < HEAD_DIM


BAND_ROWS = 2 * ATT_BLK + CHUNK


def _fill_band(band, prev_ref, cur_ref):
    band[0:ATT_BLK, :] = prev_ref[...]
    band[ATT_BLK:2 * ATT_BLK, :] = cur_ref[...]
    band[2 * ATT_BLK:, :] = jnp.zeros((CHUNK, ATTN_W), BF16)


def _pair_rows(x2, low):
    zero = jnp.zeros_like(x2)
    return jnp.concatenate([jnp.where(low, x2, zero), jnp.where(low, zero, x2)], axis=0)


def _pair_diag(o2, low):
    return jnp.where(low, o2[0:CHUNK, :], o2[CHUNK:, :])


N_PAIRS = HEADS // 2
SM_STRIP = 32
N_STRIPS = BAND_PAD // SM_STRIP
NEG = -1e30


def _fold8(x, op):
    return op(op(x[0:8], x[8:16]), op(x[16:24], x[24:32]))


def _strip(k):
    return pl.ds(pl.multiple_of(k * SM_STRIP, SM_STRIP), SM_STRIP)


def _band_probs(k2, qcat, bias_t, first_key):
    kpos = lax.broadcasted_iota(jnp.int32, (BAND_PAD, 1), 0)
    st = lax.dot_general(k2, qcat, (NT, ((), ())), preferred_element_type=F32)
    st = jnp.where(kpos + first_key >= 0, st + bias_t, NEG)
    e = jnp.exp(st - jnp.max(st, axis=0, keepdims=True))
    return e * (1.0 / jnp.sum(e, axis=0, keepdims=True))


def _attn_specs(nblk):
    cur = lambda col: pl.BlockSpec((ATT_BLK, ATTN_W), lambda s: (jnp.minimum(s, nblk - 1), col))
    prev = lambda col: pl.BlockSpec(
        (ATT_BLK, ATTN_W), lambda s: (jnp.maximum(jnp.minimum(s, nblk - 1) - 1, 0), col))
    return cur, prev


def _attn_fwd(name, proj, bias, comm=None):
    t = proj.shape[0]
    nblk = t // ATT_BLK
    cur, prev = _attn_specs(nblk)

    def body(q_ref, kp_ref, kc_ref, vp_ref, vc_ref, b_ref, o_ref, p_ref, kband, vband):
        s = pl.program_id(0)
        _fill_band(kband, kp_ref, kc_ref)
        _fill_band(vband, vp_ref, vc_ref)
        low = _head_masks()

        def chunk(ci):
            r0 = pl.multiple_of(ci * CHUNK, CHUNK)
            for hp in range(N_PAIRS):
                cols = slice(hp * 128, (hp + 1) * 128)
                qcat = _pair_rows(q_ref[pl.ds(r0, CHUNK), cols] * ATTN_SCALE, low)
                p = _band_probs(kband[pl.ds(r0, BAND_PAD), cols], qcat, b_ref[hp],
                                (s * 8 - 8 + ci) * CHUNK).astype(BF16)
                p_ref[ci, hp] = p
                o2 = lax.dot_general(p, vband[pl.ds(r0, BAND_PAD), cols],
                                     (TN, ((), ())), preferred_element_type=F32)
                o_ref[pl.ds(r0, CHUNK), cols] = _pair_diag(o2, low).astype(BF16)

        def two_chunks(j, carry):
            chunk(2 * j)
            chunk(2 * j + 1)
            return carry

        lax.fori_loop(0, 4, two_chunks, 0)

    outs, moved = _pcall(
        body, name, (nblk,),
        [cur(0), prev(1), cur(1), prev(2), cur(2),
         pl.BlockSpec((N_PAIRS, BAND_PAD, 128), lambda s: (0, 0, 0))],
        [pl.BlockSpec((ATT_BLK, ATTN_W), lambda s: (s, 0)),
         pl.BlockSpec((8, N_PAIRS, BAND_PAD, 128), lambda s: (s, 0, 0, 0))],
        [_sds((t, ATTN_W), BF16), _sds((t // CHUNK, N_PAIRS, BAND_PAD, 128), BF16)],
        [pltpu.VMEM((BAND_ROWS, ATTN_W), BF16), pltpu.VMEM((BAND_ROWS, ATTN_W), BF16)],
        _cparams("arbitrary"), (proj, proj, proj, proj, proj, bias), comm)
    return outs if comm is None else (*outs, moved)


def _attn_bwd(name, proj, datt, probs, comm=None):
    t = proj.shape[0]
    nblk = t // ATT_BLK
    cur, prev = _attn_specs(nblk)
    late = pl.BlockSpec((ATT_BLK, 3 * ATTN_W), lambda s: (jnp.maximum(s - 1, 0), 0))

    def body(q_ref, kp_ref, kc_ref, vp_ref, vc_ref, do_ref, p_ref,
             dqkv_ref, db_ref, kband, vband, dkacc, dvacc,
             dp_ref, dsb_ref, qc_ref, dc_ref, dq_ref, dq_held):
        s = pl.program_id(0)

        @pl.when(s == 0)
        def _():
            dkacc[...] = jnp.zeros_like(dkacc)
            dvacc[...] = jnp.zeros_like(dvacc)
            db_ref[...] = jnp.zeros_like(db_ref)
            dq_ref[...] = jnp.zeros_like(dq_ref)

        @pl.when(s < nblk)
        def _():
            _fill_band(kband, kp_ref, kc_ref)
            _fill_band(vband, vp_ref, vc_ref)
            low = _head_masks()

            def chunk(ci, carry):
                r0 = pl.multiple_of(ci * CHUNK, CHUNK)
                for hp in range(N_PAIRS):
                    cols = slice(hp * 128, (hp + 1) * 128)
                    qc_ref[hp] = _pair_rows(q_ref[pl.ds(r0, CHUNK), cols] * ATTN_SCALE, low)
                    dc_ref[hp] = _pair_rows(do_ref[pl.ds(r0, CHUNK), cols], low)
                    dp_ref[hp] = lax.dot_general(vband[pl.ds(r0, BAND_PAD), cols], dc_ref[hp],
                                                 (NT, ((), ())), preferred_element_type=F32)

                def sums(j, acc):
                    for u in range(2):
                        rows = _strip(2 * j + u)
                        acc = tuple(acc[hp] + _fold8(p_ref[ci, hp, rows, :].astype(F32)
                                                     * dp_ref[hp, rows, :], jnp.add)
                                    for hp in range(N_PAIRS))
                    return acc

                acc = lax.fori_loop(0, N_STRIPS // 2, sums, (jnp.zeros((8, 128), F32),) * N_PAIRS)
                delta = [jnp.sum(a, axis=0, keepdims=True) for a in acc]

                def grads(j, c):
                    for u in range(2):
                        rows = _strip(2 * j + u)
                        for hp in range(N_PAIRS):
                            ds = (p_ref[ci, hp, rows, :].astype(F32)
                                  * (dp_ref[hp, rows, :] - delta[hp]))
                            db_ref[hp, rows, :] += ds
                            dsb_ref[hp, rows, :] = ds.astype(BF16)
                    return c

                lax.fori_loop(0, N_STRIPS // 2, grads, 0)
                for hp in range(N_PAIRS):
                    cols = slice(hp * 128, (hp + 1) * 128)
                    dq2 = lax.dot_general(dsb_ref[hp], kband[pl.ds(r0, BAND_PAD), cols],
                                          (TN, ((), ())), preferred_element_type=F32)
                    dq_ref[pl.ds(r0, CHUNK), cols] = (_pair_diag(dq2, low) * ATTN_SCALE).astype(BF16)
                    dkacc[pl.ds(r0, BAND_PAD), cols] += jnp.dot(dsb_ref[hp], qc_ref[hp],
                                                               preferred_element_type=F32)
                    dvacc[pl.ds(r0, BAND_PAD), cols] += jnp.dot(p_ref[ci, hp], dc_ref[hp],
                                                               preferred_element_type=F32)
                return carry

            dq_held[...] = dq_ref[...]
            lax.fori_loop(0, 8, chunk, 0)

        @pl.when(s == nblk)
        def _():
            dq_held[...] = dq_ref[...]

        dqkv_ref[:, 0:ATTN_W] = dq_held[...]
        dqkv_ref[:, ATTN_W:2 * ATTN_W] = dkacc[0:ATT_BLK, :].astype(BF16)
        dqkv_ref[:, 2 * ATTN_W:] = dvacc[0:ATT_BLK, :].astype(BF16)
        dkacc[0:ATT_BLK, :] = dkacc[ATT_BLK:2 * ATT_BLK, :]
        dvacc[0:ATT_BLK, :] = dvacc[ATT_BLK:2 * ATT_BLK, :]
        dkacc[ATT_BLK:, :] = jnp.zeros((ATT_BLK + CHUNK, ATTN_W), F32)
        dvacc[ATT_BLK:, :] = jnp.zeros((ATT_BLK + CHUNK, ATTN_W), F32)

    outs, moved = _pcall(
        body, name, (nblk + 1,),
        [cur(0), prev(1), cur(1), prev(2), cur(2),
         pl.BlockSpec((ATT_BLK, ATTN_W), lambda s: (jnp.minimum(s, nblk - 1), 0)),
         pl.BlockSpec((8, N_PAIRS, BAND_PAD, 128), lambda s: (jnp.minimum(s, nblk - 1), 0, 0, 0))],
        [late, pl.BlockSpec((HEADS // 2, BAND_PAD, 128), lambda s: (0, 0, 0))],
        [_sds((t, 3 * ATTN_W), BF16), _sds((HEADS // 2, BAND_PAD, 128), F32)],
        [pltpu.VMEM((BAND_ROWS, ATTN_W), BF16), pltpu.VMEM((BAND_ROWS, ATTN_W), BF16),
         pltpu.VMEM((BAND_ROWS, ATTN_W), F32), pltpu.VMEM((BAND_ROWS, ATTN_W), F32),
         pltpu.VMEM((N_PAIRS, BAND_PAD, 128), F32), pltpu.VMEM((N_PAIRS, BAND_PAD, 128), BF16),
         pltpu.VMEM((N_PAIRS, 2 * CHUNK, 128), BF16), pltpu.VMEM((N_PAIRS, 2 * CHUNK, 128), BF16),
         pltpu.VMEM((ATT_BLK, ATTN_W), BF16), pltpu.VMEM((ATT_BLK, ATTN_W), BF16)],
        _cparams("arbitrary"), (proj, proj, proj, proj, proj, datt, probs), comm)
    return outs if comm is None else (*outs, moved)


def _diag_onehot(rel_rows):
    d0 = lax.broadcasted_iota(jnp.int32, (BIAS_LANES, BIAS_LANES), 0)
    d1 = lax.broadcasted_iota(jnp.int32, (BIAS_LANES, BIAS_LANES), 1)
    m, n = (d0, d1) if rel_rows else (d1, d0)
    hit = (m == jnp.minimum(BAND - 1 + MAX_REL - n, 2 * MAX_REL)) & (n < BAND + CHUNK - 1)
    return jnp.where(hit, 1.0, 0.0).astype(F32)


def _bias_table(name, rel_bias_l):
    rel_pad = jnp.pad(rel_bias_l, ((0, 0), (0, BIAS_LANES - N_REL)))

    def body(r_ref, o_ref):
        diag = jnp.dot(r_ref[...], _diag_onehot(True), preferred_element_type=F32,
                       precision=lax.Precision.HIGHEST)
        rowid = lax.broadcasted_iota(jnp.int32, (8, BIAS_LANES), 0)
        lane = lax.broadcasted_iota(jnp.int32, (8, BIAS_LANES), 1)
        for h in range(HEADS):
            d8 = jnp.broadcast_to(diag[h:h + 1, :], (8, BIAS_LANES))
            slab0 = pltpu.roll(d8, BIAS_LANES - CHUNK + 1, axis=1)
            for b in range(1, 8):
                slab0 = jnp.where(rowid == b, pltpu.roll(d8, BIAS_LANES - CHUNK + 1 + b, axis=1),
                                  slab0)
            for a in range(8):
                slab = slab0 if a == 0 else pltpu.roll(slab0, 8 * a, axis=1)
                o_ref[h * CHUNK + 8 * a:h * CHUNK + 8 * a + 8, :] = jnp.where(lane < BAND, slab, NEG)

    tab = pl.pallas_call(
        body, name=name,
        in_specs=[pl.BlockSpec(memory_space=pltpu.VMEM)],
        out_specs=pl.BlockSpec(memory_space=pltpu.VMEM),
        out_shape=_sds((HEADS * CHUNK, BIAS_LANES), F32),
    )(rel_pad)
    tab = tab.reshape(HEADS // 2, 2, CHUNK, BIAS_LANES)
    return jnp.transpose(tab, (0, 3, 1, 2)).reshape(HEADS // 2, BIAS_LANES, 2 * CHUNK)


def _bias_fold(name, dbias_t):
    rows = HEADS * CHUNK
    dbias = jnp.transpose(dbias_t.reshape(HEADS // 2, BIAS_LANES, 2, CHUNK), (0, 2, 3, 1))

    def body(d_ref, o_ref):
        rowid = lax.broadcasted_iota(jnp.int32, (8, BIAS_LANES), 0)
        diags = []
        for h in range(HEADS):
            acc = d_ref[h * CHUNK + 56:h * CHUNK + 64, :]
            for a in range(7):
                slab = d_ref[h * CHUNK + 8 * a:h * CHUNK + 8 * a + 8, :]
                acc = acc + pltpu.roll(slab, 56 - 8 * a, axis=1)
            tot = jnp.where(rowid == 7, acc, 0.0)
            for b in range(7):
                tot = tot + jnp.where(rowid == b, pltpu.roll(acc, 7 - b, axis=1), 0.0)
            diags.append(jnp.sum(tot, axis=0, keepdims=True))
        diag = jnp.concatenate(diags, axis=0)
        o_ref[...] = jnp.dot(diag, _diag_onehot(False), preferred_element_type=F32,
                             precision=lax.Precision.HIGHEST)

    return pl.pallas_call(
        body, name=name,
        in_specs=[pl.BlockSpec(memory_space=pltpu.VMEM)],
        out_specs=pl.BlockSpec(memory_space=pltpu.VMEM),
        out_shape=_sds((HEADS, BIAS_LANES), F32),
    )(dbias.reshape(rows, BIAS_LANES))


def _inv_counts(i):
    trow = lax.broadcasted_iota(jnp.int32, (TOK + HALO, 1), 0) + i * TOK
    return [1.0 / jnp.minimum(trow + 1, w).astype(F32) for w in POOL_WINDOWS]


def _pool_fwd(name, proj, wg, scale, comm=None):
    t = proj.shape[0]
    hb = TOK // HALO

    def body(u_ref, up_ref, wg_ref, sc_ref, pooled_ref, mixed_ref, b0, b1, b2, b3):
        i = pl.program_id(0)
        halo = up_ref[...].astype(F32)
        b0[0:HALO, :] = jnp.where(i == 0, jnp.zeros_like(halo), halo)
        b0[HALO:, :] = u_ref[...].astype(F32)
        n = TOK + HALO
        b1[8:n, :] = b0[8:n, :] + b0[7:n - 1, :]
        b2[16:n, 128:] = b1[16:n, 128:] + b1[14:n - 2, 128:]
        b3[24:n, 256:] = b2[24:n, 256:] + b2[20:n - 4, 256:]
        wins = [b1[HALO:n, 0:128], b2[HALO:n, 128:256], b3[HALO:n, 256:384],
                b3[HALO:n, 384:512] + b3[HALO - 8:n - 8, 384:512]]
        inv = _inv_counts(i)
        for g in range(4):
            cols = slice(g * POOL_GD, (g + 1) * POOL_GD)
            pooled = (wins[g] * inv[g][0:TOK] - b0[HALO:n, cols]).astype(BF16)
            pooled_ref[:, cols] = pooled
            pre = jnp.dot(pooled, wg_ref[g], preferred_element_type=F32)
            mixed_ref[:, cols] = (pre * sc_ref[:, cols]).astype(BF16)

    buf = pltpu.VMEM((TOK + HALO, POOL_W), F32)
    outs, moved = _pcall(
        body, name, (t // TOK,),
        [_row_spec(POOL_W, 3),
         pl.BlockSpec((HALO, POOL_W), lambda i: (jnp.maximum(i * hb - 1, 0), 3)),
         pl.BlockSpec((4, POOL_GD, POOL_GD), lambda i: (0, 0, 0)), _vec_spec(POOL_W)],
        [_row_spec(POOL_W), _row_spec(POOL_W)],
        [_sds((t, POOL_W), BF16), _sds((t, POOL_W), BF16)], [buf, buf, buf, buf],
        _cparams("arbitrary"), (proj, proj, wg, scale), comm)
    return outs if comm is None else (*outs, moved)


def _pool_bwd(name, dmixed, pooled, wg, scale, comm=None):
    t = dmixed.shape[0]
    nt = t // TOK
    hb = TOK // HALO

    def body(dm_ref, dmn_ref, p_ref, wg_ref, sc_ref, du_ref, dwg_ref, dsc_ref, c0, c1, c2, c3):
        i = pl.program_id(0)

        @pl.when(i == 0)
        def _():
            dwg_ref[...] = jnp.zeros_like(dwg_ref)
            dsc_ref[...] = jnp.zeros_like(dsc_ref)

        n = TOK + HALO
        inv = _inv_counts(i)
        dmv = dm_ref[...].astype(F32)
        dmn = dmn_ref[...].astype(F32)
        dmn = jnp.where(i == nt - 1, jnp.zeros_like(dmn), dmn)
        for g in range(4):
            cols = slice(g * POOL_GD, (g + 1) * POOL_GD)
            scg = sc_ref[:, cols]
            pg = p_ref[:, cols]
            dpre = (dmv[:, cols] * scg).astype(BF16)
            dpre_n = (dmn[:, cols] * scg).astype(BF16)
            pre = jnp.dot(pg, wg_ref[g], preferred_element_type=F32)
            dsc_ref[:, cols] += jnp.sum(dmv[:, cols] * pre, axis=0, keepdims=True)
            dwg_ref[g] += lax.dot_general(pg, dpre, (TN, ((), ())), preferred_element_type=F32)
            dpool = lax.dot_general(dpre, wg_ref[g], (NT, ((), ())), preferred_element_type=F32)
            dpool_n = lax.dot_general(dpre_n, wg_ref[g], (NT, ((), ())),
                                      preferred_element_type=F32)
            c0[0:TOK, cols] = dpool
            c0[TOK:n, cols] = dpool_n
            c1[0:TOK, cols] = dpool * inv[g][0:TOK]
            c1[TOK:n, cols] = dpool_n * inv[g][TOK:n]
        c2[0:n - 8, :] = c1[0:n - 8, :] + c1[1:n - 7, :]
        c3[0:n - 16, 128:] = c2[0:n - 16, 128:] + c2[2:n - 14, 128:]
        c1[0:n - 24, 256:] = c3[0:n - 24, 256:] + c3[4:n - 20, 256:]
        wins = [c2[0:TOK, 0:128], c3[0:TOK, 128:256], c1[0:TOK, 256:384],
                c1[0:TOK, 384:512] + c1[8:TOK + 8, 384:512]]
        for g in range(4):
            cols = slice(g * POOL_GD, (g + 1) * POOL_GD)
            du_ref[:, cols] = (wins[g] - c0[0:TOK, cols]).astype(BF16)

    buf = pltpu.VMEM((TOK + HALO, POOL_W), F32)
    outs, moved = _pcall(
        body, name, (nt,),
        [_row_spec(POOL_W),
         pl.BlockSpec((HALO, POOL_W), lambda i: (jnp.minimum((i + 1) * hb, nt * hb - 1), 0)),
         _row_spec(POOL_W), pl.BlockSpec((4, POOL_GD, POOL_GD), lambda i: (0, 0, 0)),
         _vec_spec(POOL_W)],
        [_row_spec(POOL_W), pl.BlockSpec((4, POOL_GD, POOL_GD), lambda i: (0, 0, 0)),
         _vec_spec(POOL_W)],
        [_sds((t, POOL_W), BF16), _sds((4, POOL_GD, POOL_GD), F32), _sds((1, POOL_W), F32)],
        [buf, buf, buf, buf], _cparams("arbitrary"), (dmixed, dmixed, pooled, wg, scale), comm)
    return outs if comm is None else (*outs, moved)


GELU_C = math.sqrt(2.0 / math.pi)


GELU_K = 0.044715


def _gelu_parts(x):
    x2 = x * x
    s = 0.5 + 0.5 * jnp.tanh(x * (GELU_C + (GELU_C * GELU_K) * x2))
    return x * s, s, x2


def _gelu(x):
    return _gelu_parts(x)[0]


def _gelu_and_grad(x):
    g, s, x2 = _gelu_parts(x)
    return g, s + g * (1.0 - s) * ((2 * GELU_C) + (6 * GELU_C * GELU_K) * x2)


def _taps(buf, r, rows):
    a = buf[pl.ds(r, rows + 8), :]
    return a[8:], pltpu.roll(a, 1, axis=0)[8:], pltpu.roll(a, 2, axis=0)[8:]


def _conv(taps, w_ref, b_ref):
    return b_ref[...] + w_ref[2:3, :] * taps[0] + w_ref[1:2, :] * taps[1] + w_ref[0:1, :] * taps[2]


def _stage(dst, prev_ref, cur_ref, next_ref, first, last):
    rows = cur_ref.shape[0]
    h = prev_ref[...].astype(F32)
    dst[0:8, :] = jnp.where(first, jnp.zeros_like(h), h)
    dst[8:8 + rows, :] = cur_ref[...].astype(F32)
    if next_ref is not None:
        h = next_ref[...].astype(F32)
        dst[8 + rows:, :] = jnp.where(last, jnp.zeros_like(h), h)


FWD_STRIP = 32
BWD_STRIP = 16


def _ffn_gate_fwd(name, hu, conv_w, conv_b, comm=None):
    t = hu.shape[0]
    ncol = D_FF // FF_COL
    hb = FF_TOK // 8

    def tile(off):
        return pl.BlockSpec((FF_TOK, FF_COL), lambda i, j: (i, j + off))

    def halo(off):
        return pl.BlockSpec((8, FF_COL), lambda i, j: (jnp.maximum(i * hb - 1, 0), j + off))

    def wspec(off):
        return pl.BlockSpec((3, FF_COL), lambda i, j: (0, j + off))

    def bspec(off):
        return pl.BlockSpec((1, FF_COL), lambda i, j: (0, j + off))

    def body(v_ref, vp_ref, g_ref, gp_ref, wv_ref, wg_ref, bv_ref, bg_ref, a_ref, hc_ref, vb, gb):
        first = pl.program_id(0) == 0
        _stage(vb, vp_ref, v_ref, None, first, None)
        _stage(gb, gp_ref, g_ref, None, first, None)

        def strip(k, carry):
            for u in range(2):
                r = pl.multiple_of((2 * k + u) * FWD_STRIP, FWD_STRIP)
                val = _conv(_taps(vb, r, FWD_STRIP), wv_ref, bv_ref)
                gate = _conv(_taps(gb, r, FWD_STRIP), wg_ref, bg_ref)
                a_ref[pl.ds(r, FWD_STRIP), :] = (_gelu(gate) * val).astype(BF16)
                hc_ref[0, pl.ds(r, FWD_STRIP), :] = val.astype(BF16)
                hc_ref[1, pl.ds(r, FWD_STRIP), :] = gate.astype(BF16)
            return carry

        lax.fori_loop(0, FF_TOK // (2 * FWD_STRIP), strip, 0)

    buf = pltpu.VMEM((FF_TOK + 8, FF_COL), F32)
    outs, moved = _pcall(
        body, name, (t // FF_TOK, ncol),
        [tile(0), halo(0), tile(ncol), halo(ncol), wspec(0), wspec(ncol), bspec(0), bspec(ncol)],
        [pl.BlockSpec((FF_TOK, FF_COL), lambda i, j: (i, j)),
         pl.BlockSpec((2, FF_TOK, FF_COL), lambda i, j: (0, i, j))],
        [_sds((t, D_FF), BF16), _sds((2, t, D_FF), BF16)], [buf, buf],
        _cparams("arbitrary", "arbitrary"),
        (hu, hu, hu, hu, conv_w, conv_w, conv_b, conv_b), comm)
    return outs if comm is None else (*outs, moved)


def _ffn_gate_bwd(name, da, hu, hc, conv_w, comm=None):
    t = hu.shape[0]
    nt = t // FF_TOK
    ncol = D_FF // FF_COL
    hb = FF_TOK // 8

    def tile(off):
        return pl.BlockSpec((FF_TOK, FF_COL), lambda j, i: (i, j + off))

    def nxt_rows(i):
        return jnp.minimum((i + 1) * hb, nt * hb - 1)

    def wspec(off):
        return pl.BlockSpec((3, FF_COL), lambda j, i: (0, j + off))

    def body(da_ref, dan_ref, v_ref, g_ref, hc_ref, hcn_ref, wv_ref, wg_ref,
             dh_ref, dwv_ref, dwg_ref):
        i = pl.program_id(1)
        first, last = i == 0, i == nt - 1

        @pl.when(first)
        def _():
            dwv_ref[...] = jnp.zeros_like(dwv_ref)
            dwg_ref[...] = jnp.zeros_like(dwg_ref)

        def grads(dav, val, gate):
            g, dg = _gelu_and_grad(gate.astype(F32))
            dav = dav.astype(F32)
            return dav * g, dav * val.astype(F32) * dg

        def fold(x):
            return x[0:8] + x[8:16]

        def strip(j, carry):
            for u in range(2):
                carry = one_strip(2 * j + u, carry)
            return carry

        def one_strip(k, carry):
            r = pl.multiple_of(FF_TOK - BWD_STRIP - k * BWD_STRIP, BWD_STRIP)
            rows = pl.ds(r, BWD_STRIP)
            dval, dgate = grads(da_ref[rows, :], hc_ref[0, rows, :], hc_ref[1, rows, :])
            new = (dval[0:8], dgate[0:8])
            for half, (d, below, h_ref, w_ref, dw_ref) in enumerate((
                    (dval, carry[0], v_ref, wv_ref, dwv_ref),
                    (dgate, carry[1], g_ref, wg_ref, dwg_ref))):
                e = jnp.concatenate([d, below], axis=0)
                e1 = pltpu.roll(e, BWD_STRIP + 7, axis=0)[0:BWD_STRIP]
                e2 = pltpu.roll(e, BWD_STRIP + 6, axis=0)[0:BWD_STRIP]
                dh = w_ref[2:3, :] * d + w_ref[1:2, :] * e1 + w_ref[0:1, :] * e2
                dh_ref[half, rows, :] = dh.astype(BF16)
                huv = h_ref[rows, :].astype(F32)
                dw_ref[0:8, :] += fold(e2 * huv)
                dw_ref[8:16, :] += fold(e1 * huv)
                dw_ref[16:24, :] += fold(d * huv)
                dw_ref[24:32, :] += fold(d)
            return new

        dan = dan_ref[...]
        dan = jnp.where(last, jnp.zeros_like(dan), dan)
        lax.fori_loop(0, FF_TOK // (2 * BWD_STRIP), strip, grads(dan, hcn_ref[0], hcn_ref[1]))

        @pl.when(last)
        def _():
            for dw_ref in (dwv_ref, dwg_ref):
                for q in range(4):
                    dw_ref[8 * q:8 * q + 1, :] = jnp.sum(dw_ref[8 * q:8 * q + 8, :], axis=0,
                                                         keepdims=True)

    acc = pl.BlockSpec((32, FF_COL), lambda j, i: (0, j))
    (dhu, dwv, dwg), moved = _pcall(
        body, name, (ncol, nt),
        [tile(0), pl.BlockSpec((8, FF_COL), lambda j, i: (nxt_rows(i), j)),
         tile(0), tile(ncol),
         pl.BlockSpec((2, FF_TOK, FF_COL), lambda j, i: (0, i, j)),
         pl.BlockSpec((2, 8, FF_COL), lambda j, i: (0, nxt_rows(i), j)),
         wspec(0), wspec(ncol)],
        [pl.BlockSpec((2, FF_TOK, FF_COL), lambda j, i: (0, i, j)), acc, acc],
        [_sds((2, t, D_FF), BF16), _sds((32, D_FF), F32), _sds((32, D_FF), F32)],
        [], _cparams("arbitrary", "arbitrary"),
        (da, da, hu, hu, hc, hc, conv_w, conv_w), comm)
    dconv = jnp.concatenate([dwv, dwg], axis=1).reshape(4, 8, 2 * D_FF)[:, 0]
    return (dhu, dconv) if comm is None else (dhu, dconv, moved)


def _mesh_pos():
    x, y, c = lax.axis_index("x"), lax.axis_index("y"), lax.axis_index("c")
    return x, y, c, [(1 - x, y), (x, 1 - y), (1 - x, 1 - y)]


def _remote(src, dst, send_sems, recv_sems, i, dev):
    return pltpu.make_async_remote_copy(src_ref=src, dst_ref=dst, send_sem=send_sems.at[i],
                                        recv_sem=recv_sems.at[i], device_id=dev,
                                        device_id_type=MESH)


def _mine(c, rows):
    return pl.ds(pl.multiple_of(c * (rows // 2), 16), rows // 2)


def _gather_send(shards, conv_shard, gathered, l):
    nbig = len(shards)
    with_conv = conv_shard is not None
    if gathered is None:
        ins = list(shards) + ([conv_shard] if with_conv else [])
        outs = [_sds((DEPTH, N_CHIPS) + s.shape[1:], s.dtype) for s in ins]
        alias = {}
    else:
        ins = list(shards) + list(gathered)
        outs = [_sds(g.shape, g.dtype) for g in gathered]
        alias = {nbig + k: k for k in range(nbig)}

    def copies(cin, cout, ssem, rsem):
        x, y, c, chips = _mesh_pos()
        me = 2 * x + y
        out = []
        for k in range(nbig):
            rows = shards[k].shape[1]
            for j, (cx, cy) in enumerate(chips):
                out.append(_remote(cin[k].at[l, _mine(c, rows)], cout[k].at[l, me, _mine(c, rows)],
                                   ssem, rsem, 4 * k + j, (cx, cy, c)))
            out.append(_remote(cin[k].at[l], cout[k].at[l, me], ssem, rsem, 4 * k + 3,
                               (x, y, 1 - c)))
        if with_conv:
            base = 4 * nbig
            for j, (cx, cy) in enumerate(chips):
                out.append(_remote(cin[nbig].at[c], cout[nbig].at[c, me], ssem, rsem, base + j,
                                   (cx, cy, c)))
            for ll in range(DEPTH):
                out.append(_remote(cin[nbig].at[ll], cout[nbig].at[ll, me], ssem, rsem,
                                   base + 3 + ll, (x, y, 1 - c)))
        return out

    return _Comm(ins, outs, copies, 4 * nbig + 5, alias)


def _gather_forward(gathered, nbig, rows, l):
    with_conv = len(gathered) > nbig
    alias = {k: k for k in range(len(gathered))}

    def copies(cin, cout, ssem, rsem):
        x, y, c, chips = _mesh_pos()
        out = []
        for k in range(nbig):
            for j, (cx, cy) in enumerate(chips):
                blk = cout[k].at[l, 2 * cx + cy, _mine(c, rows[k])]
                out.append(_remote(blk, blk, ssem, rsem, 3 * k + j, (x, y, 1 - c)))
        if with_conv:
            for j, (cx, cy) in enumerate(chips):
                blk = cout[nbig].at[c, 2 * cx + cy]
                out.append(_remote(blk, blk, ssem, rsem, 3 * nbig + j, (x, y, 1 - c)))
        return out

    return _Comm(gathered, [_sds(g.shape, g.dtype) for g in gathered], copies, 3 * nbig + 3, alias)


def _reduce_swap(grads, l):
    def copies(cin, cout, ssem, rsem):
        x, y, c, _ = _mesh_pos()
        return [_remote(cin[k].at[l, :, _mine(1 - c, g.shape[2])], cout[k], ssem, rsem, k,
                        (x, y, 1 - c)) for k, g in enumerate(grads)]

    outs = [_sds((N_CHIPS, g.shape[2] // 2, g.shape[3]), g.dtype) for g in grads]
    return _Comm(grads, outs, copies, len(grads))


def _reduce_scatter(sums):
    def copies(cin, cout, ssem, rsem):
        x, y, c, chips = _mesh_pos()
        return [_remote(cin[k].at[2 * cx + cy], cout[k].at[j], ssem, rsem, 3 * k + j, (cx, cy, c))
                for k in range(len(sums)) for j, (cx, cy) in enumerate(chips)]

    outs = [_sds((3,) + s.shape[1:], s.dtype) for s in sums]
    return _Comm(sums, outs, copies, 3 * len(sums))


def _reduce_share(reds, l):
    def copies(cin, cout, ssem, rsem):
        x, y, c, _ = _mesh_pos()
        out = []
        for k, r in enumerate(reds):
            half = cout[k].at[l, _mine(c, r.shape[1])]
            out.append(_remote(half, half, ssem, rsem, k, (x, y, 1 - c)))
        return out

    return _Comm(reds, [_sds(r.shape, r.dtype) for r in reds], copies, len(reds),
                 {k: k for k in range(len(reds))})


def _allreduce_small(per_layer):
    kinds = len(per_layer[0])
    shapes = [a.shape[1:] if a.shape[0] == 1 else a.shape for a in per_layer[0]]

    def body(*refs):
        ins = refs[:DEPTH * kinds]
        outs = refs[DEPTH * kinds:(DEPTH + 1) * kinds]
        gbufs = refs[(DEPTH + 1) * kinds:(DEPTH + 2) * kinds]
        send_sems, recv_sems = refs[-2], refs[-1]
        x, y, c, chips = _mesh_pos()
        sibling = (x, y, 1 - c)

        def copy(k, i, block, to):
            px, py, pc = block
            slot = gbufs[k].at[4 * px + 2 * py + pc]
            return _remote(slot, slot, send_sems, recv_sems, 7 * k + i, to)

        me = (x, y, c)
        first, passed = [], []
        for k in range(kinds):
            for l in range(DEPTH):
                a = ins[l * kinds + k]
                if per_layer[l][k].shape[0] == 1:
                    gbufs[k][4 * x + 2 * y + c, l:l + 1] = a[...]
                else:
                    gbufs[k][4 * x + 2 * y + c, l] = a[...]
            first.append(copy(k, 0, me, sibling))
            first += [copy(k, 1 + j, me, (*chip, c)) for j, chip in enumerate(chips)]
            passed += [copy(k, 4 + j, (*chip, c), sibling) for j, chip in enumerate(chips)]
        for cp in first:
            cp.start()
        for k in range(kinds):
            for j, chip in enumerate(chips):
                copy(k, 1 + j, (*chip, c), me).wait_recv()
                passed[3 * k + j].start()
        for k in range(kinds):
            copy(k, 0, sibling, me).wait_recv()
            for j, chip in enumerate(chips):
                copy(k, 4 + j, (*chip, 1 - c), me).wait_recv()
        for cp in first + passed:
            cp.wait_send()
        for k in range(kinds):
            acc = gbufs[k][0]
            for d in range(1, 8):
                acc = acc + gbufs[k][d]
            outs[k][...] = acc

    vmem = pl.BlockSpec(memory_space=pltpu.VMEM)
    return pl.pallas_call(
        body, name="allreduce_small",
        in_specs=[vmem] * (DEPTH * kinds), out_specs=[vmem] * kinds,
        out_shape=[_sds((DEPTH,) + s, F32) for s in shapes],
        scratch_shapes=[pltpu.VMEM((8, DEPTH) + s, F32) for s in shapes]
        + [pltpu.SemaphoreType.DMA((7 * kinds,)), pltpu.SemaphoreType.DMA((7 * kinds,))],
        compiler_params=pltpu.CompilerParams(vmem_limit_bytes=VMEM_LIMIT_V7X),
    )(*per_layer[0], *per_layer[1])


def _adamw_small(ws, gs, ms, vs):
    n = len(ws)
    c1 = 1.0 - ADAM_B1 ** ADAM_STEP
    c2 = 1.0 - ADAM_B2 ** ADAM_STEP

    def body(*refs):
        for i in range(n):
            w_ref, g_ref, m_ref, v_ref = (refs[j * n + i] for j in range(4))
            d_ref, nm_ref, nv_ref = (refs[(4 + j) * n + i] for j in range(3))
            gv = g_ref[...]
            nm = ADAM_B1 * m_ref[...] + (1.0 - ADAM_B1) * gv
            nv = ADAM_B2 * v_ref[...] + (1.0 - ADAM_B2) * (gv * gv)
            nm_ref[...] = nm
            nv_ref[...] = nv
            d_ref[...] = -ADAM_LR * ((nm / c1) / (jnp.sqrt(nv / c2) + ADAM_EPS)
                                     + ADAM_WD * w_ref[...])

    vmem = pl.BlockSpec(memory_space=pltpu.VMEM)
    outs = pl.pallas_call(
        body, name="adamw_small", in_specs=[vmem] * (4 * n), out_specs=[vmem] * (3 * n),
        out_shape=[_sds(w.shape, F32) for w in ws] * 3,
        compiler_params=pltpu.CompilerParams(vmem_limit_bytes=VMEM_LIMIT_V7X),
    )(*ws, *gs, *ms, *vs)
    return outs[:n], outs[n:2 * n], outs[2 * n:]


def _core_index():
    return jnp.reshape(lax.axis_index("c"), (1,)).astype(jnp.int32)


def _chip_index():
    return jnp.reshape(2 * lax.axis_index("x") + lax.axis_index("y"), (1,)).astype(jnp.int32)


def _chip_sums(name, stacked, sibs, l):
    n = len(stacked)
    dims = [(s.shape[2] // 2, s.shape[3]) for s in stacked]

    def body(c_ref, *refs):
        for k in range(n):
            a_ref, b_ref, o_ref = refs[k], refs[n + k], refs[2 * n + k]
            o_ref[...] = (a_ref[...].astype(F32) + b_ref[...].astype(F32)).astype(BF16)

    return pl.pallas_call(
        body, name=name,
        grid_spec=pltpu.PrefetchScalarGridSpec(
            num_scalar_prefetch=1, grid=(N_CHIPS,),
            in_specs=[pl.BlockSpec((None, None, hr, cd), lambda j, cr: (l, j, cr[0], 0))
                      for hr, cd in dims]
            + [pl.BlockSpec((None, hr, cd), lambda j, cr: (j, 0, 0)) for hr, cd in dims],
            out_specs=[pl.BlockSpec((None, hr, cd), lambda j, cr: (j, 0, 0)) for hr, cd in dims]),
        out_shape=[_sds((N_CHIPS, hr, cd), BF16) for hr, cd in dims],
        compiler_params=_cparams("parallel"))(_core_index(), *stacked, *sibs)


def _final_sums(name, sums, recvs, l, fills):
    n = len(sums)
    dims = [(s.shape[1] // 2, s.shape[2]) for s in sums]
    filled = fills[0] is not None

    def body(m_ref, *refs):
        outs = refs[-n:]
        for k in range(n):
            acc = refs[k][...].astype(F32)
            for j in range(3):
                acc = acc + refs[n + k][j].astype(F32)
            outs[k][...] = acc

    in_specs = ([pl.BlockSpec((None, tr, cd), lambda i, mr: (mr[0], i, 0)) for tr, cd in dims]
                + [pl.BlockSpec((3, tr, cd), lambda i, mr: (0, i, 0)) for tr, cd in dims])
    args = [jnp.concatenate([_chip_index(), _core_index()]), *sums, *recvs]
    aliases = {}
    if filled:
        in_specs += [pl.BlockSpec(memory_space=pl.ANY)] * n
        args += list(fills)
        aliases = {1 + 2 * n + k: k for k in range(n)}
    return pl.pallas_call(
        body, name=name,
        grid_spec=pltpu.PrefetchScalarGridSpec(
            num_scalar_prefetch=1, grid=(2,), in_specs=in_specs,
            out_specs=[pl.BlockSpec((None, tr, cd), lambda i, mr: (l, 2 * mr[1] + i, 0))
                       for tr, cd in dims]),
        out_shape=[_sds((DEPTH, 4 * tr, cd), F32) for tr, cd in dims],
        input_output_aliases=aliases,
        compiler_params=_cparams("parallel"))(*args)


def _adamw(name, w, g, m, v):
    nl, r, cdim = w.shape
    tr = r // 4 if r % 32 == 0 else r
    c1 = 1.0 - ADAM_B1 ** ADAM_STEP
    c2 = 1.0 - ADAM_B2 ** ADAM_STEP

    def body(w_ref, g_ref, m_ref, v_ref, d_ref, nm_ref, nv_ref, go_ref):
        gv = g_ref[...]
        go_ref[...] = gv
        nm = ADAM_B1 * m_ref[...] + (1.0 - ADAM_B1) * gv
        nv = ADAM_B2 * v_ref[...] + (1.0 - ADAM_B2) * (gv * gv)
        nm_ref[...] = nm
        nv_ref[...] = nv
        d_ref[...] = -ADAM_LR * ((nm / c1) / (jnp.sqrt(nv / c2) + ADAM_EPS) + ADAM_WD * w_ref[...])

    spec = pl.BlockSpec((None, tr, cdim), lambda l, i: (l, i, 0))
    out = _sds(w.shape, F32)
    return pl.pallas_call(
        body, name=name, grid=(nl, r // tr), in_specs=[spec] * 4, out_specs=[spec] * 4,
        out_shape=[out] * 4, compiler_params=_cparams("parallel", "parallel"))(w, g, m, v)


def kernel(x, norm_mix_pre, w_in, b_gate, rel_bias, w_attn_out, w_pool_group, pool_scale, w_pool_out, w_o, norm_mix_post, norm_ffn_pre, w_up, conv_w, conv_b, w_down, norm_ffn_post, loss_target, m_norm_mix_pre, m_w_in, m_b_gate, m_rel_bias, m_w_attn_out, m_w_pool_group, m_pool_scale, m_w_pool_out, m_w_o, m_norm_mix_post, m_norm_ffn_pre, m_w_up, m_conv_w, m_conv_b, m_w_down, m_norm_ffn_post, v_norm_mix_pre, v_w_in, v_b_gate, v_rel_bias, v_w_attn_out, v_w_pool_group, v_pool_scale, v_w_pool_out, v_w_o, v_norm_mix_post, v_norm_ffn_pre, v_w_up, v_conv_w, v_conv_b, v_w_down, v_norm_ffn_post):
    t = x.shape[1]
    xs = x.reshape(t, D_MODEL)
    target = loss_target.reshape(t, D_MODEL)

    names = ["w_in", "w_attn_out", "w_pool_out", "w_o", "w_up", "w_down"]
    shards = [w.astype(BF16) for w in (w_in, w_attn_out, w_pool_out, w_o, w_up, w_down)]
    rows = [s.shape[1] for s in shards]
    nbig = len(shards)
    h, g = _norm_fwd("l0_norm_mix_pre", x.reshape(t, D_MODEL), norm_mix_pre[0:1],
                     _gather_send(shards[:1], conv_w, None, 0))
    g = _comm_call("gather0_forward", _gather_forward(g, 1, rows[:1], 0))
    cw_full = jnp.transpose(g[1], (0, 2, 1, 3)).reshape(DEPTH, 3, 2 * D_FF)
    g = g[:1]
    wg_bf = w_pool_group.astype(BF16)

    def views(gathered):
        win_g, wao_g, wpo_g, wo_g, wup_g, wdn_g = gathered
        return (win_g, wao_g, wpo_g, wo_g.reshape(DEPTH, D_MODEL, D_MODEL), wup_g,
                wdn_g.reshape(DEPTH, D_FF, D_MODEL))

    saved = []
    xcur = xs
    for l in range(DEPTH):
        tag = f"l{l}_"
        bias = _bias_table(tag + "bias_table", rel_bias[l])
        proj = _mm_nn_blocked(tag + "proj", h, g[0], l, BF16)
        if l == 0:
            att, probs, rest = _attn_fwd(tag + "attn_fwd", proj, bias,
                                         _gather_send(shards[1:], None, None, 0))
            pooled, mixed, rest = _pool_fwd(tag + "pool_fwd", proj, wg_bf[l], pool_scale[l:l + 1],
                                            _gather_forward(rest, nbig - 1, rows[1:], 0))
            g = g + rest
        else:
            att, probs = _attn_fwd(tag + "attn_fwd", proj, bias)
            pooled, mixed = _pool_fwd(tag + "pool_fwd", proj, wg_bf[l], pool_scale[l:l + 1])
        win_g, wao_g, wpo_g, wo_full, wup_g, wdn_full = views(g)
        ya = _narrow_nn(tag + "attn_out", att, wao_g, l)
        yb = _narrow_nn(tag + "pool_out", mixed, wpo_g, l)
        z = _gate_fwd(tag + "gate_fwd", proj, b_gate[l:l + 1], ya, yb)
        mix = _mm_nn(tag + "mix", z, wo_full, l, D_MODEL, F32)
        x1, h2 = _post_pre_fwd(tag + "norm_mix_post", xcur, mix, norm_mix_post[l:l + 1],
                               norm_ffn_pre[l:l + 1])
        if l == 0:
            hu, mixing = _mm_nn_blocked(tag + "ffn_up", h2, wup_g, l, BF16,
                                        _gather_send(shards[:4], None, g[:4], 1))
            a, hc, ffn_g = _ffn_gate_fwd(tag + "ffn_gate_fwd", hu, cw_full[l], conv_b[l:l + 1],
                                         _gather_send(shards[4:], None, g[4:], 1))
            g = mixing + ffn_g
            wdn_full = views(g)[5]
        else:
            hu = _mm_nn_blocked(tag + "ffn_up", h2, wup_g, l, BF16)
            a, hc = _ffn_gate_fwd(tag + "ffn_gate_fwd", hu, cw_full[l], conv_b[l:l + 1])
        f = _mm_nn(tag + "ffn_down", a, wdn_full, l, D_FF, F32)
        saved.append(dict(x=xcur, h=h, proj=proj, att=att, pooled=pooled, mixed=mixed, ya=ya,
                          yb=yb, z=z, mix=mix, x1=x1, h2=h2, hu=hu, hc=hc, a=a, f=f, probs=probs))
        if l == 0:
            xcur, h, g = _post_pre_fwd(tag + "norm_ffn_post", x1, f, norm_ffn_post[l:l + 1],
                                       norm_mix_pre[l + 1:l + 2], _gather_forward(g, nbig, rows, 1))
        elif l < DEPTH - 1:
            xcur, h = _post_pre_fwd(tag + "norm_ffn_post", x1, f, norm_ffn_post[l:l + 1],
                                    norm_mix_pre[l + 1:l + 2])
    win_g, wao_g, wpo_g, wo_full, wup_g, wdn_full = views(g)

    dy, df, d_nfpost, loss_local = _tail("tail", saved[-1]["x1"], saved[-1]["f"],
                                         norm_ffn_post[DEPTH - 1:DEPTH], target)
    loss = lax.psum(loss_local, ("x", "y", "c"))

    dx = dy
    dws = dict.fromkeys(names)
    reds = [None] * nbig
    small_grads = [None] * DEPTH
    ffn = [4, 5]
    outs3 = [1, 2, 3]

    def blocks(ks):
        return [dws[names[k]].reshape(DEPTH, N_CHIPS, rows[k], -1) for k in ks]

    def chip_sums(ks, sib, l):
        return _chip_sums(f"chip_sums{l}_" + names[ks[0]], blocks(ks), sib, l)

    def final_sums(ks, sums, recv, l):
        outs = _final_sums(f"final_sums{l}_" + names[ks[0]], sums, recv, l, [reds[k] for k in ks])
        for k, r in zip(ks, outs):
            reds[k] = r

    for l in reversed(range(DEPTH)):
        tag = f"l{l}_"
        sv = saved[l]
        every = list(range(nbig))
        if l == 0:
            da, sib = _mm_nt(tag + "ffn_down_dx", df, wdn_full, l, D_FF // 2, BF16,
                             _reduce_swap(blocks(every), 1))
            sums = chip_sums(every, sib, 1)
        else:
            da = _mm_nt(tag + "ffn_down_dx", df, wdn_full, l, D_FF // 2, BF16)
        dws["w_down"] = _mm_tn(tag + "ffn_down_dw", sv["a"], df, D_FF // 2, l, dws["w_down"])
        if l == 0:
            dhu, dconv, recv = _ffn_gate_bwd(tag + "ffn_gate_bwd", da, sv["hu"], sv["hc"],
                                             cw_full[l], _reduce_scatter(sums))
            final_sums(every, sums, recv, 1)
            dh2, reds = _mm_nt_blocked(tag + "ffn_up_dx", dhu, wup_g, l, F32,
                                       _reduce_share(reds, 1))
        else:
            dhu, dconv = _ffn_gate_bwd(tag + "ffn_gate_bwd", da, sv["hu"], sv["hc"], cw_full[l])
            dh2 = _mm_nt_blocked(tag + "ffn_up_dx", dhu, wup_g, l, F32)
        dws["w_up"] = _mm_tn_blocked(tag + "ffn_up_dw", sv["h2"], dhu, l, dws["w_up"])
        if l == 0:
            dx1, d_nfpre, dmix, d_nmpost, sib = _pre_post_bwd(
                tag + "norm_ffn_pre_bwd", dh2, sv["x1"], dx, norm_ffn_pre[l:l + 1], sv["mix"],
                norm_mix_post[l:l + 1], _reduce_swap(blocks(ffn), 0))
            sums = chip_sums(ffn, sib, 0)
        else:
            dx1, d_nfpre, dmix, d_nmpost = _pre_post_bwd(
                tag + "norm_ffn_pre_bwd", dh2, sv["x1"], dx, norm_ffn_pre[l:l + 1], sv["mix"],
                norm_mix_post[l:l + 1])
        dz = _mm_nt(tag + "mix_dx", dmix, wo_full, l, D_MODEL, BF16)
        dws["w_o"] = _mm_tn(tag + "mix_dw", sv["z"], dmix, D_MODEL, l, dws["w_o"])
        dya, dyb, dgates, d_bgate = _gate_bwd(tag + "gate_bwd", dz, sv["proj"], b_gate[l:l + 1],
                                              sv["ya"], sv["yb"])
        datt = _narrow_nt(tag + "attn_out_dx", dya, wao_g, l)
        dws["w_attn_out"] = _narrow_tn(tag + "attn_out_dw", sv["att"], dya, l, dws["w_attn_out"])
        dmixed = _narrow_nt(tag + "pool_out_dx", dyb, wpo_g, l)
        dws["w_pool_out"] = _narrow_tn(tag + "pool_out_dw", sv["mixed"], dyb, l, dws["w_pool_out"])
        if l == 0:
            du, d_wg, d_pscale, sib = _pool_bwd(tag + "pool_bwd", dmixed, sv["pooled"], wg_bf[l],
                                                pool_scale[l:l + 1], _reduce_swap(blocks(outs3), 0))
            sums3 = chip_sums(outs3, sib, 0)
            dqkv, dbias, recv = _attn_bwd(
                tag + "attn_bwd", sv["proj"], datt, sv["probs"],
                _both(_reduce_scatter(sums), _reduce_scatter(sums3)))
            final_sums(ffn, sums, recv[:len(ffn)], 0)
            final_sums(outs3, sums3, recv[len(ffn):], 0)
        else:
            du, d_wg, d_pscale = _pool_bwd(tag + "pool_bwd", dmixed, sv["pooled"], wg_bf[l],
                                           pool_scale[l:l + 1])
            dqkv, dbias = _attn_bwd(tag + "attn_bwd", sv["proj"], datt, sv["probs"])
        d_rel = _bias_fold(tag + "bias_fold", dbias)
        if l == 0:
            dh, shared = _proj_dx(tag + "proj_dx", dqkv, du, dgates, win_g, l,
                                  _reduce_share([reds[k] for k in ffn + outs3], 0))
            for k, r in zip(ffn + outs3, shared):
                reds[k] = r
        else:
            dh = _proj_dx(tag + "proj_dx", dqkv, du, dgates, win_g, l)
        dws["w_in"] = _proj_dw(tag + "proj_dw", sv["h"], dqkv, du, dgates, l, dws["w_in"])
        small_grads[l] = [None, d_nmpost, d_nfpre, d_nfpost, d_bgate, d_rel, d_wg, d_pscale, dconv]
        if l > 0:
            dx, small_grads[l][0], df, d_nfpost = _pre_post_bwd(
                tag + "norm_mix_pre_bwd", dh, sv["x"], dx1, norm_mix_pre[l:l + 1],
                saved[l - 1]["f"], norm_ffn_post[l - 1:l])
        else:
            dx, small_grads[l][0] = _norm_pre_bwd(tag + "norm_mix_pre_bwd", dh, sv["x"], dx1,
                                                  norm_mix_pre[l:l + 1])

    grad_x = dx.reshape(x.shape)

    delta, new_m, new_v = {}, {}, {}
    sib = _comm_call("reduce_swap", _reduce_swap(blocks([0]), 0))
    sums = chip_sums([0], sib, 0)
    recv = _comm_call("reduce_scatter", _reduce_scatter(sums))
    final_sums([0], sums, recv, 0)
    g_big = _comm_call("reduce_share", _reduce_share([reds[0]], 0)) + reds[1:]

    (g_nmpre, g_nmpost, g_nfpre, g_nfpost, g_bgate, g_rel, g_wg, g_pscale,
     g_conv) = _allreduce_small(small_grads)
    g_rel = g_rel[:, :, :N_REL]
    g_cb = g_conv[:, 3]
    ncw = conv_w.shape[2]
    chip = 2 * lax.axis_index("x") + lax.axis_index("y")
    g_cw = lax.dynamic_slice_in_dim(g_conv[:, 0:3], chip * ncw, ncw, axis=2)

    grads = dict(norm_mix_pre=g_nmpre, w_in=g_big[0], b_gate=g_bgate, rel_bias=g_rel,
                 w_attn_out=g_big[1], w_pool_group=g_wg, pool_scale=g_pscale, w_pool_out=g_big[2],
                 w_o=g_big[3], norm_mix_post=g_nmpost, norm_ffn_pre=g_nfpre, w_up=g_big[4],
                 conv_w=g_cw, conv_b=g_cb, w_down=g_big[5], norm_ffn_post=g_nfpost)
    weights = dict(norm_mix_pre=norm_mix_pre, w_in=w_in, b_gate=b_gate, rel_bias=rel_bias,
                   w_attn_out=w_attn_out, w_pool_group=w_pool_group, pool_scale=pool_scale,
                   w_pool_out=w_pool_out, w_o=w_o, norm_mix_post=norm_mix_post,
                   norm_ffn_pre=norm_ffn_pre, w_up=w_up, conv_w=conv_w, conv_b=conv_b,
                   w_down=w_down, norm_ffn_post=norm_ffn_post)
    moms = dict(norm_mix_pre=(m_norm_mix_pre, v_norm_mix_pre), w_in=(m_w_in, v_w_in),
                b_gate=(m_b_gate, v_b_gate), rel_bias=(m_rel_bias, v_rel_bias),
                w_attn_out=(m_w_attn_out, v_w_attn_out),
                w_pool_group=(m_w_pool_group, v_w_pool_group),
                pool_scale=(m_pool_scale, v_pool_scale), w_pool_out=(m_w_pool_out, v_w_pool_out),
                w_o=(m_w_o, v_w_o), norm_mix_post=(m_norm_mix_post, v_norm_mix_post),
                norm_ffn_pre=(m_norm_ffn_pre, v_norm_ffn_pre), w_up=(m_w_up, v_w_up),
                conv_w=(m_conv_w, v_conv_w), conv_b=(m_conv_b, v_conv_b),
                w_down=(m_w_down, v_w_down), norm_ffn_post=(m_norm_ffn_post, v_norm_ffn_post))
    order = list(weights.keys())

    small_names = [nm for nm in order if nm not in names]
    for nm in names:
        delta[nm], new_m[nm], new_v[nm], grads[nm] = _adamw("adamw_" + nm, weights[nm], grads[nm],
                                                            *moms[nm])
    d_s, m_s, v_s = _adamw_small([weights[nm] for nm in small_names],
                                 [grads[nm] for nm in small_names],
                                 [moms[nm][0] for nm in small_names],
                                 [moms[nm][1] for nm in small_names])
    for i, nm in enumerate(small_names):
        delta[nm], new_m[nm], new_v[nm] = d_s[i], m_s[i], v_s[i]

    return (loss, grad_x, *[grads[nm] for nm in order], *[delta[nm] for nm in order],
            *[new_m[nm] for nm in order], *[new_v[nm] for nm in order])
```

```python
import functools
import math

import jax
import jax.numpy as jnp
from jax import lax
from jax.experimental import pallas as pl
from jax.experimental.pallas import tpu as pltpu

F32 = jnp.float32
BF16 = jnp.bfloat16
MESH = pl.DeviceIdType.MESH

D_MODEL = 1024
DEPTH = 2
CHUNK = 64
BAND_CHUNKS = 9
BAND = BAND_CHUNKS * CHUNK
HEADS = 8
HEAD_DIM = 64
ATTN_W = HEADS * HEAD_DIM
POOL_WINDOWS = (2, 4, 8, 16)
POOL_W = 512
POOL_GD = 128
MAX_REL = 256
N_REL = 2 * MAX_REL + 1
D_FF = 2816
IN_W = 3 * ATTN_W + POOL_W + 2 * D_MODEL
EPS = 1e-6
ATTN_SCALE = HEAD_DIM ** -0.5
BAND_PAD = 640
BIAS_LANES = BAND_PAD
N_CHIPS = 4

ADAM_LR = 0.001
ADAM_B1 = 0.9
ADAM_B2 = 0.999
ADAM_EPS = 1e-08
ADAM_WD = 0.01
ADAM_STEP = 10

VMEM_LIMIT_V7X = 56 * 1024 * 1024
TOK = 512
ATT_BLK = 8 * CHUNK
FF_COL = 256
FF_TOK = 1024
HALO = 32


def _cparams(*sem):
    return pltpu.CompilerParams(dimension_semantics=sem, vmem_limit_bytes=VMEM_LIMIT_V7X)


def _sds(shape, dtype):
    return jax.ShapeDtypeStruct(shape, dtype)


class _Comm:
    def __init__(self, ins, outs, copies, n_sems, alias=None):
        self.ins, self.outs, self.copies, self.n_sems = list(ins), list(outs), copies, n_sems
        self.alias = dict(alias or {})


class _SemsFrom:
    def __init__(self, sems, start):
        self.sems, self.start = sems, start

    @property
    def at(self):
        return self

    def __getitem__(self, i):
        return self.sems.at[self.start + i]


def _both(a, b):
    na, nao = len(a.ins), len(a.outs)

    def copies(cin, cout, ssem, rsem):
        return (a.copies(cin[:na], cout[:nao], ssem, rsem)
                + b.copies(cin[na:], cout[nao:], _SemsFrom(ssem, a.n_sems), _SemsFrom(rsem, a.n_sems)))

    alias = dict(a.alias)
    alias.update({na + i: nao + o for i, o in b.alias.items()})
    return _Comm(a.ins + b.ins, a.outs + b.outs, copies, a.n_sems + b.n_sems, alias)


def _pcall(body, name, grid, in_specs, out_specs, out_shape, scratch_shapes, compiler_params, args,
           comm=None, aliases=None):
    single = not isinstance(out_shape, (list, tuple))
    out_specs = [out_specs] if single else list(out_specs)
    out_shape = [out_shape] if single else list(out_shape)
    n_in, n_out = len(in_specs), len(out_specs)
    aliases = dict(aliases or {})
    if comm is None:
        res = pl.pallas_call(
            body, name=name, grid=grid, in_specs=list(in_specs), out_specs=out_specs,
            out_shape=out_shape, scratch_shapes=list(scratch_shapes),
            input_output_aliases=aliases, compiler_params=compiler_params)(*args)
        return (res[0] if single else res), None
    ci, co = len(comm.ins), len(comm.outs)

    def hosted(*refs):
        main_in, cin = refs[:n_in], refs[n_in:n_in + ci]
        main_out = refs[n_in + ci:n_in + ci + n_out]
        cout = refs[n_in + ci + n_out:n_in + ci + n_out + co]
        rest = refs[n_in + ci + n_out + co:]
        copies = comm.copies(cin, cout, rest[-2], rest[-1])
        ids = [pl.program_id(a) for a in range(len(grid))]
        first = functools.reduce(jnp.logical_and, [i == 0 for i in ids])
        last = functools.reduce(jnp.logical_and, [i == g - 1 for i, g in zip(ids, grid)])

        @pl.when(first)
        def _():
            for cp in copies:
                cp.start()

        body(*main_in, *main_out, *rest[:-2])

        @pl.when(last)
        def _():
            for cp in copies:
                cp.wait()

    for i, o in comm.alias.items():
        aliases[n_in + i] = n_out + o
    hbm = pl.BlockSpec(memory_space=pl.ANY)
    sems = pltpu.SemaphoreType.DMA((comm.n_sems,))
    res = pl.pallas_call(
        hosted, name=name, grid=grid, in_specs=list(in_specs) + [hbm] * ci,
        out_specs=out_specs + [hbm] * co, out_shape=out_shape + comm.outs,
        scratch_shapes=list(scratch_shapes) + [sems, sems],
        input_output_aliases=aliases, compiler_params=compiler_params)(*args, *comm.ins)
    return (res[0] if single else list(res[:n_out])), list(res[n_out:])


def _comm_call(name, comm):
    ci = len(comm.ins)

    def body(*refs):
        copies = comm.copies(refs[:ci], refs[ci:-2], refs[-2], refs[-1])
        for cp in copies:
            cp.start()
        for cp in copies:
            cp.wait()

    hbm = pl.BlockSpec(memory_space=pl.ANY)
    sems = pltpu.SemaphoreType.DMA((comm.n_sems,))
    return list(pl.pallas_call(
        body, name=name, in_specs=[hbm] * ci, out_specs=[hbm] * len(comm.outs),
        out_shape=comm.outs, scratch_shapes=[sems, sems],
        input_output_aliases=comm.alias)(*comm.ins))


def _matmul(name, a, b, a_spec, b_spec, o_spec, out_shape, grid, contract, nk, acc_shape,
            fill=None, comm=None):
    in_place = out_shape.dtype == F32

    def body(*refs):
        a_ref, b_ref = refs[0], refs[1]
        o_ref = refs[2 if fill is None else 3]
        scratch = refs[(3 if fill is None else 4):]
        part = lax.dot_general(a_ref[...], b_ref[...], (contract, ((), ())),
                               preferred_element_type=F32)
        if nk == 1:
            o_ref[...] = part.astype(o_ref.dtype)
        else:
            acc_ref = o_ref if in_place else scratch[0]
            k = pl.program_id(2)

            @pl.when(k == 0)
            def _():
                acc_ref[...] = part

            @pl.when(k > 0)
            def _():
                acc_ref[...] += part

            if not in_place:
                @pl.when(k == nk - 1)
                def _():
                    o_ref[...] = acc_ref[...].astype(o_ref.dtype)

    scratch = [] if nk == 1 or in_place else [pltpu.VMEM(acc_shape, F32)]
    in_specs, args, aliases = [a_spec, b_spec], [a, b], {}
    if fill is not None:
        in_specs.append(pl.BlockSpec(memory_space=pl.ANY))
        args.append(fill)
        aliases = {2: 0}
    out, moved = _pcall(body, name, grid, in_specs, o_spec, out_shape, scratch,
                        _cparams("parallel", "parallel", "arbitrary"), args, comm, aliases)
    return out if comm is None else (out, moved)


NN = ((1,), (0,))
NT = ((1,), (1,))
TN = ((0,), (0,))


def _tm(t):
    return min(t, 1024)


def _tt(t):
    return min(t, 2048)


def _col_block_spec(a, rows, nb, row_col):
    if a.ndim == 2:
        return pl.BlockSpec((rows, nb), row_col)

    def halves(*ids):
        r, c = row_col(*ids)
        return c // 2, r, c % 2

    return pl.BlockSpec((None, rows, nb), halves)


def _mm_nn_blocked(name, a, w, l, out_dtype, comm=None):
    t, k = a.shape
    nb = w.shape[3]
    tm = _tm(t)
    return _matmul(
        name, a, w,
        pl.BlockSpec((tm, k), lambda i, n, kk: (i, 0)),
        pl.BlockSpec((None, None, k, nb), lambda i, n, kk: (l, n, 0, 0)),
        pl.BlockSpec((tm, nb), lambda i, n, kk: (i, n)),
        _sds((t, N_CHIPS * nb), out_dtype), (t // tm, N_CHIPS, 1), NN, 1, None, comm=comm)


def _mm_nt_blocked(name, a, w, l, out_dtype, comm=None):
    t = a.shape[-2]
    k, nb = w.shape[2], w.shape[3]
    tm = _tm(t)
    return _matmul(
        name, a, w,
        _col_block_spec(a, tm, nb, lambda i, n, kk: (i, kk)),
        pl.BlockSpec((None, None, k, nb), lambda i, n, kk: (l, kk, 0, 0)),
        pl.BlockSpec((tm, k), lambda i, n, kk: (i, 0)),
        _sds((t, k), out_dtype), (t // tm, 1, N_CHIPS), NT, N_CHIPS, (tm, k), comm=comm)


def _mm_tn_blocked(name, a, g, l, fill):
    t, k = a.shape
    nb = g.shape[-1] * (g.ndim - 1) // N_CHIPS
    tt = _tt(t)
    nt = t // tt
    return _matmul(
        name, a, g,
        pl.BlockSpec((tt, k), lambda n, j, kk: (kk, 0)),
        _col_block_spec(g, tt, nb, lambda n, j, kk: (kk, n)),
        pl.BlockSpec((None, None, k, nb), lambda n, j, kk: (l, n, 0, 0)),
        _sds((DEPTH, N_CHIPS, k, nb), BF16), (N_CHIPS, 1, nt), TN, nt, (k, nb), fill)


def _proj_pieces(rows, dqkv_first):
    def piece(col):
        if dqkv_first:
            return pl.BlockSpec((rows, ATTN_W), lambda i, kk: (i, col))
        return pl.BlockSpec((rows, ATTN_W), lambda n, kk: (kk, col))
    return [piece(0), piece(1), piece(2), piece(0)]


def _proj_dx(name, dqkv, du, dgates, w, l, comm=None):
    t = du.shape[0]
    k, nb = w.shape[2], w.shape[3]
    tm = _tm(t)

    def body(dq_ref, dk_ref, dv_ref, du_ref, dg_ref, w_ref, o_ref):
        kk = pl.program_id(1)

        def mm(a):
            return lax.dot_general(a, w_ref[...], (NT, ((), ())), preferred_element_type=F32)

        @pl.when(kk == 0)
        def _():
            o_ref[...] = mm(jnp.concatenate([dq_ref[...], dk_ref[...]], axis=1))

        @pl.when(kk == 1)
        def _():
            o_ref[...] += mm(jnp.concatenate([dv_ref[...], du_ref[...]], axis=1))

        @pl.when(kk >= 2)
        def _():
            o_ref[...] += mm(dg_ref[...])

    out, moved = _pcall(
        body, name, (t // tm, N_CHIPS),
        _proj_pieces(tm, True)
        + [pl.BlockSpec((tm, nb), lambda i, kk: (i, jnp.maximum(kk - 2, 0))),
           pl.BlockSpec((None, None, k, nb), lambda i, kk: (l, kk, 0, 0))],
        pl.BlockSpec((tm, k), lambda i, kk: (i, 0)), _sds((t, k), F32),
        [], _cparams("arbitrary", "arbitrary"),
        (dqkv, dqkv, dqkv, du, dgates, w), comm)
    return out if comm is None else (out, moved)


def _proj_dw(name, h, dqkv, du, dgates, l, fill):
    t, k = h.shape
    nb = dgates.shape[1] // 2
    tt = _tm(t)
    nt = t // tt

    def body(*refs):
        h_ref, dq_ref, dk_ref, dv_ref, du_ref, dg_ref = refs[:6]
        o_ref, acc_ref = refs[-2], refs[-1]
        n, kk = pl.program_id(0), pl.program_id(1)

        def update(g):
            part = lax.dot_general(h_ref[...], g, (TN, ((), ())), preferred_element_type=F32)

            @pl.when(kk == 0)
            def _():
                acc_ref[...] = part

            @pl.when(kk > 0)
            def _():
                acc_ref[...] += part

        @pl.when(n == 0)
        def _():
            update(jnp.concatenate([dq_ref[...], dk_ref[...]], axis=1))

        @pl.when(n == 1)
        def _():
            update(jnp.concatenate([dv_ref[...], du_ref[...]], axis=1))

        @pl.when(n >= 2)
        def _():
            update(dg_ref[...])

        @pl.when(kk == nt - 1)
        def _():
            o_ref[...] = acc_ref[...].astype(BF16)

    in_specs = ([pl.BlockSpec((tt, k), lambda n, kk: (kk, 0))] + _proj_pieces(tt, False)
                + [pl.BlockSpec((tt, nb), lambda n, kk: (kk, jnp.maximum(n - 2, 0)))])
    args, aliases = [h, dqkv, dqkv, dqkv, du, dgates], {}
    if fill is not None:
        in_specs.append(pl.BlockSpec(memory_space=pl.ANY))
        args.append(fill)
        aliases = {6: 0}
    return pl.pallas_call(
        body, name=name, grid=(N_CHIPS, nt), in_specs=in_specs,
        out_specs=pl.BlockSpec((None, None, k, nb), lambda n, kk: (l, n, 0, 0)),
        out_shape=_sds((DEPTH, N_CHIPS, k, nb), BF16),
        scratch_shapes=[pltpu.VMEM((k, nb), F32)], input_output_aliases=aliases,
        compiler_params=_cparams("parallel", "arbitrary"))(*args)


def _narrow_nn(name, a, w, l):
    t, k = a.shape
    nb = w.shape[3]
    tm = _tm(t)

    def body(a_ref, w_ref, o_ref):
        av = a_ref[...]
        for j in range(N_CHIPS):
            o_ref[:, j * nb:(j + 1) * nb] = jnp.dot(
                av, w_ref[j], preferred_element_type=F32).astype(BF16)

    return pl.pallas_call(
        body, name=name, grid=(t // tm,),
        in_specs=[pl.BlockSpec((tm, k), lambda i: (i, 0)),
                  pl.BlockSpec((None, N_CHIPS, k, nb), lambda i: (l, 0, 0, 0))],
        out_specs=pl.BlockSpec((tm, N_CHIPS * nb), lambda i: (i, 0)),
        out_shape=_sds((t, N_CHIPS * nb), BF16), compiler_params=_cparams("parallel"))(a, w)


def _narrow_nt(name, a, w, l):
    t = a.shape[0]
    k, nb = w.shape[2], w.shape[3]
    tm = _tm(t)

    def body(a_ref, w_ref, o_ref):
        acc = lax.dot_general(a_ref[:, 0:nb], w_ref[0], (NT, ((), ())), preferred_element_type=F32)
        for j in range(1, N_CHIPS):
            acc = acc + lax.dot_general(a_ref[:, j * nb:(j + 1) * nb], w_ref[j], (NT, ((), ())),
                                        preferred_element_type=F32)
        o_ref[...] = acc.astype(BF16)

    return pl.pallas_call(
        body, name=name, grid=(t // tm,),
        in_specs=[pl.BlockSpec((tm, N_CHIPS * nb), lambda i: (i, 0)),
                  pl.BlockSpec((None, N_CHIPS, k, nb), lambda i: (l, 0, 0, 0))],
        out_specs=pl.BlockSpec((tm, k), lambda i: (i, 0)),
        out_shape=_sds((t, k), BF16), compiler_params=_cparams("parallel"))(a, w)


def _narrow_tn(name, a, g, l, fill):
    t, k = a.shape
    nb = g.shape[1] // N_CHIPS
    tt = _tm(t)
    nt = t // tt

    def body(*refs):
        a_ref, g_ref, o_ref, acc_ref = refs[0], refs[1], refs[-2], refs[-1]
        i = pl.program_id(0)
        part = lax.dot_general(a_ref[...], g_ref[...], (TN, ((), ())), preferred_element_type=F32)

        @pl.when(i == 0)
        def _():
            acc_ref[...] = part

        @pl.when(i > 0)
        def _():
            acc_ref[...] += part

        @pl.when(i == nt - 1)
        def _():
            for j in range(N_CHIPS):
                o_ref[j] = acc_ref[:, j * nb:(j + 1) * nb].astype(BF16)

    in_specs = [pl.BlockSpec((tt, k), lambda i: (i, 0)),
                pl.BlockSpec((tt, N_CHIPS * nb), lambda i: (i, 0))]
    args, aliases = [a, g], {}
    if fill is not None:
        in_specs.append(pl.BlockSpec(memory_space=pl.ANY))
        args.append(fill)
        aliases = {2: 0}
    return pl.pallas_call(
        body, name=name, grid=(nt,), in_specs=in_specs,
        out_specs=pl.BlockSpec((None, N_CHIPS, k, nb), lambda i: (l, 0, 0, 0)),
        out_shape=_sds((DEPTH, N_CHIPS, k, nb), BF16),
        scratch_shapes=[pltpu.VMEM((k, N_CHIPS * nb), F32)], input_output_aliases=aliases,
        compiler_params=_cparams("arbitrary"))(*args)


def _mm_nn(name, a, w, l, tk, out_dtype):
    t, k = a.shape
    n = w.shape[2]
    tm = _tm(t)
    nk = k // tk
    return _matmul(
        name, a, w,
        pl.BlockSpec((tm, tk), lambda i, j, kk: (i, kk)),
        pl.BlockSpec((None, tk, n), lambda i, j, kk: (l, kk, 0)),
        pl.BlockSpec((tm, n), lambda i, j, kk: (i, 0)),
        _sds((t, n), out_dtype), (t // tm, 1, nk), NN, nk, (tm, n))


def _mm_nt(name, a, w, l, tn, out_dtype, comm=None):
    t, n = a.shape
    k = w.shape[1]
    tm = _tm(t)
    return _matmul(
        name, a, w,
        pl.BlockSpec((tm, n), lambda i, j, kk: (i, 0)),
        pl.BlockSpec((None, tn, n), lambda i, j, kk: (l, j, 0)),
        pl.BlockSpec((tm, tn), lambda i, j, kk: (i, j)),
        _sds((t, k), out_dtype), (t // tm, k // tn, 1), NT, 1, None, comm=comm)


def _mm_tn(name, a, g, tko, l, fill):
    t, k = a.shape
    n = g.shape[1]
    tt = _tt(t)
    nt = t // tt
    return _matmul(
        name, a, g,
        pl.BlockSpec((tt, tko), lambda i, j, kk: (kk, i)),
        pl.BlockSpec((tt, n), lambda i, j, kk: (kk, 0)),
        pl.BlockSpec((None, tko, n), lambda i, j, kk: (l, i, 0)),
        _sds((DEPTH, k, n), BF16), (k // tko, 1, nt), TN, nt, (tko, n), fill)


def _row_spec(width, col=0):
    return pl.BlockSpec((TOK, width), lambda i: (i, col))


def _vec_spec(width):
    return pl.BlockSpec((1, width), lambda i: (0, 0))


def _rms(x):
    return lax.rsqrt(jnp.mean(x * x, axis=-1, keepdims=True) + EPS)


def _norm_fwd(name, x, g, comm=None):
    t = x.shape[0]

    def body(x_ref, g_ref, h_ref):
        xv = x_ref[...]
        h_ref[...] = (xv * _rms(xv) * g_ref[...]).astype(BF16)

    out, moved = _pcall(body, name, (t // TOK,), [_row_spec(D_MODEL), _vec_spec(D_MODEL)],
                        _row_spec(D_MODEL), _sds((t, D_MODEL), BF16), [], _cparams("arbitrary"),
                        (x, g), comm)
    return out if comm is None else (out, moved)


ROWS = 16
ROW_UNROLL = 8


def _rows(k):
    return pl.ds(pl.multiple_of(k * ROWS, ROWS), ROWS)


def _strips(step, init):
    def group(j, carry):
        for u in range(ROW_UNROLL):
            carry = step(j * ROW_UNROLL + u, carry)
        return carry

    return lax.fori_loop(0, TOK // (ROWS * ROW_UNROLL), group, init)


def _fold_rows(x):
    return x[0:8] + x[8:16]


def _accumulate(ref, part):
    total = jnp.sum(part, axis=0, keepdims=True)

    @pl.when(pl.program_id(0) == 0)
    def _():
        ref[...] = total

    @pl.when(pl.program_id(0) > 0)
    def _():
        ref[...] += total


def _norm_bwd_rows(d, mv, g):
    r = _rms(mv)
    n = mv * r
    dn = d * g
    return r * (dn - n * jnp.mean(dn * n, axis=-1, keepdims=True)), d * n


def _post_pre_fwd(name, xres, m, g_post, g_pre, comm=None):
    t = xres.shape[0]

    def body(x_ref, m_ref, gp_ref, gn_ref, x1_ref, h_ref):
        def strip(k, c):
            rows = _rows(k)
            mv = m_ref[rows, :]
            x1 = x_ref[rows, :] + mv * _rms(mv) * gp_ref[...]
            x1_ref[rows, :] = x1
            h_ref[rows, :] = (x1 * _rms(x1) * gn_ref[...]).astype(BF16)
            return c

        _strips(strip, 0)

    outs, moved = _pcall(
        body, name, (t // TOK,),
        [_row_spec(D_MODEL), _row_spec(D_MODEL), _vec_spec(D_MODEL), _vec_spec(D_MODEL)],
        [_row_spec(D_MODEL), _row_spec(D_MODEL)],
        [_sds((t, D_MODEL), F32), _sds((t, D_MODEL), BF16)], [], _cparams("arbitrary"),
        (xres, m, g_post, g_pre), comm)
    return outs if comm is None else (*outs, moved)


def _tail(name, xres, m, g_post, target):
    t = xres.shape[0]

    def body(x_ref, m_ref, g_ref, t_ref, dy_ref, dm_ref, dg_ref, l_ref):
        def strip(k, carry):
            rows = _rows(k)
            mv = m_ref[rows, :]
            e = x_ref[rows, :] + mv * _rms(mv) * g_ref[...] - t_ref[rows, :]
            dy = e * (1.0 / D_MODEL)
            dy_ref[rows, :] = dy
            dm, dgn = _norm_bwd_rows(dy, mv, g_ref[...])
            dm_ref[rows, :] = dm.astype(BF16)
            return carry[0] + _fold_rows(dgn), carry[1] + _fold_rows(e * e)

        zero = jnp.zeros((8, D_MODEL), F32)
        dg, sq = _strips(strip, (zero, zero))
        _accumulate(dg_ref, dg)
        _accumulate(l_ref, jnp.sum(sq, axis=1, keepdims=True))

    dy, dm, dg, sq = pl.pallas_call(
        body, name=name, grid=(t // TOK,),
        in_specs=[_row_spec(D_MODEL), _row_spec(D_MODEL), _vec_spec(D_MODEL), _row_spec(D_MODEL)],
        out_specs=[_row_spec(D_MODEL), _row_spec(D_MODEL), _vec_spec(D_MODEL),
                   pl.BlockSpec((1, 1), lambda i: (0, 0))],
        out_shape=[_sds((t, D_MODEL), F32), _sds((t, D_MODEL), BF16), _sds((1, D_MODEL), F32),
                   _sds((1, 1), F32)],
        compiler_params=_cparams("arbitrary"))(xres, m, g_post, target)
    return dy, dm, dg, sq[0, 0] * (0.5 / D_MODEL)


def _pre_post_bwd(name, dh, xin, dxo, g_pre, m, g_post, comm=None):
    t = dh.shape[0]

    def body(dh_ref, x_ref, d_ref, gq_ref, m_ref, gp_ref, dx_ref, dgq_ref, dm_ref, dgp_ref):
        def strip(k, carry):
            rows = _rows(k)
            dxin, dgq = _norm_bwd_rows(dh_ref[rows, :], x_ref[rows, :], gq_ref[...])
            dx = d_ref[rows, :] + dxin
            dx_ref[rows, :] = dx
            dm, dgp = _norm_bwd_rows(dx, m_ref[rows, :], gp_ref[...])
            dm_ref[rows, :] = dm.astype(BF16)
            return carry[0] + _fold_rows(dgq), carry[1] + _fold_rows(dgp)

        zero = jnp.zeros((8, D_MODEL), F32)
        dgq, dgp = _strips(strip, (zero, zero))
        _accumulate(dgq_ref, dgq)
        _accumulate(dgp_ref, dgp)

    outs, moved = _pcall(
        body, name, (t // TOK,),
        [_row_spec(D_MODEL), _row_spec(D_MODEL), _row_spec(D_MODEL), _vec_spec(D_MODEL),
         _row_spec(D_MODEL), _vec_spec(D_MODEL)],
        [_row_spec(D_MODEL), _vec_spec(D_MODEL), _row_spec(D_MODEL), _vec_spec(D_MODEL)],
        [_sds((t, D_MODEL), F32), _sds((1, D_MODEL), F32), _sds((t, D_MODEL), BF16),
         _sds((1, D_MODEL), F32)], [], _cparams("arbitrary"),
        (dh, xin, dxo, g_pre, m, g_post), comm)
    return outs if comm is None else (*outs, moved)


def _norm_pre_bwd(name, dh, xin, dxo, g, comm=None):
    t = dh.shape[0]

    def body(dh_ref, x_ref, d_ref, g_ref, dx_ref, dg_ref):
        xv = x_ref[...]
        dhv = dh_ref[...]
        r = _rms(xv)
        n = xv * r
        dn = dhv * g_ref[...]
        dx_ref[...] = d_ref[...] + r * (dn - n * jnp.mean(dn * n, axis=-1, keepdims=True))
        part = jnp.sum(dhv * n, axis=0, keepdims=True)

        @pl.when(pl.program_id(0) == 0)
        def _():
            dg_ref[...] = part

        @pl.when(pl.program_id(0) > 0)
        def _():
            dg_ref[...] += part

    out, moved = _pcall(
        body, name, (t // TOK,),
        [_row_spec(D_MODEL), _row_spec(D_MODEL), _row_spec(D_MODEL), _vec_spec(D_MODEL)],
        [_row_spec(D_MODEL), _vec_spec(D_MODEL)],
        [_sds((t, D_MODEL), F32), _sds((1, D_MODEL), F32)], [], _cparams("arbitrary"),
        (dh, xin, dxo, g), comm)
    return out if comm is None else (*out, moved)


def _gate_fwd(name, proj, b_gate, ya, yb):
    t = proj.shape[0]

    def body(ga_ref, gb_ref, b_ref, ya_ref, yb_ref, z_ref):
        def strip(k, c):
            rows = _rows(k)
            sa = jax.nn.sigmoid(ga_ref[rows, :].astype(F32) + b_ref[:, :D_MODEL])
            sb = jax.nn.sigmoid(gb_ref[rows, :].astype(F32) + b_ref[:, D_MODEL:])
            z_ref[rows, :] = (sa * ya_ref[rows, :].astype(F32)
                              + sb * yb_ref[rows, :].astype(F32)).astype(BF16)
            return c

        _strips(strip, 0)

    return pl.pallas_call(
        body, name=name, grid=(t // TOK,),
        in_specs=[_row_spec(D_MODEL, 2), _row_spec(D_MODEL, 3), _vec_spec(2 * D_MODEL),
                  _row_spec(D_MODEL), _row_spec(D_MODEL)],
        out_specs=_row_spec(D_MODEL), out_shape=_sds((t, D_MODEL), BF16),
        compiler_params=_cparams("parallel"))(proj, proj, b_gate, ya, yb)


def _gate_bwd(name, dz, proj, b_gate, ya, yb):
    t = proj.shape[0]

    def body(dz_ref, ga_ref, gb_ref, b_ref, ya_ref, yb_ref, dya_ref, dyb_ref, dg_ref, db_ref):
        def strip(k, carry):
            rows = _rows(k)
            dzv = dz_ref[rows, :].astype(F32)
            sa = jax.nn.sigmoid(ga_ref[rows, :].astype(F32) + b_ref[:, :D_MODEL])
            sb = jax.nn.sigmoid(gb_ref[rows, :].astype(F32) + b_ref[:, D_MODEL:])
            dya_ref[rows, :] = (dzv * sa).astype(BF16)
            dyb_ref[rows, :] = (dzv * sb).astype(BF16)
            dga = dzv * ya_ref[rows, :].astype(F32) * sa * (1.0 - sa)
            dgb = dzv * yb_ref[rows, :].astype(F32) * sb * (1.0 - sb)
            dg_ref[rows, :D_MODEL] = dga.astype(BF16)
            dg_ref[rows, D_MODEL:] = dgb.astype(BF16)
            return carry[0] + _fold_rows(dga), carry[1] + _fold_rows(dgb)

        zero = jnp.zeros((8, D_MODEL), F32)
        pa, pb = _strips(strip, (zero, zero))
        _accumulate(db_ref.at[:, :D_MODEL], pa)
        _accumulate(db_ref.at[:, D_MODEL:], pb)

    return pl.pallas_call(
        body, name=name, grid=(t // TOK,),
        in_specs=[_row_spec(D_MODEL), _row_spec(D_MODEL, 2), _row_spec(D_MODEL, 3),
                  _vec_spec(2 * D_MODEL), _row_spec(D_MODEL), _row_spec(D_MODEL)],
        out_specs=[_row_spec(D_MODEL), _row_spec(D_MODEL), _row_spec(2 * D_MODEL),
                   _vec_spec(2 * D_MODEL)],
        out_shape=[_sds((t, D_MODEL), BF16), _sds((t, D_MODEL), BF16),
                   _sds((t, 2 * D_MODEL), BF16), _sds((1, 2 * D_MODEL), F32)],
        compiler_params=_cparams("arbitrary"))(dz, proj, proj, b_gate, ya, yb)


def _head_masks():
    lane = lax.broadcasted_iota(jnp.int32, (1, 2 * HEAD_DIM), 1)
    return lane < HEAD_DIM


BAND_ROWS = 2 * ATT_BLK + CHUNK


def _fill_band(band, prev_ref, cur_ref):
    band[0:ATT_BLK, :] = prev_ref[...]
    band[ATT_BLK:2 * ATT_BLK, :] = cur_ref[...]
    band[2 * ATT_BLK:, :] = jnp.zeros((CHUNK, ATTN_W), BF16)


def _pair_rows(x2, low):
    zero = jnp.zeros_like(x2)
    return jnp.concatenate([jnp.where(low, x2, zero), jnp.where(low, zero, x2)], axis=0)


def _pair_diag(o2, low):
    return jnp.where(low, o2[0:CHUNK, :], o2[CHUNK:, :])


N_PAIRS = HEADS // 2
SM_STRIP = 32
N_STRIPS = BAND_PAD // SM_STRIP
NEG = -1e30


def _fold8(x, op):
    return op(op(x[0:8], x[8:16]), op(x[16:24], x[24:32]))


def _strip(k):
    return pl.ds(pl.multiple_of(k * SM_STRIP, SM_STRIP), SM_STRIP)


def _band_probs(k2, qcat, bias_t, first_key):
    kpos = lax.broadcasted_iota(jnp.int32, (BAND_PAD, 1), 0)
    st = lax.dot_general(k2, qcat, (NT, ((), ())), preferred_element_type=F32)
    st = jnp.where(kpos + first_key >= 0, st + bias_t, NEG)
    e = jnp.exp(st - jnp.max(st, axis=0, keepdims=True))
    return e * (1.0 / jnp.sum(e, axis=0, keepdims=True))


def _attn_specs(nblk):
    cur = lambda col: pl.BlockSpec((ATT_BLK, ATTN_W), lambda s: (jnp.minimum(s, nblk - 1), col))
    prev = lambda col: pl.BlockSpec(
        (ATT_BLK, ATTN_W), lambda s: (jnp.maximum(jnp.minimum(s, nblk - 1) - 1, 0), col))
    return cur, prev


def _attn_fwd(name, proj, bias, comm=None):
    t = proj.shape[0]
    nblk = t // ATT_BLK
    cur, prev = _attn_specs(nblk)

    def body(q_ref, kp_ref, kc_ref, vp_ref, vc_ref, b_ref, o_ref, p_ref, kband, vband):
        s = pl.program_id(0)
        _fill_band(kband, kp_ref, kc_ref)
        _fill_band(vband, vp_ref, vc_ref)
        low = _head_masks()

        def chunk(ci):
            r0 = pl.multiple_of(ci * CHUNK, CHUNK)
            for hp in range(N_PAIRS):
                cols = slice(hp * 128, (hp + 1) * 128)
                qcat = _pair_rows(q_ref[pl.ds(r0, CHUNK), cols] * ATTN_SCALE, low)
                p = _band_probs(kband[pl.ds(r0, BAND_PAD), cols], qcat, b_ref[hp],
                                (s * 8 - 8 + ci) * CHUNK).astype(BF16)
                p_ref[ci, hp] = p
                o2 = lax.dot_general(p, vband[pl.ds(r0, BAND_PAD), cols],
                                     (TN, ((), ())), preferred_element_type=F32)
                o_ref[pl.ds(r0, CHUNK), cols] = _pair_diag(o2, low).astype(BF16)

        def chunks(j, carry):
            for u in range(4):
                chunk(4 * j + u)
            return carry

        lax.fori_loop(0, 2, chunks, 0)

    outs, moved = _pcall(
        body, name, (nblk,),
        [cur(0), prev(1), cur(1), prev(2), cur(2),
         pl.BlockSpec((N_PAIRS, BAND_PAD, 128), lambda s: (0, 0, 0))],
        [pl.BlockSpec((ATT_BLK, ATTN_W), lambda s: (s, 0)),
         pl.BlockSpec((8, N_PAIRS, BAND_PAD, 128), lambda s: (s, 0, 0, 0))],
        [_sds((t, ATTN_W), BF16), _sds((t // CHUNK, N_PAIRS, BAND_PAD, 128), BF16)],
        [pltpu.VMEM((BAND_ROWS, ATTN_W), BF16), pltpu.VMEM((BAND_ROWS, ATTN_W), BF16)],
        _cparams("arbitrary"), (proj, proj, proj, proj, proj, bias), comm)
    return outs if comm is None else (*outs, moved)


def _attn_bwd(name, proj, datt, probs, comm=None):
    t = proj.shape[0]
    nblk = t // ATT_BLK
    cur, prev = _attn_specs(nblk)
    late = pl.BlockSpec((ATT_BLK, 3 * ATTN_W), lambda s: (jnp.maximum(s - 1, 0), 0))

    def body(q_ref, kp_ref, kc_ref, vp_ref, vc_ref, do_ref, p_ref,
             dqkv_ref, db_ref, kband, vband, dkacc, dvacc,
             dp_ref, dsb_ref, qc_ref, dc_ref, dq_ref, dq_held):
        s = pl.program_id(0)

        @pl.when(s == 0)
        def _():
            dkacc[...] = jnp.zeros_like(dkacc)
            dvacc[...] = jnp.zeros_like(dvacc)
            db_ref[...] = jnp.zeros_like(db_ref)
            dq_ref[...] = jnp.zeros_like(dq_ref)

        @pl.when(s < nblk)
        def _():
            _fill_band(kband, kp_ref, kc_ref)
            _fill_band(vband, vp_ref, vc_ref)
            low = _head_masks()

            def chunk(ci, carry):
                r0 = pl.multiple_of(ci * CHUNK, CHUNK)
                for hp in range(N_PAIRS):
                    cols = slice(hp * 128, (hp + 1) * 128)
                    qc_ref[hp] = _pair_rows(q_ref[pl.ds(r0, CHUNK), cols] * ATTN_SCALE, low)
                    dc_ref[hp] = _pair_rows(do_ref[pl.ds(r0, CHUNK), cols], low)
                    dp_ref[hp] = lax.dot_general(vband[pl.ds(r0, BAND_PAD), cols], dc_ref[hp],
                                                 (NT, ((), ())), preferred_element_type=F32)

                def sums(j, acc):
                    for u in range(2):
                        rows = _strip(2 * j + u)
                        acc = tuple(acc[hp] + _fold8(p_ref[ci, hp, rows, :].astype(F32)
                                                     * dp_ref[hp, rows, :], jnp.add)
                                    for hp in range(N_PAIRS))
                    return acc

                acc = lax.fori_loop(0, N_STRIPS // 2, sums, (jnp.zeros((8, 128), F32),) * N_PAIRS)
                delta = [jnp.sum(a, axis=0, keepdims=True) for a in acc]

                def grads(j, c):
                    for u in range(2):
                        rows = _strip(2 * j + u)
                        for hp in range(N_PAIRS):
                            ds = (p_ref[ci, hp, rows, :].astype(F32)
                                  * (dp_ref[hp, rows, :] - delta[hp]))
                            db_ref[hp, rows, :] += ds
                            dsb_ref[hp, rows, :] = ds.astype(BF16)
                    return c

                lax.fori_loop(0, N_STRIPS // 2, grads, 0)
                for hp in range(N_PAIRS):
                    cols = slice(hp * 128, (hp + 1) * 128)
                    dq2 = lax.dot_general(dsb_ref[hp], kband[pl.ds(r0, BAND_PAD), cols],
                                          (TN, ((), ())), preferred_element_type=F32)
                    dq_ref[pl.ds(r0, CHUNK), cols] = (_pair_diag(dq2, low) * ATTN_SCALE).astype(BF16)
                    dkacc[pl.ds(r0, BAND_PAD), cols] += jnp.dot(dsb_ref[hp], qc_ref[hp],
                                                               preferred_element_type=F32)
                    dvacc[pl.ds(r0, BAND_PAD), cols] += jnp.dot(p_ref[ci, hp], dc_ref[hp],
                                                               preferred_element_type=F32)
                return carry

            dq_held[...] = dq_ref[...]
            lax.fori_loop(0, 8, chunk, 0)

        @pl.when(s == nblk)
        def _():
            dq_held[...] = dq_ref[...]

        dqkv_ref[:, 0:ATTN_W] = dq_held[...]
        dqkv_ref[:, ATTN_W:2 * ATTN_W] = dkacc[0:ATT_BLK, :].astype(BF16)
        dqkv_ref[:, 2 * ATTN_W:] = dvacc[0:ATT_BLK, :].astype(BF16)
        dkacc[0:ATT_BLK, :] = dkacc[ATT_BLK:2 * ATT_BLK, :]
        dvacc[0:ATT_BLK, :] = dvacc[ATT_BLK:2 * ATT_BLK, :]
        dkacc[ATT_BLK:, :] = jnp.zeros((ATT_BLK + CHUNK, ATTN_W), F32)
        dvacc[ATT_BLK:, :] = jnp.zeros((ATT_BLK + CHUNK, ATTN_W), F32)

    outs, moved = _pcall(
        body, name, (nblk + 1,),
        [cur(0), prev(1), cur(1), prev(2), cur(2),
         pl.BlockSpec((ATT_BLK, ATTN_W), lambda s: (jnp.minimum(s, nblk - 1), 0)),
         pl.BlockSpec((8, N_PAIRS, BAND_PAD, 128), lambda s: (jnp.minimum(s, nblk - 1), 0, 0, 0))],
        [late, pl.BlockSpec((HEADS // 2, BAND_PAD, 128), lambda s: (0, 0, 0))],
        [_sds((t, 3 * ATTN_W), BF16), _sds((HEADS // 2, BAND_PAD, 128), F32)],
        [pltpu.VMEM((BAND_ROWS, ATTN_W), BF16), pltpu.VMEM((BAND_ROWS, ATTN_W), BF16),
         pltpu.VMEM((BAND_ROWS, ATTN_W), F32), pltpu.VMEM((BAND_ROWS, ATTN_W), F32),
         pltpu.VMEM((N_PAIRS, BAND_PAD, 128), F32), pltpu.VMEM((N_PAIRS, BAND_PAD, 128), BF16),
         pltpu.VMEM((N_PAIRS, 2 * CHUNK, 128), BF16), pltpu.VMEM((N_PAIRS, 2 * CHUNK, 128), BF16),
         pltpu.VMEM((ATT_BLK, ATTN_W), BF16), pltpu.VMEM((ATT_BLK, ATTN_W), BF16)],
        _cparams("arbitrary"), (proj, proj, proj, proj, proj, datt, probs), comm)
    return outs if comm is None else (*outs, moved)


def _diag_onehot(rel_rows):
    d0 = lax.broadcasted_iota(jnp.int32, (BIAS_LANES, BIAS_LANES), 0)
    d1 = lax.broadcasted_iota(jnp.int32, (BIAS_LANES, BIAS_LANES), 1)
    m, n = (d0, d1) if rel_rows else (d1, d0)
    hit = (m == jnp.minimum(BAND - 1 + MAX_REL - n, 2 * MAX_REL)) & (n < BAND + CHUNK - 1)
    return jnp.where(hit, 1.0, 0.0).astype(F32)


def _bias_table(name, rel_bias_l):
    rel_pad = jnp.pad(rel_bias_l, ((0, 0), (0, BIAS_LANES - N_REL)))

    def body(r_ref, o_ref):
        diag = jnp.dot(r_ref[...], _diag_onehot(True), preferred_element_type=F32,
                       precision=lax.Precision.HIGHEST)
        rowid = lax.broadcasted_iota(jnp.int32, (8, BIAS_LANES), 0)
        lane = lax.broadcasted_iota(jnp.int32, (8, BIAS_LANES), 1)
        for h in range(HEADS):
            d8 = jnp.broadcast_to(diag[h:h + 1, :], (8, BIAS_LANES))
            slab0 = pltpu.roll(d8, BIAS_LANES - CHUNK + 1, axis=1)
            for b in range(1, 8):
                slab0 = jnp.where(rowid == b, pltpu.roll(d8, BIAS_LANES - CHUNK + 1 + b, axis=1),
                                  slab0)
            for a in range(8):
                slab = slab0 if a == 0 else pltpu.roll(slab0, 8 * a, axis=1)
                o_ref[h * CHUNK + 8 * a:h * CHUNK + 8 * a + 8, :] = jnp.where(lane < BAND, slab, NEG)

    tab = pl.pallas_call(
        body, name=name,
        in_specs=[pl.BlockSpec(memory_space=pltpu.VMEM)],
        out_specs=pl.BlockSpec(memory_space=pltpu.VMEM),
        out_shape=_sds((HEADS * CHUNK, BIAS_LANES), F32),
    )(rel_pad)
    tab = tab.reshape(HEADS // 2, 2, CHUNK, BIAS_LANES)
    return jnp.transpose(tab, (0, 3, 1, 2)).reshape(HEADS // 2, BIAS_LANES, 2 * CHUNK)


def _bias_fold(name, dbias_t):
    rows = HEADS * CHUNK
    dbias = jnp.transpose(dbias_t.reshape(HEADS // 2, BIAS_LANES, 2, CHUNK), (0, 2, 3, 1))

    def body(d_ref, o_ref):
        rowid = lax.broadcasted_iota(jnp.int32, (8, BIAS_LANES), 0)
        diags = []
        for h in range(HEADS):
            acc = d_ref[h * CHUNK + 56:h * CHUNK + 64, :]
            for a in range(7):
                slab = d_ref[h * CHUNK + 8 * a:h * CHUNK + 8 * a + 8, :]
                acc = acc + pltpu.roll(slab, 56 - 8 * a, axis=1)
            tot = jnp.where(rowid == 7, acc, 0.0)
            for b in range(7):
                tot = tot + jnp.where(rowid == b, pltpu.roll(acc, 7 - b, axis=1), 0.0)
            diags.append(jnp.sum(tot, axis=0, keepdims=True))
        diag = jnp.concatenate(diags, axis=0)
        o_ref[...] = jnp.dot(diag, _diag_onehot(False), preferred_element_type=F32,
                             precision=lax.Precision.HIGHEST)

    return pl.pallas_call(
        body, name=name,
        in_specs=[pl.BlockSpec(memory_space=pltpu.VMEM)],
        out_specs=pl.BlockSpec(memory_space=pltpu.VMEM),
        out_shape=_sds((HEADS, BIAS_LANES), F32),
    )(dbias.reshape(rows, BIAS_LANES))


def _inv_counts(i):
    trow = lax.broadcasted_iota(jnp.int32, (TOK + HALO, 1), 0) + i * TOK
    return [1.0 / jnp.minimum(trow + 1, w).astype(F32) for w in POOL_WINDOWS]


def _pool_fwd(name, proj, wg, scale, comm=None):
    t = proj.shape[0]
    hb = TOK // HALO

    def body(u_ref, up_ref, wg_ref, sc_ref, pooled_ref, mixed_ref, b0, b1, b2, b3):
        i = pl.program_id(0)
        halo = up_ref[...].astype(F32)
        b0[0:HALO, :] = jnp.where(i == 0, jnp.zeros_like(halo), halo)
        b0[HALO:, :] = u_ref[...].astype(F32)
        n = TOK + HALO
        b1[8:n, :] = b0[8:n, :] + b0[7:n - 1, :]
        b2[16:n, 128:] = b1[16:n, 128:] + b1[14:n - 2, 128:]
        b3[24:n, 256:] = b2[24:n, 256:] + b2[20:n - 4, 256:]
        wins = [b1[HALO:n, 0:128], b2[HALO:n, 128:256], b3[HALO:n, 256:384],
                b3[HALO:n, 384:512] + b3[HALO - 8:n - 8, 384:512]]
        inv = _inv_counts(i)
        for g in range(4):
            cols = slice(g * POOL_GD, (g + 1) * POOL_GD)
            pooled = (wins[g] * inv[g][0:TOK] - b0[HALO:n, cols]).astype(BF16)
            pooled_ref[:, cols] = pooled
            pre = jnp.dot(pooled, wg_ref[g], preferred_element_type=F32)
            mixed_ref[:, cols] = (pre * sc_ref[:, cols]).astype(BF16)

    buf = pltpu.VMEM((TOK + HALO, POOL_W), F32)
    outs, moved = _pcall(
        body, name, (t // TOK,),
        [_row_spec(POOL_W, 3),
         pl.BlockSpec((HALO, POOL_W), lambda i: (jnp.maximum(i * hb - 1, 0), 3)),
         pl.BlockSpec((4, POOL_GD, POOL_GD), lambda i: (0, 0, 0)), _vec_spec(POOL_W)],
        [_row_spec(POOL_W), _row_spec(POOL_W)],
        [_sds((t, POOL_W), BF16), _sds((t, POOL_W), BF16)], [buf, buf, buf, buf],
        _cparams("arbitrary"), (proj, proj, wg, scale), comm)
    return outs if comm is None else (*outs, moved)


def _pool_bwd(name, dmixed, pooled, wg, scale, comm=None):
    t = dmixed.shape[0]
    nt = t // TOK
    hb = TOK // HALO

    def body(dm_ref, dmn_ref, p_ref, wg_ref, sc_ref, du_ref, dwg_ref, dsc_ref, c0, c1, c2, c3):
        i = pl.program_id(0)

        @pl.when(i == 0)
        def _():
            dwg_ref[...] = jnp.zeros_like(dwg_ref)
            dsc_ref[...] = jnp.zeros_like(dsc_ref)

        n = TOK + HALO
        inv = _inv_counts(i)
        dmv = dm_ref[...].astype(F32)
        dmn = dmn_ref[...].astype(F32)
        dmn = jnp.where(i == nt - 1, jnp.zeros_like(dmn), dmn)
        for g in range(4):
            cols = slice(g * POOL_GD, (g + 1) * POOL_GD)
            scg = sc_ref[:, cols]
            pg = p_ref[:, cols]
            dpre = (dmv[:, cols] * scg).astype(BF16)
            dpre_n = (dmn[:, cols] * scg).astype(BF16)
            pre = jnp.dot(pg, wg_ref[g], preferred_element_type=F32)
            dsc_ref[:, cols] += jnp.sum(dmv[:, cols] * pre, axis=0, keepdims=True)
            dwg_ref[g] += lax.dot_general(pg, dpre, (TN, ((), ())), preferred_element_type=F32)
            dpool = lax.dot_general(dpre, wg_ref[g], (NT, ((), ())), preferred_element_type=F32)
            dpool_n = lax.dot_general(dpre_n, wg_ref[g], (NT, ((), ())),
                                      preferred_element_type=F32)
            c0[0:TOK, cols] = dpool
            c0[TOK:n, cols] = dpool_n
            c1[0:TOK, cols] = dpool * inv[g][0:TOK]
            c1[TOK:n, cols] = dpool_n * inv[g][TOK:n]
        c2[0:n - 8, :] = c1[0:n - 8, :] + c1[1:n - 7, :]
        c3[0:n - 16, 128:] = c2[0:n - 16, 128:] + c2[2:n - 14, 128:]
        c1[0:n - 24, 256:] = c3[0:n - 24, 256:] + c3[4:n - 20, 256:]
        wins = [c2[0:TOK, 0:128], c3[0:TOK, 128:256], c1[0:TOK, 256:384],
                c1[0:TOK, 384:512] + c1[8:TOK + 8, 384:512]]
        for g in range(4):
            cols = slice(g * POOL_GD, (g + 1) * POOL_GD)
            du_ref[:, cols] = (wins[g] - c0[0:TOK, cols]).astype(BF16)

    buf = pltpu.VMEM((TOK + HALO, POOL_W), F32)
    outs, moved = _pcall(
        body, name, (nt,),
        [_row_spec(POOL_W),
         pl.BlockSpec((HALO, POOL_W), lambda i: (jnp.minimum((i + 1) * hb, nt * hb - 1), 0)),
         _row_spec(POOL_W), pl.BlockSpec((4, POOL_GD, POOL_GD), lambda i: (0, 0, 0)),
         _vec_spec(POOL_W)],
        [_row_spec(POOL_W), pl.BlockSpec((4, POOL_GD, POOL_GD), lambda i: (0, 0, 0)),
         _vec_spec(POOL_W)],
        [_sds((t, POOL_W), BF16), _sds((4, POOL_GD, POOL_GD), F32), _sds((1, POOL_W), F32)],
        [buf, buf, buf, buf], _cparams("arbitrary"), (dmixed, dmixed, pooled, wg, scale), comm)
    return outs if comm is None else (*outs, moved)


GELU_C = math.sqrt(2.0 / math.pi)


GELU_K = 0.044715


def _gelu_parts(x):
    x2 = x * x
    s = 0.5 + 0.5 * jnp.tanh(x * (GELU_C + (GELU_C * GELU_K) * x2))
    return x * s, s, x2


def _gelu(x):
    return _gelu_parts(x)[0]


def _gelu_and_grad(x):
    g, s, x2 = _gelu_parts(x)
    return g, s + g * (1.0 - s) * ((2 * GELU_C) + (6 * GELU_C * GELU_K) * x2)


def _taps(buf, r, rows):
    a = buf[pl.ds(r, rows + 8), :]
    return a[8:], pltpu.roll(a, 1, axis=0)[8:], pltpu.roll(a, 2, axis=0)[8:]


def _conv(taps, w_ref, b_ref):
    return b_ref[...] + w_ref[2:3, :] * taps[0] + w_ref[1:2, :] * taps[1] + w_ref[0:1, :] * taps[2]


def _stage(dst, prev_ref, cur_ref, next_ref, first, last):
    rows = cur_ref.shape[0]
    h = prev_ref[...].astype(F32)
    dst[0:8, :] = jnp.where(first, jnp.zeros_like(h), h)
    dst[8:8 + rows, :] = cur_ref[...].astype(F32)
    if next_ref is not None:
        h = next_ref[...].astype(F32)
        dst[8 + rows:, :] = jnp.where(last, jnp.zeros_like(h), h)


FWD_STRIP = 32
BWD_STRIP = 16


def _ffn_gate_fwd(name, hu, conv_w, conv_b, comm=None):
    t = hu.shape[0]
    ncol = D_FF // FF_COL
    hb = FF_TOK // 8

    def tile(off):
        return pl.BlockSpec((FF_TOK, FF_COL), lambda i, j: (i, j + off))

    def halo(off):
        return pl.BlockSpec((8, FF_COL), lambda i, j: (jnp.maximum(i * hb - 1, 0), j + off))

    def wspec(off):
        return pl.BlockSpec((3, FF_COL), lambda i, j: (0, j + off))

    def bspec(off):
        return pl.BlockSpec((1, FF_COL), lambda i, j: (0, j + off))

    def body(v_ref, vp_ref, g_ref, gp_ref, wv_ref, wg_ref, bv_ref, bg_ref, a_ref, hc_ref, vb, gb):
        first = pl.program_id(0) == 0
        _stage(vb, vp_ref, v_ref, None, first, None)
        _stage(gb, gp_ref, g_ref, None, first, None)

        def strip(k, carry):
            for u in range(2):
                r = pl.multiple_of((2 * k + u) * FWD_STRIP, FWD_STRIP)
                val = _conv(_taps(vb, r, FWD_STRIP), wv_ref, bv_ref)
                gate = _conv(_taps(gb, r, FWD_STRIP), wg_ref, bg_ref)
                a_ref[pl.ds(r, FWD_STRIP), :] = (_gelu(gate) * val).astype(BF16)
                hc_ref[0, pl.ds(r, FWD_STRIP), :] = val.astype(BF16)
                hc_ref[1, pl.ds(r, FWD_STRIP), :] = gate.astype(BF16)
            return carry

        lax.fori_loop(0, FF_TOK // (2 * FWD_STRIP), strip, 0)

    buf = pltpu.VMEM((FF_TOK + 8, FF_COL), F32)
    outs, moved = _pcall(
        body, name, (t // FF_TOK, ncol),
        [tile(0), halo(0), tile(ncol), halo(ncol), wspec(0), wspec(ncol), bspec(0), bspec(ncol)],
        [pl.BlockSpec((FF_TOK, FF_COL), lambda i, j: (i, j)),
         pl.BlockSpec((2, FF_TOK, FF_COL), lambda i, j: (0, i, j))],
        [_sds((t, D_FF), BF16), _sds((2, t, D_FF), BF16)], [buf, buf],
        _cparams("arbitrary", "arbitrary"),
        (hu, hu, hu, hu, conv_w, conv_w, conv_b, conv_b), comm)
    return outs if comm is None else (*outs, moved)


def _ffn_gate_bwd(name, da, hu, hc, conv_w, comm=None):
    t = hu.shape[0]
    nt = t // FF_TOK
    ncol = D_FF // FF_COL
    hb = FF_TOK // 8

    def tile(off):
        return pl.BlockSpec((FF_TOK, FF_COL), lambda j, i: (i, j + off))

    def nxt_rows(i):
        return jnp.minimum((i + 1) * hb, nt * hb - 1)

    def wspec(off):
        return pl.BlockSpec((3, FF_COL), lambda j, i: (0, j + off))

    def body(da_ref, dan_ref, v_ref, g_ref, hc_ref, hcn_ref, wv_ref, wg_ref,
             dh_ref, dwv_ref, dwg_ref):
        i = pl.program_id(1)
        first, last = i == 0, i == nt - 1

        @pl.when(first)
        def _():
            dwv_ref[...] = jnp.zeros_like(dwv_ref)
            dwg_ref[...] = jnp.zeros_like(dwg_ref)

        def grads(dav, val, gate):
            g, dg = _gelu_and_grad(gate.astype(F32))
            dav = dav.astype(F32)
            return dav * g, dav * val.astype(F32) * dg

        def fold(x):
            return x[0:8] + x[8:16]

        def strip(j, carry):
            for u in range(2):
                carry = one_strip(2 * j + u, carry)
            return carry

        def one_strip(k, carry):
            r = pl.multiple_of(FF_TOK - BWD_STRIP - k * BWD_STRIP, BWD_STRIP)
            rows = pl.ds(r, BWD_STRIP)
            dval, dgate = grads(da_ref[rows, :], hc_ref[0, rows, :], hc_ref[1, rows, :])
            new = (dval[0:8], dgate[0:8])
            for half, (d, below, h_ref, w_ref, dw_ref) in enumerate((
                    (dval, carry[0], v_ref, wv_ref, dwv_ref),
                    (dgate, carry[1], g_ref, wg_ref, dwg_ref))):
                e = jnp.concatenate([d, below], axis=0)
                e1 = pltpu.roll(e, BWD_STRIP + 7, axis=0)[0:BWD_STRIP]
                e2 = pltpu.roll(e, BWD_STRIP + 6, axis=0)[0:BWD_STRIP]
                dh = w_ref[2:3, :] * d + w_ref[1:2, :] * e1 + w_ref[0:1, :] * e2
                dh_ref[half, rows, :] = dh.astype(BF16)
                huv = h_ref[rows, :].astype(F32)
                dw_ref[0:8, :] += fold(e2 * huv)
                dw_ref[8:16, :] += fold(e1 * huv)
                dw_ref[16:24, :] += fold(d * huv)
                dw_ref[24:32, :] += fold(d)
            return new

        dan = dan_ref[...]
        dan = jnp.where(last, jnp.zeros_like(dan), dan)
        lax.fori_loop(0, FF_TOK // (2 * BWD_STRIP), strip, grads(dan, hcn_ref[0], hcn_ref[1]))

        @pl.when(last)
        def _():
            for dw_ref in (dwv_ref, dwg_ref):
                for q in range(4):
                    dw_ref[8 * q:8 * q + 1, :] = jnp.sum(dw_ref[8 * q:8 * q + 8, :], axis=0,
                                                         keepdims=True)

    acc = pl.BlockSpec((32, FF_COL), lambda j, i: (0, j))
    (dhu, dwv, dwg), moved = _pcall(
        body, name, (ncol, nt),
        [tile(0), pl.BlockSpec((8, FF_COL), lambda j, i: (nxt_rows(i), j)),
         tile(0), tile(ncol),
         pl.BlockSpec((2, FF_TOK, FF_COL), lambda j, i: (0, i, j)),
         pl.BlockSpec((2, 8, FF_COL), lambda j, i: (0, nxt_rows(i), j)),
         wspec(0), wspec(ncol)],
        [pl.BlockSpec((2, FF_TOK, FF_COL), lambda j, i: (0, i, j)), acc, acc],
        [_sds((2, t, D_FF), BF16), _sds((32, D_FF), F32), _sds((32, D_FF), F32)],
        [], _cparams("arbitrary", "arbitrary"),
        (da, da, hu, hu, hc, hc, conv_w, conv_w), comm)
    dconv = jnp.concatenate([dwv, dwg], axis=1).reshape(4, 8, 2 * D_FF)[:, 0]
    return (dhu, dconv) if comm is None else (dhu, dconv, moved)


def _mesh_pos():
    x, y, c = lax.axis_index("x"), lax.axis_index("y"), lax.axis_index("c")
    return x, y, c, [(1 - x, y), (x, 1 - y), (1 - x, 1 - y)]


def _remote(src, dst, send_sems, recv_sems, i, dev):
    return pltpu.make_async_remote_copy(src_ref=src, dst_ref=dst, send_sem=send_sems.at[i],
                                        recv_sem=recv_sems.at[i], device_id=dev,
                                        device_id_type=MESH)


def _mine(c, rows):
    return pl.ds(pl.multiple_of(c * (rows // 2), 16), rows // 2)


def _gather_send(shards, conv_shard, gathered, l):
    nbig = len(shards)
    with_conv = conv_shard is not None
    if gathered is None:
        ins = list(shards) + ([conv_shard] if with_conv else [])
        outs = [_sds((DEPTH, N_CHIPS) + s.shape[1:], s.dtype) for s in ins]
        alias = {}
    else:
        ins = list(shards) + list(gathered)
        outs = [_sds(g.shape, g.dtype) for g in gathered]
        alias = {nbig + k: k for k in range(nbig)}

    def copies(cin, cout, ssem, rsem):
        x, y, c, chips = _mesh_pos()
        me = 2 * x + y
        out = []
        for k in range(nbig):
            rows = shards[k].shape[1]
            for j, (cx, cy) in enumerate(chips):
                out.append(_remote(cin[k].at[l, _mine(c, rows)], cout[k].at[l, me, _mine(c, rows)],
                                   ssem, rsem, 4 * k + j, (cx, cy, c)))
            out.append(_remote(cin[k].at[l], cout[k].at[l, me], ssem, rsem, 4 * k + 3,
                               (x, y, 1 - c)))
        if with_conv:
            base = 4 * nbig
            for j, (cx, cy) in enumerate(chips):
                out.append(_remote(cin[nbig].at[c], cout[nbig].at[c, me], ssem, rsem, base + j,
                                   (cx, cy, c)))
            for ll in range(DEPTH):
                out.append(_remote(cin[nbig].at[ll], cout[nbig].at[ll, me], ssem, rsem,
                                   base + 3 + ll, (x, y, 1 - c)))
        return out

    return _Comm(ins, outs, copies, 4 * nbig + 5, alias)


def _gather_forward(gathered, nbig, rows, l):
    with_conv = len(gathered) > nbig
    alias = {k: k for k in range(len(gathered))}

    def copies(cin, cout, ssem, rsem):
        x, y, c, chips = _mesh_pos()
        out = []
        for k in range(nbig):
            for j, (cx, cy) in enumerate(chips):
                blk = cout[k].at[l, 2 * cx + cy, _mine(c, rows[k])]
                out.append(_remote(blk, blk, ssem, rsem, 3 * k + j, (x, y, 1 - c)))
        if with_conv:
            for j, (cx, cy) in enumerate(chips):
                blk = cout[nbig].at[c, 2 * cx + cy]
                out.append(_remote(blk, blk, ssem, rsem, 3 * nbig + j, (x, y, 1 - c)))
        return out

    return _Comm(gathered, [_sds(g.shape, g.dtype) for g in gathered], copies, 3 * nbig + 3, alias)


def _reduce_swap(grads, l):
    def copies(cin, cout, ssem, rsem):
        x, y, c, _ = _mesh_pos()
        return [_remote(cin[k].at[l, :, _mine(1 - c, g.shape[2])], cout[k], ssem, rsem, k,
                        (x, y, 1 - c)) for k, g in enumerate(grads)]

    outs = [_sds((N_CHIPS, g.shape[2] // 2, g.shape[3]), g.dtype) for g in grads]
    return _Comm(grads, outs, copies, len(grads))


def _reduce_scatter(sums):
    def copies(cin, cout, ssem, rsem):
        x, y, c, chips = _mesh_pos()
        return [_remote(cin[k].at[2 * cx + cy], cout[k].at[j], ssem, rsem, 3 * k + j, (cx, cy, c))
                for k in range(len(sums)) for j, (cx, cy) in enumerate(chips)]

    outs = [_sds((3,) + s.shape[1:], s.dtype) for s in sums]
    return _Comm(sums, outs, copies, 3 * len(sums))


def _reduce_share(reds, l):
    def copies(cin, cout, ssem, rsem):
        x, y, c, _ = _mesh_pos()
        out = []
        for k, r in enumerate(reds):
            half = cout[k].at[l, _mine(c, r.shape[1])]
            out.append(_remote(half, half, ssem, rsem, k, (x, y, 1 - c)))
        return out

    return _Comm(reds, [_sds(r.shape, r.dtype) for r in reds], copies, len(reds),
                 {k: k for k in range(len(reds))})


def _allreduce_small(per_layer):
    kinds = len(per_layer[0])
    shapes = [a.shape[1:] if a.shape[0] == 1 else a.shape for a in per_layer[0]]

    def body(*refs):
        ins = refs[:DEPTH * kinds]
        outs = refs[DEPTH * kinds:(DEPTH + 1) * kinds]
        gbufs = refs[(DEPTH + 1) * kinds:(DEPTH + 2) * kinds]
        send_sems, recv_sems = refs[-2], refs[-1]
        x, y, c, chips = _mesh_pos()
        sibling = (x, y, 1 - c)

        def copy(k, i, block, to):
            px, py, pc = block
            slot = gbufs[k].at[4 * px + 2 * py + pc]
            return _remote(slot, slot, send_sems, recv_sems, 7 * k + i, to)

        me = (x, y, c)
        first, passed = [], []
        for k in range(kinds):
            for l in range(DEPTH):
                a = ins[l * kinds + k]
                if per_layer[l][k].shape[0] == 1:
                    gbufs[k][4 * x + 2 * y + c, l:l + 1] = a[...]
                else:
                    gbufs[k][4 * x + 2 * y + c, l] = a[...]
            first.append(copy(k, 0, me, sibling))
            first += [copy(k, 1 + j, me, (*chip, c)) for j, chip in enumerate(chips)]
            passed += [copy(k, 4 + j, (*chip, c), sibling) for j, chip in enumerate(chips)]
        for cp in first:
            cp.start()
        for k in range(kinds):
            for j, chip in enumerate(chips):
                copy(k, 1 + j, (*chip, c), me).wait_recv()
                passed[3 * k + j].start()
        for k in range(kinds):
            copy(k, 0, sibling, me).wait_recv()
            for j, chip in enumerate(chips):
                copy(k, 4 + j, (*chip, 1 - c), me).wait_recv()
        for cp in first + passed:
            cp.wait_send()
        for k in range(kinds):
            acc = gbufs[k][0]
            for d in range(1, 8):
                acc = acc + gbufs[k][d]
            outs[k][...] = acc

    vmem = pl.BlockSpec(memory_space=pltpu.VMEM)
    return pl.pallas_call(
        body, name="allreduce_small",
        in_specs=[vmem] * (DEPTH * kinds), out_specs=[vmem] * kinds,
        out_shape=[_sds((DEPTH,) + s, F32) for s in shapes],
        scratch_shapes=[pltpu.VMEM((8, DEPTH) + s, F32) for s in shapes]
        + [pltpu.SemaphoreType.DMA((7 * kinds,)), pltpu.SemaphoreType.DMA((7 * kinds,))],
        compiler_params=pltpu.CompilerParams(vmem_limit_bytes=VMEM_LIMIT_V7X),
    )(*per_layer[0], *per_layer[1])


def _adamw_small(ws, gs, ms, vs):
    n = len(ws)
    c1 = 1.0 - ADAM_B1 ** ADAM_STEP
    c2 = 1.0 - ADAM_B2 ** ADAM_STEP

    def body(*refs):
        for i in range(n):
            w_ref, g_ref, m_ref, v_ref = (refs[j * n + i] for j in range(4))
            d_ref, nm_ref, nv_ref = (refs[(4 + j) * n + i] for j in range(3))
            gv = g_ref[...]
            nm = ADAM_B1 * m_ref[...] + (1.0 - ADAM_B1) * gv
            nv = ADAM_B2 * v_ref[...] + (1.0 - ADAM_B2) * (gv * gv)
            nm_ref[...] = nm
            nv_ref[...] = nv
            d_ref[...] = -ADAM_LR * ((nm / c1) / (jnp.sqrt(nv / c2) + ADAM_EPS)
                                     + ADAM_WD * w_ref[...])

    vmem = pl.BlockSpec(memory_space=pltpu.VMEM)
    outs = pl.pallas_call(
        body, name="adamw_small", in_specs=[vmem] * (4 * n), out_specs=[vmem] * (3 * n),
        out_shape=[_sds(w.shape, F32) for w in ws] * 3,
        compiler_params=pltpu.CompilerParams(vmem_limit_bytes=VMEM_LIMIT_V7X),
    )(*ws, *gs, *ms, *vs)
    return outs[:n], outs[n:2 * n], outs[2 * n:]


def _core_index():
    return jnp.reshape(lax.axis_index("c"), (1,)).astype(jnp.int32)


def _chip_index():
    return jnp.reshape(2 * lax.axis_index("x") + lax.axis_index("y"), (1,)).astype(jnp.int32)


def _chip_sums(name, stacked, sibs, l):
    n = len(stacked)
    dims = [(s.shape[2] // 2, s.shape[3]) for s in stacked]

    def body(c_ref, *refs):
        for k in range(n):
            a_ref, b_ref, o_ref = refs[k], refs[n + k], refs[2 * n + k]
            o_ref[...] = (a_ref[...].astype(F32) + b_ref[...].astype(F32)).astype(BF16)

    return pl.pallas_call(
        body, name=name,
        grid_spec=pltpu.PrefetchScalarGridSpec(
            num_scalar_prefetch=1, grid=(N_CHIPS,),
            in_specs=[pl.BlockSpec((None, None, hr, cd), lambda j, cr: (l, j, cr[0], 0))
                      for hr, cd in dims]
            + [pl.BlockSpec((None, hr, cd), lambda j, cr: (j, 0, 0)) for hr, cd in dims],
            out_specs=[pl.BlockSpec((None, hr, cd), lambda j, cr: (j, 0, 0)) for hr, cd in dims]),
        out_shape=[_sds((N_CHIPS, hr, cd), BF16) for hr, cd in dims],
        compiler_params=_cparams("parallel"))(_core_index(), *stacked, *sibs)


def _final_sums(name, sums, recvs, l, fills):
    n = len(sums)
    dims = [(s.shape[1] // 2, s.shape[2]) for s in sums]
    filled = fills[0] is not None

    def body(m_ref, *refs):
        outs = refs[-n:]
        for k in range(n):
            acc = refs[k][...].astype(F32)
            for j in range(3):
                acc = acc + refs[n + k][j].astype(F32)
            outs[k][...] = acc

    in_specs = ([pl.BlockSpec((None, tr, cd), lambda i, mr: (mr[0], i, 0)) for tr, cd in dims]
                + [pl.BlockSpec((3, tr, cd), lambda i, mr: (0, i, 0)) for tr, cd in dims])
    args = [jnp.concatenate([_chip_index(), _core_index()]), *sums, *recvs]
    aliases = {}
    if filled:
        in_specs += [pl.BlockSpec(memory_space=pl.ANY)] * n
        args += list(fills)
        aliases = {1 + 2 * n + k: k for k in range(n)}
    return pl.pallas_call(
        body, name=name,
        grid_spec=pltpu.PrefetchScalarGridSpec(
            num_scalar_prefetch=1, grid=(2,), in_specs=in_specs,
            out_specs=[pl.BlockSpec((None, tr, cd), lambda i, mr: (l, 2 * mr[1] + i, 0))
                       for tr, cd in dims]),
        out_shape=[_sds((DEPTH, 4 * tr, cd), F32) for tr, cd in dims],
        input_output_aliases=aliases,
        compiler_params=_cparams("parallel"))(*args)


def _adamw(name, w, g, m, v):
    nl, r, cdim = w.shape
    tr = r // 4 if r % 32 == 0 else r
    c1 = 1.0 - ADAM_B1 ** ADAM_STEP
    c2 = 1.0 - ADAM_B2 ** ADAM_STEP

    def body(w_ref, g_ref, m_ref, v_ref, d_ref, nm_ref, nv_ref, go_ref):
        gv = g_ref[...]
        go_ref[...] = gv
        nm = ADAM_B1 * m_ref[...] + (1.0 - ADAM_B1) * gv
        nv = ADAM_B2 * v_ref[...] + (1.0 - ADAM_B2) * (gv * gv)
        nm_ref[...] = nm
        nv_ref[...] = nv
        d_ref[...] = -ADAM_LR * ((nm / c1) / (jnp.sqrt(nv / c2) + ADAM_EPS) + ADAM_WD * w_ref[...])

    spec = pl.BlockSpec((None, tr, cdim), lambda l, i: (l, i, 0))
    out = _sds(w.shape, F32)
    return pl.pallas_call(
        body, name=name, grid=(nl, r // tr), in_specs=[spec] * 4, out_specs=[spec] * 4,
        out_shape=[out] * 4, compiler_params=_cparams("parallel", "parallel"))(w, g, m, v)


def kernel(x, norm_mix_pre, w_in, b_gate, rel_bias, w_attn_out, w_pool_group, pool_scale, w_pool_out, w_o, norm_mix_post, norm_ffn_pre, w_up, conv_w, conv_b, w_down, norm_ffn_post, loss_target, m_norm_mix_pre, m_w_in, m_b_gate, m_rel_bias, m_w_attn_out, m_w_pool_group, m_pool_scale, m_w_pool_out, m_w_o, m_norm_mix_post, m_norm_ffn_pre, m_w_up, m_conv_w, m_conv_b, m_w_down, m_norm_ffn_post, v_norm_mix_pre, v_w_in, v_b_gate, v_rel_bias, v_w_attn_out, v_w_pool_group, v_pool_scale, v_w_pool_out, v_w_o, v_norm_mix_post, v_norm_ffn_pre, v_w_up, v_conv_w, v_conv_b, v_w_down, v_norm_ffn_post):
    t = x.shape[1]
    xs = x.reshape(t, D_MODEL)
    target = loss_target.reshape(t, D_MODEL)

    names = ["w_in", "w_attn_out", "w_pool_out", "w_o", "w_up", "w_down"]
    shards = [w.astype(BF16) for w in (w_in, w_attn_out, w_pool_out, w_o, w_up, w_down)]
    rows = [s.shape[1] for s in shards]
    nbig = len(shards)
    h, g = _norm_fwd("l0_norm_mix_pre", x.reshape(t, D_MODEL), norm_mix_pre[0:1],
                     _gather_send(shards[:1], conv_w, None, 0))
    g = _comm_call("gather0_forward", _gather_forward(g, 1, rows[:1], 0))
    cw_full = jnp.transpose(g[1], (0, 2, 1, 3)).reshape(DEPTH, 3, 2 * D_FF)
    g = g[:1]
    wg_bf = w_pool_group.astype(BF16)

    def views(gathered):
        win_g, wao_g, wpo_g, wo_g, wup_g, wdn_g = gathered
        return (win_g, wao_g, wpo_g, wo_g.reshape(DEPTH, D_MODEL, D_MODEL), wup_g,
                wdn_g.reshape(DEPTH, D_FF, D_MODEL))

    saved = []
    xcur = xs
    for l in range(DEPTH):
        tag = f"l{l}_"
        bias = _bias_table(tag + "bias_table", rel_bias[l])
        proj = _mm_nn_blocked(tag + "proj", h, g[0], l, BF16)
        if l == 0:
            att, probs, rest = _attn_fwd(tag + "attn_fwd", proj, bias,
                                         _gather_send(shards[1:], None, None, 0))
            pooled, mixed, rest = _pool_fwd(tag + "pool_fwd", proj, wg_bf[l], pool_scale[l:l + 1],
                                            _gather_forward(rest, nbig - 1, rows[1:], 0))
            g = g + rest
        else:
            att, probs = _attn_fwd(tag + "attn_fwd", proj, bias)
            pooled, mixed = _pool_fwd(tag + "pool_fwd", proj, wg_bf[l], pool_scale[l:l + 1])
        win_g, wao_g, wpo_g, wo_full, wup_g, wdn_full = views(g)
        ya = _narrow_nn(tag + "attn_out", att, wao_g, l)
        yb = _narrow_nn(tag + "pool_out", mixed, wpo_g, l)
        z = _gate_fwd(tag + "gate_fwd", proj, b_gate[l:l + 1], ya, yb)
        mix = _mm_nn(tag + "mix", z, wo_full, l, D_MODEL, F32)
        x1, h2 = _post_pre_fwd(tag + "norm_mix_post", xcur, mix, norm_mix_post[l:l + 1],
                               norm_ffn_pre[l:l + 1])
        if l == 0:
            hu, mixing = _mm_nn_blocked(tag + "ffn_up", h2, wup_g, l, BF16,
                                        _gather_send(shards[:4], None, g[:4], 1))
            a, hc, ffn_g = _ffn_gate_fwd(tag + "ffn_gate_fwd", hu, cw_full[l], conv_b[l:l + 1],
                                         _gather_send(shards[4:], None, g[4:], 1))
            g = mixing + ffn_g
            wdn_full = views(g)[5]
        else:
            hu = _mm_nn_blocked(tag + "ffn_up", h2, wup_g, l, BF16)
            a, hc = _ffn_gate_fwd(tag + "ffn_gate_fwd", hu, cw_full[l], conv_b[l:l + 1])
        f = _mm_nn(tag + "ffn_down", a, wdn_full, l, D_FF, F32)
        saved.append(dict(x=xcur, h=h, proj=proj, att=att, pooled=pooled, mixed=mixed, ya=ya,
                          yb=yb, z=z, mix=mix, x1=x1, h2=h2, hu=hu, hc=hc, a=a, f=f, probs=probs))
        if l == 0:
            xcur, h, g = _post_pre_fwd(tag + "norm_ffn_post", x1, f, norm_ffn_post[l:l + 1],
                                       norm_mix_pre[l + 1:l + 2], _gather_forward(g, nbig, rows, 1))
        elif l < DEPTH - 1:
            xcur, h = _post_pre_fwd(tag + "norm_ffn_post", x1, f, norm_ffn_post[l:l + 1],
                                    norm_mix_pre[l + 1:l + 2])
    win_g, wao_g, wpo_g, wo_full, wup_g, wdn_full = views(g)

    dy, df, d_nfpost, loss_local = _tail("tail", saved[-1]["x1"], saved[-1]["f"],
                                         norm_ffn_post[DEPTH - 1:DEPTH], target)
    loss = lax.psum(loss_local, ("x", "y", "c"))

    dx = dy
    dws = dict.fromkeys(names)
    reds = [None] * nbig
    small_grads = [None] * DEPTH
    ffn = [4, 5]
    outs3 = [1, 2, 3]

    def blocks(ks):
        return [dws[names[k]].reshape(DEPTH, N_CHIPS, rows[k], -1) for k in ks]

    def chip_sums(ks, sib, l):
        return _chip_sums(f"chip_sums{l}_" + names[ks[0]], blocks(ks), sib, l)

    def final_sums(ks, sums, recv, l):
        outs = _final_sums(f"final_sums{l}_" + names[ks[0]], sums, recv, l, [reds[k] for k in ks])
        for k, r in zip(ks, outs):
            reds[k] = r

    for l in reversed(range(DEPTH)):
        tag = f"l{l}_"
        sv = saved[l]
        every = list(range(nbig))
        if l == 0:
            da, sib = _mm_nt(tag + "ffn_down_dx", df, wdn_full, l, D_FF // 2, BF16,
                             _reduce_swap(blocks(every), 1))
            sums = chip_sums(every, sib, 1)
        else:
            da = _mm_nt(tag + "ffn_down_dx", df, wdn_full, l, D_FF // 2, BF16)
        dws["w_down"] = _mm_tn(tag + "ffn_down_dw", sv["a"], df, D_FF // 2, l, dws["w_down"])
        if l == 0:
            dhu, dconv, recv = _ffn_gate_bwd(tag + "ffn_gate_bwd", da, sv["hu"], sv["hc"],
                                             cw_full[l], _reduce_scatter(sums))
            final_sums(every, sums, recv, 1)
            dh2, reds = _mm_nt_blocked(tag + "ffn_up_dx", dhu, wup_g, l, F32,
                                       _reduce_share(reds, 1))
        else:
            dhu, dconv = _ffn_gate_bwd(tag + "ffn_gate_bwd", da, sv["hu"], sv["hc"], cw_full[l])
            dh2 = _mm_nt_blocked(tag + "ffn_up_dx", dhu, wup_g, l, F32)
        dws["w_up"] = _mm_tn_blocked(tag + "ffn_up_dw", sv["h2"], dhu, l, dws["w_up"])
        if l == 0:
            dx1, d_nfpre, dmix, d_nmpost, sib = _pre_post_bwd(
                tag + "norm_ffn_pre_bwd", dh2, sv["x1"], dx, norm_ffn_pre[l:l + 1], sv["mix"],
                norm_mix_post[l:l + 1], _reduce_swap(blocks(ffn), 0))
            sums = chip_sums(ffn, sib, 0)
        else:
            dx1, d_nfpre, dmix, d_nmpost = _pre_post_bwd(
                tag + "norm_ffn_pre_bwd", dh2, sv["x1"], dx, norm_ffn_pre[l:l + 1], sv["mix"],
                norm_mix_post[l:l + 1])
        dz = _mm_nt(tag + "mix_dx", dmix, wo_full, l, D_MODEL, BF16)
        dws["w_o"] = _mm_tn(tag + "mix_dw", sv["z"], dmix, D_MODEL, l, dws["w_o"])
        dya, dyb, dgates, d_bgate = _gate_bwd(tag + "gate_bwd", dz, sv["proj"], b_gate[l:l + 1],
                                              sv["ya"], sv["yb"])
        datt = _narrow_nt(tag + "attn_out_dx", dya, wao_g, l)
        dws["w_attn_out"] = _narrow_tn(tag + "attn_out_dw", sv["att"], dya, l, dws["w_attn_out"])
        dmixed = _narrow_nt(tag + "pool_out_dx", dyb, wpo_g, l)
        dws["w_pool_out"] = _narrow_tn(tag + "pool_out_dw", sv["mixed"], dyb, l, dws["w_pool_out"])
        if l == 0:
            du, d_wg, d_pscale, sib = _pool_bwd(tag + "pool_bwd", dmixed, sv["pooled"], wg_bf[l],
                                                pool_scale[l:l + 1], _reduce_swap(blocks(outs3), 0))
            sums3 = chip_sums(outs3, sib, 0)
            dqkv, dbias, recv = _attn_bwd(
                tag + "attn_bwd", sv["proj"], datt, sv["probs"],
                _both(_reduce_scatter(sums), _reduce_scatter(sums3)))
            final_sums(ffn, sums, recv[:len(ffn)], 0)
            final_sums(outs3, sums3, recv[len(ffn):], 0)
        else:
            du, d_wg, d_pscale = _pool_bwd(tag + "pool_bwd", dmixed, sv["pooled"], wg_bf[l],
                                           pool_scale[l:l + 1])
            dqkv, dbias = _attn_bwd(tag + "attn_bwd", sv["proj"], datt, sv["probs"])
        d_rel = _bias_fold(tag + "bias_fold", dbias)
        if l == 0:
            dh, shared = _proj_dx(tag + "proj_dx", dqkv, du, dgates, win_g, l,
                                  _reduce_share([reds[k] for k in ffn + outs3], 0))
            for k, r in zip(ffn + outs3, shared):
                reds[k] = r
        else:
            dh = _proj_dx(tag + "proj_dx", dqkv, du, dgates, win_g, l)
        dws["w_in"] = _proj_dw(tag + "proj_dw", sv["h"], dqkv, du, dgates, l, dws["w_in"])
        small_grads[l] = [None, d_nmpost, d_nfpre, d_nfpost, d_bgate, d_rel, d_wg, d_pscale, dconv]
        if l > 0:
            dx, small_grads[l][0], df, d_nfpost = _pre_post_bwd(
                tag + "norm_mix_pre_bwd", dh, sv["x"], dx1, norm_mix_pre[l:l + 1],
                saved[l - 1]["f"], norm_ffn_post[l - 1:l])
        else:
            dx, small_grads[l][0] = _norm_pre_bwd(tag + "norm_mix_pre_bwd", dh, sv["x"], dx1,
                                                  norm_mix_pre[l:l + 1])

    grad_x = dx.reshape(x.shape)

    delta, new_m, new_v = {}, {}, {}
    sib = _comm_call("reduce_swap", _reduce_swap(blocks([0]), 0))
    sums = chip_sums([0], sib, 0)
    recv = _comm_call("reduce_scatter", _reduce_scatter(sums))
    final_sums([0], sums, recv, 0)
    g_big = _comm_call("reduce_share", _reduce_share([reds[0]], 0)) + reds[1:]

    (g_nmpre, g_nmpost, g_nfpre, g_nfpost, g_bgate, g_rel, g_wg, g_pscale,
     g_conv) = _allreduce_small(small_grads)
    g_rel = g_rel[:, :, :N_REL]
    g_cb = g_conv[:, 3]
    ncw = conv_w.shape[2]
    chip = 2 * lax.axis_index("x") + lax.axis_index("y")
    g_cw = lax.dynamic_slice_in_dim(g_conv[:, 0:3], chip * ncw, ncw, axis=2)

    grads = dict(norm_mix_pre=g_nmpre, w_in=g_big[0], b_gate=g_bgate, rel_bias=g_rel,
                 w_attn_out=g_big[1], w_pool_group=g_wg, pool_scale=g_pscale, w_pool_out=g_big[2],
                 w_o=g_big[3], norm_mix_post=g_nmpost, norm_ffn_pre=g_nfpre, w_up=g_big[4],
                 conv_w=g_cw, conv_b=g_cb, w_down=g_big[5], norm_ffn_post=g_nfpost)
    weights = dict(norm_mix_pre=norm_mix_pre, w_in=w_in, b_gate=b_gate, rel_bias=rel_bias,
                   w_attn_out=w_attn_out, w_pool_group=w_pool_group, pool_scale=pool_scale,
                   w_pool_out=w_pool_out, w_o=w_o, norm_mix_post=norm_mix_post,
                   norm_ffn_pre=norm_ffn_pre, w_up=w_up, conv_w=conv_w, conv_b=conv_b,
                   w_down=w_down, norm_ffn_post=norm_ffn_post)
    moms = dict(norm_mix_pre=(m_norm_mix_pre, v_norm_mix_pre), w_in=(m_w_in, v_w_in),
                b_gate=(m_b_gate, v_b_gate), rel_bias=(m_rel_bias, v_rel_bias),
                w_attn_out=(m_w_attn_out, v_w_attn_out),
                w_pool_group=(m_w_pool_group, v_w_pool_group),
                pool_scale=(m_pool_scale, v_pool_scale), w_pool_out=(m_w_pool_out, v_w_pool_out),
                w_o=(m_w_o, v_w_o), norm_mix_post=(m_norm_mix_post, v_norm_mix_post),
                norm_ffn_pre=(m_norm_ffn_pre, v_norm_ffn_pre), w_up=(m_w_up, v_w_up),
                conv_w=(m_conv_w, v_conv_w), conv_b=(m_conv_b, v_conv_b),
                w_down=(m_w_down, v_w_down), norm_ffn_post=(m_norm_ffn_post, v_norm_ffn_post))
    order = list(weights.keys())

    small_names = [nm for nm in order if nm not in names]
    for nm in names:
        delta[nm], new_m[nm], new_v[nm], grads[nm] = _adamw("adamw_" + nm, weights[nm], grads[nm],
                                                            *moms[nm])
    d_s, m_s, v_s = _adamw_small([weights[nm] for nm in small_names],
                                 [grads[nm] for nm in small_names],
                                 [moms[nm][0] for nm in small_names],
                                 [moms[nm][1] for nm in small_names])
    for i, nm in enumerate(small_names):
        delta[nm], new_m[nm], new_v[nm] = d_s[i], m_s[i], v_s[i]

    return (loss, grad_x, *[grads[nm] for nm in order], *[delta[nm] for nm in order],
            *[new_m[nm] for nm in order], *[new_v[nm] for nm in order])
```

```python
import functools
import math

import jax
import jax.numpy as jnp
from jax import lax
from jax.experimental import pallas as pl
from jax.experimental.pallas import tpu as pltpu

F32 = jnp.float32
BF16 = jnp.bfloat16
MESH = pl.DeviceIdType.MESH

D_MODEL = 1024
DEPTH = 2
CHUNK = 64
BAND_CHUNKS = 9
BAND = BAND_CHUNKS * CHUNK
HEADS = 8
HEAD_DIM = 64
ATTN_W = HEADS * HEAD_DIM
POOL_WINDOWS = (2, 4, 8, 16)
POOL_W = 512
POOL_GD = 128
MAX_REL = 256
N_REL = 2 * MAX_REL + 1
D_FF = 2816
IN_W = 3 * ATTN_W + POOL_W + 2 * D_MODEL
EPS = 1e-6
ATTN_SCALE = HEAD_DIM ** -0.5
BAND_PAD = 640
BIAS_LANES = BAND_PAD
N_CHIPS = 4

ADAM_LR = 0.001
ADAM_B1 = 0.9
ADAM_B2 = 0.999
ADAM_EPS = 1e-08
ADAM_WD = 0.01
ADAM_STEP = 10

VMEM_LIMIT_V7X = 56 * 1024 * 1024
TOK = 512
ATT_BLK = 8 * CHUNK
FF_COL = 256
FF_TOK = 1024
HALO = 32


def _cparams(*sem):
    return pltpu.CompilerParams(dimension_semantics=sem, vmem_limit_bytes=VMEM_LIMIT_V7X)


def _sds(shape, dtype):
    return jax.ShapeDtypeStruct(shape, dtype)


class _Comm:
    def __init__(self, ins, outs, copies, n_sems, alias=None):
        self.ins, self.outs, self.copies, self.n_sems = list(ins), list(outs), copies, n_sems
        self.alias = dict(alias or {})


class _SemsFrom:
    def __init__(self, sems, start):
        self.sems, self.start = sems, start

    @property
    def at(self):
        return self

    def __getitem__(self, i):
        return self.sems.at[self.start + i]


def _both(a, b):
    na, nao = len(a.ins), len(a.outs)

    def copies(cin, cout, ssem, rsem):
        return (a.copies(cin[:na], cout[:nao], ssem, rsem)
                + b.copies(cin[na:], cout[nao:], _SemsFrom(ssem, a.n_sems), _SemsFrom(rsem, a.n_sems)))

    alias = dict(a.alias)
    alias.update({na + i: nao + o for i, o in b.alias.items()})
    return _Comm(a.ins + b.ins, a.outs + b.outs, copies, a.n_sems + b.n_sems, alias)


def _pcall(body, name, grid, in_specs, out_specs, out_shape, scratch_shapes, compiler_params, args,
           comm=None, aliases=None):
    single = not isinstance(out_shape, (list, tuple))
    out_specs = [out_specs] if single else list(out_specs)
    out_shape = [out_shape] if single else list(out_shape)
    n_in, n_out = len(in_specs), len(out_specs)
    aliases = dict(aliases or {})
    if comm is None:
        res = pl.pallas_call(
            body, name=name, grid=grid, in_specs=list(in_specs), out_specs=out_specs,
            out_shape=out_shape, scratch_shapes=list(scratch_shapes),
            input_output_aliases=aliases, compiler_params=compiler_params)(*args)
        return (res[0] if single else res), None
    ci, co = len(comm.ins), len(comm.outs)

    def hosted(*refs):
        main_in, cin = refs[:n_in], refs[n_in:n_in + ci]
        main_out = refs[n_in + ci:n_in + ci + n_out]
        cout = refs[n_in + ci + n_out:n_in + ci + n_out + co]
        rest = refs[n_in + ci + n_out + co:]
        copies = comm.copies(cin, cout, rest[-2], rest[-1])
        ids = [pl.program_id(a) for a in range(len(grid))]
        first = functools.reduce(jnp.logical_and, [i == 0 for i in ids])
        last = functools.reduce(jnp.logical_and, [i == g - 1 for i, g in zip(ids, grid)])

        @pl.when(first)
        def _():
            for cp in copies:
                cp.start()

        body(*main_in, *main_out, *rest[:-2])

        @pl.when(last)
        def _():
            for cp in copies:
                cp.wait()

    for i, o in comm.alias.items():
        aliases[n_in + i] = n_out + o
    hbm = pl.BlockSpec(memory_space=pl.ANY)
    sems = pltpu.SemaphoreType.DMA((comm.n_sems,))
    res = pl.pallas_call(
        hosted, name=name, grid=grid, in_specs=list(in_specs) + [hbm] * ci,
        out_specs=out_specs + [hbm] * co, out_shape=out_shape + comm.outs,
        scratch_shapes=list(scratch_shapes) + [sems, sems],
        input_output_aliases=aliases, compiler_params=compiler_params)(*args, *comm.ins)
    return (res[0] if single else list(res[:n_out])), list(res[n_out:])


def _comm_call(name, comm):
    ci = len(comm.ins)

    def body(*refs):
        copies = comm.copies(refs[:ci], refs[ci:-2], refs[-2], refs[-1])
        for cp in copies:
            cp.start()
        for cp in copies:
            cp.wait()

    hbm = pl.BlockSpec(memory_space=pl.ANY)
    sems = pltpu.SemaphoreType.DMA((comm.n_sems,))
    return list(pl.pallas_call(
        body, name=name, in_specs=[hbm] * ci, out_specs=[hbm] * len(comm.outs),
        out_shape=comm.outs, scratch_shapes=[sems, sems],
        input_output_aliases=comm.alias)(*comm.ins))


def _matmul(name, a, b, a_spec, b_spec, o_spec, out_shape, grid, contract, nk, acc_shape,
            fill=None, comm=None):
    in_place = out_shape.dtype == F32

    def body(*refs):
        a_ref, b_ref = refs[0], refs[1]
        o_ref = refs[2 if fill is None else 3]
        scratch = refs[(3 if fill is None else 4):]
        part = lax.dot_general(a_ref[...], b_ref[...], (contract, ((), ())),
                               preferred_element_type=F32)
        if nk == 1:
            o_ref[...] = part.astype(o_ref.dtype)
        else:
            acc_ref = o_ref if in_place else scratch[0]
            k = pl.program_id(2)

            @pl.when(k == 0)
            def _():
                acc_ref[...] = part

            @pl.when(k > 0)
            def _():
                acc_ref[...] += part

            if not in_place:
                @pl.when(k == nk - 1)
                def _():
                    o_ref[...] = acc_ref[...].astype(o_ref.dtype)

    scratch = [] if nk == 1 or in_place else [pltpu.VMEM(acc_shape, F32)]
    in_specs, args, aliases = [a_spec, b_spec], [a, b], {}
    if fill is not None:
        in_specs.append(pl.BlockSpec(memory_space=pl.ANY))
        args.append(fill)
        aliases = {2: 0}
    out, moved = _pcall(body, name, grid, in_specs, o_spec, out_shape, scratch,
                        _cparams("parallel", "parallel", "arbitrary"), args, comm, aliases)
    return out if comm is None else (out, moved)


NN = ((1,), (0,))
NT = ((1,), (1,))
TN = ((0,), (0,))


def _tm(t):
    return min(t, 1024)


def _tt(t):
    return min(t, 2048)


def _col_block_spec(a, rows, nb, row_col):
    if a.ndim == 2:
        return pl.BlockSpec((rows, nb), row_col)

    def halves(*ids):
        r, c = row_col(*ids)
        return c // 2, r, c % 2

    return pl.BlockSpec((None, rows, nb), halves)


def _mm_nn_blocked(name, a, w, l, out_dtype, comm=None):
    t, k = a.shape
    nb = w.shape[3]
    tm = _tm(t)
    return _matmul(
        name, a, w,
        pl.BlockSpec((tm, k), lambda i, n, kk: (i, 0)),
        pl.BlockSpec((None, None, k, nb), lambda i, n, kk: (l, n, 0, 0)),
        pl.BlockSpec((tm, nb), lambda i, n, kk: (i, n)),
        _sds((t, N_CHIPS * nb), out_dtype), (t // tm, N_CHIPS, 1), NN, 1, None, comm=comm)


def _mm_nt_blocked(name, a, w, l, out_dtype, comm=None):
    t = a.shape[-2]
    k, nb = w.shape[2], w.shape[3]
    tm = _tm(t)
    return _matmul(
        name, a, w,
        _col_block_spec(a, tm, nb, lambda i, n, kk: (i, kk)),
        pl.BlockSpec((None, None, k, nb), lambda i, n, kk: (l, kk, 0, 0)),
        pl.BlockSpec((tm, k), lambda i, n, kk: (i, 0)),
        _sds((t, k), out_dtype), (t // tm, 1, N_CHIPS), NT, N_CHIPS, (tm, k), comm=comm)


def _mm_tn_blocked(name, a, g, l, fill):
    t, k = a.shape
    nb = g.shape[-1] * (g.ndim - 1) // N_CHIPS
    tt = _tt(t)
    nt = t // tt
    return _matmul(
        name, a, g,
        pl.BlockSpec((tt, k), lambda n, j, kk: (kk, 0)),
        _col_block_spec(g, tt, nb, lambda n, j, kk: (kk, n)),
        pl.BlockSpec((None, None, k, nb), lambda n, j, kk: (l, n, 0, 0)),
        _sds((DEPTH, N_CHIPS, k, nb), BF16), (N_CHIPS, 1, nt), TN, nt, (k, nb), fill)


def _proj_pieces(rows, dqkv_first):
    def piece(col):
        if dqkv_first:
            return pl.BlockSpec((rows, ATTN_W), lambda i, kk: (i, col))
        return pl.BlockSpec((rows, ATTN_W), lambda n, kk: (kk, col))
    return [piece(0), piece(1), piece(2), piece(0)]


def _proj_dx(name, dqkv, du, dgates, w, l, comm=None):
    t = du.shape[0]
    k, nb = w.shape[2], w.shape[3]
    tm = _tm(t)

    def body(dq_ref, dk_ref, dv_ref, du_ref, dg_ref, w_ref, o_ref):
        kk = pl.program_id(1)

        def mm(a):
            return lax.dot_general(a, w_ref[...], (NT, ((), ())), preferred_element_type=F32)

        @pl.when(kk == 0)
        def _():
            o_ref[...] = mm(jnp.concatenate([dq_ref[...], dk_ref[...]], axis=1))

        @pl.when(kk == 1)
        def _():
            o_ref[...] += mm(jnp.concatenate([dv_ref[...], du_ref[...]], axis=1))

        @pl.when(kk >= 2)
        def _():
            o_ref[...] += mm(dg_ref[...])

    out, moved = _pcall(
        body, name, (t // tm, N_CHIPS),
        _proj_pieces(tm, True)
        + [pl.BlockSpec((tm, nb), lambda i, kk: (i, jnp.maximum(kk - 2, 0))),
           pl.BlockSpec((None, None, k, nb), lambda i, kk: (l, kk, 0, 0))],
        pl.BlockSpec((tm, k), lambda i, kk: (i, 0)), _sds((t, k), F32),
        [], _cparams("arbitrary", "arbitrary"),
        (dqkv, dqkv, dqkv, du, dgates, w), comm)
    return out if comm is None else (out, moved)


def _proj_dw(name, h, dqkv, du, dgates, l, fill):
    t, k = h.shape
    nb = dgates.shape[1] // 2
    tt = _tm(t)
    nt = t // tt

    def body(*refs):
        h_ref, dq_ref, dk_ref, dv_ref, du_ref, dg_ref = refs[:6]
        o_ref, acc_ref = refs[-2], refs[-1]
        n, kk = pl.program_id(0), pl.program_id(1)

        def update(g):
            part = lax.dot_general(h_ref[...], g, (TN, ((), ())), preferred_element_type=F32)

            @pl.when(kk == 0)
            def _():
                acc_ref[...] = part

            @pl.when(kk > 0)
            def _():
                acc_ref[...] += part

        @pl.when(n == 0)
        def _():
            update(jnp.concatenate([dq_ref[...], dk_ref[...]], axis=1))

        @pl.when(n == 1)
        def _():
            update(jnp.concatenate([dv_ref[...], du_ref[...]], axis=1))

        @pl.when(n >= 2)
        def _():
            update(dg_ref[...])

        @pl.when(kk == nt - 1)
        def _():
            o_ref[...] = acc_ref[...].astype(BF16)

    in_specs = ([pl.BlockSpec((tt, k), lambda n, kk: (kk, 0))] + _proj_pieces(tt, False)
                + [pl.BlockSpec((tt, nb), lambda n, kk: (kk, jnp.maximum(n - 2, 0)))])
    args, aliases = [h, dqkv, dqkv, dqkv, du, dgates], {}
    if fill is not None:
        in_specs.append(pl.BlockSpec(memory_space=pl.ANY))
        args.append(fill)
        aliases = {6: 0}
    return pl.pallas_call(
        body, name=name, grid=(N_CHIPS, nt), in_specs=in_specs,
        out_specs=pl.BlockSpec((None, None, k, nb), lambda n, kk: (l, n, 0, 0)),
        out_shape=_sds((DEPTH, N_CHIPS, k, nb), BF16),
        scratch_shapes=[pltpu.VMEM((k, nb), F32)], input_output_aliases=aliases,
        compiler_params=_cparams("parallel", "arbitrary"))(*args)


def _narrow_nn(name, a, w, l):
    t, k = a.shape
    nb = w.shape[3]
    tm = _tm(t)

    def body(a_ref, w_ref, o_ref):
        av = a_ref[...]
        for j in range(N_CHIPS):
            o_ref[:, j * nb:(j + 1) * nb] = jnp.dot(
                av, w_ref[j], preferred_element_type=F32).astype(BF16)

    return pl.pallas_call(
        body, name=name, grid=(t // tm,),
        in_specs=[pl.BlockSpec((tm, k), lambda i: (i, 0)),
                  pl.BlockSpec((None, N_CHIPS, k, nb), lambda i: (l, 0, 0, 0))],
        out_specs=pl.BlockSpec((tm, N_CHIPS * nb), lambda i: (i, 0)),
        out_shape=_sds((t, N_CHIPS * nb), BF16), compiler_params=_cparams("parallel"))(a, w)


def _narrow_nt(name, a, w, l):
    t = a.shape[0]
    k, nb = w.shape[2], w.shape[3]
    tm = _tm(t)

    def body(a_ref, w_ref, o_ref):
        acc = lax.dot_general(a_ref[:, 0:nb], w_ref[0], (NT, ((), ())), preferred_element_type=F32)
        for j in range(1, N_CHIPS):
            acc = acc + lax.dot_general(a_ref[:, j * nb:(j + 1) * nb], w_ref[j], (NT, ((), ())),
                                        preferred_element_type=F32)
        o_ref[...] = acc.astype(BF16)

    return pl.pallas_call(
        body, name=name, grid=(t // tm,),
        in_specs=[pl.BlockSpec((tm, N_CHIPS * nb), lambda i: (i, 0)),
                  pl.BlockSpec((None, N_CHIPS, k, nb), lambda i: (l, 0, 0, 0))],
        out_specs=pl.BlockSpec((tm, k), lambda i: (i, 0)),
        out_shape=_sds((t, k), BF16), compiler_params=_cparams("parallel"))(a, w)


def _narrow_tn(name, a, g, l, fill):
    t, k = a.shape
    nb = g.shape[1] // N_CHIPS
    tt = _tm(t)
    nt = t // tt

    def body(*refs):
        a_ref, g_ref, o_ref, acc_ref = refs[0], refs[1], refs[-2], refs[-1]
        i = pl.program_id(0)
        part = lax.dot_general(a_ref[...], g_ref[...], (TN, ((), ())), preferred_element_type=F32)

        @pl.when(i == 0)
        def _():
            acc_ref[...] = part

        @pl.when(i > 0)
        def _():
            acc_ref[...] += part

        @pl.when(i == nt - 1)
        def _():
            for j in range(N_CHIPS):
                o_ref[j] = acc_ref[:, j * nb:(j + 1) * nb].astype(BF16)

    in_specs = [pl.BlockSpec((tt, k), lambda i: (i, 0)),
                pl.BlockSpec((tt, N_CHIPS * nb), lambda i: (i, 0))]
    args, aliases = [a, g], {}
    if fill is not None:
        in_specs.append(pl.BlockSpec(memory_space=pl.ANY))
        args.append(fill)
        aliases = {2: 0}
    return pl.pallas_call(
        body, name=name, grid=(nt,), in_specs=in_specs,
        out_specs=pl.BlockSpec((None, N_CHIPS, k, nb), lambda i: (l, 0, 0, 0)),
        out_shape=_sds((DEPTH, N_CHIPS, k, nb), BF16),
        scratch_shapes=[pltpu.VMEM((k, N_CHIPS * nb), F32)], input_output_aliases=aliases,
        compiler_params=_cparams("arbitrary"))(*args)


def _mm_nn(name, a, w, l, tk, out_dtype):
    t, k = a.shape
    n = w.shape[2]
    tm = _tm(t)
    nk = k // tk
    return _matmul(
        name, a, w,
        pl.BlockSpec((tm, tk), lambda i, j, kk: (i, kk)),
        pl.BlockSpec((None, tk, n), lambda i, j, kk: (l, kk, 0)),
        pl.BlockSpec((tm, n), lambda i, j, kk: (i, 0)),
        _sds((t, n), out_dtype), (t // tm, 1, nk), NN, nk, (tm, n))


def _mm_nt(name, a, w, l, tn, out_dtype, comm=None):
    t, n = a.shape
    k = w.shape[1]
    tm = _tm(t)
    return _matmul(
        name, a, w,
        pl.BlockSpec((tm, n), lambda i, j, kk: (i, 0)),
        pl.BlockSpec((None, tn, n), lambda i, j, kk: (l, j, 0)),
        pl.BlockSpec((tm, tn), lambda i, j, kk: (i, j)),
        _sds((t, k), out_dtype), (t // tm, k // tn, 1), NT, 1, None, comm=comm)


def _mm_tn(name, a, g, tko, l, fill):
    t, k = a.shape
    n = g.shape[1]
    tt = _tt(t)
    nt = t // tt
    return _matmul(
        name, a, g,
        pl.BlockSpec((tt, tko), lambda i, j, kk: (kk, i)),
        pl.BlockSpec((tt, n), lambda i, j, kk: (kk, 0)),
        pl.BlockSpec((None, tko, n), lambda i, j, kk: (l, i, 0)),
        _sds((DEPTH, k, n), BF16), (k // tko, 1, nt), TN, nt, (tko, n), fill)


def _row_spec(width, col=0):
    return pl.BlockSpec((TOK, width), lambda i: (i, col))


def _vec_spec(width):
    return pl.BlockSpec((1, width), lambda i: (0, 0))


def _rms(x):
    return lax.rsqrt(jnp.mean(x * x, axis=-1, keepdims=True) + EPS)


def _norm_fwd(name, x, g, comm=None):
    t = x.shape[0]

    def body(x_ref, g_ref, h_ref):
        xv = x_ref[...]
        h_ref[...] = (xv * _rms(xv) * g_ref[...]).astype(BF16)

    out, moved = _pcall(body, name, (t // TOK,), [_row_spec(D_MODEL), _vec_spec(D_MODEL)],
                        _row_spec(D_MODEL), _sds((t, D_MODEL), BF16), [], _cparams("arbitrary"),
                        (x, g), comm)
    return out if comm is None else (out, moved)


ROWS = 16
ROW_UNROLL = 8


def _rows(k):
    return pl.ds(pl.multiple_of(k * ROWS, ROWS), ROWS)


def _strips(step, init):
    def group(j, carry):
        for u in range(ROW_UNROLL):
            carry = step(j * ROW_UNROLL + u, carry)
        return carry

    return lax.fori_loop(0, TOK // (ROWS * ROW_UNROLL), group, init)


def _fold_rows(x):
    return x[0:8] + x[8:16]


def _accumulate(ref, part):
    total = jnp.sum(part, axis=0, keepdims=True)

    @pl.when(pl.program_id(0) == 0)
    def _():
        ref[...] = total

    @pl.when(pl.program_id(0) > 0)
    def _():
        ref[...] += total


def _norm_bwd_rows(d, mv, g):
    r = _rms(mv)
    n = mv * r
    dn = d * g
    return r * (dn - n * jnp.mean(dn * n, axis=-1, keepdims=True)), d * n


def _post_pre_fwd(name, xres, m, g_post, g_pre, comm=None):
    t = xres.shape[0]

    def body(x_ref, m_ref, gp_ref, gn_ref, x1_ref, h_ref):
        def strip(k, c):
            rows = _rows(k)
            mv = m_ref[rows, :]
            x1 = x_ref[rows, :] + mv * _rms(mv) * gp_ref[...]
            x1_ref[rows, :] = x1
            h_ref[rows, :] = (x1 * _rms(x1) * gn_ref[...]).astype(BF16)
            return c

        _strips(strip, 0)

    outs, moved = _pcall(
        body, name, (t // TOK,),
        [_row_spec(D_MODEL), _row_spec(D_MODEL), _vec_spec(D_MODEL), _vec_spec(D_MODEL)],
        [_row_spec(D_MODEL), _row_spec(D_MODEL)],
        [_sds((t, D_MODEL), F32), _sds((t, D_MODEL), BF16)], [], _cparams("arbitrary"),
        (xres, m, g_post, g_pre), comm)
    return outs if comm is None else (*outs, moved)


def _tail(name, xres, m, g_post, target):
    t = xres.shape[0]

    def body(x_ref, m_ref, g_ref, t_ref, dy_ref, dm_ref, dg_ref, l_ref):
        def strip(k, carry):
            rows = _rows(k)
            mv = m_ref[rows, :]
            e = x_ref[rows, :] + mv * _rms(mv) * g_ref[...] - t_ref[rows, :]
            dy = e * (1.0 / D_MODEL)
            dy_ref[rows, :] = dy
            dm, dgn = _norm_bwd_rows(dy, mv, g_ref[...])
            dm_ref[rows, :] = dm.astype(BF16)
            return carry[0] + _fold_rows(dgn), carry[1] + _fold_rows(e * e)

        zero = jnp.zeros((8, D_MODEL), F32)
        dg, sq = _strips(strip, (zero, zero))
        _accumulate(dg_ref, dg)
        _accumulate(l_ref, jnp.sum(sq, axis=1, keepdims=True))

    dy, dm, dg, sq = pl.pallas_call(
        body, name=name, grid=(t // TOK,),
        in_specs=[_row_spec(D_MODEL), _row_spec(D_MODEL), _vec_spec(D_MODEL), _row_spec(D_MODEL)],
        out_specs=[_row_spec(D_MODEL), _row_spec(D_MODEL), _vec_spec(D_MODEL),
                   pl.BlockSpec((1, 1), lambda i: (0, 0))],
        out_shape=[_sds((t, D_MODEL), F32), _sds((t, D_MODEL), BF16), _sds((1, D_MODEL), F32),
                   _sds((1, 1), F32)],
        compiler_params=_cparams("arbitrary"))(xres, m, g_post, target)
    return dy, dm, dg, sq[0, 0] * (0.5 / D_MODEL)


def _pre_post_bwd(name, dh, xin, dxo, g_pre, m, g_post, comm=None):
    t = dh.shape[0]

    def body(dh_ref, x_ref, d_ref, gq_ref, m_ref, gp_ref, dx_ref, dgq_ref, dm_ref, dgp_ref):
        def strip(k, carry):
            rows = _rows(k)
            dxin, dgq = _norm_bwd_rows(dh_ref[rows, :], x_ref[rows, :], gq_ref[...])
            dx = d_ref[rows, :] + dxin
            dx_ref[rows, :] = dx
            dm, dgp = _norm_bwd_rows(dx, m_ref[rows, :], gp_ref[...])
            dm_ref[rows, :] = dm.astype(BF16)
            return carry[0] + _fold_rows(dgq), carry[1] + _fold_rows(dgp)

        zero = jnp.zeros((8, D_MODEL), F32)
        dgq, dgp = _strips(strip, (zero, zero))
        _accumulate(dgq_ref, dgq)
        _accumulate(dgp_ref, dgp)

    outs, moved = _pcall(
        body, name, (t // TOK,),
        [_row_spec(D_MODEL), _row_spec(D_MODEL), _row_spec(D_MODEL), _vec_spec(D_MODEL),
         _row_spec(D_MODEL), _vec_spec(D_MODEL)],
        [_row_spec(D_MODEL), _vec_spec(D_MODEL), _row_spec(D_MODEL), _vec_spec(D_MODEL)],
        [_sds((t, D_MODEL), F32), _sds((1, D_MODEL), F32), _sds((t, D_MODEL), BF16),
         _sds((1, D_MODEL), F32)], [], _cparams("arbitrary"),
        (dh, xin, dxo, g_pre, m, g_post), comm)
    return outs if comm is None else (*outs, moved)


def _norm_pre_bwd(name, dh, xin, dxo, g, comm=None):
    t = dh.shape[0]

    def body(dh_ref, x_ref, d_ref, g_ref, dx_ref, dg_ref):
        xv = x_ref[...]
        dhv = dh_ref[...]
        r = _rms(xv)
        n = xv * r
        dn = dhv * g_ref[...]
        dx_ref[...] = d_ref[...] + r * (dn - n * jnp.mean(dn * n, axis=-1, keepdims=True))
        part = jnp.sum(dhv * n, axis=0, keepdims=True)

        @pl.when(pl.program_id(0) == 0)
        def _():
            dg_ref[...] = part

        @pl.when(pl.program_id(0) > 0)
        def _():
            dg_ref[...] += part

    out, moved = _pcall(
        body, name, (t // TOK,),
        [_row_spec(D_MODEL), _row_spec(D_MODEL), _row_spec(D_MODEL), _vec_spec(D_MODEL)],
        [_row_spec(D_MODEL), _vec_spec(D_MODEL)],
        [_sds((t, D_MODEL), F32), _sds((1, D_MODEL), F32)], [], _cparams("arbitrary"),
        (dh, xin, dxo, g), comm)
    return out if comm is None else (*out, moved)


def _gate_fwd(name, proj, b_gate, ya, yb):
    t = proj.shape[0]

    def body(ga_ref, gb_ref, b_ref, ya_ref, yb_ref, z_ref):
        def strip(k, c):
            rows = _rows(k)
            sa = jax.nn.sigmoid(ga_ref[rows, :].astype(F32) + b_ref[:, :D_MODEL])
            sb = jax.nn.sigmoid(gb_ref[rows, :].astype(F32) + b_ref[:, D_MODEL:])
            z_ref[rows, :] = (sa * ya_ref[rows, :].astype(F32)
                              + sb * yb_ref[rows, :].astype(F32)).astype(BF16)
            return c

        _strips(strip, 0)

    return pl.pallas_call(
        body, name=name, grid=(t // TOK,),
        in_specs=[_row_spec(D_MODEL, 2), _row_spec(D_MODEL, 3), _vec_spec(2 * D_MODEL),
                  _row_spec(D_MODEL), _row_spec(D_MODEL)],
        out_specs=_row_spec(D_MODEL), out_shape=_sds((t, D_MODEL), BF16),
        compiler_params=_cparams("parallel"))(proj, proj, b_gate, ya, yb)


def _gate_bwd(name, dz, proj, b_gate, ya, yb):
    t = proj.shape[0]

    def body(dz_ref, ga_ref, gb_ref, b_ref, ya_ref, yb_ref, dya_ref, dyb_ref, dg_ref, db_ref):
        def strip(k, carry):
            rows = _rows(k)
            dzv = dz_ref[rows, :].astype(F32)
            sa = jax.nn.sigmoid(ga_ref[rows, :].astype(F32) + b_ref[:, :D_MODEL])
            sb = jax.nn.sigmoid(gb_ref[rows, :].astype(F32) + b_ref[:, D_MODEL:])
            dya_ref[rows, :] = (dzv * sa).astype(BF16)
            dyb_ref[rows, :] = (dzv * sb).astype(BF16)
            dga = dzv * ya_ref[rows, :].astype(F32) * sa * (1.0 - sa)
            dgb = dzv * yb_ref[rows, :].astype(F32) * sb * (1.0 - sb)
            dg_ref[rows, :D_MODEL] = dga.astype(BF16)
            dg_ref[rows, D_MODEL:] = dgb.astype(BF16)
            return carry[0] + _fold_rows(dga), carry[1] + _fold_rows(dgb)

        zero = jnp.zeros((8, D_MODEL), F32)
        pa, pb = _strips(strip, (zero, zero))
        _accumulate(db_ref.at[:, :D_MODEL], pa)
        _accumulate(db_ref.at[:, D_MODEL:], pb)

    return pl.pallas_call(
        body, name=name, grid=(t // TOK,),
        in_specs=[_row_spec(D_MODEL), _row_spec(D_MODEL, 2), _row_spec(D_MODEL, 3),
                  _vec_spec(2 * D_MODEL), _row_spec(D_MODEL), _row_spec(D_MODEL)],
        out_specs=[_row_spec(D_MODEL), _row_spec(D_MODEL), _row_spec(2 * D_MODEL),
                   _vec_spec(2 * D_MODEL)],
        out_shape=[_sds((t, D_MODEL), BF16), _sds((t, D_MODEL), BF16),
                   _sds((t, 2 * D_MODEL), BF16), _sds((1, 2 * D_MODEL), F32)],
        compiler_params=_cparams("arbitrary"))(dz, proj, proj, b_gate, ya, yb)


def _head_masks():
    lane = lax.broadcasted_iota(jnp.int32, (1, 2 * HEAD_DIM), 1)
    return lane < HEAD_DIM


BAND_ROWS = 2 * ATT_BLK + CHUNK


def _fill_band(band, prev_ref, cur_ref):
    band[0:ATT_BLK, :] = prev_ref[...]
    band[ATT_BLK:2 * ATT_BLK, :] = cur_ref[...]
    band[2 * ATT_BLK:, :] = jnp.zeros((CHUNK, ATTN_W), BF16)


def _pair_rows(x2, low):
    zero = jnp.zeros_like(x2)
    return jnp.concatenate([jnp.where(low, x2, zero), jnp.where(low, zero, x2)], axis=0)


def _pair_diag(o2, low):
    return jnp.where(low, o2[0:CHUNK, :], o2[CHUNK:, :])


N_PAIRS = HEADS // 2
SM_STRIP = 32
N_STRIPS = BAND_PAD // SM_STRIP
NEG = -1e30


def _fold8(x, op):
    return op(op(x[0:8], x[8:16]), op(x[16:24], x[24:32]))


def _strip(k):
    return pl.ds(pl.multiple_of(k * SM_STRIP, SM_STRIP), SM_STRIP)


def _band_probs(k2, qcat, bias_t, first_key):
    kpos = lax.broadcasted_iota(jnp.int32, (BAND_PAD, 1), 0)
    st = lax.dot_general(k2, qcat, (NT, ((), ())), preferred_element_type=F32)
    st = jnp.where(kpos + first_key >= 0, st + bias_t, NEG)
    e = jnp.exp(st - jnp.max(st, axis=0, keepdims=True))
    return e * (1.0 / jnp.sum(e, axis=0, keepdims=True))


def _attn_specs(nblk):
    cur = lambda col: pl.BlockSpec((ATT_BLK, ATTN_W), lambda s: (jnp.minimum(s, nblk - 1), col))
    prev = lambda col: pl.BlockSpec(
        (ATT_BLK, ATTN_W), lambda s: (jnp.maximum(jnp.minimum(s, nblk - 1) - 1, 0), col))
    return cur, prev


def _attn_fwd(name, proj, bias, comm=None):
    t = proj.shape[0]
    nblk = t // ATT_BLK
    cur, prev = _attn_specs(nblk)

    def body(q_ref, kp_ref, kc_ref, vp_ref, vc_ref, b_ref, o_ref, p_ref, kband, vband):
        s = pl.program_id(0)
        _fill_band(kband, kp_ref, kc_ref)
        _fill_band(vband, vp_ref, vc_ref)
        low = _head_masks()

        def chunk(ci):
            r0 = pl.multiple_of(ci * CHUNK, CHUNK)
            for hp in range(N_PAIRS):
                cols = slice(hp * 128, (hp + 1) * 128)
                qcat = _pair_rows(q_ref[pl.ds(r0, CHUNK), cols] * ATTN_SCALE, low)
                p = _band_probs(kband[pl.ds(r0, BAND_PAD), cols], qcat, b_ref[hp],
                                (s * 8 - 8 + ci) * CHUNK).astype(BF16)
                p_ref[ci, hp] = p
                o2 = lax.dot_general(p, vband[pl.ds(r0, BAND_PAD), cols],
                                     (TN, ((), ())), preferred_element_type=F32)
                o_ref[pl.ds(r0, CHUNK), cols] = _pair_diag(o2, low).astype(BF16)

        def chunks(j, carry):
            for u in range(4):
                chunk(4 * j + u)
            return carry

        lax.fori_loop(0, 2, chunks, 0)

    outs, moved = _pcall(
        body, name, (nblk,),
        [cur(0), prev(1), cur(1), prev(2), cur(2),
         pl.BlockSpec((N_PAIRS, BAND_PAD, 128), lambda s: (0, 0, 0))],
        [pl.BlockSpec((ATT_BLK, ATTN_W), lambda s: (s, 0)),
         pl.BlockSpec((8, N_PAIRS, BAND_PAD, 128), lambda s: (s, 0, 0, 0))],
        [_sds((t, ATTN_W), BF16), _sds((t // CHUNK, N_PAIRS, BAND_PAD, 128), BF16)],
        [pltpu.VMEM((BAND_ROWS, ATTN_W), BF16), pltpu.VMEM((BAND_ROWS, ATTN_W), BF16)],
        _cparams("arbitrary"), (proj, proj, proj, proj, proj, bias), comm)
    return outs if comm is None else (*outs, moved)


def _attn_bwd(name, proj, datt, probs, comm=None):
    t = proj.shape[0]
    nblk = t // ATT_BLK
    cur, prev = _attn_specs(nblk)
    late = pl.BlockSpec((ATT_BLK, 3 * ATTN_W), lambda s: (jnp.maximum(s - 1, 0), 0))

    def body(q_ref, kp_ref, kc_ref, vp_ref, vc_ref, do_ref, p_ref,
             dqkv_ref, db_ref, kband, vband, dkacc, dvacc,
             dp_ref, dsb_ref, qc_ref, dc_ref, dq_ref, dq_held):
        s = pl.program_id(0)

        @pl.when(s == 0)
        def _():
            dkacc[...] = jnp.zeros_like(dkacc)
            dvacc[...] = jnp.zeros_like(dvacc)
            db_ref[...] = jnp.zeros_like(db_ref)
            dq_ref[...] = jnp.zeros_like(dq_ref)

        @pl.when(s < nblk)
        def _():
            _fill_band(kband, kp_ref, kc_ref)
            _fill_band(vband, vp_ref, vc_ref)
            low = _head_masks()

            def chunk(ci, carry):
                r0 = pl.multiple_of(ci * CHUNK, CHUNK)
                for hp in range(N_PAIRS):
                    cols = slice(hp * 128, (hp + 1) * 128)
                    qc_ref[hp] = _pair_rows(q_ref[pl.ds(r0, CHUNK), cols] * ATTN_SCALE, low)
                    dc_ref[hp] = _pair_rows(do_ref[pl.ds(r0, CHUNK), cols], low)
                    dp_ref[hp] = lax.dot_general(vband[pl.ds(r0, BAND_PAD), cols], dc_ref[hp],
                                                 (NT, ((), ())), preferred_element_type=F32)

                def sums(j, acc):
                    for u in range(2):
                        rows = _strip(2 * j + u)
                        acc = tuple(acc[hp] + _fold8(p_ref[ci, hp, rows, :].astype(F32)
                                                     * dp_ref[hp, rows, :], jnp.add)
                                    for hp in range(N_PAIRS))
                    return acc

                acc = lax.fori_loop(0, N_STRIPS // 2, sums, (jnp.zeros((8, 128), F32),) * N_PAIRS)
                delta = [jnp.sum(a, axis=0, keepdims=True) for a in acc]

                def grads(j, c):
                    for u in range(2):
                        rows = _strip(2 * j + u)
                        for hp in range(N_PAIRS):
                            ds = (p_ref[ci, hp, rows, :].astype(F32)
                                  * (dp_ref[hp, rows, :] - delta[hp]))
                            db_ref[hp, rows, :] += ds
                            dsb_ref[hp, rows, :] = ds.astype(BF16)
                    return c

                lax.fori_loop(0, N_STRIPS // 2, grads, 0)
                for hp in range(N_PAIRS):
                    cols = slice(hp * 128, (hp + 1) * 128)
                    dq2 = lax.dot_general(dsb_ref[hp], kband[pl.ds(r0, BAND_PAD), cols],
                                          (TN, ((), ())), preferred_element_type=F32)
                    dq_ref[pl.ds(r0, CHUNK), cols] = (_pair_diag(dq2, low) * ATTN_SCALE).astype(BF16)
                    dkacc[pl.ds(r0, BAND_PAD), cols] += jnp.dot(dsb_ref[hp], qc_ref[hp],
                                                               preferred_element_type=F32)
                    dvacc[pl.ds(r0, BAND_PAD), cols] += jnp.dot(p_ref[ci, hp], dc_ref[hp],
                                                               preferred_element_type=F32)
                return carry

            dq_held[...] = dq_ref[...]
            lax.fori_loop(0, 8, chunk, 0)

        @pl.when(s == nblk)
        def _():
            dq_held[...] = dq_ref[...]

        dqkv_ref[:, 0:ATTN_W] = dq_held[...]
        dqkv_ref[:, ATTN_W:2 * ATTN_W] = dkacc[0:ATT_BLK, :].astype(BF16)
        dqkv_ref[:, 2 * ATTN_W:] = dvacc[0:ATT_BLK, :].astype(BF16)
        dkacc[0:ATT_BLK, :] = dkacc[ATT_BLK:2 * ATT_BLK, :]
        dvacc[0:ATT_BLK, :] = dvacc[ATT_BLK:2 * ATT_BLK, :]
        dkacc[ATT_BLK:, :] = jnp.zeros((ATT_BLK + CHUNK, ATTN_W), F32)
        dvacc[ATT_BLK:, :] = jnp.zeros((ATT_BLK + CHUNK, ATTN_W), F32)

    outs, moved = _pcall(
        body, name, (nblk + 1,),
        [cur(0), prev(1), cur(1), prev(2), cur(2),
         pl.BlockSpec((ATT_BLK, ATTN_W), lambda s: (jnp.minimum(s, nblk - 1), 0)),
         pl.BlockSpec((8, N_PAIRS, BAND_PAD, 128), lambda s: (jnp.minimum(s, nblk - 1), 0, 0, 0))],
        [late, pl.BlockSpec((HEADS // 2, BAND_PAD, 128), lambda s: (0, 0, 0))],
        [_sds((t, 3 * ATTN_W), BF16), _sds((HEADS // 2, BAND_PAD, 128), F32)],
        [pltpu.VMEM((BAND_ROWS, ATTN_W), BF16), pltpu.VMEM((BAND_ROWS, ATTN_W), BF16),
         pltpu.VMEM((BAND_ROWS, ATTN_W), F32), pltpu.VMEM((BAND_ROWS, ATTN_W), F32),
         pltpu.VMEM((N_PAIRS, BAND_PAD, 128), F32), pltpu.VMEM((N_PAIRS, BAND_PAD, 128), BF16),
         pltpu.VMEM((N_PAIRS, 2 * CHUNK, 128), BF16), pltpu.VMEM((N_PAIRS, 2 * CHUNK, 128), BF16),
         pltpu.VMEM((ATT_BLK, ATTN_W), BF16), pltpu.VMEM((ATT_BLK, ATTN_W), BF16)],
        _cparams("arbitrary"), (proj, proj, proj, proj, proj, datt, probs), comm)
    return outs if comm is None else (*outs, moved)


def _diag_onehot(rel_rows):
    d0 = lax.broadcasted_iota(jnp.int32, (BIAS_LANES, BIAS_LANES), 0)
    d1 = lax.broadcasted_iota(jnp.int32, (BIAS_LANES, BIAS_LANES), 1)
    m, n = (d0, d1) if rel_rows else (d1, d0)
    hit = (m == jnp.minimum(BAND - 1 + MAX_REL - n, 2 * MAX_REL)) & (n < BAND + CHUNK - 1)
    return jnp.where(hit, 1.0, 0.0).astype(F32)


def _bias_table(name, rel_bias_l):
    rel_pad = jnp.pad(rel_bias_l, ((0, 0), (0, BIAS_LANES - N_REL)))

    def body(r_ref, o_ref):
        diag = jnp.dot(r_ref[...], _diag_onehot(True), preferred_element_type=F32,
                       precision=lax.Precision.HIGHEST)
        rowid = lax.broadcasted_iota(jnp.int32, (8, BIAS_LANES), 0)
        lane = lax.broadcasted_iota(jnp.int32, (8, BIAS_LANES), 1)
        for h in range(HEADS):
            d8 = jnp.broadcast_to(diag[h:h + 1, :], (8, BIAS_LANES))
            slab0 = pltpu.roll(d8, BIAS_LANES - CHUNK + 1, axis=1)
            for b in range(1, 8):
                slab0 = jnp.where(rowid == b, pltpu.roll(d8, BIAS_LANES - CHUNK + 1 + b, axis=1),
                                  slab0)
            for a in range(8):
                slab = slab0 if a == 0 else pltpu.roll(slab0, 8 * a, axis=1)
                o_ref[h * CHUNK + 8 * a:h * CHUNK + 8 * a + 8, :] = jnp.where(lane < BAND, slab, NEG)

    tab = pl.pallas_call(
        body, name=name,
        in_specs=[pl.BlockSpec(memory_space=pltpu.VMEM)],
        out_specs=pl.BlockSpec(memory_space=pltpu.VMEM),
        out_shape=_sds((HEADS * CHUNK, BIAS_LANES), F32),
    )(rel_pad)
    tab = tab.reshape(HEADS // 2, 2, CHUNK, BIAS_LANES)
    return jnp.transpose(tab, (0, 3, 1, 2)).reshape(HEADS // 2, BIAS_LANES, 2 * CHUNK)


def _bias_fold(name, dbias_t):
    rows = HEADS * CHUNK
    dbias = jnp.transpose(dbias_t.reshape(HEADS // 2, BIAS_LANES, 2, CHUNK), (0, 2, 3, 1))

    def body(d_ref, o_ref):
        rowid = lax.broadcasted_iota(jnp.int32, (8, BIAS_LANES), 0)
        diags = []
        for h in range(HEADS):
            acc = d_ref[h * CHUNK + 56:h * CHUNK + 64, :]
            for a in range(7):
                slab = d_ref[h * CHUNK + 8 * a:h * CHUNK + 8 * a + 8, :]
                acc = acc + pltpu.roll(slab, 56 - 8 * a, axis=1)
            tot = jnp.where(rowid == 7, acc, 0.0)
            for b in range(7):
                tot = tot + jnp.where(rowid == b, pltpu.roll(acc, 7 - b, axis=1), 0.0)
            diags.append(jnp.sum(tot, axis=0, keepdims=True))
        diag = jnp.concatenate(diags, axis=0)
        o_ref[...] = jnp.dot(diag, _diag_onehot(False), preferred_element_type=F32,
                             precision=lax.Precision.HIGHEST)

    return pl.pallas_call(
        body, name=name,
        in_specs=[pl.BlockSpec(memory_space=pltpu.VMEM)],
        out_specs=pl.BlockSpec(memory_space=pltpu.VMEM),
        out_shape=_sds((HEADS, BIAS_LANES), F32),
    )(dbias.reshape(rows, BIAS_LANES))


def _inv_counts(i):
    trow = lax.broadcasted_iota(jnp.int32, (TOK + HALO, 1), 0) + i * TOK
    return [1.0 / jnp.minimum(trow + 1, w).astype(F32) for w in POOL_WINDOWS]


def _pool_fwd(name, proj, wg, scale, comm=None):
    t = proj.shape[0]
    hb = TOK // HALO

    def body(u_ref, up_ref, wg_ref, sc_ref, pooled_ref, mixed_ref, b0, b1, b2, b3):
        i = pl.program_id(0)
        halo = up_ref[...].astype(F32)
        b0[0:HALO, :] = jnp.where(i == 0, jnp.zeros_like(halo), halo)
        b0[HALO:, :] = u_ref[...].astype(F32)
        n = TOK + HALO
        b1[8:n, :] = b0[8:n, :] + b0[7:n - 1, :]
        b2[16:n, 128:] = b1[16:n, 128:] + b1[14:n - 2, 128:]
        b3[24:n, 256:] = b2[24:n, 256:] + b2[20:n - 4, 256:]
        wins = [b1[HALO:n, 0:128], b2[HALO:n, 128:256], b3[HALO:n, 256:384],
                b3[HALO:n, 384:512] + b3[HALO - 8:n - 8, 384:512]]
        inv = _inv_counts(i)
        for g in range(4):
            cols = slice(g * POOL_GD, (g + 1) * POOL_GD)
            pooled = (wins[g] * inv[g][0:TOK] - b0[HALO:n, cols]).astype(BF16)
            pooled_ref[:, cols] = pooled
            pre = jnp.dot(pooled, wg_ref[g], preferred_element_type=F32)
            mixed_ref[:, cols] = (pre * sc_ref[:, cols]).astype(BF16)

    buf = pltpu.VMEM((TOK + HALO, POOL_W), F32)
    outs, moved = _pcall(
        body, name, (t // TOK,),
        [_row_spec(POOL_W, 3),
         pl.BlockSpec((HALO, POOL_W), lambda i: (jnp.maximum(i * hb - 1, 0), 3)),
         pl.BlockSpec((4, POOL_GD, POOL_GD), lambda i: (0, 0, 0)), _vec_spec(POOL_W)],
        [_row_spec(POOL_W), _row_spec(POOL_W)],
        [_sds((t, POOL_W), BF16), _sds((t, POOL_W), BF16)], [buf, buf, buf, buf],
        _cparams("arbitrary"), (proj, proj, wg, scale), comm)
    return outs if comm is None else (*outs, moved)


def _pool_bwd(name, dmixed, pooled, wg, scale, comm=None):
    t = dmixed.shape[0]
    nt = t // TOK
    hb = TOK // HALO

    def body(dm_ref, dmn_ref, p_ref, wg_ref, sc_ref, du_ref, dwg_ref, dsc_ref, c0, c1, c2, c3):
        i = pl.program_id(0)

        @pl.when(i == 0)
        def _():
            dwg_ref[...] = jnp.zeros_like(dwg_ref)
            dsc_ref[...] = jnp.zeros_like(dsc_ref)

        n = TOK + HALO
        inv = _inv_counts(i)
        dmv = dm_ref[...].astype(F32)
        dmn = dmn_ref[...].astype(F32)
        dmn = jnp.where(i == nt - 1, jnp.zeros_like(dmn), dmn)
        for g in range(4):
            cols = slice(g * POOL_GD, (g + 1) * POOL_GD)
            scg = sc_ref[:, cols]
            pg = p_ref[:, cols]
            dpre = (dmv[:, cols] * scg).astype(BF16)
            dpre_n = (dmn[:, cols] * scg).astype(BF16)
            pre = jnp.dot(pg, wg_ref[g], preferred_element_type=F32)
            dsc_ref[:, cols] += jnp.sum(dmv[:, cols] * pre, axis=0, keepdims=True)
            dwg_ref[g] += lax.dot_general(pg, dpre, (TN, ((), ())), preferred_element_type=F32)
            dpool = lax.dot_general(dpre, wg_ref[g], (NT, ((), ())), preferred_element_type=F32)
            dpool_n = lax.dot_general(dpre_n, wg_ref[g], (NT, ((), ())),
                                      preferred_element_type=F32)
            c0[0:TOK, cols] = dpool
            c0[TOK:n, cols] = dpool_n
            c1[0:TOK, cols] = dpool * inv[g][0:TOK]
            c1[TOK:n, cols] = dpool_n * inv[g][TOK:n]
        c2[0:n - 8, :] = c1[0:n - 8, :] + c1[1:n - 7, :]
        c3[0:n - 16, 128:] = c2[0:n - 16, 128:] + c2[2:n - 14, 128:]
        c1[0:n - 24, 256:] = c3[0:n - 24, 256:] + c3[4:n - 20, 256:]
        wins = [c2[0:TOK, 0:128], c3[0:TOK, 128:256], c1[0:TOK, 256:384],
                c1[0:TOK, 384:512] + c1[8:TOK + 8, 384:512]]
        for g in range(4):
            cols = slice(g * POOL_GD, (g + 1) * POOL_GD)
            du_ref[:, cols] = (wins[g] - c0[0:TOK, cols]).astype(BF16)

    buf = pltpu.VMEM((TOK + HALO, POOL_W), F32)
    outs, moved = _pcall(
        body, name, (nt,),
        [_row_spec(POOL_W),
         pl.BlockSpec((HALO, POOL_W), lambda i: (jnp.minimum((i + 1) * hb, nt * hb - 1), 0)),
         _row_spec(POOL_W), pl.BlockSpec((4, POOL_GD, POOL_GD), lambda i: (0, 0, 0)),
         _vec_spec(POOL_W)],
        [_row_spec(POOL_W), pl.BlockSpec((4, POOL_GD, POOL_GD), lambda i: (0, 0, 0)),
         _vec_spec(POOL_W)],
        [_sds((t, POOL_W), BF16), _sds((4, POOL_GD, POOL_GD), F32), _sds((1, POOL_W), F32)],
        [buf, buf, buf, buf], _cparams("arbitrary"), (dmixed, dmixed, pooled, wg, scale), comm)
    return outs if comm is None else (*outs, moved)


GELU_C = math.sqrt(2.0 / math.pi)


GELU_K = 0.044715


def _gelu_parts(x):
    x2 = x * x
    s = 0.5 + 0.5 * jnp.tanh(x * (GELU_C + (GELU_C * GELU_K) * x2))
    return x * s, s, x2


def _gelu(x):
    return _gelu_parts(x)[0]


def _gelu_and_grad(x):
    g, s, x2 = _gelu_parts(x)
    return g, s + g * (1.0 - s) * ((2 * GELU_C) + (6 * GELU_C * GELU_K) * x2)


def _taps(buf, r, rows):
    a = buf[pl.ds(r, rows + 8), :]
    return a[8:], pltpu.roll(a, 1, axis=0)[8:], pltpu.roll(a, 2, axis=0)[8:]


def _conv(taps, w_ref, b_ref):
    return b_ref[...] + w_ref[2:3, :] * taps[0] + w_ref[1:2, :] * taps[1] + w_ref[0:1, :] * taps[2]


def _stage(dst, prev_ref, cur_ref, next_ref, first, last):
    rows = cur_ref.shape[0]
    h = prev_ref[...].astype(F32)
    dst[0:8, :] = jnp.where(first, jnp.zeros_like(h), h)
    dst[8:8 + rows, :] = cur_ref[...].astype(F32)
    if next_ref is not None:
        h = next_ref[...].astype(F32)
        dst[8 + rows:, :] = jnp.where(last, jnp.zeros_like(h), h)


FWD_UNROLL = 4
FWD_STRIP = 32
BWD_STRIP = 16


def _ffn_gate_fwd(name, hu, conv_w, conv_b, comm=None):
    t = hu.shape[0]
    ncol = D_FF // FF_COL
    hb = FF_TOK // 8

    def tile(off):
        return pl.BlockSpec((FF_TOK, FF_COL), lambda i, j: (i, j + off))

    def halo(off):
        return pl.BlockSpec((8, FF_COL), lambda i, j: (jnp.maximum(i * hb - 1, 0), j + off))

    def wspec(off):
        return pl.BlockSpec((3, FF_COL), lambda i, j: (0, j + off))

    def bspec(off):
        return pl.BlockSpec((1, FF_COL), lambda i, j: (0, j + off))

    def body(v_ref, vp_ref, g_ref, gp_ref, wv_ref, wg_ref, bv_ref, bg_ref, a_ref, hc_ref, vb, gb):
        first = pl.program_id(0) == 0
        _stage(vb, vp_ref, v_ref, None, first, None)
        _stage(gb, gp_ref, g_ref, None, first, None)

        def strip(k, carry):
            for u in range(FWD_UNROLL):
                r = pl.multiple_of((FWD_UNROLL * k + u) * FWD_STRIP, FWD_STRIP)
                val = _conv(_taps(vb, r, FWD_STRIP), wv_ref, bv_ref)
                gate = _conv(_taps(gb, r, FWD_STRIP), wg_ref, bg_ref)
                a_ref[pl.ds(r, FWD_STRIP), :] = (_gelu(gate) * val).astype(BF16)
                hc_ref[0, pl.ds(r, FWD_STRIP), :] = val.astype(BF16)
                hc_ref[1, pl.ds(r, FWD_STRIP), :] = gate.astype(BF16)
            return carry

        lax.fori_loop(0, FF_TOK // (FWD_UNROLL * FWD_STRIP), strip, 0)

    buf = pltpu.VMEM((FF_TOK + 8, FF_COL), F32)
    outs, moved = _pcall(
        body, name, (t // FF_TOK, ncol),
        [tile(0), halo(0), tile(ncol), halo(ncol), wspec(0), wspec(ncol), bspec(0), bspec(ncol)],
        [pl.BlockSpec((FF_TOK, FF_COL), lambda i, j: (i, j)),
         pl.BlockSpec((2, FF_TOK, FF_COL), lambda i, j: (0, i, j))],
        [_sds((t, D_FF), BF16), _sds((2, t, D_FF), BF16)], [buf, buf],
        _cparams("arbitrary", "arbitrary"),
        (hu, hu, hu, hu, conv_w, conv_w, conv_b, conv_b), comm)
    return outs if comm is None else (*outs, moved)


def _ffn_gate_bwd(name, da, hu, hc, conv_w, comm=None):
    t = hu.shape[0]
    nt = t // FF_TOK
    ncol = D_FF // FF_COL
    hb = FF_TOK // 8

    def tile(off):
        return pl.BlockSpec((FF_TOK, FF_COL), lambda j, i: (i, j + off))

    def nxt_rows(i):
        return jnp.minimum((i + 1) * hb, nt * hb - 1)

    def wspec(off):
        return pl.BlockSpec((3, FF_COL), lambda j, i: (0, j + off))

    def body(da_ref, dan_ref, v_ref, g_ref, hc_ref, hcn_ref, wv_ref, wg_ref,
             dh_ref, dwv_ref, dwg_ref):
        i = pl.program_id(1)
        first, last = i == 0, i == nt - 1

        @pl.when(first)
        def _():
            dwv_ref[...] = jnp.zeros_like(dwv_ref)
            dwg_ref[...] = jnp.zeros_like(dwg_ref)

        def grads(dav, val, gate):
            g, dg = _gelu_and_grad(gate.astype(F32))
            dav = dav.astype(F32)
            return dav * g, dav * val.astype(F32) * dg

        def fold(x):
            return x[0:8] + x[8:16]

        def strip(j, carry):
            for u in range(2):
                carry = one_strip(2 * j + u, carry)
            return carry

        def one_strip(k, carry):
            r = pl.multiple_of(FF_TOK - BWD_STRIP - k * BWD_STRIP, BWD_STRIP)
            rows = pl.ds(r, BWD_STRIP)
            dval, dgate = grads(da_ref[rows, :], hc_ref[0, rows, :], hc_ref[1, rows, :])
            new = (dval[0:8], dgate[0:8])
            for half, (d, below, h_ref, w_ref, dw_ref) in enumerate((
                    (dval, carry[0], v_ref, wv_ref, dwv_ref),
                    (dgate, carry[1], g_ref, wg_ref, dwg_ref))):
                e = jnp.concatenate([d, below], axis=0)
                e1 = pltpu.roll(e, BWD_STRIP + 7, axis=0)[0:BWD_STRIP]
                e2 = pltpu.roll(e, BWD_STRIP + 6, axis=0)[0:BWD_STRIP]
                dh = w_ref[2:3, :] * d + w_ref[1:2, :] * e1 + w_ref[0:1, :] * e2
                dh_ref[half, rows, :] = dh.astype(BF16)
                huv = h_ref[rows, :].astype(F32)
                dw_ref[0:8, :] += fold(e2 * huv)
                dw_ref[8:16, :] += fold(e1 * huv)
                dw_ref[16:24, :] += fold(d * huv)
                dw_ref[24:32, :] += fold(d)
            return new

        dan = dan_ref[...]
        dan = jnp.where(last, jnp.zeros_like(dan), dan)
        lax.fori_loop(0, FF_TOK // (2 * BWD_STRIP), strip, grads(dan, hcn_ref[0], hcn_ref[1]))

        @pl.when(last)
        def _():
            for dw_ref in (dwv_ref, dwg_ref):
                for q in range(4):
                    dw_ref[8 * q:8 * q + 1, :] = jnp.sum(dw_ref[8 * q:8 * q + 8, :], axis=0,
                                                         keepdims=True)

    acc = pl.BlockSpec((32, FF_COL), lambda j, i: (0, j))
    (dhu, dwv, dwg), moved = _pcall(
        body, name, (ncol, nt),
        [tile(0), pl.BlockSpec((8, FF_COL), lambda j, i: (nxt_rows(i), j)),
         tile(0), tile(ncol),
         pl.BlockSpec((2, FF_TOK, FF_COL), lambda j, i: (0, i, j)),
         pl.BlockSpec((2, 8, FF_COL), lambda j, i: (0, nxt_rows(i), j)),
         wspec(0), wspec(ncol)],
        [pl.BlockSpec((2, FF_TOK, FF_COL), lambda j, i: (0, i, j)), acc, acc],
        [_sds((2, t, D_FF), BF16), _sds((32, D_FF), F32), _sds((32, D_FF), F32)],
        [], _cparams("arbitrary", "arbitrary"),
        (da, da, hu, hu, hc, hc, conv_w, conv_w), comm)
    dconv = jnp.concatenate([dwv, dwg], axis=1).reshape(4, 8, 2 * D_FF)[:, 0]
    return (dhu, dconv) if comm is None else (dhu, dconv, moved)


def _mesh_pos():
    x, y, c = lax.axis_index("x"), lax.axis_index("y"), lax.axis_index("c")
    return x, y, c, [(1 - x, y), (x, 1 - y), (1 - x, 1 - y)]


def _remote(src, dst, send_sems, recv_sems, i, dev):
    return pltpu.make_async_remote_copy(src_ref=src, dst_ref=dst, send_sem=send_sems.at[i],
                                        recv_sem=recv_sems.at[i], device_id=dev,
                                        device_id_type=MESH)


def _mine(c, rows):
    return pl.ds(pl.multiple_of(c * (rows // 2), 16), rows // 2)


def _gather_send(shards, conv_shard, gathered, l):
    nbig = len(shards)
    with_conv = conv_shard is not None
    if gathered is None:
        ins = list(shards) + ([conv_shard] if with_conv else [])
        outs = [_sds((DEPTH, N_CHIPS) + s.shape[1:], s.dtype) for s in ins]
        alias = {}
    else:
        ins = list(shards) + list(gathered)
        outs = [_sds(g.shape, g.dtype) for g in gathered]
        alias = {nbig + k: k for k in range(nbig)}

    def copies(cin, cout, ssem, rsem):
        x, y, c, chips = _mesh_pos()
        me = 2 * x + y
        out = []
        for k in range(nbig):
            rows = shards[k].shape[1]
            for j, (cx, cy) in enumerate(chips):
                out.append(_remote(cin[k].at[l, _mine(c, rows)], cout[k].at[l, me, _mine(c, rows)],
                                   ssem, rsem, 4 * k + j, (cx, cy, c)))
            out.append(_remote(cin[k].at[l], cout[k].at[l, me], ssem, rsem, 4 * k + 3,
                               (x, y, 1 - c)))
        if with_conv:
            base = 4 * nbig
            for j, (cx, cy) in enumerate(chips):
                out.append(_remote(cin[nbig].at[c], cout[nbig].at[c, me], ssem, rsem, base + j,
                                   (cx, cy, c)))
            for ll in range(DEPTH):
                out.append(_remote(cin[nbig].at[ll], cout[nbig].at[ll, me], ssem, rsem,
                                   base + 3 + ll, (x, y, 1 - c)))
        return out

    return _Comm(ins, outs, copies, 4 * nbig + 5, alias)


def _gather_forward(gathered, nbig, rows, l):
    with_conv = len(gathered) > nbig
    alias = {k: k for k in range(len(gathered))}

    def copies(cin, cout, ssem, rsem):
        x, y, c, chips = _mesh_pos()
        out = []
        for k in range(nbig):
            for j, (cx, cy) in enumerate(chips):
                blk = cout[k].at[l, 2 * cx + cy, _mine(c, rows[k])]
                out.append(_remote(blk, blk, ssem, rsem, 3 * k + j, (x, y, 1 - c)))
        if with_conv:
            for j, (cx, cy) in enumerate(chips):
                blk = cout[nbig].at[c, 2 * cx + cy]
                out.append(_remote(blk, blk, ssem, rsem, 3 * nbig + j, (x, y, 1 - c)))
        return out

    return _Comm(gathered, [_sds(g.shape, g.dtype) for g in gathered], copies, 3 * nbig + 3, alias)


def _reduce_swap(grads, l):
    def copies(cin, cout, ssem, rsem):
        x, y, c, _ = _mesh_pos()
        return [_remote(cin[k].at[l, :, _mine(1 - c, g.shape[2])], cout[k], ssem, rsem, k,
                        (x, y, 1 - c)) for k, g in enumerate(grads)]

    outs = [_sds((N_CHIPS, g.shape[2] // 2, g.shape[3]), g.dtype) for g in grads]
    return _Comm(grads, outs, copies, len(grads))


def _reduce_scatter(sums):
    def copies(cin, cout, ssem, rsem):
        x, y, c, chips = _mesh_pos()
        return [_remote(cin[k].at[2 * cx + cy], cout[k].at[j], ssem, rsem, 3 * k + j, (cx, cy, c))
                for k in range(len(sums)) for j, (cx, cy) in enumerate(chips)]

    outs = [_sds((3,) + s.shape[1:], s.dtype) for s in sums]
    return _Comm(sums, outs, copies, 3 * len(sums))


def _reduce_share(reds, l):
    def copies(cin, cout, ssem, rsem):
        x, y, c, _ = _mesh_pos()
        out = []
        for k, r in enumerate(reds):
            half = cout[k].at[l, _mine(c, r.shape[1])]
            out.append(_remote(half, half, ssem, rsem, k, (x, y, 1 - c)))
        return out

    return _Comm(reds, [_sds(r.shape, r.dtype) for r in reds], copies, len(reds),
                 {k: k for k in range(len(reds))})


def _allreduce_small(per_layer):
    kinds = len(per_layer[0])
    shapes = [a.shape[1:] if a.shape[0] == 1 else a.shape for a in per_layer[0]]

    def body(*refs):
        ins = refs[:DEPTH * kinds]
        outs = refs[DEPTH * kinds:(DEPTH + 1) * kinds]
        gbufs = refs[(DEPTH + 1) * kinds:(DEPTH + 2) * kinds]
        send_sems, recv_sems = refs[-2], refs[-1]
        x, y, c, chips = _mesh_pos()
        sibling = (x, y, 1 - c)

        def copy(k, i, block, to):
            px, py, pc = block
            slot = gbufs[k].at[4 * px + 2 * py + pc]
            return _remote(slot, slot, send_sems, recv_sems, 7 * k + i, to)

        me = (x, y, c)
        first, passed = [], []
        for k in range(kinds):
            for l in range(DEPTH):
                a = ins[l * kinds + k]
                if per_layer[l][k].shape[0] == 1:
                    gbufs[k][4 * x + 2 * y + c, l:l + 1] = a[...]
                else:
                    gbufs[k][4 * x + 2 * y + c, l] = a[...]
            first.append(copy(k, 0, me, sibling))
            first += [copy(k, 1 + j, me, (*chip, c)) for j, chip in enumerate(chips)]
            passed += [copy(k, 4 + j, (*chip, c), sibling) for j, chip in enumerate(chips)]
        for cp in first:
            cp.start()
        for k in range(kinds):
            for j, chip in enumerate(chips):
                copy(k, 1 + j, (*chip, c), me).wait_recv()
                passed[3 * k + j].start()
        for k in range(kinds):
            copy(k, 0, sibling, me).wait_recv()
            for j, chip in enumerate(chips):
                copy(k, 4 + j, (*chip, 1 - c), me).wait_recv()
        for cp in first + passed:
            cp.wait_send()
        for k in range(kinds):
            acc = gbufs[k][0]
            for d in range(1, 8):
                acc = acc + gbufs[k][d]
            outs[k][...] = acc

    vmem = pl.BlockSpec(memory_space=pltpu.VMEM)
    return pl.pallas_call(
        body, name="allreduce_small",
        in_specs=[vmem] * (DEPTH * kinds), out_specs=[vmem] * kinds,
        out_shape=[_sds((DEPTH,) + s, F32) for s in shapes],
        scratch_shapes=[pltpu.VMEM((8, DEPTH) + s, F32) for s in shapes]
        + [pltpu.SemaphoreType.DMA((7 * kinds,)), pltpu.SemaphoreType.DMA((7 * kinds,))],
        compiler_params=pltpu.CompilerParams(vmem_limit_bytes=VMEM_LIMIT_V7X),
    )(*per_layer[0], *per_layer[1])


def _adamw_small(ws, gs, ms, vs):
    n = len(ws)
    c1 = 1.0 - ADAM_B1 ** ADAM_STEP
    c2 = 1.0 - ADAM_B2 ** ADAM_STEP

    def body(*refs):
        for i in range(n):
            w_ref, g_ref, m_ref, v_ref = (refs[j * n + i] for j in range(4))
            d_ref, nm_ref, nv_ref = (refs[(4 + j) * n + i] for j in range(3))
            gv = g_ref[...]
            nm = ADAM_B1 * m_ref[...] + (1.0 - ADAM_B1) * gv
            nv = ADAM_B2 * v_ref[...] + (1.0 - ADAM_B2) * (gv * gv)
            nm_ref[...] = nm
            nv_ref[...] = nv
            d_ref[...] = -ADAM_LR * ((nm / c1) / (jnp.sqrt(nv / c2) + ADAM_EPS)
                                     + ADAM_WD * w_ref[...])

    vmem = pl.BlockSpec(memory_space=pltpu.VMEM)
    outs = pl.pallas_call(
        body, name="adamw_small", in_specs=[vmem] * (4 * n), out_specs=[vmem] * (3 * n),
        out_shape=[_sds(w.shape, F32) for w in ws] * 3,
        compiler_params=pltpu.CompilerParams(vmem_limit_bytes=VMEM_LIMIT_V7X),
    )(*ws, *gs, *ms, *vs)
    return outs[:n], outs[n:2 * n], outs[2 * n:]


def _core_index():
    return jnp.reshape(lax.axis_index("c"), (1,)).astype(jnp.int32)


def _chip_index():
    return jnp.reshape(2 * lax.axis_index("x") + lax.axis_index("y"), (1,)).astype(jnp.int32)


def _chip_sums(name, stacked, sibs, l):
    n = len(stacked)
    dims = [(s.shape[2] // 2, s.shape[3]) for s in stacked]

    def body(c_ref, *refs):
        for k in range(n):
            a_ref, b_ref, o_ref = refs[k], refs[n + k], refs[2 * n + k]
            o_ref[...] = (a_ref[...].astype(F32) + b_ref[...].astype(F32)).astype(BF16)

    return pl.pallas_call(
        body, name=name,
        grid_spec=pltpu.PrefetchScalarGridSpec(
            num_scalar_prefetch=1, grid=(N_CHIPS,),
            in_specs=[pl.BlockSpec((None, None, hr, cd), lambda j, cr: (l, j, cr[0], 0))
                      for hr, cd in dims]
            + [pl.BlockSpec((None, hr, cd), lambda j, cr: (j, 0, 0)) for hr, cd in dims],
            out_specs=[pl.BlockSpec((None, hr, cd), lambda j, cr: (j, 0, 0)) for hr, cd in dims]),
        out_shape=[_sds((N_CHIPS, hr, cd), BF16) for hr, cd in dims],
        compiler_params=_cparams("parallel"))(_core_index(), *stacked, *sibs)


def _final_sums(name, sums, recvs, l, fills):
    n = len(sums)
    dims = [(s.shape[1] // 2, s.shape[2]) for s in sums]
    filled = fills[0] is not None

    def body(m_ref, *refs):
        outs = refs[-n:]
        for k in range(n):
            acc = refs[k][...].astype(F32)
            for j in range(3):
                acc = acc + refs[n + k][j].astype(F32)
            outs[k][...] = acc

    in_specs = ([pl.BlockSpec((None, tr, cd), lambda i, mr: (mr[0], i, 0)) for tr, cd in dims]
                + [pl.BlockSpec((3, tr, cd), lambda i, mr: (0, i, 0)) for tr, cd in dims])
    args = [jnp.concatenate([_chip_index(), _core_index()]), *sums, *recvs]
    aliases = {}
    if filled:
        in_specs += [pl.BlockSpec(memory_space=pl.ANY)] * n
        args += list(fills)
        aliases = {1 + 2 * n + k: k for k in range(n)}
    return pl.pallas_call(
        body, name=name,
        grid_spec=pltpu.PrefetchScalarGridSpec(
            num_scalar_prefetch=1, grid=(2,), in_specs=in_specs,
            out_specs=[pl.BlockSpec((None, tr, cd), lambda i, mr: (l, 2 * mr[1] + i, 0))
                       for tr, cd in dims]),
        out_shape=[_sds((DEPTH, 4 * tr, cd), F32) for tr, cd in dims],
        input_output_aliases=aliases,
        compiler_params=_cparams("parallel"))(*args)


def _adamw(name, w, g, m, v):
    nl, r, cdim = w.shape
    tr = r // 4 if r % 32 == 0 else r
    c1 = 1.0 - ADAM_B1 ** ADAM_STEP
    c2 = 1.0 - ADAM_B2 ** ADAM_STEP

    def body(w_ref, g_ref, m_ref, v_ref, d_ref, nm_ref, nv_ref, go_ref):
        gv = g_ref[...]
        go_ref[...] = gv
        nm = ADAM_B1 * m_ref[...] + (1.0 - ADAM_B1) * gv
        nv = ADAM_B2 * v_ref[...] + (1.0 - ADAM_B2) * (gv * gv)
        nm_ref[...] = nm
        nv_ref[...] = nv
        d_ref[...] = -ADAM_LR * ((nm / c1) / (jnp.sqrt(nv / c2) + ADAM_EPS) + ADAM_WD * w_ref[...])

    spec = pl.BlockSpec((None, tr, cdim), lambda l, i: (l, i, 0))
    out = _sds(w.shape, F32)
    return pl.pallas_call(
        body, name=name, grid=(nl, r // tr), in_specs=[spec] * 4, out_specs=[spec] * 4,
        out_shape=[out] * 4, compiler_params=_cparams("parallel", "parallel"))(w, g, m, v)


def kernel(x, norm_mix_pre, w_in, b_gate, rel_bias, w_attn_out, w_pool_group, pool_scale, w_pool_out, w_o, norm_mix_post, norm_ffn_pre, w_up, conv_w, conv_b, w_down, norm_ffn_post, loss_target, m_norm_mix_pre, m_w_in, m_b_gate, m_rel_bias, m_w_attn_out, m_w_pool_group, m_pool_scale, m_w_pool_out, m_w_o, m_norm_mix_post, m_norm_ffn_pre, m_w_up, m_conv_w, m_conv_b, m_w_down, m_norm_ffn_post, v_norm_mix_pre, v_w_in, v_b_gate, v_rel_bias, v_w_attn_out, v_w_pool_group, v_pool_scale, v_w_pool_out, v_w_o, v_norm_mix_post, v_norm_ffn_pre, v_w_up, v_conv_w, v_conv_b, v_w_down, v_norm_ffn_post):
    t = x.shape[1]
    xs = x.reshape(t, D_MODEL)
    target = loss_target.reshape(t, D_MODEL)

    names = ["w_in", "w_attn_out", "w_pool_out", "w_o", "w_up", "w_down"]
    shards = [w.astype(BF16) for w in (w_in, w_attn_out, w_pool_out, w_o, w_up, w_down)]
    rows = [s.shape[1] for s in shards]
    nbig = len(shards)
    h, g = _norm_fwd("l0_norm_mix_pre", x.reshape(t, D_MODEL), norm_mix_pre[0:1],
                     _gather_send(shards[:1], conv_w, None, 0))
    g = _comm_call("gather0_forward", _gather_forward(g, 1, rows[:1], 0))
    cw_full = jnp.transpose(g[1], (0, 2, 1, 3)).reshape(DEPTH, 3, 2 * D_FF)
    g = g[:1]
    wg_bf = w_pool_group.astype(BF16)

    def views(gathered):
        win_g, wao_g, wpo_g, wo_g, wup_g, wdn_g = gathered
        return (win_g, wao_g, wpo_g, wo_g.reshape(DEPTH, D_MODEL, D_MODEL), wup_g,
                wdn_g.reshape(DEPTH, D_FF, D_MODEL))

    saved = []
    xcur = xs
    for l in range(DEPTH):
        tag = f"l{l}_"
        bias = _bias_table(tag + "bias_table", rel_bias[l])
        proj = _mm_nn_blocked(tag + "proj", h, g[0], l, BF16)
        if l == 0:
            att, probs, rest = _attn_fwd(tag + "attn_fwd", proj, bias,
                                         _gather_send(shards[1:], None, None, 0))
            pooled, mixed, rest = _pool_fwd(tag + "pool_fwd", proj, wg_bf[l], pool_scale[l:l + 1],
                                            _gather_forward(rest, nbig - 1, rows[1:], 0))
            g = g + rest
        else:
            att, probs = _attn_fwd(tag + "attn_fwd", proj, bias)
            pooled, mixed = _pool_fwd(tag + "pool_fwd", proj, wg_bf[l], pool_scale[l:l + 1])
        win_g, wao_g, wpo_g, wo_full, wup_g, wdn_full = views(g)
        ya = _narrow_nn(tag + "attn_out", att, wao_g, l)
        yb = _narrow_nn(tag + "pool_out", mixed, wpo_g, l)
        z = _gate_fwd(tag + "gate_fwd", proj, b_gate[l:l + 1], ya, yb)
        mix = _mm_nn(tag + "mix", z, wo_full, l, D_MODEL, F32)
        x1, h2 = _post_pre_fwd(tag + "norm_mix_post", xcur, mix, norm_mix_post[l:l + 1],
                               norm_ffn_pre[l:l + 1])
        if l == 0:
            hu, mixing = _mm_nn_blocked(tag + "ffn_up", h2, wup_g, l, BF16,
                                        _gather_send(shards[:4], None, g[:4], 1))
            a, hc, ffn_g = _ffn_gate_fwd(tag + "ffn_gate_fwd", hu, cw_full[l], conv_b[l:l + 1],
                                         _gather_send(shards[4:], None, g[4:], 1))
            g = mixing + ffn_g
            wdn_full = views(g)[5]
        else:
            hu = _mm_nn_blocked(tag + "ffn_up", h2, wup_g, l, BF16)
            a, hc = _ffn_gate_fwd(tag + "ffn_gate_fwd", hu, cw_full[l], conv_b[l:l + 1])
        f = _mm_nn(tag + "ffn_down", a, wdn_full, l, D_FF, F32)
        saved.append(dict(x=xcur, h=h, proj=proj, att=att, pooled=pooled, mixed=mixed, ya=ya,
                          yb=yb, z=z, mix=mix, x1=x1, h2=h2, hu=hu, hc=hc, a=a, f=f, probs=probs))
        if l == 0:
            xcur, h, g = _post_pre_fwd(tag + "norm_ffn_post", x1, f, norm_ffn_post[l:l + 1],
                                       norm_mix_pre[l + 1:l + 2], _gather_forward(g, nbig, rows, 1))
        elif l < DEPTH - 1:
            xcur, h = _post_pre_fwd(tag + "norm_ffn_post", x1, f, norm_ffn_post[l:l + 1],
                                    norm_mix_pre[l + 1:l + 2])
    win_g, wao_g, wpo_g, wo_full, wup_g, wdn_full = views(g)

    dy, df, d_nfpost, loss_local = _tail("tail", saved[-1]["x1"], saved[-1]["f"],
                                         norm_ffn_post[DEPTH - 1:DEPTH], target)
    loss = lax.psum(loss_local, ("x", "y", "c"))

    dx = dy
    dws = dict.fromkeys(names)
    reds = [None] * nbig
    small_grads = [None] * DEPTH
    ffn = [4, 5]
    outs3 = [1, 2, 3]

    def blocks(ks):
        return [dws[names[k]].reshape(DEPTH, N_CHIPS, rows[k], -1) for k in ks]

    def chip_sums(ks, sib, l):
        return _chip_sums(f"chip_sums{l}_" + names[ks[0]], blocks(ks), sib, l)

    def final_sums(ks, sums, recv, l):
        outs = _final_sums(f"final_sums{l}_" + names[ks[0]], sums, recv, l, [reds[k] for k in ks])
        for k, r in zip(ks, outs):
            reds[k] = r

    for l in reversed(range(DEPTH)):
        tag = f"l{l}_"
        sv = saved[l]
        every = list(range(nbig))
        if l == 0:
            da, sib = _mm_nt(tag + "ffn_down_dx", df, wdn_full, l, D_FF // 2, BF16,
                             _reduce_swap(blocks(every), 1))
            sums = chip_sums(every, sib, 1)
        else:
            da = _mm_nt(tag + "ffn_down_dx", df, wdn_full, l, D_FF // 2, BF16)
        dws["w_down"] = _mm_tn(tag + "ffn_down_dw", sv["a"], df, D_FF // 2, l, dws["w_down"])
        if l == 0:
            dhu, dconv, recv = _ffn_gate_bwd(tag + "ffn_gate_bwd", da, sv["hu"], sv["hc"],
                                             cw_full[l], _reduce_scatter(sums))
            final_sums(every, sums, recv, 1)
            dh2, reds = _mm_nt_blocked(tag + "ffn_up_dx", dhu, wup_g, l, F32,
                                       _reduce_share(reds, 1))
        else:
            dhu, dconv = _ffn_gate_bwd(tag + "ffn_gate_bwd", da, sv["hu"], sv["hc"], cw_full[l])
            dh2 = _mm_nt_blocked(tag + "ffn_up_dx", dhu, wup_g, l, F32)
        dws["w_up"] = _mm_tn_blocked(tag + "ffn_up_dw", sv["h2"], dhu, l, dws["w_up"])
        if l == 0:
            dx1, d_nfpre, dmix, d_nmpost, sib = _pre_post_bwd(
                tag + "norm_ffn_pre_bwd", dh2, sv["x1"], dx, norm_ffn_pre[l:l + 1], sv["mix"],
                norm_mix_post[l:l + 1], _reduce_swap(blocks(ffn), 0))
            sums = chip_sums(ffn, sib, 0)
        else:
            dx1, d_nfpre, dmix, d_nmpost = _pre_post_bwd(
                tag + "norm_ffn_pre_bwd", dh2, sv["x1"], dx, norm_ffn_pre[l:l + 1], sv["mix"],
                norm_mix_post[l:l + 1])
        dz = _mm_nt(tag + "mix_dx", dmix, wo_full, l, D_MODEL, BF16)
        dws["w_o"] = _mm_tn(tag + "mix_dw", sv["z"], dmix, D_MODEL, l, dws["w_o"])
        dya, dyb, dgates, d_bgate = _gate_bwd(tag + "gate_bwd", dz, sv["proj"], b_gate[l:l + 1],
                                              sv["ya"], sv["yb"])
        datt = _narrow_nt(tag + "attn_out_dx", dya, wao_g, l)
        dws["w_attn_out"] = _narrow_tn(tag + "attn_out_dw", sv["att"], dya, l, dws["w_attn_out"])
        dmixed = _narrow_nt(tag + "pool_out_dx", dyb, wpo_g, l)
        dws["w_pool_out"] = _narrow_tn(tag + "pool_out_dw", sv["mixed"], dyb, l, dws["w_pool_out"])
        if l == 0:
            du, d_wg, d_pscale, sib = _pool_bwd(tag + "pool_bwd", dmixed, sv["pooled"], wg_bf[l],
                                                pool_scale[l:l + 1], _reduce_swap(blocks(outs3), 0))
            sums3 = chip_sums(outs3, sib, 0)
            dqkv, dbias, recv = _attn_bwd(
                tag + "attn_bwd", sv["proj"], datt, sv["probs"],
                _both(_reduce_scatter(sums), _reduce_scatter(sums3)))
            final_sums(ffn, sums, recv[:len(ffn)], 0)
            final_sums(outs3, sums3, recv[len(ffn):], 0)
        else:
            du, d_wg, d_pscale = _pool_bwd(tag + "pool_bwd", dmixed, sv["pooled"], wg_bf[l],
                                           pool_scale[l:l + 1])
            dqkv, dbias = _attn_bwd(tag + "attn_bwd", sv["proj"], datt, sv["probs"])
        d_rel = _bias_fold(tag + "bias_fold", dbias)
        if l == 0:
            dh, shared = _proj_dx(tag + "proj_dx", dqkv, du, dgates, win_g, l,
                                  _reduce_share([reds[k] for k in ffn + outs3], 0))
            for k, r in zip(ffn + outs3, shared):
                reds[k] = r
        else:
            dh = _proj_dx(tag + "proj_dx", dqkv, du, dgates, win_g, l)
        dws["w_in"] = _proj_dw(tag + "proj_dw", sv["h"], dqkv, du, dgates, l, dws["w_in"])
        small_grads[l] = [None, d_nmpost, d_nfpre, d_nfpost, d_bgate, d_rel, d_wg, d_pscale, dconv]
        if l > 0:
            dx, small_grads[l][0], df, d_nfpost = _pre_post_bwd(
                tag + "norm_mix_pre_bwd", dh, sv["x"], dx1, norm_mix_pre[l:l + 1],
                saved[l - 1]["f"], norm_ffn_post[l - 1:l])
        else:
            dx, small_grads[l][0] = _norm_pre_bwd(tag + "norm_mix_pre_bwd", dh, sv["x"], dx1,
                                                  norm_mix_pre[l:l + 1])

    grad_x = dx.reshape(x.shape)

    delta, new_m, new_v = {}, {}, {}
    sib = _comm_call("reduce_swap", _reduce_swap(blocks([0]), 0))
    sums = chip_sums([0], sib, 0)
    recv = _comm_call("reduce_scatter", _reduce_scatter(sums))
    final_sums([0], sums, recv, 0)
    g_big = _comm_call("reduce_share", _reduce_share([reds[0]], 0)) + reds[1:]

    (g_nmpre, g_nmpost, g_nfpre, g_nfpost, g_bgate, g_rel, g_wg, g_pscale,
     g_conv) = _allreduce_small(small_grads)
    g_rel = g_rel[:, :, :N_REL]
    g_cb = g_conv[:, 3]
    ncw = conv_w.shape[2]
    chip = 2 * lax.axis_index("x") + lax.axis_index("y")
    g_cw = lax.dynamic_slice_in_dim(g_conv[:, 0:3], chip * ncw, ncw, axis=2)

    grads = dict(norm_mix_pre=g_nmpre, w_in=g_big[0], b_gate=g_bgate, rel_bias=g_rel,
                 w_attn_out=g_big[1], w_pool_group=g_wg, pool_scale=g_pscale, w_pool_out=g_big[2],
                 w_o=g_big[3], norm_mix_post=g_nmpost, norm_ffn_pre=g_nfpre, w_up=g_big[4],
                 conv_w=g_cw, conv_b=g_cb, w_down=g_big[5], norm_ffn_post=g_nfpost)
    weights = dict(norm_mix_pre=norm_mix_pre, w_in=w_in, b_gate=b_gate, rel_bias=rel_bias,
                   w_attn_out=w_attn_out, w_pool_group=w_pool_group, pool_scale=pool_scale,
                   w_pool_out=w_pool_out, w_o=w_o, norm_mix_post=norm_mix_post,
                   norm_ffn_pre=norm_ffn_pre, w_up=w_up, conv_w=conv_w, conv_b=conv_b,
                   w_down=w_down, norm_ffn_post=norm_ffn_post)
    moms = dict(norm_mix_pre=(m_norm_mix_pre, v_norm_mix_pre), w_in=(m_w_in, v_w_in),
                b_gate=(m_b_gate, v_b_gate), rel_bias=(m_rel_bias, v_rel_bias),
                w_attn_out=(m_w_attn_out, v_w_attn_out),
                w_pool_group=(m_w_pool_group, v_w_pool_group),
                pool_scale=(m_pool_scale, v_pool_scale), w_pool_out=(m_w_pool_out, v_w_pool_out),
                w_o=(m_w_o, v_w_o), norm_mix_post=(m_norm_mix_post, v_norm_mix_post),
                norm_ffn_pre=(m_norm_ffn_pre, v_norm_ffn_pre), w_up=(m_w_up, v_w_up),
                conv_w=(m_conv_w, v_conv_w), conv_b=(m_conv_b, v_conv_b),
                w_down=(m_w_down, v_w_down), norm_ffn_post=(m_norm_ffn_post, v_norm_ffn_post))
    order = list(weights.keys())

    small_names = [nm for nm in order if nm not in names]
    for nm in names:
        delta[nm], new_m[nm], new_v[nm], grads[nm] = _adamw("adamw_" + nm, weights[nm], grads[nm],
                                                            *moms[nm])
    d_s, m_s, v_s = _adamw_small([weights[nm] for nm in small_names],
                                 [grads[nm] for nm in small_names],
                                 [moms[nm][0] for nm in small_names],
                                 [moms[nm][1] for nm in small_names])
    for i, nm in enumerate(small_names):
        delta[nm], new_m[nm], new_v[nm] = d_s[i], m_s[i], v_s[i]

    return (loss, grad_x, *[grads[nm] for nm in order], *[delta[nm] for nm in order],
            *[new_m[nm] for nm in order], *[new_v[nm] for nm in order])
```

```python
import functools
import math

import jax
import jax.numpy as jnp
from jax import lax
from jax.experimental import pallas as pl
from jax.experimental.pallas import tpu as pltpu

F32 = jnp.float32
BF16 = jnp.bfloat16
MESH = pl.DeviceIdType.MESH

D_MODEL = 1024
DEPTH = 2
CHUNK = 64
BAND_CHUNKS = 9
BAND = BAND_CHUNKS * CHUNK
HEADS = 8
HEAD_DIM = 64
ATTN_W = HEADS * HEAD_DIM
POOL_WINDOWS = (2, 4, 8, 16)
POOL_W = 512
POOL_GD = 128
MAX_REL = 256
N_REL = 2 * MAX_REL + 1
D_FF = 2816
IN_W = 3 * ATTN_W + POOL_W + 2 * D_MODEL
EPS = 1e-6
ATTN_SCALE = HEAD_DIM ** -0.5
BAND_PAD = 640
BIAS_LANES = BAND_PAD
N_CHIPS = 4

ADAM_LR = 0.001
ADAM_B1 = 0.9
ADAM_B2 = 0.999
ADAM_EPS = 1e-08
ADAM_WD = 0.01
ADAM_STEP = 10

VMEM_LIMIT_V7X = 56 * 1024 * 1024
TOK = 512
ATT_BLK = 8 * CHUNK
FF_COL = 256
FF_TOK = 1024
HALO = 32


def _cparams(*sem):
    return pltpu.CompilerParams(dimension_semantics=sem, vmem_limit_bytes=VMEM_LIMIT_V7X)


def _sds(shape, dtype):
    return jax.ShapeDtypeStruct(shape, dtype)


class _Comm:
    def __init__(self, ins, outs, copies, n_sems, alias=None):
        self.ins, self.outs, self.copies, self.n_sems = list(ins), list(outs), copies, n_sems
        self.alias = dict(alias or {})


class _SemsFrom:
    def __init__(self, sems, start):
        self.sems, self.start = sems, start

    @property
    def at(self):
        return self

    def __getitem__(self, i):
        return self.sems.at[self.start + i]


def _both(a, b):
    na, nao = len(a.ins), len(a.outs)

    def copies(cin, cout, ssem, rsem):
        return (a.copies(cin[:na], cout[:nao], ssem, rsem)
                + b.copies(cin[na:], cout[nao:], _SemsFrom(ssem, a.n_sems), _SemsFrom(rsem, a.n_sems)))

    alias = dict(a.alias)
    alias.update({na + i: nao + o for i, o in b.alias.items()})
    return _Comm(a.ins + b.ins, a.outs + b.outs, copies, a.n_sems + b.n_sems, alias)


def _pcall(body, name, grid, in_specs, out_specs, out_shape, scratch_shapes, compiler_params, args,
           comm=None, aliases=None):
    single = not isinstance(out_shape, (list, tuple))
    out_specs = [out_specs] if single else list(out_specs)
    out_shape = [out_shape] if single else list(out_shape)
    n_in, n_out = len(in_specs), len(out_specs)
    aliases = dict(aliases or {})
    if comm is None:
        res = pl.pallas_call(
            body, name=name, grid=grid, in_specs=list(in_specs), out_specs=out_specs,
            out_shape=out_shape, scratch_shapes=list(scratch_shapes),
            input_output_aliases=aliases, compiler_params=compiler_params)(*args)
        return (res[0] if single else res), None
    ci, co = len(comm.ins), len(comm.outs)

    def hosted(*refs):
        main_in, cin = refs[:n_in], refs[n_in:n_in + ci]
        main_out = refs[n_in + ci:n_in + ci + n_out]
        cout = refs[n_in + ci + n_out:n_in + ci + n_out + co]
        rest = refs[n_in + ci + n_out + co:]
        copies = comm.copies(cin, cout, rest[-2], rest[-1])
        ids = [pl.program_id(a) for a in range(len(grid))]
        first = functools.reduce(jnp.logical_and, [i == 0 for i in ids])
        last = functools.reduce(jnp.logical_and, [i == g - 1 for i, g in zip(ids, grid)])

        @pl.when(first)
        def _():
            for cp in copies:
                cp.start()

        body(*main_in, *main_out, *rest[:-2])

        @pl.when(last)
        def _():
            for cp in copies:
                cp.wait()

    for i, o in comm.alias.items():
        aliases[n_in + i] = n_out + o
    hbm = pl.BlockSpec(memory_space=pl.ANY)
    sems = pltpu.SemaphoreType.DMA((comm.n_sems,))
    res = pl.pallas_call(
        hosted, name=name, grid=grid, in_specs=list(in_specs) + [hbm] * ci,
        out_specs=out_specs + [hbm] * co, out_shape=out_shape + comm.outs,
        scratch_shapes=list(scratch_shapes) + [sems, sems],
        input_output_aliases=aliases, compiler_params=compiler_params)(*args, *comm.ins)
    return (res[0] if single else list(res[:n_out])), list(res[n_out:])


def _comm_call(name, comm):
    ci = len(comm.ins)

    def body(*refs):
        copies = comm.copies(refs[:ci], refs[ci:-2], refs[-2], refs[-1])
        for cp in copies:
            cp.start()
        for cp in copies:
            cp.wait()

    hbm = pl.BlockSpec(memory_space=pl.ANY)
    sems = pltpu.SemaphoreType.DMA((comm.n_sems,))
    return list(pl.pallas_call(
        body, name=name, in_specs=[hbm] * ci, out_specs=[hbm] * len(comm.outs),
        out_shape=comm.outs, scratch_shapes=[sems, sems],
        input_output_aliases=comm.alias)(*comm.ins))


def _matmul(name, a, b, a_spec, b_spec, o_spec, out_shape, grid, contract, nk, acc_shape,
            fill=None, comm=None):
    in_place = out_shape.dtype == F32

    def body(*refs):
        a_ref, b_ref = refs[0], refs[1]
        o_ref = refs[2 if fill is None else 3]
        scratch = refs[(3 if fill is None else 4):]
        part = lax.dot_general(a_ref[...], b_ref[...], (contract, ((), ())),
                               preferred_element_type=F32)
        if nk == 1:
            o_ref[...] = part.astype(o_ref.dtype)
        else:
            acc_ref = o_ref if in_place else scratch[0]
            k = pl.program_id(2)

            @pl.when(k == 0)
            def _():
                acc_ref[...] = part

            @pl.when(k > 0)
            def _():
                acc_ref[...] += part

            if not in_place:
                @pl.when(k == nk - 1)
                def _():
                    o_ref[...] = acc_ref[...].astype(o_ref.dtype)

    scratch = [] if nk == 1 or in_place else [pltpu.VMEM(acc_shape, F32)]
    in_specs, args, aliases = [a_spec, b_spec], [a, b], {}
    if fill is not None:
        in_specs.append(pl.BlockSpec(memory_space=pl.ANY))
        args.append(fill)
        aliases = {2: 0}
    out, moved = _pcall(body, name, grid, in_specs, o_spec, out_shape, scratch,
                        _cparams("parallel", "parallel", "arbitrary"), args, comm, aliases)
    return out if comm is None else (out, moved)


NN = ((1,), (0,))
NT = ((1,), (1,))
TN = ((0,), (0,))


def _tm(t):
    return min(t, 1024)


def _tt(t):
    return min(t, 2048)


def _col_block_spec(a, rows, nb, row_col):
    if a.ndim == 2:
        return pl.BlockSpec((rows, nb), row_col)

    def halves(*ids):
        r, c = row_col(*ids)
        return c // 2, r, c % 2

    return pl.BlockSpec((None, rows, nb), halves)


def _mm_nn_blocked(name, a, w, l, out_dtype, comm=None):
    t, k = a.shape
    nb = w.shape[3]
    tm = _tm(t)
    return _matmul(
        name, a, w,
        pl.BlockSpec((tm, k), lambda i, n, kk: (i, 0)),
        pl.BlockSpec((None, None, k, nb), lambda i, n, kk: (l, n, 0, 0)),
        pl.BlockSpec((tm, nb), lambda i, n, kk: (i, n)),
        _sds((t, N_CHIPS * nb), out_dtype), (t // tm, N_CHIPS, 1), NN, 1, None, comm=comm)


def _mm_nt_blocked(name, a, w, l, out_dtype, comm=None):
    t = a.shape[-2]
    k, nb = w.shape[2], w.shape[3]
    tm = _tm(t)
    return _matmul(
        name, a, w,
        _col_block_spec(a, tm, nb, lambda i, n, kk: (i, kk)),
        pl.BlockSpec((None, None, k, nb), lambda i, n, kk: (l, kk, 0, 0)),
        pl.BlockSpec((tm, k), lambda i, n, kk: (i, 0)),
        _sds((t, k), out_dtype), (t // tm, 1, N_CHIPS), NT, N_CHIPS, (tm, k), comm=comm)


def _mm_tn_blocked(name, a, g, l, fill):
    t, k = a.shape
    nb = g.shape[-1] * (g.ndim - 1) // N_CHIPS
    tt = _tt(t)
    nt = t // tt
    return _matmul(
        name, a, g,
        pl.BlockSpec((tt, k), lambda n, j, kk: (kk, 0)),
        _col_block_spec(g, tt, nb, lambda n, j, kk: (kk, n)),
        pl.BlockSpec((None, None, k, nb), lambda n, j, kk: (l, n, 0, 0)),
        _sds((DEPTH, N_CHIPS, k, nb), BF16), (N_CHIPS, 1, nt), TN, nt, (k, nb), fill)


def _proj_pieces(rows, dqkv_first):
    def piece(col):
        if dqkv_first:
            return pl.BlockSpec((rows, ATTN_W), lambda i, kk: (i, col))
        return pl.BlockSpec((rows, ATTN_W), lambda n, kk: (kk, col))
    return [piece(0), piece(1), piece(2), piece(0)]


def _proj_dx(name, dqkv, du, dgates, w, l, comm=None):
    t = du.shape[0]
    k, nb = w.shape[2], w.shape[3]
    tm = _tm(t)

    def body(dq_ref, dk_ref, dv_ref, du_ref, dg_ref, w_ref, o_ref):
        kk = pl.program_id(1)

        def mm(a):
            return lax.dot_general(a, w_ref[...], (NT, ((), ())), preferred_element_type=F32)

        @pl.when(kk == 0)
        def _():
            o_ref[...] = mm(jnp.concatenate([dq_ref[...], dk_ref[...]], axis=1))

        @pl.when(kk == 1)
        def _():
            o_ref[...] += mm(jnp.concatenate([dv_ref[...], du_ref[...]], axis=1))

        @pl.when(kk >= 2)
        def _():
            o_ref[...] += mm(dg_ref[...])

    out, moved = _pcall(
        body, name, (t // tm, N_CHIPS),
        _proj_pieces(tm, True)
        + [pl.BlockSpec((tm, nb), lambda i, kk: (i, jnp.maximum(kk - 2, 0))),
           pl.BlockSpec((None, None, k, nb), lambda i, kk: (l, kk, 0, 0))],
        pl.BlockSpec((tm, k), lambda i, kk: (i, 0)), _sds((t, k), F32),
        [], _cparams("arbitrary", "arbitrary"),
        (dqkv, dqkv, dqkv, du, dgates, w), comm)
    return out if comm is None else (out, moved)


def _proj_dw(name, h, dqkv, du, dgates, l, fill):
    t, k = h.shape
    nb = dgates.shape[1] // 2
    tt = _tm(t)
    nt = t // tt

    def body(*refs):
        h_ref, dq_ref, dk_ref, dv_ref, du_ref, dg_ref = refs[:6]
        o_ref, acc_ref = refs[-2], refs[-1]
        n, kk = pl.program_id(0), pl.program_id(1)

        def update(g):
            part = lax.dot_general(h_ref[...], g, (TN, ((), ())), preferred_element_type=F32)

            @pl.when(kk == 0)
            def _():
                acc_ref[...] = part

            @pl.when(kk > 0)
            def _():
                acc_ref[...] += part

        @pl.when(n == 0)
        def _():
            update(jnp.concatenate([dq_ref[...], dk_ref[...]], axis=1))

        @pl.when(n == 1)
        def _():
            update(jnp.concatenate([dv_ref[...], du_ref[...]], axis=1))

        @pl.when(n >= 2)
        def _():
            update(dg_ref[...])

        @pl.when(kk == nt - 1)
        def _():
            o_ref[...] = acc_ref[...].astype(BF16)

    in_specs = ([pl.BlockSpec((tt, k), lambda n, kk: (kk, 0))] + _proj_pieces(tt, False)
                + [pl.BlockSpec((tt, nb), lambda n, kk: (kk, jnp.maximum(n - 2, 0)))])
    args, aliases = [h, dqkv, dqkv, dqkv, du, dgates], {}
    if fill is not None:
        in_specs.append(pl.BlockSpec(memory_space=pl.ANY))
        args.append(fill)
        aliases = {6: 0}
    return pl.pallas_call(
        body, name=name, grid=(N_CHIPS, nt), in_specs=in_specs,
        out_specs=pl.BlockSpec((None, None, k, nb), lambda n, kk: (l, n, 0, 0)),
        out_shape=_sds((DEPTH, N_CHIPS, k, nb), BF16),
        scratch_shapes=[pltpu.VMEM((k, nb), F32)], input_output_aliases=aliases,
        compiler_params=_cparams("parallel", "arbitrary"))(*args)


def _narrow_nn(name, a, w, l):
    t, k = a.shape
    nb = w.shape[3]
    tm = _tm(t)

    def body(a_ref, w_ref, o_ref):
        av = a_ref[...]
        for j in range(N_CHIPS):
            o_ref[:, j * nb:(j + 1) * nb] = jnp.dot(
                av, w_ref[j], preferred_element_type=F32).astype(BF16)

    return pl.pallas_call(
        body, name=name, grid=(t // tm,),
        in_specs=[pl.BlockSpec((tm, k), lambda i: (i, 0)),
                  pl.BlockSpec((None, N_CHIPS, k, nb), lambda i: (l, 0, 0, 0))],
        out_specs=pl.BlockSpec((tm, N_CHIPS * nb), lambda i: (i, 0)),
        out_shape=_sds((t, N_CHIPS * nb), BF16), compiler_params=_cparams("parallel"))(a, w)


def _narrow_nt(name, a, w, l):
    t = a.shape[0]
    k, nb = w.shape[2], w.shape[3]
    tm = _tm(t)

    def body(a_ref, w_ref, o_ref):
        acc = lax.dot_general(a_ref[:, 0:nb], w_ref[0], (NT, ((), ())), preferred_element_type=F32)
        for j in range(1, N_CHIPS):
            acc = acc + lax.dot_general(a_ref[:, j * nb:(j + 1) * nb], w_ref[j], (NT, ((), ())),
                                        preferred_element_type=F32)
        o_ref[...] = acc.astype(BF16)

    return pl.pallas_call(
        body, name=name, grid=(t // tm,),
        in_specs=[pl.BlockSpec((tm, N_CHIPS * nb), lambda i: (i, 0)),
                  pl.BlockSpec((None, N_CHIPS, k, nb), lambda i: (l, 0, 0, 0))],
        out_specs=pl.BlockSpec((tm, k), lambda i: (i, 0)),
        out_shape=_sds((t, k), BF16), compiler_params=_cparams("parallel"))(a, w)


def _narrow_tn(name, a, g, l, fill):
    t, k = a.shape
    nb = g.shape[1] // N_CHIPS
    tt = _tm(t)
    nt = t // tt

    def body(*refs):
        a_ref, g_ref, o_ref, acc_ref = refs[0], refs[1], refs[-2], refs[-1]
        i = pl.program_id(0)
        part = lax.dot_general(a_ref[...], g_ref[...], (TN, ((), ())), preferred_element_type=F32)

        @pl.when(i == 0)
        def _():
            acc_ref[...] = part

        @pl.when(i > 0)
        def _():
            acc_ref[...] += part

        @pl.when(i == nt - 1)
        def _():
            for j in range(N_CHIPS):
                o_ref[j] = acc_ref[:, j * nb:(j + 1) * nb].astype(BF16)

    in_specs = [pl.BlockSpec((tt, k), lambda i: (i, 0)),
                pl.BlockSpec((tt, N_CHIPS * nb), lambda i: (i, 0))]
    args, aliases = [a, g], {}
    if fill is not None:
        in_specs.append(pl.BlockSpec(memory_space=pl.ANY))
        args.append(fill)
        aliases = {2: 0}
    return pl.pallas_call(
        body, name=name, grid=(nt,), in_specs=in_specs,
        out_specs=pl.BlockSpec((None, N_CHIPS, k, nb), lambda i: (l, 0, 0, 0)),
        out_shape=_sds((DEPTH, N_CHIPS, k, nb), BF16),
        scratch_shapes=[pltpu.VMEM((k, N_CHIPS * nb), F32)], input_output_aliases=aliases,
        compiler_params=_cparams("arbitrary"))(*args)


def _mm_nn(name, a, w, l, tk, out_dtype):
    t, k = a.shape
    n = w.shape[2]
    tm = _tm(t)
    nk = k // tk
    return _matmul(
        name, a, w,
        pl.BlockSpec((tm, tk), lambda i, j, kk: (i, kk)),
        pl.BlockSpec((None, tk, n), lambda i, j, kk: (l, kk, 0)),
        pl.BlockSpec((tm, n), lambda i, j, kk: (i, 0)),
        _sds((t, n), out_dtype), (t // tm, 1, nk), NN, nk, (tm, n))


def _mm_nt(name, a, w, l, tn, out_dtype, comm=None):
    t, n = a.shape
    k = w.shape[1]
    tm = _tm(t)
    return _matmul(
        name, a, w,
        pl.BlockSpec((tm, n), lambda i, j, kk: (i, 0)),
        pl.BlockSpec((None, tn, n), lambda i, j, kk: (l, j, 0)),
        pl.BlockSpec((tm, tn), lambda i, j, kk: (i, j)),
        _sds((t, k), out_dtype), (t // tm, k // tn, 1), NT, 1, None, comm=comm)


def _mm_tn(name, a, g, tko, l, fill):
    t, k = a.shape
    n = g.shape[1]
    tt = _tt(t)
    nt = t // tt
    return _matmul(
        name, a, g,
        pl.BlockSpec((tt, tko), lambda i, j, kk: (kk, i)),
        pl.BlockSpec((tt, n), lambda i, j, kk: (kk, 0)),
        pl.BlockSpec((None, tko, n), lambda i, j, kk: (l, i, 0)),
        _sds((DEPTH, k, n), BF16), (k // tko, 1, nt), TN, nt, (tko, n), fill)


def _row_spec(width, col=0):
    return pl.BlockSpec((TOK, width), lambda i: (i, col))


def _vec_spec(width):
    return pl.BlockSpec((1, width), lambda i: (0, 0))


def _rms(x):
    return lax.rsqrt(jnp.mean(x * x, axis=-1, keepdims=True) + EPS)


def _norm_fwd(name, x, g, comm=None):
    t = x.shape[0]

    def body(x_ref, g_ref, h_ref):
        xv = x_ref[...]
        h_ref[...] = (xv * _rms(xv) * g_ref[...]).astype(BF16)

    out, moved = _pcall(body, name, (t // TOK,), [_row_spec(D_MODEL), _vec_spec(D_MODEL)],
                        _row_spec(D_MODEL), _sds((t, D_MODEL), BF16), [], _cparams("arbitrary"),
                        (x, g), comm)
    return out if comm is None else (out, moved)


ROWS = 16
ROW_UNROLL = 8


def _rows(k):
    return pl.ds(pl.multiple_of(k * ROWS, ROWS), ROWS)


def _strips(step, init):
    def group(j, carry):
        for u in range(ROW_UNROLL):
            carry = step(j * ROW_UNROLL + u, carry)
        return carry

    return lax.fori_loop(0, TOK // (ROWS * ROW_UNROLL), group, init)


def _fold_rows(x):
    return x[0:8] + x[8:16]


def _accumulate(ref, part):
    total = jnp.sum(part, axis=0, keepdims=True)

    @pl.when(pl.program_id(0) == 0)
    def _():
        ref[...] = total

    @pl.when(pl.program_id(0) > 0)
    def _():
        ref[...] += total


def _norm_bwd_rows(d, mv, g):
    r = _rms(mv)
    n = mv * r
    dn = d * g
    return r * (dn - n * jnp.mean(dn * n, axis=-1, keepdims=True)), d * n


def _post_pre_fwd(name, xres, m, g_post, g_pre, comm=None):
    t = xres.shape[0]

    def body(x_ref, m_ref, gp_ref, gn_ref, x1_ref, h_ref):
        def strip(k, c):
            rows = _rows(k)
            mv = m_ref[rows, :]
            x1 = x_ref[rows, :] + mv * _rms(mv) * gp_ref[...]
            x1_ref[rows, :] = x1
            h_ref[rows, :] = (x1 * _rms(x1) * gn_ref[...]).astype(BF16)
            return c

        _strips(strip, 0)

    outs, moved = _pcall(
        body, name, (t // TOK,),
        [_row_spec(D_MODEL), _row_spec(D_MODEL), _vec_spec(D_MODEL), _vec_spec(D_MODEL)],
        [_row_spec(D_MODEL), _row_spec(D_MODEL)],
        [_sds((t, D_MODEL), F32), _sds((t, D_MODEL), BF16)], [], _cparams("arbitrary"),
        (xres, m, g_post, g_pre), comm)
    return outs if comm is None else (*outs, moved)


def _tail(name, xres, m, g_post, target):
    t = xres.shape[0]

    def body(x_ref, m_ref, g_ref, t_ref, dy_ref, dm_ref, dg_ref, l_ref):
        def strip(k, carry):
            rows = _rows(k)
            mv = m_ref[rows, :]
            e = x_ref[rows, :] + mv * _rms(mv) * g_ref[...] - t_ref[rows, :]
            dy = e * (1.0 / D_MODEL)
            dy_ref[rows, :] = dy
            dm, dgn = _norm_bwd_rows(dy, mv, g_ref[...])
            dm_ref[rows, :] = dm.astype(BF16)
            return carry[0] + _fold_rows(dgn), carry[1] + _fold_rows(e * e)

        zero = jnp.zeros((8, D_MODEL), F32)
        dg, sq = _strips(strip, (zero, zero))
        _accumulate(dg_ref, dg)
        _accumulate(l_ref, jnp.sum(sq, axis=1, keepdims=True))

    dy, dm, dg, sq = pl.pallas_call(
        body, name=name, grid=(t // TOK,),
        in_specs=[_row_spec(D_MODEL), _row_spec(D_MODEL), _vec_spec(D_MODEL), _row_spec(D_MODEL)],
        out_specs=[_row_spec(D_MODEL), _row_spec(D_MODEL), _vec_spec(D_MODEL),
                   pl.BlockSpec((1, 1), lambda i: (0, 0))],
        out_shape=[_sds((t, D_MODEL), F32), _sds((t, D_MODEL), BF16), _sds((1, D_MODEL), F32),
                   _sds((1, 1), F32)],
        compiler_params=_cparams("arbitrary"))(xres, m, g_post, target)
    return dy, dm, dg, sq[0, 0] * (0.5 / D_MODEL)


def _pre_post_bwd(name, dh, xin, dxo, g_pre, m, g_post, comm=None):
    t = dh.shape[0]

    def body(dh_ref, x_ref, d_ref, gq_ref, m_ref, gp_ref, dx_ref, dgq_ref, dm_ref, dgp_ref):
        def strip(k, carry):
            rows = _rows(k)
            dxin, dgq = _norm_bwd_rows(dh_ref[rows, :], x_ref[rows, :], gq_ref[...])
            dx = d_ref[rows, :] + dxin
            dx_ref[rows, :] = dx
            dm, dgp = _norm_bwd_rows(dx, m_ref[rows, :], gp_ref[...])
            dm_ref[rows, :] = dm.astype(BF16)
            return carry[0] + _fold_rows(dgq), carry[1] + _fold_rows(dgp)

        zero = jnp.zeros((8, D_MODEL), F32)
        dgq, dgp = _strips(strip, (zero, zero))
        _accumulate(dgq_ref, dgq)
        _accumulate(dgp_ref, dgp)

    outs, moved = _pcall(
        body, name, (t // TOK,),
        [_row_spec(D_MODEL), _row_spec(D_MODEL), _row_spec(D_MODEL), _vec_spec(D_MODEL),
         _row_spec(D_MODEL), _vec_spec(D_MODEL)],
        [_row_spec(D_MODEL), _vec_spec(D_MODEL), _row_spec(D_MODEL), _vec_spec(D_MODEL)],
        [_sds((t, D_MODEL), F32), _sds((1, D_MODEL), F32), _sds((t, D_MODEL), BF16),
         _sds((1, D_MODEL), F32)], [], _cparams("arbitrary"),
        (dh, xin, dxo, g_pre, m, g_post), comm)
    return outs if comm is None else (*outs, moved)


def _norm_pre_bwd(name, dh, xin, dxo, g, comm=None):
    t = dh.shape[0]

    def body(dh_ref, x_ref, d_ref, g_ref, dx_ref, dg_ref):
        xv = x_ref[...]
        dhv = dh_ref[...]
        r = _rms(xv)
        n = xv * r
        dn = dhv * g_ref[...]
        dx_ref[...] = d_ref[...] + r * (dn - n * jnp.mean(dn * n, axis=-1, keepdims=True))
        part = jnp.sum(dhv * n, axis=0, keepdims=True)

        @pl.when(pl.program_id(0) == 0)
        def _():
            dg_ref[...] = part

        @pl.when(pl.program_id(0) > 0)
        def _():
            dg_ref[...] += part

    out, moved = _pcall(
        body, name, (t // TOK,),
        [_row_spec(D_MODEL), _row_spec(D_MODEL), _row_spec(D_MODEL), _vec_spec(D_MODEL)],
        [_row_spec(D_MODEL), _vec_spec(D_MODEL)],
        [_sds((t, D_MODEL), F32), _sds((1, D_MODEL), F32)], [], _cparams("arbitrary"),
        (dh, xin, dxo, g), comm)
    return out if comm is None else (*out, moved)


def _gate_fwd(name, proj, b_gate, ya, yb):
    t = proj.shape[0]

    def body(ga_ref, gb_ref, b_ref, ya_ref, yb_ref, z_ref):
        def strip(k, c):
            rows = _rows(k)
            sa = jax.nn.sigmoid(ga_ref[rows, :].astype(F32) + b_ref[:, :D_MODEL])
            sb = jax.nn.sigmoid(gb_ref[rows, :].astype(F32) + b_ref[:, D_MODEL:])
            z_ref[rows, :] = (sa * ya_ref[rows, :].astype(F32)
                              + sb * yb_ref[rows, :].astype(F32)).astype(BF16)
            return c

        _strips(strip, 0)

    return pl.pallas_call(
        body, name=name, grid=(t // TOK,),
        in_specs=[_row_spec(D_MODEL, 2), _row_spec(D_MODEL, 3), _vec_spec(2 * D_MODEL),
                  _row_spec(D_MODEL), _row_spec(D_MODEL)],
        out_specs=_row_spec(D_MODEL), out_shape=_sds((t, D_MODEL), BF16),
        compiler_params=_cparams("parallel"))(proj, proj, b_gate, ya, yb)


def _gate_bwd(name, dz, proj, b_gate, ya, yb):
    t = proj.shape[0]

    def body(dz_ref, ga_ref, gb_ref, b_ref, ya_ref, yb_ref, dya_ref, dyb_ref, dg_ref, db_ref):
        def strip(k, carry):
            rows = _rows(k)
            dzv = dz_ref[rows, :].astype(F32)
            sa = jax.nn.sigmoid(ga_ref[rows, :].astype(F32) + b_ref[:, :D_MODEL])
            sb = jax.nn.sigmoid(gb_ref[rows, :].astype(F32) + b_ref[:, D_MODEL:])
            dya_ref[rows, :] = (dzv * sa).astype(BF16)
            dyb_ref[rows, :] = (dzv * sb).astype(BF16)
            dga = dzv * ya_ref[rows, :].astype(F32) * sa * (1.0 - sa)
            dgb = dzv * yb_ref[rows, :].astype(F32) * sb * (1.0 - sb)
            dg_ref[rows, :D_MODEL] = dga.astype(BF16)
            dg_ref[rows, D_MODEL:] = dgb.astype(BF16)
            return carry[0] + _fold_rows(dga), carry[1] + _fold_rows(dgb)

        zero = jnp.zeros((8, D_MODEL), F32)
        pa, pb = _strips(strip, (zero, zero))
        _accumulate(db_ref.at[:, :D_MODEL], pa)
        _accumulate(db_ref.at[:, D_MODEL:], pb)

    return pl.pallas_call(
        body, name=name, grid=(t // TOK,),
        in_specs=[_row_spec(D_MODEL), _row_spec(D_MODEL, 2), _row_spec(D_MODEL, 3),
                  _vec_spec(2 * D_MODEL), _row_spec(D_MODEL), _row_spec(D_MODEL)],
        out_specs=[_row_spec(D_MODEL), _row_spec(D_MODEL), _row_spec(2 * D_MODEL),
                   _vec_spec(2 * D_MODEL)],
        out_shape=[_sds((t, D_MODEL), BF16), _sds((t, D_MODEL), BF16),
                   _sds((t, 2 * D_MODEL), BF16), _sds((1, 2 * D_MODEL), F32)],
        compiler_params=_cparams("arbitrary"))(dz, proj, proj, b_gate, ya, yb)


def _head_masks():
    lane = lax.broadcasted_iota(jnp.int32, (1, 2 * HEAD_DIM), 1)
    return lane < HEAD_DIM


BAND_ROWS = 2 * ATT_BLK + CHUNK


def _fill_band(band, prev_ref, cur_ref):
    band[0:ATT_BLK, :] = prev_ref[...]
    band[ATT_BLK:2 * ATT_BLK, :] = cur_ref[...]
    band[2 * ATT_BLK:, :] = jnp.zeros((CHUNK, ATTN_W), BF16)


def _pair_rows(x2, low):
    zero = jnp.zeros_like(x2)
    return jnp.concatenate([jnp.where(low, x2, zero), jnp.where(low, zero, x2)], axis=0)


def _pair_diag(o2, low):
    return jnp.where(low, o2[0:CHUNK, :], o2[CHUNK:, :])


N_PAIRS = HEADS // 2
SM_STRIP = 32
N_STRIPS = BAND_PAD // SM_STRIP
SM_UNROLL = 4
NEG = -1e30


def _fold8(x, op):
    return op(op(x[0:8], x[8:16]), op(x[16:24], x[24:32]))


def _strip(k):
    return pl.ds(pl.multiple_of(k * SM_STRIP, SM_STRIP), SM_STRIP)


def _band_probs(k2, qcat, bias_t, first_key):
    kpos = lax.broadcasted_iota(jnp.int32, (BAND_PAD, 1), 0)
    st = lax.dot_general(k2, qcat, (NT, ((), ())), preferred_element_type=F32)
    st = jnp.where(kpos + first_key >= 0, st + bias_t, NEG)
    e = jnp.exp(st - jnp.max(st, axis=0, keepdims=True))
    return e * (1.0 / jnp.sum(e, axis=0, keepdims=True))


def _attn_specs(nblk):
    cur = lambda col: pl.BlockSpec((ATT_BLK, ATTN_W), lambda s: (jnp.minimum(s, nblk - 1), col))
    prev = lambda col: pl.BlockSpec(
        (ATT_BLK, ATTN_W), lambda s: (jnp.maximum(jnp.minimum(s, nblk - 1) - 1, 0), col))
    return cur, prev


def _attn_fwd(name, proj, bias, comm=None):
    t = proj.shape[0]
    nblk = t // ATT_BLK
    cur, prev = _attn_specs(nblk)

    def body(q_ref, kp_ref, kc_ref, vp_ref, vc_ref, b_ref, o_ref, p_ref, kband, vband):
        s = pl.program_id(0)
        _fill_band(kband, kp_ref, kc_ref)
        _fill_band(vband, vp_ref, vc_ref)
        low = _head_masks()

        def chunk(ci):
            r0 = pl.multiple_of(ci * CHUNK, CHUNK)
            for hp in range(N_PAIRS):
                cols = slice(hp * 128, (hp + 1) * 128)
                qcat = _pair_rows(q_ref[pl.ds(r0, CHUNK), cols] * ATTN_SCALE, low)
                p = _band_probs(kband[pl.ds(r0, BAND_PAD), cols], qcat, b_ref[hp],
                                (s * 8 - 8 + ci) * CHUNK).astype(BF16)
                p_ref[ci, hp] = p
                o2 = lax.dot_general(p, vband[pl.ds(r0, BAND_PAD), cols],
                                     (TN, ((), ())), preferred_element_type=F32)
                o_ref[pl.ds(r0, CHUNK), cols] = _pair_diag(o2, low).astype(BF16)

        def chunks(j, carry):
            for u in range(4):
                chunk(4 * j + u)
            return carry

        lax.fori_loop(0, 2, chunks, 0)

    outs, moved = _pcall(
        body, name, (nblk,),
        [cur(0), prev(1), cur(1), prev(2), cur(2),
         pl.BlockSpec((N_PAIRS, BAND_PAD, 128), lambda s: (0, 0, 0))],
        [pl.BlockSpec((ATT_BLK, ATTN_W), lambda s: (s, 0)),
         pl.BlockSpec((8, N_PAIRS, BAND_PAD, 128), lambda s: (s, 0, 0, 0))],
        [_sds((t, ATTN_W), BF16), _sds((t // CHUNK, N_PAIRS, BAND_PAD, 128), BF16)],
        [pltpu.VMEM((BAND_ROWS, ATTN_W), BF16), pltpu.VMEM((BAND_ROWS, ATTN_W), BF16)],
        _cparams("arbitrary"), (proj, proj, proj, proj, proj, bias), comm)
    return outs if comm is None else (*outs, moved)


def _attn_bwd(name, proj, datt, probs, comm=None):
    t = proj.shape[0]
    nblk = t // ATT_BLK
    cur, prev = _attn_specs(nblk)
    late = pl.BlockSpec((ATT_BLK, 3 * ATTN_W), lambda s: (jnp.maximum(s - 1, 0), 0))

    def body(q_ref, kp_ref, kc_ref, vp_ref, vc_ref, do_ref, p_ref,
             dqkv_ref, db_ref, kband, vband, dkacc, dvacc,
             dp_ref, dsb_ref, qc_ref, dc_ref, dq_ref, dq_held):
        s = pl.program_id(0)

        @pl.when(s == 0)
        def _():
            dkacc[...] = jnp.zeros_like(dkacc)
            dvacc[...] = jnp.zeros_like(dvacc)
            db_ref[...] = jnp.zeros_like(db_ref)
            dq_ref[...] = jnp.zeros_like(dq_ref)

        @pl.when(s < nblk)
        def _():
            _fill_band(kband, kp_ref, kc_ref)
            _fill_band(vband, vp_ref, vc_ref)
            low = _head_masks()

            def chunk(ci, carry):
                r0 = pl.multiple_of(ci * CHUNK, CHUNK)
                for hp in range(N_PAIRS):
                    cols = slice(hp * 128, (hp + 1) * 128)
                    qc_ref[hp] = _pair_rows(q_ref[pl.ds(r0, CHUNK), cols] * ATTN_SCALE, low)
                    dc_ref[hp] = _pair_rows(do_ref[pl.ds(r0, CHUNK), cols], low)
                    dp_ref[hp] = lax.dot_general(vband[pl.ds(r0, BAND_PAD), cols], dc_ref[hp],
                                                 (NT, ((), ())), preferred_element_type=F32)

                def sums(j, acc):
                    for u in range(SM_UNROLL):
                        rows = _strip(SM_UNROLL * j + u)
                        acc = tuple(acc[hp] + _fold8(p_ref[ci, hp, rows, :].astype(F32)
                                                     * dp_ref[hp, rows, :], jnp.add)
                                    for hp in range(N_PAIRS))
                    return acc

                acc = lax.fori_loop(0, N_STRIPS // SM_UNROLL, sums,
                                    (jnp.zeros((8, 128), F32),) * N_PAIRS)
                delta = [jnp.sum(a, axis=0, keepdims=True) for a in acc]

                def grads(j, c):
                    for u in range(SM_UNROLL):
                        rows = _strip(SM_UNROLL * j + u)
                        for hp in range(N_PAIRS):
                            ds = (p_ref[ci, hp, rows, :].astype(F32)
                                  * (dp_ref[hp, rows, :] - delta[hp]))
                            db_ref[hp, rows, :] += ds
                            dsb_ref[hp, rows, :] = ds.astype(BF16)
                    return c

                lax.fori_loop(0, N_STRIPS // SM_UNROLL, grads, 0)
                for hp in range(N_PAIRS):
                    cols = slice(hp * 128, (hp + 1) * 128)
                    dq2 = lax.dot_general(dsb_ref[hp], kband[pl.ds(r0, BAND_PAD), cols],
                                          (TN, ((), ())), preferred_element_type=F32)
                    dq_ref[pl.ds(r0, CHUNK), cols] = (_pair_diag(dq2, low) * ATTN_SCALE).astype(BF16)
                    dkacc[pl.ds(r0, BAND_PAD), cols] += jnp.dot(dsb_ref[hp], qc_ref[hp],
                                                               preferred_element_type=F32)
                    dvacc[pl.ds(r0, BAND_PAD), cols] += jnp.dot(p_ref[ci, hp], dc_ref[hp],
                                                               preferred_element_type=F32)
                return carry

            dq_held[...] = dq_ref[...]
            lax.fori_loop(0, 8, chunk, 0)

        @pl.when(s == nblk)
        def _():
            dq_held[...] = dq_ref[...]

        dqkv_ref[:, 0:ATTN_W] = dq_held[...]
        dqkv_ref[:, ATTN_W:2 * ATTN_W] = dkacc[0:ATT_BLK, :].astype(BF16)
        dqkv_ref[:, 2 * ATTN_W:] = dvacc[0:ATT_BLK, :].astype(BF16)
        dkacc[0:ATT_BLK, :] = dkacc[ATT_BLK:2 * ATT_BLK, :]
        dvacc[0:ATT_BLK, :] = dvacc[ATT_BLK:2 * ATT_BLK, :]
        dkacc[ATT_BLK:, :] = jnp.zeros((ATT_BLK + CHUNK, ATTN_W), F32)
        dvacc[ATT_BLK:, :] = jnp.zeros((ATT_BLK + CHUNK, ATTN_W), F32)

    outs, moved = _pcall(
        body, name, (nblk + 1,),
        [cur(0), prev(1), cur(1), prev(2), cur(2),
         pl.BlockSpec((ATT_BLK, ATTN_W), lambda s: (jnp.minimum(s, nblk - 1), 0)),
         pl.BlockSpec((8, N_PAIRS, BAND_PAD, 128), lambda s: (jnp.minimum(s, nblk - 1), 0, 0, 0))],
        [late, pl.BlockSpec((HEADS // 2, BAND_PAD, 128), lambda s: (0, 0, 0))],
        [_sds((t, 3 * ATTN_W), BF16), _sds((HEADS // 2, BAND_PAD, 128), F32)],
        [pltpu.VMEM((BAND_ROWS, ATTN_W), BF16), pltpu.VMEM((BAND_ROWS, ATTN_W), BF16),
         pltpu.VMEM((BAND_ROWS, ATTN_W), F32), pltpu.VMEM((BAND_ROWS, ATTN_W), F32),
         pltpu.VMEM((N_PAIRS, BAND_PAD, 128), F32), pltpu.VMEM((N_PAIRS, BAND_PAD, 128), BF16),
         pltpu.VMEM((N_PAIRS, 2 * CHUNK, 128), BF16), pltpu.VMEM((N_PAIRS, 2 * CHUNK, 128), BF16),
         pltpu.VMEM((ATT_BLK, ATTN_W), BF16), pltpu.VMEM((ATT_BLK, ATTN_W), BF16)],
        _cparams("arbitrary"), (proj, proj, proj, proj, proj, datt, probs), comm)
    return outs if comm is None else (*outs, moved)


def _diag_onehot(rel_rows):
    d0 = lax.broadcasted_iota(jnp.int32, (BIAS_LANES, BIAS_LANES), 0)
    d1 = lax.broadcasted_iota(jnp.int32, (BIAS_LANES, BIAS_LANES), 1)
    m, n = (d0, d1) if rel_rows else (d1, d0)
    hit = (m == jnp.minimum(BAND - 1 + MAX_REL - n, 2 * MAX_REL)) & (n < BAND + CHUNK - 1)
    return jnp.where(hit, 1.0, 0.0).astype(F32)


def _bias_table(name, rel_bias_l):
    rel_pad = jnp.pad(rel_bias_l, ((0, 0), (0, BIAS_LANES - N_REL)))

    def body(r_ref, o_ref):
        diag = jnp.dot(r_ref[...], _diag_onehot(True), preferred_element_type=F32,
                       precision=lax.Precision.HIGHEST)
        rowid = lax.broadcasted_iota(jnp.int32, (8, BIAS_LANES), 0)
        lane = lax.broadcasted_iota(jnp.int32, (8, BIAS_LANES), 1)
        for h in range(HEADS):
            d8 = jnp.broadcast_to(diag[h:h + 1, :], (8, BIAS_LANES))
            slab0 = pltpu.roll(d8, BIAS_LANES - CHUNK + 1, axis=1)
            for b in range(1, 8):
                slab0 = jnp.where(rowid == b, pltpu.roll(d8, BIAS_LANES - CHUNK + 1 + b, axis=1),
                                  slab0)
            for a in range(8):
                slab = slab0 if a == 0 else pltpu.roll(slab0, 8 * a, axis=1)
                o_ref[h * CHUNK + 8 * a:h * CHUNK + 8 * a + 8, :] = jnp.where(lane < BAND, slab, NEG)

    tab = pl.pallas_call(
        body, name=name,
        in_specs=[pl.BlockSpec(memory_space=pltpu.VMEM)],
        out_specs=pl.BlockSpec(memory_space=pltpu.VMEM),
        out_shape=_sds((HEADS * CHUNK, BIAS_LANES), F32),
    )(rel_pad)
    tab = tab.reshape(HEADS // 2, 2, CHUNK, BIAS_LANES)
    return jnp.transpose(tab, (0, 3, 1, 2)).reshape(HEADS // 2, BIAS_LANES, 2 * CHUNK)


def _bias_fold(name, dbias_t):
    rows = HEADS * CHUNK
    dbias = jnp.transpose(dbias_t.reshape(HEADS // 2, BIAS_LANES, 2, CHUNK), (0, 2, 3, 1))

    def body(d_ref, o_ref):
        rowid = lax.broadcasted_iota(jnp.int32, (8, BIAS_LANES), 0)
        diags = []
        for h in range(HEADS):
            acc = d_ref[h * CHUNK + 56:h * CHUNK + 64, :]
            for a in range(7):
                slab = d_ref[h * CHUNK + 8 * a:h * CHUNK + 8 * a + 8, :]
                acc = acc + pltpu.roll(slab, 56 - 8 * a, axis=1)
            tot = jnp.where(rowid == 7, acc, 0.0)
            for b in range(7):
                tot = tot + jnp.where(rowid == b, pltpu.roll(acc, 7 - b, axis=1), 0.0)
            diags.append(jnp.sum(tot, axis=0, keepdims=True))
        diag = jnp.concatenate(diags, axis=0)
        o_ref[...] = jnp.dot(diag, _diag_onehot(False), preferred_element_type=F32,
                             precision=lax.Precision.HIGHEST)

    return pl.pallas_call(
        body, name=name,
        in_specs=[pl.BlockSpec(memory_space=pltpu.VMEM)],
        out_specs=pl.BlockSpec(memory_space=pltpu.VMEM),
        out_shape=_sds((HEADS, BIAS_LANES), F32),
    )(dbias.reshape(rows, BIAS_LANES))


def _inv_counts(i):
    trow = lax.broadcasted_iota(jnp.int32, (TOK + HALO, 1), 0) + i * TOK
    return [1.0 / jnp.minimum(trow + 1, w).astype(F32) for w in POOL_WINDOWS]


def _pool_fwd(name, proj, wg, scale, comm=None):
    t = proj.shape[0]
    hb = TOK // HALO

    def body(u_ref, up_ref, wg_ref, sc_ref, pooled_ref, mixed_ref, b0, b1, b2, b3):
        i = pl.program_id(0)
        halo = up_ref[...].astype(F32)
        b0[0:HALO, :] = jnp.where(i == 0, jnp.zeros_like(halo), halo)
        b0[HALO:, :] = u_ref[...].astype(F32)
        n = TOK + HALO
        b1[8:n, :] = b0[8:n, :] + b0[7:n - 1, :]
        b2[16:n, 128:] = b1[16:n, 128:] + b1[14:n - 2, 128:]
        b3[24:n, 256:] = b2[24:n, 256:] + b2[20:n - 4, 256:]
        wins = [b1[HALO:n, 0:128], b2[HALO:n, 128:256], b3[HALO:n, 256:384],
                b3[HALO:n, 384:512] + b3[HALO - 8:n - 8, 384:512]]
        inv = _inv_counts(i)
        for g in range(4):
            cols = slice(g * POOL_GD, (g + 1) * POOL_GD)
            pooled = (wins[g] * inv[g][0:TOK] - b0[HALO:n, cols]).astype(BF16)
            pooled_ref[:, cols] = pooled
            pre = jnp.dot(pooled, wg_ref[g], preferred_element_type=F32)
            mixed_ref[:, cols] = (pre * sc_ref[:, cols]).astype(BF16)

    buf = pltpu.VMEM((TOK + HALO, POOL_W), F32)
    outs, moved = _pcall(
        body, name, (t // TOK,),
        [_row_spec(POOL_W, 3),
         pl.BlockSpec((HALO, POOL_W), lambda i: (jnp.maximum(i * hb - 1, 0), 3)),
         pl.BlockSpec((4, POOL_GD, POOL_GD), lambda i: (0, 0, 0)), _vec_spec(POOL_W)],
        [_row_spec(POOL_W), _row_spec(POOL_W)],
        [_sds((t, POOL_W), BF16), _sds((t, POOL_W), BF16)], [buf, buf, buf, buf],
        _cparams("arbitrary"), (proj, proj, wg, scale), comm)
    return outs if comm is None else (*outs, moved)


def _pool_bwd(name, dmixed, pooled, wg, scale, comm=None):
    t = dmixed.shape[0]
    nt = t // TOK
    hb = TOK // HALO

    def body(dm_ref, dmn_ref, p_ref, wg_ref, sc_ref, du_ref, dwg_ref, dsc_ref, c0, c1, c2, c3):
        i = pl.program_id(0)

        @pl.when(i == 0)
        def _():
            dwg_ref[...] = jnp.zeros_like(dwg_ref)
            dsc_ref[...] = jnp.zeros_like(dsc_ref)

        n = TOK + HALO
        inv = _inv_counts(i)
        dmv = dm_ref[...].astype(F32)
        dmn = dmn_ref[...].astype(F32)
        dmn = jnp.where(i == nt - 1, jnp.zeros_like(dmn), dmn)
        for g in range(4):
            cols = slice(g * POOL_GD, (g + 1) * POOL_GD)
            scg = sc_ref[:, cols]
            pg = p_ref[:, cols]
            dpre = (dmv[:, cols] * scg).astype(BF16)
            dpre_n = (dmn[:, cols] * scg).astype(BF16)
            pre = jnp.dot(pg, wg_ref[g], preferred_element_type=F32)
            dsc_ref[:, cols] += jnp.sum(dmv[:, cols] * pre, axis=0, keepdims=True)
            dwg_ref[g] += lax.dot_general(pg, dpre, (TN, ((), ())), preferred_element_type=F32)
            dpool = lax.dot_general(dpre, wg_ref[g], (NT, ((), ())), preferred_element_type=F32)
            dpool_n = lax.dot_general(dpre_n, wg_ref[g], (NT, ((), ())),
                                      preferred_element_type=F32)
            c0[0:TOK, cols] = dpool
            c0[TOK:n, cols] = dpool_n
            c1[0:TOK, cols] = dpool * inv[g][0:TOK]
            c1[TOK:n, cols] = dpool_n * inv[g][TOK:n]
        c2[0:n - 8, :] = c1[0:n - 8, :] + c1[1:n - 7, :]
        c3[0:n - 16, 128:] = c2[0:n - 16, 128:] + c2[2:n - 14, 128:]
        c1[0:n - 24, 256:] = c3[0:n - 24, 256:] + c3[4:n - 20, 256:]
        wins = [c2[0:TOK, 0:128], c3[0:TOK, 128:256], c1[0:TOK, 256:384],
                c1[0:TOK, 384:512] + c1[8:TOK + 8, 384:512]]
        for g in range(4):
            cols = slice(g * POOL_GD, (g + 1) * POOL_GD)
            du_ref[:, cols] = (wins[g] - c0[0:TOK, cols]).astype(BF16)

    buf = pltpu.VMEM((TOK + HALO, POOL_W), F32)
    outs, moved = _pcall(
        body, name, (nt,),
        [_row_spec(POOL_W),
         pl.BlockSpec((HALO, POOL_W), lambda i: (jnp.minimum((i + 1) * hb, nt * hb - 1), 0)),
         _row_spec(POOL_W), pl.BlockSpec((4, POOL_GD, POOL_GD), lambda i: (0, 0, 0)),
         _vec_spec(POOL_W)],
        [_row_spec(POOL_W), pl.BlockSpec((4, POOL_GD, POOL_GD), lambda i: (0, 0, 0)),
         _vec_spec(POOL_W)],
        [_sds((t, POOL_W), BF16), _sds((4, POOL_GD, POOL_GD), F32), _sds((1, POOL_W), F32)],
        [buf, buf, buf, buf], _cparams("arbitrary"), (dmixed, dmixed, pooled, wg, scale), comm)
    return outs if comm is None else (*outs, moved)


GELU_C = math.sqrt(2.0 / math.pi)


GELU_K = 0.044715


def _gelu_parts(x):
    x2 = x * x
    s = 0.5 + 0.5 * jnp.tanh(x * (GELU_C + (GELU_C * GELU_K) * x2))
    return x * s, s, x2


def _gelu(x):
    return _gelu_parts(x)[0]


def _gelu_and_grad(x):
    g, s, x2 = _gelu_parts(x)
    return g, s + g * (1.0 - s) * ((2 * GELU_C) + (6 * GELU_C * GELU_K) * x2)


def _taps(buf, r, rows):
    a = buf[pl.ds(r, rows + 8), :]
    return a[8:], pltpu.roll(a, 1, axis=0)[8:], pltpu.roll(a, 2, axis=0)[8:]


def _conv(taps, w_ref, b_ref):
    return b_ref[...] + w_ref[2:3, :] * taps[0] + w_ref[1:2, :] * taps[1] + w_ref[0:1, :] * taps[2]


def _stage(dst, prev_ref, cur_ref, next_ref, first, last):
    rows = cur_ref.shape[0]
    h = prev_ref[...].astype(F32)
    dst[0:8, :] = jnp.where(first, jnp.zeros_like(h), h)
    dst[8:8 + rows, :] = cur_ref[...].astype(F32)
    if next_ref is not None:
        h = next_ref[...].astype(F32)
        dst[8 + rows:, :] = jnp.where(last, jnp.zeros_like(h), h)


FWD_UNROLL = 4
FWD_STRIP = 32
BWD_STRIP = 16


def _ffn_gate_fwd(name, hu, conv_w, conv_b, comm=None):
    t = hu.shape[0]
    ncol = D_FF // FF_COL
    hb = FF_TOK // 8

    def tile(off):
        return pl.BlockSpec((FF_TOK, FF_COL), lambda i, j: (i, j + off))

    def halo(off):
        return pl.BlockSpec((8, FF_COL), lambda i, j: (jnp.maximum(i * hb - 1, 0), j + off))

    def wspec(off):
        return pl.BlockSpec((3, FF_COL), lambda i, j: (0, j + off))

    def bspec(off):
        return pl.BlockSpec((1, FF_COL), lambda i, j: (0, j + off))

    def body(v_ref, vp_ref, g_ref, gp_ref, wv_ref, wg_ref, bv_ref, bg_ref, a_ref, hc_ref, vb, gb):
        first = pl.program_id(0) == 0
        _stage(vb, vp_ref, v_ref, None, first, None)
        _stage(gb, gp_ref, g_ref, None, first, None)

        def strip(k, carry):
            for u in range(FWD_UNROLL):
                r = pl.multiple_of((FWD_UNROLL * k + u) * FWD_STRIP, FWD_STRIP)
                val = _conv(_taps(vb, r, FWD_STRIP), wv_ref, bv_ref)
                gate = _conv(_taps(gb, r, FWD_STRIP), wg_ref, bg_ref)
                a_ref[pl.ds(r, FWD_STRIP), :] = (_gelu(gate) * val).astype(BF16)
                hc_ref[0, pl.ds(r, FWD_STRIP), :] = val.astype(BF16)
                hc_ref[1, pl.ds(r, FWD_STRIP), :] = gate.astype(BF16)
            return carry

        lax.fori_loop(0, FF_TOK // (FWD_UNROLL * FWD_STRIP), strip, 0)

    buf = pltpu.VMEM((FF_TOK + 8, FF_COL), F32)
    outs, moved = _pcall(
        body, name, (t // FF_TOK, ncol),
        [tile(0), halo(0), tile(ncol), halo(ncol), wspec(0), wspec(ncol), bspec(0), bspec(ncol)],
        [pl.BlockSpec((FF_TOK, FF_COL), lambda i, j: (i, j)),
         pl.BlockSpec((2, FF_TOK, FF_COL), lambda i, j: (0, i, j))],
        [_sds((t, D_FF), BF16), _sds((2, t, D_FF), BF16)], [buf, buf],
        _cparams("arbitrary", "arbitrary"),
        (hu, hu, hu, hu, conv_w, conv_w, conv_b, conv_b), comm)
    return outs if comm is None else (*outs, moved)


def _ffn_gate_bwd(name, da, hu, hc, conv_w, comm=None):
    t = hu.shape[0]
    nt = t // FF_TOK
    ncol = D_FF // FF_COL
    hb = FF_TOK // 8

    def tile(off):
        return pl.BlockSpec((FF_TOK, FF_COL), lambda j, i: (i, j + off))

    def nxt_rows(i):
        return jnp.minimum((i + 1) * hb, nt * hb - 1)

    def wspec(off):
        return pl.BlockSpec((3, FF_COL), lambda j, i: (0, j + off))

    def body(da_ref, dan_ref, v_ref, g_ref, hc_ref, hcn_ref, wv_ref, wg_ref,
             dh_ref, dwv_ref, dwg_ref):
        i = pl.program_id(1)
        first, last = i == 0, i == nt - 1

        @pl.when(first)
        def _():
            dwv_ref[...] = jnp.zeros_like(dwv_ref)
            dwg_ref[...] = jnp.zeros_like(dwg_ref)

        def grads(dav, val, gate):
            g, dg = _gelu_and_grad(gate.astype(F32))
            dav = dav.astype(F32)
            return dav * g, dav * val.astype(F32) * dg

        def fold(x):
            return x[0:8] + x[8:16]

        def strip(j, carry):
            for u in range(2):
                carry = one_strip(2 * j + u, carry)
            return carry

        def one_strip(k, carry):
            r = pl.multiple_of(FF_TOK - BWD_STRIP - k * BWD_STRIP, BWD_STRIP)
            rows = pl.ds(r, BWD_STRIP)
            dval, dgate = grads(da_ref[rows, :], hc_ref[0, rows, :], hc_ref[1, rows, :])
            new = (dval[0:8], dgate[0:8])
            for half, (d, below, h_ref, w_ref, dw_ref) in enumerate((
                    (dval, carry[0], v_ref, wv_ref, dwv_ref),
                    (dgate, carry[1], g_ref, wg_ref, dwg_ref))):
                e = jnp.concatenate([d, below], axis=0)
                e1 = pltpu.roll(e, BWD_STRIP + 7, axis=0)[0:BWD_STRIP]
                e2 = pltpu.roll(e, BWD_STRIP + 6, axis=0)[0:BWD_STRIP]
                dh = w_ref[2:3, :] * d + w_ref[1:2, :] * e1 + w_ref[0:1, :] * e2
                dh_ref[half, rows, :] = dh.astype(BF16)
                huv = h_ref[rows, :].astype(F32)
                dw_ref[0:8, :] += fold(e2 * huv)
                dw_ref[8:16, :] += fold(e1 * huv)
                dw_ref[16:24, :] += fold(d * huv)
                dw_ref[24:32, :] += fold(d)
            return new

        dan = dan_ref[...]
        dan = jnp.where(last, jnp.zeros_like(dan), dan)
        lax.fori_loop(0, FF_TOK // (2 * BWD_STRIP), strip, grads(dan, hcn_ref[0], hcn_ref[1]))

        @pl.when(last)
        def _():
            for dw_ref in (dwv_ref, dwg_ref):
                for q in range(4):
                    dw_ref[8 * q:8 * q + 1, :] = jnp.sum(dw_ref[8 * q:8 * q + 8, :], axis=0,
                                                         keepdims=True)

    acc = pl.BlockSpec((32, FF_COL), lambda j, i: (0, j))
    (dhu, dwv, dwg), moved = _pcall(
        body, name, (ncol, nt),
        [tile(0), pl.BlockSpec((8, FF_COL), lambda j, i: (nxt_rows(i), j)),
         tile(0), tile(ncol),
         pl.BlockSpec((2, FF_TOK, FF_COL), lambda j, i: (0, i, j)),
         pl.BlockSpec((2, 8, FF_COL), lambda j, i: (0, nxt_rows(i), j)),
         wspec(0), wspec(ncol)],
        [pl.BlockSpec((2, FF_TOK, FF_COL), lambda j, i: (0, i, j)), acc, acc],
        [_sds((2, t, D_FF), BF16), _sds((32, D_FF), F32), _sds((32, D_FF), F32)],
        [], _cparams("arbitrary", "arbitrary"),
        (da, da, hu, hu, hc, hc, conv_w, conv_w), comm)
    dconv = jnp.concatenate([dwv, dwg], axis=1).reshape(4, 8, 2 * D_FF)[:, 0]
    return (dhu, dconv) if comm is None else (dhu, dconv, moved)


def _mesh_pos():
    x, y, c = lax.axis_index("x"), lax.axis_index("y"), lax.axis_index("c")
    return x, y, c, [(1 - x, y), (x, 1 - y), (1 - x, 1 - y)]


def _remote(src, dst, send_sems, recv_sems, i, dev):
    return pltpu.make_async_remote_copy(src_ref=src, dst_ref=dst, send_sem=send_sems.at[i],
                                        recv_sem=recv_sems.at[i], device_id=dev,
                                        device_id_type=MESH)


def _mine(c, rows):
    return pl.ds(pl.multiple_of(c * (rows // 2), 16), rows // 2)


def _gather_send(shards, conv_shard, gathered, l):
    nbig = len(shards)
    with_conv = conv_shard is not None
    if gathered is None:
        ins = list(shards) + ([conv_shard] if with_conv else [])
        outs = [_sds((DEPTH, N_CHIPS) + s.shape[1:], s.dtype) for s in ins]
        alias = {}
    else:
        ins = list(shards) + list(gathered)
        outs = [_sds(g.shape, g.dtype) for g in gathered]
        alias = {nbig + k: k for k in range(nbig)}

    def copies(cin, cout, ssem, rsem):
        x, y, c, chips = _mesh_pos()
        me = 2 * x + y
        out = []
        for k in range(nbig):
            rows = shards[k].shape[1]
            for j, (cx, cy) in enumerate(chips):
                out.append(_remote(cin[k].at[l, _mine(c, rows)], cout[k].at[l, me, _mine(c, rows)],
                                   ssem, rsem, 4 * k + j, (cx, cy, c)))
            out.append(_remote(cin[k].at[l], cout[k].at[l, me], ssem, rsem, 4 * k + 3,
                               (x, y, 1 - c)))
        if with_conv:
            base = 4 * nbig
            for j, (cx, cy) in enumerate(chips):
                out.append(_remote(cin[nbig].at[c], cout[nbig].at[c, me], ssem, rsem, base + j,
                                   (cx, cy, c)))
            for ll in range(DEPTH):
                out.append(_remote(cin[nbig].at[ll], cout[nbig].at[ll, me], ssem, rsem,
                                   base + 3 + ll, (x, y, 1 - c)))
        return out

    return _Comm(ins, outs, copies, 4 * nbig + 5, alias)


def _gather_forward(gathered, nbig, rows, l):
    with_conv = len(gathered) > nbig
    alias = {k: k for k in range(len(gathered))}

    def copies(cin, cout, ssem, rsem):
        x, y, c, chips = _mesh_pos()
        out = []
        for k in range(nbig):
            for j, (cx, cy) in enumerate(chips):
                blk = cout[k].at[l, 2 * cx + cy, _mine(c, rows[k])]
                out.append(_remote(blk, blk, ssem, rsem, 3 * k + j, (x, y, 1 - c)))
        if with_conv:
            for j, (cx, cy) in enumerate(chips):
                blk = cout[nbig].at[c, 2 * cx + cy]
                out.append(_remote(blk, blk, ssem, rsem, 3 * nbig + j, (x, y, 1 - c)))
        return out

    return _Comm(gathered, [_sds(g.shape, g.dtype) for g in gathered], copies, 3 * nbig + 3, alias)


def _reduce_swap(grads, l):
    def copies(cin, cout, ssem, rsem):
        x, y, c, _ = _mesh_pos()
        return [_remote(cin[k].at[l, :, _mine(1 - c, g.shape[2])], cout[k], ssem, rsem, k,
                        (x, y, 1 - c)) for k, g in enumerate(grads)]

    outs = [_sds((N_CHIPS, g.shape[2] // 2, g.shape[3]), g.dtype) for g in grads]
    return _Comm(grads, outs, copies, len(grads))


def _reduce_scatter(sums):
    def copies(cin, cout, ssem, rsem):
        x, y, c, chips = _mesh_pos()
        return [_remote(cin[k].at[2 * cx + cy], cout[k].at[j], ssem, rsem, 3 * k + j, (cx, cy, c))
                for k in range(len(sums)) for j, (cx, cy) in enumerate(chips)]

    outs = [_sds((3,) + s.shape[1:], s.dtype) for s in sums]
    return _Comm(sums, outs, copies, 3 * len(sums))


def _reduce_share(reds, l):
    def copies(cin, cout, ssem, rsem):
        x, y, c, _ = _mesh_pos()
        out = []
        for k, r in enumerate(reds):
            half = cout[k].at[l, _mine(c, r.shape[1])]
            out.append(_remote(half, half, ssem, rsem, k, (x, y, 1 - c)))
        return out

    return _Comm(reds, [_sds(r.shape, r.dtype) for r in reds], copies, len(reds),
                 {k: k for k in range(len(reds))})


def _allreduce_small(per_layer):
    kinds = len(per_layer[0])
    shapes = [a.shape[1:] if a.shape[0] == 1 else a.shape for a in per_layer[0]]

    def body(*refs):
        ins = refs[:DEPTH * kinds]
        outs = refs[DEPTH * kinds:(DEPTH + 1) * kinds]
        gbufs = refs[(DEPTH + 1) * kinds:(DEPTH + 2) * kinds]
        send_sems, recv_sems = refs[-2], refs[-1]
        x, y, c, chips = _mesh_pos()
        sibling = (x, y, 1 - c)

        def copy(k, i, block, to):
            px, py, pc = block
            slot = gbufs[k].at[4 * px + 2 * py + pc]
            return _remote(slot, slot, send_sems, recv_sems, 7 * k + i, to)

        me = (x, y, c)
        first, passed = [], []
        for k in range(kinds):
            for l in range(DEPTH):
                a = ins[l * kinds + k]
                if per_layer[l][k].shape[0] == 1:
                    gbufs[k][4 * x + 2 * y + c, l:l + 1] = a[...]
                else:
                    gbufs[k][4 * x + 2 * y + c, l] = a[...]
            first.append(copy(k, 0, me, sibling))
            first += [copy(k, 1 + j, me, (*chip, c)) for j, chip in enumerate(chips)]
            passed += [copy(k, 4 + j, (*chip, c), sibling) for j, chip in enumerate(chips)]
        for cp in first:
            cp.start()
        for k in range(kinds):
            for j, chip in enumerate(chips):
                copy(k, 1 + j, (*chip, c), me).wait_recv()
                passed[3 * k + j].start()
        for k in range(kinds):
            copy(k, 0, sibling, me).wait_recv()
            for j, chip in enumerate(chips):
                copy(k, 4 + j, (*chip, 1 - c), me).wait_recv()
        for cp in first + passed:
            cp.wait_send()
        for k in range(kinds):
            acc = gbufs[k][0]
            for d in range(1, 8):
                acc = acc + gbufs[k][d]
            outs[k][...] = acc

    vmem = pl.BlockSpec(memory_space=pltpu.VMEM)
    return pl.pallas_call(
        body, name="allreduce_small",
        in_specs=[vmem] * (DEPTH * kinds), out_specs=[vmem] * kinds,
        out_shape=[_sds((DEPTH,) + s, F32) for s in shapes],
        scratch_shapes=[pltpu.VMEM((8, DEPTH) + s, F32) for s in shapes]
        + [pltpu.SemaphoreType.DMA((7 * kinds,)), pltpu.SemaphoreType.DMA((7 * kinds,))],
        compiler_params=pltpu.CompilerParams(vmem_limit_bytes=VMEM_LIMIT_V7X),
    )(*per_layer[0], *per_layer[1])


def _adamw_small(ws, gs, ms, vs):
    n = len(ws)
    c1 = 1.0 - ADAM_B1 ** ADAM_STEP
    c2 = 1.0 - ADAM_B2 ** ADAM_STEP

    def body(*refs):
        for i in range(n):
            w_ref, g_ref, m_ref, v_ref = (refs[j * n + i] for j in range(4))
            d_ref, nm_ref, nv_ref = (refs[(4 + j) * n + i] for j in range(3))
            gv = g_ref[...]
            nm = ADAM_B1 * m_ref[...] + (1.0 - ADAM_B1) * gv
            nv = ADAM_B2 * v_ref[...] + (1.0 - ADAM_B2) * (gv * gv)
            nm_ref[...] = nm
            nv_ref[...] = nv
            d_ref[...] = -ADAM_LR * ((nm / c1) / (jnp.sqrt(nv / c2) + ADAM_EPS)
                                     + ADAM_WD * w_ref[...])

    vmem = pl.BlockSpec(memory_space=pltpu.VMEM)
    outs = pl.pallas_call(
        body, name="adamw_small", in_specs=[vmem] * (4 * n), out_specs=[vmem] * (3 * n),
        out_shape=[_sds(w.shape, F32) for w in ws] * 3,
        compiler_params=pltpu.CompilerParams(vmem_limit_bytes=VMEM_LIMIT_V7X),
    )(*ws, *gs, *ms, *vs)
    return outs[:n], outs[n:2 * n], outs[2 * n:]


def _core_index():
    return jnp.reshape(lax.axis_index("c"), (1,)).astype(jnp.int32)


def _chip_index():
    return jnp.reshape(2 * lax.axis_index("x") + lax.axis_index("y"), (1,)).astype(jnp.int32)


def _chip_sums(name, stacked, sibs, l):
    n = len(stacked)
    dims = [(s.shape[2] // 2, s.shape[3]) for s in stacked]

    def body(c_ref, *refs):
        for k in range(n):
            a_ref, b_ref, o_ref = refs[k], refs[n + k], refs[2 * n + k]
            o_ref[...] = (a_ref[...].astype(F32) + b_ref[...].astype(F32)).astype(BF16)

    return pl.pallas_call(
        body, name=name,
        grid_spec=pltpu.PrefetchScalarGridSpec(
            num_scalar_prefetch=1, grid=(N_CHIPS,),
            in_specs=[pl.BlockSpec((None, None, hr, cd), lambda j, cr: (l, j, cr[0], 0))
                      for hr, cd in dims]
            + [pl.BlockSpec((None, hr, cd), lambda j, cr: (j, 0, 0)) for hr, cd in dims],
            out_specs=[pl.BlockSpec((None, hr, cd), lambda j, cr: (j, 0, 0)) for hr, cd in dims]),
        out_shape=[_sds((N_CHIPS, hr, cd), BF16) for hr, cd in dims],
        compiler_params=_cparams("parallel"))(_core_index(), *stacked, *sibs)


def _final_sums(name, sums, recvs, l, fills):
    n = len(sums)
    dims = [(s.shape[1] // 2, s.shape[2]) for s in sums]
    filled = fills[0] is not None

    def body(m_ref, *refs):
        outs = refs[-n:]
        for k in range(n):
            acc = refs[k][...].astype(F32)
            for j in range(3):
                acc = acc + refs[n + k][j].astype(F32)
            outs[k][...] = acc

    in_specs = ([pl.BlockSpec((None, tr, cd), lambda i, mr: (mr[0], i, 0)) for tr, cd in dims]
                + [pl.BlockSpec((3, tr, cd), lambda i, mr: (0, i, 0)) for tr, cd in dims])
    args = [jnp.concatenate([_chip_index(), _core_index()]), *sums, *recvs]
    aliases = {}
    if filled:
        in_specs += [pl.BlockSpec(memory_space=pl.ANY)] * n
        args += list(fills)
        aliases = {1 + 2 * n + k: k for k in range(n)}
    return pl.pallas_call(
        body, name=name,
        grid_spec=pltpu.PrefetchScalarGridSpec(
            num_scalar_prefetch=1, grid=(2,), in_specs=in_specs,
            out_specs=[pl.BlockSpec((None, tr, cd), lambda i, mr: (l, 2 * mr[1] + i, 0))
                       for tr, cd in dims]),
        out_shape=[_sds((DEPTH, 4 * tr, cd), F32) for tr, cd in dims],
        input_output_aliases=aliases,
        compiler_params=_cparams("parallel"))(*args)


def _adamw(name, w, g, m, v):
    nl, r, cdim = w.shape
    tr = r // 4 if r % 32 == 0 else r
    c1 = 1.0 - ADAM_B1 ** ADAM_STEP
    c2 = 1.0 - ADAM_B2 ** ADAM_STEP

    def body(w_ref, g_ref, m_ref, v_ref, d_ref, nm_ref, nv_ref, go_ref):
        gv = g_ref[...]
        go_ref[...] = gv
        nm = ADAM_B1 * m_ref[...] + (1.0 - ADAM_B1) * gv
        nv = ADAM_B2 * v_ref[...] + (1.0 - ADAM_B2) * (gv * gv)
        nm_ref[...] = nm
        nv_ref[...] = nv
        d_ref[...] = -ADAM_LR * ((nm / c1) / (jnp.sqrt(nv / c2) + ADAM_EPS) + ADAM_WD * w_ref[...])

    spec = pl.BlockSpec((None, tr, cdim), lambda l, i: (l, i, 0))
    out = _sds(w.shape, F32)
    return pl.pallas_call(
        body, name=name, grid=(nl, r // tr), in_specs=[spec] * 4, out_specs=[spec] * 4,
        out_shape=[out] * 4, compiler_params=_cparams("parallel", "parallel"))(w, g, m, v)


def kernel(x, norm_mix_pre, w_in, b_gate, rel_bias, w_attn_out, w_pool_group, pool_scale, w_pool_out, w_o, norm_mix_post, norm_ffn_pre, w_up, conv_w, conv_b, w_down, norm_ffn_post, loss_target, m_norm_mix_pre, m_w_in, m_b_gate, m_rel_bias, m_w_attn_out, m_w_pool_group, m_pool_scale, m_w_pool_out, m_w_o, m_norm_mix_post, m_norm_ffn_pre, m_w_up, m_conv_w, m_conv_b, m_w_down, m_norm_ffn_post, v_norm_mix_pre, v_w_in, v_b_gate, v_rel_bias, v_w_attn_out, v_w_pool_group, v_pool_scale, v_w_pool_out, v_w_o, v_norm_mix_post, v_norm_ffn_pre, v_w_up, v_conv_w, v_conv_b, v_w_down, v_norm_ffn_post):
    t = x.shape[1]
    xs = x.reshape(t, D_MODEL)
    target = loss_target.reshape(t, D_MODEL)

    names = ["w_in", "w_attn_out", "w_pool_out", "w_o", "w_up", "w_down"]
    shards = [w.astype(BF16) for w in (w_in, w_attn_out, w_pool_out, w_o, w_up, w_down)]
    rows = [s.shape[1] for s in shards]
    nbig = len(shards)
    h, g = _norm_fwd("l0_norm_mix_pre", x.reshape(t, D_MODEL), norm_mix_pre[0:1],
                     _gather_send(shards[:1], conv_w, None, 0))
    g = _comm_call("gather0_forward", _gather_forward(g, 1, rows[:1], 0))
    cw_full = jnp.transpose(g[1], (0, 2, 1, 3)).reshape(DEPTH, 3, 2 * D_FF)
    g = g[:1]
    wg_bf = w_pool_group.astype(BF16)

    def views(gathered):
        win_g, wao_g, wpo_g, wo_g, wup_g, wdn_g = gathered
        return (win_g, wao_g, wpo_g, wo_g.reshape(DEPTH, D_MODEL, D_MODEL), wup_g,
                wdn_g.reshape(DEPTH, D_FF, D_MODEL))

    saved = []
    xcur = xs
    for l in range(DEPTH):
        tag = f"l{l}_"
        bias = _bias_table(tag + "bias_table", rel_bias[l])
        proj = _mm_nn_blocked(tag + "proj", h, g[0], l, BF16)
        if l == 0:
            att, probs, rest = _attn_fwd(tag + "attn_fwd", proj, bias,
                                         _gather_send(shards[1:], None, None, 0))
            pooled, mixed, rest = _pool_fwd(tag + "pool_fwd", proj, wg_bf[l], pool_scale[l:l + 1],
                                            _gather_forward(rest, nbig - 1, rows[1:], 0))
            g = g + rest
        else:
            att, probs = _attn_fwd(tag + "attn_fwd", proj, bias)
            pooled, mixed = _pool_fwd(tag + "pool_fwd", proj, wg_bf[l], pool_scale[l:l + 1])
        win_g, wao_g, wpo_g, wo_full, wup_g, wdn_full = views(g)
        ya = _narrow_nn(tag + "attn_out", att, wao_g, l)
        yb = _narrow_nn(tag + "pool_out", mixed, wpo_g, l)
        z = _gate_fwd(tag + "gate_fwd", proj, b_gate[l:l + 1], ya, yb)
        mix = _mm_nn(tag + "mix", z, wo_full, l, D_MODEL, F32)
        x1, h2 = _post_pre_fwd(tag + "norm_mix_post", xcur, mix, norm_mix_post[l:l + 1],
                               norm_ffn_pre[l:l + 1])
        if l == 0:
            hu, mixing = _mm_nn_blocked(tag + "ffn_up", h2, wup_g, l, BF16,
                                        _gather_send(shards[:4], None, g[:4], 1))
            a, hc, ffn_g = _ffn_gate_fwd(tag + "ffn_gate_fwd", hu, cw_full[l], conv_b[l:l + 1],
                                         _gather_send(shards[4:], None, g[4:], 1))
            g = mixing + ffn_g
            wdn_full = views(g)[5]
        else:
            hu = _mm_nn_blocked(tag + "ffn_up", h2, wup_g, l, BF16)
            a, hc = _ffn_gate_fwd(tag + "ffn_gate_fwd", hu, cw_full[l], conv_b[l:l + 1])
        f = _mm_nn(tag + "ffn_down", a, wdn_full, l, D_FF, F32)
        saved.append(dict(x=xcur, h=h, proj=proj, att=att, pooled=pooled, mixed=mixed, ya=ya,
                          yb=yb, z=z, mix=mix, x1=x1, h2=h2, hu=hu, hc=hc, a=a, f=f, probs=probs))
        if l == 0:
            xcur, h, g = _post_pre_fwd(tag + "norm_ffn_post", x1, f, norm_ffn_post[l:l + 1],
                                       norm_mix_pre[l + 1:l + 2], _gather_forward(g, nbig, rows, 1))
        elif l < DEPTH - 1:
            xcur, h = _post_pre_fwd(tag + "norm_ffn_post", x1, f, norm_ffn_post[l:l + 1],
                                    norm_mix_pre[l + 1:l + 2])
    win_g, wao_g, wpo_g, wo_full, wup_g, wdn_full = views(g)

    dy, df, d_nfpost, loss_local = _tail("tail", saved[-1]["x1"], saved[-1]["f"],
                                         norm_ffn_post[DEPTH - 1:DEPTH], target)
    loss = lax.psum(loss_local, ("x", "y", "c"))

    dx = dy
    dws = dict.fromkeys(names)
    reds = [None] * nbig
    small_grads = [None] * DEPTH
    ffn = [4, 5]
    outs3 = [1, 2, 3]

    def blocks(ks):
        return [dws[names[k]].reshape(DEPTH, N_CHIPS, rows[k], -1) for k in ks]

    def chip_sums(ks, sib, l):
        return _chip_sums(f"chip_sums{l}_" + names[ks[0]], blocks(ks), sib, l)

    def final_sums(ks, sums, recv, l):
        outs = _final_sums(f"final_sums{l}_" + names[ks[0]], sums, recv, l, [reds[k] for k in ks])
        for k, r in zip(ks, outs):
            reds[k] = r

    for l in reversed(range(DEPTH)):
        tag = f"l{l}_"
        sv = saved[l]
        every = list(range(nbig))
        if l == 0:
            da, sib = _mm_nt(tag + "ffn_down_dx", df, wdn_full, l, D_FF // 2, BF16,
                             _reduce_swap(blocks(every), 1))
            sums = chip_sums(every, sib, 1)
        else:
            da = _mm_nt(tag + "ffn_down_dx", df, wdn_full, l, D_FF // 2, BF16)
        dws["w_down"] = _mm_tn(tag + "ffn_down_dw", sv["a"], df, D_FF // 2, l, dws["w_down"])
        if l == 0:
            dhu, dconv, recv = _ffn_gate_bwd(tag + "ffn_gate_bwd", da, sv["hu"], sv["hc"],
                                             cw_full[l], _reduce_scatter(sums))
            final_sums(every, sums, recv, 1)
            dh2, reds = _mm_nt_blocked(tag + "ffn_up_dx", dhu, wup_g, l, F32,
                                       _reduce_share(reds, 1))
        else:
            dhu, dconv = _ffn_gate_bwd(tag + "ffn_gate_bwd", da, sv["hu"], sv["hc"], cw_full[l])
            dh2 = _mm_nt_blocked(tag + "ffn_up_dx", dhu, wup_g, l, F32)
        dws["w_up"] = _mm_tn_blocked(tag + "ffn_up_dw", sv["h2"], dhu, l, dws["w_up"])
        if l == 0:
            dx1, d_nfpre, dmix, d_nmpost, sib = _pre_post_bwd(
                tag + "norm_ffn_pre_bwd", dh2, sv["x1"], dx, norm_ffn_pre[l:l + 1], sv["mix"],
                norm_mix_post[l:l + 1], _reduce_swap(blocks(ffn), 0))
            sums = chip_sums(ffn, sib, 0)
        else:
            dx1, d_nfpre, dmix, d_nmpost = _pre_post_bwd(
                tag + "norm_ffn_pre_bwd", dh2, sv["x1"], dx, norm_ffn_pre[l:l + 1], sv["mix"],
                norm_mix_post[l:l + 1])
        dz = _mm_nt(tag + "mix_dx", dmix, wo_full, l, D_MODEL, BF16)
        dws["w_o"] = _mm_tn(tag + "mix_dw", sv["z"], dmix, D_MODEL, l, dws["w_o"])
        dya, dyb, dgates, d_bgate = _gate_bwd(tag + "gate_bwd", dz, sv["proj"], b_gate[l:l + 1],
                                              sv["ya"], sv["yb"])
        datt = _narrow_nt(tag + "attn_out_dx", dya, wao_g, l)
        dws["w_attn_out"] = _narrow_tn(tag + "attn_out_dw", sv["att"], dya, l, dws["w_attn_out"])
        dmixed = _narrow_nt(tag + "pool_out_dx", dyb, wpo_g, l)
        dws["w_pool_out"] = _narrow_tn(tag + "pool_out_dw", sv["mixed"], dyb, l, dws["w_pool_out"])
        if l == 0:
            du, d_wg, d_pscale, sib = _pool_bwd(tag + "pool_bwd", dmixed, sv["pooled"], wg_bf[l],
                                                pool_scale[l:l + 1], _reduce_swap(blocks(outs3), 0))
            sums3 = chip_sums(outs3, sib, 0)
            dqkv, dbias, recv = _attn_bwd(
                tag + "attn_bwd", sv["proj"], datt, sv["probs"],
                _both(_reduce_scatter(sums), _reduce_scatter(sums3)))
            final_sums(ffn, sums, recv[:len(ffn)], 0)
            final_sums(outs3, sums3, recv[len(ffn):], 0)
        else:
            du, d_wg, d_pscale = _pool_bwd(tag + "pool_bwd", dmixed, sv["pooled"], wg_bf[l],
                                           pool_scale[l:l + 1])
            dqkv, dbias = _attn_bwd(tag + "attn_bwd", sv["proj"], datt, sv["probs"])
        d_rel = _bias_fold(tag + "bias_fold", dbias)
        if l == 0:
            dh, shared = _proj_dx(tag + "proj_dx", dqkv, du, dgates, win_g, l,
                                  _reduce_share([reds[k] for k in ffn + outs3], 0))
            for k, r in zip(ffn + outs3, shared):
                reds[k] = r
        else:
            dh = _proj_dx(tag + "proj_dx", dqkv, du, dgates, win_g, l)
        dws["w_in"] = _proj_dw(tag + "proj_dw", sv["h"], dqkv, du, dgates, l, dws["w_in"])
        small_grads[l] = [None, d_nmpost, d_nfpre, d_nfpost, d_bgate, d_rel, d_wg, d_pscale, dconv]
        if l > 0:
            dx, small_grads[l][0], df, d_nfpost = _pre_post_bwd(
                tag + "norm_mix_pre_bwd", dh, sv["x"], dx1, norm_mix_pre[l:l + 1],
                saved[l - 1]["f"], norm_ffn_post[l - 1:l])
        else:
            dx, small_grads[l][0] = _norm_pre_bwd(tag + "norm_mix_pre_bwd", dh, sv["x"], dx1,
                                                  norm_mix_pre[l:l + 1])

    grad_x = dx.reshape(x.shape)

    delta, new_m, new_v = {}, {}, {}
    sib = _comm_call("reduce_swap", _reduce_swap(blocks([0]), 0))
    sums = chip_sums([0], sib, 0)
    recv = _comm_call("reduce_scatter", _reduce_scatter(sums))
    final_sums([0], sums, recv, 0)
    g_big = _comm_call("reduce_share", _reduce_share([reds[0]], 0)) + reds[1:]

    (g_nmpre, g_nmpost, g_nfpre, g_nfpost, g_bgate, g_rel, g_wg, g_pscale,
     g_conv) = _allreduce_small(small_grads)
    g_rel = g_rel[:, :, :N_REL]
    g_cb = g_conv[:, 3]
    ncw = conv_w.shape[2]
    chip = 2 * lax.axis_index("x") + lax.axis_index("y")
    g_cw = lax.dynamic_slice_in_dim(g_conv[:, 0:3], chip * ncw, ncw, axis=2)

    grads = dict(norm_mix_pre=g_nmpre, w_in=g_big[0], b_gate=g_bgate, rel_bias=g_rel,
                 w_attn_out=g_big[1], w_pool_group=g_wg, pool_scale=g_pscale, w_pool_out=g_big[2],
                 w_o=g_big[3], norm_mix_post=g_nmpost, norm_ffn_pre=g_nfpre, w_up=g_big[4],
                 conv_w=g_cw, conv_b=g_cb, w_down=g_big[5], norm_ffn_post=g_nfpost)
    weights = dict(norm_mix_pre=norm_mix_pre, w_in=w_in, b_gate=b_gate, rel_bias=rel_bias,
                   w_attn_out=w_attn_out, w_pool_group=w_pool_group, pool_scale=pool_scale,
                   w_pool_out=w_pool_out, w_o=w_o, norm_mix_post=norm_mix_post,
                   norm_ffn_pre=norm_ffn_pre, w_up=w_up, conv_w=conv_w, conv_b=conv_b,
                   w_down=w_down, norm_ffn_post=norm_ffn_post)
    moms = dict(norm_mix_pre=(m_norm_mix_pre, v_norm_mix_pre), w_in=(m_w_in, v_w_in),
                b_gate=(m_b_gate, v_b_gate), rel_bias=(m_rel_bias, v_rel_bias),
                w_attn_out=(m_w_attn_out, v_w_attn_out),
                w_pool_group=(m_w_pool_group, v_w_pool_group),
                pool_scale=(m_pool_scale, v_pool_scale), w_pool_out=(m_w_pool_out, v_w_pool_out),
                w_o=(m_w_o, v_w_o), norm_mix_post=(m_norm_mix_post, v_norm_mix_post),
                norm_ffn_pre=(m_norm_ffn_pre, v_norm_ffn_pre), w_up=(m_w_up, v_w_up),
                conv_w=(m_conv_w, v_conv_w), conv_b=(m_conv_b, v_conv_b),
                w_down=(m_w_down, v_w_down), norm_ffn_post=(m_norm_ffn_post, v_norm_ffn_post))
    order = list(weights.keys())

    small_names = [nm for nm in order if nm not in names]
    for nm in names:
        delta[nm], new_m[nm], new_v[nm], grads[nm] = _adamw("adamw_" + nm, weights[nm], grads[nm],
                                                            *moms[nm])
    d_s, m_s, v_s = _adamw_small([weights[nm] for nm in small_names],
                                 [grads[nm] for nm in small_names],
                                 [moms[nm][0] for nm in small_names],
                                 [moms[nm][1] for nm in small_names])
    for i, nm in enumerate(small_names):
        delta[nm], new_m[nm], new_v[nm] = d_s[i], m_s[i], v_s[i]

    return (loss, grad_x, *[grads[nm] for nm in order], *[delta[nm] for nm in order],
            *[new_m[nm] for nm in order], *[new_v[nm] for nm in order])
```
